```python
import math
import jax
import jax.numpy as jnp
from jax import lax
import numpy as np

D_MODEL = 1024
BATCH = 8
SEQ = 2048
DEPTH = 2

SB_HEADS = 8
SB_HEAD_DIM = 64
SB_WIDTH = SB_HEADS * SB_HEAD_DIM
SB_BLOCK = 128
HG_HEADS = 4
HG_HEAD_DIM = 128
HG_WIDTH = HG_HEADS * HG_HEAD_DIM
HG_CHUNK = 64
CONV_WIDTH = 512
CONV_K = 3
N_BRANCH = 3
IN_COLS = 4 * SB_WIDTH + 4 * HG_WIDTH + 4 * CONV_WIDTH + N_BRANCH * D_MODEL
LN_EPS = 1e-5
RMS_EPS = 1e-6

kernel_name = "hybrid_sb_hgrn2_shortconv_deepnorm"


def _standardize(x):
    xf = x.astype(jnp.float32)
    mu = jnp.mean(xf, axis=-1, keepdims=True)
    xc = xf - mu
    var = jnp.mean(xc * xc, axis=-1, keepdims=True)
    return xc * lax.rsqrt(var + LN_EPS)


def _stick_breaking(q, k, v):
    B, S, H, dh = q.shape
    scale = dh ** -0.5
    qf = q.astype(jnp.float32)
    kf = k.astype(jnp.float32)
    vf = v.astype(jnp.float32)
    outs = []
    for blk in range(S // SB_BLOCK):
        t0 = blk * SB_BLOCK
        t1 = t0 + SB_BLOCK
        z = jnp.einsum('bthd,bshd->bhts', qf[:, t0:t1], kf[:, :t1]) * scale
        mask = jnp.arange(t1)[None, :] < (t0 + jnp.arange(SB_BLOCK))[:, None]
        log_1m_beta = jnp.where(mask, -jax.nn.softplus(z), 0.0)
        log_surv = lax.cumsum(log_1m_beta, axis=3, reverse=True) - log_1m_beta
        a = jnp.where(mask, jnp.exp(jax.nn.log_sigmoid(z) + log_surv), 0.0)
        outs.append(jnp.einsum('bhts,bshd->bthd', a, vf[:, :t1]))
    return jnp.concatenate(outs, axis=1)


def _hgrn2(q, f_pre, i_in, lb):
    B, S, H, dk = q.shape
    dv = i_in.shape[-1]
    n_chunks = S // HG_CHUNK
    f = lb + (1.0 - lb) * jax.nn.sigmoid(f_pre.astype(jnp.float32))
    k = 1.0 - f
    g = jnp.log(f)

    def to_chunks(a):
        return a.reshape(B, n_chunks, HG_CHUNK, H, a.shape[-1]).transpose(1, 0, 3, 2, 4)

    qc = to_chunks(q.astype(jnp.float32))
    kc = to_chunks(k)
    vc = to_chunks(i_in.astype(jnp.float32))
    bc = jnp.cumsum(to_chunks(g), axis=3)
    causal = jnp.tril(jnp.ones((HG_CHUNK, HG_CHUNK), dtype=bool))

    def step(state, inp):
        q_c, k_c, v_c, b_c = inp
        inter = jnp.einsum('bhtd,bhde->bhte', q_c * jnp.exp(b_c), state)
        diff = b_c[:, :, :, None, :] - b_c[:, :, None, :, :]
        decay = jnp.exp(jnp.where(causal[:, :, None], diff, -jnp.inf))
        scores = jnp.einsum('bhtsd,bhsd->bhts', q_c[:, :, :, None, :] * decay, k_c)
        intra = jnp.einsum('bhts,bhse->bhte', scores, v_c)
        b_end = b_c[:, :, -1, :]
        k_dec = k_c * jnp.exp(b_end[:, :, None, :] - b_c)
        new_state = jnp.exp(b_end)[..., None] * state + jnp.einsum('bhsd,bhse->bhde', k_dec, v_c)
        return new_state, inter + intra

    state0 = jnp.zeros((B, H, dk, dv), jnp.float32)
    _, o = lax.scan(step, state0, (qc, kc, vc, bc))
    return o.transpose(1, 0, 3, 2, 4).reshape(B, S, H, dv)


def _short_conv(u, w):
    ch = u.shape[-1]
    return lax.conv_general_dilated(
        u, w[:, None, :].astype(u.dtype), window_strides=(1,), padding=[(CONV_K - 1, 0)],
        dimension_numbers=('NWC', 'WIO', 'NWC'), feature_group_count=ch)


def _fwd_setup_inputs(seed: int = 0) -> dict:
    key = jax.random.key(seed)
    ks = jax.random.split(key, 12)
    beta = (8.0 * DEPTH) ** -0.25

    def nrm(k, shape, scale):
        return jax.random.normal(k, shape, jnp.float32) * scale

    return {
        "x": nrm(ks[0], (BATCH, SEQ, D_MODEL), 1.0),
        "c": nrm(ks[1], (BATCH, D_MODEL), 1.0),
        "w_mod": nrm(ks[2], (DEPTH, D_MODEL, 3 * D_MODEL), 0.2 * D_MODEL ** -0.5),
        "b_mod": nrm(ks[3], (DEPTH, 3 * D_MODEL), 0.01),
        "w_in": nrm(ks[4], (DEPTH, D_MODEL, IN_COLS), D_MODEL ** -0.5),
        "conv_w": nrm(ks[5], (DEPTH, CONV_K, CONV_WIDTH), CONV_K ** -0.5),
        "hgrn_norm_w": 1.0 + nrm(ks[6], (DEPTH, HG_HEAD_DIM), 0.02),
        "lower_bounds": nrm(ks[7], (DEPTH, HG_WIDTH), 0.1),
        "w_branch": nrm(ks[8], (DEPTH, N_BRANCH, SB_WIDTH, D_MODEL), beta * SB_WIDTH ** -0.5),
        "w_out": nrm(ks[9], (DEPTH, D_MODEL, D_MODEL), beta * D_MODEL ** -0.5),
        "ln_g": 1.0 + nrm(ks[10], (DEPTH, D_MODEL), 0.02),
        "ln_b": nrm(ks[11], (DEPTH, D_MODEL), 0.02),
    }


def _fwd_reference(x, c, w_mod, b_mod, w_in, conv_w, hgrn_norm_w, lower_bounds, w_branch, w_out, ln_g, ln_b):
    B, S, D = x.shape
    dt = x.dtype
    alpha = (2.0 * DEPTH) ** 0.25
    p = jax.nn.softmax(lower_bounds.astype(jnp.float32), axis=0)
    lbs = jnp.cumsum(p, axis=0) - p[0:1]
    sizes = [SB_WIDTH] * 4 + [HG_WIDTH] * 4 + [CONV_WIDTH] * 4 + [D_MODEL] * N_BRANCH
    splits = np.cumsum(sizes[:-1]).tolist()

    for l in range(DEPTH):
        mod = (c @ w_mod[l] + b_mod[l])[:, None, :]
        shift, scale, gate = jnp.split(mod.astype(jnp.float32), 3, axis=-1)
        h = (_standardize(x) * (1.0 + scale) + shift).astype(dt)

        proj = h @ w_in[l]
        (q_a, k_a, v_a, z_a, q_b, f_b, i_b, z_b,
         pre_c, post_c, u_c, z_c, g_a, g_b, g_c) = jnp.split(proj, splits, axis=-1)

        o_a = _stick_breaking(q_a.reshape(B, S, SB_HEADS, SB_HEAD_DIM),
                              k_a.reshape(B, S, SB_HEADS, SB_HEAD_DIM),
                              v_a.reshape(B, S, SB_HEADS, SB_HEAD_DIM)).reshape(B, S, SB_WIDTH)
        y_a = (o_a * jax.nn.silu(z_a.astype(jnp.float32))).astype(dt)

        o_b = _hgrn2(jax.nn.silu(q_b).reshape(B, S, HG_HEADS, HG_HEAD_DIM),
                     f_b.reshape(B, S, HG_HEADS, HG_HEAD_DIM),
                     i_b.reshape(B, S, HG_HEADS, HG_HEAD_DIM),
                     lbs[l].reshape(HG_HEADS, HG_HEAD_DIM))
        o_b = o_b * lax.rsqrt(jnp.mean(o_b * o_b, axis=-1, keepdims=True) + RMS_EPS)
        o_b = (o_b * hgrn_norm_w[l].astype(jnp.float32)).reshape(B, S, HG_WIDTH)
        y_b = (o_b * jax.nn.silu(z_b.astype(jnp.float32))).astype(dt)

        y_c = post_c * _short_conv(pre_c * u_c, conv_w[l]) * jax.nn.silu(z_c)

        merged = (jax.nn.sigmoid(g_a) * (y_a @ w_branch[l, 0])
                  + jax.nn.sigmoid(g_b) * (y_b @ w_branch[l, 1])
                  + jax.nn.sigmoid(g_c) * (y_c.astype(dt) @ w_branch[l, 2]))
        y = (merged @ w_out[l]).astype(jnp.float32)

        r = alpha * x.astype(jnp.float32) + (1.0 + gate) * y
        x = (_standardize(r) * ln_g[l] + ln_b[l]).astype(dt)
    return x


import jax as _jax
import jax.numpy as _jnp

TWIN_FORMAT = 'train_step'
FWD_PARAMS = ['x', 'c', 'w_mod', 'b_mod', 'w_in', 'conv_w', 'hgrn_norm_w', 'lower_bounds', 'w_branch', 'w_out', 'ln_g', 'ln_b']
TWIN_WEIGHTS = ['w_mod', 'b_mod', 'w_in', 'conv_w', 'hgrn_norm_w', 'lower_bounds', 'w_branch', 'w_out', 'ln_g', 'ln_b']
TWIN_DIFF_INPUT = 'x'
TWIN_INPUTS = ['x', 'c', 'w_mod', 'b_mod', 'w_in', 'conv_w', 'hgrn_norm_w', 'lower_bounds', 'w_branch', 'w_out', 'ln_g', 'ln_b', 'loss_target', 'm_w_mod', 'm_b_mod', 'm_w_in', 'm_conv_w', 'm_hgrn_norm_w', 'm_lower_bounds', 'm_w_branch', 'm_w_out', 'm_ln_g', 'm_ln_b', 'v_w_mod', 'v_b_mod', 'v_w_in', 'v_conv_w', 'v_hgrn_norm_w', 'v_lower_bounds', 'v_w_branch', 'v_w_out', 'v_ln_g', 'v_ln_b']
TWIN_OUTPUTS = ['loss', 'grad_x', 'grad_w_mod', 'grad_b_mod', 'grad_w_in', 'grad_conv_w', 'grad_hgrn_norm_w', 'grad_lower_bounds', 'grad_w_branch', 'grad_w_out', 'grad_ln_g', 'grad_ln_b', 'delta_w_mod', 'delta_b_mod', 'delta_w_in', 'delta_conv_w', 'delta_hgrn_norm_w', 'delta_lower_bounds', 'delta_w_branch', 'delta_w_out', 'delta_ln_g', 'delta_ln_b', 'new_m_w_mod', 'new_m_b_mod', 'new_m_w_in', 'new_m_conv_w', 'new_m_hgrn_norm_w', 'new_m_lower_bounds', 'new_m_w_branch', 'new_m_w_out', 'new_m_ln_g', 'new_m_ln_b', 'new_v_w_mod', 'new_v_b_mod', 'new_v_w_in', 'new_v_conv_w', 'new_v_hgrn_norm_w', 'new_v_lower_bounds', 'new_v_w_branch', 'new_v_w_out', 'new_v_ln_g', 'new_v_ln_b']
TWIN_LEAF_KINDS = {'loss': 'loss', 'grad_x': 'grad_x', 'grad_w_mod': 'grad_w', 'grad_b_mod': 'grad_w', 'grad_w_in': 'grad_w', 'grad_conv_w': 'grad_w', 'grad_hgrn_norm_w': 'grad_w', 'grad_lower_bounds': 'grad_w', 'grad_w_branch': 'grad_w', 'grad_w_out': 'grad_w', 'grad_ln_g': 'grad_w', 'grad_ln_b': 'grad_w', 'delta_w_mod': 'delta_w', 'delta_b_mod': 'delta_w', 'delta_w_in': 'delta_w', 'delta_conv_w': 'delta_w', 'delta_hgrn_norm_w': 'delta_w', 'delta_lower_bounds': 'delta_w', 'delta_w_branch': 'delta_w', 'delta_w_out': 'delta_w', 'delta_ln_g': 'delta_w', 'delta_ln_b': 'delta_w', 'new_m_w_mod': 'new_m', 'new_m_b_mod': 'new_m', 'new_m_w_in': 'new_m', 'new_m_conv_w': 'new_m', 'new_m_hgrn_norm_w': 'new_m', 'new_m_lower_bounds': 'new_m', 'new_m_w_branch': 'new_m', 'new_m_w_out': 'new_m', 'new_m_ln_g': 'new_m', 'new_m_ln_b': 'new_m', 'new_v_w_mod': 'new_v', 'new_v_b_mod': 'new_v', 'new_v_w_in': 'new_v', 'new_v_conv_w': 'new_v', 'new_v_hgrn_norm_w': 'new_v', 'new_v_lower_bounds': 'new_v', 'new_v_w_branch': 'new_v', 'new_v_w_out': 'new_v', 'new_v_ln_g': 'new_v', 'new_v_ln_b': 'new_v'}


def _forward(args):
    return _fwd_reference(*[args[k] for k in FWD_PARAMS])


def _output_shape():
    out = _jax.eval_shape(lambda: _forward(_fwd_setup_inputs(0)))
    return out.shape, out.dtype

N_MICROBATCH = 1
ADAM_LR = 0.001
ADAM_B1 = 0.9
ADAM_B2 = 0.999
ADAM_EPS = 1e-08
ADAM_WD = 0.01
ADAM_STEP = 10
PER_EXAMPLE_BATCH_AXIS = {'x': 0, 'c': 0, 'loss_target': 0}
SHARED_INPUTS = []
_WEIGHT_DTYPES = {'w_mod': _jnp.float32, 'b_mod': _jnp.float32, 'w_in': _jnp.float32, 'conv_w': _jnp.float32, 'hgrn_norm_w': _jnp.float32, 'lower_bounds': _jnp.float32, 'w_branch': _jnp.float32, 'w_out': _jnp.float32, 'ln_g': _jnp.float32, 'ln_b': _jnp.float32}
MOMENT_SCALE = {'w_mod': 2.171551e-02, 'b_mod': 2.258916e-02, 'w_in': 7.620365e-03, 'conv_w': 1.241522e-02, 'hgrn_norm_w': 2.293451e-02, 'lower_bounds': 1.015487e-03, 'w_branch': 1.471166e-02, 'w_out': 2.545598e-02, 'ln_g': 1.132834e+01, 'ln_b': 3.431853e-01}


def _to_microbatches(a, axis):
    t = _jnp.moveaxis(a, axis, 0)
    t = t.reshape((N_MICROBATCH, t.shape[0] // N_MICROBATCH) + t.shape[1:])
    return _jnp.moveaxis(t, 1, axis + 1)


def setup_inputs(seed: int = 0) -> dict:
    inp = _fwd_setup_inputs(seed)
    key = _jax.random.fold_in(_jax.random.key(seed), 7919)
    shape, _ = _output_shape()
    out = dict(inp)
    out["loss_target"] = _jax.random.normal(_jax.random.fold_in(key, 0), shape, _jnp.float32)
    for i, name in enumerate(TWIN_WEIGHTS):
        w = inp[name].astype(_jnp.float32)
        if MOMENT_SCALE is None:
            s = _jnp.sqrt(_jnp.mean(_jnp.square(w)) + 1e-30)
        else:
            s = MOMENT_SCALE[name]
        km, kv = _jax.random.split(_jax.random.fold_in(key, i + 1))
        out[name] = w
        out["m_" + name] = s * _jax.random.normal(km, w.shape, _jnp.float32)
        out["v_" + name] = (s * s) * _jax.random.uniform(kv, w.shape, _jnp.float32, 0.5, 1.5)
    if N_MICROBATCH > 1:
        for name, axis in PER_EXAMPLE_BATCH_AXIS.items():
            out[name] = _to_microbatches(out[name], axis)
    return {'x': out['x'], 'c': out['c'], 'w_mod': out['w_mod'], 'b_mod': out['b_mod'], 'w_in': out['w_in'], 'conv_w': out['conv_w'], 'hgrn_norm_w': out['hgrn_norm_w'], 'lower_bounds': out['lower_bounds'], 'w_branch': out['w_branch'], 'w_out': out['w_out'], 'ln_g': out['ln_g'], 'ln_b': out['ln_b'], 'loss_target': out['loss_target'], 'm_w_mod': out['m_w_mod'], 'm_b_mod': out['m_b_mod'], 'm_w_in': out['m_w_in'], 'm_conv_w': out['m_conv_w'], 'm_hgrn_norm_w': out['m_hgrn_norm_w'], 'm_lower_bounds': out['m_lower_bounds'], 'm_w_branch': out['m_w_branch'], 'm_w_out': out['m_w_out'], 'm_ln_g': out['m_ln_g'], 'm_ln_b': out['m_ln_b'], 'v_w_mod': out['v_w_mod'], 'v_b_mod': out['v_b_mod'], 'v_w_in': out['v_w_in'], 'v_conv_w': out['v_conv_w'], 'v_hgrn_norm_w': out['v_hgrn_norm_w'], 'v_lower_bounds': out['v_lower_bounds'], 'v_w_branch': out['v_w_branch'], 'v_w_out': out['v_w_out'], 'v_ln_g': out['v_ln_g'], 'v_ln_b': out['v_ln_b']}


def _loss(weights, diff, rest, loss_target):
    with _jax.named_scope("forward"):
        args = {**rest, TWIN_DIFF_INPUT: diff, **{k: w.astype(_WEIGHT_DTYPES[k]) for k, w in weights.items()}}
        y = _forward(args)
    with _jax.named_scope("loss_head"):
        err = _jnp.square(y.astype(_jnp.float32) - loss_target)
        return 0.5 * _jnp.sum(_jnp.mean(err, axis=-1)) if err.ndim else 0.5 * err


def _adamw(w, g, m, v):
    m = ADAM_B1 * m + (1.0 - ADAM_B1) * g
    v = ADAM_B2 * v + (1.0 - ADAM_B2) * _jnp.square(g)
    m_hat = m / (1.0 - ADAM_B1 ** ADAM_STEP)
    v_hat = v / (1.0 - ADAM_B2 ** ADAM_STEP)
    delta = -ADAM_LR * (m_hat / (_jnp.sqrt(v_hat) + ADAM_EPS) + ADAM_WD * w)
    return delta, m, v


def reference(x, c, w_mod, b_mod, w_in, conv_w, hgrn_norm_w, lower_bounds, w_branch, w_out, ln_g, ln_b, loss_target, m_w_mod, m_b_mod, m_w_in, m_conv_w, m_hgrn_norm_w, m_lower_bounds, m_w_branch, m_w_out, m_ln_g, m_ln_b, v_w_mod, v_b_mod, v_w_in, v_conv_w, v_hgrn_norm_w, v_lower_bounds, v_w_branch, v_w_out, v_ln_g, v_ln_b):
    given = dict(x=x, c=c, w_mod=w_mod, b_mod=b_mod, w_in=w_in, conv_w=conv_w, hgrn_norm_w=hgrn_norm_w, lower_bounds=lower_bounds, w_branch=w_branch, w_out=w_out, ln_g=ln_g, ln_b=ln_b, loss_target=loss_target, m_w_mod=m_w_mod, m_b_mod=m_b_mod, m_w_in=m_w_in, m_conv_w=m_conv_w, m_hgrn_norm_w=m_hgrn_norm_w, m_lower_bounds=m_lower_bounds, m_w_branch=m_w_branch, m_w_out=m_w_out, m_ln_g=m_ln_g, m_ln_b=m_ln_b, v_w_mod=v_w_mod, v_b_mod=v_b_mod, v_w_in=v_w_in, v_conv_w=v_conv_w, v_hgrn_norm_w=v_hgrn_norm_w, v_lower_bounds=v_lower_bounds, v_w_branch=v_w_branch, v_w_out=v_w_out, v_ln_g=v_ln_g, v_ln_b=v_ln_b)
    weights = {n: given[n] for n in TWIN_WEIGHTS}
    shared = {n: given[n] for n in SHARED_INPUTS}
    per_example = {n: given[n] for n in ['x', 'c']}
    grad_fn = _jax.value_and_grad(_loss, argnums=(0, 1))

    def one_microbatch(ex, loss_target):
        ex = dict(ex)
        diff = ex.pop(TWIN_DIFF_INPUT)
        return grad_fn(weights, diff, {**shared, **ex}, loss_target)

    if N_MICROBATCH == 1:
        loss, (grad_w, grad_x) = one_microbatch(per_example, given["loss_target"])
    else:
        def body(carry, xs):
            loss_sum, grad_sum = carry
            l_k, (gw_k, gx_k) = one_microbatch(xs[0], xs[1])
            with _jax.named_scope("update"):
                return (loss_sum + l_k, _jax.tree.map(_jnp.add, grad_sum, gw_k)), gx_k

        init = (_jnp.zeros((), _jnp.float32), _jax.tree.map(_jnp.zeros_like, weights))
        (loss, grad_w), grad_x = _jax.lax.scan(body, init, (per_example, given["loss_target"]))
    with _jax.named_scope("update"):
        delta_w, new_m, new_v = {}, {}, {}
        for n in TWIN_WEIGHTS:
            delta_w[n], new_m[n], new_v[n] = _adamw(weights[n], grad_w[n], given["m_" + n], given["v_" + n])
    return (loss, grad_x, *[grad_w[n] for n in TWIN_WEIGHTS], *[delta_w[n] for n in TWIN_WEIGHTS],
            *[new_m[n] for n in TWIN_WEIGHTS], *[new_v[n] for n in TWIN_WEIGHTS])
```

```python
import functools

import jax
import jax.numpy as jnp
from jax import lax
from jax.experimental import pallas as pl
from jax.experimental.pallas import tpu as pltpu

F32 = jnp.float32
BF16 = jnp.bfloat16
NDEV = 8
N_LAYERS = 2
SB_HEAD_DIM = 64
HG_HEAD_DIM = 128
WIDTH = 512
BLK = 128
LN_EPS = 1e-5
RMS_EPS = 1e-6
ALPHA = (2.0 * N_LAYERS) ** 0.25
ADAM_LR, ADAM_B1, ADAM_B2, ADAM_EPS, ADAM_WD, ADAM_STEP = 0.001, 0.9, 0.999, 1e-08, 0.01, 10
VMEM_LIMIT = 56 * 1024 * 1024
MESH = pl.DeviceIdType.MESH
HG_LEVELS = (64, 32, 16, 8, 4, 2, 1)


def _pcall(body, *, name, out_shape, grid=None, in_specs=None, out_specs=None, scratch_shapes=(),
           semantics=None, aliases=None):
    kwargs = {}
    if grid is not None:
        kwargs["grid"] = grid
    if in_specs is not None:
        kwargs["in_specs"] = in_specs
    if out_specs is not None:
        kwargs["out_specs"] = out_specs
    if aliases:
        kwargs["input_output_aliases"] = aliases
    return pl.pallas_call(
        body, name=name, out_shape=out_shape, scratch_shapes=list(scratch_shapes),
        compiler_params=pltpu.CompilerParams(dimension_semantics=semantics, vmem_limit_bytes=VMEM_LIMIT),
        interpret=False, **kwargs)


def _dot(a, b):
    return jnp.dot(a, b, preferred_element_type=F32)


def _dot_nt(a, b):
    return lax.dot_general(a, b, (((1,), (1,)), ((), ())), preferred_element_type=F32)


def _dot_tn(a, b):
    return lax.dot_general(a, b, (((0,), (0,)), ((), ())), preferred_element_type=F32)


def _split3(x):
    x1 = x.astype(BF16)
    r1 = x - x1.astype(F32)
    x2 = r1.astype(BF16)
    r2 = r1 - x2.astype(F32)
    return x1, x2, r2.astype(BF16)


def _dot_exact_r(x, m_bf16):
    x1, x2, x3 = _split3(x)
    return _dot(x1, m_bf16) + _dot(x2, m_bf16) + _dot(x3, m_bf16)


def _dot_exact_l(m_bf16, x):
    x1, x2, x3 = _split3(x)
    return _dot(m_bf16, x1) + _dot(m_bf16, x2) + _dot(m_bf16, x3)


def _sigmoid(x):
    return 1.0 / (1.0 + jnp.exp(-x))


def _silu_and_grad(x):
    s = _sigmoid(x)
    return x * s, s * (1.0 + x * (1.0 - s))


def _iota2(shape, dim):
    return lax.broadcasted_iota(jnp.int32, shape, dim)


def _standardize(x):
    mu = jnp.mean(x, axis=-1, keepdims=True)
    xc = x - mu
    var = jnp.mean(xc * xc, axis=-1, keepdims=True)
    rstd = lax.rsqrt(var + LN_EPS)
    return xc * rstd, rstd


def _standardize_bwd(xhat, rstd, dxhat):
    m1 = jnp.mean(dxhat, axis=-1, keepdims=True)
    m2 = jnp.mean(dxhat * xhat, axis=-1, keepdims=True)
    return rstd * (dxhat - m1 - xhat * m2)


def _my_index():
    return 4 * lax.axis_index("x") + 2 * lax.axis_index("y") + lax.axis_index("c")


def _exchange(name, ins, out_shapes, transfers, in_vmem):
    n_in, n_out, n_t = len(ins), len(out_shapes), len(transfers)

    def body(*refs):
        in_refs, out_refs = refs[:n_in], refs[n_in:n_in + n_out]
        send_sems, recv_sems, local_sems = refs[n_in + n_out:]
        x, y, c = lax.axis_index("x"), lax.axis_index("y"), lax.axis_index("c")
        me = 4 * x + 2 * y + c
        started = []
        for t, (i, o, src_fn, dst_fn) in enumerate(transfers):
            own = pltpu.make_async_copy(src_fn(in_refs[i], me), dst_fn(out_refs[o], me), local_sems.at[t])
            own.start()
            started.append(own)
        arrivals = []
        for k in range(1, NDEV):
            px = x ^ ((k >> 2) & 1)
            py = y ^ ((k >> 1) & 1)
            pc = c ^ (k & 1)
            peer = 4 * px + 2 * py + pc
            for t, (i, o, src_fn, dst_fn) in enumerate(transfers):
                sem = t * (NDEV - 1) + k - 1
                push = pltpu.make_async_remote_copy(
                    src_ref=src_fn(in_refs[i], peer), dst_ref=dst_fn(out_refs[o], me),
                    send_sem=send_sems.at[sem], recv_sem=recv_sems.at[sem],
                    device_id=(px, py, pc), device_id_type=MESH)
                push.start()
                started.append(push)
                arrivals.append(pltpu.make_async_remote_copy(
                    src_ref=src_fn(in_refs[i], peer), dst_ref=dst_fn(out_refs[o], peer),
                    send_sem=send_sems.at[sem], recv_sem=recv_sems.at[sem],
                    device_id=(px, py, pc), device_id_type=MESH))
        for arrival in arrivals:
            arrival.wait_recv()
        for cp in started[n_t:]:
            cp.wait_send()
        for own in started[:n_t]:
            own.wait()

    space = pltpu.VMEM if in_vmem else pl.ANY
    spec = pl.BlockSpec(memory_space=space)
    return _pcall(
        body, name=name, out_shape=out_shapes,
        in_specs=[spec] * n_in, out_specs=[spec] * n_out,
        scratch_shapes=[pltpu.SemaphoreType.DMA((n_t * (NDEV - 1),)),
                        pltpu.SemaphoreType.DMA((n_t * (NDEV - 1),)),
                        pltpu.SemaphoreType.DMA((n_t,))])(*ins)


def _whole(ref, dev):
    return ref


def _slot(ref, dev):
    return ref.at[dev]


def _all_gather_small(name, v):
    out = _exchange(name, [v], [jax.ShapeDtypeStruct((NDEV,) + v.shape, v.dtype)],
                    [(0, 0, _whole, _slot)], in_vmem=True)
    return out[0]


def _mod_fwd(c_all, w_mod, b_mod_mine):
    n_layers, _, cm = w_mod.shape

    def body(c_ref, w_ref, b_ref, o_ref):
        for l in range(n_layers):
            o_ref[l] = jnp.dot(c_ref[...], w_ref[l], preferred_element_type=F32,
                               precision=lax.Precision.HIGHEST) + b_ref[l]

    return _pcall(body, name="mod_fwd", out_shape=jax.ShapeDtypeStruct((n_layers, NDEV, cm), F32))(
        c_all, w_mod, b_mod_mine)


def _ln_proj(x, shift, scale, w_full, name):
    s_len, d = x.shape
    n = w_full.shape[1]
    tm = min(512, s_len)
    tn = 1024

    def body(x_ref, sh_ref, sc_ref, w_ref, proj_ref, h_ref, h_scr):
        @pl.when(pl.program_id(1) == 0)
        def _():
            xs, _ = _standardize(x_ref[...])
            h = (xs * (1.0 + sc_ref[...]) + sh_ref[...]).astype(BF16)
            h_scr[...] = h
            h_ref[...] = h

        proj_ref[...] = _dot(h_scr[...], w_ref[...])

    return _pcall(
        body, name=name,
        out_shape=(jax.ShapeDtypeStruct((s_len, n), F32), jax.ShapeDtypeStruct((s_len, d), BF16)),
        grid=(s_len // tm, n // tn),
        in_specs=[pl.BlockSpec((tm, d), lambda i, j: (i, 0)),
                  pl.BlockSpec((1, d), lambda i, j: (0, 0)),
                  pl.BlockSpec((1, d), lambda i, j: (0, 0)),
                  pl.BlockSpec((d, tn), lambda i, j: (0, j))],
        out_specs=(pl.BlockSpec((tm, tn), lambda i, j: (i, j)),
                   pl.BlockSpec((tm, d), lambda i, j: (i, 0))),
        scratch_shapes=[pltpu.VMEM((tm, d), BF16)],
        semantics=("arbitrary", "arbitrary"))(x, shift, scale, w_full)


def _softplus_parts(z):
    e = jnp.exp(-jnp.abs(z))
    sp = jnp.maximum(z, 0.0) + jnp.log(1.0 + e)
    r = 1.0 / (1.0 + e)
    return sp, jnp.where(z >= 0.0, r, e * r)


def _split2(x):
    x1 = x.astype(BF16)
    return x1, (x - x1.astype(F32)).astype(BF16)


def _sb_fwd(proj, name):
    s_len = proj.shape[0]
    nb = s_len // BLK
    n_pairs = WIDTH // BLK

    def body(q_ref, k_ref, v_ref, o_ref, tot_ref):
        lane = _iota2((1, BLK), 1)
        row = _iota2((BLK, BLK), 0)
        col = _iota2((BLK, BLK), 1)
        upper_incl = (row >= col).astype(BF16)
        ones = jnp.ones((BLK, BLK), BF16)
        strict = col < row

        for hh in range(2):
            head_lanes = (lane // SB_HEAD_DIM) == hh

            def tile(qm, j, carry, diag):
                o_acc, later = carry
                c0 = pl.multiple_of(j * BLK, BLK)
                kk = k_ref[pl.ds(c0, BLK), :].astype(BF16)
                z = _dot_nt(qm, kk)
                sp, _ = _softplus_parts(z)
                lg = -sp
                if diag:
                    lg = jnp.where(strict, lg, 0.0)
                l1, l2 = _split2(lg)
                cin = _dot(l1, upper_incl) + _dot(l2, upper_incl)
                rowsum = _dot(l1, ones) + _dot(l2, ones)
                a = jnp.exp(z + cin + later)
                if diag:
                    a = jnp.where(strict, a, 0.0)
                vm = jnp.where(head_lanes, v_ref[pl.ds(c0, BLK), :], 0.0).astype(BF16)
                return o_acc + _dot(a.astype(BF16), vm), later + rowsum

            def qblock(i, _):
                r0 = pl.multiple_of(i * BLK, BLK)
                qm = (jnp.where(head_lanes, q_ref[pl.ds(r0, BLK), :], 0.0) * (SB_HEAD_DIM ** -0.5)).astype(BF16)
                zero = jnp.zeros((BLK, BLK), F32)
                carry = tile(qm, i, (zero, zero), True)
                carry = lax.fori_loop(0, i, lambda jj, cr: tile(qm, i - 1 - jj, cr, False), carry)
                o_acc, later = carry
                if hh == 0:
                    o_ref[pl.ds(r0, BLK), :] = o_acc
                else:
                    o_ref[pl.ds(r0, BLK), :] += o_acc
                tot_ref[hh, pl.ds(r0, BLK), :] = later
                return 0

            lax.fori_loop(0, nb, qblock, 0)

    col_spec = lambda off: pl.BlockSpec((s_len, BLK), lambda p: (0, off + p))
    return _pcall(
        body, name=name,
        out_shape=(jax.ShapeDtypeStruct((s_len, WIDTH), F32),
                   jax.ShapeDtypeStruct((2 * n_pairs, s_len, BLK), F32)),
        grid=(n_pairs,),
        in_specs=[col_spec(0), col_spec(n_pairs), col_spec(2 * n_pairs)],
        out_specs=(pl.BlockSpec((s_len, BLK), lambda p: (0, p)),
                   pl.BlockSpec((2, s_len, BLK), lambda p: (p, 0, 0))),
        semantics=("arbitrary",))(proj, proj, proj)


def _hg_masks(mask_ref):
    row = _iota2((BLK, BLK), 0)
    col = _iota2((BLK, BLK), 1)
    for v, m in enumerate(HG_LEVELS):
        same = (row // (2 * m)) == (col // (2 * m))
        mask_ref[v] = (same & ((row & m) != 0) & ((col & m) == 0)).astype(F32)


def _hg_mid(b, m):
    if m >= 4:
        n = BLK // (2 * m)
        mid = b.reshape(n, 2 * m, BLK)[:, m - 1:m, :]
        return jnp.broadcast_to(mid, (n, 2 * m, BLK)).reshape(BLK, BLK)
    pos = _iota2((BLK, BLK), 0) & (2 * m - 1)
    out = b
    for p in range(2 * m):
        delta = (m - 1) - p
        if delta != 0:
            out = jnp.where(pos == p, pltpu.roll(b, (-delta) % BLK, 0), out)
    return out


def _hg_chunk_inputs(qraw, fpre, lb):
    sig = _sigmoid(fpre)
    f = lb + (1.0 - lb) * sig
    g = jnp.log(f)
    q, dq_fac = _silu_and_grad(qraw)
    return q, dq_fac, f, sig, g


def _hg_level_terms(q, k, b, v_idx, m, mask_ref):
    mid = _hg_mid(b, m)
    eq = jnp.exp(jnp.minimum(b - mid, 0.0))
    ek = jnp.exp(jnp.minimum(mid - b, 0.0))
    qt = (q * eq).astype(BF16)
    kt = (k * ek).astype(BF16)
    return qt, kt, eq, ek, mask_ref[v_idx]


def _hg_scores(q, k, b, mask_ref):
    sc = None
    for v_idx, m in enumerate(HG_LEVELS):
        qt, kt, _, _, msk = _hg_level_terms(q, k, b, v_idx, m, mask_ref)
        term = _dot_nt(qt, kt) * msk
        sc = term if sc is None else sc + term
    return sc


def _hgrn_fwd(proj, lb, name):
    s_len = proj.shape[0]
    nc = s_len // BLK
    nh = WIDTH // HG_HEAD_DIM
    base = 4 * WIDTH // BLK

    def body(q_ref, f_ref, i_ref, lb_ref, o_ref, mask_ref):
        _hg_masks(mask_ref)
        row = _iota2((BLK, BLK), 0)
        col = _iota2((BLK, BLK), 1)
        lower_incl = (col <= row).astype(BF16)
        lb_v = lb_ref[...]

        def chunk(ci, st):
            r0 = pl.multiple_of(ci * BLK, BLK)
            q, _, f, _, g = _hg_chunk_inputs(q_ref[pl.ds(r0, BLK), :], f_ref[pl.ds(r0, BLK), :], lb_v)
            k = 1.0 - f
            v = i_ref[pl.ds(r0, BLK), :]
            vb = v.astype(BF16)
            b = _dot_exact_l(lower_incl, g)
            b_end = b[BLK - 1:BLK, :]
            inter = _dot_nt((q * jnp.exp(b)).astype(BF16), st.astype(BF16))
            sc = _hg_scores(q, k, b, mask_ref)
            diag = jnp.sum(q * k, axis=-1, keepdims=True)
            o_ref[pl.ds(r0, BLK), :] = inter + _dot(sc.astype(BF16), vb) + diag * v
            k_dec = (k * jnp.exp(b_end - b)).astype(BF16)
            return st * jnp.exp(b_end) + _dot_tn(vb, k_dec)

        lax.fori_loop(0, nc, chunk, jnp.zeros((HG_HEAD_DIM, HG_HEAD_DIM), F32))

    col_spec = lambda off: pl.BlockSpec((s_len, BLK), lambda h: (0, off + h))
    return _pcall(
        body, name=name, out_shape=jax.ShapeDtypeStruct((s_len, WIDTH), F32),
        grid=(nh,),
        in_specs=[col_spec(base), col_spec(base + nh), col_spec(base + 2 * nh),
                  pl.BlockSpec((1, BLK), lambda h: (0, h))],
        out_specs=pl.BlockSpec((s_len, BLK), lambda h: (0, h)),
        scratch_shapes=[pltpu.VMEM((len(HG_LEVELS), BLK, BLK), F32)],
        semantics=("arbitrary",))(proj, proj, proj, lb)


def _rms_heads(o_b, norm_w):
    n_parts, h_parts, r_parts = [], [], []
    for h in range(WIDTH // HG_HEAD_DIM):
        sl = slice(h * HG_HEAD_DIM, (h + 1) * HG_HEAD_DIM)
        o = o_b[:, sl]
        rstd = lax.rsqrt(jnp.mean(o * o, axis=-1, keepdims=True) + RMS_EPS)
        ohat = o * rstd
        h_parts.append(ohat)
        n_parts.append(ohat * norm_w[:, sl])
        r_parts.append(jnp.broadcast_to(rstd, o.shape))
    cat = lambda parts: jnp.concatenate(parts, axis=-1)
    return cat(n_parts), cat(h_parts), cat(r_parts)


def _shift_rows_down(halo, cur, k):
    tm = cur.shape[0]
    ext = jnp.concatenate([halo, cur], axis=0)
    return pltpu.roll(ext, k, 0)[8:8 + tm]


def _shift_rows_up(cur, halo, k):
    tm = cur.shape[0]
    ext = jnp.concatenate([cur, halo], axis=0)
    return pltpu.roll(ext, (tm + 8 - k) % (tm + 8), 0)[0:tm]


def _merge_fwd(x, proj, o_a, o_b, gate, norm_w, conv_w, wb, w_out, ln_g, ln_b, name):
    s_len, d = x.shape
    tm = min(256, s_len)
    hb = tm // 8

    def body(x_ref, oa_ref, za_ref, ob_ref, zb_ref, pre_ref, post_ref, u_ref, zc_ref, hpre_ref, hu_ref, g_ref,
             gate_ref, nw_ref, cw_ref, wb_ref, wo_ref, lg_ref, lbias_ref, xn_ref, mg_ref, yc_ref):
        i = pl.program_id(0)
        sa, _ = _silu_and_grad(za_ref[...])
        y_a = (oa_ref[...] * sa).astype(BF16)
        n_b, _, _ = _rms_heads(ob_ref[...], nw_ref[...])
        sb, _ = _silu_and_grad(zb_ref[...])
        y_b = (n_b * sb).astype(BF16)
        a = pre_ref[...] * u_ref[...]
        halo = jnp.where(i > 0, hpre_ref[...] * hu_ref[...], 0.0)
        cw = cw_ref[...]
        conv = cw[0:1] * _shift_rows_down(halo, a, 2) + cw[1:2] * _shift_rows_down(halo, a, 1) + cw[2:3] * a
        sc, _ = _silu_and_grad(zc_ref[...])
        y_c = (post_ref[...] * conv * sc).astype(BF16)
        merged = None
        for k, yk in enumerate((y_a, y_b, y_c)):
            yc_ref[:, k * WIDTH:(k + 1) * WIDTH] = yk
            term = _sigmoid(g_ref[:, k * d:(k + 1) * d]) * _dot(yk, wb_ref[k])
            merged = term if merged is None else merged + term
        mb = merged.astype(BF16)
        mg_ref[...] = mb
        y = _dot(mb, wo_ref[...])
        r = ALPHA * x_ref[...] + (1.0 + gate_ref[...]) * y
        rhat, _ = _standardize(r)
        xn_ref[...] = rhat * lg_ref[...] + lbias_ref[...]

    wcol = lambda cb: pl.BlockSpec((tm, WIDTH), lambda i: (i, cb))
    halo_spec = lambda cb: pl.BlockSpec((8, WIDTH), lambda i: (jnp.maximum(i * hb - 1, 0), cb))
    vec = lambda w: pl.BlockSpec((1, w), lambda i: (0, 0))
    return _pcall(
        body, name=name,
        out_shape=(jax.ShapeDtypeStruct((s_len, d), F32), jax.ShapeDtypeStruct((s_len, d), BF16),
                   jax.ShapeDtypeStruct((s_len, 3 * WIDTH), BF16)),
        grid=(s_len // tm,),
        in_specs=[pl.BlockSpec((tm, d), lambda i: (i, 0)),
                  wcol(0), wcol(3), wcol(0), wcol(7), wcol(8), wcol(9), wcol(10), wcol(11),
                  halo_spec(8), halo_spec(10),
                  pl.BlockSpec((tm, 3 * d), lambda i: (i, 2)),
                  vec(d), vec(WIDTH),
                  pl.BlockSpec((3, WIDTH), lambda i: (0, 0)),
                  pl.BlockSpec((3, WIDTH, d), lambda i: (0, 0, 0)),
                  pl.BlockSpec((d, d), lambda i: (0, 0)),
                  vec(d), vec(d)],
        out_specs=(pl.BlockSpec((tm, d), lambda i: (i, 0)), pl.BlockSpec((tm, d), lambda i: (i, 0)),
                   pl.BlockSpec((tm, 3 * WIDTH), lambda i: (i, 0))),
        semantics=("arbitrary",))(x, o_a, proj, o_b, proj, proj, proj, proj, proj, proj, proj, proj,
                                  gate, norm_w, conv_w, wb, w_out, ln_g, ln_b)


def _loss_fwd_bwd(y, target):
    s_len, d = y.shape
    tm = min(512, s_len)

    def body(y_ref, t_ref, loss_ref, dy_ref):
        @pl.when(pl.program_id(0) == 0)
        def _():
            loss_ref[...] = jnp.zeros_like(loss_ref)

        e = y_ref[...] - t_ref[...]
        dy_ref[...] = e * (1.0 / d)
        part = jnp.sum(jnp.sum(e * e, axis=-1, keepdims=True), axis=0, keepdims=True)
        loss_ref[...] += part * (0.5 / d)

    tile = pl.BlockSpec((tm, d), lambda i: (i, 0))
    return _pcall(body, name="loss", grid=(s_len // tm,),
                  out_shape=(jax.ShapeDtypeStruct((1, 1), F32), jax.ShapeDtypeStruct((s_len, d), F32)),
                  in_specs=[tile, tile],
                  out_specs=(pl.BlockSpec((1, 1), lambda i: (0, 0)), tile),
                  semantics=("arbitrary",))(y, target)


def _merge_bwd(dxn, x, merged, ycat, proj, gate, wb, w_out, ln_g, name):
    s_len, d = x.shape
    tm = min(256, s_len)

    def body(dxn_ref, x_ref, mg_ref, yc_ref, g_ref, gate_ref, wb_ref, wo_ref, lg_ref,
             dres_ref, dyc_ref, dg_ref, gwo_ref, gwb_ref, vec_ref):
        @pl.when(pl.program_id(0) == 0)
        def _():
            gwo_ref[...] = jnp.zeros_like(gwo_ref)
            gwb_ref[...] = jnp.zeros_like(gwb_ref)
            vec_ref[...] = jnp.zeros_like(vec_ref)

        mb = mg_ref[...]
        one_gate = 1.0 + gate_ref[...]
        y = _dot(mb, wo_ref[...])
        r = ALPHA * x_ref[...] + one_gate * y
        rhat, rstd = _standardize(r)
        dxn = dxn_ref[...]
        dr = _standardize_bwd(rhat, rstd, dxn * lg_ref[...])
        vec_ref[0:1, :] += jnp.sum(dxn * rhat, axis=0, keepdims=True)
        vec_ref[1:2, :] += jnp.sum(dxn, axis=0, keepdims=True)
        vec_ref[2:3, :] += jnp.sum(dr * y, axis=0, keepdims=True)
        dres_ref[...] = ALPHA * dr
        dy = (one_gate * dr).astype(BF16)
        gwo_ref[...] += _dot_tn(mb, dy)
        dmerged = _dot_nt(dy, wo_ref[...])
        for k in range(3):
            yk = yc_ref[:, k * WIDTH:(k + 1) * WIDTH]
            sg = _sigmoid(g_ref[:, k * d:(k + 1) * d])
            pk = _dot(yk, wb_ref[k])
            dg_ref[:, k * d:(k + 1) * d] = dmerged * pk * sg * (1.0 - sg)
            dpk = (dmerged * sg).astype(BF16)
            dyc_ref[:, k * WIDTH:(k + 1) * WIDTH] = _dot_nt(dpk, wb_ref[k])
            gwb_ref[k] += _dot_tn(yk, dpk)

    tile = lambda w: pl.BlockSpec((tm, w), lambda i: (i, 0))
    vec = pl.BlockSpec((1, d), lambda i: (0, 0))
    return _pcall(
        body, name=name,
        out_shape=(jax.ShapeDtypeStruct((s_len, d), F32), jax.ShapeDtypeStruct((s_len, 3 * WIDTH), F32),
                   jax.ShapeDtypeStruct((s_len, 3 * d), F32), jax.ShapeDtypeStruct((d, d), F32),
                   jax.ShapeDtypeStruct((3, WIDTH, d), F32), jax.ShapeDtypeStruct((8, d), F32)),
        grid=(s_len // tm,),
        in_specs=[tile(d), tile(d), tile(d), tile(3 * WIDTH),
                  pl.BlockSpec((tm, 3 * d), lambda i: (i, 2)),
                  vec, pl.BlockSpec((3, WIDTH, d), lambda i: (0, 0, 0)),
                  pl.BlockSpec((d, d), lambda i: (0, 0)), vec],
        out_specs=(tile(d), tile(3 * WIDTH), tile(3 * d),
                   pl.BlockSpec((d, d), lambda i: (0, 0)),
                   pl.BlockSpec((3, WIDTH, d), lambda i: (0, 0, 0)),
                   pl.BlockSpec((8, d), lambda i: (0, 0))),
        semantics=("arbitrary",))(dxn, x, merged, ycat, proj, gate, wb, w_out, ln_g)


def _branch_bwd(dycat, proj, o_a, o_b, norm_w, conv_w, name):
    s_len = proj.shape[0]
    tm = min(256, s_len)
    hb = tm // 8
    n_tiles = s_len // tm

    def body(dya_ref, dyb_ref, dyc_ref, oa_ref, za_ref, ob_ref, zb_ref, pre_ref, post_ref, u_ref, zc_ref,
             hpre_ref, hu_ref, ndyc_ref, npost_ref, nzc_ref, nw_ref, cw_ref,
             doa_ref, dob_ref, dza_ref, dzb_ref, dpre_ref, dpost_ref, du_ref, dzc_ref, vec_ref):
        i = pl.program_id(0)

        @pl.when(i == 0)
        def _():
            vec_ref[...] = jnp.zeros_like(vec_ref)

        sa, dsa = _silu_and_grad(za_ref[...])
        dya = dya_ref[...]
        doa_ref[...] = dya * sa
        dza_ref[...] = dya * oa_ref[...] * dsa
        nw = nw_ref[...]
        n_b, ohat, rstd = _rms_heads(ob_ref[...], nw)
        sb, dsb = _silu_and_grad(zb_ref[...])
        dyb = dyb_ref[...]
        dzb_ref[...] = dyb * n_b * dsb
        dn = dyb * sb
        vec_ref[0:1, :] += jnp.sum(dn * ohat, axis=0, keepdims=True)
        dnw = dn * nw
        parts = []
        for h in range(WIDTH // HG_HEAD_DIM):
            sl = slice(h * HG_HEAD_DIM, (h + 1) * HG_HEAD_DIM)
            m2 = jnp.mean(dnw[:, sl] * ohat[:, sl], axis=-1, keepdims=True)
            parts.append(rstd[:, sl] * (dnw[:, sl] - ohat[:, sl] * m2))
        dob_ref[...] = jnp.concatenate(parts, axis=-1)
        cw = cw_ref[...]
        pre, u, post = pre_ref[...], u_ref[...], post_ref[...]
        a = pre * u
        halo = jnp.where(i > 0, hpre_ref[...] * hu_ref[...], 0.0)
        a1 = _shift_rows_down(halo, a, 1)
        a2 = _shift_rows_down(halo, a, 2)
        conv = cw[0:1] * a2 + cw[1:2] * a1 + cw[2:3] * a
        sc, dsc = _silu_and_grad(zc_ref[...])
        dyc = dyc_ref[...]
        dpost_ref[...] = dyc * conv * sc
        dzc_ref[...] = dyc * post * conv * dsc
        dconv = dyc * post * sc
        nsc, _ = _silu_and_grad(nzc_ref[...])
        nxt = jnp.where(i < n_tiles - 1, ndyc_ref[...] * npost_ref[...] * nsc, 0.0)
        da = cw[2:3] * dconv + cw[1:2] * _shift_rows_up(dconv, nxt, 1) + cw[0:1] * _shift_rows_up(dconv, nxt, 2)
        dpre_ref[...] = da * u
        du_ref[...] = da * pre
        vec_ref[1:2, :] += jnp.sum(dconv * a2, axis=0, keepdims=True)
        vec_ref[2:3, :] += jnp.sum(dconv * a1, axis=0, keepdims=True)
        vec_ref[3:4, :] += jnp.sum(dconv * a, axis=0, keepdims=True)

    wcol = lambda cb: pl.BlockSpec((tm, WIDTH), lambda i: (i, cb))
    prev = lambda cb: pl.BlockSpec((8, WIDTH), lambda i: (jnp.maximum(i * hb - 1, 0), cb))
    nxt = lambda cb: pl.BlockSpec((8, WIDTH), lambda i: (jnp.minimum((i + 1) * hb, s_len // 8 - 1), cb))
    out = jax.ShapeDtypeStruct((s_len, WIDTH), F32)
    return _pcall(
        body, name=name, out_shape=(out,) * 8 + (jax.ShapeDtypeStruct((8, WIDTH), F32),),
        grid=(n_tiles,),
        in_specs=[wcol(0), wcol(1), wcol(2), wcol(0), wcol(3), wcol(0), wcol(7), wcol(8), wcol(9), wcol(10), wcol(11),
                  prev(8), prev(10), nxt(2), nxt(9), nxt(11),
                  pl.BlockSpec((1, WIDTH), lambda i: (0, 0)), pl.BlockSpec((3, WIDTH), lambda i: (0, 0))],
        out_specs=(wcol(0),) * 8 + (pl.BlockSpec((8, WIDTH), lambda i: (0, 0)),),
        semantics=("arbitrary",))(dycat, dycat, dycat, o_a, proj, o_b, proj, proj, proj, proj, proj,
                                  proj, proj, dycat, proj, proj, norm_w, conv_w)


def _sb_bwd(proj, do_a, totals, name):
    s_len = proj.shape[0]
    nb = s_len // BLK
    n_pairs = WIDTH // BLK
    scale = SB_HEAD_DIM ** -0.5

    def body(q_ref, k_ref, v_ref, do_ref, tot_ref, dq_ref, dk_ref, dv_ref):
        lane = _iota2((1, BLK), 1)
        row = _iota2((BLK, BLK), 0)
        col = _iota2((BLK, BLK), 1)
        before = (row < col).astype(BF16)
        upto = (row <= col).astype(BF16)
        ones = jnp.ones((BLK, BLK), BF16)
        strict = col < row
        dk_ref[...] = jnp.zeros_like(dk_ref)
        dv_ref[...] = jnp.zeros_like(dv_ref)

        for hh in range(2):
            head_lanes = (lane // SB_HEAD_DIM) == hh

            def qblock(i, _):
                r0 = pl.multiple_of(i * BLK, BLK)
                qm = (jnp.where(head_lanes, q_ref[pl.ds(r0, BLK), :], 0.0) * scale).astype(BF16)
                dom = jnp.where(head_lanes, do_ref[pl.ds(r0, BLK), :], 0.0).astype(BF16)
                total = tot_ref[hh, pl.ds(r0, BLK), :]

                def tile(j, carry, diag):
                    dq_acc, l_before, g_before = carry
                    c0 = pl.multiple_of(j * BLK, BLK)
                    km = jnp.where(head_lanes, k_ref[pl.ds(c0, BLK), :], 0.0).astype(BF16)
                    vm = jnp.where(head_lanes, v_ref[pl.ds(c0, BLK), :], 0.0).astype(BF16)
                    z = _dot_nt(qm, km)
                    sp, sig = _softplus_parts(z)
                    lg = -sp
                    if diag:
                        lg = jnp.where(strict, lg, 0.0)
                    l1, l2 = _split2(lg)
                    l_excl = _dot(l1, before) + _dot(l2, before)
                    a = jnp.exp(z + (total - l_before - l_excl))
                    if diag:
                        a = jnp.where(strict, a, 0.0)
                    gmat = a * _dot_nt(dom, vm)
                    g1, g2 = _split2(gmat)
                    g_incl = g_before + _dot(g1, upto) + _dot(g2, upto)
                    dz = gmat - sig * g_incl
                    if diag:
                        dz = jnp.where(strict, dz, 0.0)
                    dzb = dz.astype(BF16)
                    dk_ref[pl.ds(c0, BLK), :] += _dot_tn(dzb, qm)
                    dv_ref[pl.ds(c0, BLK), :] += _dot_tn(a.astype(BF16), dom)
                    return (dq_acc + _dot(dzb, km),
                            l_before + _dot(l1, ones) + _dot(l2, ones),
                            g_before + _dot(g1, ones) + _dot(g2, ones))

                zero = jnp.zeros((BLK, BLK), F32)
                carry = lax.fori_loop(0, i, lambda j, cr: tile(j, cr, False), (zero, zero, zero))
                dq_acc, _, _ = tile(i, carry, True)
                if hh == 0:
                    dq_ref[pl.ds(r0, BLK), :] = dq_acc * scale
                else:
                    dq_ref[pl.ds(r0, BLK), :] += dq_acc * scale
                return 0

            lax.fori_loop(0, nb, qblock, 0)

    col_spec = lambda off: pl.BlockSpec((s_len, BLK), lambda p: (0, off + p))
    out = jax.ShapeDtypeStruct((s_len, WIDTH), F32)
    return _pcall(
        body, name=name, out_shape=(out, out, out), grid=(n_pairs,),
        in_specs=[col_spec(0), col_spec(n_pairs), col_spec(2 * n_pairs), col_spec(0),
                  pl.BlockSpec((2, s_len, BLK), lambda p: (p, 0, 0))],
        out_specs=(col_spec(0), col_spec(0), col_spec(0)),
        semantics=("arbitrary",))(proj, proj, proj, do_a, totals)


def _hgrn_bwd(proj, do_b, lb, name):
    s_len = proj.shape[0]
    nc = s_len // BLK
    nh = WIDTH // HG_HEAD_DIM
    base = 4 * WIDTH // BLK

    def body(q_ref, f_ref, i_ref, do_ref, lb_ref, dq_ref, df_ref, di_ref, dlb_ref, mask_ref, st_ref):
        _hg_masks(mask_ref)
        row = _iota2((BLK, BLK), 0)
        col = _iota2((BLK, BLK), 1)
        lower_incl = (col <= row).astype(BF16)
        upper_incl = (col >= row).astype(BF16)
        lb_v = lb_ref[...]

        def load(ci):
            r0 = pl.multiple_of(ci * BLK, BLK)
            q, dq_fac, f, sig, g = _hg_chunk_inputs(q_ref[pl.ds(r0, BLK), :], f_ref[pl.ds(r0, BLK), :], lb_v)
            b = _dot_exact_l(lower_incl, g)
            return r0, q, dq_fac, f, sig, b, i_ref[pl.ds(r0, BLK), :]

        def fwd_chunk(ci, st):
            st_ref[ci] = st
            _, _, _, f, _, b, v = load(ci)
            b_end = b[BLK - 1:BLK, :]
            k_dec = ((1.0 - f) * jnp.exp(b_end - b)).astype(BF16)
            return st * jnp.exp(b_end) + _dot_tn(v.astype(BF16), k_dec)

        lax.fori_loop(0, nc, fwd_chunk, jnp.zeros((HG_HEAD_DIM, HG_HEAD_DIM), F32))

        def bwd_chunk(cc, carry):
            dst, suffix, dlb = carry
            ci = nc - 1 - cc
            r0, q, dq_fac, f, sig, b, v = load(ci)
            k = 1.0 - f
            vb = v.astype(BF16)
            do = do_ref[pl.ds(r0, BLK), :]
            dob = do.astype(BF16)
            b_end = b[BLK - 1:BLK, :]
            e_q = jnp.exp(b)
            e_k = jnp.exp(b_end - b)
            qe = (q * e_q).astype(BF16)
            kh = (k * e_k).astype(BF16)
            st1, st2 = _split2(st_ref[ci])
            ds1, ds2 = _split2(dst)
            dqe = _dot(dob, st1) + _dot(dob, st2)
            dkh = _dot(vb, ds1) + _dot(vb, ds2)
            dq = e_q * dqe
            dk = e_k * dkh
            dv = _dot_nt(kh, ds1)
            dst_new = dst * jnp.exp(b_end) + _dot_tn(dob, qe)
            dlog = qe.astype(F32) * dqe - kh.astype(F32) * dkh
            da = _dot_nt(dob, vb)
            sc = None
            for v_idx, m in enumerate(HG_LEVELS):
                qm, km, eq, ek, msk = _hg_level_terms(q, k, b, v_idx, m, mask_ref)
                term = _dot_nt(qm, km) * msk
                sc = term if sc is None else sc + term
                pm = (da * msk).astype(BF16)
                dqm = _dot(pm, km)
                dkm = _dot_tn(pm, qm)
                dq = dq + dqm * eq
                dk = dk + dkm * ek
                dlog = dlog + (qm.astype(F32) * dqm - km.astype(F32) * dkm)
            a_diag = jnp.sum(do * v, axis=-1, keepdims=True)
            s_diag = jnp.sum(q * k, axis=-1, keepdims=True)
            dq = dq + a_diag * k
            dk = dk + a_diag * q
            dv = dv + _dot_tn(sc.astype(BF16), dob) + s_diag * do
            dg = _dot_exact_l(upper_incl, dlog) + suffix
            dfull = dg / f - dk
            dq_ref[pl.ds(r0, BLK), :] = dq * dq_fac
            df_ref[pl.ds(r0, BLK), :] = dfull * (1.0 - lb_v) * sig * (1.0 - sig)
            di_ref[pl.ds(r0, BLK), :] = dv
            dlb = dlb + jnp.sum(dfull * (1.0 - sig), axis=0, keepdims=True)
            return dst_new, dg[0:1, :], dlb

        zero_row = jnp.zeros((1, BLK), F32)
        _, _, dlb = lax.fori_loop(0, nc, bwd_chunk,
                                  (jnp.zeros((HG_HEAD_DIM, HG_HEAD_DIM), F32), zero_row, zero_row))
        dlb_ref[...] = jnp.broadcast_to(dlb, dlb_ref.shape)

    col_spec = lambda off: pl.BlockSpec((s_len, BLK), lambda h: (0, off + h))
    out = jax.ShapeDtypeStruct((s_len, WIDTH), F32)
    return _pcall(
        body, name=name, out_shape=(out, out, out, jax.ShapeDtypeStruct((8, WIDTH), F32)),
        grid=(nh,),
        in_specs=[col_spec(base), col_spec(base + nh), col_spec(base + 2 * nh), col_spec(0),
                  pl.BlockSpec((1, BLK), lambda h: (0, h))],
        out_specs=(col_spec(0), col_spec(0), col_spec(0), pl.BlockSpec((8, BLK), lambda h: (0, h))),
        scratch_shapes=[pltpu.VMEM((len(HG_LEVELS), BLK, BLK), F32),
                        pltpu.VMEM((nc, HG_HEAD_DIM, HG_HEAD_DIM), F32)],
        semantics=("arbitrary",))(proj, proj, proj, do_b, lb)


def _in_bwd(pieces, dg, h, w_full, name):
    s_len, d = h.shape
    n = w_full.shape[1]
    tk = WIDTH
    n_steps = n // tk
    n_pieces = len(pieces)
    tm = min(256, s_len)
    n_tiles = s_len // tm

    def body(*refs):
        piece_refs = refs[:n_pieces]
        dg_ref, h_ref, w_ref, dh_ref, gw_ref, acc_ref = refs[n_pieces:]
        s = pl.program_id(0)
        i = pl.program_id(1)
        rows = pl.ds(pl.multiple_of(i * tm, tm), tm)

        def step(piece):
            pb = piece.astype(BF16)
            contrib = _dot_nt(pb, w_ref[...])

            @pl.when(s == 0)
            def _():
                dh_ref[rows, :] = contrib

            @pl.when(s > 0)
            def _():
                dh_ref[rows, :] += contrib

            part = _dot_tn(h_ref[rows, :], pb)

            @pl.when(i == 0)
            def _():
                acc_ref[...] = part

            @pl.when(i > 0)
            def _():
                acc_ref[...] += part

        for p in range(n_pieces):
            @pl.when(s == p)
            def _(p=p):
                step(piece_refs[p][...])

        @pl.when(s >= n_pieces)
        def _():
            step(dg_ref[...])

        @pl.when(i == n_tiles - 1)
        def _():
            gw_ref[...] = acc_ref[...].astype(BF16)

    def piece_spec(p):
        return pl.BlockSpec((tm, tk), lambda s, i: (jnp.where(s == p, i, 0), 0))

    dg_spec = pl.BlockSpec((tm, tk), lambda s, i: (jnp.where(s >= n_pieces, i, 0),
                                                     jnp.maximum(s - n_pieces, 0)))
    return _pcall(
        body, name=name,
        out_shape=(jax.ShapeDtypeStruct((s_len, d), F32), jax.ShapeDtypeStruct((d, n), BF16)),
        grid=(n_steps, n_tiles),
        in_specs=[piece_spec(p) for p in range(n_pieces)] + [
            dg_spec,
            pl.BlockSpec((s_len, d), lambda s, i: (0, 0)),
            pl.BlockSpec((d, tk), lambda s, i: (0, s))],
        out_specs=(pl.BlockSpec((s_len, d), lambda s, i: (0, 0)),
                   pl.BlockSpec((d, tk), lambda s, i: (0, s))),
        scratch_shapes=[pltpu.VMEM((d, tk), F32)],
        semantics=("arbitrary", "arbitrary"))(*pieces, dg, h, w_full)


def _ln_bwd(dh, x, scale, dres, name):
    s_len, d = x.shape
    tm = min(512, s_len)

    def body(dh_ref, x_ref, sc_ref, dres_ref, dx_ref, vec_ref):
        @pl.when(pl.program_id(0) == 0)
        def _():
            vec_ref[...] = jnp.zeros_like(vec_ref)

        dh = dh_ref[...]
        xs, rstd = _standardize(x_ref[...])
        vec_ref[0:1, :] += jnp.sum(dh, axis=0, keepdims=True)
        vec_ref[1:2, :] += jnp.sum(dh * xs, axis=0, keepdims=True)
        dx_ref[...] = _standardize_bwd(xs, rstd, dh * (1.0 + sc_ref[...])) + dres_ref[...]

    tile = pl.BlockSpec((tm, d), lambda i: (i, 0))
    return _pcall(body, name=name, grid=(s_len // tm,),
                  out_shape=(jax.ShapeDtypeStruct((s_len, d), F32), jax.ShapeDtypeStruct((8, d), F32)),
                  in_specs=[tile, tile, pl.BlockSpec((1, d), lambda i: (0, 0)), tile],
                  out_specs=(tile, pl.BlockSpec((8, d), lambda i: (0, 0))),
                  semantics=("arbitrary",))(dh, x, scale, dres)


def _wmod_grad(c_t, dmod):
    d = c_t.shape[0]
    n_layers, _, cm = dmod.shape

    def body(c_ref, dm_ref, o_ref):
        for l in range(n_layers):
            acc = None
            for b in range(NDEV):
                term = c_ref[:, b:b + 1] * dm_ref[l, b:b + 1, :]
                acc = term if acc is None else acc + term
            o_ref[l] = acc

    return _pcall(body, name="wmod_grad", out_shape=jax.ShapeDtypeStruct((n_layers, d, cm), F32))(c_t, dmod)


def _sum_adamw(parts, w, m, v, name):
    n_src, rows, cols = parts.shape
    tr = rows
    for cand in (512, 256, 128, 64, 32, 16, 8):
        if rows % cand == 0 and cand * cols * 4 <= (2 << 20):
            tr = cand
            break

    def body(p_ref, w_ref, m_ref, v_ref, g_ref, d_ref, nm_ref, nv_ref):
        g = p_ref[0].astype(F32)
        for s in range(1, n_src):
            g = g + p_ref[s].astype(F32)
        nm = ADAM_B1 * m_ref[...] + (1.0 - ADAM_B1) * g
        nv = ADAM_B2 * v_ref[...] + (1.0 - ADAM_B2) * (g * g)
        m_hat = nm / (1.0 - ADAM_B1 ** ADAM_STEP)
        v_hat = nv / (1.0 - ADAM_B2 ** ADAM_STEP)
        g_ref[...] = g
        d_ref[...] = -ADAM_LR * (m_hat / (jnp.sqrt(v_hat) + ADAM_EPS) + ADAM_WD * w_ref[...])
        nm_ref[...] = nm
        nv_ref[...] = nv

    tile = pl.BlockSpec((tr, cols), lambda i: (i, 0))
    out = jax.ShapeDtypeStruct((rows, cols), F32)
    return _pcall(body, name=name, grid=(rows // tr,), out_shape=(out,) * 4,
                  in_specs=[pl.BlockSpec((n_src, tr, cols), lambda i: (0, i, 0)), tile, tile, tile],
                  out_specs=(tile,) * 4, semantics=("arbitrary",))(parts, w, m, v)


def _sum_parts(parts, name):
    n_src = parts.shape[0]

    def body(p_ref, o_ref):
        acc = p_ref[0]
        for s in range(1, n_src):
            acc = acc + p_ref[s]
        o_ref[...] = acc

    return _pcall(body, name=name, out_shape=jax.ShapeDtypeStruct(parts.shape[1:], F32))(parts)


def _lower_bound_table(lower_bounds):
    p = jax.nn.softmax(lower_bounds.astype(F32), axis=0)
    return jnp.cumsum(p, axis=0) - p[0:1]


def _pad_rows(v, width):
    n = v.shape[0]
    rows = -(-n // width)
    rows = -(-rows // 8) * 8
    return jnp.pad(v, (0, rows * width - n)).reshape(rows, width)


def kernel(x, c, w_mod, b_mod, w_in, conv_w, hgrn_norm_w, lower_bounds, w_branch, w_out, ln_g, ln_b, loss_target, m_w_mod, m_b_mod, m_w_in, m_conv_w, m_hgrn_norm_w, m_lower_bounds, m_w_branch, m_w_out, m_ln_g, m_ln_b, v_w_mod, v_b_mod, v_w_in, v_conv_w, v_hgrn_norm_w, v_lower_bounds, v_w_branch, v_w_out, v_ln_g, v_ln_b):
    n_layers = N_LAYERS
    s_len, d = x.shape[1], x.shape[2]
    n_cols = w_in.shape[2] * NDEV
    cw_cols = conv_w.shape[2]
    cm = w_mod.shape[2]
    me = _my_index()
    x0 = x[0]
    target = loss_target[0]

    small = _pad_rows(jnp.concatenate([c.reshape(-1), conv_w.reshape(-1)]), BLK)
    small_all = _all_gather_small("gather_c_conv", small).reshape(NDEV, -1)
    c_all = small_all[:, :d]
    conv_full = small_all[:, d:d + n_layers * 3 * cw_cols].reshape(NDEV, n_layers, 3, cw_cols)
    conv_full = conv_full.transpose(1, 2, 0, 3).reshape(n_layers, 3, WIDTH)

    b_mod_mine = lax.dynamic_slice_in_dim(b_mod, me * cm, cm, axis=1).reshape(n_layers, 1, cm)
    mod_cols = _mod_fwd(c_all, w_mod, b_mod_mine)
    mod_all = _all_gather_small("gather_mod", mod_cols.reshape(n_layers * NDEV, cm))
    mod_all = mod_all.reshape(NDEV, n_layers, NDEV, cm)
    mod_mine = lax.dynamic_index_in_dim(mod_all, me, axis=2, keepdims=False)
    mod_mine = mod_mine.transpose(1, 0, 2).reshape(n_layers, 3, 1, d)

    shard = w_in.shape[2]
    w_in_b, w_branch_b, w_out_b = w_in.astype(BF16), w_branch.astype(BF16), w_out.astype(BF16)

    def in_dst(l):
        return lambda ref, dev: ref.at[l, :, pl.ds(pl.multiple_of(dev * shard, BLK), shard)]

    transfers = []
    for l in range(n_layers):
        transfers.append((0, 0, (lambda l: lambda ref, dev: ref.at[l])(l), in_dst(l)))
    transfers.append((1, 1, _whole, _slot))
    transfers.append((2, 2, _whole, _slot))
    w_in_full, w_branch_all, w_out_all = _exchange(
        "gather_weights", [w_in_b, w_branch_b, w_out_b],
        [jax.ShapeDtypeStruct((n_layers, d, n_cols), BF16),
         jax.ShapeDtypeStruct((NDEV,) + w_branch_b.shape, BF16),
         jax.ShapeDtypeStruct((NDEV,) + w_out_b.shape, BF16)],
        transfers, in_vmem=False)
    wb_full = w_branch_all.transpose(1, 2, 3, 0, 4).reshape(n_layers, 3, WIDTH, d)
    wo_full = w_out_all.transpose(1, 0, 2, 3).reshape(n_layers, d, d)

    lbs = _lower_bound_table(lower_bounds)
    norm_w4 = jnp.tile(hgrn_norm_w, (1, WIDTH // HG_HEAD_DIM))

    saved = []
    xl = x0
    for l in range(n_layers):
        shift, scale, gate = mod_mine[l, 0], mod_mine[l, 1], mod_mine[l, 2]
        proj, h = _ln_proj(xl, shift, scale, w_in_full[l], f"ln_proj_{l}")
        o_a, totals = _sb_fwd(proj, f"sb_fwd_{l}")
        o_b = _hgrn_fwd(proj, lbs[l:l + 1], f"hgrn_fwd_{l}")
        x_new, merged, ycat = _merge_fwd(xl, proj, o_a, o_b, gate, norm_w4[l:l + 1], conv_full[l],
                                         wb_full[l], wo_full[l], ln_g[l:l + 1], ln_b[l:l + 1], f"merge_fwd_{l}")
        saved.append((xl, proj, h, o_a, totals, o_b, merged, ycat))
        xl = x_new

    loss_part, dx = _loss_fwd_bwd(xl, target)
    loss = lax.psum(loss_part[0, 0], ("x", "y", "c"))

    gw_in, gw_branch, gw_out = [None] * n_layers, [None] * n_layers, [None] * n_layers
    small_grads = [None] * n_layers
    dmod = [None] * n_layers
    for l in reversed(range(n_layers)):
        xl, proj, h, o_a, totals, o_b, merged, ycat = saved[l]
        scale, gate = mod_mine[l, 1], mod_mine[l, 2]
        dres, dycat, dg, gwo, gwb, mvec = _merge_bwd(dx, xl, merged, ycat, proj, gate, wb_full[l], wo_full[l],
                                                     ln_g[l:l + 1], f"merge_bwd_{l}")
        do_a, do_b, dz_a, dz_b, dpre, dpost, du, dz_c, bvec = _branch_bwd(
            dycat, proj, o_a, o_b, norm_w4[l:l + 1], conv_full[l], f"branch_bwd_{l}")
        dq_a, dk_a, dv_a = _sb_bwd(proj, do_a, totals, f"sb_bwd_{l}")
        dq_b, df_b, di_b, dlb = _hgrn_bwd(proj, do_b, lbs[l:l + 1], f"hgrn_bwd_{l}")
        pieces = [dq_a, dk_a, dv_a, dz_a, dq_b, df_b, di_b, dz_b, dpre, dpost, du, dz_c]
        dh, gwi = _in_bwd(pieces, dg, h, w_in_full[l], f"in_bwd_{l}")
        dx, lvec = _ln_bwd(dh, xl, scale, dres, f"ln_bwd_{l}")
        gw_in[l], gw_branch[l], gw_out[l] = gwi, gwb, gwo
        dmod[l] = jnp.concatenate([lvec[0], lvec[1], mvec[2]])
        norm_grad = bvec[0].reshape(WIDTH // HG_HEAD_DIM, HG_HEAD_DIM).sum(axis=0)
        small_grads[l] = jnp.concatenate([mvec[0], mvec[1], norm_grad, dlb[0], bvec[1:4].reshape(-1)])
    grad_x = dx[None]

    gw_in_all = jnp.stack(gw_in)
    gw_branch_all = jnp.stack(gw_branch).astype(BF16).reshape(n_layers, 3, WIDTH, NDEV, d // NDEV)
    gw_branch_all = gw_branch_all.transpose(3, 0, 1, 2, 4)
    gw_out_all = jnp.stack(gw_out).astype(BF16).reshape(n_layers, NDEV, d // NDEV, d).transpose(1, 0, 2, 3)

    def in_src(l):
        return lambda ref, dev: ref.at[l, :, pl.ds(pl.multiple_of(dev * shard, BLK), shard)]

    transfers = []
    for l in range(n_layers):
        transfers.append((0, 0, in_src(l), (lambda l: lambda ref, dev: ref.at[dev, l])(l)))
    transfers.append((1, 1, _slot, _slot))
    transfers.append((2, 2, _slot, _slot))
    p_in, p_branch, p_out = _exchange(
        "scatter_grads", [gw_in_all, gw_branch_all, gw_out_all],
        [jax.ShapeDtypeStruct((NDEV, n_layers, d, shard), BF16),
         jax.ShapeDtypeStruct(gw_branch_all.shape, BF16),
         jax.ShapeDtypeStruct(gw_out_all.shape, BF16)],
        transfers, in_vmem=False)

    small_vec = jnp.concatenate(dmod + small_grads)
    n_small = small_vec.shape[0]
    small_all = _all_gather_small("gather_small_grads", _pad_rows(small_vec, BLK))
    small_sum = _sum_parts(small_all, "sum_small_grads").reshape(-1)[:n_small]
    dmod_all = small_all.reshape(NDEV, -1)[:, :n_layers * 3 * d].reshape(NDEV, n_layers, 3 * d)

    off = n_layers * 3 * d
    grad_b_mod = small_sum[:off].reshape(n_layers, 3 * d)
    per_layer = 2 * d + HG_HEAD_DIM + WIDTH + 3 * WIDTH
    g_ln_g, g_ln_b, g_norm, g_lbs, g_conv = [], [], [], [], []
    for l in range(n_layers):
        seg = small_sum[off + l * per_layer: off + (l + 1) * per_layer]
        g_ln_g.append(seg[:d])
        g_ln_b.append(seg[d:2 * d])
        g_norm.append(seg[2 * d:2 * d + HG_HEAD_DIM])
        g_lbs.append(seg[2 * d + HG_HEAD_DIM:2 * d + HG_HEAD_DIM + WIDTH])
        g_conv.append(seg[2 * d + HG_HEAD_DIM + WIDTH:].reshape(3, WIDTH))
    grad_ln_g, grad_ln_b = jnp.stack(g_ln_g), jnp.stack(g_ln_b)
    grad_norm = jnp.stack(g_norm)
    _, lbs_vjp = jax.vjp(_lower_bound_table, lower_bounds)
    grad_lower = lbs_vjp(jnp.stack(g_lbs))[0]
    grad_conv = lax.dynamic_slice_in_dim(jnp.stack(g_conv), me * cw_cols, cw_cols, axis=2)

    dmod_mine = lax.dynamic_slice_in_dim(dmod_all, me * cm, cm, axis=2).transpose(1, 0, 2)
    grad_w_mod = _wmod_grad(c_all.T, dmod_mine)

    def adam(parts, w, m, v, name):
        shape = w.shape
        cols = shape[-1]
        flat = lambda a: a.reshape(-1, cols)
        outs = _sum_adamw(parts.reshape(parts.shape[0], -1, cols), flat(w), flat(m), flat(v), name)
        return [o.reshape(shape) for o in outs]

    r_w_in = adam(p_in, w_in, m_w_in, v_w_in, "adamw_w_in")
    r_w_branch = adam(p_branch, w_branch, m_w_branch, v_w_branch, "adamw_w_branch")
    r_w_out = adam(p_out, w_out, m_w_out, v_w_out, "adamw_w_out")
    r_w_mod = adam(grad_w_mod[None], w_mod, m_w_mod, v_w_mod, "adamw_w_mod")

    small_names = ["b_mod", "conv_w", "hgrn_norm_w", "lower_bounds", "ln_g", "ln_b"]
    small_g = [grad_b_mod, grad_conv, grad_norm, grad_lower, grad_ln_g, grad_ln_b]
    small_w = [b_mod, conv_w, hgrn_norm_w, lower_bounds, ln_g, ln_b]
    small_m = [m_b_mod, m_conv_w, m_hgrn_norm_w, m_lower_bounds, m_ln_g, m_ln_b]
    small_v = [v_b_mod, v_conv_w, v_hgrn_norm_w, v_lower_bounds, v_ln_g, v_ln_b]
    pack = lambda arrs: _pad_rows(jnp.concatenate([a.reshape(-1) for a in arrs]), BLK)
    packed = _sum_adamw(pack(small_g)[None], pack(small_w), pack(small_m), pack(small_v), "adamw_small")
    r_small = {n: [] for n in small_names}
    for res in packed:
        flat = res.reshape(-1)
        pos = 0
        for n, w in zip(small_names, small_w):
            r_small[n].append(flat[pos:pos + w.size].reshape(w.shape))
            pos += w.size

    results = {"w_mod": r_w_mod, "w_in": r_w_in, "w_branch": r_w_branch, "w_out": r_w_out, **r_small}
    order = ["w_mod", "b_mod", "w_in", "conv_w", "hgrn_norm_w", "lower_bounds", "w_branch", "w_out", "ln_g", "ln_b"]
    outs = [loss, grad_x]
    for idx in range(4):
        outs.extend(results[n][idx] for n in order)
    return tuple(outs)
```

```python
import jax
import jax.numpy as jnp
from jax import lax
from jax.experimental import pallas as pl
from jax.experimental.pallas import tpu as pltpu

F32 = jnp.float32
BF16 = jnp.bfloat16
NDEV = 8
N_LAYERS = 2
SB_HEAD_DIM = 64
HG_HEAD_DIM = 128
WIDTH = 512
BLK = 128
LN_EPS = 1e-5
RMS_EPS = 1e-6
ALPHA = (2.0 * N_LAYERS) ** 0.25
ADAM_LR, ADAM_B1, ADAM_B2, ADAM_EPS, ADAM_WD, ADAM_STEP = 0.001, 0.9, 0.999, 1e-08, 0.01, 10
VMEM_LIMIT = 56 * 1024 * 1024
MESH = pl.DeviceIdType.MESH
HG_LEVELS = (64, 32, 16, 8, 4, 2, 1)


def _pcall(body, *, name, out_shape, grid=None, in_specs=None, out_specs=None, scratch_shapes=(),
           semantics=None, aliases=None):
    kwargs = {}
    if grid is not None:
        kwargs["grid"] = grid
    if in_specs is not None:
        kwargs["in_specs"] = in_specs
    if out_specs is not None:
        kwargs["out_specs"] = out_specs
    if aliases:
        kwargs["input_output_aliases"] = aliases
    return pl.pallas_call(
        body, name=name, out_shape=out_shape, scratch_shapes=list(scratch_shapes),
        compiler_params=pltpu.CompilerParams(dimension_semantics=semantics, vmem_limit_bytes=VMEM_LIMIT),
        interpret=False, **kwargs)


def _dot(a, b):
    return jnp.dot(a, b, preferred_element_type=F32)


def _dot_nt(a, b):
    return lax.dot_general(a, b, (((1,), (1,)), ((), ())), preferred_element_type=F32)


def _dot_tn(a, b):
    return lax.dot_general(a, b, (((0,), (0,)), ((), ())), preferred_element_type=F32)


def _split3(x):
    x1 = x.astype(BF16)
    r1 = x - x1.astype(F32)
    x2 = r1.astype(BF16)
    r2 = r1 - x2.astype(F32)
    return x1, x2, r2.astype(BF16)


def _split2(x):
    x1 = x.astype(BF16)
    return x1, (x - x1.astype(F32)).astype(BF16)


def _dot_exact_l(m_bf16, x):
    x1, x2, x3 = _split3(x)
    return _dot(m_bf16, x1) + _dot(m_bf16, x2) + _dot(m_bf16, x3)


def _sigmoid(x):
    return 1.0 / (1.0 + jnp.exp(-x))


def _silu_and_grad(x):
    s = _sigmoid(x)
    return x * s, s * (1.0 + x * (1.0 - s))


def _softplus_parts(z):
    e = jnp.exp(-jnp.abs(z))
    sp = jnp.maximum(z, 0.0) + jnp.log(1.0 + e)
    r = 1.0 / (1.0 + e)
    return sp, jnp.where(z >= 0.0, r, e * r)


def _iota2(shape, dim):
    return lax.broadcasted_iota(jnp.int32, shape, dim)


def _standardize(x):
    mu = jnp.mean(x, axis=-1, keepdims=True)
    xc = x - mu
    var = jnp.mean(xc * xc, axis=-1, keepdims=True)
    rstd = lax.rsqrt(var + LN_EPS)
    return xc * rstd, rstd


def _standardize_bwd(xhat, rstd, dxhat):
    m1 = jnp.mean(dxhat, axis=-1, keepdims=True)
    m2 = jnp.mean(dxhat * xhat, axis=-1, keepdims=True)
    return rstd * (dxhat - m1 - xhat * m2)


def _my_index():
    return 4 * lax.axis_index("x") + 2 * lax.axis_index("y") + lax.axis_index("c")


def _exchange(name, ins, out_shapes, transfers, in_vmem):
    n_in, n_out, n_t = len(ins), len(out_shapes), len(transfers)

    def body(*refs):
        in_refs, out_refs = refs[:n_in], refs[n_in:n_in + n_out]
        send_sems, recv_sems, local_sems = refs[n_in + n_out:]
        x, y, c = lax.axis_index("x"), lax.axis_index("y"), lax.axis_index("c")
        me = 4 * x + 2 * y + c
        started = []
        for t, (i, o, src_fn, dst_fn) in enumerate(transfers):
            own = pltpu.make_async_copy(src_fn(in_refs[i], me), dst_fn(out_refs[o], me), local_sems.at[t])
            own.start()
            started.append(own)
        arrivals = []
        for k in range(1, NDEV):
            px = x ^ ((k >> 2) & 1)
            py = y ^ ((k >> 1) & 1)
            pc = c ^ (k & 1)
            peer = 4 * px + 2 * py + pc
            for t, (i, o, src_fn, dst_fn) in enumerate(transfers):
                sem = t * (NDEV - 1) + k - 1
                push = pltpu.make_async_remote_copy(
                    src_ref=src_fn(in_refs[i], peer), dst_ref=dst_fn(out_refs[o], me),
                    send_sem=send_sems.at[sem], recv_sem=recv_sems.at[sem],
                    device_id=(px, py, pc), device_id_type=MESH)
                push.start()
                started.append(push)
                arrivals.append(pltpu.make_async_remote_copy(
                    src_ref=src_fn(in_refs[i], peer), dst_ref=dst_fn(out_refs[o], peer),
                    send_sem=send_sems.at[sem], recv_sem=recv_sems.at[sem],
                    device_id=(px, py, pc), device_id_type=MESH))
        for arrival in arrivals:
            arrival.wait_recv()
        for cp in started[n_t:]:
            cp.wait_send()
        for own in started[:n_t]:
            own.wait()

    space = pltpu.VMEM if in_vmem else pl.ANY
    spec = pl.BlockSpec(memory_space=space)
    return _pcall(
        body, name=name, out_shape=out_shapes,
        in_specs=[spec] * n_in, out_specs=[spec] * n_out,
        scratch_shapes=[pltpu.SemaphoreType.DMA((n_t * (NDEV - 1),)),
                        pltpu.SemaphoreType.DMA((n_t * (NDEV - 1),)),
                        pltpu.SemaphoreType.DMA((n_t,))])(*ins)


def _whole(ref, dev):
    return ref


def _slot(ref, dev):
    return ref.at[dev]


def _all_gather_small(name, v):
    out = _exchange(name, [v], [jax.ShapeDtypeStruct((NDEV,) + v.shape, v.dtype)],
                    [(0, 0, _whole, _slot)], in_vmem=True)
    return out[0]


def _mod_fwd(c_all, w_mod, b_mod_mine):
    n_layers, _, cm = w_mod.shape

    def body(c_ref, w_ref, b_ref, o_ref):
        for l in range(n_layers):
            o_ref[l] = jnp.dot(c_ref[...], w_ref[l], preferred_element_type=F32,
                               precision=lax.Precision.HIGHEST) + b_ref[l]

    return _pcall(body, name="mod_fwd", out_shape=jax.ShapeDtypeStruct((n_layers, NDEV, cm), F32))(
        c_all, w_mod, b_mod_mine)


def _ln_proj(x, shift, scale, w_full, name):
    s_len, d = x.shape
    n = w_full.shape[1]
    tm = min(512, s_len)
    tn = 1024

    def body(x_ref, sh_ref, sc_ref, w_ref, proj_ref, ht_ref, h_scr):
        @pl.when(pl.program_id(1) == 0)
        def _():
            xs, _ = _standardize(x_ref[...])
            h = xs * (1.0 + sc_ref[...]) + sh_ref[...]
            h_scr[...] = h.astype(BF16)
            ht_ref[...] = h.T.astype(BF16)

        proj_ref[...] = _dot(h_scr[...], w_ref[...])

    return _pcall(
        body, name=name,
        out_shape=(jax.ShapeDtypeStruct((s_len, n), F32), jax.ShapeDtypeStruct((d, s_len), BF16)),
        grid=(s_len // tm, n // tn),
        in_specs=[pl.BlockSpec((tm, d), lambda i, j: (i, 0)),
                  pl.BlockSpec((1, d), lambda i, j: (0, 0)),
                  pl.BlockSpec((1, d), lambda i, j: (0, 0)),
                  pl.BlockSpec((d, tn), lambda i, j: (0, j))],
        out_specs=(pl.BlockSpec((tm, tn), lambda i, j: (i, j)),
                   pl.BlockSpec((d, tm), lambda i, j: (0, i))),
        scratch_shapes=[pltpu.VMEM((tm, d), BF16)],
        semantics=("arbitrary", "arbitrary"))(x, shift, scale, w_full)


def _sb_group_blocks(nb):
    return min(4, nb)


def _sb_fwd(proj, name):
    s_len = proj.shape[0]
    nb = s_len // BLK
    n_pairs = WIDTH // BLK
    gb = _sb_group_blocks(nb)
    kw = gb * BLK

    def body(q_ref, k_ref, v_ref, o_ref, tot_ref):
        lane = _iota2((1, BLK), 1)
        row = _iota2((BLK, BLK), 0)
        col = _iota2((BLK, BLK), 1)
        suffix_and_sum = jnp.concatenate([(row >= col).astype(BF16), jnp.ones((BLK, BLK), BF16)], axis=1)
        qpos = _iota2((BLK, kw), 0)
        kpos = _iota2((BLK, kw), 1)
        head_lanes = [(lane // SB_HEAD_DIM) == hh for hh in range(2)]

        def group(i, gi, qms, carry, masked):
            c0 = pl.multiple_of(gi * kw, kw)
            kb = k_ref[pl.ds(c0, kw), :].astype(BF16)
            vf = v_ref[pl.ds(c0, kw), :]
            if masked:
                valid = (c0 + kpos) < (i * BLK + qpos)
            out = []
            for hh in range(2):
                o_acc, later = carry[2 * hh], carry[2 * hh + 1]
                z = _dot_nt(qms[hh], kb)
                sp, _ = _softplus_parts(z)
                lg = -sp
                if masked:
                    lg = jnp.where(valid, lg, 0.0)
                l1, l2 = _split2(lg)
                parts = [None] * gb
                for b in reversed(range(gb)):
                    sl = slice(b * BLK, (b + 1) * BLK)
                    both = _dot(l1[:, sl], suffix_and_sum) + _dot(l2[:, sl], suffix_and_sum)
                    parts[b] = both[:, :BLK] + later
                    later = later + both[:, BLK:]
                a = jnp.exp(z + jnp.concatenate(parts, axis=1))
                if masked:
                    a = jnp.where(valid, a, 0.0)
                vm = jnp.where(head_lanes[hh], vf, 0.0).astype(BF16)
                out += [o_acc + _dot(a.astype(BF16), vm), later]
            return tuple(out)

        def qblock(i, _):
            r0 = pl.multiple_of(i * BLK, BLK)
            qf = q_ref[pl.ds(r0, BLK), :] * (SB_HEAD_DIM ** -0.5)
            qms = [jnp.where(head_lanes[hh], qf, 0.0).astype(BF16) for hh in range(2)]
            zero = jnp.zeros((BLK, BLK), F32)
            last = i // gb
            carry = group(i, last, qms, (zero, zero, zero, zero), True)
            carry = lax.fori_loop(0, last, lambda jj, cr: group(i, last - 1 - jj, qms, cr, False), carry)
            o_ref[pl.ds(r0, BLK), :] = carry[0] + carry[2]
            tot_ref[0, pl.ds(r0, BLK), :] = carry[1]
            tot_ref[1, pl.ds(r0, BLK), :] = carry[3]
            return 0

        lax.fori_loop(0, nb, qblock, 0)

    col_spec = lambda off: pl.BlockSpec((s_len, BLK), lambda p: (0, off + p))
    return _pcall(
        body, name=name,
        out_shape=(jax.ShapeDtypeStruct((s_len, WIDTH), F32),
                   jax.ShapeDtypeStruct((2 * n_pairs, s_len, BLK), F32)),
        grid=(n_pairs,),
        in_specs=[col_spec(0), col_spec(n_pairs), col_spec(2 * n_pairs)],
        out_specs=(pl.BlockSpec((s_len, BLK), lambda p: (0, p)),
                   pl.BlockSpec((2, s_len, BLK), lambda p: (p, 0, 0))),
        semantics=("arbitrary",))(proj, proj, proj)


def _hg_masks(mask_ref):
    row = _iota2((BLK, BLK), 0)
    col = _iota2((BLK, BLK), 1)
    for v, m in enumerate(HG_LEVELS):
        same = (row // (2 * m)) == (col // (2 * m))
        mask_ref[v] = (same & ((row & m) != 0) & ((col & m) == 0)).astype(F32)


def _hg_mid(b, m):
    if m >= 4:
        n = BLK // (2 * m)
        mid = b.reshape(n, 2 * m, BLK)[:, m - 1:m, :]
        return jnp.broadcast_to(mid, (n, 2 * m, BLK)).reshape(BLK, BLK)
    pos = _iota2((BLK, BLK), 0) & (2 * m - 1)
    out = b
    for p in range(2 * m):
        delta = (m - 1) - p
        if delta != 0:
            out = jnp.where(pos == p, pltpu.roll(b, (-delta) % BLK, 0), out)
    return out


def _hg_chunk_inputs(qraw, fpre, lb):
    sig = _sigmoid(fpre)
    f = lb + (1.0 - lb) * sig
    g = jnp.log(f)
    q, dq_fac = _silu_and_grad(qraw)
    return q, dq_fac, f, sig, g


def _hg_level_terms(q, k, b, v_idx, m, mask_ref):
    mid = _hg_mid(b, m)
    eq = jnp.exp(jnp.minimum(b - mid, 0.0))
    ek = jnp.exp(jnp.minimum(mid - b, 0.0))
    qt = (q * eq).astype(BF16)
    kt = (k * ek).astype(BF16)
    return qt, kt, eq, ek, mask_ref[v_idx]


def _hg_scores(q, k, b, mask_ref):
    sc = None
    for v_idx, m in enumerate(HG_LEVELS):
        qt, kt, _, _, msk = _hg_level_terms(q, k, b, v_idx, m, mask_ref)
        term = _dot_nt(qt, kt) * msk
        sc = term if sc is None else sc + term
    return sc


def _hgrn_fwd(proj, lb, name):
    s_len = proj.shape[0]
    nc = s_len // BLK
    nh = WIDTH // HG_HEAD_DIM
    base = 4 * WIDTH // BLK

    def body(q_ref, f_ref, i_ref, lb_ref, o_ref, mask_ref):
        _hg_masks(mask_ref)
        row = _iota2((BLK, BLK), 0)
        col = _iota2((BLK, BLK), 1)
        lower_incl = (col <= row).astype(BF16)
        lb_v = lb_ref[...]

        def chunk(ci, st):
            r0 = pl.multiple_of(ci * BLK, BLK)
            q, _, f, _, g = _hg_chunk_inputs(q_ref[pl.ds(r0, BLK), :], f_ref[pl.ds(r0, BLK), :], lb_v)
            k = 1.0 - f
            v = i_ref[pl.ds(r0, BLK), :]
            vb = v.astype(BF16)
            b = _dot_exact_l(lower_incl, g)
            b_end = b[BLK - 1:BLK, :]
            inter = _dot_nt((q * jnp.exp(b)).astype(BF16), st.astype(BF16))
            sc = _hg_scores(q, k, b, mask_ref)
            diag = jnp.sum(q * k, axis=-1, keepdims=True)
            o_ref[pl.ds(r0, BLK), :] = inter + _dot(sc.astype(BF16), vb) + diag * v
            k_dec = (k * jnp.exp(b_end - b)).astype(BF16)
            return st * jnp.exp(b_end) + _dot_tn(vb, k_dec)

        lax.fori_loop(0, nc, chunk, jnp.zeros((HG_HEAD_DIM, HG_HEAD_DIM), F32))

    col_spec = lambda off: pl.BlockSpec((s_len, BLK), lambda h: (0, off + h))
    return _pcall(
        body, name=name, out_shape=jax.ShapeDtypeStruct((s_len, WIDTH), F32),
        grid=(nh,),
        in_specs=[col_spec(base), col_spec(base + nh), col_spec(base + 2 * nh),
                  pl.BlockSpec((1, BLK), lambda h: (0, h))],
        out_specs=pl.BlockSpec((s_len, BLK), lambda h: (0, h)),
        scratch_shapes=[pltpu.VMEM((len(HG_LEVELS), BLK, BLK), F32)],
        semantics=("arbitrary",))(proj, proj, proj, lb)


def _rms_heads(o_b, norm_w):
    n_parts, h_parts, r_parts = [], [], []
    for h in range(WIDTH // HG_HEAD_DIM):
        sl = slice(h * HG_HEAD_DIM, (h + 1) * HG_HEAD_DIM)
        o = o_b[:, sl]
        rstd = lax.rsqrt(jnp.mean(o * o, axis=-1, keepdims=True) + RMS_EPS)
        ohat = o * rstd
        h_parts.append(ohat)
        n_parts.append(ohat * norm_w[:, sl])
        r_parts.append(jnp.broadcast_to(rstd, o.shape))
    cat = lambda parts: jnp.concatenate(parts, axis=-1)
    return cat(n_parts), cat(h_parts), cat(r_parts)


def _shift_rows_down(halo, cur, k):
    tm = cur.shape[0]
    ext = jnp.concatenate([halo, cur], axis=0)
    return pltpu.roll(ext, k, 0)[8:8 + tm]


def _shift_rows_up(cur, halo, k):
    tm = cur.shape[0]
    ext = jnp.concatenate([cur, halo], axis=0)
    return pltpu.roll(ext, (tm + 8 - k) % (tm + 8), 0)[0:tm]


def _merge_fwd(x, proj, o_a, o_b, gate, norm_w, conv_w, wb, w_out, ln_g, ln_b, name):
    s_len, d = x.shape
    tm = min(256, s_len)
    hb = tm // 8

    def body(x_ref, oa_ref, za_ref, ob_ref, zb_ref, pre_ref, post_ref, u_ref, zc_ref, hpre_ref, hu_ref, g_ref,
             gate_ref, nw_ref, cw_ref, wb_ref, wo_ref, lg_ref, lbias_ref, xn_ref, mg_ref, yc_ref):
        i = pl.program_id(0)
        sa, _ = _silu_and_grad(za_ref[...])
        y_a = (oa_ref[...] * sa).astype(BF16)
        n_b, _, _ = _rms_heads(ob_ref[...], nw_ref[...])
        sb, _ = _silu_and_grad(zb_ref[...])
        y_b = (n_b * sb).astype(BF16)
        a = pre_ref[...] * u_ref[...]
        halo = jnp.where(i > 0, hpre_ref[...] * hu_ref[...], 0.0)
        cw = cw_ref[...]
        conv = cw[0:1] * _shift_rows_down(halo, a, 2) + cw[1:2] * _shift_rows_down(halo, a, 1) + cw[2:3] * a
        sc, _ = _silu_and_grad(zc_ref[...])
        y_c = (post_ref[...] * conv * sc).astype(BF16)
        merged = None
        for k, yk in enumerate((y_a, y_b, y_c)):
            yc_ref[:, k * WIDTH:(k + 1) * WIDTH] = yk
            term = _sigmoid(g_ref[:, k * d:(k + 1) * d]) * _dot(yk, wb_ref[k])
            merged = term if merged is None else merged + term
        mb = merged.astype(BF16)
        mg_ref[...] = mb
        y = _dot(mb, wo_ref[...])
        r = ALPHA * x_ref[...] + (1.0 + gate_ref[...]) * y
        rhat, _ = _standardize(r)
        xn_ref[...] = rhat * lg_ref[...] + lbias_ref[...]

    wcol = lambda cb: pl.BlockSpec((tm, WIDTH), lambda i: (i, cb))
    halo_spec = lambda cb: pl.BlockSpec((8, WIDTH), lambda i: (jnp.maximum(i * hb - 1, 0), cb))
    vec = lambda w: pl.BlockSpec((1, w), lambda i: (0, 0))
    return _pcall(
        body, name=name,
        out_shape=(jax.ShapeDtypeStruct((s_len, d), F32), jax.ShapeDtypeStruct((s_len, d), BF16),
                   jax.ShapeDtypeStruct((s_len, 3 * WIDTH), BF16)),
        grid=(s_len // tm,),
        in_specs=[pl.BlockSpec((tm, d), lambda i: (i, 0)),
                  wcol(0), wcol(3), wcol(0), wcol(7), wcol(8), wcol(9), wcol(10), wcol(11),
                  halo_spec(8), halo_spec(10),
                  pl.BlockSpec((tm, 3 * d), lambda i: (i, 2)),
                  vec(d), vec(WIDTH),
                  pl.BlockSpec((3, WIDTH), lambda i: (0, 0)),
                  pl.BlockSpec((3, WIDTH, d), lambda i: (0, 0, 0)),
                  pl.BlockSpec((d, d), lambda i: (0, 0)),
                  vec(d), vec(d)],
        out_specs=(pl.BlockSpec((tm, d), lambda i: (i, 0)), pl.BlockSpec((tm, d), lambda i: (i, 0)),
                   pl.BlockSpec((tm, 3 * WIDTH), lambda i: (i, 0))),
        semantics=("arbitrary",))(x, o_a, proj, o_b, proj, proj, proj, proj, proj, proj, proj, proj,
                                  gate, norm_w, conv_w, wb, w_out, ln_g, ln_b)


def _loss_fwd_bwd(y, target):
    s_len, d = y.shape
    tm = min(512, s_len)

    def body(y_ref, t_ref, loss_ref, dy_ref):
        @pl.when(pl.program_id(0) == 0)
        def _():
            loss_ref[...] = jnp.zeros_like(loss_ref)

        e = y_ref[...] - t_ref[...]
        dy_ref[...] = e * (1.0 / d)
        part = jnp.sum(jnp.sum(e * e, axis=-1, keepdims=True), axis=0, keepdims=True)
        loss_ref[...] += part * (0.5 / d)

    tile = pl.BlockSpec((tm, d), lambda i: (i, 0))
    return _pcall(body, name="loss", grid=(s_len // tm,),
                  out_shape=(jax.ShapeDtypeStruct((1, 1), F32), jax.ShapeDtypeStruct((s_len, d), F32)),
                  in_specs=[tile, tile],
                  out_specs=(pl.BlockSpec((1, 1), lambda i: (0, 0)), tile),
                  semantics=("arbitrary",))(y, target)


def _merge_bwd(dxn, x, merged, ycat, proj, gate, wb, w_out, ln_g, name):
    s_len, d = x.shape
    tm = min(256, s_len)

    def body(dxn_ref, x_ref, mg_ref, yc_ref, g_ref, gate_ref, wb_ref, wo_ref, lg_ref,
             dres_ref, dyc_ref, dg_ref, gwo_ref, gwb_ref, vec_ref):
        @pl.when(pl.program_id(0) == 0)
        def _():
            gwo_ref[...] = jnp.zeros_like(gwo_ref)
            gwb_ref[...] = jnp.zeros_like(gwb_ref)
            vec_ref[...] = jnp.zeros_like(vec_ref)

        mb = mg_ref[...]
        one_gate = 1.0 + gate_ref[...]
        y = _dot(mb, wo_ref[...])
        r = ALPHA * x_ref[...] + one_gate * y
        rhat, rstd = _standardize(r)
        dxn = dxn_ref[...]
        dr = _standardize_bwd(rhat, rstd, dxn * lg_ref[...])
        vec_ref[0:1, :] += jnp.sum(dxn * rhat, axis=0, keepdims=True)
        vec_ref[1:2, :] += jnp.sum(dxn, axis=0, keepdims=True)
        vec_ref[2:3, :] += jnp.sum(dr * y, axis=0, keepdims=True)
        dres_ref[...] = ALPHA * dr
        dy = (one_gate * dr).astype(BF16)
        gwo_ref[...] += _dot_tn(mb, dy)
        dmerged = _dot_nt(dy, wo_ref[...])
        for k in range(3):
            yk = yc_ref[:, k * WIDTH:(k + 1) * WIDTH]
            sg = _sigmoid(g_ref[:, k * d:(k + 1) * d])
            pk = _dot(yk, wb_ref[k])
            dg_ref[:, k * d:(k + 1) * d] = (dmerged * pk * sg * (1.0 - sg)).astype(BF16)
            dpk = (dmerged * sg).astype(BF16)
            dyc_ref[:, k * WIDTH:(k + 1) * WIDTH] = _dot_nt(dpk, wb_ref[k])
            gwb_ref[k] += _dot_tn(yk, dpk)

    tile = lambda w: pl.BlockSpec((tm, w), lambda i: (i, 0))
    vec = pl.BlockSpec((1, d), lambda i: (0, 0))
    return _pcall(
        body, name=name,
        out_shape=(jax.ShapeDtypeStruct((s_len, d), F32), jax.ShapeDtypeStruct((s_len, 3 * WIDTH), F32),
                   jax.ShapeDtypeStruct(proj.shape, BF16), jax.ShapeDtypeStruct((d, d), F32),
                   jax.ShapeDtypeStruct((3, WIDTH, d), F32), jax.ShapeDtypeStruct((8, d), F32)),
        grid=(s_len // tm,),
        in_specs=[tile(d), tile(d), tile(d), tile(3 * WIDTH),
                  pl.BlockSpec((tm, 3 * d), lambda i: (i, 2)),
                  vec, pl.BlockSpec((3, WIDTH, d), lambda i: (0, 0, 0)),
                  pl.BlockSpec((d, d), lambda i: (0, 0)), vec],
        out_specs=(tile(d), tile(3 * WIDTH), pl.BlockSpec((tm, 3 * d), lambda i: (i, 2)),
                   pl.BlockSpec((d, d), lambda i: (0, 0)),
                   pl.BlockSpec((3, WIDTH, d), lambda i: (0, 0, 0)),
                   pl.BlockSpec((8, d), lambda i: (0, 0))),
        semantics=("arbitrary",))(dxn, x, merged, ycat, proj, gate, wb, w_out, ln_g)


def _branch_bwd(dycat, proj, o_a, o_b, norm_w, conv_w, dproj, name):
    s_len = proj.shape[0]
    tm = min(256, s_len)
    hb = tm // 8
    n_tiles = s_len // tm

    def body(dya_ref, dyb_ref, dyc_ref, oa_ref, za_ref, ob_ref, zb_ref, pre_ref, post_ref, u_ref, zc_ref,
             hpre_ref, hu_ref, ndyc_ref, npost_ref, nzc_ref, nw_ref, cw_ref, dproj_in,
             dproj_ref, doa_ref, dob_ref, vec_ref, dza_scr, dzb_scr, dc_scr, sems):
        del dproj_in
        i = pl.program_id(0)

        @pl.when(i == 0)
        def _():
            vec_ref[...] = jnp.zeros_like(vec_ref)

        sa, dsa = _silu_and_grad(za_ref[...])
        dya = dya_ref[...]
        doa_ref[...] = dya * sa
        dza_scr[...] = (dya * oa_ref[...] * dsa).astype(BF16)
        nw = nw_ref[...]
        n_b, ohat, rstd = _rms_heads(ob_ref[...], nw)
        sb, dsb = _silu_and_grad(zb_ref[...])
        dyb = dyb_ref[...]
        dzb_scr[...] = (dyb * n_b * dsb).astype(BF16)
        dn = dyb * sb
        vec_ref[0:1, :] += jnp.sum(dn * ohat, axis=0, keepdims=True)
        dnw = dn * nw
        parts = []
        for h in range(WIDTH // HG_HEAD_DIM):
            sl = slice(h * HG_HEAD_DIM, (h + 1) * HG_HEAD_DIM)
            m2 = jnp.mean(dnw[:, sl] * ohat[:, sl], axis=-1, keepdims=True)
            parts.append(rstd[:, sl] * (dnw[:, sl] - ohat[:, sl] * m2))
        dob_ref[...] = jnp.concatenate(parts, axis=-1)
        cw = cw_ref[...]
        pre, u, post = pre_ref[...], u_ref[...], post_ref[...]
        a = pre * u
        halo = jnp.where(i > 0, hpre_ref[...] * hu_ref[...], 0.0)
        a1 = _shift_rows_down(halo, a, 1)
        a2 = _shift_rows_down(halo, a, 2)
        conv = cw[0:1] * a2 + cw[1:2] * a1 + cw[2:3] * a
        sc, dsc = _silu_and_grad(zc_ref[...])
        dyc = dyc_ref[...]
        dconv = dyc * post * sc
        nsc, _ = _silu_and_grad(nzc_ref[...])
        nxt = jnp.where(i < n_tiles - 1, ndyc_ref[...] * npost_ref[...] * nsc, 0.0)
        da = cw[2:3] * dconv + cw[1:2] * _shift_rows_up(dconv, nxt, 1) + cw[0:1] * _shift_rows_up(dconv, nxt, 2)
        dc_scr[:, 0 * WIDTH:1 * WIDTH] = (da * u).astype(BF16)
        dc_scr[:, 1 * WIDTH:2 * WIDTH] = (dyc * conv * sc).astype(BF16)
        dc_scr[:, 2 * WIDTH:3 * WIDTH] = (da * pre).astype(BF16)
        dc_scr[:, 3 * WIDTH:4 * WIDTH] = (dyc * post * conv * dsc).astype(BF16)
        vec_ref[1:2, :] += jnp.sum(dconv * a2, axis=0, keepdims=True)
        vec_ref[2:3, :] += jnp.sum(dconv * a1, axis=0, keepdims=True)
        vec_ref[3:4, :] += jnp.sum(dconv * a, axis=0, keepdims=True)
        rows = pl.ds(pl.multiple_of(i * tm, tm), tm)
        copies = [pltpu.make_async_copy(dza_scr, dproj_ref.at[rows, 3 * WIDTH:4 * WIDTH], sems.at[0]),
                  pltpu.make_async_copy(dzb_scr, dproj_ref.at[rows, 7 * WIDTH:8 * WIDTH], sems.at[1]),
                  pltpu.make_async_copy(dc_scr, dproj_ref.at[rows, 8 * WIDTH:12 * WIDTH], sems.at[2])]
        for cp in copies:
            cp.start()
        for cp in copies:
            cp.wait()

    wcol = lambda cb: pl.BlockSpec((tm, WIDTH), lambda i: (i, cb))
    prev = lambda cb: pl.BlockSpec((8, WIDTH), lambda i: (jnp.maximum(i * hb - 1, 0), cb))
    nxt = lambda cb: pl.BlockSpec((8, WIDTH), lambda i: (jnp.minimum((i + 1) * hb, s_len // 8 - 1), cb))
    anyspec = pl.BlockSpec(memory_space=pl.ANY)
    out = jax.ShapeDtypeStruct((s_len, WIDTH), F32)
    return _pcall(
        body, name=name,
        out_shape=(jax.ShapeDtypeStruct(dproj.shape, dproj.dtype), out, out, jax.ShapeDtypeStruct((8, WIDTH), F32)),
        grid=(n_tiles,),
        in_specs=[wcol(0), wcol(1), wcol(2), wcol(0), wcol(3), wcol(0), wcol(7), wcol(8), wcol(9), wcol(10), wcol(11),
                  prev(8), prev(10), nxt(2), nxt(9), nxt(11),
                  pl.BlockSpec((1, WIDTH), lambda i: (0, 0)), pl.BlockSpec((3, WIDTH), lambda i: (0, 0)), anyspec],
        out_specs=(anyspec, wcol(0), wcol(0), pl.BlockSpec((8, WIDTH), lambda i: (0, 0))),
        scratch_shapes=[pltpu.VMEM((tm, WIDTH), BF16), pltpu.VMEM((tm, WIDTH), BF16),
                        pltpu.VMEM((tm, 4 * WIDTH), BF16), pltpu.SemaphoreType.DMA((3,))],
        aliases={18: 0},
        semantics=("arbitrary",))(dycat, dycat, dycat, o_a, proj, o_b, proj, proj, proj, proj, proj,
                                  proj, proj, dycat, proj, proj, norm_w, conv_w, dproj)


def _sb_bwd(proj, do_a, totals, dproj, name):
    s_len = proj.shape[0]
    nb = s_len // BLK
    n_pairs = WIDTH // BLK
    scale = SB_HEAD_DIM ** -0.5
    gb = _sb_group_blocks(nb)
    kw = gb * BLK

    def body(q_ref, k_ref, v_ref, do_ref, tot_ref, dproj_in, dproj_ref, dq_ref, dk_ref, dv_ref, out_scr, sems):
        del dproj_in
        lane = _iota2((1, BLK), 1)
        row = _iota2((BLK, BLK), 0)
        col = _iota2((BLK, BLK), 1)
        ones = jnp.ones((BLK, BLK), BF16)
        before_and_sum = jnp.concatenate([(row < col).astype(BF16), ones], axis=1)
        upto_and_sum = jnp.concatenate([(row <= col).astype(BF16), ones], axis=1)
        qpos = _iota2((BLK, kw), 0)
        kpos = _iota2((BLK, kw), 1)
        head_lanes = [(lane // SB_HEAD_DIM) == hh for hh in range(2)]
        dk_ref[...] = jnp.zeros_like(dk_ref)
        dv_ref[...] = jnp.zeros_like(dv_ref)

        def group(i, gi, qms, doms, totals_i, carry, masked):
            c0 = pl.multiple_of(gi * kw, kw)
            kf = k_ref[pl.ds(c0, kw), :]
            vf = v_ref[pl.ds(c0, kw), :]
            if masked:
                valid = (c0 + kpos) < (i * BLK + qpos)
            out = []
            dk_add = None
            dv_add = None
            for hh in range(2):
                dq_acc, l_before, g_before = carry[3 * hh:3 * hh + 3]
                km = jnp.where(head_lanes[hh], kf, 0.0).astype(BF16)
                vm = jnp.where(head_lanes[hh], vf, 0.0).astype(BF16)
                z = _dot_nt(qms[hh], km)
                sp, sig = _softplus_parts(z)
                lg = -sp
                if masked:
                    lg = jnp.where(valid, lg, 0.0)
                l1, l2 = _split2(lg)
                parts = []
                for b in range(gb):
                    sl = slice(b * BLK, (b + 1) * BLK)
                    both = _dot(l1[:, sl], before_and_sum) + _dot(l2[:, sl], before_and_sum)
                    parts.append(totals_i[hh] - l_before - both[:, :BLK])
                    l_before = l_before + both[:, BLK:]
                a = jnp.exp(z + jnp.concatenate(parts, axis=1))
                if masked:
                    a = jnp.where(valid, a, 0.0)
                gmat = a * _dot_nt(doms[hh], vm)
                g1, g2 = _split2(gmat)
                parts = []
                for b in range(gb):
                    sl = slice(b * BLK, (b + 1) * BLK)
                    both = _dot(g1[:, sl], upto_and_sum) + _dot(g2[:, sl], upto_and_sum)
                    parts.append(g_before + both[:, :BLK])
                    g_before = g_before + both[:, BLK:]
                dz = gmat - sig * jnp.concatenate(parts, axis=1)
                if masked:
                    dz = jnp.where(valid, dz, 0.0)
                dzb = dz.astype(BF16)
                dk_h = _dot_tn(dzb, qms[hh])
                dv_h = _dot_tn(a.astype(BF16), doms[hh])
                dk_add = dk_h if dk_add is None else dk_add + dk_h
                dv_add = dv_h if dv_add is None else dv_add + dv_h
                out += [dq_acc + _dot(dzb, km), l_before, g_before]
            dk_ref[pl.ds(c0, kw), :] += dk_add
            dv_ref[pl.ds(c0, kw), :] += dv_add
            return tuple(out)

        def qblock(i, _):
            r0 = pl.multiple_of(i * BLK, BLK)
            qf = q_ref[pl.ds(r0, BLK), :] * scale
            dof = do_ref[pl.ds(r0, BLK), :]
            qms = [jnp.where(head_lanes[hh], qf, 0.0).astype(BF16) for hh in range(2)]
            doms = [jnp.where(head_lanes[hh], dof, 0.0).astype(BF16) for hh in range(2)]
            totals_i = [tot_ref[hh, pl.ds(r0, BLK), :] for hh in range(2)]
            zero = jnp.zeros((BLK, BLK), F32)
            last = i // gb
            carry = lax.fori_loop(0, last, lambda gi, cr: group(i, gi, qms, doms, totals_i, cr, False), (zero,) * 6)
            carry = group(i, last, qms, doms, totals_i, carry, True)
            dq_ref[pl.ds(r0, BLK), :] = (carry[0] + carry[3]) * scale
            return 0

        lax.fori_loop(0, nb, qblock, 0)
        pair = pl.program_id(0)
        copies = []
        for t, ref in enumerate((dq_ref, dk_ref, dv_ref)):
            out_scr[t] = ref[...].astype(BF16)
            col = pl.multiple_of((t * n_pairs + pair) * BLK, BLK)
            copies.append(pltpu.make_async_copy(out_scr.at[t], dproj_ref.at[:, pl.ds(col, BLK)], sems.at[t]))
            copies[-1].start()
        for cp in copies:
            cp.wait()

    col_spec = lambda off: pl.BlockSpec((s_len, BLK), lambda p: (0, off + p))
    anyspec = pl.BlockSpec(memory_space=pl.ANY)
    return _pcall(
        body, name=name, out_shape=jax.ShapeDtypeStruct(dproj.shape, dproj.dtype), grid=(n_pairs,),
        in_specs=[col_spec(0), col_spec(n_pairs), col_spec(2 * n_pairs), col_spec(0),
                  pl.BlockSpec((2, s_len, BLK), lambda p: (p, 0, 0)), anyspec],
        out_specs=anyspec,
        scratch_shapes=[pltpu.VMEM((s_len, BLK), F32)] * 3 + [pltpu.VMEM((3, s_len, BLK), BF16),
                                                              pltpu.SemaphoreType.DMA((3,))],
        aliases={5: 0},
        semantics=("arbitrary",))(proj, proj, proj, do_a, totals, dproj)


def _hgrn_bwd(proj, do_b, lb, dproj, name):
    s_len = proj.shape[0]
    nc = s_len // BLK
    nh = WIDTH // HG_HEAD_DIM
    base = 4 * WIDTH // BLK

    def body(q_ref, f_ref, i_ref, do_ref, lb_ref, dproj_in, dproj_ref, dlb_ref, mask_ref, st_ref, out_scr, sems):
        del dproj_in
        _hg_masks(mask_ref)
        row = _iota2((BLK, BLK), 0)
        col = _iota2((BLK, BLK), 1)
        lower_incl = (col <= row).astype(BF16)
        upper_incl = (col >= row).astype(BF16)
        lb_v = lb_ref[...]

        def load(ci):
            r0 = pl.multiple_of(ci * BLK, BLK)
            q, dq_fac, f, sig, g = _hg_chunk_inputs(q_ref[pl.ds(r0, BLK), :], f_ref[pl.ds(r0, BLK), :], lb_v)
            b = _dot_exact_l(lower_incl, g)
            return r0, q, dq_fac, f, sig, b, i_ref[pl.ds(r0, BLK), :]

        def fwd_chunk(ci, st):
            st_ref[ci] = st
            _, _, _, f, _, b, v = load(ci)
            b_end = b[BLK - 1:BLK, :]
            k_dec = ((1.0 - f) * jnp.exp(b_end - b)).astype(BF16)
            return st * jnp.exp(b_end) + _dot_tn(v.astype(BF16), k_dec)

        lax.fori_loop(0, nc, fwd_chunk, jnp.zeros((HG_HEAD_DIM, HG_HEAD_DIM), F32))

        def bwd_chunk(cc, carry):
            dst, suffix, dlb = carry
            ci = nc - 1 - cc
            r0, q, dq_fac, f, sig, b, v = load(ci)
            k = 1.0 - f
            vb = v.astype(BF16)
            do = do_ref[pl.ds(r0, BLK), :]
            dob = do.astype(BF16)
            b_end = b[BLK - 1:BLK, :]
            e_q = jnp.exp(b)
            e_k = jnp.exp(b_end - b)
            qe = (q * e_q).astype(BF16)
            kh = (k * e_k).astype(BF16)
            st1, st2 = _split2(st_ref[ci])
            ds1, ds2 = _split2(dst)
            dqe = _dot(dob, st1) + _dot(dob, st2)
            dkh = _dot(vb, ds1) + _dot(vb, ds2)
            dq = e_q * dqe
            dk = e_k * dkh
            dv = _dot_nt(kh, ds1)
            dst_new = dst * jnp.exp(b_end) + _dot_tn(dob, qe)
            dlog = qe.astype(F32) * dqe - kh.astype(F32) * dkh
            da = _dot_nt(dob, vb)
            sc = None
            for v_idx, m in enumerate(HG_LEVELS):
                qm, km, eq, ek, msk = _hg_level_terms(q, k, b, v_idx, m, mask_ref)
                term = _dot_nt(qm, km) * msk
                sc = term if sc is None else sc + term
                pm = (da * msk).astype(BF16)
                dqm = _dot(pm, km)
                dkm = _dot_tn(pm, qm)
                dq = dq + dqm * eq
                dk = dk + dkm * ek
                dlog = dlog + (qm.astype(F32) * dqm - km.astype(F32) * dkm)
            a_diag = jnp.sum(do * v, axis=-1, keepdims=True)
            s_diag = jnp.sum(q * k, axis=-1, keepdims=True)
            dq = dq + a_diag * k
            dk = dk + a_diag * q
            dv = dv + _dot_tn(sc.astype(BF16), dob) + s_diag * do
            dg = _dot_exact_l(upper_incl, dlog) + suffix
            dfull = dg / f - dk
            out_scr[0, pl.ds(r0, BLK), :] = (dq * dq_fac).astype(BF16)
            out_scr[1, pl.ds(r0, BLK), :] = (dfull * (1.0 - lb_v) * sig * (1.0 - sig)).astype(BF16)
            out_scr[2, pl.ds(r0, BLK), :] = dv.astype(BF16)
            dlb = dlb + jnp.sum(dfull * (1.0 - sig), axis=0, keepdims=True)
            return dst_new, dg[0:1, :], dlb

        zero_row = jnp.zeros((1, BLK), F32)
        _, _, dlb = lax.fori_loop(0, nc, bwd_chunk,
                                  (jnp.zeros((HG_HEAD_DIM, HG_HEAD_DIM), F32), zero_row, zero_row))
        dlb_ref[...] = jnp.broadcast_to(dlb, dlb_ref.shape)
        head = pl.program_id(0)
        copies = []
        for t in range(3):
            col = pl.multiple_of((base + t * nh + head) * BLK, BLK)
            copies.append(pltpu.make_async_copy(out_scr.at[t], dproj_ref.at[:, pl.ds(col, BLK)], sems.at[t]))
            copies[-1].start()
        for cp in copies:
            cp.wait()

    col_spec = lambda off: pl.BlockSpec((s_len, BLK), lambda h: (0, off + h))
    anyspec = pl.BlockSpec(memory_space=pl.ANY)
    return _pcall(
        body, name=name,
        out_shape=(jax.ShapeDtypeStruct(dproj.shape, dproj.dtype), jax.ShapeDtypeStruct((8, WIDTH), F32)),
        grid=(nh,),
        in_specs=[col_spec(base), col_spec(base + nh), col_spec(base + 2 * nh), col_spec(0),
                  pl.BlockSpec((1, BLK), lambda h: (0, h)), anyspec],
        out_specs=(anyspec, pl.BlockSpec((8, BLK), lambda h: (0, h))),
        scratch_shapes=[pltpu.VMEM((len(HG_LEVELS), BLK, BLK), F32),
                        pltpu.VMEM((nc, HG_HEAD_DIM, HG_HEAD_DIM), F32),
                        pltpu.VMEM((3, s_len, BLK), BF16), pltpu.SemaphoreType.DMA((3,))],
        aliases={5: 0},
        semantics=("arbitrary",))(proj, proj, proj, do_b, lb, dproj)


def _dh_matmul(dproj, w_full, name):
    s_len, n = dproj.shape
    d = w_full.shape[0]
    tm = min(512, s_len)
    tk = 1536

    def body(dp_ref, w_ref, dh_ref):
        part = _dot_nt(dp_ref[...], w_ref[...])

        @pl.when(pl.program_id(1) == 0)
        def _():
            dh_ref[...] = part

        @pl.when(pl.program_id(1) > 0)
        def _():
            dh_ref[...] += part

    return _pcall(
        body, name=name, out_shape=jax.ShapeDtypeStruct((s_len, d), F32),
        grid=(s_len // tm, n // tk),
        in_specs=[pl.BlockSpec((tm, tk), lambda i, k: (i, k)), pl.BlockSpec((d, tk), lambda i, k: (0, k))],
        out_specs=pl.BlockSpec((tm, d), lambda i, k: (i, 0)),
        semantics=("arbitrary", "arbitrary"))(dproj, w_full)


def _gw_matmul(h_t, dproj, name):
    d, s_len = h_t.shape
    n = dproj.shape[1]
    tn = 1152

    def body(ht_ref, dp_ref, gw_ref):
        gw_ref[...] = _dot(ht_ref[...], dp_ref[...]).astype(BF16)

    return _pcall(
        body, name=name, out_shape=jax.ShapeDtypeStruct((d, n), BF16),
        grid=(n // tn,),
        in_specs=[pl.BlockSpec((d, s_len), lambda j: (0, 0)), pl.BlockSpec((s_len, tn), lambda j: (0, j))],
        out_specs=pl.BlockSpec((d, tn), lambda j: (0, j)),
        semantics=("arbitrary",))(h_t, dproj)


def _ln_bwd(dh, x, scale, dres, name):
    s_len, d = x.shape
    tm = min(512, s_len)

    def body(dh_ref, x_ref, sc_ref, dres_ref, dx_ref, vec_ref):
        @pl.when(pl.program_id(0) == 0)
        def _():
            vec_ref[...] = jnp.zeros_like(vec_ref)

        dh = dh_ref[...]
        xs, rstd = _standardize(x_ref[...])
        vec_ref[0:1, :] += jnp.sum(dh, axis=0, keepdims=True)
        vec_ref[1:2, :] += jnp.sum(dh * xs, axis=0, keepdims=True)
        dx_ref[...] = _standardize_bwd(xs, rstd, dh * (1.0 + sc_ref[...])) + dres_ref[...]

    tile = pl.BlockSpec((tm, d), lambda i: (i, 0))
    return _pcall(body, name=name, grid=(s_len // tm,),
                  out_shape=(jax.ShapeDtypeStruct((s_len, d), F32), jax.ShapeDtypeStruct((8, d), F32)),
                  in_specs=[tile, tile, pl.BlockSpec((1, d), lambda i: (0, 0)), tile],
                  out_specs=(tile, pl.BlockSpec((8, d), lambda i: (0, 0))),
                  semantics=("arbitrary",))(dh, x, scale, dres)


def _wmod_grad(c_t, dmod):
    d = c_t.shape[0]
    n_layers, _, cm = dmod.shape

    def body(c_ref, dm_ref, o_ref):
        for l in range(n_layers):
            acc = None
            for b in range(NDEV):
                term = c_ref[:, b:b + 1] * dm_ref[l, b:b + 1, :]
                acc = term if acc is None else acc + term
            o_ref[l] = acc

    return _pcall(body, name="wmod_grad", out_shape=jax.ShapeDtypeStruct((n_layers, d, cm), F32))(c_t, dmod)


def _sum_adamw(parts, w, m, v, name):
    n_src, rows, cols = parts.shape
    tr = rows
    for cand in (512, 256, 128, 64, 32, 16, 8):
        if rows % cand == 0 and cand * cols * 4 <= (2 << 20):
            tr = cand
            break

    def body(p_ref, w_ref, m_ref, v_ref, g_ref, d_ref, nm_ref, nv_ref):
        g = p_ref[0].astype(F32)
        for s in range(1, n_src):
            g = g + p_ref[s].astype(F32)
        nm = ADAM_B1 * m_ref[...] + (1.0 - ADAM_B1) * g
        nv = ADAM_B2 * v_ref[...] + (1.0 - ADAM_B2) * (g * g)
        m_hat = nm / (1.0 - ADAM_B1 ** ADAM_STEP)
        v_hat = nv / (1.0 - ADAM_B2 ** ADAM_STEP)
        g_ref[...] = g
        d_ref[...] = -ADAM_LR * (m_hat / (jnp.sqrt(v_hat) + ADAM_EPS) + ADAM_WD * w_ref[...])
        nm_ref[...] = nm
        nv_ref[...] = nv

    tile = pl.BlockSpec((tr, cols), lambda i: (i, 0))
    out = jax.ShapeDtypeStruct((rows, cols), F32)
    return _pcall(body, name=name, grid=(rows // tr,), out_shape=(out,) * 4,
                  in_specs=[pl.BlockSpec((n_src, tr, cols), lambda i: (0, i, 0)), tile, tile, tile],
                  out_specs=(tile,) * 4, semantics=("arbitrary",))(parts, w, m, v)


def _sum_parts(parts, name):
    n_src = parts.shape[0]

    def body(p_ref, o_ref):
        acc = p_ref[0]
        for s in range(1, n_src):
            acc = acc + p_ref[s]
        o_ref[...] = acc

    return _pcall(body, name=name, out_shape=jax.ShapeDtypeStruct(parts.shape[1:], F32))(parts)


def _lower_bound_table(lower_bounds):
    p = jax.nn.softmax(lower_bounds.astype(F32), axis=0)
    return jnp.cumsum(p, axis=0) - p[0:1]


def _pad_rows(v, width):
    n = v.shape[0]
    rows = -(-n // width)
    rows = -(-rows // 8) * 8
    return jnp.pad(v, (0, rows * width - n)).reshape(rows, width)


def kernel(x, c, w_mod, b_mod, w_in, conv_w, hgrn_norm_w, lower_bounds, w_branch, w_out, ln_g, ln_b, loss_target, m_w_mod, m_b_mod, m_w_in, m_conv_w, m_hgrn_norm_w, m_lower_bounds, m_w_branch, m_w_out, m_ln_g, m_ln_b, v_w_mod, v_b_mod, v_w_in, v_conv_w, v_hgrn_norm_w, v_lower_bounds, v_w_branch, v_w_out, v_ln_g, v_ln_b):
    n_layers = N_LAYERS
    s_len, d = x.shape[1], x.shape[2]
    n_cols = w_in.shape[2] * NDEV
    cw_cols = conv_w.shape[2]
    cm = w_mod.shape[2]
    me = _my_index()
    x0 = x[0]
    target = loss_target[0]

    small = _pad_rows(jnp.concatenate([c.reshape(-1), conv_w.reshape(-1)]), BLK)
    small_all = _all_gather_small("gather_c_conv", small).reshape(NDEV, -1)
    c_all = small_all[:, :d]
    conv_full = small_all[:, d:d + n_layers * 3 * cw_cols].reshape(NDEV, n_layers, 3, cw_cols)
    conv_full = conv_full.transpose(1, 2, 0, 3).reshape(n_layers, 3, WIDTH)

    b_mod_mine = lax.dynamic_slice_in_dim(b_mod, me * cm, cm, axis=1).reshape(n_layers, 1, cm)
    mod_cols = _mod_fwd(c_all, w_mod, b_mod_mine)
    mod_all = _all_gather_small("gather_mod", mod_cols.reshape(n_layers * NDEV, cm))
    mod_all = mod_all.reshape(NDEV, n_layers, NDEV, cm)
    mod_mine = lax.dynamic_index_in_dim(mod_all, me, axis=2, keepdims=False)
    mod_mine = mod_mine.transpose(1, 0, 2).reshape(n_layers, 3, 1, d)

    shard = w_in.shape[2]
    w_in_b, w_branch_b, w_out_b = w_in.astype(BF16), w_branch.astype(BF16), w_out.astype(BF16)

    def in_dst(l):
        return lambda ref, dev: ref.at[l, :, pl.ds(pl.multiple_of(dev * shard, BLK), shard)]

    transfers = []
    for l in range(n_layers):
        transfers.append((0, 0, (lambda l: lambda ref, dev: ref.at[l])(l), in_dst(l)))
    transfers.append((1, 1, _whole, _slot))
    transfers.append((2, 2, _whole, _slot))
    w_in_full, w_branch_all, w_out_all = _exchange(
        "gather_weights", [w_in_b, w_branch_b, w_out_b],
        [jax.ShapeDtypeStruct((n_layers, d, n_cols), BF16),
         jax.ShapeDtypeStruct((NDEV,) + w_branch_b.shape, BF16),
         jax.ShapeDtypeStruct((NDEV,) + w_out_b.shape, BF16)],
        transfers, in_vmem=False)
    wb_full = w_branch_all.transpose(1, 2, 3, 0, 4).reshape(n_layers, 3, WIDTH, d)
    wo_full = w_out_all.transpose(1, 0, 2, 3).reshape(n_layers, d, d)

    lbs = _lower_bound_table(lower_bounds)
    norm_w4 = jnp.tile(hgrn_norm_w, (1, WIDTH // HG_HEAD_DIM))

    saved = []
    xl = x0
    for l in range(n_layers):
        shift, scale, gate = mod_mine[l, 0], mod_mine[l, 1], mod_mine[l, 2]
        proj, h_t = _ln_proj(xl, shift, scale, w_in_full[l], f"ln_proj_{l}")
        o_a, totals = _sb_fwd(proj, f"sb_fwd_{l}")
        o_b = _hgrn_fwd(proj, lbs[l:l + 1], f"hgrn_fwd_{l}")
        x_new, merged, ycat = _merge_fwd(xl, proj, o_a, o_b, gate, norm_w4[l:l + 1], conv_full[l],
                                         wb_full[l], wo_full[l], ln_g[l:l + 1], ln_b[l:l + 1], f"merge_fwd_{l}")
        saved.append((xl, proj, h_t, o_a, totals, o_b, merged, ycat))
        xl = x_new

    loss_part, dx = _loss_fwd_bwd(xl, target)
    loss = lax.psum(loss_part[0, 0], ("x", "y", "c"))

    gw_in, gw_branch, gw_out = [None] * n_layers, [None] * n_layers, [None] * n_layers
    small_grads = [None] * n_layers
    dmod = [None] * n_layers
    for l in reversed(range(n_layers)):
        xl, proj, h_t, o_a, totals, o_b, merged, ycat = saved[l]
        scale, gate = mod_mine[l, 1], mod_mine[l, 2]
        dres, dycat, dproj, gwo, gwb, mvec = _merge_bwd(dx, xl, merged, ycat, proj, gate, wb_full[l], wo_full[l],
                                                        ln_g[l:l + 1], f"merge_bwd_{l}")
        dproj, do_a, do_b, bvec = _branch_bwd(dycat, proj, o_a, o_b, norm_w4[l:l + 1], conv_full[l], dproj,
                                              f"branch_bwd_{l}")
        dproj = _sb_bwd(proj, do_a, totals, dproj, f"sb_bwd_{l}")
        dproj, dlb = _hgrn_bwd(proj, do_b, lbs[l:l + 1], dproj, f"hgrn_bwd_{l}")
        dh = _dh_matmul(dproj, w_in_full[l], f"dh_matmul_{l}")
        gwi = _gw_matmul(h_t, dproj, f"gw_matmul_{l}")
        dx, lvec = _ln_bwd(dh, xl, scale, dres, f"ln_bwd_{l}")
        gw_in[l], gw_branch[l], gw_out[l] = gwi, gwb, gwo
        dmod[l] = jnp.concatenate([lvec[0], lvec[1], mvec[2]])
        norm_grad = bvec[0].reshape(WIDTH // HG_HEAD_DIM, HG_HEAD_DIM).sum(axis=0)
        small_grads[l] = jnp.concatenate([mvec[0], mvec[1], norm_grad, dlb[0], bvec[1:4].reshape(-1)])
    grad_x = dx[None]

    gw_in_all = jnp.stack(gw_in)
    gw_branch_all = jnp.stack(gw_branch).astype(BF16).reshape(n_layers, 3, WIDTH, NDEV, d // NDEV)
    gw_branch_all = gw_branch_all.transpose(3, 0, 1, 2, 4)
    gw_out_all = jnp.stack(gw_out).astype(BF16).reshape(n_layers, NDEV, d // NDEV, d).transpose(1, 0, 2, 3)

    def in_src(l):
        return lambda ref, dev: ref.at[l, :, pl.ds(pl.multiple_of(dev * shard, BLK), shard)]

    transfers = []
    for l in range(n_layers):
        transfers.append((0, 0, in_src(l), (lambda l: lambda ref, dev: ref.at[dev, l])(l)))
    transfers.append((1, 1, _slot, _slot))
    transfers.append((2, 2, _slot, _slot))
    p_in, p_branch, p_out = _exchange(
        "scatter_grads", [gw_in_all, gw_branch_all, gw_out_all],
        [jax.ShapeDtypeStruct((NDEV, n_layers, d, shard), BF16),
         jax.ShapeDtypeStruct(gw_branch_all.shape, BF16),
         jax.ShapeDtypeStruct(gw_out_all.shape, BF16)],
        transfers, in_vmem=False)

    small_vec = jnp.concatenate(dmod + small_grads)
    n_small = small_vec.shape[0]
    small_all = _all_gather_small("gather_small_grads", _pad_rows(small_vec, BLK))
    small_sum = _sum_parts(small_all, "sum_small_grads").reshape(-1)[:n_small]
    dmod_all = small_all.reshape(NDEV, -1)[:, :n_layers * 3 * d].reshape(NDEV, n_layers, 3 * d)

    off = n_layers * 3 * d
    grad_b_mod = small_sum[:off].reshape(n_layers, 3 * d)
    per_layer = 2 * d + HG_HEAD_DIM + WIDTH + 3 * WIDTH
    g_ln_g, g_ln_b, g_norm, g_lbs, g_conv = [], [], [], [], []
    for l in range(n_layers):
        seg = small_sum[off + l * per_layer: off + (l + 1) * per_layer]
        g_ln_g.append(seg[:d])
        g_ln_b.append(seg[d:2 * d])
        g_norm.append(seg[2 * d:2 * d + HG_HEAD_DIM])
        g_lbs.append(seg[2 * d + HG_HEAD_DIM:2 * d + HG_HEAD_DIM + WIDTH])
        g_conv.append(seg[2 * d + HG_HEAD_DIM + WIDTH:].reshape(3, WIDTH))
    grad_ln_g, grad_ln_b = jnp.stack(g_ln_g), jnp.stack(g_ln_b)
    grad_norm = jnp.stack(g_norm)
    _, lbs_vjp = jax.vjp(_lower_bound_table, lower_bounds)
    grad_lower = lbs_vjp(jnp.stack(g_lbs))[0]
    grad_conv = lax.dynamic_slice_in_dim(jnp.stack(g_conv), me * cw_cols, cw_cols, axis=2)

    dmod_mine = lax.dynamic_slice_in_dim(dmod_all, me * cm, cm, axis=2).transpose(1, 0, 2)
    grad_w_mod = _wmod_grad(c_all.T, dmod_mine)

    def adam(parts, w, m, v, name):
        shape = w.shape
        cols = shape[-1]
        flat = lambda a: a.reshape(-1, cols)
        outs = _sum_adamw(parts.reshape(parts.shape[0], -1, cols), flat(w), flat(m), flat(v), name)
        return [o.reshape(shape) for o in outs]

    r_w_in = adam(p_in, w_in, m_w_in, v_w_in, "adamw_w_in")
    r_w_branch = adam(p_branch, w_branch, m_w_branch, v_w_branch, "adamw_w_branch")
    r_w_out = adam(p_out, w_out, m_w_out, v_w_out, "adamw_w_out")
    r_w_mod = adam(grad_w_mod[None], w_mod, m_w_mod, v_w_mod, "adamw_w_mod")

    small_names = ["b_mod", "conv_w", "hgrn_norm_w", "lower_bounds", "ln_g", "ln_b"]
    small_g = [grad_b_mod, grad_conv, grad_norm, grad_lower, grad_ln_g, grad_ln_b]
    small_w = [b_mod, conv_w, hgrn_norm_w, lower_bounds, ln_g, ln_b]
    small_m = [m_b_mod, m_conv_w, m_hgrn_norm_w, m_lower_bounds, m_ln_g, m_ln_b]
    small_v = [v_b_mod, v_conv_w, v_hgrn_norm_w, v_lower_bounds, v_ln_g, v_ln_b]
    pack = lambda arrs: _pad_rows(jnp.concatenate([a.reshape(-1) for a in arrs]), BLK)
    packed = _sum_adamw(pack(small_g)[None], pack(small_w), pack(small_m), pack(small_v), "adamw_small")
    r_small = {n: [] for n in small_names}
    for res in packed:
        flat = res.reshape(-1)
        pos = 0
        for n, w in zip(small_names, small_w):
            r_small[n].append(flat[pos:pos + w.size].reshape(w.shape))
            pos += w.size

    results = {"w_mod": r_w_mod, "w_in": r_w_in, "w_branch": r_w_branch, "w_out": r_w_out, **r_small}
    order = ["w_mod", "b_mod", "w_in", "conv_w", "hgrn_norm_w", "lower_bounds", "w_branch", "w_out", "ln_g", "ln_b"]
    outs = [loss, grad_x]
    for idx in range(4):
        outs.extend(results[n][idx] for n in order)
    return tuple(outs)
```

```python
import jax
import jax.numpy as jnp
from jax import lax
from jax.experimental import pallas as pl
from jax.experimental.pallas import tpu as pltpu

F32 = jnp.float32
BF16 = jnp.bfloat16
NDEV = 8
N_LAYERS = 2
SB_HEAD_DIM = 64
HG_HEAD_DIM = 128
WIDTH = 512
BLK = 128
LN_EPS = 1e-5
RMS_EPS = 1e-6
ALPHA = (2.0 * N_LAYERS) ** 0.25
ADAM_LR, ADAM_B1, ADAM_B2, ADAM_EPS, ADAM_WD, ADAM_STEP = 0.001, 0.9, 0.999, 1e-08, 0.01, 10
VMEM_LIMIT = 56 * 1024 * 1024
MESH = pl.DeviceIdType.MESH
HG_LEVELS = (64, 32, 16, 8, 4, 2, 1)


def _pcall(body, *, name, out_shape, grid=None, in_specs=None, out_specs=None, scratch_shapes=(),
           semantics=None, aliases=None):
    kwargs = {}
    if grid is not None:
        kwargs["grid"] = grid
    if in_specs is not None:
        kwargs["in_specs"] = in_specs
    if out_specs is not None:
        kwargs["out_specs"] = out_specs
    if aliases:
        kwargs["input_output_aliases"] = aliases
    return pl.pallas_call(
        body, name=name, out_shape=out_shape, scratch_shapes=list(scratch_shapes),
        compiler_params=pltpu.CompilerParams(dimension_semantics=semantics, vmem_limit_bytes=VMEM_LIMIT),
        interpret=False, **kwargs)


def _dot(a, b):
    return jnp.dot(a, b, preferred_element_type=F32)


def _dot_nt(a, b):
    return lax.dot_general(a, b, (((1,), (1,)), ((), ())), preferred_element_type=F32)


def _dot_tn(a, b):
    return lax.dot_general(a, b, (((0,), (0,)), ((), ())), preferred_element_type=F32)


def _split3(x):
    x1 = x.astype(BF16)
    r1 = x - x1.astype(F32)
    x2 = r1.astype(BF16)
    r2 = r1 - x2.astype(F32)
    return x1, x2, r2.astype(BF16)


def _split2(x):
    x1 = x.astype(BF16)
    return x1, (x - x1.astype(F32)).astype(BF16)


def _dot_exact_l(m_bf16, x):
    x1, x2, x3 = _split3(x)
    return _dot(m_bf16, x1) + _dot(m_bf16, x2) + _dot(m_bf16, x3)


def _sigmoid(x):
    return 1.0 / (1.0 + jnp.exp(-x))


def _silu_and_grad(x):
    s = _sigmoid(x)
    return x * s, s * (1.0 + x * (1.0 - s))


def _softplus_parts(z):
    e = jnp.exp(-jnp.abs(z))
    sp = jnp.maximum(z, 0.0) + jnp.log(1.0 + e)
    r = 1.0 / (1.0 + e)
    return sp, jnp.where(z >= 0.0, r, e * r)


def _iota2(shape, dim):
    return lax.broadcasted_iota(jnp.int32, shape, dim)


def _standardize(x):
    mu = jnp.mean(x, axis=-1, keepdims=True)
    xc = x - mu
    var = jnp.mean(xc * xc, axis=-1, keepdims=True)
    rstd = lax.rsqrt(var + LN_EPS)
    return xc * rstd, rstd


def _standardize_bwd(xhat, rstd, dxhat):
    m1 = jnp.mean(dxhat, axis=-1, keepdims=True)
    m2 = jnp.mean(dxhat * xhat, axis=-1, keepdims=True)
    return rstd * (dxhat - m1 - xhat * m2)


def _my_index():
    return 4 * lax.axis_index("x") + 2 * lax.axis_index("y") + lax.axis_index("c")


def _exchange(name, ins, out_shapes, transfers, in_vmem):
    n_in, n_out, n_t = len(ins), len(out_shapes), len(transfers)

    def body(*refs):
        in_refs, out_refs = refs[:n_in], refs[n_in:n_in + n_out]
        send_sems, recv_sems, local_sems = refs[n_in + n_out:]
        x, y, c = lax.axis_index("x"), lax.axis_index("y"), lax.axis_index("c")
        me = 4 * x + 2 * y + c
        started = []
        for t, (i, o, src_fn, dst_fn) in enumerate(transfers):
            own = pltpu.make_async_copy(src_fn(in_refs[i], me), dst_fn(out_refs[o], me), local_sems.at[t])
            own.start()
            started.append(own)
        arrivals = []
        for k in range(1, NDEV):
            px = x ^ ((k >> 2) & 1)
            py = y ^ ((k >> 1) & 1)
            pc = c ^ (k & 1)
            peer = 4 * px + 2 * py + pc
            for t, (i, o, src_fn, dst_fn) in enumerate(transfers):
                sem = t * (NDEV - 1) + k - 1
                push = pltpu.make_async_remote_copy(
                    src_ref=src_fn(in_refs[i], peer), dst_ref=dst_fn(out_refs[o], me),
                    send_sem=send_sems.at[sem], recv_sem=recv_sems.at[sem],
                    device_id=(px, py, pc), device_id_type=MESH)
                push.start()
                started.append(push)
                arrivals.append(pltpu.make_async_remote_copy(
                    src_ref=src_fn(in_refs[i], peer), dst_ref=dst_fn(out_refs[o], peer),
                    send_sem=send_sems.at[sem], recv_sem=recv_sems.at[sem],
                    device_id=(px, py, pc), device_id_type=MESH))
        for arrival in arrivals:
            arrival.wait_recv()
        for cp in started[n_t:]:
            cp.wait_send()
        for own in started[:n_t]:
            own.wait()

    space = pltpu.VMEM if in_vmem else pl.ANY
    spec = pl.BlockSpec(memory_space=space)
    return _pcall(
        body, name=name, out_shape=out_shapes,
        in_specs=[spec] * n_in, out_specs=[spec] * n_out,
        scratch_shapes=[pltpu.SemaphoreType.DMA((n_t * (NDEV - 1),)),
                        pltpu.SemaphoreType.DMA((n_t * (NDEV - 1),)),
                        pltpu.SemaphoreType.DMA((n_t,))])(*ins)


def _whole(ref, dev):
    return ref


def _slot(ref, dev):
    return ref.at[dev]


def _all_gather_small(name, v):
    out = _exchange(name, [v], [jax.ShapeDtypeStruct((NDEV,) + v.shape, v.dtype)],
                    [(0, 0, _whole, _slot)], in_vmem=True)
    return out[0]


_HBM_SPEC = pl.BlockSpec(memory_space=pltpu.HBM)
_SEM_SPEC = pl.BlockSpec(memory_space=pltpu.SEMAPHORE)
_DATAFLOW = pltpu.SideEffectType.DATAFLOW_SIDE_EFFECTING


def _peers(x, y, c):
    for k in range(1, NDEV):
        px = x ^ ((k >> 2) & 1)
        py = y ^ ((k >> 1) & 1)
        pc = c ^ (k & 1)
        yield k, (px, py, pc), 4 * px + 2 * py + pc


def _exchange_start(name, ins, lands, transfers):
    n_in, n_buf = len(ins), len(ins) + len(lands)
    n_sem = len(transfers) * (NDEV - 1)

    def body(*refs):
        in_refs, land_refs = refs[:n_in], refs[n_in:n_buf]
        send_sems, recv_sems, token = refs[n_buf], refs[n_buf + 1], refs[-1]
        x, y, c = lax.axis_index("x"), lax.axis_index("y"), lax.axis_index("c")
        me = 4 * x + 2 * y + c
        for k, peer_id, peer in _peers(x, y, c):
            for t, (i, o, src_fn, dst_fn) in enumerate(transfers):
                sem = t * (NDEV - 1) + k - 1
                pltpu.make_async_remote_copy(
                    src_ref=src_fn(in_refs[i], peer), dst_ref=dst_fn(land_refs[o], me),
                    send_sem=send_sems.at[sem], recv_sem=recv_sems.at[sem],
                    device_id=peer_id, device_id_type=MESH).start()
        token[...] = jnp.zeros_like(token)

    bufs = [pltpu.with_memory_space_constraint(a, pltpu.HBM) for a in list(ins) + list(lands)]
    outs = pl.pallas_call(
        body, name=name,
        out_shape=(pltpu.SemaphoreType.DMA((n_sem,)), pltpu.SemaphoreType.DMA((n_sem,)))
        + tuple(pltpu.HBM(a.shape, a.dtype) for a in bufs) + (jax.ShapeDtypeStruct((8, BLK), F32),),
        in_specs=[_HBM_SPEC] * n_buf,
        out_specs=(_SEM_SPEC, _SEM_SPEC) + (_HBM_SPEC,) * n_buf + (pl.BlockSpec(memory_space=pltpu.VMEM),),
        input_output_aliases={b: 2 + b for b in range(n_buf)},
        compiler_params=pltpu.CompilerParams(has_side_effects=_DATAFLOW),
        interpret=False)(*bufs)
    return outs[0], outs[1], list(outs[2:2 + n_in]), list(outs[2 + n_in:2 + n_buf]), outs[-1]


def _exchange_wait(name, send_sems, recv_sems, ins, lands, after, transfers):
    n_in, n_buf = len(ins), len(ins) + len(lands)

    def body(*refs):
        in_refs, land_refs = refs[:n_in], refs[n_in:n_buf]
        send_sems, recv_sems = refs[n_buf], refs[n_buf + 1]
        x, y, c = lax.axis_index("x"), lax.axis_index("y"), lax.axis_index("c")
        for k, peer_id, peer in _peers(x, y, c):
            for t, (i, o, src_fn, dst_fn) in enumerate(transfers):
                sem = t * (NDEV - 1) + k - 1
                cp = pltpu.make_async_remote_copy(
                    src_ref=src_fn(in_refs[i], peer), dst_ref=dst_fn(land_refs[o], peer),
                    send_sem=send_sems.at[sem], recv_sem=recv_sems.at[sem],
                    device_id=peer_id, device_id_type=MESH)
                cp.wait_send()
                cp.wait_recv()

    bufs = list(ins) + list(lands)
    outs = pl.pallas_call(
        body, name=name, out_shape=tuple(pltpu.HBM(a.shape, a.dtype) for a in bufs),
        in_specs=[_HBM_SPEC] * n_buf + [_SEM_SPEC, _SEM_SPEC, pl.BlockSpec(memory_space=pl.ANY)],
        out_specs=(_HBM_SPEC,) * n_buf,
        input_output_aliases={b: b for b in range(n_buf)},
        compiler_params=pltpu.CompilerParams(has_side_effects=_DATAFLOW),
        interpret=False)(*bufs, send_sems, recv_sems, after)
    return list(outs[n_in:])


def _place_own(shape, dtype, own, start):
    return lax.dynamic_update_slice(lax.empty(shape, dtype), own, start)


def _mod_fwd(c_all, w_mod, b_mod_mine):
    n_layers, _, cm = w_mod.shape

    def body(c_ref, w_ref, b_ref, o_ref):
        for l in range(n_layers):
            o_ref[l] = jnp.dot(c_ref[...], w_ref[l], preferred_element_type=F32,
                               precision=lax.Precision.HIGHEST) + b_ref[l]

    return _pcall(body, name="mod_fwd", out_shape=jax.ShapeDtypeStruct((n_layers, NDEV, cm), F32))(
        c_all, w_mod, b_mod_mine)


def _ln_proj(x, shift, scale, w_full, name):
    s_len, d = x.shape
    n = w_full.shape[1]
    tm = min(512, s_len)
    tn = 1024

    def body(x_ref, sh_ref, sc_ref, w_ref, proj_ref, ht_ref, h_scr):
        @pl.when(pl.program_id(1) == 0)
        def _():
            xs, _ = _standardize(x_ref[...])
            h = xs * (1.0 + sc_ref[...]) + sh_ref[...]
            h_scr[...] = h.astype(BF16)
            ht_ref[...] = h.T.astype(BF16)

        proj_ref[...] = _dot(h_scr[...], w_ref[...])

    return _pcall(
        body, name=name,
        out_shape=(jax.ShapeDtypeStruct((s_len, n), F32), jax.ShapeDtypeStruct((d, s_len), BF16)),
        grid=(s_len // tm, n // tn),
        in_specs=[pl.BlockSpec((tm, d), lambda i, j: (i, 0)),
                  pl.BlockSpec((1, d), lambda i, j: (0, 0)),
                  pl.BlockSpec((1, d), lambda i, j: (0, 0)),
                  pl.BlockSpec((d, tn), lambda i, j: (0, j))],
        out_specs=(pl.BlockSpec((tm, tn), lambda i, j: (i, j)),
                   pl.BlockSpec((d, tm), lambda i, j: (0, i))),
        scratch_shapes=[pltpu.VMEM((tm, d), BF16)],
        semantics=("arbitrary", "arbitrary"))(x, shift, scale, w_full)


def _sb_group_blocks(nb):
    return min(4, nb)


def _sb_fwd(proj, name):
    s_len = proj.shape[0]
    nb = s_len // BLK
    n_pairs = WIDTH // BLK
    gb = _sb_group_blocks(nb)
    kw = gb * BLK

    def body(q_ref, k_ref, v_ref, o_ref, tot_ref):
        lane = _iota2((1, BLK), 1)
        row = _iota2((BLK, BLK), 0)
        col = _iota2((BLK, BLK), 1)
        suffix_and_sum = jnp.concatenate([(row >= col).astype(BF16), jnp.ones((BLK, BLK), BF16)], axis=1)
        qpos = _iota2((BLK, kw), 0)
        kpos = _iota2((BLK, kw), 1)
        head_lanes = [(lane // SB_HEAD_DIM) == hh for hh in range(2)]

        def group(i, gi, qms, carry, masked):
            c0 = pl.multiple_of(gi * kw, kw)
            kb = k_ref[pl.ds(c0, kw), :].astype(BF16)
            vf = v_ref[pl.ds(c0, kw), :]
            if masked:
                valid = (c0 + kpos) < (i * BLK + qpos)
            out = []
            for hh in range(2):
                o_acc, later = carry[2 * hh], carry[2 * hh + 1]
                z = _dot_nt(qms[hh], kb)
                sp, _ = _softplus_parts(z)
                lg = -sp
                if masked:
                    lg = jnp.where(valid, lg, 0.0)
                l1, l2 = _split2(lg)
                parts = [None] * gb
                for b in reversed(range(gb)):
                    sl = slice(b * BLK, (b + 1) * BLK)
                    both = _dot(l1[:, sl], suffix_and_sum) + _dot(l2[:, sl], suffix_and_sum)
                    parts[b] = both[:, :BLK] + later
                    later = later + both[:, BLK:]
                a = jnp.exp(z + jnp.concatenate(parts, axis=1))
                if masked:
                    a = jnp.where(valid, a, 0.0)
                vm = jnp.where(head_lanes[hh], vf, 0.0).astype(BF16)
                out += [o_acc + _dot(a.astype(BF16), vm), later]
            return tuple(out)

        def qblock(i, _):
            r0 = pl.multiple_of(i * BLK, BLK)
            qf = q_ref[pl.ds(r0, BLK), :] * (SB_HEAD_DIM ** -0.5)
            qms = [jnp.where(head_lanes[hh], qf, 0.0).astype(BF16) for hh in range(2)]
            zero = jnp.zeros((BLK, BLK), F32)
            last = i // gb
            carry = group(i, last, qms, (zero, zero, zero, zero), True)
            carry = lax.fori_loop(0, last, lambda jj, cr: group(i, last - 1 - jj, qms, cr, False), carry)
            o_ref[pl.ds(r0, BLK), :] = carry[0] + carry[2]
            tot_ref[0, pl.ds(r0, BLK), :] = carry[1]
            tot_ref[1, pl.ds(r0, BLK), :] = carry[3]
            return 0

        lax.fori_loop(0, nb, qblock, 0)

    col_spec = lambda off: pl.BlockSpec((s_len, BLK), lambda p: (0, off + p))
    return _pcall(
        body, name=name,
        out_shape=(jax.ShapeDtypeStruct((s_len, WIDTH), F32),
                   jax.ShapeDtypeStruct((2 * n_pairs, s_len, BLK), F32)),
        grid=(n_pairs,),
        in_specs=[col_spec(0), col_spec(n_pairs), col_spec(2 * n_pairs)],
        out_specs=(pl.BlockSpec((s_len, BLK), lambda p: (0, p)),
                   pl.BlockSpec((2, s_len, BLK), lambda p: (p, 0, 0))),
        semantics=("arbitrary",))(proj, proj, proj)


def _hg_masks(mask_ref):
    row = _iota2((BLK, BLK), 0)
    col = _iota2((BLK, BLK), 1)
    for v, m in enumerate(HG_LEVELS):
        same = (row // (2 * m)) == (col // (2 * m))
        mask_ref[v] = (same & ((row & m) != 0) & ((col & m) == 0)).astype(F32)


def _hg_mid(b, m):
    if m >= 4:
        n = BLK // (2 * m)
        mid = b.reshape(n, 2 * m, BLK)[:, m - 1:m, :]
        return jnp.broadcast_to(mid, (n, 2 * m, BLK)).reshape(BLK, BLK)
    pos = _iota2((BLK, BLK), 0) & (2 * m - 1)
    out = b
    for p in range(2 * m):
        delta = (m - 1) - p
        if delta != 0:
            out = jnp.where(pos == p, pltpu.roll(b, (-delta) % BLK, 0), out)
    return out


def _hg_chunk_inputs(qraw, fpre, lb):
    sig = _sigmoid(fpre)
    f = lb + (1.0 - lb) * sig
    g = jnp.log(f)
    q, dq_fac = _silu_and_grad(qraw)
    return q, dq_fac, f, sig, g


def _hg_level_terms(q, k, b, v_idx, m, mask_ref):
    mid = _hg_mid(b, m)
    eq = jnp.exp(jnp.minimum(b - mid, 0.0))
    ek = jnp.exp(jnp.minimum(mid - b, 0.0))
    qt = (q * eq).astype(BF16)
    kt = (k * ek).astype(BF16)
    return qt, kt, eq, ek, mask_ref[v_idx]


def _hg_scores(q, k, b, mask_ref):
    sc = None
    for v_idx, m in enumerate(HG_LEVELS):
        qt, kt, _, _, msk = _hg_level_terms(q, k, b, v_idx, m, mask_ref)
        term = _dot_nt(qt, kt) * msk
        sc = term if sc is None else sc + term
    return sc


def _hgrn_fwd(proj, lb, name):
    s_len = proj.shape[0]
    nc = s_len // BLK
    nh = WIDTH // HG_HEAD_DIM
    base = 4 * WIDTH // BLK

    def body(q_ref, f_ref, i_ref, lb_ref, o_ref, mask_ref):
        _hg_masks(mask_ref)
        row = _iota2((BLK, BLK), 0)
        col = _iota2((BLK, BLK), 1)
        lower_incl = (col <= row).astype(BF16)
        lb_v = lb_ref[...]

        def chunk(ci, st):
            r0 = pl.multiple_of(ci * BLK, BLK)
            q, _, f, _, g = _hg_chunk_inputs(q_ref[pl.ds(r0, BLK), :], f_ref[pl.ds(r0, BLK), :], lb_v)
            k = 1.0 - f
            v = i_ref[pl.ds(r0, BLK), :]
            vb = v.astype(BF16)
            b = _dot_exact_l(lower_incl, g)
            b_end = b[BLK - 1:BLK, :]
            inter = _dot_nt((q * jnp.exp(b)).astype(BF16), st.astype(BF16))
            sc = _hg_scores(q, k, b, mask_ref)
            diag = jnp.sum(q * k, axis=-1, keepdims=True)
            o_ref[pl.ds(r0, BLK), :] = inter + _dot(sc.astype(BF16), vb) + diag * v
            k_dec = (k * jnp.exp(b_end - b)).astype(BF16)
            return st * jnp.exp(b_end) + _dot_tn(vb, k_dec)

        lax.fori_loop(0, nc, chunk, jnp.zeros((HG_HEAD_DIM, HG_HEAD_DIM), F32))

    col_spec = lambda off: pl.BlockSpec((s_len, BLK), lambda h: (0, off + h))
    return _pcall(
        body, name=name, out_shape=jax.ShapeDtypeStruct((s_len, WIDTH), F32),
        grid=(nh,),
        in_specs=[col_spec(base), col_spec(base + nh), col_spec(base + 2 * nh),
                  pl.BlockSpec((1, BLK), lambda h: (0, h))],
        out_specs=pl.BlockSpec((s_len, BLK), lambda h: (0, h)),
        scratch_shapes=[pltpu.VMEM((len(HG_LEVELS), BLK, BLK), F32)],
        semantics=("arbitrary",))(proj, proj, proj, lb)


def _rms_heads(o_b, norm_w):
    n_parts, h_parts, r_parts = [], [], []
    for h in range(WIDTH // HG_HEAD_DIM):
        sl = slice(h * HG_HEAD_DIM, (h + 1) * HG_HEAD_DIM)
        o = o_b[:, sl]
        rstd = lax.rsqrt(jnp.mean(o * o, axis=-1, keepdims=True) + RMS_EPS)
        ohat = o * rstd
        h_parts.append(ohat)
        n_parts.append(ohat * norm_w[:, sl])
        r_parts.append(jnp.broadcast_to(rstd, o.shape))
    cat = lambda parts: jnp.concatenate(parts, axis=-1)
    return cat(n_parts), cat(h_parts), cat(r_parts)


def _shift_rows_down(halo, cur, k):
    tm = cur.shape[0]
    ext = jnp.concatenate([halo, cur], axis=0)
    return pltpu.roll(ext, k, 0)[8:8 + tm]


def _shift_rows_up(cur, halo, k):
    tm = cur.shape[0]
    ext = jnp.concatenate([cur, halo], axis=0)
    return pltpu.roll(ext, (tm + 8 - k) % (tm + 8), 0)[0:tm]


def _merge_fwd(x, proj, o_a, o_b, gate, norm_w, conv_w, wb, w_out, ln_g, ln_b, name):
    s_len, d = x.shape
    tm = min(256, s_len)
    hb = tm // 8

    def body(x_ref, oa_ref, za_ref, ob_ref, zb_ref, pre_ref, post_ref, u_ref, zc_ref, hpre_ref, hu_ref, g_ref,
             gate_ref, nw_ref, cw_ref, wb_ref, wo_ref, lg_ref, lbias_ref, xn_ref, mg_ref, yc_ref):
        i = pl.program_id(0)
        sa, _ = _silu_and_grad(za_ref[...])
        y_a = (oa_ref[...] * sa).astype(BF16)
        n_b, _, _ = _rms_heads(ob_ref[...], nw_ref[...])
        sb, _ = _silu_and_grad(zb_ref[...])
        y_b = (n_b * sb).astype(BF16)
        a = pre_ref[...] * u_ref[...]
        halo = jnp.where(i > 0, hpre_ref[...] * hu_ref[...], 0.0)
        cw = cw_ref[...]
        conv = cw[0:1] * _shift_rows_down(halo, a, 2) + cw[1:2] * _shift_rows_down(halo, a, 1) + cw[2:3] * a
        sc, _ = _silu_and_grad(zc_ref[...])
        y_c = (post_ref[...] * conv * sc).astype(BF16)
        merged = None
        for k, yk in enumerate((y_a, y_b, y_c)):
            yc_ref[:, k * WIDTH:(k + 1) * WIDTH] = yk
            term = _sigmoid(g_ref[:, k * d:(k + 1) * d]) * _dot(yk, wb_ref[k])
            merged = term if merged is None else merged + term
        mb = merged.astype(BF16)
        mg_ref[...] = mb
        y = _dot(mb, wo_ref[...])
        r = ALPHA * x_ref[...] + (1.0 + gate_ref[...]) * y
        rhat, _ = _standardize(r)
        xn_ref[...] = rhat * lg_ref[...] + lbias_ref[...]

    wcol = lambda cb: pl.BlockSpec((tm, WIDTH), lambda i: (i, cb))
    halo_spec = lambda cb: pl.BlockSpec((8, WIDTH), lambda i: (jnp.maximum(i * hb - 1, 0), cb))
    vec = lambda w: pl.BlockSpec((1, w), lambda i: (0, 0))
    return _pcall(
        body, name=name,
        out_shape=(jax.ShapeDtypeStruct((s_len, d), F32), jax.ShapeDtypeStruct((s_len, d), BF16),
                   jax.ShapeDtypeStruct((s_len, 3 * WIDTH), BF16)),
        grid=(s_len // tm,),
        in_specs=[pl.BlockSpec((tm, d), lambda i: (i, 0)),
                  wcol(0), wcol(3), wcol(0), wcol(7), wcol(8), wcol(9), wcol(10), wcol(11),
                  halo_spec(8), halo_spec(10),
                  pl.BlockSpec((tm, 3 * d), lambda i: (i, 2)),
                  vec(d), vec(WIDTH),
                  pl.BlockSpec((3, WIDTH), lambda i: (0, 0)),
                  pl.BlockSpec((3, WIDTH, d), lambda i: (0, 0, 0)),
                  pl.BlockSpec((d, d), lambda i: (0, 0)),
                  vec(d), vec(d)],
        out_specs=(pl.BlockSpec((tm, d), lambda i: (i, 0)), pl.BlockSpec((tm, d), lambda i: (i, 0)),
                   pl.BlockSpec((tm, 3 * WIDTH), lambda i: (i, 0))),
        semantics=("arbitrary",))(x, o_a, proj, o_b, proj, proj, proj, proj, proj, proj, proj, proj,
                                  gate, norm_w, conv_w, wb, w_out, ln_g, ln_b)


def _loss_fwd_bwd(y, target):
    s_len, d = y.shape
    tm = min(512, s_len)

    def body(y_ref, t_ref, loss_ref, dy_ref):
        @pl.when(pl.program_id(0) == 0)
        def _():
            loss_ref[...] = jnp.zeros_like(loss_ref)

        e = y_ref[...] - t_ref[...]
        dy_ref[...] = e * (1.0 / d)
        part = jnp.sum(jnp.sum(e * e, axis=-1, keepdims=True), axis=0, keepdims=True)
        loss_ref[...] += part * (0.5 / d)

    tile = pl.BlockSpec((tm, d), lambda i: (i, 0))
    return _pcall(body, name="loss", grid=(s_len // tm,),
                  out_shape=(jax.ShapeDtypeStruct((1, 1), F32), jax.ShapeDtypeStruct((s_len, d), F32)),
                  in_specs=[tile, tile],
                  out_specs=(pl.BlockSpec((1, 1), lambda i: (0, 0)), tile),
                  semantics=("arbitrary",))(y, target)


def _merge_bwd(dxn, x, merged, ycat, proj, gate, wb, w_out, ln_g, name):
    s_len, d = x.shape
    tm = min(256, s_len)

    def body(dxn_ref, x_ref, mg_ref, yc_ref, g_ref, gate_ref, wb_ref, wo_ref, lg_ref,
             dres_ref, dyc_ref, dg_ref, gwo_ref, gwb_ref, vec_ref):
        @pl.when(pl.program_id(0) == 0)
        def _():
            gwo_ref[...] = jnp.zeros_like(gwo_ref)
            gwb_ref[...] = jnp.zeros_like(gwb_ref)
            vec_ref[...] = jnp.zeros_like(vec_ref)

        mb = mg_ref[...]
        one_gate = 1.0 + gate_ref[...]
        y = _dot(mb, wo_ref[...])
        r = ALPHA * x_ref[...] + one_gate * y
        rhat, rstd = _standardize(r)
        dxn = dxn_ref[...]
        dr = _standardize_bwd(rhat, rstd, dxn * lg_ref[...])
        vec_ref[0:1, :] += jnp.sum(dxn * rhat, axis=0, keepdims=True)
        vec_ref[1:2, :] += jnp.sum(dxn, axis=0, keepdims=True)
        vec_ref[2:3, :] += jnp.sum(dr * y, axis=0, keepdims=True)
        dres_ref[...] = ALPHA * dr
        dy = (one_gate * dr).astype(BF16)
        gwo_ref[...] += _dot_tn(mb, dy)
        dmerged = _dot_nt(dy, wo_ref[...])
        for k in range(3):
            yk = yc_ref[:, k * WIDTH:(k + 1) * WIDTH]
            sg = _sigmoid(g_ref[:, k * d:(k + 1) * d])
            pk = _dot(yk, wb_ref[k])
            dg_ref[:, k * d:(k + 1) * d] = (dmerged * pk * sg * (1.0 - sg)).astype(BF16)
            dpk = (dmerged * sg).astype(BF16)
            dyc_ref[:, k * WIDTH:(k + 1) * WIDTH] = _dot_nt(dpk, wb_ref[k])
            gwb_ref[k] += _dot_tn(yk, dpk)

    tile = lambda w: pl.BlockSpec((tm, w), lambda i: (i, 0))
    vec = pl.BlockSpec((1, d), lambda i: (0, 0))
    return _pcall(
        body, name=name,
        out_shape=(jax.ShapeDtypeStruct((s_len, d), F32), jax.ShapeDtypeStruct((s_len, 3 * WIDTH), F32),
                   jax.ShapeDtypeStruct(proj.shape, BF16), jax.ShapeDtypeStruct((d, d), F32),
                   jax.ShapeDtypeStruct((3, WIDTH, d), F32), jax.ShapeDtypeStruct((8, d), F32)),
        grid=(s_len // tm,),
        in_specs=[tile(d), tile(d), tile(d), tile(3 * WIDTH),
                  pl.BlockSpec((tm, 3 * d), lambda i: (i, 2)),
                  vec, pl.BlockSpec((3, WIDTH, d), lambda i: (0, 0, 0)),
                  pl.BlockSpec((d, d), lambda i: (0, 0)), vec],
        out_specs=(tile(d), tile(3 * WIDTH), pl.BlockSpec((tm, 3 * d), lambda i: (i, 2)),
                   pl.BlockSpec((d, d), lambda i: (0, 0)),
                   pl.BlockSpec((3, WIDTH, d), lambda i: (0, 0, 0)),
                   pl.BlockSpec((8, d), lambda i: (0, 0))),
        semantics=("arbitrary",))(dxn, x, merged, ycat, proj, gate, wb, w_out, ln_g)


def _branch_bwd(dycat, proj, o_a, o_b, norm_w, conv_w, dproj, name):
    s_len = proj.shape[0]
    tm = min(256, s_len)
    hb = tm // 8
    n_tiles = s_len // tm

    def body(dya_ref, dyb_ref, dyc_ref, oa_ref, za_ref, ob_ref, zb_ref, pre_ref, post_ref, u_ref, zc_ref,
             hpre_ref, hu_ref, ndyc_ref, npost_ref, nzc_ref, nw_ref, cw_ref, dproj_in,
             dproj_ref, doa_ref, dob_ref, vec_ref, dza_scr, dzb_scr, dc_scr, sems):
        del dproj_in
        i = pl.program_id(0)

        @pl.when(i == 0)
        def _():
            vec_ref[...] = jnp.zeros_like(vec_ref)

        sa, dsa = _silu_and_grad(za_ref[...])
        dya = dya_ref[...]
        doa_ref[...] = dya * sa
        dza_scr[...] = (dya * oa_ref[...] * dsa).astype(BF16)
        nw = nw_ref[...]
        n_b, ohat, rstd = _rms_heads(ob_ref[...], nw)
        sb, dsb = _silu_and_grad(zb_ref[...])
        dyb = dyb_ref[...]
        dzb_scr[...] = (dyb * n_b * dsb).astype(BF16)
        dn = dyb * sb
        vec_ref[0:1, :] += jnp.sum(dn * ohat, axis=0, keepdims=True)
        dnw = dn * nw
        parts = []
        for h in range(WIDTH // HG_HEAD_DIM):
            sl = slice(h * HG_HEAD_DIM, (h + 1) * HG_HEAD_DIM)
            m2 = jnp.mean(dnw[:, sl] * ohat[:, sl], axis=-1, keepdims=True)
            parts.append(rstd[:, sl] * (dnw[:, sl] - ohat[:, sl] * m2))
        dob_ref[...] = jnp.concatenate(parts, axis=-1)
        cw = cw_ref[...]
        pre, u, post = pre_ref[...], u_ref[...], post_ref[...]
        a = pre * u
        halo = jnp.where(i > 0, hpre_ref[...] * hu_ref[...], 0.0)
        a1 = _shift_rows_down(halo, a, 1)
        a2 = _shift_rows_down(halo, a, 2)
        conv = cw[0:1] * a2 + cw[1:2] * a1 + cw[2:3] * a
        sc, dsc = _silu_and_grad(zc_ref[...])
        dyc = dyc_ref[...]
        dconv = dyc * post * sc
        nsc, _ = _silu_and_grad(nzc_ref[...])
        nxt = jnp.where(i < n_tiles - 1, ndyc_ref[...] * npost_ref[...] * nsc, 0.0)
        da = cw[2:3] * dconv + cw[1:2] * _shift_rows_up(dconv, nxt, 1) + cw[0:1] * _shift_rows_up(dconv, nxt, 2)
        dc_scr[:, 0 * WIDTH:1 * WIDTH] = (da * u).astype(BF16)
        dc_scr[:, 1 * WIDTH:2 * WIDTH] = (dyc * conv * sc).astype(BF16)
        dc_scr[:, 2 * WIDTH:3 * WIDTH] = (da * pre).astype(BF16)
        dc_scr[:, 3 * WIDTH:4 * WIDTH] = (dyc * post * conv * dsc).astype(BF16)
        vec_ref[1:2, :] += jnp.sum(dconv * a2, axis=0, keepdims=True)
        vec_ref[2:3, :] += jnp.sum(dconv * a1, axis=0, keepdims=True)
        vec_ref[3:4, :] += jnp.sum(dconv * a, axis=0, keepdims=True)
        rows = pl.ds(pl.multiple_of(i * tm, tm), tm)
        copies = [pltpu.make_async_copy(dza_scr, dproj_ref.at[rows, 3 * WIDTH:4 * WIDTH], sems.at[0]),
                  pltpu.make_async_copy(dzb_scr, dproj_ref.at[rows, 7 * WIDTH:8 * WIDTH], sems.at[1]),
                  pltpu.make_async_copy(dc_scr, dproj_ref.at[rows, 8 * WIDTH:12 * WIDTH], sems.at[2])]
        for cp in copies:
            cp.start()
        for cp in copies:
            cp.wait()

    wcol = lambda cb: pl.BlockSpec((tm, WIDTH), lambda i: (i, cb))
    prev = lambda cb: pl.BlockSpec((8, WIDTH), lambda i: (jnp.maximum(i * hb - 1, 0), cb))
    nxt = lambda cb: pl.BlockSpec((8, WIDTH), lambda i: (jnp.minimum((i + 1) * hb, s_len // 8 - 1), cb))
    anyspec = pl.BlockSpec(memory_space=pl.ANY)
    out = jax.ShapeDtypeStruct((s_len, WIDTH), F32)
    return _pcall(
        body, name=name,
        out_shape=(jax.ShapeDtypeStruct(dproj.shape, dproj.dtype), out, out, jax.ShapeDtypeStruct((8, WIDTH), F32)),
        grid=(n_tiles,),
        in_specs=[wcol(0), wcol(1), wcol(2), wcol(0), wcol(3), wcol(0), wcol(7), wcol(8), wcol(9), wcol(10), wcol(11),
                  prev(8), prev(10), nxt(2), nxt(9), nxt(11),
                  pl.BlockSpec((1, WIDTH), lambda i: (0, 0)), pl.BlockSpec((3, WIDTH), lambda i: (0, 0)), anyspec],
        out_specs=(anyspec, wcol(0), wcol(0), pl.BlockSpec((8, WIDTH), lambda i: (0, 0))),
        scratch_shapes=[pltpu.VMEM((tm, WIDTH), BF16), pltpu.VMEM((tm, WIDTH), BF16),
                        pltpu.VMEM((tm, 4 * WIDTH), BF16), pltpu.SemaphoreType.DMA((3,))],
        aliases={18: 0},
        semantics=("arbitrary",))(dycat, dycat, dycat, o_a, proj, o_b, proj, proj, proj, proj, proj,
                                  proj, proj, dycat, proj, proj, norm_w, conv_w, dproj)


def _sb_bwd(proj, do_a, totals, dproj, name):
    s_len = proj.shape[0]
    nb = s_len // BLK
    n_pairs = WIDTH // BLK
    scale = SB_HEAD_DIM ** -0.5
    gb = _sb_group_blocks(nb)
    kw = gb * BLK

    def body(q_ref, k_ref, v_ref, do_ref, tot_ref, dproj_in, dproj_ref, dq_ref, dk_ref, dv_ref, out_scr, sems):
        del dproj_in
        lane = _iota2((1, BLK), 1)
        row = _iota2((BLK, BLK), 0)
        col = _iota2((BLK, BLK), 1)
        ones = jnp.ones((BLK, BLK), BF16)
        before_and_sum = jnp.concatenate([(row < col).astype(BF16), ones], axis=1)
        upto_and_sum = jnp.concatenate([(row <= col).astype(BF16), ones], axis=1)
        qpos = _iota2((BLK, kw), 0)
        kpos = _iota2((BLK, kw), 1)
        head_lanes = [(lane // SB_HEAD_DIM) == hh for hh in range(2)]
        dk_ref[...] = jnp.zeros_like(dk_ref)
        dv_ref[...] = jnp.zeros_like(dv_ref)

        def group(i, gi, qms, doms, totals_i, carry, masked):
            c0 = pl.multiple_of(gi * kw, kw)
            kf = k_ref[pl.ds(c0, kw), :]
            vf = v_ref[pl.ds(c0, kw), :]
            if masked:
                valid = (c0 + kpos) < (i * BLK + qpos)
            out = []
            dk_add = None
            dv_add = None
            for hh in range(2):
                dq_acc, l_before, g_before = carry[3 * hh:3 * hh + 3]
                km = jnp.where(head_lanes[hh], kf, 0.0).astype(BF16)
                vm = jnp.where(head_lanes[hh], vf, 0.0).astype(BF16)
                z = _dot_nt(qms[hh], km)
                sp, sig = _softplus_parts(z)
                lg = -sp
                if masked:
                    lg = jnp.where(valid, lg, 0.0)
                l1, l2 = _split2(lg)
                parts = []
                for b in range(gb):
                    sl = slice(b * BLK, (b + 1) * BLK)
                    both = _dot(l1[:, sl], before_and_sum) + _dot(l2[:, sl], before_and_sum)
                    parts.append(totals_i[hh] - l_before - both[:, :BLK])
                    l_before = l_before + both[:, BLK:]
                a = jnp.exp(z + jnp.concatenate(parts, axis=1))
                if masked:
                    a = jnp.where(valid, a, 0.0)
                gmat = a * _dot_nt(doms[hh], vm)
                g1, g2 = _split2(gmat)
                parts = []
                for b in range(gb):
                    sl = slice(b * BLK, (b + 1) * BLK)
                    both = _dot(g1[:, sl], upto_and_sum) + _dot(g2[:, sl], upto_and_sum)
                    parts.append(g_before + both[:, :BLK])
                    g_before = g_before + both[:, BLK:]
                dz = gmat - sig * jnp.concatenate(parts, axis=1)
                if masked:
                    dz = jnp.where(valid, dz, 0.0)
                dzb = dz.astype(BF16)
                dk_h = _dot_tn(dzb, qms[hh])
                dv_h = _dot_tn(a.astype(BF16), doms[hh])
                dk_add = dk_h if dk_add is None else dk_add + dk_h
                dv_add = dv_h if dv_add is None else dv_add + dv_h
                out += [dq_acc + _dot(dzb, km), l_before, g_before]
            dk_ref[pl.ds(c0, kw), :] += dk_add
            dv_ref[pl.ds(c0, kw), :] += dv_add
            return tuple(out)

        def qblock(i, _):
            r0 = pl.multiple_of(i * BLK, BLK)
            qf = q_ref[pl.ds(r0, BLK), :] * scale
            dof = do_ref[pl.ds(r0, BLK), :]
            qms = [jnp.where(head_lanes[hh], qf, 0.0).astype(BF16) for hh in range(2)]
            doms = [jnp.where(head_lanes[hh], dof, 0.0).astype(BF16) for hh in range(2)]
            totals_i = [tot_ref[hh, pl.ds(r0, BLK), :] for hh in range(2)]
            zero = jnp.zeros((BLK, BLK), F32)
            last = i // gb
            carry = lax.fori_loop(0, last, lambda gi, cr: group(i, gi, qms, doms, totals_i, cr, False), (zero,) * 6)
            carry = group(i, last, qms, doms, totals_i, carry, True)
            dq_ref[pl.ds(r0, BLK), :] = (carry[0] + carry[3]) * scale
            return 0

        lax.fori_loop(0, nb, qblock, 0)
        pair = pl.program_id(0)
        copies = []
        for t, ref in enumerate((dq_ref, dk_ref, dv_ref)):
            out_scr[t] = ref[...].astype(BF16)
            col = pl.multiple_of((t * n_pairs + pair) * BLK, BLK)
            copies.append(pltpu.make_async_copy(out_scr.at[t], dproj_ref.at[:, pl.ds(col, BLK)], sems.at[t]))
            copies[-1].start()
        for cp in copies:
            cp.wait()

    col_spec = lambda off: pl.BlockSpec((s_len, BLK), lambda p: (0, off + p))
    anyspec = pl.BlockSpec(memory_space=pl.ANY)
    return _pcall(
        body, name=name, out_shape=jax.ShapeDtypeStruct(dproj.shape, dproj.dtype), grid=(n_pairs,),
        in_specs=[col_spec(0), col_spec(n_pairs), col_spec(2 * n_pairs), col_spec(0),
                  pl.BlockSpec((2, s_len, BLK), lambda p: (p, 0, 0)), anyspec],
        out_specs=anyspec,
        scratch_shapes=[pltpu.VMEM((s_len, BLK), F32)] * 3 + [pltpu.VMEM((3, s_len, BLK), BF16),
                                                              pltpu.SemaphoreType.DMA((3,))],
        aliases={5: 0},
        semantics=("arbitrary",))(proj, proj, proj, do_a, totals, dproj)


def _hgrn_bwd(proj, do_b, lb, dproj, name):
    s_len = proj.shape[0]
    nc = s_len // BLK
    nh = WIDTH // HG_HEAD_DIM
    base = 4 * WIDTH // BLK

    def body(q_ref, f_ref, i_ref, do_ref, lb_ref, dproj_in, dproj_ref, dlb_ref, mask_ref, st_ref, out_scr, sems):
        del dproj_in
        _hg_masks(mask_ref)
        row = _iota2((BLK, BLK), 0)
        col = _iota2((BLK, BLK), 1)
        lower_incl = (col <= row).astype(BF16)
        upper_incl = (col >= row).astype(BF16)
        lb_v = lb_ref[...]

        def load(ci):
            r0 = pl.multiple_of(ci * BLK, BLK)
            q, dq_fac, f, sig, g = _hg_chunk_inputs(q_ref[pl.ds(r0, BLK), :], f_ref[pl.ds(r0, BLK), :], lb_v)
            b = _dot_exact_l(lower_incl, g)
            return r0, q, dq_fac, f, sig, b, i_ref[pl.ds(r0, BLK), :]

        def fwd_chunk(ci, st):
            st_ref[ci] = st
            _, _, _, f, _, b, v = load(ci)
            b_end = b[BLK - 1:BLK, :]
            k_dec = ((1.0 - f) * jnp.exp(b_end - b)).astype(BF16)
            return st * jnp.exp(b_end) + _dot_tn(v.astype(BF16), k_dec)

        lax.fori_loop(0, nc, fwd_chunk, jnp.zeros((HG_HEAD_DIM, HG_HEAD_DIM), F32))

        def bwd_chunk(cc, carry):
            dst, suffix, dlb = carry
            ci = nc - 1 - cc
            r0, q, dq_fac, f, sig, b, v = load(ci)
            k = 1.0 - f
            vb = v.astype(BF16)
            do = do_ref[pl.ds(r0, BLK), :]
            dob = do.astype(BF16)
            b_end = b[BLK - 1:BLK, :]
            e_q = jnp.exp(b)
            e_k = jnp.exp(b_end - b)
            qe = (q * e_q).astype(BF16)
            kh = (k * e_k).astype(BF16)
            st1, st2 = _split2(st_ref[ci])
            ds1, ds2 = _split2(dst)
            dqe = _dot(dob, st1) + _dot(dob, st2)
            dkh = _dot(vb, ds1) + _dot(vb, ds2)
            dq = e_q * dqe
            dk = e_k * dkh
            dv = _dot_nt(kh, ds1)
            dst_new = dst * jnp.exp(b_end) + _dot_tn(dob, qe)
            dlog = qe.astype(F32) * dqe - kh.astype(F32) * dkh
            da = _dot_nt(dob, vb)
            sc = None
            for v_idx, m in enumerate(HG_LEVELS):
                qm, km, eq, ek, msk = _hg_level_terms(q, k, b, v_idx, m, mask_ref)
                term = _dot_nt(qm, km) * msk
                sc = term if sc is None else sc + term
                pm = (da * msk).astype(BF16)
                dqm = _dot(pm, km)
                dkm = _dot_tn(pm, qm)
                dq = dq + dqm * eq
                dk = dk + dkm * ek
                dlog = dlog + (qm.astype(F32) * dqm - km.astype(F32) * dkm)
            a_diag = jnp.sum(do * v, axis=-1, keepdims=True)
            s_diag = jnp.sum(q * k, axis=-1, keepdims=True)
            dq = dq + a_diag * k
            dk = dk + a_diag * q
            dv = dv + _dot_tn(sc.astype(BF16), dob) + s_diag * do
            dg = _dot_exact_l(upper_incl, dlog) + suffix
            dfull = dg / f - dk
            out_scr[0, pl.ds(r0, BLK), :] = (dq * dq_fac).astype(BF16)
            out_scr[1, pl.ds(r0, BLK), :] = (dfull * (1.0 - lb_v) * sig * (1.0 - sig)).astype(BF16)
            out_scr[2, pl.ds(r0, BLK), :] = dv.astype(BF16)
            dlb = dlb + jnp.sum(dfull * (1.0 - sig), axis=0, keepdims=True)
            return dst_new, dg[0:1, :], dlb

        zero_row = jnp.zeros((1, BLK), F32)
        _, _, dlb = lax.fori_loop(0, nc, bwd_chunk,
                                  (jnp.zeros((HG_HEAD_DIM, HG_HEAD_DIM), F32), zero_row, zero_row))
        dlb_ref[...] = jnp.broadcast_to(dlb, dlb_ref.shape)
        head = pl.program_id(0)
        copies = []
        for t in range(3):
            col = pl.multiple_of((base + t * nh + head) * BLK, BLK)
            copies.append(pltpu.make_async_copy(out_scr.at[t], dproj_ref.at[:, pl.ds(col, BLK)], sems.at[t]))
            copies[-1].start()
        for cp in copies:
            cp.wait()

    col_spec = lambda off: pl.BlockSpec((s_len, BLK), lambda h: (0, off + h))
    anyspec = pl.BlockSpec(memory_space=pl.ANY)
    return _pcall(
        body, name=name,
        out_shape=(jax.ShapeDtypeStruct(dproj.shape, dproj.dtype), jax.ShapeDtypeStruct((8, WIDTH), F32)),
        grid=(nh,),
        in_specs=[col_spec(base), col_spec(base + nh), col_spec(base + 2 * nh), col_spec(0),
                  pl.BlockSpec((1, BLK), lambda h: (0, h)), anyspec],
        out_specs=(anyspec, pl.BlockSpec((8, BLK), lambda h: (0, h))),
        scratch_shapes=[pltpu.VMEM((len(HG_LEVELS), BLK, BLK), F32),
                        pltpu.VMEM((nc, HG_HEAD_DIM, HG_HEAD_DIM), F32),
                        pltpu.VMEM((3, s_len, BLK), BF16), pltpu.SemaphoreType.DMA((3,))],
        aliases={5: 0},
        semantics=("arbitrary",))(proj, proj, proj, do_b, lb, dproj)


def _dh_matmul(dproj, w_full, name):
    s_len, n = dproj.shape
    d = w_full.shape[0]
    tm = min(512, s_len)
    tk = 1536

    def body(dp_ref, w_ref, dh_ref):
        part = _dot_nt(dp_ref[...], w_ref[...])

        @pl.when(pl.program_id(1) == 0)
        def _():
            dh_ref[...] = part

        @pl.when(pl.program_id(1) > 0)
        def _():
            dh_ref[...] += part

    return _pcall(
        body, name=name, out_shape=jax.ShapeDtypeStruct((s_len, d), F32),
        grid=(s_len // tm, n // tk),
        in_specs=[pl.BlockSpec((tm, tk), lambda i, k: (i, k)), pl.BlockSpec((d, tk), lambda i, k: (0, k))],
        out_specs=pl.BlockSpec((tm, d), lambda i, k: (i, 0)),
        semantics=("arbitrary", "arbitrary"))(dproj, w_full)


def _gw_matmul(h_t, dproj, name):
    d, s_len = h_t.shape
    n = dproj.shape[1]
    tn = 1152

    def body(ht_ref, dp_ref, gw_ref):
        gw_ref[...] = _dot(ht_ref[...], dp_ref[...]).astype(BF16)

    return _pcall(
        body, name=name, out_shape=jax.ShapeDtypeStruct((d, n), BF16),
        grid=(n // tn,),
        in_specs=[pl.BlockSpec((d, s_len), lambda j: (0, 0)), pl.BlockSpec((s_len, tn), lambda j: (0, j))],
        out_specs=pl.BlockSpec((d, tn), lambda j: (0, j)),
        semantics=("arbitrary",))(h_t, dproj)


def _ln_bwd(dh, x, scale, dres, name):
    s_len, d = x.shape
    tm = min(512, s_len)

    def body(dh_ref, x_ref, sc_ref, dres_ref, dx_ref, vec_ref):
        @pl.when(pl.program_id(0) == 0)
        def _():
            vec_ref[...] = jnp.zeros_like(vec_ref)

        dh = dh_ref[...]
        xs, rstd = _standardize(x_ref[...])
        vec_ref[0:1, :] += jnp.sum(dh, axis=0, keepdims=True)
        vec_ref[1:2, :] += jnp.sum(dh * xs, axis=0, keepdims=True)
        dx_ref[...] = _standardize_bwd(xs, rstd, dh * (1.0 + sc_ref[...])) + dres_ref[...]

    tile = pl.BlockSpec((tm, d), lambda i: (i, 0))
    return _pcall(body, name=name, grid=(s_len // tm,),
                  out_shape=(jax.ShapeDtypeStruct((s_len, d), F32), jax.ShapeDtypeStruct((8, d), F32)),
                  in_specs=[tile, tile, pl.BlockSpec((1, d), lambda i: (0, 0)), tile],
                  out_specs=(tile, pl.BlockSpec((8, d), lambda i: (0, 0))),
                  semantics=("arbitrary",))(dh, x, scale, dres)


def _wmod_grad(c_t, dmod):
    d = c_t.shape[0]
    n_layers, _, cm = dmod.shape

    def body(c_ref, dm_ref, o_ref):
        for l in range(n_layers):
            acc = None
            for b in range(NDEV):
                term = c_ref[:, b:b + 1] * dm_ref[l, b:b + 1, :]
                acc = term if acc is None else acc + term
            o_ref[l] = acc

    return _pcall(body, name="wmod_grad", out_shape=jax.ShapeDtypeStruct((n_layers, d, cm), F32))(c_t, dmod)


def _sum_adamw(parts_list, w, m, v, name):
    n_ranges = len(parts_list)
    n_src, range_rows, cols = parts_list[0].shape
    rows = range_rows * n_ranges
    tr = range_rows
    for cand in (512, 256, 128, 64, 32, 16, 8):
        if range_rows % cand == 0 and cand * cols * 4 <= (2 << 20):
            tr = cand
            break
    tiles = range_rows // tr

    def body(*refs):
        p_refs = refs[:n_ranges]
        w_ref, m_ref, v_ref, g_ref, d_ref, nm_ref, nv_ref = refs[n_ranges:]

        def step(p_ref):
            g = p_ref[0].astype(F32)
            for s in range(1, n_src):
                g = g + p_ref[s].astype(F32)
            nm = ADAM_B1 * m_ref[...] + (1.0 - ADAM_B1) * g
            nv = ADAM_B2 * v_ref[...] + (1.0 - ADAM_B2) * (g * g)
            m_hat = nm / (1.0 - ADAM_B1 ** ADAM_STEP)
            v_hat = nv / (1.0 - ADAM_B2 ** ADAM_STEP)
            g_ref[...] = g
            d_ref[...] = -ADAM_LR * (m_hat / (jnp.sqrt(v_hat) + ADAM_EPS) + ADAM_WD * w_ref[...])
            nm_ref[...] = nm
            nv_ref[...] = nv

        if n_ranges == 1:
            step(p_refs[0])
        else:
            for j in range(n_ranges):
                @pl.when(pl.program_id(0) // tiles == j)
                def _(j=j):
                    step(p_refs[j])

    def part_spec(j):
        return pl.BlockSpec((n_src, tr, cols), lambda i: (0, jnp.clip(i - j * tiles, 0, tiles - 1), 0))

    tile = pl.BlockSpec((tr, cols), lambda i: (i, 0))
    out = jax.ShapeDtypeStruct((rows, cols), F32)
    return _pcall(body, name=name, grid=(rows // tr,), out_shape=(out,) * 4,
                  in_specs=[part_spec(j) for j in range(n_ranges)] + [tile, tile, tile],
                  out_specs=(tile,) * 4, semantics=("arbitrary",))(*parts_list, w, m, v)


def _sum_parts(parts, name):
    n_src = parts.shape[0]

    def body(p_ref, o_ref):
        acc = p_ref[0]
        for s in range(1, n_src):
            acc = acc + p_ref[s]
        o_ref[...] = acc

    return _pcall(body, name=name, out_shape=jax.ShapeDtypeStruct(parts.shape[1:], F32))(parts)


def _lower_bound_table(lower_bounds):
    p = jax.nn.softmax(lower_bounds.astype(F32), axis=0)
    return jnp.cumsum(p, axis=0) - p[0:1]


def _pad_rows(v, width):
    n = v.shape[0]
    rows = -(-n // width)
    rows = -(-rows // 8) * 8
    return jnp.pad(v, (0, rows * width - n)).reshape(rows, width)


def kernel(x, c, w_mod, b_mod, w_in, conv_w, hgrn_norm_w, lower_bounds, w_branch, w_out, ln_g, ln_b, loss_target, m_w_mod, m_b_mod, m_w_in, m_conv_w, m_hgrn_norm_w, m_lower_bounds, m_w_branch, m_w_out, m_ln_g, m_ln_b, v_w_mod, v_b_mod, v_w_in, v_conv_w, v_hgrn_norm_w, v_lower_bounds, v_w_branch, v_w_out, v_ln_g, v_ln_b):
    n_layers = N_LAYERS
    s_len, d = x.shape[1], x.shape[2]
    n_cols = w_in.shape[2] * NDEV
    cw_cols = conv_w.shape[2]
    cm = w_mod.shape[2]
    me = _my_index()
    x0 = x[0]
    target = loss_target[0]

    small = _pad_rows(jnp.concatenate([c.reshape(-1), conv_w.reshape(-1)]), BLK)
    small_all = _all_gather_small("gather_c_conv", small).reshape(NDEV, -1)
    c_all = small_all[:, :d]
    conv_full = small_all[:, d:d + n_layers * 3 * cw_cols].reshape(NDEV, n_layers, 3, cw_cols)
    conv_full = conv_full.transpose(1, 2, 0, 3).reshape(n_layers, 3, WIDTH)

    b_mod_mine = lax.dynamic_slice_in_dim(b_mod, me * cm, cm, axis=1).reshape(n_layers, 1, cm)
    mod_cols = _mod_fwd(c_all, w_mod, b_mod_mine)
    mod_all = _all_gather_small("gather_mod", mod_cols.reshape(n_layers * NDEV, cm))
    mod_all = mod_all.reshape(NDEV, n_layers, NDEV, cm)
    mod_mine = lax.dynamic_index_in_dim(mod_all, me, axis=2, keepdims=False)
    mod_mine = mod_mine.transpose(1, 0, 2).reshape(n_layers, 3, 1, d)

    shard = w_in.shape[2]
    dsh = d // NDEV
    w_in_b, w_branch_b, w_out_b = w_in.astype(BF16), w_branch.astype(BF16), w_out.astype(BF16)
    window = lambda ref, dev: ref.at[:, pl.ds(pl.multiple_of(dev * shard, BLK), shard)]
    gather_transfers = [(0, 0, _whole, window), (1, 1, _whole, _slot), (2, 2, _whole, _slot)]
    gathered_shapes = [((d, n_cols), BF16), ((NDEV, 3, WIDTH, dsh), BF16), ((NDEV, dsh, d), BF16)]
    weights = [None] * n_layers
    weights[0] = _exchange("gather_weights_0", [w_in_b[0], w_branch_b[0], w_out_b[0]],
                           [jax.ShapeDtypeStruct(s, t) for s, t in gathered_shapes], gather_transfers, in_vmem=False)
    pending = [None] * n_layers
    tie = weights[0][2]
    for l in range(1, n_layers):
        lands = [_place_own((d, n_cols), BF16, w_in_b[l], (0, me * shard)),
                 _place_own((NDEV, 3, WIDTH, dsh), BF16, w_branch_b[l][None], (me, 0, 0, 0)),
                 _place_own((NDEV, dsh, d), BF16, w_out_b[l][None], (me, 0, 0))]
        tie, lands = lax.optimization_barrier((tie, lands))
        pending[l] = _exchange_start(f"gather_weights_{l}_start", [w_in_b[l], w_branch_b[l], w_out_b[l]], lands,
                                     gather_transfers)
        tie = pending[l][4]

    def full_weights(l):
        w_in_l, w_branch_l, w_out_l = weights[l]
        return w_in_l, w_branch_l.transpose(1, 2, 0, 3).reshape(3, WIDTH, d), w_out_l.reshape(d, d)

    lbs = _lower_bound_table(lower_bounds)
    norm_w4 = jnp.tile(hgrn_norm_w, (1, WIDTH // HG_HEAD_DIM))

    saved = []
    xl = x0
    for l in range(n_layers):
        shift, scale, gate = mod_mine[l, 0], mod_mine[l, 1], mod_mine[l, 2]
        if l == 0 and n_layers > 1:
            shift = shift + tie[0, 0]
        if l > 0:
            send_sems, recv_sems, p_ins, p_lands, _ = pending[l]
            weights[l] = _exchange_wait(f"gather_weights_{l}_wait", send_sems, recv_sems, p_ins, p_lands, xl,
                                        gather_transfers)
        w_in_l, wb_l, wo_l = full_weights(l)
        proj, h_t = _ln_proj(xl, shift, scale, w_in_l, f"ln_proj_{l}")
        o_a, totals = _sb_fwd(proj, f"sb_fwd_{l}")
        o_b = _hgrn_fwd(proj, lbs[l:l + 1], f"hgrn_fwd_{l}")
        x_new, merged, ycat = _merge_fwd(xl, proj, o_a, o_b, gate, norm_w4[l:l + 1], conv_full[l],
                                         wb_l, wo_l, ln_g[l:l + 1], ln_b[l:l + 1], f"merge_fwd_{l}")
        saved.append((xl, proj, h_t, o_a, totals, o_b, merged, ycat, w_in_l, wb_l, wo_l))
        xl = x_new

    loss_part, dx = _loss_fwd_bwd(xl, target)
    loss = lax.psum(loss_part[0, 0], ("x", "y", "c"))

    scatter_transfers = [(0, 0, window, _slot), (1, 1, _slot, _slot), (2, 2, _slot, _slot)]
    scattered_shapes = [((NDEV, d, shard), BF16), ((NDEV, 3, WIDTH, dsh), BF16), ((NDEV, dsh, d), BF16)]
    partials = [None] * n_layers
    pending = [None] * n_layers
    small_grads = [None] * n_layers
    dmod = [None] * n_layers
    tie = None
    for l in reversed(range(n_layers)):
        xl, proj, h_t, o_a, totals, o_b, merged, ycat, w_in_l, wb_l, wo_l = saved[l]
        scale, gate = mod_mine[l, 1], mod_mine[l, 2]
        if tie is not None:
            gate = gate + tie[0, 0]
        dres, dycat, dproj, gwo, gwb, mvec = _merge_bwd(dx, xl, merged, ycat, proj, gate, wb_l, wo_l,
                                                        ln_g[l:l + 1], f"merge_bwd_{l}")
        dproj, do_a, do_b, bvec = _branch_bwd(dycat, proj, o_a, o_b, norm_w4[l:l + 1], conv_full[l], dproj,
                                              f"branch_bwd_{l}")
        dproj = _sb_bwd(proj, do_a, totals, dproj, f"sb_bwd_{l}")
        dproj, dlb = _hgrn_bwd(proj, do_b, lbs[l:l + 1], dproj, f"hgrn_bwd_{l}")
        dh = _dh_matmul(dproj, w_in_l, f"dh_matmul_{l}")
        gwi = _gw_matmul(h_t, dproj, f"gw_matmul_{l}")
        dx, lvec = _ln_bwd(dh, xl, scale, dres, f"ln_bwd_{l}")
        dmod[l] = jnp.concatenate([lvec[0], lvec[1], mvec[2]])
        norm_grad = bvec[0].reshape(WIDTH // HG_HEAD_DIM, HG_HEAD_DIM).sum(axis=0)
        small_grads[l] = jnp.concatenate([mvec[0], mvec[1], norm_grad, dlb[0], bvec[1:4].reshape(-1)])
        gwb_by_owner = gwb.astype(BF16).reshape(3, WIDTH, NDEV, dsh).transpose(2, 0, 1, 3)
        gwo_by_owner = gwo.astype(BF16).reshape(NDEV, dsh, d)
        sources = [gwi, gwb_by_owner, gwo_by_owner]
        if l > 0:
            lands = [_place_own((NDEV, d, shard), BF16, lax.dynamic_slice_in_dim(gwi, me * shard, shard, axis=1)[None],
                                (me, 0, 0)),
                     _place_own((NDEV, 3, WIDTH, dsh), BF16, lax.dynamic_slice_in_dim(gwb_by_owner, me, 1, axis=0),
                                (me, 0, 0, 0)),
                     _place_own((NDEV, dsh, d), BF16, lax.dynamic_slice_in_dim(gwo_by_owner, me, 1, axis=0),
                                (me, 0, 0))]
            pending[l] = _exchange_start(f"scatter_grads_{l}_start", sources, lands, scatter_transfers)
            tie = pending[l][4]
        else:
            for j in range(1, n_layers):
                send_sems, recv_sems, p_ins, p_lands, _ = pending[j]
                partials[j] = _exchange_wait(f"scatter_grads_{j}_wait", send_sems, recv_sems, p_ins, p_lands, dx,
                                             scatter_transfers)
            partials[0] = _exchange("scatter_grads_0", sources, [jax.ShapeDtypeStruct(s, t) for s, t in scattered_shapes],
                                    scatter_transfers, in_vmem=False)
    grad_x = dx[None]
    p_in = [partials[l][0] for l in range(n_layers)]
    p_branch = [partials[l][1].reshape(NDEV, 3 * WIDTH, dsh) for l in range(n_layers)]
    p_out = [partials[l][2] for l in range(n_layers)]

    small_vec = jnp.concatenate(dmod + small_grads)
    n_small = small_vec.shape[0]
    small_all = _all_gather_small("gather_small_grads", _pad_rows(small_vec, BLK))
    small_sum = _sum_parts(small_all, "sum_small_grads").reshape(-1)[:n_small]
    dmod_all = small_all.reshape(NDEV, -1)[:, :n_layers * 3 * d].reshape(NDEV, n_layers, 3 * d)

    off = n_layers * 3 * d
    grad_b_mod = small_sum[:off].reshape(n_layers, 3 * d)
    per_layer = 2 * d + HG_HEAD_DIM + WIDTH + 3 * WIDTH
    g_ln_g, g_ln_b, g_norm, g_lbs, g_conv = [], [], [], [], []
    for l in range(n_layers):
        seg = small_sum[off + l * per_layer: off + (l + 1) * per_layer]
        g_ln_g.append(seg[:d])
        g_ln_b.append(seg[d:2 * d])
        g_norm.append(seg[2 * d:2 * d + HG_HEAD_DIM])
        g_lbs.append(seg[2 * d + HG_HEAD_DIM:2 * d + HG_HEAD_DIM + WIDTH])
        g_conv.append(seg[2 * d + HG_HEAD_DIM + WIDTH:].reshape(3, WIDTH))
    grad_ln_g, grad_ln_b = jnp.stack(g_ln_g), jnp.stack(g_ln_b)
    grad_norm = jnp.stack(g_norm)
    _, lbs_vjp = jax.vjp(_lower_bound_table, lower_bounds)
    grad_lower = lbs_vjp(jnp.stack(g_lbs))[0]
    grad_conv = lax.dynamic_slice_in_dim(jnp.stack(g_conv), me * cw_cols, cw_cols, axis=2)

    dmod_mine = lax.dynamic_slice_in_dim(dmod_all, me * cm, cm, axis=2).transpose(1, 0, 2)
    grad_w_mod = _wmod_grad(c_all.T, dmod_mine)

    def adam(parts_list, w, m, v, name):
        shape = w.shape
        cols = shape[-1]
        flat = lambda a: a.reshape(-1, cols)
        outs = _sum_adamw(parts_list, flat(w), flat(m), flat(v), name)
        return [o.reshape(shape) for o in outs]

    r_w_in = adam(p_in, w_in, m_w_in, v_w_in, "adamw_w_in")
    r_w_branch = adam(p_branch, w_branch, m_w_branch, v_w_branch, "adamw_w_branch")
    r_w_out = adam(p_out, w_out, m_w_out, v_w_out, "adamw_w_out")
    r_w_mod = adam([grad_w_mod.reshape(1, -1, cm)], w_mod, m_w_mod, v_w_mod, "adamw_w_mod")

    small_names = ["b_mod", "conv_w", "hgrn_norm_w", "lower_bounds", "ln_g", "ln_b"]
    small_g = [grad_b_mod, grad_conv, grad_norm, grad_lower, grad_ln_g, grad_ln_b]
    small_w = [b_mod, conv_w, hgrn_norm_w, lower_bounds, ln_g, ln_b]
    small_m = [m_b_mod, m_conv_w, m_hgrn_norm_w, m_lower_bounds, m_ln_g, m_ln_b]
    small_v = [v_b_mod, v_conv_w, v_hgrn_norm_w, v_lower_bounds, v_ln_g, v_ln_b]
    pack = lambda arrs: _pad_rows(jnp.concatenate([a.reshape(-1) for a in arrs]), BLK)
    packed = _sum_adamw([pack(small_g)[None]], pack(small_w), pack(small_m), pack(small_v), "adamw_small")
    r_small = {n: [] for n in small_names}
    for res in packed:
        flat = res.reshape(-1)
        pos = 0
        for n, w in zip(small_names, small_w):
            r_small[n].append(flat[pos:pos + w.size].reshape(w.shape))
            pos += w.size

    results = {"w_mod": r_w_mod, "w_in": r_w_in, "w_branch": r_w_branch, "w_out": r_w_out, **r_small}
    order = ["w_mod", "b_mod", "w_in", "conv_w", "hgrn_norm_w", "lower_bounds", "w_branch", "w_out", "ln_g", "ln_b"]
    outs = [loss, grad_x]
    for idx in range(4):
        outs.extend(results[n][idx] for n in order)
    return tuple(outs)
```

```python
import jax
import jax.numpy as jnp
from jax import lax
from jax.experimental import pallas as pl
from jax.experimental.pallas import tpu as pltpu

F32 = jnp.float32
BF16 = jnp.bfloat16
NDEV = 8
N_LAYERS = 2
SB_HEAD_DIM = 64
HG_HEAD_DIM = 128
WIDTH = 512
BLK = 128
LN_EPS = 1e-5
RMS_EPS = 1e-6
ALPHA = (2.0 * N_LAYERS) ** 0.25
ADAM_LR, ADAM_B1, ADAM_B2, ADAM_EPS, ADAM_WD, ADAM_STEP = 0.001, 0.9, 0.999, 1e-08, 0.01, 10
VMEM_LIMIT = 56 * 1024 * 1024
MESH = pl.DeviceIdType.MESH
HG_LEVELS = (64, 32, 16, 8, 4, 2, 1)


def _pcall(body, *, name, out_shape, grid=None, in_specs=None, out_specs=None, scratch_shapes=(),
           semantics=None, aliases=None):
    kwargs = {}
    if grid is not None:
        kwargs["grid"] = grid
    if in_specs is not None:
        kwargs["in_specs"] = in_specs
    if out_specs is not None:
        kwargs["out_specs"] = out_specs
    if aliases:
        kwargs["input_output_aliases"] = aliases
    return pl.pallas_call(
        body, name=name, out_shape=out_shape, scratch_shapes=list(scratch_shapes),
        compiler_params=pltpu.CompilerParams(dimension_semantics=semantics, vmem_limit_bytes=VMEM_LIMIT),
        interpret=False, **kwargs)


def _dot(a, b):
    return jnp.dot(a, b, preferred_element_type=F32)


def _dot_nt(a, b):
    return lax.dot_general(a, b, (((1,), (1,)), ((), ())), preferred_element_type=F32)


def _dot_tn(a, b):
    return lax.dot_general(a, b, (((0,), (0,)), ((), ())), preferred_element_type=F32)


def _split3(x):
    x1 = x.astype(BF16)
    r1 = x - x1.astype(F32)
    x2 = r1.astype(BF16)
    r2 = r1 - x2.astype(F32)
    return x1, x2, r2.astype(BF16)


def _split2(x):
    x1 = x.astype(BF16)
    return x1, (x - x1.astype(F32)).astype(BF16)


def _dot_exact_l(m_bf16, x):
    x1, x2, x3 = _split3(x)
    return _dot(m_bf16, x1) + _dot(m_bf16, x2) + _dot(m_bf16, x3)


def _sigmoid(x):
    return 1.0 / (1.0 + jnp.exp(-x))


def _silu_and_grad(x):
    s = _sigmoid(x)
    return x * s, s * (1.0 + x * (1.0 - s))


LOG2E = 1.4426950408889634
MASKED_SCORE = -1e30


def _softplus2_parts(z2):
    minus_abs = lax.bitcast_convert_type(lax.bitcast_convert_type(z2, jnp.int32) | jnp.int32(-2 ** 31), F32)
    e = jnp.exp2(minus_abs)
    sp2 = jnp.maximum(z2, 0.0) + jnp.log2(1.0 + e)
    r = 1.0 / (1.0 + e)
    return sp2, jnp.where(z2 >= 0.0, r, e * r)


def _split2_lanes(x):
    x1 = x.astype(BF16)
    return jnp.concatenate([x1, (x - x1.astype(F32)).astype(BF16)], axis=1)


def _iota2(shape, dim):
    return lax.broadcasted_iota(jnp.int32, shape, dim)


def _standardize(x):
    mu = jnp.mean(x, axis=-1, keepdims=True)
    xc = x - mu
    var = jnp.mean(xc * xc, axis=-1, keepdims=True)
    rstd = lax.rsqrt(var + LN_EPS)
    return xc * rstd, rstd


def _standardize_bwd(xhat, rstd, dxhat):
    m1 = jnp.mean(dxhat, axis=-1, keepdims=True)
    m2 = jnp.mean(dxhat * xhat, axis=-1, keepdims=True)
    return rstd * (dxhat - m1 - xhat * m2)


def _my_index():
    return 4 * lax.axis_index("x") + 2 * lax.axis_index("y") + lax.axis_index("c")


def _exchange(name, ins, out_shapes, transfers, in_vmem):
    n_in, n_out, n_t = len(ins), len(out_shapes), len(transfers)

    def body(*refs):
        in_refs, out_refs = refs[:n_in], refs[n_in:n_in + n_out]
        send_sems, recv_sems, local_sems = refs[n_in + n_out:]
        x, y, c = lax.axis_index("x"), lax.axis_index("y"), lax.axis_index("c")
        me = 4 * x + 2 * y + c
        started = []
        for t, (i, o, src_fn, dst_fn) in enumerate(transfers):
            own = pltpu.make_async_copy(src_fn(in_refs[i], me), dst_fn(out_refs[o], me), local_sems.at[t])
            own.start()
            started.append(own)
        arrivals = []
        for k in range(1, NDEV):
            px = x ^ ((k >> 2) & 1)
            py = y ^ ((k >> 1) & 1)
            pc = c ^ (k & 1)
            peer = 4 * px + 2 * py + pc
            for t, (i, o, src_fn, dst_fn) in enumerate(transfers):
                sem = t * (NDEV - 1) + k - 1
                push = pltpu.make_async_remote_copy(
                    src_ref=src_fn(in_refs[i], peer), dst_ref=dst_fn(out_refs[o], me),
                    send_sem=send_sems.at[sem], recv_sem=recv_sems.at[sem],
                    device_id=(px, py, pc), device_id_type=MESH)
                push.start()
                started.append(push)
                arrivals.append(pltpu.make_async_remote_copy(
                    src_ref=src_fn(in_refs[i], peer), dst_ref=dst_fn(out_refs[o], peer),
                    send_sem=send_sems.at[sem], recv_sem=recv_sems.at[sem],
                    device_id=(px, py, pc), device_id_type=MESH))
        for arrival in arrivals:
            arrival.wait_recv()
        for cp in started[n_t:]:
            cp.wait_send()
        for own in started[:n_t]:
            own.wait()

    space = pltpu.VMEM if in_vmem else pl.ANY
    spec = pl.BlockSpec(memory_space=space)
    return _pcall(
        body, name=name, out_shape=out_shapes,
        in_specs=[spec] * n_in, out_specs=[spec] * n_out,
        scratch_shapes=[pltpu.SemaphoreType.DMA((n_t * (NDEV - 1),)),
                        pltpu.SemaphoreType.DMA((n_t * (NDEV - 1),)),
                        pltpu.SemaphoreType.DMA((n_t,))])(*ins)


def _whole(ref, dev):
    return ref


def _slot(ref, dev):
    return ref.at[dev]


def _all_gather_small(name, v):
    out = _exchange(name, [v], [jax.ShapeDtypeStruct((NDEV,) + v.shape, v.dtype)],
                    [(0, 0, _whole, _slot)], in_vmem=True)
    return out[0]


_HBM_SPEC = pl.BlockSpec(memory_space=pltpu.HBM)
_SEM_SPEC = pl.BlockSpec(memory_space=pltpu.SEMAPHORE)
_DATAFLOW = pltpu.SideEffectType.DATAFLOW_SIDE_EFFECTING


def _peers(x, y, c):
    for k in range(1, NDEV):
        px = x ^ ((k >> 2) & 1)
        py = y ^ ((k >> 1) & 1)
        pc = c ^ (k & 1)
        yield k, (px, py, pc), 4 * px + 2 * py + pc


def _exchange_start(name, ins, lands, transfers):
    n_in, n_buf = len(ins), len(ins) + len(lands)
    n_sem = len(transfers) * (NDEV - 1)

    def body(*refs):
        in_refs, land_refs = refs[:n_in], refs[n_in:n_buf]
        send_sems, recv_sems, token = refs[n_buf], refs[n_buf + 1], refs[-1]
        x, y, c = lax.axis_index("x"), lax.axis_index("y"), lax.axis_index("c")
        me = 4 * x + 2 * y + c
        for k, peer_id, peer in _peers(x, y, c):
            for t, (i, o, src_fn, dst_fn) in enumerate(transfers):
                sem = t * (NDEV - 1) + k - 1
                pltpu.make_async_remote_copy(
                    src_ref=src_fn(in_refs[i], peer), dst_ref=dst_fn(land_refs[o], me),
                    send_sem=send_sems.at[sem], recv_sem=recv_sems.at[sem],
                    device_id=peer_id, device_id_type=MESH).start()
        token[...] = jnp.zeros_like(token)

    bufs = [pltpu.with_memory_space_constraint(a, pltpu.HBM) for a in list(ins) + list(lands)]
    outs = pl.pallas_call(
        body, name=name,
        out_shape=(pltpu.SemaphoreType.DMA((n_sem,)), pltpu.SemaphoreType.DMA((n_sem,)))
        + tuple(pltpu.HBM(a.shape, a.dtype) for a in bufs) + (jax.ShapeDtypeStruct((8, BLK), F32),),
        in_specs=[_HBM_SPEC] * n_buf,
        out_specs=(_SEM_SPEC, _SEM_SPEC) + (_HBM_SPEC,) * n_buf + (pl.BlockSpec(memory_space=pltpu.VMEM),),
        input_output_aliases={b: 2 + b for b in range(n_buf)},
        compiler_params=pltpu.CompilerParams(has_side_effects=_DATAFLOW),
        interpret=False)(*bufs)
    return outs[0], outs[1], list(outs[2:2 + n_in]), list(outs[2 + n_in:2 + n_buf]), outs[-1]


def _exchange_wait(name, send_sems, recv_sems, ins, lands, after, transfers):
    n_in, n_buf = len(ins), len(ins) + len(lands)

    def body(*refs):
        in_refs, land_refs = refs[:n_in], refs[n_in:n_buf]
        send_sems, recv_sems = refs[n_buf], refs[n_buf + 1]
        x, y, c = lax.axis_index("x"), lax.axis_index("y"), lax.axis_index("c")
        for k, peer_id, peer in _peers(x, y, c):
            for t, (i, o, src_fn, dst_fn) in enumerate(transfers):
                sem = t * (NDEV - 1) + k - 1
                cp = pltpu.make_async_remote_copy(
                    src_ref=src_fn(in_refs[i], peer), dst_ref=dst_fn(land_refs[o], peer),
                    send_sem=send_sems.at[sem], recv_sem=recv_sems.at[sem],
                    device_id=peer_id, device_id_type=MESH)
                cp.wait_send()
                cp.wait_recv()

    bufs = list(ins) + list(lands)
    outs = pl.pallas_call(
        body, name=name, out_shape=tuple(pltpu.HBM(a.shape, a.dtype) for a in bufs),
        in_specs=[_HBM_SPEC] * n_buf + [_SEM_SPEC, _SEM_SPEC, pl.BlockSpec(memory_space=pl.ANY)],
        out_specs=(_HBM_SPEC,) * n_buf,
        input_output_aliases={b: b for b in range(n_buf)},
        compiler_params=pltpu.CompilerParams(has_side_effects=_DATAFLOW),
        interpret=False)(*bufs, send_sems, recv_sems, after)
    return list(outs[n_in:])


def _place_own(shape, dtype, own, start):
    return lax.dynamic_update_slice(lax.empty(shape, dtype), own, start)


def _mod_fwd(c_all, w_mod, b_mod_mine):
    n_layers, _, cm = w_mod.shape

    def body(c_ref, w_ref, b_ref, o_ref):
        for l in range(n_layers):
            o_ref[l] = jnp.dot(c_ref[...], w_ref[l], preferred_element_type=F32,
                               precision=lax.Precision.HIGHEST) + b_ref[l]

    return _pcall(body, name="mod_fwd", out_shape=jax.ShapeDtypeStruct((n_layers, NDEV, cm), F32))(
        c_all, w_mod, b_mod_mine)


def _ln_proj(x, shift, scale, w_full, name):
    s_len, d = x.shape
    n = w_full.shape[1]
    tm = min(512, s_len)
    tn = 1024

    def body(x_ref, sh_ref, sc_ref, w_ref, proj_ref, ht_ref, h_scr):
        @pl.when(pl.program_id(1) == 0)
        def _():
            xs, _ = _standardize(x_ref[...])
            h = xs * (1.0 + sc_ref[...]) + sh_ref[...]
            h_scr[...] = h.astype(BF16)
            ht_ref[...] = h.T.astype(BF16)

        proj_ref[...] = _dot(h_scr[...], w_ref[...])

    return _pcall(
        body, name=name,
        out_shape=(jax.ShapeDtypeStruct((s_len, n), F32), jax.ShapeDtypeStruct((d, s_len), BF16)),
        grid=(s_len // tm, n // tn),
        in_specs=[pl.BlockSpec((tm, d), lambda i, j: (i, 0)),
                  pl.BlockSpec((1, d), lambda i, j: (0, 0)),
                  pl.BlockSpec((1, d), lambda i, j: (0, 0)),
                  pl.BlockSpec((d, tn), lambda i, j: (0, j))],
        out_specs=(pl.BlockSpec((tm, tn), lambda i, j: (i, j)),
                   pl.BlockSpec((d, tm), lambda i, j: (0, i))),
        scratch_shapes=[pltpu.VMEM((tm, d), BF16)],
        semantics=("arbitrary", "arbitrary"))(x, shift, scale, w_full)


def _sb_group_blocks(nb):
    return min(4, nb)


def _sb_fwd(proj, name):
    s_len = proj.shape[0]
    nb = s_len // BLK
    n_pairs = WIDTH // BLK
    gb = _sb_group_blocks(nb)
    kw = gb * BLK

    def body(q_ref, k_ref, v_ref, o_ref, tot_ref):
        lane = _iota2((1, BLK), 1)
        row = _iota2((BLK, BLK), 0)
        col = _iota2((BLK, BLK), 1)
        half = jnp.concatenate([(row >= col).astype(BF16), jnp.ones((BLK, BLK), BF16)], axis=1)
        suffix_and_sum = jnp.concatenate([half, half], axis=0)
        qpos = _iota2((BLK, kw), 0)
        kpos = _iota2((BLK, kw), 1)
        head_lanes = [(lane // SB_HEAD_DIM) == hh for hh in range(2)]

        def scores(i, gi, qms, masked):
            c0 = pl.multiple_of(gi * kw, kw)
            kb = k_ref[pl.ds(c0, kw), :].astype(BF16)
            z2s = [_dot_nt(qms[hh], kb) for hh in range(2)]
            if masked:
                valid = (c0 + kpos) < (i * BLK + qpos)
                z2s = [jnp.where(valid, z2, MASKED_SCORE) for z2 in z2s]
            return tuple(z2s)

        def accumulate(gi, z2s, carry):
            c0 = pl.multiple_of(gi * kw, kw)
            vf = v_ref[pl.ds(c0, kw), :]
            sp2s = [_softplus2_parts(z2)[0] for z2 in z2s]
            terms = [[_split2_lanes(sp2[:, b * BLK:(b + 1) * BLK]) for b in range(gb)] for sp2 in sp2s]
            sums = [[_dot(t, suffix_and_sum) for t in head_terms] for head_terms in terms]
            weights, laters = [], []
            for hh in range(2):
                later = carry[2 * hh + 1]
                parts = [None] * gb
                for b in reversed(range(gb)):
                    parts[b] = sums[hh][b][:, :BLK] + later
                    later = later + sums[hh][b][:, BLK:]
                weights.append(jnp.exp2(z2s[hh] - jnp.concatenate(parts, axis=1)).astype(BF16))
                laters.append(later)
            outs = [_dot(weights[hh], jnp.where(head_lanes[hh], vf, 0.0).astype(BF16)) for hh in range(2)]
            return (carry[0] + outs[0], laters[0], carry[2] + outs[1], laters[1])

        def queries(i):
            qf = q_ref[pl.ds(pl.multiple_of(i * BLK, BLK), BLK), :] * (SB_HEAD_DIM ** -0.5 * LOG2E)
            return [jnp.where(head_lanes[hh], qf, 0.0).astype(BF16) for hh in range(2)]

        def qblock(i, first_scores):
            r0 = pl.multiple_of(i * BLK, BLK)
            qms = queries(i)
            zero = jnp.zeros((BLK, BLK), F32)
            last = i // gb

            def step(jj, state):
                gi = last - 1 - jj
                return scores(i, gi, qms, False) + accumulate(gi + 1, state[:2], state[2:])

            state = lax.fori_loop(0, last, step, first_scores + (zero,) * 4)
            nxt = jnp.minimum(i + 1, nb - 1)
            next_scores = scores(nxt, nxt // gb, queries(nxt), True)
            carry = accumulate(0, state[:2], state[2:])
            o_ref[pl.ds(r0, BLK), :] = carry[0] + carry[2]
            tot_ref[0, pl.ds(r0, BLK), :] = carry[1]
            tot_ref[1, pl.ds(r0, BLK), :] = carry[3]
            return next_scores

        lax.fori_loop(0, nb, qblock, scores(0, 0, queries(0), True))

    col_spec = lambda off: pl.BlockSpec((s_len, BLK), lambda p: (0, off + p))
    return _pcall(
        body, name=name,
        out_shape=(jax.ShapeDtypeStruct((s_len, WIDTH), F32),
                   jax.ShapeDtypeStruct((2 * n_pairs, s_len, BLK), F32)),
        grid=(n_pairs,),
        in_specs=[col_spec(0), col_spec(n_pairs), col_spec(2 * n_pairs)],
        out_specs=(pl.BlockSpec((s_len, BLK), lambda p: (0, p)),
                   pl.BlockSpec((2, s_len, BLK), lambda p: (p, 0, 0))),
        semantics=("arbitrary",))(proj, proj, proj)


def _hg_masks(mask_ref):
    row = _iota2((BLK, BLK), 0)
    col = _iota2((BLK, BLK), 1)
    for v, m in enumerate(HG_LEVELS):
        same = (row // (2 * m)) == (col // (2 * m))
        mask_ref[v] = (same & ((row & m) != 0) & ((col & m) == 0)).astype(F32)


def _hg_mid(b, m):
    if m >= 4:
        n = BLK // (2 * m)
        mid = b.reshape(n, 2 * m, BLK)[:, m - 1:m, :]
        return jnp.broadcast_to(mid, (n, 2 * m, BLK)).reshape(BLK, BLK)
    pos = _iota2((BLK, BLK), 0) & (2 * m - 1)
    out = b
    for p in range(2 * m):
        delta = (m - 1) - p
        if delta != 0:
            out = jnp.where(pos == p, pltpu.roll(b, (-delta) % BLK, 0), out)
    return out


def _hg_chunk_inputs(qraw, fpre, lb):
    sig = _sigmoid(fpre)
    f = lb + (1.0 - lb) * sig
    g = jnp.log(f)
    q, dq_fac = _silu_and_grad(qraw)
    return q, dq_fac, f, sig, g


def _hg_level_terms(q, k, b, v_idx, m, mask_ref):
    mid = _hg_mid(b, m)
    eq = jnp.exp(jnp.minimum(b - mid, 0.0))
    ek = jnp.exp(jnp.minimum(mid - b, 0.0))
    qt = (q * eq).astype(BF16)
    kt = (k * ek).astype(BF16)
    return qt, kt, eq, ek, mask_ref[v_idx]


def _hg_scores(q, k, b, mask_ref):
    sc = None
    for v_idx, m in enumerate(HG_LEVELS):
        qt, kt, _, _, msk = _hg_level_terms(q, k, b, v_idx, m, mask_ref)
        term = _dot_nt(qt, kt) * msk
        sc = term if sc is None else sc + term
    return sc


def _hgrn_fwd(proj, lb, name):
    s_len = proj.shape[0]
    nc = s_len // BLK
    nh = WIDTH // HG_HEAD_DIM
    base = 4 * WIDTH // BLK

    def body(q_ref, f_ref, i_ref, lb_ref, o_ref, mask_ref):
        _hg_masks(mask_ref)
        row = _iota2((BLK, BLK), 0)
        col = _iota2((BLK, BLK), 1)
        lower_incl = (col <= row).astype(BF16)
        lb_v = lb_ref[...]

        def chunk(ci, st):
            r0 = pl.multiple_of(ci * BLK, BLK)
            q, _, f, _, g = _hg_chunk_inputs(q_ref[pl.ds(r0, BLK), :], f_ref[pl.ds(r0, BLK), :], lb_v)
            k = 1.0 - f
            v = i_ref[pl.ds(r0, BLK), :]
            vb = v.astype(BF16)
            b = _dot_exact_l(lower_incl, g)
            b_end = b[BLK - 1:BLK, :]
            inter = _dot_nt((q * jnp.exp(b)).astype(BF16), st.astype(BF16))
            sc = _hg_scores(q, k, b, mask_ref)
            diag = jnp.sum(q * k, axis=-1, keepdims=True)
            o_ref[pl.ds(r0, BLK), :] = inter + _dot(sc.astype(BF16), vb) + diag * v
            k_dec = (k * jnp.exp(b_end - b)).astype(BF16)
            return st * jnp.exp(b_end) + _dot_tn(vb, k_dec)

        lax.fori_loop(0, nc, chunk, jnp.zeros((HG_HEAD_DIM, HG_HEAD_DIM), F32))

    col_spec = lambda off: pl.BlockSpec((s_len, BLK), lambda h: (0, off + h))
    return _pcall(
        body, name=name, out_shape=jax.ShapeDtypeStruct((s_len, WIDTH), F32),
        grid=(nh,),
        in_specs=[col_spec(base), col_spec(base + nh), col_spec(base + 2 * nh),
                  pl.BlockSpec((1, BLK), lambda h: (0, h))],
        out_specs=pl.BlockSpec((s_len, BLK), lambda h: (0, h)),
        scratch_shapes=[pltpu.VMEM((len(HG_LEVELS), BLK, BLK), F32)],
        semantics=("arbitrary",))(proj, proj, proj, lb)


def _rms_heads(o_b, norm_w):
    n_parts, h_parts, r_parts = [], [], []
    for h in range(WIDTH // HG_HEAD_DIM):
        sl = slice(h * HG_HEAD_DIM, (h + 1) * HG_HEAD_DIM)
        o = o_b[:, sl]
        rstd = lax.rsqrt(jnp.mean(o * o, axis=-1, keepdims=True) + RMS_EPS)
        ohat = o * rstd
        h_parts.append(ohat)
        n_parts.append(ohat * norm_w[:, sl])
        r_parts.append(jnp.broadcast_to(rstd, o.shape))
    cat = lambda parts: jnp.concatenate(parts, axis=-1)
    return cat(n_parts), cat(h_parts), cat(r_parts)


def _shift_rows_down(halo, cur, k):
    tm = cur.shape[0]
    ext = jnp.concatenate([halo, cur], axis=0)
    return pltpu.roll(ext, k, 0)[8:8 + tm]


def _shift_rows_up(cur, halo, k):
    tm = cur.shape[0]
    ext = jnp.concatenate([cur, halo], axis=0)
    return pltpu.roll(ext, (tm + 8 - k) % (tm + 8), 0)[0:tm]


def _merge_fwd(x, proj, o_a, o_b, gate, norm_w, conv_w, wb, w_out, ln_g, ln_b, name):
    s_len, d = x.shape
    tm = min(256, s_len)
    hb = tm // 8

    def body(x_ref, oa_ref, za_ref, ob_ref, zb_ref, pre_ref, post_ref, u_ref, zc_ref, hpre_ref, hu_ref, g_ref,
             gate_ref, nw_ref, cw_ref, wb_ref, wo_ref, lg_ref, lbias_ref, xn_ref, mg_ref, yc_ref):
        i = pl.program_id(0)
        sa, _ = _silu_and_grad(za_ref[...])
        y_a = (oa_ref[...] * sa).astype(BF16)
        n_b, _, _ = _rms_heads(ob_ref[...], nw_ref[...])
        sb, _ = _silu_and_grad(zb_ref[...])
        y_b = (n_b * sb).astype(BF16)
        a = pre_ref[...] * u_ref[...]
        halo = jnp.where(i > 0, hpre_ref[...] * hu_ref[...], 0.0)
        cw = cw_ref[...]
        conv = cw[0:1] * _shift_rows_down(halo, a, 2) + cw[1:2] * _shift_rows_down(halo, a, 1) + cw[2:3] * a
        sc, _ = _silu_and_grad(zc_ref[...])
        y_c = (post_ref[...] * conv * sc).astype(BF16)
        merged = None
        for k, yk in enumerate((y_a, y_b, y_c)):
            yc_ref[:, k * WIDTH:(k + 1) * WIDTH] = yk
            term = _sigmoid(g_ref[:, k * d:(k + 1) * d]) * _dot(yk, wb_ref[k])
            merged = term if merged is None else merged + term
        mb = merged.astype(BF16)
        mg_ref[...] = mb
        y = _dot(mb, wo_ref[...])
        r = ALPHA * x_ref[...] + (1.0 + gate_ref[...]) * y
        rhat, _ = _standardize(r)
        xn_ref[...] = rhat * lg_ref[...] + lbias_ref[...]

    wcol = lambda cb: pl.BlockSpec((tm, WIDTH), lambda i: (i, cb))
    halo_spec = lambda cb: pl.BlockSpec((8, WIDTH), lambda i: (jnp.maximum(i * hb - 1, 0), cb))
    vec = lambda w: pl.BlockSpec((1, w), lambda i: (0, 0))
    return _pcall(
        body, name=name,
        out_shape=(jax.ShapeDtypeStruct((s_len, d), F32), jax.ShapeDtypeStruct((s_len, d), BF16),
                   jax.ShapeDtypeStruct((s_len, 3 * WIDTH), BF16)),
        grid=(s_len // tm,),
        in_specs=[pl.BlockSpec((tm, d), lambda i: (i, 0)),
                  wcol(0), wcol(3), wcol(0), wcol(7), wcol(8), wcol(9), wcol(10), wcol(11),
                  halo_spec(8), halo_spec(10),
                  pl.BlockSpec((tm, 3 * d), lambda i: (i, 2)),
                  vec(d), vec(WIDTH),
                  pl.BlockSpec((3, WIDTH), lambda i: (0, 0)),
                  pl.BlockSpec((3, WIDTH, d), lambda i: (0, 0, 0)),
                  pl.BlockSpec((d, d), lambda i: (0, 0)),
                  vec(d), vec(d)],
        out_specs=(pl.BlockSpec((tm, d), lambda i: (i, 0)), pl.BlockSpec((tm, d), lambda i: (i, 0)),
                   pl.BlockSpec((tm, 3 * WIDTH), lambda i: (i, 0))),
        semantics=("arbitrary",))(x, o_a, proj, o_b, proj, proj, proj, proj, proj, proj, proj, proj,
                                  gate, norm_w, conv_w, wb, w_out, ln_g, ln_b)


def _loss_fwd_bwd(y, target):
    s_len, d = y.shape
    tm = min(512, s_len)

    def body(y_ref, t_ref, loss_ref, dy_ref):
        @pl.when(pl.program_id(0) == 0)
        def _():
            loss_ref[...] = jnp.zeros_like(loss_ref)

        e = y_ref[...] - t_ref[...]
        dy_ref[...] = e * (1.0 / d)
        part = jnp.sum(jnp.sum(e * e, axis=-1, keepdims=True), axis=0, keepdims=True)
        loss_ref[...] += part * (0.5 / d)

    tile = pl.BlockSpec((tm, d), lambda i: (i, 0))
    return _pcall(body, name="loss", grid=(s_len // tm,),
                  out_shape=(jax.ShapeDtypeStruct((1, 1), F32), jax.ShapeDtypeStruct((s_len, d), F32)),
                  in_specs=[tile, tile],
                  out_specs=(pl.BlockSpec((1, 1), lambda i: (0, 0)), tile),
                  semantics=("arbitrary",))(y, target)


def _merge_bwd(dxn, x, merged, ycat, proj, gate, wb, w_out, ln_g, name):
    s_len, d = x.shape
    tm = min(256, s_len)

    def body(dxn_ref, x_ref, mg_ref, yc_ref, g_ref, gate_ref, wb_ref, wo_ref, lg_ref,
             dres_ref, dyc_ref, dg_ref, gwo_ref, gwb_ref, vec_ref):
        @pl.when(pl.program_id(0) == 0)
        def _():
            gwo_ref[...] = jnp.zeros_like(gwo_ref)
            gwb_ref[...] = jnp.zeros_like(gwb_ref)
            vec_ref[...] = jnp.zeros_like(vec_ref)

        mb = mg_ref[...]
        one_gate = 1.0 + gate_ref[...]
        y = _dot(mb, wo_ref[...])
        r = ALPHA * x_ref[...] + one_gate * y
        rhat, rstd = _standardize(r)
        dxn = dxn_ref[...]
        dr = _standardize_bwd(rhat, rstd, dxn * lg_ref[...])
        vec_ref[0:1, :] += jnp.sum(dxn * rhat, axis=0, keepdims=True)
        vec_ref[1:2, :] += jnp.sum(dxn, axis=0, keepdims=True)
        vec_ref[2:3, :] += jnp.sum(dr * y, axis=0, keepdims=True)
        dres_ref[...] = ALPHA * dr
        dy = (one_gate * dr).astype(BF16)
        gwo_ref[...] += _dot_tn(mb, dy)
        dmerged = _dot_nt(dy, wo_ref[...])
        for k in range(3):
            yk = yc_ref[:, k * WIDTH:(k + 1) * WIDTH]
            sg = _sigmoid(g_ref[:, k * d:(k + 1) * d])
            pk = _dot(yk, wb_ref[k])
            dg_ref[:, k * d:(k + 1) * d] = (dmerged * pk * sg * (1.0 - sg)).astype(BF16)
            dpk = (dmerged * sg).astype(BF16)
            dyc_ref[:, k * WIDTH:(k + 1) * WIDTH] = _dot_nt(dpk, wb_ref[k])
            gwb_ref[k] += _dot_tn(yk, dpk)

    tile = lambda w: pl.BlockSpec((tm, w), lambda i: (i, 0))
    vec = pl.BlockSpec((1, d), lambda i: (0, 0))
    return _pcall(
        body, name=name,
        out_shape=(jax.ShapeDtypeStruct((s_len, d), F32), jax.ShapeDtypeStruct((s_len, 3 * WIDTH), F32),
                   jax.ShapeDtypeStruct(proj.shape, BF16), jax.ShapeDtypeStruct((d, d), F32),
                   jax.ShapeDtypeStruct((3, WIDTH, d), F32), jax.ShapeDtypeStruct((8, d), F32)),
        grid=(s_len // tm,),
        in_specs=[tile(d), tile(d), tile(d), tile(3 * WIDTH),
                  pl.BlockSpec((tm, 3 * d), lambda i: (i, 2)),
                  vec, pl.BlockSpec((3, WIDTH, d), lambda i: (0, 0, 0)),
                  pl.BlockSpec((d, d), lambda i: (0, 0)), vec],
        out_specs=(tile(d), tile(3 * WIDTH), pl.BlockSpec((tm, 3 * d), lambda i: (i, 2)),
                   pl.BlockSpec((d, d), lambda i: (0, 0)),
                   pl.BlockSpec((3, WIDTH, d), lambda i: (0, 0, 0)),
                   pl.BlockSpec((8, d), lambda i: (0, 0))),
        semantics=("arbitrary",))(dxn, x, merged, ycat, proj, gate, wb, w_out, ln_g)


def _branch_bwd(dycat, proj, o_a, o_b, norm_w, conv_w, dproj, name):
    s_len = proj.shape[0]
    tm = min(256, s_len)
    hb = tm // 8
    n_tiles = s_len // tm

    def body(dya_ref, dyb_ref, dyc_ref, oa_ref, za_ref, ob_ref, zb_ref, pre_ref, post_ref, u_ref, zc_ref,
             hpre_ref, hu_ref, ndyc_ref, npost_ref, nzc_ref, nw_ref, cw_ref, dproj_in,
             dproj_ref, doa_ref, dob_ref, vec_ref, dza_scr, dzb_scr, dc_scr, sems):
        del dproj_in
        i = pl.program_id(0)

        @pl.when(i == 0)
        def _():
            vec_ref[...] = jnp.zeros_like(vec_ref)

        sa, dsa = _silu_and_grad(za_ref[...])
        dya = dya_ref[...]
        doa_ref[...] = dya * sa
        dza_scr[...] = (dya * oa_ref[...] * dsa).astype(BF16)
        nw = nw_ref[...]
        n_b, ohat, rstd = _rms_heads(ob_ref[...], nw)
        sb, dsb = _silu_and_grad(zb_ref[...])
        dyb = dyb_ref[...]
        dzb_scr[...] = (dyb * n_b * dsb).astype(BF16)
        dn = dyb * sb
        vec_ref[0:1, :] += jnp.sum(dn * ohat, axis=0, keepdims=True)
        dnw = dn * nw
        parts = []
        for h in range(WIDTH // HG_HEAD_DIM):
            sl = slice(h * HG_HEAD_DIM, (h + 1) * HG_HEAD_DIM)
            m2 = jnp.mean(dnw[:, sl] * ohat[:, sl], axis=-1, keepdims=True)
            parts.append(rstd[:, sl] * (dnw[:, sl] - ohat[:, sl] * m2))
        dob_ref[...] = jnp.concatenate(parts, axis=-1)
        cw = cw_ref[...]
        pre, u, post = pre_ref[...], u_ref[...], post_ref[...]
        a = pre * u
        halo = jnp.where(i > 0, hpre_ref[...] * hu_ref[...], 0.0)
        a1 = _shift_rows_down(halo, a, 1)
        a2 = _shift_rows_down(halo, a, 2)
        conv = cw[0:1] * a2 + cw[1:2] * a1 + cw[2:3] * a
        sc, dsc = _silu_and_grad(zc_ref[...])
        dyc = dyc_ref[...]
        dconv = dyc * post * sc
        nsc, _ = _silu_and_grad(nzc_ref[...])
        nxt = jnp.where(i < n_tiles - 1, ndyc_ref[...] * npost_ref[...] * nsc, 0.0)
        da = cw[2:3] * dconv + cw[1:2] * _shift_rows_up(dconv, nxt, 1) + cw[0:1] * _shift_rows_up(dconv, nxt, 2)
        dc_scr[:, 0 * WIDTH:1 * WIDTH] = (da * u).astype(BF16)
        dc_scr[:, 1 * WIDTH:2 * WIDTH] = (dyc * conv * sc).astype(BF16)
        dc_scr[:, 2 * WIDTH:3 * WIDTH] = (da * pre).astype(BF16)
        dc_scr[:, 3 * WIDTH:4 * WIDTH] = (dyc * post * conv * dsc).astype(BF16)
        vec_ref[1:2, :] += jnp.sum(dconv * a2, axis=0, keepdims=True)
        vec_ref[2:3, :] += jnp.sum(dconv * a1, axis=0, keepdims=True)
        vec_ref[3:4, :] += jnp.sum(dconv * a, axis=0, keepdims=True)
        rows = pl.ds(pl.multiple_of(i * tm, tm), tm)
        copies = [pltpu.make_async_copy(dza_scr, dproj_ref.at[rows, 3 * WIDTH:4 * WIDTH], sems.at[0]),
                  pltpu.make_async_copy(dzb_scr, dproj_ref.at[rows, 7 * WIDTH:8 * WIDTH], sems.at[1]),
                  pltpu.make_async_copy(dc_scr, dproj_ref.at[rows, 8 * WIDTH:12 * WIDTH], sems.at[2])]
        for cp in copies:
            cp.start()
        for cp in copies:
            cp.wait()

    wcol = lambda cb: pl.BlockSpec((tm, WIDTH), lambda i: (i, cb))
    prev = lambda cb: pl.BlockSpec((8, WIDTH), lambda i: (jnp.maximum(i * hb - 1, 0), cb))
    nxt = lambda cb: pl.BlockSpec((8, WIDTH), lambda i: (jnp.minimum((i + 1) * hb, s_len // 8 - 1), cb))
    anyspec = pl.BlockSpec(memory_space=pl.ANY)
    out = jax.ShapeDtypeStruct((s_len, WIDTH), F32)
    return _pcall(
        body, name=name,
        out_shape=(jax.ShapeDtypeStruct(dproj.shape, dproj.dtype), out, out, jax.ShapeDtypeStruct((8, WIDTH), F32)),
        grid=(n_tiles,),
        in_specs=[wcol(0), wcol(1), wcol(2), wcol(0), wcol(3), wcol(0), wcol(7), wcol(8), wcol(9), wcol(10), wcol(11),
                  prev(8), prev(10), nxt(2), nxt(9), nxt(11),
                  pl.BlockSpec((1, WIDTH), lambda i: (0, 0)), pl.BlockSpec((3, WIDTH), lambda i: (0, 0)), anyspec],
        out_specs=(anyspec, wcol(0), wcol(0), pl.BlockSpec((8, WIDTH), lambda i: (0, 0))),
        scratch_shapes=[pltpu.VMEM((tm, WIDTH), BF16), pltpu.VMEM((tm, WIDTH), BF16),
                        pltpu.VMEM((tm, 4 * WIDTH), BF16), pltpu.SemaphoreType.DMA((3,))],
        aliases={18: 0},
        semantics=("arbitrary",))(dycat, dycat, dycat, o_a, proj, o_b, proj, proj, proj, proj, proj,
                                  proj, proj, dycat, proj, proj, norm_w, conv_w, dproj)


def _sb_bwd(proj, do_a, totals, dproj, name):
    s_len = proj.shape[0]
    nb = s_len // BLK
    n_pairs = WIDTH // BLK
    scale = SB_HEAD_DIM ** -0.5
    gb = _sb_group_blocks(nb)
    kw = gb * BLK

    def body(q_ref, k_ref, v_ref, do_ref, tot_ref, dproj_in, dproj_ref, dq_ref, dk_ref, dv_ref, out_scr, sems):
        del dproj_in
        lane = _iota2((1, BLK), 1)
        row = _iota2((BLK, BLK), 0)
        col = _iota2((BLK, BLK), 1)
        ones = jnp.ones((BLK, BLK), BF16)
        twice = lambda m: jnp.concatenate([m, m], axis=0)
        before_and_sum = twice(jnp.concatenate([(row < col).astype(BF16), ones], axis=1))
        upto_and_sum = twice(jnp.concatenate([(row <= col).astype(BF16), ones], axis=1))
        qpos = _iota2((BLK, kw), 0)
        kpos = _iota2((BLK, kw), 1)
        head_lanes = [(lane // SB_HEAD_DIM) == hh for hh in range(2)]
        dk_ref[...] = jnp.zeros_like(dk_ref)
        dv_ref[...] = jnp.zeros_like(dv_ref)

        causal = kpos - qpos

        def scores(i, gi, qms):
            c0 = pl.multiple_of(gi * kw, kw)
            kb = k_ref[pl.ds(c0, kw), :].astype(BF16)
            valid = causal < i * BLK - c0
            return tuple(jnp.where(valid, _dot_nt(qms[hh], kb), MASKED_SCORE) for hh in range(2))

        def process(gi, z2s, qms, doms, totals_i, carry):
            c0 = pl.multiple_of(gi * kw, kw)
            kf = k_ref[pl.ds(c0, kw), :]
            vf = v_ref[pl.ds(c0, kw), :]
            kms = [jnp.where(head_lanes[hh], kf, 0.0).astype(BF16) for hh in range(2)]
            vms = [jnp.where(head_lanes[hh], vf, 0.0).astype(BF16) for hh in range(2)]
            das = [_dot_nt(doms[hh], vms[hh]) for hh in range(2)]
            halves = [_softplus2_parts(z2) for z2 in z2s]
            terms = [[_split2_lanes(sp2[:, b * BLK:(b + 1) * BLK]) for b in range(gb)] for sp2, _ in halves]
            sums = [[_dot(t, before_and_sum) for t in head_terms] for head_terms in terms]
            weights, gmats, l_befores = [], [], []
            for hh in range(2):
                l_before = carry[3 * hh + 1]
                parts = []
                for b in range(gb):
                    parts.append(totals_i[hh] - l_before - sums[hh][b][:, :BLK])
                    l_before = l_before + sums[hh][b][:, BLK:]
                a = jnp.exp2(z2s[hh] - jnp.concatenate(parts, axis=1))
                weights.append(a.astype(BF16))
                gmats.append(a * das[hh])
                l_befores.append(l_before)
            terms = [[_split2_lanes(g[:, b * BLK:(b + 1) * BLK]) for b in range(gb)] for g in gmats]
            sums = [[_dot(t, upto_and_sum) for t in head_terms] for head_terms in terms]
            dzs, g_befores = [], []
            for hh in range(2):
                g_before = carry[3 * hh + 2]
                parts = []
                for b in range(gb):
                    parts.append(g_before + sums[hh][b][:, :BLK])
                    g_before = g_before + sums[hh][b][:, BLK:]
                dzs.append((gmats[hh] - halves[hh][1] * jnp.concatenate(parts, axis=1)).astype(BF16))
                g_befores.append(g_before)
            dks = [_dot_tn(dzs[hh], qms[hh]) for hh in range(2)]
            dvs = [_dot_tn(weights[hh], doms[hh]) for hh in range(2)]
            dqs = [_dot(dzs[hh], kms[hh]) for hh in range(2)]
            dk_ref[pl.ds(c0, kw), :] += (dks[0] + dks[1]) * (1.0 / LOG2E)
            dv_ref[pl.ds(c0, kw), :] += dvs[0] + dvs[1]
            return (carry[0] + dqs[0], l_befores[0], g_befores[0], carry[3] + dqs[1], l_befores[1], g_befores[1])

        def queries(i):
            qf = q_ref[pl.ds(pl.multiple_of(i * BLK, BLK), BLK), :] * (scale * LOG2E)
            return [jnp.where(head_lanes[hh], qf, 0.0).astype(BF16) for hh in range(2)]

        def qblock(i, first_scores):
            r0 = pl.multiple_of(i * BLK, BLK)
            qms = queries(i)
            dof = do_ref[pl.ds(r0, BLK), :]
            doms = [jnp.where(head_lanes[hh], dof, 0.0).astype(BF16) for hh in range(2)]
            totals_i = [tot_ref[hh, pl.ds(r0, BLK), :] for hh in range(2)]
            zero = jnp.zeros((BLK, BLK), F32)
            last = i // gb

            def step(gi, state):
                return scores(i, gi + 1, qms) + process(gi, state[:2], qms, doms, totals_i, state[2:])

            state = lax.fori_loop(0, last, step, first_scores + (zero,) * 6)
            nxt = jnp.minimum(i + 1, nb - 1)
            next_scores = scores(nxt, 0, queries(nxt))
            carry = process(last, state[:2], qms, doms, totals_i, state[2:])
            dq_ref[pl.ds(r0, BLK), :] = (carry[0] + carry[3]) * scale
            return next_scores

        lax.fori_loop(0, nb, qblock, scores(0, 0, queries(0)))
        pair = pl.program_id(0)
        copies = []
        for t, ref in enumerate((dq_ref, dk_ref, dv_ref)):
            out_scr[t] = ref[...].astype(BF16)
            col = pl.multiple_of((t * n_pairs + pair) * BLK, BLK)
            copies.append(pltpu.make_async_copy(out_scr.at[t], dproj_ref.at[:, pl.ds(col, BLK)], sems.at[t]))
            copies[-1].start()
        for cp in copies:
            cp.wait()

    col_spec = lambda off: pl.BlockSpec((s_len, BLK), lambda p: (0, off + p))
    anyspec = pl.BlockSpec(memory_space=pl.ANY)
    return _pcall(
        body, name=name, out_shape=jax.ShapeDtypeStruct(dproj.shape, dproj.dtype), grid=(n_pairs,),
        in_specs=[col_spec(0), col_spec(n_pairs), col_spec(2 * n_pairs), col_spec(0),
                  pl.BlockSpec((2, s_len, BLK), lambda p: (p, 0, 0)), anyspec],
        out_specs=anyspec,
        scratch_shapes=[pltpu.VMEM((s_len, BLK), F32)] * 3 + [pltpu.VMEM((3, s_len, BLK), BF16),
                                                              pltpu.SemaphoreType.DMA((3,))],
        aliases={5: 0},
        semantics=("arbitrary",))(proj, proj, proj, do_a, totals, dproj)


def _hgrn_bwd(proj, do_b, lb, dproj, name):
    s_len = proj.shape[0]
    nc = s_len // BLK
    nh = WIDTH // HG_HEAD_DIM
    base = 4 * WIDTH // BLK

    def body(q_ref, f_ref, i_ref, do_ref, lb_ref, dproj_in, dproj_ref, dlb_ref, mask_ref, st_ref, out_scr, sems):
        del dproj_in
        _hg_masks(mask_ref)
        row = _iota2((BLK, BLK), 0)
        col = _iota2((BLK, BLK), 1)
        lower_incl = (col <= row).astype(BF16)
        upper_incl = (col >= row).astype(BF16)
        lb_v = lb_ref[...]

        def load(ci):
            r0 = pl.multiple_of(ci * BLK, BLK)
            q, dq_fac, f, sig, g = _hg_chunk_inputs(q_ref[pl.ds(r0, BLK), :], f_ref[pl.ds(r0, BLK), :], lb_v)
            b = _dot_exact_l(lower_incl, g)
            return r0, q, dq_fac, f, sig, b, i_ref[pl.ds(r0, BLK), :]

        def fwd_chunk(ci, st):
            st_ref[ci] = st
            _, _, _, f, _, b, v = load(ci)
            b_end = b[BLK - 1:BLK, :]
            k_dec = ((1.0 - f) * jnp.exp(b_end - b)).astype(BF16)
            return st * jnp.exp(b_end) + _dot_tn(v.astype(BF16), k_dec)

        lax.fori_loop(0, nc, fwd_chunk, jnp.zeros((HG_HEAD_DIM, HG_HEAD_DIM), F32))

        def bwd_chunk(cc, carry):
            dst, suffix, dlb = carry
            ci = nc - 1 - cc
            r0, q, dq_fac, f, sig, b, v = load(ci)
            k = 1.0 - f
            vb = v.astype(BF16)
            do = do_ref[pl.ds(r0, BLK), :]
            dob = do.astype(BF16)
            b_end = b[BLK - 1:BLK, :]
            e_q = jnp.exp(b)
            e_k = jnp.exp(b_end - b)
            qe = (q * e_q).astype(BF16)
            kh = (k * e_k).astype(BF16)
            st1, st2 = _split2(st_ref[ci])
            ds1, ds2 = _split2(dst)
            dqe = _dot(dob, st1) + _dot(dob, st2)
            dkh = _dot(vb, ds1) + _dot(vb, ds2)
            dq = e_q * dqe
            dk = e_k * dkh
            dv = _dot_nt(kh, ds1)
            dst_new = dst * jnp.exp(b_end) + _dot_tn(dob, qe)
            dlog = qe.astype(F32) * dqe - kh.astype(F32) * dkh
            da = _dot_nt(dob, vb)
            sc = None
            for v_idx, m in enumerate(HG_LEVELS):
                qm, km, eq, ek, msk = _hg_level_terms(q, k, b, v_idx, m, mask_ref)
                term = _dot_nt(qm, km) * msk
                sc = term if sc is None else sc + term
                pm = (da * msk).astype(BF16)
                dqm = _dot(pm, km)
                dkm = _dot_tn(pm, qm)
                dq = dq + dqm * eq
                dk = dk + dkm * ek
                dlog = dlog + (qm.astype(F32) * dqm - km.astype(F32) * dkm)
            a_diag = jnp.sum(do * v, axis=-1, keepdims=True)
            s_diag = jnp.sum(q * k, axis=-1, keepdims=True)
            dq = dq + a_diag * k
            dk = dk + a_diag * q
            dv = dv + _dot_tn(sc.astype(BF16), dob) + s_diag * do
            dg = _dot_exact_l(upper_incl, dlog) + suffix
            dfull = dg / f - dk
            out_scr[0, pl.ds(r0, BLK), :] = (dq * dq_fac).astype(BF16)
            out_scr[1, pl.ds(r0, BLK), :] = (dfull * (1.0 - lb_v) * sig * (1.0 - sig)).astype(BF16)
            out_scr[2, pl.ds(r0, BLK), :] = dv.astype(BF16)
            dlb = dlb + jnp.sum(dfull * (1.0 - sig), axis=0, keepdims=True)
            return dst_new, dg[0:1, :], dlb

        zero_row = jnp.zeros((1, BLK), F32)
        _, _, dlb = lax.fori_loop(0, nc, bwd_chunk,
                                  (jnp.zeros((HG_HEAD_DIM, HG_HEAD_DIM), F32), zero_row, zero_row))
        dlb_ref[...] = jnp.broadcast_to(dlb, dlb_ref.shape)
        head = pl.program_id(0)
        copies = []
        for t in range(3):
            col = pl.multiple_of((base + t * nh + head) * BLK, BLK)
            copies.append(pltpu.make_async_copy(out_scr.at[t], dproj_ref.at[:, pl.ds(col, BLK)], sems.at[t]))
            copies[-1].start()
        for cp in copies:
            cp.wait()

    col_spec = lambda off: pl.BlockSpec((s_len, BLK), lambda h: (0, off + h))
    anyspec = pl.BlockSpec(memory_space=pl.ANY)
    return _pcall(
        body, name=name,
        out_shape=(jax.ShapeDtypeStruct(dproj.shape, dproj.dtype), jax.ShapeDtypeStruct((8, WIDTH), F32)),
        grid=(nh,),
        in_specs=[col_spec(base), col_spec(base + nh), col_spec(base + 2 * nh), col_spec(0),
                  pl.BlockSpec((1, BLK), lambda h: (0, h)), anyspec],
        out_specs=(anyspec, pl.BlockSpec((8, BLK), lambda h: (0, h))),
        scratch_shapes=[pltpu.VMEM((len(HG_LEVELS), BLK, BLK), F32),
                        pltpu.VMEM((nc, HG_HEAD_DIM, HG_HEAD_DIM), F32),
                        pltpu.VMEM((3, s_len, BLK), BF16), pltpu.SemaphoreType.DMA((3,))],
        aliases={5: 0},
        semantics=("arbitrary",))(proj, proj, proj, do_b, lb, dproj)


def _dh_matmul(dproj, w_full, name):
    s_len, n = dproj.shape
    d = w_full.shape[0]
    tm = min(512, s_len)
    tk = 1536

    def body(dp_ref, w_ref, dh_ref):
        part = _dot_nt(dp_ref[...], w_ref[...])

        @pl.when(pl.program_id(1) == 0)
        def _():
            dh_ref[...] = part

        @pl.when(pl.program_id(1) > 0)
        def _():
            dh_ref[...] += part

    return _pcall(
        body, name=name, out_shape=jax.ShapeDtypeStruct((s_len, d), F32),
        grid=(s_len // tm, n // tk),
        in_specs=[pl.BlockSpec((tm, tk), lambda i, k: (i, k)), pl.BlockSpec((d, tk), lambda i, k: (0, k))],
        out_specs=pl.BlockSpec((tm, d), lambda i, k: (i, 0)),
        semantics=("arbitrary", "arbitrary"))(dproj, w_full)


def _gw_matmul(h_t, dproj, name):
    d, s_len = h_t.shape
    n = dproj.shape[1]
    tn = 1152

    def body(ht_ref, dp_ref, gw_ref):
        gw_ref[...] = _dot(ht_ref[...], dp_ref[...]).astype(BF16)

    return _pcall(
        body, name=name, out_shape=jax.ShapeDtypeStruct((d, n), BF16),
        grid=(n // tn,),
        in_specs=[pl.BlockSpec((d, s_len), lambda j: (0, 0)), pl.BlockSpec((s_len, tn), lambda j: (0, j))],
        out_specs=pl.BlockSpec((d, tn), lambda j: (0, j)),
        semantics=("arbitrary",))(h_t, dproj)


def _ln_bwd(dh, x, scale, dres, name):
    s_len, d = x.shape
    tm = min(512, s_len)

    def body(dh_ref, x_ref, sc_ref, dres_ref, dx_ref, vec_ref):
        @pl.when(pl.program_id(0) == 0)
        def _():
            vec_ref[...] = jnp.zeros_like(vec_ref)

        dh = dh_ref[...]
        xs, rstd = _standardize(x_ref[...])
        vec_ref[0:1, :] += jnp.sum(dh, axis=0, keepdims=True)
        vec_ref[1:2, :] += jnp.sum(dh * xs, axis=0, keepdims=True)
        dx_ref[...] = _standardize_bwd(xs, rstd, dh * (1.0 + sc_ref[...])) + dres_ref[...]

    tile = pl.BlockSpec((tm, d), lambda i: (i, 0))
    return _pcall(body, name=name, grid=(s_len // tm,),
                  out_shape=(jax.ShapeDtypeStruct((s_len, d), F32), jax.ShapeDtypeStruct((8, d), F32)),
                  in_specs=[tile, tile, pl.BlockSpec((1, d), lambda i: (0, 0)), tile],
                  out_specs=(tile, pl.BlockSpec((8, d), lambda i: (0, 0))),
                  semantics=("arbitrary",))(dh, x, scale, dres)


def _wmod_grad(c_t, dmod):
    d = c_t.shape[0]
    n_layers, _, cm = dmod.shape

    def body(c_ref, dm_ref, o_ref):
        for l in range(n_layers):
            acc = None
            for b in range(NDEV):
                term = c_ref[:, b:b + 1] * dm_ref[l, b:b + 1, :]
                acc = term if acc is None else acc + term
            o_ref[l] = acc

    return _pcall(body, name="wmod_grad", out_shape=jax.ShapeDtypeStruct((n_layers, d, cm), F32))(c_t, dmod)


def _sum_adamw(parts_list, w, m, v, name):
    n_ranges = len(parts_list)
    n_src, range_rows, cols = parts_list[0].shape
    rows = range_rows * n_ranges
    tr = range_rows
    for cand in (512, 256, 128, 64, 32, 16, 8):
        if range_rows % cand == 0 and cand * cols * 4 <= (2 << 20):
            tr = cand
            break
    tiles = range_rows // tr

    def body(*refs):
        p_refs = refs[:n_ranges]
        w_ref, m_ref, v_ref, g_ref, d_ref, nm_ref, nv_ref = refs[n_ranges:]

        def step(p_ref):
            g = p_ref[0].astype(F32)
            for s in range(1, n_src):
                g = g + p_ref[s].astype(F32)
            nm = ADAM_B1 * m_ref[...] + (1.0 - ADAM_B1) * g
            nv = ADAM_B2 * v_ref[...] + (1.0 - ADAM_B2) * (g * g)
            m_hat = nm / (1.0 - ADAM_B1 ** ADAM_STEP)
            v_hat = nv / (1.0 - ADAM_B2 ** ADAM_STEP)
            g_ref[...] = g
            d_ref[...] = -ADAM_LR * (m_hat / (jnp.sqrt(v_hat) + ADAM_EPS) + ADAM_WD * w_ref[...])
            nm_ref[...] = nm
            nv_ref[...] = nv

        if n_ranges == 1:
            step(p_refs[0])
        else:
            for j in range(n_ranges):
                @pl.when(pl.program_id(0) // tiles == j)
                def _(j=j):
                    step(p_refs[j])

    def part_spec(j):
        return pl.BlockSpec((n_src, tr, cols), lambda i: (0, jnp.clip(i - j * tiles, 0, tiles - 1), 0))

    tile = pl.BlockSpec((tr, cols), lambda i: (i, 0))
    out = jax.ShapeDtypeStruct((rows, cols), F32)
    return _pcall(body, name=name, grid=(rows // tr,), out_shape=(out,) * 4,
                  in_specs=[part_spec(j) for j in range(n_ranges)] + [tile, tile, tile],
                  out_specs=(tile,) * 4, semantics=("arbitrary",))(*parts_list, w, m, v)


def _sum_parts(parts, name):
    n_src = parts.shape[0]

    def body(p_ref, o_ref):
        acc = p_ref[0]
        for s in range(1, n_src):
            acc = acc + p_ref[s]
        o_ref[...] = acc

    return _pcall(body, name=name, out_shape=jax.ShapeDtypeStruct(parts.shape[1:], F32))(parts)


def _lower_bound_table(lower_bounds):
    p = jax.nn.softmax(lower_bounds.astype(F32), axis=0)
    return jnp.cumsum(p, axis=0) - p[0:1]


def _pad_rows(v, width):
    n = v.shape[0]
    rows = -(-n // width)
    rows = -(-rows // 8) * 8
    return jnp.pad(v, (0, rows * width - n)).reshape(rows, width)


def kernel(x, c, w_mod, b_mod, w_in, conv_w, hgrn_norm_w, lower_bounds, w_branch, w_out, ln_g, ln_b, loss_target, m_w_mod, m_b_mod, m_w_in, m_conv_w, m_hgrn_norm_w, m_lower_bounds, m_w_branch, m_w_out, m_ln_g, m_ln_b, v_w_mod, v_b_mod, v_w_in, v_conv_w, v_hgrn_norm_w, v_lower_bounds, v_w_branch, v_w_out, v_ln_g, v_ln_b):
    n_layers = N_LAYERS
    s_len, d = x.shape[1], x.shape[2]
    n_cols = w_in.shape[2] * NDEV
    cw_cols = conv_w.shape[2]
    cm = w_mod.shape[2]
    me = _my_index()
    x0 = x[0]
    target = loss_target[0]

    small = _pad_rows(jnp.concatenate([c.reshape(-1), conv_w.reshape(-1)]), BLK)
    small_all = _all_gather_small("gather_c_conv", small).reshape(NDEV, -1)
    c_all = small_all[:, :d]
    conv_full = small_all[:, d:d + n_layers * 3 * cw_cols].reshape(NDEV, n_layers, 3, cw_cols)
    conv_full = conv_full.transpose(1, 2, 0, 3).reshape(n_layers, 3, WIDTH)

    b_mod_mine = lax.dynamic_slice_in_dim(b_mod, me * cm, cm, axis=1).reshape(n_layers, 1, cm)
    mod_cols = _mod_fwd(c_all, w_mod, b_mod_mine)
    mod_all = _all_gather_small("gather_mod", mod_cols.reshape(n_layers * NDEV, cm))
    mod_all = mod_all.reshape(NDEV, n_layers, NDEV, cm)
    mod_mine = lax.dynamic_index_in_dim(mod_all, me, axis=2, keepdims=False)
    mod_mine = mod_mine.transpose(1, 0, 2).reshape(n_layers, 3, 1, d)

    shard = w_in.shape[2]
    dsh = d // NDEV
    w_in_b, w_branch_b, w_out_b = w_in.astype(BF16), w_branch.astype(BF16), w_out.astype(BF16)
    window = lambda ref, dev: ref.at[:, pl.ds(pl.multiple_of(dev * shard, BLK), shard)]
    gather_transfers = [(0, 0, _whole, window), (1, 1, _whole, _slot), (2, 2, _whole, _slot)]
    gathered_shapes = [((d, n_cols), BF16), ((NDEV, 3, WIDTH, dsh), BF16), ((NDEV, dsh, d), BF16)]
    weights = [None] * n_layers
    weights[0] = _exchange("gather_weights_0", [w_in_b[0], w_branch_b[0], w_out_b[0]],
                           [jax.ShapeDtypeStruct(s, t) for s, t in gathered_shapes], gather_transfers, in_vmem=False)
    pending = [None] * n_layers
    tie = weights[0][2]
    for l in range(1, n_layers):
        lands = [_place_own((d, n_cols), BF16, w_in_b[l], (0, me * shard)),
                 _place_own((NDEV, 3, WIDTH, dsh), BF16, w_branch_b[l][None], (me, 0, 0, 0)),
                 _place_own((NDEV, dsh, d), BF16, w_out_b[l][None], (me, 0, 0))]
        tie, lands = lax.optimization_barrier((tie, lands))
        pending[l] = _exchange_start(f"gather_weights_{l}_start", [w_in_b[l], w_branch_b[l], w_out_b[l]], lands,
                                     gather_transfers)
        tie = pending[l][4]

    def full_weights(l):
        w_in_l, w_branch_l, w_out_l = weights[l]
        return w_in_l, w_branch_l.transpose(1, 2, 0, 3).reshape(3, WIDTH, d), w_out_l.reshape(d, d)

    lbs = _lower_bound_table(lower_bounds)
    norm_w4 = jnp.tile(hgrn_norm_w, (1, WIDTH // HG_HEAD_DIM))

    saved = []
    xl = x0
    for l in range(n_layers):
        shift, scale, gate = mod_mine[l, 0], mod_mine[l, 1], mod_mine[l, 2]
        if l == 0 and n_layers > 1:
            shift = shift + tie[0, 0]
        if l > 0:
            send_sems, recv_sems, p_ins, p_lands, _ = pending[l]
            weights[l] = _exchange_wait(f"gather_weights_{l}_wait", send_sems, recv_sems, p_ins, p_lands, xl,
                                        gather_transfers)
        w_in_l, wb_l, wo_l = full_weights(l)
        proj, h_t = _ln_proj(xl, shift, scale, w_in_l, f"ln_proj_{l}")
        o_a, totals = _sb_fwd(proj, f"sb_fwd_{l}")
        o_b = _hgrn_fwd(proj, lbs[l:l + 1], f"hgrn_fwd_{l}")
        x_new, merged, ycat = _merge_fwd(xl, proj, o_a, o_b, gate, norm_w4[l:l + 1], conv_full[l],
                                         wb_l, wo_l, ln_g[l:l + 1], ln_b[l:l + 1], f"merge_fwd_{l}")
        saved.append((xl, proj, h_t, o_a, totals, o_b, merged, ycat, w_in_l, wb_l, wo_l))
        xl = x_new

    loss_part, dx = _loss_fwd_bwd(xl, target)
    loss = lax.psum(loss_part[0, 0], ("x", "y", "c"))

    scatter_transfers = [(0, 0, window, _slot), (1, 1, _slot, _slot), (2, 2, _slot, _slot)]
    scattered_shapes = [((NDEV, d, shard), BF16), ((NDEV, 3, WIDTH, dsh), BF16), ((NDEV, dsh, d), BF16)]
    partials = [None] * n_layers
    pending = [None] * n_layers
    small_grads = [None] * n_layers
    dmod = [None] * n_layers
    tie = None
    for l in reversed(range(n_layers)):
        xl, proj, h_t, o_a, totals, o_b, merged, ycat, w_in_l, wb_l, wo_l = saved[l]
        scale, gate = mod_mine[l, 1], mod_mine[l, 2]
        if tie is not None:
            gate = gate + tie[0, 0]
        dres, dycat, dproj, gwo, gwb, mvec = _merge_bwd(dx, xl, merged, ycat, proj, gate, wb_l, wo_l,
                                                        ln_g[l:l + 1], f"merge_bwd_{l}")
        dproj, do_a, do_b, bvec = _branch_bwd(dycat, proj, o_a, o_b, norm_w4[l:l + 1], conv_full[l], dproj,
                                              f"branch_bwd_{l}")
        dproj = _sb_bwd(proj, do_a, totals, dproj, f"sb_bwd_{l}")
        dproj, dlb = _hgrn_bwd(proj, do_b, lbs[l:l + 1], dproj, f"hgrn_bwd_{l}")
        dh = _dh_matmul(dproj, w_in_l, f"dh_matmul_{l}")
        gwi = _gw_matmul(h_t, dproj, f"gw_matmul_{l}")
        dx, lvec = _ln_bwd(dh, xl, scale, dres, f"ln_bwd_{l}")
        dmod[l] = jnp.concatenate([lvec[0], lvec[1], mvec[2]])
        norm_grad = bvec[0].reshape(WIDTH // HG_HEAD_DIM, HG_HEAD_DIM).sum(axis=0)
        small_grads[l] = jnp.concatenate([mvec[0], mvec[1], norm_grad, dlb[0], bvec[1:4].reshape(-1)])
        gwb_by_owner = gwb.astype(BF16).reshape(3, WIDTH, NDEV, dsh).transpose(2, 0, 1, 3)
        gwo_by_owner = gwo.astype(BF16).reshape(NDEV, dsh, d)
        sources = [gwi, gwb_by_owner, gwo_by_owner]
        if l > 0:
            lands = [_place_own((NDEV, d, shard), BF16, lax.dynamic_slice_in_dim(gwi, me * shard, shard, axis=1)[None],
                                (me, 0, 0)),
                     _place_own((NDEV, 3, WIDTH, dsh), BF16, lax.dynamic_slice_in_dim(gwb_by_owner, me, 1, axis=0),
                                (me, 0, 0, 0)),
                     _place_own((NDEV, dsh, d), BF16, lax.dynamic_slice_in_dim(gwo_by_owner, me, 1, axis=0),
                                (me, 0, 0))]
            pending[l] = _exchange_start(f"scatter_grads_{l}_start", sources, lands, scatter_transfers)
            tie = pending[l][4]
        else:
            for j in range(1, n_layers):
                send_sems, recv_sems, p_ins, p_lands, _ = pending[j]
                partials[j] = _exchange_wait(f"scatter_grads_{j}_wait", send_sems, recv_sems, p_ins, p_lands, dx,
                                             scatter_transfers)
            partials[0] = _exchange("scatter_grads_0", sources, [jax.ShapeDtypeStruct(s, t) for s, t in scattered_shapes],
                                    scatter_transfers, in_vmem=False)
    grad_x = dx[None]
    p_in = [partials[l][0] for l in range(n_layers)]
    p_branch = [partials[l][1].reshape(NDEV, 3 * WIDTH, dsh) for l in range(n_layers)]
    p_out = [partials[l][2] for l in range(n_layers)]

    small_vec = jnp.concatenate(dmod + small_grads)
    n_small = small_vec.shape[0]
    small_all = _all_gather_small("gather_small_grads", _pad_rows(small_vec, BLK))
    small_sum = _sum_parts(small_all, "sum_small_grads").reshape(-1)[:n_small]
    dmod_all = small_all.reshape(NDEV, -1)[:, :n_layers * 3 * d].reshape(NDEV, n_layers, 3 * d)

    off = n_layers * 3 * d
    grad_b_mod = small_sum[:off].reshape(n_layers, 3 * d)
    per_layer = 2 * d + HG_HEAD_DIM + WIDTH + 3 * WIDTH
    g_ln_g, g_ln_b, g_norm, g_lbs, g_conv = [], [], [], [], []
    for l in range(n_layers):
        seg = small_sum[off + l * per_layer: off + (l + 1) * per_layer]
        g_ln_g.append(seg[:d])
        g_ln_b.append(seg[d:2 * d])
        g_norm.append(seg[2 * d:2 * d + HG_HEAD_DIM])
        g_lbs.append(seg[2 * d + HG_HEAD_DIM:2 * d + HG_HEAD_DIM + WIDTH])
        g_conv.append(seg[2 * d + HG_HEAD_DIM + WIDTH:].reshape(3, WIDTH))
    grad_ln_g, grad_ln_b = jnp.stack(g_ln_g), jnp.stack(g_ln_b)
    grad_norm = jnp.stack(g_norm)
    _, lbs_vjp = jax.vjp(_lower_bound_table, lower_bounds)
    grad_lower = lbs_vjp(jnp.stack(g_lbs))[0]
    grad_conv = lax.dynamic_slice_in_dim(jnp.stack(g_conv), me * cw_cols, cw_cols, axis=2)

    dmod_mine = lax.dynamic_slice_in_dim(dmod_all, me * cm, cm, axis=2).transpose(1, 0, 2)
    grad_w_mod = _wmod_grad(c_all.T, dmod_mine)

    def adam(parts_list, w, m, v, name):
        shape = w.shape
        cols = shape[-1]
        flat = lambda a: a.reshape(-1, cols)
        outs = _sum_adamw(parts_list, flat(w), flat(m), flat(v), name)
        return [o.reshape(shape) for o in outs]

    r_w_in = adam(p_in, w_in, m_w_in, v_w_in, "adamw_w_in")
    r_w_branch = adam(p_branch, w_branch, m_w_branch, v_w_branch, "adamw_w_branch")
    r_w_out = adam(p_out, w_out, m_w_out, v_w_out, "adamw_w_out")
    r_w_mod = adam([grad_w_mod.reshape(1, -1, cm)], w_mod, m_w_mod, v_w_mod, "adamw_w_mod")

    small_names = ["b_mod", "conv_w", "hgrn_norm_w", "lower_bounds", "ln_g", "ln_b"]
    small_g = [grad_b_mod, grad_conv, grad_norm, grad_lower, grad_ln_g, grad_ln_b]
    small_w = [b_mod, conv_w, hgrn_norm_w, lower_bounds, ln_g, ln_b]
    small_m = [m_b_mod, m_conv_w, m_hgrn_norm_w, m_lower_bounds, m_ln_g, m_ln_b]
    small_v = [v_b_mod, v_conv_w, v_hgrn_norm_w, v_lower_bounds, v_ln_g, v_ln_b]
    pack = lambda arrs: _pad_rows(jnp.concatenate([a.reshape(-1) for a in arrs]), BLK)
    packed = _sum_adamw([pack(small_g)[None]], pack(small_w), pack(small_m), pack(small_v), "adamw_small")
    r_small = {n: [] for n in small_names}
    for res in packed:
        flat = res.reshape(-1)
        pos = 0
        for n, w in zip(small_names, small_w):
            r_small[n].append(flat[pos:pos + w.size].reshape(w.shape))
            pos += w.size

    results = {"w_mod": r_w_mod, "w_in": r_w_in, "w_branch": r_w_branch, "w_out": r_w_out, **r_small}
    order = ["w_mod", "b_mod", "w_in", "conv_w", "hgrn_norm_w", "lower_bounds", "w_branch", "w_out", "ln_g", "ln_b"]
    outs = [loss, grad_x]
    for idx in range(4):
        outs.extend(results[n][idx] for n in order)
    return tuple(outs)
```

```python
import jax
import jax.numpy as jnp
from jax import lax
from jax.experimental import pallas as pl
from jax.experimental.pallas import tpu as pltpu

F32 = jnp.float32
BF16 = jnp.bfloat16
NDEV = 8
N_LAYERS = 2
SB_HEAD_DIM = 64
HG_HEAD_DIM = 128
WIDTH = 512
BLK = 128
LN_EPS = 1e-5
RMS_EPS = 1e-6
ALPHA = (2.0 * N_LAYERS) ** 0.25
ADAM_LR, ADAM_B1, ADAM_B2, ADAM_EPS, ADAM_WD, ADAM_STEP = 0.001, 0.9, 0.999, 1e-08, 0.01, 10
VMEM_LIMIT = 56 * 1024 * 1024
MESH = pl.DeviceIdType.MESH
HG_LEVELS = (64, 32, 16, 8, 4, 2, 1)


def _pcall(body, *, name, out_shape, grid=None, in_specs=None, out_specs=None, scratch_shapes=(),
           semantics=None, aliases=None):
    kwargs = {}
    if grid is not None:
        kwargs["grid"] = grid
    if in_specs is not None:
        kwargs["in_specs"] = in_specs
    if out_specs is not None:
        kwargs["out_specs"] = out_specs
    if aliases:
        kwargs["input_output_aliases"] = aliases
    return pl.pallas_call(
        body, name=name, out_shape=out_shape, scratch_shapes=list(scratch_shapes),
        compiler_params=pltpu.CompilerParams(dimension_semantics=semantics, vmem_limit_bytes=VMEM_LIMIT),
        interpret=False, **kwargs)


def _dot(a, b):
    return jnp.dot(a, b, preferred_element_type=F32)


def _dot_nt(a, b):
    return lax.dot_general(a, b, (((1,), (1,)), ((), ())), preferred_element_type=F32)


def _dot_tn(a, b):
    return lax.dot_general(a, b, (((0,), (0,)), ((), ())), preferred_element_type=F32)


def _split3(x):
    x1 = x.astype(BF16)
    r1 = x - x1.astype(F32)
    x2 = r1.astype(BF16)
    r2 = r1 - x2.astype(F32)
    return x1, x2, r2.astype(BF16)


def _split2(x):
    x1 = x.astype(BF16)
    return x1, (x - x1.astype(F32)).astype(BF16)


def _dot_exact_l(m_bf16, x):
    x1, x2, x3 = _split3(x)
    return _dot(m_bf16, x1) + _dot(m_bf16, x2) + _dot(m_bf16, x3)


def _sigmoid(x):
    return 1.0 / (1.0 + jnp.exp(-x))


def _silu_and_grad(x):
    s = _sigmoid(x)
    return x * s, s * (1.0 + x * (1.0 - s))


LOG2E = 1.4426950408889634
MASKED_SCORE = -1e30


def _softplus2_parts(z2):
    minus_abs = lax.bitcast_convert_type(lax.bitcast_convert_type(z2, jnp.int32) | jnp.int32(-2 ** 31), F32)
    e = jnp.exp2(minus_abs)
    sp2 = jnp.maximum(z2, 0.0) + jnp.log2(1.0 + e)
    r = 1.0 / (1.0 + e)
    return sp2, jnp.where(z2 >= 0.0, r, e * r)


def _split2_lanes(x):
    x1 = x.astype(BF16)
    return jnp.concatenate([x1, (x - x1.astype(F32)).astype(BF16)], axis=1)


def _iota2(shape, dim):
    return lax.broadcasted_iota(jnp.int32, shape, dim)


def _standardize(x):
    mu = jnp.mean(x, axis=-1, keepdims=True)
    xc = x - mu
    var = jnp.mean(xc * xc, axis=-1, keepdims=True)
    rstd = lax.rsqrt(var + LN_EPS)
    return xc * rstd, rstd


def _standardize_bwd(xhat, rstd, dxhat):
    m1 = jnp.mean(dxhat, axis=-1, keepdims=True)
    m2 = jnp.mean(dxhat * xhat, axis=-1, keepdims=True)
    return rstd * (dxhat - m1 - xhat * m2)


def _my_index():
    return 4 * lax.axis_index("x") + 2 * lax.axis_index("y") + lax.axis_index("c")


def _exchange(name, ins, out_shapes, transfers, in_vmem):
    n_in, n_out, n_t = len(ins), len(out_shapes), len(transfers)

    def body(*refs):
        in_refs, out_refs = refs[:n_in], refs[n_in:n_in + n_out]
        send_sems, recv_sems, local_sems = refs[n_in + n_out:]
        x, y, c = lax.axis_index("x"), lax.axis_index("y"), lax.axis_index("c")
        me = 4 * x + 2 * y + c
        started = []
        for t, (i, o, src_fn, dst_fn) in enumerate(transfers):
            own = pltpu.make_async_copy(src_fn(in_refs[i], me), dst_fn(out_refs[o], me), local_sems.at[t])
            own.start()
            started.append(own)
        arrivals = []
        for k in range(1, NDEV):
            px = x ^ ((k >> 2) & 1)
            py = y ^ ((k >> 1) & 1)
            pc = c ^ (k & 1)
            peer = 4 * px + 2 * py + pc
            for t, (i, o, src_fn, dst_fn) in enumerate(transfers):
                sem = t * (NDEV - 1) + k - 1
                push = pltpu.make_async_remote_copy(
                    src_ref=src_fn(in_refs[i], peer), dst_ref=dst_fn(out_refs[o], me),
                    send_sem=send_sems.at[sem], recv_sem=recv_sems.at[sem],
                    device_id=(px, py, pc), device_id_type=MESH)
                push.start()
                started.append(push)
                arrivals.append(pltpu.make_async_remote_copy(
                    src_ref=src_fn(in_refs[i], peer), dst_ref=dst_fn(out_refs[o], peer),
                    send_sem=send_sems.at[sem], recv_sem=recv_sems.at[sem],
                    device_id=(px, py, pc), device_id_type=MESH))
        for arrival in arrivals:
            arrival.wait_recv()
        for cp in started[n_t:]:
            cp.wait_send()
        for own in started[:n_t]:
            own.wait()

    space = pltpu.VMEM if in_vmem else pl.ANY
    spec = pl.BlockSpec(memory_space=space)
    return _pcall(
        body, name=name, out_shape=out_shapes,
        in_specs=[spec] * n_in, out_specs=[spec] * n_out,
        scratch_shapes=[pltpu.SemaphoreType.DMA((n_t * (NDEV - 1),)),
                        pltpu.SemaphoreType.DMA((n_t * (NDEV - 1),)),
                        pltpu.SemaphoreType.DMA((n_t,))])(*ins)


def _whole(ref, dev):
    return ref


def _slot(ref, dev):
    return ref.at[dev]


def _all_gather_small(name, v):
    out = _exchange(name, [v], [jax.ShapeDtypeStruct((NDEV,) + v.shape, v.dtype)],
                    [(0, 0, _whole, _slot)], in_vmem=True)
    return out[0]


_HBM_SPEC = pl.BlockSpec(memory_space=pltpu.HBM)
_SEM_SPEC = pl.BlockSpec(memory_space=pltpu.SEMAPHORE)
_DATAFLOW = pltpu.SideEffectType.DATAFLOW_SIDE_EFFECTING


def _peer(x, y, c, k):
    px = x ^ ((k >> 2) & 1)
    py = y ^ ((k >> 1) & 1)
    pc = c ^ (k & 1)
    return (px, py, pc), 4 * px + 2 * py + pc


def _direct_sends(transfers):
    sends = []
    for k in range(1, NDEV):
        for i, o, src_fn, dst_fn in transfers:
            sends.append((k,
                          lambda ins, lands, me, i=i, k=k, src_fn=src_fn: src_fn(ins[i], me ^ k),
                          lambda lands, me, o=o, dst_fn=dst_fn: dst_fn(lands[o], me),
                          lambda lands, me, o=o, k=k, dst_fn=dst_fn: dst_fn(lands[o], me ^ k)))
    return sends


def _exchange_start(name, ins, lands, sends):
    n_in, n_buf = len(ins), len(ins) + len(lands)
    n_sem = len(sends)

    def body(*refs):
        in_refs, land_refs = refs[:n_in], refs[n_in:n_buf]
        send_sems, recv_sems, token = refs[n_buf], refs[n_buf + 1], refs[-1]
        x, y, c = lax.axis_index("x"), lax.axis_index("y"), lax.axis_index("c")
        me = 4 * x + 2 * y + c
        for t, (k, src_fn, dst_fn, _) in enumerate(sends):
            pltpu.make_async_remote_copy(
                src_ref=src_fn(in_refs, land_refs, me), dst_ref=dst_fn(land_refs, me),
                send_sem=send_sems.at[t], recv_sem=recv_sems.at[t],
                device_id=_peer(x, y, c, k)[0], device_id_type=MESH).start()
        token[...] = jnp.zeros_like(token)

    bufs = [pltpu.with_memory_space_constraint(a, pltpu.HBM) for a in list(ins) + list(lands)]
    outs = pl.pallas_call(
        body, name=name,
        out_shape=(pltpu.SemaphoreType.DMA((n_sem,)), pltpu.SemaphoreType.DMA((n_sem,)))
        + tuple(pltpu.HBM(a.shape, a.dtype) for a in bufs) + (jax.ShapeDtypeStruct((8, BLK), F32),),
        in_specs=[_HBM_SPEC] * n_buf,
        out_specs=(_SEM_SPEC, _SEM_SPEC) + (_HBM_SPEC,) * n_buf + (pl.BlockSpec(memory_space=pltpu.VMEM),),
        input_output_aliases={b: 2 + b for b in range(n_buf)},
        compiler_params=pltpu.CompilerParams(has_side_effects=_DATAFLOW),
        interpret=False)(*bufs)
    return outs[0], outs[1], list(outs[2:2 + n_in]), list(outs[2 + n_in:2 + n_buf]), outs[-1]


def _exchange_wait(name, started, after, sends):
    send_sems, recv_sems, ins, lands, _ = started
    n_in, n_buf = len(ins), len(ins) + len(lands)

    def body(*refs):
        in_refs, land_refs = refs[:n_in], refs[n_in:n_buf]
        send_sems, recv_sems = refs[n_buf], refs[n_buf + 1]
        x, y, c = lax.axis_index("x"), lax.axis_index("y"), lax.axis_index("c")
        me = 4 * x + 2 * y + c
        for t, (k, src_fn, _, rcv_fn) in enumerate(sends):
            cp = pltpu.make_async_remote_copy(
                src_ref=src_fn(in_refs, land_refs, me), dst_ref=rcv_fn(land_refs, me),
                send_sem=send_sems.at[t], recv_sem=recv_sems.at[t],
                device_id=_peer(x, y, c, k)[0], device_id_type=MESH)
            cp.wait_send()
            cp.wait_recv()

    bufs = list(ins) + list(lands)
    outs = pl.pallas_call(
        body, name=name, out_shape=tuple(pltpu.HBM(a.shape, a.dtype) for a in bufs),
        in_specs=[_HBM_SPEC] * n_buf + [_SEM_SPEC, _SEM_SPEC, pl.BlockSpec(memory_space=pl.ANY)],
        out_specs=(_HBM_SPEC,) * n_buf,
        input_output_aliases={b: b for b in range(n_buf)},
        compiler_params=pltpu.CompilerParams(has_side_effects=_DATAFLOW),
        interpret=False)(*bufs, send_sems, recv_sems, after)
    return list(outs[n_in:])


def _place_own(shape, dtype, own, start):
    return lax.dynamic_update_slice(lax.empty(shape, dtype), own, start)


def _mod_fwd(c_all, w_mod, b_mod_mine):
    n_layers, _, cm = w_mod.shape

    def body(c_ref, w_ref, b_ref, o_ref):
        for l in range(n_layers):
            o_ref[l] = jnp.dot(c_ref[...], w_ref[l], preferred_element_type=F32,
                               precision=lax.Precision.HIGHEST) + b_ref[l]

    return _pcall(body, name="mod_fwd", out_shape=jax.ShapeDtypeStruct((n_layers, NDEV, cm), F32))(
        c_all, w_mod, b_mod_mine)


def _ln_proj(x, shift, scale, w_full, name):
    s_len, d = x.shape
    n = w_full.shape[1]
    tm = min(512, s_len)
    tn = 1024

    def body(x_ref, sh_ref, sc_ref, w_ref, proj_ref, ht_ref, h_scr):
        @pl.when(pl.program_id(1) == 0)
        def _():
            xs, _ = _standardize(x_ref[...])
            h = xs * (1.0 + sc_ref[...]) + sh_ref[...]
            h_scr[...] = h.astype(BF16)
            ht_ref[...] = h.T.astype(BF16)

        proj_ref[...] = _dot(h_scr[...], w_ref[...])

    return _pcall(
        body, name=name,
        out_shape=(jax.ShapeDtypeStruct((s_len, n), F32), jax.ShapeDtypeStruct((d, s_len), BF16)),
        grid=(s_len // tm, n // tn),
        in_specs=[pl.BlockSpec((tm, d), lambda i, j: (i, 0)),
                  pl.BlockSpec((1, d), lambda i, j: (0, 0)),
                  pl.BlockSpec((1, d), lambda i, j: (0, 0)),
                  pl.BlockSpec((d, tn), lambda i, j: (0, j))],
        out_specs=(pl.BlockSpec((tm, tn), lambda i, j: (i, j)),
                   pl.BlockSpec((d, tm), lambda i, j: (0, i))),
        scratch_shapes=[pltpu.VMEM((tm, d), BF16)],
        semantics=("arbitrary", "arbitrary"))(x, shift, scale, w_full)


def _sb_group_blocks(nb):
    return min(4, nb)


def _sb_fwd(proj, name):
    s_len = proj.shape[0]
    nb = s_len // BLK
    n_pairs = WIDTH // BLK
    gb = _sb_group_blocks(nb)
    kw = gb * BLK

    def body(q_ref, k_ref, v_ref, o_ref, tot_ref):
        lane = _iota2((1, BLK), 1)
        row = _iota2((BLK, BLK), 0)
        col = _iota2((BLK, BLK), 1)
        half = jnp.concatenate([(row >= col).astype(BF16), jnp.ones((BLK, BLK), BF16)], axis=1)
        suffix_and_sum = jnp.concatenate([half, half], axis=0)
        qpos = _iota2((BLK, kw), 0)
        kpos = _iota2((BLK, kw), 1)
        head_lanes = [(lane // SB_HEAD_DIM) == hh for hh in range(2)]

        def scores(i, gi, qms, masked):
            c0 = pl.multiple_of(gi * kw, kw)
            kb = k_ref[pl.ds(c0, kw), :].astype(BF16)
            z2s = [_dot_nt(qms[hh], kb) for hh in range(2)]
            if masked:
                valid = (c0 + kpos) < (i * BLK + qpos)
                z2s = [jnp.where(valid, z2, MASKED_SCORE) for z2 in z2s]
            return tuple(z2s)

        def accumulate(gi, z2s, carry):
            c0 = pl.multiple_of(gi * kw, kw)
            vf = v_ref[pl.ds(c0, kw), :]
            sp2s = [_softplus2_parts(z2)[0] for z2 in z2s]
            terms = [[_split2_lanes(sp2[:, b * BLK:(b + 1) * BLK]) for b in range(gb)] for sp2 in sp2s]
            sums = [[_dot(t, suffix_and_sum) for t in head_terms] for head_terms in terms]
            weights, laters = [], []
            for hh in range(2):
                later = carry[2 * hh + 1]
                parts = [None] * gb
                for b in reversed(range(gb)):
                    parts[b] = sums[hh][b][:, :BLK] + later
                    later = later + sums[hh][b][:, BLK:]
                weights.append(jnp.exp2(z2s[hh] - jnp.concatenate(parts, axis=1)).astype(BF16))
                laters.append(later)
            outs = [_dot(weights[hh], jnp.where(head_lanes[hh], vf, 0.0).astype(BF16)) for hh in range(2)]
            return (carry[0] + outs[0], laters[0], carry[2] + outs[1], laters[1])

        def queries(i):
            qf = q_ref[pl.ds(pl.multiple_of(i * BLK, BLK), BLK), :] * (SB_HEAD_DIM ** -0.5 * LOG2E)
            return [jnp.where(head_lanes[hh], qf, 0.0).astype(BF16) for hh in range(2)]

        def qblock(i, first_scores):
            r0 = pl.multiple_of(i * BLK, BLK)
            qms = queries(i)
            zero = jnp.zeros((BLK, BLK), F32)
            last = i // gb

            def step(jj, state):
                gi = last - 1 - jj
                return scores(i, gi, qms, False) + accumulate(gi + 1, state[:2], state[2:])

            state = lax.fori_loop(0, last, step, first_scores + (zero,) * 4)
            nxt = jnp.minimum(i + 1, nb - 1)
            next_scores = scores(nxt, nxt // gb, queries(nxt), True)
            carry = accumulate(0, state[:2], state[2:])
            o_ref[pl.ds(r0, BLK), :] = carry[0] + carry[2]
            tot_ref[0, pl.ds(r0, BLK), :] = carry[1]
            tot_ref[1, pl.ds(r0, BLK), :] = carry[3]
            return next_scores

        lax.fori_loop(0, nb, qblock, scores(0, 0, queries(0), True))

    col_spec = lambda off: pl.BlockSpec((s_len, BLK), lambda p: (0, off + p))
    return _pcall(
        body, name=name,
        out_shape=(jax.ShapeDtypeStruct((s_len, WIDTH), F32),
                   jax.ShapeDtypeStruct((2 * n_pairs, s_len, BLK), F32)),
        grid=(n_pairs,),
        in_specs=[col_spec(0), col_spec(n_pairs), col_spec(2 * n_pairs)],
        out_specs=(pl.BlockSpec((s_len, BLK), lambda p: (0, p)),
                   pl.BlockSpec((2, s_len, BLK), lambda p: (p, 0, 0))),
        semantics=("arbitrary",))(proj, proj, proj)


def _hg_masks(mask_ref):
    row = _iota2((BLK, BLK), 0)
    col = _iota2((BLK, BLK), 1)
    for v, m in enumerate(HG_LEVELS):
        same = (row // (2 * m)) == (col // (2 * m))
        mask_ref[v] = (same & ((row & m) != 0) & ((col & m) == 0)).astype(F32)


def _hg_mid(b, m):
    if m >= 4:
        n = BLK // (2 * m)
        mid = b.reshape(n, 2 * m, BLK)[:, m - 1:m, :]
        return jnp.broadcast_to(mid, (n, 2 * m, BLK)).reshape(BLK, BLK)
    pos = _iota2((BLK, BLK), 0) & (2 * m - 1)
    out = b
    for p in range(2 * m):
        delta = (m - 1) - p
        if delta != 0:
            out = jnp.where(pos == p, pltpu.roll(b, (-delta) % BLK, 0), out)
    return out


def _hg_chunk_inputs(qraw, fpre, lb):
    sig = _sigmoid(fpre)
    f = lb + (1.0 - lb) * sig
    g = jnp.log(f)
    q, dq_fac = _silu_and_grad(qraw)
    return q, dq_fac, f, sig, g


def _hg_level_terms(q, k, b, v_idx, m, mask_ref):
    mid = _hg_mid(b, m)
    eq = jnp.exp(jnp.minimum(b - mid, 0.0))
    ek = jnp.exp(jnp.minimum(mid - b, 0.0))
    qt = (q * eq).astype(BF16)
    kt = (k * ek).astype(BF16)
    return qt, kt, eq, ek, mask_ref[v_idx]


def _hg_scores(q, k, b, mask_ref):
    sc = None
    for v_idx, m in enumerate(HG_LEVELS):
        qt, kt, _, _, msk = _hg_level_terms(q, k, b, v_idx, m, mask_ref)
        term = _dot_nt(qt, kt) * msk
        sc = term if sc is None else sc + term
    return sc


def _hgrn_fwd(proj, lb, name):
    s_len = proj.shape[0]
    nc = s_len // BLK
    nh = WIDTH // HG_HEAD_DIM
    base = 4 * WIDTH // BLK

    def body(q_ref, f_ref, i_ref, lb_ref, o_ref, mask_ref):
        _hg_masks(mask_ref)
        row = _iota2((BLK, BLK), 0)
        col = _iota2((BLK, BLK), 1)
        lower_incl = (col <= row).astype(BF16)
        lb_v = lb_ref[...]

        def chunk(ci, st):
            r0 = pl.multiple_of(ci * BLK, BLK)
            q, _, f, _, g = _hg_chunk_inputs(q_ref[pl.ds(r0, BLK), :], f_ref[pl.ds(r0, BLK), :], lb_v)
            k = 1.0 - f
            v = i_ref[pl.ds(r0, BLK), :]
            vb = v.astype(BF16)
            b = _dot_exact_l(lower_incl, g)
            b_end = b[BLK - 1:BLK, :]
            inter = _dot_nt((q * jnp.exp(b)).astype(BF16), st.astype(BF16))
            sc = _hg_scores(q, k, b, mask_ref)
            diag = jnp.sum(q * k, axis=-1, keepdims=True)
            o_ref[pl.ds(r0, BLK), :] = inter + _dot(sc.astype(BF16), vb) + diag * v
            k_dec = (k * jnp.exp(b_end - b)).astype(BF16)
            return st * jnp.exp(b_end) + _dot_tn(vb, k_dec)

        lax.fori_loop(0, nc, chunk, jnp.zeros((HG_HEAD_DIM, HG_HEAD_DIM), F32))

    col_spec = lambda off: pl.BlockSpec((s_len, BLK), lambda h: (0, off + h))
    return _pcall(
        body, name=name, out_shape=jax.ShapeDtypeStruct((s_len, WIDTH), F32),
        grid=(nh,),
        in_specs=[col_spec(base), col_spec(base + nh), col_spec(base + 2 * nh),
                  pl.BlockSpec((1, BLK), lambda h: (0, h))],
        out_specs=pl.BlockSpec((s_len, BLK), lambda h: (0, h)),
        scratch_shapes=[pltpu.VMEM((len(HG_LEVELS), BLK, BLK), F32)],
        semantics=("arbitrary",))(proj, proj, proj, lb)


def _rms_heads(o_b, norm_w):
    n_parts, h_parts, r_parts = [], [], []
    for h in range(WIDTH // HG_HEAD_DIM):
        sl = slice(h * HG_HEAD_DIM, (h + 1) * HG_HEAD_DIM)
        o = o_b[:, sl]
        rstd = lax.rsqrt(jnp.mean(o * o, axis=-1, keepdims=True) + RMS_EPS)
        ohat = o * rstd
        h_parts.append(ohat)
        n_parts.append(ohat * norm_w[:, sl])
        r_parts.append(jnp.broadcast_to(rstd, o.shape))
    cat = lambda parts: jnp.concatenate(parts, axis=-1)
    return cat(n_parts), cat(h_parts), cat(r_parts)


def _shift_rows_down(halo, cur, k):
    tm = cur.shape[0]
    ext = jnp.concatenate([halo, cur], axis=0)
    return pltpu.roll(ext, k, 0)[8:8 + tm]


def _shift_rows_up(cur, halo, k):
    tm = cur.shape[0]
    ext = jnp.concatenate([cur, halo], axis=0)
    return pltpu.roll(ext, (tm + 8 - k) % (tm + 8), 0)[0:tm]


def _merge_fwd(x, proj, o_a, o_b, gate, norm_w, conv_w, wb, w_out, ln_g, ln_b, name):
    s_len, d = x.shape
    tm = min(256, s_len)
    hb = tm // 8

    def body(x_ref, oa_ref, za_ref, ob_ref, zb_ref, pre_ref, post_ref, u_ref, zc_ref, hpre_ref, hu_ref, g_ref,
             gate_ref, nw_ref, cw_ref, wb_ref, wo_ref, lg_ref, lbias_ref, xn_ref, mg_ref, yc_ref):
        i = pl.program_id(0)
        sa, _ = _silu_and_grad(za_ref[...])
        y_a = (oa_ref[...] * sa).astype(BF16)
        n_b, _, _ = _rms_heads(ob_ref[...], nw_ref[...])
        sb, _ = _silu_and_grad(zb_ref[...])
        y_b = (n_b * sb).astype(BF16)
        a = pre_ref[...] * u_ref[...]
        halo = jnp.where(i > 0, hpre_ref[...] * hu_ref[...], 0.0)
        cw = cw_ref[...]
        conv = cw[0:1] * _shift_rows_down(halo, a, 2) + cw[1:2] * _shift_rows_down(halo, a, 1) + cw[2:3] * a
        sc, _ = _silu_and_grad(zc_ref[...])
        y_c = (post_ref[...] * conv * sc).astype(BF16)
        merged = None
        for k, yk in enumerate((y_a, y_b, y_c)):
            yc_ref[:, k * WIDTH:(k + 1) * WIDTH] = yk
            term = _sigmoid(g_ref[:, k * d:(k + 1) * d]) * _dot(yk, wb_ref[k])
            merged = term if merged is None else merged + term
        mb = merged.astype(BF16)
        mg_ref[...] = mb
        y = _dot(mb, wo_ref[...])
        r = ALPHA * x_ref[...] + (1.0 + gate_ref[...]) * y
        rhat, _ = _standardize(r)
        xn_ref[...] = rhat * lg_ref[...] + lbias_ref[...]

    wcol = lambda cb: pl.BlockSpec((tm, WIDTH), lambda i: (i, cb))
    halo_spec = lambda cb: pl.BlockSpec((8, WIDTH), lambda i: (jnp.maximum(i * hb - 1, 0), cb))
    vec = lambda w: pl.BlockSpec((1, w), lambda i: (0, 0))
    return _pcall(
        body, name=name,
        out_shape=(jax.ShapeDtypeStruct((s_len, d), F32), jax.ShapeDtypeStruct((s_len, d), BF16),
                   jax.ShapeDtypeStruct((s_len, 3 * WIDTH), BF16)),
        grid=(s_len // tm,),
        in_specs=[pl.BlockSpec((tm, d), lambda i: (i, 0)),
                  wcol(0), wcol(3), wcol(0), wcol(7), wcol(8), wcol(9), wcol(10), wcol(11),
                  halo_spec(8), halo_spec(10),
                  pl.BlockSpec((tm, 3 * d), lambda i: (i, 2)),
                  vec(d), vec(WIDTH),
                  pl.BlockSpec((3, WIDTH), lambda i: (0, 0)),
                  pl.BlockSpec((3, WIDTH, d), lambda i: (0, 0, 0)),
                  pl.BlockSpec((d, d), lambda i: (0, 0)),
                  vec(d), vec(d)],
        out_specs=(pl.BlockSpec((tm, d), lambda i: (i, 0)), pl.BlockSpec((tm, d), lambda i: (i, 0)),
                   pl.BlockSpec((tm, 3 * WIDTH), lambda i: (i, 0))),
        semantics=("arbitrary",))(x, o_a, proj, o_b, proj, proj, proj, proj, proj, proj, proj, proj,
                                  gate, norm_w, conv_w, wb, w_out, ln_g, ln_b)


def _loss_fwd_bwd(y, target):
    s_len, d = y.shape
    tm = min(512, s_len)

    def body(y_ref, t_ref, loss_ref, dy_ref):
        @pl.when(pl.program_id(0) == 0)
        def _():
            loss_ref[...] = jnp.zeros_like(loss_ref)

        e = y_ref[...] - t_ref[...]
        dy_ref[...] = e * (1.0 / d)
        part = jnp.sum(jnp.sum(e * e, axis=-1, keepdims=True), axis=0, keepdims=True)
        loss_ref[...] += part * (0.5 / d)

    tile = pl.BlockSpec((tm, d), lambda i: (i, 0))
    return _pcall(body, name="loss", grid=(s_len // tm,),
                  out_shape=(jax.ShapeDtypeStruct((1, 1), F32), jax.ShapeDtypeStruct((s_len, d), F32)),
                  in_specs=[tile, tile],
                  out_specs=(pl.BlockSpec((1, 1), lambda i: (0, 0)), tile),
                  semantics=("arbitrary",))(y, target)


def _merge_bwd(dxn, x, merged, ycat, proj, gate, wb, w_out, ln_g, name):
    s_len, d = x.shape
    tm = min(256, s_len)

    def body(dxn_ref, x_ref, mg_ref, yc_ref, g_ref, gate_ref, wb_ref, wo_ref, lg_ref,
             dres_ref, dyc_ref, dg_ref, gwo_ref, gwb_ref, vec_ref):
        @pl.when(pl.program_id(0) == 0)
        def _():
            gwo_ref[...] = jnp.zeros_like(gwo_ref)
            gwb_ref[...] = jnp.zeros_like(gwb_ref)
            vec_ref[...] = jnp.zeros_like(vec_ref)

        mb = mg_ref[...]
        one_gate = 1.0 + gate_ref[...]
        y = _dot(mb, wo_ref[...])
        r = ALPHA * x_ref[...] + one_gate * y
        rhat, rstd = _standardize(r)
        dxn = dxn_ref[...]
        dr = _standardize_bwd(rhat, rstd, dxn * lg_ref[...])
        vec_ref[0:1, :] += jnp.sum(dxn * rhat, axis=0, keepdims=True)
        vec_ref[1:2, :] += jnp.sum(dxn, axis=0, keepdims=True)
        vec_ref[2:3, :] += jnp.sum(dr * y, axis=0, keepdims=True)
        dres_ref[...] = ALPHA * dr
        dy = (one_gate * dr).astype(BF16)
        gwo_ref[...] += _dot_tn(mb, dy)
        dmerged = _dot_nt(dy, wo_ref[...])
        for k in range(3):
            yk = yc_ref[:, k * WIDTH:(k + 1) * WIDTH]
            sg = _sigmoid(g_ref[:, k * d:(k + 1) * d])
            pk = _dot(yk, wb_ref[k])
            dg_ref[:, k * d:(k + 1) * d] = (dmerged * pk * sg * (1.0 - sg)).astype(BF16)
            dpk = (dmerged * sg).astype(BF16)
            dyc_ref[:, k * WIDTH:(k + 1) * WIDTH] = _dot_nt(dpk, wb_ref[k])
            gwb_ref[k] += _dot_tn(yk, dpk)

    tile = lambda w: pl.BlockSpec((tm, w), lambda i: (i, 0))
    vec = pl.BlockSpec((1, d), lambda i: (0, 0))
    return _pcall(
        body, name=name,
        out_shape=(jax.ShapeDtypeStruct((s_len, d), F32), jax.ShapeDtypeStruct((s_len, 3 * WIDTH), F32),
                   jax.ShapeDtypeStruct(proj.shape, BF16), jax.ShapeDtypeStruct((d, d), F32),
                   jax.ShapeDtypeStruct((3, WIDTH, d), F32), jax.ShapeDtypeStruct((8, d), F32)),
        grid=(s_len // tm,),
        in_specs=[tile(d), tile(d), tile(d), tile(3 * WIDTH),
                  pl.BlockSpec((tm, 3 * d), lambda i: (i, 2)),
                  vec, pl.BlockSpec((3, WIDTH, d), lambda i: (0, 0, 0)),
                  pl.BlockSpec((d, d), lambda i: (0, 0)), vec],
        out_specs=(tile(d), tile(3 * WIDTH), pl.BlockSpec((tm, 3 * d), lambda i: (i, 2)),
                   pl.BlockSpec((d, d), lambda i: (0, 0)),
                   pl.BlockSpec((3, WIDTH, d), lambda i: (0, 0, 0)),
                   pl.BlockSpec((8, d), lambda i: (0, 0))),
        semantics=("arbitrary",))(dxn, x, merged, ycat, proj, gate, wb, w_out, ln_g)


def _branch_bwd(dycat, proj, o_a, o_b, norm_w, conv_w, dproj, name):
    s_len = proj.shape[0]
    tm = min(256, s_len)
    hb = tm // 8
    n_tiles = s_len // tm

    def body(dya_ref, dyb_ref, dyc_ref, oa_ref, za_ref, ob_ref, zb_ref, pre_ref, post_ref, u_ref, zc_ref,
             hpre_ref, hu_ref, ndyc_ref, npost_ref, nzc_ref, nw_ref, cw_ref, dproj_in,
             dproj_ref, doa_ref, dob_ref, vec_ref, dza_scr, dzb_scr, dc_scr, sems):
        del dproj_in
        i = pl.program_id(0)

        @pl.when(i == 0)
        def _():
            vec_ref[...] = jnp.zeros_like(vec_ref)

        sa, dsa = _silu_and_grad(za_ref[...])
        dya = dya_ref[...]
        doa_ref[...] = dya * sa
        dza_scr[...] = (dya * oa_ref[...] * dsa).astype(BF16)
        nw = nw_ref[...]
        n_b, ohat, rstd = _rms_heads(ob_ref[...], nw)
        sb, dsb = _silu_and_grad(zb_ref[...])
        dyb = dyb_ref[...]
        dzb_scr[...] = (dyb * n_b * dsb).astype(BF16)
        dn = dyb * sb
        vec_ref[0:1, :] += jnp.sum(dn * ohat, axis=0, keepdims=True)
        dnw = dn * nw
        parts = []
        for h in range(WIDTH // HG_HEAD_DIM):
            sl = slice(h * HG_HEAD_DIM, (h + 1) * HG_HEAD_DIM)
            m2 = jnp.mean(dnw[:, sl] * ohat[:, sl], axis=-1, keepdims=True)
            parts.append(rstd[:, sl] * (dnw[:, sl] - ohat[:, sl] * m2))
        dob_ref[...] = jnp.concatenate(parts, axis=-1)
        cw = cw_ref[...]
        pre, u, post = pre_ref[...], u_ref[...], post_ref[...]
        a = pre * u
        halo = jnp.where(i > 0, hpre_ref[...] * hu_ref[...], 0.0)
        a1 = _shift_rows_down(halo, a, 1)
        a2 = _shift_rows_down(halo, a, 2)
        conv = cw[0:1] * a2 + cw[1:2] * a1 + cw[2:3] * a
        sc, dsc = _silu_and_grad(zc_ref[...])
        dyc = dyc_ref[...]
        dconv = dyc * post * sc
        nsc, _ = _silu_and_grad(nzc_ref[...])
        nxt = jnp.where(i < n_tiles - 1, ndyc_ref[...] * npost_ref[...] * nsc, 0.0)
        da = cw[2:3] * dconv + cw[1:2] * _shift_rows_up(dconv, nxt, 1) + cw[0:1] * _shift_rows_up(dconv, nxt, 2)
        dc_scr[:, 0 * WIDTH:1 * WIDTH] = (da * u).astype(BF16)
        dc_scr[:, 1 * WIDTH:2 * WIDTH] = (dyc * conv * sc).astype(BF16)
        dc_scr[:, 2 * WIDTH:3 * WIDTH] = (da * pre).astype(BF16)
        dc_scr[:, 3 * WIDTH:4 * WIDTH] = (dyc * post * conv * dsc).astype(BF16)
        vec_ref[1:2, :] += jnp.sum(dconv * a2, axis=0, keepdims=True)
        vec_ref[2:3, :] += jnp.sum(dconv * a1, axis=0, keepdims=True)
        vec_ref[3:4, :] += jnp.sum(dconv * a, axis=0, keepdims=True)
        rows = pl.ds(pl.multiple_of(i * tm, tm), tm)
        copies = [pltpu.make_async_copy(dza_scr, dproj_ref.at[rows, 3 * WIDTH:4 * WIDTH], sems.at[0]),
                  pltpu.make_async_copy(dzb_scr, dproj_ref.at[rows, 7 * WIDTH:8 * WIDTH], sems.at[1]),
                  pltpu.make_async_copy(dc_scr, dproj_ref.at[rows, 8 * WIDTH:12 * WIDTH], sems.at[2])]
        for cp in copies:
            cp.start()
        for cp in copies:
            cp.wait()

    wcol = lambda cb: pl.BlockSpec((tm, WIDTH), lambda i: (i, cb))
    prev = lambda cb: pl.BlockSpec((8, WIDTH), lambda i: (jnp.maximum(i * hb - 1, 0), cb))
    nxt = lambda cb: pl.BlockSpec((8, WIDTH), lambda i: (jnp.minimum((i + 1) * hb, s_len // 8 - 1), cb))
    anyspec = pl.BlockSpec(memory_space=pl.ANY)
    out = jax.ShapeDtypeStruct((s_len, WIDTH), F32)
    return _pcall(
        body, name=name,
        out_shape=(jax.ShapeDtypeStruct(dproj.shape, dproj.dtype), out, out, jax.ShapeDtypeStruct((8, WIDTH), F32)),
        grid=(n_tiles,),
        in_specs=[wcol(0), wcol(1), wcol(2), wcol(0), wcol(3), wcol(0), wcol(7), wcol(8), wcol(9), wcol(10), wcol(11),
                  prev(8), prev(10), nxt(2), nxt(9), nxt(11),
                  pl.BlockSpec((1, WIDTH), lambda i: (0, 0)), pl.BlockSpec((3, WIDTH), lambda i: (0, 0)), anyspec],
        out_specs=(anyspec, wcol(0), wcol(0), pl.BlockSpec((8, WIDTH), lambda i: (0, 0))),
        scratch_shapes=[pltpu.VMEM((tm, WIDTH), BF16), pltpu.VMEM((tm, WIDTH), BF16),
                        pltpu.VMEM((tm, 4 * WIDTH), BF16), pltpu.SemaphoreType.DMA((3,))],
        aliases={18: 0},
        semantics=("arbitrary",))(dycat, dycat, dycat, o_a, proj, o_b, proj, proj, proj, proj, proj,
                                  proj, proj, dycat, proj, proj, norm_w, conv_w, dproj)


def _sb_bwd(proj, do_a, totals, dproj, name):
    s_len = proj.shape[0]
    nb = s_len // BLK
    n_pairs = WIDTH // BLK
    scale = SB_HEAD_DIM ** -0.5
    gb = _sb_group_blocks(nb)
    kw = gb * BLK

    def body(q_ref, k_ref, v_ref, do_ref, tot_ref, dproj_in, dproj_ref, dq_ref, dk_ref, dv_ref, out_scr, sems):
        del dproj_in
        lane = _iota2((1, BLK), 1)
        row = _iota2((BLK, BLK), 0)
        col = _iota2((BLK, BLK), 1)
        ones = jnp.ones((BLK, BLK), BF16)
        twice = lambda m: jnp.concatenate([m, m], axis=0)
        before_and_sum = twice(jnp.concatenate([(row < col).astype(BF16), ones], axis=1))
        upto_and_sum = twice(jnp.concatenate([(row <= col).astype(BF16), ones], axis=1))
        qpos = _iota2((BLK, kw), 0)
        kpos = _iota2((BLK, kw), 1)
        head_lanes = [(lane // SB_HEAD_DIM) == hh for hh in range(2)]
        dk_ref[...] = jnp.zeros_like(dk_ref)
        dv_ref[...] = jnp.zeros_like(dv_ref)

        causal = kpos - qpos

        def scores(i, gi, qms):
            c0 = pl.multiple_of(gi * kw, kw)
            kb = k_ref[pl.ds(c0, kw), :].astype(BF16)
            valid = causal < i * BLK - c0
            return tuple(jnp.where(valid, _dot_nt(qms[hh], kb), MASKED_SCORE) for hh in range(2))

        def process(gi, z2s, qms, doms, totals_i, carry):
            c0 = pl.multiple_of(gi * kw, kw)
            kf = k_ref[pl.ds(c0, kw), :]
            vf = v_ref[pl.ds(c0, kw), :]
            kms = [jnp.where(head_lanes[hh], kf, 0.0).astype(BF16) for hh in range(2)]
            vms = [jnp.where(head_lanes[hh], vf, 0.0).astype(BF16) for hh in range(2)]
            das = [_dot_nt(doms[hh], vms[hh]) for hh in range(2)]
            halves = [_softplus2_parts(z2) for z2 in z2s]
            terms = [[_split2_lanes(sp2[:, b * BLK:(b + 1) * BLK]) for b in range(gb)] for sp2, _ in halves]
            sums = [[_dot(t, before_and_sum) for t in head_terms] for head_terms in terms]
            weights, gmats, l_befores = [], [], []
            for hh in range(2):
                l_before = carry[3 * hh + 1]
                parts = []
                for b in range(gb):
                    parts.append(totals_i[hh] - l_before - sums[hh][b][:, :BLK])
                    l_before = l_before + sums[hh][b][:, BLK:]
                a = jnp.exp2(z2s[hh] - jnp.concatenate(parts, axis=1))
                weights.append(a.astype(BF16))
                gmats.append(a * das[hh])
                l_befores.append(l_before)
            terms = [[_split2_lanes(g[:, b * BLK:(b + 1) * BLK]) for b in range(gb)] for g in gmats]
            sums = [[_dot(t, upto_and_sum) for t in head_terms] for head_terms in terms]
            dzs, g_befores = [], []
            for hh in range(2):
                g_before = carry[3 * hh + 2]
                parts = []
                for b in range(gb):
                    parts.append(g_before + sums[hh][b][:, :BLK])
                    g_before = g_before + sums[hh][b][:, BLK:]
                dzs.append((gmats[hh] - halves[hh][1] * jnp.concatenate(parts, axis=1)).astype(BF16))
                g_befores.append(g_before)
            dks = [_dot_tn(dzs[hh], qms[hh]) for hh in range(2)]
            dvs = [_dot_tn(weights[hh], doms[hh]) for hh in range(2)]
            dqs = [_dot(dzs[hh], kms[hh]) for hh in range(2)]
            dk_ref[pl.ds(c0, kw), :] += (dks[0] + dks[1]) * (1.0 / LOG2E)
            dv_ref[pl.ds(c0, kw), :] += dvs[0] + dvs[1]
            return (carry[0] + dqs[0], l_befores[0], g_befores[0], carry[3] + dqs[1], l_befores[1], g_befores[1])

        def queries(i):
            qf = q_ref[pl.ds(pl.multiple_of(i * BLK, BLK), BLK), :] * (scale * LOG2E)
            return [jnp.where(head_lanes[hh], qf, 0.0).astype(BF16) for hh in range(2)]

        def qblock(i, first_scores):
            r0 = pl.multiple_of(i * BLK, BLK)
            qms = queries(i)
            dof = do_ref[pl.ds(r0, BLK), :]
            doms = [jnp.where(head_lanes[hh], dof, 0.0).astype(BF16) for hh in range(2)]
            totals_i = [tot_ref[hh, pl.ds(r0, BLK), :] for hh in range(2)]
            zero = jnp.zeros((BLK, BLK), F32)
            last = i // gb

            def step(gi, state):
                return scores(i, gi + 1, qms) + process(gi, state[:2], qms, doms, totals_i, state[2:])

            state = lax.fori_loop(0, last, step, first_scores + (zero,) * 6)
            nxt = jnp.minimum(i + 1, nb - 1)
            next_scores = scores(nxt, 0, queries(nxt))
            carry = process(last, state[:2], qms, doms, totals_i, state[2:])
            dq_ref[pl.ds(r0, BLK), :] = (carry[0] + carry[3]) * scale
            return next_scores

        lax.fori_loop(0, nb, qblock, scores(0, 0, queries(0)))
        pair = pl.program_id(0)
        copies = []
        for t, ref in enumerate((dq_ref, dk_ref, dv_ref)):
            out_scr[t] = ref[...].astype(BF16)
            col = pl.multiple_of((t * n_pairs + pair) * BLK, BLK)
            copies.append(pltpu.make_async_copy(out_scr.at[t], dproj_ref.at[:, pl.ds(col, BLK)], sems.at[t]))
            copies[-1].start()
        for cp in copies:
            cp.wait()

    col_spec = lambda off: pl.BlockSpec((s_len, BLK), lambda p: (0, off + p))
    anyspec = pl.BlockSpec(memory_space=pl.ANY)
    return _pcall(
        body, name=name, out_shape=jax.ShapeDtypeStruct(dproj.shape, dproj.dtype), grid=(n_pairs,),
        in_specs=[col_spec(0), col_spec(n_pairs), col_spec(2 * n_pairs), col_spec(0),
                  pl.BlockSpec((2, s_len, BLK), lambda p: (p, 0, 0)), anyspec],
        out_specs=anyspec,
        scratch_shapes=[pltpu.VMEM((s_len, BLK), F32)] * 3 + [pltpu.VMEM((3, s_len, BLK), BF16),
                                                              pltpu.SemaphoreType.DMA((3,))],
        aliases={5: 0},
        semantics=("arbitrary",))(proj, proj, proj, do_a, totals, dproj)


def _hgrn_bwd(proj, do_b, lb, dproj, name):
    s_len = proj.shape[0]
    nc = s_len // BLK
    nh = WIDTH // HG_HEAD_DIM
    base = 4 * WIDTH // BLK

    def body(q_ref, f_ref, i_ref, do_ref, lb_ref, dproj_in, dproj_ref, dlb_ref, mask_ref, st_ref, out_scr, sems):
        del dproj_in
        _hg_masks(mask_ref)
        row = _iota2((BLK, BLK), 0)
        col = _iota2((BLK, BLK), 1)
        lower_incl = (col <= row).astype(BF16)
        upper_incl = (col >= row).astype(BF16)
        lb_v = lb_ref[...]

        def load(ci):
            r0 = pl.multiple_of(ci * BLK, BLK)
            q, dq_fac, f, sig, g = _hg_chunk_inputs(q_ref[pl.ds(r0, BLK), :], f_ref[pl.ds(r0, BLK), :], lb_v)
            b = _dot_exact_l(lower_incl, g)
            return r0, q, dq_fac, f, sig, b, i_ref[pl.ds(r0, BLK), :]

        def fwd_chunk(ci, st):
            st_ref[ci] = st
            _, _, _, f, _, b, v = load(ci)
            b_end = b[BLK - 1:BLK, :]
            k_dec = ((1.0 - f) * jnp.exp(b_end - b)).astype(BF16)
            return st * jnp.exp(b_end) + _dot_tn(v.astype(BF16), k_dec)

        lax.fori_loop(0, nc, fwd_chunk, jnp.zeros((HG_HEAD_DIM, HG_HEAD_DIM), F32))

        def bwd_chunk(cc, carry):
            dst, suffix, dlb = carry
            ci = nc - 1 - cc
            r0, q, dq_fac, f, sig, b, v = load(ci)
            k = 1.0 - f
            vb = v.astype(BF16)
            do = do_ref[pl.ds(r0, BLK), :]
            dob = do.astype(BF16)
            b_end = b[BLK - 1:BLK, :]
            e_q = jnp.exp(b)
            e_k = jnp.exp(b_end - b)
            qe = (q * e_q).astype(BF16)
            kh = (k * e_k).astype(BF16)
            st1, st2 = _split2(st_ref[ci])
            ds1, ds2 = _split2(dst)
            dqe = _dot(dob, st1) + _dot(dob, st2)
            dkh = _dot(vb, ds1) + _dot(vb, ds2)
            dq = e_q * dqe
            dk = e_k * dkh
            dv = _dot_nt(kh, ds1)
            dst_new = dst * jnp.exp(b_end) + _dot_tn(dob, qe)
            dlog = qe.astype(F32) * dqe - kh.astype(F32) * dkh
            da = _dot_nt(dob, vb)
            sc = None
            for v_idx, m in enumerate(HG_LEVELS):
                qm, km, eq, ek, msk = _hg_level_terms(q, k, b, v_idx, m, mask_ref)
                term = _dot_nt(qm, km) * msk
                sc = term if sc is None else sc + term
                pm = (da * msk).astype(BF16)
                dqm = _dot(pm, km)
                dkm = _dot_tn(pm, qm)
                dq = dq + dqm * eq
                dk = dk + dkm * ek
                dlog = dlog + (qm.astype(F32) * dqm - km.astype(F32) * dkm)
            a_diag = jnp.sum(do * v, axis=-1, keepdims=True)
            s_diag = jnp.sum(q * k, axis=-1, keepdims=True)
            dq = dq + a_diag * k
            dk = dk + a_diag * q
            dv = dv + _dot_tn(sc.astype(BF16), dob) + s_diag * do
            dg = _dot_exact_l(upper_incl, dlog) + suffix
            dfull = dg / f - dk
            out_scr[0, pl.ds(r0, BLK), :] = (dq * dq_fac).astype(BF16)
            out_scr[1, pl.ds(r0, BLK), :] = (dfull * (1.0 - lb_v) * sig * (1.0 - sig)).astype(BF16)
            out_scr[2, pl.ds(r0, BLK), :] = dv.astype(BF16)
            dlb = dlb + jnp.sum(dfull * (1.0 - sig), axis=0, keepdims=True)
            return dst_new, dg[0:1, :], dlb

        zero_row = jnp.zeros((1, BLK), F32)
        _, _, dlb = lax.fori_loop(0, nc, bwd_chunk,
                                  (jnp.zeros((HG_HEAD_DIM, HG_HEAD_DIM), F32), zero_row, zero_row))
        dlb_ref[...] = jnp.broadcast_to(dlb, dlb_ref.shape)
        head = pl.program_id(0)
        copies = []
        for t in range(3):
            col = pl.multiple_of((base + t * nh + head) * BLK, BLK)
            copies.append(pltpu.make_async_copy(out_scr.at[t], dproj_ref.at[:, pl.ds(col, BLK)], sems.at[t]))
            copies[-1].start()
        for cp in copies:
            cp.wait()

    col_spec = lambda off: pl.BlockSpec((s_len, BLK), lambda h: (0, off + h))
    anyspec = pl.BlockSpec(memory_space=pl.ANY)
    return _pcall(
        body, name=name,
        out_shape=(jax.ShapeDtypeStruct(dproj.shape, dproj.dtype), jax.ShapeDtypeStruct((8, WIDTH), F32)),
        grid=(nh,),
        in_specs=[col_spec(base), col_spec(base + nh), col_spec(base + 2 * nh), col_spec(0),
                  pl.BlockSpec((1, BLK), lambda h: (0, h)), anyspec],
        out_specs=(anyspec, pl.BlockSpec((8, BLK), lambda h: (0, h))),
        scratch_shapes=[pltpu.VMEM((len(HG_LEVELS), BLK, BLK), F32),
                        pltpu.VMEM((nc, HG_HEAD_DIM, HG_HEAD_DIM), F32),
                        pltpu.VMEM((3, s_len, BLK), BF16), pltpu.SemaphoreType.DMA((3,))],
        aliases={5: 0},
        semantics=("arbitrary",))(proj, proj, proj, do_b, lb, dproj)


def _dh_matmul(dproj, w_full, name):
    s_len, n = dproj.shape
    d = w_full.shape[0]
    tm = min(512, s_len)
    tk = 1536

    def body(dp_ref, w_ref, dh_ref):
        part = _dot_nt(dp_ref[...], w_ref[...])

        @pl.when(pl.program_id(1) == 0)
        def _():
            dh_ref[...] = part

        @pl.when(pl.program_id(1) > 0)
        def _():
            dh_ref[...] += part

    return _pcall(
        body, name=name, out_shape=jax.ShapeDtypeStruct((s_len, d), F32),
        grid=(s_len // tm, n // tk),
        in_specs=[pl.BlockSpec((tm, tk), lambda i, k: (i, k)), pl.BlockSpec((d, tk), lambda i, k: (0, k))],
        out_specs=pl.BlockSpec((tm, d), lambda i, k: (i, 0)),
        semantics=("arbitrary", "arbitrary"))(dproj, w_full)


def _gw_matmul(h_t, dproj, name):
    d, s_len = h_t.shape
    n = dproj.shape[1]
    tn = 1152

    def body(ht_ref, dp_ref, gw_ref):
        gw_ref[...] = _dot(ht_ref[...], dp_ref[...]).astype(BF16)

    return _pcall(
        body, name=name, out_shape=jax.ShapeDtypeStruct((d, n), BF16),
        grid=(n // tn,),
        in_specs=[pl.BlockSpec((d, s_len), lambda j: (0, 0)), pl.BlockSpec((s_len, tn), lambda j: (0, j))],
        out_specs=pl.BlockSpec((d, tn), lambda j: (0, j)),
        semantics=("arbitrary",))(h_t, dproj)


def _ln_bwd(dh, x, scale, dres, name):
    s_len, d = x.shape
    tm = min(512, s_len)

    def body(dh_ref, x_ref, sc_ref, dres_ref, dx_ref, vec_ref):
        @pl.when(pl.program_id(0) == 0)
        def _():
            vec_ref[...] = jnp.zeros_like(vec_ref)

        dh = dh_ref[...]
        xs, rstd = _standardize(x_ref[...])
        vec_ref[0:1, :] += jnp.sum(dh, axis=0, keepdims=True)
        vec_ref[1:2, :] += jnp.sum(dh * xs, axis=0, keepdims=True)
        dx_ref[...] = _standardize_bwd(xs, rstd, dh * (1.0 + sc_ref[...])) + dres_ref[...]

    tile = pl.BlockSpec((tm, d), lambda i: (i, 0))
    return _pcall(body, name=name, grid=(s_len // tm,),
                  out_shape=(jax.ShapeDtypeStruct((s_len, d), F32), jax.ShapeDtypeStruct((8, d), F32)),
                  in_specs=[tile, tile, pl.BlockSpec((1, d), lambda i: (0, 0)), tile],
                  out_specs=(tile, pl.BlockSpec((8, d), lambda i: (0, 0))),
                  semantics=("arbitrary",))(dh, x, scale, dres)


def _wmod_grad(c_t, dmod):
    d = c_t.shape[0]
    n_layers, _, cm = dmod.shape

    def body(c_ref, dm_ref, o_ref):
        for l in range(n_layers):
            acc = None
            for b in range(NDEV):
                term = c_ref[:, b:b + 1] * dm_ref[l, b:b + 1, :]
                acc = term if acc is None else acc + term
            o_ref[l] = acc

    return _pcall(body, name="wmod_grad", out_shape=jax.ShapeDtypeStruct((n_layers, d, cm), F32))(c_t, dmod)


def _sum_adamw(parts_list, w, m, v, name):
    n_ranges = len(parts_list)
    n_src, range_rows, cols = parts_list[0].shape
    rows = range_rows * n_ranges
    tr = range_rows
    for cand in (512, 256, 128, 64, 32, 16, 8):
        if range_rows % cand == 0 and cand * cols * 4 <= (2 << 20):
            tr = cand
            break
    tiles = range_rows // tr

    def body(*refs):
        p_refs = refs[:n_ranges]
        w_ref, m_ref, v_ref, g_ref, d_ref, nm_ref, nv_ref = refs[n_ranges:]

        def step(p_ref):
            g = p_ref[0].astype(F32)
            for s in range(1, n_src):
                g = g + p_ref[s].astype(F32)
            nm = ADAM_B1 * m_ref[...] + (1.0 - ADAM_B1) * g
            nv = ADAM_B2 * v_ref[...] + (1.0 - ADAM_B2) * (g * g)
            m_hat = nm / (1.0 - ADAM_B1 ** ADAM_STEP)
            v_hat = nv / (1.0 - ADAM_B2 ** ADAM_STEP)
            g_ref[...] = g
            d_ref[...] = -ADAM_LR * (m_hat / (jnp.sqrt(v_hat) + ADAM_EPS) + ADAM_WD * w_ref[...])
            nm_ref[...] = nm
            nv_ref[...] = nv

        if n_ranges == 1:
            step(p_refs[0])
        else:
            for j in range(n_ranges):
                @pl.when(pl.program_id(0) // tiles == j)
                def _(j=j):
                    step(p_refs[j])

    def part_spec(j):
        return pl.BlockSpec((n_src, tr, cols), lambda i: (0, jnp.clip(i - j * tiles, 0, tiles - 1), 0))

    tile = pl.BlockSpec((tr, cols), lambda i: (i, 0))
    out = jax.ShapeDtypeStruct((rows, cols), F32)
    return _pcall(body, name=name, grid=(rows // tr,), out_shape=(out,) * 4,
                  in_specs=[part_spec(j) for j in range(n_ranges)] + [tile, tile, tile],
                  out_specs=(tile,) * 4, semantics=("arbitrary",))(*parts_list, w, m, v)


def _sum_parts(parts, name):
    n_src = parts.shape[0]

    def body(p_ref, o_ref):
        acc = p_ref[0]
        for s in range(1, n_src):
            acc = acc + p_ref[s]
        o_ref[...] = acc

    return _pcall(body, name=name, out_shape=jax.ShapeDtypeStruct(parts.shape[1:], F32))(parts)


def _lower_bound_table(lower_bounds):
    p = jax.nn.softmax(lower_bounds.astype(F32), axis=0)
    return jnp.cumsum(p, axis=0) - p[0:1]


def _pad_rows(v, width):
    n = v.shape[0]
    rows = -(-n // width)
    rows = -(-rows // 8) * 8
    return jnp.pad(v, (0, rows * width - n)).reshape(rows, width)


def kernel(x, c, w_mod, b_mod, w_in, conv_w, hgrn_norm_w, lower_bounds, w_branch, w_out, ln_g, ln_b, loss_target, m_w_mod, m_b_mod, m_w_in, m_conv_w, m_hgrn_norm_w, m_lower_bounds, m_w_branch, m_w_out, m_ln_g, m_ln_b, v_w_mod, v_b_mod, v_w_in, v_conv_w, v_hgrn_norm_w, v_lower_bounds, v_w_branch, v_w_out, v_ln_g, v_ln_b):
    n_layers = N_LAYERS
    s_len, d = x.shape[1], x.shape[2]
    n_cols = w_in.shape[2] * NDEV
    cw_cols = conv_w.shape[2]
    cm = w_mod.shape[2]
    me = _my_index()
    x0 = x[0]
    target = loss_target[0]

    shard = w_in.shape[2]
    dsh = d // NDEV
    w_in_b, w_branch_b, w_out_b = w_in.astype(BF16), w_branch.astype(BF16), w_out.astype(BF16)
    window = lambda ref, dev: ref.at[:, pl.ds(pl.multiple_of(dev * shard, BLK), shard)]
    places = [window, _slot, _slot]
    chip_sends, sibling_sends = [], []
    for k in (1, 2, 4, 6):
        for a in range(3):
            chip_sends.append((k, lambda ins, lands, me, a=a: ins[a],
                               lambda lands, me, a=a: places[a](lands[a], me),
                               lambda lands, me, a=a, k=k: places[a](lands[a], me ^ k)))
    for j in (2, 4, 6):
        for a in range(3):
            sibling_sends.append((1, lambda ins, lands, me, a=a, j=j: places[a](lands[a], me ^ j),
                                  lambda lands, me, a=a, j=j: places[a](lands[a], me ^ j),
                                  lambda lands, me, a=a, j=j: places[a](lands[a], me ^ 1 ^ j)))

    def gather_start(l, tie):
        lands = [_place_own((d, n_cols), BF16, w_in_b[l], (0, me * shard)),
                 _place_own((NDEV, 3, WIDTH, dsh), BF16, w_branch_b[l][None], (me, 0, 0, 0)),
                 _place_own((NDEV, dsh, d), BF16, w_out_b[l][None], (me, 0, 0))]
        if tie is not None:
            tie, lands = lax.optimization_barrier((tie, lands))
        return _exchange_start(f"gather_weights_{l}_chips_start", [w_in_b[l], w_branch_b[l], w_out_b[l]], lands,
                               chip_sends)

    def gather_pass_on(l, started, after):
        lands = _exchange_wait(f"gather_weights_{l}_chips_wait", started, after, chip_sends)
        return _exchange_start(f"gather_weights_{l}_sibling_start", [], lands, sibling_sends)

    def gather_finish(l, started, after):
        w_in_l, w_branch_l, w_out_l = _exchange_wait(f"gather_weights_{l}_sibling_wait", started, after,
                                                      sibling_sends)
        return w_in_l, w_branch_l.transpose(1, 2, 0, 3).reshape(3, WIDTH, d), w_out_l.reshape(d, d)

    gathering = gather_start(0, None)

    small = _pad_rows(jnp.concatenate([c.reshape(-1), conv_w.reshape(-1)]), BLK) + gathering[4][0, 0]
    small_all = _all_gather_small("gather_c_conv", small).reshape(NDEV, -1)
    c_all = small_all[:, :d]
    conv_full = small_all[:, d:d + n_layers * 3 * cw_cols].reshape(NDEV, n_layers, 3, cw_cols)
    conv_full = conv_full.transpose(1, 2, 0, 3).reshape(n_layers, 3, WIDTH)

    b_mod_mine = lax.dynamic_slice_in_dim(b_mod, me * cm, cm, axis=1).reshape(n_layers, 1, cm)
    mod_cols = _mod_fwd(c_all, w_mod, b_mod_mine)
    mod_all = _all_gather_small("gather_mod", mod_cols.reshape(n_layers * NDEV, cm))
    mod_all = mod_all.reshape(NDEV, n_layers, NDEV, cm)
    mod_mine = lax.dynamic_index_in_dim(mod_all, me, axis=2, keepdims=False)
    mod_mine = mod_mine.transpose(1, 0, 2).reshape(n_layers, 3, 1, d)

    passing = gather_pass_on(0, gathering, mod_mine)
    weights = gather_finish(0, passing, passing[4])

    lbs = _lower_bound_table(lower_bounds)
    norm_w4 = jnp.tile(hgrn_norm_w, (1, WIDTH // HG_HEAD_DIM))

    saved = []
    xl = x0
    for l in range(n_layers):
        shift, scale, gate = mod_mine[l, 0], mod_mine[l, 1], mod_mine[l, 2]
        w_in_l, wb_l, wo_l = weights
        if l + 1 < n_layers:
            gathering = gather_start(l + 1, wo_l)
            shift = shift + gathering[4][0, 0]
        proj, h_t = _ln_proj(xl, shift, scale, w_in_l, f"ln_proj_{l}")
        o_a, totals = _sb_fwd(proj, f"sb_fwd_{l}")
        o_b = _hgrn_fwd(proj, lbs[l:l + 1], f"hgrn_fwd_{l}")
        if l + 1 < n_layers:
            passing = gather_pass_on(l + 1, gathering, o_b)
            gate_fwd = gate + passing[4][0, 0]
        else:
            gate_fwd = gate
        x_new, merged, ycat = _merge_fwd(xl, proj, o_a, o_b, gate_fwd, norm_w4[l:l + 1], conv_full[l],
                                         wb_l, wo_l, ln_g[l:l + 1], ln_b[l:l + 1], f"merge_fwd_{l}")
        saved.append((xl, proj, h_t, o_a, totals, o_b, merged, ycat, w_in_l, wb_l, wo_l))
        if l + 1 < n_layers:
            weights = gather_finish(l + 1, passing, x_new)
        xl = x_new

    loss_part, dx = _loss_fwd_bwd(xl, target)
    loss = lax.psum(loss_part[0, 0], ("x", "y", "c"))

    scatter_sends = _direct_sends([(0, 0, window, _slot), (1, 1, _slot, _slot), (2, 2, _slot, _slot)])
    scattering = [None] * n_layers
    small_grads = [None] * n_layers
    dmod = [None] * n_layers
    tie = None
    for l in reversed(range(n_layers)):
        xl, proj, h_t, o_a, totals, o_b, merged, ycat, w_in_l, wb_l, wo_l = saved[l]
        scale, gate = mod_mine[l, 1], mod_mine[l, 2]
        if tie is not None:
            gate = gate + tie[0, 0]
        dres, dycat, dproj, gwo, gwb, mvec = _merge_bwd(dx, xl, merged, ycat, proj, gate, wb_l, wo_l,
                                                        ln_g[l:l + 1], f"merge_bwd_{l}")
        dproj, do_a, do_b, bvec = _branch_bwd(dycat, proj, o_a, o_b, norm_w4[l:l + 1], conv_full[l], dproj,
                                              f"branch_bwd_{l}")
        dproj = _sb_bwd(proj, do_a, totals, dproj, f"sb_bwd_{l}")
        dproj, dlb = _hgrn_bwd(proj, do_b, lbs[l:l + 1], dproj, f"hgrn_bwd_{l}")
        gwi = _gw_matmul(h_t, dproj, f"gw_matmul_{l}")
        gwb_by_owner = gwb.astype(BF16).reshape(3, WIDTH, NDEV, dsh).transpose(2, 0, 1, 3)
        gwo_by_owner = gwo.astype(BF16).reshape(NDEV, dsh, d)
        lands = [_place_own((NDEV, d, shard), BF16, lax.dynamic_slice_in_dim(gwi, me * shard, shard, axis=1)[None],
                            (me, 0, 0)),
                 _place_own((NDEV, 3, WIDTH, dsh), BF16, lax.dynamic_slice_in_dim(gwb_by_owner, me, 1, axis=0),
                            (me, 0, 0, 0)),
                 _place_own((NDEV, dsh, d), BF16, lax.dynamic_slice_in_dim(gwo_by_owner, me, 1, axis=0),
                            (me, 0, 0))]
        scattering[l] = _exchange_start(f"scatter_grads_{l}_start", [gwi, gwb_by_owner, gwo_by_owner], lands,
                                        scatter_sends)
        tie = scattering[l][4]
        dh = _dh_matmul(dproj, w_in_l, f"dh_matmul_{l}")
        dx, lvec = _ln_bwd(dh, xl, scale + tie[0, 0], dres, f"ln_bwd_{l}")
        dmod[l] = jnp.concatenate([lvec[0], lvec[1], mvec[2]])
        norm_grad = bvec[0].reshape(WIDTH // HG_HEAD_DIM, HG_HEAD_DIM).sum(axis=0)
        small_grads[l] = jnp.concatenate([mvec[0], mvec[1], norm_grad, dlb[0], bvec[1:4].reshape(-1)])
    grad_x = dx[None]

    small_vec = jnp.concatenate(dmod + small_grads)
    n_small = small_vec.shape[0]
    small_all = _all_gather_small("gather_small_grads", _pad_rows(small_vec, BLK))
    small_sum = _sum_parts(small_all, "sum_small_grads").reshape(-1)[:n_small]
    dmod_all = small_all.reshape(NDEV, -1)[:, :n_layers * 3 * d].reshape(NDEV, n_layers, 3 * d)

    off = n_layers * 3 * d
    grad_b_mod = small_sum[:off].reshape(n_layers, 3 * d)
    per_layer = 2 * d + HG_HEAD_DIM + WIDTH + 3 * WIDTH
    g_ln_g, g_ln_b, g_norm, g_lbs, g_conv = [], [], [], [], []
    for l in range(n_layers):
        seg = small_sum[off + l * per_layer: off + (l + 1) * per_layer]
        g_ln_g.append(seg[:d])
        g_ln_b.append(seg[d:2 * d])
        g_norm.append(seg[2 * d:2 * d + HG_HEAD_DIM])
        g_lbs.append(seg[2 * d + HG_HEAD_DIM:2 * d + HG_HEAD_DIM + WIDTH])
        g_conv.append(seg[2 * d + HG_HEAD_DIM + WIDTH:].reshape(3, WIDTH))
    grad_ln_g, grad_ln_b = jnp.stack(g_ln_g), jnp.stack(g_ln_b)
    grad_norm = jnp.stack(g_norm)
    _, lbs_vjp = jax.vjp(_lower_bound_table, lower_bounds)
    grad_lower = lbs_vjp(jnp.stack(g_lbs))[0]
    grad_conv = lax.dynamic_slice_in_dim(jnp.stack(g_conv), me * cw_cols, cw_cols, axis=2)

    dmod_mine = lax.dynamic_slice_in_dim(dmod_all, me * cm, cm, axis=2).transpose(1, 0, 2)
    grad_w_mod = _wmod_grad(c_all.T, dmod_mine)

    partials = [_exchange_wait(f"scatter_grads_{l}_wait", scattering[l], grad_w_mod, scatter_sends)
                for l in reversed(range(n_layers))][::-1]
    p_in = [partials[l][0] for l in range(n_layers)]
    p_branch = [partials[l][1].reshape(NDEV, 3 * WIDTH, dsh) for l in range(n_layers)]
    p_out = [partials[l][2] for l in range(n_layers)]

    def adam(parts_list, w, m, v, name):
        shape = w.shape
        cols = shape[-1]
        flat = lambda a: a.reshape(-1, cols)
        outs = _sum_adamw(parts_list, flat(w), flat(m), flat(v), name)
        return [o.reshape(shape) for o in outs]

    r_w_in = adam(p_in, w_in, m_w_in, v_w_in, "adamw_w_in")
    r_w_branch = adam(p_branch, w_branch, m_w_branch, v_w_branch, "adamw_w_branch")
    r_w_out = adam(p_out, w_out, m_w_out, v_w_out, "adamw_w_out")
    r_w_mod = adam([grad_w_mod.reshape(1, -1, cm)], w_mod, m_w_mod, v_w_mod, "adamw_w_mod")

    small_names = ["b_mod", "conv_w", "hgrn_norm_w", "lower_bounds", "ln_g", "ln_b"]
    small_g = [grad_b_mod, grad_conv, grad_norm, grad_lower, grad_ln_g, grad_ln_b]
    small_w = [b_mod, conv_w, hgrn_norm_w, lower_bounds, ln_g, ln_b]
    small_m = [m_b_mod, m_conv_w, m_hgrn_norm_w, m_lower_bounds, m_ln_g, m_ln_b]
    small_v = [v_b_mod, v_conv_w, v_hgrn_norm_w, v_lower_bounds, v_ln_g, v_ln_b]
    pack = lambda arrs: _pad_rows(jnp.concatenate([a.reshape(-1) for a in arrs]), BLK)
    packed = _sum_adamw([pack(small_g)[None]], pack(small_w), pack(small_m), pack(small_v), "adamw_small")
    r_small = {n: [] for n in small_names}
    for res in packed:
        flat = res.reshape(-1)
        pos = 0
        for n, w in zip(small_names, small_w):
            r_small[n].append(flat[pos:pos + w.size].reshape(w.shape))
            pos += w.size

    results = {"w_mod": r_w_mod, "w_in": r_w_in, "w_branch": r_w_branch, "w_out": r_w_out, **r_small}
    order = ["w_mod", "b_mod", "w_in", "conv_w", "hgrn_norm_w", "lower_bounds", "w_branch", "w_out", "ln_g", "ln_b"]
    outs = [loss, grad_x]
    for idx in range(4):
        outs.extend(results[n][idx] for n in order)
    return tuple(outs)
```

```python
import jax
import jax.numpy as jnp
from jax import lax
from jax.experimental import pallas as pl
from jax.experimental.pallas import tpu as pltpu

F32 = jnp.float32
BF16 = jnp.bfloat16
NDEV = 8
N_LAYERS = 2
SB_HEAD_DIM = 64
HG_HEAD_DIM = 128
WIDTH = 512
BLK = 128
LN_EPS = 1e-5
RMS_EPS = 1e-6
ALPHA = (2.0 * N_LAYERS) ** 0.25
ADAM_LR, ADAM_B1, ADAM_B2, ADAM_EPS, ADAM_WD, ADAM_STEP = 0.001, 0.9, 0.999, 1e-08, 0.01, 10
VMEM_LIMIT = 56 * 1024 * 1024
MESH = pl.DeviceIdType.MESH
HG_LEVELS = (64, 32, 16, 8, 4, 2, 1)


def _pcall(body, *, name, out_shape, grid=None, in_specs=None, out_specs=None, scratch_shapes=(),
           semantics=None, aliases=None):
    kwargs = {}
    if grid is not None:
        kwargs["grid"] = grid
    if in_specs is not None:
        kwargs["in_specs"] = in_specs
    if out_specs is not None:
        kwargs["out_specs"] = out_specs
    if aliases:
        kwargs["input_output_aliases"] = aliases
    return pl.pallas_call(
        body, name=name, out_shape=out_shape, scratch_shapes=list(scratch_shapes),
        compiler_params=pltpu.CompilerParams(dimension_semantics=semantics, vmem_limit_bytes=VMEM_LIMIT),
        interpret=False, **kwargs)


def _dot(a, b):
    return jnp.dot(a, b, preferred_element_type=F32)


def _dot_nt(a, b):
    return lax.dot_general(a, b, (((1,), (1,)), ((), ())), preferred_element_type=F32)


def _dot_tn(a, b):
    return lax.dot_general(a, b, (((0,), (0,)), ((), ())), preferred_element_type=F32)


def _split3(x):
    x1 = x.astype(BF16)
    r1 = x - x1.astype(F32)
    x2 = r1.astype(BF16)
    r2 = r1 - x2.astype(F32)
    return x1, x2, r2.astype(BF16)


def _split2(x):
    x1 = x.astype(BF16)
    return x1, (x - x1.astype(F32)).astype(BF16)


def _dot_exact_l(m_bf16, x):
    x1, x2, x3 = _split3(x)
    return _dot(m_bf16, x1) + _dot(m_bf16, x2) + _dot(m_bf16, x3)


def _sigmoid(x):
    return 1.0 / (1.0 + jnp.exp(-x))


def _silu_and_grad(x):
    s = _sigmoid(x)
    return x * s, s * (1.0 + x * (1.0 - s))


LOG2E = 1.4426950408889634
MASKED_SCORE = -1e30


def _softplus2_parts(z2):
    minus_abs = lax.bitcast_convert_type(lax.bitcast_convert_type(z2, jnp.int32) | jnp.int32(-2 ** 31), F32)
    e = jnp.exp2(minus_abs)
    sp2 = jnp.maximum(z2, 0.0) + jnp.log2(1.0 + e)
    r = 1.0 / (1.0 + e)
    return sp2, jnp.where(z2 >= 0.0, r, e * r)


def _split2_lanes(x):
    x1 = x.astype(BF16)
    return jnp.concatenate([x1, (x - x1.astype(F32)).astype(BF16)], axis=1)


def _iota2(shape, dim):
    return lax.broadcasted_iota(jnp.int32, shape, dim)


def _standardize(x):
    mu = jnp.mean(x, axis=-1, keepdims=True)
    xc = x - mu
    var = jnp.mean(xc * xc, axis=-1, keepdims=True)
    rstd = lax.rsqrt(var + LN_EPS)
    return xc * rstd, rstd


def _standardize_bwd(xhat, rstd, dxhat):
    m1 = jnp.mean(dxhat, axis=-1, keepdims=True)
    m2 = jnp.mean(dxhat * xhat, axis=-1, keepdims=True)
    return rstd * (dxhat - m1 - xhat * m2)


def _my_index():
    return 4 * lax.axis_index("x") + 2 * lax.axis_index("y") + lax.axis_index("c")


def _exchange(name, ins, out_shapes, transfers, in_vmem):
    n_in, n_out, n_t = len(ins), len(out_shapes), len(transfers)

    def body(*refs):
        in_refs, out_refs = refs[:n_in], refs[n_in:n_in + n_out]
        send_sems, recv_sems, local_sems = refs[n_in + n_out:]
        x, y, c = lax.axis_index("x"), lax.axis_index("y"), lax.axis_index("c")
        me = 4 * x + 2 * y + c
        started = []
        for t, (i, o, src_fn, dst_fn) in enumerate(transfers):
            own = pltpu.make_async_copy(src_fn(in_refs[i], me), dst_fn(out_refs[o], me), local_sems.at[t])
            own.start()
            started.append(own)
        arrivals = []
        for k in range(1, NDEV):
            px = x ^ ((k >> 2) & 1)
            py = y ^ ((k >> 1) & 1)
            pc = c ^ (k & 1)
            peer = 4 * px + 2 * py + pc
            for t, (i, o, src_fn, dst_fn) in enumerate(transfers):
                sem = t * (NDEV - 1) + k - 1
                push = pltpu.make_async_remote_copy(
                    src_ref=src_fn(in_refs[i], peer), dst_ref=dst_fn(out_refs[o], me),
                    send_sem=send_sems.at[sem], recv_sem=recv_sems.at[sem],
                    device_id=(px, py, pc), device_id_type=MESH)
                push.start()
                started.append(push)
                arrivals.append(pltpu.make_async_remote_copy(
                    src_ref=src_fn(in_refs[i], peer), dst_ref=dst_fn(out_refs[o], peer),
                    send_sem=send_sems.at[sem], recv_sem=recv_sems.at[sem],
                    device_id=(px, py, pc), device_id_type=MESH))
        for arrival in arrivals:
            arrival.wait_recv()
        for cp in started[n_t:]:
            cp.wait_send()
        for own in started[:n_t]:
            own.wait()

    space = pltpu.VMEM if in_vmem else pl.ANY
    spec = pl.BlockSpec(memory_space=space)
    return _pcall(
        body, name=name, out_shape=out_shapes,
        in_specs=[spec] * n_in, out_specs=[spec] * n_out,
        scratch_shapes=[pltpu.SemaphoreType.DMA((n_t * (NDEV - 1),)),
                        pltpu.SemaphoreType.DMA((n_t * (NDEV - 1),)),
                        pltpu.SemaphoreType.DMA((n_t,))])(*ins)


def _whole(ref, dev):
    return ref


def _slot(ref, dev):
    return ref.at[dev]


def _all_gather_small(name, v):
    out = _exchange(name, [v], [jax.ShapeDtypeStruct((NDEV,) + v.shape, v.dtype)],
                    [(0, 0, _whole, _slot)], in_vmem=True)
    return out[0]


_HBM_SPEC = pl.BlockSpec(memory_space=pltpu.HBM)
_SEM_SPEC = pl.BlockSpec(memory_space=pltpu.SEMAPHORE)
_DATAFLOW = pltpu.SideEffectType.DATAFLOW_SIDE_EFFECTING


def _peer(x, y, c, k):
    px = x ^ ((k >> 2) & 1)
    py = y ^ ((k >> 1) & 1)
    pc = c ^ (k & 1)
    return (px, py, pc), 4 * px + 2 * py + pc


def _direct_sends(transfers):
    sends = []
    for k in range(1, NDEV):
        for i, o, src_fn, dst_fn in transfers:
            sends.append((k,
                          lambda ins, lands, me, i=i, k=k, src_fn=src_fn: src_fn(ins[i], me ^ k),
                          lambda lands, me, o=o, dst_fn=dst_fn: dst_fn(lands[o], me),
                          lambda lands, me, o=o, k=k, dst_fn=dst_fn: dst_fn(lands[o], me ^ k)))
    return sends


def _exchange_start(name, ins, lands, sends):
    n_in, n_buf = len(ins), len(ins) + len(lands)
    n_sem = len(sends)

    def body(*refs):
        in_refs, land_refs = refs[:n_in], refs[n_in:n_buf]
        send_sems, recv_sems, token = refs[n_buf], refs[n_buf + 1], refs[-1]
        x, y, c = lax.axis_index("x"), lax.axis_index("y"), lax.axis_index("c")
        me = 4 * x + 2 * y + c
        for t, (k, src_fn, dst_fn, _) in enumerate(sends):
            pltpu.make_async_remote_copy(
                src_ref=src_fn(in_refs, land_refs, me), dst_ref=dst_fn(land_refs, me),
                send_sem=send_sems.at[t], recv_sem=recv_sems.at[t],
                device_id=_peer(x, y, c, k)[0], device_id_type=MESH).start()
        token[...] = jnp.zeros_like(token)

    bufs = [pltpu.with_memory_space_constraint(a, pltpu.HBM) for a in list(ins) + list(lands)]
    outs = pl.pallas_call(
        body, name=name,
        out_shape=(pltpu.SemaphoreType.DMA((n_sem,)), pltpu.SemaphoreType.DMA((n_sem,)))
        + tuple(pltpu.HBM(a.shape, a.dtype) for a in bufs) + (jax.ShapeDtypeStruct((8, BLK), F32),),
        in_specs=[_HBM_SPEC] * n_buf,
        out_specs=(_SEM_SPEC, _SEM_SPEC) + (_HBM_SPEC,) * n_buf + (pl.BlockSpec(memory_space=pltpu.VMEM),),
        input_output_aliases={b: 2 + b for b in range(n_buf)},
        compiler_params=pltpu.CompilerParams(has_side_effects=_DATAFLOW),
        interpret=False)(*bufs)
    return outs[0], outs[1], list(outs[2:2 + n_in]), list(outs[2 + n_in:2 + n_buf]), outs[-1]


def _exchange_wait(name, started, after, sends):
    send_sems, recv_sems, ins, lands, _ = started
    n_in, n_buf = len(ins), len(ins) + len(lands)

    def body(*refs):
        in_refs, land_refs = refs[:n_in], refs[n_in:n_buf]
        send_sems, recv_sems = refs[n_buf], refs[n_buf + 1]
        x, y, c = lax.axis_index("x"), lax.axis_index("y"), lax.axis_index("c")
        me = 4 * x + 2 * y + c
        for t, (k, src_fn, _, rcv_fn) in enumerate(sends):
            cp = pltpu.make_async_remote_copy(
                src_ref=src_fn(in_refs, land_refs, me), dst_ref=rcv_fn(land_refs, me),
                send_sem=send_sems.at[t], recv_sem=recv_sems.at[t],
                device_id=_peer(x, y, c, k)[0], device_id_type=MESH)
            cp.wait_send()
            cp.wait_recv()

    bufs = list(ins) + list(lands)
    outs = pl.pallas_call(
        body, name=name, out_shape=tuple(pltpu.HBM(a.shape, a.dtype) for a in bufs),
        in_specs=[_HBM_SPEC] * n_buf + [_SEM_SPEC, _SEM_SPEC, pl.BlockSpec(memory_space=pl.ANY)],
        out_specs=(_HBM_SPEC,) * n_buf,
        input_output_aliases={b: b for b in range(n_buf)},
        compiler_params=pltpu.CompilerParams(has_side_effects=_DATAFLOW),
        interpret=False)(*bufs, send_sems, recv_sems, after)
    return list(outs[n_in:])


def _place_own(shape, dtype, own, start):
    return lax.dynamic_update_slice(lax.empty(shape, dtype), own, start)


def _mod_fwd(c_all, w_mod, b_mod_mine):
    n_layers, _, cm = w_mod.shape

    def body(c_ref, w_ref, b_ref, o_ref):
        for l in range(n_layers):
            o_ref[l] = jnp.dot(c_ref[...], w_ref[l], preferred_element_type=F32,
                               precision=lax.Precision.HIGHEST) + b_ref[l]

    return _pcall(body, name="mod_fwd", out_shape=jax.ShapeDtypeStruct((n_layers, NDEV, cm), F32))(
        c_all, w_mod, b_mod_mine)


def _ln_proj(x, shift, scale, w_full, name):
    s_len, d = x.shape
    n = w_full.shape[1]
    tm = min(512, s_len)
    tn = 1024

    def body(x_ref, sh_ref, sc_ref, w_ref, proj_ref, ht_ref, h_scr):
        @pl.when(pl.program_id(1) == 0)
        def _():
            xs, _ = _standardize(x_ref[...])
            h = xs * (1.0 + sc_ref[...]) + sh_ref[...]
            h_scr[...] = h.astype(BF16)
            ht_ref[...] = h.T.astype(BF16)

        proj_ref[...] = _dot(h_scr[...], w_ref[...])

    return _pcall(
        body, name=name,
        out_shape=(jax.ShapeDtypeStruct((s_len, n), F32), jax.ShapeDtypeStruct((d, s_len), BF16)),
        grid=(s_len // tm, n // tn),
        in_specs=[pl.BlockSpec((tm, d), lambda i, j: (i, 0)),
                  pl.BlockSpec((1, d), lambda i, j: (0, 0)),
                  pl.BlockSpec((1, d), lambda i, j: (0, 0)),
                  pl.BlockSpec((d, tn), lambda i, j: (0, j))],
        out_specs=(pl.BlockSpec((tm, tn), lambda i, j: (i, j)),
                   pl.BlockSpec((d, tm), lambda i, j: (0, i))),
        scratch_shapes=[pltpu.VMEM((tm, d), BF16)],
        semantics=("arbitrary", "arbitrary"))(x, shift, scale, w_full)


def _sb_group_blocks(nb):
    return min(4, nb)


def _sb_fwd(proj, name):
    s_len = proj.shape[0]
    nb = s_len // BLK
    n_pairs = WIDTH // BLK
    gb = _sb_group_blocks(nb)
    kw = gb * BLK

    def body(q_ref, k_ref, v_ref, o_ref, tot_ref):
        lane = _iota2((1, BLK), 1)
        row = _iota2((BLK, BLK), 0)
        col = _iota2((BLK, BLK), 1)
        half = jnp.concatenate([(row >= col).astype(BF16), jnp.ones((BLK, BLK), BF16)], axis=1)
        suffix_and_sum = jnp.concatenate([half, half], axis=0)
        qpos = _iota2((BLK, kw), 0)
        kpos = _iota2((BLK, kw), 1)
        head_lanes = [(lane // SB_HEAD_DIM) == hh for hh in range(2)]

        def scores(i, gi, qms, masked):
            c0 = pl.multiple_of(gi * kw, kw)
            kb = k_ref[pl.ds(c0, kw), :].astype(BF16)
            z2s = [_dot_nt(qms[hh], kb) for hh in range(2)]
            if masked:
                valid = (c0 + kpos) < (i * BLK + qpos)
                z2s = [jnp.where(valid, z2, MASKED_SCORE) for z2 in z2s]
            return tuple(z2s)

        def accumulate(gi, z2s, carry):
            c0 = pl.multiple_of(gi * kw, kw)
            vf = v_ref[pl.ds(c0, kw), :]
            sp2s = [_softplus2_parts(z2)[0] for z2 in z2s]
            terms = [[_split2_lanes(sp2[:, b * BLK:(b + 1) * BLK]) for b in range(gb)] for sp2 in sp2s]
            sums = [[_dot(t, suffix_and_sum) for t in head_terms] for head_terms in terms]
            weights, laters = [], []
            for hh in range(2):
                later = carry[2 * hh + 1]
                parts = [None] * gb
                for b in reversed(range(gb)):
                    parts[b] = sums[hh][b][:, :BLK] + later
                    later = later + sums[hh][b][:, BLK:]
                weights.append(jnp.exp2(z2s[hh] - jnp.concatenate(parts, axis=1)).astype(BF16))
                laters.append(later)
            outs = [_dot(weights[hh], jnp.where(head_lanes[hh], vf, 0.0).astype(BF16)) for hh in range(2)]
            return (carry[0] + outs[0], laters[0], carry[2] + outs[1], laters[1])

        def queries(i):
            qf = q_ref[pl.ds(pl.multiple_of(i * BLK, BLK), BLK), :] * (SB_HEAD_DIM ** -0.5 * LOG2E)
            return [jnp.where(head_lanes[hh], qf, 0.0).astype(BF16) for hh in range(2)]

        def qblock(i, first_scores):
            r0 = pl.multiple_of(i * BLK, BLK)
            qms = queries(i)
            zero = jnp.zeros((BLK, BLK), F32)
            last = i // gb

            def step(jj, state):
                gi = last - 1 - jj
                return scores(i, gi, qms, False) + accumulate(gi + 1, state[:2], state[2:])

            state = lax.fori_loop(0, last, step, first_scores + (zero,) * 4)
            nxt = jnp.minimum(i + 1, nb - 1)
            next_scores = scores(nxt, nxt // gb, queries(nxt), True)
            carry = accumulate(0, state[:2], state[2:])
            o_ref[pl.ds(r0, BLK), :] = carry[0] + carry[2]
            tot_ref[0, pl.ds(r0, BLK), :] = carry[1]
            tot_ref[1, pl.ds(r0, BLK), :] = carry[3]
            return next_scores

        lax.fori_loop(0, nb, qblock, scores(0, 0, queries(0), True))

    col_spec = lambda off: pl.BlockSpec((s_len, BLK), lambda p: (0, off + p))
    return _pcall(
        body, name=name,
        out_shape=(jax.ShapeDtypeStruct((s_len, WIDTH), F32),
                   jax.ShapeDtypeStruct((2 * n_pairs, s_len, BLK), F32)),
        grid=(n_pairs,),
        in_specs=[col_spec(0), col_spec(n_pairs), col_spec(2 * n_pairs)],
        out_specs=(pl.BlockSpec((s_len, BLK), lambda p: (0, p)),
                   pl.BlockSpec((2, s_len, BLK), lambda p: (p, 0, 0))),
        semantics=("arbitrary",))(proj, proj, proj)


def _hg_masks(mask_ref):
    row = _iota2((BLK, BLK), 0)
    col = _iota2((BLK, BLK), 1)
    for v, m in enumerate(HG_LEVELS):
        same = (row // (2 * m)) == (col // (2 * m))
        mask_ref[v] = (same & ((row & m) != 0) & ((col & m) == 0)).astype(F32)


def _hg_mid(b, m):
    if m >= 4:
        n = BLK // (2 * m)
        mid = b.reshape(n, 2 * m, BLK)[:, m - 1:m, :]
        return jnp.broadcast_to(mid, (n, 2 * m, BLK)).reshape(BLK, BLK)
    pos = _iota2((BLK, BLK), 0) & (2 * m - 1)
    out = b
    for p in range(2 * m):
        delta = (m - 1) - p
        if delta != 0:
            out = jnp.where(pos == p, pltpu.roll(b, (-delta) % BLK, 0), out)
    return out


def _hg_chunk_inputs(qraw, fpre, lb):
    sig = _sigmoid(fpre)
    f = lb + (1.0 - lb) * sig
    g = jnp.log(f)
    q, dq_fac = _silu_and_grad(qraw)
    return q, dq_fac, f, sig, g


def _hg_level_terms(q, k, b, v_idx, m, mask_ref):
    mid = _hg_mid(b, m)
    eq = jnp.exp(jnp.minimum(b - mid, 0.0))
    ek = jnp.exp(jnp.minimum(mid - b, 0.0))
    qt = (q * eq).astype(BF16)
    kt = (k * ek).astype(BF16)
    return qt, kt, eq, ek, mask_ref[v_idx]


def _hg_scores(q, k, b, mask_ref):
    sc = None
    for v_idx, m in enumerate(HG_LEVELS):
        qt, kt, _, _, msk = _hg_level_terms(q, k, b, v_idx, m, mask_ref)
        term = _dot_nt(qt, kt) * msk
        sc = term if sc is None else sc + term
    return sc


def _hgrn_fwd(proj, lb, name):
    s_len = proj.shape[0]
    nc = s_len // BLK
    nh = WIDTH // HG_HEAD_DIM
    base = 4 * WIDTH // BLK

    def body(q_ref, f_ref, i_ref, lb_ref, o_ref, mask_ref):
        _hg_masks(mask_ref)
        row = _iota2((BLK, BLK), 0)
        col = _iota2((BLK, BLK), 1)
        lower_incl = (col <= row).astype(BF16)
        lb_v = lb_ref[...]

        def chunk(ci, st):
            r0 = pl.multiple_of(ci * BLK, BLK)
            q, _, f, _, g = _hg_chunk_inputs(q_ref[pl.ds(r0, BLK), :], f_ref[pl.ds(r0, BLK), :], lb_v)
            k = 1.0 - f
            v = i_ref[pl.ds(r0, BLK), :]
            vb = v.astype(BF16)
            b = _dot_exact_l(lower_incl, g)
            b_end = b[BLK - 1:BLK, :]
            inter = _dot_nt((q * jnp.exp(b)).astype(BF16), st.astype(BF16))
            sc = _hg_scores(q, k, b, mask_ref)
            diag = jnp.sum(q * k, axis=-1, keepdims=True)
            o_ref[pl.ds(r0, BLK), :] = inter + _dot(sc.astype(BF16), vb) + diag * v
            k_dec = (k * jnp.exp(b_end - b)).astype(BF16)
            return st * jnp.exp(b_end) + _dot_tn(vb, k_dec)

        lax.fori_loop(0, nc, chunk, jnp.zeros((HG_HEAD_DIM, HG_HEAD_DIM), F32))

    col_spec = lambda off: pl.BlockSpec((s_len, BLK), lambda h: (0, off + h))
    return _pcall(
        body, name=name, out_shape=jax.ShapeDtypeStruct((s_len, WIDTH), F32),
        grid=(nh,),
        in_specs=[col_spec(base), col_spec(base + nh), col_spec(base + 2 * nh),
                  pl.BlockSpec((1, BLK), lambda h: (0, h))],
        out_specs=pl.BlockSpec((s_len, BLK), lambda h: (0, h)),
        scratch_shapes=[pltpu.VMEM((len(HG_LEVELS), BLK, BLK), F32)],
        semantics=("arbitrary",))(proj, proj, proj, lb)


def _rms_heads(o_b, norm_w):
    n_parts, h_parts, r_parts = [], [], []
    for h in range(WIDTH // HG_HEAD_DIM):
        sl = slice(h * HG_HEAD_DIM, (h + 1) * HG_HEAD_DIM)
        o = o_b[:, sl]
        rstd = lax.rsqrt(jnp.mean(o * o, axis=-1, keepdims=True) + RMS_EPS)
        ohat = o * rstd
        h_parts.append(ohat)
        n_parts.append(ohat * norm_w[:, sl])
        r_parts.append(jnp.broadcast_to(rstd, o.shape))
    cat = lambda parts: jnp.concatenate(parts, axis=-1)
    return cat(n_parts), cat(h_parts), cat(r_parts)


def _shift_rows_down(halo, cur, k):
    tm = cur.shape[0]
    ext = jnp.concatenate([halo, cur], axis=0)
    return pltpu.roll(ext, k, 0)[8:8 + tm]


def _shift_rows_up(cur, halo, k):
    tm = cur.shape[0]
    ext = jnp.concatenate([cur, halo], axis=0)
    return pltpu.roll(ext, (tm + 8 - k) % (tm + 8), 0)[0:tm]


def _merge_fwd(x, proj, o_a, o_b, gate, norm_w, conv_w, wb, w_out, ln_g, ln_b, name):
    s_len, d = x.shape
    tm = min(256, s_len)
    hb = tm // 8

    def body(x_ref, oa_ref, za_ref, ob_ref, zb_ref, pre_ref, post_ref, u_ref, zc_ref, hpre_ref, hu_ref, g_ref,
             gate_ref, nw_ref, cw_ref, wb_ref, wo_ref, lg_ref, lbias_ref, xn_ref, mg_ref, yc_ref):
        i = pl.program_id(0)
        sa, _ = _silu_and_grad(za_ref[...])
        y_a = (oa_ref[...] * sa).astype(BF16)
        n_b, _, _ = _rms_heads(ob_ref[...], nw_ref[...])
        sb, _ = _silu_and_grad(zb_ref[...])
        y_b = (n_b * sb).astype(BF16)
        a = pre_ref[...] * u_ref[...]
        halo = jnp.where(i > 0, hpre_ref[...] * hu_ref[...], 0.0)
        cw = cw_ref[...]
        conv = cw[0:1] * _shift_rows_down(halo, a, 2) + cw[1:2] * _shift_rows_down(halo, a, 1) + cw[2:3] * a
        sc, _ = _silu_and_grad(zc_ref[...])
        y_c = (post_ref[...] * conv * sc).astype(BF16)
        merged = None
        for k, yk in enumerate((y_a, y_b, y_c)):
            yc_ref[:, k * WIDTH:(k + 1) * WIDTH] = yk
            term = _sigmoid(g_ref[:, k * d:(k + 1) * d]) * _dot(yk, wb_ref[k])
            merged = term if merged is None else merged + term
        mb = merged.astype(BF16)
        mg_ref[...] = mb
        y = _dot(mb, wo_ref[...])
        r = ALPHA * x_ref[...] + (1.0 + gate_ref[...]) * y
        rhat, _ = _standardize(r)
        xn_ref[...] = rhat * lg_ref[...] + lbias_ref[...]

    wcol = lambda cb: pl.BlockSpec((tm, WIDTH), lambda i: (i, cb))
    halo_spec = lambda cb: pl.BlockSpec((8, WIDTH), lambda i: (jnp.maximum(i * hb - 1, 0), cb))
    vec = lambda w: pl.BlockSpec((1, w), lambda i: (0, 0))
    return _pcall(
        body, name=name,
        out_shape=(jax.ShapeDtypeStruct((s_len, d), F32), jax.ShapeDtypeStruct((s_len, d), BF16),
                   jax.ShapeDtypeStruct((s_len, 3 * WIDTH), BF16)),
        grid=(s_len // tm,),
        in_specs=[pl.BlockSpec((tm, d), lambda i: (i, 0)),
                  wcol(0), wcol(3), wcol(0), wcol(7), wcol(8), wcol(9), wcol(10), wcol(11),
                  halo_spec(8), halo_spec(10),
                  pl.BlockSpec((tm, 3 * d), lambda i: (i, 2)),
                  vec(d), vec(WIDTH),
                  pl.BlockSpec((3, WIDTH), lambda i: (0, 0)),
                  pl.BlockSpec((3, WIDTH, d), lambda i: (0, 0, 0)),
                  pl.BlockSpec((d, d), lambda i: (0, 0)),
                  vec(d), vec(d)],
        out_specs=(pl.BlockSpec((tm, d), lambda i: (i, 0)), pl.BlockSpec((tm, d), lambda i: (i, 0)),
                   pl.BlockSpec((tm, 3 * WIDTH), lambda i: (i, 0))),
        semantics=("arbitrary",))(x, o_a, proj, o_b, proj, proj, proj, proj, proj, proj, proj, proj,
                                  gate, norm_w, conv_w, wb, w_out, ln_g, ln_b)


def _loss_fwd_bwd(y, target):
    s_len, d = y.shape
    tm = min(512, s_len)

    def body(y_ref, t_ref, loss_ref, dy_ref):
        @pl.when(pl.program_id(0) == 0)
        def _():
            loss_ref[...] = jnp.zeros_like(loss_ref)

        e = y_ref[...] - t_ref[...]
        dy_ref[...] = e * (1.0 / d)
        part = jnp.sum(jnp.sum(e * e, axis=-1, keepdims=True), axis=0, keepdims=True)
        loss_ref[...] += part * (0.5 / d)

    tile = pl.BlockSpec((tm, d), lambda i: (i, 0))
    return _pcall(body, name="loss", grid=(s_len // tm,),
                  out_shape=(jax.ShapeDtypeStruct((1, 1), F32), jax.ShapeDtypeStruct((s_len, d), F32)),
                  in_specs=[tile, tile],
                  out_specs=(pl.BlockSpec((1, 1), lambda i: (0, 0)), tile),
                  semantics=("arbitrary",))(y, target)


def _merge_bwd(dxn, x, merged, ycat, proj, gate, wb, w_out, ln_g, name):
    s_len, d = x.shape
    tm = min(256, s_len)

    def body(dxn_ref, x_ref, mg_ref, yc_ref, g_ref, gate_ref, wb_ref, wo_ref, lg_ref,
             dres_ref, dyc_ref, dg_ref, gwo_ref, gwb_ref, vec_ref):
        @pl.when(pl.program_id(0) == 0)
        def _():
            gwo_ref[...] = jnp.zeros_like(gwo_ref)
            gwb_ref[...] = jnp.zeros_like(gwb_ref)
            vec_ref[...] = jnp.zeros_like(vec_ref)

        mb = mg_ref[...]
        one_gate = 1.0 + gate_ref[...]
        y = _dot(mb, wo_ref[...])
        r = ALPHA * x_ref[...] + one_gate * y
        rhat, rstd = _standardize(r)
        dxn = dxn_ref[...]
        dr = _standardize_bwd(rhat, rstd, dxn * lg_ref[...])
        vec_ref[0:1, :] += jnp.sum(dxn * rhat, axis=0, keepdims=True)
        vec_ref[1:2, :] += jnp.sum(dxn, axis=0, keepdims=True)
        vec_ref[2:3, :] += jnp.sum(dr * y, axis=0, keepdims=True)
        dres_ref[...] = ALPHA * dr
        dy = (one_gate * dr).astype(BF16)
        gwo_ref[...] += _dot_tn(mb, dy)
        dmerged = _dot_nt(dy, wo_ref[...])
        for k in range(3):
            yk = yc_ref[:, k * WIDTH:(k + 1) * WIDTH]
            sg = _sigmoid(g_ref[:, k * d:(k + 1) * d])
            pk = _dot(yk, wb_ref[k])
            dg_ref[:, k * d:(k + 1) * d] = (dmerged * pk * sg * (1.0 - sg)).astype(BF16)
            dpk = (dmerged * sg).astype(BF16)
            dyc_ref[:, k * WIDTH:(k + 1) * WIDTH] = _dot_nt(dpk, wb_ref[k])
            gwb_ref[k] += _dot_tn(yk, dpk)

    tile = lambda w: pl.BlockSpec((tm, w), lambda i: (i, 0))
    vec = pl.BlockSpec((1, d), lambda i: (0, 0))
    return _pcall(
        body, name=name,
        out_shape=(jax.ShapeDtypeStruct((s_len, d), F32), jax.ShapeDtypeStruct((s_len, 3 * WIDTH), F32),
                   jax.ShapeDtypeStruct(proj.shape, BF16), jax.ShapeDtypeStruct((d, d), F32),
                   jax.ShapeDtypeStruct((3, WIDTH, d), F32), jax.ShapeDtypeStruct((8, d), F32)),
        grid=(s_len // tm,),
        in_specs=[tile(d), tile(d), tile(d), tile(3 * WIDTH),
                  pl.BlockSpec((tm, 3 * d), lambda i: (i, 2)),
                  vec, pl.BlockSpec((3, WIDTH, d), lambda i: (0, 0, 0)),
                  pl.BlockSpec((d, d), lambda i: (0, 0)), vec],
        out_specs=(tile(d), tile(3 * WIDTH), pl.BlockSpec((tm, 3 * d), lambda i: (i, 2)),
                   pl.BlockSpec((d, d), lambda i: (0, 0)),
                   pl.BlockSpec((3, WIDTH, d), lambda i: (0, 0, 0)),
                   pl.BlockSpec((8, d), lambda i: (0, 0))),
        semantics=("arbitrary",))(dxn, x, merged, ycat, proj, gate, wb, w_out, ln_g)


def _branch_bwd(dycat, proj, o_a, o_b, norm_w, conv_w, dproj, name):
    s_len = proj.shape[0]
    tm = min(256, s_len)
    hb = tm // 8
    n_tiles = s_len // tm

    def body(dya_ref, dyb_ref, dyc_ref, oa_ref, za_ref, ob_ref, zb_ref, pre_ref, post_ref, u_ref, zc_ref,
             hpre_ref, hu_ref, ndyc_ref, npost_ref, nzc_ref, nw_ref, cw_ref, dproj_in,
             dproj_ref, doa_ref, dob_ref, vec_ref, dza_scr, dzb_scr, dc_scr, sems):
        del dproj_in
        i = pl.program_id(0)

        @pl.when(i == 0)
        def _():
            vec_ref[...] = jnp.zeros_like(vec_ref)

        sa, dsa = _silu_and_grad(za_ref[...])
        dya = dya_ref[...]
        doa_ref[...] = dya * sa
        dza_scr[...] = (dya * oa_ref[...] * dsa).astype(BF16)
        nw = nw_ref[...]
        n_b, ohat, rstd = _rms_heads(ob_ref[...], nw)
        sb, dsb = _silu_and_grad(zb_ref[...])
        dyb = dyb_ref[...]
        dzb_scr[...] = (dyb * n_b * dsb).astype(BF16)
        dn = dyb * sb
        vec_ref[0:1, :] += jnp.sum(dn * ohat, axis=0, keepdims=True)
        dnw = dn * nw
        parts = []
        for h in range(WIDTH // HG_HEAD_DIM):
            sl = slice(h * HG_HEAD_DIM, (h + 1) * HG_HEAD_DIM)
            m2 = jnp.mean(dnw[:, sl] * ohat[:, sl], axis=-1, keepdims=True)
            parts.append(rstd[:, sl] * (dnw[:, sl] - ohat[:, sl] * m2))
        dob_ref[...] = jnp.concatenate(parts, axis=-1)
        cw = cw_ref[...]
        pre, u, post = pre_ref[...], u_ref[...], post_ref[...]
        a = pre * u
        halo = jnp.where(i > 0, hpre_ref[...] * hu_ref[...], 0.0)
        a1 = _shift_rows_down(halo, a, 1)
        a2 = _shift_rows_down(halo, a, 2)
        conv = cw[0:1] * a2 + cw[1:2] * a1 + cw[2:3] * a
        sc, dsc = _silu_and_grad(zc_ref[...])
        dyc = dyc_ref[...]
        dconv = dyc * post * sc
        nsc, _ = _silu_and_grad(nzc_ref[...])
        nxt = jnp.where(i < n_tiles - 1, ndyc_ref[...] * npost_ref[...] * nsc, 0.0)
        da = cw[2:3] * dconv + cw[1:2] * _shift_rows_up(dconv, nxt, 1) + cw[0:1] * _shift_rows_up(dconv, nxt, 2)
        dc_scr[:, 0 * WIDTH:1 * WIDTH] = (da * u).astype(BF16)
        dc_scr[:, 1 * WIDTH:2 * WIDTH] = (dyc * conv * sc).astype(BF16)
        dc_scr[:, 2 * WIDTH:3 * WIDTH] = (da * pre).astype(BF16)
        dc_scr[:, 3 * WIDTH:4 * WIDTH] = (dyc * post * conv * dsc).astype(BF16)
        vec_ref[1:2, :] += jnp.sum(dconv * a2, axis=0, keepdims=True)
        vec_ref[2:3, :] += jnp.sum(dconv * a1, axis=0, keepdims=True)
        vec_ref[3:4, :] += jnp.sum(dconv * a, axis=0, keepdims=True)
        rows = pl.ds(pl.multiple_of(i * tm, tm), tm)
        copies = [pltpu.make_async_copy(dza_scr, dproj_ref.at[rows, 3 * WIDTH:4 * WIDTH], sems.at[0]),
                  pltpu.make_async_copy(dzb_scr, dproj_ref.at[rows, 7 * WIDTH:8 * WIDTH], sems.at[1]),
                  pltpu.make_async_copy(dc_scr, dproj_ref.at[rows, 8 * WIDTH:12 * WIDTH], sems.at[2])]
        for cp in copies:
            cp.start()
        for cp in copies:
            cp.wait()

    wcol = lambda cb: pl.BlockSpec((tm, WIDTH), lambda i: (i, cb))
    prev = lambda cb: pl.BlockSpec((8, WIDTH), lambda i: (jnp.maximum(i * hb - 1, 0), cb))
    nxt = lambda cb: pl.BlockSpec((8, WIDTH), lambda i: (jnp.minimum((i + 1) * hb, s_len // 8 - 1), cb))
    anyspec = pl.BlockSpec(memory_space=pl.ANY)
    out = jax.ShapeDtypeStruct((s_len, WIDTH), F32)
    return _pcall(
        body, name=name,
        out_shape=(jax.ShapeDtypeStruct(dproj.shape, dproj.dtype), out, out, jax.ShapeDtypeStruct((8, WIDTH), F32)),
        grid=(n_tiles,),
        in_specs=[wcol(0), wcol(1), wcol(2), wcol(0), wcol(3), wcol(0), wcol(7), wcol(8), wcol(9), wcol(10), wcol(11),
                  prev(8), prev(10), nxt(2), nxt(9), nxt(11),
                  pl.BlockSpec((1, WIDTH), lambda i: (0, 0)), pl.BlockSpec((3, WIDTH), lambda i: (0, 0)), anyspec],
        out_specs=(anyspec, wcol(0), wcol(0), pl.BlockSpec((8, WIDTH), lambda i: (0, 0))),
        scratch_shapes=[pltpu.VMEM((tm, WIDTH), BF16), pltpu.VMEM((tm, WIDTH), BF16),
                        pltpu.VMEM((tm, 4 * WIDTH), BF16), pltpu.SemaphoreType.DMA((3,))],
        aliases={18: 0},
        semantics=("arbitrary",))(dycat, dycat, dycat, o_a, proj, o_b, proj, proj, proj, proj, proj,
                                  proj, proj, dycat, proj, proj, norm_w, conv_w, dproj)


def _sb_bwd(proj, do_a, totals, dproj, name):
    s_len = proj.shape[0]
    nb = s_len // BLK
    n_pairs = WIDTH // BLK
    scale = SB_HEAD_DIM ** -0.5
    gb = _sb_group_blocks(nb)
    kw = gb * BLK

    def body(q_ref, k_ref, v_ref, do_ref, tot_ref, dproj_in, dproj_ref, dq_ref, dk_ref, dv_ref, out_scr, sems):
        del dproj_in
        lane = _iota2((1, BLK), 1)
        row = _iota2((BLK, BLK), 0)
        col = _iota2((BLK, BLK), 1)
        ones = jnp.ones((BLK, BLK), BF16)
        twice = lambda m: jnp.concatenate([m, m], axis=0)
        before_and_sum = twice(jnp.concatenate([(row < col).astype(BF16), ones], axis=1))
        upto_and_sum = twice(jnp.concatenate([(row <= col).astype(BF16), ones], axis=1))
        qpos = _iota2((BLK, kw), 0)
        kpos = _iota2((BLK, kw), 1)
        head_lanes = [(lane // SB_HEAD_DIM) == hh for hh in range(2)]
        dk_ref[...] = jnp.zeros_like(dk_ref)
        dv_ref[...] = jnp.zeros_like(dv_ref)

        causal = kpos - qpos

        def scores(i, gi, qms):
            c0 = pl.multiple_of(gi * kw, kw)
            kb = k_ref[pl.ds(c0, kw), :].astype(BF16)
            valid = causal < i * BLK - c0
            return tuple(jnp.where(valid, _dot_nt(qms[hh], kb), MASKED_SCORE) for hh in range(2))

        def process(gi, z2s, qms, doms, totals_i, carry):
            c0 = pl.multiple_of(gi * kw, kw)
            kf = k_ref[pl.ds(c0, kw), :]
            vf = v_ref[pl.ds(c0, kw), :]
            kms = [jnp.where(head_lanes[hh], kf, 0.0).astype(BF16) for hh in range(2)]
            vms = [jnp.where(head_lanes[hh], vf, 0.0).astype(BF16) for hh in range(2)]
            das = [_dot_nt(doms[hh], vms[hh]) for hh in range(2)]
            halves = [_softplus2_parts(z2) for z2 in z2s]
            terms = [[_split2_lanes(sp2[:, b * BLK:(b + 1) * BLK]) for b in range(gb)] for sp2, _ in halves]
            sums = [[_dot(t, before_and_sum) for t in head_terms] for head_terms in terms]
            weights, gmats, l_befores = [], [], []
            for hh in range(2):
                l_before = carry[3 * hh + 1]
                parts = []
                for b in range(gb):
                    parts.append(totals_i[hh] - l_before - sums[hh][b][:, :BLK])
                    l_before = l_before + sums[hh][b][:, BLK:]
                a = jnp.exp2(z2s[hh] - jnp.concatenate(parts, axis=1))
                weights.append(a.astype(BF16))
                gmats.append(a * das[hh])
                l_befores.append(l_before)
            terms = [[_split2_lanes(g[:, b * BLK:(b + 1) * BLK]) for b in range(gb)] for g in gmats]
            sums = [[_dot(t, upto_and_sum) for t in head_terms] for head_terms in terms]
            dzs, g_befores = [], []
            for hh in range(2):
                g_before = carry[3 * hh + 2]
                parts = []
                for b in range(gb):
                    parts.append(g_before + sums[hh][b][:, :BLK])
                    g_before = g_before + sums[hh][b][:, BLK:]
                dzs.append((gmats[hh] - halves[hh][1] * jnp.concatenate(parts, axis=1)).astype(BF16))
                g_befores.append(g_before)
            dks = [_dot_tn(dzs[hh], qms[hh]) for hh in range(2)]
            dvs = [_dot_tn(weights[hh], doms[hh]) for hh in range(2)]
            dqs = [_dot(dzs[hh], kms[hh]) for hh in range(2)]
            dk_ref[pl.ds(c0, kw), :] += (dks[0] + dks[1]) * (1.0 / LOG2E)
            dv_ref[pl.ds(c0, kw), :] += dvs[0] + dvs[1]
            return (carry[0] + dqs[0], l_befores[0], g_befores[0], carry[3] + dqs[1], l_befores[1], g_befores[1])

        def queries(i):
            qf = q_ref[pl.ds(pl.multiple_of(i * BLK, BLK), BLK), :] * (scale * LOG2E)
            return [jnp.where(head_lanes[hh], qf, 0.0).astype(BF16) for hh in range(2)]

        def qblock(i, first_scores):
            r0 = pl.multiple_of(i * BLK, BLK)
            qms = queries(i)
            dof = do_ref[pl.ds(r0, BLK), :]
            doms = [jnp.where(head_lanes[hh], dof, 0.0).astype(BF16) for hh in range(2)]
            totals_i = [tot_ref[hh, pl.ds(r0, BLK), :] for hh in range(2)]
            zero = jnp.zeros((BLK, BLK), F32)
            last = i // gb

            def step(gi, state):
                return scores(i, gi + 1, qms) + process(gi, state[:2], qms, doms, totals_i, state[2:])

            state = lax.fori_loop(0, last, step, first_scores + (zero,) * 6)
            nxt = jnp.minimum(i + 1, nb - 1)
            next_scores = scores(nxt, 0, queries(nxt))
            carry = process(last, state[:2], qms, doms, totals_i, state[2:])
            dq_ref[pl.ds(r0, BLK), :] = (carry[0] + carry[3]) * scale
            return next_scores

        lax.fori_loop(0, nb, qblock, scores(0, 0, queries(0)))
        pair = pl.program_id(0)
        copies = []
        for t, ref in enumerate((dq_ref, dk_ref, dv_ref)):
            out_scr[t] = ref[...].astype(BF16)
            col = pl.multiple_of((t * n_pairs + pair) * BLK, BLK)
            copies.append(pltpu.make_async_copy(out_scr.at[t], dproj_ref.at[:, pl.ds(col, BLK)], sems.at[t]))
            copies[-1].start()
        for cp in copies:
            cp.wait()

    col_spec = lambda off: pl.BlockSpec((s_len, BLK), lambda p: (0, off + p))
    anyspec = pl.BlockSpec(memory_space=pl.ANY)
    return _pcall(
        body, name=name, out_shape=jax.ShapeDtypeStruct(dproj.shape, dproj.dtype), grid=(n_pairs,),
        in_specs=[col_spec(0), col_spec(n_pairs), col_spec(2 * n_pairs), col_spec(0),
                  pl.BlockSpec((2, s_len, BLK), lambda p: (p, 0, 0)), anyspec],
        out_specs=anyspec,
        scratch_shapes=[pltpu.VMEM((s_len, BLK), F32)] * 3 + [pltpu.VMEM((3, s_len, BLK), BF16),
                                                              pltpu.SemaphoreType.DMA((3,))],
        aliases={5: 0},
        semantics=("arbitrary",))(proj, proj, proj, do_a, totals, dproj)


def _hgrn_bwd(proj, do_b, lb, dproj, name):
    s_len = proj.shape[0]
    nc = s_len // BLK
    nh = WIDTH // HG_HEAD_DIM
    base = 4 * WIDTH // BLK

    def body(q_ref, f_ref, i_ref, do_ref, lb_ref, dproj_in, dproj_ref, dlb_ref, mask_ref, st_ref, out_scr, sems):
        del dproj_in
        _hg_masks(mask_ref)
        row = _iota2((BLK, BLK), 0)
        col = _iota2((BLK, BLK), 1)
        lower_incl = (col <= row).astype(BF16)
        upper_incl = (col >= row).astype(BF16)
        lb_v = lb_ref[...]

        def load(ci):
            r0 = pl.multiple_of(ci * BLK, BLK)
            q, dq_fac, f, sig, g = _hg_chunk_inputs(q_ref[pl.ds(r0, BLK), :], f_ref[pl.ds(r0, BLK), :], lb_v)
            b = _dot_exact_l(lower_incl, g)
            return r0, q, dq_fac, f, sig, b, i_ref[pl.ds(r0, BLK), :]

        def fwd_chunk(ci, st):
            st_ref[ci] = st
            _, _, _, f, _, b, v = load(ci)
            b_end = b[BLK - 1:BLK, :]
            k_dec = ((1.0 - f) * jnp.exp(b_end - b)).astype(BF16)
            return st * jnp.exp(b_end) + _dot_tn(v.astype(BF16), k_dec)

        lax.fori_loop(0, nc, fwd_chunk, jnp.zeros((HG_HEAD_DIM, HG_HEAD_DIM), F32))

        def bwd_chunk(cc, carry):
            dst, suffix, dlb = carry
            ci = nc - 1 - cc
            r0, q, dq_fac, f, sig, b, v = load(ci)
            k = 1.0 - f
            vb = v.astype(BF16)
            do = do_ref[pl.ds(r0, BLK), :]
            dob = do.astype(BF16)
            b_end = b[BLK - 1:BLK, :]
            e_q = jnp.exp(b)
            e_k = jnp.exp(b_end - b)
            qe = (q * e_q).astype(BF16)
            kh = (k * e_k).astype(BF16)
            st1, st2 = _split2(st_ref[ci])
            ds1, ds2 = _split2(dst)
            dqe = _dot(dob, st1) + _dot(dob, st2)
            dkh = _dot(vb, ds1) + _dot(vb, ds2)
            dq = e_q * dqe
            dk = e_k * dkh
            dv = _dot_nt(kh, ds1)
            dst_new = dst * jnp.exp(b_end) + _dot_tn(dob, qe)
            dlog = qe.astype(F32) * dqe - kh.astype(F32) * dkh
            da = _dot_nt(dob, vb)
            sc = None
            for v_idx, m in enumerate(HG_LEVELS):
                qm, km, eq, ek, msk = _hg_level_terms(q, k, b, v_idx, m, mask_ref)
                term = _dot_nt(qm, km) * msk
                sc = term if sc is None else sc + term
                pm = (da * msk).astype(BF16)
                dqm = _dot(pm, km)
                dkm = _dot_tn(pm, qm)
                dq = dq + dqm * eq
                dk = dk + dkm * ek
                dlog = dlog + (qm.astype(F32) * dqm - km.astype(F32) * dkm)
            a_diag = jnp.sum(do * v, axis=-1, keepdims=True)
            s_diag = jnp.sum(q * k, axis=-1, keepdims=True)
            dq = dq + a_diag * k
            dk = dk + a_diag * q
            dv = dv + _dot_tn(sc.astype(BF16), dob) + s_diag * do
            dg = _dot_exact_l(upper_incl, dlog) + suffix
            dfull = dg / f - dk
            out_scr[0, pl.ds(r0, BLK), :] = (dq * dq_fac).astype(BF16)
            out_scr[1, pl.ds(r0, BLK), :] = (dfull * (1.0 - lb_v) * sig * (1.0 - sig)).astype(BF16)
            out_scr[2, pl.ds(r0, BLK), :] = dv.astype(BF16)
            dlb = dlb + jnp.sum(dfull * (1.0 - sig), axis=0, keepdims=True)
            return dst_new, dg[0:1, :], dlb

        zero_row = jnp.zeros((1, BLK), F32)
        _, _, dlb = lax.fori_loop(0, nc, bwd_chunk,
                                  (jnp.zeros((HG_HEAD_DIM, HG_HEAD_DIM), F32), zero_row, zero_row))
        dlb_ref[...] = jnp.broadcast_to(dlb, dlb_ref.shape)
        head = pl.program_id(0)
        copies = []
        for t in range(3):
            col = pl.multiple_of((base + t * nh + head) * BLK, BLK)
            copies.append(pltpu.make_async_copy(out_scr.at[t], dproj_ref.at[:, pl.ds(col, BLK)], sems.at[t]))
            copies[-1].start()
        for cp in copies:
            cp.wait()

    col_spec = lambda off: pl.BlockSpec((s_len, BLK), lambda h: (0, off + h))
    anyspec = pl.BlockSpec(memory_space=pl.ANY)
    return _pcall(
        body, name=name,
        out_shape=(jax.ShapeDtypeStruct(dproj.shape, dproj.dtype), jax.ShapeDtypeStruct((8, WIDTH), F32)),
        grid=(nh,),
        in_specs=[col_spec(base), col_spec(base + nh), col_spec(base + 2 * nh), col_spec(0),
                  pl.BlockSpec((1, BLK), lambda h: (0, h)), anyspec],
        out_specs=(anyspec, pl.BlockSpec((8, BLK), lambda h: (0, h))),
        scratch_shapes=[pltpu.VMEM((len(HG_LEVELS), BLK, BLK), F32),
                        pltpu.VMEM((nc, HG_HEAD_DIM, HG_HEAD_DIM), F32),
                        pltpu.VMEM((3, s_len, BLK), BF16), pltpu.SemaphoreType.DMA((3,))],
        aliases={5: 0},
        semantics=("arbitrary",))(proj, proj, proj, do_b, lb, dproj)


def _dh_matmul(dproj, w_full, name):
    s_len, n = dproj.shape
    d = w_full.shape[0]
    tm = min(512, s_len)
    tk = 1536

    def body(dp_ref, w_ref, dh_ref):
        part = _dot_nt(dp_ref[...], w_ref[...])

        @pl.when(pl.program_id(1) == 0)
        def _():
            dh_ref[...] = part

        @pl.when(pl.program_id(1) > 0)
        def _():
            dh_ref[...] += part

    return _pcall(
        body, name=name, out_shape=jax.ShapeDtypeStruct((s_len, d), F32),
        grid=(s_len // tm, n // tk),
        in_specs=[pl.BlockSpec((tm, tk), lambda i, k: (i, k)), pl.BlockSpec((d, tk), lambda i, k: (0, k))],
        out_specs=pl.BlockSpec((tm, d), lambda i, k: (i, 0)),
        semantics=("arbitrary", "arbitrary"))(dproj, w_full)


def _gw_matmul(h_t, dproj, name):
    d, s_len = h_t.shape
    n = dproj.shape[1]
    tn = 1152

    def body(ht_ref, dp_ref, gw_ref):
        gw_ref[...] = _dot(ht_ref[...], dp_ref[...]).astype(BF16)

    return _pcall(
        body, name=name, out_shape=jax.ShapeDtypeStruct((d, n), BF16),
        grid=(n // tn,),
        in_specs=[pl.BlockSpec((d, s_len), lambda j: (0, 0)), pl.BlockSpec((s_len, tn), lambda j: (0, j))],
        out_specs=pl.BlockSpec((d, tn), lambda j: (0, j)),
        semantics=("arbitrary",))(h_t, dproj)


def _ln_bwd(dh, x, scale, dres, name):
    s_len, d = x.shape
    tm = min(512, s_len)

    def body(dh_ref, x_ref, sc_ref, dres_ref, dx_ref, vec_ref):
        @pl.when(pl.program_id(0) == 0)
        def _():
            vec_ref[...] = jnp.zeros_like(vec_ref)

        dh = dh_ref[...]
        xs, rstd = _standardize(x_ref[...])
        vec_ref[0:1, :] += jnp.sum(dh, axis=0, keepdims=True)
        vec_ref[1:2, :] += jnp.sum(dh * xs, axis=0, keepdims=True)
        dx_ref[...] = _standardize_bwd(xs, rstd, dh * (1.0 + sc_ref[...])) + dres_ref[...]

    tile = pl.BlockSpec((tm, d), lambda i: (i, 0))
    return _pcall(body, name=name, grid=(s_len // tm,),
                  out_shape=(jax.ShapeDtypeStruct((s_len, d), F32), jax.ShapeDtypeStruct((8, d), F32)),
                  in_specs=[tile, tile, pl.BlockSpec((1, d), lambda i: (0, 0)), tile],
                  out_specs=(tile, pl.BlockSpec((8, d), lambda i: (0, 0))),
                  semantics=("arbitrary",))(dh, x, scale, dres)


def _wmod_grad(c_t, dmod):
    d = c_t.shape[0]
    n_layers, _, cm = dmod.shape

    def body(c_ref, dm_ref, o_ref):
        for l in range(n_layers):
            acc = None
            for b in range(NDEV):
                term = c_ref[:, b:b + 1] * dm_ref[l, b:b + 1, :]
                acc = term if acc is None else acc + term
            o_ref[l] = acc

    return _pcall(body, name="wmod_grad", out_shape=jax.ShapeDtypeStruct((n_layers, d, cm), F32))(c_t, dmod)


def _sum_adamw(parts_list, w, m, v, name):
    n_ranges = len(parts_list)
    n_src, range_rows, cols = parts_list[0].shape
    rows = range_rows * n_ranges
    tr = range_rows
    for cand in (512, 256, 128, 64, 32, 16, 8):
        if range_rows % cand == 0 and cand * cols * 4 <= (2 << 20):
            tr = cand
            break
    tiles = range_rows // tr

    def body(*refs):
        p_refs = refs[:n_ranges]
        w_ref, m_ref, v_ref, g_ref, d_ref, nm_ref, nv_ref = refs[n_ranges:]

        def step(p_ref):
            g = p_ref[0].astype(F32)
            for s in range(1, n_src):
                g = g + p_ref[s].astype(F32)
            nm = ADAM_B1 * m_ref[...] + (1.0 - ADAM_B1) * g
            nv = ADAM_B2 * v_ref[...] + (1.0 - ADAM_B2) * (g * g)
            m_hat = nm / (1.0 - ADAM_B1 ** ADAM_STEP)
            v_hat = nv / (1.0 - ADAM_B2 ** ADAM_STEP)
            g_ref[...] = g
            d_ref[...] = -ADAM_LR * (m_hat / (jnp.sqrt(v_hat) + ADAM_EPS) + ADAM_WD * w_ref[...])
            nm_ref[...] = nm
            nv_ref[...] = nv

        if n_ranges == 1:
            step(p_refs[0])
        else:
            for j in range(n_ranges):
                @pl.when(pl.program_id(0) // tiles == j)
                def _(j=j):
                    step(p_refs[j])

    def part_spec(j):
        return pl.BlockSpec((n_src, tr, cols), lambda i: (0, jnp.clip(i - j * tiles, 0, tiles - 1), 0))

    tile = pl.BlockSpec((tr, cols), lambda i: (i, 0))
    out = jax.ShapeDtypeStruct((rows, cols), F32)
    return _pcall(body, name=name, grid=(rows // tr,), out_shape=(out,) * 4,
                  in_specs=[part_spec(j) for j in range(n_ranges)] + [tile, tile, tile],
                  out_specs=(tile,) * 4, semantics=("arbitrary",))(*parts_list, w, m, v)


def _sum_parts(parts, name):
    n_src = parts.shape[0]

    def body(p_ref, o_ref):
        acc = p_ref[0]
        for s in range(1, n_src):
            acc = acc + p_ref[s]
        o_ref[...] = acc

    return _pcall(body, name=name, out_shape=jax.ShapeDtypeStruct(parts.shape[1:], F32))(parts)


def _lower_bound_table(lower_bounds):
    p = jax.nn.softmax(lower_bounds.astype(F32), axis=0)
    return jnp.cumsum(p, axis=0) - p[0:1]


def _pad_rows(v, width):
    n = v.shape[0]
    rows = -(-n // width)
    rows = -(-rows // 8) * 8
    return jnp.pad(v, (0, rows * width - n)).reshape(rows, width)


def kernel(x, c, w_mod, b_mod, w_in, conv_w, hgrn_norm_w, lower_bounds, w_branch, w_out, ln_g, ln_b, loss_target, m_w_mod, m_b_mod, m_w_in, m_conv_w, m_hgrn_norm_w, m_lower_bounds, m_w_branch, m_w_out, m_ln_g, m_ln_b, v_w_mod, v_b_mod, v_w_in, v_conv_w, v_hgrn_norm_w, v_lower_bounds, v_w_branch, v_w_out, v_ln_g, v_ln_b):
    n_layers = N_LAYERS
    s_len, d = x.shape[1], x.shape[2]
    n_cols = w_in.shape[2] * NDEV
    cw_cols = conv_w.shape[2]
    cm = w_mod.shape[2]
    me = _my_index()
    x0 = x[0]
    target = loss_target[0]

    small = _pad_rows(jnp.concatenate([c.reshape(-1), conv_w.reshape(-1)]), BLK)
    small_all = _all_gather_small("gather_c_conv", small).reshape(NDEV, -1)
    c_all = small_all[:, :d]
    conv_full = small_all[:, d:d + n_layers * 3 * cw_cols].reshape(NDEV, n_layers, 3, cw_cols)
    conv_full = conv_full.transpose(1, 2, 0, 3).reshape(n_layers, 3, WIDTH)

    b_mod_mine = lax.dynamic_slice_in_dim(b_mod, me * cm, cm, axis=1).reshape(n_layers, 1, cm)
    mod_cols = _mod_fwd(c_all, w_mod, b_mod_mine)
    mod_all = _all_gather_small("gather_mod", mod_cols.reshape(n_layers * NDEV, cm))
    mod_all = mod_all.reshape(NDEV, n_layers, NDEV, cm)
    mod_mine = lax.dynamic_index_in_dim(mod_all, me, axis=2, keepdims=False)
    mod_mine = mod_mine.transpose(1, 0, 2).reshape(n_layers, 3, 1, d)

    shard = w_in.shape[2]
    dsh = d // NDEV
    w_in_b, w_branch_b, w_out_b = w_in.astype(BF16), w_branch.astype(BF16), w_out.astype(BF16)
    window = lambda ref, dev: ref.at[:, pl.ds(pl.multiple_of(dev * shard, BLK), shard)]

    def two_step_sends(places):
        chips, sibling = [], []
        for k in (1, 2, 4, 6):
            for a, place in enumerate(places):
                chips.append((k, lambda ins, lands, me, a=a: ins[a],
                              lambda lands, me, a=a, place=place: place(lands[a], me),
                              lambda lands, me, a=a, k=k, place=place: place(lands[a], me ^ k)))
        for j in (2, 4, 6):
            for a, place in enumerate(places):
                sibling.append((1, lambda ins, lands, me, a=a, j=j, place=place: place(lands[a], me ^ j),
                                lambda lands, me, a=a, j=j, place=place: place(lands[a], me ^ j),
                                lambda lands, me, a=a, j=j, place=place: place(lands[a], me ^ 1 ^ j)))
        return chips, sibling

    in_sends = two_step_sends([window])
    rest_sends = two_step_sends([_slot, _slot])
    layer_sends = two_step_sends([window, _slot, _slot])

    def in_land(l):
        return _place_own((d, n_cols), BF16, w_in_b[l], (0, me * shard))

    def rest_lands(l):
        return [_place_own((NDEV, 3, WIDTH, dsh), BF16, w_branch_b[l][None], (me, 0, 0, 0)),
                _place_own((NDEV, dsh, d), BF16, w_out_b[l][None], (me, 0, 0))]

    def gather_start(name, shards, lands, sends, tie):
        if tie is not None:
            tie, lands = lax.optimization_barrier((tie, lands))
        return _exchange_start(f"{name}_chips_start", shards, lands, sends[0])

    def gather_pass_on(name, started, after, sends):
        lands = _exchange_wait(f"{name}_chips_wait", started, after, sends[0])
        return _exchange_start(f"{name}_sibling_start", [], lands, sends[1])

    def gather_finish(name, started, after, sends):
        return _exchange_wait(f"{name}_sibling_wait", started, after, sends[1])

    def branch_out_weights(w_branch_l, w_out_l):
        return w_branch_l.transpose(1, 2, 0, 3).reshape(3, WIDTH, d), w_out_l.reshape(d, d)

    gathering = gather_start("gather_w_in_0", [w_in_b[0]], [in_land(0)], in_sends, mod_mine)
    passing = gather_pass_on("gather_w_in_0", gathering, gathering[4], in_sends)
    rest_gathering = gather_start("gather_rest_0", [w_branch_b[0], w_out_b[0]], rest_lands(0), rest_sends, passing[4])
    next_gathering = None
    if n_layers > 1:
        next_gathering = gather_start("gather_weights_1", [w_in_b[1], w_branch_b[1], w_out_b[1]],
                                      [in_land(1)] + rest_lands(1), layer_sends, rest_gathering[4])
    w_in_l = gather_finish("gather_w_in_0", passing, (next_gathering or rest_gathering)[4], in_sends)[0]

    lbs = _lower_bound_table(lower_bounds)
    norm_w4 = jnp.tile(hgrn_norm_w, (1, WIDTH // HG_HEAD_DIM))

    saved = []
    xl = x0
    for l in range(n_layers):
        shift, scale, gate = mod_mine[l, 0], mod_mine[l, 1], mod_mine[l, 2]
        proj, h_t = _ln_proj(xl, shift, scale, w_in_l, f"ln_proj_{l}")
        o_a, totals = _sb_fwd(proj, f"sb_fwd_{l}")
        if l == 0:
            rest_passing = gather_pass_on("gather_rest_0", rest_gathering, o_a, rest_sends)
        o_b = _hgrn_fwd(proj, lbs[l:l + 1], f"hgrn_fwd_{l}")
        if l == 0:
            wb_l, wo_l = branch_out_weights(*gather_finish("gather_rest_0", rest_passing, o_b, rest_sends))
            if n_layers > 1:
                next_passing = gather_pass_on("gather_weights_1", next_gathering, o_b, layer_sends)
                gate = gate + next_passing[4][0, 0]
        x_new, merged, ycat = _merge_fwd(xl, proj, o_a, o_b, gate, norm_w4[l:l + 1], conv_full[l],
                                         wb_l, wo_l, ln_g[l:l + 1], ln_b[l:l + 1], f"merge_fwd_{l}")
        saved.append((xl, proj, h_t, o_a, totals, o_b, merged, ycat, w_in_l, wb_l, wo_l))
        if l == 0 and n_layers > 1:
            w_in_l, w_branch_l, w_out_l = gather_finish("gather_weights_1", next_passing, x_new, layer_sends)
            wb_l, wo_l = branch_out_weights(w_branch_l, w_out_l)
        xl = x_new

    loss_part, dx = _loss_fwd_bwd(xl, target)
    loss = lax.psum(loss_part[0, 0], ("x", "y", "c"))

    in_scatter = _direct_sends([(0, 0, window, _slot)])
    rest_scatter = _direct_sends([(0, 0, _slot, _slot), (1, 1, _slot, _slot)])
    scattering = [None] * n_layers
    small_grads = [None] * n_layers
    dmod = [None] * n_layers
    tie = None
    for l in reversed(range(n_layers)):
        xl, proj, h_t, o_a, totals, o_b, merged, ycat, w_in_l, wb_l, wo_l = saved[l]
        scale, gate = mod_mine[l, 1], mod_mine[l, 2]
        if tie is not None:
            gate = gate + tie[0, 0]
        dres, dycat, dproj, gwo, gwb, mvec = _merge_bwd(dx, xl, merged, ycat, proj, gate, wb_l, wo_l,
                                                        ln_g[l:l + 1], f"merge_bwd_{l}")
        gwb_by_owner = gwb.astype(BF16).reshape(3, WIDTH, NDEV, dsh).transpose(2, 0, 1, 3)
        gwo_by_owner = gwo.astype(BF16).reshape(NDEV, dsh, d)
        lands = [_place_own((NDEV, 3, WIDTH, dsh), BF16, lax.dynamic_slice_in_dim(gwb_by_owner, me, 1, axis=0),
                            (me, 0, 0, 0)),
                 _place_own((NDEV, dsh, d), BF16, lax.dynamic_slice_in_dim(gwo_by_owner, me, 1, axis=0),
                            (me, 0, 0))]
        rest_started = _exchange_start(f"scatter_rest_{l}_start", [gwb_by_owner, gwo_by_owner], lands, rest_scatter)
        dproj, do_a, do_b, bvec = _branch_bwd(dycat, proj, o_a, o_b, norm_w4[l:l + 1] + rest_started[4][0, 0],
                                              conv_full[l], dproj, f"branch_bwd_{l}")
        dproj = _sb_bwd(proj, do_a, totals, dproj, f"sb_bwd_{l}")
        dproj, dlb = _hgrn_bwd(proj, do_b, lbs[l:l + 1], dproj, f"hgrn_bwd_{l}")
        gwi = _gw_matmul(h_t, dproj, f"gw_matmul_{l}")
        land = _place_own((NDEV, d, shard), BF16, lax.dynamic_slice_in_dim(gwi, me * shard, shard, axis=1)[None],
                          (me, 0, 0))
        in_started = _exchange_start(f"scatter_in_{l}_start", [gwi], [land], in_scatter)
        scattering[l] = (in_started, rest_started)
        tie = in_started[4]
        dh = _dh_matmul(dproj, w_in_l, f"dh_matmul_{l}")
        dx, lvec = _ln_bwd(dh, xl, scale + tie[0, 0], dres, f"ln_bwd_{l}")
        dmod[l] = jnp.concatenate([lvec[0], lvec[1], mvec[2]])
        norm_grad = bvec[0].reshape(WIDTH // HG_HEAD_DIM, HG_HEAD_DIM).sum(axis=0)
        small_grads[l] = jnp.concatenate([mvec[0], mvec[1], norm_grad, dlb[0], bvec[1:4].reshape(-1)])
    grad_x = dx[None]

    small_vec = jnp.concatenate(dmod + small_grads)
    n_small = small_vec.shape[0]
    small_all = _all_gather_small("gather_small_grads", _pad_rows(small_vec, BLK))
    small_sum = _sum_parts(small_all, "sum_small_grads").reshape(-1)[:n_small]
    dmod_all = small_all.reshape(NDEV, -1)[:, :n_layers * 3 * d].reshape(NDEV, n_layers, 3 * d)

    off = n_layers * 3 * d
    grad_b_mod = small_sum[:off].reshape(n_layers, 3 * d)
    per_layer = 2 * d + HG_HEAD_DIM + WIDTH + 3 * WIDTH
    g_ln_g, g_ln_b, g_norm, g_lbs, g_conv = [], [], [], [], []
    for l in range(n_layers):
        seg = small_sum[off + l * per_layer: off + (l + 1) * per_layer]
        g_ln_g.append(seg[:d])
        g_ln_b.append(seg[d:2 * d])
        g_norm.append(seg[2 * d:2 * d + HG_HEAD_DIM])
        g_lbs.append(seg[2 * d + HG_HEAD_DIM:2 * d + HG_HEAD_DIM + WIDTH])
        g_conv.append(seg[2 * d + HG_HEAD_DIM + WIDTH:].reshape(3, WIDTH))
    grad_ln_g, grad_ln_b = jnp.stack(g_ln_g), jnp.stack(g_ln_b)
    grad_norm = jnp.stack(g_norm)
    _, lbs_vjp = jax.vjp(_lower_bound_table, lower_bounds)
    grad_lower = lbs_vjp(jnp.stack(g_lbs))[0]
    grad_conv = lax.dynamic_slice_in_dim(jnp.stack(g_conv), me * cw_cols, cw_cols, axis=2)

    dmod_mine = lax.dynamic_slice_in_dim(dmod_all, me * cm, cm, axis=2).transpose(1, 0, 2)
    grad_w_mod = _wmod_grad(c_all.T, dmod_mine)

    p_in, p_branch, p_out = [None] * n_layers, [None] * n_layers, [None] * n_layers
    for l in reversed(range(n_layers)):
        in_started, rest_started = scattering[l]
        p_branch_l, p_out[l] = _exchange_wait(f"scatter_rest_{l}_wait", rest_started, grad_w_mod, rest_scatter)
        p_branch[l] = p_branch_l.reshape(NDEV, 3 * WIDTH, dsh)
        p_in[l] = _exchange_wait(f"scatter_in_{l}_wait", in_started, grad_w_mod, in_scatter)[0]

    def adam(parts_list, w, m, v, name):
        shape = w.shape
        cols = shape[-1]
        flat = lambda a: a.reshape(-1, cols)
        outs = _sum_adamw(parts_list, flat(w), flat(m), flat(v), name)
        return [o.reshape(shape) for o in outs]

    r_w_in = adam(p_in, w_in, m_w_in, v_w_in, "adamw_w_in")
    r_w_branch = adam(p_branch, w_branch, m_w_branch, v_w_branch, "adamw_w_branch")
    r_w_out = adam(p_out, w_out, m_w_out, v_w_out, "adamw_w_out")
    r_w_mod = adam([grad_w_mod.reshape(1, -1, cm)], w_mod, m_w_mod, v_w_mod, "adamw_w_mod")

    small_names = ["b_mod", "conv_w", "hgrn_norm_w", "lower_bounds", "ln_g", "ln_b"]
    small_g = [grad_b_mod, grad_conv, grad_norm, grad_lower, grad_ln_g, grad_ln_b]
    small_w = [b_mod, conv_w, hgrn_norm_w, lower_bounds, ln_g, ln_b]
    small_m = [m_b_mod, m_conv_w, m_hgrn_norm_w, m_lower_bounds, m_ln_g, m_ln_b]
    small_v = [v_b_mod, v_conv_w, v_hgrn_norm_w, v_lower_bounds, v_ln_g, v_ln_b]
    pack = lambda arrs: _pad_rows(jnp.concatenate([a.reshape(-1) for a in arrs]), BLK)
    packed = _sum_adamw([pack(small_g)[None]], pack(small_w), pack(small_m), pack(small_v), "adamw_small")
    r_small = {n: [] for n in small_names}
    for res in packed:
        flat = res.reshape(-1)
        pos = 0
        for n, w in zip(small_names, small_w):
            r_small[n].append(flat[pos:pos + w.size].reshape(w.shape))
            pos += w.size

    results = {"w_mod": r_w_mod, "w_in": r_w_in, "w_branch": r_w_branch, "w_out": r_w_out, **r_small}
    order = ["w_mod", "b_mod", "w_in", "conv_w", "hgrn_norm_w", "lower_bounds", "w_branch", "w_out", "ln_g", "ln_b"]
    outs = [loss, grad_x]
    for idx in range(4):
        outs.extend(results[n][idx] for n in order)
    return tuple(outs)
```

```python
import jax
import jax.numpy as jnp
from jax import lax
from jax.experimental import pallas as pl
from jax.experimental.pallas import tpu as pltpu

F32 = jnp.float32
BF16 = jnp.bfloat16
NDEV = 8
N_LAYERS = 2
SB_HEAD_DIM = 64
HG_HEAD_DIM = 128
WIDTH = 512
BLK = 128
LN_EPS = 1e-5
RMS_EPS = 1e-6
ALPHA = (2.0 * N_LAYERS) ** 0.25
ADAM_LR, ADAM_B1, ADAM_B2, ADAM_EPS, ADAM_WD, ADAM_STEP = 0.001, 0.9, 0.999, 1e-08, 0.01, 10
VMEM_LIMIT = 56 * 1024 * 1024
MESH = pl.DeviceIdType.MESH
HG_LEVELS = (64, 32, 16, 8, 4, 2, 1)


def _pcall(body, *, name, out_shape, grid=None, in_specs=None, out_specs=None, scratch_shapes=(),
           semantics=None, aliases=None):
    kwargs = {}
    if grid is not None:
        kwargs["grid"] = grid
    if in_specs is not None:
        kwargs["in_specs"] = in_specs
    if out_specs is not None:
        kwargs["out_specs"] = out_specs
    if aliases:
        kwargs["input_output_aliases"] = aliases
    return pl.pallas_call(
        body, name=name, out_shape=out_shape, scratch_shapes=list(scratch_shapes),
        compiler_params=pltpu.CompilerParams(dimension_semantics=semantics, vmem_limit_bytes=VMEM_LIMIT),
        interpret=False, **kwargs)


def _dot(a, b):
    return jnp.dot(a, b, preferred_element_type=F32)


def _dot_nt(a, b):
    return lax.dot_general(a, b, (((1,), (1,)), ((), ())), preferred_element_type=F32)


def _dot_tn(a, b):
    return lax.dot_general(a, b, (((0,), (0,)), ((), ())), preferred_element_type=F32)


def _split3(x):
    x1 = x.astype(BF16)
    r1 = x - x1.astype(F32)
    x2 = r1.astype(BF16)
    r2 = r1 - x2.astype(F32)
    return x1, x2, r2.astype(BF16)


def _split2(x):
    x1 = x.astype(BF16)
    return x1, (x - x1.astype(F32)).astype(BF16)


def _dot_exact_l(m_bf16, x):
    x1, x2, x3 = _split3(x)
    return _dot(m_bf16, x1) + _dot(m_bf16, x2) + _dot(m_bf16, x3)


def _sigmoid(x):
    return 1.0 / (1.0 + jnp.exp(-x))


def _silu_and_grad(x):
    s = _sigmoid(x)
    return x * s, s * (1.0 + x * (1.0 - s))


LOG2E = 1.4426950408889634
MASKED_SCORE = -1e30


def _softplus2_parts(z2):
    minus_abs = lax.bitcast_convert_type(lax.bitcast_convert_type(z2, jnp.int32) | jnp.int32(-2 ** 31), F32)
    e = jnp.exp2(minus_abs)
    sp2 = jnp.maximum(z2, 0.0) + jnp.log2(1.0 + e)
    r = 1.0 / (1.0 + e)
    return sp2, jnp.where(z2 >= 0.0, r, e * r)


def _split2_lanes(x):
    x1 = x.astype(BF16)
    return jnp.concatenate([x1, (x - x1.astype(F32)).astype(BF16)], axis=1)


def _iota2(shape, dim):
    return lax.broadcasted_iota(jnp.int32, shape, dim)


def _standardize(x):
    mu = jnp.mean(x, axis=-1, keepdims=True)
    xc = x - mu
    var = jnp.mean(xc * xc, axis=-1, keepdims=True)
    rstd = lax.rsqrt(var + LN_EPS)
    return xc * rstd, rstd


def _standardize_bwd(xhat, rstd, dxhat):
    m1 = jnp.mean(dxhat, axis=-1, keepdims=True)
    m2 = jnp.mean(dxhat * xhat, axis=-1, keepdims=True)
    return rstd * (dxhat - m1 - xhat * m2)


def _my_index():
    return 4 * lax.axis_index("x") + 2 * lax.axis_index("y") + lax.axis_index("c")


def _exchange(name, ins, out_shapes, transfers, in_vmem):
    n_in, n_out, n_t = len(ins), len(out_shapes), len(transfers)

    def body(*refs):
        in_refs, out_refs = refs[:n_in], refs[n_in:n_in + n_out]
        send_sems, recv_sems, local_sems = refs[n_in + n_out:]
        x, y, c = lax.axis_index("x"), lax.axis_index("y"), lax.axis_index("c")
        me = 4 * x + 2 * y + c
        started = []
        for t, (i, o, src_fn, dst_fn) in enumerate(transfers):
            own = pltpu.make_async_copy(src_fn(in_refs[i], me), dst_fn(out_refs[o], me), local_sems.at[t])
            own.start()
            started.append(own)
        arrivals = []
        for k in range(1, NDEV):
            px = x ^ ((k >> 2) & 1)
            py = y ^ ((k >> 1) & 1)
            pc = c ^ (k & 1)
            peer = 4 * px + 2 * py + pc
            for t, (i, o, src_fn, dst_fn) in enumerate(transfers):
                sem = t * (NDEV - 1) + k - 1
                push = pltpu.make_async_remote_copy(
                    src_ref=src_fn(in_refs[i], peer), dst_ref=dst_fn(out_refs[o], me),
                    send_sem=send_sems.at[sem], recv_sem=recv_sems.at[sem],
                    device_id=(px, py, pc), device_id_type=MESH)
                push.start()
                started.append(push)
                arrivals.append(pltpu.make_async_remote_copy(
                    src_ref=src_fn(in_refs[i], peer), dst_ref=dst_fn(out_refs[o], peer),
                    send_sem=send_sems.at[sem], recv_sem=recv_sems.at[sem],
                    device_id=(px, py, pc), device_id_type=MESH))
        for arrival in arrivals:
            arrival.wait_recv()
        for cp in started[n_t:]:
            cp.wait_send()
        for own in started[:n_t]:
            own.wait()

    space = pltpu.VMEM if in_vmem else pl.ANY
    spec = pl.BlockSpec(memory_space=space)
    return _pcall(
        body, name=name, out_shape=out_shapes,
        in_specs=[spec] * n_in, out_specs=[spec] * n_out,
        scratch_shapes=[pltpu.SemaphoreType.DMA((n_t * (NDEV - 1),)),
                        pltpu.SemaphoreType.DMA((n_t * (NDEV - 1),)),
                        pltpu.SemaphoreType.DMA((n_t,))])(*ins)


def _whole(ref, dev):
    return ref


def _slot(ref, dev):
    return ref.at[dev]


def _all_gather_small(name, v):
    out = _exchange(name, [v], [jax.ShapeDtypeStruct((NDEV,) + v.shape, v.dtype)],
                    [(0, 0, _whole, _slot)], in_vmem=True)
    return out[0]


_HBM_SPEC = pl.BlockSpec(memory_space=pltpu.HBM)
_SEM_SPEC = pl.BlockSpec(memory_space=pltpu.SEMAPHORE)
_DATAFLOW = pltpu.SideEffectType.DATAFLOW_SIDE_EFFECTING


def _peer(x, y, c, k):
    px = x ^ ((k >> 2) & 1)
    py = y ^ ((k >> 1) & 1)
    pc = c ^ (k & 1)
    return (px, py, pc), 4 * px + 2 * py + pc


def _direct_sends(transfers):
    sends = []
    for k in range(1, NDEV):
        for i, o, src_fn, dst_fn in transfers:
            sends.append((k,
                          lambda ins, lands, me, i=i, k=k, src_fn=src_fn: src_fn(ins[i], me ^ k),
                          lambda lands, me, o=o, dst_fn=dst_fn: dst_fn(lands[o], me),
                          lambda lands, me, o=o, k=k, dst_fn=dst_fn: dst_fn(lands[o], me ^ k)))
    return sends


def _exchange_start(name, ins, lands, sends):
    n_in, n_buf = len(ins), len(ins) + len(lands)
    n_sem = len(sends)

    def body(*refs):
        in_refs, land_refs = refs[:n_in], refs[n_in:n_buf]
        send_sems, recv_sems, token = refs[n_buf], refs[n_buf + 1], refs[-1]
        x, y, c = lax.axis_index("x"), lax.axis_index("y"), lax.axis_index("c")
        me = 4 * x + 2 * y + c
        for t, (k, src_fn, dst_fn, _) in enumerate(sends):
            pltpu.make_async_remote_copy(
                src_ref=src_fn(in_refs, land_refs, me), dst_ref=dst_fn(land_refs, me),
                send_sem=send_sems.at[t], recv_sem=recv_sems.at[t],
                device_id=_peer(x, y, c, k)[0], device_id_type=MESH).start()
        token[...] = jnp.zeros_like(token)

    bufs = [pltpu.with_memory_space_constraint(a, pltpu.HBM) for a in list(ins) + list(lands)]
    outs = pl.pallas_call(
        body, name=name,
        out_shape=(pltpu.SemaphoreType.DMA((n_sem,)), pltpu.SemaphoreType.DMA((n_sem,)))
        + tuple(pltpu.HBM(a.shape, a.dtype) for a in bufs) + (jax.ShapeDtypeStruct((8, BLK), F32),),
        in_specs=[_HBM_SPEC] * n_buf,
        out_specs=(_SEM_SPEC, _SEM_SPEC) + (_HBM_SPEC,) * n_buf + (pl.BlockSpec(memory_space=pltpu.VMEM),),
        input_output_aliases={b: 2 + b for b in range(n_buf)},
        compiler_params=pltpu.CompilerParams(has_side_effects=_DATAFLOW),
        interpret=False)(*bufs)
    return outs[0], outs[1], list(outs[2:2 + n_in]), list(outs[2 + n_in:2 + n_buf]), outs[-1]


def _exchange_wait(name, started, after, sends):
    send_sems, recv_sems, ins, lands, _ = started
    n_in, n_buf = len(ins), len(ins) + len(lands)

    def body(*refs):
        in_refs, land_refs = refs[:n_in], refs[n_in:n_buf]
        send_sems, recv_sems = refs[n_buf], refs[n_buf + 1]
        x, y, c = lax.axis_index("x"), lax.axis_index("y"), lax.axis_index("c")
        me = 4 * x + 2 * y + c
        for t, (k, src_fn, _, rcv_fn) in enumerate(sends):
            cp = pltpu.make_async_remote_copy(
                src_ref=src_fn(in_refs, land_refs, me), dst_ref=rcv_fn(land_refs, me),
                send_sem=send_sems.at[t], recv_sem=recv_sems.at[t],
                device_id=_peer(x, y, c, k)[0], device_id_type=MESH)
            cp.wait_send()
            cp.wait_recv()

    bufs = list(ins) + list(lands)
    outs = pl.pallas_call(
        body, name=name, out_shape=tuple(pltpu.HBM(a.shape, a.dtype) for a in bufs),
        in_specs=[_HBM_SPEC] * n_buf + [_SEM_SPEC, _SEM_SPEC, pl.BlockSpec(memory_space=pl.ANY)],
        out_specs=(_HBM_SPEC,) * n_buf,
        input_output_aliases={b: b for b in range(n_buf)},
        compiler_params=pltpu.CompilerParams(has_side_effects=_DATAFLOW),
        interpret=False)(*bufs, send_sems, recv_sems, after)
    return list(outs[n_in:])


def _place_own(shape, dtype, own, start):
    return lax.dynamic_update_slice(lax.empty(shape, dtype), own, start)


def _mod_fwd(c_all, w_mod, b_mod_mine):
    n_layers, _, cm = w_mod.shape

    def body(c_ref, w_ref, b_ref, o_ref):
        for l in range(n_layers):
            o_ref[l] = jnp.dot(c_ref[...], w_ref[l], preferred_element_type=F32,
                               precision=lax.Precision.HIGHEST) + b_ref[l]

    return _pcall(body, name="mod_fwd", out_shape=jax.ShapeDtypeStruct((n_layers, NDEV, cm), F32))(
        c_all, w_mod, b_mod_mine)


def _ln_proj(x, shift, scale, w_full, name):
    s_len, d = x.shape
    n = w_full.shape[1]
    tm = min(512, s_len)
    tn = 1024

    def body(x_ref, sh_ref, sc_ref, w_ref, proj_ref, ht_ref, h_scr):
        @pl.when(pl.program_id(1) == 0)
        def _():
            xs, _ = _standardize(x_ref[...])
            h = xs * (1.0 + sc_ref[...]) + sh_ref[...]
            h_scr[...] = h.astype(BF16)
            ht_ref[...] = h.T.astype(BF16)

        proj_ref[...] = _dot(h_scr[...], w_ref[...])

    return _pcall(
        body, name=name,
        out_shape=(jax.ShapeDtypeStruct((s_len, n), F32), jax.ShapeDtypeStruct((d, s_len), BF16)),
        grid=(s_len // tm, n // tn),
        in_specs=[pl.BlockSpec((tm, d), lambda i, j: (i, 0)),
                  pl.BlockSpec((1, d), lambda i, j: (0, 0)),
                  pl.BlockSpec((1, d), lambda i, j: (0, 0)),
                  pl.BlockSpec((d, tn), lambda i, j: (0, j))],
        out_specs=(pl.BlockSpec((tm, tn), lambda i, j: (i, j)),
                   pl.BlockSpec((d, tm), lambda i, j: (0, i))),
        scratch_shapes=[pltpu.VMEM((tm, d), BF16)],
        semantics=("arbitrary", "arbitrary"))(x, shift, scale, w_full)


def _sb_group_blocks(nb):
    return min(4, nb)


def _sb_fwd(proj, name):
    s_len = proj.shape[0]
    nb = s_len // BLK
    n_pairs = WIDTH // BLK
    gb = _sb_group_blocks(nb)
    kw = gb * BLK

    def body(q_ref, k_ref, v_ref, o_ref, tot_ref):
        lane = _iota2((1, BLK), 1)
        row = _iota2((BLK, BLK), 0)
        col = _iota2((BLK, BLK), 1)
        half = jnp.concatenate([(row >= col).astype(BF16), jnp.ones((BLK, BLK), BF16)], axis=1)
        suffix_and_sum = jnp.concatenate([half, half], axis=0)
        qpos = _iota2((BLK, kw), 0)
        kpos = _iota2((BLK, kw), 1)
        head_lanes = [(lane // SB_HEAD_DIM) == hh for hh in range(2)]

        def scores(i, gi, qms, masked):
            c0 = pl.multiple_of(gi * kw, kw)
            kb = k_ref[pl.ds(c0, kw), :].astype(BF16)
            z2s = [_dot_nt(qms[hh], kb) for hh in range(2)]
            if masked:
                valid = (c0 + kpos) < (i * BLK + qpos)
                z2s = [jnp.where(valid, z2, MASKED_SCORE) for z2 in z2s]
            return tuple(z2s)

        def accumulate(gi, z2s, carry):
            c0 = pl.multiple_of(gi * kw, kw)
            vf = v_ref[pl.ds(c0, kw), :]
            sp2s = [_softplus2_parts(z2)[0] for z2 in z2s]
            terms = [[_split2_lanes(sp2[:, b * BLK:(b + 1) * BLK]) for b in range(gb)] for sp2 in sp2s]
            sums = [[_dot(t, suffix_and_sum) for t in head_terms] for head_terms in terms]
            weights, laters = [], []
            for hh in range(2):
                later = carry[2 * hh + 1]
                parts = [None] * gb
                for b in reversed(range(gb)):
                    parts[b] = sums[hh][b][:, :BLK] + later
                    later = later + sums[hh][b][:, BLK:]
                weights.append(jnp.exp2(z2s[hh] - jnp.concatenate(parts, axis=1)).astype(BF16))
                laters.append(later)
            outs = [_dot(weights[hh], jnp.where(head_lanes[hh], vf, 0.0).astype(BF16)) for hh in range(2)]
            return (carry[0] + outs[0], laters[0], carry[2] + outs[1], laters[1])

        def queries(i):
            qf = q_ref[pl.ds(pl.multiple_of(i * BLK, BLK), BLK), :] * (SB_HEAD_DIM ** -0.5 * LOG2E)
            return [jnp.where(head_lanes[hh], qf, 0.0).astype(BF16) for hh in range(2)]

        def qblock(i, first_scores):
            r0 = pl.multiple_of(i * BLK, BLK)
            qms = queries(i)
            zero = jnp.zeros((BLK, BLK), F32)
            last = i // gb

            def step(jj, state):
                gi = last - 1 - jj
                return scores(i, gi, qms, False) + accumulate(gi + 1, state[:2], state[2:])

            state = lax.fori_loop(0, last, step, first_scores + (zero,) * 4)
            nxt = jnp.minimum(i + 1, nb - 1)
            next_scores = scores(nxt, nxt // gb, queries(nxt), True)
            carry = accumulate(0, state[:2], state[2:])
            o_ref[pl.ds(r0, BLK), :] = carry[0] + carry[2]
            tot_ref[0, pl.ds(r0, BLK), :] = carry[1]
            tot_ref[1, pl.ds(r0, BLK), :] = carry[3]
            return next_scores

        lax.fori_loop(0, nb, qblock, scores(0, 0, queries(0), True))

    col_spec = lambda off: pl.BlockSpec((s_len, BLK), lambda p: (0, off + p))
    return _pcall(
        body, name=name,
        out_shape=(jax.ShapeDtypeStruct((s_len, WIDTH), F32),
                   jax.ShapeDtypeStruct((2 * n_pairs, s_len, BLK), F32)),
        grid=(n_pairs,),
        in_specs=[col_spec(0), col_spec(n_pairs), col_spec(2 * n_pairs)],
        out_specs=(pl.BlockSpec((s_len, BLK), lambda p: (0, p)),
                   pl.BlockSpec((2, s_len, BLK), lambda p: (p, 0, 0))),
        semantics=("arbitrary",))(proj, proj, proj)


def _hg_masks(mask_ref):
    row = _iota2((BLK, BLK), 0)
    col = _iota2((BLK, BLK), 1)
    for v, m in enumerate(HG_LEVELS):
        same = (row // (2 * m)) == (col // (2 * m))
        mask_ref[v] = (same & ((row & m) != 0) & ((col & m) == 0)).astype(F32)


def _hg_mid(b, m):
    if m >= 4:
        n = BLK // (2 * m)
        mid = b.reshape(n, 2 * m, BLK)[:, m - 1:m, :]
        return jnp.broadcast_to(mid, (n, 2 * m, BLK)).reshape(BLK, BLK)
    pos = _iota2((BLK, BLK), 0) & (2 * m - 1)
    out = b
    for p in range(2 * m):
        delta = (m - 1) - p
        if delta != 0:
            out = jnp.where(pos == p, pltpu.roll(b, (-delta) % BLK, 0), out)
    return out


def _hg_chunk_inputs(qraw, fpre, lb):
    sig = _sigmoid(fpre)
    f = lb + (1.0 - lb) * sig
    g = jnp.log(f)
    q, dq_fac = _silu_and_grad(qraw)
    return q, dq_fac, f, sig, g


def _hg_level_terms(q, k, b, v_idx, m, mask_ref):
    mid = _hg_mid(b, m)
    eq = jnp.exp(jnp.minimum(b - mid, 0.0))
    ek = jnp.exp(jnp.minimum(mid - b, 0.0))
    qt = (q * eq).astype(BF16)
    kt = (k * ek).astype(BF16)
    return qt, kt, eq, ek, mask_ref[v_idx]


def _hg_scores(q, k, b, mask_ref):
    sc = None
    for v_idx, m in enumerate(HG_LEVELS):
        qt, kt, _, _, msk = _hg_level_terms(q, k, b, v_idx, m, mask_ref)
        term = _dot_nt(qt, kt) * msk
        sc = term if sc is None else sc + term
    return sc


def _hgrn_fwd(proj, lb, name):
    s_len = proj.shape[0]
    nc = s_len // BLK
    nh = WIDTH // HG_HEAD_DIM
    base = 4 * WIDTH // BLK

    def body(q_ref, f_ref, i_ref, lb_ref, o_ref, mask_ref):
        _hg_masks(mask_ref)
        row = _iota2((BLK, BLK), 0)
        col = _iota2((BLK, BLK), 1)
        lower_incl = (col <= row).astype(BF16)
        lb_v = lb_ref[...]

        def chunk(ci, st):
            r0 = pl.multiple_of(ci * BLK, BLK)
            q, _, f, _, g = _hg_chunk_inputs(q_ref[pl.ds(r0, BLK), :], f_ref[pl.ds(r0, BLK), :], lb_v)
            k = 1.0 - f
            v = i_ref[pl.ds(r0, BLK), :]
            vb = v.astype(BF16)
            b = _dot_exact_l(lower_incl, g)
            b_end = b[BLK - 1:BLK, :]
            inter = _dot_nt((q * jnp.exp(b)).astype(BF16), st.astype(BF16))
            sc = _hg_scores(q, k, b, mask_ref)
            diag = jnp.sum(q * k, axis=-1, keepdims=True)
            o_ref[pl.ds(r0, BLK), :] = inter + _dot(sc.astype(BF16), vb) + diag * v
            k_dec = (k * jnp.exp(b_end - b)).astype(BF16)
            return st * jnp.exp(b_end) + _dot_tn(vb, k_dec)

        lax.fori_loop(0, nc, chunk, jnp.zeros((HG_HEAD_DIM, HG_HEAD_DIM), F32))

    col_spec = lambda off: pl.BlockSpec((s_len, BLK), lambda h: (0, off + h))
    return _pcall(
        body, name=name, out_shape=jax.ShapeDtypeStruct((s_len, WIDTH), F32),
        grid=(nh,),
        in_specs=[col_spec(base), col_spec(base + nh), col_spec(base + 2 * nh),
                  pl.BlockSpec((1, BLK), lambda h: (0, h))],
        out_specs=pl.BlockSpec((s_len, BLK), lambda h: (0, h)),
        scratch_shapes=[pltpu.VMEM((len(HG_LEVELS), BLK, BLK), F32)],
        semantics=("arbitrary",))(proj, proj, proj, lb)


def _rms_heads(o_b, norm_w):
    n_parts, h_parts, r_parts = [], [], []
    for h in range(WIDTH // HG_HEAD_DIM):
        sl = slice(h * HG_HEAD_DIM, (h + 1) * HG_HEAD_DIM)
        o = o_b[:, sl]
        rstd = lax.rsqrt(jnp.mean(o * o, axis=-1, keepdims=True) + RMS_EPS)
        ohat = o * rstd
        h_parts.append(ohat)
        n_parts.append(ohat * norm_w[:, sl])
        r_parts.append(jnp.broadcast_to(rstd, o.shape))
    cat = lambda parts: jnp.concatenate(parts, axis=-1)
    return cat(n_parts), cat(h_parts), cat(r_parts)


def _shift_rows_down(halo, cur, k):
    tm = cur.shape[0]
    ext = jnp.concatenate([halo, cur], axis=0)
    return pltpu.roll(ext, k, 0)[8:8 + tm]


def _shift_rows_up(cur, halo, k):
    tm = cur.shape[0]
    ext = jnp.concatenate([cur, halo], axis=0)
    return pltpu.roll(ext, (tm + 8 - k) % (tm + 8), 0)[0:tm]


def _merge_fwd(x, proj, o_a, o_b, gate, norm_w, conv_w, wb, w_out, ln_g, ln_b, name):
    s_len, d = x.shape
    tm = min(256, s_len)
    hb = tm // 8

    def body(x_ref, oa_ref, za_ref, ob_ref, zb_ref, pre_ref, post_ref, u_ref, zc_ref, hpre_ref, hu_ref, g_ref,
             gate_ref, nw_ref, cw_ref, wb_ref, wo_ref, lg_ref, lbias_ref, xn_ref, mg_ref, yc_ref):
        i = pl.program_id(0)
        sa, _ = _silu_and_grad(za_ref[...])
        y_a = (oa_ref[...] * sa).astype(BF16)
        n_b, _, _ = _rms_heads(ob_ref[...], nw_ref[...])
        sb, _ = _silu_and_grad(zb_ref[...])
        y_b = (n_b * sb).astype(BF16)
        a = pre_ref[...] * u_ref[...]
        halo = jnp.where(i > 0, hpre_ref[...] * hu_ref[...], 0.0)
        cw = cw_ref[...]
        conv = cw[0:1] * _shift_rows_down(halo, a, 2) + cw[1:2] * _shift_rows_down(halo, a, 1) + cw[2:3] * a
        sc, _ = _silu_and_grad(zc_ref[...])
        y_c = (post_ref[...] * conv * sc).astype(BF16)
        merged = None
        for k, yk in enumerate((y_a, y_b, y_c)):
            yc_ref[:, k * WIDTH:(k + 1) * WIDTH] = yk
            term = _sigmoid(g_ref[:, k * d:(k + 1) * d]) * _dot(yk, wb_ref[k])
            merged = term if merged is None else merged + term
        mb = merged.astype(BF16)
        mg_ref[...] = mb
        y = _dot(mb, wo_ref[...])
        r = ALPHA * x_ref[...] + (1.0 + gate_ref[...]) * y
        rhat, _ = _standardize(r)
        xn_ref[...] = rhat * lg_ref[...] + lbias_ref[...]

    wcol = lambda cb: pl.BlockSpec((tm, WIDTH), lambda i: (i, cb))
    halo_spec = lambda cb: pl.BlockSpec((8, WIDTH), lambda i: (jnp.maximum(i * hb - 1, 0), cb))
    vec = lambda w: pl.BlockSpec((1, w), lambda i: (0, 0))
    return _pcall(
        body, name=name,
        out_shape=(jax.ShapeDtypeStruct((s_len, d), F32), jax.ShapeDtypeStruct((s_len, d), BF16),
                   jax.ShapeDtypeStruct((s_len, 3 * WIDTH), BF16)),
        grid=(s_len // tm,),
        in_specs=[pl.BlockSpec((tm, d), lambda i: (i, 0)),
                  wcol(0), wcol(3), wcol(0), wcol(7), wcol(8), wcol(9), wcol(10), wcol(11),
                  halo_spec(8), halo_spec(10),
                  pl.BlockSpec((tm, 3 * d), lambda i: (i, 2)),
                  vec(d), vec(WIDTH),
                  pl.BlockSpec((3, WIDTH), lambda i: (0, 0)),
                  pl.BlockSpec((3, WIDTH, d), lambda i: (0, 0, 0)),
                  pl.BlockSpec((d, d), lambda i: (0, 0)),
                  vec(d), vec(d)],
        out_specs=(pl.BlockSpec((tm, d), lambda i: (i, 0)), pl.BlockSpec((tm, d), lambda i: (i, 0)),
                   pl.BlockSpec((tm, 3 * WIDTH), lambda i: (i, 0))),
        semantics=("arbitrary",))(x, o_a, proj, o_b, proj, proj, proj, proj, proj, proj, proj, proj,
                                  gate, norm_w, conv_w, wb, w_out, ln_g, ln_b)


def _loss_fwd_bwd(y, target):
    s_len, d = y.shape
    tm = min(512, s_len)

    def body(y_ref, t_ref, loss_ref, dy_ref):
        @pl.when(pl.program_id(0) == 0)
        def _():
            loss_ref[...] = jnp.zeros_like(loss_ref)

        e = y_ref[...] - t_ref[...]
        dy_ref[...] = e * (1.0 / d)
        part = jnp.sum(jnp.sum(e * e, axis=-1, keepdims=True), axis=0, keepdims=True)
        loss_ref[...] += part * (0.5 / d)

    tile = pl.BlockSpec((tm, d), lambda i: (i, 0))
    return _pcall(body, name="loss", grid=(s_len // tm,),
                  out_shape=(jax.ShapeDtypeStruct((1, 1), F32), jax.ShapeDtypeStruct((s_len, d), F32)),
                  in_specs=[tile, tile],
                  out_specs=(pl.BlockSpec((1, 1), lambda i: (0, 0)), tile),
                  semantics=("arbitrary",))(y, target)


def _merge_bwd(dxn, x, merged, ycat, proj, gate, wb, w_out, ln_g, name):
    s_len, d = x.shape
    tm = min(256, s_len)

    def body(dxn_ref, x_ref, mg_ref, yc_ref, g_ref, gate_ref, wb_ref, wo_ref, lg_ref,
             dres_ref, dyc_ref, dg_ref, gwo_ref, gwb_ref, vec_ref):
        @pl.when(pl.program_id(0) == 0)
        def _():
            gwo_ref[...] = jnp.zeros_like(gwo_ref)
            gwb_ref[...] = jnp.zeros_like(gwb_ref)
            vec_ref[...] = jnp.zeros_like(vec_ref)

        mb = mg_ref[...]
        one_gate = 1.0 + gate_ref[...]
        y = _dot(mb, wo_ref[...])
        r = ALPHA * x_ref[...] + one_gate * y
        rhat, rstd = _standardize(r)
        dxn = dxn_ref[...]
        dr = _standardize_bwd(rhat, rstd, dxn * lg_ref[...])
        vec_ref[0:1, :] += jnp.sum(dxn * rhat, axis=0, keepdims=True)
        vec_ref[1:2, :] += jnp.sum(dxn, axis=0, keepdims=True)
        vec_ref[2:3, :] += jnp.sum(dr * y, axis=0, keepdims=True)
        dres_ref[...] = ALPHA * dr
        dy = (one_gate * dr).astype(BF16)
        gwo_ref[...] += _dot_tn(mb, dy)
        dmerged = _dot_nt(dy, wo_ref[...])
        for k in range(3):
            yk = yc_ref[:, k * WIDTH:(k + 1) * WIDTH]
            sg = _sigmoid(g_ref[:, k * d:(k + 1) * d])
            pk = _dot(yk, wb_ref[k])
            dg_ref[:, k * d:(k + 1) * d] = (dmerged * pk * sg * (1.0 - sg)).astype(BF16)
            dpk = (dmerged * sg).astype(BF16)
            dyc_ref[:, k * WIDTH:(k + 1) * WIDTH] = _dot_nt(dpk, wb_ref[k])
            gwb_ref[k] += _dot_tn(yk, dpk)

    tile = lambda w: pl.BlockSpec((tm, w), lambda i: (i, 0))
    vec = pl.BlockSpec((1, d), lambda i: (0, 0))
    return _pcall(
        body, name=name,
        out_shape=(jax.ShapeDtypeStruct((s_len, d), F32), jax.ShapeDtypeStruct((s_len, 3 * WIDTH), F32),
                   jax.ShapeDtypeStruct(proj.shape, BF16), jax.ShapeDtypeStruct((d, d), F32),
                   jax.ShapeDtypeStruct((3, WIDTH, d), F32), jax.ShapeDtypeStruct((8, d), F32)),
        grid=(s_len // tm,),
        in_specs=[tile(d), tile(d), tile(d), tile(3 * WIDTH),
                  pl.BlockSpec((tm, 3 * d), lambda i: (i, 2)),
                  vec, pl.BlockSpec((3, WIDTH, d), lambda i: (0, 0, 0)),
                  pl.BlockSpec((d, d), lambda i: (0, 0)), vec],
        out_specs=(tile(d), tile(3 * WIDTH), pl.BlockSpec((tm, 3 * d), lambda i: (i, 2)),
                   pl.BlockSpec((d, d), lambda i: (0, 0)),
                   pl.BlockSpec((3, WIDTH, d), lambda i: (0, 0, 0)),
                   pl.BlockSpec((8, d), lambda i: (0, 0))),
        semantics=("arbitrary",))(dxn, x, merged, ycat, proj, gate, wb, w_out, ln_g)


def _branch_bwd(dycat, proj, o_a, o_b, norm_w, conv_w, dproj, name):
    s_len = proj.shape[0]
    tm = min(256, s_len)
    hb = tm // 8
    n_tiles = s_len // tm

    def body(dya_ref, dyb_ref, dyc_ref, oa_ref, za_ref, ob_ref, zb_ref, pre_ref, post_ref, u_ref, zc_ref,
             hpre_ref, hu_ref, ndyc_ref, npost_ref, nzc_ref, nw_ref, cw_ref, dproj_in,
             dproj_ref, doa_ref, dob_ref, vec_ref, dza_scr, dzb_scr, dc_scr, sems):
        del dproj_in
        i = pl.program_id(0)

        @pl.when(i == 0)
        def _():
            vec_ref[...] = jnp.zeros_like(vec_ref)

        sa, dsa = _silu_and_grad(za_ref[...])
        dya = dya_ref[...]
        doa_ref[...] = dya * sa
        dza_scr[...] = (dya * oa_ref[...] * dsa).astype(BF16)
        nw = nw_ref[...]
        n_b, ohat, rstd = _rms_heads(ob_ref[...], nw)
        sb, dsb = _silu_and_grad(zb_ref[...])
        dyb = dyb_ref[...]
        dzb_scr[...] = (dyb * n_b * dsb).astype(BF16)
        dn = dyb * sb
        vec_ref[0:1, :] += jnp.sum(dn * ohat, axis=0, keepdims=True)
        dnw = dn * nw
        parts = []
        for h in range(WIDTH // HG_HEAD_DIM):
            sl = slice(h * HG_HEAD_DIM, (h + 1) * HG_HEAD_DIM)
            m2 = jnp.mean(dnw[:, sl] * ohat[:, sl], axis=-1, keepdims=True)
            parts.append(rstd[:, sl] * (dnw[:, sl] - ohat[:, sl] * m2))
        dob_ref[...] = jnp.concatenate(parts, axis=-1)
        cw = cw_ref[...]
        pre, u, post = pre_ref[...], u_ref[...], post_ref[...]
        a = pre * u
        halo = jnp.where(i > 0, hpre_ref[...] * hu_ref[...], 0.0)
        a1 = _shift_rows_down(halo, a, 1)
        a2 = _shift_rows_down(halo, a, 2)
        conv = cw[0:1] * a2 + cw[1:2] * a1 + cw[2:3] * a
        sc, dsc = _silu_and_grad(zc_ref[...])
        dyc = dyc_ref[...]
        dconv = dyc * post * sc
        nsc, _ = _silu_and_grad(nzc_ref[...])
        nxt = jnp.where(i < n_tiles - 1, ndyc_ref[...] * npost_ref[...] * nsc, 0.0)
        da = cw[2:3] * dconv + cw[1:2] * _shift_rows_up(dconv, nxt, 1) + cw[0:1] * _shift_rows_up(dconv, nxt, 2)
        dc_scr[:, 0 * WIDTH:1 * WIDTH] = (da * u).astype(BF16)
        dc_scr[:, 1 * WIDTH:2 * WIDTH] = (dyc * conv * sc).astype(BF16)
        dc_scr[:, 2 * WIDTH:3 * WIDTH] = (da * pre).astype(BF16)
        dc_scr[:, 3 * WIDTH:4 * WIDTH] = (dyc * post * conv * dsc).astype(BF16)
        vec_ref[1:2, :] += jnp.sum(dconv * a2, axis=0, keepdims=True)
        vec_ref[2:3, :] += jnp.sum(dconv * a1, axis=0, keepdims=True)
        vec_ref[3:4, :] += jnp.sum(dconv * a, axis=0, keepdims=True)
        rows = pl.ds(pl.multiple_of(i * tm, tm), tm)
        copies = [pltpu.make_async_copy(dza_scr, dproj_ref.at[rows, 3 * WIDTH:4 * WIDTH], sems.at[0]),
                  pltpu.make_async_copy(dzb_scr, dproj_ref.at[rows, 7 * WIDTH:8 * WIDTH], sems.at[1]),
                  pltpu.make_async_copy(dc_scr, dproj_ref.at[rows, 8 * WIDTH:12 * WIDTH], sems.at[2])]
        for cp in copies:
            cp.start()
        for cp in copies:
            cp.wait()

    wcol = lambda cb: pl.BlockSpec((tm, WIDTH), lambda i: (i, cb))
    prev = lambda cb: pl.BlockSpec((8, WIDTH), lambda i: (jnp.maximum(i * hb - 1, 0), cb))
    nxt = lambda cb: pl.BlockSpec((8, WIDTH), lambda i: (jnp.minimum((i + 1) * hb, s_len // 8 - 1), cb))
    anyspec = pl.BlockSpec(memory_space=pl.ANY)
    out = jax.ShapeDtypeStruct((s_len, WIDTH), F32)
    return _pcall(
        body, name=name,
        out_shape=(jax.ShapeDtypeStruct(dproj.shape, dproj.dtype), out, out, jax.ShapeDtypeStruct((8, WIDTH), F32)),
        grid=(n_tiles,),
        in_specs=[wcol(0), wcol(1), wcol(2), wcol(0), wcol(3), wcol(0), wcol(7), wcol(8), wcol(9), wcol(10), wcol(11),
                  prev(8), prev(10), nxt(2), nxt(9), nxt(11),
                  pl.BlockSpec((1, WIDTH), lambda i: (0, 0)), pl.BlockSpec((3, WIDTH), lambda i: (0, 0)), anyspec],
        out_specs=(anyspec, wcol(0), wcol(0), pl.BlockSpec((8, WIDTH), lambda i: (0, 0))),
        scratch_shapes=[pltpu.VMEM((tm, WIDTH), BF16), pltpu.VMEM((tm, WIDTH), BF16),
                        pltpu.VMEM((tm, 4 * WIDTH), BF16), pltpu.SemaphoreType.DMA((3,))],
        aliases={18: 0},
        semantics=("arbitrary",))(dycat, dycat, dycat, o_a, proj, o_b, proj, proj, proj, proj, proj,
                                  proj, proj, dycat, proj, proj, norm_w, conv_w, dproj)


def _sb_bwd(proj, do_a, totals, dproj, name):
    s_len = proj.shape[0]
    nb = s_len // BLK
    n_pairs = WIDTH // BLK
    scale = SB_HEAD_DIM ** -0.5
    gb = _sb_group_blocks(nb)
    kw = gb * BLK

    def body(q_ref, k_ref, v_ref, do_ref, tot_ref, dproj_in, dproj_ref, dq_ref, dk_ref, dv_ref, out_scr, sems):
        del dproj_in
        lane = _iota2((1, BLK), 1)
        row = _iota2((BLK, BLK), 0)
        col = _iota2((BLK, BLK), 1)
        ones = jnp.ones((BLK, BLK), BF16)
        twice = lambda m: jnp.concatenate([m, m], axis=0)
        before_and_sum = twice(jnp.concatenate([(row < col).astype(BF16), ones], axis=1))
        upto_and_sum = twice(jnp.concatenate([(row <= col).astype(BF16), ones], axis=1))
        qpos = _iota2((BLK, kw), 0)
        kpos = _iota2((BLK, kw), 1)
        head_lanes = [(lane // SB_HEAD_DIM) == hh for hh in range(2)]
        dk_ref[...] = jnp.zeros_like(dk_ref)
        dv_ref[...] = jnp.zeros_like(dv_ref)

        causal = kpos - qpos

        def scores(i, gi, qms):
            c0 = pl.multiple_of(gi * kw, kw)
            kb = k_ref[pl.ds(c0, kw), :].astype(BF16)
            valid = causal < i * BLK - c0
            return tuple(jnp.where(valid, _dot_nt(qms[hh], kb), MASKED_SCORE) for hh in range(2))

        def process(gi, z2s, qms, doms, totals_i, carry):
            c0 = pl.multiple_of(gi * kw, kw)
            kf = k_ref[pl.ds(c0, kw), :]
            vf = v_ref[pl.ds(c0, kw), :]
            kms = [jnp.where(head_lanes[hh], kf, 0.0).astype(BF16) for hh in range(2)]
            vms = [jnp.where(head_lanes[hh], vf, 0.0).astype(BF16) for hh in range(2)]
            das = [_dot_nt(doms[hh], vms[hh]) for hh in range(2)]
            halves = [_softplus2_parts(z2) for z2 in z2s]
            terms = [[_split2_lanes(sp2[:, b * BLK:(b + 1) * BLK]) for b in range(gb)] for sp2, _ in halves]
            sums = [[_dot(t, before_and_sum) for t in head_terms] for head_terms in terms]
            weights, gmats, l_befores = [], [], []
            for hh in range(2):
                l_before = carry[3 * hh + 1]
                parts = []
                for b in range(gb):
                    parts.append(totals_i[hh] - l_before - sums[hh][b][:, :BLK])
                    l_before = l_before + sums[hh][b][:, BLK:]
                a = jnp.exp2(z2s[hh] - jnp.concatenate(parts, axis=1))
                weights.append(a.astype(BF16))
                gmats.append(a * das[hh])
                l_befores.append(l_before)
            terms = [[_split2_lanes(g[:, b * BLK:(b + 1) * BLK]) for b in range(gb)] for g in gmats]
            sums = [[_dot(t, upto_and_sum) for t in head_terms] for head_terms in terms]
            dzs, g_befores = [], []
            for hh in range(2):
                g_before = carry[3 * hh + 2]
                parts = []
                for b in range(gb):
                    parts.append(g_before + sums[hh][b][:, :BLK])
                    g_before = g_before + sums[hh][b][:, BLK:]
                dzs.append((gmats[hh] - halves[hh][1] * jnp.concatenate(parts, axis=1)).astype(BF16))
                g_befores.append(g_before)
            dks = [_dot_tn(dzs[hh], qms[hh]) for hh in range(2)]
            dvs = [_dot_tn(weights[hh], doms[hh]) for hh in range(2)]
            dqs = [_dot(dzs[hh], kms[hh]) for hh in range(2)]
            dk_ref[pl.ds(c0, kw), :] += (dks[0] + dks[1]) * (1.0 / LOG2E)
            dv_ref[pl.ds(c0, kw), :] += dvs[0] + dvs[1]
            return (carry[0] + dqs[0], l_befores[0], g_befores[0], carry[3] + dqs[1], l_befores[1], g_befores[1])

        def queries(i):
            qf = q_ref[pl.ds(pl.multiple_of(i * BLK, BLK), BLK), :] * (scale * LOG2E)
            return [jnp.where(head_lanes[hh], qf, 0.0).astype(BF16) for hh in range(2)]

        def qblock(i, first_scores):
            r0 = pl.multiple_of(i * BLK, BLK)
            qms = queries(i)
            dof = do_ref[pl.ds(r0, BLK), :]
            doms = [jnp.where(head_lanes[hh], dof, 0.0).astype(BF16) for hh in range(2)]
            totals_i = [tot_ref[hh, pl.ds(r0, BLK), :] for hh in range(2)]
            zero = jnp.zeros((BLK, BLK), F32)
            last = i // gb

            def step(gi, state):
                return scores(i, gi + 1, qms) + process(gi, state[:2], qms, doms, totals_i, state[2:])

            state = lax.fori_loop(0, last, step, first_scores + (zero,) * 6)
            nxt = jnp.minimum(i + 1, nb - 1)
            next_scores = scores(nxt, 0, queries(nxt))
            carry = process(last, state[:2], qms, doms, totals_i, state[2:])
            dq_ref[pl.ds(r0, BLK), :] = (carry[0] + carry[3]) * scale
            return next_scores

        lax.fori_loop(0, nb, qblock, scores(0, 0, queries(0)))
        pair = pl.program_id(0)
        copies = []
        for t, ref in enumerate((dq_ref, dk_ref, dv_ref)):
            out_scr[t] = ref[...].astype(BF16)
            col = pl.multiple_of((t * n_pairs + pair) * BLK, BLK)
            copies.append(pltpu.make_async_copy(out_scr.at[t], dproj_ref.at[:, pl.ds(col, BLK)], sems.at[t]))
            copies[-1].start()
        for cp in copies:
            cp.wait()

    col_spec = lambda off: pl.BlockSpec((s_len, BLK), lambda p: (0, off + p))
    anyspec = pl.BlockSpec(memory_space=pl.ANY)
    return _pcall(
        body, name=name, out_shape=jax.ShapeDtypeStruct(dproj.shape, dproj.dtype), grid=(n_pairs,),
        in_specs=[col_spec(0), col_spec(n_pairs), col_spec(2 * n_pairs), col_spec(0),
                  pl.BlockSpec((2, s_len, BLK), lambda p: (p, 0, 0)), anyspec],
        out_specs=anyspec,
        scratch_shapes=[pltpu.VMEM((s_len, BLK), F32)] * 3 + [pltpu.VMEM((3, s_len, BLK), BF16),
                                                              pltpu.SemaphoreType.DMA((3,))],
        aliases={5: 0},
        semantics=("arbitrary",))(proj, proj, proj, do_a, totals, dproj)


def _hgrn_bwd(proj, do_b, lb, dproj, name):
    s_len = proj.shape[0]
    nc = s_len // BLK
    nh = WIDTH // HG_HEAD_DIM
    base = 4 * WIDTH // BLK

    def body(q_ref, f_ref, i_ref, do_ref, lb_ref, dproj_in, dproj_ref, dlb_ref, mask_ref, st_ref, out_scr, sems):
        del dproj_in
        _hg_masks(mask_ref)
        row = _iota2((BLK, BLK), 0)
        col = _iota2((BLK, BLK), 1)
        lower_incl = (col <= row).astype(BF16)
        upper_incl = (col >= row).astype(BF16)
        lb_v = lb_ref[...]

        def load(ci):
            r0 = pl.multiple_of(ci * BLK, BLK)
            q, dq_fac, f, sig, g = _hg_chunk_inputs(q_ref[pl.ds(r0, BLK), :], f_ref[pl.ds(r0, BLK), :], lb_v)
            b = _dot_exact_l(lower_incl, g)
            return r0, q, dq_fac, f, sig, b, i_ref[pl.ds(r0, BLK), :]

        def fwd_chunk(ci, st):
            st_ref[ci] = st
            _, _, _, f, _, b, v = load(ci)
            b_end = b[BLK - 1:BLK, :]
            k_dec = ((1.0 - f) * jnp.exp(b_end - b)).astype(BF16)
            return st * jnp.exp(b_end) + _dot_tn(v.astype(BF16), k_dec)

        lax.fori_loop(0, nc, fwd_chunk, jnp.zeros((HG_HEAD_DIM, HG_HEAD_DIM), F32))

        def bwd_chunk(cc, carry):
            dst, suffix, dlb = carry
            ci = nc - 1 - cc
            r0, q, dq_fac, f, sig, b, v = load(ci)
            k = 1.0 - f
            vb = v.astype(BF16)
            do = do_ref[pl.ds(r0, BLK), :]
            dob = do.astype(BF16)
            b_end = b[BLK - 1:BLK, :]
            e_q = jnp.exp(b)
            e_k = jnp.exp(b_end - b)
            qe = (q * e_q).astype(BF16)
            kh = (k * e_k).astype(BF16)
            st1, st2 = _split2(st_ref[ci])
            ds1, ds2 = _split2(dst)
            dqe = _dot(dob, st1) + _dot(dob, st2)
            dkh = _dot(vb, ds1) + _dot(vb, ds2)
            dq = e_q * dqe
            dk = e_k * dkh
            dv = _dot_nt(kh, ds1)
            dst_new = dst * jnp.exp(b_end) + _dot_tn(dob, qe)
            dlog = qe.astype(F32) * dqe - kh.astype(F32) * dkh
            da = _dot_nt(dob, vb)
            sc = None
            for v_idx, m in enumerate(HG_LEVELS):
                qm, km, eq, ek, msk = _hg_level_terms(q, k, b, v_idx, m, mask_ref)
                term = _dot_nt(qm, km) * msk
                sc = term if sc is None else sc + term
                pm = (da * msk).astype(BF16)
                dqm = _dot(pm, km)
                dkm = _dot_tn(pm, qm)
                dq = dq + dqm * eq
                dk = dk + dkm * ek
                dlog = dlog + (qm.astype(F32) * dqm - km.astype(F32) * dkm)
            a_diag = jnp.sum(do * v, axis=-1, keepdims=True)
            s_diag = jnp.sum(q * k, axis=-1, keepdims=True)
            dq = dq + a_diag * k
            dk = dk + a_diag * q
            dv = dv + _dot_tn(sc.astype(BF16), dob) + s_diag * do
            dg = _dot_exact_l(upper_incl, dlog) + suffix
            dfull = dg / f - dk
            out_scr[0, pl.ds(r0, BLK), :] = (dq * dq_fac).astype(BF16)
            out_scr[1, pl.ds(r0, BLK), :] = (dfull * (1.0 - lb_v) * sig * (1.0 - sig)).astype(BF16)
            out_scr[2, pl.ds(r0, BLK), :] = dv.astype(BF16)
            dlb = dlb + jnp.sum(dfull * (1.0 - sig), axis=0, keepdims=True)
            return dst_new, dg[0:1, :], dlb

        zero_row = jnp.zeros((1, BLK), F32)
        _, _, dlb = lax.fori_loop(0, nc, bwd_chunk,
                                  (jnp.zeros((HG_HEAD_DIM, HG_HEAD_DIM), F32), zero_row, zero_row))
        dlb_ref[...] = jnp.broadcast_to(dlb, dlb_ref.shape)
        head = pl.program_id(0)
        copies = []
        for t in range(3):
            col = pl.multiple_of((base + t * nh + head) * BLK, BLK)
            copies.append(pltpu.make_async_copy(out_scr.at[t], dproj_ref.at[:, pl.ds(col, BLK)], sems.at[t]))
            copies[-1].start()
        for cp in copies:
            cp.wait()

    col_spec = lambda off: pl.BlockSpec((s_len, BLK), lambda h: (0, off + h))
    anyspec = pl.BlockSpec(memory_space=pl.ANY)
    return _pcall(
        body, name=name,
        out_shape=(jax.ShapeDtypeStruct(dproj.shape, dproj.dtype), jax.ShapeDtypeStruct((8, WIDTH), F32)),
        grid=(nh,),
        in_specs=[col_spec(base), col_spec(base + nh), col_spec(base + 2 * nh), col_spec(0),
                  pl.BlockSpec((1, BLK), lambda h: (0, h)), anyspec],
        out_specs=(anyspec, pl.BlockSpec((8, BLK), lambda h: (0, h))),
        scratch_shapes=[pltpu.VMEM((len(HG_LEVELS), BLK, BLK), F32),
                        pltpu.VMEM((nc, HG_HEAD_DIM, HG_HEAD_DIM), F32),
                        pltpu.VMEM((3, s_len, BLK), BF16), pltpu.SemaphoreType.DMA((3,))],
        aliases={5: 0},
        semantics=("arbitrary",))(proj, proj, proj, do_b, lb, dproj)


def _dh_matmul(dproj, w_full, name):
    s_len, n = dproj.shape
    d = w_full.shape[0]
    tm = min(512, s_len)
    tk = 1536

    def body(dp_ref, w_ref, dh_ref):
        part = _dot_nt(dp_ref[...], w_ref[...])

        @pl.when(pl.program_id(1) == 0)
        def _():
            dh_ref[...] = part

        @pl.when(pl.program_id(1) > 0)
        def _():
            dh_ref[...] += part

    return _pcall(
        body, name=name, out_shape=jax.ShapeDtypeStruct((s_len, d), F32),
        grid=(s_len // tm, n // tk),
        in_specs=[pl.BlockSpec((tm, tk), lambda i, k: (i, k)), pl.BlockSpec((d, tk), lambda i, k: (0, k))],
        out_specs=pl.BlockSpec((tm, d), lambda i, k: (i, 0)),
        semantics=("arbitrary", "arbitrary"))(dproj, w_full)


def _gw_matmul(h_t, dproj, name):
    d, s_len = h_t.shape
    n = dproj.shape[1]
    tn = 1152

    def body(ht_ref, dp_ref, gw_ref):
        gw_ref[...] = _dot(ht_ref[...], dp_ref[...]).astype(BF16)

    return _pcall(
        body, name=name, out_shape=jax.ShapeDtypeStruct((d, n), BF16),
        grid=(n // tn,),
        in_specs=[pl.BlockSpec((d, s_len), lambda j: (0, 0)), pl.BlockSpec((s_len, tn), lambda j: (0, j))],
        out_specs=pl.BlockSpec((d, tn), lambda j: (0, j)),
        semantics=("arbitrary",))(h_t, dproj)


def _ln_bwd(dh, x, scale, dres, name):
    s_len, d = x.shape
    tm = min(512, s_len)

    def body(dh_ref, x_ref, sc_ref, dres_ref, dx_ref, vec_ref):
        @pl.when(pl.program_id(0) == 0)
        def _():
            vec_ref[...] = jnp.zeros_like(vec_ref)

        dh = dh_ref[...]
        xs, rstd = _standardize(x_ref[...])
        vec_ref[0:1, :] += jnp.sum(dh, axis=0, keepdims=True)
        vec_ref[1:2, :] += jnp.sum(dh * xs, axis=0, keepdims=True)
        dx_ref[...] = _standardize_bwd(xs, rstd, dh * (1.0 + sc_ref[...])) + dres_ref[...]

    tile = pl.BlockSpec((tm, d), lambda i: (i, 0))
    return _pcall(body, name=name, grid=(s_len // tm,),
                  out_shape=(jax.ShapeDtypeStruct((s_len, d), F32), jax.ShapeDtypeStruct((8, d), F32)),
                  in_specs=[tile, tile, pl.BlockSpec((1, d), lambda i: (0, 0)), tile],
                  out_specs=(tile, pl.BlockSpec((8, d), lambda i: (0, 0))),
                  semantics=("arbitrary",))(dh, x, scale, dres)


def _wmod_grad(c_t, dmod):
    d = c_t.shape[0]
    n_layers, _, cm = dmod.shape

    def body(c_ref, dm_ref, o_ref):
        for l in range(n_layers):
            acc = None
            for b in range(NDEV):
                term = c_ref[:, b:b + 1] * dm_ref[l, b:b + 1, :]
                acc = term if acc is None else acc + term
            o_ref[l] = acc

    return _pcall(body, name="wmod_grad", out_shape=jax.ShapeDtypeStruct((n_layers, d, cm), F32))(c_t, dmod)


def _sum_adamw(parts_list, w, m, v, name):
    n_ranges = len(parts_list)
    n_src, range_rows, cols = parts_list[0].shape
    rows = range_rows * n_ranges
    tr = range_rows
    for cand in (512, 256, 128, 64, 32, 16, 8):
        if range_rows % cand == 0 and cand * cols * 4 <= (2 << 20):
            tr = cand
            break
    tiles = range_rows // tr

    def body(*refs):
        p_refs = refs[:n_ranges]
        w_ref, m_ref, v_ref, g_ref, d_ref, nm_ref, nv_ref = refs[n_ranges:]

        def step(p_ref):
            g = p_ref[0].astype(F32)
            for s in range(1, n_src):
                g = g + p_ref[s].astype(F32)
            nm = ADAM_B1 * m_ref[...] + (1.0 - ADAM_B1) * g
            nv = ADAM_B2 * v_ref[...] + (1.0 - ADAM_B2) * (g * g)
            m_hat = nm / (1.0 - ADAM_B1 ** ADAM_STEP)
            v_hat = nv / (1.0 - ADAM_B2 ** ADAM_STEP)
            g_ref[...] = g
            d_ref[...] = -ADAM_LR * (m_hat / (jnp.sqrt(v_hat) + ADAM_EPS) + ADAM_WD * w_ref[...])
            nm_ref[...] = nm
            nv_ref[...] = nv

        if n_ranges == 1:
            step(p_refs[0])
        else:
            for j in range(n_ranges):
                @pl.when(pl.program_id(0) // tiles == j)
                def _(j=j):
                    step(p_refs[j])

    def part_spec(j):
        return pl.BlockSpec((n_src, tr, cols), lambda i: (0, jnp.clip(i - j * tiles, 0, tiles - 1), 0))

    tile = pl.BlockSpec((tr, cols), lambda i: (i, 0))
    out = jax.ShapeDtypeStruct((rows, cols), F32)
    return _pcall(body, name=name, grid=(rows // tr,), out_shape=(out,) * 4,
                  in_specs=[part_spec(j) for j in range(n_ranges)] + [tile, tile, tile],
                  out_specs=(tile,) * 4, semantics=("arbitrary",))(*parts_list, w, m, v)


def _sum_parts(parts, name):
    n_src = parts.shape[0]

    def body(p_ref, o_ref):
        acc = p_ref[0]
        for s in range(1, n_src):
            acc = acc + p_ref[s]
        o_ref[...] = acc

    return _pcall(body, name=name, out_shape=jax.ShapeDtypeStruct(parts.shape[1:], F32))(parts)


def _pair_sum(gw, stage, me, name):
    d = gw.shape[0]
    n_slots, _, shard = stage.shape

    def body(me_ref, g_ref, s_ref, o_ref):
        del me_ref
        o_ref[0] = (g_ref[...].astype(F32) + s_ref[0].astype(F32)).astype(BF16)

    slot = pl.BlockSpec((1, d, shard), lambda jj, me_ref: (jj, 0, 0))
    return pl.pallas_call(
        body, name=name, out_shape=jax.ShapeDtypeStruct(stage.shape, BF16),
        grid_spec=pltpu.PrefetchScalarGridSpec(
            num_scalar_prefetch=1, grid=(n_slots,),
            in_specs=[pl.BlockSpec((d, shard), lambda jj, me_ref: (0, me_ref[0] ^ (2 * jj))), slot],
            out_specs=slot),
        compiler_params=pltpu.CompilerParams(dimension_semantics=("arbitrary",), vmem_limit_bytes=VMEM_LIMIT),
        interpret=False)(me.reshape(1).astype(jnp.int32), gw, stage)


def _lower_bound_table(lower_bounds):
    p = jax.nn.softmax(lower_bounds.astype(F32), axis=0)
    return jnp.cumsum(p, axis=0) - p[0:1]


def _pad_rows(v, width):
    n = v.shape[0]
    rows = -(-n // width)
    rows = -(-rows // 8) * 8
    return jnp.pad(v, (0, rows * width - n)).reshape(rows, width)


def kernel(x, c, w_mod, b_mod, w_in, conv_w, hgrn_norm_w, lower_bounds, w_branch, w_out, ln_g, ln_b, loss_target, m_w_mod, m_b_mod, m_w_in, m_conv_w, m_hgrn_norm_w, m_lower_bounds, m_w_branch, m_w_out, m_ln_g, m_ln_b, v_w_mod, v_b_mod, v_w_in, v_conv_w, v_hgrn_norm_w, v_lower_bounds, v_w_branch, v_w_out, v_ln_g, v_ln_b):
    n_layers = N_LAYERS
    s_len, d = x.shape[1], x.shape[2]
    n_cols = w_in.shape[2] * NDEV
    cw_cols = conv_w.shape[2]
    cm = w_mod.shape[2]
    me = _my_index()
    x0 = x[0]
    target = loss_target[0]

    small = _pad_rows(jnp.concatenate([c.reshape(-1), conv_w.reshape(-1)]), BLK)
    small_all = _all_gather_small("gather_c_conv", small).reshape(NDEV, -1)
    c_all = small_all[:, :d]
    conv_full = small_all[:, d:d + n_layers * 3 * cw_cols].reshape(NDEV, n_layers, 3, cw_cols)
    conv_full = conv_full.transpose(1, 2, 0, 3).reshape(n_layers, 3, WIDTH)

    b_mod_mine = lax.dynamic_slice_in_dim(b_mod, me * cm, cm, axis=1).reshape(n_layers, 1, cm)
    mod_cols = _mod_fwd(c_all, w_mod, b_mod_mine)
    mod_all = _all_gather_small("gather_mod", mod_cols.reshape(n_layers * NDEV, cm))
    mod_all = mod_all.reshape(NDEV, n_layers, NDEV, cm)
    mod_mine = lax.dynamic_index_in_dim(mod_all, me, axis=2, keepdims=False)
    mod_mine = mod_mine.transpose(1, 0, 2).reshape(n_layers, 3, 1, d)

    shard = w_in.shape[2]
    dsh = d // NDEV
    w_in_b, w_branch_b, w_out_b = w_in.astype(BF16), w_branch.astype(BF16), w_out.astype(BF16)
    window = lambda ref, dev: ref.at[:, pl.ds(pl.multiple_of(dev * shard, BLK), shard)]

    def two_step_sends(places):
        chips, sibling = [], []
        for k in (1, 2, 4, 6):
            for a, place in enumerate(places):
                chips.append((k, lambda ins, lands, me, a=a: ins[a],
                              lambda lands, me, a=a, place=place: place(lands[a], me),
                              lambda lands, me, a=a, k=k, place=place: place(lands[a], me ^ k)))
        for j in (2, 4, 6):
            for a, place in enumerate(places):
                sibling.append((1, lambda ins, lands, me, a=a, j=j, place=place: place(lands[a], me ^ j),
                                lambda lands, me, a=a, j=j, place=place: place(lands[a], me ^ j),
                                lambda lands, me, a=a, j=j, place=place: place(lands[a], me ^ 1 ^ j)))
        return chips, sibling

    in_sends = two_step_sends([window])
    rest_sends = two_step_sends([_slot, _slot])
    layer_sends = two_step_sends([window, _slot, _slot])

    def in_land(l):
        return _place_own((d, n_cols), BF16, w_in_b[l], (0, me * shard))

    def rest_lands(l):
        return [_place_own((NDEV, 3, WIDTH, dsh), BF16, w_branch_b[l][None], (me, 0, 0, 0)),
                _place_own((NDEV, dsh, d), BF16, w_out_b[l][None], (me, 0, 0))]

    def gather_start(name, shards, lands, sends, tie):
        if tie is not None:
            tie, lands = lax.optimization_barrier((tie, lands))
        return _exchange_start(f"{name}_chips_start", shards, lands, sends[0])

    def gather_pass_on(name, started, after, sends):
        lands = _exchange_wait(f"{name}_chips_wait", started, after, sends[0])
        return _exchange_start(f"{name}_sibling_start", [], lands, sends[1])

    def gather_finish(name, started, after, sends):
        return _exchange_wait(f"{name}_sibling_wait", started, after, sends[1])

    def branch_out_weights(w_branch_l, w_out_l):
        return w_branch_l.transpose(1, 2, 0, 3).reshape(3, WIDTH, d), w_out_l.reshape(d, d)

    gathering = gather_start("gather_w_in_0", [w_in_b[0]], [in_land(0)], in_sends, mod_mine)
    passing = gather_pass_on("gather_w_in_0", gathering, gathering[4], in_sends)
    rest_gathering = gather_start("gather_rest_0", [w_branch_b[0], w_out_b[0]], rest_lands(0), rest_sends, passing[4])
    next_gathering = None
    if n_layers > 1:
        next_gathering = gather_start("gather_weights_1", [w_in_b[1], w_branch_b[1], w_out_b[1]],
                                      [in_land(1)] + rest_lands(1), layer_sends, rest_gathering[4])
    w_in_l = gather_finish("gather_w_in_0", passing, (next_gathering or rest_gathering)[4], in_sends)[0]

    lbs = _lower_bound_table(lower_bounds)
    norm_w4 = jnp.tile(hgrn_norm_w, (1, WIDTH // HG_HEAD_DIM))

    saved = []
    xl = x0
    for l in range(n_layers):
        shift, scale, gate = mod_mine[l, 0], mod_mine[l, 1], mod_mine[l, 2]
        proj, h_t = _ln_proj(xl, shift, scale, w_in_l, f"ln_proj_{l}")
        o_a, totals = _sb_fwd(proj, f"sb_fwd_{l}")
        if l == 0:
            rest_passing = gather_pass_on("gather_rest_0", rest_gathering, o_a, rest_sends)
        o_b = _hgrn_fwd(proj, lbs[l:l + 1], f"hgrn_fwd_{l}")
        if l == 0:
            wb_l, wo_l = branch_out_weights(*gather_finish("gather_rest_0", rest_passing, o_b, rest_sends))
            if n_layers > 1:
                next_passing = gather_pass_on("gather_weights_1", next_gathering, o_b, layer_sends)
                gate = gate + next_passing[4][0, 0]
        x_new, merged, ycat = _merge_fwd(xl, proj, o_a, o_b, gate, norm_w4[l:l + 1], conv_full[l],
                                         wb_l, wo_l, ln_g[l:l + 1], ln_b[l:l + 1], f"merge_fwd_{l}")
        saved.append((xl, proj, h_t, o_a, totals, o_b, merged, ycat, w_in_l, wb_l, wo_l))
        if l == 0 and n_layers > 1:
            w_in_l, w_branch_l, w_out_l = gather_finish("gather_weights_1", next_passing, x_new, layer_sends)
            wb_l, wo_l = branch_out_weights(w_branch_l, w_out_l)
        xl = x_new

    loss_part, dx = _loss_fwd_bwd(xl, target)
    loss = lax.psum(loss_part[0, 0], ("x", "y", "c"))

    pair_sends = [(1, lambda ins, lands, me, j=j: window(ins[0], me ^ 1 ^ j),
                   lambda lands, me, jj=jj: lands[0].at[jj], lambda lands, me, jj=jj: lands[0].at[jj])
                  for jj, j in enumerate((0, 2, 4, 6))]
    chip_sum_sends = [(j, lambda ins, lands, me, jj=jj: ins[0].at[jj],
                       lambda lands, me, jj=jj: lands[0].at[jj], lambda lands, me, jj=jj: lands[0].at[jj])
                      for jj, j in ((1, 2), (2, 4), (3, 6))]
    rest_scatter = _direct_sends([(0, 0, _slot, _slot), (1, 1, _slot, _slot)])
    scattering = [None] * n_layers
    small_grads = [None] * n_layers
    dmod = [None] * n_layers
    tie = None
    for l in reversed(range(n_layers)):
        xl, proj, h_t, o_a, totals, o_b, merged, ycat, w_in_l, wb_l, wo_l = saved[l]
        scale, gate = mod_mine[l, 1], mod_mine[l, 2]
        if tie is not None:
            gate = gate + tie[0, 0]
        dres, dycat, dproj, gwo, gwb, mvec = _merge_bwd(dx, xl, merged, ycat, proj, gate, wb_l, wo_l,
                                                        ln_g[l:l + 1], f"merge_bwd_{l}")
        gwb_by_owner = gwb.astype(BF16).reshape(3, WIDTH, NDEV, dsh).transpose(2, 0, 1, 3)
        gwo_by_owner = gwo.astype(BF16).reshape(NDEV, dsh, d)
        lands = [_place_own((NDEV, 3, WIDTH, dsh), BF16, lax.dynamic_slice_in_dim(gwb_by_owner, me, 1, axis=0),
                            (me, 0, 0, 0)),
                 _place_own((NDEV, dsh, d), BF16, lax.dynamic_slice_in_dim(gwo_by_owner, me, 1, axis=0),
                            (me, 0, 0))]
        rest_started = _exchange_start(f"scatter_rest_{l}_start", [gwb_by_owner, gwo_by_owner], lands, rest_scatter)
        dproj, do_a, do_b, bvec = _branch_bwd(dycat, proj, o_a, o_b, norm_w4[l:l + 1] + rest_started[4][0, 0],
                                              conv_full[l], dproj, f"branch_bwd_{l}")
        dproj = _sb_bwd(proj, do_a, totals, dproj, f"sb_bwd_{l}")
        dproj, dlb = _hgrn_bwd(proj, do_b, lbs[l:l + 1], dproj, f"hgrn_bwd_{l}")
        gwi = _gw_matmul(h_t, dproj, f"gw_matmul_{l}")
        swapping = _exchange_start(f"scatter_in_{l}_sibling_start", [gwi], [lax.empty((4, d, shard), BF16)], pair_sends)
        stage = _exchange_wait(f"scatter_in_{l}_sibling_wait", swapping, swapping[4], pair_sends)[0]
        chip_sums = _pair_sum(gwi, stage, me, f"pair_sum_{l}")
        in_started = _exchange_start(f"scatter_in_{l}_chips_start", [chip_sums],
                                     [_place_own((4, d, shard), BF16, chip_sums[0:1], (0, 0, 0))], chip_sum_sends)
        scattering[l] = (in_started, rest_started)
        tie = in_started[4]
        dh = _dh_matmul(dproj, w_in_l, f"dh_matmul_{l}")
        dx, lvec = _ln_bwd(dh, xl, scale + tie[0, 0], dres, f"ln_bwd_{l}")
        dmod[l] = jnp.concatenate([lvec[0], lvec[1], mvec[2]])
        norm_grad = bvec[0].reshape(WIDTH // HG_HEAD_DIM, HG_HEAD_DIM).sum(axis=0)
        small_grads[l] = jnp.concatenate([mvec[0], mvec[1], norm_grad, dlb[0], bvec[1:4].reshape(-1)])
    grad_x = dx[None]

    small_vec = jnp.concatenate(dmod + small_grads)
    n_small = small_vec.shape[0]
    small_all = _all_gather_small("gather_small_grads", _pad_rows(small_vec, BLK))
    small_sum = _sum_parts(small_all, "sum_small_grads").reshape(-1)[:n_small]
    dmod_all = small_all.reshape(NDEV, -1)[:, :n_layers * 3 * d].reshape(NDEV, n_layers, 3 * d)

    off = n_layers * 3 * d
    grad_b_mod = small_sum[:off].reshape(n_layers, 3 * d)
    per_layer = 2 * d + HG_HEAD_DIM + WIDTH + 3 * WIDTH
    g_ln_g, g_ln_b, g_norm, g_lbs, g_conv = [], [], [], [], []
    for l in range(n_layers):
        seg = small_sum[off + l * per_layer: off + (l + 1) * per_layer]
        g_ln_g.append(seg[:d])
        g_ln_b.append(seg[d:2 * d])
        g_norm.append(seg[2 * d:2 * d + HG_HEAD_DIM])
        g_lbs.append(seg[2 * d + HG_HEAD_DIM:2 * d + HG_HEAD_DIM + WIDTH])
        g_conv.append(seg[2 * d + HG_HEAD_DIM + WIDTH:].reshape(3, WIDTH))
    grad_ln_g, grad_ln_b = jnp.stack(g_ln_g), jnp.stack(g_ln_b)
    grad_norm = jnp.stack(g_norm)
    _, lbs_vjp = jax.vjp(_lower_bound_table, lower_bounds)
    grad_lower = lbs_vjp(jnp.stack(g_lbs))[0]
    grad_conv = lax.dynamic_slice_in_dim(jnp.stack(g_conv), me * cw_cols, cw_cols, axis=2)

    dmod_mine = lax.dynamic_slice_in_dim(dmod_all, me * cm, cm, axis=2).transpose(1, 0, 2)
    grad_w_mod = _wmod_grad(c_all.T, dmod_mine)

    p_in, p_branch, p_out = [None] * n_layers, [None] * n_layers, [None] * n_layers
    for l in reversed(range(n_layers)):
        in_started, rest_started = scattering[l]
        p_branch_l, p_out[l] = _exchange_wait(f"scatter_rest_{l}_wait", rest_started, grad_w_mod, rest_scatter)
        p_branch[l] = p_branch_l.reshape(NDEV, 3 * WIDTH, dsh)
        p_in[l] = _exchange_wait(f"scatter_in_{l}_chips_wait", in_started, grad_w_mod, chip_sum_sends)[0]

    def adam(parts_list, w, m, v, name):
        shape = w.shape
        cols = shape[-1]
        flat = lambda a: a.reshape(-1, cols)
        outs = _sum_adamw(parts_list, flat(w), flat(m), flat(v), name)
        return [o.reshape(shape) for o in outs]

    r_w_in = adam(p_in, w_in, m_w_in, v_w_in, "adamw_w_in")
    r_w_branch = adam(p_branch, w_branch, m_w_branch, v_w_branch, "adamw_w_branch")
    r_w_out = adam(p_out, w_out, m_w_out, v_w_out, "adamw_w_out")
    r_w_mod = adam([grad_w_mod.reshape(1, -1, cm)], w_mod, m_w_mod, v_w_mod, "adamw_w_mod")

    small_names = ["b_mod", "conv_w", "hgrn_norm_w", "lower_bounds", "ln_g", "ln_b"]
    small_g = [grad_b_mod, grad_conv, grad_norm, grad_lower, grad_ln_g, grad_ln_b]
    small_w = [b_mod, conv_w, hgrn_norm_w, lower_bounds, ln_g, ln_b]
    small_m = [m_b_mod, m_conv_w, m_hgrn_norm_w, m_lower_bounds, m_ln_g, m_ln_b]
    small_v = [v_b_mod, v_conv_w, v_hgrn_norm_w, v_lower_bounds, v_ln_g, v_ln_b]
    pack = lambda arrs: _pad_rows(jnp.concatenate([a.reshape(-1) for a in arrs]), BLK)
    packed = _sum_adamw([pack(small_g)[None]], pack(small_w), pack(small_m), pack(small_v), "adamw_small")
    r_small = {n: [] for n in small_names}
    for res in packed:
        flat = res.reshape(-1)
        pos = 0
        for n, w in zip(small_names, small_w):
            r_small[n].append(flat[pos:pos + w.size].reshape(w.shape))
            pos += w.size

    results = {"w_mod": r_w_mod, "w_in": r_w_in, "w_branch": r_w_branch, "w_out": r_w_out, **r_small}
    order = ["w_mod", "b_mod", "w_in", "conv_w", "hgrn_norm_w", "lower_bounds", "w_branch", "w_out", "ln_g", "ln_b"]
    outs = [loss, grad_x]
    for idx in range(4):
        outs.extend(results[n][idx] for n in order)
    return tuple(outs)
```

```python
import jax
import jax.numpy as jnp
from jax import lax
from jax.experimental import pallas as pl
from jax.experimental.pallas import tpu as pltpu

F32 = jnp.float32
BF16 = jnp.bfloat16
NDEV = 8
N_LAYERS = 2
SB_HEAD_DIM = 64
HG_HEAD_DIM = 128
WIDTH = 512
BLK = 128
LN_EPS = 1e-5
RMS_EPS = 1e-6
ALPHA = (2.0 * N_LAYERS) ** 0.25
ADAM_LR, ADAM_B1, ADAM_B2, ADAM_EPS, ADAM_WD, ADAM_STEP = 0.001, 0.9, 0.999, 1e-08, 0.01, 10
VMEM_LIMIT = 56 * 1024 * 1024
MESH = pl.DeviceIdType.MESH
HG_LEVELS = (64, 32, 16, 8, 4, 2, 1)


def _pcall(body, *, name, out_shape, grid=None, in_specs=None, out_specs=None, scratch_shapes=(),
           semantics=None, aliases=None):
    kwargs = {}
    if grid is not None:
        kwargs["grid"] = grid
    if in_specs is not None:
        kwargs["in_specs"] = in_specs
    if out_specs is not None:
        kwargs["out_specs"] = out_specs
    if aliases:
        kwargs["input_output_aliases"] = aliases
    return pl.pallas_call(
        body, name=name, out_shape=out_shape, scratch_shapes=list(scratch_shapes),
        compiler_params=pltpu.CompilerParams(dimension_semantics=semantics, vmem_limit_bytes=VMEM_LIMIT),
        interpret=False, **kwargs)


def _dot(a, b):
    return jnp.dot(a, b, preferred_element_type=F32)


def _dot_nt(a, b):
    return lax.dot_general(a, b, (((1,), (1,)), ((), ())), preferred_element_type=F32)


def _dot_tn(a, b):
    return lax.dot_general(a, b, (((0,), (0,)), ((), ())), preferred_element_type=F32)


def _split3(x):
    x1 = x.astype(BF16)
    r1 = x - x1.astype(F32)
    x2 = r1.astype(BF16)
    r2 = r1 - x2.astype(F32)
    return x1, x2, r2.astype(BF16)


def _split2(x):
    x1 = x.astype(BF16)
    return x1, (x - x1.astype(F32)).astype(BF16)


def _dot_exact_l(m_bf16, x):
    x1, x2, x3 = _split3(x)
    return _dot(m_bf16, x1) + _dot(m_bf16, x2) + _dot(m_bf16, x3)


def _sigmoid(x):
    return 1.0 / (1.0 + jnp.exp(-x))


def _silu_and_grad(x):
    s = _sigmoid(x)
    return x * s, s * (1.0 + x * (1.0 - s))


LOG2E = 1.4426950408889634
MASKED_SCORE = -1e30


def _softplus2_parts(z2):
    minus_abs = lax.bitcast_convert_type(lax.bitcast_convert_type(z2, jnp.int32) | jnp.int32(-2 ** 31), F32)
    e = jnp.exp2(minus_abs)
    sp2 = jnp.maximum(z2, 0.0) + jnp.log2(1.0 + e)
    r = 1.0 / (1.0 + e)
    return sp2, jnp.where(z2 >= 0.0, r, e * r)


def _split2_lanes(x):
    x1 = x.astype(BF16)
    return jnp.concatenate([x1, (x - x1.astype(F32)).astype(BF16)], axis=1)


def _iota2(shape, dim):
    return lax.broadcasted_iota(jnp.int32, shape, dim)


def _standardize(x):
    mu = jnp.mean(x, axis=-1, keepdims=True)
    xc = x - mu
    var = jnp.mean(xc * xc, axis=-1, keepdims=True)
    rstd = lax.rsqrt(var + LN_EPS)
    return xc * rstd, rstd


def _standardize_bwd(xhat, rstd, dxhat):
    m1 = jnp.mean(dxhat, axis=-1, keepdims=True)
    m2 = jnp.mean(dxhat * xhat, axis=-1, keepdims=True)
    return rstd * (dxhat - m1 - xhat * m2)


def _my_index():
    return 4 * lax.axis_index("x") + 2 * lax.axis_index("y") + lax.axis_index("c")


def _exchange(name, ins, out_shapes, transfers, in_vmem):
    n_in, n_out, n_t = len(ins), len(out_shapes), len(transfers)

    def body(*refs):
        in_refs, out_refs = refs[:n_in], refs[n_in:n_in + n_out]
        send_sems, recv_sems, local_sems = refs[n_in + n_out:]
        x, y, c = lax.axis_index("x"), lax.axis_index("y"), lax.axis_index("c")
        me = 4 * x + 2 * y + c
        started = []
        for t, (i, o, src_fn, dst_fn) in enumerate(transfers):
            own = pltpu.make_async_copy(src_fn(in_refs[i], me), dst_fn(out_refs[o], me), local_sems.at[t])
            own.start()
            started.append(own)
        arrivals = []
        for k in range(1, NDEV):
            px = x ^ ((k >> 2) & 1)
            py = y ^ ((k >> 1) & 1)
            pc = c ^ (k & 1)
            peer = 4 * px + 2 * py + pc
            for t, (i, o, src_fn, dst_fn) in enumerate(transfers):
                sem = t * (NDEV - 1) + k - 1
                push = pltpu.make_async_remote_copy(
                    src_ref=src_fn(in_refs[i], peer), dst_ref=dst_fn(out_refs[o], me),
                    send_sem=send_sems.at[sem], recv_sem=recv_sems.at[sem],
                    device_id=(px, py, pc), device_id_type=MESH)
                push.start()
                started.append(push)
                arrivals.append(pltpu.make_async_remote_copy(
                    src_ref=src_fn(in_refs[i], peer), dst_ref=dst_fn(out_refs[o], peer),
                    send_sem=send_sems.at[sem], recv_sem=recv_sems.at[sem],
                    device_id=(px, py, pc), device_id_type=MESH))
        for arrival in arrivals:
            arrival.wait_recv()
        for cp in started[n_t:]:
            cp.wait_send()
        for own in started[:n_t]:
            own.wait()

    space = pltpu.VMEM if in_vmem else pl.ANY
    spec = pl.BlockSpec(memory_space=space)
    return _pcall(
        body, name=name, out_shape=out_shapes,
        in_specs=[spec] * n_in, out_specs=[spec] * n_out,
        scratch_shapes=[pltpu.SemaphoreType.DMA((n_t * (NDEV - 1),)),
                        pltpu.SemaphoreType.DMA((n_t * (NDEV - 1),)),
                        pltpu.SemaphoreType.DMA((n_t,))])(*ins)


def _whole(ref, dev):
    return ref


def _slot(ref, dev):
    return ref.at[dev]


def _all_gather_small(name, v):
    out = _exchange(name, [v], [jax.ShapeDtypeStruct((NDEV,) + v.shape, v.dtype)],
                    [(0, 0, _whole, _slot)], in_vmem=True)
    return out[0]


_HBM_SPEC = pl.BlockSpec(memory_space=pltpu.HBM)
_SEM_SPEC = pl.BlockSpec(memory_space=pltpu.SEMAPHORE)
_DATAFLOW = pltpu.SideEffectType.DATAFLOW_SIDE_EFFECTING


def _peer(x, y, c, k):
    px = x ^ ((k >> 2) & 1)
    py = y ^ ((k >> 1) & 1)
    pc = c ^ (k & 1)
    return (px, py, pc), 4 * px + 2 * py + pc


def _direct_sends(transfers):
    sends = []
    for k in range(1, NDEV):
        for i, o, src_fn, dst_fn in transfers:
            sends.append((k,
                          lambda ins, lands, me, i=i, k=k, src_fn=src_fn: src_fn(ins[i], me ^ k),
                          lambda lands, me, o=o, dst_fn=dst_fn: dst_fn(lands[o], me),
                          lambda lands, me, o=o, k=k, dst_fn=dst_fn: dst_fn(lands[o], me ^ k)))
    return sends


def _exchange_start(name, ins, lands, sends, after=None):
    n_in, n_buf = len(ins), len(ins) + len(lands)
    n_sem = len(sends)

    def body(*refs):
        in_refs, land_refs = refs[:n_in], refs[n_in:n_buf]
        n_skip = n_buf + (0 if after is None else 1)
        send_sems, recv_sems, token = refs[n_skip], refs[n_skip + 1], refs[-1]
        x, y, c = lax.axis_index("x"), lax.axis_index("y"), lax.axis_index("c")
        me = 4 * x + 2 * y + c
        for t, (k, src_fn, dst_fn, _) in enumerate(sends):
            pltpu.make_async_remote_copy(
                src_ref=src_fn(in_refs, land_refs, me), dst_ref=dst_fn(land_refs, me),
                send_sem=send_sems.at[t], recv_sem=recv_sems.at[t],
                device_id=_peer(x, y, c, k)[0], device_id_type=MESH).start()
        token[...] = jnp.zeros_like(token)

    bufs = [pltpu.with_memory_space_constraint(a, pltpu.HBM) for a in list(ins) + list(lands)]
    extra = [] if after is None else [after]
    outs = pl.pallas_call(
        body, name=name,
        out_shape=(pltpu.SemaphoreType.DMA((n_sem,)), pltpu.SemaphoreType.DMA((n_sem,)))
        + tuple(pltpu.HBM(a.shape, a.dtype) for a in bufs) + (jax.ShapeDtypeStruct((8, BLK), F32),),
        in_specs=[_HBM_SPEC] * n_buf + [pl.BlockSpec(memory_space=pl.ANY)] * len(extra),
        out_specs=(_SEM_SPEC, _SEM_SPEC) + (_HBM_SPEC,) * n_buf + (pl.BlockSpec(memory_space=pltpu.VMEM),),
        input_output_aliases={b: 2 + b for b in range(n_buf)},
        compiler_params=pltpu.CompilerParams(has_side_effects=_DATAFLOW),
        interpret=False)(*bufs, *extra)
    return outs[0], outs[1], list(outs[2:2 + n_in]), list(outs[2 + n_in:2 + n_buf]), outs[-1]


def _exchange_wait(name, started, after, sends):
    send_sems, recv_sems, ins, lands, _ = started
    n_in, n_buf = len(ins), len(ins) + len(lands)

    def body(*refs):
        in_refs, land_refs = refs[:n_in], refs[n_in:n_buf]
        send_sems, recv_sems = refs[n_buf], refs[n_buf + 1]
        x, y, c = lax.axis_index("x"), lax.axis_index("y"), lax.axis_index("c")
        me = 4 * x + 2 * y + c
        for t, (k, src_fn, _, rcv_fn) in enumerate(sends):
            cp = pltpu.make_async_remote_copy(
                src_ref=src_fn(in_refs, land_refs, me), dst_ref=rcv_fn(land_refs, me),
                send_sem=send_sems.at[t], recv_sem=recv_sems.at[t],
                device_id=_peer(x, y, c, k)[0], device_id_type=MESH)
            cp.wait_send()
            cp.wait_recv()

    bufs = list(ins) + list(lands)
    outs = pl.pallas_call(
        body, name=name, out_shape=tuple(pltpu.HBM(a.shape, a.dtype) for a in bufs),
        in_specs=[_HBM_SPEC] * n_buf + [_SEM_SPEC, _SEM_SPEC, pl.BlockSpec(memory_space=pl.ANY)],
        out_specs=(_HBM_SPEC,) * n_buf,
        input_output_aliases={b: b for b in range(n_buf)},
        compiler_params=pltpu.CompilerParams(has_side_effects=_DATAFLOW),
        interpret=False)(*bufs, send_sems, recv_sems, after)
    return list(outs[:n_in]), list(outs[n_in:])


def _place_own(shape, dtype, own, start):
    return lax.dynamic_update_slice(lax.empty(shape, dtype), own, start)


def _place_own_window(name, shape, own, place):
    def body(zone_in, own_ref, zone_ref, sem):
        del zone_in
        cp = pltpu.make_async_copy(own_ref, place(zone_ref, _my_index()), sem)
        cp.start()
        cp.wait()

    anyspec = pl.BlockSpec(memory_space=pl.ANY)
    return _pcall(body, name=name, out_shape=jax.ShapeDtypeStruct(shape, own.dtype),
                  in_specs=[anyspec, anyspec], out_specs=anyspec,
                  scratch_shapes=[pltpu.SemaphoreType.DMA(())], aliases={0: 0})(lax.empty(shape, own.dtype), own)


def _mod_fwd(c_all, w_mod, b_mod_mine):
    n_layers, _, cm = w_mod.shape

    def body(c_ref, w_ref, b_ref, o_ref):
        for l in range(n_layers):
            o_ref[l] = jnp.dot(c_ref[...], w_ref[l], preferred_element_type=F32,
                               precision=lax.Precision.HIGHEST) + b_ref[l]

    return _pcall(body, name="mod_fwd", out_shape=jax.ShapeDtypeStruct((n_layers, NDEV, cm), F32))(
        c_all, w_mod, b_mod_mine)


def _ln_proj(x, shift, scale, w_full, name):
    s_len, d = x.shape
    n = w_full.shape[1]
    tm = min(512, s_len)
    tn = 1024

    def body(x_ref, sh_ref, sc_ref, w_ref, proj_ref, ht_ref, h_scr):
        @pl.when(pl.program_id(1) == 0)
        def _():
            xs, _ = _standardize(x_ref[...])
            h = xs * (1.0 + sc_ref[...]) + sh_ref[...]
            h_scr[...] = h.astype(BF16)
            ht_ref[...] = h.T.astype(BF16)

        proj_ref[...] = _dot(h_scr[...], w_ref[...])

    return _pcall(
        body, name=name,
        out_shape=(jax.ShapeDtypeStruct((s_len, n), F32), jax.ShapeDtypeStruct((d, s_len), BF16)),
        grid=(s_len // tm, n // tn),
        in_specs=[pl.BlockSpec((tm, d), lambda i, j: (i, 0)),
                  pl.BlockSpec((1, d), lambda i, j: (0, 0)),
                  pl.BlockSpec((1, d), lambda i, j: (0, 0)),
                  pl.BlockSpec((d, tn), lambda i, j: (0, j))],
        out_specs=(pl.BlockSpec((tm, tn), lambda i, j: (i, j)),
                   pl.BlockSpec((d, tm), lambda i, j: (0, i))),
        scratch_shapes=[pltpu.VMEM((tm, d), BF16)],
        semantics=("arbitrary", "arbitrary"))(x, shift, scale, w_full)


def _sb_group_blocks(nb):
    return min(4, nb)


def _sb_fwd(proj, name):
    s_len = proj.shape[0]
    nb = s_len // BLK
    n_pairs = WIDTH // BLK
    gb = _sb_group_blocks(nb)
    kw = gb * BLK

    def body(q_ref, k_ref, v_ref, o_ref, tot_ref):
        lane = _iota2((1, BLK), 1)
        row = _iota2((BLK, BLK), 0)
        col = _iota2((BLK, BLK), 1)
        half = jnp.concatenate([(row >= col).astype(BF16), jnp.ones((BLK, BLK), BF16)], axis=1)
        suffix_and_sum = jnp.concatenate([half, half], axis=0)
        qpos = _iota2((BLK, kw), 0)
        kpos = _iota2((BLK, kw), 1)
        head_lanes = [(lane // SB_HEAD_DIM) == hh for hh in range(2)]

        def scores(i, gi, qms, masked):
            c0 = pl.multiple_of(gi * kw, kw)
            kb = k_ref[pl.ds(c0, kw), :].astype(BF16)
            z2s = [_dot_nt(qms[hh], kb) for hh in range(2)]
            if masked:
                valid = (c0 + kpos) < (i * BLK + qpos)
                z2s = [jnp.where(valid, z2, MASKED_SCORE) for z2 in z2s]
            return tuple(z2s)

        def accumulate(gi, z2s, carry):
            c0 = pl.multiple_of(gi * kw, kw)
            vf = v_ref[pl.ds(c0, kw), :]
            sp2s = [_softplus2_parts(z2)[0] for z2 in z2s]
            terms = [[_split2_lanes(sp2[:, b * BLK:(b + 1) * BLK]) for b in range(gb)] for sp2 in sp2s]
            sums = [[_dot(t, suffix_and_sum) for t in head_terms] for head_terms in terms]
            weights, laters = [], []
            for hh in range(2):
                later = carry[2 * hh + 1]
                parts = [None] * gb
                for b in reversed(range(gb)):
                    parts[b] = sums[hh][b][:, :BLK] + later
                    later = later + sums[hh][b][:, BLK:]
                weights.append(jnp.exp2(z2s[hh] - jnp.concatenate(parts, axis=1)).astype(BF16))
                laters.append(later)
            outs = [_dot(weights[hh], jnp.where(head_lanes[hh], vf, 0.0).astype(BF16)) for hh in range(2)]
            return (carry[0] + outs[0], laters[0], carry[2] + outs[1], laters[1])

        def queries(i):
            qf = q_ref[pl.ds(pl.multiple_of(i * BLK, BLK), BLK), :] * (SB_HEAD_DIM ** -0.5 * LOG2E)
            return [jnp.where(head_lanes[hh], qf, 0.0).astype(BF16) for hh in range(2)]

        def qblock(i, first_scores):
            r0 = pl.multiple_of(i * BLK, BLK)
            qms = queries(i)
            zero = jnp.zeros((BLK, BLK), F32)
            last = i // gb

            def step(jj, state):
                gi = last - 1 - jj
                return scores(i, gi, qms, False) + accumulate(gi + 1, state[:2], state[2:])

            state = lax.fori_loop(0, last, step, first_scores + (zero,) * 4)
            nxt = jnp.minimum(i + 1, nb - 1)
            next_scores = scores(nxt, nxt // gb, queries(nxt), True)
            carry = accumulate(0, state[:2], state[2:])
            o_ref[pl.ds(r0, BLK), :] = carry[0] + carry[2]
            tot_ref[0, pl.ds(r0, BLK), :] = carry[1]
            tot_ref[1, pl.ds(r0, BLK), :] = carry[3]
            return next_scores

        lax.fori_loop(0, nb, qblock, scores(0, 0, queries(0), True))

    col_spec = lambda off: pl.BlockSpec((s_len, BLK), lambda p: (0, off + p))
    return _pcall(
        body, name=name,
        out_shape=(jax.ShapeDtypeStruct((s_len, WIDTH), F32),
                   jax.ShapeDtypeStruct((2 * n_pairs, s_len, BLK), F32)),
        grid=(n_pairs,),
        in_specs=[col_spec(0), col_spec(n_pairs), col_spec(2 * n_pairs)],
        out_specs=(pl.BlockSpec((s_len, BLK), lambda p: (0, p)),
                   pl.BlockSpec((2, s_len, BLK), lambda p: (p, 0, 0))),
        semantics=("arbitrary",))(proj, proj, proj)


def _hg_masks(mask_ref):
    row = _iota2((BLK, BLK), 0)
    col = _iota2((BLK, BLK), 1)
    for v, m in enumerate(HG_LEVELS):
        same = (row // (2 * m)) == (col // (2 * m))
        mask_ref[v] = (same & ((row & m) != 0) & ((col & m) == 0)).astype(F32)


def _hg_mid(b, m):
    if m >= 4:
        n = BLK // (2 * m)
        mid = b.reshape(n, 2 * m, BLK)[:, m - 1:m, :]
        return jnp.broadcast_to(mid, (n, 2 * m, BLK)).reshape(BLK, BLK)
    pos = _iota2((BLK, BLK), 0) & (2 * m - 1)
    out = b
    for p in range(2 * m):
        delta = (m - 1) - p
        if delta != 0:
            out = jnp.where(pos == p, pltpu.roll(b, (-delta) % BLK, 0), out)
    return out


def _hg_chunk_inputs(qraw, fpre, lb):
    sig = _sigmoid(fpre)
    f = lb + (1.0 - lb) * sig
    g = jnp.log(f)
    q, dq_fac = _silu_and_grad(qraw)
    return q, dq_fac, f, sig, g


def _hg_level_terms(q, k, b, v_idx, m, mask_ref):
    mid = _hg_mid(b, m)
    eq = jnp.exp(jnp.minimum(b - mid, 0.0))
    ek = jnp.exp(jnp.minimum(mid - b, 0.0))
    qt = (q * eq).astype(BF16)
    kt = (k * ek).astype(BF16)
    return qt, kt, eq, ek, mask_ref[v_idx]


def _hg_scores(q, k, b, mask_ref):
    sc = None
    for v_idx, m in enumerate(HG_LEVELS):
        qt, kt, _, _, msk = _hg_level_terms(q, k, b, v_idx, m, mask_ref)
        term = _dot_nt(qt, kt) * msk
        sc = term if sc is None else sc + term
    return sc


def _hgrn_fwd(proj, lb, name):
    s_len = proj.shape[0]
    nc = s_len // BLK
    nh = WIDTH // HG_HEAD_DIM
    base = 4 * WIDTH // BLK

    def body(q_ref, f_ref, i_ref, lb_ref, o_ref, mask_ref):
        _hg_masks(mask_ref)
        row = _iota2((BLK, BLK), 0)
        col = _iota2((BLK, BLK), 1)
        lower_incl = (col <= row).astype(BF16)
        lb_v = lb_ref[...]

        def chunk(ci, st):
            r0 = pl.multiple_of(ci * BLK, BLK)
            q, _, f, _, g = _hg_chunk_inputs(q_ref[pl.ds(r0, BLK), :], f_ref[pl.ds(r0, BLK), :], lb_v)
            k = 1.0 - f
            v = i_ref[pl.ds(r0, BLK), :]
            vb = v.astype(BF16)
            b = _dot_exact_l(lower_incl, g)
            b_end = b[BLK - 1:BLK, :]
            inter = _dot_nt((q * jnp.exp(b)).astype(BF16), st.astype(BF16))
            sc = _hg_scores(q, k, b, mask_ref)
            diag = jnp.sum(q * k, axis=-1, keepdims=True)
            o_ref[pl.ds(r0, BLK), :] = inter + _dot(sc.astype(BF16), vb) + diag * v
            k_dec = (k * jnp.exp(b_end - b)).astype(BF16)
            return st * jnp.exp(b_end) + _dot_tn(vb, k_dec)

        lax.fori_loop(0, nc, chunk, jnp.zeros((HG_HEAD_DIM, HG_HEAD_DIM), F32))

    col_spec = lambda off: pl.BlockSpec((s_len, BLK), lambda h: (0, off + h))
    return _pcall(
        body, name=name, out_shape=jax.ShapeDtypeStruct((s_len, WIDTH), F32),
        grid=(nh,),
        in_specs=[col_spec(base), col_spec(base + nh), col_spec(base + 2 * nh),
                  pl.BlockSpec((1, BLK), lambda h: (0, h))],
        out_specs=pl.BlockSpec((s_len, BLK), lambda h: (0, h)),
        scratch_shapes=[pltpu.VMEM((len(HG_LEVELS), BLK, BLK), F32)],
        semantics=("arbitrary",))(proj, proj, proj, lb)


def _rms_heads(o_b, norm_w):
    n_parts, h_parts, r_parts = [], [], []
    for h in range(WIDTH // HG_HEAD_DIM):
        sl = slice(h * HG_HEAD_DIM, (h + 1) * HG_HEAD_DIM)
        o = o_b[:, sl]
        rstd = lax.rsqrt(jnp.mean(o * o, axis=-1, keepdims=True) + RMS_EPS)
        ohat = o * rstd
        h_parts.append(ohat)
        n_parts.append(ohat * norm_w[:, sl])
        r_parts.append(jnp.broadcast_to(rstd, o.shape))
    cat = lambda parts: jnp.concatenate(parts, axis=-1)
    return cat(n_parts), cat(h_parts), cat(r_parts)


def _shift_rows_down(halo, cur, k):
    tm = cur.shape[0]
    ext = jnp.concatenate([halo, cur], axis=0)
    return pltpu.roll(ext, k, 0)[8:8 + tm]


def _shift_rows_up(cur, halo, k):
    tm = cur.shape[0]
    ext = jnp.concatenate([cur, halo], axis=0)
    return pltpu.roll(ext, (tm + 8 - k) % (tm + 8), 0)[0:tm]


def _merge_fwd(x, proj, o_a, o_b, gate, norm_w, conv_w, wb, w_out, ln_g, ln_b, name):
    s_len, d = x.shape
    tm = min(256, s_len)
    hb = tm // 8

    def body(x_ref, oa_ref, za_ref, ob_ref, zb_ref, pre_ref, post_ref, u_ref, zc_ref, hpre_ref, hu_ref, g_ref,
             gate_ref, nw_ref, cw_ref, wb_ref, wo_ref, lg_ref, lbias_ref, xn_ref, mg_ref, yc_ref):
        i = pl.program_id(0)
        sa, _ = _silu_and_grad(za_ref[...])
        y_a = (oa_ref[...] * sa).astype(BF16)
        n_b, _, _ = _rms_heads(ob_ref[...], nw_ref[...])
        sb, _ = _silu_and_grad(zb_ref[...])
        y_b = (n_b * sb).astype(BF16)
        a = pre_ref[...] * u_ref[...]
        halo = jnp.where(i > 0, hpre_ref[...] * hu_ref[...], 0.0)
        cw = cw_ref[...]
        conv = cw[0:1] * _shift_rows_down(halo, a, 2) + cw[1:2] * _shift_rows_down(halo, a, 1) + cw[2:3] * a
        sc, _ = _silu_and_grad(zc_ref[...])
        y_c = (post_ref[...] * conv * sc).astype(BF16)
        merged = None
        for k, yk in enumerate((y_a, y_b, y_c)):
            yc_ref[:, k * WIDTH:(k + 1) * WIDTH] = yk
            term = _sigmoid(g_ref[:, k * d:(k + 1) * d]) * _dot(yk, wb_ref[k])
            merged = term if merged is None else merged + term
        mb = merged.astype(BF16)
        mg_ref[...] = mb
        y = _dot(mb, wo_ref[...])
        r = ALPHA * x_ref[...] + (1.0 + gate_ref[...]) * y
        rhat, _ = _standardize(r)
        xn_ref[...] = rhat * lg_ref[...] + lbias_ref[...]

    wcol = lambda cb: pl.BlockSpec((tm, WIDTH), lambda i: (i, cb))
    halo_spec = lambda cb: pl.BlockSpec((8, WIDTH), lambda i: (jnp.maximum(i * hb - 1, 0), cb))
    vec = lambda w: pl.BlockSpec((1, w), lambda i: (0, 0))
    return _pcall(
        body, name=name,
        out_shape=(jax.ShapeDtypeStruct((s_len, d), F32), jax.ShapeDtypeStruct((s_len, d), BF16),
                   jax.ShapeDtypeStruct((s_len, 3 * WIDTH), BF16)),
        grid=(s_len // tm,),
        in_specs=[pl.BlockSpec((tm, d), lambda i: (i, 0)),
                  wcol(0), wcol(3), wcol(0), wcol(7), wcol(8), wcol(9), wcol(10), wcol(11),
                  halo_spec(8), halo_spec(10),
                  pl.BlockSpec((tm, 3 * d), lambda i: (i, 2)),
                  vec(d), vec(WIDTH),
                  pl.BlockSpec((3, WIDTH), lambda i: (0, 0)),
                  pl.BlockSpec((3, WIDTH, d), lambda i: (0, 0, 0)),
                  pl.BlockSpec((d, d), lambda i: (0, 0)),
                  vec(d), vec(d)],
        out_specs=(pl.BlockSpec((tm, d), lambda i: (i, 0)), pl.BlockSpec((tm, d), lambda i: (i, 0)),
                   pl.BlockSpec((tm, 3 * WIDTH), lambda i: (i, 0))),
        semantics=("arbitrary",))(x, o_a, proj, o_b, proj, proj, proj, proj, proj, proj, proj, proj,
                                  gate, norm_w, conv_w, wb, w_out, ln_g, ln_b)


def _loss_fwd_bwd(y, target):
    s_len, d = y.shape
    tm = min(512, s_len)

    def body(y_ref, t_ref, loss_ref, dy_ref):
        @pl.when(pl.program_id(0) == 0)
        def _():
            loss_ref[...] = jnp.zeros_like(loss_ref)

        e = y_ref[...] - t_ref[...]
        dy_ref[...] = e * (1.0 / d)
        part = jnp.sum(jnp.sum(e * e, axis=-1, keepdims=True), axis=0, keepdims=True)
        loss_ref[...] += part * (0.5 / d)

    tile = pl.BlockSpec((tm, d), lambda i: (i, 0))
    return _pcall(body, name="loss", grid=(s_len // tm,),
                  out_shape=(jax.ShapeDtypeStruct((1, 1), F32), jax.ShapeDtypeStruct((s_len, d), F32)),
                  in_specs=[tile, tile],
                  out_specs=(pl.BlockSpec((1, 1), lambda i: (0, 0)), tile),
                  semantics=("arbitrary",))(y, target)


def _merge_bwd(dxn, x, merged, ycat, proj, gate, wb, w_out, ln_g, name):
    s_len, d = x.shape
    tm = min(256, s_len)

    def body(dxn_ref, x_ref, mg_ref, yc_ref, g_ref, gate_ref, wb_ref, wo_ref, lg_ref,
             dres_ref, dyc_ref, dg_ref, gwo_ref, gwb_ref, vec_ref):
        @pl.when(pl.program_id(0) == 0)
        def _():
            gwo_ref[...] = jnp.zeros_like(gwo_ref)
            gwb_ref[...] = jnp.zeros_like(gwb_ref)
            vec_ref[...] = jnp.zeros_like(vec_ref)

        mb = mg_ref[...]
        one_gate = 1.0 + gate_ref[...]
        y = _dot(mb, wo_ref[...])
        r = ALPHA * x_ref[...] + one_gate * y
        rhat, rstd = _standardize(r)
        dxn = dxn_ref[...]
        dr = _standardize_bwd(rhat, rstd, dxn * lg_ref[...])
        vec_ref[0:1, :] += jnp.sum(dxn * rhat, axis=0, keepdims=True)
        vec_ref[1:2, :] += jnp.sum(dxn, axis=0, keepdims=True)
        vec_ref[2:3, :] += jnp.sum(dr * y, axis=0, keepdims=True)
        dres_ref[...] = ALPHA * dr
        dy = (one_gate * dr).astype(BF16)
        gwo_ref[...] += _dot_tn(mb, dy)
        dmerged = _dot_nt(dy, wo_ref[...])
        for k in range(3):
            yk = yc_ref[:, k * WIDTH:(k + 1) * WIDTH]
            sg = _sigmoid(g_ref[:, k * d:(k + 1) * d])
            pk = _dot(yk, wb_ref[k])
            dg_ref[:, k * d:(k + 1) * d] = (dmerged * pk * sg * (1.0 - sg)).astype(BF16)
            dpk = (dmerged * sg).astype(BF16)
            dyc_ref[:, k * WIDTH:(k + 1) * WIDTH] = _dot_nt(dpk, wb_ref[k])
            gwb_ref[k] += _dot_tn(yk, dpk)

    tile = lambda w: pl.BlockSpec((tm, w), lambda i: (i, 0))
    vec = pl.BlockSpec((1, d), lambda i: (0, 0))
    return _pcall(
        body, name=name,
        out_shape=(jax.ShapeDtypeStruct((s_len, d), F32), jax.ShapeDtypeStruct((s_len, 3 * WIDTH), F32),
                   jax.ShapeDtypeStruct(proj.shape, BF16), jax.ShapeDtypeStruct((d, d), F32),
                   jax.ShapeDtypeStruct((3, WIDTH, d), F32), jax.ShapeDtypeStruct((8, d), F32)),
        grid=(s_len // tm,),
        in_specs=[tile(d), tile(d), tile(d), tile(3 * WIDTH),
                  pl.BlockSpec((tm, 3 * d), lambda i: (i, 2)),
                  vec, pl.BlockSpec((3, WIDTH, d), lambda i: (0, 0, 0)),
                  pl.BlockSpec((d, d), lambda i: (0, 0)), vec],
        out_specs=(tile(d), tile(3 * WIDTH), pl.BlockSpec((tm, 3 * d), lambda i: (i, 2)),
                   pl.BlockSpec((d, d), lambda i: (0, 0)),
                   pl.BlockSpec((3, WIDTH, d), lambda i: (0, 0, 0)),
                   pl.BlockSpec((8, d), lambda i: (0, 0))),
        semantics=("arbitrary",))(dxn, x, merged, ycat, proj, gate, wb, w_out, ln_g)


def _branch_bwd(dycat, proj, o_a, o_b, norm_w, conv_w, dproj, name):
    s_len = proj.shape[0]
    tm = min(256, s_len)
    hb = tm // 8
    n_tiles = s_len // tm

    def body(dya_ref, dyb_ref, dyc_ref, oa_ref, za_ref, ob_ref, zb_ref, pre_ref, post_ref, u_ref, zc_ref,
             hpre_ref, hu_ref, ndyc_ref, npost_ref, nzc_ref, nw_ref, cw_ref, dproj_in,
             dproj_ref, doa_ref, dob_ref, vec_ref, dza_scr, dzb_scr, dc_scr, sems):
        del dproj_in
        i = pl.program_id(0)

        @pl.when(i == 0)
        def _():
            vec_ref[...] = jnp.zeros_like(vec_ref)

        sa, dsa = _silu_and_grad(za_ref[...])
        dya = dya_ref[...]
        doa_ref[...] = dya * sa
        dza_scr[...] = (dya * oa_ref[...] * dsa).astype(BF16)
        nw = nw_ref[...]
        n_b, ohat, rstd = _rms_heads(ob_ref[...], nw)
        sb, dsb = _silu_and_grad(zb_ref[...])
        dyb = dyb_ref[...]
        dzb_scr[...] = (dyb * n_b * dsb).astype(BF16)
        dn = dyb * sb
        vec_ref[0:1, :] += jnp.sum(dn * ohat, axis=0, keepdims=True)
        dnw = dn * nw
        parts = []
        for h in range(WIDTH // HG_HEAD_DIM):
            sl = slice(h * HG_HEAD_DIM, (h + 1) * HG_HEAD_DIM)
            m2 = jnp.mean(dnw[:, sl] * ohat[:, sl], axis=-1, keepdims=True)
            parts.append(rstd[:, sl] * (dnw[:, sl] - ohat[:, sl] * m2))
        dob_ref[...] = jnp.concatenate(parts, axis=-1)
        cw = cw_ref[...]
        pre, u, post = pre_ref[...], u_ref[...], post_ref[...]
        a = pre * u
        halo = jnp.where(i > 0, hpre_ref[...] * hu_ref[...], 0.0)
        a1 = _shift_rows_down(halo, a, 1)
        a2 = _shift_rows_down(halo, a, 2)
        conv = cw[0:1] * a2 + cw[1:2] * a1 + cw[2:3] * a
        sc, dsc = _silu_and_grad(zc_ref[...])
        dyc = dyc_ref[...]
        dconv = dyc * post * sc
        nsc, _ = _silu_and_grad(nzc_ref[...])
        nxt = jnp.where(i < n_tiles - 1, ndyc_ref[...] * npost_ref[...] * nsc, 0.0)
        da = cw[2:3] * dconv + cw[1:2] * _shift_rows_up(dconv, nxt, 1) + cw[0:1] * _shift_rows_up(dconv, nxt, 2)
        dc_scr[:, 0 * WIDTH:1 * WIDTH] = (da * u).astype(BF16)
        dc_scr[:, 1 * WIDTH:2 * WIDTH] = (dyc * conv * sc).astype(BF16)
        dc_scr[:, 2 * WIDTH:3 * WIDTH] = (da * pre).astype(BF16)
        dc_scr[:, 3 * WIDTH:4 * WIDTH] = (dyc * post * conv * dsc).astype(BF16)
        vec_ref[1:2, :] += jnp.sum(dconv * a2, axis=0, keepdims=True)
        vec_ref[2:3, :] += jnp.sum(dconv * a1, axis=0, keepdims=True)
        vec_ref[3:4, :] += jnp.sum(dconv * a, axis=0, keepdims=True)
        rows = pl.ds(pl.multiple_of(i * tm, tm), tm)
        copies = [pltpu.make_async_copy(dza_scr, dproj_ref.at[rows, 3 * WIDTH:4 * WIDTH], sems.at[0]),
                  pltpu.make_async_copy(dzb_scr, dproj_ref.at[rows, 7 * WIDTH:8 * WIDTH], sems.at[1]),
                  pltpu.make_async_copy(dc_scr, dproj_ref.at[rows, 8 * WIDTH:12 * WIDTH], sems.at[2])]
        for cp in copies:
            cp.start()
        for cp in copies:
            cp.wait()

    wcol = lambda cb: pl.BlockSpec((tm, WIDTH), lambda i: (i, cb))
    prev = lambda cb: pl.BlockSpec((8, WIDTH), lambda i: (jnp.maximum(i * hb - 1, 0), cb))
    nxt = lambda cb: pl.BlockSpec((8, WIDTH), lambda i: (jnp.minimum((i + 1) * hb, s_len // 8 - 1), cb))
    anyspec = pl.BlockSpec(memory_space=pl.ANY)
    out = jax.ShapeDtypeStruct((s_len, WIDTH), F32)
    return _pcall(
        body, name=name,
        out_shape=(jax.ShapeDtypeStruct(dproj.shape, dproj.dtype), out, out, jax.ShapeDtypeStruct((8, WIDTH), F32)),
        grid=(n_tiles,),
        in_specs=[wcol(0), wcol(1), wcol(2), wcol(0), wcol(3), wcol(0), wcol(7), wcol(8), wcol(9), wcol(10), wcol(11),
                  prev(8), prev(10), nxt(2), nxt(9), nxt(11),
                  pl.BlockSpec((1, WIDTH), lambda i: (0, 0)), pl.BlockSpec((3, WIDTH), lambda i: (0, 0)), anyspec],
        out_specs=(anyspec, wcol(0), wcol(0), pl.BlockSpec((8, WIDTH), lambda i: (0, 0))),
        scratch_shapes=[pltpu.VMEM((tm, WIDTH), BF16), pltpu.VMEM((tm, WIDTH), BF16),
                        pltpu.VMEM((tm, 4 * WIDTH), BF16), pltpu.SemaphoreType.DMA((3,))],
        aliases={18: 0},
        semantics=("arbitrary",))(dycat, dycat, dycat, o_a, proj, o_b, proj, proj, proj, proj, proj,
                                  proj, proj, dycat, proj, proj, norm_w, conv_w, dproj)


def _sb_bwd(proj, do_a, totals, dproj, name):
    s_len = proj.shape[0]
    nb = s_len // BLK
    n_pairs = WIDTH // BLK
    scale = SB_HEAD_DIM ** -0.5
    gb = _sb_group_blocks(nb)
    kw = gb * BLK

    def body(q_ref, k_ref, v_ref, do_ref, tot_ref, dproj_in, dproj_ref, dq_ref, dk_ref, dv_ref, out_scr, sems):
        del dproj_in
        lane = _iota2((1, BLK), 1)
        row = _iota2((BLK, BLK), 0)
        col = _iota2((BLK, BLK), 1)
        ones = jnp.ones((BLK, BLK), BF16)
        twice = lambda m: jnp.concatenate([m, m], axis=0)
        before_and_sum = twice(jnp.concatenate([(row < col).astype(BF16), ones], axis=1))
        upto_and_sum = twice(jnp.concatenate([(row <= col).astype(BF16), ones], axis=1))
        qpos = _iota2((BLK, kw), 0)
        kpos = _iota2((BLK, kw), 1)
        head_lanes = [(lane // SB_HEAD_DIM) == hh for hh in range(2)]
        dk_ref[...] = jnp.zeros_like(dk_ref)
        dv_ref[...] = jnp.zeros_like(dv_ref)

        causal = kpos - qpos

        def scores(i, gi, qms):
            c0 = pl.multiple_of(gi * kw, kw)
            kb = k_ref[pl.ds(c0, kw), :].astype(BF16)
            valid = causal < i * BLK - c0
            return tuple(jnp.where(valid, _dot_nt(qms[hh], kb), MASKED_SCORE) for hh in range(2))

        def process(gi, z2s, qms, doms, totals_i, carry):
            c0 = pl.multiple_of(gi * kw, kw)
            kf = k_ref[pl.ds(c0, kw), :]
            vf = v_ref[pl.ds(c0, kw), :]
            kms = [jnp.where(head_lanes[hh], kf, 0.0).astype(BF16) for hh in range(2)]
            vms = [jnp.where(head_lanes[hh], vf, 0.0).astype(BF16) for hh in range(2)]
            das = [_dot_nt(doms[hh], vms[hh]) for hh in range(2)]
            halves = [_softplus2_parts(z2) for z2 in z2s]
            terms = [[_split2_lanes(sp2[:, b * BLK:(b + 1) * BLK]) for b in range(gb)] for sp2, _ in halves]
            sums = [[_dot(t, before_and_sum) for t in head_terms] for head_terms in terms]
            weights, gmats, l_befores = [], [], []
            for hh in range(2):
                l_before = carry[3 * hh + 1]
                parts = []
                for b in range(gb):
                    parts.append(totals_i[hh] - l_before - sums[hh][b][:, :BLK])
                    l_before = l_before + sums[hh][b][:, BLK:]
                a = jnp.exp2(z2s[hh] - jnp.concatenate(parts, axis=1))
                weights.append(a.astype(BF16))
                gmats.append(a * das[hh])
                l_befores.append(l_before)
            terms = [[_split2_lanes(g[:, b * BLK:(b + 1) * BLK]) for b in range(gb)] for g in gmats]
            sums = [[_dot(t, upto_and_sum) for t in head_terms] for head_terms in terms]
            dzs, g_befores = [], []
            for hh in range(2):
                g_before = carry[3 * hh + 2]
                parts = []
                for b in range(gb):
                    parts.append(g_before + sums[hh][b][:, :BLK])
                    g_before = g_before + sums[hh][b][:, BLK:]
                dzs.append((gmats[hh] - halves[hh][1] * jnp.concatenate(parts, axis=1)).astype(BF16))
                g_befores.append(g_before)
            dks = [_dot_tn(dzs[hh], qms[hh]) for hh in range(2)]
            dvs = [_dot_tn(weights[hh], doms[hh]) for hh in range(2)]
            dqs = [_dot(dzs[hh], kms[hh]) for hh in range(2)]
            dk_ref[pl.ds(c0, kw), :] += (dks[0] + dks[1]) * (1.0 / LOG2E)
            dv_ref[pl.ds(c0, kw), :] += dvs[0] + dvs[1]
            return (carry[0] + dqs[0], l_befores[0], g_befores[0], carry[3] + dqs[1], l_befores[1], g_befores[1])

        def queries(i):
            qf = q_ref[pl.ds(pl.multiple_of(i * BLK, BLK), BLK), :] * (scale * LOG2E)
            return [jnp.where(head_lanes[hh], qf, 0.0).astype(BF16) for hh in range(2)]

        def qblock(i, first_scores):
            r0 = pl.multiple_of(i * BLK, BLK)
            qms = queries(i)
            dof = do_ref[pl.ds(r0, BLK), :]
            doms = [jnp.where(head_lanes[hh], dof, 0.0).astype(BF16) for hh in range(2)]
            totals_i = [tot_ref[hh, pl.ds(r0, BLK), :] for hh in range(2)]
            zero = jnp.zeros((BLK, BLK), F32)
            last = i // gb

            def step(gi, state):
                return scores(i, gi + 1, qms) + process(gi, state[:2], qms, doms, totals_i, state[2:])

            state = lax.fori_loop(0, last, step, first_scores + (zero,) * 6)
            nxt = jnp.minimum(i + 1, nb - 1)
            next_scores = scores(nxt, 0, queries(nxt))
            carry = process(last, state[:2], qms, doms, totals_i, state[2:])
            dq_ref[pl.ds(r0, BLK), :] = (carry[0] + carry[3]) * scale
            return next_scores

        lax.fori_loop(0, nb, qblock, scores(0, 0, queries(0)))
        pair = pl.program_id(0)
        copies = []
        for t, ref in enumerate((dq_ref, dk_ref, dv_ref)):
            out_scr[t] = ref[...].astype(BF16)
            col = pl.multiple_of((t * n_pairs + pair) * BLK, BLK)
            copies.append(pltpu.make_async_copy(out_scr.at[t], dproj_ref.at[:, pl.ds(col, BLK)], sems.at[t]))
            copies[-1].start()
        for cp in copies:
            cp.wait()

    col_spec = lambda off: pl.BlockSpec((s_len, BLK), lambda p: (0, off + p))
    anyspec = pl.BlockSpec(memory_space=pl.ANY)
    return _pcall(
        body, name=name, out_shape=jax.ShapeDtypeStruct(dproj.shape, dproj.dtype), grid=(n_pairs,),
        in_specs=[col_spec(0), col_spec(n_pairs), col_spec(2 * n_pairs), col_spec(0),
                  pl.BlockSpec((2, s_len, BLK), lambda p: (p, 0, 0)), anyspec],
        out_specs=anyspec,
        scratch_shapes=[pltpu.VMEM((s_len, BLK), F32)] * 3 + [pltpu.VMEM((3, s_len, BLK), BF16),
                                                              pltpu.SemaphoreType.DMA((3,))],
        aliases={5: 0},
        semantics=("arbitrary",))(proj, proj, proj, do_a, totals, dproj)


def _hgrn_bwd(proj, do_b, lb, dproj, name):
    s_len = proj.shape[0]
    nc = s_len // BLK
    nh = WIDTH // HG_HEAD_DIM
    base = 4 * WIDTH // BLK

    def body(q_ref, f_ref, i_ref, do_ref, lb_ref, dproj_in, dproj_ref, dlb_ref, mask_ref, st_ref, out_scr, sems):
        del dproj_in
        _hg_masks(mask_ref)
        row = _iota2((BLK, BLK), 0)
        col = _iota2((BLK, BLK), 1)
        lower_incl = (col <= row).astype(BF16)
        upper_incl = (col >= row).astype(BF16)
        lb_v = lb_ref[...]

        def load(ci):
            r0 = pl.multiple_of(ci * BLK, BLK)
            q, dq_fac, f, sig, g = _hg_chunk_inputs(q_ref[pl.ds(r0, BLK), :], f_ref[pl.ds(r0, BLK), :], lb_v)
            b = _dot_exact_l(lower_incl, g)
            return r0, q, dq_fac, f, sig, b, i_ref[pl.ds(r0, BLK), :]

        def fwd_chunk(ci, st):
            st_ref[ci] = st
            _, _, _, f, _, b, v = load(ci)
            b_end = b[BLK - 1:BLK, :]
            k_dec = ((1.0 - f) * jnp.exp(b_end - b)).astype(BF16)
            return st * jnp.exp(b_end) + _dot_tn(v.astype(BF16), k_dec)

        lax.fori_loop(0, nc, fwd_chunk, jnp.zeros((HG_HEAD_DIM, HG_HEAD_DIM), F32))

        def bwd_chunk(cc, carry):
            dst, suffix, dlb = carry
            ci = nc - 1 - cc
            r0, q, dq_fac, f, sig, b, v = load(ci)
            k = 1.0 - f
            vb = v.astype(BF16)
            do = do_ref[pl.ds(r0, BLK), :]
            dob = do.astype(BF16)
            b_end = b[BLK - 1:BLK, :]
            e_q = jnp.exp(b)
            e_k = jnp.exp(b_end - b)
            qe = (q * e_q).astype(BF16)
            kh = (k * e_k).astype(BF16)
            st1, st2 = _split2(st_ref[ci])
            ds1, ds2 = _split2(dst)
            dqe = _dot(dob, st1) + _dot(dob, st2)
            dkh = _dot(vb, ds1) + _dot(vb, ds2)
            dq = e_q * dqe
            dk = e_k * dkh
            dv = _dot_nt(kh, ds1)
            dst_new = dst * jnp.exp(b_end) + _dot_tn(dob, qe)
            dlog = qe.astype(F32) * dqe - kh.astype(F32) * dkh
            da = _dot_nt(dob, vb)
            sc = None
            for v_idx, m in enumerate(HG_LEVELS):
                qm, km, eq, ek, msk = _hg_level_terms(q, k, b, v_idx, m, mask_ref)
                term = _dot_nt(qm, km) * msk
                sc = term if sc is None else sc + term
                pm = (da * msk).astype(BF16)
                dqm = _dot(pm, km)
                dkm = _dot_tn(pm, qm)
                dq = dq + dqm * eq
                dk = dk + dkm * ek
                dlog = dlog + (qm.astype(F32) * dqm - km.astype(F32) * dkm)
            a_diag = jnp.sum(do * v, axis=-1, keepdims=True)
            s_diag = jnp.sum(q * k, axis=-1, keepdims=True)
            dq = dq + a_diag * k
            dk = dk + a_diag * q
            dv = dv + _dot_tn(sc.astype(BF16), dob) + s_diag * do
            dg = _dot_exact_l(upper_incl, dlog) + suffix
            dfull = dg / f - dk
            out_scr[0, pl.ds(r0, BLK), :] = (dq * dq_fac).astype(BF16)
            out_scr[1, pl.ds(r0, BLK), :] = (dfull * (1.0 - lb_v) * sig * (1.0 - sig)).astype(BF16)
            out_scr[2, pl.ds(r0, BLK), :] = dv.astype(BF16)
            dlb = dlb + jnp.sum(dfull * (1.0 - sig), axis=0, keepdims=True)
            return dst_new, dg[0:1, :], dlb

        zero_row = jnp.zeros((1, BLK), F32)
        _, _, dlb = lax.fori_loop(0, nc, bwd_chunk,
                                  (jnp.zeros((HG_HEAD_DIM, HG_HEAD_DIM), F32), zero_row, zero_row))
        dlb_ref[...] = jnp.broadcast_to(dlb, dlb_ref.shape)
        head = pl.program_id(0)
        copies = []
        for t in range(3):
            col = pl.multiple_of((base + t * nh + head) * BLK, BLK)
            copies.append(pltpu.make_async_copy(out_scr.at[t], dproj_ref.at[:, pl.ds(col, BLK)], sems.at[t]))
            copies[-1].start()
        for cp in copies:
            cp.wait()

    col_spec = lambda off: pl.BlockSpec((s_len, BLK), lambda h: (0, off + h))
    anyspec = pl.BlockSpec(memory_space=pl.ANY)
    return _pcall(
        body, name=name,
        out_shape=(jax.ShapeDtypeStruct(dproj.shape, dproj.dtype), jax.ShapeDtypeStruct((8, WIDTH), F32)),
        grid=(nh,),
        in_specs=[col_spec(base), col_spec(base + nh), col_spec(base + 2 * nh), col_spec(0),
                  pl.BlockSpec((1, BLK), lambda h: (0, h)), anyspec],
        out_specs=(anyspec, pl.BlockSpec((8, BLK), lambda h: (0, h))),
        scratch_shapes=[pltpu.VMEM((len(HG_LEVELS), BLK, BLK), F32),
                        pltpu.VMEM((nc, HG_HEAD_DIM, HG_HEAD_DIM), F32),
                        pltpu.VMEM((3, s_len, BLK), BF16), pltpu.SemaphoreType.DMA((3,))],
        aliases={5: 0},
        semantics=("arbitrary",))(proj, proj, proj, do_b, lb, dproj)


def _dh_matmul(dproj, w_full, after, name):
    s_len, n = dproj.shape
    d = w_full.shape[0]
    tm = min(512, s_len)
    tk = 1536

    def body(dp_ref, w_ref, after_ref, dh_ref):
        del after_ref
        part = _dot_nt(dp_ref[...], w_ref[...])

        @pl.when(pl.program_id(1) == 0)
        def _():
            dh_ref[...] = part

        @pl.when(pl.program_id(1) > 0)
        def _():
            dh_ref[...] += part

    return _pcall(
        body, name=name, out_shape=jax.ShapeDtypeStruct((s_len, d), F32),
        grid=(s_len // tm, n // tk),
        in_specs=[pl.BlockSpec((tm, tk), lambda i, k: (i, k)), pl.BlockSpec((d, tk), lambda i, k: (0, k)),
                  pl.BlockSpec(memory_space=pl.ANY)],
        out_specs=pl.BlockSpec((tm, d), lambda i, k: (i, 0)),
        semantics=("arbitrary", "arbitrary"))(dproj, w_full, after)


def _gw_matmul(h_t, dproj, name):
    d, s_len = h_t.shape
    n = dproj.shape[1]
    tn = 1152

    def body(ht_ref, dp_ref, gw_ref):
        gw_ref[...] = _dot(ht_ref[...], dp_ref[...]).astype(BF16)

    return _pcall(
        body, name=name, out_shape=jax.ShapeDtypeStruct((d, n), BF16),
        grid=(n // tn,),
        in_specs=[pl.BlockSpec((d, s_len), lambda j: (0, 0)), pl.BlockSpec((s_len, tn), lambda j: (0, j))],
        out_specs=pl.BlockSpec((d, tn), lambda j: (0, j)),
        semantics=("arbitrary",))(h_t, dproj)


def _ln_bwd(dh, x, scale, dres, name):
    s_len, d = x.shape
    tm = min(512, s_len)

    def body(dh_ref, x_ref, sc_ref, dres_ref, dx_ref, vec_ref):
        @pl.when(pl.program_id(0) == 0)
        def _():
            vec_ref[...] = jnp.zeros_like(vec_ref)

        dh = dh_ref[...]
        xs, rstd = _standardize(x_ref[...])
        vec_ref[0:1, :] += jnp.sum(dh, axis=0, keepdims=True)
        vec_ref[1:2, :] += jnp.sum(dh * xs, axis=0, keepdims=True)
        dx_ref[...] = _standardize_bwd(xs, rstd, dh * (1.0 + sc_ref[...])) + dres_ref[...]

    tile = pl.BlockSpec((tm, d), lambda i: (i, 0))
    return _pcall(body, name=name, grid=(s_len // tm,),
                  out_shape=(jax.ShapeDtypeStruct((s_len, d), F32), jax.ShapeDtypeStruct((8, d), F32)),
                  in_specs=[tile, tile, pl.BlockSpec((1, d), lambda i: (0, 0)), tile],
                  out_specs=(tile, pl.BlockSpec((8, d), lambda i: (0, 0))),
                  semantics=("arbitrary",))(dh, x, scale, dres)


def _wmod_grad(c_t, dmod):
    d = c_t.shape[0]
    n_layers, _, cm = dmod.shape

    def body(c_ref, dm_ref, o_ref):
        for l in range(n_layers):
            acc = None
            for b in range(NDEV):
                term = c_ref[:, b:b + 1] * dm_ref[l, b:b + 1, :]
                acc = term if acc is None else acc + term
            o_ref[l] = acc

    return _pcall(body, name="wmod_grad", out_shape=jax.ShapeDtypeStruct((n_layers, d, cm), F32))(c_t, dmod)


def _sum_adamw(parts_list, w, m, v, name):
    n_ranges = len(parts_list)
    n_src, range_rows, cols = parts_list[0].shape
    rows = range_rows * n_ranges
    tr = range_rows
    for cand in (512, 256, 128, 64, 32, 16, 8):
        if range_rows % cand == 0 and cand * cols * 4 <= (2 << 20):
            tr = cand
            break
    tiles = range_rows // tr

    def body(*refs):
        p_refs = refs[:n_ranges]
        w_ref, m_ref, v_ref, g_ref, d_ref, nm_ref, nv_ref = refs[n_ranges:]

        def step(p_ref):
            g = p_ref[0].astype(F32)
            for s in range(1, n_src):
                g = g + p_ref[s].astype(F32)
            nm = ADAM_B1 * m_ref[...] + (1.0 - ADAM_B1) * g
            nv = ADAM_B2 * v_ref[...] + (1.0 - ADAM_B2) * (g * g)
            m_hat = nm / (1.0 - ADAM_B1 ** ADAM_STEP)
            v_hat = nv / (1.0 - ADAM_B2 ** ADAM_STEP)
            g_ref[...] = g
            d_ref[...] = -ADAM_LR * (m_hat / (jnp.sqrt(v_hat) + ADAM_EPS) + ADAM_WD * w_ref[...])
            nm_ref[...] = nm
            nv_ref[...] = nv

        if n_ranges == 1:
            step(p_refs[0])
        else:
            for j in range(n_ranges):
                @pl.when(pl.program_id(0) // tiles == j)
                def _(j=j):
                    step(p_refs[j])

    def part_spec(j):
        return pl.BlockSpec((n_src, tr, cols), lambda i: (0, jnp.clip(i - j * tiles, 0, tiles - 1), 0))

    tile = pl.BlockSpec((tr, cols), lambda i: (i, 0))
    out = jax.ShapeDtypeStruct((rows, cols), F32)
    return _pcall(body, name=name, grid=(rows // tr,), out_shape=(out,) * 4,
                  in_specs=[part_spec(j) for j in range(n_ranges)] + [tile, tile, tile],
                  out_specs=(tile,) * 4, semantics=("arbitrary",))(*parts_list, w, m, v)


def _sum_parts(parts, name):
    n_src = parts.shape[0]

    def body(p_ref, o_ref):
        acc = p_ref[0]
        for s in range(1, n_src):
            acc = acc + p_ref[s]
        o_ref[...] = acc

    return _pcall(body, name=name, out_shape=jax.ShapeDtypeStruct(parts.shape[1:], F32))(parts)


def _pair_sum(gw, stage, me, name):
    d = gw.shape[0]
    n_slots, _, shard = stage.shape

    def body(me_ref, g_ref, s_ref, own_ref, o_ref):
        del me_ref
        total = (g_ref[...].astype(F32) + s_ref[0].astype(F32)).astype(BF16)
        o_ref[0] = total

        @pl.when(pl.program_id(0) == 0)
        def _():
            own_ref[0] = total

    slot = pl.BlockSpec((1, d, shard), lambda jj, me_ref: (jj, 0, 0))
    out = jax.ShapeDtypeStruct(stage.shape, BF16)
    return pl.pallas_call(
        body, name=name, out_shape=(out, out),
        grid_spec=pltpu.PrefetchScalarGridSpec(
            num_scalar_prefetch=1, grid=(n_slots,),
            in_specs=[pl.BlockSpec((d, shard), lambda jj, me_ref: (0, me_ref[0] ^ (2 * jj))), slot],
            out_specs=(pl.BlockSpec((1, d, shard), lambda jj, me_ref: (0, 0, 0)), slot)),
        compiler_params=pltpu.CompilerParams(dimension_semantics=("arbitrary",), vmem_limit_bytes=VMEM_LIMIT),
        interpret=False)(me.reshape(1).astype(jnp.int32), gw, stage)


def _lower_bound_table(lower_bounds):
    p = jax.nn.softmax(lower_bounds.astype(F32), axis=0)
    return jnp.cumsum(p, axis=0) - p[0:1]


def _pad_rows(v, width):
    n = v.shape[0]
    rows = -(-n // width)
    rows = -(-rows // 8) * 8
    return jnp.pad(v, (0, rows * width - n)).reshape(rows, width)


def kernel(x, c, w_mod, b_mod, w_in, conv_w, hgrn_norm_w, lower_bounds, w_branch, w_out, ln_g, ln_b, loss_target, m_w_mod, m_b_mod, m_w_in, m_conv_w, m_hgrn_norm_w, m_lower_bounds, m_w_branch, m_w_out, m_ln_g, m_ln_b, v_w_mod, v_b_mod, v_w_in, v_conv_w, v_hgrn_norm_w, v_lower_bounds, v_w_branch, v_w_out, v_ln_g, v_ln_b):
    n_layers = N_LAYERS
    s_len, d = x.shape[1], x.shape[2]
    n_cols = w_in.shape[2] * NDEV
    cw_cols = conv_w.shape[2]
    cm = w_mod.shape[2]
    me = _my_index()
    x0 = x[0]
    target = loss_target[0]

    small = _pad_rows(jnp.concatenate([c.reshape(-1), conv_w.reshape(-1)]), BLK)
    small_all = _all_gather_small("gather_c_conv", small).reshape(NDEV, -1)
    c_all = small_all[:, :d]
    conv_full = small_all[:, d:d + n_layers * 3 * cw_cols].reshape(NDEV, n_layers, 3, cw_cols)
    conv_full = conv_full.transpose(1, 2, 0, 3).reshape(n_layers, 3, WIDTH)

    b_mod_mine = lax.dynamic_slice_in_dim(b_mod, me * cm, cm, axis=1).reshape(n_layers, 1, cm)
    mod_cols = _mod_fwd(c_all, w_mod, b_mod_mine)
    mod_all = _all_gather_small("gather_mod", mod_cols.reshape(n_layers * NDEV, cm))
    mod_all = mod_all.reshape(NDEV, n_layers, NDEV, cm)
    mod_mine = lax.dynamic_index_in_dim(mod_all, me, axis=2, keepdims=False)
    mod_mine = mod_mine.transpose(1, 0, 2).reshape(n_layers, 3, 1, d)

    shard = w_in.shape[2]
    dsh = d // NDEV
    w_in_b, w_branch_b, w_out_b = w_in.astype(BF16), w_branch.astype(BF16), w_out.astype(BF16)
    window = lambda ref, dev: ref.at[:, pl.ds(pl.multiple_of(dev * shard, BLK), shard)]

    def two_step_sends(places):
        chips, sibling = [], []
        for k in (1, 2, 4, 6):
            for a, place in enumerate(places):
                chips.append((k, lambda ins, lands, me, a=a: ins[a],
                              lambda lands, me, a=a, place=place: place(lands[a], me),
                              lambda lands, me, a=a, k=k, place=place: place(lands[a], me ^ k)))
        for j in (2, 4, 6):
            for a, place in enumerate(places):
                sibling.append((1, lambda ins, lands, me, a=a, j=j, place=place: place(lands[a], me ^ j),
                                lambda lands, me, a=a, j=j, place=place: place(lands[a], me ^ j),
                                lambda lands, me, a=a, j=j, place=place: place(lands[a], me ^ 1 ^ j)))
        return chips, sibling

    in_sends = two_step_sends([window])
    rest_sends = two_step_sends([_slot, _slot])
    layer_sends = two_step_sends([window, _slot, _slot])

    def in_land(l):
        return _place_own_window(f"place_w_in_{l}", (d, n_cols), w_in_b[l], window)

    def rest_lands(l):
        return [_place_own((NDEV, 3, WIDTH, dsh), BF16, w_branch_b[l][None], (me, 0, 0, 0)),
                _place_own((NDEV, dsh, d), BF16, w_out_b[l][None], (me, 0, 0))]

    def gather_start(name, shards, lands, sends, after):
        return _exchange_start(f"{name}_chips_start", shards, lands, sends[0], after)

    def gather_pass_on(name, started, after, sends):
        _, lands = _exchange_wait(f"{name}_chips_wait", started, after, sends[0])
        return _exchange_start(f"{name}_sibling_start", [], lands, sends[1])

    def gather_finish(name, started, after, sends):
        return _exchange_wait(f"{name}_sibling_wait", started, after, sends[1])[1]

    def branch_out_weights(w_branch_l, w_out_l):
        return w_branch_l.transpose(1, 2, 0, 3).reshape(3, WIDTH, d), w_out_l.reshape(d, d)

    gathering = gather_start("gather_w_in_0", [w_in_b[0]], [in_land(0)], in_sends, mod_mine)
    passing = gather_pass_on("gather_w_in_0", gathering, gathering[4], in_sends)
    rest_gathering = gather_start("gather_rest_0", [w_branch_b[0], w_out_b[0]], rest_lands(0), rest_sends, passing[4])
    next_gathering = None
    if n_layers > 1:
        next_gathering = gather_start("gather_weights_1", [w_in_b[1], w_branch_b[1], w_out_b[1]],
                                      [in_land(1)] + rest_lands(1), layer_sends, rest_gathering[4])
    w_in_l = gather_finish("gather_w_in_0", passing, (next_gathering or rest_gathering)[4], in_sends)[0]

    lbs = _lower_bound_table(lower_bounds)
    norm_w4 = jnp.tile(hgrn_norm_w, (1, WIDTH // HG_HEAD_DIM))

    saved = []
    xl = x0
    for l in range(n_layers):
        shift, scale, gate = mod_mine[l, 0], mod_mine[l, 1], mod_mine[l, 2]
        proj, h_t = _ln_proj(xl, shift, scale, w_in_l, f"ln_proj_{l}")
        o_a, totals = _sb_fwd(proj, f"sb_fwd_{l}")
        if l == 0:
            rest_passing = gather_pass_on("gather_rest_0", rest_gathering, o_a, rest_sends)
        o_b = _hgrn_fwd(proj, lbs[l:l + 1], f"hgrn_fwd_{l}")
        if l == 0:
            wb_l, wo_l = branch_out_weights(*gather_finish("gather_rest_0", rest_passing, o_b, rest_sends))
            if n_layers > 1:
                next_passing = gather_pass_on("gather_weights_1", next_gathering, o_b, layer_sends)
                gate = gate + next_passing[4][0, 0]
        x_new, merged, ycat = _merge_fwd(xl, proj, o_a, o_b, gate, norm_w4[l:l + 1], conv_full[l],
                                         wb_l, wo_l, ln_g[l:l + 1], ln_b[l:l + 1], f"merge_fwd_{l}")
        saved.append((xl, proj, h_t, o_a, totals, o_b, merged, ycat, w_in_l, wb_l, wo_l))
        if l == 0 and n_layers > 1:
            w_in_l, w_branch_l, w_out_l = gather_finish("gather_weights_1", next_passing, x_new, layer_sends)
            wb_l, wo_l = branch_out_weights(w_branch_l, w_out_l)
        xl = x_new

    loss_part, dx = _loss_fwd_bwd(xl, target)
    loss = lax.psum(loss_part[0, 0], ("x", "y", "c"))

    pair_sends = [(1, lambda ins, lands, me, j=j: window(ins[0], me ^ 1 ^ j),
                   lambda lands, me, jj=jj: lands[0].at[jj], lambda lands, me, jj=jj: lands[0].at[jj])
                  for jj, j in enumerate((0, 2, 4, 6))]
    chip_sum_sends = [(j, lambda ins, lands, me, jj=jj: ins[0].at[jj],
                       lambda lands, me, jj=jj: lands[0].at[jj], lambda lands, me, jj=jj: lands[0].at[jj])
                      for jj, j in ((1, 2), (2, 4), (3, 6))]
    rest_scatter = _direct_sends([(0, 0, _slot, _slot), (1, 1, _slot, _slot)])
    scattering = [None] * n_layers
    small_grads = [None] * n_layers
    dmod = [None] * n_layers
    tie = None
    for l in reversed(range(n_layers)):
        xl, proj, h_t, o_a, totals, o_b, merged, ycat, w_in_l, wb_l, wo_l = saved[l]
        scale, gate = mod_mine[l, 1], mod_mine[l, 2]
        if tie is not None:
            gate = gate + tie[0, 0]
        dres, dycat, dproj, gwo, gwb, mvec = _merge_bwd(dx, xl, merged, ycat, proj, gate, wb_l, wo_l,
                                                        ln_g[l:l + 1], f"merge_bwd_{l}")
        gwb_by_owner = gwb.astype(BF16).reshape(3, WIDTH, NDEV, dsh).transpose(2, 0, 1, 3)
        gwo_by_owner = gwo.astype(BF16).reshape(NDEV, dsh, d)
        lands = [_place_own((NDEV, 3, WIDTH, dsh), BF16, lax.dynamic_slice_in_dim(gwb_by_owner, me, 1, axis=0),
                            (me, 0, 0, 0)),
                 _place_own((NDEV, dsh, d), BF16, lax.dynamic_slice_in_dim(gwo_by_owner, me, 1, axis=0),
                            (me, 0, 0))]
        rest_started = _exchange_start(f"scatter_rest_{l}_start", [gwb_by_owner, gwo_by_owner], lands, rest_scatter)
        dproj, do_a, do_b, bvec = _branch_bwd(dycat, proj, o_a, o_b, norm_w4[l:l + 1] + rest_started[4][0, 0],
                                              conv_full[l], dproj, f"branch_bwd_{l}")
        dproj = _sb_bwd(proj, do_a, totals, dproj, f"sb_bwd_{l}")
        dproj, dlb = _hgrn_bwd(proj, do_b, lbs[l:l + 1], dproj, f"hgrn_bwd_{l}")
        gwi = _gw_matmul(h_t, dproj, f"gw_matmul_{l}")
        swapping = _exchange_start(f"scatter_in_{l}_sibling_start", [gwi], [lax.empty((4, d, shard), BF16)], pair_sends)
        (gwi,), (stage,) = _exchange_wait(f"scatter_in_{l}_sibling_wait", swapping, swapping[4], pair_sends)
        land, chip_sums = _pair_sum(gwi, stage, me, f"pair_sum_{l}")
        in_started = _exchange_start(f"scatter_in_{l}_chips_start", [chip_sums], [land], chip_sum_sends)
        scattering[l] = (in_started, rest_started)
        tie = in_started[4]
        dh = _dh_matmul(dproj, w_in_l, tie, f"dh_matmul_{l}")
        dx, lvec = _ln_bwd(dh, xl, scale + tie[0, 0], dres, f"ln_bwd_{l}")
        dmod[l] = jnp.concatenate([lvec[0], lvec[1], mvec[2]])
        norm_grad = bvec[0].reshape(WIDTH // HG_HEAD_DIM, HG_HEAD_DIM).sum(axis=0)
        small_grads[l] = jnp.concatenate([mvec[0], mvec[1], norm_grad, dlb[0], bvec[1:4].reshape(-1)])
    grad_x = dx[None]

    small_vec = jnp.concatenate(dmod + small_grads)
    n_small = small_vec.shape[0]
    small_all = _all_gather_small("gather_small_grads", _pad_rows(small_vec, BLK))
    small_sum = _sum_parts(small_all, "sum_small_grads").reshape(-1)[:n_small]
    dmod_all = small_all.reshape(NDEV, -1)[:, :n_layers * 3 * d].reshape(NDEV, n_layers, 3 * d)

    off = n_layers * 3 * d
    grad_b_mod = small_sum[:off].reshape(n_layers, 3 * d)
    per_layer = 2 * d + HG_HEAD_DIM + WIDTH + 3 * WIDTH
    g_ln_g, g_ln_b, g_norm, g_lbs, g_conv = [], [], [], [], []
    for l in range(n_layers):
        seg = small_sum[off + l * per_layer: off + (l + 1) * per_layer]
        g_ln_g.append(seg[:d])
        g_ln_b.append(seg[d:2 * d])
        g_norm.append(seg[2 * d:2 * d + HG_HEAD_DIM])
        g_lbs.append(seg[2 * d + HG_HEAD_DIM:2 * d + HG_HEAD_DIM + WIDTH])
        g_conv.append(seg[2 * d + HG_HEAD_DIM + WIDTH:].reshape(3, WIDTH))
    grad_ln_g, grad_ln_b = jnp.stack(g_ln_g), jnp.stack(g_ln_b)
    grad_norm = jnp.stack(g_norm)
    _, lbs_vjp = jax.vjp(_lower_bound_table, lower_bounds)
    grad_lower = lbs_vjp(jnp.stack(g_lbs))[0]
    grad_conv = lax.dynamic_slice_in_dim(jnp.stack(g_conv), me * cw_cols, cw_cols, axis=2)

    dmod_mine = lax.dynamic_slice_in_dim(dmod_all, me * cm, cm, axis=2).transpose(1, 0, 2)
    grad_w_mod = _wmod_grad(c_all.T, dmod_mine)

    p_in, p_branch, p_out = [None] * n_layers, [None] * n_layers, [None] * n_layers
    for l in reversed(range(n_layers)):
        in_started, rest_started = scattering[l]
        p_branch_l, p_out[l] = _exchange_wait(f"scatter_rest_{l}_wait", rest_started, grad_w_mod, rest_scatter)[1]
        p_branch[l] = p_branch_l.reshape(NDEV, 3 * WIDTH, dsh)
        p_in[l] = _exchange_wait(f"scatter_in_{l}_chips_wait", in_started, grad_w_mod, chip_sum_sends)[1][0]

    def adam(parts_list, w, m, v, name):
        shape = w.shape
        cols = shape[-1]
        flat = lambda a: a.reshape(-1, cols)
        outs = _sum_adamw(parts_list, flat(w), flat(m), flat(v), name)
        return [o.reshape(shape) for o in outs]

    r_w_in = adam(p_in, w_in, m_w_in, v_w_in, "adamw_w_in")
    r_w_branch = adam(p_branch, w_branch, m_w_branch, v_w_branch, "adamw_w_branch")
    r_w_out = adam(p_out, w_out, m_w_out, v_w_out, "adamw_w_out")
    r_w_mod = adam([grad_w_mod.reshape(1, -1, cm)], w_mod, m_w_mod, v_w_mod, "adamw_w_mod")

    small_names = ["b_mod", "conv_w", "hgrn_norm_w", "lower_bounds", "ln_g", "ln_b"]
    small_g = [grad_b_mod, grad_conv, grad_norm, grad_lower, grad_ln_g, grad_ln_b]
    small_w = [b_mod, conv_w, hgrn_norm_w, lower_bounds, ln_g, ln_b]
    small_m = [m_b_mod, m_conv_w, m_hgrn_norm_w, m_lower_bounds, m_ln_g, m_ln_b]
    small_v = [v_b_mod, v_conv_w, v_hgrn_norm_w, v_lower_bounds, v_ln_g, v_ln_b]
    pack = lambda arrs: _pad_rows(jnp.concatenate([a.reshape(-1) for a in arrs]), BLK)
    packed = _sum_adamw([pack(small_g)[None]], pack(small_w), pack(small_m), pack(small_v), "adamw_small")
    r_small = {n: [] for n in small_names}
    for res in packed:
        flat = res.reshape(-1)
        pos = 0
        for n, w in zip(small_names, small_w):
            r_small[n].append(flat[pos:pos + w.size].reshape(w.shape))
            pos += w.size

    results = {"w_mod": r_w_mod, "w_in": r_w_in, "w_branch": r_w_branch, "w_out": r_w_out, **r_small}
    order = ["w_mod", "b_mod", "w_in", "conv_w", "hgrn_norm_w", "lower_bounds", "w_branch", "w_out", "ln_g", "ln_b"]
    outs = [loss, grad_x]
    for idx in range(4):
        outs.extend(results[n][idx] for n in order)
    return tuple(outs)
```

```python
import jax
import jax.numpy as jnp
from jax import lax
from jax.experimental import pallas as pl
from jax.experimental.pallas import tpu as pltpu

F32 = jnp.float32
BF16 = jnp.bfloat16
NDEV = 8
N_LAYERS = 2
SB_HEAD_DIM = 64
HG_HEAD_DIM = 128
WIDTH = 512
BLK = 128
LN_EPS = 1e-5
RMS_EPS = 1e-6
ALPHA = (2.0 * N_LAYERS) ** 0.25
ADAM_LR, ADAM_B1, ADAM_B2, ADAM_EPS, ADAM_WD, ADAM_STEP = 0.001, 0.9, 0.999, 1e-08, 0.01, 10
VMEM_LIMIT = 56 * 1024 * 1024
MESH = pl.DeviceIdType.MESH
HG_LEVELS = (64, 32, 16, 8, 4, 2, 1)


def _pcall(body, *, name, out_shape, grid=None, in_specs=None, out_specs=None, scratch_shapes=(),
           semantics=None, aliases=None):
    kwargs = {}
    if grid is not None:
        kwargs["grid"] = grid
    if in_specs is not None:
        kwargs["in_specs"] = in_specs
    if out_specs is not None:
        kwargs["out_specs"] = out_specs
    if aliases:
        kwargs["input_output_aliases"] = aliases
    return pl.pallas_call(
        body, name=name, out_shape=out_shape, scratch_shapes=list(scratch_shapes),
        compiler_params=pltpu.CompilerParams(dimension_semantics=semantics, vmem_limit_bytes=VMEM_LIMIT),
        interpret=False, **kwargs)


def _dot(a, b):
    return jnp.dot(a, b, preferred_element_type=F32)


def _dot_nt(a, b):
    return lax.dot_general(a, b, (((1,), (1,)), ((), ())), preferred_element_type=F32)


def _dot_tn(a, b):
    return lax.dot_general(a, b, (((0,), (0,)), ((), ())), preferred_element_type=F32)


def _split3(x):
    x1 = x.astype(BF16)
    r1 = x - x1.astype(F32)
    x2 = r1.astype(BF16)
    r2 = r1 - x2.astype(F32)
    return x1, x2, r2.astype(BF16)


def _split2(x):
    x1 = x.astype(BF16)
    return x1, (x - x1.astype(F32)).astype(BF16)


def _dot_exact_l(m_bf16, x):
    x1, x2, x3 = _split3(x)
    return _dot(m_bf16, x1) + _dot(m_bf16, x2) + _dot(m_bf16, x3)


def _sigmoid(x):
    return 1.0 / (1.0 + jnp.exp(-x))


def _silu_and_grad(x):
    s = _sigmoid(x)
    return x * s, s * (1.0 + x * (1.0 - s))


LOG2E = 1.4426950408889634
MASKED_SCORE = -1e30


def _softplus2_parts(z2):
    minus_abs = lax.bitcast_convert_type(lax.bitcast_convert_type(z2, jnp.int32) | jnp.int32(-2 ** 31), F32)
    e = jnp.exp2(minus_abs)
    sp2 = jnp.maximum(z2, 0.0) + jnp.log2(1.0 + e)
    r = 1.0 / (1.0 + e)
    return sp2, jnp.where(z2 >= 0.0, r, e * r)


def _split2_lanes(x):
    x1 = x.astype(BF16)
    return jnp.concatenate([x1, (x - x1.astype(F32)).astype(BF16)], axis=1)


def _iota2(shape, dim):
    return lax.broadcasted_iota(jnp.int32, shape, dim)


def _standardize(x):
    mu = jnp.mean(x, axis=-1, keepdims=True)
    xc = x - mu
    var = jnp.mean(xc * xc, axis=-1, keepdims=True)
    rstd = lax.rsqrt(var + LN_EPS)
    return xc * rstd, rstd


def _standardize_bwd(xhat, rstd, dxhat):
    m1 = jnp.mean(dxhat, axis=-1, keepdims=True)
    m2 = jnp.mean(dxhat * xhat, axis=-1, keepdims=True)
    return rstd * (dxhat - m1 - xhat * m2)


def _my_index():
    return 4 * lax.axis_index("x") + 2 * lax.axis_index("y") + lax.axis_index("c")


def _exchange(name, ins, out_shapes, transfers, in_vmem):
    n_in, n_out, n_t = len(ins), len(out_shapes), len(transfers)

    def body(*refs):
        in_refs, out_refs = refs[:n_in], refs[n_in:n_in + n_out]
        send_sems, recv_sems, local_sems = refs[n_in + n_out:]
        x, y, c = lax.axis_index("x"), lax.axis_index("y"), lax.axis_index("c")
        me = 4 * x + 2 * y + c
        started = []
        for t, (i, o, src_fn, dst_fn) in enumerate(transfers):
            own = pltpu.make_async_copy(src_fn(in_refs[i], me), dst_fn(out_refs[o], me), local_sems.at[t])
            own.start()
            started.append(own)
        arrivals = []
        for k in range(1, NDEV):
            px = x ^ ((k >> 2) & 1)
            py = y ^ ((k >> 1) & 1)
            pc = c ^ (k & 1)
            peer = 4 * px + 2 * py + pc
            for t, (i, o, src_fn, dst_fn) in enumerate(transfers):
                sem = t * (NDEV - 1) + k - 1
                push = pltpu.make_async_remote_copy(
                    src_ref=src_fn(in_refs[i], peer), dst_ref=dst_fn(out_refs[o], me),
                    send_sem=send_sems.at[sem], recv_sem=recv_sems.at[sem],
                    device_id=(px, py, pc), device_id_type=MESH)
                push.start()
                started.append(push)
                arrivals.append(pltpu.make_async_remote_copy(
                    src_ref=src_fn(in_refs[i], peer), dst_ref=dst_fn(out_refs[o], peer),
                    send_sem=send_sems.at[sem], recv_sem=recv_sems.at[sem],
                    device_id=(px, py, pc), device_id_type=MESH))
        for arrival in arrivals:
            arrival.wait_recv()
        for cp in started[n_t:]:
            cp.wait_send()
        for own in started[:n_t]:
            own.wait()

    space = pltpu.VMEM if in_vmem else pl.ANY
    spec = pl.BlockSpec(memory_space=space)
    return _pcall(
        body, name=name, out_shape=out_shapes,
        in_specs=[spec] * n_in, out_specs=[spec] * n_out,
        scratch_shapes=[pltpu.SemaphoreType.DMA((n_t * (NDEV - 1),)),
                        pltpu.SemaphoreType.DMA((n_t * (NDEV - 1),)),
                        pltpu.SemaphoreType.DMA((n_t,))])(*ins)


def _whole(ref, dev):
    return ref


def _slot(ref, dev):
    return ref.at[dev]


def _all_gather_small(name, v):
    out = _exchange(name, [v], [jax.ShapeDtypeStruct((NDEV,) + v.shape, v.dtype)],
                    [(0, 0, _whole, _slot)], in_vmem=True)
    return out[0]


_HBM_SPEC = pl.BlockSpec(memory_space=pltpu.HBM)
_SEM_SPEC = pl.BlockSpec(memory_space=pltpu.SEMAPHORE)
_DATAFLOW = pltpu.SideEffectType.DATAFLOW_SIDE_EFFECTING


def _peer(x, y, c, k):
    px = x ^ ((k >> 2) & 1)
    py = y ^ ((k >> 1) & 1)
    pc = c ^ (k & 1)
    return (px, py, pc), 4 * px + 2 * py + pc


def _direct_sends(transfers):
    sends = []
    for k in range(1, NDEV):
        for i, o, src_fn, dst_fn in transfers:
            sends.append((k,
                          lambda ins, lands, me, i=i, k=k, src_fn=src_fn: src_fn(ins[i], me ^ k),
                          lambda lands, me, o=o, dst_fn=dst_fn: dst_fn(lands[o], me),
                          lambda lands, me, o=o, k=k, dst_fn=dst_fn: dst_fn(lands[o], me ^ k)))
    return sends


def _exchange_start(name, ins, lands, sends, after=None):
    n_in, n_buf = len(ins), len(ins) + len(lands)
    n_sem = len(sends)

    def body(*refs):
        in_refs, land_refs = refs[:n_in], refs[n_in:n_buf]
        n_skip = n_buf + (0 if after is None else 1)
        send_sems, recv_sems, token = refs[n_skip], refs[n_skip + 1], refs[-1]
        x, y, c = lax.axis_index("x"), lax.axis_index("y"), lax.axis_index("c")
        me = 4 * x + 2 * y + c
        for t, (k, src_fn, dst_fn, _) in enumerate(sends):
            pltpu.make_async_remote_copy(
                src_ref=src_fn(in_refs, land_refs, me), dst_ref=dst_fn(land_refs, me),
                send_sem=send_sems.at[t], recv_sem=recv_sems.at[t],
                device_id=_peer(x, y, c, k)[0], device_id_type=MESH).start()
        token[...] = jnp.zeros_like(token)

    bufs = [pltpu.with_memory_space_constraint(a, pltpu.HBM) for a in list(ins) + list(lands)]
    extra = [] if after is None else [after]
    outs = pl.pallas_call(
        body, name=name,
        out_shape=(pltpu.SemaphoreType.DMA((n_sem,)), pltpu.SemaphoreType.DMA((n_sem,)))
        + tuple(pltpu.HBM(a.shape, a.dtype) for a in bufs) + (jax.ShapeDtypeStruct((8, BLK), F32),),
        in_specs=[_HBM_SPEC] * n_buf + [pl.BlockSpec(memory_space=pl.ANY)] * len(extra),
        out_specs=(_SEM_SPEC, _SEM_SPEC) + (_HBM_SPEC,) * n_buf + (pl.BlockSpec(memory_space=pltpu.VMEM),),
        input_output_aliases={b: 2 + b for b in range(n_buf)},
        compiler_params=pltpu.CompilerParams(has_side_effects=_DATAFLOW),
        interpret=False)(*bufs, *extra)
    return outs[0], outs[1], list(outs[2:2 + n_in]), list(outs[2 + n_in:2 + n_buf]), outs[-1]


def _exchange_wait(name, started, after, sends):
    send_sems, recv_sems, ins, lands, _ = started
    n_in, n_buf = len(ins), len(ins) + len(lands)

    def body(*refs):
        in_refs, land_refs = refs[:n_in], refs[n_in:n_buf]
        send_sems, recv_sems = refs[n_buf], refs[n_buf + 1]
        x, y, c = lax.axis_index("x"), lax.axis_index("y"), lax.axis_index("c")
        me = 4 * x + 2 * y + c
        for t, (k, src_fn, _, rcv_fn) in enumerate(sends):
            cp = pltpu.make_async_remote_copy(
                src_ref=src_fn(in_refs, land_refs, me), dst_ref=rcv_fn(land_refs, me),
                send_sem=send_sems.at[t], recv_sem=recv_sems.at[t],
                device_id=_peer(x, y, c, k)[0], device_id_type=MESH)
            cp.wait_send()
            cp.wait_recv()

    bufs = list(ins) + list(lands)
    outs = pl.pallas_call(
        body, name=name, out_shape=tuple(pltpu.HBM(a.shape, a.dtype) for a in bufs),
        in_specs=[_HBM_SPEC] * n_buf + [_SEM_SPEC, _SEM_SPEC, pl.BlockSpec(memory_space=pl.ANY)],
        out_specs=(_HBM_SPEC,) * n_buf,
        input_output_aliases={b: b for b in range(n_buf)},
        compiler_params=pltpu.CompilerParams(has_side_effects=_DATAFLOW),
        interpret=False)(*bufs, send_sems, recv_sems, after)
    return list(outs[:n_in]), list(outs[n_in:])


def _place_own(shape, dtype, own, start):
    return lax.dynamic_update_slice(lax.empty(shape, dtype), own, start)


def _place_own_window(name, shape, own, me):
    rows, cols = own.shape

    def body(me_ref, zone_in, own_ref, zone_ref):
        del me_ref, zone_in
        zone_ref[...] = own_ref[...]

    return pl.pallas_call(
        body, name=name, out_shape=jax.ShapeDtypeStruct(shape, own.dtype),
        grid_spec=pltpu.PrefetchScalarGridSpec(
            num_scalar_prefetch=1, grid=(1,),
            in_specs=[pl.BlockSpec(memory_space=pl.ANY), pl.BlockSpec((rows, cols), lambda i, me_ref: (0, 0))],
            out_specs=pl.BlockSpec((rows, cols), lambda i, me_ref: (0, me_ref[0]))),
        input_output_aliases={1: 0},
        compiler_params=pltpu.CompilerParams(dimension_semantics=("arbitrary",), vmem_limit_bytes=VMEM_LIMIT),
        interpret=False)(me.reshape(1).astype(jnp.int32), lax.empty(shape, own.dtype), own)


def _mod_fwd(c_all, w_mod, b_mod_mine):
    n_layers, _, cm = w_mod.shape

    def body(c_ref, w_ref, b_ref, o_ref):
        for l in range(n_layers):
            o_ref[l] = jnp.dot(c_ref[...], w_ref[l], preferred_element_type=F32,
                               precision=lax.Precision.HIGHEST) + b_ref[l]

    return _pcall(body, name="mod_fwd", out_shape=jax.ShapeDtypeStruct((n_layers, NDEV, cm), F32))(
        c_all, w_mod, b_mod_mine)


def _ln_proj(x, shift, scale, w_full, name):
    s_len, d = x.shape
    n = w_full.shape[1]
    tm = min(512, s_len)
    tn = 1024

    def body(x_ref, sh_ref, sc_ref, w_ref, proj_ref, ht_ref, h_scr):
        @pl.when(pl.program_id(1) == 0)
        def _():
            xs, _ = _standardize(x_ref[...])
            h = xs * (1.0 + sc_ref[...]) + sh_ref[...]
            h_scr[...] = h.astype(BF16)
            ht_ref[...] = h.T.astype(BF16)

        proj_ref[...] = _dot(h_scr[...], w_ref[...])

    return _pcall(
        body, name=name,
        out_shape=(jax.ShapeDtypeStruct((s_len, n), F32), jax.ShapeDtypeStruct((d, s_len), BF16)),
        grid=(s_len // tm, n // tn),
        in_specs=[pl.BlockSpec((tm, d), lambda i, j: (i, 0)),
                  pl.BlockSpec((1, d), lambda i, j: (0, 0)),
                  pl.BlockSpec((1, d), lambda i, j: (0, 0)),
                  pl.BlockSpec((d, tn), lambda i, j: (0, j))],
        out_specs=(pl.BlockSpec((tm, tn), lambda i, j: (i, j)),
                   pl.BlockSpec((d, tm), lambda i, j: (0, i))),
        scratch_shapes=[pltpu.VMEM((tm, d), BF16)],
        semantics=("arbitrary", "arbitrary"))(x, shift, scale, w_full)


def _sb_group_blocks(nb):
    return min(4, nb)


def _sb_fwd(proj, name):
    s_len = proj.shape[0]
    nb = s_len // BLK
    n_pairs = WIDTH // BLK
    gb = _sb_group_blocks(nb)
    kw = gb * BLK

    def body(q_ref, k_ref, v_ref, o_ref, tot_ref):
        lane = _iota2((1, BLK), 1)
        row = _iota2((BLK, BLK), 0)
        col = _iota2((BLK, BLK), 1)
        half = jnp.concatenate([(row >= col).astype(BF16), jnp.ones((BLK, BLK), BF16)], axis=1)
        suffix_and_sum = jnp.concatenate([half, half], axis=0)
        qpos = _iota2((BLK, kw), 0)
        kpos = _iota2((BLK, kw), 1)
        head_lanes = [(lane // SB_HEAD_DIM) == hh for hh in range(2)]

        def scores(i, gi, qms, masked):
            c0 = pl.multiple_of(gi * kw, kw)
            kb = k_ref[pl.ds(c0, kw), :].astype(BF16)
            z2s = [_dot_nt(qms[hh], kb) for hh in range(2)]
            if masked:
                valid = (c0 + kpos) < (i * BLK + qpos)
                z2s = [jnp.where(valid, z2, MASKED_SCORE) for z2 in z2s]
            return tuple(z2s)

        def accumulate(gi, z2s, carry):
            c0 = pl.multiple_of(gi * kw, kw)
            vf = v_ref[pl.ds(c0, kw), :]
            sp2s = [_softplus2_parts(z2)[0] for z2 in z2s]
            terms = [[_split2_lanes(sp2[:, b * BLK:(b + 1) * BLK]) for b in range(gb)] for sp2 in sp2s]
            sums = [[_dot(t, suffix_and_sum) for t in head_terms] for head_terms in terms]
            weights, laters = [], []
            for hh in range(2):
                later = carry[2 * hh + 1]
                parts = [None] * gb
                for b in reversed(range(gb)):
                    parts[b] = sums[hh][b][:, :BLK] + later
                    later = later + sums[hh][b][:, BLK:]
                weights.append(jnp.exp2(z2s[hh] - jnp.concatenate(parts, axis=1)).astype(BF16))
                laters.append(later)
            outs = [_dot(weights[hh], jnp.where(head_lanes[hh], vf, 0.0).astype(BF16)) for hh in range(2)]
            return (carry[0] + outs[0], laters[0], carry[2] + outs[1], laters[1])

        def queries(i):
            qf = q_ref[pl.ds(pl.multiple_of(i * BLK, BLK), BLK), :] * (SB_HEAD_DIM ** -0.5 * LOG2E)
            return [jnp.where(head_lanes[hh], qf, 0.0).astype(BF16) for hh in range(2)]

        def qblock(i, first_scores):
            r0 = pl.multiple_of(i * BLK, BLK)
            qms = queries(i)
            zero = jnp.zeros((BLK, BLK), F32)
            last = i // gb

            def step(jj, state):
                gi = last - 1 - jj
                return scores(i, gi, qms, False) + accumulate(gi + 1, state[:2], state[2:])

            state = lax.fori_loop(0, last, step, first_scores + (zero,) * 4)
            nxt = jnp.minimum(i + 1, nb - 1)
            next_scores = scores(nxt, nxt // gb, queries(nxt), True)
            carry = accumulate(0, state[:2], state[2:])
            o_ref[pl.ds(r0, BLK), :] = carry[0] + carry[2]
            tot_ref[0, pl.ds(r0, BLK), :] = carry[1]
            tot_ref[1, pl.ds(r0, BLK), :] = carry[3]
            return next_scores

        lax.fori_loop(0, nb, qblock, scores(0, 0, queries(0), True))

    col_spec = lambda off: pl.BlockSpec((s_len, BLK), lambda p: (0, off + p))
    return _pcall(
        body, name=name,
        out_shape=(jax.ShapeDtypeStruct((s_len, WIDTH), F32),
                   jax.ShapeDtypeStruct((2 * n_pairs, s_len, BLK), F32)),
        grid=(n_pairs,),
        in_specs=[col_spec(0), col_spec(n_pairs), col_spec(2 * n_pairs)],
        out_specs=(pl.BlockSpec((s_len, BLK), lambda p: (0, p)),
                   pl.BlockSpec((2, s_len, BLK), lambda p: (p, 0, 0))),
        semantics=("arbitrary",))(proj, proj, proj)


def _hg_masks(mask_ref):
    row = _iota2((BLK, BLK), 0)
    col = _iota2((BLK, BLK), 1)
    for v, m in enumerate(HG_LEVELS):
        same = (row // (2 * m)) == (col // (2 * m))
        mask_ref[v] = (same & ((row & m) != 0) & ((col & m) == 0)).astype(F32)


def _hg_mid(b, m):
    if m >= 4:
        n = BLK // (2 * m)
        mid = b.reshape(n, 2 * m, BLK)[:, m - 1:m, :]
        return jnp.broadcast_to(mid, (n, 2 * m, BLK)).reshape(BLK, BLK)
    pos = _iota2((BLK, BLK), 0) & (2 * m - 1)
    out = b
    for p in range(2 * m):
        delta = (m - 1) - p
        if delta != 0:
            out = jnp.where(pos == p, pltpu.roll(b, (-delta) % BLK, 0), out)
    return out


def _hg_chunk_inputs(qraw, fpre, lb):
    sig = _sigmoid(fpre)
    f = lb + (1.0 - lb) * sig
    g = jnp.log(f)
    q, dq_fac = _silu_and_grad(qraw)
    return q, dq_fac, f, sig, g


def _hg_level_terms(q, k, b, v_idx, m, mask_ref):
    mid = _hg_mid(b, m)
    eq = jnp.exp(jnp.minimum(b - mid, 0.0))
    ek = jnp.exp(jnp.minimum(mid - b, 0.0))
    qt = (q * eq).astype(BF16)
    kt = (k * ek).astype(BF16)
    return qt, kt, eq, ek, mask_ref[v_idx]


def _hg_scores(q, k, b, mask_ref):
    sc = None
    for v_idx, m in enumerate(HG_LEVELS):
        qt, kt, _, _, msk = _hg_level_terms(q, k, b, v_idx, m, mask_ref)
        term = _dot_nt(qt, kt) * msk
        sc = term if sc is None else sc + term
    return sc


def _hgrn_fwd(proj, lb, name):
    s_len = proj.shape[0]
    nc = s_len // BLK
    nh = WIDTH // HG_HEAD_DIM
    base = 4 * WIDTH // BLK

    def body(q_ref, f_ref, i_ref, lb_ref, o_ref, mask_ref):
        _hg_masks(mask_ref)
        row = _iota2((BLK, BLK), 0)
        col = _iota2((BLK, BLK), 1)
        lower_incl = (col <= row).astype(BF16)
        lb_v = lb_ref[...]

        def chunk(ci, st):
            r0 = pl.multiple_of(ci * BLK, BLK)
            q, _, f, _, g = _hg_chunk_inputs(q_ref[pl.ds(r0, BLK), :], f_ref[pl.ds(r0, BLK), :], lb_v)
            k = 1.0 - f
            v = i_ref[pl.ds(r0, BLK), :]
            vb = v.astype(BF16)
            b = _dot_exact_l(lower_incl, g)
            b_end = b[BLK - 1:BLK, :]
            inter = _dot_nt((q * jnp.exp(b)).astype(BF16), st.astype(BF16))
            sc = _hg_scores(q, k, b, mask_ref)
            diag = jnp.sum(q * k, axis=-1, keepdims=True)
            o_ref[pl.ds(r0, BLK), :] = inter + _dot(sc.astype(BF16), vb) + diag * v
            k_dec = (k * jnp.exp(b_end - b)).astype(BF16)
            return st * jnp.exp(b_end) + _dot_tn(vb, k_dec)

        lax.fori_loop(0, nc, chunk, jnp.zeros((HG_HEAD_DIM, HG_HEAD_DIM), F32))

    col_spec = lambda off: pl.BlockSpec((s_len, BLK), lambda h: (0, off + h))
    return _pcall(
        body, name=name, out_shape=jax.ShapeDtypeStruct((s_len, WIDTH), F32),
        grid=(nh,),
        in_specs=[col_spec(base), col_spec(base + nh), col_spec(base + 2 * nh),
                  pl.BlockSpec((1, BLK), lambda h: (0, h))],
        out_specs=pl.BlockSpec((s_len, BLK), lambda h: (0, h)),
        scratch_shapes=[pltpu.VMEM((len(HG_LEVELS), BLK, BLK), F32)],
        semantics=("arbitrary",))(proj, proj, proj, lb)


def _rms_heads(o_b, norm_w):
    n_parts, h_parts, r_parts = [], [], []
    for h in range(WIDTH // HG_HEAD_DIM):
        sl = slice(h * HG_HEAD_DIM, (h + 1) * HG_HEAD_DIM)
        o = o_b[:, sl]
        rstd = lax.rsqrt(jnp.mean(o * o, axis=-1, keepdims=True) + RMS_EPS)
        ohat = o * rstd
        h_parts.append(ohat)
        n_parts.append(ohat * norm_w[:, sl])
        r_parts.append(jnp.broadcast_to(rstd, o.shape))
    cat = lambda parts: jnp.concatenate(parts, axis=-1)
    return cat(n_parts), cat(h_parts), cat(r_parts)


def _shift_rows_down(halo, cur, k):
    tm = cur.shape[0]
    ext = jnp.concatenate([halo, cur], axis=0)
    return pltpu.roll(ext, k, 0)[8:8 + tm]


def _shift_rows_up(cur, halo, k):
    tm = cur.shape[0]
    ext = jnp.concatenate([cur, halo], axis=0)
    return pltpu.roll(ext, (tm + 8 - k) % (tm + 8), 0)[0:tm]


def _merge_fwd(x, proj, o_a, o_b, gate, norm_w, conv_w, wb, w_out, ln_g, ln_b, name):
    s_len, d = x.shape
    tm = min(256, s_len)
    hb = tm // 8

    def body(x_ref, oa_ref, za_ref, ob_ref, zb_ref, pre_ref, post_ref, u_ref, zc_ref, hpre_ref, hu_ref, g_ref,
             gate_ref, nw_ref, cw_ref, wb_ref, wo_ref, lg_ref, lbias_ref, xn_ref, mg_ref, yc_ref):
        i = pl.program_id(0)
        sa, _ = _silu_and_grad(za_ref[...])
        y_a = (oa_ref[...] * sa).astype(BF16)
        n_b, _, _ = _rms_heads(ob_ref[...], nw_ref[...])
        sb, _ = _silu_and_grad(zb_ref[...])
        y_b = (n_b * sb).astype(BF16)
        a = pre_ref[...] * u_ref[...]
        halo = jnp.where(i > 0, hpre_ref[...] * hu_ref[...], 0.0)
        cw = cw_ref[...]
        conv = cw[0:1] * _shift_rows_down(halo, a, 2) + cw[1:2] * _shift_rows_down(halo, a, 1) + cw[2:3] * a
        sc, _ = _silu_and_grad(zc_ref[...])
        y_c = (post_ref[...] * conv * sc).astype(BF16)
        merged = None
        for k, yk in enumerate((y_a, y_b, y_c)):
            yc_ref[:, k * WIDTH:(k + 1) * WIDTH] = yk
            term = _sigmoid(g_ref[:, k * d:(k + 1) * d]) * _dot(yk, wb_ref[k])
            merged = term if merged is None else merged + term
        mb = merged.astype(BF16)
        mg_ref[...] = mb
        y = _dot(mb, wo_ref[...])
        r = ALPHA * x_ref[...] + (1.0 + gate_ref[...]) * y
        rhat, _ = _standardize(r)
        xn_ref[...] = rhat * lg_ref[...] + lbias_ref[...]

    wcol = lambda cb: pl.BlockSpec((tm, WIDTH), lambda i: (i, cb))
    halo_spec = lambda cb: pl.BlockSpec((8, WIDTH), lambda i: (jnp.maximum(i * hb - 1, 0), cb))
    vec = lambda w: pl.BlockSpec((1, w), lambda i: (0, 0))
    return _pcall(
        body, name=name,
        out_shape=(jax.ShapeDtypeStruct((s_len, d), F32), jax.ShapeDtypeStruct((s_len, d), BF16),
                   jax.ShapeDtypeStruct((s_len, 3 * WIDTH), BF16)),
        grid=(s_len // tm,),
        in_specs=[pl.BlockSpec((tm, d), lambda i: (i, 0)),
                  wcol(0), wcol(3), wcol(0), wcol(7), wcol(8), wcol(9), wcol(10), wcol(11),
                  halo_spec(8), halo_spec(10),
                  pl.BlockSpec((tm, 3 * d), lambda i: (i, 2)),
                  vec(d), vec(WIDTH),
                  pl.BlockSpec((3, WIDTH), lambda i: (0, 0)),
                  pl.BlockSpec((3, WIDTH, d), lambda i: (0, 0, 0)),
                  pl.BlockSpec((d, d), lambda i: (0, 0)),
                  vec(d), vec(d)],
        out_specs=(pl.BlockSpec((tm, d), lambda i: (i, 0)), pl.BlockSpec((tm, d), lambda i: (i, 0)),
                   pl.BlockSpec((tm, 3 * WIDTH), lambda i: (i, 0))),
        semantics=("arbitrary",))(x, o_a, proj, o_b, proj, proj, proj, proj, proj, proj, proj, proj,
                                  gate, norm_w, conv_w, wb, w_out, ln_g, ln_b)


def _loss_fwd_bwd(y, target):
    s_len, d = y.shape
    tm = min(512, s_len)

    def body(y_ref, t_ref, loss_ref, dy_ref):
        @pl.when(pl.program_id(0) == 0)
        def _():
            loss_ref[...] = jnp.zeros_like(loss_ref)

        e = y_ref[...] - t_ref[...]
        dy_ref[...] = e * (1.0 / d)
        part = jnp.sum(jnp.sum(e * e, axis=-1, keepdims=True), axis=0, keepdims=True)
        loss_ref[...] += part * (0.5 / d)

    tile = pl.BlockSpec((tm, d), lambda i: (i, 0))
    return _pcall(body, name="loss", grid=(s_len // tm,),
                  out_shape=(jax.ShapeDtypeStruct((1, 1), F32), jax.ShapeDtypeStruct((s_len, d), F32)),
                  in_specs=[tile, tile],
                  out_specs=(pl.BlockSpec((1, 1), lambda i: (0, 0)), tile),
                  semantics=("arbitrary",))(y, target)


def _merge_bwd(dxn, x, merged, ycat, proj, gate, wb, w_out, ln_g, name):
    s_len, d = x.shape
    tm = min(256, s_len)

    def body(dxn_ref, x_ref, mg_ref, yc_ref, g_ref, gate_ref, wb_ref, wo_ref, lg_ref,
             dres_ref, dyc_ref, dg_ref, gwo_ref, gwb_ref, vec_ref):
        @pl.when(pl.program_id(0) == 0)
        def _():
            gwo_ref[...] = jnp.zeros_like(gwo_ref)
            gwb_ref[...] = jnp.zeros_like(gwb_ref)
            vec_ref[...] = jnp.zeros_like(vec_ref)

        mb = mg_ref[...]
        one_gate = 1.0 + gate_ref[...]
        y = _dot(mb, wo_ref[...])
        r = ALPHA * x_ref[...] + one_gate * y
        rhat, rstd = _standardize(r)
        dxn = dxn_ref[...]
        dr = _standardize_bwd(rhat, rstd, dxn * lg_ref[...])
        vec_ref[0:1, :] += jnp.sum(dxn * rhat, axis=0, keepdims=True)
        vec_ref[1:2, :] += jnp.sum(dxn, axis=0, keepdims=True)
        vec_ref[2:3, :] += jnp.sum(dr * y, axis=0, keepdims=True)
        dres_ref[...] = ALPHA * dr
        dy = (one_gate * dr).astype(BF16)
        gwo_ref[...] += _dot_tn(mb, dy)
        dmerged = _dot_nt(dy, wo_ref[...])
        for k in range(3):
            yk = yc_ref[:, k * WIDTH:(k + 1) * WIDTH]
            sg = _sigmoid(g_ref[:, k * d:(k + 1) * d])
            pk = _dot(yk, wb_ref[k])
            dg_ref[:, k * d:(k + 1) * d] = (dmerged * pk * sg * (1.0 - sg)).astype(BF16)
            dpk = (dmerged * sg).astype(BF16)
            dyc_ref[:, k * WIDTH:(k + 1) * WIDTH] = _dot_nt(dpk, wb_ref[k])
            gwb_ref[k] += _dot_tn(yk, dpk)

    tile = lambda w: pl.BlockSpec((tm, w), lambda i: (i, 0))
    vec = pl.BlockSpec((1, d), lambda i: (0, 0))
    return _pcall(
        body, name=name,
        out_shape=(jax.ShapeDtypeStruct((s_len, d), F32), jax.ShapeDtypeStruct((s_len, 3 * WIDTH), F32),
                   jax.ShapeDtypeStruct(proj.shape, BF16), jax.ShapeDtypeStruct((d, d), F32),
                   jax.ShapeDtypeStruct((3, WIDTH, d), F32), jax.ShapeDtypeStruct((8, d), F32)),
        grid=(s_len // tm,),
        in_specs=[tile(d), tile(d), tile(d), tile(3 * WIDTH),
                  pl.BlockSpec((tm, 3 * d), lambda i: (i, 2)),
                  vec, pl.BlockSpec((3, WIDTH, d), lambda i: (0, 0, 0)),
                  pl.BlockSpec((d, d), lambda i: (0, 0)), vec],
        out_specs=(tile(d), tile(3 * WIDTH), pl.BlockSpec((tm, 3 * d), lambda i: (i, 2)),
                   pl.BlockSpec((d, d), lambda i: (0, 0)),
                   pl.BlockSpec((3, WIDTH, d), lambda i: (0, 0, 0)),
                   pl.BlockSpec((8, d), lambda i: (0, 0))),
        semantics=("arbitrary",))(dxn, x, merged, ycat, proj, gate, wb, w_out, ln_g)


def _branch_bwd(dycat, proj, o_a, o_b, norm_w, conv_w, dproj, name):
    s_len = proj.shape[0]
    tm = min(256, s_len)
    hb = tm // 8
    n_tiles = s_len // tm

    def body(dya_ref, dyb_ref, dyc_ref, oa_ref, za_ref, ob_ref, zb_ref, pre_ref, post_ref, u_ref, zc_ref,
             hpre_ref, hu_ref, ndyc_ref, npost_ref, nzc_ref, nw_ref, cw_ref, dproj_in,
             dproj_ref, doa_ref, dob_ref, vec_ref, dza_scr, dzb_scr, dc_scr, sems):
        del dproj_in
        i = pl.program_id(0)

        @pl.when(i == 0)
        def _():
            vec_ref[...] = jnp.zeros_like(vec_ref)

        sa, dsa = _silu_and_grad(za_ref[...])
        dya = dya_ref[...]
        doa_ref[...] = dya * sa
        dza_scr[...] = (dya * oa_ref[...] * dsa).astype(BF16)
        nw = nw_ref[...]
        n_b, ohat, rstd = _rms_heads(ob_ref[...], nw)
        sb, dsb = _silu_and_grad(zb_ref[...])
        dyb = dyb_ref[...]
        dzb_scr[...] = (dyb * n_b * dsb).astype(BF16)
        dn = dyb * sb
        vec_ref[0:1, :] += jnp.sum(dn * ohat, axis=0, keepdims=True)
        dnw = dn * nw
        parts = []
        for h in range(WIDTH // HG_HEAD_DIM):
            sl = slice(h * HG_HEAD_DIM, (h + 1) * HG_HEAD_DIM)
            m2 = jnp.mean(dnw[:, sl] * ohat[:, sl], axis=-1, keepdims=True)
            parts.append(rstd[:, sl] * (dnw[:, sl] - ohat[:, sl] * m2))
        dob_ref[...] = jnp.concatenate(parts, axis=-1)
        cw = cw_ref[...]
        pre, u, post = pre_ref[...], u_ref[...], post_ref[...]
        a = pre * u
        halo = jnp.where(i > 0, hpre_ref[...] * hu_ref[...], 0.0)
        a1 = _shift_rows_down(halo, a, 1)
        a2 = _shift_rows_down(halo, a, 2)
        conv = cw[0:1] * a2 + cw[1:2] * a1 + cw[2:3] * a
        sc, dsc = _silu_and_grad(zc_ref[...])
        dyc = dyc_ref[...]
        dconv = dyc * post * sc
        nsc, _ = _silu_and_grad(nzc_ref[...])
        nxt = jnp.where(i < n_tiles - 1, ndyc_ref[...] * npost_ref[...] * nsc, 0.0)
        da = cw[2:3] * dconv + cw[1:2] * _shift_rows_up(dconv, nxt, 1) + cw[0:1] * _shift_rows_up(dconv, nxt, 2)
        dc_scr[:, 0 * WIDTH:1 * WIDTH] = (da * u).astype(BF16)
        dc_scr[:, 1 * WIDTH:2 * WIDTH] = (dyc * conv * sc).astype(BF16)
        dc_scr[:, 2 * WIDTH:3 * WIDTH] = (da * pre).astype(BF16)
        dc_scr[:, 3 * WIDTH:4 * WIDTH] = (dyc * post * conv * dsc).astype(BF16)
        vec_ref[1:2, :] += jnp.sum(dconv * a2, axis=0, keepdims=True)
        vec_ref[2:3, :] += jnp.sum(dconv * a1, axis=0, keepdims=True)
        vec_ref[3:4, :] += jnp.sum(dconv * a, axis=0, keepdims=True)
        rows = pl.ds(pl.multiple_of(i * tm, tm), tm)
        copies = [pltpu.make_async_copy(dza_scr, dproj_ref.at[rows, 3 * WIDTH:4 * WIDTH], sems.at[0]),
                  pltpu.make_async_copy(dzb_scr, dproj_ref.at[rows, 7 * WIDTH:8 * WIDTH], sems.at[1]),
                  pltpu.make_async_copy(dc_scr, dproj_ref.at[rows, 8 * WIDTH:12 * WIDTH], sems.at[2])]
        for cp in copies:
            cp.start()
        for cp in copies:
            cp.wait()

    wcol = lambda cb: pl.BlockSpec((tm, WIDTH), lambda i: (i, cb))
    prev = lambda cb: pl.BlockSpec((8, WIDTH), lambda i: (jnp.maximum(i * hb - 1, 0), cb))
    nxt = lambda cb: pl.BlockSpec((8, WIDTH), lambda i: (jnp.minimum((i + 1) * hb, s_len // 8 - 1), cb))
    anyspec = pl.BlockSpec(memory_space=pl.ANY)
    out = jax.ShapeDtypeStruct((s_len, WIDTH), F32)
    return _pcall(
        body, name=name,
        out_shape=(jax.ShapeDtypeStruct(dproj.shape, dproj.dtype), out, out, jax.ShapeDtypeStruct((8, WIDTH), F32)),
        grid=(n_tiles,),
        in_specs=[wcol(0), wcol(1), wcol(2), wcol(0), wcol(3), wcol(0), wcol(7), wcol(8), wcol(9), wcol(10), wcol(11),
                  prev(8), prev(10), nxt(2), nxt(9), nxt(11),
                  pl.BlockSpec((1, WIDTH), lambda i: (0, 0)), pl.BlockSpec((3, WIDTH), lambda i: (0, 0)), anyspec],
        out_specs=(anyspec, wcol(0), wcol(0), pl.BlockSpec((8, WIDTH), lambda i: (0, 0))),
        scratch_shapes=[pltpu.VMEM((tm, WIDTH), BF16), pltpu.VMEM((tm, WIDTH), BF16),
                        pltpu.VMEM((tm, 4 * WIDTH), BF16), pltpu.SemaphoreType.DMA((3,))],
        aliases={18: 0},
        semantics=("arbitrary",))(dycat, dycat, dycat, o_a, proj, o_b, proj, proj, proj, proj, proj,
                                  proj, proj, dycat, proj, proj, norm_w, conv_w, dproj)


def _sb_bwd(proj, do_a, totals, dproj, name):
    s_len = proj.shape[0]
    nb = s_len // BLK
    n_pairs = WIDTH // BLK
    scale = SB_HEAD_DIM ** -0.5
    gb = _sb_group_blocks(nb)
    kw = gb * BLK

    def body(q_ref, k_ref, v_ref, do_ref, tot_ref, dproj_in, dproj_ref, dq_ref, dk_ref, dv_ref, out_scr, sems):
        del dproj_in
        lane = _iota2((1, BLK), 1)
        row = _iota2((BLK, BLK), 0)
        col = _iota2((BLK, BLK), 1)
        ones = jnp.ones((BLK, BLK), BF16)
        twice = lambda m: jnp.concatenate([m, m], axis=0)
        before_and_sum = twice(jnp.concatenate([(row < col).astype(BF16), ones], axis=1))
        upto_and_sum = twice(jnp.concatenate([(row <= col).astype(BF16), ones], axis=1))
        qpos = _iota2((BLK, kw), 0)
        kpos = _iota2((BLK, kw), 1)
        head_lanes = [(lane // SB_HEAD_DIM) == hh for hh in range(2)]
        dk_ref[...] = jnp.zeros_like(dk_ref)
        dv_ref[...] = jnp.zeros_like(dv_ref)

        causal = kpos - qpos

        def scores(i, gi, qms):
            c0 = pl.multiple_of(gi * kw, kw)
            kb = k_ref[pl.ds(c0, kw), :].astype(BF16)
            valid = causal < i * BLK - c0
            return tuple(jnp.where(valid, _dot_nt(qms[hh], kb), MASKED_SCORE) for hh in range(2))

        def process(gi, z2s, qms, doms, totals_i, carry):
            c0 = pl.multiple_of(gi * kw, kw)
            kf = k_ref[pl.ds(c0, kw), :]
            vf = v_ref[pl.ds(c0, kw), :]
            kms = [jnp.where(head_lanes[hh], kf, 0.0).astype(BF16) for hh in range(2)]
            vms = [jnp.where(head_lanes[hh], vf, 0.0).astype(BF16) for hh in range(2)]
            das = [_dot_nt(doms[hh], vms[hh]) for hh in range(2)]
            halves = [_softplus2_parts(z2) for z2 in z2s]
            terms = [[_split2_lanes(sp2[:, b * BLK:(b + 1) * BLK]) for b in range(gb)] for sp2, _ in halves]
            sums = [[_dot(t, before_and_sum) for t in head_terms] for head_terms in terms]
            weights, gmats, l_befores = [], [], []
            for hh in range(2):
                l_before = carry[3 * hh + 1]
                parts = []
                for b in range(gb):
                    parts.append(totals_i[hh] - l_before - sums[hh][b][:, :BLK])
                    l_before = l_before + sums[hh][b][:, BLK:]
                a = jnp.exp2(z2s[hh] - jnp.concatenate(parts, axis=1))
                weights.append(a.astype(BF16))
                gmats.append(a * das[hh])
                l_befores.append(l_before)
            terms = [[_split2_lanes(g[:, b * BLK:(b + 1) * BLK]) for b in range(gb)] for g in gmats]
            sums = [[_dot(t, upto_and_sum) for t in head_terms] for head_terms in terms]
            dzs, g_befores = [], []
            for hh in range(2):
                g_before = carry[3 * hh + 2]
                parts = []
                for b in range(gb):
                    parts.append(g_before + sums[hh][b][:, :BLK])
                    g_before = g_before + sums[hh][b][:, BLK:]
                dzs.append((gmats[hh] - halves[hh][1] * jnp.concatenate(parts, axis=1)).astype(BF16))
                g_befores.append(g_before)
            dks = [_dot_tn(dzs[hh], qms[hh]) for hh in range(2)]
            dvs = [_dot_tn(weights[hh], doms[hh]) for hh in range(2)]
            dqs = [_dot(dzs[hh], kms[hh]) for hh in range(2)]
            dk_ref[pl.ds(c0, kw), :] += (dks[0] + dks[1]) * (1.0 / LOG2E)
            dv_ref[pl.ds(c0, kw), :] += dvs[0] + dvs[1]
            return (carry[0] + dqs[0], l_befores[0], g_befores[0], carry[3] + dqs[1], l_befores[1], g_befores[1])

        def queries(i):
            qf = q_ref[pl.ds(pl.multiple_of(i * BLK, BLK), BLK), :] * (scale * LOG2E)
            return [jnp.where(head_lanes[hh], qf, 0.0).astype(BF16) for hh in range(2)]

        def qblock(i, first_scores):
            r0 = pl.multiple_of(i * BLK, BLK)
            qms = queries(i)
            dof = do_ref[pl.ds(r0, BLK), :]
            doms = [jnp.where(head_lanes[hh], dof, 0.0).astype(BF16) for hh in range(2)]
            totals_i = [tot_ref[hh, pl.ds(r0, BLK), :] for hh in range(2)]
            zero = jnp.zeros((BLK, BLK), F32)
            last = i // gb

            def step(gi, state):
                return scores(i, gi + 1, qms) + process(gi, state[:2], qms, doms, totals_i, state[2:])

            state = lax.fori_loop(0, last, step, first_scores + (zero,) * 6)
            nxt = jnp.minimum(i + 1, nb - 1)
            next_scores = scores(nxt, 0, queries(nxt))
            carry = process(last, state[:2], qms, doms, totals_i, state[2:])
            dq_ref[pl.ds(r0, BLK), :] = (carry[0] + carry[3]) * scale
            return next_scores

        lax.fori_loop(0, nb, qblock, scores(0, 0, queries(0)))
        pair = pl.program_id(0)
        copies = []
        for t, ref in enumerate((dq_ref, dk_ref, dv_ref)):
            out_scr[t] = ref[...].astype(BF16)
            col = pl.multiple_of((t * n_pairs + pair) * BLK, BLK)
            copies.append(pltpu.make_async_copy(out_scr.at[t], dproj_ref.at[:, pl.ds(col, BLK)], sems.at[t]))
            copies[-1].start()
        for cp in copies:
            cp.wait()

    col_spec = lambda off: pl.BlockSpec((s_len, BLK), lambda p: (0, off + p))
    anyspec = pl.BlockSpec(memory_space=pl.ANY)
    return _pcall(
        body, name=name, out_shape=jax.ShapeDtypeStruct(dproj.shape, dproj.dtype), grid=(n_pairs,),
        in_specs=[col_spec(0), col_spec(n_pairs), col_spec(2 * n_pairs), col_spec(0),
                  pl.BlockSpec((2, s_len, BLK), lambda p: (p, 0, 0)), anyspec],
        out_specs=anyspec,
        scratch_shapes=[pltpu.VMEM((s_len, BLK), F32)] * 3 + [pltpu.VMEM((3, s_len, BLK), BF16),
                                                              pltpu.SemaphoreType.DMA((3,))],
        aliases={5: 0},
        semantics=("arbitrary",))(proj, proj, proj, do_a, totals, dproj)


def _hgrn_bwd(proj, do_b, lb, dproj, name):
    s_len = proj.shape[0]
    nc = s_len // BLK
    nh = WIDTH // HG_HEAD_DIM
    base = 4 * WIDTH // BLK

    def body(q_ref, f_ref, i_ref, do_ref, lb_ref, dproj_in, dproj_ref, dlb_ref, mask_ref, st_ref, out_scr, sems):
        del dproj_in
        _hg_masks(mask_ref)
        row = _iota2((BLK, BLK), 0)
        col = _iota2((BLK, BLK), 1)
        lower_incl = (col <= row).astype(BF16)
        upper_incl = (col >= row).astype(BF16)
        lb_v = lb_ref[...]

        def load(ci):
            r0 = pl.multiple_of(ci * BLK, BLK)
            q, dq_fac, f, sig, g = _hg_chunk_inputs(q_ref[pl.ds(r0, BLK), :], f_ref[pl.ds(r0, BLK), :], lb_v)
            b = _dot_exact_l(lower_incl, g)
            return r0, q, dq_fac, f, sig, b, i_ref[pl.ds(r0, BLK), :]

        def fwd_chunk(ci, st):
            st_ref[ci] = st
            _, _, _, f, _, b, v = load(ci)
            b_end = b[BLK - 1:BLK, :]
            k_dec = ((1.0 - f) * jnp.exp(b_end - b)).astype(BF16)
            return st * jnp.exp(b_end) + _dot_tn(v.astype(BF16), k_dec)

        lax.fori_loop(0, nc, fwd_chunk, jnp.zeros((HG_HEAD_DIM, HG_HEAD_DIM), F32))

        def bwd_chunk(cc, carry):
            dst, suffix, dlb = carry
            ci = nc - 1 - cc
            r0, q, dq_fac, f, sig, b, v = load(ci)
            k = 1.0 - f
            vb = v.astype(BF16)
            do = do_ref[pl.ds(r0, BLK), :]
            dob = do.astype(BF16)
            b_end = b[BLK - 1:BLK, :]
            e_q = jnp.exp(b)
            e_k = jnp.exp(b_end - b)
            qe = (q * e_q).astype(BF16)
            kh = (k * e_k).astype(BF16)
            st1, st2 = _split2(st_ref[ci])
            ds1, ds2 = _split2(dst)
            dqe = _dot(dob, st1) + _dot(dob, st2)
            dkh = _dot(vb, ds1) + _dot(vb, ds2)
            dq = e_q * dqe
            dk = e_k * dkh
            dv = _dot_nt(kh, ds1)
            dst_new = dst * jnp.exp(b_end) + _dot_tn(dob, qe)
            dlog = qe.astype(F32) * dqe - kh.astype(F32) * dkh
            da = _dot_nt(dob, vb)
            sc = None
            for v_idx, m in enumerate(HG_LEVELS):
                qm, km, eq, ek, msk = _hg_level_terms(q, k, b, v_idx, m, mask_ref)
                term = _dot_nt(qm, km) * msk
                sc = term if sc is None else sc + term
                pm = (da * msk).astype(BF16)
                dqm = _dot(pm, km)
                dkm = _dot_tn(pm, qm)
                dq = dq + dqm * eq
                dk = dk + dkm * ek
                dlog = dlog + (qm.astype(F32) * dqm - km.astype(F32) * dkm)
            a_diag = jnp.sum(do * v, axis=-1, keepdims=True)
            s_diag = jnp.sum(q * k, axis=-1, keepdims=True)
            dq = dq + a_diag * k
            dk = dk + a_diag * q
            dv = dv + _dot_tn(sc.astype(BF16), dob) + s_diag * do
            dg = _dot_exact_l(upper_incl, dlog) + suffix
            dfull = dg / f - dk
            out_scr[0, pl.ds(r0, BLK), :] = (dq * dq_fac).astype(BF16)
            out_scr[1, pl.ds(r0, BLK), :] = (dfull * (1.0 - lb_v) * sig * (1.0 - sig)).astype(BF16)
            out_scr[2, pl.ds(r0, BLK), :] = dv.astype(BF16)
            dlb = dlb + jnp.sum(dfull * (1.0 - sig), axis=0, keepdims=True)
            return dst_new, dg[0:1, :], dlb

        zero_row = jnp.zeros((1, BLK), F32)
        _, _, dlb = lax.fori_loop(0, nc, bwd_chunk,
                                  (jnp.zeros((HG_HEAD_DIM, HG_HEAD_DIM), F32), zero_row, zero_row))
        dlb_ref[...] = jnp.broadcast_to(dlb, dlb_ref.shape)
        head = pl.program_id(0)
        copies = []
        for t in range(3):
            col = pl.multiple_of((base + t * nh + head) * BLK, BLK)
            copies.append(pltpu.make_async_copy(out_scr.at[t], dproj_ref.at[:, pl.ds(col, BLK)], sems.at[t]))
            copies[-1].start()
        for cp in copies:
            cp.wait()

    col_spec = lambda off: pl.BlockSpec((s_len, BLK), lambda h: (0, off + h))
    anyspec = pl.BlockSpec(memory_space=pl.ANY)
    return _pcall(
        body, name=name,
        out_shape=(jax.ShapeDtypeStruct(dproj.shape, dproj.dtype), jax.ShapeDtypeStruct((8, WIDTH), F32)),
        grid=(nh,),
        in_specs=[col_spec(base), col_spec(base + nh), col_spec(base + 2 * nh), col_spec(0),
                  pl.BlockSpec((1, BLK), lambda h: (0, h)), anyspec],
        out_specs=(anyspec, pl.BlockSpec((8, BLK), lambda h: (0, h))),
        scratch_shapes=[pltpu.VMEM((len(HG_LEVELS), BLK, BLK), F32),
                        pltpu.VMEM((nc, HG_HEAD_DIM, HG_HEAD_DIM), F32),
                        pltpu.VMEM((3, s_len, BLK), BF16), pltpu.SemaphoreType.DMA((3,))],
        aliases={5: 0},
        semantics=("arbitrary",))(proj, proj, proj, do_b, lb, dproj)


def _dh_matmul(dproj, w_full, after, name):
    s_len, n = dproj.shape
    d = w_full.shape[0]
    tm = min(512, s_len)
    tk = 1536

    def body(dp_ref, w_ref, after_ref, dh_ref):
        del after_ref
        part = _dot_nt(dp_ref[...], w_ref[...])

        @pl.when(pl.program_id(1) == 0)
        def _():
            dh_ref[...] = part

        @pl.when(pl.program_id(1) > 0)
        def _():
            dh_ref[...] += part

    return _pcall(
        body, name=name, out_shape=jax.ShapeDtypeStruct((s_len, d), F32),
        grid=(s_len // tm, n // tk),
        in_specs=[pl.BlockSpec((tm, tk), lambda i, k: (i, k)), pl.BlockSpec((d, tk), lambda i, k: (0, k)),
                  pl.BlockSpec(memory_space=pl.ANY)],
        out_specs=pl.BlockSpec((tm, d), lambda i, k: (i, 0)),
        semantics=("arbitrary", "arbitrary"))(dproj, w_full, after)


def _gw_matmul(h_t, dproj, name):
    d, s_len = h_t.shape
    n = dproj.shape[1]
    tn = 1152

    def body(ht_ref, dp_ref, gw_ref):
        gw_ref[...] = _dot(ht_ref[...], dp_ref[...]).astype(BF16)

    return _pcall(
        body, name=name, out_shape=jax.ShapeDtypeStruct((d, n), BF16),
        grid=(n // tn,),
        in_specs=[pl.BlockSpec((d, s_len), lambda j: (0, 0)), pl.BlockSpec((s_len, tn), lambda j: (0, j))],
        out_specs=pl.BlockSpec((d, tn), lambda j: (0, j)),
        semantics=("arbitrary",))(h_t, dproj)


def _ln_bwd(dh, x, scale, dres, name):
    s_len, d = x.shape
    tm = min(512, s_len)

    def body(dh_ref, x_ref, sc_ref, dres_ref, dx_ref, vec_ref):
        @pl.when(pl.program_id(0) == 0)
        def _():
            vec_ref[...] = jnp.zeros_like(vec_ref)

        dh = dh_ref[...]
        xs, rstd = _standardize(x_ref[...])
        vec_ref[0:1, :] += jnp.sum(dh, axis=0, keepdims=True)
        vec_ref[1:2, :] += jnp.sum(dh * xs, axis=0, keepdims=True)
        dx_ref[...] = _standardize_bwd(xs, rstd, dh * (1.0 + sc_ref[...])) + dres_ref[...]

    tile = pl.BlockSpec((tm, d), lambda i: (i, 0))
    return _pcall(body, name=name, grid=(s_len // tm,),
                  out_shape=(jax.ShapeDtypeStruct((s_len, d), F32), jax.ShapeDtypeStruct((8, d), F32)),
                  in_specs=[tile, tile, pl.BlockSpec((1, d), lambda i: (0, 0)), tile],
                  out_specs=(tile, pl.BlockSpec((8, d), lambda i: (0, 0))),
                  semantics=("arbitrary",))(dh, x, scale, dres)


def _wmod_grad(c_t, dmod):
    d = c_t.shape[0]
    n_layers, _, cm = dmod.shape

    def body(c_ref, dm_ref, o_ref):
        for l in range(n_layers):
            acc = None
            for b in range(NDEV):
                term = c_ref[:, b:b + 1] * dm_ref[l, b:b + 1, :]
                acc = term if acc is None else acc + term
            o_ref[l] = acc

    return _pcall(body, name="wmod_grad", out_shape=jax.ShapeDtypeStruct((n_layers, d, cm), F32))(c_t, dmod)


def _sum_adamw(parts_list, w, m, v, name):
    n_ranges = len(parts_list)
    n_src, range_rows, cols = parts_list[0].shape
    rows = range_rows * n_ranges
    tr = range_rows
    for cand in (512, 256, 128, 64, 32, 16, 8):
        if range_rows % cand == 0 and cand * cols * 4 <= (2 << 20):
            tr = cand
            break
    tiles = range_rows // tr

    def body(*refs):
        p_refs = refs[:n_ranges]
        w_ref, m_ref, v_ref, g_ref, d_ref, nm_ref, nv_ref = refs[n_ranges:]

        def step(p_ref):
            g = p_ref[0].astype(F32)
            for s in range(1, n_src):
                g = g + p_ref[s].astype(F32)
            nm = ADAM_B1 * m_ref[...] + (1.0 - ADAM_B1) * g
            nv = ADAM_B2 * v_ref[...] + (1.0 - ADAM_B2) * (g * g)
            m_hat = nm / (1.0 - ADAM_B1 ** ADAM_STEP)
            v_hat = nv / (1.0 - ADAM_B2 ** ADAM_STEP)
            g_ref[...] = g
            d_ref[...] = -ADAM_LR * (m_hat / (jnp.sqrt(v_hat) + ADAM_EPS) + ADAM_WD * w_ref[...])
            nm_ref[...] = nm
            nv_ref[...] = nv

        if n_ranges == 1:
            step(p_refs[0])
        else:
            for j in range(n_ranges):
                @pl.when(pl.program_id(0) // tiles == j)
                def _(j=j):
                    step(p_refs[j])

    def part_spec(j):
        return pl.BlockSpec((n_src, tr, cols), lambda i: (0, jnp.clip(i - j * tiles, 0, tiles - 1), 0))

    tile = pl.BlockSpec((tr, cols), lambda i: (i, 0))
    out = jax.ShapeDtypeStruct((rows, cols), F32)
    return _pcall(body, name=name, grid=(rows // tr,), out_shape=(out,) * 4,
                  in_specs=[part_spec(j) for j in range(n_ranges)] + [tile, tile, tile],
                  out_specs=(tile,) * 4, semantics=("arbitrary",))(*parts_list, w, m, v)


def _sum_parts(parts, name):
    n_src = parts.shape[0]

    def body(p_ref, o_ref):
        acc = p_ref[0]
        for s in range(1, n_src):
            acc = acc + p_ref[s]
        o_ref[...] = acc

    return _pcall(body, name=name, out_shape=jax.ShapeDtypeStruct(parts.shape[1:], F32))(parts)


def _pair_sum(gw, stage, me, name):
    d = gw.shape[0]
    n_slots, _, shard = stage.shape

    def body(me_ref, g_ref, s_ref, own_ref, o_ref):
        del me_ref
        total = (g_ref[...].astype(F32) + s_ref[0].astype(F32)).astype(BF16)
        o_ref[0] = total

        @pl.when(pl.program_id(0) == 0)
        def _():
            own_ref[0] = total

    slot = pl.BlockSpec((1, d, shard), lambda jj, me_ref: (jj, 0, 0))
    out = jax.ShapeDtypeStruct(stage.shape, BF16)
    return pl.pallas_call(
        body, name=name, out_shape=(out, out),
        grid_spec=pltpu.PrefetchScalarGridSpec(
            num_scalar_prefetch=1, grid=(n_slots,),
            in_specs=[pl.BlockSpec((d, shard), lambda jj, me_ref: (0, me_ref[0] ^ (2 * jj))), slot],
            out_specs=(pl.BlockSpec((1, d, shard), lambda jj, me_ref: (0, 0, 0)), slot)),
        compiler_params=pltpu.CompilerParams(dimension_semantics=("arbitrary",), vmem_limit_bytes=VMEM_LIMIT),
        interpret=False)(me.reshape(1).astype(jnp.int32), gw, stage)


def _lower_bound_table(lower_bounds):
    p = jax.nn.softmax(lower_bounds.astype(F32), axis=0)
    return jnp.cumsum(p, axis=0) - p[0:1]


def _pad_rows(v, width):
    n = v.shape[0]
    rows = -(-n // width)
    rows = -(-rows // 8) * 8
    return jnp.pad(v, (0, rows * width - n)).reshape(rows, width)


def kernel(x, c, w_mod, b_mod, w_in, conv_w, hgrn_norm_w, lower_bounds, w_branch, w_out, ln_g, ln_b, loss_target, m_w_mod, m_b_mod, m_w_in, m_conv_w, m_hgrn_norm_w, m_lower_bounds, m_w_branch, m_w_out, m_ln_g, m_ln_b, v_w_mod, v_b_mod, v_w_in, v_conv_w, v_hgrn_norm_w, v_lower_bounds, v_w_branch, v_w_out, v_ln_g, v_ln_b):
    n_layers = N_LAYERS
    s_len, d = x.shape[1], x.shape[2]
    n_cols = w_in.shape[2] * NDEV
    cw_cols = conv_w.shape[2]
    cm = w_mod.shape[2]
    me = _my_index()
    x0 = x[0]
    target = loss_target[0]

    small = _pad_rows(jnp.concatenate([c.reshape(-1), conv_w.reshape(-1)]), BLK)
    small_all = _all_gather_small("gather_c_conv", small).reshape(NDEV, -1)
    c_all = small_all[:, :d]
    conv_full = small_all[:, d:d + n_layers * 3 * cw_cols].reshape(NDEV, n_layers, 3, cw_cols)
    conv_full = conv_full.transpose(1, 2, 0, 3).reshape(n_layers, 3, WIDTH)

    b_mod_mine = lax.dynamic_slice_in_dim(b_mod, me * cm, cm, axis=1).reshape(n_layers, 1, cm)
    mod_cols = _mod_fwd(c_all, w_mod, b_mod_mine)
    mod_all = _all_gather_small("gather_mod", mod_cols.reshape(n_layers * NDEV, cm))
    mod_all = mod_all.reshape(NDEV, n_layers, NDEV, cm)
    mod_mine = lax.dynamic_index_in_dim(mod_all, me, axis=2, keepdims=False)
    mod_mine = mod_mine.transpose(1, 0, 2).reshape(n_layers, 3, 1, d)

    shard = w_in.shape[2]
    dsh = d // NDEV
    w_in_b, w_branch_b, w_out_b = w_in.astype(BF16), w_branch.astype(BF16), w_out.astype(BF16)
    window = lambda ref, dev: ref.at[:, pl.ds(pl.multiple_of(dev * shard, BLK), shard)]

    def two_step_sends(places):
        chips, sibling = [], []
        for k in (1, 2, 4, 6):
            for a, place in enumerate(places):
                chips.append((k, lambda ins, lands, me, a=a: ins[a],
                              lambda lands, me, a=a, place=place: place(lands[a], me),
                              lambda lands, me, a=a, k=k, place=place: place(lands[a], me ^ k)))
        for j in (2, 4, 6):
            for a, place in enumerate(places):
                sibling.append((1, lambda ins, lands, me, a=a, j=j, place=place: place(lands[a], me ^ j),
                                lambda lands, me, a=a, j=j, place=place: place(lands[a], me ^ j),
                                lambda lands, me, a=a, j=j, place=place: place(lands[a], me ^ 1 ^ j)))
        return chips, sibling

    in_sends = two_step_sends([window])
    rest_sends = two_step_sends([_slot, _slot])
    layer_sends = two_step_sends([window, _slot, _slot])

    def in_land(l):
        return _place_own_window(f"place_w_in_{l}", (d, n_cols), w_in_b[l], me)

    def rest_lands(l):
        return [_place_own((NDEV, 3, WIDTH, dsh), BF16, w_branch_b[l][None], (me, 0, 0, 0)),
                _place_own((NDEV, dsh, d), BF16, w_out_b[l][None], (me, 0, 0))]

    def gather_start(name, shards, lands, sends, after):
        return _exchange_start(f"{name}_chips_start", shards, lands, sends[0], after)

    def gather_pass_on(name, started, after, sends):
        _, lands = _exchange_wait(f"{name}_chips_wait", started, after, sends[0])
        return _exchange_start(f"{name}_sibling_start", [], lands, sends[1])

    def gather_finish(name, started, after, sends):
        return _exchange_wait(f"{name}_sibling_wait", started, after, sends[1])[1]

    def branch_out_weights(w_branch_l, w_out_l):
        return w_branch_l.transpose(1, 2, 0, 3).reshape(3, WIDTH, d), w_out_l.reshape(d, d)

    gathering = gather_start("gather_w_in_0", [w_in_b[0]], [in_land(0)], in_sends, mod_mine)
    passing = gather_pass_on("gather_w_in_0", gathering, gathering[4], in_sends)
    rest_gathering = gather_start("gather_rest_0", [w_branch_b[0], w_out_b[0]], rest_lands(0), rest_sends, passing[4])
    next_gathering = None
    if n_layers > 1:
        next_gathering = gather_start("gather_weights_1", [w_in_b[1], w_branch_b[1], w_out_b[1]],
                                      [in_land(1)] + rest_lands(1), layer_sends, rest_gathering[4])
    w_in_l = gather_finish("gather_w_in_0", passing, (next_gathering or rest_gathering)[4], in_sends)[0]

    lbs = _lower_bound_table(lower_bounds)
    norm_w4 = jnp.tile(hgrn_norm_w, (1, WIDTH // HG_HEAD_DIM))

    saved = []
    xl = x0
    for l in range(n_layers):
        shift, scale, gate = mod_mine[l, 0], mod_mine[l, 1], mod_mine[l, 2]
        proj, h_t = _ln_proj(xl, shift, scale, w_in_l, f"ln_proj_{l}")
        o_a, totals = _sb_fwd(proj, f"sb_fwd_{l}")
        if l == 0:
            rest_passing = gather_pass_on("gather_rest_0", rest_gathering, o_a, rest_sends)
        o_b = _hgrn_fwd(proj, lbs[l:l + 1], f"hgrn_fwd_{l}")
        if l == 0:
            wb_l, wo_l = branch_out_weights(*gather_finish("gather_rest_0", rest_passing, o_b, rest_sends))
            if n_layers > 1:
                next_passing = gather_pass_on("gather_weights_1", next_gathering, o_b, layer_sends)
                gate = gate + next_passing[4][0, 0]
        x_new, merged, ycat = _merge_fwd(xl, proj, o_a, o_b, gate, norm_w4[l:l + 1], conv_full[l],
                                         wb_l, wo_l, ln_g[l:l + 1], ln_b[l:l + 1], f"merge_fwd_{l}")
        saved.append((xl, proj, h_t, o_a, totals, o_b, merged, ycat, w_in_l, wb_l, wo_l))
        if l == 0 and n_layers > 1:
            w_in_l, w_branch_l, w_out_l = gather_finish("gather_weights_1", next_passing, x_new, layer_sends)
            wb_l, wo_l = branch_out_weights(w_branch_l, w_out_l)
        xl = x_new

    loss_part, dx = _loss_fwd_bwd(xl, target)
    loss = lax.psum(loss_part[0, 0], ("x", "y", "c"))

    pair_sends = [(1, lambda ins, lands, me, j=j: window(ins[0], me ^ 1 ^ j),
                   lambda lands, me, jj=jj: lands[0].at[jj], lambda lands, me, jj=jj: lands[0].at[jj])
                  for jj, j in enumerate((0, 2, 4, 6))]
    chip_sum_sends = [(j, lambda ins, lands, me, jj=jj: ins[0].at[jj],
                       lambda lands, me, jj=jj: lands[0].at[jj], lambda lands, me, jj=jj: lands[0].at[jj])
                      for jj, j in ((1, 2), (2, 4), (3, 6))]
    rest_scatter = _direct_sends([(0, 0, _slot, _slot), (1, 1, _slot, _slot)])
    scattering = [None] * n_layers
    small_grads = [None] * n_layers
    dmod = [None] * n_layers
    tie = None
    for l in reversed(range(n_layers)):
        xl, proj, h_t, o_a, totals, o_b, merged, ycat, w_in_l, wb_l, wo_l = saved[l]
        scale, gate = mod_mine[l, 1], mod_mine[l, 2]
        if tie is not None:
            gate = gate + tie[0, 0]
        dres, dycat, dproj, gwo, gwb, mvec = _merge_bwd(dx, xl, merged, ycat, proj, gate, wb_l, wo_l,
                                                        ln_g[l:l + 1], f"merge_bwd_{l}")
        gwb_by_owner = gwb.astype(BF16).reshape(3, WIDTH, NDEV, dsh).transpose(2, 0, 1, 3)
        gwo_by_owner = gwo.astype(BF16).reshape(NDEV, dsh, d)
        lands = [_place_own((NDEV, 3, WIDTH, dsh), BF16, lax.dynamic_slice_in_dim(gwb_by_owner, me, 1, axis=0),
                            (me, 0, 0, 0)),
                 _place_own((NDEV, dsh, d), BF16, lax.dynamic_slice_in_dim(gwo_by_owner, me, 1, axis=0),
                            (me, 0, 0))]
        rest_started = _exchange_start(f"scatter_rest_{l}_start", [gwb_by_owner, gwo_by_owner], lands, rest_scatter)
        dproj, do_a, do_b, bvec = _branch_bwd(dycat, proj, o_a, o_b, norm_w4[l:l + 1] + rest_started[4][0, 0],
                                              conv_full[l], dproj, f"branch_bwd_{l}")
        dproj = _sb_bwd(proj, do_a, totals, dproj, f"sb_bwd_{l}")
        dproj, dlb = _hgrn_bwd(proj, do_b, lbs[l:l + 1], dproj, f"hgrn_bwd_{l}")
        gwi = _gw_matmul(h_t, dproj, f"gw_matmul_{l}")
        swapping = _exchange_start(f"scatter_in_{l}_sibling_start", [gwi], [lax.empty((4, d, shard), BF16)], pair_sends)
        (gwi,), (stage,) = _exchange_wait(f"scatter_in_{l}_sibling_wait", swapping, swapping[4], pair_sends)
        land, chip_sums = _pair_sum(gwi, stage, me, f"pair_sum_{l}")
        in_started = _exchange_start(f"scatter_in_{l}_chips_start", [chip_sums], [land], chip_sum_sends)
        scattering[l] = (in_started, rest_started)
        tie = in_started[4]
        dh = _dh_matmul(dproj, w_in_l, tie, f"dh_matmul_{l}")
        dx, lvec = _ln_bwd(dh, xl, scale + tie[0, 0], dres, f"ln_bwd_{l}")
        dmod[l] = jnp.concatenate([lvec[0], lvec[1], mvec[2]])
        norm_grad = bvec[0].reshape(WIDTH // HG_HEAD_DIM, HG_HEAD_DIM).sum(axis=0)
        small_grads[l] = jnp.concatenate([mvec[0], mvec[1], norm_grad, dlb[0], bvec[1:4].reshape(-1)])
    grad_x = dx[None]

    small_vec = jnp.concatenate(dmod + small_grads)
    n_small = small_vec.shape[0]
    small_all = _all_gather_small("gather_small_grads", _pad_rows(small_vec, BLK))
    small_sum = _sum_parts(small_all, "sum_small_grads").reshape(-1)[:n_small]
    dmod_all = small_all.reshape(NDEV, -1)[:, :n_layers * 3 * d].reshape(NDEV, n_layers, 3 * d)

    off = n_layers * 3 * d
    grad_b_mod = small_sum[:off].reshape(n_layers, 3 * d)
    per_layer = 2 * d + HG_HEAD_DIM + WIDTH + 3 * WIDTH
    g_ln_g, g_ln_b, g_norm, g_lbs, g_conv = [], [], [], [], []
    for l in range(n_layers):
        seg = small_sum[off + l * per_layer: off + (l + 1) * per_layer]
        g_ln_g.append(seg[:d])
        g_ln_b.append(seg[d:2 * d])
        g_norm.append(seg[2 * d:2 * d + HG_HEAD_DIM])
        g_lbs.append(seg[2 * d + HG_HEAD_DIM:2 * d + HG_HEAD_DIM + WIDTH])
        g_conv.append(seg[2 * d + HG_HEAD_DIM + WIDTH:].reshape(3, WIDTH))
    grad_ln_g, grad_ln_b = jnp.stack(g_ln_g), jnp.stack(g_ln_b)
    grad_norm = jnp.stack(g_norm)
    _, lbs_vjp = jax.vjp(_lower_bound_table, lower_bounds)
    grad_lower = lbs_vjp(jnp.stack(g_lbs))[0]
    grad_conv = lax.dynamic_slice_in_dim(jnp.stack(g_conv), me * cw_cols, cw_cols, axis=2)

    dmod_mine = lax.dynamic_slice_in_dim(dmod_all, me * cm, cm, axis=2).transpose(1, 0, 2)
    grad_w_mod = _wmod_grad(c_all.T, dmod_mine)

    p_in, p_branch, p_out = [None] * n_layers, [None] * n_layers, [None] * n_layers
    for l in reversed(range(n_layers)):
        in_started, rest_started = scattering[l]
        p_branch_l, p_out[l] = _exchange_wait(f"scatter_rest_{l}_wait", rest_started, grad_w_mod, rest_scatter)[1]
        p_branch[l] = p_branch_l.reshape(NDEV, 3 * WIDTH, dsh)
        p_in[l] = _exchange_wait(f"scatter_in_{l}_chips_wait", in_started, grad_w_mod, chip_sum_sends)[1][0]

    def adam(parts_list, w, m, v, name):
        shape = w.shape
        cols = shape[-1]
        flat = lambda a: a.reshape(-1, cols)
        outs = _sum_adamw(parts_list, flat(w), flat(m), flat(v), name)
        return [o.reshape(shape) for o in outs]

    r_w_in = adam(p_in, w_in, m_w_in, v_w_in, "adamw_w_in")
    r_w_branch = adam(p_branch, w_branch, m_w_branch, v_w_branch, "adamw_w_branch")
    r_w_out = adam(p_out, w_out, m_w_out, v_w_out, "adamw_w_out")
    r_w_mod = adam([grad_w_mod.reshape(1, -1, cm)], w_mod, m_w_mod, v_w_mod, "adamw_w_mod")

    small_names = ["b_mod", "conv_w", "hgrn_norm_w", "lower_bounds", "ln_g", "ln_b"]
    small_g = [grad_b_mod, grad_conv, grad_norm, grad_lower, grad_ln_g, grad_ln_b]
    small_w = [b_mod, conv_w, hgrn_norm_w, lower_bounds, ln_g, ln_b]
    small_m = [m_b_mod, m_conv_w, m_hgrn_norm_w, m_lower_bounds, m_ln_g, m_ln_b]
    small_v = [v_b_mod, v_conv_w, v_hgrn_norm_w, v_lower_bounds, v_ln_g, v_ln_b]
    pack = lambda arrs: _pad_rows(jnp.concatenate([a.reshape(-1) for a in arrs]), BLK)
    packed = _sum_adamw([pack(small_g)[None]], pack(small_w), pack(small_m), pack(small_v), "adamw_small")
    r_small = {n: [] for n in small_names}
    for res in packed:
        flat = res.reshape(-1)
        pos = 0
        for n, w in zip(small_names, small_w):
            r_small[n].append(flat[pos:pos + w.size].reshape(w.shape))
            pos += w.size

    results = {"w_mod": r_w_mod, "w_in": r_w_in, "w_branch": r_w_branch, "w_out": r_w_out, **r_small}
    order = ["w_mod", "b_mod", "w_in", "conv_w", "hgrn_norm_w", "lower_bounds", "w_branch", "w_out", "ln_g", "ln_b"]
    outs = [loss, grad_x]
    for idx in range(4):
        outs.extend(results[n][idx] for n in order)
    return tuple(outs)
```

```python
import jax
import jax.numpy as jnp
from jax import lax
from jax.experimental import pallas as pl
from jax.experimental.pallas import tpu as pltpu

F32 = jnp.float32
BF16 = jnp.bfloat16
NDEV = 8
N_LAYERS = 2
SB_HEAD_DIM = 64
HG_HEAD_DIM = 128
WIDTH = 512
BLK = 128
LN_EPS = 1e-5
RMS_EPS = 1e-6
ALPHA = (2.0 * N_LAYERS) ** 0.25
ADAM_LR, ADAM_B1, ADAM_B2, ADAM_EPS, ADAM_WD, ADAM_STEP = 0.001, 0.9, 0.999, 1e-08, 0.01, 10
VMEM_LIMIT = 56 * 1024 * 1024
MESH = pl.DeviceIdType.MESH
HG_LEVELS = (64, 32, 16, 8, 4, 2, 1)


def _pcall(body, *, name, out_shape, grid=None, in_specs=None, out_specs=None, scratch_shapes=(),
           semantics=None, aliases=None):
    kwargs = {}
    if grid is not None:
        kwargs["grid"] = grid
    if in_specs is not None:
        kwargs["in_specs"] = in_specs
    if out_specs is not None:
        kwargs["out_specs"] = out_specs
    if aliases:
        kwargs["input_output_aliases"] = aliases
    return pl.pallas_call(
        body, name=name, out_shape=out_shape, scratch_shapes=list(scratch_shapes),
        compiler_params=pltpu.CompilerParams(dimension_semantics=semantics, vmem_limit_bytes=VMEM_LIMIT),
        interpret=False, **kwargs)


def _dot(a, b):
    return jnp.dot(a, b, preferred_element_type=F32)


def _dot_nt(a, b):
    return lax.dot_general(a, b, (((1,), (1,)), ((), ())), preferred_element_type=F32)


def _dot_tn(a, b):
    return lax.dot_general(a, b, (((0,), (0,)), ((), ())), preferred_element_type=F32)


def _split3(x):
    x1 = x.astype(BF16)
    r1 = x - x1.astype(F32)
    x2 = r1.astype(BF16)
    r2 = r1 - x2.astype(F32)
    return x1, x2, r2.astype(BF16)


def _split2(x):
    x1 = x.astype(BF16)
    return x1, (x - x1.astype(F32)).astype(BF16)


def _dot_exact_l(m_bf16, x):
    x1, x2, x3 = _split3(x)
    return _dot(m_bf16, x1) + _dot(m_bf16, x2) + _dot(m_bf16, x3)


def _sigmoid(x):
    return 1.0 / (1.0 + jnp.exp(-x))


def _silu_and_grad(x):
    s = _sigmoid(x)
    return x * s, s * (1.0 + x * (1.0 - s))


LOG2E = 1.4426950408889634
MASKED_SCORE = -1e30


def _softplus2_parts(z2):
    minus_abs = lax.bitcast_convert_type(lax.bitcast_convert_type(z2, jnp.int32) | jnp.int32(-2 ** 31), F32)
    e = jnp.exp2(minus_abs)
    sp2 = jnp.maximum(z2, 0.0) + jnp.log2(1.0 + e)
    r = 1.0 / (1.0 + e)
    return sp2, jnp.where(z2 >= 0.0, r, e * r)


def _split2_lanes(x):
    x1 = x.astype(BF16)
    return jnp.concatenate([x1, (x - x1.astype(F32)).astype(BF16)], axis=1)


def _iota2(shape, dim):
    return lax.broadcasted_iota(jnp.int32, shape, dim)


def _standardize(x):
    mu = jnp.mean(x, axis=-1, keepdims=True)
    xc = x - mu
    var = jnp.mean(xc * xc, axis=-1, keepdims=True)
    rstd = lax.rsqrt(var + LN_EPS)
    return xc * rstd, rstd


def _standardize_bwd(xhat, rstd, dxhat):
    m1 = jnp.mean(dxhat, axis=-1, keepdims=True)
    m2 = jnp.mean(dxhat * xhat, axis=-1, keepdims=True)
    return rstd * (dxhat - m1 - xhat * m2)


def _my_index():
    return 4 * lax.axis_index("x") + 2 * lax.axis_index("y") + lax.axis_index("c")


def _exchange(name, ins, out_shapes, transfers, in_vmem):
    n_in, n_out, n_t = len(ins), len(out_shapes), len(transfers)

    def body(*refs):
        in_refs, out_refs = refs[:n_in], refs[n_in:n_in + n_out]
        send_sems, recv_sems, local_sems = refs[n_in + n_out:]
        x, y, c = lax.axis_index("x"), lax.axis_index("y"), lax.axis_index("c")
        me = 4 * x + 2 * y + c
        started = []
        for t, (i, o, src_fn, dst_fn) in enumerate(transfers):
            own = pltpu.make_async_copy(src_fn(in_refs[i], me), dst_fn(out_refs[o], me), local_sems.at[t])
            own.start()
            started.append(own)
        arrivals = []
        for k in range(1, NDEV):
            px = x ^ ((k >> 2) & 1)
            py = y ^ ((k >> 1) & 1)
            pc = c ^ (k & 1)
            peer = 4 * px + 2 * py + pc
            for t, (i, o, src_fn, dst_fn) in enumerate(transfers):
                sem = t * (NDEV - 1) + k - 1
                push = pltpu.make_async_remote_copy(
                    src_ref=src_fn(in_refs[i], peer), dst_ref=dst_fn(out_refs[o], me),
                    send_sem=send_sems.at[sem], recv_sem=recv_sems.at[sem],
                    device_id=(px, py, pc), device_id_type=MESH)
                push.start()
                started.append(push)
                arrivals.append(pltpu.make_async_remote_copy(
                    src_ref=src_fn(in_refs[i], peer), dst_ref=dst_fn(out_refs[o], peer),
                    send_sem=send_sems.at[sem], recv_sem=recv_sems.at[sem],
                    device_id=(px, py, pc), device_id_type=MESH))
        for arrival in arrivals:
            arrival.wait_recv()
        for cp in started[n_t:]:
            cp.wait_send()
        for own in started[:n_t]:
            own.wait()

    space = pltpu.VMEM if in_vmem else pl.ANY
    spec = pl.BlockSpec(memory_space=space)
    return _pcall(
        body, name=name, out_shape=out_shapes,
        in_specs=[spec] * n_in, out_specs=[spec] * n_out,
        scratch_shapes=[pltpu.SemaphoreType.DMA((n_t * (NDEV - 1),)),
                        pltpu.SemaphoreType.DMA((n_t * (NDEV - 1),)),
                        pltpu.SemaphoreType.DMA((n_t,))])(*ins)


def _whole(ref, dev):
    return ref


def _slot(ref, dev):
    return ref.at[dev]


def _all_gather_small(name, v):
    out = _exchange(name, [v], [jax.ShapeDtypeStruct((NDEV,) + v.shape, v.dtype)],
                    [(0, 0, _whole, _slot)], in_vmem=True)
    return out[0]


_HBM_SPEC = pl.BlockSpec(memory_space=pltpu.HBM)
_SEM_SPEC = pl.BlockSpec(memory_space=pltpu.SEMAPHORE)
_DATAFLOW = pltpu.SideEffectType.DATAFLOW_SIDE_EFFECTING


def _peer(x, y, c, k):
    px = x ^ ((k >> 2) & 1)
    py = y ^ ((k >> 1) & 1)
    pc = c ^ (k & 1)
    return (px, py, pc), 4 * px + 2 * py + pc


def _direct_sends(transfers):
    sends = []
    for k in range(1, NDEV):
        for i, o, src_fn, dst_fn in transfers:
            sends.append((k,
                          lambda ins, lands, me, i=i, k=k, src_fn=src_fn: src_fn(ins[i], me ^ k),
                          lambda lands, me, o=o, dst_fn=dst_fn: dst_fn(lands[o], me),
                          lambda lands, me, o=o, k=k, dst_fn=dst_fn: dst_fn(lands[o], me ^ k)))
    return sends


def _exchange_start(name, ins, lands, sends, after=None):
    n_in, n_buf = len(ins), len(ins) + len(lands)
    n_sem = len(sends)

    def body(*refs):
        in_refs, land_refs = refs[:n_in], refs[n_in:n_buf]
        n_skip = n_buf + (0 if after is None else 1)
        send_sems, recv_sems, token = refs[n_skip], refs[n_skip + 1], refs[-1]
        x, y, c = lax.axis_index("x"), lax.axis_index("y"), lax.axis_index("c")
        me = 4 * x + 2 * y + c
        for t, (k, src_fn, dst_fn, _) in enumerate(sends):
            pltpu.make_async_remote_copy(
                src_ref=src_fn(in_refs, land_refs, me), dst_ref=dst_fn(land_refs, me),
                send_sem=send_sems.at[t], recv_sem=recv_sems.at[t],
                device_id=_peer(x, y, c, k)[0], device_id_type=MESH).start()
        token[...] = jnp.zeros_like(token)

    bufs = [pltpu.with_memory_space_constraint(a, pltpu.HBM) for a in list(ins) + list(lands)]
    extra = [] if after is None else [after]
    outs = pl.pallas_call(
        body, name=name,
        out_shape=(pltpu.SemaphoreType.DMA((n_sem,)), pltpu.SemaphoreType.DMA((n_sem,)))
        + tuple(pltpu.HBM(a.shape, a.dtype) for a in bufs) + (jax.ShapeDtypeStruct((8, BLK), F32),),
        in_specs=[_HBM_SPEC] * n_buf + [pl.BlockSpec(memory_space=pl.ANY)] * len(extra),
        out_specs=(_SEM_SPEC, _SEM_SPEC) + (_HBM_SPEC,) * n_buf + (pl.BlockSpec(memory_space=pltpu.VMEM),),
        input_output_aliases={b: 2 + b for b in range(n_buf)},
        compiler_params=pltpu.CompilerParams(has_side_effects=_DATAFLOW),
        interpret=False)(*bufs, *extra)
    return outs[0], outs[1], list(outs[2:2 + n_in]), list(outs[2 + n_in:2 + n_buf]), outs[-1]


def _exchange_wait(name, started, after, sends):
    send_sems, recv_sems, ins, lands, _ = started
    n_in, n_buf = len(ins), len(ins) + len(lands)

    def body(*refs):
        in_refs, land_refs = refs[:n_in], refs[n_in:n_buf]
        send_sems, recv_sems = refs[n_buf], refs[n_buf + 1]
        x, y, c = lax.axis_index("x"), lax.axis_index("y"), lax.axis_index("c")
        me = 4 * x + 2 * y + c
        for t, (k, src_fn, _, rcv_fn) in enumerate(sends):
            cp = pltpu.make_async_remote_copy(
                src_ref=src_fn(in_refs, land_refs, me), dst_ref=rcv_fn(land_refs, me),
                send_sem=send_sems.at[t], recv_sem=recv_sems.at[t],
                device_id=_peer(x, y, c, k)[0], device_id_type=MESH)
            cp.wait_send()
            cp.wait_recv()

    bufs = list(ins) + list(lands)
    outs = pl.pallas_call(
        body, name=name, out_shape=tuple(pltpu.HBM(a.shape, a.dtype) for a in bufs),
        in_specs=[_HBM_SPEC] * n_buf + [_SEM_SPEC, _SEM_SPEC, pl.BlockSpec(memory_space=pl.ANY)],
        out_specs=(_HBM_SPEC,) * n_buf,
        input_output_aliases={b: b for b in range(n_buf)},
        compiler_params=pltpu.CompilerParams(has_side_effects=_DATAFLOW),
        interpret=False)(*bufs, send_sems, recv_sems, after)
    return list(outs[:n_in]), list(outs[n_in:])


def _place_own(shape, dtype, own, start):
    return lax.dynamic_update_slice(lax.empty(shape, dtype), own, start)


def _place_own_window(name, shape, own, me):
    rows, cols = own.shape

    def body(me_ref, zone_in, own_ref, zone_ref):
        del me_ref, zone_in
        zone_ref[...] = own_ref[...]

    return pl.pallas_call(
        body, name=name, out_shape=jax.ShapeDtypeStruct(shape, own.dtype),
        grid_spec=pltpu.PrefetchScalarGridSpec(
            num_scalar_prefetch=1, grid=(1,),
            in_specs=[pl.BlockSpec(memory_space=pl.ANY), pl.BlockSpec((rows, cols), lambda i, me_ref: (0, 0))],
            out_specs=pl.BlockSpec((rows, cols), lambda i, me_ref: (0, me_ref[0]))),
        input_output_aliases={1: 0},
        compiler_params=pltpu.CompilerParams(dimension_semantics=("arbitrary",), vmem_limit_bytes=VMEM_LIMIT),
        interpret=False)(me.reshape(1).astype(jnp.int32), lax.empty(shape, own.dtype), own)


def _mod_fwd(c_all, w_mod, b_mod_mine):
    n_layers, _, cm = w_mod.shape

    def body(c_ref, w_ref, b_ref, o_ref):
        for l in range(n_layers):
            o_ref[l] = jnp.dot(c_ref[...], w_ref[l], preferred_element_type=F32,
                               precision=lax.Precision.HIGHEST) + b_ref[l]

    return _pcall(body, name="mod_fwd", out_shape=jax.ShapeDtypeStruct((n_layers, NDEV, cm), F32))(
        c_all, w_mod, b_mod_mine)


def _ln_proj(x, shift, scale, w_full, name):
    s_len, d = x.shape
    n = w_full.shape[1]
    tm = min(512, s_len)
    tn = 1024

    def body(x_ref, sh_ref, sc_ref, w_ref, proj_ref, ht_ref, h_scr):
        @pl.when(pl.program_id(1) == 0)
        def _():
            xs, _ = _standardize(x_ref[...])
            h = xs * (1.0 + sc_ref[...]) + sh_ref[...]
            h_scr[...] = h.astype(BF16)
            ht_ref[...] = h.T.astype(BF16)

        proj_ref[...] = _dot(h_scr[...], w_ref[...])

    return _pcall(
        body, name=name,
        out_shape=(jax.ShapeDtypeStruct((s_len, n), F32), jax.ShapeDtypeStruct((d, s_len), BF16)),
        grid=(s_len // tm, n // tn),
        in_specs=[pl.BlockSpec((tm, d), lambda i, j: (i, 0)),
                  pl.BlockSpec((1, d), lambda i, j: (0, 0)),
                  pl.BlockSpec((1, d), lambda i, j: (0, 0)),
                  pl.BlockSpec((d, tn), lambda i, j: (0, j))],
        out_specs=(pl.BlockSpec((tm, tn), lambda i, j: (i, j)),
                   pl.BlockSpec((d, tm), lambda i, j: (0, i))),
        scratch_shapes=[pltpu.VMEM((tm, d), BF16)],
        semantics=("arbitrary", "arbitrary"))(x, shift, scale, w_full)


def _sb_group_blocks(nb):
    return min(4, nb)


def _sb_fwd(proj, name):
    s_len = proj.shape[0]
    nb = s_len // BLK
    n_pairs = WIDTH // BLK
    gb = _sb_group_blocks(nb)
    kw = gb * BLK

    def body(q_ref, k_ref, v_ref, o_ref, tot_ref):
        lane = _iota2((1, BLK), 1)
        row = _iota2((BLK, BLK), 0)
        col = _iota2((BLK, BLK), 1)
        half = jnp.concatenate([(row >= col).astype(BF16), jnp.ones((BLK, BLK), BF16)], axis=1)
        suffix_and_sum = jnp.concatenate([half, half], axis=0)
        qpos = _iota2((BLK, kw), 0)
        kpos = _iota2((BLK, kw), 1)
        head_lanes = [(lane // SB_HEAD_DIM) == hh for hh in range(2)]

        def scores(i, gi, qms, masked):
            c0 = pl.multiple_of(gi * kw, kw)
            kb = k_ref[pl.ds(c0, kw), :].astype(BF16)
            z2s = [_dot_nt(qms[hh], kb) for hh in range(2)]
            if masked:
                valid = (c0 + kpos) < (i * BLK + qpos)
                z2s = [jnp.where(valid, z2, MASKED_SCORE) for z2 in z2s]
            return tuple(z2s)

        def accumulate(gi, z2s, carry):
            c0 = pl.multiple_of(gi * kw, kw)
            vf = v_ref[pl.ds(c0, kw), :]
            sp2s = [_softplus2_parts(z2)[0] for z2 in z2s]
            terms = [[_split2_lanes(sp2[:, b * BLK:(b + 1) * BLK]) for b in range(gb)] for sp2 in sp2s]
            sums = [[_dot(t, suffix_and_sum) for t in head_terms] for head_terms in terms]
            weights, laters = [], []
            for hh in range(2):
                later = carry[2 * hh + 1]
                parts = [None] * gb
                for b in reversed(range(gb)):
                    parts[b] = sums[hh][b][:, :BLK] + later
                    later = later + sums[hh][b][:, BLK:]
                weights.append(jnp.exp2(z2s[hh] - jnp.concatenate(parts, axis=1)).astype(BF16))
                laters.append(later)
            outs = [_dot(weights[hh], jnp.where(head_lanes[hh], vf, 0.0).astype(BF16)) for hh in range(2)]
            return (carry[0] + outs[0], laters[0], carry[2] + outs[1], laters[1])

        def queries(i):
            qf = q_ref[pl.ds(pl.multiple_of(i * BLK, BLK), BLK), :] * (SB_HEAD_DIM ** -0.5 * LOG2E)
            return [jnp.where(head_lanes[hh], qf, 0.0).astype(BF16) for hh in range(2)]

        def qblock(i, first_scores):
            r0 = pl.multiple_of(i * BLK, BLK)
            qms = queries(i)
            zero = jnp.zeros((BLK, BLK), F32)
            last = i // gb

            def step(jj, state):
                gi = last - 1 - jj
                return scores(i, gi, qms, False) + accumulate(gi + 1, state[:2], state[2:])

            state = lax.fori_loop(0, last, step, first_scores + (zero,) * 4)
            nxt = jnp.minimum(i + 1, nb - 1)
            next_scores = scores(nxt, nxt // gb, queries(nxt), True)
            carry = accumulate(0, state[:2], state[2:])
            o_ref[pl.ds(r0, BLK), :] = carry[0] + carry[2]
            tot_ref[0, pl.ds(r0, BLK), :] = carry[1]
            tot_ref[1, pl.ds(r0, BLK), :] = carry[3]
            return next_scores

        lax.fori_loop(0, nb, qblock, scores(0, 0, queries(0), True))

    col_spec = lambda off: pl.BlockSpec((s_len, BLK), lambda p: (0, off + p))
    return _pcall(
        body, name=name,
        out_shape=(jax.ShapeDtypeStruct((s_len, WIDTH), F32),
                   jax.ShapeDtypeStruct((2 * n_pairs, s_len, BLK), F32)),
        grid=(n_pairs,),
        in_specs=[col_spec(0), col_spec(n_pairs), col_spec(2 * n_pairs)],
        out_specs=(pl.BlockSpec((s_len, BLK), lambda p: (0, p)),
                   pl.BlockSpec((2, s_len, BLK), lambda p: (p, 0, 0))),
        semantics=("arbitrary",))(proj, proj, proj)


def _hg_masks(mask_ref):
    row = _iota2((BLK, BLK), 0)
    col = _iota2((BLK, BLK), 1)
    for v, m in enumerate(HG_LEVELS):
        same = (row // (2 * m)) == (col // (2 * m))
        mask_ref[v] = (same & ((row & m) != 0) & ((col & m) == 0)).astype(F32)


def _hg_mid(b, m):
    if m >= 4:
        n = BLK // (2 * m)
        mid = b.reshape(n, 2 * m, BLK)[:, m - 1:m, :]
        return jnp.broadcast_to(mid, (n, 2 * m, BLK)).reshape(BLK, BLK)
    pos = _iota2((BLK, BLK), 0) & (2 * m - 1)
    out = b
    for p in range(2 * m):
        delta = (m - 1) - p
        if delta != 0:
            out = jnp.where(pos == p, pltpu.roll(b, (-delta) % BLK, 0), out)
    return out


def _hg_chunk_inputs(qraw, fpre, lb):
    sig = _sigmoid(fpre)
    f = lb + (1.0 - lb) * sig
    g = jnp.log(f)
    q, dq_fac = _silu_and_grad(qraw)
    return q, dq_fac, f, sig, g


HG_GROUP = 2


def _neg_abs(x):
    return lax.bitcast_convert_type(lax.bitcast_convert_type(x, jnp.int32) | jnp.int32(-2 ** 31), F32)


def _hg_level_terms(qs, ks, bs, m):
    es = [jnp.exp(_neg_abs(b - _hg_mid(b, m))) for b in bs]
    qts = [(q * e).astype(BF16) for q, e in zip(qs, es)]
    kts = [(k * e).astype(BF16) for k, e in zip(ks, es)]
    return es, qts, kts


def _hg_load(refs, r0, lb_v, lower_incl):
    q_ref, f_ref, i_ref = refs
    heads = []
    for h in range(HG_GROUP):
        sl = slice(h * HG_HEAD_DIM, (h + 1) * HG_HEAD_DIM)
        heads.append(_hg_chunk_inputs(q_ref[pl.ds(r0, BLK), sl], f_ref[pl.ds(r0, BLK), sl], lb_v[:, sl])
                     + (i_ref[pl.ds(r0, BLK), sl],))
    bs = [_dot_exact_l(lower_incl, hd[4]) for hd in heads]
    return heads, bs


def _hgrn_fwd(proj, lb, name):
    s_len = proj.shape[0]
    nc = s_len // BLK
    gw = HG_GROUP * HG_HEAD_DIM
    n_groups = WIDTH // gw
    base = 4 * WIDTH // gw

    def body(q_ref, f_ref, i_ref, lb_ref, o_ref, mask_ref):
        _hg_masks(mask_ref)
        row = _iota2((BLK, BLK), 0)
        col = _iota2((BLK, BLK), 1)
        lower_incl = (col <= row).astype(BF16)
        lb_v = lb_ref[...]

        def chunk(ci, sts):
            r0 = pl.multiple_of(ci * BLK, BLK)
            heads, bs = _hg_load((q_ref, f_ref, i_ref), r0, lb_v, lower_incl)
            qs = [hd[0] for hd in heads]
            ks = [1.0 - hd[2] for hd in heads]
            vs = [hd[5] for hd in heads]
            vbs = [v.astype(BF16) for v in vs]
            b_ends = [b[BLK - 1:BLK, :] for b in bs]
            inters = [_dot_nt((q * jnp.exp(b)).astype(BF16), st.astype(BF16)) for q, b, st in zip(qs, bs, sts)]
            scs = [None] * HG_GROUP
            for v_idx, m in enumerate(HG_LEVELS):
                _, qts, kts = _hg_level_terms(qs, ks, bs, m)
                terms = [_dot_nt(qt, kt) for qt, kt in zip(qts, kts)]
                msk = mask_ref[v_idx]
                scs = [t * msk if sc is None else sc + t * msk for sc, t in zip(scs, terms)]
            intras = [_dot(sc.astype(BF16), vb) for sc, vb in zip(scs, vbs)]
            k_decs = [(k * jnp.exp(b_end - b)).astype(BF16) for k, b, b_end in zip(ks, bs, b_ends)]
            grown = [_dot_tn(vb, k_dec) for vb, k_dec in zip(vbs, k_decs)]
            for h in range(HG_GROUP):
                diag = jnp.sum(qs[h] * ks[h], axis=-1, keepdims=True)
                o_ref[pl.ds(r0, BLK), h * HG_HEAD_DIM:(h + 1) * HG_HEAD_DIM] = inters[h] + intras[h] + diag * vs[h]
            return tuple(st * jnp.exp(b_end) + g for st, b_end, g in zip(sts, b_ends, grown))

        lax.fori_loop(0, nc, chunk, (jnp.zeros((HG_HEAD_DIM, HG_HEAD_DIM), F32),) * HG_GROUP)

    col_spec = lambda off: pl.BlockSpec((s_len, gw), lambda h: (0, off + h))
    return _pcall(
        body, name=name, out_shape=jax.ShapeDtypeStruct((s_len, WIDTH), F32),
        grid=(n_groups,),
        in_specs=[col_spec(base), col_spec(base + n_groups), col_spec(base + 2 * n_groups),
                  pl.BlockSpec((1, gw), lambda h: (0, h))],
        out_specs=pl.BlockSpec((s_len, gw), lambda h: (0, h)),
        scratch_shapes=[pltpu.VMEM((len(HG_LEVELS), BLK, BLK), F32)],
        semantics=("arbitrary",))(proj, proj, proj, lb)


def _rms_heads(o_b, norm_w):
    n_parts, h_parts, r_parts = [], [], []
    for h in range(WIDTH // HG_HEAD_DIM):
        sl = slice(h * HG_HEAD_DIM, (h + 1) * HG_HEAD_DIM)
        o = o_b[:, sl]
        rstd = lax.rsqrt(jnp.mean(o * o, axis=-1, keepdims=True) + RMS_EPS)
        ohat = o * rstd
        h_parts.append(ohat)
        n_parts.append(ohat * norm_w[:, sl])
        r_parts.append(jnp.broadcast_to(rstd, o.shape))
    cat = lambda parts: jnp.concatenate(parts, axis=-1)
    return cat(n_parts), cat(h_parts), cat(r_parts)


def _shift_rows_down(halo, cur, k):
    tm = cur.shape[0]
    ext = jnp.concatenate([halo, cur], axis=0)
    return pltpu.roll(ext, k, 0)[8:8 + tm]


def _shift_rows_up(cur, halo, k):
    tm = cur.shape[0]
    ext = jnp.concatenate([cur, halo], axis=0)
    return pltpu.roll(ext, (tm + 8 - k) % (tm + 8), 0)[0:tm]


def _merge_fwd(x, proj, o_a, o_b, gate, norm_w, conv_w, wb, w_out, ln_g, ln_b, name):
    s_len, d = x.shape
    tm = min(256, s_len)
    hb = tm // 8

    def body(x_ref, oa_ref, za_ref, ob_ref, zb_ref, pre_ref, post_ref, u_ref, zc_ref, hpre_ref, hu_ref, g_ref,
             gate_ref, nw_ref, cw_ref, wb_ref, wo_ref, lg_ref, lbias_ref, xn_ref, mg_ref, yc_ref):
        i = pl.program_id(0)
        sa, _ = _silu_and_grad(za_ref[...])
        y_a = (oa_ref[...] * sa).astype(BF16)
        n_b, _, _ = _rms_heads(ob_ref[...], nw_ref[...])
        sb, _ = _silu_and_grad(zb_ref[...])
        y_b = (n_b * sb).astype(BF16)
        a = pre_ref[...] * u_ref[...]
        halo = jnp.where(i > 0, hpre_ref[...] * hu_ref[...], 0.0)
        cw = cw_ref[...]
        conv = cw[0:1] * _shift_rows_down(halo, a, 2) + cw[1:2] * _shift_rows_down(halo, a, 1) + cw[2:3] * a
        sc, _ = _silu_and_grad(zc_ref[...])
        y_c = (post_ref[...] * conv * sc).astype(BF16)
        merged = None
        for k, yk in enumerate((y_a, y_b, y_c)):
            yc_ref[:, k * WIDTH:(k + 1) * WIDTH] = yk
            term = _sigmoid(g_ref[:, k * d:(k + 1) * d]) * _dot(yk, wb_ref[k])
            merged = term if merged is None else merged + term
        mb = merged.astype(BF16)
        mg_ref[...] = mb
        y = _dot(mb, wo_ref[...])
        r = ALPHA * x_ref[...] + (1.0 + gate_ref[...]) * y
        rhat, _ = _standardize(r)
        xn_ref[...] = rhat * lg_ref[...] + lbias_ref[...]

    wcol = lambda cb: pl.BlockSpec((tm, WIDTH), lambda i: (i, cb))
    halo_spec = lambda cb: pl.BlockSpec((8, WIDTH), lambda i: (jnp.maximum(i * hb - 1, 0), cb))
    vec = lambda w: pl.BlockSpec((1, w), lambda i: (0, 0))
    return _pcall(
        body, name=name,
        out_shape=(jax.ShapeDtypeStruct((s_len, d), F32), jax.ShapeDtypeStruct((s_len, d), BF16),
                   jax.ShapeDtypeStruct((s_len, 3 * WIDTH), BF16)),
        grid=(s_len // tm,),
        in_specs=[pl.BlockSpec((tm, d), lambda i: (i, 0)),
                  wcol(0), wcol(3), wcol(0), wcol(7), wcol(8), wcol(9), wcol(10), wcol(11),
                  halo_spec(8), halo_spec(10),
                  pl.BlockSpec((tm, 3 * d), lambda i: (i, 2)),
                  vec(d), vec(WIDTH),
                  pl.BlockSpec((3, WIDTH), lambda i: (0, 0)),
                  pl.BlockSpec((3, WIDTH, d), lambda i: (0, 0, 0)),
                  pl.BlockSpec((d, d), lambda i: (0, 0)),
                  vec(d), vec(d)],
        out_specs=(pl.BlockSpec((tm, d), lambda i: (i, 0)), pl.BlockSpec((tm, d), lambda i: (i, 0)),
                   pl.BlockSpec((tm, 3 * WIDTH), lambda i: (i, 0))),
        semantics=("arbitrary",))(x, o_a, proj, o_b, proj, proj, proj, proj, proj, proj, proj, proj,
                                  gate, norm_w, conv_w, wb, w_out, ln_g, ln_b)


def _loss_fwd_bwd(y, target):
    s_len, d = y.shape
    tm = min(512, s_len)

    def body(y_ref, t_ref, loss_ref, dy_ref):
        @pl.when(pl.program_id(0) == 0)
        def _():
            loss_ref[...] = jnp.zeros_like(loss_ref)

        e = y_ref[...] - t_ref[...]
        dy_ref[...] = e * (1.0 / d)
        part = jnp.sum(jnp.sum(e * e, axis=-1, keepdims=True), axis=0, keepdims=True)
        loss_ref[...] += part * (0.5 / d)

    tile = pl.BlockSpec((tm, d), lambda i: (i, 0))
    return _pcall(body, name="loss", grid=(s_len // tm,),
                  out_shape=(jax.ShapeDtypeStruct((1, 1), F32), jax.ShapeDtypeStruct((s_len, d), F32)),
                  in_specs=[tile, tile],
                  out_specs=(pl.BlockSpec((1, 1), lambda i: (0, 0)), tile),
                  semantics=("arbitrary",))(y, target)


def _merge_bwd(dxn, x, merged, ycat, proj, gate, wb, w_out, ln_g, name):
    s_len, d = x.shape
    tm = min(256, s_len)

    def body(dxn_ref, x_ref, mg_ref, yc_ref, g_ref, gate_ref, wb_ref, wo_ref, lg_ref,
             dres_ref, dyc_ref, dg_ref, gwo_ref, gwb_ref, vec_ref):
        @pl.when(pl.program_id(0) == 0)
        def _():
            gwo_ref[...] = jnp.zeros_like(gwo_ref)
            gwb_ref[...] = jnp.zeros_like(gwb_ref)
            vec_ref[...] = jnp.zeros_like(vec_ref)

        mb = mg_ref[...]
        one_gate = 1.0 + gate_ref[...]
        y = _dot(mb, wo_ref[...])
        r = ALPHA * x_ref[...] + one_gate * y
        rhat, rstd = _standardize(r)
        dxn = dxn_ref[...]
        dr = _standardize_bwd(rhat, rstd, dxn * lg_ref[...])
        vec_ref[0:1, :] += jnp.sum(dxn * rhat, axis=0, keepdims=True)
        vec_ref[1:2, :] += jnp.sum(dxn, axis=0, keepdims=True)
        vec_ref[2:3, :] += jnp.sum(dr * y, axis=0, keepdims=True)
        dres_ref[...] = ALPHA * dr
        dy = (one_gate * dr).astype(BF16)
        gwo_ref[...] += _dot_tn(mb, dy)
        dmerged = _dot_nt(dy, wo_ref[...])
        for k in range(3):
            yk = yc_ref[:, k * WIDTH:(k + 1) * WIDTH]
            sg = _sigmoid(g_ref[:, k * d:(k + 1) * d])
            pk = _dot(yk, wb_ref[k])
            dg_ref[:, k * d:(k + 1) * d] = (dmerged * pk * sg * (1.0 - sg)).astype(BF16)
            dpk = (dmerged * sg).astype(BF16)
            dyc_ref[:, k * WIDTH:(k + 1) * WIDTH] = _dot_nt(dpk, wb_ref[k])
            gwb_ref[k] += _dot_tn(yk, dpk)

    tile = lambda w: pl.BlockSpec((tm, w), lambda i: (i, 0))
    vec = pl.BlockSpec((1, d), lambda i: (0, 0))
    return _pcall(
        body, name=name,
        out_shape=(jax.ShapeDtypeStruct((s_len, d), F32), jax.ShapeDtypeStruct((s_len, 3 * WIDTH), F32),
                   jax.ShapeDtypeStruct(proj.shape, BF16), jax.ShapeDtypeStruct((d, d), F32),
                   jax.ShapeDtypeStruct((3, WIDTH, d), F32), jax.ShapeDtypeStruct((8, d), F32)),
        grid=(s_len // tm,),
        in_specs=[tile(d), tile(d), tile(d), tile(3 * WIDTH),
                  pl.BlockSpec((tm, 3 * d), lambda i: (i, 2)),
                  vec, pl.BlockSpec((3, WIDTH, d), lambda i: (0, 0, 0)),
                  pl.BlockSpec((d, d), lambda i: (0, 0)), vec],
        out_specs=(tile(d), tile(3 * WIDTH), pl.BlockSpec((tm, 3 * d), lambda i: (i, 2)),
                   pl.BlockSpec((d, d), lambda i: (0, 0)),
                   pl.BlockSpec((3, WIDTH, d), lambda i: (0, 0, 0)),
                   pl.BlockSpec((8, d), lambda i: (0, 0))),
        semantics=("arbitrary",))(dxn, x, merged, ycat, proj, gate, wb, w_out, ln_g)


def _branch_bwd(dycat, proj, o_a, o_b, norm_w, conv_w, dproj, name):
    s_len = proj.shape[0]
    tm = min(256, s_len)
    hb = tm // 8
    n_tiles = s_len // tm

    def body(dya_ref, dyb_ref, dyc_ref, oa_ref, za_ref, ob_ref, zb_ref, pre_ref, post_ref, u_ref, zc_ref,
             hpre_ref, hu_ref, ndyc_ref, npost_ref, nzc_ref, nw_ref, cw_ref, dproj_in,
             dproj_ref, doa_ref, dob_ref, vec_ref, dza_scr, dzb_scr, dc_scr, sems):
        del dproj_in
        i = pl.program_id(0)

        @pl.when(i == 0)
        def _():
            vec_ref[...] = jnp.zeros_like(vec_ref)

        sa, dsa = _silu_and_grad(za_ref[...])
        dya = dya_ref[...]
        doa_ref[...] = dya * sa
        dza_scr[...] = (dya * oa_ref[...] * dsa).astype(BF16)
        nw = nw_ref[...]
        n_b, ohat, rstd = _rms_heads(ob_ref[...], nw)
        sb, dsb = _silu_and_grad(zb_ref[...])
        dyb = dyb_ref[...]
        dzb_scr[...] = (dyb * n_b * dsb).astype(BF16)
        dn = dyb * sb
        vec_ref[0:1, :] += jnp.sum(dn * ohat, axis=0, keepdims=True)
        dnw = dn * nw
        parts = []
        for h in range(WIDTH // HG_HEAD_DIM):
            sl = slice(h * HG_HEAD_DIM, (h + 1) * HG_HEAD_DIM)
            m2 = jnp.mean(dnw[:, sl] * ohat[:, sl], axis=-1, keepdims=True)
            parts.append(rstd[:, sl] * (dnw[:, sl] - ohat[:, sl] * m2))
        dob_ref[...] = jnp.concatenate(parts, axis=-1)
        cw = cw_ref[...]
        pre, u, post = pre_ref[...], u_ref[...], post_ref[...]
        a = pre * u
        halo = jnp.where(i > 0, hpre_ref[...] * hu_ref[...], 0.0)
        a1 = _shift_rows_down(halo, a, 1)
        a2 = _shift_rows_down(halo, a, 2)
        conv = cw[0:1] * a2 + cw[1:2] * a1 + cw[2:3] * a
        sc, dsc = _silu_and_grad(zc_ref[...])
        dyc = dyc_ref[...]
        dconv = dyc * post * sc
        nsc, _ = _silu_and_grad(nzc_ref[...])
        nxt = jnp.where(i < n_tiles - 1, ndyc_ref[...] * npost_ref[...] * nsc, 0.0)
        da = cw[2:3] * dconv + cw[1:2] * _shift_rows_up(dconv, nxt, 1) + cw[0:1] * _shift_rows_up(dconv, nxt, 2)
        dc_scr[:, 0 * WIDTH:1 * WIDTH] = (da * u).astype(BF16)
        dc_scr[:, 1 * WIDTH:2 * WIDTH] = (dyc * conv * sc).astype(BF16)
        dc_scr[:, 2 * WIDTH:3 * WIDTH] = (da * pre).astype(BF16)
        dc_scr[:, 3 * WIDTH:4 * WIDTH] = (dyc * post * conv * dsc).astype(BF16)
        vec_ref[1:2, :] += jnp.sum(dconv * a2, axis=0, keepdims=True)
        vec_ref[2:3, :] += jnp.sum(dconv * a1, axis=0, keepdims=True)
        vec_ref[3:4, :] += jnp.sum(dconv * a, axis=0, keepdims=True)
        rows = pl.ds(pl.multiple_of(i * tm, tm), tm)
        copies = [pltpu.make_async_copy(dza_scr, dproj_ref.at[rows, 3 * WIDTH:4 * WIDTH], sems.at[0]),
                  pltpu.make_async_copy(dzb_scr, dproj_ref.at[rows, 7 * WIDTH:8 * WIDTH], sems.at[1]),
                  pltpu.make_async_copy(dc_scr, dproj_ref.at[rows, 8 * WIDTH:12 * WIDTH], sems.at[2])]
        for cp in copies:
            cp.start()
        for cp in copies:
            cp.wait()

    wcol = lambda cb: pl.BlockSpec((tm, WIDTH), lambda i: (i, cb))
    prev = lambda cb: pl.BlockSpec((8, WIDTH), lambda i: (jnp.maximum(i * hb - 1, 0), cb))
    nxt = lambda cb: pl.BlockSpec((8, WIDTH), lambda i: (jnp.minimum((i + 1) * hb, s_len // 8 - 1), cb))
    anyspec = pl.BlockSpec(memory_space=pl.ANY)
    out = jax.ShapeDtypeStruct((s_len, WIDTH), F32)
    return _pcall(
        body, name=name,
        out_shape=(jax.ShapeDtypeStruct(dproj.shape, dproj.dtype), out, out, jax.ShapeDtypeStruct((8, WIDTH), F32)),
        grid=(n_tiles,),
        in_specs=[wcol(0), wcol(1), wcol(2), wcol(0), wcol(3), wcol(0), wcol(7), wcol(8), wcol(9), wcol(10), wcol(11),
                  prev(8), prev(10), nxt(2), nxt(9), nxt(11),
                  pl.BlockSpec((1, WIDTH), lambda i: (0, 0)), pl.BlockSpec((3, WIDTH), lambda i: (0, 0)), anyspec],
        out_specs=(anyspec, wcol(0), wcol(0), pl.BlockSpec((8, WIDTH), lambda i: (0, 0))),
        scratch_shapes=[pltpu.VMEM((tm, WIDTH), BF16), pltpu.VMEM((tm, WIDTH), BF16),
                        pltpu.VMEM((tm, 4 * WIDTH), BF16), pltpu.SemaphoreType.DMA((3,))],
        aliases={18: 0},
        semantics=("arbitrary",))(dycat, dycat, dycat, o_a, proj, o_b, proj, proj, proj, proj, proj,
                                  proj, proj, dycat, proj, proj, norm_w, conv_w, dproj)


def _sb_bwd(proj, do_a, totals, dproj, name):
    s_len = proj.shape[0]
    nb = s_len // BLK
    n_pairs = WIDTH // BLK
    scale = SB_HEAD_DIM ** -0.5
    gb = _sb_group_blocks(nb)
    kw = gb * BLK

    def body(q_ref, k_ref, v_ref, do_ref, tot_ref, dproj_in, dproj_ref, dq_ref, dk_ref, dv_ref, out_scr, sems):
        del dproj_in
        lane = _iota2((1, BLK), 1)
        row = _iota2((BLK, BLK), 0)
        col = _iota2((BLK, BLK), 1)
        ones = jnp.ones((BLK, BLK), BF16)
        twice = lambda m: jnp.concatenate([m, m], axis=0)
        before_and_sum = twice(jnp.concatenate([(row < col).astype(BF16), ones], axis=1))
        upto_and_sum = twice(jnp.concatenate([(row <= col).astype(BF16), ones], axis=1))
        qpos = _iota2((BLK, kw), 0)
        kpos = _iota2((BLK, kw), 1)
        head_lanes = [(lane // SB_HEAD_DIM) == hh for hh in range(2)]
        dk_ref[...] = jnp.zeros_like(dk_ref)
        dv_ref[...] = jnp.zeros_like(dv_ref)

        causal = kpos - qpos

        def scores(i, gi, qms):
            c0 = pl.multiple_of(gi * kw, kw)
            kb = k_ref[pl.ds(c0, kw), :].astype(BF16)
            valid = causal < i * BLK - c0
            return tuple(jnp.where(valid, _dot_nt(qms[hh], kb), MASKED_SCORE) for hh in range(2))

        def process(gi, z2s, qms, doms, totals_i, carry):
            c0 = pl.multiple_of(gi * kw, kw)
            kf = k_ref[pl.ds(c0, kw), :]
            vf = v_ref[pl.ds(c0, kw), :]
            kms = [jnp.where(head_lanes[hh], kf, 0.0).astype(BF16) for hh in range(2)]
            vms = [jnp.where(head_lanes[hh], vf, 0.0).astype(BF16) for hh in range(2)]
            das = [_dot_nt(doms[hh], vms[hh]) for hh in range(2)]
            halves = [_softplus2_parts(z2) for z2 in z2s]
            terms = [[_split2_lanes(sp2[:, b * BLK:(b + 1) * BLK]) for b in range(gb)] for sp2, _ in halves]
            sums = [[_dot(t, before_and_sum) for t in head_terms] for head_terms in terms]
            weights, gmats, l_befores = [], [], []
            for hh in range(2):
                l_before = carry[3 * hh + 1]
                parts = []
                for b in range(gb):
                    parts.append(totals_i[hh] - l_before - sums[hh][b][:, :BLK])
                    l_before = l_before + sums[hh][b][:, BLK:]
                a = jnp.exp2(z2s[hh] - jnp.concatenate(parts, axis=1))
                weights.append(a.astype(BF16))
                gmats.append(a * das[hh])
                l_befores.append(l_before)
            terms = [[_split2_lanes(g[:, b * BLK:(b + 1) * BLK]) for b in range(gb)] for g in gmats]
            sums = [[_dot(t, upto_and_sum) for t in head_terms] for head_terms in terms]
            dzs, g_befores = [], []
            for hh in range(2):
                g_before = carry[3 * hh + 2]
                parts = []
                for b in range(gb):
                    parts.append(g_before + sums[hh][b][:, :BLK])
                    g_before = g_before + sums[hh][b][:, BLK:]
                dzs.append((gmats[hh] - halves[hh][1] * jnp.concatenate(parts, axis=1)).astype(BF16))
                g_befores.append(g_before)
            dks = [_dot_tn(dzs[hh], qms[hh]) for hh in range(2)]
            dvs = [_dot_tn(weights[hh], doms[hh]) for hh in range(2)]
            dqs = [_dot(dzs[hh], kms[hh]) for hh in range(2)]
            dk_ref[pl.ds(c0, kw), :] += (dks[0] + dks[1]) * (1.0 / LOG2E)
            dv_ref[pl.ds(c0, kw), :] += dvs[0] + dvs[1]
            return (carry[0] + dqs[0], l_befores[0], g_befores[0], carry[3] + dqs[1], l_befores[1], g_befores[1])

        def queries(i):
            qf = q_ref[pl.ds(pl.multiple_of(i * BLK, BLK), BLK), :] * (scale * LOG2E)
            return [jnp.where(head_lanes[hh], qf, 0.0).astype(BF16) for hh in range(2)]

        def qblock(i, first_scores):
            r0 = pl.multiple_of(i * BLK, BLK)
            qms = queries(i)
            dof = do_ref[pl.ds(r0, BLK), :]
            doms = [jnp.where(head_lanes[hh], dof, 0.0).astype(BF16) for hh in range(2)]
            totals_i = [tot_ref[hh, pl.ds(r0, BLK), :] for hh in range(2)]
            zero = jnp.zeros((BLK, BLK), F32)
            last = i // gb

            def step(gi, state):
                return scores(i, gi + 1, qms) + process(gi, state[:2], qms, doms, totals_i, state[2:])

            state = lax.fori_loop(0, last, step, first_scores + (zero,) * 6)
            nxt = jnp.minimum(i + 1, nb - 1)
            next_scores = scores(nxt, 0, queries(nxt))
            carry = process(last, state[:2], qms, doms, totals_i, state[2:])
            dq_ref[pl.ds(r0, BLK), :] = (carry[0] + carry[3]) * scale
            return next_scores

        lax.fori_loop(0, nb, qblock, scores(0, 0, queries(0)))
        pair = pl.program_id(0)
        copies = []
        for t, ref in enumerate((dq_ref, dk_ref, dv_ref)):
            out_scr[t] = ref[...].astype(BF16)
            col = pl.multiple_of((t * n_pairs + pair) * BLK, BLK)
            copies.append(pltpu.make_async_copy(out_scr.at[t], dproj_ref.at[:, pl.ds(col, BLK)], sems.at[t]))
            copies[-1].start()
        for cp in copies:
            cp.wait()

    col_spec = lambda off: pl.BlockSpec((s_len, BLK), lambda p: (0, off + p))
    anyspec = pl.BlockSpec(memory_space=pl.ANY)
    return _pcall(
        body, name=name, out_shape=jax.ShapeDtypeStruct(dproj.shape, dproj.dtype), grid=(n_pairs,),
        in_specs=[col_spec(0), col_spec(n_pairs), col_spec(2 * n_pairs), col_spec(0),
                  pl.BlockSpec((2, s_len, BLK), lambda p: (p, 0, 0)), anyspec],
        out_specs=anyspec,
        scratch_shapes=[pltpu.VMEM((s_len, BLK), F32)] * 3 + [pltpu.VMEM((3, s_len, BLK), BF16),
                                                              pltpu.SemaphoreType.DMA((3,))],
        aliases={5: 0},
        semantics=("arbitrary",))(proj, proj, proj, do_a, totals, dproj)


def _hgrn_bwd(proj, do_b, lb, dproj, name):
    s_len = proj.shape[0]
    nc = s_len // BLK
    gw = HG_GROUP * HG_HEAD_DIM
    n_groups = WIDTH // gw
    base = 4 * WIDTH // gw
    heads_of = range(HG_GROUP)

    def body(q_ref, f_ref, i_ref, do_ref, lb_ref, dproj_in, dproj_ref, dlb_ref, mask_ref, st_ref, out_scr, sems):
        del dproj_in
        _hg_masks(mask_ref)
        row = _iota2((BLK, BLK), 0)
        col = _iota2((BLK, BLK), 1)
        lower_incl = (col <= row).astype(BF16)
        upper_incl = (col >= row).astype(BF16)
        lb_v = lb_ref[...]
        refs = (q_ref, f_ref, i_ref)

        def fwd_chunk(ci, sts):
            for h in heads_of:
                st_ref[ci, h] = sts[h]
            heads, bs = _hg_load(refs, pl.multiple_of(ci * BLK, BLK), lb_v, lower_incl)
            b_ends = [b[BLK - 1:BLK, :] for b in bs]
            k_decs = [((1.0 - hd[2]) * jnp.exp(b_end - b)).astype(BF16) for hd, b, b_end in zip(heads, bs, b_ends)]
            grown = [_dot_tn(hd[5].astype(BF16), k_dec) for hd, k_dec in zip(heads, k_decs)]
            return tuple(st * jnp.exp(b_end) + g for st, b_end, g in zip(sts, b_ends, grown))

        zero_state = (jnp.zeros((HG_HEAD_DIM, HG_HEAD_DIM), F32),) * HG_GROUP
        lax.fori_loop(0, nc, fwd_chunk, zero_state)

        def bwd_chunk(cc, carry):
            dsts, suffixes, dlbs = carry
            ci = nc - 1 - cc
            r0 = pl.multiple_of(ci * BLK, BLK)
            heads, bs = _hg_load(refs, r0, lb_v, lower_incl)
            qs = [hd[0] for hd in heads]
            fs = [hd[2] for hd in heads]
            ks = [1.0 - f for f in fs]
            vs = [hd[5] for hd in heads]
            vbs = [v.astype(BF16) for v in vs]
            dos = [do_ref[pl.ds(r0, BLK), h * HG_HEAD_DIM:(h + 1) * HG_HEAD_DIM] for h in heads_of]
            dobs = [do.astype(BF16) for do in dos]
            b_ends = [b[BLK - 1:BLK, :] for b in bs]
            e_qs = [jnp.exp(b) for b in bs]
            e_ks = [jnp.exp(b_end - b) for b, b_end in zip(bs, b_ends)]
            qes = [(q * e).astype(BF16) for q, e in zip(qs, e_qs)]
            khs = [(k * e).astype(BF16) for k, e in zip(ks, e_ks)]
            st_terms = [_split2_lanes(st_ref[ci, h]) for h in heads_of]
            ds_terms = [_split2_lanes(dst) for dst in dsts]
            dqes = [_dot(dob, t[:, :HG_HEAD_DIM]) + _dot(dob, t[:, HG_HEAD_DIM:]) for dob, t in zip(dobs, st_terms)]
            dkhs = [_dot(vb, t[:, :HG_HEAD_DIM]) + _dot(vb, t[:, HG_HEAD_DIM:]) for vb, t in zip(vbs, ds_terms)]
            dvs = [_dot_nt(kh, t[:, :HG_HEAD_DIM]) for kh, t in zip(khs, ds_terms)]
            grown = [_dot_tn(dob, qe) for dob, qe in zip(dobs, qes)]
            das = [_dot_nt(dob, vb) for dob, vb in zip(dobs, vbs)]
            dqs = [e * dqe for e, dqe in zip(e_qs, dqes)]
            dks = [e * dkh for e, dkh in zip(e_ks, dkhs)]
            dlogs = [qe.astype(F32) * dqe - kh.astype(F32) * dkh for qe, dqe, kh, dkh in zip(qes, dqes, khs, dkhs)]
            scs = [None] * HG_GROUP
            for v_idx, m in enumerate(HG_LEVELS):
                es, qms, kms = _hg_level_terms(qs, ks, bs, m)
                msk = mask_ref[v_idx]
                terms = [_dot_nt(qm, km) for qm, km in zip(qms, kms)]
                pms = [(da * msk).astype(BF16) for da in das]
                dqms = [_dot(pm, km) for pm, km in zip(pms, kms)]
                dkms = [_dot_tn(pm, qm) for pm, qm in zip(pms, qms)]
                scs = [t * msk if sc is None else sc + t * msk for sc, t in zip(scs, terms)]
                dqs = [dq + dqm * e for dq, dqm, e in zip(dqs, dqms, es)]
                dks = [dk + dkm * e for dk, dkm, e in zip(dks, dkms, es)]
                dlogs = [dl + (qm.astype(F32) * dqm - km.astype(F32) * dkm)
                         for dl, qm, dqm, km, dkm in zip(dlogs, qms, dqms, kms, dkms)]
            intras = [_dot_tn(sc.astype(BF16), dob) for sc, dob in zip(scs, dobs)]
            dgs = [_dot_exact_l(upper_incl, dl) + sfx for dl, sfx in zip(dlogs, suffixes)]
            new_dlbs = []
            for h in heads_of:
                q, dq_fac, f, sig = heads[h][0], heads[h][1], heads[h][2], heads[h][3]
                a_diag = jnp.sum(dos[h] * vs[h], axis=-1, keepdims=True)
                s_diag = jnp.sum(q * ks[h], axis=-1, keepdims=True)
                dq = dqs[h] + a_diag * ks[h]
                dk = dks[h] + a_diag * q
                dv = dvs[h] + intras[h] + s_diag * dos[h]
                dfull = dgs[h] / f - dk
                sl = slice(h * HG_HEAD_DIM, (h + 1) * HG_HEAD_DIM)
                out_scr[0, pl.ds(r0, BLK), sl] = (dq * dq_fac).astype(BF16)
                out_scr[1, pl.ds(r0, BLK), sl] = (dfull * (1.0 - lb_v[:, sl]) * sig * (1.0 - sig)).astype(BF16)
                out_scr[2, pl.ds(r0, BLK), sl] = dv.astype(BF16)
                new_dlbs.append(dlbs[h] + jnp.sum(dfull * (1.0 - sig), axis=0, keepdims=True))
            new_dsts = tuple(dst * jnp.exp(b_end) + g for dst, b_end, g in zip(dsts, b_ends, grown))
            return new_dsts, tuple(dg[0:1, :] for dg in dgs), tuple(new_dlbs)

        zero_row = (jnp.zeros((1, HG_HEAD_DIM), F32),) * HG_GROUP
        _, _, dlbs = lax.fori_loop(0, nc, bwd_chunk, (zero_state, zero_row, zero_row))
        dlb_ref[...] = jnp.broadcast_to(jnp.concatenate(dlbs, axis=1), dlb_ref.shape)
        group = pl.program_id(0)
        copies = []
        for t in range(3):
            col = pl.multiple_of((base + t * n_groups + group) * gw, gw)
            copies.append(pltpu.make_async_copy(out_scr.at[t], dproj_ref.at[:, pl.ds(col, gw)], sems.at[t]))
            copies[-1].start()
        for cp in copies:
            cp.wait()

    col_spec = lambda off: pl.BlockSpec((s_len, gw), lambda h: (0, off + h))
    anyspec = pl.BlockSpec(memory_space=pl.ANY)
    return _pcall(
        body, name=name,
        out_shape=(jax.ShapeDtypeStruct(dproj.shape, dproj.dtype), jax.ShapeDtypeStruct((8, WIDTH), F32)),
        grid=(n_groups,),
        in_specs=[col_spec(base), col_spec(base + n_groups), col_spec(base + 2 * n_groups), col_spec(0),
                  pl.BlockSpec((1, gw), lambda h: (0, h)), anyspec],
        out_specs=(anyspec, pl.BlockSpec((8, gw), lambda h: (0, h))),
        scratch_shapes=[pltpu.VMEM((len(HG_LEVELS), BLK, BLK), F32),
                        pltpu.VMEM((nc, HG_GROUP, HG_HEAD_DIM, HG_HEAD_DIM), F32),
                        pltpu.VMEM((3, s_len, gw), BF16), pltpu.SemaphoreType.DMA((3,))],
        aliases={5: 0},
        semantics=("arbitrary",))(proj, proj, proj, do_b, lb, dproj)


def _dh_matmul(dproj, w_full, after, name):
    s_len, n = dproj.shape
    d = w_full.shape[0]
    tm = min(512, s_len)
    tk = 1536

    def body(dp_ref, w_ref, after_ref, dh_ref):
        del after_ref
        part = _dot_nt(dp_ref[...], w_ref[...])

        @pl.when(pl.program_id(1) == 0)
        def _():
            dh_ref[...] = part

        @pl.when(pl.program_id(1) > 0)
        def _():
            dh_ref[...] += part

    return _pcall(
        body, name=name, out_shape=jax.ShapeDtypeStruct((s_len, d), F32),
        grid=(s_len // tm, n // tk),
        in_specs=[pl.BlockSpec((tm, tk), lambda i, k: (i, k)), pl.BlockSpec((d, tk), lambda i, k: (0, k)),
                  pl.BlockSpec(memory_space=pl.ANY)],
        out_specs=pl.BlockSpec((tm, d), lambda i, k: (i, 0)),
        semantics=("arbitrary", "arbitrary"))(dproj, w_full, after)


def _gw_matmul(h_t, dproj, name):
    d, s_len = h_t.shape
    n = dproj.shape[1]
    tn = 1152

    def body(ht_ref, dp_ref, gw_ref):
        gw_ref[...] = _dot(ht_ref[...], dp_ref[...]).astype(BF16)

    return _pcall(
        body, name=name, out_shape=jax.ShapeDtypeStruct((d, n), BF16),
        grid=(n // tn,),
        in_specs=[pl.BlockSpec((d, s_len), lambda j: (0, 0)), pl.BlockSpec((s_len, tn), lambda j: (0, j))],
        out_specs=pl.BlockSpec((d, tn), lambda j: (0, j)),
        semantics=("arbitrary",))(h_t, dproj)


def _ln_bwd(dh, x, scale, dres, name):
    s_len, d = x.shape
    tm = min(512, s_len)

    def body(dh_ref, x_ref, sc_ref, dres_ref, dx_ref, vec_ref):
        @pl.when(pl.program_id(0) == 0)
        def _():
            vec_ref[...] = jnp.zeros_like(vec_ref)

        dh = dh_ref[...]
        xs, rstd = _standardize(x_ref[...])
        vec_ref[0:1, :] += jnp.sum(dh, axis=0, keepdims=True)
        vec_ref[1:2, :] += jnp.sum(dh * xs, axis=0, keepdims=True)
        dx_ref[...] = _standardize_bwd(xs, rstd, dh * (1.0 + sc_ref[...])) + dres_ref[...]

    tile = pl.BlockSpec((tm, d), lambda i: (i, 0))
    return _pcall(body, name=name, grid=(s_len // tm,),
                  out_shape=(jax.ShapeDtypeStruct((s_len, d), F32), jax.ShapeDtypeStruct((8, d), F32)),
                  in_specs=[tile, tile, pl.BlockSpec((1, d), lambda i: (0, 0)), tile],
                  out_specs=(tile, pl.BlockSpec((8, d), lambda i: (0, 0))),
                  semantics=("arbitrary",))(dh, x, scale, dres)


def _wmod_grad(c_t, dmod):
    d = c_t.shape[0]
    n_layers, _, cm = dmod.shape

    def body(c_ref, dm_ref, o_ref):
        for l in range(n_layers):
            acc = None
            for b in range(NDEV):
                term = c_ref[:, b:b + 1] * dm_ref[l, b:b + 1, :]
                acc = term if acc is None else acc + term
            o_ref[l] = acc

    return _pcall(body, name="wmod_grad", out_shape=jax.ShapeDtypeStruct((n_layers, d, cm), F32))(c_t, dmod)


def _sum_adamw(parts_list, w, m, v, name):
    n_ranges = len(parts_list)
    n_src, range_rows, cols = parts_list[0].shape
    rows = range_rows * n_ranges
    tr = range_rows
    for cand in (512, 256, 128, 64, 32, 16, 8):
        if range_rows % cand == 0 and cand * cols * 4 <= (2 << 20):
            tr = cand
            break
    tiles = range_rows // tr

    def body(*refs):
        p_refs = refs[:n_ranges]
        w_ref, m_ref, v_ref, g_ref, d_ref, nm_ref, nv_ref = refs[n_ranges:]

        def step(p_ref):
            g = p_ref[0].astype(F32)
            for s in range(1, n_src):
                g = g + p_ref[s].astype(F32)
            nm = ADAM_B1 * m_ref[...] + (1.0 - ADAM_B1) * g
            nv = ADAM_B2 * v_ref[...] + (1.0 - ADAM_B2) * (g * g)
            m_hat = nm / (1.0 - ADAM_B1 ** ADAM_STEP)
            v_hat = nv / (1.0 - ADAM_B2 ** ADAM_STEP)
            g_ref[...] = g
            d_ref[...] = -ADAM_LR * (m_hat / (jnp.sqrt(v_hat) + ADAM_EPS) + ADAM_WD * w_ref[...])
            nm_ref[...] = nm
            nv_ref[...] = nv

        if n_ranges == 1:
            step(p_refs[0])
        else:
            for j in range(n_ranges):
                @pl.when(pl.program_id(0) // tiles == j)
                def _(j=j):
                    step(p_refs[j])

    def part_spec(j):
        return pl.BlockSpec((n_src, tr, cols), lambda i: (0, jnp.clip(i - j * tiles, 0, tiles - 1), 0))

    tile = pl.BlockSpec((tr, cols), lambda i: (i, 0))
    out = jax.ShapeDtypeStruct((rows, cols), F32)
    return _pcall(body, name=name, grid=(rows // tr,), out_shape=(out,) * 4,
                  in_specs=[part_spec(j) for j in range(n_ranges)] + [tile, tile, tile],
                  out_specs=(tile,) * 4, semantics=("arbitrary",))(*parts_list, w, m, v)


def _sum_parts(parts, name):
    n_src = parts.shape[0]

    def body(p_ref, o_ref):
        acc = p_ref[0]
        for s in range(1, n_src):
            acc = acc + p_ref[s]
        o_ref[...] = acc

    return _pcall(body, name=name, out_shape=jax.ShapeDtypeStruct(parts.shape[1:], F32))(parts)


def _pair_sum(gw, stage, me, name):
    d = gw.shape[0]
    n_slots, _, shard = stage.shape

    def body(me_ref, g_ref, s_ref, own_ref, o_ref):
        del me_ref
        total = (g_ref[...].astype(F32) + s_ref[0].astype(F32)).astype(BF16)
        o_ref[0] = total

        @pl.when(pl.program_id(0) == 0)
        def _():
            own_ref[0] = total

    slot = pl.BlockSpec((1, d, shard), lambda jj, me_ref: (jj, 0, 0))
    out = jax.ShapeDtypeStruct(stage.shape, BF16)
    return pl.pallas_call(
        body, name=name, out_shape=(out, out),
        grid_spec=pltpu.PrefetchScalarGridSpec(
            num_scalar_prefetch=1, grid=(n_slots,),
            in_specs=[pl.BlockSpec((d, shard), lambda jj, me_ref: (0, me_ref[0] ^ (2 * jj))), slot],
            out_specs=(pl.BlockSpec((1, d, shard), lambda jj, me_ref: (0, 0, 0)), slot)),
        compiler_params=pltpu.CompilerParams(dimension_semantics=("arbitrary",), vmem_limit_bytes=VMEM_LIMIT),
        interpret=False)(me.reshape(1).astype(jnp.int32), gw, stage)


def _lower_bound_table(lower_bounds):
    p = jax.nn.softmax(lower_bounds.astype(F32), axis=0)
    return jnp.cumsum(p, axis=0) - p[0:1]


def _pad_rows(v, width):
    n = v.shape[0]
    rows = -(-n // width)
    rows = -(-rows // 8) * 8
    return jnp.pad(v, (0, rows * width - n)).reshape(rows, width)


def kernel(x, c, w_mod, b_mod, w_in, conv_w, hgrn_norm_w, lower_bounds, w_branch, w_out, ln_g, ln_b, loss_target, m_w_mod, m_b_mod, m_w_in, m_conv_w, m_hgrn_norm_w, m_lower_bounds, m_w_branch, m_w_out, m_ln_g, m_ln_b, v_w_mod, v_b_mod, v_w_in, v_conv_w, v_hgrn_norm_w, v_lower_bounds, v_w_branch, v_w_out, v_ln_g, v_ln_b):
    n_layers = N_LAYERS
    s_len, d = x.shape[1], x.shape[2]
    n_cols = w_in.shape[2] * NDEV
    cw_cols = conv_w.shape[2]
    cm = w_mod.shape[2]
    me = _my_index()
    x0 = x[0]
    target = loss_target[0]

    small = _pad_rows(jnp.concatenate([c.reshape(-1), conv_w.reshape(-1)]), BLK)
    small_all = _all_gather_small("gather_c_conv", small).reshape(NDEV, -1)
    c_all = small_all[:, :d]
    conv_full = small_all[:, d:d + n_layers * 3 * cw_cols].reshape(NDEV, n_layers, 3, cw_cols)
    conv_full = conv_full.transpose(1, 2, 0, 3).reshape(n_layers, 3, WIDTH)

    b_mod_mine = lax.dynamic_slice_in_dim(b_mod, me * cm, cm, axis=1).reshape(n_layers, 1, cm)
    mod_cols = _mod_fwd(c_all, w_mod, b_mod_mine)
    mod_all = _all_gather_small("gather_mod", mod_cols.reshape(n_layers * NDEV, cm))
    mod_all = mod_all.reshape(NDEV, n_layers, NDEV, cm)
    mod_mine = lax.dynamic_index_in_dim(mod_all, me, axis=2, keepdims=False)
    mod_mine = mod_mine.transpose(1, 0, 2).reshape(n_layers, 3, 1, d)

    shard = w_in.shape[2]
    dsh = d // NDEV
    w_in_b, w_branch_b, w_out_b = w_in.astype(BF16), w_branch.astype(BF16), w_out.astype(BF16)
    window = lambda ref, dev: ref.at[:, pl.ds(pl.multiple_of(dev * shard, BLK), shard)]

    def two_step_sends(places):
        chips, sibling = [], []
        for k in (1, 2, 4, 6):
            for a, place in enumerate(places):
                chips.append((k, lambda ins, lands, me, a=a: ins[a],
                              lambda lands, me, a=a, place=place: place(lands[a], me),
                              lambda lands, me, a=a, k=k, place=place: place(lands[a], me ^ k)))
        for j in (2, 4, 6):
            for a, place in enumerate(places):
                sibling.append((1, lambda ins, lands, me, a=a, j=j, place=place: place(lands[a], me ^ j),
                                lambda lands, me, a=a, j=j, place=place: place(lands[a], me ^ j),
                                lambda lands, me, a=a, j=j, place=place: place(lands[a], me ^ 1 ^ j)))
        return chips, sibling

    in_sends = two_step_sends([window])
    rest_sends = two_step_sends([_slot, _slot])
    layer_sends = two_step_sends([window, _slot, _slot])

    def in_land(l):
        return _place_own_window(f"place_w_in_{l}", (d, n_cols), w_in_b[l], me)

    def rest_lands(l):
        return [_place_own((NDEV, 3, WIDTH, dsh), BF16, w_branch_b[l][None], (me, 0, 0, 0)),
                _place_own((NDEV, dsh, d), BF16, w_out_b[l][None], (me, 0, 0))]

    def gather_start(name, shards, lands, sends, after):
        return _exchange_start(f"{name}_chips_start", shards, lands, sends[0], after)

    def gather_pass_on(name, started, after, sends):
        _, lands = _exchange_wait(f"{name}_chips_wait", started, after, sends[0])
        return _exchange_start(f"{name}_sibling_start", [], lands, sends[1])

    def gather_finish(name, started, after, sends):
        return _exchange_wait(f"{name}_sibling_wait", started, after, sends[1])[1]

    def branch_out_weights(w_branch_l, w_out_l):
        return w_branch_l.transpose(1, 2, 0, 3).reshape(3, WIDTH, d), w_out_l.reshape(d, d)

    gathering = gather_start("gather_w_in_0", [w_in_b[0]], [in_land(0)], in_sends, mod_mine)
    passing = gather_pass_on("gather_w_in_0", gathering, gathering[4], in_sends)
    rest_gathering = gather_start("gather_rest_0", [w_branch_b[0], w_out_b[0]], rest_lands(0), rest_sends, passing[4])
    next_gathering = None
    if n_layers > 1:
        next_gathering = gather_start("gather_weights_1", [w_in_b[1], w_branch_b[1], w_out_b[1]],
                                      [in_land(1)] + rest_lands(1), layer_sends, rest_gathering[4])
    w_in_l = gather_finish("gather_w_in_0", passing, (next_gathering or rest_gathering)[4], in_sends)[0]

    lbs = _lower_bound_table(lower_bounds)
    norm_w4 = jnp.tile(hgrn_norm_w, (1, WIDTH // HG_HEAD_DIM))

    saved = []
    xl = x0
    for l in range(n_layers):
        shift, scale, gate = mod_mine[l, 0], mod_mine[l, 1], mod_mine[l, 2]
        proj, h_t = _ln_proj(xl, shift, scale, w_in_l, f"ln_proj_{l}")
        o_a, totals = _sb_fwd(proj, f"sb_fwd_{l}")
        if l == 0:
            rest_passing = gather_pass_on("gather_rest_0", rest_gathering, o_a, rest_sends)
        o_b = _hgrn_fwd(proj, lbs[l:l + 1], f"hgrn_fwd_{l}")
        if l == 0:
            wb_l, wo_l = branch_out_weights(*gather_finish("gather_rest_0", rest_passing, o_b, rest_sends))
            if n_layers > 1:
                next_passing = gather_pass_on("gather_weights_1", next_gathering, o_b, layer_sends)
                gate = gate + next_passing[4][0, 0]
        x_new, merged, ycat = _merge_fwd(xl, proj, o_a, o_b, gate, norm_w4[l:l + 1], conv_full[l],
                                         wb_l, wo_l, ln_g[l:l + 1], ln_b[l:l + 1], f"merge_fwd_{l}")
        saved.append((xl, proj, h_t, o_a, totals, o_b, merged, ycat, w_in_l, wb_l, wo_l))
        if l == 0 and n_layers > 1:
            w_in_l, w_branch_l, w_out_l = gather_finish("gather_weights_1", next_passing, x_new, layer_sends)
            wb_l, wo_l = branch_out_weights(w_branch_l, w_out_l)
        xl = x_new

    loss_part, dx = _loss_fwd_bwd(xl, target)
    loss = lax.psum(loss_part[0, 0], ("x", "y", "c"))

    pair_sends = [(1, lambda ins, lands, me, j=j: window(ins[0], me ^ 1 ^ j),
                   lambda lands, me, jj=jj: lands[0].at[jj], lambda lands, me, jj=jj: lands[0].at[jj])
                  for jj, j in enumerate((0, 2, 4, 6))]
    chip_sum_sends = [(j, lambda ins, lands, me, jj=jj: ins[0].at[jj],
                       lambda lands, me, jj=jj: lands[0].at[jj], lambda lands, me, jj=jj: lands[0].at[jj])
                      for jj, j in ((1, 2), (2, 4), (3, 6))]
    rest_scatter = _direct_sends([(0, 0, _slot, _slot), (1, 1, _slot, _slot)])
    scattering = [None] * n_layers
    small_grads = [None] * n_layers
    dmod = [None] * n_layers
    tie = None
    for l in reversed(range(n_layers)):
        xl, proj, h_t, o_a, totals, o_b, merged, ycat, w_in_l, wb_l, wo_l = saved[l]
        scale, gate = mod_mine[l, 1], mod_mine[l, 2]
        if tie is not None:
            gate = gate + tie[0, 0]
        dres, dycat, dproj, gwo, gwb, mvec = _merge_bwd(dx, xl, merged, ycat, proj, gate, wb_l, wo_l,
                                                        ln_g[l:l + 1], f"merge_bwd_{l}")
        gwb_by_owner = gwb.astype(BF16).reshape(3, WIDTH, NDEV, dsh).transpose(2, 0, 1, 3)
        gwo_by_owner = gwo.astype(BF16).reshape(NDEV, dsh, d)
        lands = [_place_own((NDEV, 3, WIDTH, dsh), BF16, lax.dynamic_slice_in_dim(gwb_by_owner, me, 1, axis=0),
                            (me, 0, 0, 0)),
                 _place_own((NDEV, dsh, d), BF16, lax.dynamic_slice_in_dim(gwo_by_owner, me, 1, axis=0),
                            (me, 0, 0))]
        rest_started = _exchange_start(f"scatter_rest_{l}_start", [gwb_by_owner, gwo_by_owner], lands, rest_scatter)
        dproj, do_a, do_b, bvec = _branch_bwd(dycat, proj, o_a, o_b, norm_w4[l:l + 1] + rest_started[4][0, 0],
                                              conv_full[l], dproj, f"branch_bwd_{l}")
        dproj = _sb_bwd(proj, do_a, totals, dproj, f"sb_bwd_{l}")
        dproj, dlb = _hgrn_bwd(proj, do_b, lbs[l:l + 1], dproj, f"hgrn_bwd_{l}")
        gwi = _gw_matmul(h_t, dproj, f"gw_matmul_{l}")
        swapping = _exchange_start(f"scatter_in_{l}_sibling_start", [gwi], [lax.empty((4, d, shard), BF16)], pair_sends)
        (gwi,), (stage,) = _exchange_wait(f"scatter_in_{l}_sibling_wait", swapping, swapping[4], pair_sends)
        land, chip_sums = _pair_sum(gwi, stage, me, f"pair_sum_{l}")
        in_started = _exchange_start(f"scatter_in_{l}_chips_start", [chip_sums], [land], chip_sum_sends)
        scattering[l] = (in_started, rest_started)
        tie = in_started[4]
        dh = _dh_matmul(dproj, w_in_l, tie, f"dh_matmul_{l}")
        dx, lvec = _ln_bwd(dh, xl, scale + tie[0, 0], dres, f"ln_bwd_{l}")
        dmod[l] = jnp.concatenate([lvec[0], lvec[1], mvec[2]])
        norm_grad = bvec[0].reshape(WIDTH // HG_HEAD_DIM, HG_HEAD_DIM).sum(axis=0)
        small_grads[l] = jnp.concatenate([mvec[0], mvec[1], norm_grad, dlb[0], bvec[1:4].reshape(-1)])
    grad_x = dx[None]

    small_vec = jnp.concatenate(dmod + small_grads)
    n_small = small_vec.shape[0]
    small_all = _all_gather_small("gather_small_grads", _pad_rows(small_vec, BLK))
    small_sum = _sum_parts(small_all, "sum_small_grads").reshape(-1)[:n_small]
    dmod_all = small_all.reshape(NDEV, -1)[:, :n_layers * 3 * d].reshape(NDEV, n_layers, 3 * d)

    off = n_layers * 3 * d
    grad_b_mod = small_sum[:off].reshape(n_layers, 3 * d)
    per_layer = 2 * d + HG_HEAD_DIM + WIDTH + 3 * WIDTH
    g_ln_g, g_ln_b, g_norm, g_lbs, g_conv = [], [], [], [], []
    for l in range(n_layers):
        seg = small_sum[off + l * per_layer: off + (l + 1) * per_layer]
        g_ln_g.append(seg[:d])
        g_ln_b.append(seg[d:2 * d])
        g_norm.append(seg[2 * d:2 * d + HG_HEAD_DIM])
        g_lbs.append(seg[2 * d + HG_HEAD_DIM:2 * d + HG_HEAD_DIM + WIDTH])
        g_conv.append(seg[2 * d + HG_HEAD_DIM + WIDTH:].reshape(3, WIDTH))
    grad_ln_g, grad_ln_b = jnp.stack(g_ln_g), jnp.stack(g_ln_b)
    grad_norm = jnp.stack(g_norm)
    _, lbs_vjp = jax.vjp(_lower_bound_table, lower_bounds)
    grad_lower = lbs_vjp(jnp.stack(g_lbs))[0]
    grad_conv = lax.dynamic_slice_in_dim(jnp.stack(g_conv), me * cw_cols, cw_cols, axis=2)

    dmod_mine = lax.dynamic_slice_in_dim(dmod_all, me * cm, cm, axis=2).transpose(1, 0, 2)
    grad_w_mod = _wmod_grad(c_all.T, dmod_mine)

    p_in, p_branch, p_out = [None] * n_layers, [None] * n_layers, [None] * n_layers
    for l in reversed(range(n_layers)):
        in_started, rest_started = scattering[l]
        p_branch_l, p_out[l] = _exchange_wait(f"scatter_rest_{l}_wait", rest_started, grad_w_mod, rest_scatter)[1]
        p_branch[l] = p_branch_l.reshape(NDEV, 3 * WIDTH, dsh)
        p_in[l] = _exchange_wait(f"scatter_in_{l}_chips_wait", in_started, grad_w_mod, chip_sum_sends)[1][0]

    def adam(parts_list, w, m, v, name):
        shape = w.shape
        cols = shape[-1]
        flat = lambda a: a.reshape(-1, cols)
        outs = _sum_adamw(parts_list, flat(w), flat(m), flat(v), name)
        return [o.reshape(shape) for o in outs]

    r_w_in = adam(p_in, w_in, m_w_in, v_w_in, "adamw_w_in")
    r_w_branch = adam(p_branch, w_branch, m_w_branch, v_w_branch, "adamw_w_branch")
    r_w_out = adam(p_out, w_out, m_w_out, v_w_out, "adamw_w_out")
    r_w_mod = adam([grad_w_mod.reshape(1, -1, cm)], w_mod, m_w_mod, v_w_mod, "adamw_w_mod")

    small_names = ["b_mod", "conv_w", "hgrn_norm_w", "lower_bounds", "ln_g", "ln_b"]
    small_g = [grad_b_mod, grad_conv, grad_norm, grad_lower, grad_ln_g, grad_ln_b]
    small_w = [b_mod, conv_w, hgrn_norm_w, lower_bounds, ln_g, ln_b]
    small_m = [m_b_mod, m_conv_w, m_hgrn_norm_w, m_lower_bounds, m_ln_g, m_ln_b]
    small_v = [v_b_mod, v_conv_w, v_hgrn_norm_w, v_lower_bounds, v_ln_g, v_ln_b]
    pack = lambda arrs: _pad_rows(jnp.concatenate([a.reshape(-1) for a in arrs]), BLK)
    packed = _sum_adamw([pack(small_g)[None]], pack(small_w), pack(small_m), pack(small_v), "adamw_small")
    r_small = {n: [] for n in small_names}
    for res in packed:
        flat = res.reshape(-1)
        pos = 0
        for n, w in zip(small_names, small_w):
            r_small[n].append(flat[pos:pos + w.size].reshape(w.shape))
            pos += w.size

    results = {"w_mod": r_w_mod, "w_in": r_w_in, "w_branch": r_w_branch, "w_out": r_w_out, **r_small}
    order = ["w_mod", "b_mod", "w_in", "conv_w", "hgrn_norm_w", "lower_bounds", "w_branch", "w_out", "ln_g", "ln_b"]
    outs = [loss, grad_x]
    for idx in range(4):
        outs.extend(results[n][idx] for n in order)
    return tuple(outs)
```

```python
import jax
import jax.numpy as jnp
from jax import lax
from jax.experimental import pallas as pl
from jax.experimental.pallas import tpu as pltpu

F32 = jnp.float32
BF16 = jnp.bfloat16
NDEV = 8
N_LAYERS = 2
SB_HEAD_DIM = 64
HG_HEAD_DIM = 128
WIDTH = 512
BLK = 128
LN_EPS = 1e-5
RMS_EPS = 1e-6
ALPHA = (2.0 * N_LAYERS) ** 0.25
ADAM_LR, ADAM_B1, ADAM_B2, ADAM_EPS, ADAM_WD, ADAM_STEP = 0.001, 0.9, 0.999, 1e-08, 0.01, 10
VMEM_LIMIT = 56 * 1024 * 1024
MESH = pl.DeviceIdType.MESH
HG_LEVELS = (64, 32, 16, 8, 4, 2, 1)


def _pcall(body, *, name, out_shape, grid=None, in_specs=None, out_specs=None, scratch_shapes=(),
           semantics=None, aliases=None):
    kwargs = {}
    if grid is not None:
        kwargs["grid"] = grid
    if in_specs is not None:
        kwargs["in_specs"] = in_specs
    if out_specs is not None:
        kwargs["out_specs"] = out_specs
    if aliases:
        kwargs["input_output_aliases"] = aliases
    return pl.pallas_call(
        body, name=name, out_shape=out_shape, scratch_shapes=list(scratch_shapes),
        compiler_params=pltpu.CompilerParams(dimension_semantics=semantics, vmem_limit_bytes=VMEM_LIMIT),
        interpret=False, **kwargs)


def _dot(a, b):
    return jnp.dot(a, b, preferred_element_type=F32)


def _dot_nt(a, b):
    return lax.dot_general(a, b, (((1,), (1,)), ((), ())), preferred_element_type=F32)


def _dot_tn(a, b):
    return lax.dot_general(a, b, (((0,), (0,)), ((), ())), preferred_element_type=F32)


def _split3(x):
    x1 = x.astype(BF16)
    r1 = x - x1.astype(F32)
    x2 = r1.astype(BF16)
    r2 = r1 - x2.astype(F32)
    return x1, x2, r2.astype(BF16)


def _split2(x):
    x1 = x.astype(BF16)
    return x1, (x - x1.astype(F32)).astype(BF16)


def _dot_exact_l(m_bf16, x):
    x1, x2, x3 = _split3(x)
    return _dot(m_bf16, x1) + _dot(m_bf16, x2) + _dot(m_bf16, x3)


def _sigmoid(x):
    return 1.0 / (1.0 + jnp.exp(-x))


def _silu_and_grad(x):
    s = _sigmoid(x)
    return x * s, s * (1.0 + x * (1.0 - s))


LOG2E = 1.4426950408889634
MASKED_SCORE = -1e30


def _softplus2_parts(z2):
    minus_abs = lax.bitcast_convert_type(lax.bitcast_convert_type(z2, jnp.int32) | jnp.int32(-2 ** 31), F32)
    e = jnp.exp2(minus_abs)
    sp2 = jnp.maximum(z2, 0.0) + jnp.log2(1.0 + e)
    r = 1.0 / (1.0 + e)
    return sp2, jnp.where(z2 >= 0.0, r, e * r)


def _split2_lanes(x):
    x1 = x.astype(BF16)
    return jnp.concatenate([x1, (x - x1.astype(F32)).astype(BF16)], axis=1)


def _iota2(shape, dim):
    return lax.broadcasted_iota(jnp.int32, shape, dim)


def _standardize(x):
    mu = jnp.mean(x, axis=-1, keepdims=True)
    xc = x - mu
    var = jnp.mean(xc * xc, axis=-1, keepdims=True)
    rstd = lax.rsqrt(var + LN_EPS)
    return xc * rstd, rstd


def _standardize_bwd(xhat, rstd, dxhat):
    m1 = jnp.mean(dxhat, axis=-1, keepdims=True)
    m2 = jnp.mean(dxhat * xhat, axis=-1, keepdims=True)
    return rstd * (dxhat - m1 - xhat * m2)


def _my_index():
    return 4 * lax.axis_index("x") + 2 * lax.axis_index("y") + lax.axis_index("c")


def _exchange(name, ins, out_shapes, transfers, in_vmem):
    n_in, n_out, n_t = len(ins), len(out_shapes), len(transfers)

    def body(*refs):
        in_refs, out_refs = refs[:n_in], refs[n_in:n_in + n_out]
        send_sems, recv_sems, local_sems = refs[n_in + n_out:]
        x, y, c = lax.axis_index("x"), lax.axis_index("y"), lax.axis_index("c")
        me = 4 * x + 2 * y + c
        started = []
        for t, (i, o, src_fn, dst_fn) in enumerate(transfers):
            own = pltpu.make_async_copy(src_fn(in_refs[i], me), dst_fn(out_refs[o], me), local_sems.at[t])
            own.start()
            started.append(own)
        arrivals = []
        for k in range(1, NDEV):
            px = x ^ ((k >> 2) & 1)
            py = y ^ ((k >> 1) & 1)
            pc = c ^ (k & 1)
            peer = 4 * px + 2 * py + pc
            for t, (i, o, src_fn, dst_fn) in enumerate(transfers):
                sem = t * (NDEV - 1) + k - 1
                push = pltpu.make_async_remote_copy(
                    src_ref=src_fn(in_refs[i], peer), dst_ref=dst_fn(out_refs[o], me),
                    send_sem=send_sems.at[sem], recv_sem=recv_sems.at[sem],
                    device_id=(px, py, pc), device_id_type=MESH)
                push.start()
                started.append(push)
                arrivals.append(pltpu.make_async_remote_copy(
                    src_ref=src_fn(in_refs[i], peer), dst_ref=dst_fn(out_refs[o], peer),
                    send_sem=send_sems.at[sem], recv_sem=recv_sems.at[sem],
                    device_id=(px, py, pc), device_id_type=MESH))
        for arrival in arrivals:
            arrival.wait_recv()
        for cp in started[n_t:]:
            cp.wait_send()
        for own in started[:n_t]:
            own.wait()

    space = pltpu.VMEM if in_vmem else pl.ANY
    spec = pl.BlockSpec(memory_space=space)
    return _pcall(
        body, name=name, out_shape=out_shapes,
        in_specs=[spec] * n_in, out_specs=[spec] * n_out,
        scratch_shapes=[pltpu.SemaphoreType.DMA((n_t * (NDEV - 1),)),
                        pltpu.SemaphoreType.DMA((n_t * (NDEV - 1),)),
                        pltpu.SemaphoreType.DMA((n_t,))])(*ins)


def _whole(ref, dev):
    return ref


def _slot(ref, dev):
    return ref.at[dev]


def _all_gather_small(name, v):
    out = _exchange(name, [v], [jax.ShapeDtypeStruct((NDEV,) + v.shape, v.dtype)],
                    [(0, 0, _whole, _slot)], in_vmem=True)
    return out[0]


_HBM_SPEC = pl.BlockSpec(memory_space=pltpu.HBM)
_SEM_SPEC = pl.BlockSpec(memory_space=pltpu.SEMAPHORE)
_DATAFLOW = pltpu.SideEffectType.DATAFLOW_SIDE_EFFECTING


def _peer(x, y, c, k):
    px = x ^ ((k >> 2) & 1)
    py = y ^ ((k >> 1) & 1)
    pc = c ^ (k & 1)
    return (px, py, pc), 4 * px + 2 * py + pc


def _direct_sends(transfers):
    sends = []
    for k in range(1, NDEV):
        for i, o, src_fn, dst_fn in transfers:
            sends.append((k,
                          lambda ins, lands, me, i=i, k=k, src_fn=src_fn: src_fn(ins[i], me ^ k),
                          lambda lands, me, o=o, dst_fn=dst_fn: dst_fn(lands[o], me),
                          lambda lands, me, o=o, k=k, dst_fn=dst_fn: dst_fn(lands[o], me ^ k)))
    return sends


def _exchange_start(name, ins, lands, sends, after=None):
    n_in, n_buf = len(ins), len(ins) + len(lands)
    n_sem = len(sends)

    def body(*refs):
        in_refs, land_refs = refs[:n_in], refs[n_in:n_buf]
        n_skip = n_buf + (0 if after is None else 1)
        send_sems, recv_sems, token = refs[n_skip], refs[n_skip + 1], refs[-1]
        x, y, c = lax.axis_index("x"), lax.axis_index("y"), lax.axis_index("c")
        me = 4 * x + 2 * y + c
        for t, (k, src_fn, dst_fn, _) in enumerate(sends):
            pltpu.make_async_remote_copy(
                src_ref=src_fn(in_refs, land_refs, me), dst_ref=dst_fn(land_refs, me),
                send_sem=send_sems.at[t], recv_sem=recv_sems.at[t],
                device_id=_peer(x, y, c, k)[0], device_id_type=MESH).start()
        token[...] = jnp.zeros_like(token)

    bufs = [pltpu.with_memory_space_constraint(a, pltpu.HBM) for a in list(ins) + list(lands)]
    extra = [] if after is None else [after]
    outs = pl.pallas_call(
        body, name=name,
        out_shape=(pltpu.SemaphoreType.DMA((n_sem,)), pltpu.SemaphoreType.DMA((n_sem,)))
        + tuple(pltpu.HBM(a.shape, a.dtype) for a in bufs) + (jax.ShapeDtypeStruct((8, BLK), F32),),
        in_specs=[_HBM_SPEC] * n_buf + [pl.BlockSpec(memory_space=pl.ANY)] * len(extra),
        out_specs=(_SEM_SPEC, _SEM_SPEC) + (_HBM_SPEC,) * n_buf + (pl.BlockSpec(memory_space=pltpu.VMEM),),
        input_output_aliases={b: 2 + b for b in range(n_buf)},
        compiler_params=pltpu.CompilerParams(has_side_effects=_DATAFLOW),
        interpret=False)(*bufs, *extra)
    return outs[0], outs[1], list(outs[2:2 + n_in]), list(outs[2 + n_in:2 + n_buf]), outs[-1]


def _exchange_wait(name, started, after, sends):
    send_sems, recv_sems, ins, lands, _ = started
    n_in, n_buf = len(ins), len(ins) + len(lands)

    def body(*refs):
        in_refs, land_refs = refs[:n_in], refs[n_in:n_buf]
        send_sems, recv_sems = refs[n_buf], refs[n_buf + 1]
        x, y, c = lax.axis_index("x"), lax.axis_index("y"), lax.axis_index("c")
        me = 4 * x + 2 * y + c
        for t, (k, src_fn, _, rcv_fn) in enumerate(sends):
            cp = pltpu.make_async_remote_copy(
                src_ref=src_fn(in_refs, land_refs, me), dst_ref=rcv_fn(land_refs, me),
                send_sem=send_sems.at[t], recv_sem=recv_sems.at[t],
                device_id=_peer(x, y, c, k)[0], device_id_type=MESH)
            cp.wait_send()
            cp.wait_recv()

    bufs = list(ins) + list(lands)
    outs = pl.pallas_call(
        body, name=name, out_shape=tuple(pltpu.HBM(a.shape, a.dtype) for a in bufs),
        in_specs=[_HBM_SPEC] * n_buf + [_SEM_SPEC, _SEM_SPEC, pl.BlockSpec(memory_space=pl.ANY)],
        out_specs=(_HBM_SPEC,) * n_buf,
        input_output_aliases={b: b for b in range(n_buf)},
        compiler_params=pltpu.CompilerParams(has_side_effects=_DATAFLOW),
        interpret=False)(*bufs, send_sems, recv_sems, after)
    return list(outs[:n_in]), list(outs[n_in:])


def _place_own(shape, dtype, own, start):
    return lax.dynamic_update_slice(lax.empty(shape, dtype), own, start)


def _place_own_window(name, shape, own, me):
    rows, cols = own.shape

    def body(me_ref, zone_in, own_ref, zone_ref):
        del me_ref, zone_in
        zone_ref[...] = own_ref[...]

    return pl.pallas_call(
        body, name=name, out_shape=jax.ShapeDtypeStruct(shape, own.dtype),
        grid_spec=pltpu.PrefetchScalarGridSpec(
            num_scalar_prefetch=1, grid=(1,),
            in_specs=[pl.BlockSpec(memory_space=pl.ANY), pl.BlockSpec((rows, cols), lambda i, me_ref: (0, 0))],
            out_specs=pl.BlockSpec((rows, cols), lambda i, me_ref: (0, me_ref[0]))),
        input_output_aliases={1: 0},
        compiler_params=pltpu.CompilerParams(dimension_semantics=("arbitrary",), vmem_limit_bytes=VMEM_LIMIT),
        interpret=False)(me.reshape(1).astype(jnp.int32), lax.empty(shape, own.dtype), own)


def _mod_fwd(c_all, w_mod, b_mod_mine):
    n_layers, _, cm = w_mod.shape

    def body(c_ref, w_ref, b_ref, o_ref):
        for l in range(n_layers):
            o_ref[l] = jnp.dot(c_ref[...], w_ref[l], preferred_element_type=F32,
                               precision=lax.Precision.HIGHEST) + b_ref[l]

    return _pcall(body, name="mod_fwd", out_shape=jax.ShapeDtypeStruct((n_layers, NDEV, cm), F32))(
        c_all, w_mod, b_mod_mine)


def _ln_proj(x, shift, scale, w_full, name):
    s_len, d = x.shape
    n = w_full.shape[1]
    tm = min(512, s_len)
    tn = 1024

    def body(x_ref, sh_ref, sc_ref, w_ref, proj_ref, ht_ref, h_scr):
        @pl.when(pl.program_id(1) == 0)
        def _():
            xs, _ = _standardize(x_ref[...])
            h = xs * (1.0 + sc_ref[...]) + sh_ref[...]
            h_scr[...] = h.astype(BF16)
            ht_ref[...] = h.T.astype(BF16)

        proj_ref[...] = _dot(h_scr[...], w_ref[...])

    return _pcall(
        body, name=name,
        out_shape=(jax.ShapeDtypeStruct((s_len, n), F32), jax.ShapeDtypeStruct((d, s_len), BF16)),
        grid=(s_len // tm, n // tn),
        in_specs=[pl.BlockSpec((tm, d), lambda i, j: (i, 0)),
                  pl.BlockSpec((1, d), lambda i, j: (0, 0)),
                  pl.BlockSpec((1, d), lambda i, j: (0, 0)),
                  pl.BlockSpec((d, tn), lambda i, j: (0, j))],
        out_specs=(pl.BlockSpec((tm, tn), lambda i, j: (i, j)),
                   pl.BlockSpec((d, tm), lambda i, j: (0, i))),
        scratch_shapes=[pltpu.VMEM((tm, d), BF16)],
        semantics=("arbitrary", "arbitrary"))(x, shift, scale, w_full)


def _sb_group_blocks(nb):
    return min(4, nb)


def _sb_fwd(proj, name):
    s_len = proj.shape[0]
    nb = s_len // BLK
    n_pairs = WIDTH // BLK
    gb = _sb_group_blocks(nb)
    kw = gb * BLK

    def body(q_ref, k_ref, v_ref, o_ref, tot_ref):
        lane = _iota2((1, BLK), 1)
        row = _iota2((BLK, BLK), 0)
        col = _iota2((BLK, BLK), 1)
        half = jnp.concatenate([(row >= col).astype(BF16), jnp.ones((BLK, BLK), BF16)], axis=1)
        suffix_and_sum = jnp.concatenate([half, half], axis=0)
        qpos = _iota2((BLK, kw), 0)
        kpos = _iota2((BLK, kw), 1)
        head_lanes = [(lane // SB_HEAD_DIM) == hh for hh in range(2)]

        def scores(i, gi, qms, masked):
            c0 = pl.multiple_of(gi * kw, kw)
            kb = k_ref[pl.ds(c0, kw), :].astype(BF16)
            z2s = [_dot_nt(qms[hh], kb) for hh in range(2)]
            if masked:
                valid = (c0 + kpos) < (i * BLK + qpos)
                z2s = [jnp.where(valid, z2, MASKED_SCORE) for z2 in z2s]
            return tuple(z2s)

        def accumulate(gi, z2s, carry):
            c0 = pl.multiple_of(gi * kw, kw)
            vf = v_ref[pl.ds(c0, kw), :]
            sp2s = [_softplus2_parts(z2)[0] for z2 in z2s]
            terms = [[_split2_lanes(sp2[:, b * BLK:(b + 1) * BLK]) for b in range(gb)] for sp2 in sp2s]
            sums = [[_dot(t, suffix_and_sum) for t in head_terms] for head_terms in terms]
            weights, laters = [], []
            for hh in range(2):
                later = carry[2 * hh + 1]
                parts = [None] * gb
                for b in reversed(range(gb)):
                    parts[b] = sums[hh][b][:, :BLK] + later
                    later = later + sums[hh][b][:, BLK:]
                weights.append(jnp.exp2(z2s[hh] - jnp.concatenate(parts, axis=1)).astype(BF16))
                laters.append(later)
            outs = [_dot(weights[hh], jnp.where(head_lanes[hh], vf, 0.0).astype(BF16)) for hh in range(2)]
            return (carry[0] + outs[0], laters[0], carry[2] + outs[1], laters[1])

        def queries(i):
            qf = q_ref[pl.ds(pl.multiple_of(i * BLK, BLK), BLK), :] * (SB_HEAD_DIM ** -0.5 * LOG2E)
            return [jnp.where(head_lanes[hh], qf, 0.0).astype(BF16) for hh in range(2)]

        def qblock(i, first_scores):
            r0 = pl.multiple_of(i * BLK, BLK)
            qms = queries(i)
            zero = jnp.zeros((BLK, BLK), F32)
            last = i // gb

            def step(jj, state):
                gi = last - 1 - jj
                return scores(i, gi, qms, False) + accumulate(gi + 1, state[:2], state[2:])

            state = lax.fori_loop(0, last, step, first_scores + (zero,) * 4)
            nxt = jnp.minimum(i + 1, nb - 1)
            next_scores = scores(nxt, nxt // gb, queries(nxt), True)
            carry = accumulate(0, state[:2], state[2:])
            o_ref[pl.ds(r0, BLK), :] = carry[0] + carry[2]
            tot_ref[0, pl.ds(r0, BLK), :] = carry[1]
            tot_ref[1, pl.ds(r0, BLK), :] = carry[3]
            return next_scores

        lax.fori_loop(0, nb, qblock, scores(0, 0, queries(0), True))

    col_spec = lambda off: pl.BlockSpec((s_len, BLK), lambda p: (0, off + p))
    return _pcall(
        body, name=name,
        out_shape=(jax.ShapeDtypeStruct((s_len, WIDTH), F32),
                   jax.ShapeDtypeStruct((2 * n_pairs, s_len, BLK), F32)),
        grid=(n_pairs,),
        in_specs=[col_spec(0), col_spec(n_pairs), col_spec(2 * n_pairs)],
        out_specs=(pl.BlockSpec((s_len, BLK), lambda p: (0, p)),
                   pl.BlockSpec((2, s_len, BLK), lambda p: (p, 0, 0))),
        semantics=("arbitrary",))(proj, proj, proj)


def _hg_masks(mask_ref):
    row = _iota2((BLK, BLK), 0)
    col = _iota2((BLK, BLK), 1)
    for v, m in enumerate(HG_LEVELS):
        same = (row // (2 * m)) == (col // (2 * m))
        mask_ref[v] = (same & ((row & m) != 0) & ((col & m) == 0)).astype(F32)


def _hg_mid(b, m):
    if m >= 4:
        n = BLK // (2 * m)
        mid = b.reshape(n, 2 * m, BLK)[:, m - 1:m, :]
        return jnp.broadcast_to(mid, (n, 2 * m, BLK)).reshape(BLK, BLK)
    pos = _iota2((BLK, BLK), 0) & (2 * m - 1)
    out = b
    for p in range(2 * m):
        delta = (m - 1) - p
        if delta != 0:
            out = jnp.where(pos == p, pltpu.roll(b, (-delta) % BLK, 0), out)
    return out


def _hg_chunk_inputs(qraw, fpre, lb):
    sig = _sigmoid(fpre)
    f = lb + (1.0 - lb) * sig
    g = jnp.log(f)
    q, dq_fac = _silu_and_grad(qraw)
    return q, dq_fac, f, sig, g


HG_GROUP = 4


def _neg_abs(x):
    return lax.bitcast_convert_type(lax.bitcast_convert_type(x, jnp.int32) | jnp.int32(-2 ** 31), F32)


def _hg_level_terms(qs, ks, bs, m):
    es = [jnp.exp(_neg_abs(b - _hg_mid(b, m))) for b in bs]
    qts = [(q * e).astype(BF16) for q, e in zip(qs, es)]
    kts = [(k * e).astype(BF16) for k, e in zip(ks, es)]
    return es, qts, kts


def _hg_load(refs, r0, lb_v, lower_incl):
    q_ref, f_ref, i_ref = refs
    heads = []
    for h in range(HG_GROUP):
        sl = slice(h * HG_HEAD_DIM, (h + 1) * HG_HEAD_DIM)
        heads.append(_hg_chunk_inputs(q_ref[pl.ds(r0, BLK), sl], f_ref[pl.ds(r0, BLK), sl], lb_v[:, sl])
                     + (i_ref[pl.ds(r0, BLK), sl],))
    bs = [_dot_exact_l(lower_incl, hd[4]) for hd in heads]
    return heads, bs


def _hgrn_fwd(proj, lb, name):
    s_len = proj.shape[0]
    nc = s_len // BLK
    gw = HG_GROUP * HG_HEAD_DIM
    n_groups = WIDTH // gw
    base = 4 * WIDTH // gw

    def body(q_ref, f_ref, i_ref, lb_ref, o_ref, mask_ref):
        _hg_masks(mask_ref)
        row = _iota2((BLK, BLK), 0)
        col = _iota2((BLK, BLK), 1)
        lower_incl = (col <= row).astype(BF16)
        lb_v = lb_ref[...]

        def chunk(ci, sts):
            r0 = pl.multiple_of(ci * BLK, BLK)
            heads, bs = _hg_load((q_ref, f_ref, i_ref), r0, lb_v, lower_incl)
            qs = [hd[0] for hd in heads]
            ks = [1.0 - hd[2] for hd in heads]
            vs = [hd[5] for hd in heads]
            vbs = [v.astype(BF16) for v in vs]
            b_ends = [b[BLK - 1:BLK, :] for b in bs]
            inters = [_dot_nt((q * jnp.exp(b)).astype(BF16), st.astype(BF16)) for q, b, st in zip(qs, bs, sts)]
            scs = [None] * HG_GROUP
            for v_idx, m in enumerate(HG_LEVELS):
                _, qts, kts = _hg_level_terms(qs, ks, bs, m)
                terms = [_dot_nt(qt, kt) for qt, kt in zip(qts, kts)]
                msk = mask_ref[v_idx]
                scs = [t * msk if sc is None else sc + t * msk for sc, t in zip(scs, terms)]
            intras = [_dot(sc.astype(BF16), vb) for sc, vb in zip(scs, vbs)]
            k_decs = [(k * jnp.exp(b_end - b)).astype(BF16) for k, b, b_end in zip(ks, bs, b_ends)]
            grown = [_dot_tn(vb, k_dec) for vb, k_dec in zip(vbs, k_decs)]
            for h in range(HG_GROUP):
                diag = jnp.sum(qs[h] * ks[h], axis=-1, keepdims=True)
                o_ref[pl.ds(r0, BLK), h * HG_HEAD_DIM:(h + 1) * HG_HEAD_DIM] = inters[h] + intras[h] + diag * vs[h]
            return tuple(st * jnp.exp(b_end) + g for st, b_end, g in zip(sts, b_ends, grown))

        lax.fori_loop(0, nc, chunk, (jnp.zeros((HG_HEAD_DIM, HG_HEAD_DIM), F32),) * HG_GROUP)

    col_spec = lambda off: pl.BlockSpec((s_len, gw), lambda h: (0, off + h))
    return _pcall(
        body, name=name, out_shape=jax.ShapeDtypeStruct((s_len, WIDTH), F32),
        grid=(n_groups,),
        in_specs=[col_spec(base), col_spec(base + n_groups), col_spec(base + 2 * n_groups),
                  pl.BlockSpec((1, gw), lambda h: (0, h))],
        out_specs=pl.BlockSpec((s_len, gw), lambda h: (0, h)),
        scratch_shapes=[pltpu.VMEM((len(HG_LEVELS), BLK, BLK), F32)],
        semantics=("arbitrary",))(proj, proj, proj, lb)


def _rms_heads(o_b, norm_w):
    n_parts, h_parts, r_parts = [], [], []
    for h in range(WIDTH // HG_HEAD_DIM):
        sl = slice(h * HG_HEAD_DIM, (h + 1) * HG_HEAD_DIM)
        o = o_b[:, sl]
        rstd = lax.rsqrt(jnp.mean(o * o, axis=-1, keepdims=True) + RMS_EPS)
        ohat = o * rstd
        h_parts.append(ohat)
        n_parts.append(ohat * norm_w[:, sl])
        r_parts.append(jnp.broadcast_to(rstd, o.shape))
    cat = lambda parts: jnp.concatenate(parts, axis=-1)
    return cat(n_parts), cat(h_parts), cat(r_parts)


def _shift_rows_down(halo, cur, k):
    tm = cur.shape[0]
    ext = jnp.concatenate([halo, cur], axis=0)
    return pltpu.roll(ext, k, 0)[8:8 + tm]


def _shift_rows_up(cur, halo, k):
    tm = cur.shape[0]
    ext = jnp.concatenate([cur, halo], axis=0)
    return pltpu.roll(ext, (tm + 8 - k) % (tm + 8), 0)[0:tm]


def _merge_fwd(x, proj, o_a, o_b, gate, norm_w, conv_w, wb, w_out, ln_g, ln_b, name):
    s_len, d = x.shape
    tm = min(256, s_len)
    hb = tm // 8

    def body(x_ref, oa_ref, za_ref, ob_ref, zb_ref, pre_ref, post_ref, u_ref, zc_ref, hpre_ref, hu_ref, g_ref,
             gate_ref, nw_ref, cw_ref, wb_ref, wo_ref, lg_ref, lbias_ref, xn_ref, mg_ref, yc_ref):
        i = pl.program_id(0)
        sa, _ = _silu_and_grad(za_ref[...])
        y_a = (oa_ref[...] * sa).astype(BF16)
        n_b, _, _ = _rms_heads(ob_ref[...], nw_ref[...])
        sb, _ = _silu_and_grad(zb_ref[...])
        y_b = (n_b * sb).astype(BF16)
        a = pre_ref[...] * u_ref[...]
        halo = jnp.where(i > 0, hpre_ref[...] * hu_ref[...], 0.0)
        cw = cw_ref[...]
        conv = cw[0:1] * _shift_rows_down(halo, a, 2) + cw[1:2] * _shift_rows_down(halo, a, 1) + cw[2:3] * a
        sc, _ = _silu_and_grad(zc_ref[...])
        y_c = (post_ref[...] * conv * sc).astype(BF16)
        merged = None
        for k, yk in enumerate((y_a, y_b, y_c)):
            yc_ref[:, k * WIDTH:(k + 1) * WIDTH] = yk
            term = _sigmoid(g_ref[:, k * d:(k + 1) * d]) * _dot(yk, wb_ref[k])
            merged = term if merged is None else merged + term
        mb = merged.astype(BF16)
        mg_ref[...] = mb
        y = _dot(mb, wo_ref[...])
        r = ALPHA * x_ref[...] + (1.0 + gate_ref[...]) * y
        rhat, _ = _standardize(r)
        xn_ref[...] = rhat * lg_ref[...] + lbias_ref[...]

    wcol = lambda cb: pl.BlockSpec((tm, WIDTH), lambda i: (i, cb))
    halo_spec = lambda cb: pl.BlockSpec((8, WIDTH), lambda i: (jnp.maximum(i * hb - 1, 0), cb))
    vec = lambda w: pl.BlockSpec((1, w), lambda i: (0, 0))
    return _pcall(
        body, name=name,
        out_shape=(jax.ShapeDtypeStruct((s_len, d), F32), jax.ShapeDtypeStruct((s_len, d), BF16),
                   jax.ShapeDtypeStruct((s_len, 3 * WIDTH), BF16)),
        grid=(s_len // tm,),
        in_specs=[pl.BlockSpec((tm, d), lambda i: (i, 0)),
                  wcol(0), wcol(3), wcol(0), wcol(7), wcol(8), wcol(9), wcol(10), wcol(11),
                  halo_spec(8), halo_spec(10),
                  pl.BlockSpec((tm, 3 * d), lambda i: (i, 2)),
                  vec(d), vec(WIDTH),
                  pl.BlockSpec((3, WIDTH), lambda i: (0, 0)),
                  pl.BlockSpec((3, WIDTH, d), lambda i: (0, 0, 0)),
                  pl.BlockSpec((d, d), lambda i: (0, 0)),
                  vec(d), vec(d)],
        out_specs=(pl.BlockSpec((tm, d), lambda i: (i, 0)), pl.BlockSpec((tm, d), lambda i: (i, 0)),
                   pl.BlockSpec((tm, 3 * WIDTH), lambda i: (i, 0))),
        semantics=("arbitrary",))(x, o_a, proj, o_b, proj, proj, proj, proj, proj, proj, proj, proj,
                                  gate, norm_w, conv_w, wb, w_out, ln_g, ln_b)


def _loss_fwd_bwd(y, target):
    s_len, d = y.shape
    tm = min(512, s_len)

    def body(y_ref, t_ref, loss_ref, dy_ref):
        @pl.when(pl.program_id(0) == 0)
        def _():
            loss_ref[...] = jnp.zeros_like(loss_ref)

        e = y_ref[...] - t_ref[...]
        dy_ref[...] = e * (1.0 / d)
        part = jnp.sum(jnp.sum(e * e, axis=-1, keepdims=True), axis=0, keepdims=True)
        loss_ref[...] += part * (0.5 / d)

    tile = pl.BlockSpec((tm, d), lambda i: (i, 0))
    return _pcall(body, name="loss", grid=(s_len // tm,),
                  out_shape=(jax.ShapeDtypeStruct((1, 1), F32), jax.ShapeDtypeStruct((s_len, d), F32)),
                  in_specs=[tile, tile],
                  out_specs=(pl.BlockSpec((1, 1), lambda i: (0, 0)), tile),
                  semantics=("arbitrary",))(y, target)


def _merge_bwd(dxn, x, merged, ycat, proj, gate, wb, w_out, ln_g, name):
    s_len, d = x.shape
    tm = min(256, s_len)

    def body(dxn_ref, x_ref, mg_ref, yc_ref, g_ref, gate_ref, wb_ref, wo_ref, lg_ref,
             dres_ref, dyc_ref, dg_ref, gwo_ref, gwb_ref, vec_ref):
        @pl.when(pl.program_id(0) == 0)
        def _():
            gwo_ref[...] = jnp.zeros_like(gwo_ref)
            gwb_ref[...] = jnp.zeros_like(gwb_ref)
            vec_ref[...] = jnp.zeros_like(vec_ref)

        mb = mg_ref[...]
        one_gate = 1.0 + gate_ref[...]
        y = _dot(mb, wo_ref[...])
        r = ALPHA * x_ref[...] + one_gate * y
        rhat, rstd = _standardize(r)
        dxn = dxn_ref[...]
        dr = _standardize_bwd(rhat, rstd, dxn * lg_ref[...])
        vec_ref[0:1, :] += jnp.sum(dxn * rhat, axis=0, keepdims=True)
        vec_ref[1:2, :] += jnp.sum(dxn, axis=0, keepdims=True)
        vec_ref[2:3, :] += jnp.sum(dr * y, axis=0, keepdims=True)
        dres_ref[...] = ALPHA * dr
        dy = (one_gate * dr).astype(BF16)
        gwo_ref[...] += _dot_tn(mb, dy)
        dmerged = _dot_nt(dy, wo_ref[...])
        for k in range(3):
            yk = yc_ref[:, k * WIDTH:(k + 1) * WIDTH]
            sg = _sigmoid(g_ref[:, k * d:(k + 1) * d])
            pk = _dot(yk, wb_ref[k])
            dg_ref[:, k * d:(k + 1) * d] = (dmerged * pk * sg * (1.0 - sg)).astype(BF16)
            dpk = (dmerged * sg).astype(BF16)
            dyc_ref[:, k * WIDTH:(k + 1) * WIDTH] = _dot_nt(dpk, wb_ref[k])
            gwb_ref[k] += _dot_tn(yk, dpk)

    tile = lambda w: pl.BlockSpec((tm, w), lambda i: (i, 0))
    vec = pl.BlockSpec((1, d), lambda i: (0, 0))
    return _pcall(
        body, name=name,
        out_shape=(jax.ShapeDtypeStruct((s_len, d), F32), jax.ShapeDtypeStruct((s_len, 3 * WIDTH), F32),
                   jax.ShapeDtypeStruct(proj.shape, BF16), jax.ShapeDtypeStruct((d, d), F32),
                   jax.ShapeDtypeStruct((3, WIDTH, d), F32), jax.ShapeDtypeStruct((8, d), F32)),
        grid=(s_len // tm,),
        in_specs=[tile(d), tile(d), tile(d), tile(3 * WIDTH),
                  pl.BlockSpec((tm, 3 * d), lambda i: (i, 2)),
                  vec, pl.BlockSpec((3, WIDTH, d), lambda i: (0, 0, 0)),
                  pl.BlockSpec((d, d), lambda i: (0, 0)), vec],
        out_specs=(tile(d), tile(3 * WIDTH), pl.BlockSpec((tm, 3 * d), lambda i: (i, 2)),
                   pl.BlockSpec((d, d), lambda i: (0, 0)),
                   pl.BlockSpec((3, WIDTH, d), lambda i: (0, 0, 0)),
                   pl.BlockSpec((8, d), lambda i: (0, 0))),
        semantics=("arbitrary",))(dxn, x, merged, ycat, proj, gate, wb, w_out, ln_g)


def _branch_bwd(dycat, proj, o_a, o_b, norm_w, conv_w, dproj, name):
    s_len = proj.shape[0]
    tm = min(256, s_len)
    hb = tm // 8
    n_tiles = s_len // tm

    def body(dya_ref, dyb_ref, dyc_ref, oa_ref, za_ref, ob_ref, zb_ref, pre_ref, post_ref, u_ref, zc_ref,
             hpre_ref, hu_ref, ndyc_ref, npost_ref, nzc_ref, nw_ref, cw_ref, dproj_in,
             dproj_ref, doa_ref, dob_ref, vec_ref, dza_scr, dzb_scr, dc_scr, sems):
        del dproj_in
        i = pl.program_id(0)

        @pl.when(i == 0)
        def _():
            vec_ref[...] = jnp.zeros_like(vec_ref)

        sa, dsa = _silu_and_grad(za_ref[...])
        dya = dya_ref[...]
        doa_ref[...] = dya * sa
        dza_scr[...] = (dya * oa_ref[...] * dsa).astype(BF16)
        nw = nw_ref[...]
        n_b, ohat, rstd = _rms_heads(ob_ref[...], nw)
        sb, dsb = _silu_and_grad(zb_ref[...])
        dyb = dyb_ref[...]
        dzb_scr[...] = (dyb * n_b * dsb).astype(BF16)
        dn = dyb * sb
        vec_ref[0:1, :] += jnp.sum(dn * ohat, axis=0, keepdims=True)
        dnw = dn * nw
        parts = []
        for h in range(WIDTH // HG_HEAD_DIM):
            sl = slice(h * HG_HEAD_DIM, (h + 1) * HG_HEAD_DIM)
            m2 = jnp.mean(dnw[:, sl] * ohat[:, sl], axis=-1, keepdims=True)
            parts.append(rstd[:, sl] * (dnw[:, sl] - ohat[:, sl] * m2))
        dob_ref[...] = jnp.concatenate(parts, axis=-1)
        cw = cw_ref[...]
        pre, u, post = pre_ref[...], u_ref[...], post_ref[...]
        a = pre * u
        halo = jnp.where(i > 0, hpre_ref[...] * hu_ref[...], 0.0)
        a1 = _shift_rows_down(halo, a, 1)
        a2 = _shift_rows_down(halo, a, 2)
        conv = cw[0:1] * a2 + cw[1:2] * a1 + cw[2:3] * a
        sc, dsc = _silu_and_grad(zc_ref[...])
        dyc = dyc_ref[...]
        dconv = dyc * post * sc
        nsc, _ = _silu_and_grad(nzc_ref[...])
        nxt = jnp.where(i < n_tiles - 1, ndyc_ref[...] * npost_ref[...] * nsc, 0.0)
        da = cw[2:3] * dconv + cw[1:2] * _shift_rows_up(dconv, nxt, 1) + cw[0:1] * _shift_rows_up(dconv, nxt, 2)
        dc_scr[:, 0 * WIDTH:1 * WIDTH] = (da * u).astype(BF16)
        dc_scr[:, 1 * WIDTH:2 * WIDTH] = (dyc * conv * sc).astype(BF16)
        dc_scr[:, 2 * WIDTH:3 * WIDTH] = (da * pre).astype(BF16)
        dc_scr[:, 3 * WIDTH:4 * WIDTH] = (dyc * post * conv * dsc).astype(BF16)
        vec_ref[1:2, :] += jnp.sum(dconv * a2, axis=0, keepdims=True)
        vec_ref[2:3, :] += jnp.sum(dconv * a1, axis=0, keepdims=True)
        vec_ref[3:4, :] += jnp.sum(dconv * a, axis=0, keepdims=True)
        rows = pl.ds(pl.multiple_of(i * tm, tm), tm)
        copies = [pltpu.make_async_copy(dza_scr, dproj_ref.at[rows, 3 * WIDTH:4 * WIDTH], sems.at[0]),
                  pltpu.make_async_copy(dzb_scr, dproj_ref.at[rows, 7 * WIDTH:8 * WIDTH], sems.at[1]),
                  pltpu.make_async_copy(dc_scr, dproj_ref.at[rows, 8 * WIDTH:12 * WIDTH], sems.at[2])]
        for cp in copies:
            cp.start()
        for cp in copies:
            cp.wait()

    wcol = lambda cb: pl.BlockSpec((tm, WIDTH), lambda i: (i, cb))
    prev = lambda cb: pl.BlockSpec((8, WIDTH), lambda i: (jnp.maximum(i * hb - 1, 0), cb))
    nxt = lambda cb: pl.BlockSpec((8, WIDTH), lambda i: (jnp.minimum((i + 1) * hb, s_len // 8 - 1), cb))
    anyspec = pl.BlockSpec(memory_space=pl.ANY)
    out = jax.ShapeDtypeStruct((s_len, WIDTH), F32)
    return _pcall(
        body, name=name,
        out_shape=(jax.ShapeDtypeStruct(dproj.shape, dproj.dtype), out, out, jax.ShapeDtypeStruct((8, WIDTH), F32)),
        grid=(n_tiles,),
        in_specs=[wcol(0), wcol(1), wcol(2), wcol(0), wcol(3), wcol(0), wcol(7), wcol(8), wcol(9), wcol(10), wcol(11),
                  prev(8), prev(10), nxt(2), nxt(9), nxt(11),
                  pl.BlockSpec((1, WIDTH), lambda i: (0, 0)), pl.BlockSpec((3, WIDTH), lambda i: (0, 0)), anyspec],
        out_specs=(anyspec, wcol(0), wcol(0), pl.BlockSpec((8, WIDTH), lambda i: (0, 0))),
        scratch_shapes=[pltpu.VMEM((tm, WIDTH), BF16), pltpu.VMEM((tm, WIDTH), BF16),
                        pltpu.VMEM((tm, 4 * WIDTH), BF16), pltpu.SemaphoreType.DMA((3,))],
        aliases={18: 0},
        semantics=("arbitrary",))(dycat, dycat, dycat, o_a, proj, o_b, proj, proj, proj, proj, proj,
                                  proj, proj, dycat, proj, proj, norm_w, conv_w, dproj)


def _sb_bwd(proj, do_a, totals, dproj, name):
    s_len = proj.shape[0]
    nb = s_len // BLK
    n_pairs = WIDTH // BLK
    scale = SB_HEAD_DIM ** -0.5
    gb = _sb_group_blocks(nb)
    kw = gb * BLK

    def body(q_ref, k_ref, v_ref, do_ref, tot_ref, dproj_in, dproj_ref, dq_ref, dk_ref, dv_ref, out_scr, sems):
        del dproj_in
        lane = _iota2((1, BLK), 1)
        row = _iota2((BLK, BLK), 0)
        col = _iota2((BLK, BLK), 1)
        ones = jnp.ones((BLK, BLK), BF16)
        twice = lambda m: jnp.concatenate([m, m], axis=0)
        before_and_sum = twice(jnp.concatenate([(row < col).astype(BF16), ones], axis=1))
        upto_and_sum = twice(jnp.concatenate([(row <= col).astype(BF16), ones], axis=1))
        qpos = _iota2((BLK, kw), 0)
        kpos = _iota2((BLK, kw), 1)
        head_lanes = [(lane // SB_HEAD_DIM) == hh for hh in range(2)]
        dk_ref[...] = jnp.zeros_like(dk_ref)
        dv_ref[...] = jnp.zeros_like(dv_ref)

        causal = kpos - qpos

        def scores(i, gi, qms):
            c0 = pl.multiple_of(gi * kw, kw)
            kb = k_ref[pl.ds(c0, kw), :].astype(BF16)
            valid = causal < i * BLK - c0
            return tuple(jnp.where(valid, _dot_nt(qms[hh], kb), MASKED_SCORE) for hh in range(2))

        def process(gi, z2s, qms, doms, totals_i, carry):
            c0 = pl.multiple_of(gi * kw, kw)
            kf = k_ref[pl.ds(c0, kw), :]
            vf = v_ref[pl.ds(c0, kw), :]
            kms = [jnp.where(head_lanes[hh], kf, 0.0).astype(BF16) for hh in range(2)]
            vms = [jnp.where(head_lanes[hh], vf, 0.0).astype(BF16) for hh in range(2)]
            das = [_dot_nt(doms[hh], vms[hh]) for hh in range(2)]
            halves = [_softplus2_parts(z2) for z2 in z2s]
            terms = [[_split2_lanes(sp2[:, b * BLK:(b + 1) * BLK]) for b in range(gb)] for sp2, _ in halves]
            sums = [[_dot(t, before_and_sum) for t in head_terms] for head_terms in terms]
            weights, gmats, l_befores = [], [], []
            for hh in range(2):
                l_before = carry[3 * hh + 1]
                parts = []
                for b in range(gb):
                    parts.append(totals_i[hh] - l_before - sums[hh][b][:, :BLK])
                    l_before = l_before + sums[hh][b][:, BLK:]
                a = jnp.exp2(z2s[hh] - jnp.concatenate(parts, axis=1))
                weights.append(a.astype(BF16))
                gmats.append(a * das[hh])
                l_befores.append(l_before)
            terms = [[_split2_lanes(g[:, b * BLK:(b + 1) * BLK]) for b in range(gb)] for g in gmats]
            sums = [[_dot(t, upto_and_sum) for t in head_terms] for head_terms in terms]
            dzs, g_befores = [], []
            for hh in range(2):
                g_before = carry[3 * hh + 2]
                parts = []
                for b in range(gb):
                    parts.append(g_before + sums[hh][b][:, :BLK])
                    g_before = g_before + sums[hh][b][:, BLK:]
                dzs.append((gmats[hh] - halves[hh][1] * jnp.concatenate(parts, axis=1)).astype(BF16))
                g_befores.append(g_before)
            dks = [_dot_tn(dzs[hh], qms[hh]) for hh in range(2)]
            dvs = [_dot_tn(weights[hh], doms[hh]) for hh in range(2)]
            dqs = [_dot(dzs[hh], kms[hh]) for hh in range(2)]
            dk_ref[pl.ds(c0, kw), :] += (dks[0] + dks[1]) * (1.0 / LOG2E)
            dv_ref[pl.ds(c0, kw), :] += dvs[0] + dvs[1]
            return (carry[0] + dqs[0], l_befores[0], g_befores[0], carry[3] + dqs[1], l_befores[1], g_befores[1])

        def queries(i):
            qf = q_ref[pl.ds(pl.multiple_of(i * BLK, BLK), BLK), :] * (scale * LOG2E)
            return [jnp.where(head_lanes[hh], qf, 0.0).astype(BF16) for hh in range(2)]

        def qblock(i, first_scores):
            r0 = pl.multiple_of(i * BLK, BLK)
            qms = queries(i)
            dof = do_ref[pl.ds(r0, BLK), :]
            doms = [jnp.where(head_lanes[hh], dof, 0.0).astype(BF16) for hh in range(2)]
            totals_i = [tot_ref[hh, pl.ds(r0, BLK), :] for hh in range(2)]
            zero = jnp.zeros((BLK, BLK), F32)
            last = i // gb

            def step(gi, state):
                return scores(i, gi + 1, qms) + process(gi, state[:2], qms, doms, totals_i, state[2:])

            state = lax.fori_loop(0, last, step, first_scores + (zero,) * 6)
            nxt = jnp.minimum(i + 1, nb - 1)
            next_scores = scores(nxt, 0, queries(nxt))
            carry = process(last, state[:2], qms, doms, totals_i, state[2:])
            dq_ref[pl.ds(r0, BLK), :] = (carry[0] + carry[3]) * scale
            return next_scores

        lax.fori_loop(0, nb, qblock, scores(0, 0, queries(0)))
        pair = pl.program_id(0)
        copies = []
        for t, ref in enumerate((dq_ref, dk_ref, dv_ref)):
            out_scr[t] = ref[...].astype(BF16)
            col = pl.multiple_of((t * n_pairs + pair) * BLK, BLK)
            copies.append(pltpu.make_async_copy(out_scr.at[t], dproj_ref.at[:, pl.ds(col, BLK)], sems.at[t]))
            copies[-1].start()
        for cp in copies:
            cp.wait()

    col_spec = lambda off: pl.BlockSpec((s_len, BLK), lambda p: (0, off + p))
    anyspec = pl.BlockSpec(memory_space=pl.ANY)
    return _pcall(
        body, name=name, out_shape=jax.ShapeDtypeStruct(dproj.shape, dproj.dtype), grid=(n_pairs,),
        in_specs=[col_spec(0), col_spec(n_pairs), col_spec(2 * n_pairs), col_spec(0),
                  pl.BlockSpec((2, s_len, BLK), lambda p: (p, 0, 0)), anyspec],
        out_specs=anyspec,
        scratch_shapes=[pltpu.VMEM((s_len, BLK), F32)] * 3 + [pltpu.VMEM((3, s_len, BLK), BF16),
                                                              pltpu.SemaphoreType.DMA((3,))],
        aliases={5: 0},
        semantics=("arbitrary",))(proj, proj, proj, do_a, totals, dproj)


def _hgrn_bwd(proj, do_b, lb, dproj, name):
    s_len = proj.shape[0]
    nc = s_len // BLK
    gw = HG_GROUP * HG_HEAD_DIM
    n_groups = WIDTH // gw
    base = 4 * WIDTH // gw
    heads_of = range(HG_GROUP)

    def body(q_ref, f_ref, i_ref, do_ref, lb_ref, dproj_in, dproj_ref, dlb_ref, mask_ref, st_ref, out_scr, sems):
        del dproj_in
        _hg_masks(mask_ref)
        row = _iota2((BLK, BLK), 0)
        col = _iota2((BLK, BLK), 1)
        lower_incl = (col <= row).astype(BF16)
        upper_incl = (col >= row).astype(BF16)
        lb_v = lb_ref[...]
        refs = (q_ref, f_ref, i_ref)

        def fwd_chunk(ci, sts):
            for h in heads_of:
                st_ref[ci, h] = sts[h]
            heads, bs = _hg_load(refs, pl.multiple_of(ci * BLK, BLK), lb_v, lower_incl)
            b_ends = [b[BLK - 1:BLK, :] for b in bs]
            k_decs = [((1.0 - hd[2]) * jnp.exp(b_end - b)).astype(BF16) for hd, b, b_end in zip(heads, bs, b_ends)]
            grown = [_dot_tn(hd[5].astype(BF16), k_dec) for hd, k_dec in zip(heads, k_decs)]
            return tuple(st * jnp.exp(b_end) + g for st, b_end, g in zip(sts, b_ends, grown))

        zero_state = (jnp.zeros((HG_HEAD_DIM, HG_HEAD_DIM), F32),) * HG_GROUP
        lax.fori_loop(0, nc, fwd_chunk, zero_state)

        def bwd_chunk(cc, carry):
            dsts, suffixes, dlbs = carry
            ci = nc - 1 - cc
            r0 = pl.multiple_of(ci * BLK, BLK)
            heads, bs = _hg_load(refs, r0, lb_v, lower_incl)
            qs = [hd[0] for hd in heads]
            fs = [hd[2] for hd in heads]
            ks = [1.0 - f for f in fs]
            vs = [hd[5] for hd in heads]
            vbs = [v.astype(BF16) for v in vs]
            dos = [do_ref[pl.ds(r0, BLK), h * HG_HEAD_DIM:(h + 1) * HG_HEAD_DIM] for h in heads_of]
            dobs = [do.astype(BF16) for do in dos]
            b_ends = [b[BLK - 1:BLK, :] for b in bs]
            e_qs = [jnp.exp(b) for b in bs]
            e_ks = [jnp.exp(b_end - b) for b, b_end in zip(bs, b_ends)]
            qes = [(q * e).astype(BF16) for q, e in zip(qs, e_qs)]
            khs = [(k * e).astype(BF16) for k, e in zip(ks, e_ks)]
            st_terms = [_split2_lanes(st_ref[ci, h]) for h in heads_of]
            ds_terms = [_split2_lanes(dst) for dst in dsts]
            dqes = [_dot(dob, t[:, :HG_HEAD_DIM]) + _dot(dob, t[:, HG_HEAD_DIM:]) for dob, t in zip(dobs, st_terms)]
            dkhs = [_dot(vb, t[:, :HG_HEAD_DIM]) + _dot(vb, t[:, HG_HEAD_DIM:]) for vb, t in zip(vbs, ds_terms)]
            dvs = [_dot_nt(kh, t[:, :HG_HEAD_DIM]) for kh, t in zip(khs, ds_terms)]
            grown = [_dot_tn(dob, qe) for dob, qe in zip(dobs, qes)]
            das = [_dot_nt(dob, vb) for dob, vb in zip(dobs, vbs)]
            dqs = [e * dqe for e, dqe in zip(e_qs, dqes)]
            dks = [e * dkh for e, dkh in zip(e_ks, dkhs)]
            dlogs = [qe.astype(F32) * dqe - kh.astype(F32) * dkh for qe, dqe, kh, dkh in zip(qes, dqes, khs, dkhs)]
            scs = [None] * HG_GROUP
            for v_idx, m in enumerate(HG_LEVELS):
                es, qms, kms = _hg_level_terms(qs, ks, bs, m)
                msk = mask_ref[v_idx]
                terms = [_dot_nt(qm, km) for qm, km in zip(qms, kms)]
                pms = [(da * msk).astype(BF16) for da in das]
                dqms = [_dot(pm, km) for pm, km in zip(pms, kms)]
                dkms = [_dot_tn(pm, qm) for pm, qm in zip(pms, qms)]
                scs = [t * msk if sc is None else sc + t * msk for sc, t in zip(scs, terms)]
                dqs = [dq + dqm * e for dq, dqm, e in zip(dqs, dqms, es)]
                dks = [dk + dkm * e for dk, dkm, e in zip(dks, dkms, es)]
                dlogs = [dl + (qm.astype(F32) * dqm - km.astype(F32) * dkm)
                         for dl, qm, dqm, km, dkm in zip(dlogs, qms, dqms, kms, dkms)]
            intras = [_dot_tn(sc.astype(BF16), dob) for sc, dob in zip(scs, dobs)]
            dgs = [_dot_exact_l(upper_incl, dl) + sfx for dl, sfx in zip(dlogs, suffixes)]
            new_dlbs = []
            for h in heads_of:
                q, dq_fac, f, sig = heads[h][0], heads[h][1], heads[h][2], heads[h][3]
                a_diag = jnp.sum(dos[h] * vs[h], axis=-1, keepdims=True)
                s_diag = jnp.sum(q * ks[h], axis=-1, keepdims=True)
                dq = dqs[h] + a_diag * ks[h]
                dk = dks[h] + a_diag * q
                dv = dvs[h] + intras[h] + s_diag * dos[h]
                dfull = dgs[h] / f - dk
                sl = slice(h * HG_HEAD_DIM, (h + 1) * HG_HEAD_DIM)
                out_scr[0, pl.ds(r0, BLK), sl] = (dq * dq_fac).astype(BF16)
                out_scr[1, pl.ds(r0, BLK), sl] = (dfull * (1.0 - lb_v[:, sl]) * sig * (1.0 - sig)).astype(BF16)
                out_scr[2, pl.ds(r0, BLK), sl] = dv.astype(BF16)
                new_dlbs.append(dlbs[h] + jnp.sum(dfull * (1.0 - sig), axis=0, keepdims=True))
            new_dsts = tuple(dst * jnp.exp(b_end) + g for dst, b_end, g in zip(dsts, b_ends, grown))
            return new_dsts, tuple(dg[0:1, :] for dg in dgs), tuple(new_dlbs)

        zero_row = (jnp.zeros((1, HG_HEAD_DIM), F32),) * HG_GROUP
        _, _, dlbs = lax.fori_loop(0, nc, bwd_chunk, (zero_state, zero_row, zero_row))
        dlb_ref[...] = jnp.broadcast_to(jnp.concatenate(dlbs, axis=1), dlb_ref.shape)
        group = pl.program_id(0)
        copies = []
        for t in range(3):
            col = pl.multiple_of((base + t * n_groups + group) * gw, gw)
            copies.append(pltpu.make_async_copy(out_scr.at[t], dproj_ref.at[:, pl.ds(col, gw)], sems.at[t]))
            copies[-1].start()
        for cp in copies:
            cp.wait()

    col_spec = lambda off: pl.BlockSpec((s_len, gw), lambda h: (0, off + h))
    anyspec = pl.BlockSpec(memory_space=pl.ANY)
    return _pcall(
        body, name=name,
        out_shape=(jax.ShapeDtypeStruct(dproj.shape, dproj.dtype), jax.ShapeDtypeStruct((8, WIDTH), F32)),
        grid=(n_groups,),
        in_specs=[col_spec(base), col_spec(base + n_groups), col_spec(base + 2 * n_groups), col_spec(0),
                  pl.BlockSpec((1, gw), lambda h: (0, h)), anyspec],
        out_specs=(anyspec, pl.BlockSpec((8, gw), lambda h: (0, h))),
        scratch_shapes=[pltpu.VMEM((len(HG_LEVELS), BLK, BLK), F32),
                        pltpu.VMEM((nc, HG_GROUP, HG_HEAD_DIM, HG_HEAD_DIM), F32),
                        pltpu.VMEM((3, s_len, gw), BF16), pltpu.SemaphoreType.DMA((3,))],
        aliases={5: 0},
        semantics=("arbitrary",))(proj, proj, proj, do_b, lb, dproj)


def _dh_matmul(dproj, w_full, after, name):
    s_len, n = dproj.shape
    d = w_full.shape[0]
    tm = min(512, s_len)
    tk = 1536

    def body(dp_ref, w_ref, after_ref, dh_ref):
        del after_ref
        part = _dot_nt(dp_ref[...], w_ref[...])

        @pl.when(pl.program_id(1) == 0)
        def _():
            dh_ref[...] = part

        @pl.when(pl.program_id(1) > 0)
        def _():
            dh_ref[...] += part

    return _pcall(
        body, name=name, out_shape=jax.ShapeDtypeStruct((s_len, d), F32),
        grid=(s_len // tm, n // tk),
        in_specs=[pl.BlockSpec((tm, tk), lambda i, k: (i, k)), pl.BlockSpec((d, tk), lambda i, k: (0, k)),
                  pl.BlockSpec(memory_space=pl.ANY)],
        out_specs=pl.BlockSpec((tm, d), lambda i, k: (i, 0)),
        semantics=("arbitrary", "arbitrary"))(dproj, w_full, after)


def _gw_matmul(h_t, dproj, name):
    d, s_len = h_t.shape
    n = dproj.shape[1]
    tn = 1152

    def body(ht_ref, dp_ref, gw_ref):
        gw_ref[...] = _dot(ht_ref[...], dp_ref[...]).astype(BF16)

    return _pcall(
        body, name=name, out_shape=jax.ShapeDtypeStruct((d, n), BF16),
        grid=(n // tn,),
        in_specs=[pl.BlockSpec((d, s_len), lambda j: (0, 0)), pl.BlockSpec((s_len, tn), lambda j: (0, j))],
        out_specs=pl.BlockSpec((d, tn), lambda j: (0, j)),
        semantics=("arbitrary",))(h_t, dproj)


def _ln_bwd(dh, x, scale, dres, name):
    s_len, d = x.shape
    tm = min(512, s_len)

    def body(dh_ref, x_ref, sc_ref, dres_ref, dx_ref, vec_ref):
        @pl.when(pl.program_id(0) == 0)
        def _():
            vec_ref[...] = jnp.zeros_like(vec_ref)

        dh = dh_ref[...]
        xs, rstd = _standardize(x_ref[...])
        vec_ref[0:1, :] += jnp.sum(dh, axis=0, keepdims=True)
        vec_ref[1:2, :] += jnp.sum(dh * xs, axis=0, keepdims=True)
        dx_ref[...] = _standardize_bwd(xs, rstd, dh * (1.0 + sc_ref[...])) + dres_ref[...]

    tile = pl.BlockSpec((tm, d), lambda i: (i, 0))
    return _pcall(body, name=name, grid=(s_len // tm,),
                  out_shape=(jax.ShapeDtypeStruct((s_len, d), F32), jax.ShapeDtypeStruct((8, d), F32)),
                  in_specs=[tile, tile, pl.BlockSpec((1, d), lambda i: (0, 0)), tile],
                  out_specs=(tile, pl.BlockSpec((8, d), lambda i: (0, 0))),
                  semantics=("arbitrary",))(dh, x, scale, dres)


def _wmod_grad(c_t, dmod):
    d = c_t.shape[0]
    n_layers, _, cm = dmod.shape

    def body(c_ref, dm_ref, o_ref):
        for l in range(n_layers):
            acc = None
            for b in range(NDEV):
                term = c_ref[:, b:b + 1] * dm_ref[l, b:b + 1, :]
                acc = term if acc is None else acc + term
            o_ref[l] = acc

    return _pcall(body, name="wmod_grad", out_shape=jax.ShapeDtypeStruct((n_layers, d, cm), F32))(c_t, dmod)


def _sum_adamw(parts_list, w, m, v, name):
    n_ranges = len(parts_list)
    n_src, range_rows, cols = parts_list[0].shape
    rows = range_rows * n_ranges
    tr = range_rows
    for cand in (512, 256, 128, 64, 32, 16, 8):
        if range_rows % cand == 0 and cand * cols * 4 <= (2 << 20):
            tr = cand
            break
    tiles = range_rows // tr

    def body(*refs):
        p_refs = refs[:n_ranges]
        w_ref, m_ref, v_ref, g_ref, d_ref, nm_ref, nv_ref = refs[n_ranges:]

        def step(p_ref):
            g = p_ref[0].astype(F32)
            for s in range(1, n_src):
                g = g + p_ref[s].astype(F32)
            nm = ADAM_B1 * m_ref[...] + (1.0 - ADAM_B1) * g
            nv = ADAM_B2 * v_ref[...] + (1.0 - ADAM_B2) * (g * g)
            m_hat = nm / (1.0 - ADAM_B1 ** ADAM_STEP)
            v_hat = nv / (1.0 - ADAM_B2 ** ADAM_STEP)
            g_ref[...] = g
            d_ref[...] = -ADAM_LR * (m_hat / (jnp.sqrt(v_hat) + ADAM_EPS) + ADAM_WD * w_ref[...])
            nm_ref[...] = nm
            nv_ref[...] = nv

        if n_ranges == 1:
            step(p_refs[0])
        else:
            for j in range(n_ranges):
                @pl.when(pl.program_id(0) // tiles == j)
                def _(j=j):
                    step(p_refs[j])

    def part_spec(j):
        return pl.BlockSpec((n_src, tr, cols), lambda i: (0, jnp.clip(i - j * tiles, 0, tiles - 1), 0))

    tile = pl.BlockSpec((tr, cols), lambda i: (i, 0))
    out = jax.ShapeDtypeStruct((rows, cols), F32)
    return _pcall(body, name=name, grid=(rows // tr,), out_shape=(out,) * 4,
                  in_specs=[part_spec(j) for j in range(n_ranges)] + [tile, tile, tile],
                  out_specs=(tile,) * 4, semantics=("arbitrary",))(*parts_list, w, m, v)


def _sum_parts(parts, name):
    n_src = parts.shape[0]

    def body(p_ref, o_ref):
        acc = p_ref[0]
        for s in range(1, n_src):
            acc = acc + p_ref[s]
        o_ref[...] = acc

    return _pcall(body, name=name, out_shape=jax.ShapeDtypeStruct(parts.shape[1:], F32))(parts)


def _pair_sum(gw, stage, me, name):
    d = gw.shape[0]
    n_slots, _, shard = stage.shape

    def body(me_ref, g_ref, s_ref, own_ref, o_ref):
        del me_ref
        total = (g_ref[...].astype(F32) + s_ref[0].astype(F32)).astype(BF16)
        o_ref[0] = total

        @pl.when(pl.program_id(0) == 0)
        def _():
            own_ref[0] = total

    slot = pl.BlockSpec((1, d, shard), lambda jj, me_ref: (jj, 0, 0))
    out = jax.ShapeDtypeStruct(stage.shape, BF16)
    return pl.pallas_call(
        body, name=name, out_shape=(out, out),
        grid_spec=pltpu.PrefetchScalarGridSpec(
            num_scalar_prefetch=1, grid=(n_slots,),
            in_specs=[pl.BlockSpec((d, shard), lambda jj, me_ref: (0, me_ref[0] ^ (2 * jj))), slot],
            out_specs=(pl.BlockSpec((1, d, shard), lambda jj, me_ref: (0, 0, 0)), slot)),
        compiler_params=pltpu.CompilerParams(dimension_semantics=("arbitrary",), vmem_limit_bytes=VMEM_LIMIT),
        interpret=False)(me.reshape(1).astype(jnp.int32), gw, stage)


def _lower_bound_table(lower_bounds):
    p = jax.nn.softmax(lower_bounds.astype(F32), axis=0)
    return jnp.cumsum(p, axis=0) - p[0:1]


def _pad_rows(v, width):
    n = v.shape[0]
    rows = -(-n // width)
    rows = -(-rows // 8) * 8
    return jnp.pad(v, (0, rows * width - n)).reshape(rows, width)


def kernel(x, c, w_mod, b_mod, w_in, conv_w, hgrn_norm_w, lower_bounds, w_branch, w_out, ln_g, ln_b, loss_target, m_w_mod, m_b_mod, m_w_in, m_conv_w, m_hgrn_norm_w, m_lower_bounds, m_w_branch, m_w_out, m_ln_g, m_ln_b, v_w_mod, v_b_mod, v_w_in, v_conv_w, v_hgrn_norm_w, v_lower_bounds, v_w_branch, v_w_out, v_ln_g, v_ln_b):
    n_layers = N_LAYERS
    s_len, d = x.shape[1], x.shape[2]
    n_cols = w_in.shape[2] * NDEV
    cw_cols = conv_w.shape[2]
    cm = w_mod.shape[2]
    me = _my_index()
    x0 = x[0]
    target = loss_target[0]

    small = _pad_rows(jnp.concatenate([c.reshape(-1), conv_w.reshape(-1)]), BLK)
    small_all = _all_gather_small("gather_c_conv", small).reshape(NDEV, -1)
    c_all = small_all[:, :d]
    conv_full = small_all[:, d:d + n_layers * 3 * cw_cols].reshape(NDEV, n_layers, 3, cw_cols)
    conv_full = conv_full.transpose(1, 2, 0, 3).reshape(n_layers, 3, WIDTH)

    b_mod_mine = lax.dynamic_slice_in_dim(b_mod, me * cm, cm, axis=1).reshape(n_layers, 1, cm)
    mod_cols = _mod_fwd(c_all, w_mod, b_mod_mine)
    mod_all = _all_gather_small("gather_mod", mod_cols.reshape(n_layers * NDEV, cm))
    mod_all = mod_all.reshape(NDEV, n_layers, NDEV, cm)
    mod_mine = lax.dynamic_index_in_dim(mod_all, me, axis=2, keepdims=False)
    mod_mine = mod_mine.transpose(1, 0, 2).reshape(n_layers, 3, 1, d)

    shard = w_in.shape[2]
    dsh = d // NDEV
    w_in_b, w_branch_b, w_out_b = w_in.astype(BF16), w_branch.astype(BF16), w_out.astype(BF16)
    window = lambda ref, dev: ref.at[:, pl.ds(pl.multiple_of(dev * shard, BLK), shard)]

    def two_step_sends(places):
        chips, sibling = [], []
        for k in (1, 2, 4, 6):
            for a, place in enumerate(places):
                chips.append((k, lambda ins, lands, me, a=a: ins[a],
                              lambda lands, me, a=a, place=place: place(lands[a], me),
                              lambda lands, me, a=a, k=k, place=place: place(lands[a], me ^ k)))
        for j in (2, 4, 6):
            for a, place in enumerate(places):
                sibling.append((1, lambda ins, lands, me, a=a, j=j, place=place: place(lands[a], me ^ j),
                                lambda lands, me, a=a, j=j, place=place: place(lands[a], me ^ j),
                                lambda lands, me, a=a, j=j, place=place: place(lands[a], me ^ 1 ^ j)))
        return chips, sibling

    in_sends = two_step_sends([window])
    rest_sends = two_step_sends([_slot, _slot])
    layer_sends = two_step_sends([window, _slot, _slot])

    def in_land(l):
        return _place_own_window(f"place_w_in_{l}", (d, n_cols), w_in_b[l], me)

    def rest_lands(l):
        return [_place_own((NDEV, 3, WIDTH, dsh), BF16, w_branch_b[l][None], (me, 0, 0, 0)),
                _place_own((NDEV, dsh, d), BF16, w_out_b[l][None], (me, 0, 0))]

    def gather_start(name, shards, lands, sends, after):
        return _exchange_start(f"{name}_chips_start", shards, lands, sends[0], after)

    def gather_pass_on(name, started, after, sends):
        _, lands = _exchange_wait(f"{name}_chips_wait", started, after, sends[0])
        return _exchange_start(f"{name}_sibling_start", [], lands, sends[1])

    def gather_finish(name, started, after, sends):
        return _exchange_wait(f"{name}_sibling_wait", started, after, sends[1])[1]

    def branch_out_weights(w_branch_l, w_out_l):
        return w_branch_l.transpose(1, 2, 0, 3).reshape(3, WIDTH, d), w_out_l.reshape(d, d)

    gathering = gather_start("gather_w_in_0", [w_in_b[0]], [in_land(0)], in_sends, mod_mine)
    passing = gather_pass_on("gather_w_in_0", gathering, gathering[4], in_sends)
    rest_gathering = gather_start("gather_rest_0", [w_branch_b[0], w_out_b[0]], rest_lands(0), rest_sends, passing[4])
    next_gathering = None
    if n_layers > 1:
        next_gathering = gather_start("gather_weights_1", [w_in_b[1], w_branch_b[1], w_out_b[1]],
                                      [in_land(1)] + rest_lands(1), layer_sends, rest_gathering[4])
    w_in_l = gather_finish("gather_w_in_0", passing, (next_gathering or rest_gathering)[4], in_sends)[0]

    lbs = _lower_bound_table(lower_bounds)
    norm_w4 = jnp.tile(hgrn_norm_w, (1, WIDTH // HG_HEAD_DIM))

    saved = []
    xl = x0
    for l in range(n_layers):
        shift, scale, gate = mod_mine[l, 0], mod_mine[l, 1], mod_mine[l, 2]
        proj, h_t = _ln_proj(xl, shift, scale, w_in_l, f"ln_proj_{l}")
        o_a, totals = _sb_fwd(proj, f"sb_fwd_{l}")
        if l == 0:
            rest_passing = gather_pass_on("gather_rest_0", rest_gathering, o_a, rest_sends)
        o_b = _hgrn_fwd(proj, lbs[l:l + 1], f"hgrn_fwd_{l}")
        if l == 0:
            wb_l, wo_l = branch_out_weights(*gather_finish("gather_rest_0", rest_passing, o_b, rest_sends))
            if n_layers > 1:
                next_passing = gather_pass_on("gather_weights_1", next_gathering, o_b, layer_sends)
                gate = gate + next_passing[4][0, 0]
        x_new, merged, ycat = _merge_fwd(xl, proj, o_a, o_b, gate, norm_w4[l:l + 1], conv_full[l],
                                         wb_l, wo_l, ln_g[l:l + 1], ln_b[l:l + 1], f"merge_fwd_{l}")
        saved.append((xl, proj, h_t, o_a, totals, o_b, merged, ycat, w_in_l, wb_l, wo_l))
        if l == 0 and n_layers > 1:
            w_in_l, w_branch_l, w_out_l = gather_finish("gather_weights_1", next_passing, x_new, layer_sends)
            wb_l, wo_l = branch_out_weights(w_branch_l, w_out_l)
        xl = x_new

    loss_part, dx = _loss_fwd_bwd(xl, target)
    loss = lax.psum(loss_part[0, 0], ("x", "y", "c"))

    pair_sends = [(1, lambda ins, lands, me, j=j: window(ins[0], me ^ 1 ^ j),
                   lambda lands, me, jj=jj: lands[0].at[jj], lambda lands, me, jj=jj: lands[0].at[jj])
                  for jj, j in enumerate((0, 2, 4, 6))]
    chip_sum_sends = [(j, lambda ins, lands, me, jj=jj: ins[0].at[jj],
                       lambda lands, me, jj=jj: lands[0].at[jj], lambda lands, me, jj=jj: lands[0].at[jj])
                      for jj, j in ((1, 2), (2, 4), (3, 6))]
    rest_scatter = _direct_sends([(0, 0, _slot, _slot), (1, 1, _slot, _slot)])
    scattering = [None] * n_layers
    small_grads = [None] * n_layers
    dmod = [None] * n_layers
    tie = None
    for l in reversed(range(n_layers)):
        xl, proj, h_t, o_a, totals, o_b, merged, ycat, w_in_l, wb_l, wo_l = saved[l]
        scale, gate = mod_mine[l, 1], mod_mine[l, 2]
        if tie is not None:
            gate = gate + tie[0, 0]
        dres, dycat, dproj, gwo, gwb, mvec = _merge_bwd(dx, xl, merged, ycat, proj, gate, wb_l, wo_l,
                                                        ln_g[l:l + 1], f"merge_bwd_{l}")
        gwb_by_owner = gwb.astype(BF16).reshape(3, WIDTH, NDEV, dsh).transpose(2, 0, 1, 3)
        gwo_by_owner = gwo.astype(BF16).reshape(NDEV, dsh, d)
        lands = [_place_own((NDEV, 3, WIDTH, dsh), BF16, lax.dynamic_slice_in_dim(gwb_by_owner, me, 1, axis=0),
                            (me, 0, 0, 0)),
                 _place_own((NDEV, dsh, d), BF16, lax.dynamic_slice_in_dim(gwo_by_owner, me, 1, axis=0),
                            (me, 0, 0))]
        rest_started = _exchange_start(f"scatter_rest_{l}_start", [gwb_by_owner, gwo_by_owner], lands, rest_scatter)
        dproj, do_a, do_b, bvec = _branch_bwd(dycat, proj, o_a, o_b, norm_w4[l:l + 1] + rest_started[4][0, 0],
                                              conv_full[l], dproj, f"branch_bwd_{l}")
        dproj = _sb_bwd(proj, do_a, totals, dproj, f"sb_bwd_{l}")
        dproj, dlb = _hgrn_bwd(proj, do_b, lbs[l:l + 1], dproj, f"hgrn_bwd_{l}")
        gwi = _gw_matmul(h_t, dproj, f"gw_matmul_{l}")
        swapping = _exchange_start(f"scatter_in_{l}_sibling_start", [gwi], [lax.empty((4, d, shard), BF16)], pair_sends)
        if l > 0:
            dh = _dh_matmul(dproj, w_in_l, swapping[4], f"dh_matmul_{l}")
        (gwi,), (stage,) = _exchange_wait(f"scatter_in_{l}_sibling_wait", swapping, dh if l > 0 else swapping[4],
                                          pair_sends)
        land, chip_sums = _pair_sum(gwi, stage, me, f"pair_sum_{l}")
        in_started = _exchange_start(f"scatter_in_{l}_chips_start", [chip_sums], [land], chip_sum_sends)
        scattering[l] = (in_started, rest_started)
        tie = in_started[4]
        if l == 0:
            dh = _dh_matmul(dproj, w_in_l, tie, f"dh_matmul_{l}")
        dx, lvec = _ln_bwd(dh, xl, scale + tie[0, 0], dres, f"ln_bwd_{l}")
        dmod[l] = jnp.concatenate([lvec[0], lvec[1], mvec[2]])
        norm_grad = bvec[0].reshape(WIDTH // HG_HEAD_DIM, HG_HEAD_DIM).sum(axis=0)
        small_grads[l] = jnp.concatenate([mvec[0], mvec[1], norm_grad, dlb[0], bvec[1:4].reshape(-1)])
    grad_x = dx[None]

    small_vec = jnp.concatenate(dmod + small_grads)
    n_small = small_vec.shape[0]
    small_all = _all_gather_small("gather_small_grads", _pad_rows(small_vec, BLK))
    small_sum = _sum_parts(small_all, "sum_small_grads").reshape(-1)[:n_small]
    dmod_all = small_all.reshape(NDEV, -1)[:, :n_layers * 3 * d].reshape(NDEV, n_layers, 3 * d)

    off = n_layers * 3 * d
    grad_b_mod = small_sum[:off].reshape(n_layers, 3 * d)
    per_layer = 2 * d + HG_HEAD_DIM + WIDTH + 3 * WIDTH
    g_ln_g, g_ln_b, g_norm, g_lbs, g_conv = [], [], [], [], []
    for l in range(n_layers):
        seg = small_sum[off + l * per_layer: off + (l + 1) * per_layer]
        g_ln_g.append(seg[:d])
        g_ln_b.append(seg[d:2 * d])
        g_norm.append(seg[2 * d:2 * d + HG_HEAD_DIM])
        g_lbs.append(seg[2 * d + HG_HEAD_DIM:2 * d + HG_HEAD_DIM + WIDTH])
        g_conv.append(seg[2 * d + HG_HEAD_DIM + WIDTH:].reshape(3, WIDTH))
    grad_ln_g, grad_ln_b = jnp.stack(g_ln_g), jnp.stack(g_ln_b)
    grad_norm = jnp.stack(g_norm)
    _, lbs_vjp = jax.vjp(_lower_bound_table, lower_bounds)
    grad_lower = lbs_vjp(jnp.stack(g_lbs))[0]
    grad_conv = lax.dynamic_slice_in_dim(jnp.stack(g_conv), me * cw_cols, cw_cols, axis=2)

    dmod_mine = lax.dynamic_slice_in_dim(dmod_all, me * cm, cm, axis=2).transpose(1, 0, 2)
    grad_w_mod = _wmod_grad(c_all.T, dmod_mine)

    p_in, p_branch, p_out = [None] * n_layers, [None] * n_layers, [None] * n_layers
    for l in reversed(range(n_layers)):
        in_started, rest_started = scattering[l]
        p_branch_l, p_out[l] = _exchange_wait(f"scatter_rest_{l}_wait", rest_started, grad_w_mod, rest_scatter)[1]
        p_branch[l] = p_branch_l.reshape(NDEV, 3 * WIDTH, dsh)
        p_in[l] = _exchange_wait(f"scatter_in_{l}_chips_wait", in_started, grad_w_mod, chip_sum_sends)[1][0]

    def adam(parts_list, w, m, v, name):
        shape = w.shape
        cols = shape[-1]
        flat = lambda a: a.reshape(-1, cols)
        outs = _sum_adamw(parts_list, flat(w), flat(m), flat(v), name)
        return [o.reshape(shape) for o in outs]

    r_w_in = adam(p_in, w_in, m_w_in, v_w_in, "adamw_w_in")
    r_w_branch = adam(p_branch, w_branch, m_w_branch, v_w_branch, "adamw_w_branch")
    r_w_out = adam(p_out, w_out, m_w_out, v_w_out, "adamw_w_out")
    r_w_mod = adam([grad_w_mod.reshape(1, -1, cm)], w_mod, m_w_mod, v_w_mod, "adamw_w_mod")

    small_names = ["b_mod", "conv_w", "hgrn_norm_w", "lower_bounds", "ln_g", "ln_b"]
    small_g = [grad_b_mod, grad_conv, grad_norm, grad_lower, grad_ln_g, grad_ln_b]
    small_w = [b_mod, conv_w, hgrn_norm_w, lower_bounds, ln_g, ln_b]
    small_m = [m_b_mod, m_conv_w, m_hgrn_norm_w, m_lower_bounds, m_ln_g, m_ln_b]
    small_v = [v_b_mod, v_conv_w, v_hgrn_norm_w, v_lower_bounds, v_ln_g, v_ln_b]
    pack = lambda arrs: _pad_rows(jnp.concatenate([a.reshape(-1) for a in arrs]), BLK)
    packed = _sum_adamw([pack(small_g)[None]], pack(small_w), pack(small_m), pack(small_v), "adamw_small")
    r_small = {n: [] for n in small_names}
    for res in packed:
        flat = res.reshape(-1)
        pos = 0
        for n, w in zip(small_names, small_w):
            r_small[n].append(flat[pos:pos + w.size].reshape(w.shape))
            pos += w.size

    results = {"w_mod": r_w_mod, "w_in": r_w_in, "w_branch": r_w_branch, "w_out": r_w_out, **r_small}
    order = ["w_mod", "b_mod", "w_in", "conv_w", "hgrn_norm_w", "lower_bounds", "w_branch", "w_out", "ln_g", "ln_b"]
    outs = [loss, grad_x]
    for idx in range(4):
        outs.extend(results[n][idx] for n in order)
    return tuple(outs)
```

```python
import jax
import jax.numpy as jnp
from jax import lax
from jax.experimental import pallas as pl
from jax.experimental.pallas import tpu as pltpu

F32 = jnp.float32
BF16 = jnp.bfloat16
NDEV = 8
N_LAYERS = 2
SB_HEAD_DIM = 64
HG_HEAD_DIM = 128
WIDTH = 512
BLK = 128
LN_EPS = 1e-5
RMS_EPS = 1e-6
ALPHA = (2.0 * N_LAYERS) ** 0.25
ADAM_LR, ADAM_B1, ADAM_B2, ADAM_EPS, ADAM_WD, ADAM_STEP = 0.001, 0.9, 0.999, 1e-08, 0.01, 10
VMEM_LIMIT = 56 * 1024 * 1024
MESH = pl.DeviceIdType.MESH
HG_LEVELS = (64, 32, 16, 8, 4, 2, 1)


def _pcall(body, *, name, out_shape, grid=None, in_specs=None, out_specs=None, scratch_shapes=(),
           semantics=None, aliases=None):
    kwargs = {}
    if grid is not None:
        kwargs["grid"] = grid
    if in_specs is not None:
        kwargs["in_specs"] = in_specs
    if out_specs is not None:
        kwargs["out_specs"] = out_specs
    if aliases:
        kwargs["input_output_aliases"] = aliases
    return pl.pallas_call(
        body, name=name, out_shape=out_shape, scratch_shapes=list(scratch_shapes),
        compiler_params=pltpu.CompilerParams(dimension_semantics=semantics, vmem_limit_bytes=VMEM_LIMIT),
        interpret=False, **kwargs)


def _dot(a, b):
    return jnp.dot(a, b, preferred_element_type=F32)


def _dot_nt(a, b):
    return lax.dot_general(a, b, (((1,), (1,)), ((), ())), preferred_element_type=F32)


def _dot_tn(a, b):
    return lax.dot_general(a, b, (((0,), (0,)), ((), ())), preferred_element_type=F32)


def _split3(x):
    x1 = x.astype(BF16)
    r1 = x - x1.astype(F32)
    x2 = r1.astype(BF16)
    r2 = r1 - x2.astype(F32)
    return x1, x2, r2.astype(BF16)


def _split2(x):
    x1 = x.astype(BF16)
    return x1, (x - x1.astype(F32)).astype(BF16)


def _dot_exact_l(m_bf16, x):
    x1, x2, x3 = _split3(x)
    return _dot(m_bf16, x1) + _dot(m_bf16, x2) + _dot(m_bf16, x3)


def _sigmoid(x):
    return 1.0 / (1.0 + jnp.exp(-x))


def _silu_and_grad(x):
    s = _sigmoid(x)
    return x * s, s * (1.0 + x * (1.0 - s))


LOG2E = 1.4426950408889634
MASKED_SCORE = -1e30


def _softplus2_parts(z2):
    minus_abs = lax.bitcast_convert_type(lax.bitcast_convert_type(z2, jnp.int32) | jnp.int32(-2 ** 31), F32)
    e = jnp.exp2(minus_abs)
    sp2 = jnp.maximum(z2, 0.0) + jnp.log2(1.0 + e)
    r = 1.0 / (1.0 + e)
    return sp2, jnp.where(z2 >= 0.0, r, e * r)


def _split2_lanes(x):
    x1 = x.astype(BF16)
    return jnp.concatenate([x1, (x - x1.astype(F32)).astype(BF16)], axis=1)


def _iota2(shape, dim):
    return lax.broadcasted_iota(jnp.int32, shape, dim)


def _standardize(x):
    mu = jnp.mean(x, axis=-1, keepdims=True)
    xc = x - mu
    var = jnp.mean(xc * xc, axis=-1, keepdims=True)
    rstd = lax.rsqrt(var + LN_EPS)
    return xc * rstd, rstd


def _standardize_bwd(xhat, rstd, dxhat):
    m1 = jnp.mean(dxhat, axis=-1, keepdims=True)
    m2 = jnp.mean(dxhat * xhat, axis=-1, keepdims=True)
    return rstd * (dxhat - m1 - xhat * m2)


def _my_index():
    return 4 * lax.axis_index("x") + 2 * lax.axis_index("y") + lax.axis_index("c")


def _exchange(name, ins, out_shapes, transfers, in_vmem):
    n_in, n_out, n_t = len(ins), len(out_shapes), len(transfers)

    def body(*refs):
        in_refs, out_refs = refs[:n_in], refs[n_in:n_in + n_out]
        send_sems, recv_sems, local_sems = refs[n_in + n_out:]
        x, y, c = lax.axis_index("x"), lax.axis_index("y"), lax.axis_index("c")
        me = 4 * x + 2 * y + c
        started = []
        for t, (i, o, src_fn, dst_fn) in enumerate(transfers):
            own = pltpu.make_async_copy(src_fn(in_refs[i], me), dst_fn(out_refs[o], me), local_sems.at[t])
            own.start()
            started.append(own)
        arrivals = []
        for k in range(1, NDEV):
            px = x ^ ((k >> 2) & 1)
            py = y ^ ((k >> 1) & 1)
            pc = c ^ (k & 1)
            peer = 4 * px + 2 * py + pc
            for t, (i, o, src_fn, dst_fn) in enumerate(transfers):
                sem = t * (NDEV - 1) + k - 1
                push = pltpu.make_async_remote_copy(
                    src_ref=src_fn(in_refs[i], peer), dst_ref=dst_fn(out_refs[o], me),
                    send_sem=send_sems.at[sem], recv_sem=recv_sems.at[sem],
                    device_id=(px, py, pc), device_id_type=MESH)
                push.start()
                started.append(push)
                arrivals.append(pltpu.make_async_remote_copy(
                    src_ref=src_fn(in_refs[i], peer), dst_ref=dst_fn(out_refs[o], peer),
                    send_sem=send_sems.at[sem], recv_sem=recv_sems.at[sem],
                    device_id=(px, py, pc), device_id_type=MESH))
        for arrival in arrivals:
            arrival.wait_recv()
        for cp in started[n_t:]:
            cp.wait_send()
        for own in started[:n_t]:
            own.wait()

    space = pltpu.VMEM if in_vmem else pl.ANY
    spec = pl.BlockSpec(memory_space=space)
    return _pcall(
        body, name=name, out_shape=out_shapes,
        in_specs=[spec] * n_in, out_specs=[spec] * n_out,
        scratch_shapes=[pltpu.SemaphoreType.DMA((n_t * (NDEV - 1),)),
                        pltpu.SemaphoreType.DMA((n_t * (NDEV - 1),)),
                        pltpu.SemaphoreType.DMA((n_t,))])(*ins)


def _whole(ref, dev):
    return ref


def _slot(ref, dev):
    return ref.at[dev]


def _all_gather_small(name, v):
    out = _exchange(name, [v], [jax.ShapeDtypeStruct((NDEV,) + v.shape, v.dtype)],
                    [(0, 0, _whole, _slot)], in_vmem=True)
    return out[0]


_HBM_SPEC = pl.BlockSpec(memory_space=pltpu.HBM)
_SEM_SPEC = pl.BlockSpec(memory_space=pltpu.SEMAPHORE)
_DATAFLOW = pltpu.SideEffectType.DATAFLOW_SIDE_EFFECTING


def _peer(x, y, c, k):
    px = x ^ ((k >> 2) & 1)
    py = y ^ ((k >> 1) & 1)
    pc = c ^ (k & 1)
    return (px, py, pc), 4 * px + 2 * py + pc


def _direct_sends(transfers):
    sends = []
    for k in range(1, NDEV):
        for i, o, src_fn, dst_fn in transfers:
            sends.append((k,
                          lambda ins, lands, me, i=i, k=k, src_fn=src_fn: src_fn(ins[i], me ^ k),
                          lambda lands, me, o=o, dst_fn=dst_fn: dst_fn(lands[o], me),
                          lambda lands, me, o=o, k=k, dst_fn=dst_fn: dst_fn(lands[o], me ^ k)))
    return sends


def _exchange_start(name, ins, lands, sends, after=None):
    n_in, n_buf = len(ins), len(ins) + len(lands)
    n_sem = len(sends)

    def body(*refs):
        in_refs, land_refs = refs[:n_in], refs[n_in:n_buf]
        n_skip = n_buf + (0 if after is None else 1)
        send_sems, recv_sems, token = refs[n_skip], refs[n_skip + 1], refs[-1]
        x, y, c = lax.axis_index("x"), lax.axis_index("y"), lax.axis_index("c")
        me = 4 * x + 2 * y + c
        for t, (k, src_fn, dst_fn, _) in enumerate(sends):
            pltpu.make_async_remote_copy(
                src_ref=src_fn(in_refs, land_refs, me), dst_ref=dst_fn(land_refs, me),
                send_sem=send_sems.at[t], recv_sem=recv_sems.at[t],
                device_id=_peer(x, y, c, k)[0], device_id_type=MESH).start()
        token[...] = jnp.zeros_like(token)

    bufs = [pltpu.with_memory_space_constraint(a, pltpu.HBM) for a in list(ins) + list(lands)]
    extra = [] if after is None else [after]
    outs = pl.pallas_call(
        body, name=name,
        out_shape=(pltpu.SemaphoreType.DMA((n_sem,)), pltpu.SemaphoreType.DMA((n_sem,)))
        + tuple(pltpu.HBM(a.shape, a.dtype) for a in bufs) + (jax.ShapeDtypeStruct((8, BLK), F32),),
        in_specs=[_HBM_SPEC] * n_buf + [pl.BlockSpec(memory_space=pl.ANY)] * len(extra),
        out_specs=(_SEM_SPEC, _SEM_SPEC) + (_HBM_SPEC,) * n_buf + (pl.BlockSpec(memory_space=pltpu.VMEM),),
        input_output_aliases={b: 2 + b for b in range(n_buf)},
        compiler_params=pltpu.CompilerParams(has_side_effects=_DATAFLOW),
        interpret=False)(*bufs, *extra)
    return outs[0], outs[1], list(outs[2:2 + n_in]), list(outs[2 + n_in:2 + n_buf]), outs[-1]


def _exchange_wait(name, started, after, sends):
    send_sems, recv_sems, ins, lands, _ = started
    n_in, n_buf = len(ins), len(ins) + len(lands)

    def body(*refs):
        in_refs, land_refs = refs[:n_in], refs[n_in:n_buf]
        send_sems, recv_sems = refs[n_buf], refs[n_buf + 1]
        x, y, c = lax.axis_index("x"), lax.axis_index("y"), lax.axis_index("c")
        me = 4 * x + 2 * y + c
        for t, (k, src_fn, _, rcv_fn) in enumerate(sends):
            cp = pltpu.make_async_remote_copy(
                src_ref=src_fn(in_refs, land_refs, me), dst_ref=rcv_fn(land_refs, me),
                send_sem=send_sems.at[t], recv_sem=recv_sems.at[t],
                device_id=_peer(x, y, c, k)[0], device_id_type=MESH)
            cp.wait_send()
            cp.wait_recv()

    bufs = list(ins) + list(lands)
    outs = pl.pallas_call(
        body, name=name, out_shape=tuple(pltpu.HBM(a.shape, a.dtype) for a in bufs),
        in_specs=[_HBM_SPEC] * n_buf + [_SEM_SPEC, _SEM_SPEC, pl.BlockSpec(memory_space=pl.ANY)],
        out_specs=(_HBM_SPEC,) * n_buf,
        input_output_aliases={b: b for b in range(n_buf)},
        compiler_params=pltpu.CompilerParams(has_side_effects=_DATAFLOW),
        interpret=False)(*bufs, send_sems, recv_sems, after)
    return list(outs[:n_in]), list(outs[n_in:])


def _place_own(shape, dtype, own, start):
    return lax.dynamic_update_slice(lax.empty(shape, dtype), own, start)


def _place_own_window(name, shape, own, me):
    rows, cols = own.shape

    def body(me_ref, zone_in, own_ref, zone_ref):
        del me_ref, zone_in
        zone_ref[...] = own_ref[...]

    return pl.pallas_call(
        body, name=name, out_shape=jax.ShapeDtypeStruct(shape, own.dtype),
        grid_spec=pltpu.PrefetchScalarGridSpec(
            num_scalar_prefetch=1, grid=(1,),
            in_specs=[pl.BlockSpec(memory_space=pl.ANY), pl.BlockSpec((rows, cols), lambda i, me_ref: (0, 0))],
            out_specs=pl.BlockSpec((rows, cols), lambda i, me_ref: (0, me_ref[0]))),
        input_output_aliases={1: 0},
        compiler_params=pltpu.CompilerParams(dimension_semantics=("arbitrary",), vmem_limit_bytes=VMEM_LIMIT),
        interpret=False)(me.reshape(1).astype(jnp.int32), lax.empty(shape, own.dtype), own)


def _mod_fwd(c_all, w_mod, b_mod_mine):
    n_layers, _, cm = w_mod.shape

    def body(c_ref, w_ref, b_ref, o_ref):
        for l in range(n_layers):
            o_ref[l] = jnp.dot(c_ref[...], w_ref[l], preferred_element_type=F32,
                               precision=lax.Precision.HIGHEST) + b_ref[l]

    return _pcall(body, name="mod_fwd", out_shape=jax.ShapeDtypeStruct((n_layers, NDEV, cm), F32))(
        c_all, w_mod, b_mod_mine)


def _ln_proj(x, shift, scale, w_full, name):
    s_len, d = x.shape
    n = w_full.shape[1]
    tm = min(512, s_len)
    tn = 1024

    def body(x_ref, sh_ref, sc_ref, w_ref, proj_ref, ht_ref, h_scr):
        @pl.when(pl.program_id(1) == 0)
        def _():
            xs, _ = _standardize(x_ref[...])
            h = xs * (1.0 + sc_ref[...]) + sh_ref[...]
            h_scr[...] = h.astype(BF16)
            ht_ref[...] = h.T.astype(BF16)

        proj_ref[...] = _dot(h_scr[...], w_ref[...])

    return _pcall(
        body, name=name,
        out_shape=(jax.ShapeDtypeStruct((s_len, n), F32), jax.ShapeDtypeStruct((d, s_len), BF16)),
        grid=(s_len // tm, n // tn),
        in_specs=[pl.BlockSpec((tm, d), lambda i, j: (i, 0)),
                  pl.BlockSpec((1, d), lambda i, j: (0, 0)),
                  pl.BlockSpec((1, d), lambda i, j: (0, 0)),
                  pl.BlockSpec((d, tn), lambda i, j: (0, j))],
        out_specs=(pl.BlockSpec((tm, tn), lambda i, j: (i, j)),
                   pl.BlockSpec((d, tm), lambda i, j: (0, i))),
        scratch_shapes=[pltpu.VMEM((tm, d), BF16)],
        semantics=("arbitrary", "arbitrary"))(x, shift, scale, w_full)


def _sb_group_blocks(nb):
    return min(4, nb)


def _sb_fwd(proj, name):
    s_len = proj.shape[0]
    nb = s_len // BLK
    n_pairs = WIDTH // BLK
    gb = _sb_group_blocks(nb)
    kw = gb * BLK

    def body(q_ref, k_ref, v_ref, o_ref, tot_ref):
        lane = _iota2((1, BLK), 1)
        row = _iota2((BLK, BLK), 0)
        col = _iota2((BLK, BLK), 1)
        half = jnp.concatenate([(row >= col).astype(BF16), jnp.ones((BLK, BLK), BF16)], axis=1)
        suffix_and_sum = jnp.concatenate([half, half], axis=0)
        qpos = _iota2((BLK, kw), 0)
        kpos = _iota2((BLK, kw), 1)
        head_lanes = [(lane // SB_HEAD_DIM) == hh for hh in range(2)]

        def scores(i, gi, qms, masked):
            c0 = pl.multiple_of(gi * kw, kw)
            kb = k_ref[pl.ds(c0, kw), :].astype(BF16)
            z2s = [_dot_nt(qms[hh], kb) for hh in range(2)]
            if masked:
                valid = (c0 + kpos) < (i * BLK + qpos)
                z2s = [jnp.where(valid, z2, MASKED_SCORE) for z2 in z2s]
            return tuple(z2s)

        def accumulate(gi, z2s, carry):
            c0 = pl.multiple_of(gi * kw, kw)
            vf = v_ref[pl.ds(c0, kw), :]
            sp2s = [_softplus2_parts(z2)[0] for z2 in z2s]
            terms = [[_split2_lanes(sp2[:, b * BLK:(b + 1) * BLK]) for b in range(gb)] for sp2 in sp2s]
            sums = [[_dot(t, suffix_and_sum) for t in head_terms] for head_terms in terms]
            weights, laters = [], []
            for hh in range(2):
                later = carry[2 * hh + 1]
                parts = [None] * gb
                for b in reversed(range(gb)):
                    parts[b] = sums[hh][b][:, :BLK] + later
                    later = later + sums[hh][b][:, BLK:]
                weights.append(jnp.exp2(z2s[hh] - jnp.concatenate(parts, axis=1)).astype(BF16))
                laters.append(later)
            outs = [_dot(weights[hh], jnp.where(head_lanes[hh], vf, 0.0).astype(BF16)) for hh in range(2)]
            return (carry[0] + outs[0], laters[0], carry[2] + outs[1], laters[1])

        def queries(i):
            qf = q_ref[pl.ds(pl.multiple_of(i * BLK, BLK), BLK), :] * (SB_HEAD_DIM ** -0.5 * LOG2E)
            return [jnp.where(head_lanes[hh], qf, 0.0).astype(BF16) for hh in range(2)]

        def qblock(i, first_scores):
            r0 = pl.multiple_of(i * BLK, BLK)
            qms = queries(i)
            zero = jnp.zeros((BLK, BLK), F32)
            last = i // gb

            def step(jj, state):
                gi = last - 1 - jj
                return scores(i, gi, qms, False) + accumulate(gi + 1, state[:2], state[2:])

            state = lax.fori_loop(0, last, step, first_scores + (zero,) * 4)
            nxt = jnp.minimum(i + 1, nb - 1)
            next_scores = scores(nxt, nxt // gb, queries(nxt), True)
            carry = accumulate(0, state[:2], state[2:])
            o_ref[pl.ds(r0, BLK), :] = carry[0] + carry[2]
            tot_ref[0, pl.ds(r0, BLK), :] = carry[1]
            tot_ref[1, pl.ds(r0, BLK), :] = carry[3]
            return next_scores

        lax.fori_loop(0, nb, qblock, scores(0, 0, queries(0), True))

    col_spec = lambda off: pl.BlockSpec((s_len, BLK), lambda p: (0, off + p))
    return _pcall(
        body, name=name,
        out_shape=(jax.ShapeDtypeStruct((s_len, WIDTH), F32),
                   jax.ShapeDtypeStruct((2 * n_pairs, s_len, BLK), F32)),
        grid=(n_pairs,),
        in_specs=[col_spec(0), col_spec(n_pairs), col_spec(2 * n_pairs)],
        out_specs=(pl.BlockSpec((s_len, BLK), lambda p: (0, p)),
                   pl.BlockSpec((2, s_len, BLK), lambda p: (p, 0, 0))),
        semantics=("arbitrary",))(proj, proj, proj)


def _hg_masks(mask_ref):
    row = _iota2((BLK, BLK), 0)
    col = _iota2((BLK, BLK), 1)
    for v, m in enumerate(HG_LEVELS):
        same = (row // (2 * m)) == (col // (2 * m))
        mask_ref[v] = (same & ((row & m) != 0) & ((col & m) == 0)).astype(F32)


def _hg_mid(b, m):
    if m >= 4:
        n = BLK // (2 * m)
        mid = b.reshape(n, 2 * m, BLK)[:, m - 1:m, :]
        return jnp.broadcast_to(mid, (n, 2 * m, BLK)).reshape(BLK, BLK)
    pos = _iota2((BLK, BLK), 0) & (2 * m - 1)
    out = b
    for p in range(2 * m):
        delta = (m - 1) - p
        if delta != 0:
            out = jnp.where(pos == p, pltpu.roll(b, (-delta) % BLK, 0), out)
    return out


def _hg_chunk_inputs(qraw, fpre, lb):
    sig = _sigmoid(fpre)
    f = lb + (1.0 - lb) * sig
    g = jnp.log(f)
    q, dq_fac = _silu_and_grad(qraw)
    return q, dq_fac, f, sig, g


HG_GROUP = 4


def _neg_abs(x):
    return lax.bitcast_convert_type(lax.bitcast_convert_type(x, jnp.int32) | jnp.int32(-2 ** 31), F32)


def _hg_level_terms(qs, ks, bs, m):
    es = [jnp.exp(_neg_abs(b - _hg_mid(b, m))) for b in bs]
    qts = [(q * e).astype(BF16) for q, e in zip(qs, es)]
    kts = [(k * e).astype(BF16) for k, e in zip(ks, es)]
    return es, qts, kts


def _hg_load(refs, r0, lb_v, lower_incl):
    q_ref, f_ref, i_ref = refs
    heads = []
    for h in range(HG_GROUP):
        sl = slice(h * HG_HEAD_DIM, (h + 1) * HG_HEAD_DIM)
        heads.append(_hg_chunk_inputs(q_ref[pl.ds(r0, BLK), sl], f_ref[pl.ds(r0, BLK), sl], lb_v[:, sl])
                     + (i_ref[pl.ds(r0, BLK), sl],))
    bs = [_dot_exact_l(lower_incl, hd[4]) for hd in heads]
    return heads, bs


def _hgrn_fwd(proj, lb, name):
    s_len = proj.shape[0]
    nc = s_len // BLK
    gw = HG_GROUP * HG_HEAD_DIM
    n_groups = WIDTH // gw
    base = 4 * WIDTH // gw

    def body(q_ref, f_ref, i_ref, lb_ref, o_ref, mask_ref):
        _hg_masks(mask_ref)
        row = _iota2((BLK, BLK), 0)
        col = _iota2((BLK, BLK), 1)
        lower_incl = (col <= row).astype(BF16)
        lb_v = lb_ref[...]

        def chunk(ci, sts):
            r0 = pl.multiple_of(ci * BLK, BLK)
            heads, bs = _hg_load((q_ref, f_ref, i_ref), r0, lb_v, lower_incl)
            qs = [hd[0] for hd in heads]
            ks = [1.0 - hd[2] for hd in heads]
            vs = [hd[5] for hd in heads]
            vbs = [v.astype(BF16) for v in vs]
            b_ends = [b[BLK - 1:BLK, :] for b in bs]
            inters = [_dot_nt((q * jnp.exp(b)).astype(BF16), st.astype(BF16)) for q, b, st in zip(qs, bs, sts)]
            scs = [None] * HG_GROUP
            for v_idx, m in enumerate(HG_LEVELS):
                _, qts, kts = _hg_level_terms(qs, ks, bs, m)
                terms = [_dot_nt(qt, kt) for qt, kt in zip(qts, kts)]
                msk = mask_ref[v_idx]
                scs = [t * msk if sc is None else sc + t * msk for sc, t in zip(scs, terms)]
            intras = [_dot(sc.astype(BF16), vb) for sc, vb in zip(scs, vbs)]
            k_decs = [(k * jnp.exp(b_end - b)).astype(BF16) for k, b, b_end in zip(ks, bs, b_ends)]
            grown = [_dot_tn(vb, k_dec) for vb, k_dec in zip(vbs, k_decs)]
            for h in range(HG_GROUP):
                diag = jnp.sum(qs[h] * ks[h], axis=-1, keepdims=True)
                o_ref[pl.ds(r0, BLK), h * HG_HEAD_DIM:(h + 1) * HG_HEAD_DIM] = inters[h] + intras[h] + diag * vs[h]
            return tuple(st * jnp.exp(b_end) + g for st, b_end, g in zip(sts, b_ends, grown))

        lax.fori_loop(0, nc, chunk, (jnp.zeros((HG_HEAD_DIM, HG_HEAD_DIM), F32),) * HG_GROUP)

    col_spec = lambda off: pl.BlockSpec((s_len, gw), lambda h: (0, off + h))
    return _pcall(
        body, name=name, out_shape=jax.ShapeDtypeStruct((s_len, WIDTH), F32),
        grid=(n_groups,),
        in_specs=[col_spec(base), col_spec(base + n_groups), col_spec(base + 2 * n_groups),
                  pl.BlockSpec((1, gw), lambda h: (0, h))],
        out_specs=pl.BlockSpec((s_len, gw), lambda h: (0, h)),
        scratch_shapes=[pltpu.VMEM((len(HG_LEVELS), BLK, BLK), F32)],
        semantics=("arbitrary",))(proj, proj, proj, lb)


def _rms_heads(o_b, norm_w):
    n_parts, h_parts, r_parts = [], [], []
    for h in range(WIDTH // HG_HEAD_DIM):
        sl = slice(h * HG_HEAD_DIM, (h + 1) * HG_HEAD_DIM)
        o = o_b[:, sl]
        rstd = lax.rsqrt(jnp.mean(o * o, axis=-1, keepdims=True) + RMS_EPS)
        ohat = o * rstd
        h_parts.append(ohat)
        n_parts.append(ohat * norm_w[:, sl])
        r_parts.append(jnp.broadcast_to(rstd, o.shape))
    cat = lambda parts: jnp.concatenate(parts, axis=-1)
    return cat(n_parts), cat(h_parts), cat(r_parts)


def _shift_rows_down(halo, cur, k):
    tm = cur.shape[0]
    ext = jnp.concatenate([halo, cur], axis=0)
    return pltpu.roll(ext, k, 0)[8:8 + tm]


def _shift_rows_up(cur, halo, k):
    tm = cur.shape[0]
    ext = jnp.concatenate([cur, halo], axis=0)
    return pltpu.roll(ext, (tm + 8 - k) % (tm + 8), 0)[0:tm]


def _merge_fwd(x, proj, o_a, o_b, gate, norm_w, conv_w, wb, w_out, ln_g, ln_b, name):
    s_len, d = x.shape
    tm = min(256, s_len)
    hb = tm // 8

    def body(x_ref, oa_ref, za_ref, ob_ref, zb_ref, pre_ref, post_ref, u_ref, zc_ref, hpre_ref, hu_ref, g_ref,
             gate_ref, nw_ref, cw_ref, wb_ref, wo_ref, lg_ref, lbias_ref, xn_ref, mg_ref, yc_ref):
        i = pl.program_id(0)
        sa, _ = _silu_and_grad(za_ref[...])
        y_a = (oa_ref[...] * sa).astype(BF16)
        n_b, _, _ = _rms_heads(ob_ref[...], nw_ref[...])
        sb, _ = _silu_and_grad(zb_ref[...])
        y_b = (n_b * sb).astype(BF16)
        a = pre_ref[...] * u_ref[...]
        halo = jnp.where(i > 0, hpre_ref[...] * hu_ref[...], 0.0)
        cw = cw_ref[...]
        conv = cw[0:1] * _shift_rows_down(halo, a, 2) + cw[1:2] * _shift_rows_down(halo, a, 1) + cw[2:3] * a
        sc, _ = _silu_and_grad(zc_ref[...])
        y_c = (post_ref[...] * conv * sc).astype(BF16)
        merged = None
        for k, yk in enumerate((y_a, y_b, y_c)):
            yc_ref[:, k * WIDTH:(k + 1) * WIDTH] = yk
            term = _sigmoid(g_ref[:, k * d:(k + 1) * d]) * _dot(yk, wb_ref[k])
            merged = term if merged is None else merged + term
        mb = merged.astype(BF16)
        mg_ref[...] = mb
        y = _dot(mb, wo_ref[...])
        r = ALPHA * x_ref[...] + (1.0 + gate_ref[...]) * y
        rhat, _ = _standardize(r)
        xn_ref[...] = rhat * lg_ref[...] + lbias_ref[...]

    wcol = lambda cb: pl.BlockSpec((tm, WIDTH), lambda i: (i, cb))
    halo_spec = lambda cb: pl.BlockSpec((8, WIDTH), lambda i: (jnp.maximum(i * hb - 1, 0), cb))
    vec = lambda w: pl.BlockSpec((1, w), lambda i: (0, 0))
    return _pcall(
        body, name=name,
        out_shape=(jax.ShapeDtypeStruct((s_len, d), F32), jax.ShapeDtypeStruct((s_len, d), BF16),
                   jax.ShapeDtypeStruct((s_len, 3 * WIDTH), BF16)),
        grid=(s_len // tm,),
        in_specs=[pl.BlockSpec((tm, d), lambda i: (i, 0)),
                  wcol(0), wcol(3), wcol(0), wcol(7), wcol(8), wcol(9), wcol(10), wcol(11),
                  halo_spec(8), halo_spec(10),
                  pl.BlockSpec((tm, 3 * d), lambda i: (i, 2)),
                  vec(d), vec(WIDTH),
                  pl.BlockSpec((3, WIDTH), lambda i: (0, 0)),
                  pl.BlockSpec((3, WIDTH, d), lambda i: (0, 0, 0)),
                  pl.BlockSpec((d, d), lambda i: (0, 0)),
                  vec(d), vec(d)],
        out_specs=(pl.BlockSpec((tm, d), lambda i: (i, 0)), pl.BlockSpec((tm, d), lambda i: (i, 0)),
                   pl.BlockSpec((tm, 3 * WIDTH), lambda i: (i, 0))),
        semantics=("arbitrary",))(x, o_a, proj, o_b, proj, proj, proj, proj, proj, proj, proj, proj,
                                  gate, norm_w, conv_w, wb, w_out, ln_g, ln_b)


def _loss_fwd_bwd(y, target):
    s_len, d = y.shape
    tm = min(512, s_len)

    def body(y_ref, t_ref, loss_ref, dy_ref):
        @pl.when(pl.program_id(0) == 0)
        def _():
            loss_ref[...] = jnp.zeros_like(loss_ref)

        e = y_ref[...] - t_ref[...]
        dy_ref[...] = e * (1.0 / d)
        part = jnp.sum(jnp.sum(e * e, axis=-1, keepdims=True), axis=0, keepdims=True)
        loss_ref[...] += part * (0.5 / d)

    tile = pl.BlockSpec((tm, d), lambda i: (i, 0))
    return _pcall(body, name="loss", grid=(s_len // tm,),
                  out_shape=(jax.ShapeDtypeStruct((1, 1), F32), jax.ShapeDtypeStruct((s_len, d), F32)),
                  in_specs=[tile, tile],
                  out_specs=(pl.BlockSpec((1, 1), lambda i: (0, 0)), tile),
                  semantics=("arbitrary",))(y, target)


def _merge_bwd(dxn, x, merged, ycat, proj, gate, wb, w_out, ln_g, name):
    s_len, d = x.shape
    tm = min(256, s_len)
    dsh = d // NDEV
    n_tiles = s_len // tm

    def body(dxn_ref, x_ref, mg_ref, yc_ref, g_ref, gate_ref, wb_ref, wo_ref, lg_ref,
             dres_ref, dyc_ref, dg_ref, gwo_out, gwb_out, vec_ref, gwo_ref, gwb_ref):
        @pl.when(pl.program_id(0) == 0)
        def _():
            gwo_ref[...] = jnp.zeros_like(gwo_ref)
            gwb_ref[...] = jnp.zeros_like(gwb_ref)
            vec_ref[...] = jnp.zeros_like(vec_ref)

        mb = mg_ref[...]
        one_gate = 1.0 + gate_ref[...]
        y = _dot(mb, wo_ref[...])
        r = ALPHA * x_ref[...] + one_gate * y
        rhat, rstd = _standardize(r)
        dxn = dxn_ref[...]
        dr = _standardize_bwd(rhat, rstd, dxn * lg_ref[...])
        vec_ref[0:1, :] += jnp.sum(dxn * rhat, axis=0, keepdims=True)
        vec_ref[1:2, :] += jnp.sum(dxn, axis=0, keepdims=True)
        vec_ref[2:3, :] += jnp.sum(dr * y, axis=0, keepdims=True)
        dres_ref[...] = ALPHA * dr
        dy = (one_gate * dr).astype(BF16)
        gwo_ref[...] += _dot_tn(mb, dy)
        dmerged = _dot_nt(dy, wo_ref[...])
        for k in range(3):
            yk = yc_ref[:, k * WIDTH:(k + 1) * WIDTH]
            sg = _sigmoid(g_ref[:, k * d:(k + 1) * d])
            pk = _dot(yk, wb_ref[k])
            dg_ref[:, k * d:(k + 1) * d] = (dmerged * pk * sg * (1.0 - sg)).astype(BF16)
            dpk = (dmerged * sg).astype(BF16)
            dyc_ref[:, k * WIDTH:(k + 1) * WIDTH] = _dot_nt(dpk, wb_ref[k])
            gwb_ref[k] += _dot_tn(yk, dpk)

        @pl.when(pl.program_id(0) == n_tiles - 1)
        def _():
            for o in range(NDEV):
                gwo_out[o] = gwo_ref[o * dsh:(o + 1) * dsh, :].astype(BF16)
                for k in range(3):
                    gwb_out[o, k] = gwb_ref[k, :, o * dsh:(o + 1) * dsh].astype(BF16)

    tile = lambda w: pl.BlockSpec((tm, w), lambda i: (i, 0))
    vec = pl.BlockSpec((1, d), lambda i: (0, 0))
    return _pcall(
        body, name=name,
        out_shape=(jax.ShapeDtypeStruct((s_len, d), F32), jax.ShapeDtypeStruct((s_len, 3 * WIDTH), F32),
                   jax.ShapeDtypeStruct(proj.shape, BF16), jax.ShapeDtypeStruct((NDEV, dsh, d), BF16),
                   jax.ShapeDtypeStruct((NDEV, 3, WIDTH, dsh), BF16), jax.ShapeDtypeStruct((8, d), F32)),
        grid=(n_tiles,),
        in_specs=[tile(d), tile(d), tile(d), tile(3 * WIDTH),
                  pl.BlockSpec((tm, 3 * d), lambda i: (i, 2)),
                  vec, pl.BlockSpec((3, WIDTH, d), lambda i: (0, 0, 0)),
                  pl.BlockSpec((d, d), lambda i: (0, 0)), vec],
        out_specs=(tile(d), tile(3 * WIDTH), pl.BlockSpec((tm, 3 * d), lambda i: (i, 2)),
                   pl.BlockSpec((NDEV, dsh, d), lambda i: (0, 0, 0)),
                   pl.BlockSpec((NDEV, 3, WIDTH, dsh), lambda i: (0, 0, 0, 0)),
                   pl.BlockSpec((8, d), lambda i: (0, 0))),
        scratch_shapes=[pltpu.VMEM((d, d), F32), pltpu.VMEM((3, WIDTH, d), F32)],
        semantics=("arbitrary",))(dxn, x, merged, ycat, proj, gate, wb, w_out, ln_g)


def _branch_bwd(dycat, proj, o_a, o_b, norm_w, conv_w, dproj, name):
    s_len = proj.shape[0]
    tm = min(256, s_len)
    hb = tm // 8
    n_tiles = s_len // tm

    def body(dya_ref, dyb_ref, dyc_ref, oa_ref, za_ref, ob_ref, zb_ref, pre_ref, post_ref, u_ref, zc_ref,
             hpre_ref, hu_ref, ndyc_ref, npost_ref, nzc_ref, nw_ref, cw_ref, dproj_in,
             dproj_ref, doa_ref, dob_ref, vec_ref, dza_scr, dzb_scr, dc_scr, sems):
        del dproj_in
        i = pl.program_id(0)

        @pl.when(i == 0)
        def _():
            vec_ref[...] = jnp.zeros_like(vec_ref)

        sa, dsa = _silu_and_grad(za_ref[...])
        dya = dya_ref[...]
        doa_ref[...] = dya * sa
        dza_scr[...] = (dya * oa_ref[...] * dsa).astype(BF16)
        nw = nw_ref[...]
        n_b, ohat, rstd = _rms_heads(ob_ref[...], nw)
        sb, dsb = _silu_and_grad(zb_ref[...])
        dyb = dyb_ref[...]
        dzb_scr[...] = (dyb * n_b * dsb).astype(BF16)
        dn = dyb * sb
        vec_ref[0:1, :] += jnp.sum(dn * ohat, axis=0, keepdims=True)
        dnw = dn * nw
        parts = []
        for h in range(WIDTH // HG_HEAD_DIM):
            sl = slice(h * HG_HEAD_DIM, (h + 1) * HG_HEAD_DIM)
            m2 = jnp.mean(dnw[:, sl] * ohat[:, sl], axis=-1, keepdims=True)
            parts.append(rstd[:, sl] * (dnw[:, sl] - ohat[:, sl] * m2))
        dob_ref[...] = jnp.concatenate(parts, axis=-1)
        cw = cw_ref[...]
        pre, u, post = pre_ref[...], u_ref[...], post_ref[...]
        a = pre * u
        halo = jnp.where(i > 0, hpre_ref[...] * hu_ref[...], 0.0)
        a1 = _shift_rows_down(halo, a, 1)
        a2 = _shift_rows_down(halo, a, 2)
        conv = cw[0:1] * a2 + cw[1:2] * a1 + cw[2:3] * a
        sc, dsc = _silu_and_grad(zc_ref[...])
        dyc = dyc_ref[...]
        dconv = dyc * post * sc
        nsc, _ = _silu_and_grad(nzc_ref[...])
        nxt = jnp.where(i < n_tiles - 1, ndyc_ref[...] * npost_ref[...] * nsc, 0.0)
        da = cw[2:3] * dconv + cw[1:2] * _shift_rows_up(dconv, nxt, 1) + cw[0:1] * _shift_rows_up(dconv, nxt, 2)
        dc_scr[:, 0 * WIDTH:1 * WIDTH] = (da * u).astype(BF16)
        dc_scr[:, 1 * WIDTH:2 * WIDTH] = (dyc * conv * sc).astype(BF16)
        dc_scr[:, 2 * WIDTH:3 * WIDTH] = (da * pre).astype(BF16)
        dc_scr[:, 3 * WIDTH:4 * WIDTH] = (dyc * post * conv * dsc).astype(BF16)
        vec_ref[1:2, :] += jnp.sum(dconv * a2, axis=0, keepdims=True)
        vec_ref[2:3, :] += jnp.sum(dconv * a1, axis=0, keepdims=True)
        vec_ref[3:4, :] += jnp.sum(dconv * a, axis=0, keepdims=True)
        rows = pl.ds(pl.multiple_of(i * tm, tm), tm)
        copies = [pltpu.make_async_copy(dza_scr, dproj_ref.at[rows, 3 * WIDTH:4 * WIDTH], sems.at[0]),
                  pltpu.make_async_copy(dzb_scr, dproj_ref.at[rows, 7 * WIDTH:8 * WIDTH], sems.at[1]),
                  pltpu.make_async_copy(dc_scr, dproj_ref.at[rows, 8 * WIDTH:12 * WIDTH], sems.at[2])]
        for cp in copies:
            cp.start()
        for cp in copies:
            cp.wait()

    wcol = lambda cb: pl.BlockSpec((tm, WIDTH), lambda i: (i, cb))
    prev = lambda cb: pl.BlockSpec((8, WIDTH), lambda i: (jnp.maximum(i * hb - 1, 0), cb))
    nxt = lambda cb: pl.BlockSpec((8, WIDTH), lambda i: (jnp.minimum((i + 1) * hb, s_len // 8 - 1), cb))
    anyspec = pl.BlockSpec(memory_space=pl.ANY)
    out = jax.ShapeDtypeStruct((s_len, WIDTH), F32)
    return _pcall(
        body, name=name,
        out_shape=(jax.ShapeDtypeStruct(dproj.shape, dproj.dtype), out, out, jax.ShapeDtypeStruct((8, WIDTH), F32)),
        grid=(n_tiles,),
        in_specs=[wcol(0), wcol(1), wcol(2), wcol(0), wcol(3), wcol(0), wcol(7), wcol(8), wcol(9), wcol(10), wcol(11),
                  prev(8), prev(10), nxt(2), nxt(9), nxt(11),
                  pl.BlockSpec((1, WIDTH), lambda i: (0, 0)), pl.BlockSpec((3, WIDTH), lambda i: (0, 0)), anyspec],
        out_specs=(anyspec, wcol(0), wcol(0), pl.BlockSpec((8, WIDTH), lambda i: (0, 0))),
        scratch_shapes=[pltpu.VMEM((tm, WIDTH), BF16), pltpu.VMEM((tm, WIDTH), BF16),
                        pltpu.VMEM((tm, 4 * WIDTH), BF16), pltpu.SemaphoreType.DMA((3,))],
        aliases={18: 0},
        semantics=("arbitrary",))(dycat, dycat, dycat, o_a, proj, o_b, proj, proj, proj, proj, proj,
                                  proj, proj, dycat, proj, proj, norm_w, conv_w, dproj)


def _sb_bwd(proj, do_a, totals, dproj, name):
    s_len = proj.shape[0]
    nb = s_len // BLK
    n_pairs = WIDTH // BLK
    scale = SB_HEAD_DIM ** -0.5
    gb = _sb_group_blocks(nb)
    kw = gb * BLK

    def body(q_ref, k_ref, v_ref, do_ref, tot_ref, dproj_in, dproj_ref, dq_ref, dk_ref, dv_ref, out_scr, sems):
        del dproj_in
        lane = _iota2((1, BLK), 1)
        row = _iota2((BLK, BLK), 0)
        col = _iota2((BLK, BLK), 1)
        ones = jnp.ones((BLK, BLK), BF16)
        twice = lambda m: jnp.concatenate([m, m], axis=0)
        before_and_sum = twice(jnp.concatenate([(row < col).astype(BF16), ones], axis=1))
        upto_and_sum = twice(jnp.concatenate([(row <= col).astype(BF16), ones], axis=1))
        qpos = _iota2((BLK, kw), 0)
        kpos = _iota2((BLK, kw), 1)
        head_lanes = [(lane // SB_HEAD_DIM) == hh for hh in range(2)]
        dk_ref[...] = jnp.zeros_like(dk_ref)
        dv_ref[...] = jnp.zeros_like(dv_ref)

        causal = kpos - qpos

        def scores(i, gi, qms):
            c0 = pl.multiple_of(gi * kw, kw)
            kb = k_ref[pl.ds(c0, kw), :].astype(BF16)
            valid = causal < i * BLK - c0
            return tuple(jnp.where(valid, _dot_nt(qms[hh], kb), MASKED_SCORE) for hh in range(2))

        def process(gi, z2s, qms, doms, totals_i, carry):
            c0 = pl.multiple_of(gi * kw, kw)
            kf = k_ref[pl.ds(c0, kw), :]
            vf = v_ref[pl.ds(c0, kw), :]
            kms = [jnp.where(head_lanes[hh], kf, 0.0).astype(BF16) for hh in range(2)]
            vms = [jnp.where(head_lanes[hh], vf, 0.0).astype(BF16) for hh in range(2)]
            das = [_dot_nt(doms[hh], vms[hh]) for hh in range(2)]
            halves = [_softplus2_parts(z2) for z2 in z2s]
            terms = [[_split2_lanes(sp2[:, b * BLK:(b + 1) * BLK]) for b in range(gb)] for sp2, _ in halves]
            sums = [[_dot(t, before_and_sum) for t in head_terms] for head_terms in terms]
            weights, gmats, l_befores = [], [], []
            for hh in range(2):
                l_before = carry[3 * hh + 1]
                parts = []
                for b in range(gb):
                    parts.append(totals_i[hh] - l_before - sums[hh][b][:, :BLK])
                    l_before = l_before + sums[hh][b][:, BLK:]
                a = jnp.exp2(z2s[hh] - jnp.concatenate(parts, axis=1))
                weights.append(a.astype(BF16))
                gmats.append(a * das[hh])
                l_befores.append(l_before)
            terms = [[_split2_lanes(g[:, b * BLK:(b + 1) * BLK]) for b in range(gb)] for g in gmats]
            sums = [[_dot(t, upto_and_sum) for t in head_terms] for head_terms in terms]
            dzs, g_befores = [], []
            for hh in range(2):
                g_before = carry[3 * hh + 2]
                parts = []
                for b in range(gb):
                    parts.append(g_before + sums[hh][b][:, :BLK])
                    g_before = g_before + sums[hh][b][:, BLK:]
                dzs.append((gmats[hh] - halves[hh][1] * jnp.concatenate(parts, axis=1)).astype(BF16))
                g_befores.append(g_before)
            dks = [_dot_tn(dzs[hh], qms[hh]) for hh in range(2)]
            dvs = [_dot_tn(weights[hh], doms[hh]) for hh in range(2)]
            dqs = [_dot(dzs[hh], kms[hh]) for hh in range(2)]
            dk_ref[pl.ds(c0, kw), :] += (dks[0] + dks[1]) * (1.0 / LOG2E)
            dv_ref[pl.ds(c0, kw), :] += dvs[0] + dvs[1]
            return (carry[0] + dqs[0], l_befores[0], g_befores[0], carry[3] + dqs[1], l_befores[1], g_befores[1])

        def queries(i):
            qf = q_ref[pl.ds(pl.multiple_of(i * BLK, BLK), BLK), :] * (scale * LOG2E)
            return [jnp.where(head_lanes[hh], qf, 0.0).astype(BF16) for hh in range(2)]

        def qblock(i, first_scores):
            r0 = pl.multiple_of(i * BLK, BLK)
            qms = queries(i)
            dof = do_ref[pl.ds(r0, BLK), :]
            doms = [jnp.where(head_lanes[hh], dof, 0.0).astype(BF16) for hh in range(2)]
            totals_i = [tot_ref[hh, pl.ds(r0, BLK), :] for hh in range(2)]
            zero = jnp.zeros((BLK, BLK), F32)
            last = i // gb

            def step(gi, state):
                return scores(i, gi + 1, qms) + process(gi, state[:2], qms, doms, totals_i, state[2:])

            state = lax.fori_loop(0, last, step, first_scores + (zero,) * 6)
            nxt = jnp.minimum(i + 1, nb - 1)
            next_scores = scores(nxt, 0, queries(nxt))
            carry = process(last, state[:2], qms, doms, totals_i, state[2:])
            dq_ref[pl.ds(r0, BLK), :] = (carry[0] + carry[3]) * scale
            return next_scores

        lax.fori_loop(0, nb, qblock, scores(0, 0, queries(0)))
        pair = pl.program_id(0)
        copies = []
        for t, ref in enumerate((dq_ref, dk_ref, dv_ref)):
            out_scr[t] = ref[...].astype(BF16)
            col = pl.multiple_of((t * n_pairs + pair) * BLK, BLK)
            copies.append(pltpu.make_async_copy(out_scr.at[t], dproj_ref.at[:, pl.ds(col, BLK)], sems.at[t]))
            copies[-1].start()
        for cp in copies:
            cp.wait()

    col_spec = lambda off: pl.BlockSpec((s_len, BLK), lambda p: (0, off + p))
    anyspec = pl.BlockSpec(memory_space=pl.ANY)
    return _pcall(
        body, name=name, out_shape=jax.ShapeDtypeStruct(dproj.shape, dproj.dtype), grid=(n_pairs,),
        in_specs=[col_spec(0), col_spec(n_pairs), col_spec(2 * n_pairs), col_spec(0),
                  pl.BlockSpec((2, s_len, BLK), lambda p: (p, 0, 0)), anyspec],
        out_specs=anyspec,
        scratch_shapes=[pltpu.VMEM((s_len, BLK), F32)] * 3 + [pltpu.VMEM((3, s_len, BLK), BF16),
                                                              pltpu.SemaphoreType.DMA((3,))],
        aliases={5: 0},
        semantics=("arbitrary",))(proj, proj, proj, do_a, totals, dproj)


def _hgrn_bwd(proj, do_b, lb, dproj, name):
    s_len = proj.shape[0]
    nc = s_len // BLK
    gw = HG_GROUP * HG_HEAD_DIM
    n_groups = WIDTH // gw
    base = 4 * WIDTH // gw
    heads_of = range(HG_GROUP)

    def body(q_ref, f_ref, i_ref, do_ref, lb_ref, dproj_in, dproj_ref, dlb_ref, mask_ref, st_ref, out_scr, sems):
        del dproj_in
        _hg_masks(mask_ref)
        row = _iota2((BLK, BLK), 0)
        col = _iota2((BLK, BLK), 1)
        lower_incl = (col <= row).astype(BF16)
        upper_incl = (col >= row).astype(BF16)
        lb_v = lb_ref[...]
        refs = (q_ref, f_ref, i_ref)

        def fwd_chunk(ci, sts):
            for h in heads_of:
                st_ref[ci, h] = sts[h]
            heads, bs = _hg_load(refs, pl.multiple_of(ci * BLK, BLK), lb_v, lower_incl)
            b_ends = [b[BLK - 1:BLK, :] for b in bs]
            k_decs = [((1.0 - hd[2]) * jnp.exp(b_end - b)).astype(BF16) for hd, b, b_end in zip(heads, bs, b_ends)]
            grown = [_dot_tn(hd[5].astype(BF16), k_dec) for hd, k_dec in zip(heads, k_decs)]
            return tuple(st * jnp.exp(b_end) + g for st, b_end, g in zip(sts, b_ends, grown))

        zero_state = (jnp.zeros((HG_HEAD_DIM, HG_HEAD_DIM), F32),) * HG_GROUP
        lax.fori_loop(0, nc, fwd_chunk, zero_state)

        def bwd_chunk(cc, carry):
            dsts, suffixes, dlbs = carry
            ci = nc - 1 - cc
            r0 = pl.multiple_of(ci * BLK, BLK)
            heads, bs = _hg_load(refs, r0, lb_v, lower_incl)
            qs = [hd[0] for hd in heads]
            fs = [hd[2] for hd in heads]
            ks = [1.0 - f for f in fs]
            vs = [hd[5] for hd in heads]
            vbs = [v.astype(BF16) for v in vs]
            dos = [do_ref[pl.ds(r0, BLK), h * HG_HEAD_DIM:(h + 1) * HG_HEAD_DIM] for h in heads_of]
            dobs = [do.astype(BF16) for do in dos]
            b_ends = [b[BLK - 1:BLK, :] for b in bs]
            e_qs = [jnp.exp(b) for b in bs]
            e_ks = [jnp.exp(b_end - b) for b, b_end in zip(bs, b_ends)]
            qes = [(q * e).astype(BF16) for q, e in zip(qs, e_qs)]
            khs = [(k * e).astype(BF16) for k, e in zip(ks, e_ks)]
            st_terms = [_split2_lanes(st_ref[ci, h]) for h in heads_of]
            ds_terms = [_split2_lanes(dst) for dst in dsts]
            dqes = [_dot(dob, t[:, :HG_HEAD_DIM]) + _dot(dob, t[:, HG_HEAD_DIM:]) for dob, t in zip(dobs, st_terms)]
            dkhs = [_dot(vb, t[:, :HG_HEAD_DIM]) + _dot(vb, t[:, HG_HEAD_DIM:]) for vb, t in zip(vbs, ds_terms)]
            dvs = [_dot_nt(kh, t[:, :HG_HEAD_DIM]) for kh, t in zip(khs, ds_terms)]
            grown = [_dot_tn(dob, qe) for dob, qe in zip(dobs, qes)]
            das = [_dot_nt(dob, vb) for dob, vb in zip(dobs, vbs)]
            dqs = [e * dqe for e, dqe in zip(e_qs, dqes)]
            dks = [e * dkh for e, dkh in zip(e_ks, dkhs)]
            dlogs = [qe.astype(F32) * dqe - kh.astype(F32) * dkh for qe, dqe, kh, dkh in zip(qes, dqes, khs, dkhs)]
            scs = [None] * HG_GROUP
            for v_idx, m in enumerate(HG_LEVELS):
                es, qms, kms = _hg_level_terms(qs, ks, bs, m)
                msk = mask_ref[v_idx]
                terms = [_dot_nt(qm, km) for qm, km in zip(qms, kms)]
                pms = [(da * msk).astype(BF16) for da in das]
                dqms = [_dot(pm, km) for pm, km in zip(pms, kms)]
                dkms = [_dot_tn(pm, qm) for pm, qm in zip(pms, qms)]
                scs = [t * msk if sc is None else sc + t * msk for sc, t in zip(scs, terms)]
                dqs = [dq + dqm * e for dq, dqm, e in zip(dqs, dqms, es)]
                dks = [dk + dkm * e for dk, dkm, e in zip(dks, dkms, es)]
                dlogs = [dl + (qm.astype(F32) * dqm - km.astype(F32) * dkm)
                         for dl, qm, dqm, km, dkm in zip(dlogs, qms, dqms, kms, dkms)]
            intras = [_dot_tn(sc.astype(BF16), dob) for sc, dob in zip(scs, dobs)]
            dgs = [_dot_exact_l(upper_incl, dl) + sfx for dl, sfx in zip(dlogs, suffixes)]
            new_dlbs = []
            for h in heads_of:
                q, dq_fac, f, sig = heads[h][0], heads[h][1], heads[h][2], heads[h][3]
                a_diag = jnp.sum(dos[h] * vs[h], axis=-1, keepdims=True)
                s_diag = jnp.sum(q * ks[h], axis=-1, keepdims=True)
                dq = dqs[h] + a_diag * ks[h]
                dk = dks[h] + a_diag * q
                dv = dvs[h] + intras[h] + s_diag * dos[h]
                dfull = dgs[h] / f - dk
                sl = slice(h * HG_HEAD_DIM, (h + 1) * HG_HEAD_DIM)
                out_scr[0, pl.ds(r0, BLK), sl] = (dq * dq_fac).astype(BF16)
                out_scr[1, pl.ds(r0, BLK), sl] = (dfull * (1.0 - lb_v[:, sl]) * sig * (1.0 - sig)).astype(BF16)
                out_scr[2, pl.ds(r0, BLK), sl] = dv.astype(BF16)
                new_dlbs.append(dlbs[h] + jnp.sum(dfull * (1.0 - sig), axis=0, keepdims=True))
            new_dsts = tuple(dst * jnp.exp(b_end) + g for dst, b_end, g in zip(dsts, b_ends, grown))
            return new_dsts, tuple(dg[0:1, :] for dg in dgs), tuple(new_dlbs)

        zero_row = (jnp.zeros((1, HG_HEAD_DIM), F32),) * HG_GROUP
        _, _, dlbs = lax.fori_loop(0, nc, bwd_chunk, (zero_state, zero_row, zero_row))
        dlb_ref[...] = jnp.broadcast_to(jnp.concatenate(dlbs, axis=1), dlb_ref.shape)
        group = pl.program_id(0)
        copies = []
        for t in range(3):
            col = pl.multiple_of((base + t * n_groups + group) * gw, gw)
            copies.append(pltpu.make_async_copy(out_scr.at[t], dproj_ref.at[:, pl.ds(col, gw)], sems.at[t]))
            copies[-1].start()
        for cp in copies:
            cp.wait()

    col_spec = lambda off: pl.BlockSpec((s_len, gw), lambda h: (0, off + h))
    anyspec = pl.BlockSpec(memory_space=pl.ANY)
    return _pcall(
        body, name=name,
        out_shape=(jax.ShapeDtypeStruct(dproj.shape, dproj.dtype), jax.ShapeDtypeStruct((8, WIDTH), F32)),
        grid=(n_groups,),
        in_specs=[col_spec(base), col_spec(base + n_groups), col_spec(base + 2 * n_groups), col_spec(0),
                  pl.BlockSpec((1, gw), lambda h: (0, h)), anyspec],
        out_specs=(anyspec, pl.BlockSpec((8, gw), lambda h: (0, h))),
        scratch_shapes=[pltpu.VMEM((len(HG_LEVELS), BLK, BLK), F32),
                        pltpu.VMEM((nc, HG_GROUP, HG_HEAD_DIM, HG_HEAD_DIM), F32),
                        pltpu.VMEM((3, s_len, gw), BF16), pltpu.SemaphoreType.DMA((3,))],
        aliases={5: 0},
        semantics=("arbitrary",))(proj, proj, proj, do_b, lb, dproj)


def _dh_matmul(dproj, w_full, after, name):
    s_len, n = dproj.shape
    d = w_full.shape[0]
    tm = min(512, s_len)
    tk = 1536

    def body(dp_ref, w_ref, after_ref, dh_ref):
        del after_ref
        part = _dot_nt(dp_ref[...], w_ref[...])

        @pl.when(pl.program_id(1) == 0)
        def _():
            dh_ref[...] = part

        @pl.when(pl.program_id(1) > 0)
        def _():
            dh_ref[...] += part

    return _pcall(
        body, name=name, out_shape=jax.ShapeDtypeStruct((s_len, d), F32),
        grid=(s_len // tm, n // tk),
        in_specs=[pl.BlockSpec((tm, tk), lambda i, k: (i, k)), pl.BlockSpec((d, tk), lambda i, k: (0, k)),
                  pl.BlockSpec(memory_space=pl.ANY)],
        out_specs=pl.BlockSpec((tm, d), lambda i, k: (i, 0)),
        semantics=("arbitrary", "arbitrary"))(dproj, w_full, after)


def _gw_matmul(h_t, dproj, name):
    d, s_len = h_t.shape
    n = dproj.shape[1]
    tn = 1152

    def body(ht_ref, dp_ref, gw_ref):
        gw_ref[...] = _dot(ht_ref[...], dp_ref[...]).astype(BF16)

    return _pcall(
        body, name=name, out_shape=jax.ShapeDtypeStruct((d, n), BF16),
        grid=(n // tn,),
        in_specs=[pl.BlockSpec((d, s_len), lambda j: (0, 0)), pl.BlockSpec((s_len, tn), lambda j: (0, j))],
        out_specs=pl.BlockSpec((d, tn), lambda j: (0, j)),
        semantics=("arbitrary",))(h_t, dproj)


def _ln_bwd(dh, x, scale, dres, name):
    s_len, d = x.shape
    tm = min(512, s_len)

    def body(dh_ref, x_ref, sc_ref, dres_ref, dx_ref, vec_ref):
        @pl.when(pl.program_id(0) == 0)
        def _():
            vec_ref[...] = jnp.zeros_like(vec_ref)

        dh = dh_ref[...]
        xs, rstd = _standardize(x_ref[...])
        vec_ref[0:1, :] += jnp.sum(dh, axis=0, keepdims=True)
        vec_ref[1:2, :] += jnp.sum(dh * xs, axis=0, keepdims=True)
        dx_ref[...] = _standardize_bwd(xs, rstd, dh * (1.0 + sc_ref[...])) + dres_ref[...]

    tile = pl.BlockSpec((tm, d), lambda i: (i, 0))
    return _pcall(body, name=name, grid=(s_len // tm,),
                  out_shape=(jax.ShapeDtypeStruct((s_len, d), F32), jax.ShapeDtypeStruct((8, d), F32)),
                  in_specs=[tile, tile, pl.BlockSpec((1, d), lambda i: (0, 0)), tile],
                  out_specs=(tile, pl.BlockSpec((8, d), lambda i: (0, 0))),
                  semantics=("arbitrary",))(dh, x, scale, dres)


def _wmod_grad(c_t, dmod):
    d = c_t.shape[0]
    n_layers, _, cm = dmod.shape

    def body(c_ref, dm_ref, o_ref):
        for l in range(n_layers):
            acc = None
            for b in range(NDEV):
                term = c_ref[:, b:b + 1] * dm_ref[l, b:b + 1, :]
                acc = term if acc is None else acc + term
            o_ref[l] = acc

    return _pcall(body, name="wmod_grad", out_shape=jax.ShapeDtypeStruct((n_layers, d, cm), F32))(c_t, dmod)


def _sum_adamw(parts_list, w, m, v, name):
    n_ranges = len(parts_list)
    n_src, range_rows, cols = parts_list[0].shape
    rows = range_rows * n_ranges
    tr = range_rows
    for cand in (512, 256, 128, 64, 32, 16, 8):
        if range_rows % cand == 0 and cand * cols * 4 <= (2 << 20):
            tr = cand
            break
    tiles = range_rows // tr

    def body(*refs):
        p_refs = refs[:n_ranges]
        w_ref, m_ref, v_ref, g_ref, d_ref, nm_ref, nv_ref = refs[n_ranges:]

        def step(p_ref):
            g = p_ref[0].astype(F32)
            for s in range(1, n_src):
                g = g + p_ref[s].astype(F32)
            nm = ADAM_B1 * m_ref[...] + (1.0 - ADAM_B1) * g
            nv = ADAM_B2 * v_ref[...] + (1.0 - ADAM_B2) * (g * g)
            m_hat = nm / (1.0 - ADAM_B1 ** ADAM_STEP)
            v_hat = nv / (1.0 - ADAM_B2 ** ADAM_STEP)
            g_ref[...] = g
            d_ref[...] = -ADAM_LR * (m_hat / (jnp.sqrt(v_hat) + ADAM_EPS) + ADAM_WD * w_ref[...])
            nm_ref[...] = nm
            nv_ref[...] = nv

        if n_ranges == 1:
            step(p_refs[0])
        else:
            for j in range(n_ranges):
                @pl.when(pl.program_id(0) // tiles == j)
                def _(j=j):
                    step(p_refs[j])

    def part_spec(j):
        return pl.BlockSpec((n_src, tr, cols), lambda i: (0, jnp.clip(i - j * tiles, 0, tiles - 1), 0))

    tile = pl.BlockSpec((tr, cols), lambda i: (i, 0))
    out = jax.ShapeDtypeStruct((rows, cols), F32)
    return _pcall(body, name=name, grid=(rows // tr,), out_shape=(out,) * 4,
                  in_specs=[part_spec(j) for j in range(n_ranges)] + [tile, tile, tile],
                  out_specs=(tile,) * 4, semantics=("arbitrary",))(*parts_list, w, m, v)


def _sum_parts(parts, name):
    n_src = parts.shape[0]

    def body(p_ref, o_ref):
        acc = p_ref[0]
        for s in range(1, n_src):
            acc = acc + p_ref[s]
        o_ref[...] = acc

    return _pcall(body, name=name, out_shape=jax.ShapeDtypeStruct(parts.shape[1:], F32))(parts)


def _pair_sum(gw, stage, me, name):
    d = gw.shape[0]
    n_slots, _, shard = stage.shape

    def body(me_ref, g_ref, s_ref, own_ref, o_ref):
        del me_ref
        total = (g_ref[...].astype(F32) + s_ref[0].astype(F32)).astype(BF16)
        o_ref[0] = total

        @pl.when(pl.program_id(0) == 0)
        def _():
            own_ref[0] = total

    slot = pl.BlockSpec((1, d, shard), lambda jj, me_ref: (jj, 0, 0))
    out = jax.ShapeDtypeStruct(stage.shape, BF16)
    return pl.pallas_call(
        body, name=name, out_shape=(out, out),
        grid_spec=pltpu.PrefetchScalarGridSpec(
            num_scalar_prefetch=1, grid=(n_slots,),
            in_specs=[pl.BlockSpec((d, shard), lambda jj, me_ref: (0, me_ref[0] ^ (2 * jj))), slot],
            out_specs=(pl.BlockSpec((1, d, shard), lambda jj, me_ref: (0, 0, 0)), slot)),
        compiler_params=pltpu.CompilerParams(dimension_semantics=("arbitrary",), vmem_limit_bytes=VMEM_LIMIT),
        interpret=False)(me.reshape(1).astype(jnp.int32), gw, stage)


def _lower_bound_table(lower_bounds):
    p = jax.nn.softmax(lower_bounds.astype(F32), axis=0)
    return jnp.cumsum(p, axis=0) - p[0:1]


def _pad_rows(v, width):
    n = v.shape[0]
    rows = -(-n // width)
    rows = -(-rows // 8) * 8
    return jnp.pad(v, (0, rows * width - n)).reshape(rows, width)


def kernel(x, c, w_mod, b_mod, w_in, conv_w, hgrn_norm_w, lower_bounds, w_branch, w_out, ln_g, ln_b, loss_target, m_w_mod, m_b_mod, m_w_in, m_conv_w, m_hgrn_norm_w, m_lower_bounds, m_w_branch, m_w_out, m_ln_g, m_ln_b, v_w_mod, v_b_mod, v_w_in, v_conv_w, v_hgrn_norm_w, v_lower_bounds, v_w_branch, v_w_out, v_ln_g, v_ln_b):
    n_layers = N_LAYERS
    s_len, d = x.shape[1], x.shape[2]
    n_cols = w_in.shape[2] * NDEV
    cw_cols = conv_w.shape[2]
    cm = w_mod.shape[2]
    me = _my_index()
    x0 = x[0]
    target = loss_target[0]

    small = _pad_rows(jnp.concatenate([c.reshape(-1), conv_w.reshape(-1)]), BLK)
    small_all = _all_gather_small("gather_c_conv", small).reshape(NDEV, -1)
    c_all = small_all[:, :d]
    conv_full = small_all[:, d:d + n_layers * 3 * cw_cols].reshape(NDEV, n_layers, 3, cw_cols)
    conv_full = conv_full.transpose(1, 2, 0, 3).reshape(n_layers, 3, WIDTH)

    b_mod_mine = lax.dynamic_slice_in_dim(b_mod, me * cm, cm, axis=1).reshape(n_layers, 1, cm)
    mod_cols = _mod_fwd(c_all, w_mod, b_mod_mine)
    mod_all = _all_gather_small("gather_mod", mod_cols.reshape(n_layers * NDEV, cm))
    mod_all = mod_all.reshape(NDEV, n_layers, NDEV, cm)
    mod_mine = lax.dynamic_index_in_dim(mod_all, me, axis=2, keepdims=False)
    mod_mine = mod_mine.transpose(1, 0, 2).reshape(n_layers, 3, 1, d)

    shard = w_in.shape[2]
    dsh = d // NDEV
    w_in_b, w_branch_b, w_out_b = w_in.astype(BF16), w_branch.astype(BF16), w_out.astype(BF16)
    window = lambda ref, dev: ref.at[:, pl.ds(pl.multiple_of(dev * shard, BLK), shard)]

    def two_step_sends(places):
        chips, sibling = [], []
        for k in (1, 2, 4, 6):
            for a, place in enumerate(places):
                chips.append((k, lambda ins, lands, me, a=a: ins[a],
                              lambda lands, me, a=a, place=place: place(lands[a], me),
                              lambda lands, me, a=a, k=k, place=place: place(lands[a], me ^ k)))
        for j in (2, 4, 6):
            for a, place in enumerate(places):
                sibling.append((1, lambda ins, lands, me, a=a, j=j, place=place: place(lands[a], me ^ j),
                                lambda lands, me, a=a, j=j, place=place: place(lands[a], me ^ j),
                                lambda lands, me, a=a, j=j, place=place: place(lands[a], me ^ 1 ^ j)))
        return chips, sibling

    in_sends = two_step_sends([window])
    rest_sends = two_step_sends([_slot, _slot])
    layer_sends = two_step_sends([window, _slot, _slot])

    def in_land(l):
        return _place_own_window(f"place_w_in_{l}", (d, n_cols), w_in_b[l], me)

    def rest_lands(l):
        return [_place_own((NDEV, 3, WIDTH, dsh), BF16, w_branch_b[l][None], (me, 0, 0, 0)),
                _place_own((NDEV, dsh, d), BF16, w_out_b[l][None], (me, 0, 0))]

    def gather_start(name, shards, lands, sends, after):
        return _exchange_start(f"{name}_chips_start", shards, lands, sends[0], after)

    def gather_pass_on(name, started, after, sends):
        _, lands = _exchange_wait(f"{name}_chips_wait", started, after, sends[0])
        return _exchange_start(f"{name}_sibling_start", [], lands, sends[1])

    def gather_finish(name, started, after, sends):
        return _exchange_wait(f"{name}_sibling_wait", started, after, sends[1])[1]

    def branch_out_weights(w_branch_l, w_out_l):
        return w_branch_l.transpose(1, 2, 0, 3).reshape(3, WIDTH, d), w_out_l.reshape(d, d)

    gathering = gather_start("gather_w_in_0", [w_in_b[0]], [in_land(0)], in_sends, mod_mine)
    passing = gather_pass_on("gather_w_in_0", gathering, gathering[4], in_sends)
    rest_gathering = gather_start("gather_rest_0", [w_branch_b[0], w_out_b[0]], rest_lands(0), rest_sends, passing[4])
    next_gathering = None
    if n_layers > 1:
        next_gathering = gather_start("gather_weights_1", [w_in_b[1], w_branch_b[1], w_out_b[1]],
                                      [in_land(1)] + rest_lands(1), layer_sends, rest_gathering[4])
    w_in_l = gather_finish("gather_w_in_0", passing, (next_gathering or rest_gathering)[4], in_sends)[0]

    lbs = _lower_bound_table(lower_bounds)
    norm_w4 = jnp.tile(hgrn_norm_w, (1, WIDTH // HG_HEAD_DIM))

    saved = []
    xl = x0
    for l in range(n_layers):
        shift, scale, gate = mod_mine[l, 0], mod_mine[l, 1], mod_mine[l, 2]
        proj, h_t = _ln_proj(xl, shift, scale, w_in_l, f"ln_proj_{l}")
        o_a, totals = _sb_fwd(proj, f"sb_fwd_{l}")
        if l == 0:
            rest_passing = gather_pass_on("gather_rest_0", rest_gathering, o_a, rest_sends)
        lb_l = lbs[l:l + 1] + rest_passing[4][0, 0] if l == 0 else lbs[l:l + 1]
        o_b = _hgrn_fwd(proj, lb_l, f"hgrn_fwd_{l}")
        if l == 0:
            wb_l, wo_l = branch_out_weights(*gather_finish("gather_rest_0", rest_passing, o_b, rest_sends))
            if n_layers > 1:
                next_passing = gather_pass_on("gather_weights_1", next_gathering, o_b, layer_sends)
                gate = gate + next_passing[4][0, 0]
        x_new, merged, ycat = _merge_fwd(xl, proj, o_a, o_b, gate, norm_w4[l:l + 1], conv_full[l],
                                         wb_l, wo_l, ln_g[l:l + 1], ln_b[l:l + 1], f"merge_fwd_{l}")
        saved.append((xl, proj, h_t, o_a, totals, o_b, merged, ycat, w_in_l, wb_l, wo_l))
        if l == 0 and n_layers > 1:
            w_in_l, w_branch_l, w_out_l = gather_finish("gather_weights_1", next_passing, x_new, layer_sends)
            wb_l, wo_l = branch_out_weights(w_branch_l, w_out_l)
        xl = x_new

    loss_part, dx = _loss_fwd_bwd(xl, target)
    loss = lax.psum(loss_part[0, 0], ("x", "y", "c"))

    pair_sends = [(1, lambda ins, lands, me, j=j: window(ins[0], me ^ 1 ^ j),
                   lambda lands, me, jj=jj: lands[0].at[jj], lambda lands, me, jj=jj: lands[0].at[jj])
                  for jj, j in enumerate((0, 2, 4, 6))]
    chip_sum_sends = [(j, lambda ins, lands, me, jj=jj: ins[0].at[jj],
                       lambda lands, me, jj=jj: lands[0].at[jj], lambda lands, me, jj=jj: lands[0].at[jj])
                      for jj, j in ((1, 2), (2, 4), (3, 6))]
    rest_scatter = _direct_sends([(0, 0, _slot, _slot), (1, 1, _slot, _slot)])
    scattering = [None] * n_layers
    small_grads = [None] * n_layers
    dmod = [None] * n_layers
    tie = None
    for l in reversed(range(n_layers)):
        xl, proj, h_t, o_a, totals, o_b, merged, ycat, w_in_l, wb_l, wo_l = saved[l]
        scale, gate = mod_mine[l, 1], mod_mine[l, 2]
        if tie is not None:
            gate = gate + tie[0, 0]
        dres, dycat, dproj, gwo_by_owner, gwb_by_owner, mvec = _merge_bwd(
            dx, xl, merged, ycat, proj, gate, wb_l, wo_l, ln_g[l:l + 1], f"merge_bwd_{l}")
        lands = [_place_own((NDEV, 3, WIDTH, dsh), BF16, lax.dynamic_slice_in_dim(gwb_by_owner, me, 1, axis=0),
                            (me, 0, 0, 0)),
                 _place_own((NDEV, dsh, d), BF16, lax.dynamic_slice_in_dim(gwo_by_owner, me, 1, axis=0),
                            (me, 0, 0))]
        rest_started = _exchange_start(f"scatter_rest_{l}_start", [gwb_by_owner, gwo_by_owner], lands, rest_scatter)
        dproj, do_a, do_b, bvec = _branch_bwd(dycat, proj, o_a, o_b, norm_w4[l:l + 1] + rest_started[4][0, 0],
                                              conv_full[l], dproj, f"branch_bwd_{l}")
        dproj = _sb_bwd(proj, do_a, totals, dproj, f"sb_bwd_{l}")
        dproj, dlb = _hgrn_bwd(proj, do_b, lbs[l:l + 1], dproj, f"hgrn_bwd_{l}")
        gwi = _gw_matmul(h_t, dproj, f"gw_matmul_{l}")
        swapping = _exchange_start(f"scatter_in_{l}_sibling_start", [gwi], [lax.empty((4, d, shard), BF16)], pair_sends)
        if l > 0:
            dh = _dh_matmul(dproj, w_in_l, swapping[4], f"dh_matmul_{l}")
        (gwi,), (stage,) = _exchange_wait(f"scatter_in_{l}_sibling_wait", swapping, dh if l > 0 else swapping[4],
                                          pair_sends)
        land, chip_sums = _pair_sum(gwi, stage, me, f"pair_sum_{l}")
        in_started = _exchange_start(f"scatter_in_{l}_chips_start", [chip_sums], [land], chip_sum_sends)
        scattering[l] = (in_started, rest_started)
        tie = in_started[4]
        if l == 0:
            dh = _dh_matmul(dproj, w_in_l, tie, f"dh_matmul_{l}")
        dx, lvec = _ln_bwd(dh, xl, scale + tie[0, 0], dres, f"ln_bwd_{l}")
        dmod[l] = jnp.concatenate([lvec[0], lvec[1], mvec[2]])
        norm_grad = bvec[0].reshape(WIDTH // HG_HEAD_DIM, HG_HEAD_DIM).sum(axis=0)
        small_grads[l] = jnp.concatenate([mvec[0], mvec[1], norm_grad, dlb[0], bvec[1:4].reshape(-1)])
    grad_x = dx[None]

    small_vec = jnp.concatenate(dmod + small_grads)
    n_small = small_vec.shape[0]
    small_all = _all_gather_small("gather_small_grads", _pad_rows(small_vec, BLK))
    small_sum = _sum_parts(small_all, "sum_small_grads").reshape(-1)[:n_small]
    dmod_all = small_all.reshape(NDEV, -1)[:, :n_layers * 3 * d].reshape(NDEV, n_layers, 3 * d)

    off = n_layers * 3 * d
    grad_b_mod = small_sum[:off].reshape(n_layers, 3 * d)
    per_layer = 2 * d + HG_HEAD_DIM + WIDTH + 3 * WIDTH
    g_ln_g, g_ln_b, g_norm, g_lbs, g_conv = [], [], [], [], []
    for l in range(n_layers):
        seg = small_sum[off + l * per_layer: off + (l + 1) * per_layer]
        g_ln_g.append(seg[:d])
        g_ln_b.append(seg[d:2 * d])
        g_norm.append(seg[2 * d:2 * d + HG_HEAD_DIM])
        g_lbs.append(seg[2 * d + HG_HEAD_DIM:2 * d + HG_HEAD_DIM + WIDTH])
        g_conv.append(seg[2 * d + HG_HEAD_DIM + WIDTH:].reshape(3, WIDTH))
    grad_ln_g, grad_ln_b = jnp.stack(g_ln_g), jnp.stack(g_ln_b)
    grad_norm = jnp.stack(g_norm)
    _, lbs_vjp = jax.vjp(_lower_bound_table, lower_bounds)
    grad_lower = lbs_vjp(jnp.stack(g_lbs))[0]
    grad_conv = lax.dynamic_slice_in_dim(jnp.stack(g_conv), me * cw_cols, cw_cols, axis=2)

    dmod_mine = lax.dynamic_slice_in_dim(dmod_all, me * cm, cm, axis=2).transpose(1, 0, 2)
    grad_w_mod = _wmod_grad(c_all.T, dmod_mine)

    p_in, p_branch, p_out = [None] * n_layers, [None] * n_layers, [None] * n_layers
    for l in reversed(range(n_layers)):
        in_started, rest_started = scattering[l]
        p_branch_l, p_out[l] = _exchange_wait(f"scatter_rest_{l}_wait", rest_started, grad_w_mod, rest_scatter)[1]
        p_branch[l] = p_branch_l.reshape(NDEV, 3 * WIDTH, dsh)
        p_in[l] = _exchange_wait(f"scatter_in_{l}_chips_wait", in_started, grad_w_mod, chip_sum_sends)[1][0]

    def adam(parts_list, w, m, v, name):
        shape = w.shape
        cols = shape[-1]
        flat = lambda a: a.reshape(-1, cols)
        outs = _sum_adamw(parts_list, flat(w), flat(m), flat(v), name)
        return [o.reshape(shape) for o in outs]

    r_w_in = adam(p_in, w_in, m_w_in, v_w_in, "adamw_w_in")
    r_w_branch = adam(p_branch, w_branch, m_w_branch, v_w_branch, "adamw_w_branch")
    r_w_out = adam(p_out, w_out, m_w_out, v_w_out, "adamw_w_out")
    r_w_mod = adam([grad_w_mod.reshape(1, -1, cm)], w_mod, m_w_mod, v_w_mod, "adamw_w_mod")

    small_names = ["b_mod", "conv_w", "hgrn_norm_w", "lower_bounds", "ln_g", "ln_b"]
    small_g = [grad_b_mod, grad_conv, grad_norm, grad_lower, grad_ln_g, grad_ln_b]
    small_w = [b_mod, conv_w, hgrn_norm_w, lower_bounds, ln_g, ln_b]
    small_m = [m_b_mod, m_conv_w, m_hgrn_norm_w, m_lower_bounds, m_ln_g, m_ln_b]
    small_v = [v_b_mod, v_conv_w, v_hgrn_norm_w, v_lower_bounds, v_ln_g, v_ln_b]
    pack = lambda arrs: _pad_rows(jnp.concatenate([a.reshape(-1) for a in arrs]), BLK)
    packed = _sum_adamw([pack(small_g)[None]], pack(small_w), pack(small_m), pack(small_v), "adamw_small")
    r_small = {n: [] for n in small_names}
    for res in packed:
        flat = res.reshape(-1)
        pos = 0
        for n, w in zip(small_names, small_w):
            r_small[n].append(flat[pos:pos + w.size].reshape(w.shape))
            pos += w.size

    results = {"w_mod": r_w_mod, "w_in": r_w_in, "w_branch": r_w_branch, "w_out": r_w_out, **r_small}
    order = ["w_mod", "b_mod", "w_in", "conv_w", "hgrn_norm_w", "lower_bounds", "w_branch", "w_out", "ln_g", "ln_b"]
    outs = [loss, grad_x]
    for idx in range(4):
        outs.extend(results[n][idx] for n in order)
    return tuple(outs)
```

```python
import jax
import jax.numpy as jnp
from jax import lax
from jax.experimental import pallas as pl
from jax.experimental.pallas import tpu as pltpu

F32 = jnp.float32
BF16 = jnp.bfloat16
NDEV = 8
N_LAYERS = 2
SB_HEAD_DIM = 64
HG_HEAD_DIM = 128
WIDTH = 512
BLK = 128
LN_EPS = 1e-5
RMS_EPS = 1e-6
ALPHA = (2.0 * N_LAYERS) ** 0.25
ADAM_LR, ADAM_B1, ADAM_B2, ADAM_EPS, ADAM_WD, ADAM_STEP = 0.001, 0.9, 0.999, 1e-08, 0.01, 10
VMEM_LIMIT = 56 * 1024 * 1024
MESH = pl.DeviceIdType.MESH
HG_LEVELS = (64, 32, 16, 8, 4, 2, 1)


def _pcall(body, *, name, out_shape, grid=None, in_specs=None, out_specs=None, scratch_shapes=(),
           semantics=None, aliases=None):
    kwargs = {}
    if grid is not None:
        kwargs["grid"] = grid
    if in_specs is not None:
        kwargs["in_specs"] = in_specs
    if out_specs is not None:
        kwargs["out_specs"] = out_specs
    if aliases:
        kwargs["input_output_aliases"] = aliases
    return pl.pallas_call(
        body, name=name, out_shape=out_shape, scratch_shapes=list(scratch_shapes),
        compiler_params=pltpu.CompilerParams(dimension_semantics=semantics, vmem_limit_bytes=VMEM_LIMIT),
        interpret=False, **kwargs)


def _dot(a, b):
    return jnp.dot(a, b, preferred_element_type=F32)


def _dot_nt(a, b):
    return lax.dot_general(a, b, (((1,), (1,)), ((), ())), preferred_element_type=F32)


def _dot_tn(a, b):
    return lax.dot_general(a, b, (((0,), (0,)), ((), ())), preferred_element_type=F32)


def _split3(x):
    x1 = x.astype(BF16)
    r1 = x - x1.astype(F32)
    x2 = r1.astype(BF16)
    r2 = r1 - x2.astype(F32)
    return x1, x2, r2.astype(BF16)


def _split2(x):
    x1 = x.astype(BF16)
    return x1, (x - x1.astype(F32)).astype(BF16)


def _dot_exact_l(m_bf16, x):
    x1, x2, x3 = _split3(x)
    return _dot(m_bf16, x1) + _dot(m_bf16, x2) + _dot(m_bf16, x3)


def _sigmoid(x):
    return 1.0 / (1.0 + jnp.exp(-x))


def _silu_and_grad(x):
    s = _sigmoid(x)
    return x * s, s * (1.0 + x * (1.0 - s))


LOG2E = 1.4426950408889634
MASKED_SCORE = -1e30


def _softplus2_parts(z2):
    minus_abs = lax.bitcast_convert_type(lax.bitcast_convert_type(z2, jnp.int32) | jnp.int32(-2 ** 31), F32)
    e = jnp.exp2(minus_abs)
    sp2 = jnp.maximum(z2, 0.0) + jnp.log2(1.0 + e)
    r = 1.0 / (1.0 + e)
    return sp2, jnp.where(z2 >= 0.0, r, e * r)


def _split2_lanes(x):
    x1 = x.astype(BF16)
    return jnp.concatenate([x1, (x - x1.astype(F32)).astype(BF16)], axis=1)


def _iota2(shape, dim):
    return lax.broadcasted_iota(jnp.int32, shape, dim)


def _standardize(x):
    mu = jnp.mean(x, axis=-1, keepdims=True)
    xc = x - mu
    var = jnp.mean(xc * xc, axis=-1, keepdims=True)
    rstd = lax.rsqrt(var + LN_EPS)
    return xc * rstd, rstd


def _standardize_bwd(xhat, rstd, dxhat):
    m1 = jnp.mean(dxhat, axis=-1, keepdims=True)
    m2 = jnp.mean(dxhat * xhat, axis=-1, keepdims=True)
    return rstd * (dxhat - m1 - xhat * m2)


def _my_index():
    return 4 * lax.axis_index("x") + 2 * lax.axis_index("y") + lax.axis_index("c")


def _exchange(name, ins, out_shapes, transfers, in_vmem):
    n_in, n_out, n_t = len(ins), len(out_shapes), len(transfers)

    def body(*refs):
        in_refs, out_refs = refs[:n_in], refs[n_in:n_in + n_out]
        send_sems, recv_sems, local_sems = refs[n_in + n_out:]
        x, y, c = lax.axis_index("x"), lax.axis_index("y"), lax.axis_index("c")
        me = 4 * x + 2 * y + c
        started = []
        for t, (i, o, src_fn, dst_fn) in enumerate(transfers):
            own = pltpu.make_async_copy(src_fn(in_refs[i], me), dst_fn(out_refs[o], me), local_sems.at[t])
            own.start()
            started.append(own)
        arrivals = []
        for k in range(1, NDEV):
            px = x ^ ((k >> 2) & 1)
            py = y ^ ((k >> 1) & 1)
            pc = c ^ (k & 1)
            peer = 4 * px + 2 * py + pc
            for t, (i, o, src_fn, dst_fn) in enumerate(transfers):
                sem = t * (NDEV - 1) + k - 1
                push = pltpu.make_async_remote_copy(
                    src_ref=src_fn(in_refs[i], peer), dst_ref=dst_fn(out_refs[o], me),
                    send_sem=send_sems.at[sem], recv_sem=recv_sems.at[sem],
                    device_id=(px, py, pc), device_id_type=MESH)
                push.start()
                started.append(push)
                arrivals.append(pltpu.make_async_remote_copy(
                    src_ref=src_fn(in_refs[i], peer), dst_ref=dst_fn(out_refs[o], peer),
                    send_sem=send_sems.at[sem], recv_sem=recv_sems.at[sem],
                    device_id=(px, py, pc), device_id_type=MESH))
        for arrival in arrivals:
            arrival.wait_recv()
        for cp in started[n_t:]:
            cp.wait_send()
        for own in started[:n_t]:
            own.wait()

    space = pltpu.VMEM if in_vmem else pl.ANY
    spec = pl.BlockSpec(memory_space=space)
    return _pcall(
        body, name=name, out_shape=out_shapes,
        in_specs=[spec] * n_in, out_specs=[spec] * n_out,
        scratch_shapes=[pltpu.SemaphoreType.DMA((n_t * (NDEV - 1),)),
                        pltpu.SemaphoreType.DMA((n_t * (NDEV - 1),)),
                        pltpu.SemaphoreType.DMA((n_t,))])(*ins)


def _whole(ref, dev):
    return ref


def _slot(ref, dev):
    return ref.at[dev]


def _all_gather_small(name, v):
    out = _exchange(name, [v], [jax.ShapeDtypeStruct((NDEV,) + v.shape, v.dtype)],
                    [(0, 0, _whole, _slot)], in_vmem=True)
    return out[0]


_HBM_SPEC = pl.BlockSpec(memory_space=pltpu.HBM)
_SEM_SPEC = pl.BlockSpec(memory_space=pltpu.SEMAPHORE)
_DATAFLOW = pltpu.SideEffectType.DATAFLOW_SIDE_EFFECTING


def _peer(x, y, c, k):
    px = x ^ ((k >> 2) & 1)
    py = y ^ ((k >> 1) & 1)
    pc = c ^ (k & 1)
    return (px, py, pc), 4 * px + 2 * py + pc


def _direct_sends(transfers):
    sends = []
    for k in range(1, NDEV):
        for i, o, src_fn, dst_fn in transfers:
            sends.append((k,
                          lambda ins, lands, me, i=i, k=k, src_fn=src_fn: src_fn(ins[i], me ^ k),
                          lambda lands, me, o=o, dst_fn=dst_fn: dst_fn(lands[o], me),
                          lambda lands, me, o=o, k=k, dst_fn=dst_fn: dst_fn(lands[o], me ^ k)))
    return sends


def _exchange_start(name, ins, lands, sends, after=None):
    n_in, n_buf = len(ins), len(ins) + len(lands)
    n_sem = len(sends)

    def body(*refs):
        in_refs, land_refs = refs[:n_in], refs[n_in:n_buf]
        n_skip = n_buf + (0 if after is None else 1)
        send_sems, recv_sems, token = refs[n_skip], refs[n_skip + 1], refs[-1]
        x, y, c = lax.axis_index("x"), lax.axis_index("y"), lax.axis_index("c")
        me = 4 * x + 2 * y + c
        for t, (k, src_fn, dst_fn, _) in enumerate(sends):
            pltpu.make_async_remote_copy(
                src_ref=src_fn(in_refs, land_refs, me), dst_ref=dst_fn(land_refs, me),
                send_sem=send_sems.at[t], recv_sem=recv_sems.at[t],
                device_id=_peer(x, y, c, k)[0], device_id_type=MESH).start()
        token[...] = jnp.zeros_like(token)

    bufs = [pltpu.with_memory_space_constraint(a, pltpu.HBM) for a in list(ins) + list(lands)]
    extra = [] if after is None else [after]
    outs = pl.pallas_call(
        body, name=name,
        out_shape=(pltpu.SemaphoreType.DMA((n_sem,)), pltpu.SemaphoreType.DMA((n_sem,)))
        + tuple(pltpu.HBM(a.shape, a.dtype) for a in bufs) + (jax.ShapeDtypeStruct((8, BLK), F32),),
        in_specs=[_HBM_SPEC] * n_buf + [pl.BlockSpec(memory_space=pl.ANY)] * len(extra),
        out_specs=(_SEM_SPEC, _SEM_SPEC) + (_HBM_SPEC,) * n_buf + (pl.BlockSpec(memory_space=pltpu.VMEM),),
        input_output_aliases={b: 2 + b for b in range(n_buf)},
        compiler_params=pltpu.CompilerParams(has_side_effects=_DATAFLOW),
        interpret=False)(*bufs, *extra)
    return outs[0], outs[1], list(outs[2:2 + n_in]), list(outs[2 + n_in:2 + n_buf]), outs[-1]


def _exchange_wait(name, started, after, sends):
    send_sems, recv_sems, ins, lands, _ = started
    n_in, n_buf = len(ins), len(ins) + len(lands)

    def body(*refs):
        in_refs, land_refs = refs[:n_in], refs[n_in:n_buf]
        send_sems, recv_sems = refs[n_buf], refs[n_buf + 1]
        x, y, c = lax.axis_index("x"), lax.axis_index("y"), lax.axis_index("c")
        me = 4 * x + 2 * y + c
        for t, (k, src_fn, _, rcv_fn) in enumerate(sends):
            cp = pltpu.make_async_remote_copy(
                src_ref=src_fn(in_refs, land_refs, me), dst_ref=rcv_fn(land_refs, me),
                send_sem=send_sems.at[t], recv_sem=recv_sems.at[t],
                device_id=_peer(x, y, c, k)[0], device_id_type=MESH)
            cp.wait_send()
            cp.wait_recv()

    bufs = list(ins) + list(lands)
    outs = pl.pallas_call(
        body, name=name, out_shape=tuple(pltpu.HBM(a.shape, a.dtype) for a in bufs),
        in_specs=[_HBM_SPEC] * n_buf + [_SEM_SPEC, _SEM_SPEC, pl.BlockSpec(memory_space=pl.ANY)],
        out_specs=(_HBM_SPEC,) * n_buf,
        input_output_aliases={b: b for b in range(n_buf)},
        compiler_params=pltpu.CompilerParams(has_side_effects=_DATAFLOW),
        interpret=False)(*bufs, send_sems, recv_sems, after)
    return list(outs[:n_in]), list(outs[n_in:])


def _place_own(shape, dtype, own, start):
    return lax.dynamic_update_slice(lax.empty(shape, dtype), own, start)


def _place_own_window(name, shape, own, me):
    rows, cols = own.shape

    def body(me_ref, zone_in, own_ref, zone_ref):
        del me_ref, zone_in
        zone_ref[...] = own_ref[...]

    return pl.pallas_call(
        body, name=name, out_shape=jax.ShapeDtypeStruct(shape, own.dtype),
        grid_spec=pltpu.PrefetchScalarGridSpec(
            num_scalar_prefetch=1, grid=(1,),
            in_specs=[pl.BlockSpec(memory_space=pl.ANY), pl.BlockSpec((rows, cols), lambda i, me_ref: (0, 0))],
            out_specs=pl.BlockSpec((rows, cols), lambda i, me_ref: (0, me_ref[0]))),
        input_output_aliases={1: 0},
        compiler_params=pltpu.CompilerParams(dimension_semantics=("arbitrary",), vmem_limit_bytes=VMEM_LIMIT),
        interpret=False)(me.reshape(1).astype(jnp.int32), lax.empty(shape, own.dtype), own)


def _mod_fwd(c_all, w_mod, b_mod_mine):
    n_layers, _, cm = w_mod.shape

    def body(c_ref, w_ref, b_ref, o_ref):
        for l in range(n_layers):
            o_ref[l] = jnp.dot(c_ref[...], w_ref[l], preferred_element_type=F32,
                               precision=lax.Precision.HIGHEST) + b_ref[l]

    return _pcall(body, name="mod_fwd", out_shape=jax.ShapeDtypeStruct((n_layers, NDEV, cm), F32))(
        c_all, w_mod, b_mod_mine)


def _ln_proj(x, shift, scale, w_full, name):
    s_len, d = x.shape
    n = w_full.shape[1]
    tm = min(512, s_len)
    tn = 1024

    def body(x_ref, sh_ref, sc_ref, w_ref, proj_ref, ht_ref, h_scr):
        @pl.when(pl.program_id(1) == 0)
        def _():
            xs, _ = _standardize(x_ref[...])
            h = xs * (1.0 + sc_ref[...]) + sh_ref[...]
            h_scr[...] = h.astype(BF16)
            ht_ref[...] = h.T.astype(BF16)

        proj_ref[...] = _dot(h_scr[...], w_ref[...])

    return _pcall(
        body, name=name,
        out_shape=(jax.ShapeDtypeStruct((s_len, n), F32), jax.ShapeDtypeStruct((d, s_len), BF16)),
        grid=(s_len // tm, n // tn),
        in_specs=[pl.BlockSpec((tm, d), lambda i, j: (i, 0)),
                  pl.BlockSpec((1, d), lambda i, j: (0, 0)),
                  pl.BlockSpec((1, d), lambda i, j: (0, 0)),
                  pl.BlockSpec((d, tn), lambda i, j: (0, j))],
        out_specs=(pl.BlockSpec((tm, tn), lambda i, j: (i, j)),
                   pl.BlockSpec((d, tm), lambda i, j: (0, i))),
        scratch_shapes=[pltpu.VMEM((tm, d), BF16)],
        semantics=("arbitrary", "arbitrary"))(x, shift, scale, w_full)


def _sb_group_blocks(nb):
    return min(4, nb)


def _sb_fwd(proj, name):
    s_len = proj.shape[0]
    nb = s_len // BLK
    n_pairs = WIDTH // BLK
    gb = _sb_group_blocks(nb)
    kw = gb * BLK

    def body(q_ref, k_ref, v_ref, o_ref, tot_ref):
        lane = _iota2((1, BLK), 1)
        row = _iota2((BLK, BLK), 0)
        col = _iota2((BLK, BLK), 1)
        half = jnp.concatenate([(row >= col).astype(BF16), jnp.ones((BLK, BLK), BF16)], axis=1)
        suffix_and_sum = jnp.concatenate([half, half], axis=0)
        qpos = _iota2((BLK, kw), 0)
        kpos = _iota2((BLK, kw), 1)
        head_lanes = [(lane // SB_HEAD_DIM) == hh for hh in range(2)]

        def scores(i, gi, qms, masked):
            c0 = pl.multiple_of(gi * kw, kw)
            kb = k_ref[pl.ds(c0, kw), :].astype(BF16)
            z2s = [_dot_nt(qms[hh], kb) for hh in range(2)]
            if masked:
                valid = (c0 + kpos) < (i * BLK + qpos)
                z2s = [jnp.where(valid, z2, MASKED_SCORE) for z2 in z2s]
            return tuple(z2s)

        def accumulate(gi, z2s, carry):
            c0 = pl.multiple_of(gi * kw, kw)
            vf = v_ref[pl.ds(c0, kw), :]
            sp2s = [_softplus2_parts(z2)[0] for z2 in z2s]
            terms = [[_split2_lanes(sp2[:, b * BLK:(b + 1) * BLK]) for b in range(gb)] for sp2 in sp2s]
            sums = [[_dot(t, suffix_and_sum) for t in head_terms] for head_terms in terms]
            weights, laters = [], []
            for hh in range(2):
                later = carry[2 * hh + 1]
                parts = [None] * gb
                for b in reversed(range(gb)):
                    parts[b] = sums[hh][b][:, :BLK] + later
                    later = later + sums[hh][b][:, BLK:]
                weights.append(jnp.exp2(z2s[hh] - jnp.concatenate(parts, axis=1)).astype(BF16))
                laters.append(later)
            outs = [_dot(weights[hh], jnp.where(head_lanes[hh], vf, 0.0).astype(BF16)) for hh in range(2)]
            return (carry[0] + outs[0], laters[0], carry[2] + outs[1], laters[1])

        def queries(i):
            qf = q_ref[pl.ds(pl.multiple_of(i * BLK, BLK), BLK), :] * (SB_HEAD_DIM ** -0.5 * LOG2E)
            return [jnp.where(head_lanes[hh], qf, 0.0).astype(BF16) for hh in range(2)]

        def qblock(i, first_scores):
            r0 = pl.multiple_of(i * BLK, BLK)
            qms = queries(i)
            zero = jnp.zeros((BLK, BLK), F32)
            last = i // gb

            def step(jj, state):
                gi = last - 1 - jj
                return scores(i, gi, qms, False) + accumulate(gi + 1, state[:2], state[2:])

            state = lax.fori_loop(0, last, step, first_scores + (zero,) * 4)
            nxt = jnp.minimum(i + 1, nb - 1)
            next_scores = scores(nxt, nxt // gb, queries(nxt), True)
            carry = accumulate(0, state[:2], state[2:])
            o_ref[pl.ds(r0, BLK), :] = carry[0] + carry[2]
            tot_ref[0, pl.ds(r0, BLK), :] = carry[1]
            tot_ref[1, pl.ds(r0, BLK), :] = carry[3]
            return next_scores

        lax.fori_loop(0, nb, qblock, scores(0, 0, queries(0), True))

    col_spec = lambda off: pl.BlockSpec((s_len, BLK), lambda p: (0, off + p))
    return _pcall(
        body, name=name,
        out_shape=(jax.ShapeDtypeStruct((s_len, WIDTH), F32),
                   jax.ShapeDtypeStruct((2 * n_pairs, s_len, BLK), F32)),
        grid=(n_pairs,),
        in_specs=[col_spec(0), col_spec(n_pairs), col_spec(2 * n_pairs)],
        out_specs=(pl.BlockSpec((s_len, BLK), lambda p: (0, p)),
                   pl.BlockSpec((2, s_len, BLK), lambda p: (p, 0, 0))),
        semantics=("arbitrary",))(proj, proj, proj)


def _hg_masks(mask_ref):
    row = _iota2((BLK, BLK), 0)
    col = _iota2((BLK, BLK), 1)
    for v, m in enumerate(HG_LEVELS):
        same = (row // (2 * m)) == (col // (2 * m))
        mask_ref[v] = (same & ((row & m) != 0) & ((col & m) == 0)).astype(F32)


def _hg_mid(b, m):
    if m >= 4:
        n = BLK // (2 * m)
        mid = b.reshape(n, 2 * m, BLK)[:, m - 1:m, :]
        return jnp.broadcast_to(mid, (n, 2 * m, BLK)).reshape(BLK, BLK)
    pos = _iota2((BLK, BLK), 0) & (2 * m - 1)
    out = b
    for p in range(2 * m):
        delta = (m - 1) - p
        if delta != 0:
            out = jnp.where(pos == p, pltpu.roll(b, (-delta) % BLK, 0), out)
    return out


def _hg_chunk_inputs(qraw, fpre, lb):
    sig = _sigmoid(fpre)
    f = lb + (1.0 - lb) * sig
    g = jnp.log(f)
    q, dq_fac = _silu_and_grad(qraw)
    return q, dq_fac, f, sig, g


HG_GROUP = 4


def _neg_abs(x):
    return lax.bitcast_convert_type(lax.bitcast_convert_type(x, jnp.int32) | jnp.int32(-2 ** 31), F32)


def _hg_level_terms(qs, ks, bs, m):
    es = [jnp.exp(_neg_abs(b - _hg_mid(b, m))) for b in bs]
    qts = [(q * e).astype(BF16) for q, e in zip(qs, es)]
    kts = [(k * e).astype(BF16) for k, e in zip(ks, es)]
    return es, qts, kts


def _hg_load(refs, r0, lb_v, lower_incl):
    q_ref, f_ref, i_ref = refs
    heads = []
    for h in range(HG_GROUP):
        sl = slice(h * HG_HEAD_DIM, (h + 1) * HG_HEAD_DIM)
        heads.append(_hg_chunk_inputs(q_ref[pl.ds(r0, BLK), sl], f_ref[pl.ds(r0, BLK), sl], lb_v[:, sl])
                     + (i_ref[pl.ds(r0, BLK), sl],))
    bs = [_dot_exact_l(lower_incl, hd[4]) for hd in heads]
    return heads, bs


def _hgrn_fwd(proj, lb, name):
    s_len = proj.shape[0]
    nc = s_len // BLK
    gw = HG_GROUP * HG_HEAD_DIM
    n_groups = WIDTH // gw
    base = 4 * WIDTH // gw

    def body(q_ref, f_ref, i_ref, lb_ref, o_ref, mask_ref):
        _hg_masks(mask_ref)
        row = _iota2((BLK, BLK), 0)
        col = _iota2((BLK, BLK), 1)
        lower_incl = (col <= row).astype(BF16)
        lb_v = lb_ref[...]

        def chunk(ci, sts):
            r0 = pl.multiple_of(ci * BLK, BLK)
            heads, bs = _hg_load((q_ref, f_ref, i_ref), r0, lb_v, lower_incl)
            qs = [hd[0] for hd in heads]
            ks = [1.0 - hd[2] for hd in heads]
            vs = [hd[5] for hd in heads]
            vbs = [v.astype(BF16) for v in vs]
            b_ends = [b[BLK - 1:BLK, :] for b in bs]
            inters = [_dot_nt((q * jnp.exp(b)).astype(BF16), st.astype(BF16)) for q, b, st in zip(qs, bs, sts)]
            scs = [None] * HG_GROUP
            for v_idx, m in enumerate(HG_LEVELS):
                _, qts, kts = _hg_level_terms(qs, ks, bs, m)
                terms = [_dot_nt(qt, kt) for qt, kt in zip(qts, kts)]
                msk = mask_ref[v_idx]
                scs = [t * msk if sc is None else sc + t * msk for sc, t in zip(scs, terms)]
            intras = [_dot(sc.astype(BF16), vb) for sc, vb in zip(scs, vbs)]
            k_decs = [(k * jnp.exp(b_end - b)).astype(BF16) for k, b, b_end in zip(ks, bs, b_ends)]
            grown = [_dot_tn(vb, k_dec) for vb, k_dec in zip(vbs, k_decs)]
            for h in range(HG_GROUP):
                diag = jnp.sum(qs[h] * ks[h], axis=-1, keepdims=True)
                o_ref[pl.ds(r0, BLK), h * HG_HEAD_DIM:(h + 1) * HG_HEAD_DIM] = inters[h] + intras[h] + diag * vs[h]
            return tuple(st * jnp.exp(b_end) + g for st, b_end, g in zip(sts, b_ends, grown))

        lax.fori_loop(0, nc, chunk, (jnp.zeros((HG_HEAD_DIM, HG_HEAD_DIM), F32),) * HG_GROUP)

    col_spec = lambda off: pl.BlockSpec((s_len, gw), lambda h: (0, off + h))
    return _pcall(
        body, name=name, out_shape=jax.ShapeDtypeStruct((s_len, WIDTH), F32),
        grid=(n_groups,),
        in_specs=[col_spec(base), col_spec(base + n_groups), col_spec(base + 2 * n_groups),
                  pl.BlockSpec((1, gw), lambda h: (0, h))],
        out_specs=pl.BlockSpec((s_len, gw), lambda h: (0, h)),
        scratch_shapes=[pltpu.VMEM((len(HG_LEVELS), BLK, BLK), F32)],
        semantics=("arbitrary",))(proj, proj, proj, lb)


def _rms_heads(o_b, norm_w):
    n_parts, h_parts, r_parts = [], [], []
    for h in range(WIDTH // HG_HEAD_DIM):
        sl = slice(h * HG_HEAD_DIM, (h + 1) * HG_HEAD_DIM)
        o = o_b[:, sl]
        rstd = lax.rsqrt(jnp.mean(o * o, axis=-1, keepdims=True) + RMS_EPS)
        ohat = o * rstd
        h_parts.append(ohat)
        n_parts.append(ohat * norm_w[:, sl])
        r_parts.append(jnp.broadcast_to(rstd, o.shape))
    cat = lambda parts: jnp.concatenate(parts, axis=-1)
    return cat(n_parts), cat(h_parts), cat(r_parts)


def _shift_rows_down(halo, cur, k):
    tm = cur.shape[0]
    ext = jnp.concatenate([halo, cur], axis=0)
    return pltpu.roll(ext, k, 0)[8:8 + tm]


def _shift_rows_up(cur, halo, k):
    tm = cur.shape[0]
    ext = jnp.concatenate([cur, halo], axis=0)
    return pltpu.roll(ext, (tm + 8 - k) % (tm + 8), 0)[0:tm]


def _merge_fwd(x, proj, o_a, o_b, gate, norm_w, conv_w, wb, w_out, ln_g, ln_b, name):
    s_len, d = x.shape
    tm = min(256, s_len)
    hb = tm // 8

    def body(x_ref, oa_ref, za_ref, ob_ref, zb_ref, pre_ref, post_ref, u_ref, zc_ref, hpre_ref, hu_ref, g_ref,
             gate_ref, nw_ref, cw_ref, wb_ref, wo_ref, lg_ref, lbias_ref, xn_ref, mg_ref, yc_ref):
        i = pl.program_id(0)
        sa, _ = _silu_and_grad(za_ref[...])
        y_a = (oa_ref[...] * sa).astype(BF16)
        n_b, _, _ = _rms_heads(ob_ref[...], nw_ref[...])
        sb, _ = _silu_and_grad(zb_ref[...])
        y_b = (n_b * sb).astype(BF16)
        a = pre_ref[...] * u_ref[...]
        halo = jnp.where(i > 0, hpre_ref[...] * hu_ref[...], 0.0)
        cw = cw_ref[...]
        conv = cw[0:1] * _shift_rows_down(halo, a, 2) + cw[1:2] * _shift_rows_down(halo, a, 1) + cw[2:3] * a
        sc, _ = _silu_and_grad(zc_ref[...])
        y_c = (post_ref[...] * conv * sc).astype(BF16)
        merged = None
        for k, yk in enumerate((y_a, y_b, y_c)):
            yc_ref[:, k * WIDTH:(k + 1) * WIDTH] = yk
            term = _sigmoid(g_ref[:, k * d:(k + 1) * d]) * _dot(yk, wb_ref[k])
            merged = term if merged is None else merged + term
        mb = merged.astype(BF16)
        mg_ref[...] = mb
        y = _dot(mb, wo_ref[...])
        r = ALPHA * x_ref[...] + (1.0 + gate_ref[...]) * y
        rhat, _ = _standardize(r)
        xn_ref[...] = rhat * lg_ref[...] + lbias_ref[...]

    wcol = lambda cb: pl.BlockSpec((tm, WIDTH), lambda i: (i, cb))
    halo_spec = lambda cb: pl.BlockSpec((8, WIDTH), lambda i: (jnp.maximum(i * hb - 1, 0), cb))
    vec = lambda w: pl.BlockSpec((1, w), lambda i: (0, 0))
    return _pcall(
        body, name=name,
        out_shape=(jax.ShapeDtypeStruct((s_len, d), F32), jax.ShapeDtypeStruct((s_len, d), BF16),
                   jax.ShapeDtypeStruct((s_len, 3 * WIDTH), BF16)),
        grid=(s_len // tm,),
        in_specs=[pl.BlockSpec((tm, d), lambda i: (i, 0)),
                  wcol(0), wcol(3), wcol(0), wcol(7), wcol(8), wcol(9), wcol(10), wcol(11),
                  halo_spec(8), halo_spec(10),
                  pl.BlockSpec((tm, 3 * d), lambda i: (i, 2)),
                  vec(d), vec(WIDTH),
                  pl.BlockSpec((3, WIDTH), lambda i: (0, 0)),
                  pl.BlockSpec((3, WIDTH, d), lambda i: (0, 0, 0)),
                  pl.BlockSpec((d, d), lambda i: (0, 0)),
                  vec(d), vec(d)],
        out_specs=(pl.BlockSpec((tm, d), lambda i: (i, 0)), pl.BlockSpec((tm, d), lambda i: (i, 0)),
                   pl.BlockSpec((tm, 3 * WIDTH), lambda i: (i, 0))),
        semantics=("arbitrary",))(x, o_a, proj, o_b, proj, proj, proj, proj, proj, proj, proj, proj,
                                  gate, norm_w, conv_w, wb, w_out, ln_g, ln_b)


def _loss_fwd_bwd(y, target):
    s_len, d = y.shape
    tm = min(512, s_len)

    def body(y_ref, t_ref, loss_ref, dy_ref):
        @pl.when(pl.program_id(0) == 0)
        def _():
            loss_ref[...] = jnp.zeros_like(loss_ref)

        e = y_ref[...] - t_ref[...]
        dy_ref[...] = e * (1.0 / d)
        part = jnp.sum(jnp.sum(e * e, axis=-1, keepdims=True), axis=0, keepdims=True)
        loss_ref[...] += part * (0.5 / d)

    tile = pl.BlockSpec((tm, d), lambda i: (i, 0))
    return _pcall(body, name="loss", grid=(s_len // tm,),
                  out_shape=(jax.ShapeDtypeStruct((1, 1), F32), jax.ShapeDtypeStruct((s_len, d), F32)),
                  in_specs=[tile, tile],
                  out_specs=(pl.BlockSpec((1, 1), lambda i: (0, 0)), tile),
                  semantics=("arbitrary",))(y, target)


def _merge_bwd(dxn, x, merged, ycat, proj, gate, wb, w_out, ln_g, name):
    s_len, d = x.shape
    tm = min(256, s_len)
    dsh = d // NDEV
    n_tiles = s_len // tm

    def body(dxn_ref, x_ref, mg_ref, yc_ref, g_ref, gate_ref, wb_ref, wo_ref, lg_ref,
             dres_ref, dyc_ref, dg_ref, gwo_out, gwb_out, vec_ref, gwo_ref, gwb_ref):
        @pl.when(pl.program_id(0) == 0)
        def _():
            gwo_ref[...] = jnp.zeros_like(gwo_ref)
            gwb_ref[...] = jnp.zeros_like(gwb_ref)
            vec_ref[...] = jnp.zeros_like(vec_ref)

        mb = mg_ref[...]
        one_gate = 1.0 + gate_ref[...]
        y = _dot(mb, wo_ref[...])
        r = ALPHA * x_ref[...] + one_gate * y
        rhat, rstd = _standardize(r)
        dxn = dxn_ref[...]
        dr = _standardize_bwd(rhat, rstd, dxn * lg_ref[...])
        vec_ref[0:1, :] += jnp.sum(dxn * rhat, axis=0, keepdims=True)
        vec_ref[1:2, :] += jnp.sum(dxn, axis=0, keepdims=True)
        vec_ref[2:3, :] += jnp.sum(dr * y, axis=0, keepdims=True)
        dres_ref[...] = ALPHA * dr
        dy = (one_gate * dr).astype(BF16)
        gwo_ref[...] += _dot_tn(mb, dy)
        dmerged = _dot_nt(dy, wo_ref[...])
        for k in range(3):
            yk = yc_ref[:, k * WIDTH:(k + 1) * WIDTH]
            sg = _sigmoid(g_ref[:, k * d:(k + 1) * d])
            pk = _dot(yk, wb_ref[k])
            dg_ref[:, k * d:(k + 1) * d] = (dmerged * pk * sg * (1.0 - sg)).astype(BF16)
            dpk = (dmerged * sg).astype(BF16)
            dyc_ref[:, k * WIDTH:(k + 1) * WIDTH] = _dot_nt(dpk, wb_ref[k])
            gwb_ref[k] += _dot_tn(yk, dpk)

        @pl.when(pl.program_id(0) == n_tiles - 1)
        def _():
            for o in range(NDEV):
                gwo_out[o] = gwo_ref[o * dsh:(o + 1) * dsh, :].astype(BF16)
                for k in range(3):
                    gwb_out[o, k] = gwb_ref[k, :, o * dsh:(o + 1) * dsh].astype(BF16)

    tile = lambda w: pl.BlockSpec((tm, w), lambda i: (i, 0))
    vec = pl.BlockSpec((1, d), lambda i: (0, 0))
    return _pcall(
        body, name=name,
        out_shape=(jax.ShapeDtypeStruct((s_len, d), F32), jax.ShapeDtypeStruct((s_len, 3 * WIDTH), F32),
                   jax.ShapeDtypeStruct(proj.shape, BF16), jax.ShapeDtypeStruct((NDEV, dsh, d), BF16),
                   jax.ShapeDtypeStruct((NDEV, 3, WIDTH, dsh), BF16), jax.ShapeDtypeStruct((8, d), F32)),
        grid=(n_tiles,),
        in_specs=[tile(d), tile(d), tile(d), tile(3 * WIDTH),
                  pl.BlockSpec((tm, 3 * d), lambda i: (i, 2)),
                  vec, pl.BlockSpec((3, WIDTH, d), lambda i: (0, 0, 0)),
                  pl.BlockSpec((d, d), lambda i: (0, 0)), vec],
        out_specs=(tile(d), tile(3 * WIDTH), pl.BlockSpec((tm, 3 * d), lambda i: (i, 2)),
                   pl.BlockSpec((NDEV, dsh, d), lambda i: (0, 0, 0)),
                   pl.BlockSpec((NDEV, 3, WIDTH, dsh), lambda i: (0, 0, 0, 0)),
                   pl.BlockSpec((8, d), lambda i: (0, 0))),
        scratch_shapes=[pltpu.VMEM((d, d), F32), pltpu.VMEM((3, WIDTH, d), F32)],
        semantics=("arbitrary",))(dxn, x, merged, ycat, proj, gate, wb, w_out, ln_g)


def _branch_bwd(dycat, proj, o_a, o_b, norm_w, conv_w, dproj, name):
    s_len = proj.shape[0]
    tm = min(256, s_len)
    hb = tm // 8
    n_tiles = s_len // tm

    def body(dya_ref, dyb_ref, dyc_ref, oa_ref, za_ref, ob_ref, zb_ref, pre_ref, post_ref, u_ref, zc_ref,
             hpre_ref, hu_ref, ndyc_ref, npost_ref, nzc_ref, nw_ref, cw_ref, dproj_in,
             dproj_ref, doa_ref, dob_ref, vec_ref, dza_scr, dzb_scr, dc_scr, sems):
        del dproj_in
        i = pl.program_id(0)

        @pl.when(i == 0)
        def _():
            vec_ref[...] = jnp.zeros_like(vec_ref)

        sa, dsa = _silu_and_grad(za_ref[...])
        dya = dya_ref[...]
        doa_ref[...] = dya * sa
        dza_scr[...] = (dya * oa_ref[...] * dsa).astype(BF16)
        nw = nw_ref[...]
        n_b, ohat, rstd = _rms_heads(ob_ref[...], nw)
        sb, dsb = _silu_and_grad(zb_ref[...])
        dyb = dyb_ref[...]
        dzb_scr[...] = (dyb * n_b * dsb).astype(BF16)
        dn = dyb * sb
        vec_ref[0:1, :] += jnp.sum(dn * ohat, axis=0, keepdims=True)
        dnw = dn * nw
        parts = []
        for h in range(WIDTH // HG_HEAD_DIM):
            sl = slice(h * HG_HEAD_DIM, (h + 1) * HG_HEAD_DIM)
            m2 = jnp.mean(dnw[:, sl] * ohat[:, sl], axis=-1, keepdims=True)
            parts.append(rstd[:, sl] * (dnw[:, sl] - ohat[:, sl] * m2))
        dob_ref[...] = jnp.concatenate(parts, axis=-1)
        cw = cw_ref[...]
        pre, u, post = pre_ref[...], u_ref[...], post_ref[...]
        a = pre * u
        halo = jnp.where(i > 0, hpre_ref[...] * hu_ref[...], 0.0)
        a1 = _shift_rows_down(halo, a, 1)
        a2 = _shift_rows_down(halo, a, 2)
        conv = cw[0:1] * a2 + cw[1:2] * a1 + cw[2:3] * a
        sc, dsc = _silu_and_grad(zc_ref[...])
        dyc = dyc_ref[...]
        dconv = dyc * post * sc
        nsc, _ = _silu_and_grad(nzc_ref[...])
        nxt = jnp.where(i < n_tiles - 1, ndyc_ref[...] * npost_ref[...] * nsc, 0.0)
        da = cw[2:3] * dconv + cw[1:2] * _shift_rows_up(dconv, nxt, 1) + cw[0:1] * _shift_rows_up(dconv, nxt, 2)
        dc_scr[:, 0 * WIDTH:1 * WIDTH] = (da * u).astype(BF16)
        dc_scr[:, 1 * WIDTH:2 * WIDTH] = (dyc * conv * sc).astype(BF16)
        dc_scr[:, 2 * WIDTH:3 * WIDTH] = (da * pre).astype(BF16)
        dc_scr[:, 3 * WIDTH:4 * WIDTH] = (dyc * post * conv * dsc).astype(BF16)
        vec_ref[1:2, :] += jnp.sum(dconv * a2, axis=0, keepdims=True)
        vec_ref[2:3, :] += jnp.sum(dconv * a1, axis=0, keepdims=True)
        vec_ref[3:4, :] += jnp.sum(dconv * a, axis=0, keepdims=True)
        rows = pl.ds(pl.multiple_of(i * tm, tm), tm)
        copies = [pltpu.make_async_copy(dza_scr, dproj_ref.at[rows, 3 * WIDTH:4 * WIDTH], sems.at[0]),
                  pltpu.make_async_copy(dzb_scr, dproj_ref.at[rows, 7 * WIDTH:8 * WIDTH], sems.at[1]),
                  pltpu.make_async_copy(dc_scr, dproj_ref.at[rows, 8 * WIDTH:12 * WIDTH], sems.at[2])]
        for cp in copies:
            cp.start()
        for cp in copies:
            cp.wait()

    wcol = lambda cb: pl.BlockSpec((tm, WIDTH), lambda i: (i, cb))
    prev = lambda cb: pl.BlockSpec((8, WIDTH), lambda i: (jnp.maximum(i * hb - 1, 0), cb))
    nxt = lambda cb: pl.BlockSpec((8, WIDTH), lambda i: (jnp.minimum((i + 1) * hb, s_len // 8 - 1), cb))
    anyspec = pl.BlockSpec(memory_space=pl.ANY)
    out = jax.ShapeDtypeStruct((s_len, WIDTH), F32)
    return _pcall(
        body, name=name,
        out_shape=(jax.ShapeDtypeStruct(dproj.shape, dproj.dtype), out, out, jax.ShapeDtypeStruct((8, WIDTH), F32)),
        grid=(n_tiles,),
        in_specs=[wcol(0), wcol(1), wcol(2), wcol(0), wcol(3), wcol(0), wcol(7), wcol(8), wcol(9), wcol(10), wcol(11),
                  prev(8), prev(10), nxt(2), nxt(9), nxt(11),
                  pl.BlockSpec((1, WIDTH), lambda i: (0, 0)), pl.BlockSpec((3, WIDTH), lambda i: (0, 0)), anyspec],
        out_specs=(anyspec, wcol(0), wcol(0), pl.BlockSpec((8, WIDTH), lambda i: (0, 0))),
        scratch_shapes=[pltpu.VMEM((tm, WIDTH), BF16), pltpu.VMEM((tm, WIDTH), BF16),
                        pltpu.VMEM((tm, 4 * WIDTH), BF16), pltpu.SemaphoreType.DMA((3,))],
        aliases={18: 0},
        semantics=("arbitrary",))(dycat, dycat, dycat, o_a, proj, o_b, proj, proj, proj, proj, proj,
                                  proj, proj, dycat, proj, proj, norm_w, conv_w, dproj)


def _sb_bwd(proj, do_a, totals, dproj, name):
    s_len = proj.shape[0]
    nb = s_len // BLK
    n_pairs = WIDTH // BLK
    scale = SB_HEAD_DIM ** -0.5
    gb = _sb_group_blocks(nb)
    kw = gb * BLK

    def body(q_ref, k_ref, v_ref, do_ref, tot_ref, dproj_in, dproj_ref, dq_ref, dk_ref, dv_ref, out_scr, sems):
        del dproj_in
        lane = _iota2((1, BLK), 1)
        row = _iota2((BLK, BLK), 0)
        col = _iota2((BLK, BLK), 1)
        ones = jnp.ones((BLK, BLK), BF16)
        twice = lambda m: jnp.concatenate([m, m], axis=0)
        before_and_sum = twice(jnp.concatenate([(row < col).astype(BF16), ones], axis=1))
        upto_and_sum = twice(jnp.concatenate([(row <= col).astype(BF16), ones], axis=1))
        qpos = _iota2((BLK, kw), 0)
        kpos = _iota2((BLK, kw), 1)
        head_lanes = [(lane // SB_HEAD_DIM) == hh for hh in range(2)]
        dk_ref[...] = jnp.zeros_like(dk_ref)
        dv_ref[...] = jnp.zeros_like(dv_ref)

        causal = kpos - qpos

        def scores(i, gi, qms):
            c0 = pl.multiple_of(gi * kw, kw)
            kb = k_ref[pl.ds(c0, kw), :].astype(BF16)
            valid = causal < i * BLK - c0
            return tuple(jnp.where(valid, _dot_nt(qms[hh], kb), MASKED_SCORE) for hh in range(2))

        def process(gi, z2s, qms, doms, totals_i, carry):
            c0 = pl.multiple_of(gi * kw, kw)
            kf = k_ref[pl.ds(c0, kw), :]
            vf = v_ref[pl.ds(c0, kw), :]
            kms = [jnp.where(head_lanes[hh], kf, 0.0).astype(BF16) for hh in range(2)]
            vms = [jnp.where(head_lanes[hh], vf, 0.0).astype(BF16) for hh in range(2)]
            das = [_dot_nt(doms[hh], vms[hh]) for hh in range(2)]
            halves = [_softplus2_parts(z2) for z2 in z2s]
            terms = [[_split2_lanes(sp2[:, b * BLK:(b + 1) * BLK]) for b in range(gb)] for sp2, _ in halves]
            sums = [[_dot(t, before_and_sum) for t in head_terms] for head_terms in terms]
            weights, gmats, l_befores = [], [], []
            for hh in range(2):
                l_before = carry[3 * hh + 1]
                parts = []
                for b in range(gb):
                    parts.append(totals_i[hh] - l_before - sums[hh][b][:, :BLK])
                    l_before = l_before + sums[hh][b][:, BLK:]
                a = jnp.exp2(z2s[hh] - jnp.concatenate(parts, axis=1))
                weights.append(a.astype(BF16))
                gmats.append(a * das[hh])
                l_befores.append(l_before)
            terms = [[_split2_lanes(g[:, b * BLK:(b + 1) * BLK]) for b in range(gb)] for g in gmats]
            sums = [[_dot(t, upto_and_sum) for t in head_terms] for head_terms in terms]
            dzs, g_befores = [], []
            for hh in range(2):
                g_before = carry[3 * hh + 2]
                parts = []
                for b in range(gb):
                    parts.append(g_before + sums[hh][b][:, :BLK])
                    g_before = g_before + sums[hh][b][:, BLK:]
                dzs.append((gmats[hh] - halves[hh][1] * jnp.concatenate(parts, axis=1)).astype(BF16))
                g_befores.append(g_before)
            dk_t = _dot_tn(jnp.concatenate(qms, axis=0), jnp.concatenate(dzs, axis=0))
            dv_t = _dot_tn(jnp.concatenate(doms, axis=0), jnp.concatenate(weights, axis=0))
            dqs = [_dot(dzs[hh], kms[hh]) for hh in range(2)]
            dk_ref[:, pl.ds(c0, kw)] += dk_t * (1.0 / LOG2E)
            dv_ref[:, pl.ds(c0, kw)] += dv_t
            return (carry[0] + dqs[0], l_befores[0], g_befores[0], carry[3] + dqs[1], l_befores[1], g_befores[1])

        def queries(i):
            qf = q_ref[pl.ds(pl.multiple_of(i * BLK, BLK), BLK), :] * (scale * LOG2E)
            return [jnp.where(head_lanes[hh], qf, 0.0).astype(BF16) for hh in range(2)]

        def qblock(i, first_scores):
            r0 = pl.multiple_of(i * BLK, BLK)
            qms = queries(i)
            dof = do_ref[pl.ds(r0, BLK), :]
            doms = [jnp.where(head_lanes[hh], dof, 0.0).astype(BF16) for hh in range(2)]
            totals_i = [tot_ref[hh, pl.ds(r0, BLK), :] for hh in range(2)]
            zero = jnp.zeros((BLK, BLK), F32)
            last = i // gb

            def step(gi, state):
                return scores(i, gi + 1, qms) + process(gi, state[:2], qms, doms, totals_i, state[2:])

            state = lax.fori_loop(0, last, step, first_scores + (zero,) * 6)
            nxt = jnp.minimum(i + 1, nb - 1)
            next_scores = scores(nxt, 0, queries(nxt))
            carry = process(last, state[:2], qms, doms, totals_i, state[2:])
            dq_ref[pl.ds(r0, BLK), :] = (carry[0] + carry[3]) * scale
            return next_scores

        lax.fori_loop(0, nb, qblock, scores(0, 0, queries(0)))
        pair = pl.program_id(0)
        copies = []
        for t, value in enumerate((dq_ref[...], dk_ref[...].T, dv_ref[...].T)):
            out_scr[t] = value.astype(BF16)
            col = pl.multiple_of((t * n_pairs + pair) * BLK, BLK)
            copies.append(pltpu.make_async_copy(out_scr.at[t], dproj_ref.at[:, pl.ds(col, BLK)], sems.at[t]))
            copies[-1].start()
        for cp in copies:
            cp.wait()

    col_spec = lambda off: pl.BlockSpec((s_len, BLK), lambda p: (0, off + p))
    anyspec = pl.BlockSpec(memory_space=pl.ANY)
    return _pcall(
        body, name=name, out_shape=jax.ShapeDtypeStruct(dproj.shape, dproj.dtype), grid=(n_pairs,),
        in_specs=[col_spec(0), col_spec(n_pairs), col_spec(2 * n_pairs), col_spec(0),
                  pl.BlockSpec((2, s_len, BLK), lambda p: (p, 0, 0)), anyspec],
        out_specs=anyspec,
        scratch_shapes=[pltpu.VMEM((s_len, BLK), F32), pltpu.VMEM((BLK, s_len), F32), pltpu.VMEM((BLK, s_len), F32),
                        pltpu.VMEM((3, s_len, BLK), BF16), pltpu.SemaphoreType.DMA((3,))],
        aliases={5: 0},
        semantics=("arbitrary",))(proj, proj, proj, do_a, totals, dproj)


def _hgrn_bwd(proj, do_b, lb, dproj, name):
    s_len = proj.shape[0]
    nc = s_len // BLK
    gw = HG_GROUP * HG_HEAD_DIM
    n_groups = WIDTH // gw
    base = 4 * WIDTH // gw
    heads_of = range(HG_GROUP)

    def body(q_ref, f_ref, i_ref, do_ref, lb_ref, dproj_in, dproj_ref, dlb_ref, mask_ref, st_ref, out_scr, sems):
        del dproj_in
        _hg_masks(mask_ref)
        row = _iota2((BLK, BLK), 0)
        col = _iota2((BLK, BLK), 1)
        lower_incl = (col <= row).astype(BF16)
        upper_incl = (col >= row).astype(BF16)
        lb_v = lb_ref[...]
        refs = (q_ref, f_ref, i_ref)

        def fwd_chunk(ci, sts):
            for h in heads_of:
                st_ref[ci, h] = sts[h]
            heads, bs = _hg_load(refs, pl.multiple_of(ci * BLK, BLK), lb_v, lower_incl)
            b_ends = [b[BLK - 1:BLK, :] for b in bs]
            k_decs = [((1.0 - hd[2]) * jnp.exp(b_end - b)).astype(BF16) for hd, b, b_end in zip(heads, bs, b_ends)]
            grown = [_dot_tn(hd[5].astype(BF16), k_dec) for hd, k_dec in zip(heads, k_decs)]
            return tuple(st * jnp.exp(b_end) + g for st, b_end, g in zip(sts, b_ends, grown))

        zero_state = (jnp.zeros((HG_HEAD_DIM, HG_HEAD_DIM), F32),) * HG_GROUP
        lax.fori_loop(0, nc, fwd_chunk, zero_state)

        def bwd_chunk(cc, carry):
            dsts, suffixes, dlbs = carry
            ci = nc - 1 - cc
            r0 = pl.multiple_of(ci * BLK, BLK)
            heads, bs = _hg_load(refs, r0, lb_v, lower_incl)
            qs = [hd[0] for hd in heads]
            fs = [hd[2] for hd in heads]
            ks = [1.0 - f for f in fs]
            vs = [hd[5] for hd in heads]
            vbs = [v.astype(BF16) for v in vs]
            dos = [do_ref[pl.ds(r0, BLK), h * HG_HEAD_DIM:(h + 1) * HG_HEAD_DIM] for h in heads_of]
            dobs = [do.astype(BF16) for do in dos]
            b_ends = [b[BLK - 1:BLK, :] for b in bs]
            e_qs = [jnp.exp(b) for b in bs]
            e_ks = [jnp.exp(b_end - b) for b, b_end in zip(bs, b_ends)]
            qes = [(q * e).astype(BF16) for q, e in zip(qs, e_qs)]
            khs = [(k * e).astype(BF16) for k, e in zip(ks, e_ks)]
            st_terms = [_split2_lanes(st_ref[ci, h]) for h in heads_of]
            ds_terms = [_split2_lanes(dst) for dst in dsts]
            dqes = [_dot(dob, t[:, :HG_HEAD_DIM]) + _dot(dob, t[:, HG_HEAD_DIM:]) for dob, t in zip(dobs, st_terms)]
            dkhs = [_dot(vb, t[:, :HG_HEAD_DIM]) + _dot(vb, t[:, HG_HEAD_DIM:]) for vb, t in zip(vbs, ds_terms)]
            dvs = [_dot_nt(kh, t[:, :HG_HEAD_DIM]) for kh, t in zip(khs, ds_terms)]
            grown = [_dot_tn(dob, qe) for dob, qe in zip(dobs, qes)]
            das = [_dot_nt(dob, vb) for dob, vb in zip(dobs, vbs)]
            dqs = [e * dqe for e, dqe in zip(e_qs, dqes)]
            dks = [e * dkh for e, dkh in zip(e_ks, dkhs)]
            dlogs = [qe.astype(F32) * dqe - kh.astype(F32) * dkh for qe, dqe, kh, dkh in zip(qes, dqes, khs, dkhs)]
            scs = [None] * HG_GROUP
            for v_idx, m in enumerate(HG_LEVELS):
                es, qms, kms = _hg_level_terms(qs, ks, bs, m)
                msk = mask_ref[v_idx]
                terms = [_dot_nt(qm, km) for qm, km in zip(qms, kms)]
                pms = [(da * msk).astype(BF16) for da in das]
                dqms = [_dot(pm, km) for pm, km in zip(pms, kms)]
                dkms = [_dot_tn(pm, qm) for pm, qm in zip(pms, qms)]
                scs = [t * msk if sc is None else sc + t * msk for sc, t in zip(scs, terms)]
                dqs = [dq + dqm * e for dq, dqm, e in zip(dqs, dqms, es)]
                dks = [dk + dkm * e for dk, dkm, e in zip(dks, dkms, es)]
                dlogs = [dl + (qm.astype(F32) * dqm - km.astype(F32) * dkm)
                         for dl, qm, dqm, km, dkm in zip(dlogs, qms, dqms, kms, dkms)]
            intras = [_dot_tn(sc.astype(BF16), dob) for sc, dob in zip(scs, dobs)]
            dgs = [_dot_exact_l(upper_incl, dl) + sfx for dl, sfx in zip(dlogs, suffixes)]
            new_dlbs = []
            for h in heads_of:
                q, dq_fac, f, sig = heads[h][0], heads[h][1], heads[h][2], heads[h][3]
                a_diag = jnp.sum(dos[h] * vs[h], axis=-1, keepdims=True)
                s_diag = jnp.sum(q * ks[h], axis=-1, keepdims=True)
                dq = dqs[h] + a_diag * ks[h]
                dk = dks[h] + a_diag * q
                dv = dvs[h] + intras[h] + s_diag * dos[h]
                dfull = dgs[h] / f - dk
                sl = slice(h * HG_HEAD_DIM, (h + 1) * HG_HEAD_DIM)
                out_scr[0, pl.ds(r0, BLK), sl] = (dq * dq_fac).astype(BF16)
                out_scr[1, pl.ds(r0, BLK), sl] = (dfull * (1.0 - lb_v[:, sl]) * sig * (1.0 - sig)).astype(BF16)
                out_scr[2, pl.ds(r0, BLK), sl] = dv.astype(BF16)
                new_dlbs.append(dlbs[h] + jnp.sum(dfull * (1.0 - sig), axis=0, keepdims=True))
            new_dsts = tuple(dst * jnp.exp(b_end) + g for dst, b_end, g in zip(dsts, b_ends, grown))
            return new_dsts, tuple(dg[0:1, :] for dg in dgs), tuple(new_dlbs)

        zero_row = (jnp.zeros((1, HG_HEAD_DIM), F32),) * HG_GROUP
        _, _, dlbs = lax.fori_loop(0, nc, bwd_chunk, (zero_state, zero_row, zero_row))
        dlb_ref[...] = jnp.broadcast_to(jnp.concatenate(dlbs, axis=1), dlb_ref.shape)
        group = pl.program_id(0)
        copies = []
        for t in range(3):
            col = pl.multiple_of((base + t * n_groups + group) * gw, gw)
            copies.append(pltpu.make_async_copy(out_scr.at[t], dproj_ref.at[:, pl.ds(col, gw)], sems.at[t]))
            copies[-1].start()
        for cp in copies:
            cp.wait()

    col_spec = lambda off: pl.BlockSpec((s_len, gw), lambda h: (0, off + h))
    anyspec = pl.BlockSpec(memory_space=pl.ANY)
    return _pcall(
        body, name=name,
        out_shape=(jax.ShapeDtypeStruct(dproj.shape, dproj.dtype), jax.ShapeDtypeStruct((8, WIDTH), F32)),
        grid=(n_groups,),
        in_specs=[col_spec(base), col_spec(base + n_groups), col_spec(base + 2 * n_groups), col_spec(0),
                  pl.BlockSpec((1, gw), lambda h: (0, h)), anyspec],
        out_specs=(anyspec, pl.BlockSpec((8, gw), lambda h: (0, h))),
        scratch_shapes=[pltpu.VMEM((len(HG_LEVELS), BLK, BLK), F32),
                        pltpu.VMEM((nc, HG_GROUP, HG_HEAD_DIM, HG_HEAD_DIM), F32),
                        pltpu.VMEM((3, s_len, gw), BF16), pltpu.SemaphoreType.DMA((3,))],
        aliases={5: 0},
        semantics=("arbitrary",))(proj, proj, proj, do_b, lb, dproj)


def _dh_matmul(dproj, w_full, after, name):
    s_len, n = dproj.shape
    d = w_full.shape[0]
    tm = min(512, s_len)
    tk = 1536

    def body(dp_ref, w_ref, after_ref, dh_ref):
        del after_ref
        part = _dot_nt(dp_ref[...], w_ref[...])

        @pl.when(pl.program_id(1) == 0)
        def _():
            dh_ref[...] = part

        @pl.when(pl.program_id(1) > 0)
        def _():
            dh_ref[...] += part

    return _pcall(
        body, name=name, out_shape=jax.ShapeDtypeStruct((s_len, d), F32),
        grid=(s_len // tm, n // tk),
        in_specs=[pl.BlockSpec((tm, tk), lambda i, k: (i, k)), pl.BlockSpec((d, tk), lambda i, k: (0, k)),
                  pl.BlockSpec(memory_space=pl.ANY)],
        out_specs=pl.BlockSpec((tm, d), lambda i, k: (i, 0)),
        semantics=("arbitrary", "arbitrary"))(dproj, w_full, after)


def _gw_matmul(h_t, dproj, name):
    d, s_len = h_t.shape
    n = dproj.shape[1]
    tn = 1152

    def body(ht_ref, dp_ref, gw_ref):
        gw_ref[...] = _dot(ht_ref[...], dp_ref[...]).astype(BF16)

    return _pcall(
        body, name=name, out_shape=jax.ShapeDtypeStruct((d, n), BF16),
        grid=(n // tn,),
        in_specs=[pl.BlockSpec((d, s_len), lambda j: (0, 0)), pl.BlockSpec((s_len, tn), lambda j: (0, j))],
        out_specs=pl.BlockSpec((d, tn), lambda j: (0, j)),
        semantics=("arbitrary",))(h_t, dproj)


def _ln_bwd(dh, x, scale, dres, name):
    s_len, d = x.shape
    tm = min(512, s_len)

    def body(dh_ref, x_ref, sc_ref, dres_ref, dx_ref, vec_ref):
        @pl.when(pl.program_id(0) == 0)
        def _():
            vec_ref[...] = jnp.zeros_like(vec_ref)

        dh = dh_ref[...]
        xs, rstd = _standardize(x_ref[...])
        vec_ref[0:1, :] += jnp.sum(dh, axis=0, keepdims=True)
        vec_ref[1:2, :] += jnp.sum(dh * xs, axis=0, keepdims=True)
        dx_ref[...] = _standardize_bwd(xs, rstd, dh * (1.0 + sc_ref[...])) + dres_ref[...]

    tile = pl.BlockSpec((tm, d), lambda i: (i, 0))
    return _pcall(body, name=name, grid=(s_len // tm,),
                  out_shape=(jax.ShapeDtypeStruct((s_len, d), F32), jax.ShapeDtypeStruct((8, d), F32)),
                  in_specs=[tile, tile, pl.BlockSpec((1, d), lambda i: (0, 0)), tile],
                  out_specs=(tile, pl.BlockSpec((8, d), lambda i: (0, 0))),
                  semantics=("arbitrary",))(dh, x, scale, dres)


def _wmod_grad(c_t, dmod):
    d = c_t.shape[0]
    n_layers, _, cm = dmod.shape

    def body(c_ref, dm_ref, o_ref):
        for l in range(n_layers):
            acc = None
            for b in range(NDEV):
                term = c_ref[:, b:b + 1] * dm_ref[l, b:b + 1, :]
                acc = term if acc is None else acc + term
            o_ref[l] = acc

    return _pcall(body, name="wmod_grad", out_shape=jax.ShapeDtypeStruct((n_layers, d, cm), F32))(c_t, dmod)


def _sum_adamw(parts_list, w, m, v, name):
    n_ranges = len(parts_list)
    n_src, range_rows, cols = parts_list[0].shape
    rows = range_rows * n_ranges
    tr = range_rows
    for cand in (512, 256, 128, 64, 32, 16, 8):
        if range_rows % cand == 0 and cand * cols * 4 <= (2 << 20):
            tr = cand
            break
    tiles = range_rows // tr

    def body(*refs):
        p_refs = refs[:n_ranges]
        w_ref, m_ref, v_ref, g_ref, d_ref, nm_ref, nv_ref = refs[n_ranges:]

        def step(p_ref):
            g = p_ref[0].astype(F32)
            for s in range(1, n_src):
                g = g + p_ref[s].astype(F32)
            nm = ADAM_B1 * m_ref[...] + (1.0 - ADAM_B1) * g
            nv = ADAM_B2 * v_ref[...] + (1.0 - ADAM_B2) * (g * g)
            m_hat = nm / (1.0 - ADAM_B1 ** ADAM_STEP)
            v_hat = nv / (1.0 - ADAM_B2 ** ADAM_STEP)
            g_ref[...] = g
            d_ref[...] = -ADAM_LR * (m_hat / (jnp.sqrt(v_hat) + ADAM_EPS) + ADAM_WD * w_ref[...])
            nm_ref[...] = nm
            nv_ref[...] = nv

        if n_ranges == 1:
            step(p_refs[0])
        else:
            for j in range(n_ranges):
                @pl.when(pl.program_id(0) // tiles == j)
                def _(j=j):
                    step(p_refs[j])

    def part_spec(j):
        return pl.BlockSpec((n_src, tr, cols), lambda i: (0, jnp.clip(i - j * tiles, 0, tiles - 1), 0))

    tile = pl.BlockSpec((tr, cols), lambda i: (i, 0))
    out = jax.ShapeDtypeStruct((rows, cols), F32)
    return _pcall(body, name=name, grid=(rows // tr,), out_shape=(out,) * 4,
                  in_specs=[part_spec(j) for j in range(n_ranges)] + [tile, tile, tile],
                  out_specs=(tile,) * 4, semantics=("arbitrary",))(*parts_list, w, m, v)


def _sum_parts(parts, name):
    n_src = parts.shape[0]

    def body(p_ref, o_ref):
        acc = p_ref[0]
        for s in range(1, n_src):
            acc = acc + p_ref[s]
        o_ref[...] = acc

    return _pcall(body, name=name, out_shape=jax.ShapeDtypeStruct(parts.shape[1:], F32))(parts)


def _pair_sum(gw, stage, me, name):
    d = gw.shape[0]
    n_slots, _, shard = stage.shape

    def body(me_ref, g_ref, s_ref, own_ref, o_ref):
        del me_ref
        total = (g_ref[...].astype(F32) + s_ref[0].astype(F32)).astype(BF16)
        o_ref[0] = total

        @pl.when(pl.program_id(0) == 0)
        def _():
            own_ref[0] = total

    slot = pl.BlockSpec((1, d, shard), lambda jj, me_ref: (jj, 0, 0))
    out = jax.ShapeDtypeStruct(stage.shape, BF16)
    return pl.pallas_call(
        body, name=name, out_shape=(out, out),
        grid_spec=pltpu.PrefetchScalarGridSpec(
            num_scalar_prefetch=1, grid=(n_slots,),
            in_specs=[pl.BlockSpec((d, shard), lambda jj, me_ref: (0, me_ref[0] ^ (2 * jj))), slot],
            out_specs=(pl.BlockSpec((1, d, shard), lambda jj, me_ref: (0, 0, 0)), slot)),
        compiler_params=pltpu.CompilerParams(dimension_semantics=("arbitrary",), vmem_limit_bytes=VMEM_LIMIT),
        interpret=False)(me.reshape(1).astype(jnp.int32), gw, stage)


def _lower_bound_table(lower_bounds):
    p = jax.nn.softmax(lower_bounds.astype(F32), axis=0)
    return jnp.cumsum(p, axis=0) - p[0:1]


def _pad_rows(v, width):
    n = v.shape[0]
    rows = -(-n // width)
    rows = -(-rows // 8) * 8
    return jnp.pad(v, (0, rows * width - n)).reshape(rows, width)


def kernel(x, c, w_mod, b_mod, w_in, conv_w, hgrn_norm_w, lower_bounds, w_branch, w_out, ln_g, ln_b, loss_target, m_w_mod, m_b_mod, m_w_in, m_conv_w, m_hgrn_norm_w, m_lower_bounds, m_w_branch, m_w_out, m_ln_g, m_ln_b, v_w_mod, v_b_mod, v_w_in, v_conv_w, v_hgrn_norm_w, v_lower_bounds, v_w_branch, v_w_out, v_ln_g, v_ln_b):
    n_layers = N_LAYERS
    s_len, d = x.shape[1], x.shape[2]
    n_cols = w_in.shape[2] * NDEV
    cw_cols = conv_w.shape[2]
    cm = w_mod.shape[2]
    me = _my_index()
    x0 = x[0]
    target = loss_target[0]

    small = _pad_rows(jnp.concatenate([c.reshape(-1), conv_w.reshape(-1)]), BLK)
    small_all = _all_gather_small("gather_c_conv", small).reshape(NDEV, -1)
    c_all = small_all[:, :d]
    conv_full = small_all[:, d:d + n_layers * 3 * cw_cols].reshape(NDEV, n_layers, 3, cw_cols)
    conv_full = conv_full.transpose(1, 2, 0, 3).reshape(n_layers, 3, WIDTH)

    b_mod_mine = lax.dynamic_slice_in_dim(b_mod, me * cm, cm, axis=1).reshape(n_layers, 1, cm)
    mod_cols = _mod_fwd(c_all, w_mod, b_mod_mine)
    mod_all = _all_gather_small("gather_mod", mod_cols.reshape(n_layers * NDEV, cm))
    mod_all = mod_all.reshape(NDEV, n_layers, NDEV, cm)
    mod_mine = lax.dynamic_index_in_dim(mod_all, me, axis=2, keepdims=False)
    mod_mine = mod_mine.transpose(1, 0, 2).reshape(n_layers, 3, 1, d)

    shard = w_in.shape[2]
    dsh = d // NDEV
    w_in_b, w_branch_b, w_out_b = w_in.astype(BF16), w_branch.astype(BF16), w_out.astype(BF16)
    window = lambda ref, dev: ref.at[:, pl.ds(pl.multiple_of(dev * shard, BLK), shard)]

    def two_step_sends(places):
        chips, sibling = [], []
        for k in (1, 2, 4, 6):
            for a, place in enumerate(places):
                chips.append((k, lambda ins, lands, me, a=a: ins[a],
                              lambda lands, me, a=a, place=place: place(lands[a], me),
                              lambda lands, me, a=a, k=k, place=place: place(lands[a], me ^ k)))
        for j in (2, 4, 6):
            for a, place in enumerate(places):
                sibling.append((1, lambda ins, lands, me, a=a, j=j, place=place: place(lands[a], me ^ j),
                                lambda lands, me, a=a, j=j, place=place: place(lands[a], me ^ j),
                                lambda lands, me, a=a, j=j, place=place: place(lands[a], me ^ 1 ^ j)))
        return chips, sibling

    in_sends = two_step_sends([window])
    rest_sends = two_step_sends([_slot, _slot])
    layer_sends = two_step_sends([window, _slot, _slot])

    def in_land(l):
        return _place_own_window(f"place_w_in_{l}", (d, n_cols), w_in_b[l], me)

    def rest_lands(l):
        return [_place_own((NDEV, 3, WIDTH, dsh), BF16, w_branch_b[l][None], (me, 0, 0, 0)),
                _place_own((NDEV, dsh, d), BF16, w_out_b[l][None], (me, 0, 0))]

    def gather_start(name, shards, lands, sends, after):
        return _exchange_start(f"{name}_chips_start", shards, lands, sends[0], after)

    def gather_pass_on(name, started, after, sends):
        _, lands = _exchange_wait(f"{name}_chips_wait", started, after, sends[0])
        return _exchange_start(f"{name}_sibling_start", [], lands, sends[1])

    def gather_finish(name, started, after, sends):
        return _exchange_wait(f"{name}_sibling_wait", started, after, sends[1])[1]

    def branch_out_weights(w_branch_l, w_out_l):
        return w_branch_l.transpose(1, 2, 0, 3).reshape(3, WIDTH, d), w_out_l.reshape(d, d)

    gathering = gather_start("gather_w_in_0", [w_in_b[0]], [in_land(0)], in_sends, mod_mine)
    passing = gather_pass_on("gather_w_in_0", gathering, gathering[4], in_sends)
    rest_gathering = gather_start("gather_rest_0", [w_branch_b[0], w_out_b[0]], rest_lands(0), rest_sends, passing[4])
    next_gathering = None
    if n_layers > 1:
        next_gathering = gather_start("gather_weights_1", [w_in_b[1], w_branch_b[1], w_out_b[1]],
                                      [in_land(1)] + rest_lands(1), layer_sends, rest_gathering[4])
    w_in_l = gather_finish("gather_w_in_0", passing, (next_gathering or rest_gathering)[4], in_sends)[0]

    lbs = _lower_bound_table(lower_bounds)
    norm_w4 = jnp.tile(hgrn_norm_w, (1, WIDTH // HG_HEAD_DIM))

    saved = []
    xl = x0
    for l in range(n_layers):
        shift, scale, gate = mod_mine[l, 0], mod_mine[l, 1], mod_mine[l, 2]
        proj, h_t = _ln_proj(xl, shift, scale, w_in_l, f"ln_proj_{l}")
        o_a, totals = _sb_fwd(proj, f"sb_fwd_{l}")
        if l == 0:
            rest_passing = gather_pass_on("gather_rest_0", rest_gathering, o_a, rest_sends)
        lb_l = lbs[l:l + 1] + rest_passing[4][0, 0] if l == 0 else lbs[l:l + 1]
        o_b = _hgrn_fwd(proj, lb_l, f"hgrn_fwd_{l}")
        if l == 0:
            wb_l, wo_l = branch_out_weights(*gather_finish("gather_rest_0", rest_passing, o_b, rest_sends))
            if n_layers > 1:
                next_passing = gather_pass_on("gather_weights_1", next_gathering, o_b, layer_sends)
                gate = gate + next_passing[4][0, 0]
        x_new, merged, ycat = _merge_fwd(xl, proj, o_a, o_b, gate, norm_w4[l:l + 1], conv_full[l],
                                         wb_l, wo_l, ln_g[l:l + 1], ln_b[l:l + 1], f"merge_fwd_{l}")
        saved.append((xl, proj, h_t, o_a, totals, o_b, merged, ycat, w_in_l, wb_l, wo_l))
        if l == 0 and n_layers > 1:
            w_in_l, w_branch_l, w_out_l = gather_finish("gather_weights_1", next_passing, x_new, layer_sends)
            wb_l, wo_l = branch_out_weights(w_branch_l, w_out_l)
        xl = x_new

    loss_part, dx = _loss_fwd_bwd(xl, target)
    loss = lax.psum(loss_part[0, 0], ("x", "y", "c"))

    pair_sends = [(1, lambda ins, lands, me, j=j: window(ins[0], me ^ 1 ^ j),
                   lambda lands, me, jj=jj: lands[0].at[jj], lambda lands, me, jj=jj: lands[0].at[jj])
                  for jj, j in enumerate((0, 2, 4, 6))]
    chip_sum_sends = [(j, lambda ins, lands, me, jj=jj: ins[0].at[jj],
                       lambda lands, me, jj=jj: lands[0].at[jj], lambda lands, me, jj=jj: lands[0].at[jj])
                      for jj, j in ((1, 2), (2, 4), (3, 6))]
    rest_scatter = _direct_sends([(0, 0, _slot, _slot), (1, 1, _slot, _slot)])
    scattering = [None] * n_layers
    small_grads = [None] * n_layers
    dmod = [None] * n_layers
    tie = None
    for l in reversed(range(n_layers)):
        xl, proj, h_t, o_a, totals, o_b, merged, ycat, w_in_l, wb_l, wo_l = saved[l]
        scale, gate = mod_mine[l, 1], mod_mine[l, 2]
        if tie is not None:
            gate = gate + tie[0, 0]
        dres, dycat, dproj, gwo_by_owner, gwb_by_owner, mvec = _merge_bwd(
            dx, xl, merged, ycat, proj, gate, wb_l, wo_l, ln_g[l:l + 1], f"merge_bwd_{l}")
        lands = [_place_own((NDEV, 3, WIDTH, dsh), BF16, lax.dynamic_slice_in_dim(gwb_by_owner, me, 1, axis=0),
                            (me, 0, 0, 0)),
                 _place_own((NDEV, dsh, d), BF16, lax.dynamic_slice_in_dim(gwo_by_owner, me, 1, axis=0),
                            (me, 0, 0))]
        rest_started = _exchange_start(f"scatter_rest_{l}_start", [gwb_by_owner, gwo_by_owner], lands, rest_scatter)
        dproj, do_a, do_b, bvec = _branch_bwd(dycat, proj, o_a, o_b, norm_w4[l:l + 1] + rest_started[4][0, 0],
                                              conv_full[l], dproj, f"branch_bwd_{l}")
        dproj = _sb_bwd(proj, do_a, totals, dproj, f"sb_bwd_{l}")
        dproj, dlb = _hgrn_bwd(proj, do_b, lbs[l:l + 1], dproj, f"hgrn_bwd_{l}")
        gwi = _gw_matmul(h_t, dproj, f"gw_matmul_{l}")
        swapping = _exchange_start(f"scatter_in_{l}_sibling_start", [gwi], [lax.empty((4, d, shard), BF16)], pair_sends)
        if l > 0:
            dh = _dh_matmul(dproj, w_in_l, swapping[4], f"dh_matmul_{l}")
        (gwi,), (stage,) = _exchange_wait(f"scatter_in_{l}_sibling_wait", swapping, dh if l > 0 else swapping[4],
                                          pair_sends)
        land, chip_sums = _pair_sum(gwi, stage, me, f"pair_sum_{l}")
        in_started = _exchange_start(f"scatter_in_{l}_chips_start", [chip_sums], [land], chip_sum_sends)
        scattering[l] = (in_started, rest_started)
        tie = in_started[4]
        if l == 0:
            dh = _dh_matmul(dproj, w_in_l, tie, f"dh_matmul_{l}")
        dx, lvec = _ln_bwd(dh, xl, scale + tie[0, 0], dres, f"ln_bwd_{l}")
        dmod[l] = jnp.concatenate([lvec[0], lvec[1], mvec[2]])
        norm_grad = bvec[0].reshape(WIDTH // HG_HEAD_DIM, HG_HEAD_DIM).sum(axis=0)
        small_grads[l] = jnp.concatenate([mvec[0], mvec[1], norm_grad, dlb[0], bvec[1:4].reshape(-1)])
    grad_x = dx[None]

    small_vec = jnp.concatenate(dmod + small_grads)
    n_small = small_vec.shape[0]
    small_all = _all_gather_small("gather_small_grads", _pad_rows(small_vec, BLK))
    small_sum = _sum_parts(small_all, "sum_small_grads").reshape(-1)[:n_small]
    dmod_all = small_all.reshape(NDEV, -1)[:, :n_layers * 3 * d].reshape(NDEV, n_layers, 3 * d)

    off = n_layers * 3 * d
    grad_b_mod = small_sum[:off].reshape(n_layers, 3 * d)
    per_layer = 2 * d + HG_HEAD_DIM + WIDTH + 3 * WIDTH
    g_ln_g, g_ln_b, g_norm, g_lbs, g_conv = [], [], [], [], []
    for l in range(n_layers):
        seg = small_sum[off + l * per_layer: off + (l + 1) * per_layer]
        g_ln_g.append(seg[:d])
        g_ln_b.append(seg[d:2 * d])
        g_norm.append(seg[2 * d:2 * d + HG_HEAD_DIM])
        g_lbs.append(seg[2 * d + HG_HEAD_DIM:2 * d + HG_HEAD_DIM + WIDTH])
        g_conv.append(seg[2 * d + HG_HEAD_DIM + WIDTH:].reshape(3, WIDTH))
    grad_ln_g, grad_ln_b = jnp.stack(g_ln_g), jnp.stack(g_ln_b)
    grad_norm = jnp.stack(g_norm)
    _, lbs_vjp = jax.vjp(_lower_bound_table, lower_bounds)
    grad_lower = lbs_vjp(jnp.stack(g_lbs))[0]
    grad_conv = lax.dynamic_slice_in_dim(jnp.stack(g_conv), me * cw_cols, cw_cols, axis=2)

    dmod_mine = lax.dynamic_slice_in_dim(dmod_all, me * cm, cm, axis=2).transpose(1, 0, 2)
    grad_w_mod = _wmod_grad(c_all.T, dmod_mine)

    p_in, p_branch, p_out = [None] * n_layers, [None] * n_layers, [None] * n_layers
    for l in reversed(range(n_layers)):
        in_started, rest_started = scattering[l]
        p_branch_l, p_out[l] = _exchange_wait(f"scatter_rest_{l}_wait", rest_started, grad_w_mod, rest_scatter)[1]
        p_branch[l] = p_branch_l.reshape(NDEV, 3 * WIDTH, dsh)
        p_in[l] = _exchange_wait(f"scatter_in_{l}_chips_wait", in_started, grad_w_mod, chip_sum_sends)[1][0]

    def adam(parts_list, w, m, v, name):
        shape = w.shape
        cols = shape[-1]
        flat = lambda a: a.reshape(-1, cols)
        outs = _sum_adamw(parts_list, flat(w), flat(m), flat(v), name)
        return [o.reshape(shape) for o in outs]

    r_w_in = adam(p_in, w_in, m_w_in, v_w_in, "adamw_w_in")
    r_w_branch = adam(p_branch, w_branch, m_w_branch, v_w_branch, "adamw_w_branch")
    r_w_out = adam(p_out, w_out, m_w_out, v_w_out, "adamw_w_out")
    r_w_mod = adam([grad_w_mod.reshape(1, -1, cm)], w_mod, m_w_mod, v_w_mod, "adamw_w_mod")

    small_names = ["b_mod", "conv_w", "hgrn_norm_w", "lower_bounds", "ln_g", "ln_b"]
    small_g = [grad_b_mod, grad_conv, grad_norm, grad_lower, grad_ln_g, grad_ln_b]
    small_w = [b_mod, conv_w, hgrn_norm_w, lower_bounds, ln_g, ln_b]
    small_m = [m_b_mod, m_conv_w, m_hgrn_norm_w, m_lower_bounds, m_ln_g, m_ln_b]
    small_v = [v_b_mod, v_conv_w, v_hgrn_norm_w, v_lower_bounds, v_ln_g, v_ln_b]
    pack = lambda arrs: _pad_rows(jnp.concatenate([a.reshape(-1) for a in arrs]), BLK)
    packed = _sum_adamw([pack(small_g)[None]], pack(small_w), pack(small_m), pack(small_v), "adamw_small")
    r_small = {n: [] for n in small_names}
    for res in packed:
        flat = res.reshape(-1)
        pos = 0
        for n, w in zip(small_names, small_w):
            r_small[n].append(flat[pos:pos + w.size].reshape(w.shape))
            pos += w.size

    results = {"w_mod": r_w_mod, "w_in": r_w_in, "w_branch": r_w_branch, "w_out": r_w_out, **r_small}
    order = ["w_mod", "b_mod", "w_in", "conv_w", "hgrn_norm_w", "lower_bounds", "w_branch", "w_out", "ln_g", "ln_b"]
    outs = [loss, grad_x]
    for idx in range(4):
        outs.extend(results[n][idx] for n in order)
    return tuple(outs)
```

```python
import jax
import jax.numpy as jnp
from jax import lax
from jax.experimental import pallas as pl
from jax.experimental.pallas import tpu as pltpu

F32 = jnp.float32
BF16 = jnp.bfloat16
NDEV = 8
N_LAYERS = 2
SB_HEAD_DIM = 64
HG_HEAD_DIM = 128
WIDTH = 512
BLK = 128
LN_EPS = 1e-5
RMS_EPS = 1e-6
ALPHA = (2.0 * N_LAYERS) ** 0.25
ADAM_LR, ADAM_B1, ADAM_B2, ADAM_EPS, ADAM_WD, ADAM_STEP = 0.001, 0.9, 0.999, 1e-08, 0.01, 10
VMEM_LIMIT = 56 * 1024 * 1024
MESH = pl.DeviceIdType.MESH
HG_LEVELS = (64, 32, 16, 8, 4, 2, 1)


def _pcall(body, *, name, out_shape, grid=None, in_specs=None, out_specs=None, scratch_shapes=(),
           semantics=None, aliases=None):
    kwargs = {}
    if grid is not None:
        kwargs["grid"] = grid
    if in_specs is not None:
        kwargs["in_specs"] = in_specs
    if out_specs is not None:
        kwargs["out_specs"] = out_specs
    if aliases:
        kwargs["input_output_aliases"] = aliases
    return pl.pallas_call(
        body, name=name, out_shape=out_shape, scratch_shapes=list(scratch_shapes),
        compiler_params=pltpu.CompilerParams(dimension_semantics=semantics, vmem_limit_bytes=VMEM_LIMIT),
        interpret=False, **kwargs)


def _dot(a, b):
    return jnp.dot(a, b, preferred_element_type=F32)


def _dot_nt(a, b):
    return lax.dot_general(a, b, (((1,), (1,)), ((), ())), preferred_element_type=F32)


def _dot_tn(a, b):
    return lax.dot_general(a, b, (((0,), (0,)), ((), ())), preferred_element_type=F32)


def _split3(x):
    x1 = x.astype(BF16)
    r1 = x - x1.astype(F32)
    x2 = r1.astype(BF16)
    r2 = r1 - x2.astype(F32)
    return x1, x2, r2.astype(BF16)


def _split2(x):
    x1 = x.astype(BF16)
    return x1, (x - x1.astype(F32)).astype(BF16)


def _dot_exact_l(m_bf16, x):
    x1, x2, x3 = _split3(x)
    return _dot(m_bf16, x1) + _dot(m_bf16, x2) + _dot(m_bf16, x3)


def _sigmoid(x):
    return 1.0 / (1.0 + jnp.exp(-x))


def _silu_and_grad(x):
    s = _sigmoid(x)
    return x * s, s * (1.0 + x * (1.0 - s))


LOG2E = 1.4426950408889634
MASKED_SCORE = -1e30


def _softplus2_parts(z2):
    minus_abs = lax.bitcast_convert_type(lax.bitcast_convert_type(z2, jnp.int32) | jnp.int32(-2 ** 31), F32)
    e = jnp.exp2(minus_abs)
    sp2 = jnp.maximum(z2, 0.0) + jnp.log2(1.0 + e)
    r = 1.0 / (1.0 + e)
    return sp2, jnp.where(z2 >= 0.0, r, e * r)


def _split2_lanes(x):
    x1 = x.astype(BF16)
    return jnp.concatenate([x1, (x - x1.astype(F32)).astype(BF16)], axis=1)


def _iota2(shape, dim):
    return lax.broadcasted_iota(jnp.int32, shape, dim)


def _standardize(x):
    mu = jnp.mean(x, axis=-1, keepdims=True)
    xc = x - mu
    var = jnp.mean(xc * xc, axis=-1, keepdims=True)
    rstd = lax.rsqrt(var + LN_EPS)
    return xc * rstd, rstd


def _standardize_bwd(xhat, rstd, dxhat):
    m1 = jnp.mean(dxhat, axis=-1, keepdims=True)
    m2 = jnp.mean(dxhat * xhat, axis=-1, keepdims=True)
    return rstd * (dxhat - m1 - xhat * m2)


def _my_index():
    return 4 * lax.axis_index("x") + 2 * lax.axis_index("y") + lax.axis_index("c")


def _exchange(name, ins, out_shapes, transfers, in_vmem):
    n_in, n_out, n_t = len(ins), len(out_shapes), len(transfers)

    def body(*refs):
        in_refs, out_refs = refs[:n_in], refs[n_in:n_in + n_out]
        send_sems, recv_sems, local_sems = refs[n_in + n_out:]
        x, y, c = lax.axis_index("x"), lax.axis_index("y"), lax.axis_index("c")
        me = 4 * x + 2 * y + c
        started = []
        for t, (i, o, src_fn, dst_fn) in enumerate(transfers):
            own = pltpu.make_async_copy(src_fn(in_refs[i], me), dst_fn(out_refs[o], me), local_sems.at[t])
            own.start()
            started.append(own)
        arrivals = []
        for k in range(1, NDEV):
            px = x ^ ((k >> 2) & 1)
            py = y ^ ((k >> 1) & 1)
            pc = c ^ (k & 1)
            peer = 4 * px + 2 * py + pc
            for t, (i, o, src_fn, dst_fn) in enumerate(transfers):
                sem = t * (NDEV - 1) + k - 1
                push = pltpu.make_async_remote_copy(
                    src_ref=src_fn(in_refs[i], peer), dst_ref=dst_fn(out_refs[o], me),
                    send_sem=send_sems.at[sem], recv_sem=recv_sems.at[sem],
                    device_id=(px, py, pc), device_id_type=MESH)
                push.start()
                started.append(push)
                arrivals.append(pltpu.make_async_remote_copy(
                    src_ref=src_fn(in_refs[i], peer), dst_ref=dst_fn(out_refs[o], peer),
                    send_sem=send_sems.at[sem], recv_sem=recv_sems.at[sem],
                    device_id=(px, py, pc), device_id_type=MESH))
        for arrival in arrivals:
            arrival.wait_recv()
        for cp in started[n_t:]:
            cp.wait_send()
        for own in started[:n_t]:
            own.wait()

    space = pltpu.VMEM if in_vmem else pl.ANY
    spec = pl.BlockSpec(memory_space=space)
    return _pcall(
        body, name=name, out_shape=out_shapes,
        in_specs=[spec] * n_in, out_specs=[spec] * n_out,
        scratch_shapes=[pltpu.SemaphoreType.DMA((n_t * (NDEV - 1),)),
                        pltpu.SemaphoreType.DMA((n_t * (NDEV - 1),)),
                        pltpu.SemaphoreType.DMA((n_t,))])(*ins)


def _whole(ref, dev):
    return ref


def _slot(ref, dev):
    return ref.at[dev]


def _all_gather_small(name, v):
    out = _exchange(name, [v], [jax.ShapeDtypeStruct((NDEV,) + v.shape, v.dtype)],
                    [(0, 0, _whole, _slot)], in_vmem=True)
    return out[0]


_HBM_SPEC = pl.BlockSpec(memory_space=pltpu.HBM)
_SEM_SPEC = pl.BlockSpec(memory_space=pltpu.SEMAPHORE)
_DATAFLOW = pltpu.SideEffectType.DATAFLOW_SIDE_EFFECTING


def _peer(x, y, c, k):
    px = x ^ ((k >> 2) & 1)
    py = y ^ ((k >> 1) & 1)
    pc = c ^ (k & 1)
    return (px, py, pc), 4 * px + 2 * py + pc


def _direct_sends(transfers):
    sends = []
    for k in range(1, NDEV):
        for i, o, src_fn, dst_fn in transfers:
            sends.append((k,
                          lambda ins, lands, me, i=i, k=k, src_fn=src_fn: src_fn(ins[i], me ^ k),
                          lambda lands, me, o=o, dst_fn=dst_fn: dst_fn(lands[o], me),
                          lambda lands, me, o=o, k=k, dst_fn=dst_fn: dst_fn(lands[o], me ^ k)))
    return sends


def _exchange_start(name, ins, lands, sends, after=None):
    n_in, n_buf = len(ins), len(ins) + len(lands)
    n_sem = len(sends)

    def body(*refs):
        in_refs, land_refs = refs[:n_in], refs[n_in:n_buf]
        n_skip = n_buf + (0 if after is None else 1)
        send_sems, recv_sems, token = refs[n_skip], refs[n_skip + 1], refs[-1]
        x, y, c = lax.axis_index("x"), lax.axis_index("y"), lax.axis_index("c")
        me = 4 * x + 2 * y + c
        for t, (k, src_fn, dst_fn, _) in enumerate(sends):
            pltpu.make_async_remote_copy(
                src_ref=src_fn(in_refs, land_refs, me), dst_ref=dst_fn(land_refs, me),
                send_sem=send_sems.at[t], recv_sem=recv_sems.at[t],
                device_id=_peer(x, y, c, k)[0], device_id_type=MESH).start()
        token[...] = jnp.zeros_like(token)

    bufs = [pltpu.with_memory_space_constraint(a, pltpu.HBM) for a in list(ins) + list(lands)]
    extra = [] if after is None else [after]
    outs = pl.pallas_call(
        body, name=name,
        out_shape=(pltpu.SemaphoreType.DMA((n_sem,)), pltpu.SemaphoreType.DMA((n_sem,)))
        + tuple(pltpu.HBM(a.shape, a.dtype) for a in bufs) + (jax.ShapeDtypeStruct((8, BLK), F32),),
        in_specs=[_HBM_SPEC] * n_buf + [pl.BlockSpec(memory_space=pl.ANY)] * len(extra),
        out_specs=(_SEM_SPEC, _SEM_SPEC) + (_HBM_SPEC,) * n_buf + (pl.BlockSpec(memory_space=pltpu.VMEM),),
        input_output_aliases={b: 2 + b for b in range(n_buf)},
        compiler_params=pltpu.CompilerParams(has_side_effects=_DATAFLOW),
        interpret=False)(*bufs, *extra)
    return outs[0], outs[1], list(outs[2:2 + n_in]), list(outs[2 + n_in:2 + n_buf]), outs[-1]


def _exchange_wait(name, started, after, sends):
    send_sems, recv_sems, ins, lands, _ = started
    n_in, n_buf = len(ins), len(ins) + len(lands)

    def body(*refs):
        in_refs, land_refs = refs[:n_in], refs[n_in:n_buf]
        send_sems, recv_sems = refs[n_buf], refs[n_buf + 1]
        x, y, c = lax.axis_index("x"), lax.axis_index("y"), lax.axis_index("c")
        me = 4 * x + 2 * y + c
        for t, (k, src_fn, _, rcv_fn) in enumerate(sends):
            cp = pltpu.make_async_remote_copy(
                src_ref=src_fn(in_refs, land_refs, me), dst_ref=rcv_fn(land_refs, me),
                send_sem=send_sems.at[t], recv_sem=recv_sems.at[t],
                device_id=_peer(x, y, c, k)[0], device_id_type=MESH)
            cp.wait_send()
            cp.wait_recv()

    bufs = list(ins) + list(lands)
    outs = pl.pallas_call(
        body, name=name, out_shape=tuple(pltpu.HBM(a.shape, a.dtype) for a in bufs),
        in_specs=[_HBM_SPEC] * n_buf + [_SEM_SPEC, _SEM_SPEC, pl.BlockSpec(memory_space=pl.ANY)],
        out_specs=(_HBM_SPEC,) * n_buf,
        input_output_aliases={b: b for b in range(n_buf)},
        compiler_params=pltpu.CompilerParams(has_side_effects=_DATAFLOW),
        interpret=False)(*bufs, send_sems, recv_sems, after)
    return list(outs[:n_in]), list(outs[n_in:])


def _place_own(shape, dtype, own, start):
    return lax.dynamic_update_slice(lax.empty(shape, dtype), own, start)


def _place_own_window(name, shape, own, me):
    rows, cols = own.shape

    def body(me_ref, zone_in, own_ref, zone_ref):
        del me_ref, zone_in
        zone_ref[...] = own_ref[...]

    return pl.pallas_call(
        body, name=name, out_shape=jax.ShapeDtypeStruct(shape, own.dtype),
        grid_spec=pltpu.PrefetchScalarGridSpec(
            num_scalar_prefetch=1, grid=(1,),
            in_specs=[pl.BlockSpec(memory_space=pl.ANY), pl.BlockSpec((rows, cols), lambda i, me_ref: (0, 0))],
            out_specs=pl.BlockSpec((rows, cols), lambda i, me_ref: (0, me_ref[0]))),
        input_output_aliases={1: 0},
        compiler_params=pltpu.CompilerParams(dimension_semantics=("arbitrary",), vmem_limit_bytes=VMEM_LIMIT),
        interpret=False)(me.reshape(1).astype(jnp.int32), lax.empty(shape, own.dtype), own)


def _mod_fwd(c_all, w_mod, b_mod_mine):
    n_layers, _, cm = w_mod.shape

    def body(c_ref, w_ref, b_ref, o_ref):
        for l in range(n_layers):
            o_ref[l] = jnp.dot(c_ref[...], w_ref[l], preferred_element_type=F32,
                               precision=lax.Precision.HIGHEST) + b_ref[l]

    return _pcall(body, name="mod_fwd", out_shape=jax.ShapeDtypeStruct((n_layers, NDEV, cm), F32))(
        c_all, w_mod, b_mod_mine)


def _ln_proj(x, shift, scale, w_full, name):
    s_len, d = x.shape
    n = w_full.shape[1]
    tm = min(1024, s_len)
    tn = 1152

    def body(x_ref, sh_ref, sc_ref, w_ref, proj_ref, ht_ref, h_scr):
        @pl.when(pl.program_id(1) == 0)
        def _():
            xs, _ = _standardize(x_ref[...])
            h = xs * (1.0 + sc_ref[...]) + sh_ref[...]
            h_scr[...] = h.astype(BF16)
            ht_ref[...] = h.T.astype(BF16)

        proj_ref[...] = _dot(h_scr[...], w_ref[...])

    return _pcall(
        body, name=name,
        out_shape=(jax.ShapeDtypeStruct((s_len, n), F32), jax.ShapeDtypeStruct((d, s_len), BF16)),
        grid=(s_len // tm, n // tn),
        in_specs=[pl.BlockSpec((tm, d), lambda i, j: (i, 0)),
                  pl.BlockSpec((1, d), lambda i, j: (0, 0)),
                  pl.BlockSpec((1, d), lambda i, j: (0, 0)),
                  pl.BlockSpec((d, tn), lambda i, j: (0, j))],
        out_specs=(pl.BlockSpec((tm, tn), lambda i, j: (i, j)),
                   pl.BlockSpec((d, tm), lambda i, j: (0, i))),
        scratch_shapes=[pltpu.VMEM((tm, d), BF16)],
        semantics=("arbitrary", "arbitrary"))(x, shift, scale, w_full)


def _sb_group_blocks(nb):
    return min(4, nb)


def _sb_fwd(proj, name):
    s_len = proj.shape[0]
    nb = s_len // BLK
    n_pairs = WIDTH // BLK
    gb = _sb_group_blocks(nb)
    kw = gb * BLK

    def body(q_ref, k_ref, v_ref, o_ref, tot_ref):
        lane = _iota2((1, BLK), 1)
        row = _iota2((BLK, BLK), 0)
        col = _iota2((BLK, BLK), 1)
        half = jnp.concatenate([(row >= col).astype(BF16), jnp.ones((BLK, BLK), BF16)], axis=1)
        suffix_and_sum = jnp.concatenate([half, half], axis=0)
        qpos = _iota2((BLK, kw), 0)
        kpos = _iota2((BLK, kw), 1)
        head_lanes = [(lane // SB_HEAD_DIM) == hh for hh in range(2)]

        def scores(i, gi, qms, masked):
            c0 = pl.multiple_of(gi * kw, kw)
            kb = k_ref[pl.ds(c0, kw), :].astype(BF16)
            z2s = [_dot_nt(qms[hh], kb) for hh in range(2)]
            if masked:
                valid = (c0 + kpos) < (i * BLK + qpos)
                z2s = [jnp.where(valid, z2, MASKED_SCORE) for z2 in z2s]
            return tuple(z2s)

        def accumulate(gi, z2s, carry):
            c0 = pl.multiple_of(gi * kw, kw)
            vf = v_ref[pl.ds(c0, kw), :]
            sp2s = [_softplus2_parts(z2)[0] for z2 in z2s]
            terms = [[_split2_lanes(sp2[:, b * BLK:(b + 1) * BLK]) for b in range(gb)] for sp2 in sp2s]
            sums = [[_dot(t, suffix_and_sum) for t in head_terms] for head_terms in terms]
            weights, laters = [], []
            for hh in range(2):
                later = carry[2 * hh + 1]
                parts = [None] * gb
                for b in reversed(range(gb)):
                    parts[b] = sums[hh][b][:, :BLK] + later
                    later = later + sums[hh][b][:, BLK:]
                weights.append(jnp.exp2(z2s[hh] - jnp.concatenate(parts, axis=1)).astype(BF16))
                laters.append(later)
            outs = [_dot(weights[hh], jnp.where(head_lanes[hh], vf, 0.0).astype(BF16)) for hh in range(2)]
            return (carry[0] + outs[0], laters[0], carry[2] + outs[1], laters[1])

        def queries(i):
            qf = q_ref[pl.ds(pl.multiple_of(i * BLK, BLK), BLK), :] * (SB_HEAD_DIM ** -0.5 * LOG2E)
            return [jnp.where(head_lanes[hh], qf, 0.0).astype(BF16) for hh in range(2)]

        def qblock(i, first_scores):
            r0 = pl.multiple_of(i * BLK, BLK)
            qms = queries(i)
            zero = jnp.zeros((BLK, BLK), F32)
            last = i // gb

            def step(jj, state):
                gi = last - 1 - jj
                return scores(i, gi, qms, False) + accumulate(gi + 1, state[:2], state[2:])

            state = lax.fori_loop(0, last, step, first_scores + (zero,) * 4)
            nxt = jnp.minimum(i + 1, nb - 1)
            next_scores = scores(nxt, nxt // gb, queries(nxt), True)
            carry = accumulate(0, state[:2], state[2:])
            o_ref[pl.ds(r0, BLK), :] = carry[0] + carry[2]
            tot_ref[0, pl.ds(r0, BLK), :] = carry[1]
            tot_ref[1, pl.ds(r0, BLK), :] = carry[3]
            return next_scores

        lax.fori_loop(0, nb, qblock, scores(0, 0, queries(0), True))

    col_spec = lambda off: pl.BlockSpec((s_len, BLK), lambda p: (0, off + p))
    return _pcall(
        body, name=name,
        out_shape=(jax.ShapeDtypeStruct((s_len, WIDTH), F32),
                   jax.ShapeDtypeStruct((2 * n_pairs, s_len, BLK), F32)),
        grid=(n_pairs,),
        in_specs=[col_spec(0), col_spec(n_pairs), col_spec(2 * n_pairs)],
        out_specs=(pl.BlockSpec((s_len, BLK), lambda p: (0, p)),
                   pl.BlockSpec((2, s_len, BLK), lambda p: (p, 0, 0))),
        semantics=("arbitrary",))(proj, proj, proj)


def _hg_masks(mask_ref):
    row = _iota2((BLK, BLK), 0)
    col = _iota2((BLK, BLK), 1)
    for v, m in enumerate(HG_LEVELS):
        same = (row // (2 * m)) == (col // (2 * m))
        mask_ref[v] = (same & ((row & m) != 0) & ((col & m) == 0)).astype(F32)


def _hg_mid(b, m):
    if m >= 4:
        n = BLK // (2 * m)
        mid = b.reshape(n, 2 * m, BLK)[:, m - 1:m, :]
        return jnp.broadcast_to(mid, (n, 2 * m, BLK)).reshape(BLK, BLK)
    pos = _iota2((BLK, BLK), 0) & (2 * m - 1)
    out = b
    for p in range(2 * m):
        delta = (m - 1) - p
        if delta != 0:
            out = jnp.where(pos == p, pltpu.roll(b, (-delta) % BLK, 0), out)
    return out


def _hg_chunk_inputs(qraw, fpre, lb):
    sig = _sigmoid(fpre)
    f = lb + (1.0 - lb) * sig
    g = jnp.log(f)
    q, dq_fac = _silu_and_grad(qraw)
    return q, dq_fac, f, sig, g


HG_GROUP = 4


def _neg_abs(x):
    return lax.bitcast_convert_type(lax.bitcast_convert_type(x, jnp.int32) | jnp.int32(-2 ** 31), F32)


def _hg_level_terms(qs, ks, bs, m):
    es = [jnp.exp(_neg_abs(b - _hg_mid(b, m))) for b in bs]
    qts = [(q * e).astype(BF16) for q, e in zip(qs, es)]
    kts = [(k * e).astype(BF16) for k, e in zip(ks, es)]
    return es, qts, kts


def _hg_load(refs, r0, lb_v, lower_incl):
    q_ref, f_ref, i_ref = refs
    heads = []
    for h in range(HG_GROUP):
        sl = slice(h * HG_HEAD_DIM, (h + 1) * HG_HEAD_DIM)
        heads.append(_hg_chunk_inputs(q_ref[pl.ds(r0, BLK), sl], f_ref[pl.ds(r0, BLK), sl], lb_v[:, sl])
                     + (i_ref[pl.ds(r0, BLK), sl],))
    bs = [_dot_exact_l(lower_incl, hd[4]) for hd in heads]
    return heads, bs


def _hgrn_fwd(proj, lb, name):
    s_len = proj.shape[0]
    nc = s_len // BLK
    gw = HG_GROUP * HG_HEAD_DIM
    n_groups = WIDTH // gw
    base = 4 * WIDTH // gw

    def body(q_ref, f_ref, i_ref, lb_ref, o_ref, mask_ref):
        _hg_masks(mask_ref)
        row = _iota2((BLK, BLK), 0)
        col = _iota2((BLK, BLK), 1)
        lower_incl = (col <= row).astype(BF16)
        lb_v = lb_ref[...]

        def chunk(ci, sts):
            r0 = pl.multiple_of(ci * BLK, BLK)
            heads, bs = _hg_load((q_ref, f_ref, i_ref), r0, lb_v, lower_incl)
            qs = [hd[0] for hd in heads]
            ks = [1.0 - hd[2] for hd in heads]
            vs = [hd[5] for hd in heads]
            vbs = [v.astype(BF16) for v in vs]
            b_ends = [b[BLK - 1:BLK, :] for b in bs]
            inters = [_dot_nt((q * jnp.exp(b)).astype(BF16), st.astype(BF16)) for q, b, st in zip(qs, bs, sts)]
            scs = [None] * HG_GROUP
            for v_idx, m in enumerate(HG_LEVELS):
                _, qts, kts = _hg_level_terms(qs, ks, bs, m)
                terms = [_dot_nt(qt, kt) for qt, kt in zip(qts, kts)]
                msk = mask_ref[v_idx]
                scs = [t * msk if sc is None else sc + t * msk for sc, t in zip(scs, terms)]
            intras = [_dot(sc.astype(BF16), vb) for sc, vb in zip(scs, vbs)]
            k_decs = [(k * jnp.exp(b_end - b)).astype(BF16) for k, b, b_end in zip(ks, bs, b_ends)]
            grown = [_dot_tn(vb, k_dec) for vb, k_dec in zip(vbs, k_decs)]
            for h in range(HG_GROUP):
                diag = jnp.sum(qs[h] * ks[h], axis=-1, keepdims=True)
                o_ref[pl.ds(r0, BLK), h * HG_HEAD_DIM:(h + 1) * HG_HEAD_DIM] = inters[h] + intras[h] + diag * vs[h]
            return tuple(st * jnp.exp(b_end) + g for st, b_end, g in zip(sts, b_ends, grown))

        lax.fori_loop(0, nc, chunk, (jnp.zeros((HG_HEAD_DIM, HG_HEAD_DIM), F32),) * HG_GROUP)

    col_spec = lambda off: pl.BlockSpec((s_len, gw), lambda h: (0, off + h))
    return _pcall(
        body, name=name, out_shape=jax.ShapeDtypeStruct((s_len, WIDTH), F32),
        grid=(n_groups,),
        in_specs=[col_spec(base), col_spec(base + n_groups), col_spec(base + 2 * n_groups),
                  pl.BlockSpec((1, gw), lambda h: (0, h))],
        out_specs=pl.BlockSpec((s_len, gw), lambda h: (0, h)),
        scratch_shapes=[pltpu.VMEM((len(HG_LEVELS), BLK, BLK), F32)],
        semantics=("arbitrary",))(proj, proj, proj, lb)


def _rms_heads(o_b, norm_w):
    n_parts, h_parts, r_parts = [], [], []
    for h in range(WIDTH // HG_HEAD_DIM):
        sl = slice(h * HG_HEAD_DIM, (h + 1) * HG_HEAD_DIM)
        o = o_b[:, sl]
        rstd = lax.rsqrt(jnp.mean(o * o, axis=-1, keepdims=True) + RMS_EPS)
        ohat = o * rstd
        h_parts.append(ohat)
        n_parts.append(ohat * norm_w[:, sl])
        r_parts.append(jnp.broadcast_to(rstd, o.shape))
    cat = lambda parts: jnp.concatenate(parts, axis=-1)
    return cat(n_parts), cat(h_parts), cat(r_parts)


def _shift_rows_down(halo, cur, k):
    tm = cur.shape[0]
    ext = jnp.concatenate([halo, cur], axis=0)
    return pltpu.roll(ext, k, 0)[8:8 + tm]


def _shift_rows_up(cur, halo, k):
    tm = cur.shape[0]
    ext = jnp.concatenate([cur, halo], axis=0)
    return pltpu.roll(ext, (tm + 8 - k) % (tm + 8), 0)[0:tm]


def _merge_fwd(x, proj, o_a, o_b, gate, norm_w, conv_w, wb, w_out, ln_g, ln_b, name):
    s_len, d = x.shape
    tm = min(256, s_len)
    hb = tm // 8

    def body(x_ref, oa_ref, za_ref, ob_ref, zb_ref, pre_ref, post_ref, u_ref, zc_ref, hpre_ref, hu_ref, g_ref,
             gate_ref, nw_ref, cw_ref, wb_ref, wo_ref, lg_ref, lbias_ref, xn_ref, mg_ref, yc_ref):
        i = pl.program_id(0)
        sa, _ = _silu_and_grad(za_ref[...])
        y_a = (oa_ref[...] * sa).astype(BF16)
        n_b, _, _ = _rms_heads(ob_ref[...], nw_ref[...])
        sb, _ = _silu_and_grad(zb_ref[...])
        y_b = (n_b * sb).astype(BF16)
        a = pre_ref[...] * u_ref[...]
        halo = jnp.where(i > 0, hpre_ref[...] * hu_ref[...], 0.0)
        cw = cw_ref[...]
        conv = cw[0:1] * _shift_rows_down(halo, a, 2) + cw[1:2] * _shift_rows_down(halo, a, 1) + cw[2:3] * a
        sc, _ = _silu_and_grad(zc_ref[...])
        y_c = (post_ref[...] * conv * sc).astype(BF16)
        merged = None
        for k, yk in enumerate((y_a, y_b, y_c)):
            yc_ref[:, k * WIDTH:(k + 1) * WIDTH] = yk
            term = _sigmoid(g_ref[:, k * d:(k + 1) * d]) * _dot(yk, wb_ref[k])
            merged = term if merged is None else merged + term
        mb = merged.astype(BF16)
        mg_ref[...] = mb
        y = _dot(mb, wo_ref[...])
        r = ALPHA * x_ref[...] + (1.0 + gate_ref[...]) * y
        rhat, _ = _standardize(r)
        xn_ref[...] = rhat * lg_ref[...] + lbias_ref[...]

    wcol = lambda cb: pl.BlockSpec((tm, WIDTH), lambda i: (i, cb))
    halo_spec = lambda cb: pl.BlockSpec((8, WIDTH), lambda i: (jnp.maximum(i * hb - 1, 0), cb))
    vec = lambda w: pl.BlockSpec((1, w), lambda i: (0, 0))
    return _pcall(
        body, name=name,
        out_shape=(jax.ShapeDtypeStruct((s_len, d), F32), jax.ShapeDtypeStruct((s_len, d), BF16),
                   jax.ShapeDtypeStruct((s_len, 3 * WIDTH), BF16)),
        grid=(s_len // tm,),
        in_specs=[pl.BlockSpec((tm, d), lambda i: (i, 0)),
                  wcol(0), wcol(3), wcol(0), wcol(7), wcol(8), wcol(9), wcol(10), wcol(11),
                  halo_spec(8), halo_spec(10),
                  pl.BlockSpec((tm, 3 * d), lambda i: (i, 2)),
                  vec(d), vec(WIDTH),
                  pl.BlockSpec((3, WIDTH), lambda i: (0, 0)),
                  pl.BlockSpec((3, WIDTH, d), lambda i: (0, 0, 0)),
                  pl.BlockSpec((d, d), lambda i: (0, 0)),
                  vec(d), vec(d)],
        out_specs=(pl.BlockSpec((tm, d), lambda i: (i, 0)), pl.BlockSpec((tm, d), lambda i: (i, 0)),
                   pl.BlockSpec((tm, 3 * WIDTH), lambda i: (i, 0))),
        semantics=("arbitrary",))(x, o_a, proj, o_b, proj, proj, proj, proj, proj, proj, proj, proj,
                                  gate, norm_w, conv_w, wb, w_out, ln_g, ln_b)


def _loss_fwd_bwd(y, target):
    s_len, d = y.shape
    tm = min(512, s_len)

    def body(y_ref, t_ref, loss_ref, dy_ref):
        @pl.when(pl.program_id(0) == 0)
        def _():
            loss_ref[...] = jnp.zeros_like(loss_ref)

        e = y_ref[...] - t_ref[...]
        dy_ref[...] = e * (1.0 / d)
        part = jnp.sum(jnp.sum(e * e, axis=-1, keepdims=True), axis=0, keepdims=True)
        loss_ref[...] += part * (0.5 / d)

    tile = pl.BlockSpec((tm, d), lambda i: (i, 0))
    return _pcall(body, name="loss", grid=(s_len // tm,),
                  out_shape=(jax.ShapeDtypeStruct((1, 1), F32), jax.ShapeDtypeStruct((s_len, d), F32)),
                  in_specs=[tile, tile],
                  out_specs=(pl.BlockSpec((1, 1), lambda i: (0, 0)), tile),
                  semantics=("arbitrary",))(y, target)


def _merge_bwd(dxn, x, merged, ycat, proj, gate, wb, w_out, ln_g, name):
    s_len, d = x.shape
    tm = min(256, s_len)
    dsh = d // NDEV
    n_tiles = s_len // tm

    def body(dxn_ref, x_ref, mg_ref, yc_ref, g_ref, gate_ref, wb_ref, wo_ref, lg_ref,
             dres_ref, dyc_ref, dg_ref, gwo_out, gwb_out, vec_ref, gwo_ref, gwb_ref):
        @pl.when(pl.program_id(0) == 0)
        def _():
            gwo_ref[...] = jnp.zeros_like(gwo_ref)
            gwb_ref[...] = jnp.zeros_like(gwb_ref)
            vec_ref[...] = jnp.zeros_like(vec_ref)

        mb = mg_ref[...]
        one_gate = 1.0 + gate_ref[...]
        y = _dot(mb, wo_ref[...])
        r = ALPHA * x_ref[...] + one_gate * y
        rhat, rstd = _standardize(r)
        dxn = dxn_ref[...]
        dr = _standardize_bwd(rhat, rstd, dxn * lg_ref[...])
        vec_ref[0:1, :] += jnp.sum(dxn * rhat, axis=0, keepdims=True)
        vec_ref[1:2, :] += jnp.sum(dxn, axis=0, keepdims=True)
        vec_ref[2:3, :] += jnp.sum(dr * y, axis=0, keepdims=True)
        dres_ref[...] = ALPHA * dr
        dy = (one_gate * dr).astype(BF16)
        gwo_ref[...] += _dot_tn(mb, dy)
        dmerged = _dot_nt(dy, wo_ref[...])
        for k in range(3):
            yk = yc_ref[:, k * WIDTH:(k + 1) * WIDTH]
            sg = _sigmoid(g_ref[:, k * d:(k + 1) * d])
            pk = _dot(yk, wb_ref[k])
            dg_ref[:, k * d:(k + 1) * d] = (dmerged * pk * sg * (1.0 - sg)).astype(BF16)
            dpk = (dmerged * sg).astype(BF16)
            dyc_ref[:, k * WIDTH:(k + 1) * WIDTH] = _dot_nt(dpk, wb_ref[k])
            gwb_ref[k] += _dot_tn(yk, dpk)

        @pl.when(pl.program_id(0) == n_tiles - 1)
        def _():
            for o in range(NDEV):
                gwo_out[o] = gwo_ref[o * dsh:(o + 1) * dsh, :].astype(BF16)
                for k in range(3):
                    gwb_out[o, k] = gwb_ref[k, :, o * dsh:(o + 1) * dsh].astype(BF16)

    tile = lambda w: pl.BlockSpec((tm, w), lambda i: (i, 0))
    vec = pl.BlockSpec((1, d), lambda i: (0, 0))
    return _pcall(
        body, name=name,
        out_shape=(jax.ShapeDtypeStruct((s_len, d), F32), jax.ShapeDtypeStruct((s_len, 3 * WIDTH), F32),
                   jax.ShapeDtypeStruct(proj.shape, BF16), jax.ShapeDtypeStruct((NDEV, dsh, d), BF16),
                   jax.ShapeDtypeStruct((NDEV, 3, WIDTH, dsh), BF16), jax.ShapeDtypeStruct((8, d), F32)),
        grid=(n_tiles,),
        in_specs=[tile(d), tile(d), tile(d), tile(3 * WIDTH),
                  pl.BlockSpec((tm, 3 * d), lambda i: (i, 2)),
                  vec, pl.BlockSpec((3, WIDTH, d), lambda i: (0, 0, 0)),
                  pl.BlockSpec((d, d), lambda i: (0, 0)), vec],
        out_specs=(tile(d), tile(3 * WIDTH), pl.BlockSpec((tm, 3 * d), lambda i: (i, 2)),
                   pl.BlockSpec((NDEV, dsh, d), lambda i: (0, 0, 0)),
                   pl.BlockSpec((NDEV, 3, WIDTH, dsh), lambda i: (0, 0, 0, 0)),
                   pl.BlockSpec((8, d), lambda i: (0, 0))),
        scratch_shapes=[pltpu.VMEM((d, d), F32), pltpu.VMEM((3, WIDTH, d), F32)],
        semantics=("arbitrary",))(dxn, x, merged, ycat, proj, gate, wb, w_out, ln_g)


def _branch_bwd(dycat, proj, o_a, o_b, norm_w, conv_w, dproj, name):
    s_len = proj.shape[0]
    tm = min(256, s_len)
    hb = tm // 8
    n_tiles = s_len // tm

    def body(dya_ref, dyb_ref, dyc_ref, oa_ref, za_ref, ob_ref, zb_ref, pre_ref, post_ref, u_ref, zc_ref,
             hpre_ref, hu_ref, ndyc_ref, npost_ref, nzc_ref, nw_ref, cw_ref, dproj_in,
             dproj_ref, doa_ref, dob_ref, vec_ref, dza_scr, dzb_scr, dc_scr, sems):
        del dproj_in
        i = pl.program_id(0)

        @pl.when(i == 0)
        def _():
            vec_ref[...] = jnp.zeros_like(vec_ref)

        sa, dsa = _silu_and_grad(za_ref[...])
        dya = dya_ref[...]
        doa_ref[...] = dya * sa
        dza_scr[...] = (dya * oa_ref[...] * dsa).astype(BF16)
        nw = nw_ref[...]
        n_b, ohat, rstd = _rms_heads(ob_ref[...], nw)
        sb, dsb = _silu_and_grad(zb_ref[...])
        dyb = dyb_ref[...]
        dzb_scr[...] = (dyb * n_b * dsb).astype(BF16)
        dn = dyb * sb
        vec_ref[0:1, :] += jnp.sum(dn * ohat, axis=0, keepdims=True)
        dnw = dn * nw
        parts = []
        for h in range(WIDTH // HG_HEAD_DIM):
            sl = slice(h * HG_HEAD_DIM, (h + 1) * HG_HEAD_DIM)
            m2 = jnp.mean(dnw[:, sl] * ohat[:, sl], axis=-1, keepdims=True)
            parts.append(rstd[:, sl] * (dnw[:, sl] - ohat[:, sl] * m2))
        dob_ref[...] = jnp.concatenate(parts, axis=-1)
        cw = cw_ref[...]
        pre, u, post = pre_ref[...], u_ref[...], post_ref[...]
        a = pre * u
        halo = jnp.where(i > 0, hpre_ref[...] * hu_ref[...], 0.0)
        a1 = _shift_rows_down(halo, a, 1)
        a2 = _shift_rows_down(halo, a, 2)
        conv = cw[0:1] * a2 + cw[1:2] * a1 + cw[2:3] * a
        sc, dsc = _silu_and_grad(zc_ref[...])
        dyc = dyc_ref[...]
        dconv = dyc * post * sc
        nsc, _ = _silu_and_grad(nzc_ref[...])
        nxt = jnp.where(i < n_tiles - 1, ndyc_ref[...] * npost_ref[...] * nsc, 0.0)
        da = cw[2:3] * dconv + cw[1:2] * _shift_rows_up(dconv, nxt, 1) + cw[0:1] * _shift_rows_up(dconv, nxt, 2)
        dc_scr[:, 0 * WIDTH:1 * WIDTH] = (da * u).astype(BF16)
        dc_scr[:, 1 * WIDTH:2 * WIDTH] = (dyc * conv * sc).astype(BF16)
        dc_scr[:, 2 * WIDTH:3 * WIDTH] = (da * pre).astype(BF16)
        dc_scr[:, 3 * WIDTH:4 * WIDTH] = (dyc * post * conv * dsc).astype(BF16)
        vec_ref[1:2, :] += jnp.sum(dconv * a2, axis=0, keepdims=True)
        vec_ref[2:3, :] += jnp.sum(dconv * a1, axis=0, keepdims=True)
        vec_ref[3:4, :] += jnp.sum(dconv * a, axis=0, keepdims=True)
        rows = pl.ds(pl.multiple_of(i * tm, tm), tm)
        copies = [pltpu.make_async_copy(dza_scr, dproj_ref.at[rows, 3 * WIDTH:4 * WIDTH], sems.at[0]),
                  pltpu.make_async_copy(dzb_scr, dproj_ref.at[rows, 7 * WIDTH:8 * WIDTH], sems.at[1]),
                  pltpu.make_async_copy(dc_scr, dproj_ref.at[rows, 8 * WIDTH:12 * WIDTH], sems.at[2])]
        for cp in copies:
            cp.start()
        for cp in copies:
            cp.wait()

    wcol = lambda cb: pl.BlockSpec((tm, WIDTH), lambda i: (i, cb))
    prev = lambda cb: pl.BlockSpec((8, WIDTH), lambda i: (jnp.maximum(i * hb - 1, 0), cb))
    nxt = lambda cb: pl.BlockSpec((8, WIDTH), lambda i: (jnp.minimum((i + 1) * hb, s_len // 8 - 1), cb))
    anyspec = pl.BlockSpec(memory_space=pl.ANY)
    out = jax.ShapeDtypeStruct((s_len, WIDTH), F32)
    return _pcall(
        body, name=name,
        out_shape=(jax.ShapeDtypeStruct(dproj.shape, dproj.dtype), out, out, jax.ShapeDtypeStruct((8, WIDTH), F32)),
        grid=(n_tiles,),
        in_specs=[wcol(0), wcol(1), wcol(2), wcol(0), wcol(3), wcol(0), wcol(7), wcol(8), wcol(9), wcol(10), wcol(11),
                  prev(8), prev(10), nxt(2), nxt(9), nxt(11),
                  pl.BlockSpec((1, WIDTH), lambda i: (0, 0)), pl.BlockSpec((3, WIDTH), lambda i: (0, 0)), anyspec],
        out_specs=(anyspec, wcol(0), wcol(0), pl.BlockSpec((8, WIDTH), lambda i: (0, 0))),
        scratch_shapes=[pltpu.VMEM((tm, WIDTH), BF16), pltpu.VMEM((tm, WIDTH), BF16),
                        pltpu.VMEM((tm, 4 * WIDTH), BF16), pltpu.SemaphoreType.DMA((3,))],
        aliases={18: 0},
        semantics=("arbitrary",))(dycat, dycat, dycat, o_a, proj, o_b, proj, proj, proj, proj, proj,
                                  proj, proj, dycat, proj, proj, norm_w, conv_w, dproj)


def _sb_bwd(proj, do_a, totals, dproj, name):
    s_len = proj.shape[0]
    nb = s_len // BLK
    n_pairs = WIDTH // BLK
    scale = SB_HEAD_DIM ** -0.5
    gb = _sb_group_blocks(nb)
    kw = gb * BLK

    def body(q_ref, k_ref, v_ref, do_ref, tot_ref, dproj_in, dproj_ref, dq_ref, dk_ref, dv_ref, out_scr, sems):
        del dproj_in
        lane = _iota2((1, BLK), 1)
        row = _iota2((BLK, BLK), 0)
        col = _iota2((BLK, BLK), 1)
        ones = jnp.ones((BLK, BLK), BF16)
        twice = lambda m: jnp.concatenate([m, m], axis=0)
        before_and_sum = twice(jnp.concatenate([(row < col).astype(BF16), ones], axis=1))
        upto_and_sum = twice(jnp.concatenate([(row <= col).astype(BF16), ones], axis=1))
        qpos = _iota2((BLK, kw), 0)
        kpos = _iota2((BLK, kw), 1)
        head_lanes = [(lane // SB_HEAD_DIM) == hh for hh in range(2)]
        dk_ref[...] = jnp.zeros_like(dk_ref)
        dv_ref[...] = jnp.zeros_like(dv_ref)

        causal = kpos - qpos

        def scores(i, gi, qms):
            c0 = pl.multiple_of(gi * kw, kw)
            kb = k_ref[pl.ds(c0, kw), :].astype(BF16)
            valid = causal < i * BLK - c0
            return tuple(jnp.where(valid, _dot_nt(qms[hh], kb), MASKED_SCORE) for hh in range(2))

        def process(gi, z2s, qms, doms, totals_i, carry):
            c0 = pl.multiple_of(gi * kw, kw)
            kf = k_ref[pl.ds(c0, kw), :]
            vf = v_ref[pl.ds(c0, kw), :]
            kms = [jnp.where(head_lanes[hh], kf, 0.0).astype(BF16) for hh in range(2)]
            vms = [jnp.where(head_lanes[hh], vf, 0.0).astype(BF16) for hh in range(2)]
            das = [_dot_nt(doms[hh], vms[hh]) for hh in range(2)]
            halves = [_softplus2_parts(z2) for z2 in z2s]
            terms = [[_split2_lanes(sp2[:, b * BLK:(b + 1) * BLK]) for b in range(gb)] for sp2, _ in halves]
            sums = [[_dot(t, before_and_sum) for t in head_terms] for head_terms in terms]
            weights, gmats, l_befores = [], [], []
            for hh in range(2):
                l_before = carry[3 * hh + 1]
                parts = []
                for b in range(gb):
                    parts.append(totals_i[hh] - l_before - sums[hh][b][:, :BLK])
                    l_before = l_before + sums[hh][b][:, BLK:]
                a = jnp.exp2(z2s[hh] - jnp.concatenate(parts, axis=1))
                weights.append(a.astype(BF16))
                gmats.append(a * das[hh])
                l_befores.append(l_before)
            terms = [[_split2_lanes(g[:, b * BLK:(b + 1) * BLK]) for b in range(gb)] for g in gmats]
            sums = [[_dot(t, upto_and_sum) for t in head_terms] for head_terms in terms]
            dzs, g_befores = [], []
            for hh in range(2):
                g_before = carry[3 * hh + 2]
                parts = []
                for b in range(gb):
                    parts.append(g_before + sums[hh][b][:, :BLK])
                    g_before = g_before + sums[hh][b][:, BLK:]
                dzs.append((gmats[hh] - halves[hh][1] * jnp.concatenate(parts, axis=1)).astype(BF16))
                g_befores.append(g_before)
            dk_t = _dot_tn(jnp.concatenate(qms, axis=0), jnp.concatenate(dzs, axis=0))
            dv_t = _dot_tn(jnp.concatenate(doms, axis=0), jnp.concatenate(weights, axis=0))
            dqs = [_dot(dzs[hh], kms[hh]) for hh in range(2)]
            dk_ref[:, pl.ds(c0, kw)] += dk_t * (1.0 / LOG2E)
            dv_ref[:, pl.ds(c0, kw)] += dv_t
            return (carry[0] + dqs[0], l_befores[0], g_befores[0], carry[3] + dqs[1], l_befores[1], g_befores[1])

        def queries(i):
            qf = q_ref[pl.ds(pl.multiple_of(i * BLK, BLK), BLK), :] * (scale * LOG2E)
            return [jnp.where(head_lanes[hh], qf, 0.0).astype(BF16) for hh in range(2)]

        def qblock(i, first_scores):
            r0 = pl.multiple_of(i * BLK, BLK)
            qms = queries(i)
            dof = do_ref[pl.ds(r0, BLK), :]
            doms = [jnp.where(head_lanes[hh], dof, 0.0).astype(BF16) for hh in range(2)]
            totals_i = [tot_ref[hh, pl.ds(r0, BLK), :] for hh in range(2)]
            zero = jnp.zeros((BLK, BLK), F32)
            last = i // gb

            def step(gi, state):
                return scores(i, gi + 1, qms) + process(gi, state[:2], qms, doms, totals_i, state[2:])

            state = lax.fori_loop(0, last, step, first_scores + (zero,) * 6)
            nxt = jnp.minimum(i + 1, nb - 1)
            next_scores = scores(nxt, 0, queries(nxt))
            carry = process(last, state[:2], qms, doms, totals_i, state[2:])
            dq_ref[pl.ds(r0, BLK), :] = (carry[0] + carry[3]) * scale
            return next_scores

        lax.fori_loop(0, nb, qblock, scores(0, 0, queries(0)))
        pair = pl.program_id(0)
        copies = []
        for t, value in enumerate((dq_ref[...], dk_ref[...].T, dv_ref[...].T)):
            out_scr[t] = value.astype(BF16)
            col = pl.multiple_of((t * n_pairs + pair) * BLK, BLK)
            copies.append(pltpu.make_async_copy(out_scr.at[t], dproj_ref.at[:, pl.ds(col, BLK)], sems.at[t]))
            copies[-1].start()
        for cp in copies:
            cp.wait()

    col_spec = lambda off: pl.BlockSpec((s_len, BLK), lambda p: (0, off + p))
    anyspec = pl.BlockSpec(memory_space=pl.ANY)
    return _pcall(
        body, name=name, out_shape=jax.ShapeDtypeStruct(dproj.shape, dproj.dtype), grid=(n_pairs,),
        in_specs=[col_spec(0), col_spec(n_pairs), col_spec(2 * n_pairs), col_spec(0),
                  pl.BlockSpec((2, s_len, BLK), lambda p: (p, 0, 0)), anyspec],
        out_specs=anyspec,
        scratch_shapes=[pltpu.VMEM((s_len, BLK), F32), pltpu.VMEM((BLK, s_len), F32), pltpu.VMEM((BLK, s_len), F32),
                        pltpu.VMEM((3, s_len, BLK), BF16), pltpu.SemaphoreType.DMA((3,))],
        aliases={5: 0},
        semantics=("arbitrary",))(proj, proj, proj, do_a, totals, dproj)


def _hgrn_bwd(proj, do_b, lb, dproj, name):
    s_len = proj.shape[0]
    nc = s_len // BLK
    gw = HG_GROUP * HG_HEAD_DIM
    n_groups = WIDTH // gw
    base = 4 * WIDTH // gw
    heads_of = range(HG_GROUP)

    def body(q_ref, f_ref, i_ref, do_ref, lb_ref, dproj_in, dproj_ref, dlb_ref, mask_ref, st_ref, out_scr, sems):
        del dproj_in
        _hg_masks(mask_ref)
        row = _iota2((BLK, BLK), 0)
        col = _iota2((BLK, BLK), 1)
        lower_incl = (col <= row).astype(BF16)
        upper_incl = (col >= row).astype(BF16)
        lb_v = lb_ref[...]
        refs = (q_ref, f_ref, i_ref)

        def fwd_chunk(ci, sts):
            for h in heads_of:
                st_ref[ci, h] = sts[h]
            heads, bs = _hg_load(refs, pl.multiple_of(ci * BLK, BLK), lb_v, lower_incl)
            b_ends = [b[BLK - 1:BLK, :] for b in bs]
            k_decs = [((1.0 - hd[2]) * jnp.exp(b_end - b)).astype(BF16) for hd, b, b_end in zip(heads, bs, b_ends)]
            grown = [_dot_tn(hd[5].astype(BF16), k_dec) for hd, k_dec in zip(heads, k_decs)]
            return tuple(st * jnp.exp(b_end) + g for st, b_end, g in zip(sts, b_ends, grown))

        zero_state = (jnp.zeros((HG_HEAD_DIM, HG_HEAD_DIM), F32),) * HG_GROUP
        lax.fori_loop(0, nc, fwd_chunk, zero_state)

        def bwd_chunk(cc, carry):
            dsts, suffixes, dlbs = carry
            ci = nc - 1 - cc
            r0 = pl.multiple_of(ci * BLK, BLK)
            heads, bs = _hg_load(refs, r0, lb_v, lower_incl)
            qs = [hd[0] for hd in heads]
            fs = [hd[2] for hd in heads]
            ks = [1.0 - f for f in fs]
            vs = [hd[5] for hd in heads]
            vbs = [v.astype(BF16) for v in vs]
            dos = [do_ref[pl.ds(r0, BLK), h * HG_HEAD_DIM:(h + 1) * HG_HEAD_DIM] for h in heads_of]
            dobs = [do.astype(BF16) for do in dos]
            b_ends = [b[BLK - 1:BLK, :] for b in bs]
            e_qs = [jnp.exp(b) for b in bs]
            e_ks = [jnp.exp(b_end - b) for b, b_end in zip(bs, b_ends)]
            qes = [(q * e).astype(BF16) for q, e in zip(qs, e_qs)]
            khs = [(k * e).astype(BF16) for k, e in zip(ks, e_ks)]
            st_terms = [_split2_lanes(st_ref[ci, h]) for h in heads_of]
            ds_terms = [_split2_lanes(dst) for dst in dsts]
            dqes = [_dot(dob, t[:, :HG_HEAD_DIM]) + _dot(dob, t[:, HG_HEAD_DIM:]) for dob, t in zip(dobs, st_terms)]
            dkhs = [_dot(vb, t[:, :HG_HEAD_DIM]) + _dot(vb, t[:, HG_HEAD_DIM:]) for vb, t in zip(vbs, ds_terms)]
            dvs = [_dot_nt(kh, t[:, :HG_HEAD_DIM]) for kh, t in zip(khs, ds_terms)]
            grown = [_dot_tn(dob, qe) for dob, qe in zip(dobs, qes)]
            das = [_dot_nt(dob, vb) for dob, vb in zip(dobs, vbs)]
            dqs = [e * dqe for e, dqe in zip(e_qs, dqes)]
            dks = [e * dkh for e, dkh in zip(e_ks, dkhs)]
            dlogs = [qe.astype(F32) * dqe - kh.astype(F32) * dkh for qe, dqe, kh, dkh in zip(qes, dqes, khs, dkhs)]
            scs = [None] * HG_GROUP
            for v_idx, m in enumerate(HG_LEVELS):
                es, qms, kms = _hg_level_terms(qs, ks, bs, m)
                msk = mask_ref[v_idx]
                terms = [_dot_nt(qm, km) for qm, km in zip(qms, kms)]
                pms = [(da * msk).astype(BF16) for da in das]
                dqms = [_dot(pm, km) for pm, km in zip(pms, kms)]
                dkms = [_dot_tn(pm, qm) for pm, qm in zip(pms, qms)]
                scs = [t * msk if sc is None else sc + t * msk for sc, t in zip(scs, terms)]
                dqs = [dq + dqm * e for dq, dqm, e in zip(dqs, dqms, es)]
                dks = [dk + dkm * e for dk, dkm, e in zip(dks, dkms, es)]
                dlogs = [dl + (qm.astype(F32) * dqm - km.astype(F32) * dkm)
                         for dl, qm, dqm, km, dkm in zip(dlogs, qms, dqms, kms, dkms)]
            intras = [_dot_tn(sc.astype(BF16), dob) for sc, dob in zip(scs, dobs)]
            dgs = [_dot_exact_l(upper_incl, dl) + sfx for dl, sfx in zip(dlogs, suffixes)]
            new_dlbs = []
            for h in heads_of:
                q, dq_fac, f, sig = heads[h][0], heads[h][1], heads[h][2], heads[h][3]
                a_diag = jnp.sum(dos[h] * vs[h], axis=-1, keepdims=True)
                s_diag = jnp.sum(q * ks[h], axis=-1, keepdims=True)
                dq = dqs[h] + a_diag * ks[h]
                dk = dks[h] + a_diag * q
                dv = dvs[h] + intras[h] + s_diag * dos[h]
                dfull = dgs[h] / f - dk
                sl = slice(h * HG_HEAD_DIM, (h + 1) * HG_HEAD_DIM)
                out_scr[0, pl.ds(r0, BLK), sl] = (dq * dq_fac).astype(BF16)
                out_scr[1, pl.ds(r0, BLK), sl] = (dfull * (1.0 - lb_v[:, sl]) * sig * (1.0 - sig)).astype(BF16)
                out_scr[2, pl.ds(r0, BLK), sl] = dv.astype(BF16)
                new_dlbs.append(dlbs[h] + jnp.sum(dfull * (1.0 - sig), axis=0, keepdims=True))
            new_dsts = tuple(dst * jnp.exp(b_end) + g for dst, b_end, g in zip(dsts, b_ends, grown))
            return new_dsts, tuple(dg[0:1, :] for dg in dgs), tuple(new_dlbs)

        zero_row = (jnp.zeros((1, HG_HEAD_DIM), F32),) * HG_GROUP
        _, _, dlbs = lax.fori_loop(0, nc, bwd_chunk, (zero_state, zero_row, zero_row))
        dlb_ref[...] = jnp.broadcast_to(jnp.concatenate(dlbs, axis=1), dlb_ref.shape)
        group = pl.program_id(0)
        copies = []
        for t in range(3):
            col = pl.multiple_of((base + t * n_groups + group) * gw, gw)
            copies.append(pltpu.make_async_copy(out_scr.at[t], dproj_ref.at[:, pl.ds(col, gw)], sems.at[t]))
            copies[-1].start()
        for cp in copies:
            cp.wait()

    col_spec = lambda off: pl.BlockSpec((s_len, gw), lambda h: (0, off + h))
    anyspec = pl.BlockSpec(memory_space=pl.ANY)
    return _pcall(
        body, name=name,
        out_shape=(jax.ShapeDtypeStruct(dproj.shape, dproj.dtype), jax.ShapeDtypeStruct((8, WIDTH), F32)),
        grid=(n_groups,),
        in_specs=[col_spec(base), col_spec(base + n_groups), col_spec(base + 2 * n_groups), col_spec(0),
                  pl.BlockSpec((1, gw), lambda h: (0, h)), anyspec],
        out_specs=(anyspec, pl.BlockSpec((8, gw), lambda h: (0, h))),
        scratch_shapes=[pltpu.VMEM((len(HG_LEVELS), BLK, BLK), F32),
                        pltpu.VMEM((nc, HG_GROUP, HG_HEAD_DIM, HG_HEAD_DIM), F32),
                        pltpu.VMEM((3, s_len, gw), BF16), pltpu.SemaphoreType.DMA((3,))],
        aliases={5: 0},
        semantics=("arbitrary",))(proj, proj, proj, do_b, lb, dproj)


def _dh_matmul(dproj, w_full, after, name):
    s_len, n = dproj.shape
    d = w_full.shape[0]
    tm = min(1024, s_len)
    tk = 1152

    def body(dp_ref, w_ref, after_ref, dh_ref):
        del after_ref
        part = _dot_nt(dp_ref[...], w_ref[...])

        @pl.when(pl.program_id(1) == 0)
        def _():
            dh_ref[...] = part

        @pl.when(pl.program_id(1) > 0)
        def _():
            dh_ref[...] += part

    return _pcall(
        body, name=name, out_shape=jax.ShapeDtypeStruct((s_len, d), F32),
        grid=(s_len // tm, n // tk),
        in_specs=[pl.BlockSpec((tm, tk), lambda i, k: (i, k)), pl.BlockSpec((d, tk), lambda i, k: (0, k)),
                  pl.BlockSpec(memory_space=pl.ANY)],
        out_specs=pl.BlockSpec((tm, d), lambda i, k: (i, 0)),
        semantics=("arbitrary", "arbitrary"))(dproj, w_full, after)


def _gw_matmul(h_t, dproj, name):
    d, s_len = h_t.shape
    n = dproj.shape[1]
    tn = 2304

    def body(ht_ref, dp_ref, gw_ref):
        gw_ref[...] = _dot(ht_ref[...], dp_ref[...]).astype(BF16)

    return _pcall(
        body, name=name, out_shape=jax.ShapeDtypeStruct((d, n), BF16),
        grid=(n // tn,),
        in_specs=[pl.BlockSpec((d, s_len), lambda j: (0, 0)), pl.BlockSpec((s_len, tn), lambda j: (0, j))],
        out_specs=pl.BlockSpec((d, tn), lambda j: (0, j)),
        semantics=("arbitrary",))(h_t, dproj)


def _ln_bwd(dh, x, scale, dres, name):
    s_len, d = x.shape
    tm = min(512, s_len)

    def body(dh_ref, x_ref, sc_ref, dres_ref, dx_ref, vec_ref):
        @pl.when(pl.program_id(0) == 0)
        def _():
            vec_ref[...] = jnp.zeros_like(vec_ref)

        dh = dh_ref[...]
        xs, rstd = _standardize(x_ref[...])
        vec_ref[0:1, :] += jnp.sum(dh, axis=0, keepdims=True)
        vec_ref[1:2, :] += jnp.sum(dh * xs, axis=0, keepdims=True)
        dx_ref[...] = _standardize_bwd(xs, rstd, dh * (1.0 + sc_ref[...])) + dres_ref[...]

    tile = pl.BlockSpec((tm, d), lambda i: (i, 0))
    return _pcall(body, name=name, grid=(s_len // tm,),
                  out_shape=(jax.ShapeDtypeStruct((s_len, d), F32), jax.ShapeDtypeStruct((8, d), F32)),
                  in_specs=[tile, tile, pl.BlockSpec((1, d), lambda i: (0, 0)), tile],
                  out_specs=(tile, pl.BlockSpec((8, d), lambda i: (0, 0))),
                  semantics=("arbitrary",))(dh, x, scale, dres)


def _wmod_grad(c_t, dmod):
    d = c_t.shape[0]
    n_layers, _, cm = dmod.shape

    def body(c_ref, dm_ref, o_ref):
        for l in range(n_layers):
            acc = None
            for b in range(NDEV):
                term = c_ref[:, b:b + 1] * dm_ref[l, b:b + 1, :]
                acc = term if acc is None else acc + term
            o_ref[l] = acc

    return _pcall(body, name="wmod_grad", out_shape=jax.ShapeDtypeStruct((n_layers, d, cm), F32))(c_t, dmod)


def _sum_adamw(parts_list, w, m, v, name):
    n_ranges = len(parts_list)
    n_src, range_rows, cols = parts_list[0].shape
    rows = range_rows * n_ranges
    tr = range_rows
    for cand in (512, 256, 128, 64, 32, 16, 8):
        if range_rows % cand == 0 and cand * cols * 4 <= (2 << 20):
            tr = cand
            break
    tiles = range_rows // tr

    def body(*refs):
        p_refs = refs[:n_ranges]
        w_ref, m_ref, v_ref, g_ref, d_ref, nm_ref, nv_ref = refs[n_ranges:]

        def step(p_ref):
            g = p_ref[0].astype(F32)
            for s in range(1, n_src):
                g = g + p_ref[s].astype(F32)
            nm = ADAM_B1 * m_ref[...] + (1.0 - ADAM_B1) * g
            nv = ADAM_B2 * v_ref[...] + (1.0 - ADAM_B2) * (g * g)
            m_hat = nm / (1.0 - ADAM_B1 ** ADAM_STEP)
            v_hat = nv / (1.0 - ADAM_B2 ** ADAM_STEP)
            g_ref[...] = g
            d_ref[...] = -ADAM_LR * (m_hat / (jnp.sqrt(v_hat) + ADAM_EPS) + ADAM_WD * w_ref[...])
            nm_ref[...] = nm
            nv_ref[...] = nv

        if n_ranges == 1:
            step(p_refs[0])
        else:
            for j in range(n_ranges):
                @pl.when(pl.program_id(0) // tiles == j)
                def _(j=j):
                    step(p_refs[j])

    def part_spec(j):
        return pl.BlockSpec((n_src, tr, cols), lambda i: (0, jnp.clip(i - j * tiles, 0, tiles - 1), 0))

    tile = pl.BlockSpec((tr, cols), lambda i: (i, 0))
    out = jax.ShapeDtypeStruct((rows, cols), F32)
    return _pcall(body, name=name, grid=(rows // tr,), out_shape=(out,) * 4,
                  in_specs=[part_spec(j) for j in range(n_ranges)] + [tile, tile, tile],
                  out_specs=(tile,) * 4, semantics=("arbitrary",))(*parts_list, w, m, v)


def _sum_parts(parts, name):
    n_src = parts.shape[0]

    def body(p_ref, o_ref):
        acc = p_ref[0]
        for s in range(1, n_src):
            acc = acc + p_ref[s]
        o_ref[...] = acc

    return _pcall(body, name=name, out_shape=jax.ShapeDtypeStruct(parts.shape[1:], F32))(parts)


def _pair_sum(gw, stage, me, name):
    d = gw.shape[0]
    n_slots, _, shard = stage.shape

    def body(me_ref, g_ref, s_ref, own_ref, o_ref):
        del me_ref
        total = (g_ref[...].astype(F32) + s_ref[0].astype(F32)).astype(BF16)
        o_ref[0] = total

        @pl.when(pl.program_id(0) == 0)
        def _():
            own_ref[0] = total

    slot = pl.BlockSpec((1, d, shard), lambda jj, me_ref: (jj, 0, 0))
    out = jax.ShapeDtypeStruct(stage.shape, BF16)
    return pl.pallas_call(
        body, name=name, out_shape=(out, out),
        grid_spec=pltpu.PrefetchScalarGridSpec(
            num_scalar_prefetch=1, grid=(n_slots,),
            in_specs=[pl.BlockSpec((d, shard), lambda jj, me_ref: (0, me_ref[0] ^ (2 * jj))), slot],
            out_specs=(pl.BlockSpec((1, d, shard), lambda jj, me_ref: (0, 0, 0)), slot)),
        compiler_params=pltpu.CompilerParams(dimension_semantics=("arbitrary",), vmem_limit_bytes=VMEM_LIMIT),
        interpret=False)(me.reshape(1).astype(jnp.int32), gw, stage)


def _lower_bound_table(lower_bounds):
    p = jax.nn.softmax(lower_bounds.astype(F32), axis=0)
    return jnp.cumsum(p, axis=0) - p[0:1]


def _pad_rows(v, width):
    n = v.shape[0]
    rows = -(-n // width)
    rows = -(-rows // 8) * 8
    return jnp.pad(v, (0, rows * width - n)).reshape(rows, width)


def kernel(x, c, w_mod, b_mod, w_in, conv_w, hgrn_norm_w, lower_bounds, w_branch, w_out, ln_g, ln_b, loss_target, m_w_mod, m_b_mod, m_w_in, m_conv_w, m_hgrn_norm_w, m_lower_bounds, m_w_branch, m_w_out, m_ln_g, m_ln_b, v_w_mod, v_b_mod, v_w_in, v_conv_w, v_hgrn_norm_w, v_lower_bounds, v_w_branch, v_w_out, v_ln_g, v_ln_b):
    n_layers = N_LAYERS
    s_len, d = x.shape[1], x.shape[2]
    n_cols = w_in.shape[2] * NDEV
    cw_cols = conv_w.shape[2]
    cm = w_mod.shape[2]
    me = _my_index()
    x0 = x[0]
    target = loss_target[0]

    small = _pad_rows(jnp.concatenate([c.reshape(-1), conv_w.reshape(-1)]), BLK)
    small_all = _all_gather_small("gather_c_conv", small).reshape(NDEV, -1)
    c_all = small_all[:, :d]
    conv_full = small_all[:, d:d + n_layers * 3 * cw_cols].reshape(NDEV, n_layers, 3, cw_cols)
    conv_full = conv_full.transpose(1, 2, 0, 3).reshape(n_layers, 3, WIDTH)

    b_mod_mine = lax.dynamic_slice_in_dim(b_mod, me * cm, cm, axis=1).reshape(n_layers, 1, cm)
    mod_cols = _mod_fwd(c_all, w_mod, b_mod_mine)
    mod_all = _all_gather_small("gather_mod", mod_cols.reshape(n_layers * NDEV, cm))
    mod_all = mod_all.reshape(NDEV, n_layers, NDEV, cm)
    mod_mine = lax.dynamic_index_in_dim(mod_all, me, axis=2, keepdims=False)
    mod_mine = mod_mine.transpose(1, 0, 2).reshape(n_layers, 3, 1, d)

    shard = w_in.shape[2]
    dsh = d // NDEV
    w_in_b, w_branch_b, w_out_b = w_in.astype(BF16), w_branch.astype(BF16), w_out.astype(BF16)
    window = lambda ref, dev: ref.at[:, pl.ds(pl.multiple_of(dev * shard, BLK), shard)]

    def two_step_sends(places):
        chips, sibling = [], []
        for k in (1, 2, 4, 6):
            for a, place in enumerate(places):
                chips.append((k, lambda ins, lands, me, a=a: ins[a],
                              lambda lands, me, a=a, place=place: place(lands[a], me),
                              lambda lands, me, a=a, k=k, place=place: place(lands[a], me ^ k)))
        for j in (2, 4, 6):
            for a, place in enumerate(places):
                sibling.append((1, lambda ins, lands, me, a=a, j=j, place=place: place(lands[a], me ^ j),
                                lambda lands, me, a=a, j=j, place=place: place(lands[a], me ^ j),
                                lambda lands, me, a=a, j=j, place=place: place(lands[a], me ^ 1 ^ j)))
        return chips, sibling

    in_sends = two_step_sends([window])
    rest_sends = two_step_sends([_slot, _slot])
    layer_sends = two_step_sends([window, _slot, _slot])

    def in_land(l):
        return _place_own_window(f"place_w_in_{l}", (d, n_cols), w_in_b[l], me)

    def rest_lands(l):
        return [_place_own((NDEV, 3, WIDTH, dsh), BF16, w_branch_b[l][None], (me, 0, 0, 0)),
                _place_own((NDEV, dsh, d), BF16, w_out_b[l][None], (me, 0, 0))]

    def gather_start(name, shards, lands, sends, after):
        return _exchange_start(f"{name}_chips_start", shards, lands, sends[0], after)

    def gather_pass_on(name, started, after, sends):
        _, lands = _exchange_wait(f"{name}_chips_wait", started, after, sends[0])
        return _exchange_start(f"{name}_sibling_start", [], lands, sends[1])

    def gather_finish(name, started, after, sends):
        return _exchange_wait(f"{name}_sibling_wait", started, after, sends[1])[1]

    def branch_out_weights(w_branch_l, w_out_l):
        return w_branch_l.transpose(1, 2, 0, 3).reshape(3, WIDTH, d), w_out_l.reshape(d, d)

    gathering = gather_start("gather_w_in_0", [w_in_b[0]], [in_land(0)], in_sends, mod_mine)
    passing = gather_pass_on("gather_w_in_0", gathering, gathering[4], in_sends)
    rest_gathering = gather_start("gather_rest_0", [w_branch_b[0], w_out_b[0]], rest_lands(0), rest_sends, passing[4])
    next_gathering = None
    if n_layers > 1:
        next_gathering = gather_start("gather_weights_1", [w_in_b[1], w_branch_b[1], w_out_b[1]],
                                      [in_land(1)] + rest_lands(1), layer_sends, rest_gathering[4])
    w_in_l = gather_finish("gather_w_in_0", passing, (next_gathering or rest_gathering)[4], in_sends)[0]

    lbs = _lower_bound_table(lower_bounds)
    norm_w4 = jnp.tile(hgrn_norm_w, (1, WIDTH // HG_HEAD_DIM))

    saved = []
    xl = x0
    for l in range(n_layers):
        shift, scale, gate = mod_mine[l, 0], mod_mine[l, 1], mod_mine[l, 2]
        proj, h_t = _ln_proj(xl, shift, scale, w_in_l, f"ln_proj_{l}")
        o_a, totals = _sb_fwd(proj, f"sb_fwd_{l}")
        if l == 0:
            rest_passing = gather_pass_on("gather_rest_0", rest_gathering, o_a, rest_sends)
        lb_l = lbs[l:l + 1] + rest_passing[4][0, 0] if l == 0 else lbs[l:l + 1]
        o_b = _hgrn_fwd(proj, lb_l, f"hgrn_fwd_{l}")
        if l == 0:
            wb_l, wo_l = branch_out_weights(*gather_finish("gather_rest_0", rest_passing, o_b, rest_sends))
            if n_layers > 1:
                next_passing = gather_pass_on("gather_weights_1", next_gathering, o_b, layer_sends)
                gate = gate + next_passing[4][0, 0]
        x_new, merged, ycat = _merge_fwd(xl, proj, o_a, o_b, gate, norm_w4[l:l + 1], conv_full[l],
                                         wb_l, wo_l, ln_g[l:l + 1], ln_b[l:l + 1], f"merge_fwd_{l}")
        saved.append((xl, proj, h_t, o_a, totals, o_b, merged, ycat, w_in_l, wb_l, wo_l))
        if l == 0 and n_layers > 1:
            w_in_l, w_branch_l, w_out_l = gather_finish("gather_weights_1", next_passing, x_new, layer_sends)
            wb_l, wo_l = branch_out_weights(w_branch_l, w_out_l)
        xl = x_new

    loss_part, dx = _loss_fwd_bwd(xl, target)
    loss = lax.psum(loss_part[0, 0], ("x", "y", "c"))

    pair_sends = [(1, lambda ins, lands, me, j=j: window(ins[0], me ^ 1 ^ j),
                   lambda lands, me, jj=jj: lands[0].at[jj], lambda lands, me, jj=jj: lands[0].at[jj])
                  for jj, j in enumerate((0, 2, 4, 6))]
    chip_sum_sends = [(j, lambda ins, lands, me, jj=jj: ins[0].at[jj],
                       lambda lands, me, jj=jj: lands[0].at[jj], lambda lands, me, jj=jj: lands[0].at[jj])
                      for jj, j in ((1, 2), (2, 4), (3, 6))]
    rest_scatter = _direct_sends([(0, 0, _slot, _slot), (1, 1, _slot, _slot)])
    scattering = [None] * n_layers
    small_grads = [None] * n_layers
    dmod = [None] * n_layers
    tie = None
    for l in reversed(range(n_layers)):
        xl, proj, h_t, o_a, totals, o_b, merged, ycat, w_in_l, wb_l, wo_l = saved[l]
        scale, gate = mod_mine[l, 1], mod_mine[l, 2]
        if tie is not None:
            gate = gate + tie[0, 0]
        dres, dycat, dproj, gwo_by_owner, gwb_by_owner, mvec = _merge_bwd(
            dx, xl, merged, ycat, proj, gate, wb_l, wo_l, ln_g[l:l + 1], f"merge_bwd_{l}")
        lands = [_place_own((NDEV, 3, WIDTH, dsh), BF16, lax.dynamic_slice_in_dim(gwb_by_owner, me, 1, axis=0),
                            (me, 0, 0, 0)),
                 _place_own((NDEV, dsh, d), BF16, lax.dynamic_slice_in_dim(gwo_by_owner, me, 1, axis=0),
                            (me, 0, 0))]
        rest_started = _exchange_start(f"scatter_rest_{l}_start", [gwb_by_owner, gwo_by_owner], lands, rest_scatter)
        dproj, do_a, do_b, bvec = _branch_bwd(dycat, proj, o_a, o_b, norm_w4[l:l + 1] + rest_started[4][0, 0],
                                              conv_full[l], dproj, f"branch_bwd_{l}")
        dproj = _sb_bwd(proj, do_a, totals, dproj, f"sb_bwd_{l}")
        dproj, dlb = _hgrn_bwd(proj, do_b, lbs[l:l + 1], dproj, f"hgrn_bwd_{l}")
        gwi = _gw_matmul(h_t, dproj, f"gw_matmul_{l}")
        swapping = _exchange_start(f"scatter_in_{l}_sibling_start", [gwi], [lax.empty((4, d, shard), BF16)], pair_sends)
        if l > 0:
            dh = _dh_matmul(dproj, w_in_l, swapping[4], f"dh_matmul_{l}")
        (gwi,), (stage,) = _exchange_wait(f"scatter_in_{l}_sibling_wait", swapping, dh if l > 0 else swapping[4],
                                          pair_sends)
        land, chip_sums = _pair_sum(gwi, stage, me, f"pair_sum_{l}")
        in_started = _exchange_start(f"scatter_in_{l}_chips_start", [chip_sums], [land], chip_sum_sends)
        scattering[l] = (in_started, rest_started)
        tie = in_started[4]
        if l == 0:
            dh = _dh_matmul(dproj, w_in_l, tie, f"dh_matmul_{l}")
        dx, lvec = _ln_bwd(dh, xl, scale + tie[0, 0], dres, f"ln_bwd_{l}")
        dmod[l] = jnp.concatenate([lvec[0], lvec[1], mvec[2]])
        norm_grad = bvec[0].reshape(WIDTH // HG_HEAD_DIM, HG_HEAD_DIM).sum(axis=0)
        small_grads[l] = jnp.concatenate([mvec[0], mvec[1], norm_grad, dlb[0], bvec[1:4].reshape(-1)])
    grad_x = dx[None]

    small_vec = jnp.concatenate(dmod + small_grads)
    n_small = small_vec.shape[0]
    small_all = _all_gather_small("gather_small_grads", _pad_rows(small_vec, BLK))
    small_sum = _sum_parts(small_all, "sum_small_grads").reshape(-1)[:n_small]
    dmod_all = small_all.reshape(NDEV, -1)[:, :n_layers * 3 * d].reshape(NDEV, n_layers, 3 * d)

    off = n_layers * 3 * d
    grad_b_mod = small_sum[:off].reshape(n_layers, 3 * d)
    per_layer = 2 * d + HG_HEAD_DIM + WIDTH + 3 * WIDTH
    g_ln_g, g_ln_b, g_norm, g_lbs, g_conv = [], [], [], [], []
    for l in range(n_layers):
        seg = small_sum[off + l * per_layer: off + (l + 1) * per_layer]
        g_ln_g.append(seg[:d])
        g_ln_b.append(seg[d:2 * d])
        g_norm.append(seg[2 * d:2 * d + HG_HEAD_DIM])
        g_lbs.append(seg[2 * d + HG_HEAD_DIM:2 * d + HG_HEAD_DIM + WIDTH])
        g_conv.append(seg[2 * d + HG_HEAD_DIM + WIDTH:].reshape(3, WIDTH))
    grad_ln_g, grad_ln_b = jnp.stack(g_ln_g), jnp.stack(g_ln_b)
    grad_norm = jnp.stack(g_norm)
    _, lbs_vjp = jax.vjp(_lower_bound_table, lower_bounds)
    grad_lower = lbs_vjp(jnp.stack(g_lbs))[0]
    grad_conv = lax.dynamic_slice_in_dim(jnp.stack(g_conv), me * cw_cols, cw_cols, axis=2)

    dmod_mine = lax.dynamic_slice_in_dim(dmod_all, me * cm, cm, axis=2).transpose(1, 0, 2)
    grad_w_mod = _wmod_grad(c_all.T, dmod_mine)

    p_in, p_branch, p_out = [None] * n_layers, [None] * n_layers, [None] * n_layers
    for l in reversed(range(n_layers)):
        in_started, rest_started = scattering[l]
        p_branch_l, p_out[l] = _exchange_wait(f"scatter_rest_{l}_wait", rest_started, grad_w_mod, rest_scatter)[1]
        p_branch[l] = p_branch_l.reshape(NDEV, 3 * WIDTH, dsh)
        p_in[l] = _exchange_wait(f"scatter_in_{l}_chips_wait", in_started, grad_w_mod, chip_sum_sends)[1][0]

    def adam(parts_list, w, m, v, name):
        shape = w.shape
        cols = shape[-1]
        flat = lambda a: a.reshape(-1, cols)
        outs = _sum_adamw(parts_list, flat(w), flat(m), flat(v), name)
        return [o.reshape(shape) for o in outs]

    r_w_in = adam(p_in, w_in, m_w_in, v_w_in, "adamw_w_in")
    r_w_branch = adam(p_branch, w_branch, m_w_branch, v_w_branch, "adamw_w_branch")
    r_w_out = adam(p_out, w_out, m_w_out, v_w_out, "adamw_w_out")
    r_w_mod = adam([grad_w_mod.reshape(1, -1, cm)], w_mod, m_w_mod, v_w_mod, "adamw_w_mod")

    small_names = ["b_mod", "conv_w", "hgrn_norm_w", "lower_bounds", "ln_g", "ln_b"]
    small_g = [grad_b_mod, grad_conv, grad_norm, grad_lower, grad_ln_g, grad_ln_b]
    small_w = [b_mod, conv_w, hgrn_norm_w, lower_bounds, ln_g, ln_b]
    small_m = [m_b_mod, m_conv_w, m_hgrn_norm_w, m_lower_bounds, m_ln_g, m_ln_b]
    small_v = [v_b_mod, v_conv_w, v_hgrn_norm_w, v_lower_bounds, v_ln_g, v_ln_b]
    pack = lambda arrs: _pad_rows(jnp.concatenate([a.reshape(-1) for a in arrs]), BLK)
    packed = _sum_adamw([pack(small_g)[None]], pack(small_w), pack(small_m), pack(small_v), "adamw_small")
    r_small = {n: [] for n in small_names}
    for res in packed:
        flat = res.reshape(-1)
        pos = 0
        for n, w in zip(small_names, small_w):
            r_small[n].append(flat[pos:pos + w.size].reshape(w.shape))
            pos += w.size

    results = {"w_mod": r_w_mod, "w_in": r_w_in, "w_branch": r_w_branch, "w_out": r_w_out, **r_small}
    order = ["w_mod", "b_mod", "w_in", "conv_w", "hgrn_norm_w", "lower_bounds", "w_branch", "w_out", "ln_g", "ln_b"]
    outs = [loss, grad_x]
    for idx in range(4):
        outs.extend(results[n][idx] for n in order)
    return tuple(outs)
```

```python
import jax
import jax.numpy as jnp
from jax import lax
from jax.experimental import pallas as pl
from jax.experimental.pallas import tpu as pltpu

F32 = jnp.float32
BF16 = jnp.bfloat16
NDEV = 8
N_LAYERS = 2
SB_HEAD_DIM = 64
HG_HEAD_DIM = 128
WIDTH = 512
BLK = 128
LN_EPS = 1e-5
RMS_EPS = 1e-6
ALPHA = (2.0 * N_LAYERS) ** 0.25
ADAM_LR, ADAM_B1, ADAM_B2, ADAM_EPS, ADAM_WD, ADAM_STEP = 0.001, 0.9, 0.999, 1e-08, 0.01, 10
VMEM_LIMIT = 56 * 1024 * 1024
MESH = pl.DeviceIdType.MESH
HG_LEVELS = (64, 32, 16, 8, 4, 2, 1)


def _pcall(body, *, name, out_shape, grid=None, in_specs=None, out_specs=None, scratch_shapes=(),
           semantics=None, aliases=None):
    kwargs = {}
    if grid is not None:
        kwargs["grid"] = grid
    if in_specs is not None:
        kwargs["in_specs"] = in_specs
    if out_specs is not None:
        kwargs["out_specs"] = out_specs
    if aliases:
        kwargs["input_output_aliases"] = aliases
    return pl.pallas_call(
        body, name=name, out_shape=out_shape, scratch_shapes=list(scratch_shapes),
        compiler_params=pltpu.CompilerParams(dimension_semantics=semantics, vmem_limit_bytes=VMEM_LIMIT),
        interpret=False, **kwargs)


def _dot(a, b):
    return jnp.dot(a, b, preferred_element_type=F32)


def _dot_nt(a, b):
    return lax.dot_general(a, b, (((1,), (1,)), ((), ())), preferred_element_type=F32)


def _dot_tn(a, b):
    return lax.dot_general(a, b, (((0,), (0,)), ((), ())), preferred_element_type=F32)


def _split3(x):
    x1 = x.astype(BF16)
    r1 = x - x1.astype(F32)
    x2 = r1.astype(BF16)
    r2 = r1 - x2.astype(F32)
    return x1, x2, r2.astype(BF16)


def _split2(x):
    x1 = x.astype(BF16)
    return x1, (x - x1.astype(F32)).astype(BF16)


def _dot_exact_l(m_bf16, x):
    x1, x2, x3 = _split3(x)
    return _dot(m_bf16, x1) + _dot(m_bf16, x2) + _dot(m_bf16, x3)


def _sigmoid(x):
    return 1.0 / (1.0 + jnp.exp(-x))


def _silu_and_grad(x):
    s = _sigmoid(x)
    return x * s, s * (1.0 + x * (1.0 - s))


LOG2E = 1.4426950408889634
MASKED_SCORE = -1e30


def _softplus2_parts(z2):
    minus_abs = lax.bitcast_convert_type(lax.bitcast_convert_type(z2, jnp.int32) | jnp.int32(-2 ** 31), F32)
    e = jnp.exp2(minus_abs)
    sp2 = jnp.maximum(z2, 0.0) + jnp.log2(1.0 + e)
    r = 1.0 / (1.0 + e)
    return sp2, jnp.where(z2 >= 0.0, r, e * r)


def _split2_lanes(x):
    x1 = x.astype(BF16)
    return jnp.concatenate([x1, (x - x1.astype(F32)).astype(BF16)], axis=1)


def _iota2(shape, dim):
    return lax.broadcasted_iota(jnp.int32, shape, dim)


def _standardize(x):
    mu = jnp.mean(x, axis=-1, keepdims=True)
    xc = x - mu
    var = jnp.mean(xc * xc, axis=-1, keepdims=True)
    rstd = lax.rsqrt(var + LN_EPS)
    return xc * rstd, rstd


def _standardize_bwd(xhat, rstd, dxhat):
    m1 = jnp.mean(dxhat, axis=-1, keepdims=True)
    m2 = jnp.mean(dxhat * xhat, axis=-1, keepdims=True)
    return rstd * (dxhat - m1 - xhat * m2)


def _my_index():
    return 4 * lax.axis_index("x") + 2 * lax.axis_index("y") + lax.axis_index("c")


def _exchange(name, ins, out_shapes, transfers, in_vmem):
    n_in, n_out, n_t = len(ins), len(out_shapes), len(transfers)

    def body(*refs):
        in_refs, out_refs = refs[:n_in], refs[n_in:n_in + n_out]
        send_sems, recv_sems, local_sems = refs[n_in + n_out:]
        x, y, c = lax.axis_index("x"), lax.axis_index("y"), lax.axis_index("c")
        me = 4 * x + 2 * y + c
        started = []
        for t, (i, o, src_fn, dst_fn) in enumerate(transfers):
            own = pltpu.make_async_copy(src_fn(in_refs[i], me), dst_fn(out_refs[o], me), local_sems.at[t])
            own.start()
            started.append(own)
        arrivals = []
        for k in range(1, NDEV):
            px = x ^ ((k >> 2) & 1)
            py = y ^ ((k >> 1) & 1)
            pc = c ^ (k & 1)
            peer = 4 * px + 2 * py + pc
            for t, (i, o, src_fn, dst_fn) in enumerate(transfers):
                sem = t * (NDEV - 1) + k - 1
                push = pltpu.make_async_remote_copy(
                    src_ref=src_fn(in_refs[i], peer), dst_ref=dst_fn(out_refs[o], me),
                    send_sem=send_sems.at[sem], recv_sem=recv_sems.at[sem],
                    device_id=(px, py, pc), device_id_type=MESH)
                push.start()
                started.append(push)
                arrivals.append(pltpu.make_async_remote_copy(
                    src_ref=src_fn(in_refs[i], peer), dst_ref=dst_fn(out_refs[o], peer),
                    send_sem=send_sems.at[sem], recv_sem=recv_sems.at[sem],
                    device_id=(px, py, pc), device_id_type=MESH))
        for arrival in arrivals:
            arrival.wait_recv()
        for cp in started[n_t:]:
            cp.wait_send()
        for own in started[:n_t]:
            own.wait()

    space = pltpu.VMEM if in_vmem else pl.ANY
    spec = pl.BlockSpec(memory_space=space)
    return _pcall(
        body, name=name, out_shape=out_shapes,
        in_specs=[spec] * n_in, out_specs=[spec] * n_out,
        scratch_shapes=[pltpu.SemaphoreType.DMA((n_t * (NDEV - 1),)),
                        pltpu.SemaphoreType.DMA((n_t * (NDEV - 1),)),
                        pltpu.SemaphoreType.DMA((n_t,))])(*ins)


def _whole(ref, dev):
    return ref


def _slot(ref, dev):
    return ref.at[dev]


def _all_gather_small(name, v):
    out = _exchange(name, [v], [jax.ShapeDtypeStruct((NDEV,) + v.shape, v.dtype)],
                    [(0, 0, _whole, _slot)], in_vmem=True)
    return out[0]


_HBM_SPEC = pl.BlockSpec(memory_space=pltpu.HBM)
_SEM_SPEC = pl.BlockSpec(memory_space=pltpu.SEMAPHORE)
_DATAFLOW = pltpu.SideEffectType.DATAFLOW_SIDE_EFFECTING


def _peer(x, y, c, k):
    px = x ^ ((k >> 2) & 1)
    py = y ^ ((k >> 1) & 1)
    pc = c ^ (k & 1)
    return (px, py, pc), 4 * px + 2 * py + pc


def _direct_sends(transfers):
    sends = []
    for k in range(1, NDEV):
        for i, o, src_fn, dst_fn in transfers:
            sends.append((k,
                          lambda ins, lands, me, i=i, k=k, src_fn=src_fn: src_fn(ins[i], me ^ k),
                          lambda lands, me, o=o, dst_fn=dst_fn: dst_fn(lands[o], me),
                          lambda lands, me, o=o, k=k, dst_fn=dst_fn: dst_fn(lands[o], me ^ k)))
    return sends


def _exchange_start(name, ins, lands, sends, after=None):
    n_in, n_buf = len(ins), len(ins) + len(lands)
    n_sem = len(sends)

    def body(*refs):
        in_refs, land_refs = refs[:n_in], refs[n_in:n_buf]
        n_skip = n_buf + (0 if after is None else 1)
        send_sems, recv_sems, token = refs[n_skip], refs[n_skip + 1], refs[-1]
        x, y, c = lax.axis_index("x"), lax.axis_index("y"), lax.axis_index("c")
        me = 4 * x + 2 * y + c
        for t, (k, src_fn, dst_fn, _) in enumerate(sends):
            pltpu.make_async_remote_copy(
                src_ref=src_fn(in_refs, land_refs, me), dst_ref=dst_fn(land_refs, me),
                send_sem=send_sems.at[t], recv_sem=recv_sems.at[t],
                device_id=_peer(x, y, c, k)[0], device_id_type=MESH).start()
        token[...] = jnp.zeros_like(token)

    bufs = [pltpu.with_memory_space_constraint(a, pltpu.HBM) for a in list(ins) + list(lands)]
    extra = [] if after is None else [after]
    outs = pl.pallas_call(
        body, name=name,
        out_shape=(pltpu.SemaphoreType.DMA((n_sem,)), pltpu.SemaphoreType.DMA((n_sem,)))
        + tuple(pltpu.HBM(a.shape, a.dtype) for a in bufs) + (jax.ShapeDtypeStruct((8, BLK), F32),),
        in_specs=[_HBM_SPEC] * n_buf + [pl.BlockSpec(memory_space=pl.ANY)] * len(extra),
        out_specs=(_SEM_SPEC, _SEM_SPEC) + (_HBM_SPEC,) * n_buf + (pl.BlockSpec(memory_space=pltpu.VMEM),),
        input_output_aliases={b: 2 + b for b in range(n_buf)},
        compiler_params=pltpu.CompilerParams(has_side_effects=_DATAFLOW),
        interpret=False)(*bufs, *extra)
    return outs[0], outs[1], list(outs[2:2 + n_in]), list(outs[2 + n_in:2 + n_buf]), outs[-1]


def _exchange_wait(name, started, after, sends):
    send_sems, recv_sems, ins, lands, _ = started
    n_in, n_buf = len(ins), len(ins) + len(lands)

    def body(*refs):
        in_refs, land_refs = refs[:n_in], refs[n_in:n_buf]
        send_sems, recv_sems = refs[n_buf], refs[n_buf + 1]
        x, y, c = lax.axis_index("x"), lax.axis_index("y"), lax.axis_index("c")
        me = 4 * x + 2 * y + c
        for t, (k, src_fn, _, rcv_fn) in enumerate(sends):
            cp = pltpu.make_async_remote_copy(
                src_ref=src_fn(in_refs, land_refs, me), dst_ref=rcv_fn(land_refs, me),
                send_sem=send_sems.at[t], recv_sem=recv_sems.at[t],
                device_id=_peer(x, y, c, k)[0], device_id_type=MESH)
            cp.wait_send()
            cp.wait_recv()

    bufs = list(ins) + list(lands)
    outs = pl.pallas_call(
        body, name=name, out_shape=tuple(pltpu.HBM(a.shape, a.dtype) for a in bufs),
        in_specs=[_HBM_SPEC] * n_buf + [_SEM_SPEC, _SEM_SPEC, pl.BlockSpec(memory_space=pl.ANY)],
        out_specs=(_HBM_SPEC,) * n_buf,
        input_output_aliases={b: b for b in range(n_buf)},
        compiler_params=pltpu.CompilerParams(has_side_effects=_DATAFLOW),
        interpret=False)(*bufs, send_sems, recv_sems, after)
    return list(outs[:n_in]), list(outs[n_in:])


def _place_own(shape, dtype, own, start):
    return lax.dynamic_update_slice(lax.empty(shape, dtype), own, start)


def _place_own_window(name, shape, own, me):
    rows, cols = own.shape

    def body(me_ref, zone_in, own_ref, zone_ref):
        del me_ref, zone_in
        zone_ref[...] = own_ref[...]

    return pl.pallas_call(
        body, name=name, out_shape=jax.ShapeDtypeStruct(shape, own.dtype),
        grid_spec=pltpu.PrefetchScalarGridSpec(
            num_scalar_prefetch=1, grid=(1,),
            in_specs=[pl.BlockSpec(memory_space=pl.ANY), pl.BlockSpec((rows, cols), lambda i, me_ref: (0, 0))],
            out_specs=pl.BlockSpec((rows, cols), lambda i, me_ref: (0, me_ref[0]))),
        input_output_aliases={1: 0},
        compiler_params=pltpu.CompilerParams(dimension_semantics=("arbitrary",), vmem_limit_bytes=VMEM_LIMIT),
        interpret=False)(me.reshape(1).astype(jnp.int32), lax.empty(shape, own.dtype), own)


def _mod_fwd(c_all, w_mod, b_mod_mine):
    n_layers, _, cm = w_mod.shape

    def body(c_ref, w_ref, b_ref, o_ref):
        for l in range(n_layers):
            o_ref[l] = jnp.dot(c_ref[...], w_ref[l], preferred_element_type=F32,
                               precision=lax.Precision.HIGHEST) + b_ref[l]

    return _pcall(body, name="mod_fwd", out_shape=jax.ShapeDtypeStruct((n_layers, NDEV, cm), F32))(
        c_all, w_mod, b_mod_mine)


def _ln_proj(x, shift, scale, w_full, name):
    s_len, d = x.shape
    n = w_full.shape[1]
    tm = min(1024, s_len)
    tn = 1152

    def body(x_ref, sh_ref, sc_ref, w_ref, proj_ref, ht_ref, h_scr):
        @pl.when(pl.program_id(1) == 0)
        def _():
            xs, _ = _standardize(x_ref[...])
            h = xs * (1.0 + sc_ref[...]) + sh_ref[...]
            h_scr[...] = h.astype(BF16)
            ht_ref[...] = h.T.astype(BF16)

        proj_ref[...] = _dot(h_scr[...], w_ref[...])

    return _pcall(
        body, name=name,
        out_shape=(jax.ShapeDtypeStruct((s_len, n), F32), jax.ShapeDtypeStruct((d, s_len), BF16)),
        grid=(s_len // tm, n // tn),
        in_specs=[pl.BlockSpec((tm, d), lambda i, j: (i, 0)),
                  pl.BlockSpec((1, d), lambda i, j: (0, 0)),
                  pl.BlockSpec((1, d), lambda i, j: (0, 0)),
                  pl.BlockSpec((d, tn), lambda i, j: (0, j))],
        out_specs=(pl.BlockSpec((tm, tn), lambda i, j: (i, j)),
                   pl.BlockSpec((d, tm), lambda i, j: (0, i))),
        scratch_shapes=[pltpu.VMEM((tm, d), BF16)],
        semantics=("arbitrary", "arbitrary"))(x, shift, scale, w_full)


SB_Q_ROWS = 256
SB_K_BLOCKS = 2


def _sb_fwd(proj, name):
    s_len = proj.shape[0]
    n_pairs = WIDTH // BLK
    qr = min(SB_Q_ROWS, s_len)
    gb = SB_K_BLOCKS
    kw = gb * BLK
    nq = s_len // qr
    assert qr == kw

    def body(q_ref, k_ref, v_ref, o_ref, tot_ref):
        lane = _iota2((1, BLK), 1)
        row = _iota2((BLK, BLK), 0)
        col = _iota2((BLK, BLK), 1)
        half = jnp.concatenate([(row >= col).astype(BF16), jnp.ones((BLK, BLK), BF16)], axis=1)
        suffix_and_sum = jnp.concatenate([half, half], axis=0)
        strict = _iota2((qr, kw), 1) < _iota2((qr, kw), 0)
        head_lanes = [(lane // SB_HEAD_DIM) == hh for hh in range(2)]

        def scores(gi, qms, masked):
            c0 = pl.multiple_of(gi * kw, kw)
            kb = k_ref[pl.ds(c0, kw), :].astype(BF16)
            z2s = [_dot_nt(qms[hh], kb) for hh in range(2)]
            if masked:
                z2s = [jnp.where(strict, z2, MASKED_SCORE) for z2 in z2s]
            return tuple(z2s)

        def accumulate(gi, z2s, carry):
            c0 = pl.multiple_of(gi * kw, kw)
            vf = v_ref[pl.ds(c0, kw), :]
            sp2s = [_softplus2_parts(z2)[0] for z2 in z2s]
            terms = [[_split2_lanes(sp2[:, b * BLK:(b + 1) * BLK]) for b in range(gb)] for sp2 in sp2s]
            sums = [[_dot(t, suffix_and_sum) for t in head_terms] for head_terms in terms]
            weights, laters = [], []
            for hh in range(2):
                later = carry[2 * hh + 1]
                parts = [None] * gb
                for b in reversed(range(gb)):
                    parts[b] = sums[hh][b][:, :BLK] + later
                    later = later + sums[hh][b][:, BLK:]
                weights.append(jnp.exp2(z2s[hh] - jnp.concatenate(parts, axis=1)).astype(BF16))
                laters.append(later)
            outs = [_dot(weights[hh], jnp.where(head_lanes[hh], vf, 0.0).astype(BF16)) for hh in range(2)]
            return (carry[0] + outs[0], laters[0], carry[2] + outs[1], laters[1])

        def queries(i):
            qf = q_ref[pl.ds(pl.multiple_of(i * qr, qr), qr), :] * (SB_HEAD_DIM ** -0.5 * LOG2E)
            return [jnp.where(head_lanes[hh], qf, 0.0).astype(BF16) for hh in range(2)]

        def qtile(i, first_scores):
            r0 = pl.multiple_of(i * qr, qr)
            qms = queries(i)
            zero = jnp.zeros((qr, BLK), F32)

            def step(jj, state):
                gi = i - 1 - jj
                return scores(gi, qms, False) + accumulate(gi + 1, state[:2], state[2:])

            state = lax.fori_loop(0, i, step, first_scores + (zero,) * 4)
            nxt = jnp.minimum(i + 1, nq - 1)
            next_scores = scores(nxt, queries(nxt), True)
            carry = accumulate(0, state[:2], state[2:])
            o_ref[pl.ds(r0, qr), :] = carry[0] + carry[2]
            tot_ref[0, pl.ds(r0, qr), :] = carry[1]
            tot_ref[1, pl.ds(r0, qr), :] = carry[3]
            return next_scores

        lax.fori_loop(0, nq, qtile, scores(0, queries(0), True))

    col_spec = lambda off: pl.BlockSpec((s_len, BLK), lambda p: (0, off + p))
    return _pcall(
        body, name=name,
        out_shape=(jax.ShapeDtypeStruct((s_len, WIDTH), F32),
                   jax.ShapeDtypeStruct((2 * n_pairs, s_len, BLK), F32)),
        grid=(n_pairs,),
        in_specs=[col_spec(0), col_spec(n_pairs), col_spec(2 * n_pairs)],
        out_specs=(pl.BlockSpec((s_len, BLK), lambda p: (0, p)),
                   pl.BlockSpec((2, s_len, BLK), lambda p: (p, 0, 0))),
        semantics=("arbitrary",))(proj, proj, proj)


def _hg_masks(mask_ref):
    row = _iota2((BLK, BLK), 0)
    col = _iota2((BLK, BLK), 1)
    for v, m in enumerate(HG_LEVELS):
        same = (row // (2 * m)) == (col // (2 * m))
        mask_ref[v] = (same & ((row & m) != 0) & ((col & m) == 0)).astype(F32)


def _hg_mid(b, m):
    if m >= 4:
        n = BLK // (2 * m)
        mid = b.reshape(n, 2 * m, BLK)[:, m - 1:m, :]
        return jnp.broadcast_to(mid, (n, 2 * m, BLK)).reshape(BLK, BLK)
    pos = _iota2((BLK, BLK), 0) & (2 * m - 1)
    out = b
    for p in range(2 * m):
        delta = (m - 1) - p
        if delta != 0:
            out = jnp.where(pos == p, pltpu.roll(b, (-delta) % BLK, 0), out)
    return out


def _hg_chunk_inputs(qraw, fpre, lb):
    sig = _sigmoid(fpre)
    f = lb + (1.0 - lb) * sig
    g = jnp.log(f)
    q, dq_fac = _silu_and_grad(qraw)
    return q, dq_fac, f, sig, g


HG_GROUP = 4


def _neg_abs(x):
    return lax.bitcast_convert_type(lax.bitcast_convert_type(x, jnp.int32) | jnp.int32(-2 ** 31), F32)


def _hg_level_terms(qs, ks, bs, m):
    es = [jnp.exp(_neg_abs(b - _hg_mid(b, m))) for b in bs]
    qts = [(q * e).astype(BF16) for q, e in zip(qs, es)]
    kts = [(k * e).astype(BF16) for k, e in zip(ks, es)]
    return es, qts, kts


def _hg_load(refs, r0, lb_v, lower_incl):
    q_ref, f_ref, i_ref = refs
    heads = []
    for h in range(HG_GROUP):
        sl = slice(h * HG_HEAD_DIM, (h + 1) * HG_HEAD_DIM)
        heads.append(_hg_chunk_inputs(q_ref[pl.ds(r0, BLK), sl], f_ref[pl.ds(r0, BLK), sl], lb_v[:, sl])
                     + (i_ref[pl.ds(r0, BLK), sl],))
    bs = [_dot_exact_l(lower_incl, hd[4]) for hd in heads]
    return heads, bs


def _hgrn_fwd(proj, lb, name):
    s_len = proj.shape[0]
    nc = s_len // BLK
    gw = HG_GROUP * HG_HEAD_DIM
    n_groups = WIDTH // gw
    base = 4 * WIDTH // gw

    def body(q_ref, f_ref, i_ref, lb_ref, o_ref, mask_ref):
        _hg_masks(mask_ref)
        row = _iota2((BLK, BLK), 0)
        col = _iota2((BLK, BLK), 1)
        lower_incl = (col <= row).astype(BF16)
        lb_v = lb_ref[...]

        def chunk(ci, sts):
            r0 = pl.multiple_of(ci * BLK, BLK)
            heads, bs = _hg_load((q_ref, f_ref, i_ref), r0, lb_v, lower_incl)
            qs = [hd[0] for hd in heads]
            ks = [1.0 - hd[2] for hd in heads]
            vs = [hd[5] for hd in heads]
            vbs = [v.astype(BF16) for v in vs]
            b_ends = [b[BLK - 1:BLK, :] for b in bs]
            inters = [_dot_nt((q * jnp.exp(b)).astype(BF16), st.astype(BF16)) for q, b, st in zip(qs, bs, sts)]
            scs = [None] * HG_GROUP
            for v_idx, m in enumerate(HG_LEVELS):
                _, qts, kts = _hg_level_terms(qs, ks, bs, m)
                terms = [_dot_nt(qt, kt) for qt, kt in zip(qts, kts)]
                msk = mask_ref[v_idx]
                scs = [t * msk if sc is None else sc + t * msk for sc, t in zip(scs, terms)]
            intras = [_dot(sc.astype(BF16), vb) for sc, vb in zip(scs, vbs)]
            k_decs = [(k * jnp.exp(b_end - b)).astype(BF16) for k, b, b_end in zip(ks, bs, b_ends)]
            grown = [_dot_tn(vb, k_dec) for vb, k_dec in zip(vbs, k_decs)]
            for h in range(HG_GROUP):
                diag = jnp.sum(qs[h] * ks[h], axis=-1, keepdims=True)
                o_ref[pl.ds(r0, BLK), h * HG_HEAD_DIM:(h + 1) * HG_HEAD_DIM] = inters[h] + intras[h] + diag * vs[h]
            return tuple(st * jnp.exp(b_end) + g for st, b_end, g in zip(sts, b_ends, grown))

        lax.fori_loop(0, nc, chunk, (jnp.zeros((HG_HEAD_DIM, HG_HEAD_DIM), F32),) * HG_GROUP)

    col_spec = lambda off: pl.BlockSpec((s_len, gw), lambda h: (0, off + h))
    return _pcall(
        body, name=name, out_shape=jax.ShapeDtypeStruct((s_len, WIDTH), F32),
        grid=(n_groups,),
        in_specs=[col_spec(base), col_spec(base + n_groups), col_spec(base + 2 * n_groups),
                  pl.BlockSpec((1, gw), lambda h: (0, h))],
        out_specs=pl.BlockSpec((s_len, gw), lambda h: (0, h)),
        scratch_shapes=[pltpu.VMEM((len(HG_LEVELS), BLK, BLK), F32)],
        semantics=("arbitrary",))(proj, proj, proj, lb)


def _rms_heads(o_b, norm_w):
    n_parts, h_parts, r_parts = [], [], []
    for h in range(WIDTH // HG_HEAD_DIM):
        sl = slice(h * HG_HEAD_DIM, (h + 1) * HG_HEAD_DIM)
        o = o_b[:, sl]
        rstd = lax.rsqrt(jnp.mean(o * o, axis=-1, keepdims=True) + RMS_EPS)
        ohat = o * rstd
        h_parts.append(ohat)
        n_parts.append(ohat * norm_w[:, sl])
        r_parts.append(jnp.broadcast_to(rstd, o.shape))
    cat = lambda parts: jnp.concatenate(parts, axis=-1)
    return cat(n_parts), cat(h_parts), cat(r_parts)


def _shift_rows_down(halo, cur, k):
    tm = cur.shape[0]
    ext = jnp.concatenate([halo, cur], axis=0)
    return pltpu.roll(ext, k, 0)[8:8 + tm]


def _shift_rows_up(cur, halo, k):
    tm = cur.shape[0]
    ext = jnp.concatenate([cur, halo], axis=0)
    return pltpu.roll(ext, (tm + 8 - k) % (tm + 8), 0)[0:tm]


def _merge_fwd(x, proj, o_a, o_b, gate, norm_w, conv_w, wb, w_out, ln_g, ln_b, name):
    s_len, d = x.shape
    tm = min(256, s_len)
    hb = tm // 8

    def body(x_ref, oa_ref, za_ref, ob_ref, zb_ref, pre_ref, post_ref, u_ref, zc_ref, hpre_ref, hu_ref, g_ref,
             gate_ref, nw_ref, cw_ref, wb_ref, wo_ref, lg_ref, lbias_ref, xn_ref, mg_ref, yc_ref):
        i = pl.program_id(0)
        sa, _ = _silu_and_grad(za_ref[...])
        y_a = (oa_ref[...] * sa).astype(BF16)
        n_b, _, _ = _rms_heads(ob_ref[...], nw_ref[...])
        sb, _ = _silu_and_grad(zb_ref[...])
        y_b = (n_b * sb).astype(BF16)
        a = pre_ref[...] * u_ref[...]
        halo = jnp.where(i > 0, hpre_ref[...] * hu_ref[...], 0.0)
        cw = cw_ref[...]
        conv = cw[0:1] * _shift_rows_down(halo, a, 2) + cw[1:2] * _shift_rows_down(halo, a, 1) + cw[2:3] * a
        sc, _ = _silu_and_grad(zc_ref[...])
        y_c = (post_ref[...] * conv * sc).astype(BF16)
        merged = None
        for k, yk in enumerate((y_a, y_b, y_c)):
            yc_ref[:, k * WIDTH:(k + 1) * WIDTH] = yk
            term = _sigmoid(g_ref[:, k * d:(k + 1) * d]) * _dot(yk, wb_ref[k])
            merged = term if merged is None else merged + term
        mb = merged.astype(BF16)
        mg_ref[...] = mb
        y = _dot(mb, wo_ref[...])
        r = ALPHA * x_ref[...] + (1.0 + gate_ref[...]) * y
        rhat, _ = _standardize(r)
        xn_ref[...] = rhat * lg_ref[...] + lbias_ref[...]

    wcol = lambda cb: pl.BlockSpec((tm, WIDTH), lambda i: (i, cb))
    halo_spec = lambda cb: pl.BlockSpec((8, WIDTH), lambda i: (jnp.maximum(i * hb - 1, 0), cb))
    vec = lambda w: pl.BlockSpec((1, w), lambda i: (0, 0))
    return _pcall(
        body, name=name,
        out_shape=(jax.ShapeDtypeStruct((s_len, d), F32), jax.ShapeDtypeStruct((s_len, d), BF16),
                   jax.ShapeDtypeStruct((s_len, 3 * WIDTH), BF16)),
        grid=(s_len // tm,),
        in_specs=[pl.BlockSpec((tm, d), lambda i: (i, 0)),
                  wcol(0), wcol(3), wcol(0), wcol(7), wcol(8), wcol(9), wcol(10), wcol(11),
                  halo_spec(8), halo_spec(10),
                  pl.BlockSpec((tm, 3 * d), lambda i: (i, 2)),
                  vec(d), vec(WIDTH),
                  pl.BlockSpec((3, WIDTH), lambda i: (0, 0)),
                  pl.BlockSpec((3, WIDTH, d), lambda i: (0, 0, 0)),
                  pl.BlockSpec((d, d), lambda i: (0, 0)),
                  vec(d), vec(d)],
        out_specs=(pl.BlockSpec((tm, d), lambda i: (i, 0)), pl.BlockSpec((tm, d), lambda i: (i, 0)),
                   pl.BlockSpec((tm, 3 * WIDTH), lambda i: (i, 0))),
        semantics=("arbitrary",))(x, o_a, proj, o_b, proj, proj, proj, proj, proj, proj, proj, proj,
                                  gate, norm_w, conv_w, wb, w_out, ln_g, ln_b)


def _loss_fwd_bwd(y, target):
    s_len, d = y.shape
    tm = min(512, s_len)

    def body(y_ref, t_ref, loss_ref, dy_ref):
        @pl.when(pl.program_id(0) == 0)
        def _():
            loss_ref[...] = jnp.zeros_like(loss_ref)

        e = y_ref[...] - t_ref[...]
        dy_ref[...] = e * (1.0 / d)
        part = jnp.sum(jnp.sum(e * e, axis=-1, keepdims=True), axis=0, keepdims=True)
        loss_ref[...] += part * (0.5 / d)

    tile = pl.BlockSpec((tm, d), lambda i: (i, 0))
    return _pcall(body, name="loss", grid=(s_len // tm,),
                  out_shape=(jax.ShapeDtypeStruct((1, 1), F32), jax.ShapeDtypeStruct((s_len, d), F32)),
                  in_specs=[tile, tile],
                  out_specs=(pl.BlockSpec((1, 1), lambda i: (0, 0)), tile),
                  semantics=("arbitrary",))(y, target)


def _merge_bwd(dxn, x, merged, ycat, proj, gate, wb, w_out, ln_g, name):
    s_len, d = x.shape
    tm = min(256, s_len)
    dsh = d // NDEV
    n_tiles = s_len // tm

    def body(dxn_ref, x_ref, mg_ref, yc_ref, g_ref, gate_ref, wb_ref, wo_ref, lg_ref,
             dres_ref, dyc_ref, dg_ref, gwo_out, gwb_out, vec_ref, gwo_ref, gwb_ref):
        @pl.when(pl.program_id(0) == 0)
        def _():
            gwo_ref[...] = jnp.zeros_like(gwo_ref)
            gwb_ref[...] = jnp.zeros_like(gwb_ref)
            vec_ref[...] = jnp.zeros_like(vec_ref)

        mb = mg_ref[...]
        one_gate = 1.0 + gate_ref[...]
        y = _dot(mb, wo_ref[...])
        r = ALPHA * x_ref[...] + one_gate * y
        rhat, rstd = _standardize(r)
        dxn = dxn_ref[...]
        dr = _standardize_bwd(rhat, rstd, dxn * lg_ref[...])
        vec_ref[0:1, :] += jnp.sum(dxn * rhat, axis=0, keepdims=True)
        vec_ref[1:2, :] += jnp.sum(dxn, axis=0, keepdims=True)
        vec_ref[2:3, :] += jnp.sum(dr * y, axis=0, keepdims=True)
        dres_ref[...] = ALPHA * dr
        dy = (one_gate * dr).astype(BF16)
        gwo_ref[...] += _dot_tn(mb, dy)
        dmerged = _dot_nt(dy, wo_ref[...])
        for k in range(3):
            yk = yc_ref[:, k * WIDTH:(k + 1) * WIDTH]
            sg = _sigmoid(g_ref[:, k * d:(k + 1) * d])
            pk = _dot(yk, wb_ref[k])
            dg_ref[:, k * d:(k + 1) * d] = (dmerged * pk * sg * (1.0 - sg)).astype(BF16)
            dpk = (dmerged * sg).astype(BF16)
            dyc_ref[:, k * WIDTH:(k + 1) * WIDTH] = _dot_nt(dpk, wb_ref[k])
            gwb_ref[k] += _dot_tn(yk, dpk)

        @pl.when(pl.program_id(0) == n_tiles - 1)
        def _():
            for o in range(NDEV):
                gwo_out[o] = gwo_ref[o * dsh:(o + 1) * dsh, :].astype(BF16)
                for k in range(3):
                    gwb_out[o, k] = gwb_ref[k, :, o * dsh:(o + 1) * dsh].astype(BF16)

    tile = lambda w: pl.BlockSpec((tm, w), lambda i: (i, 0))
    vec = pl.BlockSpec((1, d), lambda i: (0, 0))
    return _pcall(
        body, name=name,
        out_shape=(jax.ShapeDtypeStruct((s_len, d), F32), jax.ShapeDtypeStruct((s_len, 3 * WIDTH), F32),
                   jax.ShapeDtypeStruct(proj.shape, BF16), jax.ShapeDtypeStruct((NDEV, dsh, d), BF16),
                   jax.ShapeDtypeStruct((NDEV, 3, WIDTH, dsh), BF16), jax.ShapeDtypeStruct((8, d), F32)),
        grid=(n_tiles,),
        in_specs=[tile(d), tile(d), tile(d), tile(3 * WIDTH),
                  pl.BlockSpec((tm, 3 * d), lambda i: (i, 2)),
                  vec, pl.BlockSpec((3, WIDTH, d), lambda i: (0, 0, 0)),
                  pl.BlockSpec((d, d), lambda i: (0, 0)), vec],
        out_specs=(tile(d), tile(3 * WIDTH), pl.BlockSpec((tm, 3 * d), lambda i: (i, 2)),
                   pl.BlockSpec((NDEV, dsh, d), lambda i: (0, 0, 0)),
                   pl.BlockSpec((NDEV, 3, WIDTH, dsh), lambda i: (0, 0, 0, 0)),
                   pl.BlockSpec((8, d), lambda i: (0, 0))),
        scratch_shapes=[pltpu.VMEM((d, d), F32), pltpu.VMEM((3, WIDTH, d), F32)],
        semantics=("arbitrary",))(dxn, x, merged, ycat, proj, gate, wb, w_out, ln_g)


def _branch_bwd(dycat, proj, o_a, o_b, norm_w, conv_w, dproj, name):
    s_len = proj.shape[0]
    tm = min(256, s_len)
    hb = tm // 8
    n_tiles = s_len // tm

    def body(dya_ref, dyb_ref, dyc_ref, oa_ref, za_ref, ob_ref, zb_ref, pre_ref, post_ref, u_ref, zc_ref,
             hpre_ref, hu_ref, ndyc_ref, npost_ref, nzc_ref, nw_ref, cw_ref, dproj_in,
             dproj_ref, doa_ref, dob_ref, vec_ref, dza_scr, dzb_scr, dc_scr, sems):
        del dproj_in
        i = pl.program_id(0)

        @pl.when(i == 0)
        def _():
            vec_ref[...] = jnp.zeros_like(vec_ref)

        sa, dsa = _silu_and_grad(za_ref[...])
        dya = dya_ref[...]
        doa_ref[...] = dya * sa
        dza_scr[...] = (dya * oa_ref[...] * dsa).astype(BF16)
        nw = nw_ref[...]
        n_b, ohat, rstd = _rms_heads(ob_ref[...], nw)
        sb, dsb = _silu_and_grad(zb_ref[...])
        dyb = dyb_ref[...]
        dzb_scr[...] = (dyb * n_b * dsb).astype(BF16)
        dn = dyb * sb
        vec_ref[0:1, :] += jnp.sum(dn * ohat, axis=0, keepdims=True)
        dnw = dn * nw
        parts = []
        for h in range(WIDTH // HG_HEAD_DIM):
            sl = slice(h * HG_HEAD_DIM, (h + 1) * HG_HEAD_DIM)
            m2 = jnp.mean(dnw[:, sl] * ohat[:, sl], axis=-1, keepdims=True)
            parts.append(rstd[:, sl] * (dnw[:, sl] - ohat[:, sl] * m2))
        dob_ref[...] = jnp.concatenate(parts, axis=-1)
        cw = cw_ref[...]
        pre, u, post = pre_ref[...], u_ref[...], post_ref[...]
        a = pre * u
        halo = jnp.where(i > 0, hpre_ref[...] * hu_ref[...], 0.0)
        a1 = _shift_rows_down(halo, a, 1)
        a2 = _shift_rows_down(halo, a, 2)
        conv = cw[0:1] * a2 + cw[1:2] * a1 + cw[2:3] * a
        sc, dsc = _silu_and_grad(zc_ref[...])
        dyc = dyc_ref[...]
        dconv = dyc * post * sc
        nsc, _ = _silu_and_grad(nzc_ref[...])
        nxt = jnp.where(i < n_tiles - 1, ndyc_ref[...] * npost_ref[...] * nsc, 0.0)
        da = cw[2:3] * dconv + cw[1:2] * _shift_rows_up(dconv, nxt, 1) + cw[0:1] * _shift_rows_up(dconv, nxt, 2)
        dc_scr[:, 0 * WIDTH:1 * WIDTH] = (da * u).astype(BF16)
        dc_scr[:, 1 * WIDTH:2 * WIDTH] = (dyc * conv * sc).astype(BF16)
        dc_scr[:, 2 * WIDTH:3 * WIDTH] = (da * pre).astype(BF16)
        dc_scr[:, 3 * WIDTH:4 * WIDTH] = (dyc * post * conv * dsc).astype(BF16)
        vec_ref[1:2, :] += jnp.sum(dconv * a2, axis=0, keepdims=True)
        vec_ref[2:3, :] += jnp.sum(dconv * a1, axis=0, keepdims=True)
        vec_ref[3:4, :] += jnp.sum(dconv * a, axis=0, keepdims=True)
        rows = pl.ds(pl.multiple_of(i * tm, tm), tm)
        copies = [pltpu.make_async_copy(dza_scr, dproj_ref.at[rows, 3 * WIDTH:4 * WIDTH], sems.at[0]),
                  pltpu.make_async_copy(dzb_scr, dproj_ref.at[rows, 7 * WIDTH:8 * WIDTH], sems.at[1]),
                  pltpu.make_async_copy(dc_scr, dproj_ref.at[rows, 8 * WIDTH:12 * WIDTH], sems.at[2])]
        for cp in copies:
            cp.start()
        for cp in copies:
            cp.wait()

    wcol = lambda cb: pl.BlockSpec((tm, WIDTH), lambda i: (i, cb))
    prev = lambda cb: pl.BlockSpec((8, WIDTH), lambda i: (jnp.maximum(i * hb - 1, 0), cb))
    nxt = lambda cb: pl.BlockSpec((8, WIDTH), lambda i: (jnp.minimum((i + 1) * hb, s_len // 8 - 1), cb))
    anyspec = pl.BlockSpec(memory_space=pl.ANY)
    out = jax.ShapeDtypeStruct((s_len, WIDTH), F32)
    return _pcall(
        body, name=name,
        out_shape=(jax.ShapeDtypeStruct(dproj.shape, dproj.dtype), out, out, jax.ShapeDtypeStruct((8, WIDTH), F32)),
        grid=(n_tiles,),
        in_specs=[wcol(0), wcol(1), wcol(2), wcol(0), wcol(3), wcol(0), wcol(7), wcol(8), wcol(9), wcol(10), wcol(11),
                  prev(8), prev(10), nxt(2), nxt(9), nxt(11),
                  pl.BlockSpec((1, WIDTH), lambda i: (0, 0)), pl.BlockSpec((3, WIDTH), lambda i: (0, 0)), anyspec],
        out_specs=(anyspec, wcol(0), wcol(0), pl.BlockSpec((8, WIDTH), lambda i: (0, 0))),
        scratch_shapes=[pltpu.VMEM((tm, WIDTH), BF16), pltpu.VMEM((tm, WIDTH), BF16),
                        pltpu.VMEM((tm, 4 * WIDTH), BF16), pltpu.SemaphoreType.DMA((3,))],
        aliases={18: 0},
        semantics=("arbitrary",))(dycat, dycat, dycat, o_a, proj, o_b, proj, proj, proj, proj, proj,
                                  proj, proj, dycat, proj, proj, norm_w, conv_w, dproj)


def _sb_bwd(proj, do_a, totals, dproj, name):
    s_len = proj.shape[0]
    n_pairs = WIDTH // BLK
    scale = SB_HEAD_DIM ** -0.5
    qr = min(SB_Q_ROWS, s_len)
    gb = SB_K_BLOCKS
    kw = gb * BLK
    nq = s_len // qr
    assert qr == kw

    def body(q_ref, k_ref, v_ref, do_ref, tot_ref, dproj_in, dproj_ref, dq_ref, dk_ref, dv_ref, out_scr, sems):
        del dproj_in
        lane = _iota2((1, BLK), 1)
        row = _iota2((BLK, BLK), 0)
        col = _iota2((BLK, BLK), 1)
        ones = jnp.ones((BLK, BLK), BF16)
        twice = lambda m: jnp.concatenate([m, m], axis=0)
        before_and_sum = twice(jnp.concatenate([(row < col).astype(BF16), ones], axis=1))
        upto_and_sum = twice(jnp.concatenate([(row <= col).astype(BF16), ones], axis=1))
        strict = _iota2((qr, kw), 1) < _iota2((qr, kw), 0)
        head_lanes = [(lane // SB_HEAD_DIM) == hh for hh in range(2)]
        dk_ref[...] = jnp.zeros_like(dk_ref)
        dv_ref[...] = jnp.zeros_like(dv_ref)

        def scores(gi, qms, masked):
            c0 = pl.multiple_of(gi * kw, kw)
            kb = k_ref[pl.ds(c0, kw), :].astype(BF16)
            z2s = [_dot_nt(qms[hh], kb) for hh in range(2)]
            if masked:
                z2s = [jnp.where(strict, z2, MASKED_SCORE) for z2 in z2s]
            return tuple(z2s)

        def process(gi, z2s, qms, doms, totals_i, carry):
            c0 = pl.multiple_of(gi * kw, kw)
            kf = k_ref[pl.ds(c0, kw), :]
            vf = v_ref[pl.ds(c0, kw), :]
            kms = [jnp.where(head_lanes[hh], kf, 0.0).astype(BF16) for hh in range(2)]
            vms = [jnp.where(head_lanes[hh], vf, 0.0).astype(BF16) for hh in range(2)]
            das = [_dot_nt(doms[hh], vms[hh]) for hh in range(2)]
            halves = [_softplus2_parts(z2) for z2 in z2s]
            terms = [[_split2_lanes(sp2[:, b * BLK:(b + 1) * BLK]) for b in range(gb)] for sp2, _ in halves]
            sums = [[_dot(t, before_and_sum) for t in head_terms] for head_terms in terms]
            weights, gmats, l_befores = [], [], []
            for hh in range(2):
                l_before = carry[3 * hh + 1]
                parts = []
                for b in range(gb):
                    parts.append(totals_i[hh] - l_before - sums[hh][b][:, :BLK])
                    l_before = l_before + sums[hh][b][:, BLK:]
                a = jnp.exp2(z2s[hh] - jnp.concatenate(parts, axis=1))
                weights.append(a.astype(BF16))
                gmats.append(a * das[hh])
                l_befores.append(l_before)
            terms = [[_split2_lanes(g[:, b * BLK:(b + 1) * BLK]) for b in range(gb)] for g in gmats]
            sums = [[_dot(t, upto_and_sum) for t in head_terms] for head_terms in terms]
            dzs, g_befores = [], []
            for hh in range(2):
                g_before = carry[3 * hh + 2]
                parts = []
                for b in range(gb):
                    parts.append(g_before + sums[hh][b][:, :BLK])
                    g_before = g_before + sums[hh][b][:, BLK:]
                dzs.append((gmats[hh] - halves[hh][1] * jnp.concatenate(parts, axis=1)).astype(BF16))
                g_befores.append(g_before)
            dk_t = _dot_tn(jnp.concatenate(qms, axis=0), jnp.concatenate(dzs, axis=0))
            dv_t = _dot_tn(jnp.concatenate(doms, axis=0), jnp.concatenate(weights, axis=0))
            dqs = [_dot(dzs[hh], kms[hh]) for hh in range(2)]
            dk_ref[:, pl.ds(c0, kw)] += dk_t * (1.0 / LOG2E)
            dv_ref[:, pl.ds(c0, kw)] += dv_t
            return (carry[0] + dqs[0], l_befores[0], g_befores[0], carry[3] + dqs[1], l_befores[1], g_befores[1])

        def queries(i):
            qf = q_ref[pl.ds(pl.multiple_of(i * qr, qr), qr), :] * (scale * LOG2E)
            return [jnp.where(head_lanes[hh], qf, 0.0).astype(BF16) for hh in range(2)]

        def qtile(i, first_scores):
            r0 = pl.multiple_of(i * qr, qr)
            qms = queries(i)
            dof = do_ref[pl.ds(r0, qr), :]
            doms = [jnp.where(head_lanes[hh], dof, 0.0).astype(BF16) for hh in range(2)]
            totals_i = [tot_ref[hh, pl.ds(r0, qr), :] for hh in range(2)]
            zero = jnp.zeros((qr, BLK), F32)

            def step(gi, state):
                return scores(gi + 1, qms, False) + process(gi, state[:2], qms, doms, totals_i, state[2:])

            def before_diagonal(state):
                return scores(i, qms, True) + process(i - 1, state[:2], qms, doms, totals_i, state[2:])

            state = lax.fori_loop(0, i - 1, step, first_scores + (zero,) * 6)
            state = lax.cond(i > 0, before_diagonal, lambda st: st, state)
            nxt = jnp.minimum(i + 1, nq - 1)
            next_scores = scores(0, queries(nxt), False)
            carry = process(i, state[:2], qms, doms, totals_i, state[2:])
            dq_ref[pl.ds(r0, qr), :] = (carry[0] + carry[3]) * scale
            return next_scores

        lax.fori_loop(0, nq, qtile, scores(0, queries(0), True))
        pair = pl.program_id(0)
        copies = []
        for t, value in enumerate((dq_ref[...], dk_ref[...].T, dv_ref[...].T)):
            out_scr[t] = value.astype(BF16)
            col = pl.multiple_of((t * n_pairs + pair) * BLK, BLK)
            copies.append(pltpu.make_async_copy(out_scr.at[t], dproj_ref.at[:, pl.ds(col, BLK)], sems.at[t]))
            copies[-1].start()
        for cp in copies:
            cp.wait()

    col_spec = lambda off: pl.BlockSpec((s_len, BLK), lambda p: (0, off + p))
    anyspec = pl.BlockSpec(memory_space=pl.ANY)
    return _pcall(
        body, name=name, out_shape=jax.ShapeDtypeStruct(dproj.shape, dproj.dtype), grid=(n_pairs,),
        in_specs=[col_spec(0), col_spec(n_pairs), col_spec(2 * n_pairs), col_spec(0),
                  pl.BlockSpec((2, s_len, BLK), lambda p: (p, 0, 0)), anyspec],
        out_specs=anyspec,
        scratch_shapes=[pltpu.VMEM((s_len, BLK), F32), pltpu.VMEM((BLK, s_len), F32), pltpu.VMEM((BLK, s_len), F32),
                        pltpu.VMEM((3, s_len, BLK), BF16), pltpu.SemaphoreType.DMA((3,))],
        aliases={5: 0},
        semantics=("arbitrary",))(proj, proj, proj, do_a, totals, dproj)


def _hgrn_bwd(proj, do_b, lb, dproj, name):
    s_len = proj.shape[0]
    nc = s_len // BLK
    gw = HG_GROUP * HG_HEAD_DIM
    n_groups = WIDTH // gw
    base = 4 * WIDTH // gw
    heads_of = range(HG_GROUP)

    def body(q_ref, f_ref, i_ref, do_ref, lb_ref, dproj_in, dproj_ref, dlb_ref, mask_ref, st_ref, out_scr, sems):
        del dproj_in
        _hg_masks(mask_ref)
        row = _iota2((BLK, BLK), 0)
        col = _iota2((BLK, BLK), 1)
        lower_incl = (col <= row).astype(BF16)
        upper_incl = (col >= row).astype(BF16)
        lb_v = lb_ref[...]
        refs = (q_ref, f_ref, i_ref)

        def fwd_chunk(ci, sts):
            for h in heads_of:
                st_ref[ci, h] = sts[h]
            heads, bs = _hg_load(refs, pl.multiple_of(ci * BLK, BLK), lb_v, lower_incl)
            b_ends = [b[BLK - 1:BLK, :] for b in bs]
            k_decs = [((1.0 - hd[2]) * jnp.exp(b_end - b)).astype(BF16) for hd, b, b_end in zip(heads, bs, b_ends)]
            grown = [_dot_tn(hd[5].astype(BF16), k_dec) for hd, k_dec in zip(heads, k_decs)]
            return tuple(st * jnp.exp(b_end) + g for st, b_end, g in zip(sts, b_ends, grown))

        zero_state = (jnp.zeros((HG_HEAD_DIM, HG_HEAD_DIM), F32),) * HG_GROUP
        lax.fori_loop(0, nc, fwd_chunk, zero_state)

        def bwd_chunk(cc, carry):
            dsts, suffixes, dlbs = carry
            ci = nc - 1 - cc
            r0 = pl.multiple_of(ci * BLK, BLK)
            heads, bs = _hg_load(refs, r0, lb_v, lower_incl)
            qs = [hd[0] for hd in heads]
            fs = [hd[2] for hd in heads]
            ks = [1.0 - f for f in fs]
            vs = [hd[5] for hd in heads]
            vbs = [v.astype(BF16) for v in vs]
            dos = [do_ref[pl.ds(r0, BLK), h * HG_HEAD_DIM:(h + 1) * HG_HEAD_DIM] for h in heads_of]
            dobs = [do.astype(BF16) for do in dos]
            b_ends = [b[BLK - 1:BLK, :] for b in bs]
            e_qs = [jnp.exp(b) for b in bs]
            e_ks = [jnp.exp(b_end - b) for b, b_end in zip(bs, b_ends)]
            qes = [(q * e).astype(BF16) for q, e in zip(qs, e_qs)]
            khs = [(k * e).astype(BF16) for k, e in zip(ks, e_ks)]
            st_terms = [_split2_lanes(st_ref[ci, h]) for h in heads_of]
            ds_terms = [_split2_lanes(dst) for dst in dsts]
            dqes = [_dot(dob, t[:, :HG_HEAD_DIM]) + _dot(dob, t[:, HG_HEAD_DIM:]) for dob, t in zip(dobs, st_terms)]
            dkhs = [_dot(vb, t[:, :HG_HEAD_DIM]) + _dot(vb, t[:, HG_HEAD_DIM:]) for vb, t in zip(vbs, ds_terms)]
            dvs = [_dot_nt(kh, t[:, :HG_HEAD_DIM]) for kh, t in zip(khs, ds_terms)]
            grown = [_dot_tn(dob, qe) for dob, qe in zip(dobs, qes)]
            das = [_dot_nt(dob, vb) for dob, vb in zip(dobs, vbs)]
            dqs = [e * dqe for e, dqe in zip(e_qs, dqes)]
            dks = [e * dkh for e, dkh in zip(e_ks, dkhs)]
            dlogs = [qe.astype(F32) * dqe - kh.astype(F32) * dkh for qe, dqe, kh, dkh in zip(qes, dqes, khs, dkhs)]
            scs = [None] * HG_GROUP
            for v_idx, m in enumerate(HG_LEVELS):
                es, qms, kms = _hg_level_terms(qs, ks, bs, m)
                msk = mask_ref[v_idx]
                terms = [_dot_nt(qm, km) for qm, km in zip(qms, kms)]
                pms = [(da * msk).astype(BF16) for da in das]
                dqms = [_dot(pm, km) for pm, km in zip(pms, kms)]
                dkms = [_dot_tn(pm, qm) for pm, qm in zip(pms, qms)]
                scs = [t * msk if sc is None else sc + t * msk for sc, t in zip(scs, terms)]
                dqs = [dq + dqm * e for dq, dqm, e in zip(dqs, dqms, es)]
                dks = [dk + dkm * e for dk, dkm, e in zip(dks, dkms, es)]
                dlogs = [dl + (qm.astype(F32) * dqm - km.astype(F32) * dkm)
                         for dl, qm, dqm, km, dkm in zip(dlogs, qms, dqms, kms, dkms)]
            intras = [_dot_tn(sc.astype(BF16), dob) for sc, dob in zip(scs, dobs)]
            dgs = [_dot_exact_l(upper_incl, dl) + sfx for dl, sfx in zip(dlogs, suffixes)]
            new_dlbs = []
            for h in heads_of:
                q, dq_fac, f, sig = heads[h][0], heads[h][1], heads[h][2], heads[h][3]
                a_diag = jnp.sum(dos[h] * vs[h], axis=-1, keepdims=True)
                s_diag = jnp.sum(q * ks[h], axis=-1, keepdims=True)
                dq = dqs[h] + a_diag * ks[h]
                dk = dks[h] + a_diag * q
                dv = dvs[h] + intras[h] + s_diag * dos[h]
                dfull = dgs[h] / f - dk
                sl = slice(h * HG_HEAD_DIM, (h + 1) * HG_HEAD_DIM)
                out_scr[0, pl.ds(r0, BLK), sl] = (dq * dq_fac).astype(BF16)
                out_scr[1, pl.ds(r0, BLK), sl] = (dfull * (1.0 - lb_v[:, sl]) * sig * (1.0 - sig)).astype(BF16)
                out_scr[2, pl.ds(r0, BLK), sl] = dv.astype(BF16)
                new_dlbs.append(dlbs[h] + jnp.sum(dfull * (1.0 - sig), axis=0, keepdims=True))
            new_dsts = tuple(dst * jnp.exp(b_end) + g for dst, b_end, g in zip(dsts, b_ends, grown))
            return new_dsts, tuple(dg[0:1, :] for dg in dgs), tuple(new_dlbs)

        zero_row = (jnp.zeros((1, HG_HEAD_DIM), F32),) * HG_GROUP
        _, _, dlbs = lax.fori_loop(0, nc, bwd_chunk, (zero_state, zero_row, zero_row))
        dlb_ref[...] = jnp.broadcast_to(jnp.concatenate(dlbs, axis=1), dlb_ref.shape)
        group = pl.program_id(0)
        copies = []
        for t in range(3):
            col = pl.multiple_of((base + t * n_groups + group) * gw, gw)
            copies.append(pltpu.make_async_copy(out_scr.at[t], dproj_ref.at[:, pl.ds(col, gw)], sems.at[t]))
            copies[-1].start()
        for cp in copies:
            cp.wait()

    col_spec = lambda off: pl.BlockSpec((s_len, gw), lambda h: (0, off + h))
    anyspec = pl.BlockSpec(memory_space=pl.ANY)
    return _pcall(
        body, name=name,
        out_shape=(jax.ShapeDtypeStruct(dproj.shape, dproj.dtype), jax.ShapeDtypeStruct((8, WIDTH), F32)),
        grid=(n_groups,),
        in_specs=[col_spec(base), col_spec(base + n_groups), col_spec(base + 2 * n_groups), col_spec(0),
                  pl.BlockSpec((1, gw), lambda h: (0, h)), anyspec],
        out_specs=(anyspec, pl.BlockSpec((8, gw), lambda h: (0, h))),
        scratch_shapes=[pltpu.VMEM((len(HG_LEVELS), BLK, BLK), F32),
                        pltpu.VMEM((nc, HG_GROUP, HG_HEAD_DIM, HG_HEAD_DIM), F32),
                        pltpu.VMEM((3, s_len, gw), BF16), pltpu.SemaphoreType.DMA((3,))],
        aliases={5: 0},
        semantics=("arbitrary",))(proj, proj, proj, do_b, lb, dproj)


def _dh_matmul(dproj, w_full, after, name):
    s_len, n = dproj.shape
    d = w_full.shape[0]
    tm = min(1024, s_len)
    tk = 2304

    def body(dp_ref, w_ref, after_ref, dh_ref):
        del after_ref
        part = _dot_nt(dp_ref[...], w_ref[...])

        @pl.when(pl.program_id(1) == 0)
        def _():
            dh_ref[...] = part

        @pl.when(pl.program_id(1) > 0)
        def _():
            dh_ref[...] += part

    return _pcall(
        body, name=name, out_shape=jax.ShapeDtypeStruct((s_len, d), F32),
        grid=(s_len // tm, n // tk),
        in_specs=[pl.BlockSpec((tm, tk), lambda i, k: (i, k)), pl.BlockSpec((d, tk), lambda i, k: (0, k)),
                  pl.BlockSpec(memory_space=pl.ANY)],
        out_specs=pl.BlockSpec((tm, d), lambda i, k: (i, 0)),
        semantics=("arbitrary", "arbitrary"))(dproj, w_full, after)


def _gw_matmul(h_t, dproj, name):
    d, s_len = h_t.shape
    n = dproj.shape[1]
    tn = 2304

    def body(ht_ref, dp_ref, gw_ref):
        gw_ref[...] = _dot(ht_ref[...], dp_ref[...]).astype(BF16)

    return _pcall(
        body, name=name, out_shape=jax.ShapeDtypeStruct((d, n), BF16),
        grid=(n // tn,),
        in_specs=[pl.BlockSpec((d, s_len), lambda j: (0, 0)), pl.BlockSpec((s_len, tn), lambda j: (0, j))],
        out_specs=pl.BlockSpec((d, tn), lambda j: (0, j)),
        semantics=("arbitrary",))(h_t, dproj)


def _ln_bwd(dh, x, scale, dres, name):
    s_len, d = x.shape
    tm = min(512, s_len)

    def body(dh_ref, x_ref, sc_ref, dres_ref, dx_ref, vec_ref):
        @pl.when(pl.program_id(0) == 0)
        def _():
            vec_ref[...] = jnp.zeros_like(vec_ref)

        dh = dh_ref[...]
        xs, rstd = _standardize(x_ref[...])
        vec_ref[0:1, :] += jnp.sum(dh, axis=0, keepdims=True)
        vec_ref[1:2, :] += jnp.sum(dh * xs, axis=0, keepdims=True)
        dx_ref[...] = _standardize_bwd(xs, rstd, dh * (1.0 + sc_ref[...])) + dres_ref[...]

    tile = pl.BlockSpec((tm, d), lambda i: (i, 0))
    return _pcall(body, name=name, grid=(s_len // tm,),
                  out_shape=(jax.ShapeDtypeStruct((s_len, d), F32), jax.ShapeDtypeStruct((8, d), F32)),
                  in_specs=[tile, tile, pl.BlockSpec((1, d), lambda i: (0, 0)), tile],
                  out_specs=(tile, pl.BlockSpec((8, d), lambda i: (0, 0))),
                  semantics=("arbitrary",))(dh, x, scale, dres)


def _wmod_grad(c_t, dmod):
    d = c_t.shape[0]
    n_layers, _, cm = dmod.shape

    def body(c_ref, dm_ref, o_ref):
        for l in range(n_layers):
            acc = None
            for b in range(NDEV):
                term = c_ref[:, b:b + 1] * dm_ref[l, b:b + 1, :]
                acc = term if acc is None else acc + term
            o_ref[l] = acc

    return _pcall(body, name="wmod_grad", out_shape=jax.ShapeDtypeStruct((n_layers, d, cm), F32))(c_t, dmod)


def _sum_adamw(parts_list, w, m, v, name):
    n_ranges = len(parts_list)
    n_src, range_rows, cols = parts_list[0].shape
    rows = range_rows * n_ranges
    tr = range_rows
    for cand in (512, 256, 128, 64, 32, 16, 8):
        if range_rows % cand == 0 and cand * cols * 4 <= (2 << 20):
            tr = cand
            break
    tiles = range_rows // tr

    def body(*refs):
        p_refs = refs[:n_ranges]
        w_ref, m_ref, v_ref, g_ref, d_ref, nm_ref, nv_ref = refs[n_ranges:]

        def step(p_ref):
            g = p_ref[0].astype(F32)
            for s in range(1, n_src):
                g = g + p_ref[s].astype(F32)
            nm = ADAM_B1 * m_ref[...] + (1.0 - ADAM_B1) * g
            nv = ADAM_B2 * v_ref[...] + (1.0 - ADAM_B2) * (g * g)
            m_hat = nm / (1.0 - ADAM_B1 ** ADAM_STEP)
            v_hat = nv / (1.0 - ADAM_B2 ** ADAM_STEP)
            g_ref[...] = g
            d_ref[...] = -ADAM_LR * (m_hat / (jnp.sqrt(v_hat) + ADAM_EPS) + ADAM_WD * w_ref[...])
            nm_ref[...] = nm
            nv_ref[...] = nv

        if n_ranges == 1:
            step(p_refs[0])
        else:
            for j in range(n_ranges):
                @pl.when(pl.program_id(0) // tiles == j)
                def _(j=j):
                    step(p_refs[j])

    def part_spec(j):
        return pl.BlockSpec((n_src, tr, cols), lambda i: (0, jnp.clip(i - j * tiles, 0, tiles - 1), 0))

    tile = pl.BlockSpec((tr, cols), lambda i: (i, 0))
    out = jax.ShapeDtypeStruct((rows, cols), F32)
    return _pcall(body, name=name, grid=(rows // tr,), out_shape=(out,) * 4,
                  in_specs=[part_spec(j) for j in range(n_ranges)] + [tile, tile, tile],
                  out_specs=(tile,) * 4, semantics=("arbitrary",))(*parts_list, w, m, v)


def _sum_parts(parts, name):
    n_src = parts.shape[0]

    def body(p_ref, o_ref):
        acc = p_ref[0]
        for s in range(1, n_src):
            acc = acc + p_ref[s]
        o_ref[...] = acc

    return _pcall(body, name=name, out_shape=jax.ShapeDtypeStruct(parts.shape[1:], F32))(parts)


def _pair_sum(gw, stage, me, name):
    d = gw.shape[0]
    n_slots, _, shard = stage.shape

    def body(me_ref, g_ref, s_ref, own_ref, o_ref):
        del me_ref
        total = (g_ref[...].astype(F32) + s_ref[0].astype(F32)).astype(BF16)
        o_ref[0] = total

        @pl.when(pl.program_id(0) == 0)
        def _():
            own_ref[0] = total

    slot = pl.BlockSpec((1, d, shard), lambda jj, me_ref: (jj, 0, 0))
    out = jax.ShapeDtypeStruct(stage.shape, BF16)
    return pl.pallas_call(
        body, name=name, out_shape=(out, out),
        grid_spec=pltpu.PrefetchScalarGridSpec(
            num_scalar_prefetch=1, grid=(n_slots,),
            in_specs=[pl.BlockSpec((d, shard), lambda jj, me_ref: (0, me_ref[0] ^ (2 * jj))), slot],
            out_specs=(pl.BlockSpec((1, d, shard), lambda jj, me_ref: (0, 0, 0)), slot)),
        compiler_params=pltpu.CompilerParams(dimension_semantics=("arbitrary",), vmem_limit_bytes=VMEM_LIMIT),
        interpret=False)(me.reshape(1).astype(jnp.int32), gw, stage)


def _lower_bound_table(lower_bounds):
    p = jax.nn.softmax(lower_bounds.astype(F32), axis=0)
    return jnp.cumsum(p, axis=0) - p[0:1]


def _pad_rows(v, width):
    n = v.shape[0]
    rows = -(-n // width)
    rows = -(-rows // 8) * 8
    return jnp.pad(v, (0, rows * width - n)).reshape(rows, width)


def kernel(x, c, w_mod, b_mod, w_in, conv_w, hgrn_norm_w, lower_bounds, w_branch, w_out, ln_g, ln_b, loss_target, m_w_mod, m_b_mod, m_w_in, m_conv_w, m_hgrn_norm_w, m_lower_bounds, m_w_branch, m_w_out, m_ln_g, m_ln_b, v_w_mod, v_b_mod, v_w_in, v_conv_w, v_hgrn_norm_w, v_lower_bounds, v_w_branch, v_w_out, v_ln_g, v_ln_b):
    n_layers = N_LAYERS
    s_len, d = x.shape[1], x.shape[2]
    n_cols = w_in.shape[2] * NDEV
    cw_cols = conv_w.shape[2]
    cm = w_mod.shape[2]
    me = _my_index()
    x0 = x[0]
    target = loss_target[0]

    small = _pad_rows(jnp.concatenate([c.reshape(-1), conv_w.reshape(-1)]), BLK)
    small_all = _all_gather_small("gather_c_conv", small).reshape(NDEV, -1)
    c_all = small_all[:, :d]
    conv_full = small_all[:, d:d + n_layers * 3 * cw_cols].reshape(NDEV, n_layers, 3, cw_cols)
    conv_full = conv_full.transpose(1, 2, 0, 3).reshape(n_layers, 3, WIDTH)

    b_mod_mine = lax.dynamic_slice_in_dim(b_mod, me * cm, cm, axis=1).reshape(n_layers, 1, cm)
    mod_cols = _mod_fwd(c_all, w_mod, b_mod_mine)
    mod_all = _all_gather_small("gather_mod", mod_cols.reshape(n_layers * NDEV, cm))
    mod_all = mod_all.reshape(NDEV, n_layers, NDEV, cm)
    mod_mine = lax.dynamic_index_in_dim(mod_all, me, axis=2, keepdims=False)
    mod_mine = mod_mine.transpose(1, 0, 2).reshape(n_layers, 3, 1, d)

    shard = w_in.shape[2]
    dsh = d // NDEV
    w_in_b, w_branch_b, w_out_b = w_in.astype(BF16), w_branch.astype(BF16), w_out.astype(BF16)
    window = lambda ref, dev: ref.at[:, pl.ds(pl.multiple_of(dev * shard, BLK), shard)]

    def two_step_sends(places):
        chips, sibling = [], []
        for k in (1, 2, 4, 6):
            for a, place in enumerate(places):
                chips.append((k, lambda ins, lands, me, a=a: ins[a],
                              lambda lands, me, a=a, place=place: place(lands[a], me),
                              lambda lands, me, a=a, k=k, place=place: place(lands[a], me ^ k)))
        for j in (2, 4, 6):
            for a, place in enumerate(places):
                sibling.append((1, lambda ins, lands, me, a=a, j=j, place=place: place(lands[a], me ^ j),
                                lambda lands, me, a=a, j=j, place=place: place(lands[a], me ^ j),
                                lambda lands, me, a=a, j=j, place=place: place(lands[a], me ^ 1 ^ j)))
        return chips, sibling

    in_sends = two_step_sends([window])
    rest_sends = two_step_sends([_slot, _slot])
    layer_sends = two_step_sends([window, _slot, _slot])

    def in_land(l):
        return _place_own_window(f"place_w_in_{l}", (d, n_cols), w_in_b[l], me)

    def rest_lands(l):
        return [_place_own((NDEV, 3, WIDTH, dsh), BF16, w_branch_b[l][None], (me, 0, 0, 0)),
                _place_own((NDEV, dsh, d), BF16, w_out_b[l][None], (me, 0, 0))]

    def gather_start(name, shards, lands, sends, after):
        return _exchange_start(f"{name}_chips_start", shards, lands, sends[0], after)

    def gather_pass_on(name, started, after, sends):
        _, lands = _exchange_wait(f"{name}_chips_wait", started, after, sends[0])
        return _exchange_start(f"{name}_sibling_start", [], lands, sends[1])

    def gather_finish(name, started, after, sends):
        return _exchange_wait(f"{name}_sibling_wait", started, after, sends[1])[1]

    def branch_out_weights(w_branch_l, w_out_l):
        return w_branch_l.transpose(1, 2, 0, 3).reshape(3, WIDTH, d), w_out_l.reshape(d, d)

    gathering = gather_start("gather_w_in_0", [w_in_b[0]], [in_land(0)], in_sends, mod_mine)
    passing = gather_pass_on("gather_w_in_0", gathering, gathering[4], in_sends)
    rest_gathering = gather_start("gather_rest_0", [w_branch_b[0], w_out_b[0]], rest_lands(0), rest_sends, passing[4])
    next_gathering = None
    if n_layers > 1:
        next_gathering = gather_start("gather_weights_1", [w_in_b[1], w_branch_b[1], w_out_b[1]],
                                      [in_land(1)] + rest_lands(1), layer_sends, rest_gathering[4])
    w_in_l = gather_finish("gather_w_in_0", passing, (next_gathering or rest_gathering)[4], in_sends)[0]

    lbs = _lower_bound_table(lower_bounds)
    norm_w4 = jnp.tile(hgrn_norm_w, (1, WIDTH // HG_HEAD_DIM))

    saved = []
    xl = x0
    for l in range(n_layers):
        shift, scale, gate = mod_mine[l, 0], mod_mine[l, 1], mod_mine[l, 2]
        proj, h_t = _ln_proj(xl, shift, scale, w_in_l, f"ln_proj_{l}")
        o_a, totals = _sb_fwd(proj, f"sb_fwd_{l}")
        if l == 0:
            rest_passing = gather_pass_on("gather_rest_0", rest_gathering, o_a, rest_sends)
        lb_l = lbs[l:l + 1] + rest_passing[4][0, 0] if l == 0 else lbs[l:l + 1]
        o_b = _hgrn_fwd(proj, lb_l, f"hgrn_fwd_{l}")
        if l == 0:
            wb_l, wo_l = branch_out_weights(*gather_finish("gather_rest_0", rest_passing, o_b, rest_sends))
            if n_layers > 1:
                next_passing = gather_pass_on("gather_weights_1", next_gathering, o_b, layer_sends)
                gate = gate + next_passing[4][0, 0]
        x_new, merged, ycat = _merge_fwd(xl, proj, o_a, o_b, gate, norm_w4[l:l + 1], conv_full[l],
                                         wb_l, wo_l, ln_g[l:l + 1], ln_b[l:l + 1], f"merge_fwd_{l}")
        saved.append((xl, proj, h_t, o_a, totals, o_b, merged, ycat, w_in_l, wb_l, wo_l))
        if l == 0 and n_layers > 1:
            w_in_l, w_branch_l, w_out_l = gather_finish("gather_weights_1", next_passing, x_new, layer_sends)
            wb_l, wo_l = branch_out_weights(w_branch_l, w_out_l)
        xl = x_new

    loss_part, dx = _loss_fwd_bwd(xl, target)
    loss = lax.psum(loss_part[0, 0], ("x", "y", "c"))

    pair_sends = [(1, lambda ins, lands, me, j=j: window(ins[0], me ^ 1 ^ j),
                   lambda lands, me, jj=jj: lands[0].at[jj], lambda lands, me, jj=jj: lands[0].at[jj])
                  for jj, j in enumerate((0, 2, 4, 6))]
    chip_sum_sends = [(j, lambda ins, lands, me, jj=jj: ins[0].at[jj],
                       lambda lands, me, jj=jj: lands[0].at[jj], lambda lands, me, jj=jj: lands[0].at[jj])
                      for jj, j in ((1, 2), (2, 4), (3, 6))]
    rest_scatter = _direct_sends([(0, 0, _slot, _slot), (1, 1, _slot, _slot)])
    scattering = [None] * n_layers
    small_grads = [None] * n_layers
    dmod = [None] * n_layers
    tie = None
    for l in reversed(range(n_layers)):
        xl, proj, h_t, o_a, totals, o_b, merged, ycat, w_in_l, wb_l, wo_l = saved[l]
        scale, gate = mod_mine[l, 1], mod_mine[l, 2]
        if tie is not None:
            gate = gate + tie[0, 0]
        dres, dycat, dproj, gwo_by_owner, gwb_by_owner, mvec = _merge_bwd(
            dx, xl, merged, ycat, proj, gate, wb_l, wo_l, ln_g[l:l + 1], f"merge_bwd_{l}")
        lands = [_place_own((NDEV, 3, WIDTH, dsh), BF16, lax.dynamic_slice_in_dim(gwb_by_owner, me, 1, axis=0),
                            (me, 0, 0, 0)),
                 _place_own((NDEV, dsh, d), BF16, lax.dynamic_slice_in_dim(gwo_by_owner, me, 1, axis=0),
                            (me, 0, 0))]
        rest_started = _exchange_start(f"scatter_rest_{l}_start", [gwb_by_owner, gwo_by_owner], lands, rest_scatter)
        dproj, do_a, do_b, bvec = _branch_bwd(dycat, proj, o_a, o_b, norm_w4[l:l + 1] + rest_started[4][0, 0],
                                              conv_full[l], dproj, f"branch_bwd_{l}")
        dproj = _sb_bwd(proj, do_a, totals, dproj, f"sb_bwd_{l}")
        dproj, dlb = _hgrn_bwd(proj, do_b, lbs[l:l + 1], dproj, f"hgrn_bwd_{l}")
        gwi = _gw_matmul(h_t, dproj, f"gw_matmul_{l}")
        swapping = _exchange_start(f"scatter_in_{l}_sibling_start", [gwi], [lax.empty((4, d, shard), BF16)], pair_sends)
        if l > 0:
            dh = _dh_matmul(dproj, w_in_l, swapping[4], f"dh_matmul_{l}")
        (gwi,), (stage,) = _exchange_wait(f"scatter_in_{l}_sibling_wait", swapping, dh if l > 0 else swapping[4],
                                          pair_sends)
        land, chip_sums = _pair_sum(gwi, stage, me, f"pair_sum_{l}")
        in_started = _exchange_start(f"scatter_in_{l}_chips_start", [chip_sums], [land], chip_sum_sends)
        scattering[l] = (in_started, rest_started)
        tie = in_started[4]
        if l == 0:
            dh = _dh_matmul(dproj, w_in_l, tie, f"dh_matmul_{l}")
        dx, lvec = _ln_bwd(dh, xl, scale + tie[0, 0], dres, f"ln_bwd_{l}")
        dmod[l] = jnp.concatenate([lvec[0], lvec[1], mvec[2]])
        norm_grad = bvec[0].reshape(WIDTH // HG_HEAD_DIM, HG_HEAD_DIM).sum(axis=0)
        small_grads[l] = jnp.concatenate([mvec[0], mvec[1], norm_grad, dlb[0], bvec[1:4].reshape(-1)])
    grad_x = dx[None]

    small_vec = jnp.concatenate(dmod + small_grads)
    n_small = small_vec.shape[0]
    small_all = _all_gather_small("gather_small_grads", _pad_rows(small_vec, BLK))
    small_sum = _sum_parts(small_all, "sum_small_grads").reshape(-1)[:n_small]
    dmod_all = small_all.reshape(NDEV, -1)[:, :n_layers * 3 * d].reshape(NDEV, n_layers, 3 * d)

    off = n_layers * 3 * d
    grad_b_mod = small_sum[:off].reshape(n_layers, 3 * d)
    per_layer = 2 * d + HG_HEAD_DIM + WIDTH + 3 * WIDTH
    g_ln_g, g_ln_b, g_norm, g_lbs, g_conv = [], [], [], [], []
    for l in range(n_layers):
        seg = small_sum[off + l * per_layer: off + (l + 1) * per_layer]
        g_ln_g.append(seg[:d])
        g_ln_b.append(seg[d:2 * d])
        g_norm.append(seg[2 * d:2 * d + HG_HEAD_DIM])
        g_lbs.append(seg[2 * d + HG_HEAD_DIM:2 * d + HG_HEAD_DIM + WIDTH])
        g_conv.append(seg[2 * d + HG_HEAD_DIM + WIDTH:].reshape(3, WIDTH))
    grad_ln_g, grad_ln_b = jnp.stack(g_ln_g), jnp.stack(g_ln_b)
    grad_norm = jnp.stack(g_norm)
    _, lbs_vjp = jax.vjp(_lower_bound_table, lower_bounds)
    grad_lower = lbs_vjp(jnp.stack(g_lbs))[0]
    grad_conv = lax.dynamic_slice_in_dim(jnp.stack(g_conv), me * cw_cols, cw_cols, axis=2)

    dmod_mine = lax.dynamic_slice_in_dim(dmod_all, me * cm, cm, axis=2).transpose(1, 0, 2)
    grad_w_mod = _wmod_grad(c_all.T, dmod_mine)

    p_in, p_branch, p_out = [None] * n_layers, [None] * n_layers, [None] * n_layers
    for l in reversed(range(n_layers)):
        in_started, rest_started = scattering[l]
        p_branch_l, p_out[l] = _exchange_wait(f"scatter_rest_{l}_wait", rest_started, grad_w_mod, rest_scatter)[1]
        p_branch[l] = p_branch_l.reshape(NDEV, 3 * WIDTH, dsh)
        p_in[l] = _exchange_wait(f"scatter_in_{l}_chips_wait", in_started, grad_w_mod, chip_sum_sends)[1][0]

    def adam(parts_list, w, m, v, name):
        shape = w.shape
        cols = shape[-1]
        flat = lambda a: a.reshape(-1, cols)
        outs = _sum_adamw(parts_list, flat(w), flat(m), flat(v), name)
        return [o.reshape(shape) for o in outs]

    r_w_in = adam(p_in, w_in, m_w_in, v_w_in, "adamw_w_in")
    r_w_branch = adam(p_branch, w_branch, m_w_branch, v_w_branch, "adamw_w_branch")
    r_w_out = adam(p_out, w_out, m_w_out, v_w_out, "adamw_w_out")
    r_w_mod = adam([grad_w_mod.reshape(1, -1, cm)], w_mod, m_w_mod, v_w_mod, "adamw_w_mod")

    small_names = ["b_mod", "conv_w", "hgrn_norm_w", "lower_bounds", "ln_g", "ln_b"]
    small_g = [grad_b_mod, grad_conv, grad_norm, grad_lower, grad_ln_g, grad_ln_b]
    small_w = [b_mod, conv_w, hgrn_norm_w, lower_bounds, ln_g, ln_b]
    small_m = [m_b_mod, m_conv_w, m_hgrn_norm_w, m_lower_bounds, m_ln_g, m_ln_b]
    small_v = [v_b_mod, v_conv_w, v_hgrn_norm_w, v_lower_bounds, v_ln_g, v_ln_b]
    pack = lambda arrs: _pad_rows(jnp.concatenate([a.reshape(-1) for a in arrs]), BLK)
    packed = _sum_adamw([pack(small_g)[None]], pack(small_w), pack(small_m), pack(small_v), "adamw_small")
    r_small = {n: [] for n in small_names}
    for res in packed:
        flat = res.reshape(-1)
        pos = 0
        for n, w in zip(small_names, small_w):
            r_small[n].append(flat[pos:pos + w.size].reshape(w.shape))
            pos += w.size

    results = {"w_mod": r_w_mod, "w_in": r_w_in, "w_branch": r_w_branch, "w_out": r_w_out, **r_small}
    order = ["w_mod", "b_mod", "w_in", "conv_w", "hgrn_norm_w", "lower_bounds", "w_branch", "w_out", "ln_g", "ln_b"]
    outs = [loss, grad_x]
    for idx in range(4):
        outs.extend(results[n][idx] for n in order)
    return tuple(outs)
```

```python
import jax
import jax.numpy as jnp
from jax import lax
from jax.experimental import pallas as pl
from jax.experimental.pallas import tpu as pltpu

F32 = jnp.float32
BF16 = jnp.bfloat16
NDEV = 8
N_LAYERS = 2
SB_HEAD_DIM = 64
HG_HEAD_DIM = 128
WIDTH = 512
BLK = 128
LN_EPS = 1e-5
RMS_EPS = 1e-6
ALPHA = (2.0 * N_LAYERS) ** 0.25
ADAM_LR, ADAM_B1, ADAM_B2, ADAM_EPS, ADAM_WD, ADAM_STEP = 0.001, 0.9, 0.999, 1e-08, 0.01, 10
VMEM_LIMIT = 56 * 1024 * 1024
MESH = pl.DeviceIdType.MESH
HG_LEVELS = (64, 32, 16, 8, 4, 2, 1)


def _pcall(body, *, name, out_shape, grid=None, in_specs=None, out_specs=None, scratch_shapes=(),
           semantics=None, aliases=None):
    kwargs = {}
    if grid is not None:
        kwargs["grid"] = grid
    if in_specs is not None:
        kwargs["in_specs"] = in_specs
    if out_specs is not None:
        kwargs["out_specs"] = out_specs
    if aliases:
        kwargs["input_output_aliases"] = aliases
    return pl.pallas_call(
        body, name=name, out_shape=out_shape, scratch_shapes=list(scratch_shapes),
        compiler_params=pltpu.CompilerParams(dimension_semantics=semantics, vmem_limit_bytes=VMEM_LIMIT),
        interpret=False, **kwargs)


def _dot(a, b):
    return jnp.dot(a, b, preferred_element_type=F32)


def _dot_nt(a, b):
    return lax.dot_general(a, b, (((1,), (1,)), ((), ())), preferred_element_type=F32)


def _dot_tn(a, b):
    return lax.dot_general(a, b, (((0,), (0,)), ((), ())), preferred_element_type=F32)


def _dot_01_l(m_bf16, x):
    x1 = x.astype(BF16)
    x2 = (x - x1.astype(F32)).astype(BF16)
    return _dot(jnp.concatenate([m_bf16, m_bf16], axis=1), jnp.concatenate([x1, x2], axis=0))


def _sigmoid(x):
    return 1.0 / (1.0 + jnp.exp(-x))


def _silu_and_grad(x):
    s = _sigmoid(x)
    return x * s, s * (1.0 + x * (1.0 - s))


LOG2E = 1.4426950408889634
MASKED_SCORE = -1e30


def _softplus2_parts(z2):
    minus_abs = lax.bitcast_convert_type(lax.bitcast_convert_type(z2, jnp.int32) | jnp.int32(-2 ** 31), F32)
    e = jnp.exp2(minus_abs)
    sp2 = jnp.maximum(z2, 0.0) + jnp.log2(1.0 + e)
    r = 1.0 / (1.0 + e)
    return sp2, jnp.where(z2 >= 0.0, r, e * r)


def _split2_lanes(x):
    x1 = x.astype(BF16)
    return jnp.concatenate([x1, (x - x1.astype(F32)).astype(BF16)], axis=1)


def _iota2(shape, dim):
    return lax.broadcasted_iota(jnp.int32, shape, dim)


def _standardize(x):
    mu = jnp.mean(x, axis=-1, keepdims=True)
    xc = x - mu
    var = jnp.mean(xc * xc, axis=-1, keepdims=True)
    rstd = lax.rsqrt(var + LN_EPS)
    return xc * rstd, rstd


def _standardize_bwd(xhat, rstd, dxhat):
    m1 = jnp.mean(dxhat, axis=-1, keepdims=True)
    m2 = jnp.mean(dxhat * xhat, axis=-1, keepdims=True)
    return rstd * (dxhat - m1 - xhat * m2)


def _my_index():
    return 4 * lax.axis_index("x") + 2 * lax.axis_index("y") + lax.axis_index("c")


def _exchange(name, ins, out_shapes, transfers, in_vmem):
    n_in, n_out, n_t = len(ins), len(out_shapes), len(transfers)

    def body(*refs):
        in_refs, out_refs = refs[:n_in], refs[n_in:n_in + n_out]
        send_sems, recv_sems, local_sems = refs[n_in + n_out:]
        x, y, c = lax.axis_index("x"), lax.axis_index("y"), lax.axis_index("c")
        me = 4 * x + 2 * y + c
        started = []
        for t, (i, o, src_fn, dst_fn) in enumerate(transfers):
            own = pltpu.make_async_copy(src_fn(in_refs[i], me), dst_fn(out_refs[o], me), local_sems.at[t])
            own.start()
            started.append(own)
        arrivals = []
        for k in range(1, NDEV):
            px = x ^ ((k >> 2) & 1)
            py = y ^ ((k >> 1) & 1)
            pc = c ^ (k & 1)
            peer = 4 * px + 2 * py + pc
            for t, (i, o, src_fn, dst_fn) in enumerate(transfers):
                sem = t * (NDEV - 1) + k - 1
                push = pltpu.make_async_remote_copy(
                    src_ref=src_fn(in_refs[i], peer), dst_ref=dst_fn(out_refs[o], me),
                    send_sem=send_sems.at[sem], recv_sem=recv_sems.at[sem],
                    device_id=(px, py, pc), device_id_type=MESH)
                push.start()
                started.append(push)
                arrivals.append(pltpu.make_async_remote_copy(
                    src_ref=src_fn(in_refs[i], peer), dst_ref=dst_fn(out_refs[o], peer),
                    send_sem=send_sems.at[sem], recv_sem=recv_sems.at[sem],
                    device_id=(px, py, pc), device_id_type=MESH))
        for arrival in arrivals:
            arrival.wait_recv()
        for cp in started[n_t:]:
            cp.wait_send()
        for own in started[:n_t]:
            own.wait()

    space = pltpu.VMEM if in_vmem else pl.ANY
    spec = pl.BlockSpec(memory_space=space)
    return _pcall(
        body, name=name, out_shape=out_shapes,
        in_specs=[spec] * n_in, out_specs=[spec] * n_out,
        scratch_shapes=[pltpu.SemaphoreType.DMA((n_t * (NDEV - 1),)),
                        pltpu.SemaphoreType.DMA((n_t * (NDEV - 1),)),
                        pltpu.SemaphoreType.DMA((n_t,))])(*ins)


def _whole(ref, dev):
    return ref


def _slot(ref, dev):
    return ref.at[dev]


def _all_gather_small(name, v):
    out = _exchange(name, [v], [jax.ShapeDtypeStruct((NDEV,) + v.shape, v.dtype)],
                    [(0, 0, _whole, _slot)], in_vmem=True)
    return out[0]


_HBM_SPEC = pl.BlockSpec(memory_space=pltpu.HBM)
_SEM_SPEC = pl.BlockSpec(memory_space=pltpu.SEMAPHORE)
_DATAFLOW = pltpu.SideEffectType.DATAFLOW_SIDE_EFFECTING


def _peer(x, y, c, k):
    px = x ^ ((k >> 2) & 1)
    py = y ^ ((k >> 1) & 1)
    pc = c ^ (k & 1)
    return (px, py, pc), 4 * px + 2 * py + pc


def _direct_sends(transfers):
    sends = []
    for k in range(1, NDEV):
        for i, o, src_fn, dst_fn in transfers:
            sends.append((k,
                          lambda ins, lands, me, i=i, k=k, src_fn=src_fn: src_fn(ins[i], me ^ k),
                          lambda lands, me, o=o, dst_fn=dst_fn: dst_fn(lands[o], me),
                          lambda lands, me, o=o, k=k, dst_fn=dst_fn: dst_fn(lands[o], me ^ k)))
    return sends


def _exchange_start(name, ins, lands, sends, after=None):
    n_in, n_buf = len(ins), len(ins) + len(lands)
    n_sem = len(sends)

    def body(*refs):
        in_refs, land_refs = refs[:n_in], refs[n_in:n_buf]
        n_skip = n_buf + (0 if after is None else 1)
        send_sems, recv_sems, token = refs[n_skip], refs[n_skip + 1], refs[-1]
        x, y, c = lax.axis_index("x"), lax.axis_index("y"), lax.axis_index("c")
        me = 4 * x + 2 * y + c
        for t, (k, src_fn, dst_fn, _) in enumerate(sends):
            pltpu.make_async_remote_copy(
                src_ref=src_fn(in_refs, land_refs, me), dst_ref=dst_fn(land_refs, me),
                send_sem=send_sems.at[t], recv_sem=recv_sems.at[t],
                device_id=_peer(x, y, c, k)[0], device_id_type=MESH).start()
        token[...] = jnp.zeros_like(token)

    bufs = [pltpu.with_memory_space_constraint(a, pltpu.HBM) for a in list(ins) + list(lands)]
    extra = [] if after is None else [after]
    outs = pl.pallas_call(
        body, name=name,
        out_shape=(pltpu.SemaphoreType.DMA((n_sem,)), pltpu.SemaphoreType.DMA((n_sem,)))
        + tuple(pltpu.HBM(a.shape, a.dtype) for a in bufs) + (jax.ShapeDtypeStruct((8, BLK), F32),),
        in_specs=[_HBM_SPEC] * n_buf + [pl.BlockSpec(memory_space=pl.ANY)] * len(extra),
        out_specs=(_SEM_SPEC, _SEM_SPEC) + (_HBM_SPEC,) * n_buf + (pl.BlockSpec(memory_space=pltpu.VMEM),),
        input_output_aliases={b: 2 + b for b in range(n_buf)},
        compiler_params=pltpu.CompilerParams(has_side_effects=_DATAFLOW),
        interpret=False)(*bufs, *extra)
    return outs[0], outs[1], list(outs[2:2 + n_in]), list(outs[2 + n_in:2 + n_buf]), outs[-1]


def _exchange_wait(name, started, after, sends):
    send_sems, recv_sems, ins, lands, _ = started
    n_in, n_buf = len(ins), len(ins) + len(lands)

    def body(*refs):
        in_refs, land_refs = refs[:n_in], refs[n_in:n_buf]
        send_sems, recv_sems = refs[n_buf], refs[n_buf + 1]
        x, y, c = lax.axis_index("x"), lax.axis_index("y"), lax.axis_index("c")
        me = 4 * x + 2 * y + c
        for t, (k, src_fn, _, rcv_fn) in enumerate(sends):
            cp = pltpu.make_async_remote_copy(
                src_ref=src_fn(in_refs, land_refs, me), dst_ref=rcv_fn(land_refs, me),
                send_sem=send_sems.at[t], recv_sem=recv_sems.at[t],
                device_id=_peer(x, y, c, k)[0], device_id_type=MESH)
            cp.wait_send()
            cp.wait_recv()

    bufs = list(ins) + list(lands)
    outs = pl.pallas_call(
        body, name=name, out_shape=tuple(pltpu.HBM(a.shape, a.dtype) for a in bufs),
        in_specs=[_HBM_SPEC] * n_buf + [_SEM_SPEC, _SEM_SPEC, pl.BlockSpec(memory_space=pl.ANY)],
        out_specs=(_HBM_SPEC,) * n_buf,
        input_output_aliases={b: b for b in range(n_buf)},
        compiler_params=pltpu.CompilerParams(has_side_effects=_DATAFLOW),
        interpret=False)(*bufs, send_sems, recv_sems, after)
    return list(outs[:n_in]), list(outs[n_in:])


def _place_own(shape, dtype, own, start):
    return lax.dynamic_update_slice(lax.empty(shape, dtype), own, start)


def _place_own_window(name, shape, own, me):
    rows, cols = own.shape

    def body(me_ref, zone_in, own_ref, zone_ref):
        del me_ref, zone_in
        zone_ref[...] = own_ref[...]

    return pl.pallas_call(
        body, name=name, out_shape=jax.ShapeDtypeStruct(shape, own.dtype),
        grid_spec=pltpu.PrefetchScalarGridSpec(
            num_scalar_prefetch=1, grid=(1,),
            in_specs=[pl.BlockSpec(memory_space=pl.ANY), pl.BlockSpec((rows, cols), lambda i, me_ref: (0, 0))],
            out_specs=pl.BlockSpec((rows, cols), lambda i, me_ref: (0, me_ref[0]))),
        input_output_aliases={1: 0},
        compiler_params=pltpu.CompilerParams(dimension_semantics=("arbitrary",), vmem_limit_bytes=VMEM_LIMIT),
        interpret=False)(me.reshape(1).astype(jnp.int32), lax.empty(shape, own.dtype), own)


def _mod_fwd(c_all, w_mod, b_mod_mine):
    n_layers, _, cm = w_mod.shape

    def body(c_ref, w_ref, b_ref, o_ref):
        for l in range(n_layers):
            o_ref[l] = jnp.dot(c_ref[...], w_ref[l], preferred_element_type=F32,
                               precision=lax.Precision.HIGHEST) + b_ref[l]

    return _pcall(body, name="mod_fwd", out_shape=jax.ShapeDtypeStruct((n_layers, NDEV, cm), F32))(
        c_all, w_mod, b_mod_mine)


def _ln_proj(x, shift, scale, w_full, name):
    s_len, d = x.shape
    n = w_full.shape[1]
    tm = min(1024, s_len)
    tn = 2304

    def body(x_ref, sh_ref, sc_ref, w_ref, proj_ref, ht_ref, h_scr):
        @pl.when(pl.program_id(1) == 0)
        def _():
            xs, _ = _standardize(x_ref[...])
            h = xs * (1.0 + sc_ref[...]) + sh_ref[...]
            h_scr[...] = h.astype(BF16)
            ht_ref[...] = h.T.astype(BF16)

        proj_ref[...] = _dot(h_scr[...], w_ref[...])

    return _pcall(
        body, name=name,
        out_shape=(jax.ShapeDtypeStruct((s_len, n), F32), jax.ShapeDtypeStruct((d, s_len), BF16)),
        grid=(s_len // tm, n // tn),
        in_specs=[pl.BlockSpec((tm, d), lambda i, j: (i, 0)),
                  pl.BlockSpec((1, d), lambda i, j: (0, 0)),
                  pl.BlockSpec((1, d), lambda i, j: (0, 0)),
                  pl.BlockSpec((d, tn), lambda i, j: (0, j))],
        out_specs=(pl.BlockSpec((tm, tn), lambda i, j: (i, j)),
                   pl.BlockSpec((d, tm), lambda i, j: (0, i))),
        scratch_shapes=[pltpu.VMEM((tm, d), BF16)],
        semantics=("arbitrary", "arbitrary"))(x, shift, scale, w_full)


SB_Q_ROWS = 256
SB_K_BLOCKS = 2


def _sb_fwd(proj, name):
    s_len = proj.shape[0]
    n_pairs = WIDTH // BLK
    qr = min(SB_Q_ROWS, s_len)
    gb = SB_K_BLOCKS
    kw = gb * BLK
    nq = s_len // qr
    assert qr == kw

    def body(q_ref, k_ref, v_ref, o_ref, tot_ref):
        lane = _iota2((1, BLK), 1)
        row = _iota2((BLK, BLK), 0)
        col = _iota2((BLK, BLK), 1)
        half = jnp.concatenate([(row >= col).astype(BF16), jnp.ones((BLK, BLK), BF16)], axis=1)
        suffix_and_sum = jnp.concatenate([half, half], axis=0)
        strict = _iota2((qr, kw), 1) < _iota2((qr, kw), 0)
        head_lanes = [(lane // SB_HEAD_DIM) == hh for hh in range(2)]

        def scores(gi, qms, masked):
            c0 = pl.multiple_of(gi * kw, kw)
            kb = k_ref[pl.ds(c0, kw), :].astype(BF16)
            z2s = [_dot_nt(qms[hh], kb) for hh in range(2)]
            if masked:
                z2s = [jnp.where(strict, z2, MASKED_SCORE) for z2 in z2s]
            return tuple(z2s)

        def accumulate(gi, z2s, carry):
            c0 = pl.multiple_of(gi * kw, kw)
            vf = v_ref[pl.ds(c0, kw), :]
            sp2s = [_softplus2_parts(z2)[0] for z2 in z2s]
            terms = [[_split2_lanes(sp2[:, b * BLK:(b + 1) * BLK]) for b in range(gb)] for sp2 in sp2s]
            sums = [[_dot(t, suffix_and_sum) for t in head_terms] for head_terms in terms]
            weights, laters = [], []
            for hh in range(2):
                later = carry[2 * hh + 1]
                parts = [None] * gb
                for b in reversed(range(gb)):
                    parts[b] = sums[hh][b][:, :BLK] + later
                    later = later + sums[hh][b][:, BLK:]
                weights.append(jnp.exp2(z2s[hh] - jnp.concatenate(parts, axis=1)).astype(BF16))
                laters.append(later)
            outs = [_dot(weights[hh], jnp.where(head_lanes[hh], vf, 0.0).astype(BF16)) for hh in range(2)]
            return (carry[0] + outs[0], laters[0], carry[2] + outs[1], laters[1])

        def queries(i):
            qf = q_ref[pl.ds(pl.multiple_of(i * qr, qr), qr), :] * (SB_HEAD_DIM ** -0.5 * LOG2E)
            return [jnp.where(head_lanes[hh], qf, 0.0).astype(BF16) for hh in range(2)]

        def qtile(i, first_scores):
            r0 = pl.multiple_of(i * qr, qr)
            qms = queries(i)
            zero = jnp.zeros((qr, BLK), F32)

            def step(jj, state):
                gi = i - 1 - jj
                return scores(gi, qms, False) + accumulate(gi + 1, state[:2], state[2:])

            state = lax.fori_loop(0, i, step, first_scores + (zero,) * 4)
            nxt = jnp.minimum(i + 1, nq - 1)
            next_scores = scores(nxt, queries(nxt), True)
            carry = accumulate(0, state[:2], state[2:])
            o_ref[pl.ds(r0, qr), :] = carry[0] + carry[2]
            tot_ref[0, pl.ds(r0, qr), :] = carry[1]
            tot_ref[1, pl.ds(r0, qr), :] = carry[3]
            return next_scores

        lax.fori_loop(0, nq, qtile, scores(0, queries(0), True))

    col_spec = lambda off: pl.BlockSpec((s_len, BLK), lambda p: (0, off + p))
    return _pcall(
        body, name=name,
        out_shape=(jax.ShapeDtypeStruct((s_len, WIDTH), F32),
                   jax.ShapeDtypeStruct((2 * n_pairs, s_len, BLK), F32)),
        grid=(n_pairs,),
        in_specs=[col_spec(0), col_spec(n_pairs), col_spec(2 * n_pairs)],
        out_specs=(pl.BlockSpec((s_len, BLK), lambda p: (0, p)),
                   pl.BlockSpec((2, s_len, BLK), lambda p: (p, 0, 0))),
        semantics=("arbitrary",))(proj, proj, proj)


def _hg_masks(mask_ref):
    row = _iota2((BLK, BLK), 0)
    col = _iota2((BLK, BLK), 1)
    for v, m in enumerate(HG_LEVELS):
        same = (row // (2 * m)) == (col // (2 * m))
        mask_ref[v] = (same & ((row & m) != 0) & ((col & m) == 0)).astype(F32)


def _hg_mid(b, m):
    if m >= 4:
        n = BLK // (2 * m)
        mid = b.reshape(n, 2 * m, BLK)[:, m - 1:m, :]
        return jnp.broadcast_to(mid, (n, 2 * m, BLK)).reshape(BLK, BLK)
    pos = _iota2((BLK, BLK), 0) & (2 * m - 1)
    out = b
    for p in range(2 * m):
        delta = (m - 1) - p
        if delta != 0:
            out = jnp.where(pos == p, pltpu.roll(b, (-delta) % BLK, 0), out)
    return out


def _hg_chunk_inputs(qraw, fpre, lb):
    sig = _sigmoid(fpre)
    f = lb + (1.0 - lb) * sig
    g = jnp.log(f)
    q, dq_fac = _silu_and_grad(qraw)
    return q, dq_fac, f, sig, g


HG_GROUP = 4


def _neg_abs(x):
    return lax.bitcast_convert_type(lax.bitcast_convert_type(x, jnp.int32) | jnp.int32(-2 ** 31), F32)


def _hg_level_terms(qs, ks, bs, m):
    es = [jnp.exp(_neg_abs(b - _hg_mid(b, m))) for b in bs]
    qts = [(q * e).astype(BF16) for q, e in zip(qs, es)]
    kts = [(k * e).astype(BF16) for k, e in zip(ks, es)]
    return es, qts, kts


def _hg_load(refs, r0, lb_v, lower_incl):
    q_ref, f_ref, i_ref = refs
    heads = []
    for h in range(HG_GROUP):
        sl = slice(h * HG_HEAD_DIM, (h + 1) * HG_HEAD_DIM)
        heads.append(_hg_chunk_inputs(q_ref[pl.ds(r0, BLK), sl], f_ref[pl.ds(r0, BLK), sl], lb_v[:, sl])
                     + (i_ref[pl.ds(r0, BLK), sl],))
    bs = [_dot_01_l(lower_incl, hd[4]) for hd in heads]
    return heads, bs


def _hgrn_fwd(proj, lb, name):
    s_len = proj.shape[0]
    nc = s_len // BLK
    gw = HG_GROUP * HG_HEAD_DIM
    n_groups = WIDTH // gw
    base = 4 * WIDTH // gw

    def body(q_ref, f_ref, i_ref, lb_ref, o_ref, mask_ref):
        _hg_masks(mask_ref)
        row = _iota2((BLK, BLK), 0)
        col = _iota2((BLK, BLK), 1)
        lower_incl = (col <= row).astype(BF16)
        lb_v = lb_ref[...]

        def chunk(ci, sts):
            r0 = pl.multiple_of(ci * BLK, BLK)
            heads, bs = _hg_load((q_ref, f_ref, i_ref), r0, lb_v, lower_incl)
            qs = [hd[0] for hd in heads]
            ks = [1.0 - hd[2] for hd in heads]
            vs = [hd[5] for hd in heads]
            vbs = [v.astype(BF16) for v in vs]
            b_ends = [b[BLK - 1:BLK, :] for b in bs]
            inters = [_dot_nt((q * jnp.exp(b)).astype(BF16), st.astype(BF16)) for q, b, st in zip(qs, bs, sts)]
            scs = [None] * HG_GROUP
            for v_idx, m in enumerate(HG_LEVELS):
                _, qts, kts = _hg_level_terms(qs, ks, bs, m)
                terms = [_dot_nt(qt, kt) for qt, kt in zip(qts, kts)]
                msk = mask_ref[v_idx]
                scs = [t * msk if sc is None else sc + t * msk for sc, t in zip(scs, terms)]
            intras = [_dot(sc.astype(BF16), vb) for sc, vb in zip(scs, vbs)]
            k_decs = [(k * jnp.exp(b_end - b)).astype(BF16) for k, b, b_end in zip(ks, bs, b_ends)]
            grown = [_dot_tn(vb, k_dec) for vb, k_dec in zip(vbs, k_decs)]
            for h in range(HG_GROUP):
                diag = jnp.sum(qs[h] * ks[h], axis=-1, keepdims=True)
                o_ref[pl.ds(r0, BLK), h * HG_HEAD_DIM:(h + 1) * HG_HEAD_DIM] = inters[h] + intras[h] + diag * vs[h]
            return tuple(st * jnp.exp(b_end) + g for st, b_end, g in zip(sts, b_ends, grown))

        lax.fori_loop(0, nc, chunk, (jnp.zeros((HG_HEAD_DIM, HG_HEAD_DIM), F32),) * HG_GROUP)

    col_spec = lambda off: pl.BlockSpec((s_len, gw), lambda h: (0, off + h))
    return _pcall(
        body, name=name, out_shape=jax.ShapeDtypeStruct((s_len, WIDTH), F32),
        grid=(n_groups,),
        in_specs=[col_spec(base), col_spec(base + n_groups), col_spec(base + 2 * n_groups),
                  pl.BlockSpec((1, gw), lambda h: (0, h))],
        out_specs=pl.BlockSpec((s_len, gw), lambda h: (0, h)),
        scratch_shapes=[pltpu.VMEM((len(HG_LEVELS), BLK, BLK), F32)],
        semantics=("arbitrary",))(proj, proj, proj, lb)


def _rms_heads(o_b, norm_w):
    n_parts, h_parts, r_parts = [], [], []
    for h in range(WIDTH // HG_HEAD_DIM):
        sl = slice(h * HG_HEAD_DIM, (h + 1) * HG_HEAD_DIM)
        o = o_b[:, sl]
        rstd = lax.rsqrt(jnp.mean(o * o, axis=-1, keepdims=True) + RMS_EPS)
        ohat = o * rstd
        h_parts.append(ohat)
        n_parts.append(ohat * norm_w[:, sl])
        r_parts.append(jnp.broadcast_to(rstd, o.shape))
    cat = lambda parts: jnp.concatenate(parts, axis=-1)
    return cat(n_parts), cat(h_parts), cat(r_parts)


def _shift_rows_down(halo, cur, k):
    tm = cur.shape[0]
    ext = jnp.concatenate([halo, cur], axis=0)
    return pltpu.roll(ext, k, 0)[8:8 + tm]


def _shift_rows_up(cur, halo, k):
    tm = cur.shape[0]
    ext = jnp.concatenate([cur, halo], axis=0)
    return pltpu.roll(ext, (tm + 8 - k) % (tm + 8), 0)[0:tm]


def _merge_fwd(x, proj, o_a, o_b, gate, norm_w, conv_w, wb, w_out, ln_g, ln_b, name):
    s_len, d = x.shape
    tm = min(256, s_len)
    hb = tm // 8

    def body(x_ref, oa_ref, za_ref, ob_ref, zb_ref, pre_ref, post_ref, u_ref, zc_ref, hpre_ref, hu_ref, g_ref,
             gate_ref, nw_ref, cw_ref, wb_ref, wo_ref, lg_ref, lbias_ref, xn_ref, mg_ref, yc_ref):
        i = pl.program_id(0)
        sa, _ = _silu_and_grad(za_ref[...])
        y_a = (oa_ref[...] * sa).astype(BF16)
        n_b, _, _ = _rms_heads(ob_ref[...], nw_ref[...])
        sb, _ = _silu_and_grad(zb_ref[...])
        y_b = (n_b * sb).astype(BF16)
        a = pre_ref[...] * u_ref[...]
        halo = jnp.where(i > 0, hpre_ref[...] * hu_ref[...], 0.0)
        cw = cw_ref[...]
        conv = cw[0:1] * _shift_rows_down(halo, a, 2) + cw[1:2] * _shift_rows_down(halo, a, 1) + cw[2:3] * a
        sc, _ = _silu_and_grad(zc_ref[...])
        y_c = (post_ref[...] * conv * sc).astype(BF16)
        merged = None
        for k, yk in enumerate((y_a, y_b, y_c)):
            yc_ref[:, k * WIDTH:(k + 1) * WIDTH] = yk
            term = _sigmoid(g_ref[:, k * d:(k + 1) * d]) * _dot(yk, wb_ref[k])
            merged = term if merged is None else merged + term
        mb = merged.astype(BF16)
        mg_ref[...] = mb
        y = _dot(mb, wo_ref[...])
        r = ALPHA * x_ref[...] + (1.0 + gate_ref[...]) * y
        rhat, _ = _standardize(r)
        xn_ref[...] = rhat * lg_ref[...] + lbias_ref[...]

    wcol = lambda cb: pl.BlockSpec((tm, WIDTH), lambda i: (i, cb))
    halo_spec = lambda cb: pl.BlockSpec((8, WIDTH), lambda i: (jnp.maximum(i * hb - 1, 0), cb))
    vec = lambda w: pl.BlockSpec((1, w), lambda i: (0, 0))
    return _pcall(
        body, name=name,
        out_shape=(jax.ShapeDtypeStruct((s_len, d), F32), jax.ShapeDtypeStruct((s_len, d), BF16),
                   jax.ShapeDtypeStruct((s_len, 3 * WIDTH), BF16)),
        grid=(s_len // tm,),
        in_specs=[pl.BlockSpec((tm, d), lambda i: (i, 0)),
                  wcol(0), wcol(3), wcol(0), wcol(7), wcol(8), wcol(9), wcol(10), wcol(11),
                  halo_spec(8), halo_spec(10),
                  pl.BlockSpec((tm, 3 * d), lambda i: (i, 2)),
                  vec(d), vec(WIDTH),
                  pl.BlockSpec((3, WIDTH), lambda i: (0, 0)),
                  pl.BlockSpec((3, WIDTH, d), lambda i: (0, 0, 0)),
                  pl.BlockSpec((d, d), lambda i: (0, 0)),
                  vec(d), vec(d)],
        out_specs=(pl.BlockSpec((tm, d), lambda i: (i, 0)), pl.BlockSpec((tm, d), lambda i: (i, 0)),
                   pl.BlockSpec((tm, 3 * WIDTH), lambda i: (i, 0))),
        semantics=("arbitrary",))(x, o_a, proj, o_b, proj, proj, proj, proj, proj, proj, proj, proj,
                                  gate, norm_w, conv_w, wb, w_out, ln_g, ln_b)


def _loss_fwd_bwd(y, target):
    s_len, d = y.shape
    tm = min(512, s_len)

    def body(y_ref, t_ref, loss_ref, dy_ref):
        @pl.when(pl.program_id(0) == 0)
        def _():
            loss_ref[...] = jnp.zeros_like(loss_ref)

        e = y_ref[...] - t_ref[...]
        dy_ref[...] = e * (1.0 / d)
        part = jnp.sum(jnp.sum(e * e, axis=-1, keepdims=True), axis=0, keepdims=True)
        loss_ref[...] += part * (0.5 / d)

    tile = pl.BlockSpec((tm, d), lambda i: (i, 0))
    return _pcall(body, name="loss", grid=(s_len // tm,),
                  out_shape=(jax.ShapeDtypeStruct((1, 1), F32), jax.ShapeDtypeStruct((s_len, d), F32)),
                  in_specs=[tile, tile],
                  out_specs=(pl.BlockSpec((1, 1), lambda i: (0, 0)), tile),
                  semantics=("arbitrary",))(y, target)


def _merge_bwd(dxn, x, merged, ycat, proj, gate, wb, w_out, ln_g, name):
    s_len, d = x.shape
    tm = min(256, s_len)
    dsh = d // NDEV
    n_tiles = s_len // tm

    def body(dxn_ref, x_ref, mg_ref, yc_ref, g_ref, gate_ref, wb_ref, wo_ref, lg_ref,
             dres_ref, dyc_ref, dg_ref, gwo_out, gwb_out, vec_ref, gwo_ref, gwb_ref):
        @pl.when(pl.program_id(0) == 0)
        def _():
            gwo_ref[...] = jnp.zeros_like(gwo_ref)
            gwb_ref[...] = jnp.zeros_like(gwb_ref)
            vec_ref[...] = jnp.zeros_like(vec_ref)

        mb = mg_ref[...]
        one_gate = 1.0 + gate_ref[...]
        y = _dot(mb, wo_ref[...])
        r = ALPHA * x_ref[...] + one_gate * y
        rhat, rstd = _standardize(r)
        dxn = dxn_ref[...]
        dr = _standardize_bwd(rhat, rstd, dxn * lg_ref[...])
        vec_ref[0:1, :] += jnp.sum(dxn * rhat, axis=0, keepdims=True)
        vec_ref[1:2, :] += jnp.sum(dxn, axis=0, keepdims=True)
        vec_ref[2:3, :] += jnp.sum(dr * y, axis=0, keepdims=True)
        dres_ref[...] = ALPHA * dr
        dy = (one_gate * dr).astype(BF16)
        gwo_ref[...] += _dot_tn(mb, dy)
        dmerged = _dot_nt(dy, wo_ref[...])
        for k in range(3):
            yk = yc_ref[:, k * WIDTH:(k + 1) * WIDTH]
            sg = _sigmoid(g_ref[:, k * d:(k + 1) * d])
            pk = _dot(yk, wb_ref[k])
            dg_ref[:, k * d:(k + 1) * d] = (dmerged * pk * sg * (1.0 - sg)).astype(BF16)
            dpk = (dmerged * sg).astype(BF16)
            dyc_ref[:, k * WIDTH:(k + 1) * WIDTH] = _dot_nt(dpk, wb_ref[k])
            gwb_ref[k] += _dot_tn(yk, dpk)

        @pl.when(pl.program_id(0) == n_tiles - 1)
        def _():
            for o in range(NDEV):
                gwo_out[o] = gwo_ref[o * dsh:(o + 1) * dsh, :].astype(BF16)
                for k in range(3):
                    gwb_out[o, k] = gwb_ref[k, :, o * dsh:(o + 1) * dsh].astype(BF16)

    tile = lambda w: pl.BlockSpec((tm, w), lambda i: (i, 0))
    vec = pl.BlockSpec((1, d), lambda i: (0, 0))
    return _pcall(
        body, name=name,
        out_shape=(jax.ShapeDtypeStruct((s_len, d), F32), jax.ShapeDtypeStruct((s_len, 3 * WIDTH), F32),
                   jax.ShapeDtypeStruct(proj.shape, BF16), jax.ShapeDtypeStruct((NDEV, dsh, d), BF16),
                   jax.ShapeDtypeStruct((NDEV, 3, WIDTH, dsh), BF16), jax.ShapeDtypeStruct((8, d), F32)),
        grid=(n_tiles,),
        in_specs=[tile(d), tile(d), tile(d), tile(3 * WIDTH),
                  pl.BlockSpec((tm, 3 * d), lambda i: (i, 2)),
                  vec, pl.BlockSpec((3, WIDTH, d), lambda i: (0, 0, 0)),
                  pl.BlockSpec((d, d), lambda i: (0, 0)), vec],
        out_specs=(tile(d), tile(3 * WIDTH), pl.BlockSpec((tm, 3 * d), lambda i: (i, 2)),
                   pl.BlockSpec((NDEV, dsh, d), lambda i: (0, 0, 0)),
                   pl.BlockSpec((NDEV, 3, WIDTH, dsh), lambda i: (0, 0, 0, 0)),
                   pl.BlockSpec((8, d), lambda i: (0, 0))),
        scratch_shapes=[pltpu.VMEM((d, d), F32), pltpu.VMEM((3, WIDTH, d), F32)],
        semantics=("arbitrary",))(dxn, x, merged, ycat, proj, gate, wb, w_out, ln_g)


def _branch_bwd(dycat, proj, o_a, o_b, norm_w, conv_w, dproj, name):
    s_len = proj.shape[0]
    tm = min(256, s_len)
    hb = tm // 8
    n_tiles = s_len // tm

    def body(dya_ref, dyb_ref, dyc_ref, oa_ref, za_ref, ob_ref, zb_ref, pre_ref, post_ref, u_ref, zc_ref,
             hpre_ref, hu_ref, ndyc_ref, npost_ref, nzc_ref, nw_ref, cw_ref, dproj_in,
             dproj_ref, doa_ref, dob_ref, vec_ref, dza_scr, dzb_scr, dc_scr, sems):
        del dproj_in
        i = pl.program_id(0)

        @pl.when(i == 0)
        def _():
            vec_ref[...] = jnp.zeros_like(vec_ref)

        sa, dsa = _silu_and_grad(za_ref[...])
        dya = dya_ref[...]
        doa_ref[...] = dya * sa
        dza_scr[...] = (dya * oa_ref[...] * dsa).astype(BF16)
        nw = nw_ref[...]
        n_b, ohat, rstd = _rms_heads(ob_ref[...], nw)
        sb, dsb = _silu_and_grad(zb_ref[...])
        dyb = dyb_ref[...]
        dzb_scr[...] = (dyb * n_b * dsb).astype(BF16)
        dn = dyb * sb
        vec_ref[0:1, :] += jnp.sum(dn * ohat, axis=0, keepdims=True)
        dnw = dn * nw
        parts = []
        for h in range(WIDTH // HG_HEAD_DIM):
            sl = slice(h * HG_HEAD_DIM, (h + 1) * HG_HEAD_DIM)
            m2 = jnp.mean(dnw[:, sl] * ohat[:, sl], axis=-1, keepdims=True)
            parts.append(rstd[:, sl] * (dnw[:, sl] - ohat[:, sl] * m2))
        dob_ref[...] = jnp.concatenate(parts, axis=-1)
        cw = cw_ref[...]
        pre, u, post = pre_ref[...], u_ref[...], post_ref[...]
        a = pre * u
        halo = jnp.where(i > 0, hpre_ref[...] * hu_ref[...], 0.0)
        a1 = _shift_rows_down(halo, a, 1)
        a2 = _shift_rows_down(halo, a, 2)
        conv = cw[0:1] * a2 + cw[1:2] * a1 + cw[2:3] * a
        sc, dsc = _silu_and_grad(zc_ref[...])
        dyc = dyc_ref[...]
        dconv = dyc * post * sc
        nsc, _ = _silu_and_grad(nzc_ref[...])
        nxt = jnp.where(i < n_tiles - 1, ndyc_ref[...] * npost_ref[...] * nsc, 0.0)
        da = cw[2:3] * dconv + cw[1:2] * _shift_rows_up(dconv, nxt, 1) + cw[0:1] * _shift_rows_up(dconv, nxt, 2)
        dc_scr[:, 0 * WIDTH:1 * WIDTH] = (da * u).astype(BF16)
        dc_scr[:, 1 * WIDTH:2 * WIDTH] = (dyc * conv * sc).astype(BF16)
        dc_scr[:, 2 * WIDTH:3 * WIDTH] = (da * pre).astype(BF16)
        dc_scr[:, 3 * WIDTH:4 * WIDTH] = (dyc * post * conv * dsc).astype(BF16)
        vec_ref[1:2, :] += jnp.sum(dconv * a2, axis=0, keepdims=True)
        vec_ref[2:3, :] += jnp.sum(dconv * a1, axis=0, keepdims=True)
        vec_ref[3:4, :] += jnp.sum(dconv * a, axis=0, keepdims=True)
        rows = pl.ds(pl.multiple_of(i * tm, tm), tm)
        copies = [pltpu.make_async_copy(dza_scr, dproj_ref.at[rows, 3 * WIDTH:4 * WIDTH], sems.at[0]),
                  pltpu.make_async_copy(dzb_scr, dproj_ref.at[rows, 7 * WIDTH:8 * WIDTH], sems.at[1]),
                  pltpu.make_async_copy(dc_scr, dproj_ref.at[rows, 8 * WIDTH:12 * WIDTH], sems.at[2])]
        for cp in copies:
            cp.start()
        for cp in copies:
            cp.wait()

    wcol = lambda cb: pl.BlockSpec((tm, WIDTH), lambda i: (i, cb))
    prev = lambda cb: pl.BlockSpec((8, WIDTH), lambda i: (jnp.maximum(i * hb - 1, 0), cb))
    nxt = lambda cb: pl.BlockSpec((8, WIDTH), lambda i: (jnp.minimum((i + 1) * hb, s_len // 8 - 1), cb))
    anyspec = pl.BlockSpec(memory_space=pl.ANY)
    out = jax.ShapeDtypeStruct((s_len, WIDTH), F32)
    return _pcall(
        body, name=name,
        out_shape=(jax.ShapeDtypeStruct(dproj.shape, dproj.dtype), out, out, jax.ShapeDtypeStruct((8, WIDTH), F32)),
        grid=(n_tiles,),
        in_specs=[wcol(0), wcol(1), wcol(2), wcol(0), wcol(3), wcol(0), wcol(7), wcol(8), wcol(9), wcol(10), wcol(11),
                  prev(8), prev(10), nxt(2), nxt(9), nxt(11),
                  pl.BlockSpec((1, WIDTH), lambda i: (0, 0)), pl.BlockSpec((3, WIDTH), lambda i: (0, 0)), anyspec],
        out_specs=(anyspec, wcol(0), wcol(0), pl.BlockSpec((8, WIDTH), lambda i: (0, 0))),
        scratch_shapes=[pltpu.VMEM((tm, WIDTH), BF16), pltpu.VMEM((tm, WIDTH), BF16),
                        pltpu.VMEM((tm, 4 * WIDTH), BF16), pltpu.SemaphoreType.DMA((3,))],
        aliases={18: 0},
        semantics=("arbitrary",))(dycat, dycat, dycat, o_a, proj, o_b, proj, proj, proj, proj, proj,
                                  proj, proj, dycat, proj, proj, norm_w, conv_w, dproj)


def _sb_bwd(proj, do_a, totals, dproj, name):
    s_len = proj.shape[0]
    n_pairs = WIDTH // BLK
    scale = SB_HEAD_DIM ** -0.5
    qr = min(SB_Q_ROWS, s_len)
    gb = SB_K_BLOCKS
    kw = gb * BLK
    nq = s_len // qr
    assert qr == kw

    def body(q_ref, k_ref, v_ref, do_ref, tot_ref, dproj_in, dproj_ref, dq_ref, dk_ref, dv_ref, out_scr, sems):
        del dproj_in
        lane = _iota2((1, BLK), 1)
        row = _iota2((BLK, BLK), 0)
        col = _iota2((BLK, BLK), 1)
        ones = jnp.ones((BLK, BLK), BF16)
        twice = lambda m: jnp.concatenate([m, m], axis=0)
        before_and_sum = twice(jnp.concatenate([(row < col).astype(BF16), ones], axis=1))
        upto_and_sum = twice(jnp.concatenate([(row <= col).astype(BF16), ones], axis=1))
        strict = _iota2((qr, kw), 1) < _iota2((qr, kw), 0)
        head_lanes = [(lane // SB_HEAD_DIM) == hh for hh in range(2)]
        dk_ref[...] = jnp.zeros_like(dk_ref)
        dv_ref[...] = jnp.zeros_like(dv_ref)

        def scores(gi, qms, masked):
            c0 = pl.multiple_of(gi * kw, kw)
            kb = k_ref[pl.ds(c0, kw), :].astype(BF16)
            z2s = [_dot_nt(qms[hh], kb) for hh in range(2)]
            if masked:
                z2s = [jnp.where(strict, z2, MASKED_SCORE) for z2 in z2s]
            return tuple(z2s)

        def process(gi, z2s, qms, doms, totals_i, carry):
            c0 = pl.multiple_of(gi * kw, kw)
            kf = k_ref[pl.ds(c0, kw), :]
            vf = v_ref[pl.ds(c0, kw), :]
            kms = [jnp.where(head_lanes[hh], kf, 0.0).astype(BF16) for hh in range(2)]
            vms = [jnp.where(head_lanes[hh], vf, 0.0).astype(BF16) for hh in range(2)]
            das = [_dot_nt(doms[hh], vms[hh]) for hh in range(2)]
            halves = [_softplus2_parts(z2) for z2 in z2s]
            terms = [[_split2_lanes(sp2[:, b * BLK:(b + 1) * BLK]) for b in range(gb)] for sp2, _ in halves]
            sums = [[_dot(t, before_and_sum) for t in head_terms] for head_terms in terms]
            weights, gmats, l_befores = [], [], []
            for hh in range(2):
                l_before = carry[3 * hh + 1]
                parts = []
                for b in range(gb):
                    parts.append(totals_i[hh] - l_before - sums[hh][b][:, :BLK])
                    l_before = l_before + sums[hh][b][:, BLK:]
                a = jnp.exp2(z2s[hh] - jnp.concatenate(parts, axis=1))
                weights.append(a.astype(BF16))
                gmats.append(a * das[hh])
                l_befores.append(l_before)
            terms = [[_split2_lanes(g[:, b * BLK:(b + 1) * BLK]) for b in range(gb)] for g in gmats]
            sums = [[_dot(t, upto_and_sum) for t in head_terms] for head_terms in terms]
            dzs, g_befores = [], []
            for hh in range(2):
                g_before = carry[3 * hh + 2]
                parts = []
                for b in range(gb):
                    parts.append(g_before + sums[hh][b][:, :BLK])
                    g_before = g_before + sums[hh][b][:, BLK:]
                dzs.append((gmats[hh] - halves[hh][1] * jnp.concatenate(parts, axis=1)).astype(BF16))
                g_befores.append(g_before)
            dk_t = _dot_tn(jnp.concatenate(qms, axis=0), jnp.concatenate(dzs, axis=0))
            dv_t = _dot_tn(jnp.concatenate(doms, axis=0), jnp.concatenate(weights, axis=0))
            dqs = [_dot(dzs[hh], kms[hh]) for hh in range(2)]
            dk_ref[:, pl.ds(c0, kw)] += dk_t * (1.0 / LOG2E)
            dv_ref[:, pl.ds(c0, kw)] += dv_t
            return (carry[0] + dqs[0], l_befores[0], g_befores[0], carry[3] + dqs[1], l_befores[1], g_befores[1])

        def queries(i):
            qf = q_ref[pl.ds(pl.multiple_of(i * qr, qr), qr), :] * (scale * LOG2E)
            return [jnp.where(head_lanes[hh], qf, 0.0).astype(BF16) for hh in range(2)]

        def qtile(i, first_scores):
            r0 = pl.multiple_of(i * qr, qr)
            qms = queries(i)
            dof = do_ref[pl.ds(r0, qr), :]
            doms = [jnp.where(head_lanes[hh], dof, 0.0).astype(BF16) for hh in range(2)]
            totals_i = [tot_ref[hh, pl.ds(r0, qr), :] for hh in range(2)]
            zero = jnp.zeros((qr, BLK), F32)

            def step(gi, state):
                return scores(gi + 1, qms, False) + process(gi, state[:2], qms, doms, totals_i, state[2:])

            def before_diagonal(state):
                return scores(i, qms, True) + process(i - 1, state[:2], qms, doms, totals_i, state[2:])

            state = lax.fori_loop(0, i - 1, step, first_scores + (zero,) * 6)
            state = lax.cond(i > 0, before_diagonal, lambda st: st, state)
            nxt = jnp.minimum(i + 1, nq - 1)
            next_scores = scores(0, queries(nxt), False)
            carry = process(i, state[:2], qms, doms, totals_i, state[2:])
            dq_ref[pl.ds(r0, qr), :] = (carry[0] + carry[3]) * scale
            return next_scores

        lax.fori_loop(0, nq, qtile, scores(0, queries(0), True))
        pair = pl.program_id(0)
        copies = []
        for t, value in enumerate((dq_ref[...], dk_ref[...].T, dv_ref[...].T)):
            out_scr[t] = value.astype(BF16)
            col = pl.multiple_of((t * n_pairs + pair) * BLK, BLK)
            copies.append(pltpu.make_async_copy(out_scr.at[t], dproj_ref.at[:, pl.ds(col, BLK)], sems.at[t]))
            copies[-1].start()
        for cp in copies:
            cp.wait()

    col_spec = lambda off: pl.BlockSpec((s_len, BLK), lambda p: (0, off + p))
    anyspec = pl.BlockSpec(memory_space=pl.ANY)
    return _pcall(
        body, name=name, out_shape=jax.ShapeDtypeStruct(dproj.shape, dproj.dtype), grid=(n_pairs,),
        in_specs=[col_spec(0), col_spec(n_pairs), col_spec(2 * n_pairs), col_spec(0),
                  pl.BlockSpec((2, s_len, BLK), lambda p: (p, 0, 0)), anyspec],
        out_specs=anyspec,
        scratch_shapes=[pltpu.VMEM((s_len, BLK), F32), pltpu.VMEM((BLK, s_len), F32), pltpu.VMEM((BLK, s_len), F32),
                        pltpu.VMEM((3, s_len, BLK), BF16), pltpu.SemaphoreType.DMA((3,))],
        aliases={5: 0},
        semantics=("arbitrary",))(proj, proj, proj, do_a, totals, dproj)


def _hgrn_bwd(proj, do_b, lb, dproj, name):
    s_len = proj.shape[0]
    nc = s_len // BLK
    gw = HG_GROUP * HG_HEAD_DIM
    n_groups = WIDTH // gw
    base = 4 * WIDTH // gw
    heads_of = range(HG_GROUP)

    def body(q_ref, f_ref, i_ref, do_ref, lb_ref, dproj_in, dproj_ref, dlb_ref, mask_ref, st_ref, out_scr, sems):
        del dproj_in
        _hg_masks(mask_ref)
        row = _iota2((BLK, BLK), 0)
        col = _iota2((BLK, BLK), 1)
        lower_incl = (col <= row).astype(BF16)
        upper_incl = (col >= row).astype(BF16)
        lb_v = lb_ref[...]
        refs = (q_ref, f_ref, i_ref)

        def fwd_chunk(ci, sts):
            for h in heads_of:
                st_ref[ci, h] = sts[h]
            heads, bs = _hg_load(refs, pl.multiple_of(ci * BLK, BLK), lb_v, lower_incl)
            b_ends = [b[BLK - 1:BLK, :] for b in bs]
            k_decs = [((1.0 - hd[2]) * jnp.exp(b_end - b)).astype(BF16) for hd, b, b_end in zip(heads, bs, b_ends)]
            grown = [_dot_tn(hd[5].astype(BF16), k_dec) for hd, k_dec in zip(heads, k_decs)]
            return tuple(st * jnp.exp(b_end) + g for st, b_end, g in zip(sts, b_ends, grown))

        zero_state = (jnp.zeros((HG_HEAD_DIM, HG_HEAD_DIM), F32),) * HG_GROUP
        lax.fori_loop(0, nc, fwd_chunk, zero_state)

        def bwd_chunk(cc, carry):
            dsts, suffixes, dlbs = carry
            ci = nc - 1 - cc
            r0 = pl.multiple_of(ci * BLK, BLK)
            heads, bs = _hg_load(refs, r0, lb_v, lower_incl)
            qs = [hd[0] for hd in heads]
            fs = [hd[2] for hd in heads]
            ks = [1.0 - f for f in fs]
            vs = [hd[5] for hd in heads]
            vbs = [v.astype(BF16) for v in vs]
            dos = [do_ref[pl.ds(r0, BLK), h * HG_HEAD_DIM:(h + 1) * HG_HEAD_DIM] for h in heads_of]
            dobs = [do.astype(BF16) for do in dos]
            b_ends = [b[BLK - 1:BLK, :] for b in bs]
            e_qs = [jnp.exp(b) for b in bs]
            e_ks = [jnp.exp(b_end - b) for b, b_end in zip(bs, b_ends)]
            qes = [(q * e).astype(BF16) for q, e in zip(qs, e_qs)]
            khs = [(k * e).astype(BF16) for k, e in zip(ks, e_ks)]
            st_terms = [_split2_lanes(st_ref[ci, h]) for h in heads_of]
            ds_terms = [_split2_lanes(dst) for dst in dsts]
            dqes = [_dot(dob, t[:, :HG_HEAD_DIM]) + _dot(dob, t[:, HG_HEAD_DIM:]) for dob, t in zip(dobs, st_terms)]
            dkhs = [_dot(vb, t[:, :HG_HEAD_DIM]) + _dot(vb, t[:, HG_HEAD_DIM:]) for vb, t in zip(vbs, ds_terms)]
            dvs = [_dot_nt(kh, t[:, :HG_HEAD_DIM]) for kh, t in zip(khs, ds_terms)]
            grown = [_dot_tn(dob, qe) for dob, qe in zip(dobs, qes)]
            das = [_dot_nt(dob, vb) for dob, vb in zip(dobs, vbs)]
            dqs = [e * dqe for e, dqe in zip(e_qs, dqes)]
            dks = [e * dkh for e, dkh in zip(e_ks, dkhs)]
            dlogs = [qe.astype(F32) * dqe - kh.astype(F32) * dkh for qe, dqe, kh, dkh in zip(qes, dqes, khs, dkhs)]
            scs = [None] * HG_GROUP
            for v_idx, m in enumerate(HG_LEVELS):
                es, qms, kms = _hg_level_terms(qs, ks, bs, m)
                msk = mask_ref[v_idx]
                terms = [_dot_nt(qm, km) for qm, km in zip(qms, kms)]
                pms = [(da * msk).astype(BF16) for da in das]
                dqms = [_dot(pm, km) for pm, km in zip(pms, kms)]
                dkms = [_dot_tn(pm, qm) for pm, qm in zip(pms, qms)]
                scs = [t * msk if sc is None else sc + t * msk for sc, t in zip(scs, terms)]
                dqs = [dq + dqm * e for dq, dqm, e in zip(dqs, dqms, es)]
                dks = [dk + dkm * e for dk, dkm, e in zip(dks, dkms, es)]
                dlogs = [dl + (qm.astype(F32) * dqm - km.astype(F32) * dkm)
                         for dl, qm, dqm, km, dkm in zip(dlogs, qms, dqms, kms, dkms)]
            intras = [_dot_tn(sc.astype(BF16), dob) for sc, dob in zip(scs, dobs)]
            dgs = [_dot_01_l(upper_incl, dl) + sfx for dl, sfx in zip(dlogs, suffixes)]
            new_dlbs = []
            for h in heads_of:
                q, dq_fac, f, sig = heads[h][0], heads[h][1], heads[h][2], heads[h][3]
                a_diag = jnp.sum(dos[h] * vs[h], axis=-1, keepdims=True)
                s_diag = jnp.sum(q * ks[h], axis=-1, keepdims=True)
                dq = dqs[h] + a_diag * ks[h]
                dk = dks[h] + a_diag * q
                dv = dvs[h] + intras[h] + s_diag * dos[h]
                dfull = dgs[h] / f - dk
                sl = slice(h * HG_HEAD_DIM, (h + 1) * HG_HEAD_DIM)
                out_scr[0, pl.ds(r0, BLK), sl] = (dq * dq_fac).astype(BF16)
                out_scr[1, pl.ds(r0, BLK), sl] = (dfull * (1.0 - lb_v[:, sl]) * sig * (1.0 - sig)).astype(BF16)
                out_scr[2, pl.ds(r0, BLK), sl] = dv.astype(BF16)
                new_dlbs.append(dlbs[h] + jnp.sum(dfull * (1.0 - sig), axis=0, keepdims=True))
            new_dsts = tuple(dst * jnp.exp(b_end) + g for dst, b_end, g in zip(dsts, b_ends, grown))
            return new_dsts, tuple(dg[0:1, :] for dg in dgs), tuple(new_dlbs)

        zero_row = (jnp.zeros((1, HG_HEAD_DIM), F32),) * HG_GROUP
        _, _, dlbs = lax.fori_loop(0, nc, bwd_chunk, (zero_state, zero_row, zero_row))
        dlb_ref[...] = jnp.broadcast_to(jnp.concatenate(dlbs, axis=1), dlb_ref.shape)
        group = pl.program_id(0)
        copies = []
        for t in range(3):
            col = pl.multiple_of((base + t * n_groups + group) * gw, gw)
            copies.append(pltpu.make_async_copy(out_scr.at[t], dproj_ref.at[:, pl.ds(col, gw)], sems.at[t]))
            copies[-1].start()
        for cp in copies:
            cp.wait()

    col_spec = lambda off: pl.BlockSpec((s_len, gw), lambda h: (0, off + h))
    anyspec = pl.BlockSpec(memory_space=pl.ANY)
    return _pcall(
        body, name=name,
        out_shape=(jax.ShapeDtypeStruct(dproj.shape, dproj.dtype), jax.ShapeDtypeStruct((8, WIDTH), F32)),
        grid=(n_groups,),
        in_specs=[col_spec(base), col_spec(base + n_groups), col_spec(base + 2 * n_groups), col_spec(0),
                  pl.BlockSpec((1, gw), lambda h: (0, h)), anyspec],
        out_specs=(anyspec, pl.BlockSpec((8, gw), lambda h: (0, h))),
        scratch_shapes=[pltpu.VMEM((len(HG_LEVELS), BLK, BLK), F32),
                        pltpu.VMEM((nc, HG_GROUP, HG_HEAD_DIM, HG_HEAD_DIM), F32),
                        pltpu.VMEM((3, s_len, gw), BF16), pltpu.SemaphoreType.DMA((3,))],
        aliases={5: 0},
        semantics=("arbitrary",))(proj, proj, proj, do_b, lb, dproj)


def _dh_matmul(dproj, w_full, after, name):
    s_len, n = dproj.shape
    d = w_full.shape[0]
    tm = min(1024, s_len)
    tk = 4608

    def body(dp_ref, w_ref, after_ref, dh_ref):
        del after_ref
        part = _dot_nt(dp_ref[...], w_ref[...])

        @pl.when(pl.program_id(1) == 0)
        def _():
            dh_ref[...] = part

        @pl.when(pl.program_id(1) > 0)
        def _():
            dh_ref[...] += part

    return _pcall(
        body, name=name, out_shape=jax.ShapeDtypeStruct((s_len, d), F32),
        grid=(s_len // tm, n // tk),
        in_specs=[pl.BlockSpec((tm, tk), lambda i, k: (i, k)), pl.BlockSpec((d, tk), lambda i, k: (0, k)),
                  pl.BlockSpec(memory_space=pl.ANY)],
        out_specs=pl.BlockSpec((tm, d), lambda i, k: (i, 0)),
        semantics=("arbitrary", "arbitrary"))(dproj, w_full, after)


def _gw_matmul(h_t, dproj, name):
    d, s_len = h_t.shape
    n = dproj.shape[1]
    tn = 2304

    def body(ht_ref, dp_ref, gw_ref):
        gw_ref[...] = _dot(ht_ref[...], dp_ref[...]).astype(BF16)

    return _pcall(
        body, name=name, out_shape=jax.ShapeDtypeStruct((d, n), BF16),
        grid=(n // tn,),
        in_specs=[pl.BlockSpec((d, s_len), lambda j: (0, 0)), pl.BlockSpec((s_len, tn), lambda j: (0, j))],
        out_specs=pl.BlockSpec((d, tn), lambda j: (0, j)),
        semantics=("arbitrary",))(h_t, dproj)


def _ln_bwd(dh, x, scale, dres, name):
    s_len, d = x.shape
    tm = min(512, s_len)

    def body(dh_ref, x_ref, sc_ref, dres_ref, dx_ref, vec_ref):
        @pl.when(pl.program_id(0) == 0)
        def _():
            vec_ref[...] = jnp.zeros_like(vec_ref)

        dh = dh_ref[...]
        xs, rstd = _standardize(x_ref[...])
        vec_ref[0:1, :] += jnp.sum(dh, axis=0, keepdims=True)
        vec_ref[1:2, :] += jnp.sum(dh * xs, axis=0, keepdims=True)
        dx_ref[...] = _standardize_bwd(xs, rstd, dh * (1.0 + sc_ref[...])) + dres_ref[...]

    tile = pl.BlockSpec((tm, d), lambda i: (i, 0))
    return _pcall(body, name=name, grid=(s_len // tm,),
                  out_shape=(jax.ShapeDtypeStruct((s_len, d), F32), jax.ShapeDtypeStruct((8, d), F32)),
                  in_specs=[tile, tile, pl.BlockSpec((1, d), lambda i: (0, 0)), tile],
                  out_specs=(tile, pl.BlockSpec((8, d), lambda i: (0, 0))),
                  semantics=("arbitrary",))(dh, x, scale, dres)


def _wmod_grad(c_t, dmod):
    d = c_t.shape[0]
    n_layers, _, cm = dmod.shape

    def body(c_ref, dm_ref, o_ref):
        for l in range(n_layers):
            acc = None
            for b in range(NDEV):
                term = c_ref[:, b:b + 1] * dm_ref[l, b:b + 1, :]
                acc = term if acc is None else acc + term
            o_ref[l] = acc

    return _pcall(body, name="wmod_grad", out_shape=jax.ShapeDtypeStruct((n_layers, d, cm), F32))(c_t, dmod)


def _sum_adamw(parts_list, w, m, v, name):
    n_ranges = len(parts_list)
    n_src, range_rows, cols = parts_list[0].shape
    rows = range_rows * n_ranges
    tr = range_rows
    for cand in (512, 256, 128, 64, 32, 16, 8):
        if range_rows % cand == 0 and cand * cols * 4 <= (2 << 20):
            tr = cand
            break
    tiles = range_rows // tr

    def body(*refs):
        p_refs = refs[:n_ranges]
        w_ref, m_ref, v_ref, g_ref, d_ref, nm_ref, nv_ref = refs[n_ranges:]

        def step(p_ref):
            g = p_ref[0].astype(F32)
            for s in range(1, n_src):
                g = g + p_ref[s].astype(F32)
            nm = ADAM_B1 * m_ref[...] + (1.0 - ADAM_B1) * g
            nv = ADAM_B2 * v_ref[...] + (1.0 - ADAM_B2) * (g * g)
            m_hat = nm / (1.0 - ADAM_B1 ** ADAM_STEP)
            v_hat = nv / (1.0 - ADAM_B2 ** ADAM_STEP)
            g_ref[...] = g
            d_ref[...] = -ADAM_LR * (m_hat / (jnp.sqrt(v_hat) + ADAM_EPS) + ADAM_WD * w_ref[...])
            nm_ref[...] = nm
            nv_ref[...] = nv

        if n_ranges == 1:
            step(p_refs[0])
        else:
            for j in range(n_ranges):
                @pl.when(pl.program_id(0) // tiles == j)
                def _(j=j):
                    step(p_refs[j])

    def part_spec(j):
        return pl.BlockSpec((n_src, tr, cols), lambda i: (0, jnp.clip(i - j * tiles, 0, tiles - 1), 0))

    tile = pl.BlockSpec((tr, cols), lambda i: (i, 0))
    out = jax.ShapeDtypeStruct((rows, cols), F32)
    return _pcall(body, name=name, grid=(rows // tr,), out_shape=(out,) * 4,
                  in_specs=[part_spec(j) for j in range(n_ranges)] + [tile, tile, tile],
                  out_specs=(tile,) * 4, semantics=("arbitrary",))(*parts_list, w, m, v)


def _sum_parts(parts, name):
    n_src = parts.shape[0]

    def body(p_ref, o_ref):
        acc = p_ref[0]
        for s in range(1, n_src):
            acc = acc + p_ref[s]
        o_ref[...] = acc

    return _pcall(body, name=name, out_shape=jax.ShapeDtypeStruct(parts.shape[1:], F32))(parts)


def _pair_sum(gw, stage, me, name):
    d = gw.shape[0]
    n_slots, _, shard = stage.shape

    def body(me_ref, g_ref, s_ref, own_ref, o_ref):
        del me_ref
        total = (g_ref[...].astype(F32) + s_ref[0].astype(F32)).astype(BF16)
        o_ref[0] = total

        @pl.when(pl.program_id(0) == 0)
        def _():
            own_ref[0] = total

    slot = pl.BlockSpec((1, d, shard), lambda jj, me_ref: (jj, 0, 0))
    out = jax.ShapeDtypeStruct(stage.shape, BF16)
    return pl.pallas_call(
        body, name=name, out_shape=(out, out),
        grid_spec=pltpu.PrefetchScalarGridSpec(
            num_scalar_prefetch=1, grid=(n_slots,),
            in_specs=[pl.BlockSpec((d, shard), lambda jj, me_ref: (0, me_ref[0] ^ (2 * jj))), slot],
            out_specs=(pl.BlockSpec((1, d, shard), lambda jj, me_ref: (0, 0, 0)), slot)),
        compiler_params=pltpu.CompilerParams(dimension_semantics=("arbitrary",), vmem_limit_bytes=VMEM_LIMIT),
        interpret=False)(me.reshape(1).astype(jnp.int32), gw, stage)


def _lower_bound_table(lower_bounds):
    p = jax.nn.softmax(lower_bounds.astype(F32), axis=0)
    return jnp.cumsum(p, axis=0) - p[0:1]


def _pad_rows(v, width):
    n = v.shape[0]
    rows = -(-n // width)
    rows = -(-rows // 8) * 8
    return jnp.pad(v, (0, rows * width - n)).reshape(rows, width)


def kernel(x, c, w_mod, b_mod, w_in, conv_w, hgrn_norm_w, lower_bounds, w_branch, w_out, ln_g, ln_b, loss_target, m_w_mod, m_b_mod, m_w_in, m_conv_w, m_hgrn_norm_w, m_lower_bounds, m_w_branch, m_w_out, m_ln_g, m_ln_b, v_w_mod, v_b_mod, v_w_in, v_conv_w, v_hgrn_norm_w, v_lower_bounds, v_w_branch, v_w_out, v_ln_g, v_ln_b):
    n_layers = N_LAYERS
    s_len, d = x.shape[1], x.shape[2]
    n_cols = w_in.shape[2] * NDEV
    cw_cols = conv_w.shape[2]
    cm = w_mod.shape[2]
    me = _my_index()
    x0 = x[0]
    target = loss_target[0]

    small = _pad_rows(jnp.concatenate([c.reshape(-1), conv_w.reshape(-1)]), BLK)
    small_all = _all_gather_small("gather_c_conv", small).reshape(NDEV, -1)
    c_all = small_all[:, :d]
    conv_full = small_all[:, d:d + n_layers * 3 * cw_cols].reshape(NDEV, n_layers, 3, cw_cols)
    conv_full = conv_full.transpose(1, 2, 0, 3).reshape(n_layers, 3, WIDTH)

    b_mod_mine = lax.dynamic_slice_in_dim(b_mod, me * cm, cm, axis=1).reshape(n_layers, 1, cm)
    mod_cols = _mod_fwd(c_all, w_mod, b_mod_mine)
    mod_all = _all_gather_small("gather_mod", mod_cols.reshape(n_layers * NDEV, cm))
    mod_all = mod_all.reshape(NDEV, n_layers, NDEV, cm)
    mod_mine = lax.dynamic_index_in_dim(mod_all, me, axis=2, keepdims=False)
    mod_mine = mod_mine.transpose(1, 0, 2).reshape(n_layers, 3, 1, d)

    shard = w_in.shape[2]
    dsh = d // NDEV
    w_in_b, w_branch_b, w_out_b = w_in.astype(BF16), w_branch.astype(BF16), w_out.astype(BF16)
    window = lambda ref, dev: ref.at[:, pl.ds(pl.multiple_of(dev * shard, BLK), shard)]

    def two_step_sends(places):
        chips, sibling = [], []
        for k in (1, 2, 4, 6):
            for a, place in enumerate(places):
                chips.append((k, lambda ins, lands, me, a=a: ins[a],
                              lambda lands, me, a=a, place=place: place(lands[a], me),
                              lambda lands, me, a=a, k=k, place=place: place(lands[a], me ^ k)))
        for j in (2, 4, 6):
            for a, place in enumerate(places):
                sibling.append((1, lambda ins, lands, me, a=a, j=j, place=place: place(lands[a], me ^ j),
                                lambda lands, me, a=a, j=j, place=place: place(lands[a], me ^ j),
                                lambda lands, me, a=a, j=j, place=place: place(lands[a], me ^ 1 ^ j)))
        return chips, sibling

    in_sends = two_step_sends([window])
    rest_sends = two_step_sends([_slot, _slot])
    layer_sends = two_step_sends([window, _slot, _slot])

    def in_land(l):
        return _place_own_window(f"place_w_in_{l}", (d, n_cols), w_in_b[l], me)

    def rest_lands(l):
        return [_place_own((NDEV, 3, WIDTH, dsh), BF16, w_branch_b[l][None], (me, 0, 0, 0)),
                _place_own((NDEV, dsh, d), BF16, w_out_b[l][None], (me, 0, 0))]

    def gather_start(name, shards, lands, sends, after):
        return _exchange_start(f"{name}_chips_start", shards, lands, sends[0], after)

    def gather_pass_on(name, started, after, sends):
        _, lands = _exchange_wait(f"{name}_chips_wait", started, after, sends[0])
        return _exchange_start(f"{name}_sibling_start", [], lands, sends[1])

    def gather_finish(name, started, after, sends):
        return _exchange_wait(f"{name}_sibling_wait", started, after, sends[1])[1]

    def branch_out_weights(w_branch_l, w_out_l):
        return w_branch_l.transpose(1, 2, 0, 3).reshape(3, WIDTH, d), w_out_l.reshape(d, d)

    gathering = gather_start("gather_w_in_0", [w_in_b[0]], [in_land(0)], in_sends, mod_mine)
    passing = gather_pass_on("gather_w_in_0", gathering, gathering[4], in_sends)
    rest_gathering = gather_start("gather_rest_0", [w_branch_b[0], w_out_b[0]], rest_lands(0), rest_sends, passing[4])
    next_gathering = None
    if n_layers > 1:
        next_gathering = gather_start("gather_weights_1", [w_in_b[1], w_branch_b[1], w_out_b[1]],
                                      [in_land(1)] + rest_lands(1), layer_sends, rest_gathering[4])
    w_in_l = gather_finish("gather_w_in_0", passing, (next_gathering or rest_gathering)[4], in_sends)[0]

    lbs = _lower_bound_table(lower_bounds)
    norm_w4 = jnp.tile(hgrn_norm_w, (1, WIDTH // HG_HEAD_DIM))

    saved = []
    xl = x0
    for l in range(n_layers):
        shift, scale, gate = mod_mine[l, 0], mod_mine[l, 1], mod_mine[l, 2]
        proj, h_t = _ln_proj(xl, shift, scale, w_in_l, f"ln_proj_{l}")
        o_a, totals = _sb_fwd(proj, f"sb_fwd_{l}")
        if l == 0:
            rest_passing = gather_pass_on("gather_rest_0", rest_gathering, o_a, rest_sends)
        lb_l = lbs[l:l + 1] + rest_passing[4][0, 0] if l == 0 else lbs[l:l + 1]
        o_b = _hgrn_fwd(proj, lb_l, f"hgrn_fwd_{l}")
        if l == 0:
            wb_l, wo_l = branch_out_weights(*gather_finish("gather_rest_0", rest_passing, o_b, rest_sends))
            if n_layers > 1:
                next_passing = gather_pass_on("gather_weights_1", next_gathering, o_b, layer_sends)
                gate = gate + next_passing[4][0, 0]
        x_new, merged, ycat = _merge_fwd(xl, proj, o_a, o_b, gate, norm_w4[l:l + 1], conv_full[l],
                                         wb_l, wo_l, ln_g[l:l + 1], ln_b[l:l + 1], f"merge_fwd_{l}")
        saved.append((xl, proj, h_t, o_a, totals, o_b, merged, ycat, w_in_l, wb_l, wo_l))
        if l == 0 and n_layers > 1:
            w_in_l, w_branch_l, w_out_l = gather_finish("gather_weights_1", next_passing, x_new, layer_sends)
            wb_l, wo_l = branch_out_weights(w_branch_l, w_out_l)
        xl = x_new

    loss_part, dx = _loss_fwd_bwd(xl, target)
    loss = lax.psum(loss_part[0, 0], ("x", "y", "c"))

    pair_sends = [(1, lambda ins, lands, me, j=j: window(ins[0], me ^ 1 ^ j),
                   lambda lands, me, jj=jj: lands[0].at[jj], lambda lands, me, jj=jj: lands[0].at[jj])
                  for jj, j in enumerate((0, 2, 4, 6))]
    chip_sum_sends = [(j, lambda ins, lands, me, jj=jj: ins[0].at[jj],
                       lambda lands, me, jj=jj: lands[0].at[jj], lambda lands, me, jj=jj: lands[0].at[jj])
                      for jj, j in ((1, 2), (2, 4), (3, 6))]
    rest_scatter = _direct_sends([(0, 0, _slot, _slot), (1, 1, _slot, _slot)])
    scattering = [None] * n_layers
    small_grads = [None] * n_layers
    dmod = [None] * n_layers
    tie = None
    for l in reversed(range(n_layers)):
        xl, proj, h_t, o_a, totals, o_b, merged, ycat, w_in_l, wb_l, wo_l = saved[l]
        scale, gate = mod_mine[l, 1], mod_mine[l, 2]
        if tie is not None:
            gate = gate + tie[0, 0]
        dres, dycat, dproj, gwo_by_owner, gwb_by_owner, mvec = _merge_bwd(
            dx, xl, merged, ycat, proj, gate, wb_l, wo_l, ln_g[l:l + 1], f"merge_bwd_{l}")
        lands = [_place_own((NDEV, 3, WIDTH, dsh), BF16, lax.dynamic_slice_in_dim(gwb_by_owner, me, 1, axis=0),
                            (me, 0, 0, 0)),
                 _place_own((NDEV, dsh, d), BF16, lax.dynamic_slice_in_dim(gwo_by_owner, me, 1, axis=0),
                            (me, 0, 0))]
        rest_started = _exchange_start(f"scatter_rest_{l}_start", [gwb_by_owner, gwo_by_owner], lands, rest_scatter)
        dproj, do_a, do_b, bvec = _branch_bwd(dycat, proj, o_a, o_b, norm_w4[l:l + 1] + rest_started[4][0, 0],
                                              conv_full[l], dproj, f"branch_bwd_{l}")
        dproj = _sb_bwd(proj, do_a, totals, dproj, f"sb_bwd_{l}")
        dproj, dlb = _hgrn_bwd(proj, do_b, lbs[l:l + 1], dproj, f"hgrn_bwd_{l}")
        gwi = _gw_matmul(h_t, dproj, f"gw_matmul_{l}")
        swapping = _exchange_start(f"scatter_in_{l}_sibling_start", [gwi], [lax.empty((4, d, shard), BF16)], pair_sends)
        if l > 0:
            dh = _dh_matmul(dproj, w_in_l, swapping[4], f"dh_matmul_{l}")
        (gwi,), (stage,) = _exchange_wait(f"scatter_in_{l}_sibling_wait", swapping, dh if l > 0 else swapping[4],
                                          pair_sends)
        land, chip_sums = _pair_sum(gwi, stage, me, f"pair_sum_{l}")
        in_started = _exchange_start(f"scatter_in_{l}_chips_start", [chip_sums], [land], chip_sum_sends)
        scattering[l] = (in_started, rest_started)
        tie = in_started[4]
        if l == 0:
            dh = _dh_matmul(dproj, w_in_l, tie, f"dh_matmul_{l}")
        dx, lvec = _ln_bwd(dh, xl, scale + tie[0, 0], dres, f"ln_bwd_{l}")
        dmod[l] = jnp.concatenate([lvec[0], lvec[1], mvec[2]])
        norm_grad = bvec[0].reshape(WIDTH // HG_HEAD_DIM, HG_HEAD_DIM).sum(axis=0)
        small_grads[l] = jnp.concatenate([mvec[0], mvec[1], norm_grad, dlb[0], bvec[1:4].reshape(-1)])
    grad_x = dx[None]

    small_vec = jnp.concatenate(dmod + small_grads)
    n_small = small_vec.shape[0]
    small_all = _all_gather_small("gather_small_grads", _pad_rows(small_vec, BLK))
    small_sum = _sum_parts(small_all, "sum_small_grads").reshape(-1)[:n_small]
    dmod_all = small_all.reshape(NDEV, -1)[:, :n_layers * 3 * d].reshape(NDEV, n_layers, 3 * d)

    off = n_layers * 3 * d
    grad_b_mod = small_sum[:off].reshape(n_layers, 3 * d)
    per_layer = 2 * d + HG_HEAD_DIM + WIDTH + 3 * WIDTH
    g_ln_g, g_ln_b, g_norm, g_lbs, g_conv = [], [], [], [], []
    for l in range(n_layers):
        seg = small_sum[off + l * per_layer: off + (l + 1) * per_layer]
        g_ln_g.append(seg[:d])
        g_ln_b.append(seg[d:2 * d])
        g_norm.append(seg[2 * d:2 * d + HG_HEAD_DIM])
        g_lbs.append(seg[2 * d + HG_HEAD_DIM:2 * d + HG_HEAD_DIM + WIDTH])
        g_conv.append(seg[2 * d + HG_HEAD_DIM + WIDTH:].reshape(3, WIDTH))
    grad_ln_g, grad_ln_b = jnp.stack(g_ln_g), jnp.stack(g_ln_b)
    grad_norm = jnp.stack(g_norm)
    _, lbs_vjp = jax.vjp(_lower_bound_table, lower_bounds)
    grad_lower = lbs_vjp(jnp.stack(g_lbs))[0]
    grad_conv = lax.dynamic_slice_in_dim(jnp.stack(g_conv), me * cw_cols, cw_cols, axis=2)

    dmod_mine = lax.dynamic_slice_in_dim(dmod_all, me * cm, cm, axis=2).transpose(1, 0, 2)
    grad_w_mod = _wmod_grad(c_all.T, dmod_mine)

    p_in, p_branch, p_out = [None] * n_layers, [None] * n_layers, [None] * n_layers
    for l in reversed(range(n_layers)):
        in_started, rest_started = scattering[l]
        p_branch_l, p_out[l] = _exchange_wait(f"scatter_rest_{l}_wait", rest_started, grad_w_mod, rest_scatter)[1]
        p_branch[l] = p_branch_l.reshape(NDEV, 3 * WIDTH, dsh)
        p_in[l] = _exchange_wait(f"scatter_in_{l}_chips_wait", in_started, grad_w_mod, chip_sum_sends)[1][0]

    def adam(parts_list, w, m, v, name):
        shape = w.shape
        cols = shape[-1]
        flat = lambda a: a.reshape(-1, cols)
        outs = _sum_adamw(parts_list, flat(w), flat(m), flat(v), name)
        return [o.reshape(shape) for o in outs]

    r_w_in = adam(p_in, w_in, m_w_in, v_w_in, "adamw_w_in")
    r_w_branch = adam(p_branch, w_branch, m_w_branch, v_w_branch, "adamw_w_branch")
    r_w_out = adam(p_out, w_out, m_w_out, v_w_out, "adamw_w_out")
    r_w_mod = adam([grad_w_mod.reshape(1, -1, cm)], w_mod, m_w_mod, v_w_mod, "adamw_w_mod")

    small_names = ["b_mod", "conv_w", "hgrn_norm_w", "lower_bounds", "ln_g", "ln_b"]
    small_g = [grad_b_mod, grad_conv, grad_norm, grad_lower, grad_ln_g, grad_ln_b]
    small_w = [b_mod, conv_w, hgrn_norm_w, lower_bounds, ln_g, ln_b]
    small_m = [m_b_mod, m_conv_w, m_hgrn_norm_w, m_lower_bounds, m_ln_g, m_ln_b]
    small_v = [v_b_mod, v_conv_w, v_hgrn_norm_w, v_lower_bounds, v_ln_g, v_ln_b]
    pack = lambda arrs: _pad_rows(jnp.concatenate([a.reshape(-1) for a in arrs]), BLK)
    packed = _sum_adamw([pack(small_g)[None]], pack(small_w), pack(small_m), pack(small_v), "adamw_small")
    r_small = {n: [] for n in small_names}
    for res in packed:
        flat = res.reshape(-1)
        pos = 0
        for n, w in zip(small_names, small_w):
            r_small[n].append(flat[pos:pos + w.size].reshape(w.shape))
            pos += w.size

    results = {"w_mod": r_w_mod, "w_in": r_w_in, "w_branch": r_w_branch, "w_out": r_w_out, **r_small}
    order = ["w_mod", "b_mod", "w_in", "conv_w", "hgrn_norm_w", "lower_bounds", "w_branch", "w_out", "ln_g", "ln_b"]
    outs = [loss, grad_x]
    for idx in range(4):
        outs.extend(results[n][idx] for n in order)
    return tuple(outs)
```

```python
import jax
import jax.numpy as jnp
from jax import lax
from jax.experimental import pallas as pl
from jax.experimental.pallas import tpu as pltpu

F32 = jnp.float32
BF16 = jnp.bfloat16
NDEV = 8
N_LAYERS = 2
SB_HEAD_DIM = 64
HG_HEAD_DIM = 128
WIDTH = 512
BLK = 128
LN_EPS = 1e-5
RMS_EPS = 1e-6
ALPHA = (2.0 * N_LAYERS) ** 0.25
ADAM_LR, ADAM_B1, ADAM_B2, ADAM_EPS, ADAM_WD, ADAM_STEP = 0.001, 0.9, 0.999, 1e-08, 0.01, 10
VMEM_LIMIT = 56 * 1024 * 1024
MESH = pl.DeviceIdType.MESH
HG_LEVELS = (64, 32, 16, 8, 4, 2, 1)


def _pcall(body, *, name, out_shape, grid=None, in_specs=None, out_specs=None, scratch_shapes=(),
           semantics=None, aliases=None):
    kwargs = {}
    if grid is not None:
        kwargs["grid"] = grid
    if in_specs is not None:
        kwargs["in_specs"] = in_specs
    if out_specs is not None:
        kwargs["out_specs"] = out_specs
    if aliases:
        kwargs["input_output_aliases"] = aliases
    return pl.pallas_call(
        body, name=name, out_shape=out_shape, scratch_shapes=list(scratch_shapes),
        compiler_params=pltpu.CompilerParams(dimension_semantics=semantics, vmem_limit_bytes=VMEM_LIMIT),
        interpret=False, **kwargs)


def _dot(a, b):
    return jnp.dot(a, b, preferred_element_type=F32)


def _dot_nt(a, b):
    return lax.dot_general(a, b, (((1,), (1,)), ((), ())), preferred_element_type=F32)


def _dot_tn(a, b):
    return lax.dot_general(a, b, (((0,), (0,)), ((), ())), preferred_element_type=F32)


def _dot_01_l(m_bf16, x):
    x1 = x.astype(BF16)
    x2 = (x - x1.astype(F32)).astype(BF16)
    return _dot(jnp.concatenate([m_bf16, m_bf16], axis=1), jnp.concatenate([x1, x2], axis=0))


def _sigmoid(x):
    return 1.0 / (1.0 + jnp.exp(-x))


def _silu_and_grad(x):
    s = _sigmoid(x)
    return x * s, s * (1.0 + x * (1.0 - s))


LOG2E = 1.4426950408889634
MASKED_SCORE = -1e30


def _softplus2_parts(z2):
    minus_abs = lax.bitcast_convert_type(lax.bitcast_convert_type(z2, jnp.int32) | jnp.int32(-2 ** 31), F32)
    e = jnp.exp2(minus_abs)
    sp2 = jnp.maximum(z2, 0.0) + jnp.log2(1.0 + e)
    r = 1.0 / (1.0 + e)
    return sp2, jnp.where(z2 >= 0.0, r, e * r)


def _split2_lanes(x):
    x1 = x.astype(BF16)
    return jnp.concatenate([x1, (x - x1.astype(F32)).astype(BF16)], axis=1)


def _iota2(shape, dim):
    return lax.broadcasted_iota(jnp.int32, shape, dim)


def _standardize(x):
    mu = jnp.mean(x, axis=-1, keepdims=True)
    xc = x - mu
    var = jnp.mean(xc * xc, axis=-1, keepdims=True)
    rstd = lax.rsqrt(var + LN_EPS)
    return xc * rstd, rstd


def _standardize_bwd(xhat, rstd, dxhat):
    m1 = jnp.mean(dxhat, axis=-1, keepdims=True)
    m2 = jnp.mean(dxhat * xhat, axis=-1, keepdims=True)
    return rstd * (dxhat - m1 - xhat * m2)


def _my_index():
    return 4 * lax.axis_index("x") + 2 * lax.axis_index("y") + lax.axis_index("c")


def _exchange(name, ins, out_shapes, transfers, in_vmem):
    n_in, n_out, n_t = len(ins), len(out_shapes), len(transfers)

    def body(*refs):
        in_refs, out_refs = refs[:n_in], refs[n_in:n_in + n_out]
        send_sems, recv_sems, local_sems = refs[n_in + n_out:]
        x, y, c = lax.axis_index("x"), lax.axis_index("y"), lax.axis_index("c")
        me = 4 * x + 2 * y + c
        started = []
        for t, (i, o, src_fn, dst_fn) in enumerate(transfers):
            own = pltpu.make_async_copy(src_fn(in_refs[i], me), dst_fn(out_refs[o], me), local_sems.at[t])
            own.start()
            started.append(own)
        arrivals = []
        for k in range(1, NDEV):
            px = x ^ ((k >> 2) & 1)
            py = y ^ ((k >> 1) & 1)
            pc = c ^ (k & 1)
            peer = 4 * px + 2 * py + pc
            for t, (i, o, src_fn, dst_fn) in enumerate(transfers):
                sem = t * (NDEV - 1) + k - 1
                push = pltpu.make_async_remote_copy(
                    src_ref=src_fn(in_refs[i], peer), dst_ref=dst_fn(out_refs[o], me),
                    send_sem=send_sems.at[sem], recv_sem=recv_sems.at[sem],
                    device_id=(px, py, pc), device_id_type=MESH)
                push.start()
                started.append(push)
                arrivals.append(pltpu.make_async_remote_copy(
                    src_ref=src_fn(in_refs[i], peer), dst_ref=dst_fn(out_refs[o], peer),
                    send_sem=send_sems.at[sem], recv_sem=recv_sems.at[sem],
                    device_id=(px, py, pc), device_id_type=MESH))
        for arrival in arrivals:
            arrival.wait_recv()
        for cp in started[n_t:]:
            cp.wait_send()
        for own in started[:n_t]:
            own.wait()

    space = pltpu.VMEM if in_vmem else pl.ANY
    spec = pl.BlockSpec(memory_space=space)
    return _pcall(
        body, name=name, out_shape=out_shapes,
        in_specs=[spec] * n_in, out_specs=[spec] * n_out,
        scratch_shapes=[pltpu.SemaphoreType.DMA((n_t * (NDEV - 1),)),
                        pltpu.SemaphoreType.DMA((n_t * (NDEV - 1),)),
                        pltpu.SemaphoreType.DMA((n_t,))])(*ins)


def _whole(ref, dev):
    return ref


def _slot(ref, dev):
    return ref.at[dev]


def _all_gather_small(name, v):
    out = _exchange(name, [v], [jax.ShapeDtypeStruct((NDEV,) + v.shape, v.dtype)],
                    [(0, 0, _whole, _slot)], in_vmem=True)
    return out[0]


_HBM_SPEC = pl.BlockSpec(memory_space=pltpu.HBM)
_SEM_SPEC = pl.BlockSpec(memory_space=pltpu.SEMAPHORE)
_DATAFLOW = pltpu.SideEffectType.DATAFLOW_SIDE_EFFECTING


def _peer(x, y, c, k):
    px = x ^ ((k >> 2) & 1)
    py = y ^ ((k >> 1) & 1)
    pc = c ^ (k & 1)
    return (px, py, pc), 4 * px + 2 * py + pc


def _direct_sends(transfers):
    sends = []
    for k in range(1, NDEV):
        for i, o, src_fn, dst_fn in transfers:
            sends.append((k,
                          lambda ins, lands, me, i=i, k=k, src_fn=src_fn: src_fn(ins[i], me ^ k),
                          lambda lands, me, o=o, dst_fn=dst_fn: dst_fn(lands[o], me),
                          lambda lands, me, o=o, k=k, dst_fn=dst_fn: dst_fn(lands[o], me ^ k)))
    return sends


def _exchange_start(name, ins, lands, sends, after=None):
    n_in, n_buf = len(ins), len(ins) + len(lands)
    n_sem = len(sends)

    def body(*refs):
        in_refs, land_refs = refs[:n_in], refs[n_in:n_buf]
        n_skip = n_buf + (0 if after is None else 1)
        send_sems, recv_sems, token = refs[n_skip], refs[n_skip + 1], refs[-1]
        x, y, c = lax.axis_index("x"), lax.axis_index("y"), lax.axis_index("c")
        me = 4 * x + 2 * y + c
        for t, (k, src_fn, dst_fn, _) in enumerate(sends):
            pltpu.make_async_remote_copy(
                src_ref=src_fn(in_refs, land_refs, me), dst_ref=dst_fn(land_refs, me),
                send_sem=send_sems.at[t], recv_sem=recv_sems.at[t],
                device_id=_peer(x, y, c, k)[0], device_id_type=MESH).start()
        token[...] = jnp.zeros_like(token)

    bufs = [pltpu.with_memory_space_constraint(a, pltpu.HBM) for a in list(ins) + list(lands)]
    extra = [] if after is None else [after]
    outs = pl.pallas_call(
        body, name=name,
        out_shape=(pltpu.SemaphoreType.DMA((n_sem,)), pltpu.SemaphoreType.DMA((n_sem,)))
        + tuple(pltpu.HBM(a.shape, a.dtype) for a in bufs) + (jax.ShapeDtypeStruct((8, BLK), F32),),
        in_specs=[_HBM_SPEC] * n_buf + [pl.BlockSpec(memory_space=pl.ANY)] * len(extra),
        out_specs=(_SEM_SPEC, _SEM_SPEC) + (_HBM_SPEC,) * n_buf + (pl.BlockSpec(memory_space=pltpu.VMEM),),
        input_output_aliases={b: 2 + b for b in range(n_buf)},
        compiler_params=pltpu.CompilerParams(has_side_effects=_DATAFLOW),
        interpret=False)(*bufs, *extra)
    return outs[0], outs[1], list(outs[2:2 + n_in]), list(outs[2 + n_in:2 + n_buf]), outs[-1]


def _exchange_wait(name, started, after, sends):
    send_sems, recv_sems, ins, lands, _ = started
    n_in, n_buf = len(ins), len(ins) + len(lands)

    def body(*refs):
        in_refs, land_refs = refs[:n_in], refs[n_in:n_buf]
        send_sems, recv_sems = refs[n_buf], refs[n_buf + 1]
        x, y, c = lax.axis_index("x"), lax.axis_index("y"), lax.axis_index("c")
        me = 4 * x + 2 * y + c
        for t, (k, src_fn, _, rcv_fn) in enumerate(sends):
            cp = pltpu.make_async_remote_copy(
                src_ref=src_fn(in_refs, land_refs, me), dst_ref=rcv_fn(land_refs, me),
                send_sem=send_sems.at[t], recv_sem=recv_sems.at[t],
                device_id=_peer(x, y, c, k)[0], device_id_type=MESH)
            cp.wait_send()
            cp.wait_recv()

    bufs = list(ins) + list(lands)
    outs = pl.pallas_call(
        body, name=name, out_shape=tuple(pltpu.HBM(a.shape, a.dtype) for a in bufs),
        in_specs=[_HBM_SPEC] * n_buf + [_SEM_SPEC, _SEM_SPEC, pl.BlockSpec(memory_space=pl.ANY)],
        out_specs=(_HBM_SPEC,) * n_buf,
        input_output_aliases={b: b for b in range(n_buf)},
        compiler_params=pltpu.CompilerParams(has_side_effects=_DATAFLOW),
        interpret=False)(*bufs, send_sems, recv_sems, after)
    return list(outs[:n_in]), list(outs[n_in:])


def _place_own(shape, dtype, own, start):
    return lax.dynamic_update_slice(lax.empty(shape, dtype), own, start)


def _place_own_window(name, shape, own, me):
    rows, cols = own.shape

    def body(me_ref, zone_in, own_ref, zone_ref):
        del me_ref, zone_in
        zone_ref[...] = own_ref[...]

    return pl.pallas_call(
        body, name=name, out_shape=jax.ShapeDtypeStruct(shape, own.dtype),
        grid_spec=pltpu.PrefetchScalarGridSpec(
            num_scalar_prefetch=1, grid=(1,),
            in_specs=[pl.BlockSpec(memory_space=pl.ANY), pl.BlockSpec((rows, cols), lambda i, me_ref: (0, 0))],
            out_specs=pl.BlockSpec((rows, cols), lambda i, me_ref: (0, me_ref[0]))),
        input_output_aliases={1: 0},
        compiler_params=pltpu.CompilerParams(dimension_semantics=("arbitrary",), vmem_limit_bytes=VMEM_LIMIT),
        interpret=False)(me.reshape(1).astype(jnp.int32), lax.empty(shape, own.dtype), own)


def _mod_fwd(c_all, w_mod, b_mod_mine):
    n_layers, _, cm = w_mod.shape

    def body(c_ref, w_ref, b_ref, o_ref):
        for l in range(n_layers):
            o_ref[l] = jnp.dot(c_ref[...], w_ref[l], preferred_element_type=F32,
                               precision=lax.Precision.HIGHEST) + b_ref[l]

    return _pcall(body, name="mod_fwd", out_shape=jax.ShapeDtypeStruct((n_layers, NDEV, cm), F32))(
        c_all, w_mod, b_mod_mine)


def _ln_proj(x, shift, scale, w_full, name):
    s_len, d = x.shape
    n = w_full.shape[1]
    tm = min(1024, s_len)
    tn = 2304

    def body(x_ref, sh_ref, sc_ref, w_ref, proj_ref, ht_ref, h_scr):
        @pl.when(pl.program_id(1) == 0)
        def _():
            xs, _ = _standardize(x_ref[...])
            h = xs * (1.0 + sc_ref[...]) + sh_ref[...]
            h_scr[...] = h.astype(BF16)
            ht_ref[...] = h.T.astype(BF16)

        proj_ref[...] = _dot(h_scr[...], w_ref[...])

    return _pcall(
        body, name=name,
        out_shape=(jax.ShapeDtypeStruct((s_len, n), F32), jax.ShapeDtypeStruct((d, s_len), BF16)),
        grid=(s_len // tm, n // tn),
        in_specs=[pl.BlockSpec((tm, d), lambda i, j: (i, 0)),
                  pl.BlockSpec((1, d), lambda i, j: (0, 0)),
                  pl.BlockSpec((1, d), lambda i, j: (0, 0)),
                  pl.BlockSpec((d, tn), lambda i, j: (0, j))],
        out_specs=(pl.BlockSpec((tm, tn), lambda i, j: (i, j)),
                   pl.BlockSpec((d, tm), lambda i, j: (0, i))),
        scratch_shapes=[pltpu.VMEM((tm, d), BF16)],
        semantics=("arbitrary", "arbitrary"))(x, shift, scale, w_full)


SB_Q_ROWS = 256
SB_K_BLOCKS = 2


def _sb_fwd(proj, name):
    s_len = proj.shape[0]
    n_pairs = WIDTH // BLK
    qr = min(SB_Q_ROWS, s_len)
    gb = SB_K_BLOCKS
    kw = gb * BLK
    nq = s_len // qr
    assert qr == kw

    def body(q_ref, k_ref, v_ref, o_ref, tot_ref):
        lane = _iota2((1, BLK), 1)
        row = _iota2((BLK, BLK), 0)
        col = _iota2((BLK, BLK), 1)
        half = jnp.concatenate([(row >= col).astype(BF16), jnp.ones((BLK, BLK), BF16)], axis=1)
        suffix_and_sum = jnp.concatenate([half, half], axis=0)
        strict = _iota2((qr, kw), 1) < _iota2((qr, kw), 0)
        head_lanes = [(lane // SB_HEAD_DIM) == hh for hh in range(2)]

        def scores(gi, qms, masked):
            c0 = pl.multiple_of(gi * kw, kw)
            kb = k_ref[pl.ds(c0, kw), :].astype(BF16)
            z2s = [_dot_nt(qms[hh], kb) for hh in range(2)]
            if masked:
                z2s = [jnp.where(strict, z2, MASKED_SCORE) for z2 in z2s]
            return tuple(z2s)

        def accumulate(gi, z2s, carry):
            c0 = pl.multiple_of(gi * kw, kw)
            vf = v_ref[pl.ds(c0, kw), :]
            sp2s = [_softplus2_parts(z2)[0] for z2 in z2s]
            terms = [[_split2_lanes(sp2[:, b * BLK:(b + 1) * BLK]) for b in range(gb)] for sp2 in sp2s]
            sums = [[_dot(t, suffix_and_sum) for t in head_terms] for head_terms in terms]
            weights, laters = [], []
            for hh in range(2):
                later = carry[2 * hh + 1]
                parts = [None] * gb
                for b in reversed(range(gb)):
                    parts[b] = sums[hh][b][:, :BLK] + later
                    later = later + sums[hh][b][:, BLK:]
                weights.append(jnp.exp2(z2s[hh] - jnp.concatenate(parts, axis=1)).astype(BF16))
                laters.append(later)
            outs = [_dot(weights[hh], jnp.where(head_lanes[hh], vf, 0.0).astype(BF16)) for hh in range(2)]
            return (carry[0] + outs[0], laters[0], carry[2] + outs[1], laters[1])

        def queries(i):
            qf = q_ref[pl.ds(pl.multiple_of(i * qr, qr), qr), :] * (SB_HEAD_DIM ** -0.5 * LOG2E)
            return [jnp.where(head_lanes[hh], qf, 0.0).astype(BF16) for hh in range(2)]

        def qtile(i, first_scores):
            r0 = pl.multiple_of(i * qr, qr)
            qms = queries(i)
            zero = jnp.zeros((qr, BLK), F32)

            def step(jj, state):
                gi = i - 1 - jj
                return scores(gi, qms, False) + accumulate(gi + 1, state[:2], state[2:])

            state = lax.fori_loop(0, i, step, first_scores + (zero,) * 4)
            nxt = jnp.minimum(i + 1, nq - 1)
            next_scores = scores(nxt, queries(nxt), True)
            carry = accumulate(0, state[:2], state[2:])
            o_ref[pl.ds(r0, qr), :] = carry[0] + carry[2]
            tot_ref[0, pl.ds(r0, qr), :] = carry[1]
            tot_ref[1, pl.ds(r0, qr), :] = carry[3]
            return next_scores

        lax.fori_loop(0, nq, qtile, scores(0, queries(0), True))

    col_spec = lambda off: pl.BlockSpec((s_len, BLK), lambda p: (0, off + p))
    return _pcall(
        body, name=name,
        out_shape=(jax.ShapeDtypeStruct((s_len, WIDTH), F32),
                   jax.ShapeDtypeStruct((2 * n_pairs, s_len, BLK), F32)),
        grid=(n_pairs,),
        in_specs=[col_spec(0), col_spec(n_pairs), col_spec(2 * n_pairs)],
        out_specs=(pl.BlockSpec((s_len, BLK), lambda p: (0, p)),
                   pl.BlockSpec((2, s_len, BLK), lambda p: (p, 0, 0))),
        semantics=("arbitrary",))(proj, proj, proj)


def _hg_masks(mask_ref):
    row = _iota2((BLK, BLK), 0)
    col = _iota2((BLK, BLK), 1)
    for v, m in enumerate(HG_LEVELS):
        same = (row // (2 * m)) == (col // (2 * m))
        mask_ref[v] = (same & ((row & m) != 0) & ((col & m) == 0)).astype(F32)


def _hg_mid(b, m):
    if m >= 4:
        n = BLK // (2 * m)
        mid = b.reshape(n, 2 * m, BLK)[:, m - 1:m, :]
        return jnp.broadcast_to(mid, (n, 2 * m, BLK)).reshape(BLK, BLK)
    pos = _iota2((BLK, BLK), 0) & (2 * m - 1)
    out = b
    for p in range(2 * m):
        delta = (m - 1) - p
        if delta != 0:
            out = jnp.where(pos == p, pltpu.roll(b, (-delta) % BLK, 0), out)
    return out


def _hg_chunk_inputs(qraw, fpre, lb):
    sig = _sigmoid(fpre)
    f = lb + (1.0 - lb) * sig
    g = jnp.log(f)
    q, dq_fac = _silu_and_grad(qraw)
    return q, dq_fac, f, sig, g


HG_GROUP = 4


def _neg_abs(x):
    return lax.bitcast_convert_type(lax.bitcast_convert_type(x, jnp.int32) | jnp.int32(-2 ** 31), F32)


def _hg_level_terms(qs, ks, bs, m):
    es = [jnp.exp(_neg_abs(b - _hg_mid(b, m))) for b in bs]
    qts = [(q * e).astype(BF16) for q, e in zip(qs, es)]
    kts = [(k * e).astype(BF16) for k, e in zip(ks, es)]
    return es, qts, kts


def _hg_load(refs, r0, lb_v, lower_incl):
    q_ref, f_ref, i_ref = refs
    heads = []
    for h in range(HG_GROUP):
        sl = slice(h * HG_HEAD_DIM, (h + 1) * HG_HEAD_DIM)
        heads.append(_hg_chunk_inputs(q_ref[pl.ds(r0, BLK), sl], f_ref[pl.ds(r0, BLK), sl], lb_v[:, sl])
                     + (i_ref[pl.ds(r0, BLK), sl],))
    bs = [_dot_01_l(lower_incl, hd[4]) for hd in heads]
    return heads, bs


def _hgrn_fwd(proj, lb, name):
    s_len = proj.shape[0]
    nc = s_len // BLK
    gw = HG_GROUP * HG_HEAD_DIM
    n_groups = WIDTH // gw
    base = 4 * WIDTH // gw

    def body(q_ref, f_ref, i_ref, lb_ref, o_ref, mask_ref):
        _hg_masks(mask_ref)
        row = _iota2((BLK, BLK), 0)
        col = _iota2((BLK, BLK), 1)
        lower_incl = (col <= row).astype(BF16)
        lb_v = lb_ref[...]

        def chunk(ci, sts):
            r0 = pl.multiple_of(ci * BLK, BLK)
            heads, bs = _hg_load((q_ref, f_ref, i_ref), r0, lb_v, lower_incl)
            qs = [hd[0] for hd in heads]
            ks = [1.0 - hd[2] for hd in heads]
            vs = [hd[5] for hd in heads]
            vbs = [v.astype(BF16) for v in vs]
            b_ends = [b[BLK - 1:BLK, :] for b in bs]
            inters = [_dot_nt((q * jnp.exp(b)).astype(BF16), st.astype(BF16)) for q, b, st in zip(qs, bs, sts)]
            scs = [None] * HG_GROUP
            for v_idx, m in enumerate(HG_LEVELS):
                _, qts, kts = _hg_level_terms(qs, ks, bs, m)
                terms = [_dot_nt(qt, kt) for qt, kt in zip(qts, kts)]
                msk = mask_ref[v_idx]
                scs = [t * msk if sc is None else sc + t * msk for sc, t in zip(scs, terms)]
            intras = [_dot(sc.astype(BF16), vb) for sc, vb in zip(scs, vbs)]
            k_decs = [(k * jnp.exp(b_end - b)).astype(BF16) for k, b, b_end in zip(ks, bs, b_ends)]
            grown = [_dot_tn(vb, k_dec) for vb, k_dec in zip(vbs, k_decs)]
            for h in range(HG_GROUP):
                diag = jnp.sum(qs[h] * ks[h], axis=-1, keepdims=True)
                o_ref[pl.ds(r0, BLK), h * HG_HEAD_DIM:(h + 1) * HG_HEAD_DIM] = inters[h] + intras[h] + diag * vs[h]
            return tuple(st * jnp.exp(b_end) + g for st, b_end, g in zip(sts, b_ends, grown))

        lax.fori_loop(0, nc, chunk, (jnp.zeros((HG_HEAD_DIM, HG_HEAD_DIM), F32),) * HG_GROUP)

    col_spec = lambda off: pl.BlockSpec((s_len, gw), lambda h: (0, off + h))
    return _pcall(
        body, name=name, out_shape=jax.ShapeDtypeStruct((s_len, WIDTH), F32),
        grid=(n_groups,),
        in_specs=[col_spec(base), col_spec(base + n_groups), col_spec(base + 2 * n_groups),
                  pl.BlockSpec((1, gw), lambda h: (0, h))],
        out_specs=pl.BlockSpec((s_len, gw), lambda h: (0, h)),
        scratch_shapes=[pltpu.VMEM((len(HG_LEVELS), BLK, BLK), F32)],
        semantics=("arbitrary",))(proj, proj, proj, lb)


def _rms_heads(o_b, norm_w):
    n_parts, h_parts, r_parts = [], [], []
    for h in range(WIDTH // HG_HEAD_DIM):
        sl = slice(h * HG_HEAD_DIM, (h + 1) * HG_HEAD_DIM)
        o = o_b[:, sl]
        rstd = lax.rsqrt(jnp.mean(o * o, axis=-1, keepdims=True) + RMS_EPS)
        ohat = o * rstd
        h_parts.append(ohat)
        n_parts.append(ohat * norm_w[:, sl])
        r_parts.append(jnp.broadcast_to(rstd, o.shape))
    cat = lambda parts: jnp.concatenate(parts, axis=-1)
    return cat(n_parts), cat(h_parts), cat(r_parts)


def _shift_rows_down(halo, cur, k):
    tm = cur.shape[0]
    ext = jnp.concatenate([halo, cur], axis=0)
    return pltpu.roll(ext, k, 0)[8:8 + tm]


def _shift_rows_up(cur, halo, k):
    tm = cur.shape[0]
    ext = jnp.concatenate([cur, halo], axis=0)
    return pltpu.roll(ext, (tm + 8 - k) % (tm + 8), 0)[0:tm]


def _merge_fwd(x, proj, o_a, o_b, gate, norm_w, conv_w, wb, w_out, ln_g, ln_b, name):
    s_len, d = x.shape
    tm = min(256, s_len)
    hb = tm // 8

    def body(x_ref, oa_ref, za_ref, ob_ref, zb_ref, pre_ref, post_ref, u_ref, zc_ref, hpre_ref, hu_ref, g_ref,
             gate_ref, nw_ref, cw_ref, wb_ref, wo_ref, lg_ref, lbias_ref, xn_ref, mg_ref, yc_ref):
        i = pl.program_id(0)
        sa, _ = _silu_and_grad(za_ref[...])
        y_a = (oa_ref[...] * sa).astype(BF16)
        n_b, _, _ = _rms_heads(ob_ref[...], nw_ref[...])
        sb, _ = _silu_and_grad(zb_ref[...])
        y_b = (n_b * sb).astype(BF16)
        a = pre_ref[...] * u_ref[...]
        halo = jnp.where(i > 0, hpre_ref[...] * hu_ref[...], 0.0)
        cw = cw_ref[...]
        conv = cw[0:1] * _shift_rows_down(halo, a, 2) + cw[1:2] * _shift_rows_down(halo, a, 1) + cw[2:3] * a
        sc, _ = _silu_and_grad(zc_ref[...])
        y_c = (post_ref[...] * conv * sc).astype(BF16)
        merged = None
        for k, yk in enumerate((y_a, y_b, y_c)):
            yc_ref[:, k * WIDTH:(k + 1) * WIDTH] = yk
            term = _sigmoid(g_ref[:, k * d:(k + 1) * d]) * _dot(yk, wb_ref[k])
            merged = term if merged is None else merged + term
        mb = merged.astype(BF16)
        mg_ref[...] = mb
        y = _dot(mb, wo_ref[...])
        r = ALPHA * x_ref[...] + (1.0 + gate_ref[...]) * y
        rhat, _ = _standardize(r)
        xn_ref[...] = rhat * lg_ref[...] + lbias_ref[...]

    wcol = lambda cb: pl.BlockSpec((tm, WIDTH), lambda i: (i, cb))
    halo_spec = lambda cb: pl.BlockSpec((8, WIDTH), lambda i: (jnp.maximum(i * hb - 1, 0), cb))
    vec = lambda w: pl.BlockSpec((1, w), lambda i: (0, 0))
    return _pcall(
        body, name=name,
        out_shape=(jax.ShapeDtypeStruct((s_len, d), F32), jax.ShapeDtypeStruct((s_len, d), BF16),
                   jax.ShapeDtypeStruct((s_len, 3 * WIDTH), BF16)),
        grid=(s_len // tm,),
        in_specs=[pl.BlockSpec((tm, d), lambda i: (i, 0)),
                  wcol(0), wcol(3), wcol(0), wcol(7), wcol(8), wcol(9), wcol(10), wcol(11),
                  halo_spec(8), halo_spec(10),
                  pl.BlockSpec((tm, 3 * d), lambda i: (i, 2)),
                  vec(d), vec(WIDTH),
                  pl.BlockSpec((3, WIDTH), lambda i: (0, 0)),
                  pl.BlockSpec((3, WIDTH, d), lambda i: (0, 0, 0)),
                  pl.BlockSpec((d, d), lambda i: (0, 0)),
                  vec(d), vec(d)],
        out_specs=(pl.BlockSpec((tm, d), lambda i: (i, 0)), pl.BlockSpec((tm, d), lambda i: (i, 0)),
                   pl.BlockSpec((tm, 3 * WIDTH), lambda i: (i, 0))),
        semantics=("arbitrary",))(x, o_a, proj, o_b, proj, proj, proj, proj, proj, proj, proj, proj,
                                  gate, norm_w, conv_w, wb, w_out, ln_g, ln_b)


def _loss_fwd_bwd(y, target):
    s_len, d = y.shape
    tm = min(512, s_len)

    def body(y_ref, t_ref, loss_ref, dy_ref):
        @pl.when(pl.program_id(0) == 0)
        def _():
            loss_ref[...] = jnp.zeros_like(loss_ref)

        e = y_ref[...] - t_ref[...]
        dy_ref[...] = e * (1.0 / d)
        part = jnp.sum(jnp.sum(e * e, axis=-1, keepdims=True), axis=0, keepdims=True)
        loss_ref[...] += part * (0.5 / d)

    tile = pl.BlockSpec((tm, d), lambda i: (i, 0))
    return _pcall(body, name="loss", grid=(s_len // tm,),
                  out_shape=(jax.ShapeDtypeStruct((1, 1), F32), jax.ShapeDtypeStruct((s_len, d), F32)),
                  in_specs=[tile, tile],
                  out_specs=(pl.BlockSpec((1, 1), lambda i: (0, 0)), tile),
                  semantics=("arbitrary",))(y, target)


def _merge_bwd(dxn, x, merged, ycat, proj, gate, wb, w_out, ln_g, name):
    s_len, d = x.shape
    tm = min(256, s_len)
    dsh = d // NDEV
    n_tiles = s_len // tm

    def body(dxn_ref, x_ref, mg_ref, yc_ref, g_ref, gate_ref, wb_ref, wo_ref, lg_ref,
             dres_ref, dyc_ref, dg_ref, gwo_out, gwb_out, vec_ref, gwo_ref, gwb_ref):
        @pl.when(pl.program_id(0) == 0)
        def _():
            gwo_ref[...] = jnp.zeros_like(gwo_ref)
            gwb_ref[...] = jnp.zeros_like(gwb_ref)
            vec_ref[...] = jnp.zeros_like(vec_ref)

        mb = mg_ref[...]
        one_gate = 1.0 + gate_ref[...]
        y = _dot(mb, wo_ref[...])
        r = ALPHA * x_ref[...] + one_gate * y
        rhat, rstd = _standardize(r)
        dxn = dxn_ref[...]
        dr = _standardize_bwd(rhat, rstd, dxn * lg_ref[...])
        vec_ref[0:1, :] += jnp.sum(dxn * rhat, axis=0, keepdims=True)
        vec_ref[1:2, :] += jnp.sum(dxn, axis=0, keepdims=True)
        vec_ref[2:3, :] += jnp.sum(dr * y, axis=0, keepdims=True)
        dres_ref[...] = ALPHA * dr
        dy = (one_gate * dr).astype(BF16)
        gwo_ref[...] += _dot_tn(mb, dy)
        dmerged = _dot_nt(dy, wo_ref[...])
        for k in range(3):
            yk = yc_ref[:, k * WIDTH:(k + 1) * WIDTH]
            sg = _sigmoid(g_ref[:, k * d:(k + 1) * d])
            pk = _dot(yk, wb_ref[k])
            dg_ref[:, k * d:(k + 1) * d] = (dmerged * pk * sg * (1.0 - sg)).astype(BF16)
            dpk = (dmerged * sg).astype(BF16)
            dyc_ref[:, k * WIDTH:(k + 1) * WIDTH] = _dot_nt(dpk, wb_ref[k])
            gwb_ref[k] += _dot_tn(yk, dpk)

        @pl.when(pl.program_id(0) == n_tiles - 1)
        def _():
            for o in range(NDEV):
                gwo_out[o] = gwo_ref[o * dsh:(o + 1) * dsh, :].astype(BF16)
                for k in range(3):
                    gwb_out[o, k] = gwb_ref[k, :, o * dsh:(o + 1) * dsh].astype(BF16)

    tile = lambda w: pl.BlockSpec((tm, w), lambda i: (i, 0))
    vec = pl.BlockSpec((1, d), lambda i: (0, 0))
    return _pcall(
        body, name=name,
        out_shape=(jax.ShapeDtypeStruct((s_len, d), F32), jax.ShapeDtypeStruct((s_len, 3 * WIDTH), F32),
                   jax.ShapeDtypeStruct(proj.shape, BF16), jax.ShapeDtypeStruct((NDEV, dsh, d), BF16),
                   jax.ShapeDtypeStruct((NDEV, 3, WIDTH, dsh), BF16), jax.ShapeDtypeStruct((8, d), F32)),
        grid=(n_tiles,),
        in_specs=[tile(d), tile(d), tile(d), tile(3 * WIDTH),
                  pl.BlockSpec((tm, 3 * d), lambda i: (i, 2)),
                  vec, pl.BlockSpec((3, WIDTH, d), lambda i: (0, 0, 0)),
                  pl.BlockSpec((d, d), lambda i: (0, 0)), vec],
        out_specs=(tile(d), tile(3 * WIDTH), pl.BlockSpec((tm, 3 * d), lambda i: (i, 2)),
                   pl.BlockSpec((NDEV, dsh, d), lambda i: (0, 0, 0)),
                   pl.BlockSpec((NDEV, 3, WIDTH, dsh), lambda i: (0, 0, 0, 0)),
                   pl.BlockSpec((8, d), lambda i: (0, 0))),
        scratch_shapes=[pltpu.VMEM((d, d), F32), pltpu.VMEM((3, WIDTH, d), F32)],
        semantics=("arbitrary",))(dxn, x, merged, ycat, proj, gate, wb, w_out, ln_g)


def _branch_bwd(dycat, proj, o_a, o_b, norm_w, conv_w, dproj, name):
    s_len = proj.shape[0]
    tm = min(256, s_len)
    hb = tm // 8
    n_tiles = s_len // tm

    def body(dya_ref, dyb_ref, dyc_ref, oa_ref, za_ref, ob_ref, zb_ref, pre_ref, post_ref, u_ref, zc_ref,
             hpre_ref, hu_ref, ndyc_ref, npost_ref, nzc_ref, nw_ref, cw_ref, dproj_in,
             dproj_ref, doa_ref, dob_ref, vec_ref, dza_scr, dzb_scr, dc_scr, sems):
        del dproj_in
        i = pl.program_id(0)

        @pl.when(i == 0)
        def _():
            vec_ref[...] = jnp.zeros_like(vec_ref)

        sa, dsa = _silu_and_grad(za_ref[...])
        dya = dya_ref[...]
        doa_ref[...] = dya * sa
        dza_scr[...] = (dya * oa_ref[...] * dsa).astype(BF16)
        nw = nw_ref[...]
        n_b, ohat, rstd = _rms_heads(ob_ref[...], nw)
        sb, dsb = _silu_and_grad(zb_ref[...])
        dyb = dyb_ref[...]
        dzb_scr[...] = (dyb * n_b * dsb).astype(BF16)
        dn = dyb * sb
        vec_ref[0:1, :] += jnp.sum(dn * ohat, axis=0, keepdims=True)
        dnw = dn * nw
        parts = []
        for h in range(WIDTH // HG_HEAD_DIM):
            sl = slice(h * HG_HEAD_DIM, (h + 1) * HG_HEAD_DIM)
            m2 = jnp.mean(dnw[:, sl] * ohat[:, sl], axis=-1, keepdims=True)
            parts.append(rstd[:, sl] * (dnw[:, sl] - ohat[:, sl] * m2))
        dob_ref[...] = jnp.concatenate(parts, axis=-1)
        cw = cw_ref[...]
        pre, u, post = pre_ref[...], u_ref[...], post_ref[...]
        a = pre * u
        halo = jnp.where(i > 0, hpre_ref[...] * hu_ref[...], 0.0)
        a1 = _shift_rows_down(halo, a, 1)
        a2 = _shift_rows_down(halo, a, 2)
        conv = cw[0:1] * a2 + cw[1:2] * a1 + cw[2:3] * a
        sc, dsc = _silu_and_grad(zc_ref[...])
        dyc = dyc_ref[...]
        dconv = dyc * post * sc
        nsc, _ = _silu_and_grad(nzc_ref[...])
        nxt = jnp.where(i < n_tiles - 1, ndyc_ref[...] * npost_ref[...] * nsc, 0.0)
        da = cw[2:3] * dconv + cw[1:2] * _shift_rows_up(dconv, nxt, 1) + cw[0:1] * _shift_rows_up(dconv, nxt, 2)
        dc_scr[:, 0 * WIDTH:1 * WIDTH] = (da * u).astype(BF16)
        dc_scr[:, 1 * WIDTH:2 * WIDTH] = (dyc * conv * sc).astype(BF16)
        dc_scr[:, 2 * WIDTH:3 * WIDTH] = (da * pre).astype(BF16)
        dc_scr[:, 3 * WIDTH:4 * WIDTH] = (dyc * post * conv * dsc).astype(BF16)
        vec_ref[1:2, :] += jnp.sum(dconv * a2, axis=0, keepdims=True)
        vec_ref[2:3, :] += jnp.sum(dconv * a1, axis=0, keepdims=True)
        vec_ref[3:4, :] += jnp.sum(dconv * a, axis=0, keepdims=True)
        rows = pl.ds(pl.multiple_of(i * tm, tm), tm)
        copies = [pltpu.make_async_copy(dza_scr, dproj_ref.at[rows, 3 * WIDTH:4 * WIDTH], sems.at[0]),
                  pltpu.make_async_copy(dzb_scr, dproj_ref.at[rows, 7 * WIDTH:8 * WIDTH], sems.at[1]),
                  pltpu.make_async_copy(dc_scr, dproj_ref.at[rows, 8 * WIDTH:12 * WIDTH], sems.at[2])]
        for cp in copies:
            cp.start()
        for cp in copies:
            cp.wait()

    wcol = lambda cb: pl.BlockSpec((tm, WIDTH), lambda i: (i, cb))
    prev = lambda cb: pl.BlockSpec((8, WIDTH), lambda i: (jnp.maximum(i * hb - 1, 0), cb))
    nxt = lambda cb: pl.BlockSpec((8, WIDTH), lambda i: (jnp.minimum((i + 1) * hb, s_len // 8 - 1), cb))
    anyspec = pl.BlockSpec(memory_space=pl.ANY)
    out = jax.ShapeDtypeStruct((s_len, WIDTH), F32)
    return _pcall(
        body, name=name,
        out_shape=(jax.ShapeDtypeStruct(dproj.shape, dproj.dtype), out, out, jax.ShapeDtypeStruct((8, WIDTH), F32)),
        grid=(n_tiles,),
        in_specs=[wcol(0), wcol(1), wcol(2), wcol(0), wcol(3), wcol(0), wcol(7), wcol(8), wcol(9), wcol(10), wcol(11),
                  prev(8), prev(10), nxt(2), nxt(9), nxt(11),
                  pl.BlockSpec((1, WIDTH), lambda i: (0, 0)), pl.BlockSpec((3, WIDTH), lambda i: (0, 0)), anyspec],
        out_specs=(anyspec, wcol(0), wcol(0), pl.BlockSpec((8, WIDTH), lambda i: (0, 0))),
        scratch_shapes=[pltpu.VMEM((tm, WIDTH), BF16), pltpu.VMEM((tm, WIDTH), BF16),
                        pltpu.VMEM((tm, 4 * WIDTH), BF16), pltpu.SemaphoreType.DMA((3,))],
        aliases={18: 0},
        semantics=("arbitrary",))(dycat, dycat, dycat, o_a, proj, o_b, proj, proj, proj, proj, proj,
                                  proj, proj, dycat, proj, proj, norm_w, conv_w, dproj)


def _sb_bwd(proj, do_a, totals, dproj, name):
    s_len = proj.shape[0]
    n_pairs = WIDTH // BLK
    scale = SB_HEAD_DIM ** -0.5
    qr = min(SB_Q_ROWS, s_len)
    gb = SB_K_BLOCKS
    kw = gb * BLK
    nq = s_len // qr
    assert qr == kw

    def body(q_ref, k_ref, v_ref, do_ref, tot_ref, dproj_in, dproj_ref, dq_ref, dk_ref, dv_ref, out_scr, sems):
        del dproj_in
        lane = _iota2((1, BLK), 1)
        row = _iota2((BLK, BLK), 0)
        col = _iota2((BLK, BLK), 1)
        ones = jnp.ones((BLK, BLK), BF16)
        twice = lambda m: jnp.concatenate([m, m], axis=0)
        before_and_sum = twice(jnp.concatenate([(row < col).astype(BF16), ones], axis=1))
        upto_and_sum = twice(jnp.concatenate([(row <= col).astype(BF16), ones], axis=1))
        strict = _iota2((qr, kw), 1) < _iota2((qr, kw), 0)
        head_lanes = [(lane // SB_HEAD_DIM) == hh for hh in range(2)]
        dk_ref[...] = jnp.zeros_like(dk_ref)
        dv_ref[...] = jnp.zeros_like(dv_ref)

        def scores(gi, qms, masked):
            c0 = pl.multiple_of(gi * kw, kw)
            kb = k_ref[pl.ds(c0, kw), :].astype(BF16)
            z2s = [_dot_nt(qms[hh], kb) for hh in range(2)]
            if masked:
                z2s = [jnp.where(strict, z2, MASKED_SCORE) for z2 in z2s]
            return tuple(z2s)

        def process(gi, z2s, qms, doms, totals_i, carry):
            c0 = pl.multiple_of(gi * kw, kw)
            kf = k_ref[pl.ds(c0, kw), :]
            vf = v_ref[pl.ds(c0, kw), :]
            kms = [jnp.where(head_lanes[hh], kf, 0.0).astype(BF16) for hh in range(2)]
            vms = [jnp.where(head_lanes[hh], vf, 0.0).astype(BF16) for hh in range(2)]
            das = [_dot_nt(doms[hh], vms[hh]) for hh in range(2)]
            halves = [_softplus2_parts(z2) for z2 in z2s]
            terms = [[_split2_lanes(sp2[:, b * BLK:(b + 1) * BLK]) for b in range(gb)] for sp2, _ in halves]
            sums = [[_dot(t, before_and_sum) for t in head_terms] for head_terms in terms]
            weights, gmats, l_befores = [], [], []
            for hh in range(2):
                l_before = carry[3 * hh + 1]
                parts = []
                for b in range(gb):
                    parts.append(totals_i[hh] - l_before - sums[hh][b][:, :BLK])
                    l_before = l_before + sums[hh][b][:, BLK:]
                a = jnp.exp2(z2s[hh] - jnp.concatenate(parts, axis=1))
                weights.append(a.astype(BF16))
                gmats.append(a * das[hh])
                l_befores.append(l_before)
            terms = [[_split2_lanes(g[:, b * BLK:(b + 1) * BLK]) for b in range(gb)] for g in gmats]
            sums = [[_dot(t, upto_and_sum) for t in head_terms] for head_terms in terms]
            dzs, g_befores = [], []
            for hh in range(2):
                g_before = carry[3 * hh + 2]
                parts = []
                for b in range(gb):
                    parts.append(g_before + sums[hh][b][:, :BLK])
                    g_before = g_before + sums[hh][b][:, BLK:]
                dzs.append((gmats[hh] - halves[hh][1] * jnp.concatenate(parts, axis=1)).astype(BF16))
                g_befores.append(g_before)
            dk_t = _dot_tn(jnp.concatenate(qms, axis=0), jnp.concatenate(dzs, axis=0))
            dv_t = _dot_tn(jnp.concatenate(doms, axis=0), jnp.concatenate(weights, axis=0))
            dqs = [_dot(dzs[hh], kms[hh]) for hh in range(2)]
            dk_ref[:, pl.ds(c0, kw)] += dk_t * (1.0 / LOG2E)
            dv_ref[:, pl.ds(c0, kw)] += dv_t
            return (carry[0] + dqs[0], l_befores[0], g_befores[0], carry[3] + dqs[1], l_befores[1], g_befores[1])

        def queries(i):
            qf = q_ref[pl.ds(pl.multiple_of(i * qr, qr), qr), :] * (scale * LOG2E)
            return [jnp.where(head_lanes[hh], qf, 0.0).astype(BF16) for hh in range(2)]

        def qtile(i, first_scores):
            r0 = pl.multiple_of(i * qr, qr)
            qms = queries(i)
            dof = do_ref[pl.ds(r0, qr), :]
            doms = [jnp.where(head_lanes[hh], dof, 0.0).astype(BF16) for hh in range(2)]
            totals_i = [tot_ref[hh, pl.ds(r0, qr), :] for hh in range(2)]
            zero = jnp.zeros((qr, BLK), F32)

            def step(gi, state):
                return scores(gi + 1, qms, False) + process(gi, state[:2], qms, doms, totals_i, state[2:])

            def before_diagonal(state):
                return scores(i, qms, True) + process(i - 1, state[:2], qms, doms, totals_i, state[2:])

            state = lax.fori_loop(0, i - 1, step, first_scores + (zero,) * 6)
            state = lax.cond(i > 0, before_diagonal, lambda st: st, state)
            nxt = jnp.minimum(i + 1, nq - 1)
            next_scores = scores(0, queries(nxt), False)
            carry = process(i, state[:2], qms, doms, totals_i, state[2:])
            dq_ref[pl.ds(r0, qr), :] = (carry[0] + carry[3]) * scale
            return next_scores

        lax.fori_loop(0, nq, qtile, scores(0, queries(0), True))
        pair = pl.program_id(0)
        copies = []
        for t, value in enumerate((dq_ref[...], dk_ref[...].T, dv_ref[...].T)):
            out_scr[t] = value.astype(BF16)
            col = pl.multiple_of((t * n_pairs + pair) * BLK, BLK)
            copies.append(pltpu.make_async_copy(out_scr.at[t], dproj_ref.at[:, pl.ds(col, BLK)], sems.at[t]))
            copies[-1].start()
        for cp in copies:
            cp.wait()

    col_spec = lambda off: pl.BlockSpec((s_len, BLK), lambda p: (0, off + p))
    anyspec = pl.BlockSpec(memory_space=pl.ANY)
    return _pcall(
        body, name=name, out_shape=jax.ShapeDtypeStruct(dproj.shape, dproj.dtype), grid=(n_pairs,),
        in_specs=[col_spec(0), col_spec(n_pairs), col_spec(2 * n_pairs), col_spec(0),
                  pl.BlockSpec((2, s_len, BLK), lambda p: (p, 0, 0)), anyspec],
        out_specs=anyspec,
        scratch_shapes=[pltpu.VMEM((s_len, BLK), F32), pltpu.VMEM((BLK, s_len), F32), pltpu.VMEM((BLK, s_len), F32),
                        pltpu.VMEM((3, s_len, BLK), BF16), pltpu.SemaphoreType.DMA((3,))],
        aliases={5: 0},
        semantics=("arbitrary",))(proj, proj, proj, do_a, totals, dproj)


def _hgrn_bwd(proj, do_b, lb, dproj, name):
    s_len = proj.shape[0]
    nc = s_len // BLK
    gw = HG_GROUP * HG_HEAD_DIM
    n_groups = WIDTH // gw
    base = 4 * WIDTH // gw
    heads_of = range(HG_GROUP)

    def body(q_ref, f_ref, i_ref, do_ref, lb_ref, dproj_in, dproj_ref, dlb_ref, mask_ref, st_ref, out_scr, sems):
        del dproj_in
        _hg_masks(mask_ref)
        row = _iota2((BLK, BLK), 0)
        col = _iota2((BLK, BLK), 1)
        lower_incl = (col <= row).astype(BF16)
        upper_incl = (col >= row).astype(BF16)
        lb_v = lb_ref[...]
        refs = (q_ref, f_ref, i_ref)

        def fwd_chunk(ci, sts):
            for h in heads_of:
                st_ref[ci, h] = sts[h]
            heads, bs = _hg_load(refs, pl.multiple_of(ci * BLK, BLK), lb_v, lower_incl)
            b_ends = [b[BLK - 1:BLK, :] for b in bs]
            k_decs = [((1.0 - hd[2]) * jnp.exp(b_end - b)).astype(BF16) for hd, b, b_end in zip(heads, bs, b_ends)]
            grown = [_dot_tn(hd[5].astype(BF16), k_dec) for hd, k_dec in zip(heads, k_decs)]
            return tuple(st * jnp.exp(b_end) + g for st, b_end, g in zip(sts, b_ends, grown))

        zero_state = (jnp.zeros((HG_HEAD_DIM, HG_HEAD_DIM), F32),) * HG_GROUP
        lax.fori_loop(0, nc, fwd_chunk, zero_state)

        def bwd_chunk(cc, carry):
            dsts, suffixes, dlbs = carry
            ci = nc - 1 - cc
            r0 = pl.multiple_of(ci * BLK, BLK)
            heads, bs = _hg_load(refs, r0, lb_v, lower_incl)
            qs = [hd[0] for hd in heads]
            fs = [hd[2] for hd in heads]
            ks = [1.0 - f for f in fs]
            vs = [hd[5] for hd in heads]
            vbs = [v.astype(BF16) for v in vs]
            dos = [do_ref[pl.ds(r0, BLK), h * HG_HEAD_DIM:(h + 1) * HG_HEAD_DIM] for h in heads_of]
            dobs = [do.astype(BF16) for do in dos]
            b_ends = [b[BLK - 1:BLK, :] for b in bs]
            e_qs = [jnp.exp(b) for b in bs]
            e_ks = [jnp.exp(b_end - b) for b, b_end in zip(bs, b_ends)]
            qes = [(q * e).astype(BF16) for q, e in zip(qs, e_qs)]
            khs = [(k * e).astype(BF16) for k, e in zip(ks, e_ks)]
            st_terms = [_split2_lanes(st_ref[ci, h]) for h in heads_of]
            ds_terms = [_split2_lanes(dst) for dst in dsts]
            dqes = [_dot(dob, t[:, :HG_HEAD_DIM]) + _dot(dob, t[:, HG_HEAD_DIM:]) for dob, t in zip(dobs, st_terms)]
            dkhs = [_dot(vb, t[:, :HG_HEAD_DIM]) + _dot(vb, t[:, HG_HEAD_DIM:]) for vb, t in zip(vbs, ds_terms)]
            dvs = [_dot_nt(kh, t[:, :HG_HEAD_DIM]) for kh, t in zip(khs, ds_terms)]
            grown = [_dot_tn(dob, qe) for dob, qe in zip(dobs, qes)]
            das = [_dot_nt(dob, vb) for dob, vb in zip(dobs, vbs)]
            dqs = [e * dqe for e, dqe in zip(e_qs, dqes)]
            dks = [e * dkh for e, dkh in zip(e_ks, dkhs)]
            dlogs = [qe.astype(F32) * dqe - kh.astype(F32) * dkh for qe, dqe, kh, dkh in zip(qes, dqes, khs, dkhs)]
            scs = [None] * HG_GROUP
            for v_idx, m in enumerate(HG_LEVELS):
                es, qms, kms = _hg_level_terms(qs, ks, bs, m)
                msk = mask_ref[v_idx]
                terms = [_dot_nt(qm, km) for qm, km in zip(qms, kms)]
                pms = [(da * msk).astype(BF16) for da in das]
                dqms = [_dot(pm, km) for pm, km in zip(pms, kms)]
                dkms = [_dot_tn(pm, qm) for pm, qm in zip(pms, qms)]
                scs = [t * msk if sc is None else sc + t * msk for sc, t in zip(scs, terms)]
                dqs = [dq + dqm * e for dq, dqm, e in zip(dqs, dqms, es)]
                dks = [dk + dkm * e for dk, dkm, e in zip(dks, dkms, es)]
                dlogs = [dl + (qm.astype(F32) * dqm - km.astype(F32) * dkm)
                         for dl, qm, dqm, km, dkm in zip(dlogs, qms, dqms, kms, dkms)]
            intras = [_dot_tn(sc.astype(BF16), dob) for sc, dob in zip(scs, dobs)]
            dgs = [_dot_01_l(upper_incl, dl) + sfx for dl, sfx in zip(dlogs, suffixes)]
            new_dlbs = []
            for h in heads_of:
                q, dq_fac, f, sig = heads[h][0], heads[h][1], heads[h][2], heads[h][3]
                a_diag = jnp.sum(dos[h] * vs[h], axis=-1, keepdims=True)
                s_diag = jnp.sum(q * ks[h], axis=-1, keepdims=True)
                dq = dqs[h] + a_diag * ks[h]
                dk = dks[h] + a_diag * q
                dv = dvs[h] + intras[h] + s_diag * dos[h]
                dfull = dgs[h] / f - dk
                sl = slice(h * HG_HEAD_DIM, (h + 1) * HG_HEAD_DIM)
                out_scr[0, pl.ds(r0, BLK), sl] = (dq * dq_fac).astype(BF16)
                out_scr[1, pl.ds(r0, BLK), sl] = (dfull * (1.0 - lb_v[:, sl]) * sig * (1.0 - sig)).astype(BF16)
                out_scr[2, pl.ds(r0, BLK), sl] = dv.astype(BF16)
                new_dlbs.append(dlbs[h] + jnp.sum(dfull * (1.0 - sig), axis=0, keepdims=True))
            new_dsts = tuple(dst * jnp.exp(b_end) + g for dst, b_end, g in zip(dsts, b_ends, grown))
            return new_dsts, tuple(dg[0:1, :] for dg in dgs), tuple(new_dlbs)

        zero_row = (jnp.zeros((1, HG_HEAD_DIM), F32),) * HG_GROUP
        _, _, dlbs = lax.fori_loop(0, nc, bwd_chunk, (zero_state, zero_row, zero_row))
        dlb_ref[...] = jnp.broadcast_to(jnp.concatenate(dlbs, axis=1), dlb_ref.shape)
        group = pl.program_id(0)
        copies = []
        for t in range(3):
            col = pl.multiple_of((base + t * n_groups + group) * gw, gw)
            copies.append(pltpu.make_async_copy(out_scr.at[t], dproj_ref.at[:, pl.ds(col, gw)], sems.at[t]))
            copies[-1].start()
        for cp in copies:
            cp.wait()

    col_spec = lambda off: pl.BlockSpec((s_len, gw), lambda h: (0, off + h))
    anyspec = pl.BlockSpec(memory_space=pl.ANY)
    return _pcall(
        body, name=name,
        out_shape=(jax.ShapeDtypeStruct(dproj.shape, dproj.dtype), jax.ShapeDtypeStruct((8, WIDTH), F32)),
        grid=(n_groups,),
        in_specs=[col_spec(base), col_spec(base + n_groups), col_spec(base + 2 * n_groups), col_spec(0),
                  pl.BlockSpec((1, gw), lambda h: (0, h)), anyspec],
        out_specs=(anyspec, pl.BlockSpec((8, gw), lambda h: (0, h))),
        scratch_shapes=[pltpu.VMEM((len(HG_LEVELS), BLK, BLK), F32),
                        pltpu.VMEM((nc, HG_GROUP, HG_HEAD_DIM, HG_HEAD_DIM), F32),
                        pltpu.VMEM((3, s_len, gw), BF16), pltpu.SemaphoreType.DMA((3,))],
        aliases={5: 0},
        semantics=("arbitrary",))(proj, proj, proj, do_b, lb, dproj)


def _dh_matmul(dproj, w_full, after, name):
    s_len, n = dproj.shape
    d = w_full.shape[0]
    tm = min(1024, s_len)
    tk = 4608

    def body(dp_ref, w_ref, after_ref, dh_ref):
        del after_ref
        part = _dot_nt(dp_ref[...], w_ref[...])

        @pl.when(pl.program_id(1) == 0)
        def _():
            dh_ref[...] = part

        @pl.when(pl.program_id(1) > 0)
        def _():
            dh_ref[...] += part

    return _pcall(
        body, name=name, out_shape=jax.ShapeDtypeStruct((s_len, d), F32),
        grid=(s_len // tm, n // tk),
        in_specs=[pl.BlockSpec((tm, tk), lambda i, k: (i, k)), pl.BlockSpec((d, tk), lambda i, k: (0, k)),
                  pl.BlockSpec(memory_space=pl.ANY)],
        out_specs=pl.BlockSpec((tm, d), lambda i, k: (i, 0)),
        semantics=("arbitrary", "arbitrary"))(dproj, w_full, after)


def _gw_matmul(h_t, dproj, name):
    d, s_len = h_t.shape
    n = dproj.shape[1]
    tn = 2304

    def body(ht_ref, dp_ref, gw_ref):
        gw_ref[...] = _dot(ht_ref[...], dp_ref[...]).astype(BF16)

    return _pcall(
        body, name=name, out_shape=jax.ShapeDtypeStruct((d, n), BF16),
        grid=(n // tn,),
        in_specs=[pl.BlockSpec((d, s_len), lambda j: (0, 0)), pl.BlockSpec((s_len, tn), lambda j: (0, j))],
        out_specs=pl.BlockSpec((d, tn), lambda j: (0, j)),
        semantics=("arbitrary",))(h_t, dproj)


def _ln_bwd(dh, x, scale, dres, name):
    s_len, d = x.shape
    tm = min(512, s_len)

    def body(dh_ref, x_ref, sc_ref, dres_ref, dx_ref, vec_ref):
        @pl.when(pl.program_id(0) == 0)
        def _():
            vec_ref[...] = jnp.zeros_like(vec_ref)

        dh = dh_ref[...]
        xs, rstd = _standardize(x_ref[...])
        vec_ref[0:1, :] += jnp.sum(dh, axis=0, keepdims=True)
        vec_ref[1:2, :] += jnp.sum(dh * xs, axis=0, keepdims=True)
        dx_ref[...] = _standardize_bwd(xs, rstd, dh * (1.0 + sc_ref[...])) + dres_ref[...]

    tile = pl.BlockSpec((tm, d), lambda i: (i, 0))
    return _pcall(body, name=name, grid=(s_len // tm,),
                  out_shape=(jax.ShapeDtypeStruct((s_len, d), F32), jax.ShapeDtypeStruct((8, d), F32)),
                  in_specs=[tile, tile, pl.BlockSpec((1, d), lambda i: (0, 0)), tile],
                  out_specs=(tile, pl.BlockSpec((8, d), lambda i: (0, 0))),
                  semantics=("arbitrary",))(dh, x, scale, dres)


def _wmod_grad(c_t, dmod):
    d = c_t.shape[0]
    n_layers, _, cm = dmod.shape

    def body(c_ref, dm_ref, o_ref):
        for l in range(n_layers):
            acc = None
            for b in range(NDEV):
                term = c_ref[:, b:b + 1] * dm_ref[l, b:b + 1, :]
                acc = term if acc is None else acc + term
            o_ref[l] = acc

    return _pcall(body, name="wmod_grad", out_shape=jax.ShapeDtypeStruct((n_layers, d, cm), F32))(c_t, dmod)


def _sum_adamw(parts_list, w, m, v, name):
    n_ranges = len(parts_list)
    n_src, range_rows, cols = parts_list[0].shape
    rows = range_rows * n_ranges
    tr = range_rows
    for cand in (512, 256, 128, 64, 32, 16, 8):
        if range_rows % cand == 0 and cand * cols * 4 <= (2 << 20):
            tr = cand
            break
    tiles = range_rows // tr

    def body(*refs):
        p_refs = refs[:n_ranges]
        w_ref, m_ref, v_ref, g_ref, d_ref, nm_ref, nv_ref = refs[n_ranges:]

        def step(p_ref):
            g = p_ref[0].astype(F32)
            for s in range(1, n_src):
                g = g + p_ref[s].astype(F32)
            g_ref[...] = g
            d_ref[...], nm_ref[...], nv_ref[...] = _adamw_step(g, w_ref[...], m_ref[...], v_ref[...])

        if n_ranges == 1:
            step(p_refs[0])
        else:
            for j in range(n_ranges):
                @pl.when(pl.program_id(0) // tiles == j)
                def _(j=j):
                    step(p_refs[j])

    def part_spec(j):
        return pl.BlockSpec((n_src, tr, cols), lambda i: (0, jnp.clip(i - j * tiles, 0, tiles - 1), 0))

    tile = pl.BlockSpec((tr, cols), lambda i: (i, 0))
    out = jax.ShapeDtypeStruct((rows, cols), F32)
    return _pcall(body, name=name, grid=(rows // tr,), out_shape=(out,) * 4,
                  in_specs=[part_spec(j) for j in range(n_ranges)] + [tile, tile, tile],
                  out_specs=(tile,) * 4, semantics=("arbitrary",))(*parts_list, w, m, v)


def _adamw_step(g, w, m, v):
    nm = ADAM_B1 * m + (1.0 - ADAM_B1) * g
    nv = ADAM_B2 * v + (1.0 - ADAM_B2) * (g * g)
    m_hat = nm / (1.0 - ADAM_B1 ** ADAM_STEP)
    v_hat = nv / (1.0 - ADAM_B2 ** ADAM_STEP)
    return -ADAM_LR * (m_hat / (jnp.sqrt(v_hat) + ADAM_EPS) + ADAM_WD * w), nm, nv


def _adamw_small(gs, ws, ms, vs):
    n = len(gs)

    def body(*refs):
        for p in range(n):
            results = _adamw_step(*(refs[k * n + p][...] for k in range(4)))
            for k in range(3):
                refs[(4 + k) * n + p][...] = results[k]

    shapes = [jax.ShapeDtypeStruct(w.shape, F32) for w in ws]
    outs = _pcall(body, name="adamw_small", out_shape=shapes * 3)(*gs, *ws, *ms, *vs)
    return [(outs[p], outs[n + p], outs[2 * n + p]) for p in range(n)]


def _sum_parts(parts, name):
    n_src = parts.shape[0]

    def body(p_ref, o_ref):
        acc = p_ref[0]
        for s in range(1, n_src):
            acc = acc + p_ref[s]
        o_ref[...] = acc

    return _pcall(body, name=name, out_shape=jax.ShapeDtypeStruct(parts.shape[1:], F32))(parts)


def _pair_sum(gw, stage, me, name):
    d = gw.shape[0]
    n_slots, _, shard = stage.shape

    def body(me_ref, g_ref, s_ref, own_ref, o_ref):
        del me_ref
        total = (g_ref[...].astype(F32) + s_ref[0].astype(F32)).astype(BF16)
        o_ref[0] = total

        @pl.when(pl.program_id(0) == 0)
        def _():
            own_ref[0] = total

    slot = pl.BlockSpec((1, d, shard), lambda jj, me_ref: (jj, 0, 0))
    out = jax.ShapeDtypeStruct(stage.shape, BF16)
    return pl.pallas_call(
        body, name=name, out_shape=(out, out),
        grid_spec=pltpu.PrefetchScalarGridSpec(
            num_scalar_prefetch=1, grid=(n_slots,),
            in_specs=[pl.BlockSpec((d, shard), lambda jj, me_ref: (0, me_ref[0] ^ (2 * jj))), slot],
            out_specs=(pl.BlockSpec((1, d, shard), lambda jj, me_ref: (0, 0, 0)), slot)),
        compiler_params=pltpu.CompilerParams(dimension_semantics=("arbitrary",), vmem_limit_bytes=VMEM_LIMIT),
        interpret=False)(me.reshape(1).astype(jnp.int32), gw, stage)


def _lower_bound_table(lower_bounds):
    p = jax.nn.softmax(lower_bounds.astype(F32), axis=0)
    return jnp.cumsum(p, axis=0) - p[0:1]


def _pad_rows(v, width):
    n = v.shape[0]
    rows = -(-n // width)
    rows = -(-rows // 8) * 8
    return jnp.pad(v, (0, rows * width - n)).reshape(rows, width)


def kernel(x, c, w_mod, b_mod, w_in, conv_w, hgrn_norm_w, lower_bounds, w_branch, w_out, ln_g, ln_b, loss_target, m_w_mod, m_b_mod, m_w_in, m_conv_w, m_hgrn_norm_w, m_lower_bounds, m_w_branch, m_w_out, m_ln_g, m_ln_b, v_w_mod, v_b_mod, v_w_in, v_conv_w, v_hgrn_norm_w, v_lower_bounds, v_w_branch, v_w_out, v_ln_g, v_ln_b):
    n_layers = N_LAYERS
    s_len, d = x.shape[1], x.shape[2]
    n_cols = w_in.shape[2] * NDEV
    cw_cols = conv_w.shape[2]
    cm = w_mod.shape[2]
    me = _my_index()
    x0 = x[0]
    target = loss_target[0]

    small = _pad_rows(jnp.concatenate([c.reshape(-1), conv_w.reshape(-1)]), BLK)
    small_all = _all_gather_small("gather_c_conv", small).reshape(NDEV, -1)
    c_all = small_all[:, :d]
    conv_full = small_all[:, d:d + n_layers * 3 * cw_cols].reshape(NDEV, n_layers, 3, cw_cols)
    conv_full = conv_full.transpose(1, 2, 0, 3).reshape(n_layers, 3, WIDTH)

    b_mod_mine = lax.dynamic_slice_in_dim(b_mod, me * cm, cm, axis=1).reshape(n_layers, 1, cm)
    mod_cols = _mod_fwd(c_all, w_mod, b_mod_mine)
    mod_all = _all_gather_small("gather_mod", mod_cols.reshape(n_layers * NDEV, cm))
    mod_all = mod_all.reshape(NDEV, n_layers, NDEV, cm)
    mod_mine = lax.dynamic_index_in_dim(mod_all, me, axis=2, keepdims=False)
    mod_mine = mod_mine.transpose(1, 0, 2).reshape(n_layers, 3, 1, d)

    shard = w_in.shape[2]
    dsh = d // NDEV
    w_in_b, w_branch_b, w_out_b = w_in.astype(BF16), w_branch.astype(BF16), w_out.astype(BF16)
    window = lambda ref, dev: ref.at[:, pl.ds(pl.multiple_of(dev * shard, BLK), shard)]

    def two_step_sends(places):
        chips, sibling = [], []
        for k in (1, 2, 4, 6):
            for a, place in enumerate(places):
                chips.append((k, lambda ins, lands, me, a=a: ins[a],
                              lambda lands, me, a=a, place=place: place(lands[a], me),
                              lambda lands, me, a=a, k=k, place=place: place(lands[a], me ^ k)))
        for j in (2, 4, 6):
            for a, place in enumerate(places):
                sibling.append((1, lambda ins, lands, me, a=a, j=j, place=place: place(lands[a], me ^ j),
                                lambda lands, me, a=a, j=j, place=place: place(lands[a], me ^ j),
                                lambda lands, me, a=a, j=j, place=place: place(lands[a], me ^ 1 ^ j)))
        return chips, sibling

    in_sends = two_step_sends([window])
    rest_sends = two_step_sends([_slot, _slot])
    layer_sends = two_step_sends([window, _slot, _slot])

    def in_land(l):
        return _place_own_window(f"place_w_in_{l}", (d, n_cols), w_in_b[l], me)

    def rest_lands(l):
        return [_place_own((NDEV, 3, WIDTH, dsh), BF16, w_branch_b[l][None], (me, 0, 0, 0)),
                _place_own((NDEV, dsh, d), BF16, w_out_b[l][None], (me, 0, 0))]

    def gather_start(name, shards, lands, sends, after):
        return _exchange_start(f"{name}_chips_start", shards, lands, sends[0], after)

    def gather_pass_on(name, started, after, sends):
        _, lands = _exchange_wait(f"{name}_chips_wait", started, after, sends[0])
        return _exchange_start(f"{name}_sibling_start", [], lands, sends[1])

    def gather_finish(name, started, after, sends):
        return _exchange_wait(f"{name}_sibling_wait", started, after, sends[1])[1]

    def branch_out_weights(w_branch_l, w_out_l):
        return w_branch_l.transpose(1, 2, 0, 3).reshape(3, WIDTH, d), w_out_l.reshape(d, d)

    gathering = gather_start("gather_w_in_0", [w_in_b[0]], [in_land(0)], in_sends, mod_mine)
    passing = gather_pass_on("gather_w_in_0", gathering, gathering[4], in_sends)
    rest_gathering = gather_start("gather_rest_0", [w_branch_b[0], w_out_b[0]], rest_lands(0), rest_sends, passing[4])
    next_gathering = None
    if n_layers > 1:
        next_gathering = gather_start("gather_weights_1", [w_in_b[1], w_branch_b[1], w_out_b[1]],
                                      [in_land(1)] + rest_lands(1), layer_sends, rest_gathering[4])
    w_in_l = gather_finish("gather_w_in_0", passing, (next_gathering or rest_gathering)[4], in_sends)[0]

    lbs = _lower_bound_table(lower_bounds)
    norm_w4 = jnp.tile(hgrn_norm_w, (1, WIDTH // HG_HEAD_DIM))

    saved = []
    xl = x0
    for l in range(n_layers):
        shift, scale, gate = mod_mine[l, 0], mod_mine[l, 1], mod_mine[l, 2]
        proj, h_t = _ln_proj(xl, shift, scale, w_in_l, f"ln_proj_{l}")
        o_a, totals = _sb_fwd(proj, f"sb_fwd_{l}")
        if l == 0:
            rest_passing = gather_pass_on("gather_rest_0", rest_gathering, o_a, rest_sends)
        lb_l = lbs[l:l + 1] + rest_passing[4][0, 0] if l == 0 else lbs[l:l + 1]
        o_b = _hgrn_fwd(proj, lb_l, f"hgrn_fwd_{l}")
        if l == 0:
            wb_l, wo_l = branch_out_weights(*gather_finish("gather_rest_0", rest_passing, o_b, rest_sends))
            if n_layers > 1:
                next_passing = gather_pass_on("gather_weights_1", next_gathering, o_b, layer_sends)
                gate = gate + next_passing[4][0, 0]
        x_new, merged, ycat = _merge_fwd(xl, proj, o_a, o_b, gate, norm_w4[l:l + 1], conv_full[l],
                                         wb_l, wo_l, ln_g[l:l + 1], ln_b[l:l + 1], f"merge_fwd_{l}")
        saved.append((xl, proj, h_t, o_a, totals, o_b, merged, ycat, w_in_l, wb_l, wo_l))
        if l == 0 and n_layers > 1:
            w_in_l, w_branch_l, w_out_l = gather_finish("gather_weights_1", next_passing, x_new, layer_sends)
            wb_l, wo_l = branch_out_weights(w_branch_l, w_out_l)
        xl = x_new

    loss_part, dx = _loss_fwd_bwd(xl, target)

    pair_sends = [(1, lambda ins, lands, me, j=j: window(ins[0], me ^ 1 ^ j),
                   lambda lands, me, jj=jj: lands[0].at[jj], lambda lands, me, jj=jj: lands[0].at[jj])
                  for jj, j in enumerate((0, 2, 4, 6))]
    chip_sum_sends = [(j, lambda ins, lands, me, jj=jj: ins[0].at[jj],
                       lambda lands, me, jj=jj: lands[0].at[jj], lambda lands, me, jj=jj: lands[0].at[jj])
                      for jj, j in ((1, 2), (2, 4), (3, 6))]
    rest_scatter = _direct_sends([(0, 0, _slot, _slot), (1, 1, _slot, _slot)])
    scattering = [None] * n_layers
    small_grads = [None] * n_layers
    dmod = [None] * n_layers
    tie = None
    for l in reversed(range(n_layers)):
        xl, proj, h_t, o_a, totals, o_b, merged, ycat, w_in_l, wb_l, wo_l = saved[l]
        scale, gate = mod_mine[l, 1], mod_mine[l, 2]
        if tie is not None:
            gate = gate + tie[0, 0]
        dres, dycat, dproj, gwo_by_owner, gwb_by_owner, mvec = _merge_bwd(
            dx, xl, merged, ycat, proj, gate, wb_l, wo_l, ln_g[l:l + 1], f"merge_bwd_{l}")
        lands = [_place_own((NDEV, 3, WIDTH, dsh), BF16, lax.dynamic_slice_in_dim(gwb_by_owner, me, 1, axis=0),
                            (me, 0, 0, 0)),
                 _place_own((NDEV, dsh, d), BF16, lax.dynamic_slice_in_dim(gwo_by_owner, me, 1, axis=0),
                            (me, 0, 0))]
        rest_started = _exchange_start(f"scatter_rest_{l}_start", [gwb_by_owner, gwo_by_owner], lands, rest_scatter)
        dproj, do_a, do_b, bvec = _branch_bwd(dycat, proj, o_a, o_b, norm_w4[l:l + 1] + rest_started[4][0, 0],
                                              conv_full[l], dproj, f"branch_bwd_{l}")
        dproj = _sb_bwd(proj, do_a, totals, dproj, f"sb_bwd_{l}")
        dproj, dlb = _hgrn_bwd(proj, do_b, lbs[l:l + 1], dproj, f"hgrn_bwd_{l}")
        gwi = _gw_matmul(h_t, dproj, f"gw_matmul_{l}")
        swapping = _exchange_start(f"scatter_in_{l}_sibling_start", [gwi], [lax.empty((4, d, shard), BF16)], pair_sends)
        if l > 0:
            dh = _dh_matmul(dproj, w_in_l, swapping[4], f"dh_matmul_{l}")
        (gwi,), (stage,) = _exchange_wait(f"scatter_in_{l}_sibling_wait", swapping, dh if l > 0 else swapping[4],
                                          pair_sends)
        land, chip_sums = _pair_sum(gwi, stage, me, f"pair_sum_{l}")
        in_started = _exchange_start(f"scatter_in_{l}_chips_start", [chip_sums], [land], chip_sum_sends)
        scattering[l] = (in_started, rest_started)
        tie = in_started[4]
        if l == 0:
            dh = _dh_matmul(dproj, w_in_l, tie, f"dh_matmul_{l}")
        dx, lvec = _ln_bwd(dh, xl, scale + tie[0, 0], dres, f"ln_bwd_{l}")
        dmod[l] = jnp.concatenate([lvec[0], lvec[1], mvec[2]])
        norm_grad = bvec[0].reshape(WIDTH // HG_HEAD_DIM, HG_HEAD_DIM).sum(axis=0)
        small_grads[l] = jnp.concatenate([mvec[0], mvec[1], norm_grad, dlb[0], bvec[1:4].reshape(-1)])
    grad_x = dx[None]

    small_vec = jnp.concatenate(dmod + small_grads + [loss_part.reshape(1)])
    n_small = small_vec.shape[0]
    small_all = _all_gather_small("gather_small_grads", _pad_rows(small_vec, BLK))
    small_sum = _sum_parts(small_all, "sum_small_grads").reshape(-1)[:n_small]
    dmod_all = small_all.reshape(NDEV, -1)[:, :n_layers * 3 * d].reshape(NDEV, n_layers, 3 * d)

    loss = small_sum[n_small - 1]

    off = n_layers * 3 * d
    grad_b_mod = small_sum[:off].reshape(n_layers, 3 * d)
    per_layer = 2 * d + HG_HEAD_DIM + WIDTH + 3 * WIDTH
    g_ln_g, g_ln_b, g_norm, g_lbs, g_conv = [], [], [], [], []
    for l in range(n_layers):
        seg = small_sum[off + l * per_layer: off + (l + 1) * per_layer]
        g_ln_g.append(seg[:d])
        g_ln_b.append(seg[d:2 * d])
        g_norm.append(seg[2 * d:2 * d + HG_HEAD_DIM])
        g_lbs.append(seg[2 * d + HG_HEAD_DIM:2 * d + HG_HEAD_DIM + WIDTH])
        g_conv.append(seg[2 * d + HG_HEAD_DIM + WIDTH:].reshape(3, WIDTH))
    grad_ln_g, grad_ln_b = jnp.stack(g_ln_g), jnp.stack(g_ln_b)
    grad_norm = jnp.stack(g_norm)
    _, lbs_vjp = jax.vjp(_lower_bound_table, lower_bounds)
    grad_lower = lbs_vjp(jnp.stack(g_lbs))[0]
    grad_conv = lax.dynamic_slice_in_dim(jnp.stack(g_conv), me * cw_cols, cw_cols, axis=2)

    dmod_mine = lax.dynamic_slice_in_dim(dmod_all, me * cm, cm, axis=2).transpose(1, 0, 2)
    grad_w_mod = _wmod_grad(c_all.T, dmod_mine)

    p_in, p_branch, p_out = [None] * n_layers, [None] * n_layers, [None] * n_layers
    for l in reversed(range(n_layers)):
        in_started, rest_started = scattering[l]
        p_branch_l, p_out[l] = _exchange_wait(f"scatter_rest_{l}_wait", rest_started, grad_w_mod, rest_scatter)[1]
        p_branch[l] = p_branch_l.reshape(NDEV, 3 * WIDTH, dsh)
        p_in[l] = _exchange_wait(f"scatter_in_{l}_chips_wait", in_started, grad_w_mod, chip_sum_sends)[1][0]

    def adam(parts_list, w, m, v, name):
        shape = w.shape
        cols = shape[-1]
        flat = lambda a: a.reshape(-1, cols)
        outs = _sum_adamw(parts_list, flat(w), flat(m), flat(v), name)
        return [o.reshape(shape) for o in outs]

    r_w_in = adam(p_in, w_in, m_w_in, v_w_in, "adamw_w_in")
    r_w_branch = adam(p_branch, w_branch, m_w_branch, v_w_branch, "adamw_w_branch")
    r_w_out = adam(p_out, w_out, m_w_out, v_w_out, "adamw_w_out")
    r_w_mod = adam([grad_w_mod.reshape(1, -1, cm)], w_mod, m_w_mod, v_w_mod, "adamw_w_mod")

    small_names = ["b_mod", "conv_w", "hgrn_norm_w", "lower_bounds", "ln_g", "ln_b"]
    small_g = [grad_b_mod, grad_conv, grad_norm, grad_lower, grad_ln_g, grad_ln_b]
    small_w = [b_mod, conv_w, hgrn_norm_w, lower_bounds, ln_g, ln_b]
    small_m = [m_b_mod, m_conv_w, m_hgrn_norm_w, m_lower_bounds, m_ln_g, m_ln_b]
    small_v = [v_b_mod, v_conv_w, v_hgrn_norm_w, v_lower_bounds, v_ln_g, v_ln_b]
    as_rows = lambda a: a.reshape(-1, a.shape[-1])
    updates = _adamw_small([as_rows(a) for a in small_g], [as_rows(a) for a in small_w],
                           [as_rows(a) for a in small_m], [as_rows(a) for a in small_v])
    r_small = {n: [g] + [u.reshape(w.shape) for u in upd]
               for n, g, w, upd in zip(small_names, small_g, small_w, updates)}

    results = {"w_mod": r_w_mod, "w_in": r_w_in, "w_branch": r_w_branch, "w_out": r_w_out, **r_small}
    order = ["w_mod", "b_mod", "w_in", "conv_w", "hgrn_norm_w", "lower_bounds", "w_branch", "w_out", "ln_g", "ln_b"]
    outs = [loss, grad_x]
    for idx in range(4):
        outs.extend(results[n][idx] for n in order)
    return tuple(outs)
```

```python
import jax
import jax.numpy as jnp
from jax import lax
from jax.experimental import pallas as pl
from jax.experimental.pallas import tpu as pltpu

F32 = jnp.float32
BF16 = jnp.bfloat16
NDEV = 8
N_LAYERS = 2
SB_HEAD_DIM = 64
HG_HEAD_DIM = 128
WIDTH = 512
BLK = 128
LN_EPS = 1e-5
RMS_EPS = 1e-6
ALPHA = (2.0 * N_LAYERS) ** 0.25
ADAM_LR, ADAM_B1, ADAM_B2, ADAM_EPS, ADAM_WD, ADAM_STEP = 0.001, 0.9, 0.999, 1e-08, 0.01, 10
VMEM_LIMIT = 56 * 1024 * 1024
MESH = pl.DeviceIdType.MESH
HG_LEVELS = (64, 32, 16, 8, 4, 2, 1)


def _pcall(body, *, name, out_shape, grid=None, in_specs=None, out_specs=None, scratch_shapes=(),
           semantics=None, aliases=None):
    kwargs = {}
    if grid is not None:
        kwargs["grid"] = grid
    if in_specs is not None:
        kwargs["in_specs"] = in_specs
    if out_specs is not None:
        kwargs["out_specs"] = out_specs
    if aliases:
        kwargs["input_output_aliases"] = aliases
    return pl.pallas_call(
        body, name=name, out_shape=out_shape, scratch_shapes=list(scratch_shapes),
        compiler_params=pltpu.CompilerParams(dimension_semantics=semantics, vmem_limit_bytes=VMEM_LIMIT),
        interpret=False, **kwargs)


def _dot(a, b):
    return jnp.dot(a, b, preferred_element_type=F32)


def _dot_nt(a, b):
    return lax.dot_general(a, b, (((1,), (1,)), ((), ())), preferred_element_type=F32)


def _dot_tn(a, b):
    return lax.dot_general(a, b, (((0,), (0,)), ((), ())), preferred_element_type=F32)


def _dot_01_l(m_bf16, x):
    x1 = x.astype(BF16)
    x2 = (x - x1.astype(F32)).astype(BF16)
    return _dot(jnp.concatenate([m_bf16, m_bf16], axis=1), jnp.concatenate([x1, x2], axis=0))


def _sigmoid(x):
    return 1.0 / (1.0 + jnp.exp(-x))


def _gate_sigmoid(x):
    return 0.5 * jnp.tanh(0.5 * x) + 0.5


def _silu_and_grad(x):
    s = _gate_sigmoid(x)
    return x * s, s * (1.0 + x * (1.0 - s))


LOG2E = 1.4426950408889634
MASKED_SCORE = -1e30


def _softplus2_parts(z2):
    minus_abs = lax.bitcast_convert_type(lax.bitcast_convert_type(z2, jnp.int32) | jnp.int32(-2 ** 31), F32)
    sp2 = jnp.maximum(z2, 0.0) + jnp.log2(1.0 + jnp.exp2(minus_abs))
    return sp2, _gate_sigmoid(z2 * (1.0 / LOG2E))


def _split2_lanes(x):
    x1 = x.astype(BF16)
    return jnp.concatenate([x1, (x - x1.astype(F32)).astype(BF16)], axis=1)


def _iota2(shape, dim):
    return lax.broadcasted_iota(jnp.int32, shape, dim)


def _standardize(x):
    mu = jnp.mean(x, axis=-1, keepdims=True)
    xc = x - mu
    var = jnp.mean(xc * xc, axis=-1, keepdims=True)
    rstd = lax.rsqrt(var + LN_EPS)
    return xc * rstd, rstd


def _standardize_bwd(xhat, rstd, dxhat):
    m1 = jnp.mean(dxhat, axis=-1, keepdims=True)
    m2 = jnp.mean(dxhat * xhat, axis=-1, keepdims=True)
    return rstd * (dxhat - m1 - xhat * m2)


def _my_index():
    return 4 * lax.axis_index("x") + 2 * lax.axis_index("y") + lax.axis_index("c")


def _exchange(name, ins, out_shapes, transfers, in_vmem, after=None):
    n_in, n_out, n_t = len(ins), len(out_shapes), len(transfers)

    def body(*refs):
        n_skip = n_in + (0 if after is None else 1)
        in_refs, out_refs = refs[:n_in], refs[n_skip:n_skip + n_out]
        send_sems, recv_sems, local_sems = refs[n_skip + n_out:]
        x, y, c = lax.axis_index("x"), lax.axis_index("y"), lax.axis_index("c")
        me = 4 * x + 2 * y + c
        started = []
        for t, (i, o, src_fn, dst_fn) in enumerate(transfers):
            own = pltpu.make_async_copy(src_fn(in_refs[i], me), dst_fn(out_refs[o], me), local_sems.at[t])
            own.start()
            started.append(own)
        arrivals = []
        for k in range(1, NDEV):
            px = x ^ ((k >> 2) & 1)
            py = y ^ ((k >> 1) & 1)
            pc = c ^ (k & 1)
            peer = 4 * px + 2 * py + pc
            for t, (i, o, src_fn, dst_fn) in enumerate(transfers):
                sem = t * (NDEV - 1) + k - 1
                push = pltpu.make_async_remote_copy(
                    src_ref=src_fn(in_refs[i], peer), dst_ref=dst_fn(out_refs[o], me),
                    send_sem=send_sems.at[sem], recv_sem=recv_sems.at[sem],
                    device_id=(px, py, pc), device_id_type=MESH)
                push.start()
                started.append(push)
                arrivals.append(pltpu.make_async_remote_copy(
                    src_ref=src_fn(in_refs[i], peer), dst_ref=dst_fn(out_refs[o], peer),
                    send_sem=send_sems.at[sem], recv_sem=recv_sems.at[sem],
                    device_id=(px, py, pc), device_id_type=MESH))
        for arrival in arrivals:
            arrival.wait_recv()
        for cp in started[n_t:]:
            cp.wait_send()
        for own in started[:n_t]:
            own.wait()

    space = pltpu.VMEM if in_vmem else pl.ANY
    spec = pl.BlockSpec(memory_space=space)
    extra = [] if after is None else [after]
    return _pcall(
        body, name=name, out_shape=out_shapes,
        in_specs=[spec] * n_in + [pl.BlockSpec(memory_space=pl.ANY)] * len(extra), out_specs=[spec] * n_out,
        scratch_shapes=[pltpu.SemaphoreType.DMA((n_t * (NDEV - 1),)),
                        pltpu.SemaphoreType.DMA((n_t * (NDEV - 1),)),
                        pltpu.SemaphoreType.DMA((n_t,))])(*ins, *extra)


def _whole(ref, dev):
    return ref


def _slot(ref, dev):
    return ref.at[dev]


def _all_gather_small(name, v, after=None):
    out = _exchange(name, [v], [jax.ShapeDtypeStruct((NDEV,) + v.shape, v.dtype)],
                    [(0, 0, _whole, _slot)], in_vmem=True, after=after)
    return out[0]


_HBM_SPEC = pl.BlockSpec(memory_space=pltpu.HBM)
_SEM_SPEC = pl.BlockSpec(memory_space=pltpu.SEMAPHORE)
_DATAFLOW = pltpu.SideEffectType.DATAFLOW_SIDE_EFFECTING


def _peer(x, y, c, k):
    px = x ^ ((k >> 2) & 1)
    py = y ^ ((k >> 1) & 1)
    pc = c ^ (k & 1)
    return (px, py, pc), 4 * px + 2 * py + pc


def _direct_sends(transfers):
    sends = []
    for k in range(1, NDEV):
        for i, o, src_fn, dst_fn in transfers:
            sends.append((k,
                          lambda ins, lands, me, i=i, k=k, src_fn=src_fn: src_fn(ins[i], me ^ k),
                          lambda lands, me, o=o, dst_fn=dst_fn: dst_fn(lands[o], me),
                          lambda lands, me, o=o, k=k, dst_fn=dst_fn: dst_fn(lands[o], me ^ k)))
    return sends


def _exchange_start(name, ins, lands, sends, after=None):
    n_in, n_buf = len(ins), len(ins) + len(lands)
    n_sem = len(sends)

    def body(*refs):
        in_refs, land_refs = refs[:n_in], refs[n_in:n_buf]
        n_skip = n_buf + (0 if after is None else 1)
        send_sems, recv_sems, token = refs[n_skip], refs[n_skip + 1], refs[-1]
        x, y, c = lax.axis_index("x"), lax.axis_index("y"), lax.axis_index("c")
        me = 4 * x + 2 * y + c
        for t, (k, src_fn, dst_fn, _) in enumerate(sends):
            pltpu.make_async_remote_copy(
                src_ref=src_fn(in_refs, land_refs, me), dst_ref=dst_fn(land_refs, me),
                send_sem=send_sems.at[t], recv_sem=recv_sems.at[t],
                device_id=_peer(x, y, c, k)[0], device_id_type=MESH).start()
        token[...] = jnp.zeros_like(token)

    bufs = [pltpu.with_memory_space_constraint(a, pltpu.HBM) for a in list(ins) + list(lands)]
    extra = [] if after is None else [after]
    outs = pl.pallas_call(
        body, name=name,
        out_shape=(pltpu.SemaphoreType.DMA((n_sem,)), pltpu.SemaphoreType.DMA((n_sem,)))
        + tuple(pltpu.HBM(a.shape, a.dtype) for a in bufs) + (jax.ShapeDtypeStruct((8, BLK), F32),),
        in_specs=[_HBM_SPEC] * n_buf + [pl.BlockSpec(memory_space=pl.ANY)] * len(extra),
        out_specs=(_SEM_SPEC, _SEM_SPEC) + (_HBM_SPEC,) * n_buf + (pl.BlockSpec(memory_space=pltpu.VMEM),),
        input_output_aliases={b: 2 + b for b in range(n_buf)},
        compiler_params=pltpu.CompilerParams(has_side_effects=_DATAFLOW),
        interpret=False)(*bufs, *extra)
    return outs[0], outs[1], list(outs[2:2 + n_in]), list(outs[2 + n_in:2 + n_buf]), outs[-1]


def _exchange_wait(name, started, after, sends):
    send_sems, recv_sems, ins, lands, _ = started
    n_in, n_buf = len(ins), len(ins) + len(lands)

    def body(*refs):
        in_refs, land_refs = refs[:n_in], refs[n_in:n_buf]
        send_sems, recv_sems = refs[n_buf], refs[n_buf + 1]
        x, y, c = lax.axis_index("x"), lax.axis_index("y"), lax.axis_index("c")
        me = 4 * x + 2 * y + c
        for t, (k, src_fn, _, rcv_fn) in enumerate(sends):
            cp = pltpu.make_async_remote_copy(
                src_ref=src_fn(in_refs, land_refs, me), dst_ref=rcv_fn(land_refs, me),
                send_sem=send_sems.at[t], recv_sem=recv_sems.at[t],
                device_id=_peer(x, y, c, k)[0], device_id_type=MESH)
            cp.wait_send()
            cp.wait_recv()

    bufs = list(ins) + list(lands)
    outs = pl.pallas_call(
        body, name=name, out_shape=tuple(pltpu.HBM(a.shape, a.dtype) for a in bufs),
        in_specs=[_HBM_SPEC] * n_buf + [_SEM_SPEC, _SEM_SPEC, pl.BlockSpec(memory_space=pl.ANY)],
        out_specs=(_HBM_SPEC,) * n_buf,
        input_output_aliases={b: b for b in range(n_buf)},
        compiler_params=pltpu.CompilerParams(has_side_effects=_DATAFLOW),
        interpret=False)(*bufs, send_sems, recv_sems, after)
    return list(outs[:n_in]), list(outs[n_in:])


def _place_own(shape, dtype, own, start):
    return lax.dynamic_update_slice(lax.empty(shape, dtype), own, start)


def _place_own_window(name, shape, own, me):
    rows, cols = own.shape

    def body(me_ref, zone_in, own_ref, zone_ref):
        del me_ref, zone_in
        zone_ref[...] = own_ref[...]

    return pl.pallas_call(
        body, name=name, out_shape=jax.ShapeDtypeStruct(shape, own.dtype),
        grid_spec=pltpu.PrefetchScalarGridSpec(
            num_scalar_prefetch=1, grid=(1,),
            in_specs=[pl.BlockSpec(memory_space=pl.ANY), pl.BlockSpec((rows, cols), lambda i, me_ref: (0, 0))],
            out_specs=pl.BlockSpec((rows, cols), lambda i, me_ref: (0, me_ref[0]))),
        input_output_aliases={1: 0},
        compiler_params=pltpu.CompilerParams(dimension_semantics=("arbitrary",), vmem_limit_bytes=VMEM_LIMIT),
        interpret=False)(me.reshape(1).astype(jnp.int32), lax.empty(shape, own.dtype), own)


def _mod_fwd(c_all, w_mod, b_mod_mine):
    n_layers, _, cm = w_mod.shape

    def body(c_ref, w_ref, b_ref, o_ref):
        for l in range(n_layers):
            o_ref[l] = jnp.dot(c_ref[...], w_ref[l], preferred_element_type=F32,
                               precision=lax.Precision.HIGHEST) + b_ref[l]

    return _pcall(body, name="mod_fwd", out_shape=jax.ShapeDtypeStruct((n_layers, NDEV, cm), F32))(
        c_all, w_mod, b_mod_mine)


def _ln_proj(x, shift, scale, w_full, name):
    s_len, d = x.shape
    n = w_full.shape[1]
    tm = min(1024, s_len)
    tn = 2304

    def body(x_ref, sh_ref, sc_ref, w_ref, proj_ref, ht_ref, h_scr):
        @pl.when(pl.program_id(1) == 0)
        def _():
            xs, _ = _standardize(x_ref[...])
            h = xs * (1.0 + sc_ref[...]) + sh_ref[...]
            h_scr[...] = h.astype(BF16)
            ht_ref[...] = h.T.astype(BF16)

        proj_ref[...] = _dot(h_scr[...], w_ref[...])

    return _pcall(
        body, name=name,
        out_shape=(jax.ShapeDtypeStruct((s_len, n), F32), jax.ShapeDtypeStruct((d, s_len), BF16)),
        grid=(s_len // tm, n // tn),
        in_specs=[pl.BlockSpec((tm, d), lambda i, j: (i, 0)),
                  pl.BlockSpec((1, d), lambda i, j: (0, 0)),
                  pl.BlockSpec((1, d), lambda i, j: (0, 0)),
                  pl.BlockSpec((d, tn), lambda i, j: (0, j))],
        out_specs=(pl.BlockSpec((tm, tn), lambda i, j: (i, j)),
                   pl.BlockSpec((d, tm), lambda i, j: (0, i))),
        scratch_shapes=[pltpu.VMEM((tm, d), BF16)],
        semantics=("arbitrary", "arbitrary"))(x, shift, scale, w_full)


SB_Q_ROWS = 256
SB_K_BLOCKS = 2


def _sb_fwd(proj, name):
    s_len = proj.shape[0]
    n_pairs = WIDTH // BLK
    qr = min(SB_Q_ROWS, s_len)
    gb = SB_K_BLOCKS
    kw = gb * BLK
    nq = s_len // qr
    assert qr == kw

    def body(q_ref, k_ref, v_ref, o_ref, tot_ref):
        lane = _iota2((1, BLK), 1)
        row = _iota2((BLK, BLK), 0)
        col = _iota2((BLK, BLK), 1)
        half = jnp.concatenate([(row >= col).astype(BF16), jnp.ones((BLK, BLK), BF16)], axis=1)
        suffix_and_sum = jnp.concatenate([half, half], axis=0)
        strict = _iota2((qr, kw), 1) < _iota2((qr, kw), 0)
        head_lanes = [(lane // SB_HEAD_DIM) == hh for hh in range(2)]

        def scores(gi, qms, masked):
            c0 = pl.multiple_of(gi * kw, kw)
            kb = k_ref[pl.ds(c0, kw), :].astype(BF16)
            z2s = [_dot_nt(qms[hh], kb) for hh in range(2)]
            if masked:
                z2s = [jnp.where(strict, z2, MASKED_SCORE) for z2 in z2s]
            return tuple(z2s)

        def accumulate(gi, z2s, carry):
            c0 = pl.multiple_of(gi * kw, kw)
            vf = v_ref[pl.ds(c0, kw), :]
            sp2s = [_softplus2_parts(z2)[0] for z2 in z2s]
            terms = [[_split2_lanes(sp2[:, b * BLK:(b + 1) * BLK]) for b in range(gb)] for sp2 in sp2s]
            sums = [[_dot(t, suffix_and_sum) for t in head_terms] for head_terms in terms]
            weights, laters = [], []
            for hh in range(2):
                later = carry[2 * hh + 1]
                parts = [None] * gb
                for b in reversed(range(gb)):
                    parts[b] = sums[hh][b][:, :BLK] + later
                    later = later + sums[hh][b][:, BLK:]
                weights.append(jnp.exp2(z2s[hh] - jnp.concatenate(parts, axis=1)).astype(BF16))
                laters.append(later)
            outs = [_dot(weights[hh], jnp.where(head_lanes[hh], vf, 0.0).astype(BF16)) for hh in range(2)]
            return (carry[0] + outs[0], laters[0], carry[2] + outs[1], laters[1])

        def queries(i):
            qf = q_ref[pl.ds(pl.multiple_of(i * qr, qr), qr), :] * (SB_HEAD_DIM ** -0.5 * LOG2E)
            return [jnp.where(head_lanes[hh], qf, 0.0).astype(BF16) for hh in range(2)]

        def qtile(i, first_scores):
            r0 = pl.multiple_of(i * qr, qr)
            qms = queries(i)
            zero = jnp.zeros((qr, BLK), F32)

            def step(jj, state):
                gi = i - 1 - jj
                return scores(gi, qms, False) + accumulate(gi + 1, state[:2], state[2:])

            state = lax.fori_loop(0, i, step, first_scores + (zero,) * 4)
            nxt = jnp.minimum(i + 1, nq - 1)
            next_scores = scores(nxt, queries(nxt), True)
            carry = accumulate(0, state[:2], state[2:])
            o_ref[pl.ds(r0, qr), :] = carry[0] + carry[2]
            tot_ref[0, pl.ds(r0, qr), :] = carry[1]
            tot_ref[1, pl.ds(r0, qr), :] = carry[3]
            return next_scores

        lax.fori_loop(0, nq, qtile, scores(0, queries(0), True))

    col_spec = lambda off: pl.BlockSpec((s_len, BLK), lambda p: (0, off + p))
    return _pcall(
        body, name=name,
        out_shape=(jax.ShapeDtypeStruct((s_len, WIDTH), F32),
                   jax.ShapeDtypeStruct((2 * n_pairs, s_len, BLK), F32)),
        grid=(n_pairs,),
        in_specs=[col_spec(0), col_spec(n_pairs), col_spec(2 * n_pairs)],
        out_specs=(pl.BlockSpec((s_len, BLK), lambda p: (0, p)),
                   pl.BlockSpec((2, s_len, BLK), lambda p: (p, 0, 0))),
        semantics=("arbitrary",))(proj, proj, proj)


def _hg_masks(mask_ref):
    row = _iota2((BLK, BLK), 0)
    col = _iota2((BLK, BLK), 1)
    for v, m in enumerate(HG_LEVELS):
        same = (row // (2 * m)) == (col // (2 * m))
        mask_ref[v] = (same & ((row & m) != 0) & ((col & m) == 0)).astype(F32)


def _hg_mid(b, m):
    if m >= 4:
        n = BLK // (2 * m)
        mid = b.reshape(n, 2 * m, BLK)[:, m - 1:m, :]
        return jnp.broadcast_to(mid, (n, 2 * m, BLK)).reshape(BLK, BLK)
    pos = _iota2((BLK, BLK), 0) & (2 * m - 1)
    out = b
    for p in range(2 * m):
        delta = (m - 1) - p
        if delta != 0:
            out = jnp.where(pos == p, pltpu.roll(b, (-delta) % BLK, 0), out)
    return out


def _hg_chunk_inputs(qraw, fpre, lb):
    sig = _sigmoid(fpre)
    f = lb + (1.0 - lb) * sig
    g = jnp.log(f)
    q, dq_fac = _silu_and_grad(qraw)
    return q, dq_fac, f, sig, g


HG_GROUP = 4


def _neg_abs(x):
    return lax.bitcast_convert_type(lax.bitcast_convert_type(x, jnp.int32) | jnp.int32(-2 ** 31), F32)


def _hg_level_terms(qs, ks, bs, m):
    es = [jnp.exp(_neg_abs(b - _hg_mid(b, m))) for b in bs]
    qts = [(q * e).astype(BF16) for q, e in zip(qs, es)]
    kts = [(k * e).astype(BF16) for k, e in zip(ks, es)]
    return es, qts, kts


def _hg_load(refs, r0, lb_v, lower_incl):
    q_ref, f_ref, i_ref = refs
    heads = []
    for h in range(HG_GROUP):
        sl = slice(h * HG_HEAD_DIM, (h + 1) * HG_HEAD_DIM)
        heads.append(_hg_chunk_inputs(q_ref[pl.ds(r0, BLK), sl], f_ref[pl.ds(r0, BLK), sl], lb_v[:, sl])
                     + (i_ref[pl.ds(r0, BLK), sl],))
    bs = [_dot_01_l(lower_incl, hd[4]) for hd in heads]
    return heads, bs


def _hgrn_fwd(proj, lb, name):
    s_len = proj.shape[0]
    nc = s_len // BLK
    gw = HG_GROUP * HG_HEAD_DIM
    n_groups = WIDTH // gw
    base = 4 * WIDTH // gw

    def body(q_ref, f_ref, i_ref, lb_ref, o_ref, mask_ref):
        _hg_masks(mask_ref)
        row = _iota2((BLK, BLK), 0)
        col = _iota2((BLK, BLK), 1)
        lower_incl = (col <= row).astype(BF16)
        lb_v = lb_ref[...]

        def chunk(ci, sts):
            r0 = pl.multiple_of(ci * BLK, BLK)
            heads, bs = _hg_load((q_ref, f_ref, i_ref), r0, lb_v, lower_incl)
            qs = [hd[0] for hd in heads]
            ks = [1.0 - hd[2] for hd in heads]
            vs = [hd[5] for hd in heads]
            vbs = [v.astype(BF16) for v in vs]
            b_ends = [b[BLK - 1:BLK, :] for b in bs]
            inters = [_dot_nt((q * jnp.exp(b)).astype(BF16), st.astype(BF16)) for q, b, st in zip(qs, bs, sts)]
            scs = [None] * HG_GROUP
            for v_idx, m in enumerate(HG_LEVELS):
                _, qts, kts = _hg_level_terms(qs, ks, bs, m)
                terms = [_dot_nt(qt, kt) for qt, kt in zip(qts, kts)]
                msk = mask_ref[v_idx]
                scs = [t * msk if sc is None else sc + t * msk for sc, t in zip(scs, terms)]
            intras = [_dot(sc.astype(BF16), vb) for sc, vb in zip(scs, vbs)]
            k_decs = [(k * jnp.exp(b_end - b)).astype(BF16) for k, b, b_end in zip(ks, bs, b_ends)]
            grown = [_dot_tn(vb, k_dec) for vb, k_dec in zip(vbs, k_decs)]
            for h in range(HG_GROUP):
                diag = jnp.sum(qs[h] * ks[h], axis=-1, keepdims=True)
                o_ref[pl.ds(r0, BLK), h * HG_HEAD_DIM:(h + 1) * HG_HEAD_DIM] = inters[h] + intras[h] + diag * vs[h]
            return tuple(st * jnp.exp(b_end) + g for st, b_end, g in zip(sts, b_ends, grown))

        lax.fori_loop(0, nc, chunk, (jnp.zeros((HG_HEAD_DIM, HG_HEAD_DIM), F32),) * HG_GROUP)

    col_spec = lambda off: pl.BlockSpec((s_len, gw), lambda h: (0, off + h))
    return _pcall(
        body, name=name, out_shape=jax.ShapeDtypeStruct((s_len, WIDTH), F32),
        grid=(n_groups,),
        in_specs=[col_spec(base), col_spec(base + n_groups), col_spec(base + 2 * n_groups),
                  pl.BlockSpec((1, gw), lambda h: (0, h))],
        out_specs=pl.BlockSpec((s_len, gw), lambda h: (0, h)),
        scratch_shapes=[pltpu.VMEM((len(HG_LEVELS), BLK, BLK), F32)],
        semantics=("arbitrary",))(proj, proj, proj, lb)


def _rms_heads(o_b, norm_w):
    n_parts, h_parts, r_parts = [], [], []
    for h in range(WIDTH // HG_HEAD_DIM):
        sl = slice(h * HG_HEAD_DIM, (h + 1) * HG_HEAD_DIM)
        o = o_b[:, sl]
        rstd = lax.rsqrt(jnp.mean(o * o, axis=-1, keepdims=True) + RMS_EPS)
        ohat = o * rstd
        h_parts.append(ohat)
        n_parts.append(ohat * norm_w[:, sl])
        r_parts.append(jnp.broadcast_to(rstd, o.shape))
    cat = lambda parts: jnp.concatenate(parts, axis=-1)
    return cat(n_parts), cat(h_parts), cat(r_parts)


def _shift_rows_down(halo, cur, k):
    tm = cur.shape[0]
    ext = jnp.concatenate([halo, cur], axis=0)
    return pltpu.roll(ext, k, 0)[8:8 + tm]


def _shift_rows_up(cur, halo, k):
    tm = cur.shape[0]
    ext = jnp.concatenate([cur, halo], axis=0)
    return pltpu.roll(ext, (tm + 8 - k) % (tm + 8), 0)[0:tm]


def _merge_fwd(x, proj, o_a, o_b, gate, norm_w, conv_w, wb, w_out, ln_g, ln_b, name):
    s_len, d = x.shape
    tm = min(256, s_len)
    hb = tm // 8

    def body(x_ref, oa_ref, za_ref, ob_ref, zb_ref, pre_ref, post_ref, u_ref, zc_ref, hpre_ref, hu_ref, g_ref,
             gate_ref, nw_ref, cw_ref, wb_ref, wo_ref, lg_ref, lbias_ref, xn_ref, mg_ref, yc_ref):
        i = pl.program_id(0)
        sa, _ = _silu_and_grad(za_ref[...])
        y_a = (oa_ref[...] * sa).astype(BF16)
        n_b, _, _ = _rms_heads(ob_ref[...], nw_ref[...])
        sb, _ = _silu_and_grad(zb_ref[...])
        y_b = (n_b * sb).astype(BF16)
        a = pre_ref[...] * u_ref[...]
        halo = jnp.where(i > 0, hpre_ref[...] * hu_ref[...], 0.0)
        cw = cw_ref[...]
        conv = cw[0:1] * _shift_rows_down(halo, a, 2) + cw[1:2] * _shift_rows_down(halo, a, 1) + cw[2:3] * a
        sc, _ = _silu_and_grad(zc_ref[...])
        y_c = (post_ref[...] * conv * sc).astype(BF16)
        merged = None
        for k, yk in enumerate((y_a, y_b, y_c)):
            yc_ref[:, k * WIDTH:(k + 1) * WIDTH] = yk
            term = _gate_sigmoid(g_ref[:, k * d:(k + 1) * d]) * _dot(yk, wb_ref[k])
            merged = term if merged is None else merged + term
        mb = merged.astype(BF16)
        mg_ref[...] = mb
        y = _dot(mb, wo_ref[...])
        r = ALPHA * x_ref[...] + (1.0 + gate_ref[...]) * y
        rhat, _ = _standardize(r)
        xn_ref[...] = rhat * lg_ref[...] + lbias_ref[...]

    wcol = lambda cb: pl.BlockSpec((tm, WIDTH), lambda i: (i, cb))
    halo_spec = lambda cb: pl.BlockSpec((8, WIDTH), lambda i: (jnp.maximum(i * hb - 1, 0), cb))
    vec = lambda w: pl.BlockSpec((1, w), lambda i: (0, 0))
    return _pcall(
        body, name=name,
        out_shape=(jax.ShapeDtypeStruct((s_len, d), F32), jax.ShapeDtypeStruct((s_len, d), BF16),
                   jax.ShapeDtypeStruct((s_len, 3 * WIDTH), BF16)),
        grid=(s_len // tm,),
        in_specs=[pl.BlockSpec((tm, d), lambda i: (i, 0)),
                  wcol(0), wcol(3), wcol(0), wcol(7), wcol(8), wcol(9), wcol(10), wcol(11),
                  halo_spec(8), halo_spec(10),
                  pl.BlockSpec((tm, 3 * d), lambda i: (i, 2)),
                  vec(d), vec(WIDTH),
                  pl.BlockSpec((3, WIDTH), lambda i: (0, 0)),
                  pl.BlockSpec((3, WIDTH, d), lambda i: (0, 0, 0)),
                  pl.BlockSpec((d, d), lambda i: (0, 0)),
                  vec(d), vec(d)],
        out_specs=(pl.BlockSpec((tm, d), lambda i: (i, 0)), pl.BlockSpec((tm, d), lambda i: (i, 0)),
                   pl.BlockSpec((tm, 3 * WIDTH), lambda i: (i, 0))),
        semantics=("arbitrary",))(x, o_a, proj, o_b, proj, proj, proj, proj, proj, proj, proj, proj,
                                  gate, norm_w, conv_w, wb, w_out, ln_g, ln_b)


def _loss_fwd_bwd(y, target):
    s_len, d = y.shape
    tm = min(512, s_len)

    def body(y_ref, t_ref, loss_ref, dy_ref):
        @pl.when(pl.program_id(0) == 0)
        def _():
            loss_ref[...] = jnp.zeros_like(loss_ref)

        e = y_ref[...] - t_ref[...]
        dy_ref[...] = e * (1.0 / d)
        part = jnp.sum(jnp.sum(e * e, axis=-1, keepdims=True), axis=0, keepdims=True)
        loss_ref[...] += part * (0.5 / d)

    tile = pl.BlockSpec((tm, d), lambda i: (i, 0))
    return _pcall(body, name="loss", grid=(s_len // tm,),
                  out_shape=(jax.ShapeDtypeStruct((1, 1), F32), jax.ShapeDtypeStruct((s_len, d), F32)),
                  in_specs=[tile, tile],
                  out_specs=(pl.BlockSpec((1, 1), lambda i: (0, 0)), tile),
                  semantics=("arbitrary",))(y, target)


def _merge_bwd(dxn, x, merged, ycat, proj, gate, wb, w_out, ln_g, name):
    s_len, d = x.shape
    tm = min(256, s_len)
    dsh = d // NDEV
    n_tiles = s_len // tm

    def body(dxn_ref, x_ref, mg_ref, yc_ref, g_ref, gate_ref, wb_ref, wo_ref, lg_ref,
             dres_ref, dyc_ref, dg_ref, gwo_out, gwb_out, vec_ref, gwo_ref, gwb_ref):
        @pl.when(pl.program_id(0) == 0)
        def _():
            gwo_ref[...] = jnp.zeros_like(gwo_ref)
            gwb_ref[...] = jnp.zeros_like(gwb_ref)
            vec_ref[...] = jnp.zeros_like(vec_ref)

        mb = mg_ref[...]
        one_gate = 1.0 + gate_ref[...]
        y = _dot(mb, wo_ref[...])
        r = ALPHA * x_ref[...] + one_gate * y
        rhat, rstd = _standardize(r)
        dxn = dxn_ref[...]
        dr = _standardize_bwd(rhat, rstd, dxn * lg_ref[...])
        vec_ref[0:1, :] += jnp.sum(dxn * rhat, axis=0, keepdims=True)
        vec_ref[1:2, :] += jnp.sum(dxn, axis=0, keepdims=True)
        vec_ref[2:3, :] += jnp.sum(dr * y, axis=0, keepdims=True)
        dres_ref[...] = ALPHA * dr
        dy = (one_gate * dr).astype(BF16)
        gwo_ref[...] += _dot_tn(mb, dy)
        dmerged = _dot_nt(dy, wo_ref[...])
        for k in range(3):
            yk = yc_ref[:, k * WIDTH:(k + 1) * WIDTH]
            sg = _gate_sigmoid(g_ref[:, k * d:(k + 1) * d])
            pk = _dot(yk, wb_ref[k])
            dg_ref[:, k * d:(k + 1) * d] = (dmerged * pk * sg * (1.0 - sg)).astype(BF16)
            dpk = (dmerged * sg).astype(BF16)
            dyc_ref[:, k * WIDTH:(k + 1) * WIDTH] = _dot_nt(dpk, wb_ref[k])
            gwb_ref[k] += _dot_tn(yk, dpk)

        @pl.when(pl.program_id(0) == n_tiles - 1)
        def _():
            for o in range(NDEV):
                gwo_out[o] = gwo_ref[o * dsh:(o + 1) * dsh, :].astype(BF16)
                for k in range(3):
                    gwb_out[o, k] = gwb_ref[k, :, o * dsh:(o + 1) * dsh].astype(BF16)

    tile = lambda w: pl.BlockSpec((tm, w), lambda i: (i, 0))
    vec = pl.BlockSpec((1, d), lambda i: (0, 0))
    return _pcall(
        body, name=name,
        out_shape=(jax.ShapeDtypeStruct((s_len, d), F32), jax.ShapeDtypeStruct((s_len, 3 * WIDTH), F32),
                   jax.ShapeDtypeStruct(proj.shape, BF16), jax.ShapeDtypeStruct((NDEV, dsh, d), BF16),
                   jax.ShapeDtypeStruct((NDEV, 3, WIDTH, dsh), BF16), jax.ShapeDtypeStruct((8, d), F32)),
        grid=(n_tiles,),
        in_specs=[tile(d), tile(d), tile(d), tile(3 * WIDTH),
                  pl.BlockSpec((tm, 3 * d), lambda i: (i, 2)),
                  vec, pl.BlockSpec((3, WIDTH, d), lambda i: (0, 0, 0)),
                  pl.BlockSpec((d, d), lambda i: (0, 0)), vec],
        out_specs=(tile(d), tile(3 * WIDTH), pl.BlockSpec((tm, 3 * d), lambda i: (i, 2)),
                   pl.BlockSpec((NDEV, dsh, d), lambda i: (0, 0, 0)),
                   pl.BlockSpec((NDEV, 3, WIDTH, dsh), lambda i: (0, 0, 0, 0)),
                   pl.BlockSpec((8, d), lambda i: (0, 0))),
        scratch_shapes=[pltpu.VMEM((d, d), F32), pltpu.VMEM((3, WIDTH, d), F32)],
        semantics=("arbitrary",))(dxn, x, merged, ycat, proj, gate, wb, w_out, ln_g)


def _branch_bwd(dycat, proj, o_a, o_b, norm_w, conv_w, dproj, name):
    s_len = proj.shape[0]
    tm = min(256, s_len)
    hb = tm // 8
    n_tiles = s_len // tm

    def body(dya_ref, dyb_ref, dyc_ref, oa_ref, za_ref, ob_ref, zb_ref, pre_ref, post_ref, u_ref, zc_ref,
             hpre_ref, hu_ref, ndyc_ref, npost_ref, nzc_ref, nw_ref, cw_ref, dproj_in,
             dproj_ref, doa_ref, dob_ref, vec_ref, dza_scr, dzb_scr, dc_scr, sems):
        del dproj_in
        i = pl.program_id(0)

        @pl.when(i == 0)
        def _():
            vec_ref[...] = jnp.zeros_like(vec_ref)

        sa, dsa = _silu_and_grad(za_ref[...])
        dya = dya_ref[...]
        doa_ref[...] = dya * sa
        dza_scr[...] = (dya * oa_ref[...] * dsa).astype(BF16)
        nw = nw_ref[...]
        n_b, ohat, rstd = _rms_heads(ob_ref[...], nw)
        sb, dsb = _silu_and_grad(zb_ref[...])
        dyb = dyb_ref[...]
        dzb_scr[...] = (dyb * n_b * dsb).astype(BF16)
        dn = dyb * sb
        vec_ref[0:1, :] += jnp.sum(dn * ohat, axis=0, keepdims=True)
        dnw = dn * nw
        parts = []
        for h in range(WIDTH // HG_HEAD_DIM):
            sl = slice(h * HG_HEAD_DIM, (h + 1) * HG_HEAD_DIM)
            m2 = jnp.mean(dnw[:, sl] * ohat[:, sl], axis=-1, keepdims=True)
            parts.append(rstd[:, sl] * (dnw[:, sl] - ohat[:, sl] * m2))
        dob_ref[...] = jnp.concatenate(parts, axis=-1)
        cw = cw_ref[...]
        pre, u, post = pre_ref[...], u_ref[...], post_ref[...]
        a = pre * u
        halo = jnp.where(i > 0, hpre_ref[...] * hu_ref[...], 0.0)
        a1 = _shift_rows_down(halo, a, 1)
        a2 = _shift_rows_down(halo, a, 2)
        conv = cw[0:1] * a2 + cw[1:2] * a1 + cw[2:3] * a
        sc, dsc = _silu_and_grad(zc_ref[...])
        dyc = dyc_ref[...]
        dconv = dyc * post * sc
        nsc, _ = _silu_and_grad(nzc_ref[...])
        nxt = jnp.where(i < n_tiles - 1, ndyc_ref[...] * npost_ref[...] * nsc, 0.0)
        da = cw[2:3] * dconv + cw[1:2] * _shift_rows_up(dconv, nxt, 1) + cw[0:1] * _shift_rows_up(dconv, nxt, 2)
        dc_scr[:, 0 * WIDTH:1 * WIDTH] = (da * u).astype(BF16)
        dc_scr[:, 1 * WIDTH:2 * WIDTH] = (dyc * conv * sc).astype(BF16)
        dc_scr[:, 2 * WIDTH:3 * WIDTH] = (da * pre).astype(BF16)
        dc_scr[:, 3 * WIDTH:4 * WIDTH] = (dyc * post * conv * dsc).astype(BF16)
        vec_ref[1:2, :] += jnp.sum(dconv * a2, axis=0, keepdims=True)
        vec_ref[2:3, :] += jnp.sum(dconv * a1, axis=0, keepdims=True)
        vec_ref[3:4, :] += jnp.sum(dconv * a, axis=0, keepdims=True)
        rows = pl.ds(pl.multiple_of(i * tm, tm), tm)
        copies = [pltpu.make_async_copy(dza_scr, dproj_ref.at[rows, 3 * WIDTH:4 * WIDTH], sems.at[0]),
                  pltpu.make_async_copy(dzb_scr, dproj_ref.at[rows, 7 * WIDTH:8 * WIDTH], sems.at[1]),
                  pltpu.make_async_copy(dc_scr, dproj_ref.at[rows, 8 * WIDTH:12 * WIDTH], sems.at[2])]
        for cp in copies:
            cp.start()
        for cp in copies:
            cp.wait()

    wcol = lambda cb: pl.BlockSpec((tm, WIDTH), lambda i: (i, cb))
    prev = lambda cb: pl.BlockSpec((8, WIDTH), lambda i: (jnp.maximum(i * hb - 1, 0), cb))
    nxt = lambda cb: pl.BlockSpec((8, WIDTH), lambda i: (jnp.minimum((i + 1) * hb, s_len // 8 - 1), cb))
    anyspec = pl.BlockSpec(memory_space=pl.ANY)
    out = jax.ShapeDtypeStruct((s_len, WIDTH), F32)
    return _pcall(
        body, name=name,
        out_shape=(jax.ShapeDtypeStruct(dproj.shape, dproj.dtype), out, out, jax.ShapeDtypeStruct((8, WIDTH), F32)),
        grid=(n_tiles,),
        in_specs=[wcol(0), wcol(1), wcol(2), wcol(0), wcol(3), wcol(0), wcol(7), wcol(8), wcol(9), wcol(10), wcol(11),
                  prev(8), prev(10), nxt(2), nxt(9), nxt(11),
                  pl.BlockSpec((1, WIDTH), lambda i: (0, 0)), pl.BlockSpec((3, WIDTH), lambda i: (0, 0)), anyspec],
        out_specs=(anyspec, wcol(0), wcol(0), pl.BlockSpec((8, WIDTH), lambda i: (0, 0))),
        scratch_shapes=[pltpu.VMEM((tm, WIDTH), BF16), pltpu.VMEM((tm, WIDTH), BF16),
                        pltpu.VMEM((tm, 4 * WIDTH), BF16), pltpu.SemaphoreType.DMA((3,))],
        aliases={18: 0},
        semantics=("arbitrary",))(dycat, dycat, dycat, o_a, proj, o_b, proj, proj, proj, proj, proj,
                                  proj, proj, dycat, proj, proj, norm_w, conv_w, dproj)


def _sb_bwd(proj, do_a, totals, dproj, name):
    s_len = proj.shape[0]
    n_pairs = WIDTH // BLK
    scale = SB_HEAD_DIM ** -0.5
    qr = min(SB_Q_ROWS, s_len)
    gb = SB_K_BLOCKS
    kw = gb * BLK
    nq = s_len // qr
    assert qr == kw

    def body(q_ref, k_ref, v_ref, do_ref, tot_ref, dproj_in, dproj_ref, dq_ref, dk_ref, dv_ref, out_scr, sems):
        del dproj_in
        lane = _iota2((1, BLK), 1)
        row = _iota2((BLK, BLK), 0)
        col = _iota2((BLK, BLK), 1)
        ones = jnp.ones((BLK, BLK), BF16)
        twice = lambda m: jnp.concatenate([m, m], axis=0)
        before_and_sum = twice(jnp.concatenate([(row < col).astype(BF16), ones], axis=1))
        upto_and_sum = twice(jnp.concatenate([(row <= col).astype(BF16), ones], axis=1))
        strict = _iota2((qr, kw), 1) < _iota2((qr, kw), 0)
        head_lanes = [(lane // SB_HEAD_DIM) == hh for hh in range(2)]
        dk_ref[...] = jnp.zeros_like(dk_ref)
        dv_ref[...] = jnp.zeros_like(dv_ref)

        def scores(gi, qms, masked):
            c0 = pl.multiple_of(gi * kw, kw)
            kb = k_ref[pl.ds(c0, kw), :].astype(BF16)
            z2s = [_dot_nt(qms[hh], kb) for hh in range(2)]
            if masked:
                z2s = [jnp.where(strict, z2, MASKED_SCORE) for z2 in z2s]
            return tuple(z2s)

        def process(gi, z2s, qms, doms, totals_i, carry):
            c0 = pl.multiple_of(gi * kw, kw)
            kf = k_ref[pl.ds(c0, kw), :]
            vf = v_ref[pl.ds(c0, kw), :]
            kms = [jnp.where(head_lanes[hh], kf, 0.0).astype(BF16) for hh in range(2)]
            vms = [jnp.where(head_lanes[hh], vf, 0.0).astype(BF16) for hh in range(2)]
            das = [_dot_nt(doms[hh], vms[hh]) for hh in range(2)]
            halves = [_softplus2_parts(z2) for z2 in z2s]
            terms = [[_split2_lanes(sp2[:, b * BLK:(b + 1) * BLK]) for b in range(gb)] for sp2, _ in halves]
            sums = [[_dot(t, before_and_sum) for t in head_terms] for head_terms in terms]
            weights, gmats, l_befores = [], [], []
            for hh in range(2):
                l_before = carry[3 * hh + 1]
                parts = []
                for b in range(gb):
                    parts.append(totals_i[hh] - l_before - sums[hh][b][:, :BLK])
                    l_before = l_before + sums[hh][b][:, BLK:]
                a = jnp.exp2(z2s[hh] - jnp.concatenate(parts, axis=1))
                weights.append(a.astype(BF16))
                gmats.append(a * das[hh])
                l_befores.append(l_before)
            terms = [[_split2_lanes(g[:, b * BLK:(b + 1) * BLK]) for b in range(gb)] for g in gmats]
            sums = [[_dot(t, upto_and_sum) for t in head_terms] for head_terms in terms]
            dzs, g_befores = [], []
            for hh in range(2):
                g_before = carry[3 * hh + 2]
                parts = []
                for b in range(gb):
                    parts.append(g_before + sums[hh][b][:, :BLK])
                    g_before = g_before + sums[hh][b][:, BLK:]
                dzs.append((gmats[hh] - halves[hh][1] * jnp.concatenate(parts, axis=1)).astype(BF16))
                g_befores.append(g_before)
            dk_t = _dot_tn(jnp.concatenate(qms, axis=0), jnp.concatenate(dzs, axis=0))
            dv_t = _dot_tn(jnp.concatenate(doms, axis=0), jnp.concatenate(weights, axis=0))
            dqs = [_dot(dzs[hh], kms[hh]) for hh in range(2)]
            dk_ref[:, pl.ds(c0, kw)] += dk_t * (1.0 / LOG2E)
            dv_ref[:, pl.ds(c0, kw)] += dv_t
            return (carry[0] + dqs[0], l_befores[0], g_befores[0], carry[3] + dqs[1], l_befores[1], g_befores[1])

        def queries(i):
            qf = q_ref[pl.ds(pl.multiple_of(i * qr, qr), qr), :] * (scale * LOG2E)
            return [jnp.where(head_lanes[hh], qf, 0.0).astype(BF16) for hh in range(2)]

        def qtile(i, first_scores):
            r0 = pl.multiple_of(i * qr, qr)
            qms = queries(i)
            dof = do_ref[pl.ds(r0, qr), :]
            doms = [jnp.where(head_lanes[hh], dof, 0.0).astype(BF16) for hh in range(2)]
            totals_i = [tot_ref[hh, pl.ds(r0, qr), :] for hh in range(2)]
            zero = jnp.zeros((qr, BLK), F32)

            def step(gi, state):
                return scores(gi + 1, qms, False) + process(gi, state[:2], qms, doms, totals_i, state[2:])

            def before_diagonal(state):
                return scores(i, qms, True) + process(i - 1, state[:2], qms, doms, totals_i, state[2:])

            state = lax.fori_loop(0, i - 1, step, first_scores + (zero,) * 6)
            state = lax.cond(i > 0, before_diagonal, lambda st: st, state)
            nxt = jnp.minimum(i + 1, nq - 1)
            next_scores = scores(0, queries(nxt), False)
            carry = process(i, state[:2], qms, doms, totals_i, state[2:])
            dq_ref[pl.ds(r0, qr), :] = (carry[0] + carry[3]) * scale
            return next_scores

        lax.fori_loop(0, nq, qtile, scores(0, queries(0), True))
        pair = pl.program_id(0)
        copies = []
        for t, value in enumerate((dq_ref[...], dk_ref[...].T, dv_ref[...].T)):
            out_scr[t] = value.astype(BF16)
            col = pl.multiple_of((t * n_pairs + pair) * BLK, BLK)
            copies.append(pltpu.make_async_copy(out_scr.at[t], dproj_ref.at[:, pl.ds(col, BLK)], sems.at[t]))
            copies[-1].start()
        for cp in copies:
            cp.wait()

    col_spec = lambda off: pl.BlockSpec((s_len, BLK), lambda p: (0, off + p))
    anyspec = pl.BlockSpec(memory_space=pl.ANY)
    return _pcall(
        body, name=name, out_shape=jax.ShapeDtypeStruct(dproj.shape, dproj.dtype), grid=(n_pairs,),
        in_specs=[col_spec(0), col_spec(n_pairs), col_spec(2 * n_pairs), col_spec(0),
                  pl.BlockSpec((2, s_len, BLK), lambda p: (p, 0, 0)), anyspec],
        out_specs=anyspec,
        scratch_shapes=[pltpu.VMEM((s_len, BLK), F32), pltpu.VMEM((BLK, s_len), F32), pltpu.VMEM((BLK, s_len), F32),
                        pltpu.VMEM((3, s_len, BLK), BF16), pltpu.SemaphoreType.DMA((3,))],
        aliases={5: 0},
        semantics=("arbitrary",))(proj, proj, proj, do_a, totals, dproj)


def _hgrn_bwd(proj, do_b, lb, dproj, name):
    s_len = proj.shape[0]
    nc = s_len // BLK
    gw = HG_GROUP * HG_HEAD_DIM
    n_groups = WIDTH // gw
    base = 4 * WIDTH // gw
    heads_of = range(HG_GROUP)

    def body(q_ref, f_ref, i_ref, do_ref, lb_ref, dproj_in, dproj_ref, dlb_ref, mask_ref, st_ref, out_scr, sems):
        del dproj_in
        _hg_masks(mask_ref)
        row = _iota2((BLK, BLK), 0)
        col = _iota2((BLK, BLK), 1)
        lower_incl = (col <= row).astype(BF16)
        upper_incl = (col >= row).astype(BF16)
        lb_v = lb_ref[...]
        refs = (q_ref, f_ref, i_ref)

        def fwd_chunk(ci, sts):
            for h in heads_of:
                st_ref[ci, h] = sts[h]
            heads, bs = _hg_load(refs, pl.multiple_of(ci * BLK, BLK), lb_v, lower_incl)
            b_ends = [b[BLK - 1:BLK, :] for b in bs]
            k_decs = [((1.0 - hd[2]) * jnp.exp(b_end - b)).astype(BF16) for hd, b, b_end in zip(heads, bs, b_ends)]
            grown = [_dot_tn(hd[5].astype(BF16), k_dec) for hd, k_dec in zip(heads, k_decs)]
            return tuple(st * jnp.exp(b_end) + g for st, b_end, g in zip(sts, b_ends, grown))

        zero_state = (jnp.zeros((HG_HEAD_DIM, HG_HEAD_DIM), F32),) * HG_GROUP
        lax.fori_loop(0, nc, fwd_chunk, zero_state)

        def bwd_chunk(cc, carry):
            dsts, suffixes, dlbs = carry
            ci = nc - 1 - cc
            r0 = pl.multiple_of(ci * BLK, BLK)
            heads, bs = _hg_load(refs, r0, lb_v, lower_incl)
            qs = [hd[0] for hd in heads]
            fs = [hd[2] for hd in heads]
            ks = [1.0 - f for f in fs]
            vs = [hd[5] for hd in heads]
            vbs = [v.astype(BF16) for v in vs]
            dos = [do_ref[pl.ds(r0, BLK), h * HG_HEAD_DIM:(h + 1) * HG_HEAD_DIM] for h in heads_of]
            dobs = [do.astype(BF16) for do in dos]
            b_ends = [b[BLK - 1:BLK, :] for b in bs]
            e_qs = [jnp.exp(b) for b in bs]
            e_ks = [jnp.exp(b_end - b) for b, b_end in zip(bs, b_ends)]
            qes = [(q * e).astype(BF16) for q, e in zip(qs, e_qs)]
            khs = [(k * e).astype(BF16) for k, e in zip(ks, e_ks)]
            st_terms = [_split2_lanes(st_ref[ci, h]) for h in heads_of]
            ds_terms = [_split2_lanes(dst) for dst in dsts]
            dqes = [_dot(dob, t[:, :HG_HEAD_DIM]) + _dot(dob, t[:, HG_HEAD_DIM:]) for dob, t in zip(dobs, st_terms)]
            dkhs = [_dot(vb, t[:, :HG_HEAD_DIM]) + _dot(vb, t[:, HG_HEAD_DIM:]) for vb, t in zip(vbs, ds_terms)]
            dvs = [_dot_nt(kh, t[:, :HG_HEAD_DIM]) for kh, t in zip(khs, ds_terms)]
            grown = [_dot_tn(dob, qe) for dob, qe in zip(dobs, qes)]
            das = [_dot_nt(dob, vb) for dob, vb in zip(dobs, vbs)]
            dqs = [e * dqe for e, dqe in zip(e_qs, dqes)]
            dks = [e * dkh for e, dkh in zip(e_ks, dkhs)]
            dlogs = [qe.astype(F32) * dqe - kh.astype(F32) * dkh for qe, dqe, kh, dkh in zip(qes, dqes, khs, dkhs)]
            scs = [None] * HG_GROUP
            for v_idx, m in enumerate(HG_LEVELS):
                es, qms, kms = _hg_level_terms(qs, ks, bs, m)
                msk = mask_ref[v_idx]
                terms = [_dot_nt(qm, km) for qm, km in zip(qms, kms)]
                pms = [(da * msk).astype(BF16) for da in das]
                dqms = [_dot(pm, km) for pm, km in zip(pms, kms)]
                dkms = [_dot_tn(pm, qm) for pm, qm in zip(pms, qms)]
                scs = [t * msk if sc is None else sc + t * msk for sc, t in zip(scs, terms)]
                dqs = [dq + dqm * e for dq, dqm, e in zip(dqs, dqms, es)]
                dks = [dk + dkm * e for dk, dkm, e in zip(dks, dkms, es)]
                dlogs = [dl + (qm.astype(F32) * dqm - km.astype(F32) * dkm)
                         for dl, qm, dqm, km, dkm in zip(dlogs, qms, dqms, kms, dkms)]
            intras = [_dot_tn(sc.astype(BF16), dob) for sc, dob in zip(scs, dobs)]
            dgs = [_dot_01_l(upper_incl, dl) + sfx for dl, sfx in zip(dlogs, suffixes)]
            new_dlbs = []
            for h in heads_of:
                q, dq_fac, f, sig = heads[h][0], heads[h][1], heads[h][2], heads[h][3]
                a_diag = jnp.sum(dos[h] * vs[h], axis=-1, keepdims=True)
                s_diag = jnp.sum(q * ks[h], axis=-1, keepdims=True)
                dq = dqs[h] + a_diag * ks[h]
                dk = dks[h] + a_diag * q
                dv = dvs[h] + intras[h] + s_diag * dos[h]
                dfull = dgs[h] / f - dk
                sl = slice(h * HG_HEAD_DIM, (h + 1) * HG_HEAD_DIM)
                out_scr[0, pl.ds(r0, BLK), sl] = (dq * dq_fac).astype(BF16)
                out_scr[1, pl.ds(r0, BLK), sl] = (dfull * (1.0 - lb_v[:, sl]) * sig * (1.0 - sig)).astype(BF16)
                out_scr[2, pl.ds(r0, BLK), sl] = dv.astype(BF16)
                new_dlbs.append(dlbs[h] + jnp.sum(dfull * (1.0 - sig), axis=0, keepdims=True))
            new_dsts = tuple(dst * jnp.exp(b_end) + g for dst, b_end, g in zip(dsts, b_ends, grown))
            return new_dsts, tuple(dg[0:1, :] for dg in dgs), tuple(new_dlbs)

        zero_row = (jnp.zeros((1, HG_HEAD_DIM), F32),) * HG_GROUP
        _, _, dlbs = lax.fori_loop(0, nc, bwd_chunk, (zero_state, zero_row, zero_row))
        dlb_ref[...] = jnp.broadcast_to(jnp.concatenate(dlbs, axis=1), dlb_ref.shape)
        group = pl.program_id(0)
        copies = []
        for t in range(3):
            col = pl.multiple_of((base + t * n_groups + group) * gw, gw)
            copies.append(pltpu.make_async_copy(out_scr.at[t], dproj_ref.at[:, pl.ds(col, gw)], sems.at[t]))
            copies[-1].start()
        for cp in copies:
            cp.wait()

    col_spec = lambda off: pl.BlockSpec((s_len, gw), lambda h: (0, off + h))
    anyspec = pl.BlockSpec(memory_space=pl.ANY)
    return _pcall(
        body, name=name,
        out_shape=(jax.ShapeDtypeStruct(dproj.shape, dproj.dtype), jax.ShapeDtypeStruct((8, WIDTH), F32)),
        grid=(n_groups,),
        in_specs=[col_spec(base), col_spec(base + n_groups), col_spec(base + 2 * n_groups), col_spec(0),
                  pl.BlockSpec((1, gw), lambda h: (0, h)), anyspec],
        out_specs=(anyspec, pl.BlockSpec((8, gw), lambda h: (0, h))),
        scratch_shapes=[pltpu.VMEM((len(HG_LEVELS), BLK, BLK), F32),
                        pltpu.VMEM((nc, HG_GROUP, HG_HEAD_DIM, HG_HEAD_DIM), F32),
                        pltpu.VMEM((3, s_len, gw), BF16), pltpu.SemaphoreType.DMA((3,))],
        aliases={5: 0},
        semantics=("arbitrary",))(proj, proj, proj, do_b, lb, dproj)


def _dh_matmul(dproj, w_full, after, name):
    s_len, n = dproj.shape
    d = w_full.shape[0]
    tm = min(1024, s_len)
    tk = 4608

    def body(dp_ref, w_ref, after_ref, dh_ref):
        del after_ref
        part = _dot_nt(dp_ref[...], w_ref[...])

        @pl.when(pl.program_id(1) == 0)
        def _():
            dh_ref[...] = part

        @pl.when(pl.program_id(1) > 0)
        def _():
            dh_ref[...] += part

    return _pcall(
        body, name=name, out_shape=jax.ShapeDtypeStruct((s_len, d), F32),
        grid=(s_len // tm, n // tk),
        in_specs=[pl.BlockSpec((tm, tk), lambda i, k: (i, k)), pl.BlockSpec((d, tk), lambda i, k: (0, k)),
                  pl.BlockSpec(memory_space=pl.ANY)],
        out_specs=pl.BlockSpec((tm, d), lambda i, k: (i, 0)),
        semantics=("arbitrary", "arbitrary"))(dproj, w_full, after)


def _gw_matmul(h_t, dproj, name):
    d, s_len = h_t.shape
    n = dproj.shape[1]
    tn = 2304

    def body(ht_ref, dp_ref, gw_ref):
        gw_ref[...] = _dot(ht_ref[...], dp_ref[...]).astype(BF16)

    return _pcall(
        body, name=name, out_shape=jax.ShapeDtypeStruct((d, n), BF16),
        grid=(n // tn,),
        in_specs=[pl.BlockSpec((d, s_len), lambda j: (0, 0)), pl.BlockSpec((s_len, tn), lambda j: (0, j))],
        out_specs=pl.BlockSpec((d, tn), lambda j: (0, j)),
        semantics=("arbitrary",))(h_t, dproj)


def _ln_bwd(dh, x, scale, dres, name):
    s_len, d = x.shape
    tm = min(512, s_len)

    def body(dh_ref, x_ref, sc_ref, dres_ref, dx_ref, vec_ref):
        @pl.when(pl.program_id(0) == 0)
        def _():
            vec_ref[...] = jnp.zeros_like(vec_ref)

        dh = dh_ref[...]
        xs, rstd = _standardize(x_ref[...])
        vec_ref[0:1, :] += jnp.sum(dh, axis=0, keepdims=True)
        vec_ref[1:2, :] += jnp.sum(dh * xs, axis=0, keepdims=True)
        dx_ref[...] = _standardize_bwd(xs, rstd, dh * (1.0 + sc_ref[...])) + dres_ref[...]

    tile = pl.BlockSpec((tm, d), lambda i: (i, 0))
    return _pcall(body, name=name, grid=(s_len // tm,),
                  out_shape=(jax.ShapeDtypeStruct((s_len, d), F32), jax.ShapeDtypeStruct((8, d), F32)),
                  in_specs=[tile, tile, pl.BlockSpec((1, d), lambda i: (0, 0)), tile],
                  out_specs=(tile, pl.BlockSpec((8, d), lambda i: (0, 0))),
                  semantics=("arbitrary",))(dh, x, scale, dres)


def _wmod_grad(c_t, dmod):
    d = c_t.shape[0]
    n_layers, _, cm = dmod.shape

    def body(c_ref, dm_ref, o_ref):
        for l in range(n_layers):
            acc = None
            for b in range(NDEV):
                term = c_ref[:, b:b + 1] * dm_ref[l, b:b + 1, :]
                acc = term if acc is None else acc + term
            o_ref[l] = acc

    return _pcall(body, name="wmod_grad", out_shape=jax.ShapeDtypeStruct((n_layers, d, cm), F32))(c_t, dmod)


def _sum_adamw(parts, w, m, v, name, first_row=0, into=None, after=None):
    n_src, range_rows, cols = parts.shape
    rows = w.shape[0]
    tr = range_rows
    for cand in (512, 256, 128, 64, 32, 16, 8):
        if range_rows % cand == 0 and cand * cols * 4 <= (2 << 20):
            tr = cand
            break
    first_tile = first_row // tr
    assert first_row % tr == 0
    n_extra = (0 if into is None else 4) + (0 if after is None else 1)

    def body(p_ref, w_ref, m_ref, v_ref, *rest):
        g_ref, d_ref, nm_ref, nv_ref = rest[n_extra:]
        g = p_ref[0].astype(F32)
        for s in range(1, n_src):
            g = g + p_ref[s].astype(F32)
        g_ref[...] = g
        d_ref[...], nm_ref[...], nv_ref[...] = _adamw_step(g, w_ref[...], m_ref[...], v_ref[...])

    tile = pl.BlockSpec((tr, cols), lambda i: (i + first_tile, 0))
    anyspec = pl.BlockSpec(memory_space=pl.ANY)
    out = jax.ShapeDtypeStruct((rows, cols), F32)
    extra = ([] if into is None else list(into)) + ([] if after is None else [after])
    aliases = {} if into is None else {4 + k: k for k in range(4)}
    return _pcall(body, name=name, grid=(range_rows // tr,), out_shape=(out,) * 4,
                  in_specs=[pl.BlockSpec((n_src, tr, cols), lambda i: (0, i, 0)), tile, tile, tile]
                  + [anyspec] * len(extra),
                  out_specs=(tile,) * 4, aliases=aliases, semantics=("arbitrary",))(parts, w, m, v, *extra)


def _adamw_step(g, w, m, v):
    nm = ADAM_B1 * m + (1.0 - ADAM_B1) * g
    nv = ADAM_B2 * v + (1.0 - ADAM_B2) * (g * g)
    m_hat = nm / (1.0 - ADAM_B1 ** ADAM_STEP)
    v_hat = nv / (1.0 - ADAM_B2 ** ADAM_STEP)
    return -ADAM_LR * (m_hat / (jnp.sqrt(v_hat) + ADAM_EPS) + ADAM_WD * w), nm, nv


def _adamw_small(gs, ws, ms, vs):
    n = len(gs)

    def body(*refs):
        for p in range(n):
            results = _adamw_step(*(refs[k * n + p][...] for k in range(4)))
            for k in range(3):
                refs[(4 + k) * n + p][...] = results[k]

    shapes = [jax.ShapeDtypeStruct(w.shape, F32) for w in ws]
    outs = _pcall(body, name="adamw_small", out_shape=shapes * 3)(*gs, *ws, *ms, *vs)
    return [(outs[p], outs[n + p], outs[2 * n + p]) for p in range(n)]


def _sum_parts(parts, name):
    n_src = parts.shape[0]

    def body(p_ref, o_ref):
        acc = p_ref[0]
        for s in range(1, n_src):
            acc = acc + p_ref[s]
        o_ref[...] = acc

    return _pcall(body, name=name, out_shape=jax.ShapeDtypeStruct(parts.shape[1:], F32))(parts)


def _pair_sum(gw, stage, me, name):
    d = gw.shape[0]
    n_slots, _, shard = stage.shape

    def body(me_ref, g_ref, s_ref, own_ref, o_ref):
        del me_ref
        total = (g_ref[...].astype(F32) + s_ref[0].astype(F32)).astype(BF16)
        o_ref[0] = total

        @pl.when(pl.program_id(0) == 0)
        def _():
            own_ref[0] = total

    slot = pl.BlockSpec((1, d, shard), lambda jj, me_ref: (jj, 0, 0))
    out = jax.ShapeDtypeStruct(stage.shape, BF16)
    return pl.pallas_call(
        body, name=name, out_shape=(out, out),
        grid_spec=pltpu.PrefetchScalarGridSpec(
            num_scalar_prefetch=1, grid=(n_slots,),
            in_specs=[pl.BlockSpec((d, shard), lambda jj, me_ref: (0, me_ref[0] ^ (2 * jj))), slot],
            out_specs=(pl.BlockSpec((1, d, shard), lambda jj, me_ref: (0, 0, 0)), slot)),
        compiler_params=pltpu.CompilerParams(dimension_semantics=("arbitrary",), vmem_limit_bytes=VMEM_LIMIT),
        interpret=False)(me.reshape(1).astype(jnp.int32), gw, stage)


def _lower_bound_table(lower_bounds):
    p = jax.nn.softmax(lower_bounds.astype(F32), axis=0)
    return jnp.cumsum(p, axis=0) - p[0:1]


def _pad_rows(v, width):
    n = v.shape[0]
    rows = -(-n // width)
    rows = -(-rows // 8) * 8
    return jnp.pad(v, (0, rows * width - n)).reshape(rows, width)


def kernel(x, c, w_mod, b_mod, w_in, conv_w, hgrn_norm_w, lower_bounds, w_branch, w_out, ln_g, ln_b, loss_target, m_w_mod, m_b_mod, m_w_in, m_conv_w, m_hgrn_norm_w, m_lower_bounds, m_w_branch, m_w_out, m_ln_g, m_ln_b, v_w_mod, v_b_mod, v_w_in, v_conv_w, v_hgrn_norm_w, v_lower_bounds, v_w_branch, v_w_out, v_ln_g, v_ln_b):
    n_layers = N_LAYERS
    s_len, d = x.shape[1], x.shape[2]
    n_cols = w_in.shape[2] * NDEV
    cw_cols = conv_w.shape[2]
    cm = w_mod.shape[2]
    me = _my_index()
    x0 = x[0]
    target = loss_target[0]

    small = _pad_rows(jnp.concatenate([c.reshape(-1), conv_w.reshape(-1)]), BLK)
    small_all = _all_gather_small("gather_c_conv", small).reshape(NDEV, -1)
    c_all = small_all[:, :d]
    conv_full = small_all[:, d:d + n_layers * 3 * cw_cols].reshape(NDEV, n_layers, 3, cw_cols)
    conv_full = conv_full.transpose(1, 2, 0, 3).reshape(n_layers, 3, WIDTH)

    b_mod_mine = lax.dynamic_slice_in_dim(b_mod, me * cm, cm, axis=1).reshape(n_layers, 1, cm)
    mod_cols = _mod_fwd(c_all, w_mod, b_mod_mine)
    mod_all = _all_gather_small("gather_mod", mod_cols.reshape(n_layers * NDEV, cm))
    mod_all = mod_all.reshape(NDEV, n_layers, NDEV, cm)
    mod_mine = lax.dynamic_index_in_dim(mod_all, me, axis=2, keepdims=False)
    mod_mine = mod_mine.transpose(1, 0, 2).reshape(n_layers, 3, 1, d)

    shard = w_in.shape[2]
    dsh = d // NDEV
    w_in_b, w_branch_b, w_out_b = w_in.astype(BF16), w_branch.astype(BF16), w_out.astype(BF16)
    window = lambda ref, dev: ref.at[:, pl.ds(pl.multiple_of(dev * shard, BLK), shard)]

    def two_step_sends(places):
        chips, sibling = [], []
        for k in (1, 2, 4, 6):
            for a, place in enumerate(places):
                chips.append((k, lambda ins, lands, me, a=a: ins[a],
                              lambda lands, me, a=a, place=place: place(lands[a], me),
                              lambda lands, me, a=a, k=k, place=place: place(lands[a], me ^ k)))
        for j in (2, 4, 6):
            for a, place in enumerate(places):
                sibling.append((1, lambda ins, lands, me, a=a, j=j, place=place: place(lands[a], me ^ j),
                                lambda lands, me, a=a, j=j, place=place: place(lands[a], me ^ j),
                                lambda lands, me, a=a, j=j, place=place: place(lands[a], me ^ 1 ^ j)))
        return chips, sibling

    in_sends = two_step_sends([window])
    rest_sends = two_step_sends([_slot, _slot])
    layer_sends = two_step_sends([window, _slot, _slot])

    def in_land(l):
        return _place_own_window(f"place_w_in_{l}", (d, n_cols), w_in_b[l], me)

    def rest_lands(l):
        return [_place_own((NDEV, 3, WIDTH, dsh), BF16, w_branch_b[l][None], (me, 0, 0, 0)),
                _place_own((NDEV, dsh, d), BF16, w_out_b[l][None], (me, 0, 0))]

    def gather_start(name, shards, lands, sends, after):
        return _exchange_start(f"{name}_chips_start", shards, lands, sends[0], after)

    def gather_pass_on(name, started, after, sends):
        _, lands = _exchange_wait(f"{name}_chips_wait", started, after, sends[0])
        return _exchange_start(f"{name}_sibling_start", [], lands, sends[1])

    def gather_finish(name, started, after, sends):
        return _exchange_wait(f"{name}_sibling_wait", started, after, sends[1])[1]

    def branch_out_weights(w_branch_l, w_out_l):
        return w_branch_l.transpose(1, 2, 0, 3).reshape(3, WIDTH, d), w_out_l.reshape(d, d)

    gathering = gather_start("gather_w_in_0", [w_in_b[0]], [in_land(0)], in_sends, mod_mine)
    passing = gather_pass_on("gather_w_in_0", gathering, gathering[4], in_sends)
    rest_gathering = gather_start("gather_rest_0", [w_branch_b[0], w_out_b[0]], rest_lands(0), rest_sends, passing[4])
    next_gathering = None
    if n_layers > 1:
        next_gathering = gather_start("gather_weights_1", [w_in_b[1], w_branch_b[1], w_out_b[1]],
                                      [in_land(1)] + rest_lands(1), layer_sends, rest_gathering[4])
    w_in_l = gather_finish("gather_w_in_0", passing, (next_gathering or rest_gathering)[4], in_sends)[0]

    lbs = _lower_bound_table(lower_bounds)
    norm_w4 = jnp.tile(hgrn_norm_w, (1, WIDTH // HG_HEAD_DIM))

    saved = []
    xl = x0
    for l in range(n_layers):
        shift, scale, gate = mod_mine[l, 0], mod_mine[l, 1], mod_mine[l, 2]
        proj, h_t = _ln_proj(xl, shift, scale, w_in_l, f"ln_proj_{l}")
        o_a, totals = _sb_fwd(proj, f"sb_fwd_{l}")
        if l == 0:
            rest_passing = gather_pass_on("gather_rest_0", rest_gathering, o_a, rest_sends)
        lb_l = lbs[l:l + 1] + rest_passing[4][0, 0] if l == 0 else lbs[l:l + 1]
        o_b = _hgrn_fwd(proj, lb_l, f"hgrn_fwd_{l}")
        if l == 0:
            wb_l, wo_l = branch_out_weights(*gather_finish("gather_rest_0", rest_passing, o_b, rest_sends))
            if n_layers > 1:
                next_passing = gather_pass_on("gather_weights_1", next_gathering, o_b, layer_sends)
                gate = gate + next_passing[4][0, 0]
        x_new, merged, ycat = _merge_fwd(xl, proj, o_a, o_b, gate, norm_w4[l:l + 1], conv_full[l],
                                         wb_l, wo_l, ln_g[l:l + 1], ln_b[l:l + 1], f"merge_fwd_{l}")
        saved.append((xl, proj, h_t, o_a, totals, o_b, merged, ycat, w_in_l, wb_l, wo_l))
        if l == 0 and n_layers > 1:
            w_in_l, w_branch_l, w_out_l = gather_finish("gather_weights_1", next_passing, x_new, layer_sends)
            wb_l, wo_l = branch_out_weights(w_branch_l, w_out_l)
        xl = x_new

    loss_part, dx = _loss_fwd_bwd(xl, target)

    pair_sends = [(1, lambda ins, lands, me, j=j: window(ins[0], me ^ 1 ^ j),
                   lambda lands, me, jj=jj: lands[0].at[jj], lambda lands, me, jj=jj: lands[0].at[jj])
                  for jj, j in enumerate((0, 2, 4, 6))]
    chip_sum_sends = [(j, lambda ins, lands, me, jj=jj: ins[0].at[jj],
                       lambda lands, me, jj=jj: lands[0].at[jj], lambda lands, me, jj=jj: lands[0].at[jj])
                      for jj, j in ((1, 2), (2, 4), (3, 6))]
    rest_scatter = _direct_sends([(0, 0, _slot, _slot), (1, 1, _slot, _slot)])
    scattering = [None] * n_layers
    small_grads = [None] * n_layers
    dmod = [None] * n_layers
    tie = None
    for l in reversed(range(n_layers)):
        xl, proj, h_t, o_a, totals, o_b, merged, ycat, w_in_l, wb_l, wo_l = saved[l]
        scale, gate = mod_mine[l, 1], mod_mine[l, 2]
        if tie is not None:
            gate = gate + tie[0, 0]
        dres, dycat, dproj, gwo_by_owner, gwb_by_owner, mvec = _merge_bwd(
            dx, xl, merged, ycat, proj, gate, wb_l, wo_l, ln_g[l:l + 1], f"merge_bwd_{l}")
        lands = [_place_own((NDEV, 3, WIDTH, dsh), BF16, lax.dynamic_slice_in_dim(gwb_by_owner, me, 1, axis=0),
                            (me, 0, 0, 0)),
                 _place_own((NDEV, dsh, d), BF16, lax.dynamic_slice_in_dim(gwo_by_owner, me, 1, axis=0),
                            (me, 0, 0))]
        rest_started = _exchange_start(f"scatter_rest_{l}_start", [gwb_by_owner, gwo_by_owner], lands, rest_scatter)
        dproj, do_a, do_b, bvec = _branch_bwd(dycat, proj, o_a, o_b, norm_w4[l:l + 1] + rest_started[4][0, 0],
                                              conv_full[l], dproj, f"branch_bwd_{l}")
        dproj = _sb_bwd(proj, do_a, totals, dproj, f"sb_bwd_{l}")
        dproj, dlb = _hgrn_bwd(proj, do_b, lbs[l:l + 1], dproj, f"hgrn_bwd_{l}")
        gwi = _gw_matmul(h_t, dproj, f"gw_matmul_{l}")
        swapping = _exchange_start(f"scatter_in_{l}_sibling_start", [gwi], [lax.empty((4, d, shard), BF16)], pair_sends)
        if l > 0:
            dh = _dh_matmul(dproj, w_in_l, swapping[4], f"dh_matmul_{l}")
        (gwi,), (stage,) = _exchange_wait(f"scatter_in_{l}_sibling_wait", swapping, dh if l > 0 else swapping[4],
                                          pair_sends)
        land, chip_sums = _pair_sum(gwi, stage, me, f"pair_sum_{l}")
        in_started = _exchange_start(f"scatter_in_{l}_chips_start", [chip_sums], [land], chip_sum_sends)
        scattering[l] = (in_started, rest_started)
        tie = in_started[4]
        if l == 0:
            dh = _dh_matmul(dproj, w_in_l, tie, f"dh_matmul_{l}")
        dx, lvec = _ln_bwd(dh, xl, scale + tie[0, 0], dres, f"ln_bwd_{l}")
        dmod[l] = jnp.concatenate([lvec[0], lvec[1], mvec[2]])
        norm_grad = bvec[0].reshape(WIDTH // HG_HEAD_DIM, HG_HEAD_DIM).sum(axis=0)
        small_grads[l] = jnp.concatenate([mvec[0], mvec[1], norm_grad, dlb[0], bvec[1:4].reshape(-1)])
    grad_x = dx[None]

    flat = lambda a: a.reshape(-1, a.shape[-1])
    big = {"w_in": (w_in, m_w_in, v_w_in), "w_branch": (w_branch, m_w_branch, v_w_branch),
           "w_out": (w_out, m_w_out, v_w_out)}
    big_results = {n: None for n in big}

    def adam_layer(l, after):
        in_started, rest_started = scattering[l]
        p_branch_l, p_out_l = _exchange_wait(f"scatter_rest_{l}_wait", rest_started, after, rest_scatter)[1]
        p_in_l = _exchange_wait(f"scatter_in_{l}_chips_wait", in_started, after, chip_sum_sends)[1][0]
        parts = {"w_in": p_in_l, "w_branch": p_branch_l.reshape(NDEV, 3 * WIDTH, dsh), "w_out": p_out_l}
        for n, (w, m, v) in big.items():
            rows_per_layer = flat(w).shape[0] // n_layers
            big_results[n] = _sum_adamw(parts[n], flat(w), flat(m), flat(v), f"adamw_{n}_{l}",
                                        first_row=l * rows_per_layer, into=big_results[n])

    for l in reversed(range(1, n_layers)):
        adam_layer(l, tie)

    small_vec = jnp.concatenate(dmod + small_grads + [loss_part.reshape(1)])
    n_small = small_vec.shape[0]
    small_all = _all_gather_small("gather_small_grads", _pad_rows(small_vec, BLK),
                                  after=big_results["w_in"][3] if n_layers > 1 else None)
    small_sum = _sum_parts(small_all, "sum_small_grads").reshape(-1)[:n_small]
    dmod_all = small_all.reshape(NDEV, -1)[:, :n_layers * 3 * d].reshape(NDEV, n_layers, 3 * d)

    loss = small_sum[n_small - 1]

    off = n_layers * 3 * d
    grad_b_mod = small_sum[:off].reshape(n_layers, 3 * d)
    per_layer = 2 * d + HG_HEAD_DIM + WIDTH + 3 * WIDTH
    g_ln_g, g_ln_b, g_norm, g_lbs, g_conv = [], [], [], [], []
    for l in range(n_layers):
        seg = small_sum[off + l * per_layer: off + (l + 1) * per_layer]
        g_ln_g.append(seg[:d])
        g_ln_b.append(seg[d:2 * d])
        g_norm.append(seg[2 * d:2 * d + HG_HEAD_DIM])
        g_lbs.append(seg[2 * d + HG_HEAD_DIM:2 * d + HG_HEAD_DIM + WIDTH])
        g_conv.append(seg[2 * d + HG_HEAD_DIM + WIDTH:].reshape(3, WIDTH))
    grad_ln_g, grad_ln_b = jnp.stack(g_ln_g), jnp.stack(g_ln_b)
    grad_norm = jnp.stack(g_norm)
    _, lbs_vjp = jax.vjp(_lower_bound_table, lower_bounds)
    grad_lower = lbs_vjp(jnp.stack(g_lbs))[0]
    grad_conv = lax.dynamic_slice_in_dim(jnp.stack(g_conv), me * cw_cols, cw_cols, axis=2)

    dmod_mine = lax.dynamic_slice_in_dim(dmod_all, me * cm, cm, axis=2).transpose(1, 0, 2)
    grad_w_mod = _wmod_grad(c_all.T, dmod_mine)

    adam_layer(0, grad_w_mod)
    r_w_in, r_w_branch, r_w_out = ([o.reshape(big[n][0].shape) for o in big_results[n]]
                                   for n in ("w_in", "w_branch", "w_out"))
    r_w_mod = [o.reshape(w_mod.shape) for o in
               _sum_adamw(grad_w_mod.reshape(1, -1, cm), flat(w_mod), flat(m_w_mod), flat(v_w_mod), "adamw_w_mod")]

    small_names = ["b_mod", "conv_w", "hgrn_norm_w", "lower_bounds", "ln_g", "ln_b"]
    small_g = [grad_b_mod, grad_conv, grad_norm, grad_lower, grad_ln_g, grad_ln_b]
    small_w = [b_mod, conv_w, hgrn_norm_w, lower_bounds, ln_g, ln_b]
    small_m = [m_b_mod, m_conv_w, m_hgrn_norm_w, m_lower_bounds, m_ln_g, m_ln_b]
    small_v = [v_b_mod, v_conv_w, v_hgrn_norm_w, v_lower_bounds, v_ln_g, v_ln_b]
    as_rows = lambda a: a.reshape(-1, a.shape[-1])
    updates = _adamw_small([as_rows(a) for a in small_g], [as_rows(a) for a in small_w],
                           [as_rows(a) for a in small_m], [as_rows(a) for a in small_v])
    r_small = {n: [g] + [u.reshape(w.shape) for u in upd]
               for n, g, w, upd in zip(small_names, small_g, small_w, updates)}

    results = {"w_mod": r_w_mod, "w_in": r_w_in, "w_branch": r_w_branch, "w_out": r_w_out, **r_small}
    order = ["w_mod", "b_mod", "w_in", "conv_w", "hgrn_norm_w", "lower_bounds", "w_branch", "w_out", "ln_g", "ln_b"]
    outs = [loss, grad_x]
    for idx in range(4):
        outs.extend(results[n][idx] for n in order)
    return tuple(outs)
```

```python
import jax
import jax.numpy as jnp
from jax import lax
from jax.experimental import pallas as pl
from jax.experimental.pallas import tpu as pltpu

F32 = jnp.float32
BF16 = jnp.bfloat16
NDEV = 8
N_LAYERS = 2
SB_HEAD_DIM = 64
HG_HEAD_DIM = 128
WIDTH = 512
BLK = 128
LN_EPS = 1e-5
RMS_EPS = 1e-6
ALPHA = (2.0 * N_LAYERS) ** 0.25
ADAM_LR, ADAM_B1, ADAM_B2, ADAM_EPS, ADAM_WD, ADAM_STEP = 0.001, 0.9, 0.999, 1e-08, 0.01, 10
VMEM_LIMIT = 56 * 1024 * 1024
MESH = pl.DeviceIdType.MESH
HG_LEVELS = (64, 32, 16, 8, 4, 2, 1)


def _pcall(body, *, name, out_shape, grid=None, in_specs=None, out_specs=None, scratch_shapes=(),
           semantics=None, aliases=None):
    kwargs = {}
    if grid is not None:
        kwargs["grid"] = grid
    if in_specs is not None:
        kwargs["in_specs"] = in_specs
    if out_specs is not None:
        kwargs["out_specs"] = out_specs
    if aliases:
        kwargs["input_output_aliases"] = aliases
    return pl.pallas_call(
        body, name=name, out_shape=out_shape, scratch_shapes=list(scratch_shapes),
        compiler_params=pltpu.CompilerParams(dimension_semantics=semantics, vmem_limit_bytes=VMEM_LIMIT),
        interpret=False, **kwargs)


def _dot(a, b):
    return jnp.dot(a, b, preferred_element_type=F32)


def _dot_nt(a, b):
    return lax.dot_general(a, b, (((1,), (1,)), ((), ())), preferred_element_type=F32)


def _dot_tn(a, b):
    return lax.dot_general(a, b, (((0,), (0,)), ((), ())), preferred_element_type=F32)


def _dot_01_l(m_bf16, x):
    x1 = x.astype(BF16)
    x2 = (x - x1.astype(F32)).astype(BF16)
    return _dot(jnp.concatenate([m_bf16, m_bf16], axis=1), jnp.concatenate([x1, x2], axis=0))


def _sigmoid(x):
    return 1.0 / (1.0 + jnp.exp(-x))


def _silu_and_grad(x):
    s = _sigmoid(x)
    return x * s, s * (1.0 + x * (1.0 - s))


LOG2E = 1.4426950408889634
MASKED_SCORE = -1e30


def _softplus2_parts(z2):
    minus_abs = lax.bitcast_convert_type(lax.bitcast_convert_type(z2, jnp.int32) | jnp.int32(-2 ** 31), F32)
    e = jnp.exp2(minus_abs)
    sp2 = jnp.maximum(z2, 0.0) + jnp.log2(1.0 + e)
    r = 1.0 / (1.0 + e)
    return sp2, jnp.where(z2 >= 0.0, r, e * r)


def _split2_lanes(x):
    x1 = x.astype(BF16)
    return jnp.concatenate([x1, (x - x1.astype(F32)).astype(BF16)], axis=1)


def _iota2(shape, dim):
    return lax.broadcasted_iota(jnp.int32, shape, dim)


def _standardize(x):
    mu = jnp.mean(x, axis=-1, keepdims=True)
    xc = x - mu
    var = jnp.mean(xc * xc, axis=-1, keepdims=True)
    rstd = lax.rsqrt(var + LN_EPS)
    return xc * rstd, rstd


def _standardize_bwd(xhat, rstd, dxhat):
    m1 = jnp.mean(dxhat, axis=-1, keepdims=True)
    m2 = jnp.mean(dxhat * xhat, axis=-1, keepdims=True)
    return rstd * (dxhat - m1 - xhat * m2)


def _my_index():
    return 4 * lax.axis_index("x") + 2 * lax.axis_index("y") + lax.axis_index("c")


def _exchange(name, ins, out_shapes, transfers, in_vmem, after=None):
    n_in, n_out, n_t = len(ins), len(out_shapes), len(transfers)

    def body(*refs):
        n_skip = n_in + (0 if after is None else 1)
        in_refs, out_refs = refs[:n_in], refs[n_skip:n_skip + n_out]
        send_sems, recv_sems, local_sems = refs[n_skip + n_out:]
        x, y, c = lax.axis_index("x"), lax.axis_index("y"), lax.axis_index("c")
        me = 4 * x + 2 * y + c
        started = []
        for t, (i, o, src_fn, dst_fn) in enumerate(transfers):
            own = pltpu.make_async_copy(src_fn(in_refs[i], me), dst_fn(out_refs[o], me), local_sems.at[t])
            own.start()
            started.append(own)
        arrivals = []
        for k in range(1, NDEV):
            px = x ^ ((k >> 2) & 1)
            py = y ^ ((k >> 1) & 1)
            pc = c ^ (k & 1)
            peer = 4 * px + 2 * py + pc
            for t, (i, o, src_fn, dst_fn) in enumerate(transfers):
                sem = t * (NDEV - 1) + k - 1
                push = pltpu.make_async_remote_copy(
                    src_ref=src_fn(in_refs[i], peer), dst_ref=dst_fn(out_refs[o], me),
                    send_sem=send_sems.at[sem], recv_sem=recv_sems.at[sem],
                    device_id=(px, py, pc), device_id_type=MESH)
                push.start()
                started.append(push)
                arrivals.append(pltpu.make_async_remote_copy(
                    src_ref=src_fn(in_refs[i], peer), dst_ref=dst_fn(out_refs[o], peer),
                    send_sem=send_sems.at[sem], recv_sem=recv_sems.at[sem],
                    device_id=(px, py, pc), device_id_type=MESH))
        for arrival in arrivals:
            arrival.wait_recv()
        for cp in started[n_t:]:
            cp.wait_send()
        for own in started[:n_t]:
            own.wait()

    space = pltpu.VMEM if in_vmem else pl.ANY
    spec = pl.BlockSpec(memory_space=space)
    extra = [] if after is None else [after]
    return _pcall(
        body, name=name, out_shape=out_shapes,
        in_specs=[spec] * n_in + [pl.BlockSpec(memory_space=pl.ANY)] * len(extra), out_specs=[spec] * n_out,
        scratch_shapes=[pltpu.SemaphoreType.DMA((n_t * (NDEV - 1),)),
                        pltpu.SemaphoreType.DMA((n_t * (NDEV - 1),)),
                        pltpu.SemaphoreType.DMA((n_t,))])(*ins, *extra)


def _whole(ref, dev):
    return ref


def _slot(ref, dev):
    return ref.at[dev]


def _all_gather_small(name, v, after=None):
    out = _exchange(name, [v], [jax.ShapeDtypeStruct((NDEV,) + v.shape, v.dtype)],
                    [(0, 0, _whole, _slot)], in_vmem=True, after=after)
    return out[0]


_HBM_SPEC = pl.BlockSpec(memory_space=pltpu.HBM)
_SEM_SPEC = pl.BlockSpec(memory_space=pltpu.SEMAPHORE)
_DATAFLOW = pltpu.SideEffectType.DATAFLOW_SIDE_EFFECTING


def _peer(x, y, c, k):
    px = x ^ ((k >> 2) & 1)
    py = y ^ ((k >> 1) & 1)
    pc = c ^ (k & 1)
    return (px, py, pc), 4 * px + 2 * py + pc


def _direct_sends(transfers):
    sends = []
    for k in range(1, NDEV):
        for i, o, src_fn, dst_fn in transfers:
            sends.append((k,
                          lambda ins, lands, me, i=i, k=k, src_fn=src_fn: src_fn(ins[i], me ^ k),
                          lambda lands, me, o=o, dst_fn=dst_fn: dst_fn(lands[o], me),
                          lambda lands, me, o=o, k=k, dst_fn=dst_fn: dst_fn(lands[o], me ^ k)))
    return sends


def _exchange_start(name, ins, lands, sends, after=None):
    n_in, n_buf = len(ins), len(ins) + len(lands)
    n_sem = len(sends)

    def body(*refs):
        in_refs, land_refs = refs[:n_in], refs[n_in:n_buf]
        n_skip = n_buf + (0 if after is None else 1)
        send_sems, recv_sems, token = refs[n_skip], refs[n_skip + 1], refs[-1]
        x, y, c = lax.axis_index("x"), lax.axis_index("y"), lax.axis_index("c")
        me = 4 * x + 2 * y + c
        for t, (k, src_fn, dst_fn, _) in enumerate(sends):
            pltpu.make_async_remote_copy(
                src_ref=src_fn(in_refs, land_refs, me), dst_ref=dst_fn(land_refs, me),
                send_sem=send_sems.at[t], recv_sem=recv_sems.at[t],
                device_id=_peer(x, y, c, k)[0], device_id_type=MESH).start()
        token[...] = jnp.zeros_like(token)

    bufs = [pltpu.with_memory_space_constraint(a, pltpu.HBM) for a in list(ins) + list(lands)]
    extra = [] if after is None else [after]
    outs = pl.pallas_call(
        body, name=name,
        out_shape=(pltpu.SemaphoreType.DMA((n_sem,)), pltpu.SemaphoreType.DMA((n_sem,)))
        + tuple(pltpu.HBM(a.shape, a.dtype) for a in bufs) + (jax.ShapeDtypeStruct((8, BLK), F32),),
        in_specs=[_HBM_SPEC] * n_buf + [pl.BlockSpec(memory_space=pl.ANY)] * len(extra),
        out_specs=(_SEM_SPEC, _SEM_SPEC) + (_HBM_SPEC,) * n_buf + (pl.BlockSpec(memory_space=pltpu.VMEM),),
        input_output_aliases={b: 2 + b for b in range(n_buf)},
        compiler_params=pltpu.CompilerParams(has_side_effects=_DATAFLOW),
        interpret=False)(*bufs, *extra)
    return outs[0], outs[1], list(outs[2:2 + n_in]), list(outs[2 + n_in:2 + n_buf]), outs[-1]


def _exchange_wait(name, started, after, sends):
    send_sems, recv_sems, ins, lands, _ = started
    n_in, n_buf = len(ins), len(ins) + len(lands)

    def body(*refs):
        in_refs, land_refs = refs[:n_in], refs[n_in:n_buf]
        send_sems, recv_sems = refs[n_buf], refs[n_buf + 1]
        x, y, c = lax.axis_index("x"), lax.axis_index("y"), lax.axis_index("c")
        me = 4 * x + 2 * y + c
        for t, (k, src_fn, _, rcv_fn) in enumerate(sends):
            cp = pltpu.make_async_remote_copy(
                src_ref=src_fn(in_refs, land_refs, me), dst_ref=rcv_fn(land_refs, me),
                send_sem=send_sems.at[t], recv_sem=recv_sems.at[t],
                device_id=_peer(x, y, c, k)[0], device_id_type=MESH)
            cp.wait_send()
            cp.wait_recv()

    bufs = list(ins) + list(lands)
    outs = pl.pallas_call(
        body, name=name, out_shape=tuple(pltpu.HBM(a.shape, a.dtype) for a in bufs),
        in_specs=[_HBM_SPEC] * n_buf + [_SEM_SPEC, _SEM_SPEC, pl.BlockSpec(memory_space=pl.ANY)],
        out_specs=(_HBM_SPEC,) * n_buf,
        input_output_aliases={b: b for b in range(n_buf)},
        compiler_params=pltpu.CompilerParams(has_side_effects=_DATAFLOW),
        interpret=False)(*bufs, send_sems, recv_sems, after)
    return list(outs[:n_in]), list(outs[n_in:])


def _place_own(shape, dtype, own, start):
    return lax.dynamic_update_slice(lax.empty(shape, dtype), own, start)


def _place_own_window(name, shape, own, me):
    rows, cols = own.shape

    def body(me_ref, zone_in, own_ref, zone_ref):
        del me_ref, zone_in
        zone_ref[...] = own_ref[...]

    return pl.pallas_call(
        body, name=name, out_shape=jax.ShapeDtypeStruct(shape, own.dtype),
        grid_spec=pltpu.PrefetchScalarGridSpec(
            num_scalar_prefetch=1, grid=(1,),
            in_specs=[pl.BlockSpec(memory_space=pl.ANY), pl.BlockSpec((rows, cols), lambda i, me_ref: (0, 0))],
            out_specs=pl.BlockSpec((rows, cols), lambda i, me_ref: (0, me_ref[0]))),
        input_output_aliases={1: 0},
        compiler_params=pltpu.CompilerParams(dimension_semantics=("arbitrary",), vmem_limit_bytes=VMEM_LIMIT),
        interpret=False)(me.reshape(1).astype(jnp.int32), lax.empty(shape, own.dtype), own)


def _mod_fwd(c_all, w_mod, b_mod_mine):
    n_layers, _, cm = w_mod.shape

    def body(c_ref, w_ref, b_ref, o_ref):
        for l in range(n_layers):
            o_ref[l] = jnp.dot(c_ref[...], w_ref[l], preferred_element_type=F32,
                               precision=lax.Precision.HIGHEST) + b_ref[l]

    return _pcall(body, name="mod_fwd", out_shape=jax.ShapeDtypeStruct((n_layers, NDEV, cm), F32))(
        c_all, w_mod, b_mod_mine)


def _ln_proj(x, shift, scale, w_full, name):
    s_len, d = x.shape
    n = w_full.shape[1]
    tm = min(1024, s_len)
    tn = 2304

    def body(x_ref, sh_ref, sc_ref, w_ref, proj_ref, ht_ref, h_scr):
        @pl.when(pl.program_id(1) == 0)
        def _():
            xs, _ = _standardize(x_ref[...])
            h = xs * (1.0 + sc_ref[...]) + sh_ref[...]
            h_scr[...] = h.astype(BF16)
            ht_ref[...] = h.T.astype(BF16)

        proj_ref[...] = _dot(h_scr[...], w_ref[...])

    return _pcall(
        body, name=name,
        out_shape=(jax.ShapeDtypeStruct((s_len, n), F32), jax.ShapeDtypeStruct((d, s_len), BF16)),
        grid=(s_len // tm, n // tn),
        in_specs=[pl.BlockSpec((tm, d), lambda i, j: (i, 0)),
                  pl.BlockSpec((1, d), lambda i, j: (0, 0)),
                  pl.BlockSpec((1, d), lambda i, j: (0, 0)),
                  pl.BlockSpec((d, tn), lambda i, j: (0, j))],
        out_specs=(pl.BlockSpec((tm, tn), lambda i, j: (i, j)),
                   pl.BlockSpec((d, tm), lambda i, j: (0, i))),
        scratch_shapes=[pltpu.VMEM((tm, d), BF16)],
        semantics=("arbitrary", "arbitrary"))(x, shift, scale, w_full)


SB_Q_ROWS = 256
SB_K_BLOCKS = 2


def _sb_fwd(proj, name):
    s_len = proj.shape[0]
    n_pairs = WIDTH // BLK
    qr = min(SB_Q_ROWS, s_len)
    gb = SB_K_BLOCKS
    kw = gb * BLK
    nq = s_len // qr
    assert qr == kw

    def body(q_ref, k_ref, v_ref, o_ref, tot_ref):
        lane = _iota2((1, BLK), 1)
        row = _iota2((BLK, BLK), 0)
        col = _iota2((BLK, BLK), 1)
        half = jnp.concatenate([(row >= col).astype(BF16), jnp.ones((BLK, BLK), BF16)], axis=1)
        suffix_and_sum = jnp.concatenate([half, half], axis=0)
        strict = _iota2((qr, kw), 1) < _iota2((qr, kw), 0)
        head_lanes = [(lane // SB_HEAD_DIM) == hh for hh in range(2)]

        def scores(gi, qms, masked):
            c0 = pl.multiple_of(gi * kw, kw)
            kb = k_ref[pl.ds(c0, kw), :].astype(BF16)
            z2s = [_dot_nt(qms[hh], kb) for hh in range(2)]
            if masked:
                z2s = [jnp.where(strict, z2, MASKED_SCORE) for z2 in z2s]
            return tuple(z2s)

        def accumulate(gi, z2s, carry):
            c0 = pl.multiple_of(gi * kw, kw)
            vf = v_ref[pl.ds(c0, kw), :]
            sp2s = [_softplus2_parts(z2)[0] for z2 in z2s]
            terms = [[_split2_lanes(sp2[:, b * BLK:(b + 1) * BLK]) for b in range(gb)] for sp2 in sp2s]
            sums = [[_dot(t, suffix_and_sum) for t in head_terms] for head_terms in terms]
            weights, laters = [], []
            for hh in range(2):
                later = carry[2 * hh + 1]
                parts = [None] * gb
                for b in reversed(range(gb)):
                    parts[b] = sums[hh][b][:, :BLK] + later
                    later = later + sums[hh][b][:, BLK:]
                weights.append(jnp.exp2(z2s[hh] - jnp.concatenate(parts, axis=1)).astype(BF16))
                laters.append(later)
            outs = [_dot(weights[hh], jnp.where(head_lanes[hh], vf, 0.0).astype(BF16)) for hh in range(2)]
            return (carry[0] + outs[0], laters[0], carry[2] + outs[1], laters[1])

        def queries(i):
            qf = q_ref[pl.ds(pl.multiple_of(i * qr, qr), qr), :] * (SB_HEAD_DIM ** -0.5 * LOG2E)
            return [jnp.where(head_lanes[hh], qf, 0.0).astype(BF16) for hh in range(2)]

        def qtile(i, first_scores):
            r0 = pl.multiple_of(i * qr, qr)
            qms = queries(i)
            zero = jnp.zeros((qr, BLK), F32)

            def step(jj, state):
                gi = i - 1 - jj
                return scores(gi, qms, False) + accumulate(gi + 1, state[:2], state[2:])

            state = lax.fori_loop(0, i, step, first_scores + (zero,) * 4)
            nxt = jnp.minimum(i + 1, nq - 1)
            next_scores = scores(nxt, queries(nxt), True)
            carry = accumulate(0, state[:2], state[2:])
            o_ref[pl.ds(r0, qr), :] = carry[0] + carry[2]
            tot_ref[0, pl.ds(r0, qr), :] = carry[1]
            tot_ref[1, pl.ds(r0, qr), :] = carry[3]
            return next_scores

        lax.fori_loop(0, nq, qtile, scores(0, queries(0), True))

    col_spec = lambda off: pl.BlockSpec((s_len, BLK), lambda p: (0, off + p))
    return _pcall(
        body, name=name,
        out_shape=(jax.ShapeDtypeStruct((s_len, WIDTH), F32),
                   jax.ShapeDtypeStruct((2 * n_pairs, s_len, BLK), F32)),
        grid=(n_pairs,),
        in_specs=[col_spec(0), col_spec(n_pairs), col_spec(2 * n_pairs)],
        out_specs=(pl.BlockSpec((s_len, BLK), lambda p: (0, p)),
                   pl.BlockSpec((2, s_len, BLK), lambda p: (p, 0, 0))),
        semantics=("arbitrary",))(proj, proj, proj)


def _hg_masks(mask_ref):
    row = _iota2((BLK, BLK), 0)
    col = _iota2((BLK, BLK), 1)
    for v, m in enumerate(HG_LEVELS):
        same = (row // (2 * m)) == (col // (2 * m))
        mask_ref[v] = (same & ((row & m) != 0) & ((col & m) == 0)).astype(F32)


def _hg_mid(b, m):
    if m >= 4:
        n = BLK // (2 * m)
        mid = b.reshape(n, 2 * m, BLK)[:, m - 1:m, :]
        return jnp.broadcast_to(mid, (n, 2 * m, BLK)).reshape(BLK, BLK)
    pos = _iota2((BLK, BLK), 0) & (2 * m - 1)
    out = b
    for p in range(2 * m):
        delta = (m - 1) - p
        if delta != 0:
            out = jnp.where(pos == p, pltpu.roll(b, (-delta) % BLK, 0), out)
    return out


def _hg_chunk_inputs(qraw, fpre, lb):
    sig = _sigmoid(fpre)
    f = lb + (1.0 - lb) * sig
    g = jnp.log(f)
    q, dq_fac = _silu_and_grad(qraw)
    return q, dq_fac, f, sig, g


HG_GROUP = 4


def _neg_abs(x):
    return lax.bitcast_convert_type(lax.bitcast_convert_type(x, jnp.int32) | jnp.int32(-2 ** 31), F32)


def _hg_level_terms(qs, ks, bs, m):
    es = [jnp.exp(_neg_abs(b - _hg_mid(b, m))) for b in bs]
    qts = [(q * e).astype(BF16) for q, e in zip(qs, es)]
    kts = [(k * e).astype(BF16) for k, e in zip(ks, es)]
    return es, qts, kts


def _hg_load(refs, r0, lb_v, lower_incl):
    q_ref, f_ref, i_ref = refs
    heads = []
    for h in range(HG_GROUP):
        sl = slice(h * HG_HEAD_DIM, (h + 1) * HG_HEAD_DIM)
        heads.append(_hg_chunk_inputs(q_ref[pl.ds(r0, BLK), sl], f_ref[pl.ds(r0, BLK), sl], lb_v[:, sl])
                     + (i_ref[pl.ds(r0, BLK), sl],))
    bs = [_dot_01_l(lower_incl, hd[4]) for hd in heads]
    return heads, bs


def _hgrn_fwd(proj, lb, name):
    s_len = proj.shape[0]
    nc = s_len // BLK
    gw = HG_GROUP * HG_HEAD_DIM
    n_groups = WIDTH // gw
    base = 4 * WIDTH // gw

    def body(q_ref, f_ref, i_ref, lb_ref, o_ref, mask_ref):
        _hg_masks(mask_ref)
        row = _iota2((BLK, BLK), 0)
        col = _iota2((BLK, BLK), 1)
        lower_incl = (col <= row).astype(BF16)
        lb_v = lb_ref[...]

        def chunk(ci, sts):
            r0 = pl.multiple_of(ci * BLK, BLK)
            heads, bs = _hg_load((q_ref, f_ref, i_ref), r0, lb_v, lower_incl)
            qs = [hd[0] for hd in heads]
            ks = [1.0 - hd[2] for hd in heads]
            vs = [hd[5] for hd in heads]
            vbs = [v.astype(BF16) for v in vs]
            b_ends = [b[BLK - 1:BLK, :] for b in bs]
            inters = [_dot_nt((q * jnp.exp(b)).astype(BF16), st.astype(BF16)) for q, b, st in zip(qs, bs, sts)]
            scs = [None] * HG_GROUP
            for v_idx, m in enumerate(HG_LEVELS):
                _, qts, kts = _hg_level_terms(qs, ks, bs, m)
                terms = [_dot_nt(qt, kt) for qt, kt in zip(qts, kts)]
                msk = mask_ref[v_idx]
                scs = [t * msk if sc is None else sc + t * msk for sc, t in zip(scs, terms)]
            intras = [_dot(sc.astype(BF16), vb) for sc, vb in zip(scs, vbs)]
            k_decs = [(k * jnp.exp(b_end - b)).astype(BF16) for k, b, b_end in zip(ks, bs, b_ends)]
            grown = [_dot_tn(vb, k_dec) for vb, k_dec in zip(vbs, k_decs)]
            for h in range(HG_GROUP):
                diag = jnp.sum(qs[h] * ks[h], axis=-1, keepdims=True)
                o_ref[pl.ds(r0, BLK), h * HG_HEAD_DIM:(h + 1) * HG_HEAD_DIM] = inters[h] + intras[h] + diag * vs[h]
            return tuple(st * jnp.exp(b_end) + g for st, b_end, g in zip(sts, b_ends, grown))

        lax.fori_loop(0, nc, chunk, (jnp.zeros((HG_HEAD_DIM, HG_HEAD_DIM), F32),) * HG_GROUP)

    col_spec = lambda off: pl.BlockSpec((s_len, gw), lambda h: (0, off + h))
    return _pcall(
        body, name=name, out_shape=jax.ShapeDtypeStruct((s_len, WIDTH), F32),
        grid=(n_groups,),
        in_specs=[col_spec(base), col_spec(base + n_groups), col_spec(base + 2 * n_groups),
                  pl.BlockSpec((1, gw), lambda h: (0, h))],
        out_specs=pl.BlockSpec((s_len, gw), lambda h: (0, h)),
        scratch_shapes=[pltpu.VMEM((len(HG_LEVELS), BLK, BLK), F32)],
        semantics=("arbitrary",))(proj, proj, proj, lb)


def _rms_heads(o_b, norm_w):
    n_parts, h_parts, r_parts = [], [], []
    for h in range(WIDTH // HG_HEAD_DIM):
        sl = slice(h * HG_HEAD_DIM, (h + 1) * HG_HEAD_DIM)
        o = o_b[:, sl]
        rstd = lax.rsqrt(jnp.mean(o * o, axis=-1, keepdims=True) + RMS_EPS)
        ohat = o * rstd
        h_parts.append(ohat)
        n_parts.append(ohat * norm_w[:, sl])
        r_parts.append(jnp.broadcast_to(rstd, o.shape))
    cat = lambda parts: jnp.concatenate(parts, axis=-1)
    return cat(n_parts), cat(h_parts), cat(r_parts)


def _shift_rows_down(halo, cur, k):
    tm = cur.shape[0]
    ext = jnp.concatenate([halo, cur], axis=0)
    return pltpu.roll(ext, k, 0)[8:8 + tm]


def _shift_rows_up(cur, halo, k):
    tm = cur.shape[0]
    ext = jnp.concatenate([cur, halo], axis=0)
    return pltpu.roll(ext, (tm + 8 - k) % (tm + 8), 0)[0:tm]


def _merge_fwd(x, proj, o_a, o_b, gate, norm_w, conv_w, wb, w_out, ln_g, ln_b, name):
    s_len, d = x.shape
    tm = min(256, s_len)
    hb = tm // 8

    def body(x_ref, oa_ref, za_ref, ob_ref, zb_ref, pre_ref, post_ref, u_ref, zc_ref, hpre_ref, hu_ref, g_ref,
             gate_ref, nw_ref, cw_ref, wb_ref, wo_ref, lg_ref, lbias_ref, xn_ref, mg_ref, yc_ref):
        i = pl.program_id(0)
        sa, _ = _silu_and_grad(za_ref[...])
        y_a = (oa_ref[...] * sa).astype(BF16)
        n_b, _, _ = _rms_heads(ob_ref[...], nw_ref[...])
        sb, _ = _silu_and_grad(zb_ref[...])
        y_b = (n_b * sb).astype(BF16)
        a = pre_ref[...] * u_ref[...]
        halo = jnp.where(i > 0, hpre_ref[...] * hu_ref[...], 0.0)
        cw = cw_ref[...]
        conv = cw[0:1] * _shift_rows_down(halo, a, 2) + cw[1:2] * _shift_rows_down(halo, a, 1) + cw[2:3] * a
        sc, _ = _silu_and_grad(zc_ref[...])
        y_c = (post_ref[...] * conv * sc).astype(BF16)
        merged = None
        for k, yk in enumerate((y_a, y_b, y_c)):
            yc_ref[:, k * WIDTH:(k + 1) * WIDTH] = yk
            term = _sigmoid(g_ref[:, k * d:(k + 1) * d]) * _dot(yk, wb_ref[k])
            merged = term if merged is None else merged + term
        mb = merged.astype(BF16)
        mg_ref[...] = mb
        y = _dot(mb, wo_ref[...])
        r = ALPHA * x_ref[...] + (1.0 + gate_ref[...]) * y
        rhat, _ = _standardize(r)
        xn_ref[...] = rhat * lg_ref[...] + lbias_ref[...]

    wcol = lambda cb: pl.BlockSpec((tm, WIDTH), lambda i: (i, cb))
    halo_spec = lambda cb: pl.BlockSpec((8, WIDTH), lambda i: (jnp.maximum(i * hb - 1, 0), cb))
    vec = lambda w: pl.BlockSpec((1, w), lambda i: (0, 0))
    return _pcall(
        body, name=name,
        out_shape=(jax.ShapeDtypeStruct((s_len, d), F32), jax.ShapeDtypeStruct((s_len, d), BF16),
                   jax.ShapeDtypeStruct((s_len, 3 * WIDTH), BF16)),
        grid=(s_len // tm,),
        in_specs=[pl.BlockSpec((tm, d), lambda i: (i, 0)),
                  wcol(0), wcol(3), wcol(0), wcol(7), wcol(8), wcol(9), wcol(10), wcol(11),
                  halo_spec(8), halo_spec(10),
                  pl.BlockSpec((tm, 3 * d), lambda i: (i, 2)),
                  vec(d), vec(WIDTH),
                  pl.BlockSpec((3, WIDTH), lambda i: (0, 0)),
                  pl.BlockSpec((3, WIDTH, d), lambda i: (0, 0, 0)),
                  pl.BlockSpec((d, d), lambda i: (0, 0)),
                  vec(d), vec(d)],
        out_specs=(pl.BlockSpec((tm, d), lambda i: (i, 0)), pl.BlockSpec((tm, d), lambda i: (i, 0)),
                   pl.BlockSpec((tm, 3 * WIDTH), lambda i: (i, 0))),
        semantics=("arbitrary",))(x, o_a, proj, o_b, proj, proj, proj, proj, proj, proj, proj, proj,
                                  gate, norm_w, conv_w, wb, w_out, ln_g, ln_b)


def _loss_fwd_bwd(y, target):
    s_len, d = y.shape
    tm = min(512, s_len)

    def body(y_ref, t_ref, loss_ref, dy_ref):
        @pl.when(pl.program_id(0) == 0)
        def _():
            loss_ref[...] = jnp.zeros_like(loss_ref)

        e = y_ref[...] - t_ref[...]
        dy_ref[...] = e * (1.0 / d)
        part = jnp.sum(jnp.sum(e * e, axis=-1, keepdims=True), axis=0, keepdims=True)
        loss_ref[...] += part * (0.5 / d)

    tile = pl.BlockSpec((tm, d), lambda i: (i, 0))
    return _pcall(body, name="loss", grid=(s_len // tm,),
                  out_shape=(jax.ShapeDtypeStruct((1, 1), F32), jax.ShapeDtypeStruct((s_len, d), F32)),
                  in_specs=[tile, tile],
                  out_specs=(pl.BlockSpec((1, 1), lambda i: (0, 0)), tile),
                  semantics=("arbitrary",))(y, target)


def _merge_bwd(dxn, x, merged, ycat, proj, gate, wb, w_out, ln_g, name):
    s_len, d = x.shape
    tm = min(256, s_len)
    dsh = d // NDEV
    n_tiles = s_len // tm

    def body(dxn_ref, x_ref, mg_ref, yc_ref, g_ref, gate_ref, wb_ref, wo_ref, lg_ref,
             dres_ref, dyc_ref, dg_ref, gwo_out, gwb_out, vec_ref, gwo_ref, gwb_ref):
        @pl.when(pl.program_id(0) == 0)
        def _():
            gwo_ref[...] = jnp.zeros_like(gwo_ref)
            gwb_ref[...] = jnp.zeros_like(gwb_ref)
            vec_ref[...] = jnp.zeros_like(vec_ref)

        mb = mg_ref[...]
        one_gate = 1.0 + gate_ref[...]
        y = _dot(mb, wo_ref[...])
        r = ALPHA * x_ref[...] + one_gate * y
        rhat, rstd = _standardize(r)
        dxn = dxn_ref[...]
        dr = _standardize_bwd(rhat, rstd, dxn * lg_ref[...])
        vec_ref[0:1, :] += jnp.sum(dxn * rhat, axis=0, keepdims=True)
        vec_ref[1:2, :] += jnp.sum(dxn, axis=0, keepdims=True)
        vec_ref[2:3, :] += jnp.sum(dr * y, axis=0, keepdims=True)
        dres_ref[...] = ALPHA * dr
        dy = (one_gate * dr).astype(BF16)
        gwo_ref[...] += _dot_tn(mb, dy)
        dmerged = _dot_nt(dy, wo_ref[...])
        for k in range(3):
            yk = yc_ref[:, k * WIDTH:(k + 1) * WIDTH]
            sg = _sigmoid(g_ref[:, k * d:(k + 1) * d])
            pk = _dot(yk, wb_ref[k])
            dg_ref[:, k * d:(k + 1) * d] = (dmerged * pk * sg * (1.0 - sg)).astype(BF16)
            dpk = (dmerged * sg).astype(BF16)
            dyc_ref[:, k * WIDTH:(k + 1) * WIDTH] = _dot_nt(dpk, wb_ref[k])
            gwb_ref[k] += _dot_tn(yk, dpk)

        @pl.when(pl.program_id(0) == n_tiles - 1)
        def _():
            for o in range(NDEV):
                gwo_out[o] = gwo_ref[o * dsh:(o + 1) * dsh, :].astype(BF16)
                for k in range(3):
                    gwb_out[o, k] = gwb_ref[k, :, o * dsh:(o + 1) * dsh].astype(BF16)

    tile = lambda w: pl.BlockSpec((tm, w), lambda i: (i, 0))
    vec = pl.BlockSpec((1, d), lambda i: (0, 0))
    return _pcall(
        body, name=name,
        out_shape=(jax.ShapeDtypeStruct((s_len, d), F32), jax.ShapeDtypeStruct((s_len, 3 * WIDTH), F32),
                   jax.ShapeDtypeStruct(proj.shape, BF16), jax.ShapeDtypeStruct((NDEV, dsh, d), BF16),
                   jax.ShapeDtypeStruct((NDEV, 3, WIDTH, dsh), BF16), jax.ShapeDtypeStruct((8, d), F32)),
        grid=(n_tiles,),
        in_specs=[tile(d), tile(d), tile(d), tile(3 * WIDTH),
                  pl.BlockSpec((tm, 3 * d), lambda i: (i, 2)),
                  vec, pl.BlockSpec((3, WIDTH, d), lambda i: (0, 0, 0)),
                  pl.BlockSpec((d, d), lambda i: (0, 0)), vec],
        out_specs=(tile(d), tile(3 * WIDTH), pl.BlockSpec((tm, 3 * d), lambda i: (i, 2)),
                   pl.BlockSpec((NDEV, dsh, d), lambda i: (0, 0, 0)),
                   pl.BlockSpec((NDEV, 3, WIDTH, dsh), lambda i: (0, 0, 0, 0)),
                   pl.BlockSpec((8, d), lambda i: (0, 0))),
        scratch_shapes=[pltpu.VMEM((d, d), F32), pltpu.VMEM((3, WIDTH, d), F32)],
        semantics=("arbitrary",))(dxn, x, merged, ycat, proj, gate, wb, w_out, ln_g)


def _branch_bwd(dycat, proj, o_a, o_b, norm_w, conv_w, dproj, name):
    s_len = proj.shape[0]
    tm = min(256, s_len)
    hb = tm // 8
    n_tiles = s_len // tm

    def body(dya_ref, dyb_ref, dyc_ref, oa_ref, za_ref, ob_ref, zb_ref, pre_ref, post_ref, u_ref, zc_ref,
             hpre_ref, hu_ref, ndyc_ref, npost_ref, nzc_ref, nw_ref, cw_ref, dproj_in,
             dproj_ref, doa_ref, dob_ref, vec_ref, dza_scr, dzb_scr, dc_scr, sems):
        del dproj_in
        i = pl.program_id(0)

        @pl.when(i == 0)
        def _():
            vec_ref[...] = jnp.zeros_like(vec_ref)

        sa, dsa = _silu_and_grad(za_ref[...])
        dya = dya_ref[...]
        doa_ref[...] = dya * sa
        dza_scr[...] = (dya * oa_ref[...] * dsa).astype(BF16)
        nw = nw_ref[...]
        n_b, ohat, rstd = _rms_heads(ob_ref[...], nw)
        sb, dsb = _silu_and_grad(zb_ref[...])
        dyb = dyb_ref[...]
        dzb_scr[...] = (dyb * n_b * dsb).astype(BF16)
        dn = dyb * sb
        vec_ref[0:1, :] += jnp.sum(dn * ohat, axis=0, keepdims=True)
        dnw = dn * nw
        parts = []
        for h in range(WIDTH // HG_HEAD_DIM):
            sl = slice(h * HG_HEAD_DIM, (h + 1) * HG_HEAD_DIM)
            m2 = jnp.mean(dnw[:, sl] * ohat[:, sl], axis=-1, keepdims=True)
            parts.append(rstd[:, sl] * (dnw[:, sl] - ohat[:, sl] * m2))
        dob_ref[...] = jnp.concatenate(parts, axis=-1)
        cw = cw_ref[...]
        pre, u, post = pre_ref[...], u_ref[...], post_ref[...]
        a = pre * u
        halo = jnp.where(i > 0, hpre_ref[...] * hu_ref[...], 0.0)
        a1 = _shift_rows_down(halo, a, 1)
        a2 = _shift_rows_down(halo, a, 2)
        conv = cw[0:1] * a2 + cw[1:2] * a1 + cw[2:3] * a
        sc, dsc = _silu_and_grad(zc_ref[...])
        dyc = dyc_ref[...]
        dconv = dyc * post * sc
        nsc, _ = _silu_and_grad(nzc_ref[...])
        nxt = jnp.where(i < n_tiles - 1, ndyc_ref[...] * npost_ref[...] * nsc, 0.0)
        da = cw[2:3] * dconv + cw[1:2] * _shift_rows_up(dconv, nxt, 1) + cw[0:1] * _shift_rows_up(dconv, nxt, 2)
        dc_scr[:, 0 * WIDTH:1 * WIDTH] = (da * u).astype(BF16)
        dc_scr[:, 1 * WIDTH:2 * WIDTH] = (dyc * conv * sc).astype(BF16)
        dc_scr[:, 2 * WIDTH:3 * WIDTH] = (da * pre).astype(BF16)
        dc_scr[:, 3 * WIDTH:4 * WIDTH] = (dyc * post * conv * dsc).astype(BF16)
        vec_ref[1:2, :] += jnp.sum(dconv * a2, axis=0, keepdims=True)
        vec_ref[2:3, :] += jnp.sum(dconv * a1, axis=0, keepdims=True)
        vec_ref[3:4, :] += jnp.sum(dconv * a, axis=0, keepdims=True)
        rows = pl.ds(pl.multiple_of(i * tm, tm), tm)
        copies = [pltpu.make_async_copy(dza_scr, dproj_ref.at[rows, 3 * WIDTH:4 * WIDTH], sems.at[0]),
                  pltpu.make_async_copy(dzb_scr, dproj_ref.at[rows, 7 * WIDTH:8 * WIDTH], sems.at[1]),
                  pltpu.make_async_copy(dc_scr, dproj_ref.at[rows, 8 * WIDTH:12 * WIDTH], sems.at[2])]
        for cp in copies:
            cp.start()
        for cp in copies:
            cp.wait()

    wcol = lambda cb: pl.BlockSpec((tm, WIDTH), lambda i: (i, cb))
    prev = lambda cb: pl.BlockSpec((8, WIDTH), lambda i: (jnp.maximum(i * hb - 1, 0), cb))
    nxt = lambda cb: pl.BlockSpec((8, WIDTH), lambda i: (jnp.minimum((i + 1) * hb, s_len // 8 - 1), cb))
    anyspec = pl.BlockSpec(memory_space=pl.ANY)
    out = jax.ShapeDtypeStruct((s_len, WIDTH), F32)
    return _pcall(
        body, name=name,
        out_shape=(jax.ShapeDtypeStruct(dproj.shape, dproj.dtype), out, out, jax.ShapeDtypeStruct((8, WIDTH), F32)),
        grid=(n_tiles,),
        in_specs=[wcol(0), wcol(1), wcol(2), wcol(0), wcol(3), wcol(0), wcol(7), wcol(8), wcol(9), wcol(10), wcol(11),
                  prev(8), prev(10), nxt(2), nxt(9), nxt(11),
                  pl.BlockSpec((1, WIDTH), lambda i: (0, 0)), pl.BlockSpec((3, WIDTH), lambda i: (0, 0)), anyspec],
        out_specs=(anyspec, wcol(0), wcol(0), pl.BlockSpec((8, WIDTH), lambda i: (0, 0))),
        scratch_shapes=[pltpu.VMEM((tm, WIDTH), BF16), pltpu.VMEM((tm, WIDTH), BF16),
                        pltpu.VMEM((tm, 4 * WIDTH), BF16), pltpu.SemaphoreType.DMA((3,))],
        aliases={18: 0},
        semantics=("arbitrary",))(dycat, dycat, dycat, o_a, proj, o_b, proj, proj, proj, proj, proj,
                                  proj, proj, dycat, proj, proj, norm_w, conv_w, dproj)


def _sb_bwd(proj, do_a, totals, dproj, name):
    s_len = proj.shape[0]
    n_pairs = WIDTH // BLK
    scale = SB_HEAD_DIM ** -0.5
    qr = min(SB_Q_ROWS, s_len)
    gb = SB_K_BLOCKS
    kw = gb * BLK
    nq = s_len // qr
    assert qr == kw

    def body(q_ref, k_ref, v_ref, do_ref, tot_ref, dproj_in, dproj_ref, dq_ref, dk_ref, dv_ref, out_scr, sems):
        del dproj_in
        lane = _iota2((1, BLK), 1)
        row = _iota2((BLK, BLK), 0)
        col = _iota2((BLK, BLK), 1)
        ones = jnp.ones((BLK, BLK), BF16)
        twice = lambda m: jnp.concatenate([m, m], axis=0)
        before_and_sum = twice(jnp.concatenate([(row < col).astype(BF16), ones], axis=1))
        upto_and_sum = twice(jnp.concatenate([(row <= col).astype(BF16), ones], axis=1))
        strict = _iota2((qr, kw), 1) < _iota2((qr, kw), 0)
        head_lanes = [(lane // SB_HEAD_DIM) == hh for hh in range(2)]
        dk_ref[...] = jnp.zeros_like(dk_ref)
        dv_ref[...] = jnp.zeros_like(dv_ref)

        def scores(gi, qms, masked):
            c0 = pl.multiple_of(gi * kw, kw)
            kb = k_ref[pl.ds(c0, kw), :].astype(BF16)
            z2s = [_dot_nt(qms[hh], kb) for hh in range(2)]
            if masked:
                z2s = [jnp.where(strict, z2, MASKED_SCORE) for z2 in z2s]
            return tuple(z2s)

        def process(gi, z2s, qms, doms, totals_i, carry):
            c0 = pl.multiple_of(gi * kw, kw)
            kf = k_ref[pl.ds(c0, kw), :]
            vf = v_ref[pl.ds(c0, kw), :]
            kms = [jnp.where(head_lanes[hh], kf, 0.0).astype(BF16) for hh in range(2)]
            vms = [jnp.where(head_lanes[hh], vf, 0.0).astype(BF16) for hh in range(2)]
            das = [_dot_nt(doms[hh], vms[hh]) for hh in range(2)]
            halves = [_softplus2_parts(z2) for z2 in z2s]
            terms = [[_split2_lanes(sp2[:, b * BLK:(b + 1) * BLK]) for b in range(gb)] for sp2, _ in halves]
            sums = [[_dot(t, before_and_sum) for t in head_terms] for head_terms in terms]
            weights, gmats, l_befores = [], [], []
            for hh in range(2):
                l_before = carry[3 * hh + 1]
                parts = []
                for b in range(gb):
                    parts.append(totals_i[hh] - l_before - sums[hh][b][:, :BLK])
                    l_before = l_before + sums[hh][b][:, BLK:]
                a = jnp.exp2(z2s[hh] - jnp.concatenate(parts, axis=1))
                weights.append(a.astype(BF16))
                gmats.append(a * das[hh])
                l_befores.append(l_before)
            terms = [[_split2_lanes(g[:, b * BLK:(b + 1) * BLK]) for b in range(gb)] for g in gmats]
            sums = [[_dot(t, upto_and_sum) for t in head_terms] for head_terms in terms]
            dzs, g_befores = [], []
            for hh in range(2):
                g_before = carry[3 * hh + 2]
                parts = []
                for b in range(gb):
                    parts.append(g_before + sums[hh][b][:, :BLK])
                    g_before = g_before + sums[hh][b][:, BLK:]
                dzs.append((gmats[hh] - halves[hh][1] * jnp.concatenate(parts, axis=1)).astype(BF16))
                g_befores.append(g_before)
            dk_t = _dot_tn(jnp.concatenate(qms, axis=0), jnp.concatenate(dzs, axis=0))
            dv_t = _dot_tn(jnp.concatenate(doms, axis=0), jnp.concatenate(weights, axis=0))
            dqs = [_dot(dzs[hh], kms[hh]) for hh in range(2)]
            dk_ref[:, pl.ds(c0, kw)] += dk_t * (1.0 / LOG2E)
            dv_ref[:, pl.ds(c0, kw)] += dv_t
            return (carry[0] + dqs[0], l_befores[0], g_befores[0], carry[3] + dqs[1], l_befores[1], g_befores[1])

        def queries(i):
            qf = q_ref[pl.ds(pl.multiple_of(i * qr, qr), qr), :] * (scale * LOG2E)
            return [jnp.where(head_lanes[hh], qf, 0.0).astype(BF16) for hh in range(2)]

        def qtile(i, first_scores):
            r0 = pl.multiple_of(i * qr, qr)
            qms = queries(i)
            dof = do_ref[pl.ds(r0, qr), :]
            doms = [jnp.where(head_lanes[hh], dof, 0.0).astype(BF16) for hh in range(2)]
            totals_i = [tot_ref[hh, pl.ds(r0, qr), :] for hh in range(2)]
            zero = jnp.zeros((qr, BLK), F32)

            def step(gi, state):
                return scores(gi + 1, qms, False) + process(gi, state[:2], qms, doms, totals_i, state[2:])

            def before_diagonal(state):
                return scores(i, qms, True) + process(i - 1, state[:2], qms, doms, totals_i, state[2:])

            state = lax.fori_loop(0, i - 1, step, first_scores + (zero,) * 6)
            state = lax.cond(i > 0, before_diagonal, lambda st: st, state)
            nxt = jnp.minimum(i + 1, nq - 1)
            next_scores = scores(0, queries(nxt), False)
            carry = process(i, state[:2], qms, doms, totals_i, state[2:])
            dq_ref[pl.ds(r0, qr), :] = (carry[0] + carry[3]) * scale
            return next_scores

        lax.fori_loop(0, nq, qtile, scores(0, queries(0), True))
        pair = pl.program_id(0)
        copies = []
        for t, value in enumerate((dq_ref[...], dk_ref[...].T, dv_ref[...].T)):
            out_scr[t] = value.astype(BF16)
            col = pl.multiple_of((t * n_pairs + pair) * BLK, BLK)
            copies.append(pltpu.make_async_copy(out_scr.at[t], dproj_ref.at[:, pl.ds(col, BLK)], sems.at[t]))
            copies[-1].start()
        for cp in copies:
            cp.wait()

    col_spec = lambda off: pl.BlockSpec((s_len, BLK), lambda p: (0, off + p))
    anyspec = pl.BlockSpec(memory_space=pl.ANY)
    return _pcall(
        body, name=name, out_shape=jax.ShapeDtypeStruct(dproj.shape, dproj.dtype), grid=(n_pairs,),
        in_specs=[col_spec(0), col_spec(n_pairs), col_spec(2 * n_pairs), col_spec(0),
                  pl.BlockSpec((2, s_len, BLK), lambda p: (p, 0, 0)), anyspec],
        out_specs=anyspec,
        scratch_shapes=[pltpu.VMEM((s_len, BLK), F32), pltpu.VMEM((BLK, s_len), F32), pltpu.VMEM((BLK, s_len), F32),
                        pltpu.VMEM((3, s_len, BLK), BF16), pltpu.SemaphoreType.DMA((3,))],
        aliases={5: 0},
        semantics=("arbitrary",))(proj, proj, proj, do_a, totals, dproj)


def _hgrn_bwd(proj, do_b, lb, dproj, name):
    s_len = proj.shape[0]
    nc = s_len // BLK
    gw = HG_GROUP * HG_HEAD_DIM
    n_groups = WIDTH // gw
    base = 4 * WIDTH // gw
    heads_of = range(HG_GROUP)

    def body(q_ref, f_ref, i_ref, do_ref, lb_ref, dproj_in, dproj_ref, dlb_ref, mask_ref, st_ref, out_scr, sems):
        del dproj_in
        _hg_masks(mask_ref)
        row = _iota2((BLK, BLK), 0)
        col = _iota2((BLK, BLK), 1)
        lower_incl = (col <= row).astype(BF16)
        upper_incl = (col >= row).astype(BF16)
        lb_v = lb_ref[...]
        refs = (q_ref, f_ref, i_ref)

        def fwd_chunk(ci, sts):
            for h in heads_of:
                st_ref[ci, h] = sts[h]
            heads, bs = _hg_load(refs, pl.multiple_of(ci * BLK, BLK), lb_v, lower_incl)
            b_ends = [b[BLK - 1:BLK, :] for b in bs]
            k_decs = [((1.0 - hd[2]) * jnp.exp(b_end - b)).astype(BF16) for hd, b, b_end in zip(heads, bs, b_ends)]
            grown = [_dot_tn(hd[5].astype(BF16), k_dec) for hd, k_dec in zip(heads, k_decs)]
            return tuple(st * jnp.exp(b_end) + g for st, b_end, g in zip(sts, b_ends, grown))

        zero_state = (jnp.zeros((HG_HEAD_DIM, HG_HEAD_DIM), F32),) * HG_GROUP
        lax.fori_loop(0, nc, fwd_chunk, zero_state)

        def bwd_chunk(cc, carry):
            dsts, suffixes, dlbs = carry
            ci = nc - 1 - cc
            r0 = pl.multiple_of(ci * BLK, BLK)
            heads, bs = _hg_load(refs, r0, lb_v, lower_incl)
            qs = [hd[0] for hd in heads]
            fs = [hd[2] for hd in heads]
            ks = [1.0 - f for f in fs]
            vs = [hd[5] for hd in heads]
            vbs = [v.astype(BF16) for v in vs]
            dos = [do_ref[pl.ds(r0, BLK), h * HG_HEAD_DIM:(h + 1) * HG_HEAD_DIM] for h in heads_of]
            dobs = [do.astype(BF16) for do in dos]
            b_ends = [b[BLK - 1:BLK, :] for b in bs]
            e_qs = [jnp.exp(b) for b in bs]
            e_ks = [jnp.exp(b_end - b) for b, b_end in zip(bs, b_ends)]
            qes = [(q * e).astype(BF16) for q, e in zip(qs, e_qs)]
            khs = [(k * e).astype(BF16) for k, e in zip(ks, e_ks)]
            st_terms = [_split2_lanes(st_ref[ci, h]) for h in heads_of]
            ds_terms = [_split2_lanes(dst) for dst in dsts]
            dqes = [_dot(dob, t[:, :HG_HEAD_DIM]) + _dot(dob, t[:, HG_HEAD_DIM:]) for dob, t in zip(dobs, st_terms)]
            dkhs = [_dot(vb, t[:, :HG_HEAD_DIM]) + _dot(vb, t[:, HG_HEAD_DIM:]) for vb, t in zip(vbs, ds_terms)]
            dvs = [_dot_nt(kh, t[:, :HG_HEAD_DIM]) for kh, t in zip(khs, ds_terms)]
            grown = [_dot_tn(dob, qe) for dob, qe in zip(dobs, qes)]
            das = [_dot_nt(dob, vb) for dob, vb in zip(dobs, vbs)]
            dqs = [e * dqe for e, dqe in zip(e_qs, dqes)]
            dks = [e * dkh for e, dkh in zip(e_ks, dkhs)]
            dlogs = [qe.astype(F32) * dqe - kh.astype(F32) * dkh for qe, dqe, kh, dkh in zip(qes, dqes, khs, dkhs)]
            scs = [None] * HG_GROUP
            for v_idx, m in enumerate(HG_LEVELS):
                es, qms, kms = _hg_level_terms(qs, ks, bs, m)
                msk = mask_ref[v_idx]
                terms = [_dot_nt(qm, km) for qm, km in zip(qms, kms)]
                pms = [(da * msk).astype(BF16) for da in das]
                dqms = [_dot(pm, km) for pm, km in zip(pms, kms)]
                dkms = [_dot_tn(pm, qm) for pm, qm in zip(pms, qms)]
                scs = [t * msk if sc is None else sc + t * msk for sc, t in zip(scs, terms)]
                dqs = [dq + dqm * e for dq, dqm, e in zip(dqs, dqms, es)]
                dks = [dk + dkm * e for dk, dkm, e in zip(dks, dkms, es)]
                dlogs = [dl + (qm.astype(F32) * dqm - km.astype(F32) * dkm)
                         for dl, qm, dqm, km, dkm in zip(dlogs, qms, dqms, kms, dkms)]
            intras = [_dot_tn(sc.astype(BF16), dob) for sc, dob in zip(scs, dobs)]
            dgs = [_dot_01_l(upper_incl, dl) + sfx for dl, sfx in zip(dlogs, suffixes)]
            new_dlbs = []
            for h in heads_of:
                q, dq_fac, f, sig = heads[h][0], heads[h][1], heads[h][2], heads[h][3]
                a_diag = jnp.sum(dos[h] * vs[h], axis=-1, keepdims=True)
                s_diag = jnp.sum(q * ks[h], axis=-1, keepdims=True)
                dq = dqs[h] + a_diag * ks[h]
                dk = dks[h] + a_diag * q
                dv = dvs[h] + intras[h] + s_diag * dos[h]
                dfull = dgs[h] / f - dk
                sl = slice(h * HG_HEAD_DIM, (h + 1) * HG_HEAD_DIM)
                out_scr[0, pl.ds(r0, BLK), sl] = (dq * dq_fac).astype(BF16)
                out_scr[1, pl.ds(r0, BLK), sl] = (dfull * (1.0 - lb_v[:, sl]) * sig * (1.0 - sig)).astype(BF16)
                out_scr[2, pl.ds(r0, BLK), sl] = dv.astype(BF16)
                new_dlbs.append(dlbs[h] + jnp.sum(dfull * (1.0 - sig), axis=0, keepdims=True))
            new_dsts = tuple(dst * jnp.exp(b_end) + g for dst, b_end, g in zip(dsts, b_ends, grown))
            return new_dsts, tuple(dg[0:1, :] for dg in dgs), tuple(new_dlbs)

        zero_row = (jnp.zeros((1, HG_HEAD_DIM), F32),) * HG_GROUP
        _, _, dlbs = lax.fori_loop(0, nc, bwd_chunk, (zero_state, zero_row, zero_row))
        dlb_ref[...] = jnp.broadcast_to(jnp.concatenate(dlbs, axis=1), dlb_ref.shape)
        group = pl.program_id(0)
        copies = []
        for t in range(3):
            col = pl.multiple_of((base + t * n_groups + group) * gw, gw)
            copies.append(pltpu.make_async_copy(out_scr.at[t], dproj_ref.at[:, pl.ds(col, gw)], sems.at[t]))
            copies[-1].start()
        for cp in copies:
            cp.wait()

    col_spec = lambda off: pl.BlockSpec((s_len, gw), lambda h: (0, off + h))
    anyspec = pl.BlockSpec(memory_space=pl.ANY)
    return _pcall(
        body, name=name,
        out_shape=(jax.ShapeDtypeStruct(dproj.shape, dproj.dtype), jax.ShapeDtypeStruct((8, WIDTH), F32)),
        grid=(n_groups,),
        in_specs=[col_spec(base), col_spec(base + n_groups), col_spec(base + 2 * n_groups), col_spec(0),
                  pl.BlockSpec((1, gw), lambda h: (0, h)), anyspec],
        out_specs=(anyspec, pl.BlockSpec((8, gw), lambda h: (0, h))),
        scratch_shapes=[pltpu.VMEM((len(HG_LEVELS), BLK, BLK), F32),
                        pltpu.VMEM((nc, HG_GROUP, HG_HEAD_DIM, HG_HEAD_DIM), F32),
                        pltpu.VMEM((3, s_len, gw), BF16), pltpu.SemaphoreType.DMA((3,))],
        aliases={5: 0},
        semantics=("arbitrary",))(proj, proj, proj, do_b, lb, dproj)


def _dh_matmul(dproj, w_full, after, name):
    s_len, n = dproj.shape
    d = w_full.shape[0]
    tm = min(1024, s_len)
    tk = 4608

    def body(dp_ref, w_ref, after_ref, dh_ref):
        del after_ref
        part = _dot_nt(dp_ref[...], w_ref[...])

        @pl.when(pl.program_id(1) == 0)
        def _():
            dh_ref[...] = part

        @pl.when(pl.program_id(1) > 0)
        def _():
            dh_ref[...] += part

    return _pcall(
        body, name=name, out_shape=jax.ShapeDtypeStruct((s_len, d), F32),
        grid=(s_len // tm, n // tk),
        in_specs=[pl.BlockSpec((tm, tk), lambda i, k: (i, k)), pl.BlockSpec((d, tk), lambda i, k: (0, k)),
                  pl.BlockSpec(memory_space=pl.ANY)],
        out_specs=pl.BlockSpec((tm, d), lambda i, k: (i, 0)),
        semantics=("arbitrary", "arbitrary"))(dproj, w_full, after)


def _gw_matmul(h_t, dproj, name):
    d, s_len = h_t.shape
    n = dproj.shape[1]
    tn = 2304

    def body(ht_ref, dp_ref, gw_ref):
        gw_ref[...] = _dot(ht_ref[...], dp_ref[...]).astype(BF16)

    return _pcall(
        body, name=name, out_shape=jax.ShapeDtypeStruct((d, n), BF16),
        grid=(n // tn,),
        in_specs=[pl.BlockSpec((d, s_len), lambda j: (0, 0)), pl.BlockSpec((s_len, tn), lambda j: (0, j))],
        out_specs=pl.BlockSpec((d, tn), lambda j: (0, j)),
        semantics=("arbitrary",))(h_t, dproj)


def _ln_bwd(dh, x, scale, dres, name):
    s_len, d = x.shape
    tm = min(512, s_len)

    def body(dh_ref, x_ref, sc_ref, dres_ref, dx_ref, vec_ref):
        @pl.when(pl.program_id(0) == 0)
        def _():
            vec_ref[...] = jnp.zeros_like(vec_ref)

        dh = dh_ref[...]
        xs, rstd = _standardize(x_ref[...])
        vec_ref[0:1, :] += jnp.sum(dh, axis=0, keepdims=True)
        vec_ref[1:2, :] += jnp.sum(dh * xs, axis=0, keepdims=True)
        dx_ref[...] = _standardize_bwd(xs, rstd, dh * (1.0 + sc_ref[...])) + dres_ref[...]

    tile = pl.BlockSpec((tm, d), lambda i: (i, 0))
    return _pcall(body, name=name, grid=(s_len // tm,),
                  out_shape=(jax.ShapeDtypeStruct((s_len, d), F32), jax.ShapeDtypeStruct((8, d), F32)),
                  in_specs=[tile, tile, pl.BlockSpec((1, d), lambda i: (0, 0)), tile],
                  out_specs=(tile, pl.BlockSpec((8, d), lambda i: (0, 0))),
                  semantics=("arbitrary",))(dh, x, scale, dres)


def _wmod_grad(c_t, dmod):
    d = c_t.shape[0]
    n_layers, _, cm = dmod.shape

    def body(c_ref, dm_ref, o_ref):
        for l in range(n_layers):
            acc = None
            for b in range(NDEV):
                term = c_ref[:, b:b + 1] * dm_ref[l, b:b + 1, :]
                acc = term if acc is None else acc + term
            o_ref[l] = acc

    return _pcall(body, name="wmod_grad", out_shape=jax.ShapeDtypeStruct((n_layers, d, cm), F32))(c_t, dmod)


def _sum_adamw(parts, w, m, v, name, first_row=0, into=None, after=None):
    n_src, range_rows, cols = parts.shape
    rows = w.shape[0]
    tr = range_rows
    for cand in (512, 256, 128, 64, 32, 16, 8):
        if range_rows % cand == 0 and cand * cols * 4 <= (2 << 20):
            tr = cand
            break
    first_tile = first_row // tr
    assert first_row % tr == 0
    n_extra = (0 if into is None else 4) + (0 if after is None else 1)

    def body(p_ref, w_ref, m_ref, v_ref, *rest):
        g_ref, d_ref, nm_ref, nv_ref = rest[n_extra:]
        g = p_ref[0].astype(F32)
        for s in range(1, n_src):
            g = g + p_ref[s].astype(F32)
        g_ref[...] = g
        d_ref[...], nm_ref[...], nv_ref[...] = _adamw_step(g, w_ref[...], m_ref[...], v_ref[...])

    tile = pl.BlockSpec((tr, cols), lambda i: (i + first_tile, 0))
    anyspec = pl.BlockSpec(memory_space=pl.ANY)
    out = jax.ShapeDtypeStruct((rows, cols), F32)
    extra = ([] if into is None else list(into)) + ([] if after is None else [after])
    aliases = {} if into is None else {4 + k: k for k in range(4)}
    return _pcall(body, name=name, grid=(range_rows // tr,), out_shape=(out,) * 4,
                  in_specs=[pl.BlockSpec((n_src, tr, cols), lambda i: (0, i, 0)), tile, tile, tile]
                  + [anyspec] * len(extra),
                  out_specs=(tile,) * 4, aliases=aliases, semantics=("arbitrary",))(parts, w, m, v, *extra)


def _adamw_step(g, w, m, v):
    nm = ADAM_B1 * m + (1.0 - ADAM_B1) * g
    nv = ADAM_B2 * v + (1.0 - ADAM_B2) * (g * g)
    m_hat = nm / (1.0 - ADAM_B1 ** ADAM_STEP)
    v_hat = nv / (1.0 - ADAM_B2 ** ADAM_STEP)
    return -ADAM_LR * (m_hat / (jnp.sqrt(v_hat) + ADAM_EPS) + ADAM_WD * w), nm, nv


def _adamw_small(gs, ws, ms, vs):
    n = len(gs)

    def body(*refs):
        for p in range(n):
            results = _adamw_step(*(refs[k * n + p][...] for k in range(4)))
            for k in range(3):
                refs[(4 + k) * n + p][...] = results[k]

    shapes = [jax.ShapeDtypeStruct(w.shape, F32) for w in ws]
    outs = _pcall(body, name="adamw_small", out_shape=shapes * 3)(*gs, *ws, *ms, *vs)
    return [(outs[p], outs[n + p], outs[2 * n + p]) for p in range(n)]


def _sum_parts(parts, name):
    n_src = parts.shape[0]

    def body(p_ref, o_ref):
        acc = p_ref[0]
        for s in range(1, n_src):
            acc = acc + p_ref[s]
        o_ref[...] = acc

    return _pcall(body, name=name, out_shape=jax.ShapeDtypeStruct(parts.shape[1:], F32))(parts)


def _pair_sum(gw, stage, me, name):
    d = gw.shape[0]
    n_slots, _, shard = stage.shape

    def body(me_ref, g_ref, s_ref, own_ref, o_ref):
        del me_ref
        total = (g_ref[...].astype(F32) + s_ref[0].astype(F32)).astype(BF16)
        o_ref[0] = total

        @pl.when(pl.program_id(0) == 0)
        def _():
            own_ref[0] = total

    slot = pl.BlockSpec((1, d, shard), lambda jj, me_ref: (jj, 0, 0))
    out = jax.ShapeDtypeStruct(stage.shape, BF16)
    return pl.pallas_call(
        body, name=name, out_shape=(out, out),
        grid_spec=pltpu.PrefetchScalarGridSpec(
            num_scalar_prefetch=1, grid=(n_slots,),
            in_specs=[pl.BlockSpec((d, shard), lambda jj, me_ref: (0, me_ref[0] ^ (2 * jj))), slot],
            out_specs=(pl.BlockSpec((1, d, shard), lambda jj, me_ref: (0, 0, 0)), slot)),
        compiler_params=pltpu.CompilerParams(dimension_semantics=("arbitrary",), vmem_limit_bytes=VMEM_LIMIT),
        interpret=False)(me.reshape(1).astype(jnp.int32), gw, stage)


def _lower_bound_table(lower_bounds):
    p = jax.nn.softmax(lower_bounds.astype(F32), axis=0)
    return jnp.cumsum(p, axis=0) - p[0:1]


def _pad_rows(v, width):
    n = v.shape[0]
    rows = -(-n // width)
    rows = -(-rows // 8) * 8
    return jnp.pad(v, (0, rows * width - n)).reshape(rows, width)


def kernel(x, c, w_mod, b_mod, w_in, conv_w, hgrn_norm_w, lower_bounds, w_branch, w_out, ln_g, ln_b, loss_target, m_w_mod, m_b_mod, m_w_in, m_conv_w, m_hgrn_norm_w, m_lower_bounds, m_w_branch, m_w_out, m_ln_g, m_ln_b, v_w_mod, v_b_mod, v_w_in, v_conv_w, v_hgrn_norm_w, v_lower_bounds, v_w_branch, v_w_out, v_ln_g, v_ln_b):
    n_layers = N_LAYERS
    s_len, d = x.shape[1], x.shape[2]
    n_cols = w_in.shape[2] * NDEV
    cw_cols = conv_w.shape[2]
    cm = w_mod.shape[2]
    me = _my_index()
    x0 = x[0]
    target = loss_target[0]

    small = _pad_rows(jnp.concatenate([c.reshape(-1), conv_w.reshape(-1)]), BLK)
    small_all = _all_gather_small("gather_c_conv", small).reshape(NDEV, -1)
    c_all = small_all[:, :d]
    conv_full = small_all[:, d:d + n_layers * 3 * cw_cols].reshape(NDEV, n_layers, 3, cw_cols)
    conv_full = conv_full.transpose(1, 2, 0, 3).reshape(n_layers, 3, WIDTH)

    b_mod_mine = lax.dynamic_slice_in_dim(b_mod, me * cm, cm, axis=1).reshape(n_layers, 1, cm)
    mod_cols = _mod_fwd(c_all, w_mod, b_mod_mine)
    mod_all = _all_gather_small("gather_mod", mod_cols.reshape(n_layers * NDEV, cm))
    mod_all = mod_all.reshape(NDEV, n_layers, NDEV, cm)
    mod_mine = lax.dynamic_index_in_dim(mod_all, me, axis=2, keepdims=False)
    mod_mine = mod_mine.transpose(1, 0, 2).reshape(n_layers, 3, 1, d)

    shard = w_in.shape[2]
    dsh = d // NDEV
    w_in_b, w_branch_b, w_out_b = w_in.astype(BF16), w_branch.astype(BF16), w_out.astype(BF16)
    window = lambda ref, dev: ref.at[:, pl.ds(pl.multiple_of(dev * shard, BLK), shard)]

    def two_step_sends(places):
        chips, sibling = [], []
        for k in (1, 2, 4, 6):
            for a, place in enumerate(places):
                chips.append((k, lambda ins, lands, me, a=a: ins[a],
                              lambda lands, me, a=a, place=place: place(lands[a], me),
                              lambda lands, me, a=a, k=k, place=place: place(lands[a], me ^ k)))
        for j in (2, 4, 6):
            for a, place in enumerate(places):
                sibling.append((1, lambda ins, lands, me, a=a, j=j, place=place: place(lands[a], me ^ j),
                                lambda lands, me, a=a, j=j, place=place: place(lands[a], me ^ j),
                                lambda lands, me, a=a, j=j, place=place: place(lands[a], me ^ 1 ^ j)))
        return chips, sibling

    in_sends = two_step_sends([window])
    rest_sends = two_step_sends([_slot, _slot])
    layer_sends = two_step_sends([window, _slot, _slot])

    def in_land(l):
        return _place_own_window(f"place_w_in_{l}", (d, n_cols), w_in_b[l], me)

    def rest_lands(l):
        return [_place_own((NDEV, 3, WIDTH, dsh), BF16, w_branch_b[l][None], (me, 0, 0, 0)),
                _place_own((NDEV, dsh, d), BF16, w_out_b[l][None], (me, 0, 0))]

    def gather_start(name, shards, lands, sends, after):
        return _exchange_start(f"{name}_chips_start", shards, lands, sends[0], after)

    def gather_pass_on(name, started, after, sends):
        _, lands = _exchange_wait(f"{name}_chips_wait", started, after, sends[0])
        return _exchange_start(f"{name}_sibling_start", [], lands, sends[1])

    def gather_finish(name, started, after, sends):
        return _exchange_wait(f"{name}_sibling_wait", started, after, sends[1])[1]

    def branch_out_weights(w_branch_l, w_out_l):
        return w_branch_l.transpose(1, 2, 0, 3).reshape(3, WIDTH, d), w_out_l.reshape(d, d)

    gathering = gather_start("gather_w_in_0", [w_in_b[0]], [in_land(0)], in_sends, mod_mine)
    passing = gather_pass_on("gather_w_in_0", gathering, gathering[4], in_sends)
    rest_gathering = gather_start("gather_rest_0", [w_branch_b[0], w_out_b[0]], rest_lands(0), rest_sends, passing[4])
    next_gathering = None
    if n_layers > 1:
        next_gathering = gather_start("gather_weights_1", [w_in_b[1], w_branch_b[1], w_out_b[1]],
                                      [in_land(1)] + rest_lands(1), layer_sends, rest_gathering[4])
    w_in_l = gather_finish("gather_w_in_0", passing, (next_gathering or rest_gathering)[4], in_sends)[0]

    lbs = _lower_bound_table(lower_bounds)
    norm_w4 = jnp.tile(hgrn_norm_w, (1, WIDTH // HG_HEAD_DIM))

    saved = []
    xl = x0
    for l in range(n_layers):
        shift, scale, gate = mod_mine[l, 0], mod_mine[l, 1], mod_mine[l, 2]
        proj, h_t = _ln_proj(xl, shift, scale, w_in_l, f"ln_proj_{l}")
        o_a, totals = _sb_fwd(proj, f"sb_fwd_{l}")
        if l == 0:
            rest_passing = gather_pass_on("gather_rest_0", rest_gathering, o_a, rest_sends)
        lb_l = lbs[l:l + 1] + rest_passing[4][0, 0] if l == 0 else lbs[l:l + 1]
        o_b = _hgrn_fwd(proj, lb_l, f"hgrn_fwd_{l}")
        if l == 0:
            wb_l, wo_l = branch_out_weights(*gather_finish("gather_rest_0", rest_passing, o_b, rest_sends))
            if n_layers > 1:
                next_passing = gather_pass_on("gather_weights_1", next_gathering, o_b, layer_sends)
                gate = gate + next_passing[4][0, 0]
        x_new, merged, ycat = _merge_fwd(xl, proj, o_a, o_b, gate, norm_w4[l:l + 1], conv_full[l],
                                         wb_l, wo_l, ln_g[l:l + 1], ln_b[l:l + 1], f"merge_fwd_{l}")
        saved.append((xl, proj, h_t, o_a, totals, o_b, merged, ycat, w_in_l, wb_l, wo_l))
        if l == 0 and n_layers > 1:
            w_in_l, w_branch_l, w_out_l = gather_finish("gather_weights_1", next_passing, x_new, layer_sends)
            wb_l, wo_l = branch_out_weights(w_branch_l, w_out_l)
        xl = x_new

    loss_part, dx = _loss_fwd_bwd(xl, target)

    pair_sends = [(1, lambda ins, lands, me, j=j: window(ins[0], me ^ 1 ^ j),
                   lambda lands, me, jj=jj: lands[0].at[jj], lambda lands, me, jj=jj: lands[0].at[jj])
                  for jj, j in enumerate((0, 2, 4, 6))]
    chip_sum_sends = [(j, lambda ins, lands, me, jj=jj: ins[0].at[jj],
                       lambda lands, me, jj=jj: lands[0].at[jj], lambda lands, me, jj=jj: lands[0].at[jj])
                      for jj, j in ((1, 2), (2, 4), (3, 6))]
    rest_scatter = _direct_sends([(0, 0, _slot, _slot), (1, 1, _slot, _slot)])
    scattering = [None] * n_layers
    small_grads = [None] * n_layers
    dmod = [None] * n_layers
    tie = None
    for l in reversed(range(n_layers)):
        xl, proj, h_t, o_a, totals, o_b, merged, ycat, w_in_l, wb_l, wo_l = saved[l]
        scale, gate = mod_mine[l, 1], mod_mine[l, 2]
        if tie is not None:
            gate = gate + tie[0, 0]
        dres, dycat, dproj, gwo_by_owner, gwb_by_owner, mvec = _merge_bwd(
            dx, xl, merged, ycat, proj, gate, wb_l, wo_l, ln_g[l:l + 1], f"merge_bwd_{l}")
        lands = [_place_own((NDEV, 3, WIDTH, dsh), BF16, lax.dynamic_slice_in_dim(gwb_by_owner, me, 1, axis=0),
                            (me, 0, 0, 0)),
                 _place_own((NDEV, dsh, d), BF16, lax.dynamic_slice_in_dim(gwo_by_owner, me, 1, axis=0),
                            (me, 0, 0))]
        rest_started = _exchange_start(f"scatter_rest_{l}_start", [gwb_by_owner, gwo_by_owner], lands, rest_scatter)
        dproj, do_a, do_b, bvec = _branch_bwd(dycat, proj, o_a, o_b, norm_w4[l:l + 1] + rest_started[4][0, 0],
                                              conv_full[l], dproj, f"branch_bwd_{l}")
        dproj = _sb_bwd(proj, do_a, totals, dproj, f"sb_bwd_{l}")
        dproj, dlb = _hgrn_bwd(proj, do_b, lbs[l:l + 1], dproj, f"hgrn_bwd_{l}")
        gwi = _gw_matmul(h_t, dproj, f"gw_matmul_{l}")
        swapping = _exchange_start(f"scatter_in_{l}_sibling_start", [gwi], [lax.empty((4, d, shard), BF16)], pair_sends)
        if l > 0:
            dh = _dh_matmul(dproj, w_in_l, swapping[4], f"dh_matmul_{l}")
        (gwi,), (stage,) = _exchange_wait(f"scatter_in_{l}_sibling_wait", swapping, dh if l > 0 else swapping[4],
                                          pair_sends)
        land, chip_sums = _pair_sum(gwi, stage, me, f"pair_sum_{l}")
        in_started = _exchange_start(f"scatter_in_{l}_chips_start", [chip_sums], [land], chip_sum_sends)
        scattering[l] = (in_started, rest_started)
        tie = in_started[4]
        if l == 0:
            dh = _dh_matmul(dproj, w_in_l, tie, f"dh_matmul_{l}")
        dx, lvec = _ln_bwd(dh, xl, scale + tie[0, 0], dres, f"ln_bwd_{l}")
        dmod[l] = jnp.concatenate([lvec[0], lvec[1], mvec[2]])
        norm_grad = bvec[0].reshape(WIDTH // HG_HEAD_DIM, HG_HEAD_DIM).sum(axis=0)
        small_grads[l] = jnp.concatenate([mvec[0], mvec[1], norm_grad, dlb[0], bvec[1:4].reshape(-1)])
    grad_x = dx[None]

    flat = lambda a: a.reshape(-1, a.shape[-1])
    big = {"w_in": (w_in, m_w_in, v_w_in), "w_branch": (w_branch, m_w_branch, v_w_branch),
           "w_out": (w_out, m_w_out, v_w_out)}
    big_results = {n: None for n in big}

    def adam_layer(l, after):
        in_started, rest_started = scattering[l]
        p_branch_l, p_out_l = _exchange_wait(f"scatter_rest_{l}_wait", rest_started, after, rest_scatter)[1]
        p_in_l = _exchange_wait(f"scatter_in_{l}_chips_wait", in_started, after, chip_sum_sends)[1][0]
        parts = {"w_in": p_in_l, "w_branch": p_branch_l.reshape(NDEV, 3 * WIDTH, dsh), "w_out": p_out_l}
        last = None
        for n, (w, m, v) in big.items():
            rows_per_layer = flat(w).shape[0] // n_layers
            big_results[n] = _sum_adamw(parts[n], flat(w), flat(m), flat(v), f"adamw_{n}_{l}",
                                        first_row=l * rows_per_layer, into=big_results[n], after=last)
            last = big_results[n][3]
        return last

    after_adam = None
    for l in reversed(range(1, n_layers)):
        after_adam = adam_layer(l, tie)

    small_vec = jnp.concatenate(dmod + small_grads + [loss_part.reshape(1)])
    n_small = small_vec.shape[0]
    small_all = _all_gather_small("gather_small_grads", _pad_rows(small_vec, BLK), after=after_adam)
    small_sum = _sum_parts(small_all, "sum_small_grads").reshape(-1)[:n_small]
    dmod_all = small_all.reshape(NDEV, -1)[:, :n_layers * 3 * d].reshape(NDEV, n_layers, 3 * d)

    loss = small_sum[n_small - 1]

    off = n_layers * 3 * d
    grad_b_mod = small_sum[:off].reshape(n_layers, 3 * d)
    per_layer = 2 * d + HG_HEAD_DIM + WIDTH + 3 * WIDTH
    g_ln_g, g_ln_b, g_norm, g_lbs, g_conv = [], [], [], [], []
    for l in range(n_layers):
        seg = small_sum[off + l * per_layer: off + (l + 1) * per_layer]
        g_ln_g.append(seg[:d])
        g_ln_b.append(seg[d:2 * d])
        g_norm.append(seg[2 * d:2 * d + HG_HEAD_DIM])
        g_lbs.append(seg[2 * d + HG_HEAD_DIM:2 * d + HG_HEAD_DIM + WIDTH])
        g_conv.append(seg[2 * d + HG_HEAD_DIM + WIDTH:].reshape(3, WIDTH))
    grad_ln_g, grad_ln_b = jnp.stack(g_ln_g), jnp.stack(g_ln_b)
    grad_norm = jnp.stack(g_norm)
    _, lbs_vjp = jax.vjp(_lower_bound_table, lower_bounds)
    grad_lower = lbs_vjp(jnp.stack(g_lbs))[0]
    grad_conv = lax.dynamic_slice_in_dim(jnp.stack(g_conv), me * cw_cols, cw_cols, axis=2)

    dmod_mine = lax.dynamic_slice_in_dim(dmod_all, me * cm, cm, axis=2).transpose(1, 0, 2)
    grad_w_mod = _wmod_grad(c_all.T, dmod_mine)

    adam_layer(0, grad_w_mod)
    r_w_in, r_w_branch, r_w_out = ([o.reshape(big[n][0].shape) for o in big_results[n]]
                                   for n in ("w_in", "w_branch", "w_out"))
    r_w_mod = [o.reshape(w_mod.shape) for o in
               _sum_adamw(grad_w_mod.reshape(1, -1, cm), flat(w_mod), flat(m_w_mod), flat(v_w_mod), "adamw_w_mod")]

    small_names = ["b_mod", "conv_w", "hgrn_norm_w", "lower_bounds", "ln_g", "ln_b"]
    small_g = [grad_b_mod, grad_conv, grad_norm, grad_lower, grad_ln_g, grad_ln_b]
    small_w = [b_mod, conv_w, hgrn_norm_w, lower_bounds, ln_g, ln_b]
    small_m = [m_b_mod, m_conv_w, m_hgrn_norm_w, m_lower_bounds, m_ln_g, m_ln_b]
    small_v = [v_b_mod, v_conv_w, v_hgrn_norm_w, v_lower_bounds, v_ln_g, v_ln_b]
    as_rows = lambda a: a.reshape(-1, a.shape[-1])
    updates = _adamw_small([as_rows(a) for a in small_g], [as_rows(a) for a in small_w],
                           [as_rows(a) for a in small_m], [as_rows(a) for a in small_v])
    r_small = {n: [g] + [u.reshape(w.shape) for u in upd]
               for n, g, w, upd in zip(small_names, small_g, small_w, updates)}

    results = {"w_mod": r_w_mod, "w_in": r_w_in, "w_branch": r_w_branch, "w_out": r_w_out, **r_small}
    order = ["w_mod", "b_mod", "w_in", "conv_w", "hgrn_norm_w", "lower_bounds", "w_branch", "w_out", "ln_g", "ln_b"]
    outs = [loss, grad_x]
    for idx in range(4):
        outs.extend(results[n][idx] for n in order)
    return tuple(outs)
```

```python
import jax
import jax.numpy as jnp
from jax import lax
from jax.experimental import pallas as pl
from jax.experimental.pallas import tpu as pltpu

F32 = jnp.float32
BF16 = jnp.bfloat16
NDEV = 8
N_LAYERS = 2
SB_HEAD_DIM = 64
HG_HEAD_DIM = 128
WIDTH = 512
BLK = 128
LN_EPS = 1e-5
RMS_EPS = 1e-6
ALPHA = (2.0 * N_LAYERS) ** 0.25
ADAM_LR, ADAM_B1, ADAM_B2, ADAM_EPS, ADAM_WD, ADAM_STEP = 0.001, 0.9, 0.999, 1e-08, 0.01, 10
VMEM_LIMIT = 56 * 1024 * 1024
MESH = pl.DeviceIdType.MESH
HG_LEVELS = (64, 32, 16, 8, 4, 2, 1)


def _pcall(body, *, name, out_shape, grid=None, in_specs=None, out_specs=None, scratch_shapes=(),
           semantics=None, aliases=None):
    kwargs = {}
    if grid is not None:
        kwargs["grid"] = grid
    if in_specs is not None:
        kwargs["in_specs"] = in_specs
    if out_specs is not None:
        kwargs["out_specs"] = out_specs
    if aliases:
        kwargs["input_output_aliases"] = aliases
    return pl.pallas_call(
        body, name=name, out_shape=out_shape, scratch_shapes=list(scratch_shapes),
        compiler_params=pltpu.CompilerParams(dimension_semantics=semantics, vmem_limit_bytes=VMEM_LIMIT),
        interpret=False, **kwargs)


def _dot(a, b):
    return jnp.dot(a, b, preferred_element_type=F32)


def _dot_nt(a, b):
    return lax.dot_general(a, b, (((1,), (1,)), ((), ())), preferred_element_type=F32)


def _dot_tn(a, b):
    return lax.dot_general(a, b, (((0,), (0,)), ((), ())), preferred_element_type=F32)


def _dot_01_l(m_bf16, x):
    x1 = x.astype(BF16)
    x2 = (x - x1.astype(F32)).astype(BF16)
    return _dot(jnp.concatenate([m_bf16, m_bf16], axis=1), jnp.concatenate([x1, x2], axis=0))


def _sigmoid(x):
    return 1.0 / (1.0 + jnp.exp(-x))


def _silu_and_grad(x):
    s = _sigmoid(x)
    return x * s, s * (1.0 + x * (1.0 - s))


LOG2E = 1.4426950408889634
MASKED_SCORE = -1e30


def _softplus2_parts(z2):
    minus_abs = lax.bitcast_convert_type(lax.bitcast_convert_type(z2, jnp.int32) | jnp.int32(-2 ** 31), F32)
    sp2 = jnp.maximum(z2, 0.0) + jnp.log2(1.0 + jnp.exp2(minus_abs))
    return sp2, jnp.exp2(z2 - sp2)


def _split2_lanes(x):
    x1 = x.astype(BF16)
    return jnp.concatenate([x1, (x - x1.astype(F32)).astype(BF16)], axis=1)


def _iota2(shape, dim):
    return lax.broadcasted_iota(jnp.int32, shape, dim)


def _standardize(x):
    mu = jnp.mean(x, axis=-1, keepdims=True)
    xc = x - mu
    var = jnp.mean(xc * xc, axis=-1, keepdims=True)
    rstd = lax.rsqrt(var + LN_EPS)
    return xc * rstd, rstd


def _standardize_bwd(xhat, rstd, dxhat):
    m1 = jnp.mean(dxhat, axis=-1, keepdims=True)
    m2 = jnp.mean(dxhat * xhat, axis=-1, keepdims=True)
    return rstd * (dxhat - m1 - xhat * m2)


def _my_index():
    return 4 * lax.axis_index("x") + 2 * lax.axis_index("y") + lax.axis_index("c")


def _exchange(name, ins, out_shapes, transfers, in_vmem, after=None):
    n_in, n_out, n_t = len(ins), len(out_shapes), len(transfers)

    def body(*refs):
        n_skip = n_in + (0 if after is None else 1)
        in_refs, out_refs = refs[:n_in], refs[n_skip:n_skip + n_out]
        send_sems, recv_sems, local_sems = refs[n_skip + n_out:]
        x, y, c = lax.axis_index("x"), lax.axis_index("y"), lax.axis_index("c")
        me = 4 * x + 2 * y + c
        started = []
        for t, (i, o, src_fn, dst_fn) in enumerate(transfers):
            own = pltpu.make_async_copy(src_fn(in_refs[i], me), dst_fn(out_refs[o], me), local_sems.at[t])
            own.start()
            started.append(own)
        arrivals = []
        for k in range(1, NDEV):
            px = x ^ ((k >> 2) & 1)
            py = y ^ ((k >> 1) & 1)
            pc = c ^ (k & 1)
            peer = 4 * px + 2 * py + pc
            for t, (i, o, src_fn, dst_fn) in enumerate(transfers):
                sem = t * (NDEV - 1) + k - 1
                push = pltpu.make_async_remote_copy(
                    src_ref=src_fn(in_refs[i], peer), dst_ref=dst_fn(out_refs[o], me),
                    send_sem=send_sems.at[sem], recv_sem=recv_sems.at[sem],
                    device_id=(px, py, pc), device_id_type=MESH)
                push.start()
                started.append(push)
                arrivals.append(pltpu.make_async_remote_copy(
                    src_ref=src_fn(in_refs[i], peer), dst_ref=dst_fn(out_refs[o], peer),
                    send_sem=send_sems.at[sem], recv_sem=recv_sems.at[sem],
                    device_id=(px, py, pc), device_id_type=MESH))
        for arrival in arrivals:
            arrival.wait_recv()
        for cp in started[n_t:]:
            cp.wait_send()
        for own in started[:n_t]:
            own.wait()

    space = pltpu.VMEM if in_vmem else pl.ANY
    spec = pl.BlockSpec(memory_space=space)
    extra = [] if after is None else [after]
    return _pcall(
        body, name=name, out_shape=out_shapes,
        in_specs=[spec] * n_in + [pl.BlockSpec(memory_space=pl.ANY)] * len(extra), out_specs=[spec] * n_out,
        scratch_shapes=[pltpu.SemaphoreType.DMA((n_t * (NDEV - 1),)),
                        pltpu.SemaphoreType.DMA((n_t * (NDEV - 1),)),
                        pltpu.SemaphoreType.DMA((n_t,))])(*ins, *extra)


def _whole(ref, dev):
    return ref


def _slot(ref, dev):
    return ref.at[dev]


def _all_gather_small(name, v, after=None):
    out = _exchange(name, [v], [jax.ShapeDtypeStruct((NDEV,) + v.shape, v.dtype)],
                    [(0, 0, _whole, _slot)], in_vmem=True, after=after)
    return out[0]


_HBM_SPEC = pl.BlockSpec(memory_space=pltpu.HBM)
_SEM_SPEC = pl.BlockSpec(memory_space=pltpu.SEMAPHORE)
_DATAFLOW = pltpu.SideEffectType.DATAFLOW_SIDE_EFFECTING


def _peer(x, y, c, k):
    px = x ^ ((k >> 2) & 1)
    py = y ^ ((k >> 1) & 1)
    pc = c ^ (k & 1)
    return (px, py, pc), 4 * px + 2 * py + pc


def _direct_sends(transfers):
    sends = []
    for k in range(1, NDEV):
        for i, o, src_fn, dst_fn in transfers:
            sends.append((k,
                          lambda ins, lands, me, i=i, k=k, src_fn=src_fn: src_fn(ins[i], me ^ k),
                          lambda lands, me, o=o, dst_fn=dst_fn: dst_fn(lands[o], me),
                          lambda lands, me, o=o, k=k, dst_fn=dst_fn: dst_fn(lands[o], me ^ k)))
    return sends


def _exchange_start(name, ins, lands, sends, after=None):
    n_in, n_buf = len(ins), len(ins) + len(lands)
    n_sem = len(sends)

    def body(*refs):
        in_refs, land_refs = refs[:n_in], refs[n_in:n_buf]
        n_skip = n_buf + (0 if after is None else 1)
        send_sems, recv_sems, token = refs[n_skip], refs[n_skip + 1], refs[-1]
        x, y, c = lax.axis_index("x"), lax.axis_index("y"), lax.axis_index("c")
        me = 4 * x + 2 * y + c
        for t, (k, src_fn, dst_fn, _) in enumerate(sends):
            pltpu.make_async_remote_copy(
                src_ref=src_fn(in_refs, land_refs, me), dst_ref=dst_fn(land_refs, me),
                send_sem=send_sems.at[t], recv_sem=recv_sems.at[t],
                device_id=_peer(x, y, c, k)[0], device_id_type=MESH).start()
        token[...] = jnp.zeros_like(token)

    bufs = [pltpu.with_memory_space_constraint(a, pltpu.HBM) for a in list(ins) + list(lands)]
    extra = [] if after is None else [after]
    outs = pl.pallas_call(
        body, name=name,
        out_shape=(pltpu.SemaphoreType.DMA((n_sem,)), pltpu.SemaphoreType.DMA((n_sem,)))
        + tuple(pltpu.HBM(a.shape, a.dtype) for a in bufs) + (jax.ShapeDtypeStruct((8, BLK), F32),),
        in_specs=[_HBM_SPEC] * n_buf + [pl.BlockSpec(memory_space=pl.ANY)] * len(extra),
        out_specs=(_SEM_SPEC, _SEM_SPEC) + (_HBM_SPEC,) * n_buf + (pl.BlockSpec(memory_space=pltpu.VMEM),),
        input_output_aliases={b: 2 + b for b in range(n_buf)},
        compiler_params=pltpu.CompilerParams(has_side_effects=_DATAFLOW),
        interpret=False)(*bufs, *extra)
    return outs[0], outs[1], list(outs[2:2 + n_in]), list(outs[2 + n_in:2 + n_buf]), outs[-1]


def _exchange_wait(name, started, after, sends):
    send_sems, recv_sems, ins, lands, _ = started
    n_in, n_buf = len(ins), len(ins) + len(lands)

    def body(*refs):
        in_refs, land_refs = refs[:n_in], refs[n_in:n_buf]
        send_sems, recv_sems = refs[n_buf], refs[n_buf + 1]
        x, y, c = lax.axis_index("x"), lax.axis_index("y"), lax.axis_index("c")
        me = 4 * x + 2 * y + c
        for t, (k, src_fn, _, rcv_fn) in enumerate(sends):
            cp = pltpu.make_async_remote_copy(
                src_ref=src_fn(in_refs, land_refs, me), dst_ref=rcv_fn(land_refs, me),
                send_sem=send_sems.at[t], recv_sem=recv_sems.at[t],
                device_id=_peer(x, y, c, k)[0], device_id_type=MESH)
            cp.wait_send()
            cp.wait_recv()

    bufs = list(ins) + list(lands)
    outs = pl.pallas_call(
        body, name=name, out_shape=tuple(pltpu.HBM(a.shape, a.dtype) for a in bufs),
        in_specs=[_HBM_SPEC] * n_buf + [_SEM_SPEC, _SEM_SPEC, pl.BlockSpec(memory_space=pl.ANY)],
        out_specs=(_HBM_SPEC,) * n_buf,
        input_output_aliases={b: b for b in range(n_buf)},
        compiler_params=pltpu.CompilerParams(has_side_effects=_DATAFLOW),
        interpret=False)(*bufs, send_sems, recv_sems, after)
    return list(outs[:n_in]), list(outs[n_in:])


def _place_own(shape, dtype, own, start):
    return lax.dynamic_update_slice(lax.empty(shape, dtype), own, start)


def _place_own_window(name, shape, own, me):
    rows, cols = own.shape

    def body(me_ref, zone_in, own_ref, zone_ref):
        del me_ref, zone_in
        zone_ref[...] = own_ref[...]

    return pl.pallas_call(
        body, name=name, out_shape=jax.ShapeDtypeStruct(shape, own.dtype),
        grid_spec=pltpu.PrefetchScalarGridSpec(
            num_scalar_prefetch=1, grid=(1,),
            in_specs=[pl.BlockSpec(memory_space=pl.ANY), pl.BlockSpec((rows, cols), lambda i, me_ref: (0, 0))],
            out_specs=pl.BlockSpec((rows, cols), lambda i, me_ref: (0, me_ref[0]))),
        input_output_aliases={1: 0},
        compiler_params=pltpu.CompilerParams(dimension_semantics=("arbitrary",), vmem_limit_bytes=VMEM_LIMIT),
        interpret=False)(me.reshape(1).astype(jnp.int32), lax.empty(shape, own.dtype), own)


def _mod_fwd(c_all, w_mod, b_mod_mine):
    n_layers, _, cm = w_mod.shape

    def body(c_ref, w_ref, b_ref, o_ref):
        for l in range(n_layers):
            o_ref[l] = jnp.dot(c_ref[...], w_ref[l], preferred_element_type=F32,
                               precision=lax.Precision.HIGHEST) + b_ref[l]

    return _pcall(body, name="mod_fwd", out_shape=jax.ShapeDtypeStruct((n_layers, NDEV, cm), F32))(
        c_all, w_mod, b_mod_mine)


def _ln_proj(x, shift, scale, w_full, name):
    s_len, d = x.shape
    n = w_full.shape[1]
    tm = min(1024, s_len)
    tn = 2304

    def body(x_ref, sh_ref, sc_ref, w_ref, proj_ref, ht_ref, h_scr):
        @pl.when(pl.program_id(1) == 0)
        def _():
            xs, _ = _standardize(x_ref[...])
            h = xs * (1.0 + sc_ref[...]) + sh_ref[...]
            h_scr[...] = h.astype(BF16)
            ht_ref[...] = h.T.astype(BF16)

        proj_ref[...] = _dot(h_scr[...], w_ref[...])

    return _pcall(
        body, name=name,
        out_shape=(jax.ShapeDtypeStruct((s_len, n), F32), jax.ShapeDtypeStruct((d, s_len), BF16)),
        grid=(s_len // tm, n // tn),
        in_specs=[pl.BlockSpec((tm, d), lambda i, j: (i, 0)),
                  pl.BlockSpec((1, d), lambda i, j: (0, 0)),
                  pl.BlockSpec((1, d), lambda i, j: (0, 0)),
                  pl.BlockSpec((d, tn), lambda i, j: (0, j))],
        out_specs=(pl.BlockSpec((tm, tn), lambda i, j: (i, j)),
                   pl.BlockSpec((d, tm), lambda i, j: (0, i))),
        scratch_shapes=[pltpu.VMEM((tm, d), BF16)],
        semantics=("arbitrary", "arbitrary"))(x, shift, scale, w_full)


SB_Q_ROWS = 256
SB_K_BLOCKS = 2


def _sb_fwd(proj, name):
    s_len = proj.shape[0]
    n_pairs = WIDTH // BLK
    qr = min(SB_Q_ROWS, s_len)
    gb = SB_K_BLOCKS
    kw = gb * BLK
    nq = s_len // qr
    assert qr == kw

    def body(q_ref, k_ref, v_ref, o_ref, tot_ref):
        lane = _iota2((1, BLK), 1)
        row = _iota2((BLK, BLK), 0)
        col = _iota2((BLK, BLK), 1)
        half = jnp.concatenate([(row >= col).astype(BF16), jnp.ones((BLK, BLK), BF16)], axis=1)
        suffix_and_sum = jnp.concatenate([half, half], axis=0)
        strict = _iota2((qr, kw), 1) < _iota2((qr, kw), 0)
        head_lanes = [(lane // SB_HEAD_DIM) == hh for hh in range(2)]

        def scores(gi, qms, masked):
            c0 = pl.multiple_of(gi * kw, kw)
            kb = k_ref[pl.ds(c0, kw), :].astype(BF16)
            z2s = [_dot_nt(qms[hh], kb) for hh in range(2)]
            if masked:
                z2s = [jnp.where(strict, z2, MASKED_SCORE) for z2 in z2s]
            return tuple(z2s)

        def accumulate(gi, z2s, carry):
            c0 = pl.multiple_of(gi * kw, kw)
            vf = v_ref[pl.ds(c0, kw), :]
            sp2s = [_softplus2_parts(z2)[0] for z2 in z2s]
            terms = [[_split2_lanes(sp2[:, b * BLK:(b + 1) * BLK]) for b in range(gb)] for sp2 in sp2s]
            sums = [[_dot(t, suffix_and_sum) for t in head_terms] for head_terms in terms]
            weights, laters = [], []
            for hh in range(2):
                later = carry[2 * hh + 1]
                parts = [None] * gb
                for b in reversed(range(gb)):
                    parts[b] = sums[hh][b][:, :BLK] + later
                    later = later + sums[hh][b][:, BLK:]
                weights.append(jnp.exp2(z2s[hh] - jnp.concatenate(parts, axis=1)).astype(BF16))
                laters.append(later)
            outs = [_dot(weights[hh], jnp.where(head_lanes[hh], vf, 0.0).astype(BF16)) for hh in range(2)]
            return (carry[0] + outs[0], laters[0], carry[2] + outs[1], laters[1])

        def queries(i):
            qf = q_ref[pl.ds(pl.multiple_of(i * qr, qr), qr), :] * (SB_HEAD_DIM ** -0.5 * LOG2E)
            return [jnp.where(head_lanes[hh], qf, 0.0).astype(BF16) for hh in range(2)]

        def qtile(i, first_scores):
            r0 = pl.multiple_of(i * qr, qr)
            qms = queries(i)
            zero = jnp.zeros((qr, BLK), F32)

            def step(jj, state):
                gi = i - 1 - jj
                return scores(gi, qms, False) + accumulate(gi + 1, state[:2], state[2:])

            state = lax.fori_loop(0, i, step, first_scores + (zero,) * 4)
            nxt = jnp.minimum(i + 1, nq - 1)
            next_scores = scores(nxt, queries(nxt), True)
            carry = accumulate(0, state[:2], state[2:])
            o_ref[pl.ds(r0, qr), :] = carry[0] + carry[2]
            tot_ref[0, pl.ds(r0, qr), :] = carry[1]
            tot_ref[1, pl.ds(r0, qr), :] = carry[3]
            return next_scores

        lax.fori_loop(0, nq, qtile, scores(0, queries(0), True))

    col_spec = lambda off: pl.BlockSpec((s_len, BLK), lambda p: (0, off + p))
    return _pcall(
        body, name=name,
        out_shape=(jax.ShapeDtypeStruct((s_len, WIDTH), F32),
                   jax.ShapeDtypeStruct((2 * n_pairs, s_len, BLK), F32)),
        grid=(n_pairs,),
        in_specs=[col_spec(0), col_spec(n_pairs), col_spec(2 * n_pairs)],
        out_specs=(pl.BlockSpec((s_len, BLK), lambda p: (0, p)),
                   pl.BlockSpec((2, s_len, BLK), lambda p: (p, 0, 0))),
        semantics=("arbitrary",))(proj, proj, proj)


def _hg_masks(mask_ref):
    row = _iota2((BLK, BLK), 0)
    col = _iota2((BLK, BLK), 1)
    for v, m in enumerate(HG_LEVELS):
        same = (row // (2 * m)) == (col // (2 * m))
        mask_ref[v] = (same & ((row & m) != 0) & ((col & m) == 0)).astype(F32)


def _hg_mid(b, m):
    if m >= 4:
        n = BLK // (2 * m)
        mid = b.reshape(n, 2 * m, BLK)[:, m - 1:m, :]
        return jnp.broadcast_to(mid, (n, 2 * m, BLK)).reshape(BLK, BLK)
    pos = _iota2((BLK, BLK), 0) & (2 * m - 1)
    out = b
    for p in range(2 * m):
        delta = (m - 1) - p
        if delta != 0:
            out = jnp.where(pos == p, pltpu.roll(b, (-delta) % BLK, 0), out)
    return out


def _hg_chunk_inputs(qraw, fpre, lb):
    sig = _sigmoid(fpre)
    f = lb + (1.0 - lb) * sig
    g = jnp.log(f)
    q, dq_fac = _silu_and_grad(qraw)
    return q, dq_fac, f, sig, g


HG_GROUP = 4


def _neg_abs(x):
    return lax.bitcast_convert_type(lax.bitcast_convert_type(x, jnp.int32) | jnp.int32(-2 ** 31), F32)


def _hg_level_terms(qs, ks, bs, m):
    es = [jnp.exp(_neg_abs(b - _hg_mid(b, m))) for b in bs]
    qts = [(q * e).astype(BF16) for q, e in zip(qs, es)]
    kts = [(k * e).astype(BF16) for k, e in zip(ks, es)]
    return es, qts, kts


def _hg_load(refs, r0, lb_v, lower_incl):
    q_ref, f_ref, i_ref = refs
    heads = []
    for h in range(HG_GROUP):
        sl = slice(h * HG_HEAD_DIM, (h + 1) * HG_HEAD_DIM)
        heads.append(_hg_chunk_inputs(q_ref[pl.ds(r0, BLK), sl], f_ref[pl.ds(r0, BLK), sl], lb_v[:, sl])
                     + (i_ref[pl.ds(r0, BLK), sl],))
    bs = [_dot_01_l(lower_incl, hd[4]) for hd in heads]
    return heads, bs


def _hgrn_fwd(proj, lb, name):
    s_len = proj.shape[0]
    nc = s_len // BLK
    gw = HG_GROUP * HG_HEAD_DIM
    n_groups = WIDTH // gw
    base = 4 * WIDTH // gw

    def body(q_ref, f_ref, i_ref, lb_ref, o_ref, mask_ref):
        _hg_masks(mask_ref)
        row = _iota2((BLK, BLK), 0)
        col = _iota2((BLK, BLK), 1)
        lower_incl = (col <= row).astype(BF16)
        lb_v = lb_ref[...]

        def chunk(ci, sts):
            r0 = pl.multiple_of(ci * BLK, BLK)
            heads, bs = _hg_load((q_ref, f_ref, i_ref), r0, lb_v, lower_incl)
            qs = [hd[0] for hd in heads]
            ks = [1.0 - hd[2] for hd in heads]
            vs = [hd[5] for hd in heads]
            vbs = [v.astype(BF16) for v in vs]
            b_ends = [b[BLK - 1:BLK, :] for b in bs]
            inters = [_dot_nt((q * jnp.exp(b)).astype(BF16), st.astype(BF16)) for q, b, st in zip(qs, bs, sts)]
            scs = [None] * HG_GROUP
            for v_idx, m in enumerate(HG_LEVELS):
                _, qts, kts = _hg_level_terms(qs, ks, bs, m)
                terms = [_dot_nt(qt, kt) for qt, kt in zip(qts, kts)]
                msk = mask_ref[v_idx]
                scs = [t * msk if sc is None else sc + t * msk for sc, t in zip(scs, terms)]
            intras = [_dot(sc.astype(BF16), vb) for sc, vb in zip(scs, vbs)]
            k_decs = [(k * jnp.exp(b_end - b)).astype(BF16) for k, b, b_end in zip(ks, bs, b_ends)]
            grown = [_dot_tn(vb, k_dec) for vb, k_dec in zip(vbs, k_decs)]
            for h in range(HG_GROUP):
                diag = jnp.sum(qs[h] * ks[h], axis=-1, keepdims=True)
                o_ref[pl.ds(r0, BLK), h * HG_HEAD_DIM:(h + 1) * HG_HEAD_DIM] = inters[h] + intras[h] + diag * vs[h]
            return tuple(st * jnp.exp(b_end) + g for st, b_end, g in zip(sts, b_ends, grown))

        lax.fori_loop(0, nc, chunk, (jnp.zeros((HG_HEAD_DIM, HG_HEAD_DIM), F32),) * HG_GROUP)

    col_spec = lambda off: pl.BlockSpec((s_len, gw), lambda h: (0, off + h))
    return _pcall(
        body, name=name, out_shape=jax.ShapeDtypeStruct((s_len, WIDTH), F32),
        grid=(n_groups,),
        in_specs=[col_spec(base), col_spec(base + n_groups), col_spec(base + 2 * n_groups),
                  pl.BlockSpec((1, gw), lambda h: (0, h))],
        out_specs=pl.BlockSpec((s_len, gw), lambda h: (0, h)),
        scratch_shapes=[pltpu.VMEM((len(HG_LEVELS), BLK, BLK), F32)],
        semantics=("arbitrary",))(proj, proj, proj, lb)


def _rms_heads(o_b, norm_w):
    n_parts, h_parts, r_parts = [], [], []
    for h in range(WIDTH // HG_HEAD_DIM):
        sl = slice(h * HG_HEAD_DIM, (h + 1) * HG_HEAD_DIM)
        o = o_b[:, sl]
        rstd = lax.rsqrt(jnp.mean(o * o, axis=-1, keepdims=True) + RMS_EPS)
        ohat = o * rstd
        h_parts.append(ohat)
        n_parts.append(ohat * norm_w[:, sl])
        r_parts.append(jnp.broadcast_to(rstd, o.shape))
    cat = lambda parts: jnp.concatenate(parts, axis=-1)
    return cat(n_parts), cat(h_parts), cat(r_parts)


def _shift_rows_down(halo, cur, k):
    tm = cur.shape[0]
    ext = jnp.concatenate([halo, cur], axis=0)
    return pltpu.roll(ext, k, 0)[8:8 + tm]


def _shift_rows_up(cur, halo, k):
    tm = cur.shape[0]
    ext = jnp.concatenate([cur, halo], axis=0)
    return pltpu.roll(ext, (tm + 8 - k) % (tm + 8), 0)[0:tm]


def _merge_fwd(x, proj, o_a, o_b, gate, norm_w, conv_w, wb, w_out, ln_g, ln_b, name):
    s_len, d = x.shape
    tm = min(256, s_len)
    hb = tm // 8

    def body(x_ref, oa_ref, za_ref, ob_ref, zb_ref, pre_ref, post_ref, u_ref, zc_ref, hpre_ref, hu_ref, g_ref,
             gate_ref, nw_ref, cw_ref, wb_ref, wo_ref, lg_ref, lbias_ref, xn_ref, mg_ref, yc_ref):
        i = pl.program_id(0)
        sa, _ = _silu_and_grad(za_ref[...])
        y_a = (oa_ref[...] * sa).astype(BF16)
        n_b, _, _ = _rms_heads(ob_ref[...], nw_ref[...])
        sb, _ = _silu_and_grad(zb_ref[...])
        y_b = (n_b * sb).astype(BF16)
        a = pre_ref[...] * u_ref[...]
        halo = jnp.where(i > 0, hpre_ref[...] * hu_ref[...], 0.0)
        cw = cw_ref[...]
        conv = cw[0:1] * _shift_rows_down(halo, a, 2) + cw[1:2] * _shift_rows_down(halo, a, 1) + cw[2:3] * a
        sc, _ = _silu_and_grad(zc_ref[...])
        y_c = (post_ref[...] * conv * sc).astype(BF16)
        merged = None
        for k, yk in enumerate((y_a, y_b, y_c)):
            yc_ref[:, k * WIDTH:(k + 1) * WIDTH] = yk
            term = _sigmoid(g_ref[:, k * d:(k + 1) * d]) * _dot(yk, wb_ref[k])
            merged = term if merged is None else merged + term
        mb = merged.astype(BF16)
        mg_ref[...] = mb
        y = _dot(mb, wo_ref[...])
        r = ALPHA * x_ref[...] + (1.0 + gate_ref[...]) * y
        rhat, _ = _standardize(r)
        xn_ref[...] = rhat * lg_ref[...] + lbias_ref[...]

    wcol = lambda cb: pl.BlockSpec((tm, WIDTH), lambda i: (i, cb))
    halo_spec = lambda cb: pl.BlockSpec((8, WIDTH), lambda i: (jnp.maximum(i * hb - 1, 0), cb))
    vec = lambda w: pl.BlockSpec((1, w), lambda i: (0, 0))
    return _pcall(
        body, name=name,
        out_shape=(jax.ShapeDtypeStruct((s_len, d), F32), jax.ShapeDtypeStruct((s_len, d), BF16),
                   jax.ShapeDtypeStruct((s_len, 3 * WIDTH), BF16)),
        grid=(s_len // tm,),
        in_specs=[pl.BlockSpec((tm, d), lambda i: (i, 0)),
                  wcol(0), wcol(3), wcol(0), wcol(7), wcol(8), wcol(9), wcol(10), wcol(11),
                  halo_spec(8), halo_spec(10),
                  pl.BlockSpec((tm, 3 * d), lambda i: (i, 2)),
                  vec(d), vec(WIDTH),
                  pl.BlockSpec((3, WIDTH), lambda i: (0, 0)),
                  pl.BlockSpec((3, WIDTH, d), lambda i: (0, 0, 0)),
                  pl.BlockSpec((d, d), lambda i: (0, 0)),
                  vec(d), vec(d)],
        out_specs=(pl.BlockSpec((tm, d), lambda i: (i, 0)), pl.BlockSpec((tm, d), lambda i: (i, 0)),
                   pl.BlockSpec((tm, 3 * WIDTH), lambda i: (i, 0))),
        semantics=("arbitrary",))(x, o_a, proj, o_b, proj, proj, proj, proj, proj, proj, proj, proj,
                                  gate, norm_w, conv_w, wb, w_out, ln_g, ln_b)


def _loss_fwd_bwd(y, target):
    s_len, d = y.shape
    tm = min(512, s_len)

    def body(y_ref, t_ref, loss_ref, dy_ref):
        @pl.when(pl.program_id(0) == 0)
        def _():
            loss_ref[...] = jnp.zeros_like(loss_ref)

        e = y_ref[...] - t_ref[...]
        dy_ref[...] = e * (1.0 / d)
        part = jnp.sum(jnp.sum(e * e, axis=-1, keepdims=True), axis=0, keepdims=True)
        loss_ref[...] += part * (0.5 / d)

    tile = pl.BlockSpec((tm, d), lambda i: (i, 0))
    return _pcall(body, name="loss", grid=(s_len // tm,),
                  out_shape=(jax.ShapeDtypeStruct((1, 1), F32), jax.ShapeDtypeStruct((s_len, d), F32)),
                  in_specs=[tile, tile],
                  out_specs=(pl.BlockSpec((1, 1), lambda i: (0, 0)), tile),
                  semantics=("arbitrary",))(y, target)


def _merge_bwd(dxn, x, merged, ycat, proj, gate, wb, w_out, ln_g, name):
    s_len, d = x.shape
    tm = min(256, s_len)
    dsh = d // NDEV
    n_tiles = s_len // tm

    def body(dxn_ref, x_ref, mg_ref, yc_ref, g_ref, gate_ref, wb_ref, wo_ref, lg_ref,
             dres_ref, dyc_ref, dg_ref, gwo_out, gwb_out, vec_ref, gwo_ref, gwb_ref):
        @pl.when(pl.program_id(0) == 0)
        def _():
            gwo_ref[...] = jnp.zeros_like(gwo_ref)
            gwb_ref[...] = jnp.zeros_like(gwb_ref)
            vec_ref[...] = jnp.zeros_like(vec_ref)

        mb = mg_ref[...]
        one_gate = 1.0 + gate_ref[...]
        y = _dot(mb, wo_ref[...])
        r = ALPHA * x_ref[...] + one_gate * y
        rhat, rstd = _standardize(r)
        dxn = dxn_ref[...]
        dr = _standardize_bwd(rhat, rstd, dxn * lg_ref[...])
        vec_ref[0:1, :] += jnp.sum(dxn * rhat, axis=0, keepdims=True)
        vec_ref[1:2, :] += jnp.sum(dxn, axis=0, keepdims=True)
        vec_ref[2:3, :] += jnp.sum(dr * y, axis=0, keepdims=True)
        dres_ref[...] = ALPHA * dr
        dy = (one_gate * dr).astype(BF16)
        gwo_ref[...] += _dot_tn(mb, dy)
        dmerged = _dot_nt(dy, wo_ref[...])
        for k in range(3):
            yk = yc_ref[:, k * WIDTH:(k + 1) * WIDTH]
            sg = _sigmoid(g_ref[:, k * d:(k + 1) * d])
            pk = _dot(yk, wb_ref[k])
            dg_ref[:, k * d:(k + 1) * d] = (dmerged * pk * sg * (1.0 - sg)).astype(BF16)
            dpk = (dmerged * sg).astype(BF16)
            dyc_ref[:, k * WIDTH:(k + 1) * WIDTH] = _dot_nt(dpk, wb_ref[k])
            gwb_ref[k] += _dot_tn(yk, dpk)

        @pl.when(pl.program_id(0) == n_tiles - 1)
        def _():
            for o in range(NDEV):
                gwo_out[o] = gwo_ref[o * dsh:(o + 1) * dsh, :].astype(BF16)
                for k in range(3):
                    gwb_out[o, k] = gwb_ref[k, :, o * dsh:(o + 1) * dsh].astype(BF16)

    tile = lambda w: pl.BlockSpec((tm, w), lambda i: (i, 0))
    vec = pl.BlockSpec((1, d), lambda i: (0, 0))
    return _pcall(
        body, name=name,
        out_shape=(jax.ShapeDtypeStruct((s_len, d), F32), jax.ShapeDtypeStruct((s_len, 3 * WIDTH), F32),
                   jax.ShapeDtypeStruct(proj.shape, BF16), jax.ShapeDtypeStruct((NDEV, dsh, d), BF16),
                   jax.ShapeDtypeStruct((NDEV, 3, WIDTH, dsh), BF16), jax.ShapeDtypeStruct((8, d), F32)),
        grid=(n_tiles,),
        in_specs=[tile(d), tile(d), tile(d), tile(3 * WIDTH),
                  pl.BlockSpec((tm, 3 * d), lambda i: (i, 2)),
                  vec, pl.BlockSpec((3, WIDTH, d), lambda i: (0, 0, 0)),
                  pl.BlockSpec((d, d), lambda i: (0, 0)), vec],
        out_specs=(tile(d), tile(3 * WIDTH), pl.BlockSpec((tm, 3 * d), lambda i: (i, 2)),
                   pl.BlockSpec((NDEV, dsh, d), lambda i: (0, 0, 0)),
                   pl.BlockSpec((NDEV, 3, WIDTH, dsh), lambda i: (0, 0, 0, 0)),
                   pl.BlockSpec((8, d), lambda i: (0, 0))),
        scratch_shapes=[pltpu.VMEM((d, d), F32), pltpu.VMEM((3, WIDTH, d), F32)],
        semantics=("arbitrary",))(dxn, x, merged, ycat, proj, gate, wb, w_out, ln_g)


def _branch_bwd(dycat, proj, o_a, o_b, norm_w, conv_w, dproj, name):
    s_len = proj.shape[0]
    tm = min(256, s_len)
    hb = tm // 8
    n_tiles = s_len // tm

    def body(dya_ref, dyb_ref, dyc_ref, oa_ref, za_ref, ob_ref, zb_ref, pre_ref, post_ref, u_ref, zc_ref,
             hpre_ref, hu_ref, ndyc_ref, npost_ref, nzc_ref, nw_ref, cw_ref, dproj_in,
             dproj_ref, doa_ref, dob_ref, vec_ref, dza_scr, dzb_scr, dc_scr, sems):
        del dproj_in
        i = pl.program_id(0)

        @pl.when(i == 0)
        def _():
            vec_ref[...] = jnp.zeros_like(vec_ref)

        sa, dsa = _silu_and_grad(za_ref[...])
        dya = dya_ref[...]
        doa_ref[...] = dya * sa
        dza_scr[...] = (dya * oa_ref[...] * dsa).astype(BF16)
        nw = nw_ref[...]
        n_b, ohat, rstd = _rms_heads(ob_ref[...], nw)
        sb, dsb = _silu_and_grad(zb_ref[...])
        dyb = dyb_ref[...]
        dzb_scr[...] = (dyb * n_b * dsb).astype(BF16)
        dn = dyb * sb
        vec_ref[0:1, :] += jnp.sum(dn * ohat, axis=0, keepdims=True)
        dnw = dn * nw
        parts = []
        for h in range(WIDTH // HG_HEAD_DIM):
            sl = slice(h * HG_HEAD_DIM, (h + 1) * HG_HEAD_DIM)
            m2 = jnp.mean(dnw[:, sl] * ohat[:, sl], axis=-1, keepdims=True)
            parts.append(rstd[:, sl] * (dnw[:, sl] - ohat[:, sl] * m2))
        dob_ref[...] = jnp.concatenate(parts, axis=-1)
        cw = cw_ref[...]
        pre, u, post = pre_ref[...], u_ref[...], post_ref[...]
        a = pre * u
        halo = jnp.where(i > 0, hpre_ref[...] * hu_ref[...], 0.0)
        a1 = _shift_rows_down(halo, a, 1)
        a2 = _shift_rows_down(halo, a, 2)
        conv = cw[0:1] * a2 + cw[1:2] * a1 + cw[2:3] * a
        sc, dsc = _silu_and_grad(zc_ref[...])
        dyc = dyc_ref[...]
        dconv = dyc * post * sc
        nsc, _ = _silu_and_grad(nzc_ref[...])
        nxt = jnp.where(i < n_tiles - 1, ndyc_ref[...] * npost_ref[...] * nsc, 0.0)
        da = cw[2:3] * dconv + cw[1:2] * _shift_rows_up(dconv, nxt, 1) + cw[0:1] * _shift_rows_up(dconv, nxt, 2)
        dc_scr[:, 0 * WIDTH:1 * WIDTH] = (da * u).astype(BF16)
        dc_scr[:, 1 * WIDTH:2 * WIDTH] = (dyc * conv * sc).astype(BF16)
        dc_scr[:, 2 * WIDTH:3 * WIDTH] = (da * pre).astype(BF16)
        dc_scr[:, 3 * WIDTH:4 * WIDTH] = (dyc * post * conv * dsc).astype(BF16)
        vec_ref[1:2, :] += jnp.sum(dconv * a2, axis=0, keepdims=True)
        vec_ref[2:3, :] += jnp.sum(dconv * a1, axis=0, keepdims=True)
        vec_ref[3:4, :] += jnp.sum(dconv * a, axis=0, keepdims=True)
        rows = pl.ds(pl.multiple_of(i * tm, tm), tm)
        copies = [pltpu.make_async_copy(dza_scr, dproj_ref.at[rows, 3 * WIDTH:4 * WIDTH], sems.at[0]),
                  pltpu.make_async_copy(dzb_scr, dproj_ref.at[rows, 7 * WIDTH:8 * WIDTH], sems.at[1]),
                  pltpu.make_async_copy(dc_scr, dproj_ref.at[rows, 8 * WIDTH:12 * WIDTH], sems.at[2])]
        for cp in copies:
            cp.start()
        for cp in copies:
            cp.wait()

    wcol = lambda cb: pl.BlockSpec((tm, WIDTH), lambda i: (i, cb))
    prev = lambda cb: pl.BlockSpec((8, WIDTH), lambda i: (jnp.maximum(i * hb - 1, 0), cb))
    nxt = lambda cb: pl.BlockSpec((8, WIDTH), lambda i: (jnp.minimum((i + 1) * hb, s_len // 8 - 1), cb))
    anyspec = pl.BlockSpec(memory_space=pl.ANY)
    out = jax.ShapeDtypeStruct((s_len, WIDTH), F32)
    return _pcall(
        body, name=name,
        out_shape=(jax.ShapeDtypeStruct(dproj.shape, dproj.dtype), out, out, jax.ShapeDtypeStruct((8, WIDTH), F32)),
        grid=(n_tiles,),
        in_specs=[wcol(0), wcol(1), wcol(2), wcol(0), wcol(3), wcol(0), wcol(7), wcol(8), wcol(9), wcol(10), wcol(11),
                  prev(8), prev(10), nxt(2), nxt(9), nxt(11),
                  pl.BlockSpec((1, WIDTH), lambda i: (0, 0)), pl.BlockSpec((3, WIDTH), lambda i: (0, 0)), anyspec],
        out_specs=(anyspec, wcol(0), wcol(0), pl.BlockSpec((8, WIDTH), lambda i: (0, 0))),
        scratch_shapes=[pltpu.VMEM((tm, WIDTH), BF16), pltpu.VMEM((tm, WIDTH), BF16),
                        pltpu.VMEM((tm, 4 * WIDTH), BF16), pltpu.SemaphoreType.DMA((3,))],
        aliases={18: 0},
        semantics=("arbitrary",))(dycat, dycat, dycat, o_a, proj, o_b, proj, proj, proj, proj, proj,
                                  proj, proj, dycat, proj, proj, norm_w, conv_w, dproj)


def _sb_bwd(proj, do_a, totals, dproj, name):
    s_len = proj.shape[0]
    n_pairs = WIDTH // BLK
    scale = SB_HEAD_DIM ** -0.5
    qr = min(SB_Q_ROWS, s_len)
    gb = SB_K_BLOCKS
    kw = gb * BLK
    nq = s_len // qr
    assert qr == kw

    def body(q_ref, k_ref, v_ref, do_ref, tot_ref, dproj_in, dproj_ref, dq_ref, dk_ref, dv_ref, out_scr, sems):
        del dproj_in
        lane = _iota2((1, BLK), 1)
        row = _iota2((BLK, BLK), 0)
        col = _iota2((BLK, BLK), 1)
        ones = jnp.ones((BLK, BLK), BF16)
        twice = lambda m: jnp.concatenate([m, m], axis=0)
        before_and_sum = twice(jnp.concatenate([(row < col).astype(BF16), ones], axis=1))
        upto_and_sum = twice(jnp.concatenate([(row <= col).astype(BF16), ones], axis=1))
        strict = _iota2((qr, kw), 1) < _iota2((qr, kw), 0)
        head_lanes = [(lane // SB_HEAD_DIM) == hh for hh in range(2)]
        dk_ref[...] = jnp.zeros_like(dk_ref)
        dv_ref[...] = jnp.zeros_like(dv_ref)

        def scores(gi, qms, masked):
            c0 = pl.multiple_of(gi * kw, kw)
            kb = k_ref[pl.ds(c0, kw), :].astype(BF16)
            z2s = [_dot_nt(qms[hh], kb) for hh in range(2)]
            if masked:
                z2s = [jnp.where(strict, z2, MASKED_SCORE) for z2 in z2s]
            return tuple(z2s)

        def process(gi, z2s, qms, doms, totals_i, carry):
            c0 = pl.multiple_of(gi * kw, kw)
            kf = k_ref[pl.ds(c0, kw), :]
            vf = v_ref[pl.ds(c0, kw), :]
            kms = [jnp.where(head_lanes[hh], kf, 0.0).astype(BF16) for hh in range(2)]
            vms = [jnp.where(head_lanes[hh], vf, 0.0).astype(BF16) for hh in range(2)]
            das = [_dot_nt(doms[hh], vms[hh]) for hh in range(2)]
            halves = [_softplus2_parts(z2) for z2 in z2s]
            terms = [[_split2_lanes(sp2[:, b * BLK:(b + 1) * BLK]) for b in range(gb)] for sp2, _ in halves]
            sums = [[_dot(t, before_and_sum) for t in head_terms] for head_terms in terms]
            weights, gmats, l_befores = [], [], []
            for hh in range(2):
                l_before = carry[3 * hh + 1]
                parts = []
                for b in range(gb):
                    parts.append(totals_i[hh] - l_before - sums[hh][b][:, :BLK])
                    l_before = l_before + sums[hh][b][:, BLK:]
                a = jnp.exp2(z2s[hh] - jnp.concatenate(parts, axis=1))
                weights.append(a.astype(BF16))
                gmats.append(a * das[hh])
                l_befores.append(l_before)
            terms = [[_split2_lanes(g[:, b * BLK:(b + 1) * BLK]) for b in range(gb)] for g in gmats]
            sums = [[_dot(t, upto_and_sum) for t in head_terms] for head_terms in terms]
            dzs, g_befores = [], []
            for hh in range(2):
                g_before = carry[3 * hh + 2]
                parts = []
                for b in range(gb):
                    parts.append(g_before + sums[hh][b][:, :BLK])
                    g_before = g_before + sums[hh][b][:, BLK:]
                dzs.append((gmats[hh] - halves[hh][1] * jnp.concatenate(parts, axis=1)).astype(BF16))
                g_befores.append(g_before)
            dk_t = _dot_tn(jnp.concatenate(qms, axis=0), jnp.concatenate(dzs, axis=0))
            dv_t = _dot_tn(jnp.concatenate(doms, axis=0), jnp.concatenate(weights, axis=0))
            dqs = [_dot(dzs[hh], kms[hh]) for hh in range(2)]
            dk_ref[:, pl.ds(c0, kw)] += dk_t * (1.0 / LOG2E)
            dv_ref[:, pl.ds(c0, kw)] += dv_t
            return (carry[0] + dqs[0], l_befores[0], g_befores[0], carry[3] + dqs[1], l_befores[1], g_befores[1])

        def queries(i):
            qf = q_ref[pl.ds(pl.multiple_of(i * qr, qr), qr), :] * (scale * LOG2E)
            return [jnp.where(head_lanes[hh], qf, 0.0).astype(BF16) for hh in range(2)]

        def qtile(i, first_scores):
            r0 = pl.multiple_of(i * qr, qr)
            qms = queries(i)
            dof = do_ref[pl.ds(r0, qr), :]
            doms = [jnp.where(head_lanes[hh], dof, 0.0).astype(BF16) for hh in range(2)]
            totals_i = [tot_ref[hh, pl.ds(r0, qr), :] for hh in range(2)]
            zero = jnp.zeros((qr, BLK), F32)

            def step(gi, state):
                return scores(gi + 1, qms, False) + process(gi, state[:2], qms, doms, totals_i, state[2:])

            def before_diagonal(state):
                return scores(i, qms, True) + process(i - 1, state[:2], qms, doms, totals_i, state[2:])

            state = lax.fori_loop(0, i - 1, step, first_scores + (zero,) * 6)
            state = lax.cond(i > 0, before_diagonal, lambda st: st, state)
            nxt = jnp.minimum(i + 1, nq - 1)
            next_scores = scores(0, queries(nxt), False)
            carry = process(i, state[:2], qms, doms, totals_i, state[2:])
            dq_ref[pl.ds(r0, qr), :] = (carry[0] + carry[3]) * scale
            return next_scores

        lax.fori_loop(0, nq, qtile, scores(0, queries(0), True))
        pair = pl.program_id(0)
        copies = []
        for t, value in enumerate((dq_ref[...], dk_ref[...].T, dv_ref[...].T)):
            out_scr[t] = value.astype(BF16)
            col = pl.multiple_of((t * n_pairs + pair) * BLK, BLK)
            copies.append(pltpu.make_async_copy(out_scr.at[t], dproj_ref.at[:, pl.ds(col, BLK)], sems.at[t]))
            copies[-1].start()
        for cp in copies:
            cp.wait()

    col_spec = lambda off: pl.BlockSpec((s_len, BLK), lambda p: (0, off + p))
    anyspec = pl.BlockSpec(memory_space=pl.ANY)
    return _pcall(
        body, name=name, out_shape=jax.ShapeDtypeStruct(dproj.shape, dproj.dtype), grid=(n_pairs,),
        in_specs=[col_spec(0), col_spec(n_pairs), col_spec(2 * n_pairs), col_spec(0),
                  pl.BlockSpec((2, s_len, BLK), lambda p: (p, 0, 0)), anyspec],
        out_specs=anyspec,
        scratch_shapes=[pltpu.VMEM((s_len, BLK), F32), pltpu.VMEM((BLK, s_len), F32), pltpu.VMEM((BLK, s_len), F32),
                        pltpu.VMEM((3, s_len, BLK), BF16), pltpu.SemaphoreType.DMA((3,))],
        aliases={5: 0},
        semantics=("arbitrary",))(proj, proj, proj, do_a, totals, dproj)


def _hgrn_bwd(proj, do_b, lb, dproj, name):
    s_len = proj.shape[0]
    nc = s_len // BLK
    gw = HG_GROUP * HG_HEAD_DIM
    n_groups = WIDTH // gw
    base = 4 * WIDTH // gw
    heads_of = range(HG_GROUP)

    def body(q_ref, f_ref, i_ref, do_ref, lb_ref, dproj_in, dproj_ref, dlb_ref, mask_ref, st_ref, out_scr, sems):
        del dproj_in
        _hg_masks(mask_ref)
        row = _iota2((BLK, BLK), 0)
        col = _iota2((BLK, BLK), 1)
        lower_incl = (col <= row).astype(BF16)
        upper_incl = (col >= row).astype(BF16)
        lb_v = lb_ref[...]
        refs = (q_ref, f_ref, i_ref)

        def fwd_chunk(ci, sts):
            for h in heads_of:
                st_ref[ci, h] = sts[h]
            heads, bs = _hg_load(refs, pl.multiple_of(ci * BLK, BLK), lb_v, lower_incl)
            b_ends = [b[BLK - 1:BLK, :] for b in bs]
            k_decs = [((1.0 - hd[2]) * jnp.exp(b_end - b)).astype(BF16) for hd, b, b_end in zip(heads, bs, b_ends)]
            grown = [_dot_tn(hd[5].astype(BF16), k_dec) for hd, k_dec in zip(heads, k_decs)]
            return tuple(st * jnp.exp(b_end) + g for st, b_end, g in zip(sts, b_ends, grown))

        zero_state = (jnp.zeros((HG_HEAD_DIM, HG_HEAD_DIM), F32),) * HG_GROUP
        lax.fori_loop(0, nc, fwd_chunk, zero_state)

        def bwd_chunk(cc, carry):
            dsts, suffixes, dlbs = carry
            ci = nc - 1 - cc
            r0 = pl.multiple_of(ci * BLK, BLK)
            heads, bs = _hg_load(refs, r0, lb_v, lower_incl)
            qs = [hd[0] for hd in heads]
            fs = [hd[2] for hd in heads]
            ks = [1.0 - f for f in fs]
            vs = [hd[5] for hd in heads]
            vbs = [v.astype(BF16) for v in vs]
            dos = [do_ref[pl.ds(r0, BLK), h * HG_HEAD_DIM:(h + 1) * HG_HEAD_DIM] for h in heads_of]
            dobs = [do.astype(BF16) for do in dos]
            b_ends = [b[BLK - 1:BLK, :] for b in bs]
            e_qs = [jnp.exp(b) for b in bs]
            e_ks = [jnp.exp(b_end - b) for b, b_end in zip(bs, b_ends)]
            qes = [(q * e).astype(BF16) for q, e in zip(qs, e_qs)]
            khs = [(k * e).astype(BF16) for k, e in zip(ks, e_ks)]
            st_terms = [_split2_lanes(st_ref[ci, h]) for h in heads_of]
            ds_terms = [_split2_lanes(dst) for dst in dsts]
            dqes = [_dot(dob, t[:, :HG_HEAD_DIM]) + _dot(dob, t[:, HG_HEAD_DIM:]) for dob, t in zip(dobs, st_terms)]
            dkhs = [_dot(vb, t[:, :HG_HEAD_DIM]) + _dot(vb, t[:, HG_HEAD_DIM:]) for vb, t in zip(vbs, ds_terms)]
            dvs = [_dot_nt(kh, t[:, :HG_HEAD_DIM]) for kh, t in zip(khs, ds_terms)]
            grown = [_dot_tn(dob, qe) for dob, qe in zip(dobs, qes)]
            das = [_dot_nt(dob, vb) for dob, vb in zip(dobs, vbs)]
            dqs = [e * dqe for e, dqe in zip(e_qs, dqes)]
            dks = [e * dkh for e, dkh in zip(e_ks, dkhs)]
            dlogs = [qe.astype(F32) * dqe - kh.astype(F32) * dkh for qe, dqe, kh, dkh in zip(qes, dqes, khs, dkhs)]
            scs = [None] * HG_GROUP
            for v_idx, m in enumerate(HG_LEVELS):
                es, qms, kms = _hg_level_terms(qs, ks, bs, m)
                msk = mask_ref[v_idx]
                terms = [_dot_nt(qm, km) for qm, km in zip(qms, kms)]
                pms = [(da * msk).astype(BF16) for da in das]
                dqms = [_dot(pm, km) for pm, km in zip(pms, kms)]
                dkms = [_dot_tn(pm, qm) for pm, qm in zip(pms, qms)]
                scs = [t * msk if sc is None else sc + t * msk for sc, t in zip(scs, terms)]
                dqs = [dq + dqm * e for dq, dqm, e in zip(dqs, dqms, es)]
                dks = [dk + dkm * e for dk, dkm, e in zip(dks, dkms, es)]
                dlogs = [dl + (qm.astype(F32) * dqm - km.astype(F32) * dkm)
                         for dl, qm, dqm, km, dkm in zip(dlogs, qms, dqms, kms, dkms)]
            intras = [_dot_tn(sc.astype(BF16), dob) for sc, dob in zip(scs, dobs)]
            dgs = [_dot_01_l(upper_incl, dl) + sfx for dl, sfx in zip(dlogs, suffixes)]
            new_dlbs = []
            for h in heads_of:
                q, dq_fac, f, sig = heads[h][0], heads[h][1], heads[h][2], heads[h][3]
                a_diag = jnp.sum(dos[h] * vs[h], axis=-1, keepdims=True)
                s_diag = jnp.sum(q * ks[h], axis=-1, keepdims=True)
                dq = dqs[h] + a_diag * ks[h]
                dk = dks[h] + a_diag * q
                dv = dvs[h] + intras[h] + s_diag * dos[h]
                dfull = dgs[h] / f - dk
                sl = slice(h * HG_HEAD_DIM, (h + 1) * HG_HEAD_DIM)
                out_scr[0, pl.ds(r0, BLK), sl] = (dq * dq_fac).astype(BF16)
                out_scr[1, pl.ds(r0, BLK), sl] = (dfull * (1.0 - lb_v[:, sl]) * sig * (1.0 - sig)).astype(BF16)
                out_scr[2, pl.ds(r0, BLK), sl] = dv.astype(BF16)
                new_dlbs.append(dlbs[h] + jnp.sum(dfull * (1.0 - sig), axis=0, keepdims=True))
            new_dsts = tuple(dst * jnp.exp(b_end) + g for dst, b_end, g in zip(dsts, b_ends, grown))
            return new_dsts, tuple(dg[0:1, :] for dg in dgs), tuple(new_dlbs)

        zero_row = (jnp.zeros((1, HG_HEAD_DIM), F32),) * HG_GROUP
        _, _, dlbs = lax.fori_loop(0, nc, bwd_chunk, (zero_state, zero_row, zero_row))
        dlb_ref[...] = jnp.broadcast_to(jnp.concatenate(dlbs, axis=1), dlb_ref.shape)
        group = pl.program_id(0)
        copies = []
        for t in range(3):
            col = pl.multiple_of((base + t * n_groups + group) * gw, gw)
            copies.append(pltpu.make_async_copy(out_scr.at[t], dproj_ref.at[:, pl.ds(col, gw)], sems.at[t]))
            copies[-1].start()
        for cp in copies:
            cp.wait()

    col_spec = lambda off: pl.BlockSpec((s_len, gw), lambda h: (0, off + h))
    anyspec = pl.BlockSpec(memory_space=pl.ANY)
    return _pcall(
        body, name=name,
        out_shape=(jax.ShapeDtypeStruct(dproj.shape, dproj.dtype), jax.ShapeDtypeStruct((8, WIDTH), F32)),
        grid=(n_groups,),
        in_specs=[col_spec(base), col_spec(base + n_groups), col_spec(base + 2 * n_groups), col_spec(0),
                  pl.BlockSpec((1, gw), lambda h: (0, h)), anyspec],
        out_specs=(anyspec, pl.BlockSpec((8, gw), lambda h: (0, h))),
        scratch_shapes=[pltpu.VMEM((len(HG_LEVELS), BLK, BLK), F32),
                        pltpu.VMEM((nc, HG_GROUP, HG_HEAD_DIM, HG_HEAD_DIM), F32),
                        pltpu.VMEM((3, s_len, gw), BF16), pltpu.SemaphoreType.DMA((3,))],
        aliases={5: 0},
        semantics=("arbitrary",))(proj, proj, proj, do_b, lb, dproj)


def _dh_matmul(dproj, w_full, after, name):
    s_len, n = dproj.shape
    d = w_full.shape[0]
    tm = min(1024, s_len)
    tk = 4608

    def body(dp_ref, w_ref, after_ref, dh_ref):
        del after_ref
        part = _dot_nt(dp_ref[...], w_ref[...])

        @pl.when(pl.program_id(1) == 0)
        def _():
            dh_ref[...] = part

        @pl.when(pl.program_id(1) > 0)
        def _():
            dh_ref[...] += part

    return _pcall(
        body, name=name, out_shape=jax.ShapeDtypeStruct((s_len, d), F32),
        grid=(s_len // tm, n // tk),
        in_specs=[pl.BlockSpec((tm, tk), lambda i, k: (i, k)), pl.BlockSpec((d, tk), lambda i, k: (0, k)),
                  pl.BlockSpec(memory_space=pl.ANY)],
        out_specs=pl.BlockSpec((tm, d), lambda i, k: (i, 0)),
        semantics=("arbitrary", "arbitrary"))(dproj, w_full, after)


def _gw_matmul(h_t, dproj, name):
    d, s_len = h_t.shape
    n = dproj.shape[1]
    tn = 2304

    def body(ht_ref, dp_ref, gw_ref):
        gw_ref[...] = _dot(ht_ref[...], dp_ref[...]).astype(BF16)

    return _pcall(
        body, name=name, out_shape=jax.ShapeDtypeStruct((d, n), BF16),
        grid=(n // tn,),
        in_specs=[pl.BlockSpec((d, s_len), lambda j: (0, 0)), pl.BlockSpec((s_len, tn), lambda j: (0, j))],
        out_specs=pl.BlockSpec((d, tn), lambda j: (0, j)),
        semantics=("arbitrary",))(h_t, dproj)


def _ln_bwd(dh, x, scale, dres, name):
    s_len, d = x.shape
    tm = min(512, s_len)

    def body(dh_ref, x_ref, sc_ref, dres_ref, dx_ref, vec_ref):
        @pl.when(pl.program_id(0) == 0)
        def _():
            vec_ref[...] = jnp.zeros_like(vec_ref)

        dh = dh_ref[...]
        xs, rstd = _standardize(x_ref[...])
        vec_ref[0:1, :] += jnp.sum(dh, axis=0, keepdims=True)
        vec_ref[1:2, :] += jnp.sum(dh * xs, axis=0, keepdims=True)
        dx_ref[...] = _standardize_bwd(xs, rstd, dh * (1.0 + sc_ref[...])) + dres_ref[...]

    tile = pl.BlockSpec((tm, d), lambda i: (i, 0))
    return _pcall(body, name=name, grid=(s_len // tm,),
                  out_shape=(jax.ShapeDtypeStruct((s_len, d), F32), jax.ShapeDtypeStruct((8, d), F32)),
                  in_specs=[tile, tile, pl.BlockSpec((1, d), lambda i: (0, 0)), tile],
                  out_specs=(tile, pl.BlockSpec((8, d), lambda i: (0, 0))),
                  semantics=("arbitrary",))(dh, x, scale, dres)


def _wmod_grad(c_t, dmod):
    d = c_t.shape[0]
    n_layers, _, cm = dmod.shape

    def body(c_ref, dm_ref, o_ref):
        for l in range(n_layers):
            acc = None
            for b in range(NDEV):
                term = c_ref[:, b:b + 1] * dm_ref[l, b:b + 1, :]
                acc = term if acc is None else acc + term
            o_ref[l] = acc

    return _pcall(body, name="wmod_grad", out_shape=jax.ShapeDtypeStruct((n_layers, d, cm), F32))(c_t, dmod)


def _sum_adamw(parts, w, m, v, name, first_row=0, into=None, after=None):
    n_src, range_rows, cols = parts.shape
    rows = w.shape[0]
    tr = range_rows
    for cand in (512, 256, 128, 64, 32, 16, 8):
        if range_rows % cand == 0 and cand * cols * 4 <= (2 << 20):
            tr = cand
            break
    first_tile = first_row // tr
    assert first_row % tr == 0
    n_extra = (0 if into is None else 4) + (0 if after is None else 1)

    def body(p_ref, w_ref, m_ref, v_ref, *rest):
        g_ref, d_ref, nm_ref, nv_ref = rest[n_extra:]
        g = p_ref[0].astype(F32)
        for s in range(1, n_src):
            g = g + p_ref[s].astype(F32)
        g_ref[...] = g
        d_ref[...], nm_ref[...], nv_ref[...] = _adamw_step(g, w_ref[...], m_ref[...], v_ref[...])

    tile = pl.BlockSpec((tr, cols), lambda i: (i + first_tile, 0))
    anyspec = pl.BlockSpec(memory_space=pl.ANY)
    out = jax.ShapeDtypeStruct((rows, cols), F32)
    extra = ([] if into is None else list(into)) + ([] if after is None else [after])
    aliases = {} if into is None else {4 + k: k for k in range(4)}
    return _pcall(body, name=name, grid=(range_rows // tr,), out_shape=(out,) * 4,
                  in_specs=[pl.BlockSpec((n_src, tr, cols), lambda i: (0, i, 0)), tile, tile, tile]
                  + [anyspec] * len(extra),
                  out_specs=(tile,) * 4, aliases=aliases, semantics=("arbitrary",))(parts, w, m, v, *extra)


def _adamw_step(g, w, m, v):
    nm = ADAM_B1 * m + (1.0 - ADAM_B1) * g
    nv = ADAM_B2 * v + (1.0 - ADAM_B2) * (g * g)
    m_hat = nm / (1.0 - ADAM_B1 ** ADAM_STEP)
    v_hat = nv / (1.0 - ADAM_B2 ** ADAM_STEP)
    return -ADAM_LR * (m_hat / (jnp.sqrt(v_hat) + ADAM_EPS) + ADAM_WD * w), nm, nv


def _adamw_small(gs, ws, ms, vs):
    n = len(gs)

    def body(*refs):
        for p in range(n):
            results = _adamw_step(*(refs[k * n + p][...] for k in range(4)))
            for k in range(3):
                refs[(4 + k) * n + p][...] = results[k]

    shapes = [jax.ShapeDtypeStruct(w.shape, F32) for w in ws]
    outs = _pcall(body, name="adamw_small", out_shape=shapes * 3)(*gs, *ws, *ms, *vs)
    return [(outs[p], outs[n + p], outs[2 * n + p]) for p in range(n)]


def _sum_parts(parts, name):
    n_src = parts.shape[0]

    def body(p_ref, o_ref):
        acc = p_ref[0]
        for s in range(1, n_src):
            acc = acc + p_ref[s]
        o_ref[...] = acc

    return _pcall(body, name=name, out_shape=jax.ShapeDtypeStruct(parts.shape[1:], F32))(parts)


def _pair_sum(gw, stage, me, name):
    d = gw.shape[0]
    n_slots, _, shard = stage.shape

    def body(me_ref, g_ref, s_ref, own_ref, o_ref):
        del me_ref
        total = (g_ref[...].astype(F32) + s_ref[0].astype(F32)).astype(BF16)
        o_ref[0] = total

        @pl.when(pl.program_id(0) == 0)
        def _():
            own_ref[0] = total

    slot = pl.BlockSpec((1, d, shard), lambda jj, me_ref: (jj, 0, 0))
    out = jax.ShapeDtypeStruct(stage.shape, BF16)
    return pl.pallas_call(
        body, name=name, out_shape=(out, out),
        grid_spec=pltpu.PrefetchScalarGridSpec(
            num_scalar_prefetch=1, grid=(n_slots,),
            in_specs=[pl.BlockSpec((d, shard), lambda jj, me_ref: (0, me_ref[0] ^ (2 * jj))), slot],
            out_specs=(pl.BlockSpec((1, d, shard), lambda jj, me_ref: (0, 0, 0)), slot)),
        compiler_params=pltpu.CompilerParams(dimension_semantics=("arbitrary",), vmem_limit_bytes=VMEM_LIMIT),
        interpret=False)(me.reshape(1).astype(jnp.int32), gw, stage)


def _lower_bound_table(lower_bounds):
    p = jax.nn.softmax(lower_bounds.astype(F32), axis=0)
    return jnp.cumsum(p, axis=0) - p[0:1]


def _pad_rows(v, width):
    n = v.shape[0]
    rows = -(-n // width)
    rows = -(-rows // 8) * 8
    return jnp.pad(v, (0, rows * width - n)).reshape(rows, width)


def kernel(x, c, w_mod, b_mod, w_in, conv_w, hgrn_norm_w, lower_bounds, w_branch, w_out, ln_g, ln_b, loss_target, m_w_mod, m_b_mod, m_w_in, m_conv_w, m_hgrn_norm_w, m_lower_bounds, m_w_branch, m_w_out, m_ln_g, m_ln_b, v_w_mod, v_b_mod, v_w_in, v_conv_w, v_hgrn_norm_w, v_lower_bounds, v_w_branch, v_w_out, v_ln_g, v_ln_b):
    n_layers = N_LAYERS
    s_len, d = x.shape[1], x.shape[2]
    n_cols = w_in.shape[2] * NDEV
    cw_cols = conv_w.shape[2]
    cm = w_mod.shape[2]
    me = _my_index()
    x0 = x[0]
    target = loss_target[0]

    small = _pad_rows(jnp.concatenate([c.reshape(-1), conv_w.reshape(-1)]), BLK)
    small_all = _all_gather_small("gather_c_conv", small).reshape(NDEV, -1)
    c_all = small_all[:, :d]
    conv_full = small_all[:, d:d + n_layers * 3 * cw_cols].reshape(NDEV, n_layers, 3, cw_cols)
    conv_full = conv_full.transpose(1, 2, 0, 3).reshape(n_layers, 3, WIDTH)

    b_mod_mine = lax.dynamic_slice_in_dim(b_mod, me * cm, cm, axis=1).reshape(n_layers, 1, cm)
    mod_cols = _mod_fwd(c_all, w_mod, b_mod_mine)
    mod_all = _all_gather_small("gather_mod", mod_cols.reshape(n_layers * NDEV, cm))
    mod_all = mod_all.reshape(NDEV, n_layers, NDEV, cm)
    mod_mine = lax.dynamic_index_in_dim(mod_all, me, axis=2, keepdims=False)
    mod_mine = mod_mine.transpose(1, 0, 2).reshape(n_layers, 3, 1, d)

    shard = w_in.shape[2]
    dsh = d // NDEV
    w_in_b, w_branch_b, w_out_b = w_in.astype(BF16), w_branch.astype(BF16), w_out.astype(BF16)
    window = lambda ref, dev: ref.at[:, pl.ds(pl.multiple_of(dev * shard, BLK), shard)]

    def two_step_sends(places):
        chips, sibling = [], []
        for k in (1, 2, 4, 6):
            for a, place in enumerate(places):
                chips.append((k, lambda ins, lands, me, a=a: ins[a],
                              lambda lands, me, a=a, place=place: place(lands[a], me),
                              lambda lands, me, a=a, k=k, place=place: place(lands[a], me ^ k)))
        for j in (2, 4, 6):
            for a, place in enumerate(places):
                sibling.append((1, lambda ins, lands, me, a=a, j=j, place=place: place(lands[a], me ^ j),
                                lambda lands, me, a=a, j=j, place=place: place(lands[a], me ^ j),
                                lambda lands, me, a=a, j=j, place=place: place(lands[a], me ^ 1 ^ j)))
        return chips, sibling

    in_sends = two_step_sends([window])
    rest_sends = two_step_sends([_slot, _slot])
    layer_sends = two_step_sends([window, _slot, _slot])

    def in_land(l):
        return _place_own_window(f"place_w_in_{l}", (d, n_cols), w_in_b[l], me)

    def rest_lands(l):
        return [_place_own((NDEV, 3, WIDTH, dsh), BF16, w_branch_b[l][None], (me, 0, 0, 0)),
                _place_own((NDEV, dsh, d), BF16, w_out_b[l][None], (me, 0, 0))]

    def gather_start(name, shards, lands, sends, after):
        return _exchange_start(f"{name}_chips_start", shards, lands, sends[0], after)

    def gather_pass_on(name, started, after, sends):
        _, lands = _exchange_wait(f"{name}_chips_wait", started, after, sends[0])
        return _exchange_start(f"{name}_sibling_start", [], lands, sends[1])

    def gather_finish(name, started, after, sends):
        return _exchange_wait(f"{name}_sibling_wait", started, after, sends[1])[1]

    def branch_out_weights(w_branch_l, w_out_l):
        return w_branch_l.transpose(1, 2, 0, 3).reshape(3, WIDTH, d), w_out_l.reshape(d, d)

    gathering = gather_start("gather_w_in_0", [w_in_b[0]], [in_land(0)], in_sends, mod_mine)
    passing = gather_pass_on("gather_w_in_0", gathering, gathering[4], in_sends)
    rest_gathering = gather_start("gather_rest_0", [w_branch_b[0], w_out_b[0]], rest_lands(0), rest_sends, passing[4])
    next_gathering = None
    if n_layers > 1:
        next_gathering = gather_start("gather_weights_1", [w_in_b[1], w_branch_b[1], w_out_b[1]],
                                      [in_land(1)] + rest_lands(1), layer_sends, rest_gathering[4])
    w_in_l = gather_finish("gather_w_in_0", passing, (next_gathering or rest_gathering)[4], in_sends)[0]

    lbs = _lower_bound_table(lower_bounds)
    norm_w4 = jnp.tile(hgrn_norm_w, (1, WIDTH // HG_HEAD_DIM))

    saved = []
    xl = x0
    for l in range(n_layers):
        shift, scale, gate = mod_mine[l, 0], mod_mine[l, 1], mod_mine[l, 2]
        proj, h_t = _ln_proj(xl, shift, scale, w_in_l, f"ln_proj_{l}")
        o_a, totals = _sb_fwd(proj, f"sb_fwd_{l}")
        if l == 0:
            rest_passing = gather_pass_on("gather_rest_0", rest_gathering, o_a, rest_sends)
        lb_l = lbs[l:l + 1] + rest_passing[4][0, 0] if l == 0 else lbs[l:l + 1]
        o_b = _hgrn_fwd(proj, lb_l, f"hgrn_fwd_{l}")
        if l == 0:
            wb_l, wo_l = branch_out_weights(*gather_finish("gather_rest_0", rest_passing, o_b, rest_sends))
            if n_layers > 1:
                next_passing = gather_pass_on("gather_weights_1", next_gathering, o_b, layer_sends)
                gate = gate + next_passing[4][0, 0]
        x_new, merged, ycat = _merge_fwd(xl, proj, o_a, o_b, gate, norm_w4[l:l + 1], conv_full[l],
                                         wb_l, wo_l, ln_g[l:l + 1], ln_b[l:l + 1], f"merge_fwd_{l}")
        saved.append((xl, proj, h_t, o_a, totals, o_b, merged, ycat, w_in_l, wb_l, wo_l))
        if l == 0 and n_layers > 1:
            w_in_l, w_branch_l, w_out_l = gather_finish("gather_weights_1", next_passing, x_new, layer_sends)
            wb_l, wo_l = branch_out_weights(w_branch_l, w_out_l)
        xl = x_new

    loss_part, dx = _loss_fwd_bwd(xl, target)

    pair_sends = [(1, lambda ins, lands, me, j=j: window(ins[0], me ^ 1 ^ j),
                   lambda lands, me, jj=jj: lands[0].at[jj], lambda lands, me, jj=jj: lands[0].at[jj])
                  for jj, j in enumerate((0, 2, 4, 6))]
    chip_sum_sends = [(j, lambda ins, lands, me, jj=jj: ins[0].at[jj],
                       lambda lands, me, jj=jj: lands[0].at[jj], lambda lands, me, jj=jj: lands[0].at[jj])
                      for jj, j in ((1, 2), (2, 4), (3, 6))]
    rest_scatter = _direct_sends([(0, 0, _slot, _slot), (1, 1, _slot, _slot)])
    scattering = [None] * n_layers
    small_grads = [None] * n_layers
    dmod = [None] * n_layers
    tie = None
    for l in reversed(range(n_layers)):
        xl, proj, h_t, o_a, totals, o_b, merged, ycat, w_in_l, wb_l, wo_l = saved[l]
        scale, gate = mod_mine[l, 1], mod_mine[l, 2]
        if tie is not None:
            gate = gate + tie[0, 0]
        dres, dycat, dproj, gwo_by_owner, gwb_by_owner, mvec = _merge_bwd(
            dx, xl, merged, ycat, proj, gate, wb_l, wo_l, ln_g[l:l + 1], f"merge_bwd_{l}")
        lands = [_place_own((NDEV, 3, WIDTH, dsh), BF16, lax.dynamic_slice_in_dim(gwb_by_owner, me, 1, axis=0),
                            (me, 0, 0, 0)),
                 _place_own((NDEV, dsh, d), BF16, lax.dynamic_slice_in_dim(gwo_by_owner, me, 1, axis=0),
                            (me, 0, 0))]
        rest_started = _exchange_start(f"scatter_rest_{l}_start", [gwb_by_owner, gwo_by_owner], lands, rest_scatter)
        dproj, do_a, do_b, bvec = _branch_bwd(dycat, proj, o_a, o_b, norm_w4[l:l + 1] + rest_started[4][0, 0],
                                              conv_full[l], dproj, f"branch_bwd_{l}")
        dproj = _sb_bwd(proj, do_a, totals, dproj, f"sb_bwd_{l}")
        dproj, dlb = _hgrn_bwd(proj, do_b, lbs[l:l + 1], dproj, f"hgrn_bwd_{l}")
        gwi = _gw_matmul(h_t, dproj, f"gw_matmul_{l}")
        swapping = _exchange_start(f"scatter_in_{l}_sibling_start", [gwi], [lax.empty((4, d, shard), BF16)], pair_sends)
        if l > 0:
            dh = _dh_matmul(dproj, w_in_l, swapping[4], f"dh_matmul_{l}")
        (gwi,), (stage,) = _exchange_wait(f"scatter_in_{l}_sibling_wait", swapping, dh if l > 0 else swapping[4],
                                          pair_sends)
        land, chip_sums = _pair_sum(gwi, stage, me, f"pair_sum_{l}")
        in_started = _exchange_start(f"scatter_in_{l}_chips_start", [chip_sums], [land], chip_sum_sends)
        scattering[l] = (in_started, rest_started)
        tie = in_started[4]
        if l == 0:
            dh = _dh_matmul(dproj, w_in_l, tie, f"dh_matmul_{l}")
        dx, lvec = _ln_bwd(dh, xl, scale + tie[0, 0], dres, f"ln_bwd_{l}")
        dmod[l] = jnp.concatenate([lvec[0], lvec[1], mvec[2]])
        norm_grad = bvec[0].reshape(WIDTH // HG_HEAD_DIM, HG_HEAD_DIM).sum(axis=0)
        small_grads[l] = jnp.concatenate([mvec[0], mvec[1], norm_grad, dlb[0], bvec[1:4].reshape(-1)])
    grad_x = dx[None]

    flat = lambda a: a.reshape(-1, a.shape[-1])
    big = {"w_in": (w_in, m_w_in, v_w_in), "w_branch": (w_branch, m_w_branch, v_w_branch),
           "w_out": (w_out, m_w_out, v_w_out)}
    big_results = {n: None for n in big}

    def adam_layer(l, after):
        in_started, rest_started = scattering[l]
        p_branch_l, p_out_l = _exchange_wait(f"scatter_rest_{l}_wait", rest_started, after, rest_scatter)[1]
        p_in_l = _exchange_wait(f"scatter_in_{l}_chips_wait", in_started, after, chip_sum_sends)[1][0]
        parts = {"w_in": p_in_l, "w_branch": p_branch_l.reshape(NDEV, 3 * WIDTH, dsh), "w_out": p_out_l}
        last = None
        for n, (w, m, v) in big.items():
            rows_per_layer = flat(w).shape[0] // n_layers
            big_results[n] = _sum_adamw(parts[n], flat(w), flat(m), flat(v), f"adamw_{n}_{l}",
                                        first_row=l * rows_per_layer, into=big_results[n], after=last)
            last = big_results[n][3]
        return last

    after_adam = None
    for l in reversed(range(1, n_layers)):
        after_adam = adam_layer(l, tie)

    small_vec = jnp.concatenate(dmod + small_grads + [loss_part.reshape(1)])
    n_small = small_vec.shape[0]
    small_all = _all_gather_small("gather_small_grads", _pad_rows(small_vec, BLK), after=after_adam)
    small_sum = _sum_parts(small_all, "sum_small_grads").reshape(-1)[:n_small]
    dmod_all = small_all.reshape(NDEV, -1)[:, :n_layers * 3 * d].reshape(NDEV, n_layers, 3 * d)

    loss = small_sum[n_small - 1]

    off = n_layers * 3 * d
    grad_b_mod = small_sum[:off].reshape(n_layers, 3 * d)
    per_layer = 2 * d + HG_HEAD_DIM + WIDTH + 3 * WIDTH
    g_ln_g, g_ln_b, g_norm, g_lbs, g_conv = [], [], [], [], []
    for l in range(n_layers):
        seg = small_sum[off + l * per_layer: off + (l + 1) * per_layer]
        g_ln_g.append(seg[:d])
        g_ln_b.append(seg[d:2 * d])
        g_norm.append(seg[2 * d:2 * d + HG_HEAD_DIM])
        g_lbs.append(seg[2 * d + HG_HEAD_DIM:2 * d + HG_HEAD_DIM + WIDTH])
        g_conv.append(seg[2 * d + HG_HEAD_DIM + WIDTH:].reshape(3, WIDTH))
    grad_ln_g, grad_ln_b = jnp.stack(g_ln_g), jnp.stack(g_ln_b)
    grad_norm = jnp.stack(g_norm)
    _, lbs_vjp = jax.vjp(_lower_bound_table, lower_bounds)
    grad_lower = lbs_vjp(jnp.stack(g_lbs))[0]
    grad_conv = lax.dynamic_slice_in_dim(jnp.stack(g_conv), me * cw_cols, cw_cols, axis=2)

    dmod_mine = lax.dynamic_slice_in_dim(dmod_all, me * cm, cm, axis=2).transpose(1, 0, 2)
    grad_w_mod = _wmod_grad(c_all.T, dmod_mine)

    adam_layer(0, grad_w_mod)
    r_w_in, r_w_branch, r_w_out = ([o.reshape(big[n][0].shape) for o in big_results[n]]
                                   for n in ("w_in", "w_branch", "w_out"))
    r_w_mod = [o.reshape(w_mod.shape) for o in
               _sum_adamw(grad_w_mod.reshape(1, -1, cm), flat(w_mod), flat(m_w_mod), flat(v_w_mod), "adamw_w_mod")]

    small_names = ["b_mod", "conv_w", "hgrn_norm_w", "lower_bounds", "ln_g", "ln_b"]
    small_g = [grad_b_mod, grad_conv, grad_norm, grad_lower, grad_ln_g, grad_ln_b]
    small_w = [b_mod, conv_w, hgrn_norm_w, lower_bounds, ln_g, ln_b]
    small_m = [m_b_mod, m_conv_w, m_hgrn_norm_w, m_lower_bounds, m_ln_g, m_ln_b]
    small_v = [v_b_mod, v_conv_w, v_hgrn_norm_w, v_lower_bounds, v_ln_g, v_ln_b]
    as_rows = lambda a: a.reshape(-1, a.shape[-1])
    updates = _adamw_small([as_rows(a) for a in small_g], [as_rows(a) for a in small_w],
                           [as_rows(a) for a in small_m], [as_rows(a) for a in small_v])
    r_small = {n: [g] + [u.reshape(w.shape) for u in upd]
               for n, g, w, upd in zip(small_names, small_g, small_w, updates)}

    results = {"w_mod": r_w_mod, "w_in": r_w_in, "w_branch": r_w_branch, "w_out": r_w_out, **r_small}
    order = ["w_mod", "b_mod", "w_in", "conv_w", "hgrn_norm_w", "lower_bounds", "w_branch", "w_out", "ln_g", "ln_b"]
    outs = [loss, grad_x]
    for idx in range(4):
        outs.extend(results[n][idx] for n in order)
    return tuple(outs)
```

```python
import jax
import jax.numpy as jnp
from jax import lax
from jax.experimental import pallas as pl
from jax.experimental.pallas import tpu as pltpu

F32 = jnp.float32
BF16 = jnp.bfloat16
NDEV = 8
N_LAYERS = 2
SB_HEAD_DIM = 64
HG_HEAD_DIM = 128
WIDTH = 512
BLK = 128
LN_EPS = 1e-5
RMS_EPS = 1e-6
ALPHA = (2.0 * N_LAYERS) ** 0.25
ADAM_LR, ADAM_B1, ADAM_B2, ADAM_EPS, ADAM_WD, ADAM_STEP = 0.001, 0.9, 0.999, 1e-08, 0.01, 10
VMEM_LIMIT = 56 * 1024 * 1024
MESH = pl.DeviceIdType.MESH
HG_LEVELS = (64, 32, 16, 8, 4, 2, 1)


def _pcall(body, *, name, out_shape, grid=None, in_specs=None, out_specs=None, scratch_shapes=(),
           semantics=None, aliases=None):
    kwargs = {}
    if grid is not None:
        kwargs["grid"] = grid
    if in_specs is not None:
        kwargs["in_specs"] = in_specs
    if out_specs is not None:
        kwargs["out_specs"] = out_specs
    if aliases:
        kwargs["input_output_aliases"] = aliases
    return pl.pallas_call(
        body, name=name, out_shape=out_shape, scratch_shapes=list(scratch_shapes),
        compiler_params=pltpu.CompilerParams(dimension_semantics=semantics, vmem_limit_bytes=VMEM_LIMIT),
        interpret=False, **kwargs)


def _dot(a, b):
    return jnp.dot(a, b, preferred_element_type=F32)


def _dot_nt(a, b):
    return lax.dot_general(a, b, (((1,), (1,)), ((), ())), preferred_element_type=F32)


def _dot_tn(a, b):
    return lax.dot_general(a, b, (((0,), (0,)), ((), ())), preferred_element_type=F32)


def _dot_01_l(m_bf16, x):
    x1 = x.astype(BF16)
    x2 = (x - x1.astype(F32)).astype(BF16)
    return _dot(jnp.concatenate([m_bf16, m_bf16], axis=1), jnp.concatenate([x1, x2], axis=0))


def _sigmoid(x):
    return 1.0 / (1.0 + jnp.exp(-x))


def _silu_and_grad(x):
    s = _sigmoid(x)
    return x * s, s * (1.0 + x * (1.0 - s))


LOG2E = 1.4426950408889634
MASKED_SCORE = -1e30


def _softplus2_parts(z2):
    minus_abs = lax.bitcast_convert_type(lax.bitcast_convert_type(z2, jnp.int32) | jnp.int32(-2 ** 31), F32)
    sp2 = jnp.maximum(z2, 0.0) + jnp.log2(1.0 + jnp.exp2(minus_abs))
    return sp2, jnp.exp2(z2 - sp2)


def _split2_lanes(x):
    x1 = x.astype(BF16)
    return jnp.concatenate([x1, (x - x1.astype(F32)).astype(BF16)], axis=1)


def _iota2(shape, dim):
    return lax.broadcasted_iota(jnp.int32, shape, dim)


def _standardize(x):
    mu = jnp.mean(x, axis=-1, keepdims=True)
    xc = x - mu
    var = jnp.mean(xc * xc, axis=-1, keepdims=True)
    rstd = lax.rsqrt(var + LN_EPS)
    return xc * rstd, rstd


def _standardize_bwd(xhat, rstd, dxhat):
    m1 = jnp.mean(dxhat, axis=-1, keepdims=True)
    m2 = jnp.mean(dxhat * xhat, axis=-1, keepdims=True)
    return rstd * (dxhat - m1 - xhat * m2)


def _my_index():
    return 4 * lax.axis_index("x") + 2 * lax.axis_index("y") + lax.axis_index("c")


def _exchange(name, ins, out_shapes, transfers, in_vmem, after=None):
    n_in, n_out, n_t = len(ins), len(out_shapes), len(transfers)

    def body(*refs):
        n_skip = n_in + (0 if after is None else 1)
        in_refs, out_refs = refs[:n_in], refs[n_skip:n_skip + n_out]
        send_sems, recv_sems, local_sems = refs[n_skip + n_out:]
        x, y, c = lax.axis_index("x"), lax.axis_index("y"), lax.axis_index("c")
        me = 4 * x + 2 * y + c
        started = []
        for t, (i, o, src_fn, dst_fn) in enumerate(transfers):
            own = pltpu.make_async_copy(src_fn(in_refs[i], me), dst_fn(out_refs[o], me), local_sems.at[t])
            own.start()
            started.append(own)
        arrivals = []
        for k in range(1, NDEV):
            px = x ^ ((k >> 2) & 1)
            py = y ^ ((k >> 1) & 1)
            pc = c ^ (k & 1)
            peer = 4 * px + 2 * py + pc
            for t, (i, o, src_fn, dst_fn) in enumerate(transfers):
                sem = t * (NDEV - 1) + k - 1
                push = pltpu.make_async_remote_copy(
                    src_ref=src_fn(in_refs[i], peer), dst_ref=dst_fn(out_refs[o], me),
                    send_sem=send_sems.at[sem], recv_sem=recv_sems.at[sem],
                    device_id=(px, py, pc), device_id_type=MESH)
                push.start()
                started.append(push)
                arrivals.append(pltpu.make_async_remote_copy(
                    src_ref=src_fn(in_refs[i], peer), dst_ref=dst_fn(out_refs[o], peer),
                    send_sem=send_sems.at[sem], recv_sem=recv_sems.at[sem],
                    device_id=(px, py, pc), device_id_type=MESH))
        for arrival in arrivals:
            arrival.wait_recv()
        for cp in started[n_t:]:
            cp.wait_send()
        for own in started[:n_t]:
            own.wait()

    space = pltpu.VMEM if in_vmem else pl.ANY
    spec = pl.BlockSpec(memory_space=space)
    extra = [] if after is None else [after]
    return _pcall(
        body, name=name, out_shape=out_shapes,
        in_specs=[spec] * n_in + [pl.BlockSpec(memory_space=pl.ANY)] * len(extra), out_specs=[spec] * n_out,
        scratch_shapes=[pltpu.SemaphoreType.DMA((n_t * (NDEV - 1),)),
                        pltpu.SemaphoreType.DMA((n_t * (NDEV - 1),)),
                        pltpu.SemaphoreType.DMA((n_t,))])(*ins, *extra)


def _whole(ref, dev):
    return ref


def _slot(ref, dev):
    return ref.at[dev]


def _all_gather_small(name, v, after=None):
    out = _exchange(name, [v], [jax.ShapeDtypeStruct((NDEV,) + v.shape, v.dtype)],
                    [(0, 0, _whole, _slot)], in_vmem=True, after=after)
    return out[0]


_HBM_SPEC = pl.BlockSpec(memory_space=pltpu.HBM)
_SEM_SPEC = pl.BlockSpec(memory_space=pltpu.SEMAPHORE)
_DATAFLOW = pltpu.SideEffectType.DATAFLOW_SIDE_EFFECTING


def _peer(x, y, c, k):
    px = x ^ ((k >> 2) & 1)
    py = y ^ ((k >> 1) & 1)
    pc = c ^ (k & 1)
    return (px, py, pc), 4 * px + 2 * py + pc


def _direct_sends(transfers):
    sends = []
    for k in range(1, NDEV):
        for i, o, src_fn, dst_fn in transfers:
            sends.append((k,
                          lambda ins, lands, me, i=i, k=k, src_fn=src_fn: src_fn(ins[i], me ^ k),
                          lambda lands, me, o=o, dst_fn=dst_fn: dst_fn(lands[o], me),
                          lambda lands, me, o=o, k=k, dst_fn=dst_fn: dst_fn(lands[o], me ^ k)))
    return sends


def _exchange_start(name, ins, lands, sends, after=None):
    n_in, n_buf = len(ins), len(ins) + len(lands)
    n_sem = len(sends)

    def body(*refs):
        in_refs, land_refs = refs[:n_in], refs[n_in:n_buf]
        n_skip = n_buf + (0 if after is None else 1)
        send_sems, recv_sems, token = refs[n_skip], refs[n_skip + 1], refs[-1]
        x, y, c = lax.axis_index("x"), lax.axis_index("y"), lax.axis_index("c")
        me = 4 * x + 2 * y + c
        for t, (k, src_fn, dst_fn, _) in enumerate(sends):
            pltpu.make_async_remote_copy(
                src_ref=src_fn(in_refs, land_refs, me), dst_ref=dst_fn(land_refs, me),
                send_sem=send_sems.at[t], recv_sem=recv_sems.at[t],
                device_id=_peer(x, y, c, k)[0], device_id_type=MESH).start()
        token[...] = jnp.zeros_like(token)

    bufs = [pltpu.with_memory_space_constraint(a, pltpu.HBM) for a in list(ins) + list(lands)]
    extra = [] if after is None else [after]
    outs = pl.pallas_call(
        body, name=name,
        out_shape=(pltpu.SemaphoreType.DMA((n_sem,)), pltpu.SemaphoreType.DMA((n_sem,)))
        + tuple(pltpu.HBM(a.shape, a.dtype) for a in bufs) + (jax.ShapeDtypeStruct((8, BLK), F32),),
        in_specs=[_HBM_SPEC] * n_buf + [pl.BlockSpec(memory_space=pl.ANY)] * len(extra),
        out_specs=(_SEM_SPEC, _SEM_SPEC) + (_HBM_SPEC,) * n_buf + (pl.BlockSpec(memory_space=pltpu.VMEM),),
        input_output_aliases={b: 2 + b for b in range(n_buf)},
        compiler_params=pltpu.CompilerParams(has_side_effects=_DATAFLOW),
        interpret=False)(*bufs, *extra)
    return outs[0], outs[1], list(outs[2:2 + n_in]), list(outs[2 + n_in:2 + n_buf]), outs[-1]


def _exchange_wait(name, started, after, sends):
    send_sems, recv_sems, ins, lands, _ = started
    n_in, n_buf = len(ins), len(ins) + len(lands)

    def body(*refs):
        in_refs, land_refs = refs[:n_in], refs[n_in:n_buf]
        send_sems, recv_sems = refs[n_buf], refs[n_buf + 1]
        x, y, c = lax.axis_index("x"), lax.axis_index("y"), lax.axis_index("c")
        me = 4 * x + 2 * y + c
        for t, (k, src_fn, _, rcv_fn) in enumerate(sends):
            cp = pltpu.make_async_remote_copy(
                src_ref=src_fn(in_refs, land_refs, me), dst_ref=rcv_fn(land_refs, me),
                send_sem=send_sems.at[t], recv_sem=recv_sems.at[t],
                device_id=_peer(x, y, c, k)[0], device_id_type=MESH)
            cp.wait_send()
            cp.wait_recv()

    bufs = list(ins) + list(lands)
    outs = pl.pallas_call(
        body, name=name, out_shape=tuple(pltpu.HBM(a.shape, a.dtype) for a in bufs),
        in_specs=[_HBM_SPEC] * n_buf + [_SEM_SPEC, _SEM_SPEC, pl.BlockSpec(memory_space=pl.ANY)],
        out_specs=(_HBM_SPEC,) * n_buf,
        input_output_aliases={b: b for b in range(n_buf)},
        compiler_params=pltpu.CompilerParams(has_side_effects=_DATAFLOW),
        interpret=False)(*bufs, send_sems, recv_sems, after)
    return list(outs[:n_in]), list(outs[n_in:])


def _place_own(shape, dtype, own, start):
    return lax.dynamic_update_slice(lax.empty(shape, dtype), own, start)


def _place_own_window(name, shape, own, me):
    rows, cols = own.shape

    def body(me_ref, zone_in, own_ref, zone_ref):
        del me_ref, zone_in
        zone_ref[...] = own_ref[...]

    return pl.pallas_call(
        body, name=name, out_shape=jax.ShapeDtypeStruct(shape, own.dtype),
        grid_spec=pltpu.PrefetchScalarGridSpec(
            num_scalar_prefetch=1, grid=(1,),
            in_specs=[pl.BlockSpec(memory_space=pl.ANY), pl.BlockSpec((rows, cols), lambda i, me_ref: (0, 0))],
            out_specs=pl.BlockSpec((rows, cols), lambda i, me_ref: (0, me_ref[0]))),
        input_output_aliases={1: 0},
        compiler_params=pltpu.CompilerParams(dimension_semantics=("arbitrary",), vmem_limit_bytes=VMEM_LIMIT),
        interpret=False)(me.reshape(1).astype(jnp.int32), lax.empty(shape, own.dtype), own)


def _mod_fwd(c_all, w_mod, b_mod_mine):
    n_layers, _, cm = w_mod.shape

    def body(c_ref, w_ref, b_ref, o_ref):
        for l in range(n_layers):
            o_ref[l] = jnp.dot(c_ref[...], w_ref[l], preferred_element_type=F32,
                               precision=lax.Precision.HIGHEST) + b_ref[l]

    return _pcall(body, name="mod_fwd", out_shape=jax.ShapeDtypeStruct((n_layers, NDEV, cm), F32))(
        c_all, w_mod, b_mod_mine)


def _ln_proj(x, shift, scale, w_full, name):
    s_len, d = x.shape
    n = w_full.shape[1]
    tm = min(1024, s_len)
    tn = 2304

    def body(x_ref, sh_ref, sc_ref, w_ref, proj_ref, ht_ref, h_scr):
        @pl.when(pl.program_id(1) == 0)
        def _():
            xs, _ = _standardize(x_ref[...])
            h = xs * (1.0 + sc_ref[...]) + sh_ref[...]
            h_scr[...] = h.astype(BF16)
            ht_ref[...] = h.T.astype(BF16)

        proj_ref[...] = _dot(h_scr[...], w_ref[...])

    return _pcall(
        body, name=name,
        out_shape=(jax.ShapeDtypeStruct((s_len, n), F32), jax.ShapeDtypeStruct((d, s_len), BF16)),
        grid=(s_len // tm, n // tn),
        in_specs=[pl.BlockSpec((tm, d), lambda i, j: (i, 0)),
                  pl.BlockSpec((1, d), lambda i, j: (0, 0)),
                  pl.BlockSpec((1, d), lambda i, j: (0, 0)),
                  pl.BlockSpec((d, tn), lambda i, j: (0, j))],
        out_specs=(pl.BlockSpec((tm, tn), lambda i, j: (i, j)),
                   pl.BlockSpec((d, tm), lambda i, j: (0, i))),
        scratch_shapes=[pltpu.VMEM((tm, d), BF16)],
        semantics=("arbitrary", "arbitrary"))(x, shift, scale, w_full)


SB_Q_ROWS = 256
SB_K_BLOCKS = 2


def _sb_fwd(proj, name):
    s_len = proj.shape[0]
    n_pairs = WIDTH // BLK
    qr = min(SB_Q_ROWS, s_len)
    gb = SB_K_BLOCKS
    kw = gb * BLK
    nq = s_len // qr
    assert qr == kw

    def body(q_ref, k_ref, v_ref, o_ref, tot_ref):
        lane = _iota2((1, BLK), 1)
        row = _iota2((BLK, BLK), 0)
        col = _iota2((BLK, BLK), 1)
        half = jnp.concatenate([(row >= col).astype(BF16), jnp.ones((BLK, BLK), BF16)], axis=1)
        suffix_and_sum = jnp.concatenate([half, half], axis=0)
        strict = _iota2((qr, kw), 1) < _iota2((qr, kw), 0)
        head_lanes = [(lane // SB_HEAD_DIM) == hh for hh in range(2)]

        def scores(gi, qms, masked):
            c0 = pl.multiple_of(gi * kw, kw)
            kb = k_ref[pl.ds(c0, kw), :].astype(BF16)
            z2s = [_dot_nt(qms[hh], kb) for hh in range(2)]
            if masked:
                z2s = [jnp.where(strict, z2, MASKED_SCORE) for z2 in z2s]
            return tuple(z2s)

        def accumulate(gi, z2s, carry):
            c0 = pl.multiple_of(gi * kw, kw)
            vb = v_ref[pl.ds(c0, kw), :].astype(BF16)
            sp2s = [_softplus2_parts(z2)[0] for z2 in z2s]
            terms = [[_split2_lanes(sp2[:, b * BLK:(b + 1) * BLK]) for b in range(gb)] for sp2 in sp2s]
            sums = [[_dot(t, suffix_and_sum) for t in head_terms] for head_terms in terms]
            weights, laters = [], []
            for hh in range(2):
                later = carry[2 * hh + 1]
                parts = [None] * gb
                for b in reversed(range(gb)):
                    parts[b] = sums[hh][b][:, :BLK] + later
                    later = later + sums[hh][b][:, BLK:]
                weights.append(jnp.exp2(z2s[hh] - jnp.concatenate(parts, axis=1)).astype(BF16))
                laters.append(later)
            outs = [_dot(weights[hh], vb) for hh in range(2)]
            return (carry[0] + outs[0], laters[0], carry[2] + outs[1], laters[1])

        def queries(i):
            qf = q_ref[pl.ds(pl.multiple_of(i * qr, qr), qr), :] * (SB_HEAD_DIM ** -0.5 * LOG2E)
            return [jnp.where(head_lanes[hh], qf, 0.0).astype(BF16) for hh in range(2)]

        def qtile(i, first_scores):
            r0 = pl.multiple_of(i * qr, qr)
            qms = queries(i)
            zero = jnp.zeros((qr, BLK), F32)

            def step(jj, state):
                gi = i - 1 - jj
                return scores(gi, qms, False) + accumulate(gi + 1, state[:2], state[2:])

            state = lax.fori_loop(0, i, step, first_scores + (zero,) * 4)
            nxt = jnp.minimum(i + 1, nq - 1)
            next_scores = scores(nxt, queries(nxt), True)
            carry = accumulate(0, state[:2], state[2:])
            o_ref[pl.ds(r0, qr), :] = jnp.where(head_lanes[0], carry[0], carry[2])
            tot_ref[0, pl.ds(r0, qr), :] = carry[1]
            tot_ref[1, pl.ds(r0, qr), :] = carry[3]
            return next_scores

        lax.fori_loop(0, nq, qtile, scores(0, queries(0), True))

    col_spec = lambda off: pl.BlockSpec((s_len, BLK), lambda p: (0, off + p))
    return _pcall(
        body, name=name,
        out_shape=(jax.ShapeDtypeStruct((s_len, WIDTH), F32),
                   jax.ShapeDtypeStruct((2 * n_pairs, s_len, BLK), F32)),
        grid=(n_pairs,),
        in_specs=[col_spec(0), col_spec(n_pairs), col_spec(2 * n_pairs)],
        out_specs=(pl.BlockSpec((s_len, BLK), lambda p: (0, p)),
                   pl.BlockSpec((2, s_len, BLK), lambda p: (p, 0, 0))),
        semantics=("arbitrary",))(proj, proj, proj)


def _hg_masks(mask_ref):
    row = _iota2((BLK, BLK), 0)
    col = _iota2((BLK, BLK), 1)
    for v, m in enumerate(HG_LEVELS):
        same = (row // (2 * m)) == (col // (2 * m))
        mask_ref[v] = (same & ((row & m) != 0) & ((col & m) == 0)).astype(F32)


def _hg_mid(b, m):
    if m >= 4:
        n = BLK // (2 * m)
        mid = b.reshape(n, 2 * m, BLK)[:, m - 1:m, :]
        return jnp.broadcast_to(mid, (n, 2 * m, BLK)).reshape(BLK, BLK)
    pos = _iota2((BLK, BLK), 0) & (2 * m - 1)
    out = b
    for p in range(2 * m):
        delta = (m - 1) - p
        if delta != 0:
            out = jnp.where(pos == p, pltpu.roll(b, (-delta) % BLK, 0), out)
    return out


def _hg_chunk_inputs(qraw, fpre, lb):
    sig = _sigmoid(fpre)
    f = lb + (1.0 - lb) * sig
    g = jnp.log(f)
    q, dq_fac = _silu_and_grad(qraw)
    return q, dq_fac, f, sig, g


HG_GROUP = 4


def _neg_abs(x):
    return lax.bitcast_convert_type(lax.bitcast_convert_type(x, jnp.int32) | jnp.int32(-2 ** 31), F32)


def _hg_level_terms(qs, ks, bs, m):
    es = [jnp.exp(_neg_abs(b - _hg_mid(b, m))) for b in bs]
    qts = [(q * e).astype(BF16) for q, e in zip(qs, es)]
    kts = [(k * e).astype(BF16) for k, e in zip(ks, es)]
    return es, qts, kts


def _hg_load(refs, r0, lb_v, lower_incl):
    q_ref, f_ref, i_ref = refs
    heads = []
    for h in range(HG_GROUP):
        sl = slice(h * HG_HEAD_DIM, (h + 1) * HG_HEAD_DIM)
        heads.append(_hg_chunk_inputs(q_ref[pl.ds(r0, BLK), sl], f_ref[pl.ds(r0, BLK), sl], lb_v[:, sl])
                     + (i_ref[pl.ds(r0, BLK), sl],))
    bs = [_dot_01_l(lower_incl, hd[4]) for hd in heads]
    return heads, bs


def _hgrn_fwd(proj, lb, name):
    s_len = proj.shape[0]
    nc = s_len // BLK
    gw = HG_GROUP * HG_HEAD_DIM
    n_groups = WIDTH // gw
    base = 4 * WIDTH // gw

    def body(q_ref, f_ref, i_ref, lb_ref, o_ref, mask_ref):
        _hg_masks(mask_ref)
        row = _iota2((BLK, BLK), 0)
        col = _iota2((BLK, BLK), 1)
        lower_incl = (col <= row).astype(BF16)
        lb_v = lb_ref[...]

        def chunk(ci, sts):
            r0 = pl.multiple_of(ci * BLK, BLK)
            heads, bs = _hg_load((q_ref, f_ref, i_ref), r0, lb_v, lower_incl)
            qs = [hd[0] for hd in heads]
            ks = [1.0 - hd[2] for hd in heads]
            vs = [hd[5] for hd in heads]
            vbs = [v.astype(BF16) for v in vs]
            b_ends = [b[BLK - 1:BLK, :] for b in bs]
            inters = [_dot_nt((q * jnp.exp(b)).astype(BF16), st.astype(BF16)) for q, b, st in zip(qs, bs, sts)]
            scs = [None] * HG_GROUP
            for v_idx, m in enumerate(HG_LEVELS):
                _, qts, kts = _hg_level_terms(qs, ks, bs, m)
                terms = [_dot_nt(qt, kt) for qt, kt in zip(qts, kts)]
                msk = mask_ref[v_idx]
                scs = [t * msk if sc is None else sc + t * msk for sc, t in zip(scs, terms)]
            intras = [_dot(sc.astype(BF16), vb) for sc, vb in zip(scs, vbs)]
            k_decs = [(k * jnp.exp(b_end - b)).astype(BF16) for k, b, b_end in zip(ks, bs, b_ends)]
            grown = [_dot_tn(vb, k_dec) for vb, k_dec in zip(vbs, k_decs)]
            for h in range(HG_GROUP):
                diag = jnp.sum(qs[h] * ks[h], axis=-1, keepdims=True)
                o_ref[pl.ds(r0, BLK), h * HG_HEAD_DIM:(h + 1) * HG_HEAD_DIM] = inters[h] + intras[h] + diag * vs[h]
            return tuple(st * jnp.exp(b_end) + g for st, b_end, g in zip(sts, b_ends, grown))

        lax.fori_loop(0, nc, chunk, (jnp.zeros((HG_HEAD_DIM, HG_HEAD_DIM), F32),) * HG_GROUP)

    col_spec = lambda off: pl.BlockSpec((s_len, gw), lambda h: (0, off + h))
    return _pcall(
        body, name=name, out_shape=jax.ShapeDtypeStruct((s_len, WIDTH), F32),
        grid=(n_groups,),
        in_specs=[col_spec(base), col_spec(base + n_groups), col_spec(base + 2 * n_groups),
                  pl.BlockSpec((1, gw), lambda h: (0, h))],
        out_specs=pl.BlockSpec((s_len, gw), lambda h: (0, h)),
        scratch_shapes=[pltpu.VMEM((len(HG_LEVELS), BLK, BLK), F32)],
        semantics=("arbitrary",))(proj, proj, proj, lb)


def _rms_heads(o_b, norm_w):
    n_parts, h_parts, r_parts = [], [], []
    for h in range(WIDTH // HG_HEAD_DIM):
        sl = slice(h * HG_HEAD_DIM, (h + 1) * HG_HEAD_DIM)
        o = o_b[:, sl]
        rstd = lax.rsqrt(jnp.mean(o * o, axis=-1, keepdims=True) + RMS_EPS)
        ohat = o * rstd
        h_parts.append(ohat)
        n_parts.append(ohat * norm_w[:, sl])
        r_parts.append(jnp.broadcast_to(rstd, o.shape))
    cat = lambda parts: jnp.concatenate(parts, axis=-1)
    return cat(n_parts), cat(h_parts), cat(r_parts)


def _shift_rows_down(halo, cur, k):
    tm = cur.shape[0]
    ext = jnp.concatenate([halo, cur], axis=0)
    return pltpu.roll(ext, k, 0)[8:8 + tm]


def _shift_rows_up(cur, halo, k):
    tm = cur.shape[0]
    ext = jnp.concatenate([cur, halo], axis=0)
    return pltpu.roll(ext, (tm + 8 - k) % (tm + 8), 0)[0:tm]


def _merge_fwd(x, proj, o_a, o_b, gate, norm_w, conv_w, wb, w_out, ln_g, ln_b, name):
    s_len, d = x.shape
    tm = min(256, s_len)
    hb = tm // 8

    def body(x_ref, oa_ref, za_ref, ob_ref, zb_ref, pre_ref, post_ref, u_ref, zc_ref, hpre_ref, hu_ref, g_ref,
             gate_ref, nw_ref, cw_ref, wb_ref, wo_ref, lg_ref, lbias_ref, xn_ref, mg_ref, yc_ref):
        i = pl.program_id(0)
        sa, _ = _silu_and_grad(za_ref[...])
        y_a = (oa_ref[...] * sa).astype(BF16)
        n_b, _, _ = _rms_heads(ob_ref[...], nw_ref[...])
        sb, _ = _silu_and_grad(zb_ref[...])
        y_b = (n_b * sb).astype(BF16)
        a = pre_ref[...] * u_ref[...]
        halo = jnp.where(i > 0, hpre_ref[...] * hu_ref[...], 0.0)
        cw = cw_ref[...]
        conv = cw[0:1] * _shift_rows_down(halo, a, 2) + cw[1:2] * _shift_rows_down(halo, a, 1) + cw[2:3] * a
        sc, _ = _silu_and_grad(zc_ref[...])
        y_c = (post_ref[...] * conv * sc).astype(BF16)
        merged = None
        for k, yk in enumerate((y_a, y_b, y_c)):
            yc_ref[:, k * WIDTH:(k + 1) * WIDTH] = yk
            term = _sigmoid(g_ref[:, k * d:(k + 1) * d]) * _dot(yk, wb_ref[k])
            merged = term if merged is None else merged + term
        mb = merged.astype(BF16)
        mg_ref[...] = mb
        y = _dot(mb, wo_ref[...])
        r = ALPHA * x_ref[...] + (1.0 + gate_ref[...]) * y
        rhat, _ = _standardize(r)
        xn_ref[...] = rhat * lg_ref[...] + lbias_ref[...]

    wcol = lambda cb: pl.BlockSpec((tm, WIDTH), lambda i: (i, cb))
    halo_spec = lambda cb: pl.BlockSpec((8, WIDTH), lambda i: (jnp.maximum(i * hb - 1, 0), cb))
    vec = lambda w: pl.BlockSpec((1, w), lambda i: (0, 0))
    return _pcall(
        body, name=name,
        out_shape=(jax.ShapeDtypeStruct((s_len, d), F32), jax.ShapeDtypeStruct((s_len, d), BF16),
                   jax.ShapeDtypeStruct((s_len, 3 * WIDTH), BF16)),
        grid=(s_len // tm,),
        in_specs=[pl.BlockSpec((tm, d), lambda i: (i, 0)),
                  wcol(0), wcol(3), wcol(0), wcol(7), wcol(8), wcol(9), wcol(10), wcol(11),
                  halo_spec(8), halo_spec(10),
                  pl.BlockSpec((tm, 3 * d), lambda i: (i, 2)),
                  vec(d), vec(WIDTH),
                  pl.BlockSpec((3, WIDTH), lambda i: (0, 0)),
                  pl.BlockSpec((3, WIDTH, d), lambda i: (0, 0, 0)),
                  pl.BlockSpec((d, d), lambda i: (0, 0)),
                  vec(d), vec(d)],
        out_specs=(pl.BlockSpec((tm, d), lambda i: (i, 0)), pl.BlockSpec((tm, d), lambda i: (i, 0)),
                   pl.BlockSpec((tm, 3 * WIDTH), lambda i: (i, 0))),
        semantics=("arbitrary",))(x, o_a, proj, o_b, proj, proj, proj, proj, proj, proj, proj, proj,
                                  gate, norm_w, conv_w, wb, w_out, ln_g, ln_b)


def _loss_fwd_bwd(y, target):
    s_len, d = y.shape
    tm = min(512, s_len)

    def body(y_ref, t_ref, loss_ref, dy_ref):
        @pl.when(pl.program_id(0) == 0)
        def _():
            loss_ref[...] = jnp.zeros_like(loss_ref)

        e = y_ref[...] - t_ref[...]
        dy_ref[...] = e * (1.0 / d)
        part = jnp.sum(jnp.sum(e * e, axis=-1, keepdims=True), axis=0, keepdims=True)
        loss_ref[...] += part * (0.5 / d)

    tile = pl.BlockSpec((tm, d), lambda i: (i, 0))
    return _pcall(body, name="loss", grid=(s_len // tm,),
                  out_shape=(jax.ShapeDtypeStruct((1, 1), F32), jax.ShapeDtypeStruct((s_len, d), F32)),
                  in_specs=[tile, tile],
                  out_specs=(pl.BlockSpec((1, 1), lambda i: (0, 0)), tile),
                  semantics=("arbitrary",))(y, target)


def _merge_bwd(dxn, x, merged, ycat, proj, gate, wb, w_out, ln_g, name):
    s_len, d = x.shape
    tm = min(256, s_len)
    dsh = d // NDEV
    n_tiles = s_len // tm

    def body(dxn_ref, x_ref, mg_ref, yc_ref, g_ref, gate_ref, wb_ref, wo_ref, lg_ref,
             dres_ref, dyc_ref, dg_ref, gwo_out, gwb_out, vec_ref, gwo_ref, gwb_ref):
        @pl.when(pl.program_id(0) == 0)
        def _():
            gwo_ref[...] = jnp.zeros_like(gwo_ref)
            gwb_ref[...] = jnp.zeros_like(gwb_ref)
            vec_ref[...] = jnp.zeros_like(vec_ref)

        mb = mg_ref[...]
        one_gate = 1.0 + gate_ref[...]
        y = _dot(mb, wo_ref[...])
        r = ALPHA * x_ref[...] + one_gate * y
        rhat, rstd = _standardize(r)
        dxn = dxn_ref[...]
        dr = _standardize_bwd(rhat, rstd, dxn * lg_ref[...])
        vec_ref[0:1, :] += jnp.sum(dxn * rhat, axis=0, keepdims=True)
        vec_ref[1:2, :] += jnp.sum(dxn, axis=0, keepdims=True)
        vec_ref[2:3, :] += jnp.sum(dr * y, axis=0, keepdims=True)
        dres_ref[...] = ALPHA * dr
        dy = (one_gate * dr).astype(BF16)
        gwo_ref[...] += _dot_tn(mb, dy)
        dmerged = _dot_nt(dy, wo_ref[...])
        for k in range(3):
            yk = yc_ref[:, k * WIDTH:(k + 1) * WIDTH]
            sg = _sigmoid(g_ref[:, k * d:(k + 1) * d])
            pk = _dot(yk, wb_ref[k])
            dg_ref[:, k * d:(k + 1) * d] = (dmerged * pk * sg * (1.0 - sg)).astype(BF16)
            dpk = (dmerged * sg).astype(BF16)
            dyc_ref[:, k * WIDTH:(k + 1) * WIDTH] = _dot_nt(dpk, wb_ref[k])
            gwb_ref[k] += _dot_tn(yk, dpk)

        @pl.when(pl.program_id(0) == n_tiles - 1)
        def _():
            for o in range(NDEV):
                gwo_out[o] = gwo_ref[o * dsh:(o + 1) * dsh, :].astype(BF16)
                for k in range(3):
                    gwb_out[o, k] = gwb_ref[k, :, o * dsh:(o + 1) * dsh].astype(BF16)

    tile = lambda w: pl.BlockSpec((tm, w), lambda i: (i, 0))
    vec = pl.BlockSpec((1, d), lambda i: (0, 0))
    return _pcall(
        body, name=name,
        out_shape=(jax.ShapeDtypeStruct((s_len, d), F32), jax.ShapeDtypeStruct((s_len, 3 * WIDTH), F32),
                   jax.ShapeDtypeStruct(proj.shape, BF16), jax.ShapeDtypeStruct((NDEV, dsh, d), BF16),
                   jax.ShapeDtypeStruct((NDEV, 3, WIDTH, dsh), BF16), jax.ShapeDtypeStruct((8, d), F32)),
        grid=(n_tiles,),
        in_specs=[tile(d), tile(d), tile(d), tile(3 * WIDTH),
                  pl.BlockSpec((tm, 3 * d), lambda i: (i, 2)),
                  vec, pl.BlockSpec((3, WIDTH, d), lambda i: (0, 0, 0)),
                  pl.BlockSpec((d, d), lambda i: (0, 0)), vec],
        out_specs=(tile(d), tile(3 * WIDTH), pl.BlockSpec((tm, 3 * d), lambda i: (i, 2)),
                   pl.BlockSpec((NDEV, dsh, d), lambda i: (0, 0, 0)),
                   pl.BlockSpec((NDEV, 3, WIDTH, dsh), lambda i: (0, 0, 0, 0)),
                   pl.BlockSpec((8, d), lambda i: (0, 0))),
        scratch_shapes=[pltpu.VMEM((d, d), F32), pltpu.VMEM((3, WIDTH, d), F32)],
        semantics=("arbitrary",))(dxn, x, merged, ycat, proj, gate, wb, w_out, ln_g)


def _branch_bwd(dycat, proj, o_a, o_b, norm_w, conv_w, dproj, name):
    s_len = proj.shape[0]
    tm = min(256, s_len)
    hb = tm // 8
    n_tiles = s_len // tm

    def body(dya_ref, dyb_ref, dyc_ref, oa_ref, za_ref, ob_ref, zb_ref, pre_ref, post_ref, u_ref, zc_ref,
             hpre_ref, hu_ref, ndyc_ref, npost_ref, nzc_ref, nw_ref, cw_ref, dproj_in,
             dproj_ref, doa_ref, dob_ref, vec_ref, dza_scr, dzb_scr, dc_scr, sems):
        del dproj_in
        i = pl.program_id(0)

        @pl.when(i == 0)
        def _():
            vec_ref[...] = jnp.zeros_like(vec_ref)

        sa, dsa = _silu_and_grad(za_ref[...])
        dya = dya_ref[...]
        doa_ref[...] = dya * sa
        dza_scr[...] = (dya * oa_ref[...] * dsa).astype(BF16)
        nw = nw_ref[...]
        n_b, ohat, rstd = _rms_heads(ob_ref[...], nw)
        sb, dsb = _silu_and_grad(zb_ref[...])
        dyb = dyb_ref[...]
        dzb_scr[...] = (dyb * n_b * dsb).astype(BF16)
        dn = dyb * sb
        vec_ref[0:1, :] += jnp.sum(dn * ohat, axis=0, keepdims=True)
        dnw = dn * nw
        parts = []
        for h in range(WIDTH // HG_HEAD_DIM):
            sl = slice(h * HG_HEAD_DIM, (h + 1) * HG_HEAD_DIM)
            m2 = jnp.mean(dnw[:, sl] * ohat[:, sl], axis=-1, keepdims=True)
            parts.append(rstd[:, sl] * (dnw[:, sl] - ohat[:, sl] * m2))
        dob_ref[...] = jnp.concatenate(parts, axis=-1)
        cw = cw_ref[...]
        pre, u, post = pre_ref[...], u_ref[...], post_ref[...]
        a = pre * u
        halo = jnp.where(i > 0, hpre_ref[...] * hu_ref[...], 0.0)
        a1 = _shift_rows_down(halo, a, 1)
        a2 = _shift_rows_down(halo, a, 2)
        conv = cw[0:1] * a2 + cw[1:2] * a1 + cw[2:3] * a
        sc, dsc = _silu_and_grad(zc_ref[...])
        dyc = dyc_ref[...]
        dconv = dyc * post * sc
        nsc, _ = _silu_and_grad(nzc_ref[...])
        nxt = jnp.where(i < n_tiles - 1, ndyc_ref[...] * npost_ref[...] * nsc, 0.0)
        da = cw[2:3] * dconv + cw[1:2] * _shift_rows_up(dconv, nxt, 1) + cw[0:1] * _shift_rows_up(dconv, nxt, 2)
        dc_scr[:, 0 * WIDTH:1 * WIDTH] = (da * u).astype(BF16)
        dc_scr[:, 1 * WIDTH:2 * WIDTH] = (dyc * conv * sc).astype(BF16)
        dc_scr[:, 2 * WIDTH:3 * WIDTH] = (da * pre).astype(BF16)
        dc_scr[:, 3 * WIDTH:4 * WIDTH] = (dyc * post * conv * dsc).astype(BF16)
        vec_ref[1:2, :] += jnp.sum(dconv * a2, axis=0, keepdims=True)
        vec_ref[2:3, :] += jnp.sum(dconv * a1, axis=0, keepdims=True)
        vec_ref[3:4, :] += jnp.sum(dconv * a, axis=0, keepdims=True)
        rows = pl.ds(pl.multiple_of(i * tm, tm), tm)
        copies = [pltpu.make_async_copy(dza_scr, dproj_ref.at[rows, 3 * WIDTH:4 * WIDTH], sems.at[0]),
                  pltpu.make_async_copy(dzb_scr, dproj_ref.at[rows, 7 * WIDTH:8 * WIDTH], sems.at[1]),
                  pltpu.make_async_copy(dc_scr, dproj_ref.at[rows, 8 * WIDTH:12 * WIDTH], sems.at[2])]
        for cp in copies:
            cp.start()
        for cp in copies:
            cp.wait()

    wcol = lambda cb: pl.BlockSpec((tm, WIDTH), lambda i: (i, cb))
    prev = lambda cb: pl.BlockSpec((8, WIDTH), lambda i: (jnp.maximum(i * hb - 1, 0), cb))
    nxt = lambda cb: pl.BlockSpec((8, WIDTH), lambda i: (jnp.minimum((i + 1) * hb, s_len // 8 - 1), cb))
    anyspec = pl.BlockSpec(memory_space=pl.ANY)
    out = jax.ShapeDtypeStruct((s_len, WIDTH), F32)
    return _pcall(
        body, name=name,
        out_shape=(jax.ShapeDtypeStruct(dproj.shape, dproj.dtype), out, out, jax.ShapeDtypeStruct((8, WIDTH), F32)),
        grid=(n_tiles,),
        in_specs=[wcol(0), wcol(1), wcol(2), wcol(0), wcol(3), wcol(0), wcol(7), wcol(8), wcol(9), wcol(10), wcol(11),
                  prev(8), prev(10), nxt(2), nxt(9), nxt(11),
                  pl.BlockSpec((1, WIDTH), lambda i: (0, 0)), pl.BlockSpec((3, WIDTH), lambda i: (0, 0)), anyspec],
        out_specs=(anyspec, wcol(0), wcol(0), pl.BlockSpec((8, WIDTH), lambda i: (0, 0))),
        scratch_shapes=[pltpu.VMEM((tm, WIDTH), BF16), pltpu.VMEM((tm, WIDTH), BF16),
                        pltpu.VMEM((tm, 4 * WIDTH), BF16), pltpu.SemaphoreType.DMA((3,))],
        aliases={18: 0},
        semantics=("arbitrary",))(dycat, dycat, dycat, o_a, proj, o_b, proj, proj, proj, proj, proj,
                                  proj, proj, dycat, proj, proj, norm_w, conv_w, dproj)


def _sb_bwd(proj, do_a, totals, dproj, name):
    s_len = proj.shape[0]
    n_pairs = WIDTH // BLK
    scale = SB_HEAD_DIM ** -0.5
    qr = min(SB_Q_ROWS, s_len)
    gb = SB_K_BLOCKS
    kw = gb * BLK
    nq = s_len // qr
    assert qr == kw

    def body(q_ref, k_ref, v_ref, do_ref, tot_ref, dproj_in, dproj_ref, dq_ref, dk_ref, dv_ref, out_scr, sems):
        del dproj_in
        lane = _iota2((1, BLK), 1)
        row = _iota2((BLK, BLK), 0)
        col = _iota2((BLK, BLK), 1)
        ones = jnp.ones((BLK, BLK), BF16)
        twice = lambda m: jnp.concatenate([m, m], axis=0)
        before_and_sum = twice(jnp.concatenate([(row < col).astype(BF16), ones], axis=1))
        upto_and_sum = twice(jnp.concatenate([(row <= col).astype(BF16), ones], axis=1))
        strict = _iota2((qr, kw), 1) < _iota2((qr, kw), 0)
        head_lanes = [(lane // SB_HEAD_DIM) == hh for hh in range(2)]
        dk_ref[...] = jnp.zeros_like(dk_ref)
        dv_ref[...] = jnp.zeros_like(dv_ref)

        def scores(gi, qms, masked):
            c0 = pl.multiple_of(gi * kw, kw)
            kb = k_ref[pl.ds(c0, kw), :].astype(BF16)
            z2s = [_dot_nt(qms[hh], kb) for hh in range(2)]
            if masked:
                z2s = [jnp.where(strict, z2, MASKED_SCORE) for z2 in z2s]
            return tuple(z2s)

        def process(gi, z2s, qms, doms, totals_i, carry):
            c0 = pl.multiple_of(gi * kw, kw)
            kb = k_ref[pl.ds(c0, kw), :].astype(BF16)
            vb = v_ref[pl.ds(c0, kw), :].astype(BF16)
            das = [_dot_nt(doms[hh], vb) for hh in range(2)]
            halves = [_softplus2_parts(z2) for z2 in z2s]
            terms = [[_split2_lanes(sp2[:, b * BLK:(b + 1) * BLK]) for b in range(gb)] for sp2, _ in halves]
            sums = [[_dot(t, before_and_sum) for t in head_terms] for head_terms in terms]
            weights, gmats, l_befores = [], [], []
            for hh in range(2):
                l_before = carry[3 * hh + 1]
                parts = []
                for b in range(gb):
                    parts.append(totals_i[hh] - l_before - sums[hh][b][:, :BLK])
                    l_before = l_before + sums[hh][b][:, BLK:]
                a = jnp.exp2(z2s[hh] - jnp.concatenate(parts, axis=1))
                weights.append(a.astype(BF16))
                gmats.append(a * das[hh])
                l_befores.append(l_before)
            terms = [[_split2_lanes(g[:, b * BLK:(b + 1) * BLK]) for b in range(gb)] for g in gmats]
            sums = [[_dot(t, upto_and_sum) for t in head_terms] for head_terms in terms]
            dzs, g_befores = [], []
            for hh in range(2):
                g_before = carry[3 * hh + 2]
                parts = []
                for b in range(gb):
                    parts.append(g_before + sums[hh][b][:, :BLK])
                    g_before = g_before + sums[hh][b][:, BLK:]
                dzs.append((gmats[hh] - halves[hh][1] * jnp.concatenate(parts, axis=1)).astype(BF16))
                g_befores.append(g_before)
            dk_t = _dot_tn(jnp.concatenate(qms, axis=0), jnp.concatenate(dzs, axis=0))
            dv_t = _dot_tn(jnp.concatenate(doms, axis=0), jnp.concatenate(weights, axis=0))
            dqs = [_dot(dzs[hh], kb) for hh in range(2)]
            dk_ref[:, pl.ds(c0, kw)] += dk_t * (1.0 / LOG2E)
            dv_ref[:, pl.ds(c0, kw)] += dv_t
            return (carry[0] + dqs[0], l_befores[0], g_befores[0], carry[3] + dqs[1], l_befores[1], g_befores[1])

        def queries(i):
            qf = q_ref[pl.ds(pl.multiple_of(i * qr, qr), qr), :] * (scale * LOG2E)
            return [jnp.where(head_lanes[hh], qf, 0.0).astype(BF16) for hh in range(2)]

        def qtile(i, first_scores):
            r0 = pl.multiple_of(i * qr, qr)
            qms = queries(i)
            dof = do_ref[pl.ds(r0, qr), :]
            doms = [jnp.where(head_lanes[hh], dof, 0.0).astype(BF16) for hh in range(2)]
            totals_i = [tot_ref[hh, pl.ds(r0, qr), :] for hh in range(2)]
            zero = jnp.zeros((qr, BLK), F32)

            def step(gi, state):
                return scores(gi + 1, qms, False) + process(gi, state[:2], qms, doms, totals_i, state[2:])

            def before_diagonal(state):
                return scores(i, qms, True) + process(i - 1, state[:2], qms, doms, totals_i, state[2:])

            state = lax.fori_loop(0, i - 1, step, first_scores + (zero,) * 6)
            state = lax.cond(i > 0, before_diagonal, lambda st: st, state)
            nxt = jnp.minimum(i + 1, nq - 1)
            next_scores = scores(0, queries(nxt), False)
            carry = process(i, state[:2], qms, doms, totals_i, state[2:])
            dq_ref[pl.ds(r0, qr), :] = jnp.where(head_lanes[0], carry[0], carry[3]) * scale
            return next_scores

        lax.fori_loop(0, nq, qtile, scores(0, queries(0), True))
        pair = pl.program_id(0)
        copies = []
        for t, value in enumerate((dq_ref[...], dk_ref[...].T, dv_ref[...].T)):
            out_scr[t] = value.astype(BF16)
            col = pl.multiple_of((t * n_pairs + pair) * BLK, BLK)
            copies.append(pltpu.make_async_copy(out_scr.at[t], dproj_ref.at[:, pl.ds(col, BLK)], sems.at[t]))
            copies[-1].start()
        for cp in copies:
            cp.wait()

    col_spec = lambda off: pl.BlockSpec((s_len, BLK), lambda p: (0, off + p))
    anyspec = pl.BlockSpec(memory_space=pl.ANY)
    return _pcall(
        body, name=name, out_shape=jax.ShapeDtypeStruct(dproj.shape, dproj.dtype), grid=(n_pairs,),
        in_specs=[col_spec(0), col_spec(n_pairs), col_spec(2 * n_pairs), col_spec(0),
                  pl.BlockSpec((2, s_len, BLK), lambda p: (p, 0, 0)), anyspec],
        out_specs=anyspec,
        scratch_shapes=[pltpu.VMEM((s_len, BLK), F32), pltpu.VMEM((BLK, s_len), F32), pltpu.VMEM((BLK, s_len), F32),
                        pltpu.VMEM((3, s_len, BLK), BF16), pltpu.SemaphoreType.DMA((3,))],
        aliases={5: 0},
        semantics=("arbitrary",))(proj, proj, proj, do_a, totals, dproj)


def _hgrn_bwd(proj, do_b, lb, dproj, name):
    s_len = proj.shape[0]
    nc = s_len // BLK
    gw = HG_GROUP * HG_HEAD_DIM
    n_groups = WIDTH // gw
    base = 4 * WIDTH // gw
    heads_of = range(HG_GROUP)

    def body(q_ref, f_ref, i_ref, do_ref, lb_ref, dproj_in, dproj_ref, dlb_ref, mask_ref, st_ref, out_scr, sems):
        del dproj_in
        _hg_masks(mask_ref)
        row = _iota2((BLK, BLK), 0)
        col = _iota2((BLK, BLK), 1)
        lower_incl = (col <= row).astype(BF16)
        upper_incl = (col >= row).astype(BF16)
        lb_v = lb_ref[...]
        refs = (q_ref, f_ref, i_ref)

        def fwd_chunk(ci, sts):
            for h in heads_of:
                st_ref[ci, h] = sts[h]
            heads, bs = _hg_load(refs, pl.multiple_of(ci * BLK, BLK), lb_v, lower_incl)
            b_ends = [b[BLK - 1:BLK, :] for b in bs]
            k_decs = [((1.0 - hd[2]) * jnp.exp(b_end - b)).astype(BF16) for hd, b, b_end in zip(heads, bs, b_ends)]
            grown = [_dot_tn(hd[5].astype(BF16), k_dec) for hd, k_dec in zip(heads, k_decs)]
            return tuple(st * jnp.exp(b_end) + g for st, b_end, g in zip(sts, b_ends, grown))

        zero_state = (jnp.zeros((HG_HEAD_DIM, HG_HEAD_DIM), F32),) * HG_GROUP
        lax.fori_loop(0, nc, fwd_chunk, zero_state)

        def bwd_chunk(cc, carry):
            dsts, suffixes, dlbs = carry
            ci = nc - 1 - cc
            r0 = pl.multiple_of(ci * BLK, BLK)
            heads, bs = _hg_load(refs, r0, lb_v, lower_incl)
            qs = [hd[0] for hd in heads]
            fs = [hd[2] for hd in heads]
            ks = [1.0 - f for f in fs]
            vs = [hd[5] for hd in heads]
            vbs = [v.astype(BF16) for v in vs]
            dos = [do_ref[pl.ds(r0, BLK), h * HG_HEAD_DIM:(h + 1) * HG_HEAD_DIM] for h in heads_of]
            dobs = [do.astype(BF16) for do in dos]
            b_ends = [b[BLK - 1:BLK, :] for b in bs]
            e_qs = [jnp.exp(b) for b in bs]
            e_ks = [jnp.exp(b_end - b) for b, b_end in zip(bs, b_ends)]
            qes = [(q * e).astype(BF16) for q, e in zip(qs, e_qs)]
            khs = [(k * e).astype(BF16) for k, e in zip(ks, e_ks)]
            st_terms = [_split2_lanes(st_ref[ci, h]) for h in heads_of]
            ds_terms = [_split2_lanes(dst) for dst in dsts]
            dqes = [_dot(dob, t[:, :HG_HEAD_DIM]) + _dot(dob, t[:, HG_HEAD_DIM:]) for dob, t in zip(dobs, st_terms)]
            dkhs = [_dot(vb, t[:, :HG_HEAD_DIM]) + _dot(vb, t[:, HG_HEAD_DIM:]) for vb, t in zip(vbs, ds_terms)]
            dvs = [_dot_nt(kh, t[:, :HG_HEAD_DIM]) for kh, t in zip(khs, ds_terms)]
            grown = [_dot_tn(dob, qe) for dob, qe in zip(dobs, qes)]
            das = [_dot_nt(dob, vb) for dob, vb in zip(dobs, vbs)]
            dqs = [e * dqe for e, dqe in zip(e_qs, dqes)]
            dks = [e * dkh for e, dkh in zip(e_ks, dkhs)]
            dlogs = [qe.astype(F32) * dqe - kh.astype(F32) * dkh for qe, dqe, kh, dkh in zip(qes, dqes, khs, dkhs)]
            scs = [None] * HG_GROUP
            for v_idx, m in enumerate(HG_LEVELS):
                es, qms, kms = _hg_level_terms(qs, ks, bs, m)
                msk = mask_ref[v_idx]
                terms = [_dot_nt(qm, km) for qm, km in zip(qms, kms)]
                pms = [(da * msk).astype(BF16) for da in das]
                dqms = [_dot(pm, km) for pm, km in zip(pms, kms)]
                dkms = [_dot_tn(pm, qm) for pm, qm in zip(pms, qms)]
                scs = [t * msk if sc is None else sc + t * msk for sc, t in zip(scs, terms)]
                dqs = [dq + dqm * e for dq, dqm, e in zip(dqs, dqms, es)]
                dks = [dk + dkm * e for dk, dkm, e in zip(dks, dkms, es)]
                dlogs = [dl + (qm.astype(F32) * dqm - km.astype(F32) * dkm)
                         for dl, qm, dqm, km, dkm in zip(dlogs, qms, dqms, kms, dkms)]
            intras = [_dot_tn(sc.astype(BF16), dob) for sc, dob in zip(scs, dobs)]
            dgs = [_dot_01_l(upper_incl, dl) + sfx for dl, sfx in zip(dlogs, suffixes)]
            new_dlbs = []
            for h in heads_of:
                q, dq_fac, f, sig = heads[h][0], heads[h][1], heads[h][2], heads[h][3]
                a_diag = jnp.sum(dos[h] * vs[h], axis=-1, keepdims=True)
                s_diag = jnp.sum(q * ks[h], axis=-1, keepdims=True)
                dq = dqs[h] + a_diag * ks[h]
                dk = dks[h] + a_diag * q
                dv = dvs[h] + intras[h] + s_diag * dos[h]
                dfull = dgs[h] / f - dk
                sl = slice(h * HG_HEAD_DIM, (h + 1) * HG_HEAD_DIM)
                out_scr[0, pl.ds(r0, BLK), sl] = (dq * dq_fac).astype(BF16)
                out_scr[1, pl.ds(r0, BLK), sl] = (dfull * (1.0 - lb_v[:, sl]) * sig * (1.0 - sig)).astype(BF16)
                out_scr[2, pl.ds(r0, BLK), sl] = dv.astype(BF16)
                new_dlbs.append(dlbs[h] + jnp.sum(dfull * (1.0 - sig), axis=0, keepdims=True))
            new_dsts = tuple(dst * jnp.exp(b_end) + g for dst, b_end, g in zip(dsts, b_ends, grown))
            return new_dsts, tuple(dg[0:1, :] for dg in dgs), tuple(new_dlbs)

        zero_row = (jnp.zeros((1, HG_HEAD_DIM), F32),) * HG_GROUP
        _, _, dlbs = lax.fori_loop(0, nc, bwd_chunk, (zero_state, zero_row, zero_row))
        dlb_ref[...] = jnp.broadcast_to(jnp.concatenate(dlbs, axis=1), dlb_ref.shape)
        group = pl.program_id(0)
        copies = []
        for t in range(3):
            col = pl.multiple_of((base + t * n_groups + group) * gw, gw)
            copies.append(pltpu.make_async_copy(out_scr.at[t], dproj_ref.at[:, pl.ds(col, gw)], sems.at[t]))
            copies[-1].start()
        for cp in copies:
            cp.wait()

    col_spec = lambda off: pl.BlockSpec((s_len, gw), lambda h: (0, off + h))
    anyspec = pl.BlockSpec(memory_space=pl.ANY)
    return _pcall(
        body, name=name,
        out_shape=(jax.ShapeDtypeStruct(dproj.shape, dproj.dtype), jax.ShapeDtypeStruct((8, WIDTH), F32)),
        grid=(n_groups,),
        in_specs=[col_spec(base), col_spec(base + n_groups), col_spec(base + 2 * n_groups), col_spec(0),
                  pl.BlockSpec((1, gw), lambda h: (0, h)), anyspec],
        out_specs=(anyspec, pl.BlockSpec((8, gw), lambda h: (0, h))),
        scratch_shapes=[pltpu.VMEM((len(HG_LEVELS), BLK, BLK), F32),
                        pltpu.VMEM((nc, HG_GROUP, HG_HEAD_DIM, HG_HEAD_DIM), F32),
                        pltpu.VMEM((3, s_len, gw), BF16), pltpu.SemaphoreType.DMA((3,))],
        aliases={5: 0},
        semantics=("arbitrary",))(proj, proj, proj, do_b, lb, dproj)


def _dh_matmul(dproj, w_full, after, name):
    s_len, n = dproj.shape
    d = w_full.shape[0]
    tm = min(1024, s_len)
    tk = 4608

    def body(dp_ref, w_ref, after_ref, dh_ref):
        del after_ref
        part = _dot_nt(dp_ref[...], w_ref[...])

        @pl.when(pl.program_id(1) == 0)
        def _():
            dh_ref[...] = part

        @pl.when(pl.program_id(1) > 0)
        def _():
            dh_ref[...] += part

    return _pcall(
        body, name=name, out_shape=jax.ShapeDtypeStruct((s_len, d), F32),
        grid=(s_len // tm, n // tk),
        in_specs=[pl.BlockSpec((tm, tk), lambda i, k: (i, k)), pl.BlockSpec((d, tk), lambda i, k: (0, k)),
                  pl.BlockSpec(memory_space=pl.ANY)],
        out_specs=pl.BlockSpec((tm, d), lambda i, k: (i, 0)),
        semantics=("arbitrary", "arbitrary"))(dproj, w_full, after)


def _gw_matmul(h_t, dproj, name):
    d, s_len = h_t.shape
    n = dproj.shape[1]
    tn = 2304

    def body(ht_ref, dp_ref, gw_ref):
        gw_ref[...] = _dot(ht_ref[...], dp_ref[...]).astype(BF16)

    return _pcall(
        body, name=name, out_shape=jax.ShapeDtypeStruct((d, n), BF16),
        grid=(n // tn,),
        in_specs=[pl.BlockSpec((d, s_len), lambda j: (0, 0)), pl.BlockSpec((s_len, tn), lambda j: (0, j))],
        out_specs=pl.BlockSpec((d, tn), lambda j: (0, j)),
        semantics=("arbitrary",))(h_t, dproj)


def _ln_bwd(dh, x, scale, dres, name):
    s_len, d = x.shape
    tm = min(512, s_len)

    def body(dh_ref, x_ref, sc_ref, dres_ref, dx_ref, vec_ref):
        @pl.when(pl.program_id(0) == 0)
        def _():
            vec_ref[...] = jnp.zeros_like(vec_ref)

        dh = dh_ref[...]
        xs, rstd = _standardize(x_ref[...])
        vec_ref[0:1, :] += jnp.sum(dh, axis=0, keepdims=True)
        vec_ref[1:2, :] += jnp.sum(dh * xs, axis=0, keepdims=True)
        dx_ref[...] = _standardize_bwd(xs, rstd, dh * (1.0 + sc_ref[...])) + dres_ref[...]

    tile = pl.BlockSpec((tm, d), lambda i: (i, 0))
    return _pcall(body, name=name, grid=(s_len // tm,),
                  out_shape=(jax.ShapeDtypeStruct((s_len, d), F32), jax.ShapeDtypeStruct((8, d), F32)),
                  in_specs=[tile, tile, pl.BlockSpec((1, d), lambda i: (0, 0)), tile],
                  out_specs=(tile, pl.BlockSpec((8, d), lambda i: (0, 0))),
                  semantics=("arbitrary",))(dh, x, scale, dres)


def _wmod_grad(c_t, dmod):
    d = c_t.shape[0]
    n_layers, _, cm = dmod.shape

    def body(c_ref, dm_ref, o_ref):
        for l in range(n_layers):
            acc = None
            for b in range(NDEV):
                term = c_ref[:, b:b + 1] * dm_ref[l, b:b + 1, :]
                acc = term if acc is None else acc + term
            o_ref[l] = acc

    return _pcall(body, name="wmod_grad", out_shape=jax.ShapeDtypeStruct((n_layers, d, cm), F32))(c_t, dmod)


def _sum_adamw(parts, w, m, v, name, first_row=0, into=None, after=None):
    n_src, range_rows, cols = parts.shape
    rows = w.shape[0]
    tr = range_rows
    for cand in (512, 256, 128, 64, 32, 16, 8):
        if range_rows % cand == 0 and cand * cols * 4 <= (2 << 20):
            tr = cand
            break
    first_tile = first_row // tr
    assert first_row % tr == 0
    n_extra = (0 if into is None else 4) + (0 if after is None else 1)

    def body(p_ref, w_ref, m_ref, v_ref, *rest):
        g_ref, d_ref, nm_ref, nv_ref = rest[n_extra:]
        g = p_ref[0].astype(F32)
        for s in range(1, n_src):
            g = g + p_ref[s].astype(F32)
        g_ref[...] = g
        d_ref[...], nm_ref[...], nv_ref[...] = _adamw_step(g, w_ref[...], m_ref[...], v_ref[...])

    tile = pl.BlockSpec((tr, cols), lambda i: (i + first_tile, 0))
    anyspec = pl.BlockSpec(memory_space=pl.ANY)
    out = jax.ShapeDtypeStruct((rows, cols), F32)
    extra = ([] if into is None else list(into)) + ([] if after is None else [after])
    aliases = {} if into is None else {4 + k: k for k in range(4)}
    return _pcall(body, name=name, grid=(range_rows // tr,), out_shape=(out,) * 4,
                  in_specs=[pl.BlockSpec((n_src, tr, cols), lambda i: (0, i, 0)), tile, tile, tile]
                  + [anyspec] * len(extra),
                  out_specs=(tile,) * 4, aliases=aliases, semantics=("arbitrary",))(parts, w, m, v, *extra)


def _adamw_step(g, w, m, v):
    nm = ADAM_B1 * m + (1.0 - ADAM_B1) * g
    nv = ADAM_B2 * v + (1.0 - ADAM_B2) * (g * g)
    m_hat = nm / (1.0 - ADAM_B1 ** ADAM_STEP)
    v_hat = nv / (1.0 - ADAM_B2 ** ADAM_STEP)
    return -ADAM_LR * (m_hat / (jnp.sqrt(v_hat) + ADAM_EPS) + ADAM_WD * w), nm, nv


def _adamw_small(gs, ws, ms, vs):
    n = len(gs)

    def body(*refs):
        for p in range(n):
            results = _adamw_step(*(refs[k * n + p][...] for k in range(4)))
            for k in range(3):
                refs[(4 + k) * n + p][...] = results[k]

    shapes = [jax.ShapeDtypeStruct(w.shape, F32) for w in ws]
    outs = _pcall(body, name="adamw_small", out_shape=shapes * 3)(*gs, *ws, *ms, *vs)
    return [(outs[p], outs[n + p], outs[2 * n + p]) for p in range(n)]


def _sum_parts(parts, name):
    n_src = parts.shape[0]

    def body(p_ref, o_ref):
        acc = p_ref[0]
        for s in range(1, n_src):
            acc = acc + p_ref[s]
        o_ref[...] = acc

    return _pcall(body, name=name, out_shape=jax.ShapeDtypeStruct(parts.shape[1:], F32))(parts)


def _pair_sum(gw, stage, me, name):
    d = gw.shape[0]
    n_slots, _, shard = stage.shape

    def body(me_ref, g_ref, s_ref, own_ref, o_ref):
        del me_ref
        total = (g_ref[...].astype(F32) + s_ref[0].astype(F32)).astype(BF16)
        o_ref[0] = total

        @pl.when(pl.program_id(0) == 0)
        def _():
            own_ref[0] = total

    slot = pl.BlockSpec((1, d, shard), lambda jj, me_ref: (jj, 0, 0))
    out = jax.ShapeDtypeStruct(stage.shape, BF16)
    return pl.pallas_call(
        body, name=name, out_shape=(out, out),
        grid_spec=pltpu.PrefetchScalarGridSpec(
            num_scalar_prefetch=1, grid=(n_slots,),
            in_specs=[pl.BlockSpec((d, shard), lambda jj, me_ref: (0, me_ref[0] ^ (2 * jj))), slot],
            out_specs=(pl.BlockSpec((1, d, shard), lambda jj, me_ref: (0, 0, 0)), slot)),
        compiler_params=pltpu.CompilerParams(dimension_semantics=("arbitrary",), vmem_limit_bytes=VMEM_LIMIT),
        interpret=False)(me.reshape(1).astype(jnp.int32), gw, stage)


def _lower_bound_table(lower_bounds):
    p = jax.nn.softmax(lower_bounds.astype(F32), axis=0)
    return jnp.cumsum(p, axis=0) - p[0:1]


def _pad_rows(v, width):
    n = v.shape[0]
    rows = -(-n // width)
    rows = -(-rows // 8) * 8
    return jnp.pad(v, (0, rows * width - n)).reshape(rows, width)


def kernel(x, c, w_mod, b_mod, w_in, conv_w, hgrn_norm_w, lower_bounds, w_branch, w_out, ln_g, ln_b, loss_target, m_w_mod, m_b_mod, m_w_in, m_conv_w, m_hgrn_norm_w, m_lower_bounds, m_w_branch, m_w_out, m_ln_g, m_ln_b, v_w_mod, v_b_mod, v_w_in, v_conv_w, v_hgrn_norm_w, v_lower_bounds, v_w_branch, v_w_out, v_ln_g, v_ln_b):
    n_layers = N_LAYERS
    s_len, d = x.shape[1], x.shape[2]
    n_cols = w_in.shape[2] * NDEV
    cw_cols = conv_w.shape[2]
    cm = w_mod.shape[2]
    me = _my_index()
    x0 = x[0]
    target = loss_target[0]

    small = _pad_rows(jnp.concatenate([c.reshape(-1), conv_w.reshape(-1)]), BLK)
    small_all = _all_gather_small("gather_c_conv", small).reshape(NDEV, -1)
    c_all = small_all[:, :d]
    conv_full = small_all[:, d:d + n_layers * 3 * cw_cols].reshape(NDEV, n_layers, 3, cw_cols)
    conv_full = conv_full.transpose(1, 2, 0, 3).reshape(n_layers, 3, WIDTH)

    b_mod_mine = lax.dynamic_slice_in_dim(b_mod, me * cm, cm, axis=1).reshape(n_layers, 1, cm)
    mod_cols = _mod_fwd(c_all, w_mod, b_mod_mine)
    mod_all = _all_gather_small("gather_mod", mod_cols.reshape(n_layers * NDEV, cm))
    mod_all = mod_all.reshape(NDEV, n_layers, NDEV, cm)
    mod_mine = lax.dynamic_index_in_dim(mod_all, me, axis=2, keepdims=False)
    mod_mine = mod_mine.transpose(1, 0, 2).reshape(n_layers, 3, 1, d)

    shard = w_in.shape[2]
    dsh = d // NDEV
    w_in_b, w_branch_b, w_out_b = w_in.astype(BF16), w_branch.astype(BF16), w_out.astype(BF16)
    window = lambda ref, dev: ref.at[:, pl.ds(pl.multiple_of(dev * shard, BLK), shard)]

    def two_step_sends(places):
        chips, sibling = [], []
        for k in (1, 2, 4, 6):
            for a, place in enumerate(places):
                chips.append((k, lambda ins, lands, me, a=a: ins[a],
                              lambda lands, me, a=a, place=place: place(lands[a], me),
                              lambda lands, me, a=a, k=k, place=place: place(lands[a], me ^ k)))
        for j in (2, 4, 6):
            for a, place in enumerate(places):
                sibling.append((1, lambda ins, lands, me, a=a, j=j, place=place: place(lands[a], me ^ j),
                                lambda lands, me, a=a, j=j, place=place: place(lands[a], me ^ j),
                                lambda lands, me, a=a, j=j, place=place: place(lands[a], me ^ 1 ^ j)))
        return chips, sibling

    in_sends = two_step_sends([window])
    rest_sends = two_step_sends([_slot, _slot])
    layer_sends = two_step_sends([window, _slot, _slot])

    def in_land(l):
        return _place_own_window(f"place_w_in_{l}", (d, n_cols), w_in_b[l], me)

    def rest_lands(l):
        return [_place_own((NDEV, 3, WIDTH, dsh), BF16, w_branch_b[l][None], (me, 0, 0, 0)),
                _place_own((NDEV, dsh, d), BF16, w_out_b[l][None], (me, 0, 0))]

    def gather_start(name, shards, lands, sends, after):
        return _exchange_start(f"{name}_chips_start", shards, lands, sends[0], after)

    def gather_pass_on(name, started, after, sends):
        _, lands = _exchange_wait(f"{name}_chips_wait", started, after, sends[0])
        return _exchange_start(f"{name}_sibling_start", [], lands, sends[1])

    def gather_finish(name, started, after, sends):
        return _exchange_wait(f"{name}_sibling_wait", started, after, sends[1])[1]

    def branch_out_weights(w_branch_l, w_out_l):
        return w_branch_l.transpose(1, 2, 0, 3).reshape(3, WIDTH, d), w_out_l.reshape(d, d)

    gathering = gather_start("gather_w_in_0", [w_in_b[0]], [in_land(0)], in_sends, mod_mine)
    rest_gathering = gather_start("gather_rest_0", [w_branch_b[0], w_out_b[0]], rest_lands(0), rest_sends,
                                  gathering[4])
    next_gathering = None
    if n_layers > 1:
        next_gathering = gather_start("gather_weights_1", [w_in_b[1], w_branch_b[1], w_out_b[1]],
                                      [in_land(1)] + rest_lands(1), layer_sends, rest_gathering[4])
    passing = gather_pass_on("gather_w_in_0", gathering, (next_gathering or rest_gathering)[4], in_sends)
    w_in_l = gather_finish("gather_w_in_0", passing, passing[4], in_sends)[0]

    lbs = _lower_bound_table(lower_bounds)
    norm_w4 = jnp.tile(hgrn_norm_w, (1, WIDTH // HG_HEAD_DIM))

    saved = []
    xl = x0
    for l in range(n_layers):
        shift, scale, gate = mod_mine[l, 0], mod_mine[l, 1], mod_mine[l, 2]
        proj, h_t = _ln_proj(xl, shift, scale, w_in_l, f"ln_proj_{l}")
        o_a, totals = _sb_fwd(proj, f"sb_fwd_{l}")
        if l == 0:
            rest_passing = gather_pass_on("gather_rest_0", rest_gathering, o_a, rest_sends)
        lb_l = lbs[l:l + 1] + rest_passing[4][0, 0] if l == 0 else lbs[l:l + 1]
        o_b = _hgrn_fwd(proj, lb_l, f"hgrn_fwd_{l}")
        if l == 0:
            wb_l, wo_l = branch_out_weights(*gather_finish("gather_rest_0", rest_passing, o_b, rest_sends))
            if n_layers > 1:
                next_passing = gather_pass_on("gather_weights_1", next_gathering, o_b, layer_sends)
                gate = gate + next_passing[4][0, 0]
        x_new, merged, ycat = _merge_fwd(xl, proj, o_a, o_b, gate, norm_w4[l:l + 1], conv_full[l],
                                         wb_l, wo_l, ln_g[l:l + 1], ln_b[l:l + 1], f"merge_fwd_{l}")
        saved.append((xl, proj, h_t, o_a, totals, o_b, merged, ycat, w_in_l, wb_l, wo_l))
        if l == 0 and n_layers > 1:
            w_in_l, w_branch_l, w_out_l = gather_finish("gather_weights_1", next_passing, x_new, layer_sends)
            wb_l, wo_l = branch_out_weights(w_branch_l, w_out_l)
        xl = x_new

    loss_part, dx = _loss_fwd_bwd(xl, target)

    pair_sends = [(1, lambda ins, lands, me, j=j: window(ins[0], me ^ 1 ^ j),
                   lambda lands, me, jj=jj: lands[0].at[jj], lambda lands, me, jj=jj: lands[0].at[jj])
                  for jj, j in enumerate((0, 2, 4, 6))]
    chip_sum_sends = [(j, lambda ins, lands, me, jj=jj: ins[0].at[jj],
                       lambda lands, me, jj=jj: lands[0].at[jj], lambda lands, me, jj=jj: lands[0].at[jj])
                      for jj, j in ((1, 2), (2, 4), (3, 6))]
    rest_scatter = _direct_sends([(0, 0, _slot, _slot), (1, 1, _slot, _slot)])
    scattering = [None] * n_layers
    small_grads = [None] * n_layers
    dmod = [None] * n_layers
    tie = None
    for l in reversed(range(n_layers)):
        xl, proj, h_t, o_a, totals, o_b, merged, ycat, w_in_l, wb_l, wo_l = saved[l]
        scale, gate = mod_mine[l, 1], mod_mine[l, 2]
        if tie is not None:
            gate = gate + tie[0, 0]
        dres, dycat, dproj, gwo_by_owner, gwb_by_owner, mvec = _merge_bwd(
            dx, xl, merged, ycat, proj, gate, wb_l, wo_l, ln_g[l:l + 1], f"merge_bwd_{l}")
        lands = [_place_own((NDEV, 3, WIDTH, dsh), BF16, lax.dynamic_slice_in_dim(gwb_by_owner, me, 1, axis=0),
                            (me, 0, 0, 0)),
                 _place_own((NDEV, dsh, d), BF16, lax.dynamic_slice_in_dim(gwo_by_owner, me, 1, axis=0),
                            (me, 0, 0))]
        rest_started = _exchange_start(f"scatter_rest_{l}_start", [gwb_by_owner, gwo_by_owner], lands, rest_scatter)
        dproj, do_a, do_b, bvec = _branch_bwd(dycat, proj, o_a, o_b, norm_w4[l:l + 1] + rest_started[4][0, 0],
                                              conv_full[l], dproj, f"branch_bwd_{l}")
        dproj = _sb_bwd(proj, do_a, totals, dproj, f"sb_bwd_{l}")
        dproj, dlb = _hgrn_bwd(proj, do_b, lbs[l:l + 1], dproj, f"hgrn_bwd_{l}")
        gwi = _gw_matmul(h_t, dproj, f"gw_matmul_{l}")
        swapping = _exchange_start(f"scatter_in_{l}_sibling_start", [gwi], [lax.empty((4, d, shard), BF16)], pair_sends)
        if l > 0:
            dh = _dh_matmul(dproj, w_in_l, swapping[4], f"dh_matmul_{l}")
        (gwi,), (stage,) = _exchange_wait(f"scatter_in_{l}_sibling_wait", swapping, dh if l > 0 else swapping[4],
                                          pair_sends)
        land, chip_sums = _pair_sum(gwi, stage, me, f"pair_sum_{l}")
        in_started = _exchange_start(f"scatter_in_{l}_chips_start", [chip_sums], [land], chip_sum_sends)
        scattering[l] = (in_started, rest_started)
        tie = in_started[4]
        if l == 0:
            dh = _dh_matmul(dproj, w_in_l, tie, f"dh_matmul_{l}")
        dx, lvec = _ln_bwd(dh, xl, scale + tie[0, 0], dres, f"ln_bwd_{l}")
        dmod[l] = jnp.concatenate([lvec[0], lvec[1], mvec[2]])
        norm_grad = bvec[0].reshape(WIDTH // HG_HEAD_DIM, HG_HEAD_DIM).sum(axis=0)
        small_grads[l] = jnp.concatenate([mvec[0], mvec[1], norm_grad, dlb[0], bvec[1:4].reshape(-1)])
    grad_x = dx[None]

    flat = lambda a: a.reshape(-1, a.shape[-1])
    big = {"w_in": (w_in, m_w_in, v_w_in), "w_branch": (w_branch, m_w_branch, v_w_branch),
           "w_out": (w_out, m_w_out, v_w_out)}
    big_results = {n: None for n in big}

    def adam_layer(l, after):
        in_started, rest_started = scattering[l]
        p_branch_l, p_out_l = _exchange_wait(f"scatter_rest_{l}_wait", rest_started, after, rest_scatter)[1]
        p_in_l = _exchange_wait(f"scatter_in_{l}_chips_wait", in_started, after, chip_sum_sends)[1][0]
        parts = {"w_in": p_in_l, "w_branch": p_branch_l.reshape(NDEV, 3 * WIDTH, dsh), "w_out": p_out_l}
        last = None
        for n, (w, m, v) in big.items():
            rows_per_layer = flat(w).shape[0] // n_layers
            big_results[n] = _sum_adamw(parts[n], flat(w), flat(m), flat(v), f"adamw_{n}_{l}",
                                        first_row=l * rows_per_layer, into=big_results[n], after=last)
            last = big_results[n][3]
        return last

    after_adam = None
    for l in reversed(range(1, n_layers)):
        after_adam = adam_layer(l, tie)

    small_vec = jnp.concatenate(dmod + small_grads + [loss_part.reshape(1)])
    n_small = small_vec.shape[0]
    small_all = _all_gather_small("gather_small_grads", _pad_rows(small_vec, BLK), after=after_adam)
    small_sum = _sum_parts(small_all, "sum_small_grads").reshape(-1)[:n_small]
    dmod_all = small_all.reshape(NDEV, -1)[:, :n_layers * 3 * d].reshape(NDEV, n_layers, 3 * d)

    loss = small_sum[n_small - 1]

    off = n_layers * 3 * d
    grad_b_mod = small_sum[:off].reshape(n_layers, 3 * d)
    per_layer = 2 * d + HG_HEAD_DIM + WIDTH + 3 * WIDTH
    g_ln_g, g_ln_b, g_norm, g_lbs, g_conv = [], [], [], [], []
    for l in range(n_layers):
        seg = small_sum[off + l * per_layer: off + (l + 1) * per_layer]
        g_ln_g.append(seg[:d])
        g_ln_b.append(seg[d:2 * d])
        g_norm.append(seg[2 * d:2 * d + HG_HEAD_DIM])
        g_lbs.append(seg[2 * d + HG_HEAD_DIM:2 * d + HG_HEAD_DIM + WIDTH])
        g_conv.append(seg[2 * d + HG_HEAD_DIM + WIDTH:].reshape(3, WIDTH))
    grad_ln_g, grad_ln_b = jnp.stack(g_ln_g), jnp.stack(g_ln_b)
    grad_norm = jnp.stack(g_norm)
    _, lbs_vjp = jax.vjp(_lower_bound_table, lower_bounds)
    grad_lower = lbs_vjp(jnp.stack(g_lbs))[0]
    grad_conv = lax.dynamic_slice_in_dim(jnp.stack(g_conv), me * cw_cols, cw_cols, axis=2)

    dmod_mine = lax.dynamic_slice_in_dim(dmod_all, me * cm, cm, axis=2).transpose(1, 0, 2)
    grad_w_mod = _wmod_grad(c_all.T, dmod_mine)

    adam_layer(0, grad_w_mod)
    r_w_in, r_w_branch, r_w_out = ([o.reshape(big[n][0].shape) for o in big_results[n]]
                                   for n in ("w_in", "w_branch", "w_out"))
    r_w_mod = [o.reshape(w_mod.shape) for o in
               _sum_adamw(grad_w_mod.reshape(1, -1, cm), flat(w_mod), flat(m_w_mod), flat(v_w_mod), "adamw_w_mod")]

    small_names = ["b_mod", "conv_w", "hgrn_norm_w", "lower_bounds", "ln_g", "ln_b"]
    small_g = [grad_b_mod, grad_conv, grad_norm, grad_lower, grad_ln_g, grad_ln_b]
    small_w = [b_mod, conv_w, hgrn_norm_w, lower_bounds, ln_g, ln_b]
    small_m = [m_b_mod, m_conv_w, m_hgrn_norm_w, m_lower_bounds, m_ln_g, m_ln_b]
    small_v = [v_b_mod, v_conv_w, v_hgrn_norm_w, v_lower_bounds, v_ln_g, v_ln_b]
    as_rows = lambda a: a.reshape(-1, a.shape[-1])
    updates = _adamw_small([as_rows(a) for a in small_g], [as_rows(a) for a in small_w],
                           [as_rows(a) for a in small_m], [as_rows(a) for a in small_v])
    r_small = {n: [g] + [u.reshape(w.shape) for u in upd]
               for n, g, w, upd in zip(small_names, small_g, small_w, updates)}

    results = {"w_mod": r_w_mod, "w_in": r_w_in, "w_branch": r_w_branch, "w_out": r_w_out, **r_small}
    order = ["w_mod", "b_mod", "w_in", "conv_w", "hgrn_norm_w", "lower_bounds", "w_branch", "w_out", "ln_g", "ln_b"]
    outs = [loss, grad_x]
    for idx in range(4):
        outs.extend(results[n][idx] for n in order)
    return tuple(outs)
```

```python
import jax
import jax.numpy as jnp
from jax import lax
from jax.experimental import pallas as pl
from jax.experimental.pallas import tpu as pltpu

F32 = jnp.float32
BF16 = jnp.bfloat16
NDEV = 8
N_LAYERS = 2
SB_HEAD_DIM = 64
HG_HEAD_DIM = 128
WIDTH = 512
BLK = 128
LN_EPS = 1e-5
RMS_EPS = 1e-6
ALPHA = (2.0 * N_LAYERS) ** 0.25
ADAM_LR, ADAM_B1, ADAM_B2, ADAM_EPS, ADAM_WD, ADAM_STEP = 0.001, 0.9, 0.999, 1e-08, 0.01, 10
VMEM_LIMIT = 56 * 1024 * 1024
MESH = pl.DeviceIdType.MESH
HG_LEVELS = (64, 32, 16, 8, 4, 2, 1)


def _pcall(body, *, name, out_shape, grid=None, in_specs=None, out_specs=None, scratch_shapes=(),
           semantics=None, aliases=None):
    kwargs = {}
    if grid is not None:
        kwargs["grid"] = grid
    if in_specs is not None:
        kwargs["in_specs"] = in_specs
    if out_specs is not None:
        kwargs["out_specs"] = out_specs
    if aliases:
        kwargs["input_output_aliases"] = aliases
    return pl.pallas_call(
        body, name=name, out_shape=out_shape, scratch_shapes=list(scratch_shapes),
        compiler_params=pltpu.CompilerParams(dimension_semantics=semantics, vmem_limit_bytes=VMEM_LIMIT),
        interpret=False, **kwargs)


def _dot(a, b):
    return jnp.dot(a, b, preferred_element_type=F32)


def _dot_nt(a, b):
    return lax.dot_general(a, b, (((1,), (1,)), ((), ())), preferred_element_type=F32)


def _dot_tn(a, b):
    return lax.dot_general(a, b, (((0,), (0,)), ((), ())), preferred_element_type=F32)


def _dot_01_l(m_bf16, x):
    x1 = x.astype(BF16)
    x2 = (x - x1.astype(F32)).astype(BF16)
    return _dot(jnp.concatenate([m_bf16, m_bf16], axis=1), jnp.concatenate([x1, x2], axis=0))


def _sigmoid(x):
    return 1.0 / (1.0 + jnp.exp(-x))


def _silu_and_grad(x):
    s = _sigmoid(x)
    return x * s, s * (1.0 + x * (1.0 - s))


LOG2E = 1.4426950408889634
MASKED_SCORE = -1e30


def _softplus2_parts(z2):
    minus_abs = lax.bitcast_convert_type(lax.bitcast_convert_type(z2, jnp.int32) | jnp.int32(-2 ** 31), F32)
    sp2 = jnp.maximum(z2, 0.0) + jnp.log2(1.0 + jnp.exp2(minus_abs))
    return sp2, jnp.exp2(z2 - sp2)


def _split2_lanes(x):
    x1 = x.astype(BF16)
    return jnp.concatenate([x1, (x - x1.astype(F32)).astype(BF16)], axis=1)


def _iota2(shape, dim):
    return lax.broadcasted_iota(jnp.int32, shape, dim)


def _standardize(x):
    mu = jnp.mean(x, axis=-1, keepdims=True)
    xc = x - mu
    var = jnp.mean(xc * xc, axis=-1, keepdims=True)
    rstd = lax.rsqrt(var + LN_EPS)
    return xc * rstd, rstd


def _standardize_bwd(xhat, rstd, dxhat):
    m1 = jnp.mean(dxhat, axis=-1, keepdims=True)
    m2 = jnp.mean(dxhat * xhat, axis=-1, keepdims=True)
    return rstd * (dxhat - m1 - xhat * m2)


def _my_index():
    return 4 * lax.axis_index("x") + 2 * lax.axis_index("y") + lax.axis_index("c")


def _exchange(name, ins, out_shapes, transfers, in_vmem, after=None):
    n_in, n_out, n_t = len(ins), len(out_shapes), len(transfers)

    def body(*refs):
        n_skip = n_in + (0 if after is None else 1)
        in_refs, out_refs = refs[:n_in], refs[n_skip:n_skip + n_out]
        send_sems, recv_sems, local_sems = refs[n_skip + n_out:]
        x, y, c = lax.axis_index("x"), lax.axis_index("y"), lax.axis_index("c")
        me = 4 * x + 2 * y + c
        started = []
        for t, (i, o, src_fn, dst_fn) in enumerate(transfers):
            own = pltpu.make_async_copy(src_fn(in_refs[i], me), dst_fn(out_refs[o], me), local_sems.at[t])
            own.start()
            started.append(own)
        arrivals = []
        for k in range(1, NDEV):
            px = x ^ ((k >> 2) & 1)
            py = y ^ ((k >> 1) & 1)
            pc = c ^ (k & 1)
            peer = 4 * px + 2 * py + pc
            for t, (i, o, src_fn, dst_fn) in enumerate(transfers):
                sem = t * (NDEV - 1) + k - 1
                push = pltpu.make_async_remote_copy(
                    src_ref=src_fn(in_refs[i], peer), dst_ref=dst_fn(out_refs[o], me),
                    send_sem=send_sems.at[sem], recv_sem=recv_sems.at[sem],
                    device_id=(px, py, pc), device_id_type=MESH)
                push.start()
                started.append(push)
                arrivals.append(pltpu.make_async_remote_copy(
                    src_ref=src_fn(in_refs[i], peer), dst_ref=dst_fn(out_refs[o], peer),
                    send_sem=send_sems.at[sem], recv_sem=recv_sems.at[sem],
                    device_id=(px, py, pc), device_id_type=MESH))
        for arrival in arrivals:
            arrival.wait_recv()
        for cp in started[n_t:]:
            cp.wait_send()
        for own in started[:n_t]:
            own.wait()

    space = pltpu.VMEM if in_vmem else pl.ANY
    spec = pl.BlockSpec(memory_space=space)
    extra = [] if after is None else [after]
    return _pcall(
        body, name=name, out_shape=out_shapes,
        in_specs=[spec] * n_in + [pl.BlockSpec(memory_space=pl.ANY)] * len(extra), out_specs=[spec] * n_out,
        scratch_shapes=[pltpu.SemaphoreType.DMA((n_t * (NDEV - 1),)),
                        pltpu.SemaphoreType.DMA((n_t * (NDEV - 1),)),
                        pltpu.SemaphoreType.DMA((n_t,))])(*ins, *extra)


def _whole(ref, dev):
    return ref


def _slot(ref, dev):
    return ref.at[dev]


def _all_gather_small(name, v, after=None):
    out = _exchange(name, [v], [jax.ShapeDtypeStruct((NDEV,) + v.shape, v.dtype)],
                    [(0, 0, _whole, _slot)], in_vmem=True, after=after)
    return out[0]


_HBM_SPEC = pl.BlockSpec(memory_space=pltpu.HBM)
_SEM_SPEC = pl.BlockSpec(memory_space=pltpu.SEMAPHORE)
_DATAFLOW = pltpu.SideEffectType.DATAFLOW_SIDE_EFFECTING


def _peer(x, y, c, k):
    px = x ^ ((k >> 2) & 1)
    py = y ^ ((k >> 1) & 1)
    pc = c ^ (k & 1)
    return (px, py, pc), 4 * px + 2 * py + pc


def _direct_sends(transfers):
    sends = []
    for k in range(1, NDEV):
        for i, o, src_fn, dst_fn in transfers:
            sends.append((k,
                          lambda ins, lands, me, i=i, k=k, src_fn=src_fn: src_fn(ins[i], me ^ k),
                          lambda lands, me, o=o, dst_fn=dst_fn: dst_fn(lands[o], me),
                          lambda lands, me, o=o, k=k, dst_fn=dst_fn: dst_fn(lands[o], me ^ k)))
    return sends


def _exchange_start(name, ins, lands, sends, after=None):
    n_in, n_buf = len(ins), len(ins) + len(lands)
    n_sem = len(sends)

    def body(*refs):
        in_refs, land_refs = refs[:n_in], refs[n_in:n_buf]
        n_skip = n_buf + (0 if after is None else 1)
        send_sems, recv_sems, token = refs[n_skip], refs[n_skip + 1], refs[-1]
        x, y, c = lax.axis_index("x"), lax.axis_index("y"), lax.axis_index("c")
        me = 4 * x + 2 * y + c
        for t, (k, src_fn, dst_fn, _) in enumerate(sends):
            pltpu.make_async_remote_copy(
                src_ref=src_fn(in_refs, land_refs, me), dst_ref=dst_fn(land_refs, me),
                send_sem=send_sems.at[t], recv_sem=recv_sems.at[t],
                device_id=_peer(x, y, c, k)[0], device_id_type=MESH).start()
        token[...] = jnp.zeros_like(token)

    bufs = [pltpu.with_memory_space_constraint(a, pltpu.HBM) for a in list(ins) + list(lands)]
    extra = [] if after is None else [after]
    outs = pl.pallas_call(
        body, name=name,
        out_shape=(pltpu.SemaphoreType.DMA((n_sem,)), pltpu.SemaphoreType.DMA((n_sem,)))
        + tuple(pltpu.HBM(a.shape, a.dtype) for a in bufs) + (jax.ShapeDtypeStruct((8, BLK), F32),),
        in_specs=[_HBM_SPEC] * n_buf + [pl.BlockSpec(memory_space=pl.ANY)] * len(extra),
        out_specs=(_SEM_SPEC, _SEM_SPEC) + (_HBM_SPEC,) * n_buf + (pl.BlockSpec(memory_space=pltpu.VMEM),),
        input_output_aliases={b: 2 + b for b in range(n_buf)},
        compiler_params=pltpu.CompilerParams(has_side_effects=_DATAFLOW),
        interpret=False)(*bufs, *extra)
    return outs[0], outs[1], list(outs[2:2 + n_in]), list(outs[2 + n_in:2 + n_buf]), outs[-1]


def _exchange_wait(name, started, after, sends):
    send_sems, recv_sems, ins, lands, _ = started
    n_in, n_buf = len(ins), len(ins) + len(lands)

    def body(*refs):
        in_refs, land_refs = refs[:n_in], refs[n_in:n_buf]
        send_sems, recv_sems = refs[n_buf], refs[n_buf + 1]
        x, y, c = lax.axis_index("x"), lax.axis_index("y"), lax.axis_index("c")
        me = 4 * x + 2 * y + c
        for t, (k, src_fn, _, rcv_fn) in enumerate(sends):
            cp = pltpu.make_async_remote_copy(
                src_ref=src_fn(in_refs, land_refs, me), dst_ref=rcv_fn(land_refs, me),
                send_sem=send_sems.at[t], recv_sem=recv_sems.at[t],
                device_id=_peer(x, y, c, k)[0], device_id_type=MESH)
            cp.wait_send()
            cp.wait_recv()

    bufs = list(ins) + list(lands)
    outs = pl.pallas_call(
        body, name=name, out_shape=tuple(pltpu.HBM(a.shape, a.dtype) for a in bufs),
        in_specs=[_HBM_SPEC] * n_buf + [_SEM_SPEC, _SEM_SPEC, pl.BlockSpec(memory_space=pl.ANY)],
        out_specs=(_HBM_SPEC,) * n_buf,
        input_output_aliases={b: b for b in range(n_buf)},
        compiler_params=pltpu.CompilerParams(has_side_effects=_DATAFLOW),
        interpret=False)(*bufs, send_sems, recv_sems, after)
    return list(outs[:n_in]), list(outs[n_in:])


def _place_own(shape, dtype, own, start):
    return lax.dynamic_update_slice(lax.empty(shape, dtype), own, start)


def _place_own_window(name, shape, own, me):
    rows, cols = own.shape

    def body(me_ref, zone_in, own_ref, zone_ref):
        del me_ref, zone_in
        zone_ref[...] = own_ref[...]

    return pl.pallas_call(
        body, name=name, out_shape=jax.ShapeDtypeStruct(shape, own.dtype),
        grid_spec=pltpu.PrefetchScalarGridSpec(
            num_scalar_prefetch=1, grid=(1,),
            in_specs=[pl.BlockSpec(memory_space=pl.ANY), pl.BlockSpec((rows, cols), lambda i, me_ref: (0, 0))],
            out_specs=pl.BlockSpec((rows, cols), lambda i, me_ref: (0, me_ref[0]))),
        input_output_aliases={1: 0},
        compiler_params=pltpu.CompilerParams(dimension_semantics=("arbitrary",), vmem_limit_bytes=VMEM_LIMIT),
        interpret=False)(me.reshape(1).astype(jnp.int32), lax.empty(shape, own.dtype), own)


def _mod_fwd(c_all, w_mod, b_mod_mine):
    n_layers, _, cm = w_mod.shape

    def body(c_ref, w_ref, b_ref, o_ref):
        for l in range(n_layers):
            o_ref[l] = jnp.dot(c_ref[...], w_ref[l], preferred_element_type=F32,
                               precision=lax.Precision.HIGHEST) + b_ref[l]

    return _pcall(body, name="mod_fwd", out_shape=jax.ShapeDtypeStruct((n_layers, NDEV, cm), F32))(
        c_all, w_mod, b_mod_mine)


def _ln_proj(x, shift, scale, w_full, name):
    s_len, d = x.shape
    n = w_full.shape[1]
    tm = min(1024, s_len)
    tn = 2304

    def body(x_ref, sh_ref, sc_ref, w_ref, proj_ref, ht_ref, h_scr):
        @pl.when(pl.program_id(1) == 0)
        def _():
            xs, _ = _standardize(x_ref[...])
            h = xs * (1.0 + sc_ref[...]) + sh_ref[...]
            h_scr[...] = h.astype(BF16)
            ht_ref[...] = h.T.astype(BF16)

        proj_ref[...] = _dot(h_scr[...], w_ref[...])

    return _pcall(
        body, name=name,
        out_shape=(jax.ShapeDtypeStruct((s_len, n), F32), jax.ShapeDtypeStruct((d, s_len), BF16)),
        grid=(s_len // tm, n // tn),
        in_specs=[pl.BlockSpec((tm, d), lambda i, j: (i, 0)),
                  pl.BlockSpec((1, d), lambda i, j: (0, 0)),
                  pl.BlockSpec((1, d), lambda i, j: (0, 0)),
                  pl.BlockSpec((d, tn), lambda i, j: (0, j))],
        out_specs=(pl.BlockSpec((tm, tn), lambda i, j: (i, j)),
                   pl.BlockSpec((d, tm), lambda i, j: (0, i))),
        scratch_shapes=[pltpu.VMEM((tm, d), BF16)],
        semantics=("arbitrary", "arbitrary"))(x, shift, scale, w_full)


SB_Q_ROWS = 256
SB_K_BLOCKS = 2


def _sb_fwd(proj, name):
    s_len = proj.shape[0]
    n_pairs = WIDTH // BLK
    qr = min(SB_Q_ROWS, s_len)
    gb = SB_K_BLOCKS
    kw = gb * BLK
    nq = s_len // qr
    assert qr == kw

    def body(q_ref, k_ref, v_ref, o_ref, tot_ref):
        lane = _iota2((1, BLK), 1)
        row = _iota2((BLK, BLK), 0)
        col = _iota2((BLK, BLK), 1)
        half = jnp.concatenate([(row >= col).astype(BF16), jnp.ones((BLK, BLK), BF16)], axis=1)
        suffix_and_sum = jnp.concatenate([half, half], axis=0)
        strict = _iota2((qr, kw), 1) < _iota2((qr, kw), 0)
        head_lanes = [(lane // SB_HEAD_DIM) == hh for hh in range(2)]

        def scores(gi, qms, masked):
            c0 = pl.multiple_of(gi * kw, kw)
            kb = k_ref[pl.ds(c0, kw), :].astype(BF16)
            z2s = [_dot_nt(qms[hh], kb) for hh in range(2)]
            if masked:
                z2s = [jnp.where(strict, z2, MASKED_SCORE) for z2 in z2s]
            return tuple(z2s)

        def accumulate(gi, z2s, carry):
            c0 = pl.multiple_of(gi * kw, kw)
            vb = v_ref[pl.ds(c0, kw), :].astype(BF16)
            sp2s = [_softplus2_parts(z2)[0] for z2 in z2s]
            terms = [[_split2_lanes(sp2[:, b * BLK:(b + 1) * BLK]) for b in range(gb)] for sp2 in sp2s]
            sums = [[_dot(t, suffix_and_sum) for t in head_terms] for head_terms in terms]
            weights, laters = [], []
            for hh in range(2):
                later = carry[2 * hh + 1]
                parts = [None] * gb
                for b in reversed(range(gb)):
                    parts[b] = sums[hh][b][:, :BLK] + later
                    later = later + sums[hh][b][:, BLK:]
                weights.append(jnp.exp2(z2s[hh] - jnp.concatenate(parts, axis=1)).astype(BF16))
                laters.append(later)
            outs = [_dot(weights[hh], vb) for hh in range(2)]
            return (carry[0] + outs[0], laters[0], carry[2] + outs[1], laters[1])

        def queries(i):
            qf = q_ref[pl.ds(pl.multiple_of(i * qr, qr), qr), :] * (SB_HEAD_DIM ** -0.5 * LOG2E)
            return [jnp.where(head_lanes[hh], qf, 0.0).astype(BF16) for hh in range(2)]

        def qtile(i, first_scores):
            r0 = pl.multiple_of(i * qr, qr)
            qms = queries(i)
            zero = jnp.zeros((qr, BLK), F32)

            def step(jj, state):
                gi = i - 1 - jj
                return scores(gi, qms, False) + accumulate(gi + 1, state[:2], state[2:])

            state = lax.fori_loop(0, i, step, first_scores + (zero,) * 4)
            nxt = jnp.minimum(i + 1, nq - 1)
            next_scores = scores(nxt, queries(nxt), True)
            carry = accumulate(0, state[:2], state[2:])
            o_ref[pl.ds(r0, qr), :] = jnp.where(head_lanes[0], carry[0], carry[2])
            tot_ref[0, pl.ds(r0, qr), :] = carry[1]
            tot_ref[1, pl.ds(r0, qr), :] = carry[3]
            return next_scores

        lax.fori_loop(0, nq, qtile, scores(0, queries(0), True))

    col_spec = lambda off: pl.BlockSpec((s_len, BLK), lambda p: (0, off + p))
    return _pcall(
        body, name=name,
        out_shape=(jax.ShapeDtypeStruct((s_len, WIDTH), F32),
                   jax.ShapeDtypeStruct((2 * n_pairs, s_len, BLK), F32)),
        grid=(n_pairs,),
        in_specs=[col_spec(0), col_spec(n_pairs), col_spec(2 * n_pairs)],
        out_specs=(pl.BlockSpec((s_len, BLK), lambda p: (0, p)),
                   pl.BlockSpec((2, s_len, BLK), lambda p: (p, 0, 0))),
        semantics=("arbitrary",))(proj, proj, proj)


def _hg_masks(mask_ref):
    row = _iota2((BLK, BLK), 0)
    col = _iota2((BLK, BLK), 1)
    for v, m in enumerate(HG_LEVELS):
        same = (row // (2 * m)) == (col // (2 * m))
        mask_ref[v] = (same & ((row & m) != 0) & ((col & m) == 0)).astype(F32)


def _hg_mid(b, m):
    if m >= 4:
        n = BLK // (2 * m)
        mid = b.reshape(n, 2 * m, BLK)[:, m - 1:m, :]
        return jnp.broadcast_to(mid, (n, 2 * m, BLK)).reshape(BLK, BLK)
    pos = _iota2((BLK, BLK), 0) & (2 * m - 1)
    out = b
    for p in range(2 * m):
        delta = (m - 1) - p
        if delta != 0:
            out = jnp.where(pos == p, pltpu.roll(b, (-delta) % BLK, 0), out)
    return out


def _hg_chunk_inputs(qraw, fpre, lb):
    sig = _sigmoid(fpre)
    f = lb + (1.0 - lb) * sig
    g = jnp.log(f)
    q, dq_fac = _silu_and_grad(qraw)
    return q, dq_fac, f, sig, g


HG_GROUP = 4


def _neg_abs(x):
    return lax.bitcast_convert_type(lax.bitcast_convert_type(x, jnp.int32) | jnp.int32(-2 ** 31), F32)


def _hg_level_terms(qs, ks, bs, m):
    es = [jnp.exp(_neg_abs(b - _hg_mid(b, m))) for b in bs]
    qts = [(q * e).astype(BF16) for q, e in zip(qs, es)]
    kts = [(k * e).astype(BF16) for k, e in zip(ks, es)]
    return es, qts, kts


def _hg_load(refs, r0, lb_v, lower_incl):
    q_ref, f_ref, i_ref = refs
    heads = []
    for h in range(HG_GROUP):
        sl = slice(h * HG_HEAD_DIM, (h + 1) * HG_HEAD_DIM)
        heads.append(_hg_chunk_inputs(q_ref[pl.ds(r0, BLK), sl], f_ref[pl.ds(r0, BLK), sl], lb_v[:, sl])
                     + (i_ref[pl.ds(r0, BLK), sl],))
    bs = [_dot_01_l(lower_incl, hd[4]) for hd in heads]
    return heads, bs


def _hgrn_fwd(proj, lb, name):
    s_len = proj.shape[0]
    nc = s_len // BLK
    gw = HG_GROUP * HG_HEAD_DIM
    n_groups = WIDTH // gw
    base = 4 * WIDTH // gw

    def body(q_ref, f_ref, i_ref, lb_ref, o_ref, mask_ref):
        _hg_masks(mask_ref)
        row = _iota2((BLK, BLK), 0)
        col = _iota2((BLK, BLK), 1)
        lower_incl = (col <= row).astype(BF16)
        lb_v = lb_ref[...]

        def chunk(ci, sts):
            r0 = pl.multiple_of(ci * BLK, BLK)
            heads, bs = _hg_load((q_ref, f_ref, i_ref), r0, lb_v, lower_incl)
            qs = [hd[0] for hd in heads]
            ks = [1.0 - hd[2] for hd in heads]
            vs = [hd[5] for hd in heads]
            vbs = [v.astype(BF16) for v in vs]
            b_ends = [b[BLK - 1:BLK, :] for b in bs]
            inters = [_dot_nt((q * jnp.exp(b)).astype(BF16), st.astype(BF16)) for q, b, st in zip(qs, bs, sts)]
            scs = [None] * HG_GROUP
            for v_idx, m in enumerate(HG_LEVELS):
                _, qts, kts = _hg_level_terms(qs, ks, bs, m)
                terms = [_dot_nt(qt, kt) for qt, kt in zip(qts, kts)]
                msk = mask_ref[v_idx]
                scs = [t * msk if sc is None else sc + t * msk for sc, t in zip(scs, terms)]
            intras = [_dot(sc.astype(BF16), vb) for sc, vb in zip(scs, vbs)]
            k_decs = [(k * jnp.exp(b_end - b)).astype(BF16) for k, b, b_end in zip(ks, bs, b_ends)]
            grown = [_dot_tn(vb, k_dec) for vb, k_dec in zip(vbs, k_decs)]
            for h in range(HG_GROUP):
                diag = jnp.sum(qs[h] * ks[h], axis=-1, keepdims=True)
                o_ref[pl.ds(r0, BLK), h * HG_HEAD_DIM:(h + 1) * HG_HEAD_DIM] = inters[h] + intras[h] + diag * vs[h]
            return tuple(st * jnp.exp(b_end) + g for st, b_end, g in zip(sts, b_ends, grown))

        lax.fori_loop(0, nc, chunk, (jnp.zeros((HG_HEAD_DIM, HG_HEAD_DIM), F32),) * HG_GROUP)

    col_spec = lambda off: pl.BlockSpec((s_len, gw), lambda h: (0, off + h))
    return _pcall(
        body, name=name, out_shape=jax.ShapeDtypeStruct((s_len, WIDTH), F32),
        grid=(n_groups,),
        in_specs=[col_spec(base), col_spec(base + n_groups), col_spec(base + 2 * n_groups),
                  pl.BlockSpec((1, gw), lambda h: (0, h))],
        out_specs=pl.BlockSpec((s_len, gw), lambda h: (0, h)),
        scratch_shapes=[pltpu.VMEM((len(HG_LEVELS), BLK, BLK), F32)],
        semantics=("arbitrary",))(proj, proj, proj, lb)


def _rms_heads(o_b, norm_w):
    n_parts, h_parts, r_parts = [], [], []
    for h in range(WIDTH // HG_HEAD_DIM):
        sl = slice(h * HG_HEAD_DIM, (h + 1) * HG_HEAD_DIM)
        o = o_b[:, sl]
        rstd = lax.rsqrt(jnp.mean(o * o, axis=-1, keepdims=True) + RMS_EPS)
        ohat = o * rstd
        h_parts.append(ohat)
        n_parts.append(ohat * norm_w[:, sl])
        r_parts.append(jnp.broadcast_to(rstd, o.shape))
    cat = lambda parts: jnp.concatenate(parts, axis=-1)
    return cat(n_parts), cat(h_parts), cat(r_parts)


def _shift_rows_down(halo, cur, k):
    tm = cur.shape[0]
    ext = jnp.concatenate([halo, cur], axis=0)
    return pltpu.roll(ext, k, 0)[8:8 + tm]


def _shift_rows_up(cur, halo, k):
    tm = cur.shape[0]
    ext = jnp.concatenate([cur, halo], axis=0)
    return pltpu.roll(ext, (tm + 8 - k) % (tm + 8), 0)[0:tm]


def _merge_fwd(x, proj, o_a, o_b, gate, norm_w, conv_w, wb, w_out, ln_g, ln_b, name, target=None):
    s_len, d = x.shape
    tm = min(256, s_len)
    hb = tm // 8
    n_in = 19 + (0 if target is None else 1)

    def body(*refs):
        (x_ref, oa_ref, za_ref, ob_ref, zb_ref, pre_ref, post_ref, u_ref, zc_ref, hpre_ref, hu_ref, g_ref,
         gate_ref, nw_ref, cw_ref, wb_ref, wo_ref, lg_ref, lbias_ref) = refs[:19]
        xn_ref, mg_ref, yc_ref = refs[n_in:n_in + 3]
        i = pl.program_id(0)
        sa, _ = _silu_and_grad(za_ref[...])
        y_a = (oa_ref[...] * sa).astype(BF16)
        n_b, _, _ = _rms_heads(ob_ref[...], nw_ref[...])
        sb, _ = _silu_and_grad(zb_ref[...])
        y_b = (n_b * sb).astype(BF16)
        a = pre_ref[...] * u_ref[...]
        halo = jnp.where(i > 0, hpre_ref[...] * hu_ref[...], 0.0)
        cw = cw_ref[...]
        conv = cw[0:1] * _shift_rows_down(halo, a, 2) + cw[1:2] * _shift_rows_down(halo, a, 1) + cw[2:3] * a
        sc, _ = _silu_and_grad(zc_ref[...])
        y_c = (post_ref[...] * conv * sc).astype(BF16)
        merged = None
        for k, yk in enumerate((y_a, y_b, y_c)):
            yc_ref[:, k * WIDTH:(k + 1) * WIDTH] = yk
            term = _sigmoid(g_ref[:, k * d:(k + 1) * d]) * _dot(yk, wb_ref[k])
            merged = term if merged is None else merged + term
        mb = merged.astype(BF16)
        mg_ref[...] = mb
        y = _dot(mb, wo_ref[...])
        r = ALPHA * x_ref[...] + (1.0 + gate_ref[...]) * y
        rhat, _ = _standardize(r)
        xn = rhat * lg_ref[...] + lbias_ref[...]
        if target is None:
            xn_ref[...] = xn
        else:
            t_ref, loss_ref = refs[19], refs[n_in + 3]

            @pl.when(i == 0)
            def _():
                loss_ref[...] = jnp.zeros_like(loss_ref)

            e = xn - t_ref[...]
            xn_ref[...] = e * (1.0 / d)
            part = jnp.sum(jnp.sum(e * e, axis=-1, keepdims=True), axis=0, keepdims=True)
            loss_ref[...] += part * (0.5 / d)

    wcol = lambda cb: pl.BlockSpec((tm, WIDTH), lambda i: (i, cb))
    halo_spec = lambda cb: pl.BlockSpec((8, WIDTH), lambda i: (jnp.maximum(i * hb - 1, 0), cb))
    vec = lambda w: pl.BlockSpec((1, w), lambda i: (0, 0))
    tile = pl.BlockSpec((tm, d), lambda i: (i, 0))
    with_loss = target is not None
    return _pcall(
        body, name=name,
        out_shape=(jax.ShapeDtypeStruct((s_len, d), F32), jax.ShapeDtypeStruct((s_len, d), BF16),
                   jax.ShapeDtypeStruct((s_len, 3 * WIDTH), BF16))
        + ((jax.ShapeDtypeStruct((1, 1), F32),) if with_loss else ()),
        grid=(s_len // tm,),
        in_specs=[tile,
                  wcol(0), wcol(3), wcol(0), wcol(7), wcol(8), wcol(9), wcol(10), wcol(11),
                  halo_spec(8), halo_spec(10),
                  pl.BlockSpec((tm, 3 * d), lambda i: (i, 2)),
                  vec(d), vec(WIDTH),
                  pl.BlockSpec((3, WIDTH), lambda i: (0, 0)),
                  pl.BlockSpec((3, WIDTH, d), lambda i: (0, 0, 0)),
                  pl.BlockSpec((d, d), lambda i: (0, 0)),
                  vec(d), vec(d)] + ([tile] if with_loss else []),
        out_specs=(tile, tile, pl.BlockSpec((tm, 3 * WIDTH), lambda i: (i, 0)))
        + ((pl.BlockSpec((1, 1), lambda i: (0, 0)),) if with_loss else ()),
        semantics=("arbitrary",))(x, o_a, proj, o_b, proj, proj, proj, proj, proj, proj, proj, proj,
                                  gate, norm_w, conv_w, wb, w_out, ln_g, ln_b, *([target] if with_loss else []))


def _merge_bwd(dxn, x, merged, ycat, proj, gate, wb, w_out, ln_g, name):
    s_len, d = x.shape
    tm = min(256, s_len)
    dsh = d // NDEV
    n_tiles = s_len // tm

    def body(dxn_ref, x_ref, mg_ref, yc_ref, g_ref, gate_ref, wb_ref, wo_ref, lg_ref,
             dres_ref, dyc_ref, dg_ref, gwo_out, gwb_out, vec_ref, gwo_ref, gwb_ref):
        @pl.when(pl.program_id(0) == 0)
        def _():
            gwo_ref[...] = jnp.zeros_like(gwo_ref)
            gwb_ref[...] = jnp.zeros_like(gwb_ref)
            vec_ref[...] = jnp.zeros_like(vec_ref)

        mb = mg_ref[...]
        one_gate = 1.0 + gate_ref[...]
        y = _dot(mb, wo_ref[...])
        r = ALPHA * x_ref[...] + one_gate * y
        rhat, rstd = _standardize(r)
        dxn = dxn_ref[...]
        dr = _standardize_bwd(rhat, rstd, dxn * lg_ref[...])
        vec_ref[0:1, :] += jnp.sum(dxn * rhat, axis=0, keepdims=True)
        vec_ref[1:2, :] += jnp.sum(dxn, axis=0, keepdims=True)
        vec_ref[2:3, :] += jnp.sum(dr * y, axis=0, keepdims=True)
        dres_ref[...] = ALPHA * dr
        dy = (one_gate * dr).astype(BF16)
        gwo_ref[...] += _dot_tn(mb, dy)
        dmerged = _dot_nt(dy, wo_ref[...])
        for k in range(3):
            yk = yc_ref[:, k * WIDTH:(k + 1) * WIDTH]
            sg = _sigmoid(g_ref[:, k * d:(k + 1) * d])
            pk = _dot(yk, wb_ref[k])
            dg_ref[:, k * d:(k + 1) * d] = (dmerged * pk * sg * (1.0 - sg)).astype(BF16)
            dpk = (dmerged * sg).astype(BF16)
            dyc_ref[:, k * WIDTH:(k + 1) * WIDTH] = _dot_nt(dpk, wb_ref[k])
            gwb_ref[k] += _dot_tn(yk, dpk)

        @pl.when(pl.program_id(0) == n_tiles - 1)
        def _():
            for o in range(NDEV):
                gwo_out[o] = gwo_ref[o * dsh:(o + 1) * dsh, :].astype(BF16)
                for k in range(3):
                    gwb_out[o, k] = gwb_ref[k, :, o * dsh:(o + 1) * dsh].astype(BF16)

    tile = lambda w: pl.BlockSpec((tm, w), lambda i: (i, 0))
    vec = pl.BlockSpec((1, d), lambda i: (0, 0))
    return _pcall(
        body, name=name,
        out_shape=(jax.ShapeDtypeStruct((s_len, d), F32), jax.ShapeDtypeStruct((s_len, 3 * WIDTH), F32),
                   jax.ShapeDtypeStruct(proj.shape, BF16), jax.ShapeDtypeStruct((NDEV, dsh, d), BF16),
                   jax.ShapeDtypeStruct((NDEV, 3, WIDTH, dsh), BF16), jax.ShapeDtypeStruct((8, d), F32)),
        grid=(n_tiles,),
        in_specs=[tile(d), tile(d), tile(d), tile(3 * WIDTH),
                  pl.BlockSpec((tm, 3 * d), lambda i: (i, 2)),
                  vec, pl.BlockSpec((3, WIDTH, d), lambda i: (0, 0, 0)),
                  pl.BlockSpec((d, d), lambda i: (0, 0)), vec],
        out_specs=(tile(d), tile(3 * WIDTH), pl.BlockSpec((tm, 3 * d), lambda i: (i, 2)),
                   pl.BlockSpec((NDEV, dsh, d), lambda i: (0, 0, 0)),
                   pl.BlockSpec((NDEV, 3, WIDTH, dsh), lambda i: (0, 0, 0, 0)),
                   pl.BlockSpec((8, d), lambda i: (0, 0))),
        scratch_shapes=[pltpu.VMEM((d, d), F32), pltpu.VMEM((3, WIDTH, d), F32)],
        semantics=("arbitrary",))(dxn, x, merged, ycat, proj, gate, wb, w_out, ln_g)


def _branch_bwd(dycat, proj, o_a, o_b, norm_w, conv_w, dproj, name):
    s_len = proj.shape[0]
    tm = min(256, s_len)
    hb = tm // 8
    n_tiles = s_len // tm

    def body(dya_ref, dyb_ref, dyc_ref, oa_ref, za_ref, ob_ref, zb_ref, pre_ref, post_ref, u_ref, zc_ref,
             hpre_ref, hu_ref, ndyc_ref, npost_ref, nzc_ref, nw_ref, cw_ref, dproj_in,
             dproj_ref, doa_ref, dob_ref, vec_ref, dza_scr, dzb_scr, dc_scr, sems):
        del dproj_in
        i = pl.program_id(0)

        @pl.when(i == 0)
        def _():
            vec_ref[...] = jnp.zeros_like(vec_ref)

        sa, dsa = _silu_and_grad(za_ref[...])
        dya = dya_ref[...]
        doa_ref[...] = dya * sa
        dza_scr[...] = (dya * oa_ref[...] * dsa).astype(BF16)
        nw = nw_ref[...]
        n_b, ohat, rstd = _rms_heads(ob_ref[...], nw)
        sb, dsb = _silu_and_grad(zb_ref[...])
        dyb = dyb_ref[...]
        dzb_scr[...] = (dyb * n_b * dsb).astype(BF16)
        dn = dyb * sb
        vec_ref[0:1, :] += jnp.sum(dn * ohat, axis=0, keepdims=True)
        dnw = dn * nw
        parts = []
        for h in range(WIDTH // HG_HEAD_DIM):
            sl = slice(h * HG_HEAD_DIM, (h + 1) * HG_HEAD_DIM)
            m2 = jnp.mean(dnw[:, sl] * ohat[:, sl], axis=-1, keepdims=True)
            parts.append(rstd[:, sl] * (dnw[:, sl] - ohat[:, sl] * m2))
        dob_ref[...] = jnp.concatenate(parts, axis=-1)
        cw = cw_ref[...]
        pre, u, post = pre_ref[...], u_ref[...], post_ref[...]
        a = pre * u
        halo = jnp.where(i > 0, hpre_ref[...] * hu_ref[...], 0.0)
        a1 = _shift_rows_down(halo, a, 1)
        a2 = _shift_rows_down(halo, a, 2)
        conv = cw[0:1] * a2 + cw[1:2] * a1 + cw[2:3] * a
        sc, dsc = _silu_and_grad(zc_ref[...])
        dyc = dyc_ref[...]
        dconv = dyc * post * sc
        nsc, _ = _silu_and_grad(nzc_ref[...])
        nxt = jnp.where(i < n_tiles - 1, ndyc_ref[...] * npost_ref[...] * nsc, 0.0)
        da = cw[2:3] * dconv + cw[1:2] * _shift_rows_up(dconv, nxt, 1) + cw[0:1] * _shift_rows_up(dconv, nxt, 2)
        dc_scr[:, 0 * WIDTH:1 * WIDTH] = (da * u).astype(BF16)
        dc_scr[:, 1 * WIDTH:2 * WIDTH] = (dyc * conv * sc).astype(BF16)
        dc_scr[:, 2 * WIDTH:3 * WIDTH] = (da * pre).astype(BF16)
        dc_scr[:, 3 * WIDTH:4 * WIDTH] = (dyc * post * conv * dsc).astype(BF16)
        vec_ref[1:2, :] += jnp.sum(dconv * a2, axis=0, keepdims=True)
        vec_ref[2:3, :] += jnp.sum(dconv * a1, axis=0, keepdims=True)
        vec_ref[3:4, :] += jnp.sum(dconv * a, axis=0, keepdims=True)
        rows = pl.ds(pl.multiple_of(i * tm, tm), tm)
        copies = [pltpu.make_async_copy(dza_scr, dproj_ref.at[rows, 3 * WIDTH:4 * WIDTH], sems.at[0]),
                  pltpu.make_async_copy(dzb_scr, dproj_ref.at[rows, 7 * WIDTH:8 * WIDTH], sems.at[1]),
                  pltpu.make_async_copy(dc_scr, dproj_ref.at[rows, 8 * WIDTH:12 * WIDTH], sems.at[2])]
        for cp in copies:
            cp.start()
        for cp in copies:
            cp.wait()

    wcol = lambda cb: pl.BlockSpec((tm, WIDTH), lambda i: (i, cb))
    prev = lambda cb: pl.BlockSpec((8, WIDTH), lambda i: (jnp.maximum(i * hb - 1, 0), cb))
    nxt = lambda cb: pl.BlockSpec((8, WIDTH), lambda i: (jnp.minimum((i + 1) * hb, s_len // 8 - 1), cb))
    anyspec = pl.BlockSpec(memory_space=pl.ANY)
    out = jax.ShapeDtypeStruct((s_len, WIDTH), F32)
    return _pcall(
        body, name=name,
        out_shape=(jax.ShapeDtypeStruct(dproj.shape, dproj.dtype), out, out, jax.ShapeDtypeStruct((8, WIDTH), F32)),
        grid=(n_tiles,),
        in_specs=[wcol(0), wcol(1), wcol(2), wcol(0), wcol(3), wcol(0), wcol(7), wcol(8), wcol(9), wcol(10), wcol(11),
                  prev(8), prev(10), nxt(2), nxt(9), nxt(11),
                  pl.BlockSpec((1, WIDTH), lambda i: (0, 0)), pl.BlockSpec((3, WIDTH), lambda i: (0, 0)), anyspec],
        out_specs=(anyspec, wcol(0), wcol(0), pl.BlockSpec((8, WIDTH), lambda i: (0, 0))),
        scratch_shapes=[pltpu.VMEM((tm, WIDTH), BF16), pltpu.VMEM((tm, WIDTH), BF16),
                        pltpu.VMEM((tm, 4 * WIDTH), BF16), pltpu.SemaphoreType.DMA((3,))],
        aliases={18: 0},
        semantics=("arbitrary",))(dycat, dycat, dycat, o_a, proj, o_b, proj, proj, proj, proj, proj,
                                  proj, proj, dycat, proj, proj, norm_w, conv_w, dproj)


def _sb_bwd(proj, do_a, totals, dproj, name):
    s_len = proj.shape[0]
    n_pairs = WIDTH // BLK
    scale = SB_HEAD_DIM ** -0.5
    qr = min(SB_Q_ROWS, s_len)
    gb = SB_K_BLOCKS
    kw = gb * BLK
    nq = s_len // qr
    assert qr == kw

    def body(q_ref, k_ref, v_ref, do_ref, tot_ref, dproj_in, dproj_ref, dq_ref, dk_ref, dv_ref, out_scr, sems):
        del dproj_in
        lane = _iota2((1, BLK), 1)
        row = _iota2((BLK, BLK), 0)
        col = _iota2((BLK, BLK), 1)
        ones = jnp.ones((BLK, BLK), BF16)
        twice = lambda m: jnp.concatenate([m, m], axis=0)
        before_and_sum = twice(jnp.concatenate([(row < col).astype(BF16), ones], axis=1))
        upto_and_sum = twice(jnp.concatenate([(row <= col).astype(BF16), ones], axis=1))
        strict = _iota2((qr, kw), 1) < _iota2((qr, kw), 0)
        head_lanes = [(lane // SB_HEAD_DIM) == hh for hh in range(2)]
        dk_ref[...] = jnp.zeros_like(dk_ref)
        dv_ref[...] = jnp.zeros_like(dv_ref)

        def scores(gi, qms, masked):
            c0 = pl.multiple_of(gi * kw, kw)
            kb = k_ref[pl.ds(c0, kw), :].astype(BF16)
            z2s = [_dot_nt(qms[hh], kb) for hh in range(2)]
            if masked:
                z2s = [jnp.where(strict, z2, MASKED_SCORE) for z2 in z2s]
            return tuple(z2s)

        def process(gi, z2s, qms, doms, totals_i, carry):
            c0 = pl.multiple_of(gi * kw, kw)
            kb = k_ref[pl.ds(c0, kw), :].astype(BF16)
            vb = v_ref[pl.ds(c0, kw), :].astype(BF16)
            das = [_dot_nt(doms[hh], vb) for hh in range(2)]
            halves = [_softplus2_parts(z2) for z2 in z2s]
            terms = [[_split2_lanes(sp2[:, b * BLK:(b + 1) * BLK]) for b in range(gb)] for sp2, _ in halves]
            sums = [[_dot(t, before_and_sum) for t in head_terms] for head_terms in terms]
            weights, gmats, l_befores = [], [], []
            for hh in range(2):
                l_before = carry[3 * hh + 1]
                parts = []
                for b in range(gb):
                    parts.append(totals_i[hh] - l_before - sums[hh][b][:, :BLK])
                    l_before = l_before + sums[hh][b][:, BLK:]
                a = jnp.exp2(z2s[hh] - jnp.concatenate(parts, axis=1))
                weights.append(a.astype(BF16))
                gmats.append(a * das[hh])
                l_befores.append(l_before)
            terms = [[_split2_lanes(g[:, b * BLK:(b + 1) * BLK]) for b in range(gb)] for g in gmats]
            sums = [[_dot(t, upto_and_sum) for t in head_terms] for head_terms in terms]
            dzs, g_befores = [], []
            for hh in range(2):
                g_before = carry[3 * hh + 2]
                parts = []
                for b in range(gb):
                    parts.append(g_before + sums[hh][b][:, :BLK])
                    g_before = g_before + sums[hh][b][:, BLK:]
                dzs.append((gmats[hh] - halves[hh][1] * jnp.concatenate(parts, axis=1)).astype(BF16))
                g_befores.append(g_before)
            dk_t = _dot_tn(jnp.concatenate(qms, axis=0), jnp.concatenate(dzs, axis=0))
            dv_t = _dot_tn(jnp.concatenate(doms, axis=0), jnp.concatenate(weights, axis=0))
            dqs = [_dot(dzs[hh], kb) for hh in range(2)]
            dk_ref[:, pl.ds(c0, kw)] += dk_t * (1.0 / LOG2E)
            dv_ref[:, pl.ds(c0, kw)] += dv_t
            return (carry[0] + dqs[0], l_befores[0], g_befores[0], carry[3] + dqs[1], l_befores[1], g_befores[1])

        def queries(i):
            qf = q_ref[pl.ds(pl.multiple_of(i * qr, qr), qr), :] * (scale * LOG2E)
            return [jnp.where(head_lanes[hh], qf, 0.0).astype(BF16) for hh in range(2)]

        def qtile(i, first_scores):
            r0 = pl.multiple_of(i * qr, qr)
            qms = queries(i)
            dof = do_ref[pl.ds(r0, qr), :]
            doms = [jnp.where(head_lanes[hh], dof, 0.0).astype(BF16) for hh in range(2)]
            totals_i = [tot_ref[hh, pl.ds(r0, qr), :] for hh in range(2)]
            zero = jnp.zeros((qr, BLK), F32)

            def step(gi, state):
                return scores(gi + 1, qms, False) + process(gi, state[:2], qms, doms, totals_i, state[2:])

            def before_diagonal(state):
                return scores(i, qms, True) + process(i - 1, state[:2], qms, doms, totals_i, state[2:])

            state = lax.fori_loop(0, i - 1, step, first_scores + (zero,) * 6)
            state = lax.cond(i > 0, before_diagonal, lambda st: st, state)
            nxt = jnp.minimum(i + 1, nq - 1)
            next_scores = scores(0, queries(nxt), False)
            carry = process(i, state[:2], qms, doms, totals_i, state[2:])
            dq_ref[pl.ds(r0, qr), :] = jnp.where(head_lanes[0], carry[0], carry[3]) * scale
            return next_scores

        lax.fori_loop(0, nq, qtile, scores(0, queries(0), True))
        pair = pl.program_id(0)
        copies = []
        for t, value in enumerate((dq_ref[...], dk_ref[...].T, dv_ref[...].T)):
            out_scr[t] = value.astype(BF16)
            col = pl.multiple_of((t * n_pairs + pair) * BLK, BLK)
            copies.append(pltpu.make_async_copy(out_scr.at[t], dproj_ref.at[:, pl.ds(col, BLK)], sems.at[t]))
            copies[-1].start()
        for cp in copies:
            cp.wait()

    col_spec = lambda off: pl.BlockSpec((s_len, BLK), lambda p: (0, off + p))
    anyspec = pl.BlockSpec(memory_space=pl.ANY)
    return _pcall(
        body, name=name, out_shape=jax.ShapeDtypeStruct(dproj.shape, dproj.dtype), grid=(n_pairs,),
        in_specs=[col_spec(0), col_spec(n_pairs), col_spec(2 * n_pairs), col_spec(0),
                  pl.BlockSpec((2, s_len, BLK), lambda p: (p, 0, 0)), anyspec],
        out_specs=anyspec,
        scratch_shapes=[pltpu.VMEM((s_len, BLK), F32), pltpu.VMEM((BLK, s_len), F32), pltpu.VMEM((BLK, s_len), F32),
                        pltpu.VMEM((3, s_len, BLK), BF16), pltpu.SemaphoreType.DMA((3,))],
        aliases={5: 0},
        semantics=("arbitrary",))(proj, proj, proj, do_a, totals, dproj)


def _hgrn_bwd(proj, do_b, lb, dproj, name):
    s_len = proj.shape[0]
    nc = s_len // BLK
    gw = HG_GROUP * HG_HEAD_DIM
    n_groups = WIDTH // gw
    base = 4 * WIDTH // gw
    heads_of = range(HG_GROUP)

    def body(q_ref, f_ref, i_ref, do_ref, lb_ref, dproj_in, dproj_ref, dlb_ref, mask_ref, st_ref, out_scr, sems):
        del dproj_in
        _hg_masks(mask_ref)
        row = _iota2((BLK, BLK), 0)
        col = _iota2((BLK, BLK), 1)
        lower_incl = (col <= row).astype(BF16)
        upper_incl = (col >= row).astype(BF16)
        lb_v = lb_ref[...]
        refs = (q_ref, f_ref, i_ref)

        def fwd_chunk(ci, sts):
            for h in heads_of:
                st_ref[ci, h] = sts[h]
            heads, bs = _hg_load(refs, pl.multiple_of(ci * BLK, BLK), lb_v, lower_incl)
            b_ends = [b[BLK - 1:BLK, :] for b in bs]
            k_decs = [((1.0 - hd[2]) * jnp.exp(b_end - b)).astype(BF16) for hd, b, b_end in zip(heads, bs, b_ends)]
            grown = [_dot_tn(hd[5].astype(BF16), k_dec) for hd, k_dec in zip(heads, k_decs)]
            return tuple(st * jnp.exp(b_end) + g for st, b_end, g in zip(sts, b_ends, grown))

        zero_state = (jnp.zeros((HG_HEAD_DIM, HG_HEAD_DIM), F32),) * HG_GROUP
        lax.fori_loop(0, nc, fwd_chunk, zero_state)

        def bwd_chunk(cc, carry):
            dsts, suffixes, dlbs = carry
            ci = nc - 1 - cc
            r0 = pl.multiple_of(ci * BLK, BLK)
            heads, bs = _hg_load(refs, r0, lb_v, lower_incl)
            qs = [hd[0] for hd in heads]
            fs = [hd[2] for hd in heads]
            ks = [1.0 - f for f in fs]
            vs = [hd[5] for hd in heads]
            vbs = [v.astype(BF16) for v in vs]
            dos = [do_ref[pl.ds(r0, BLK), h * HG_HEAD_DIM:(h + 1) * HG_HEAD_DIM] for h in heads_of]
            dobs = [do.astype(BF16) for do in dos]
            b_ends = [b[BLK - 1:BLK, :] for b in bs]
            e_qs = [jnp.exp(b) for b in bs]
            e_ks = [jnp.exp(b_end - b) for b, b_end in zip(bs, b_ends)]
            qes = [(q * e).astype(BF16) for q, e in zip(qs, e_qs)]
            khs = [(k * e).astype(BF16) for k, e in zip(ks, e_ks)]
            st_terms = [_split2_lanes(st_ref[ci, h]) for h in heads_of]
            ds_terms = [_split2_lanes(dst) for dst in dsts]
            dqes = [_dot(dob, t[:, :HG_HEAD_DIM]) + _dot(dob, t[:, HG_HEAD_DIM:]) for dob, t in zip(dobs, st_terms)]
            dkhs = [_dot(vb, t[:, :HG_HEAD_DIM]) + _dot(vb, t[:, HG_HEAD_DIM:]) for vb, t in zip(vbs, ds_terms)]
            dvs = [_dot_nt(kh, t[:, :HG_HEAD_DIM]) for kh, t in zip(khs, ds_terms)]
            grown = [_dot_tn(dob, qe) for dob, qe in zip(dobs, qes)]
            das = [_dot_nt(dob, vb) for dob, vb in zip(dobs, vbs)]
            dqs = [e * dqe for e, dqe in zip(e_qs, dqes)]
            dks = [e * dkh for e, dkh in zip(e_ks, dkhs)]
            dlogs = [qe.astype(F32) * dqe - kh.astype(F32) * dkh for qe, dqe, kh, dkh in zip(qes, dqes, khs, dkhs)]
            scs = [None] * HG_GROUP
            for v_idx, m in enumerate(HG_LEVELS):
                es, qms, kms = _hg_level_terms(qs, ks, bs, m)
                msk = mask_ref[v_idx]
                terms = [_dot_nt(qm, km) for qm, km in zip(qms, kms)]
                pms = [(da * msk).astype(BF16) for da in das]
                dqms = [_dot(pm, km) for pm, km in zip(pms, kms)]
                dkms = [_dot_tn(pm, qm) for pm, qm in zip(pms, qms)]
                scs = [t * msk if sc is None else sc + t * msk for sc, t in zip(scs, terms)]
                dqs = [dq + dqm * e for dq, dqm, e in zip(dqs, dqms, es)]
                dks = [dk + dkm * e for dk, dkm, e in zip(dks, dkms, es)]
                dlogs = [dl + (qm.astype(F32) * dqm - km.astype(F32) * dkm)
                         for dl, qm, dqm, km, dkm in zip(dlogs, qms, dqms, kms, dkms)]
            intras = [_dot_tn(sc.astype(BF16), dob) for sc, dob in zip(scs, dobs)]
            dgs = [_dot_01_l(upper_incl, dl) + sfx for dl, sfx in zip(dlogs, suffixes)]
            new_dlbs = []
            for h in heads_of:
                q, dq_fac, f, sig = heads[h][0], heads[h][1], heads[h][2], heads[h][3]
                a_diag = jnp.sum(dos[h] * vs[h], axis=-1, keepdims=True)
                s_diag = jnp.sum(q * ks[h], axis=-1, keepdims=True)
                dq = dqs[h] + a_diag * ks[h]
                dk = dks[h] + a_diag * q
                dv = dvs[h] + intras[h] + s_diag * dos[h]
                dfull = dgs[h] / f - dk
                sl = slice(h * HG_HEAD_DIM, (h + 1) * HG_HEAD_DIM)
                out_scr[0, pl.ds(r0, BLK), sl] = (dq * dq_fac).astype(BF16)
                out_scr[1, pl.ds(r0, BLK), sl] = (dfull * (1.0 - lb_v[:, sl]) * sig * (1.0 - sig)).astype(BF16)
                out_scr[2, pl.ds(r0, BLK), sl] = dv.astype(BF16)
                new_dlbs.append(dlbs[h] + jnp.sum(dfull * (1.0 - sig), axis=0, keepdims=True))
            new_dsts = tuple(dst * jnp.exp(b_end) + g for dst, b_end, g in zip(dsts, b_ends, grown))
            return new_dsts, tuple(dg[0:1, :] for dg in dgs), tuple(new_dlbs)

        zero_row = (jnp.zeros((1, HG_HEAD_DIM), F32),) * HG_GROUP
        _, _, dlbs = lax.fori_loop(0, nc, bwd_chunk, (zero_state, zero_row, zero_row))
        dlb_ref[...] = jnp.broadcast_to(jnp.concatenate(dlbs, axis=1), dlb_ref.shape)
        group = pl.program_id(0)
        copies = []
        for t in range(3):
            col = pl.multiple_of((base + t * n_groups + group) * gw, gw)
            copies.append(pltpu.make_async_copy(out_scr.at[t], dproj_ref.at[:, pl.ds(col, gw)], sems.at[t]))
            copies[-1].start()
        for cp in copies:
            cp.wait()

    col_spec = lambda off: pl.BlockSpec((s_len, gw), lambda h: (0, off + h))
    anyspec = pl.BlockSpec(memory_space=pl.ANY)
    return _pcall(
        body, name=name,
        out_shape=(jax.ShapeDtypeStruct(dproj.shape, dproj.dtype), jax.ShapeDtypeStruct((8, WIDTH), F32)),
        grid=(n_groups,),
        in_specs=[col_spec(base), col_spec(base + n_groups), col_spec(base + 2 * n_groups), col_spec(0),
                  pl.BlockSpec((1, gw), lambda h: (0, h)), anyspec],
        out_specs=(anyspec, pl.BlockSpec((8, gw), lambda h: (0, h))),
        scratch_shapes=[pltpu.VMEM((len(HG_LEVELS), BLK, BLK), F32),
                        pltpu.VMEM((nc, HG_GROUP, HG_HEAD_DIM, HG_HEAD_DIM), F32),
                        pltpu.VMEM((3, s_len, gw), BF16), pltpu.SemaphoreType.DMA((3,))],
        aliases={5: 0},
        semantics=("arbitrary",))(proj, proj, proj, do_b, lb, dproj)


def _dh_matmul(dproj, w_full, after, name):
    s_len, n = dproj.shape
    d = w_full.shape[0]
    tm = min(1024, s_len)
    tk = 4608

    def body(dp_ref, w_ref, after_ref, dh_ref):
        del after_ref
        part = _dot_nt(dp_ref[...], w_ref[...])

        @pl.when(pl.program_id(1) == 0)
        def _():
            dh_ref[...] = part

        @pl.when(pl.program_id(1) > 0)
        def _():
            dh_ref[...] += part

    return _pcall(
        body, name=name, out_shape=jax.ShapeDtypeStruct((s_len, d), F32),
        grid=(s_len // tm, n // tk),
        in_specs=[pl.BlockSpec((tm, tk), lambda i, k: (i, k)), pl.BlockSpec((d, tk), lambda i, k: (0, k)),
                  pl.BlockSpec(memory_space=pl.ANY)],
        out_specs=pl.BlockSpec((tm, d), lambda i, k: (i, 0)),
        semantics=("arbitrary", "arbitrary"))(dproj, w_full, after)


def _gw_matmul(h_t, dproj, name):
    d, s_len = h_t.shape
    n = dproj.shape[1]
    tn = 2304

    def body(ht_ref, dp_ref, gw_ref):
        gw_ref[...] = _dot(ht_ref[...], dp_ref[...]).astype(BF16)

    return _pcall(
        body, name=name, out_shape=jax.ShapeDtypeStruct((d, n), BF16),
        grid=(n // tn,),
        in_specs=[pl.BlockSpec((d, s_len), lambda j: (0, 0)), pl.BlockSpec((s_len, tn), lambda j: (0, j))],
        out_specs=pl.BlockSpec((d, tn), lambda j: (0, j)),
        semantics=("arbitrary",))(h_t, dproj)


def _ln_bwd(dh, x, scale, dres, name):
    s_len, d = x.shape
    tm = min(512, s_len)

    def body(dh_ref, x_ref, sc_ref, dres_ref, dx_ref, vec_ref):
        @pl.when(pl.program_id(0) == 0)
        def _():
            vec_ref[...] = jnp.zeros_like(vec_ref)

        dh = dh_ref[...]
        xs, rstd = _standardize(x_ref[...])
        vec_ref[0:1, :] += jnp.sum(dh, axis=0, keepdims=True)
        vec_ref[1:2, :] += jnp.sum(dh * xs, axis=0, keepdims=True)
        dx_ref[...] = _standardize_bwd(xs, rstd, dh * (1.0 + sc_ref[...])) + dres_ref[...]

    tile = pl.BlockSpec((tm, d), lambda i: (i, 0))
    return _pcall(body, name=name, grid=(s_len // tm,),
                  out_shape=(jax.ShapeDtypeStruct((s_len, d), F32), jax.ShapeDtypeStruct((8, d), F32)),
                  in_specs=[tile, tile, pl.BlockSpec((1, d), lambda i: (0, 0)), tile],
                  out_specs=(tile, pl.BlockSpec((8, d), lambda i: (0, 0))),
                  semantics=("arbitrary",))(dh, x, scale, dres)


def _wmod_grad(c_t, dmod):
    d = c_t.shape[0]
    n_layers, _, cm = dmod.shape

    def body(c_ref, dm_ref, o_ref):
        for l in range(n_layers):
            acc = None
            for b in range(NDEV):
                term = c_ref[:, b:b + 1] * dm_ref[l, b:b + 1, :]
                acc = term if acc is None else acc + term
            o_ref[l] = acc

    return _pcall(body, name="wmod_grad", out_shape=jax.ShapeDtypeStruct((n_layers, d, cm), F32))(c_t, dmod)


def _sum_adamw(parts, w, m, v, name, first_row=0, into=None, after=None):
    n_src, range_rows, cols = parts.shape
    rows = w.shape[0]
    tr = range_rows
    for cand in (512, 256, 128, 64, 32, 16, 8):
        if range_rows % cand == 0 and cand * cols * 4 <= (2 << 20):
            tr = cand
            break
    first_tile = first_row // tr
    assert first_row % tr == 0
    n_extra = (0 if into is None else 4) + (0 if after is None else 1)

    def body(p_ref, w_ref, m_ref, v_ref, *rest):
        g_ref, d_ref, nm_ref, nv_ref = rest[n_extra:]
        g = p_ref[0].astype(F32)
        for s in range(1, n_src):
            g = g + p_ref[s].astype(F32)
        g_ref[...] = g
        d_ref[...], nm_ref[...], nv_ref[...] = _adamw_step(g, w_ref[...], m_ref[...], v_ref[...])

    tile = pl.BlockSpec((tr, cols), lambda i: (i + first_tile, 0))
    anyspec = pl.BlockSpec(memory_space=pl.ANY)
    out = jax.ShapeDtypeStruct((rows, cols), F32)
    extra = ([] if into is None else list(into)) + ([] if after is None else [after])
    aliases = {} if into is None else {4 + k: k for k in range(4)}
    return _pcall(body, name=name, grid=(range_rows // tr,), out_shape=(out,) * 4,
                  in_specs=[pl.BlockSpec((n_src, tr, cols), lambda i: (0, i, 0)), tile, tile, tile]
                  + [anyspec] * len(extra),
                  out_specs=(tile,) * 4, aliases=aliases, semantics=("arbitrary",))(parts, w, m, v, *extra)


def _adamw_step(g, w, m, v):
    nm = ADAM_B1 * m + (1.0 - ADAM_B1) * g
    nv = ADAM_B2 * v + (1.0 - ADAM_B2) * (g * g)
    m_hat = nm / (1.0 - ADAM_B1 ** ADAM_STEP)
    v_hat = nv / (1.0 - ADAM_B2 ** ADAM_STEP)
    return -ADAM_LR * (m_hat / (jnp.sqrt(v_hat) + ADAM_EPS) + ADAM_WD * w), nm, nv


def _adamw_small(gs, ws, ms, vs):
    n = len(gs)

    def body(*refs):
        for p in range(n):
            results = _adamw_step(*(refs[k * n + p][...] for k in range(4)))
            for k in range(3):
                refs[(4 + k) * n + p][...] = results[k]

    shapes = [jax.ShapeDtypeStruct(w.shape, F32) for w in ws]
    outs = _pcall(body, name="adamw_small", out_shape=shapes * 3)(*gs, *ws, *ms, *vs)
    return [(outs[p], outs[n + p], outs[2 * n + p]) for p in range(n)]


def _sum_parts(parts, name):
    n_src = parts.shape[0]

    def body(p_ref, o_ref):
        acc = p_ref[0]
        for s in range(1, n_src):
            acc = acc + p_ref[s]
        o_ref[...] = acc

    return _pcall(body, name=name, out_shape=jax.ShapeDtypeStruct(parts.shape[1:], F32))(parts)


def _pair_sum(gw, stage, me, name):
    d = gw.shape[0]
    n_slots, _, shard = stage.shape

    def body(me_ref, g_ref, s_ref, own_ref, o_ref):
        del me_ref
        total = (g_ref[...].astype(F32) + s_ref[0].astype(F32)).astype(BF16)
        o_ref[0] = total

        @pl.when(pl.program_id(0) == 0)
        def _():
            own_ref[0] = total

    slot = pl.BlockSpec((1, d, shard), lambda jj, me_ref: (jj, 0, 0))
    out = jax.ShapeDtypeStruct(stage.shape, BF16)
    return pl.pallas_call(
        body, name=name, out_shape=(out, out),
        grid_spec=pltpu.PrefetchScalarGridSpec(
            num_scalar_prefetch=1, grid=(n_slots,),
            in_specs=[pl.BlockSpec((d, shard), lambda jj, me_ref: (0, me_ref[0] ^ (2 * jj))), slot],
            out_specs=(pl.BlockSpec((1, d, shard), lambda jj, me_ref: (0, 0, 0)), slot)),
        compiler_params=pltpu.CompilerParams(dimension_semantics=("arbitrary",), vmem_limit_bytes=VMEM_LIMIT),
        interpret=False)(me.reshape(1).astype(jnp.int32), gw, stage)


def _lower_bound_table(lower_bounds):
    p = jax.nn.softmax(lower_bounds.astype(F32), axis=0)
    return jnp.cumsum(p, axis=0) - p[0:1]


def _pad_rows(v, width):
    n = v.shape[0]
    rows = -(-n // width)
    rows = -(-rows // 8) * 8
    return jnp.pad(v, (0, rows * width - n)).reshape(rows, width)


def kernel(x, c, w_mod, b_mod, w_in, conv_w, hgrn_norm_w, lower_bounds, w_branch, w_out, ln_g, ln_b, loss_target, m_w_mod, m_b_mod, m_w_in, m_conv_w, m_hgrn_norm_w, m_lower_bounds, m_w_branch, m_w_out, m_ln_g, m_ln_b, v_w_mod, v_b_mod, v_w_in, v_conv_w, v_hgrn_norm_w, v_lower_bounds, v_w_branch, v_w_out, v_ln_g, v_ln_b):
    n_layers = N_LAYERS
    s_len, d = x.shape[1], x.shape[2]
    n_cols = w_in.shape[2] * NDEV
    cw_cols = conv_w.shape[2]
    cm = w_mod.shape[2]
    me = _my_index()
    x0 = x[0]
    target = loss_target[0]

    small = _pad_rows(jnp.concatenate([c.reshape(-1), conv_w.reshape(-1)]), BLK)
    small_all = _all_gather_small("gather_c_conv", small).reshape(NDEV, -1)
    c_all = small_all[:, :d]
    conv_full = small_all[:, d:d + n_layers * 3 * cw_cols].reshape(NDEV, n_layers, 3, cw_cols)
    conv_full = conv_full.transpose(1, 2, 0, 3).reshape(n_layers, 3, WIDTH)

    b_mod_mine = lax.dynamic_slice_in_dim(b_mod, me * cm, cm, axis=1).reshape(n_layers, 1, cm)
    mod_cols = _mod_fwd(c_all, w_mod, b_mod_mine)
    mod_all = _all_gather_small("gather_mod", mod_cols.reshape(n_layers * NDEV, cm))
    mod_all = mod_all.reshape(NDEV, n_layers, NDEV, cm)
    mod_mine = lax.dynamic_index_in_dim(mod_all, me, axis=2, keepdims=False)
    mod_mine = mod_mine.transpose(1, 0, 2).reshape(n_layers, 3, 1, d)

    shard = w_in.shape[2]
    dsh = d // NDEV
    w_in_b, w_branch_b, w_out_b = w_in.astype(BF16), w_branch.astype(BF16), w_out.astype(BF16)
    window = lambda ref, dev: ref.at[:, pl.ds(pl.multiple_of(dev * shard, BLK), shard)]

    def two_step_sends(places):
        chips, sibling = [], []
        for k in (1, 2, 4, 6):
            for a, place in enumerate(places):
                chips.append((k, lambda ins, lands, me, a=a: ins[a],
                              lambda lands, me, a=a, place=place: place(lands[a], me),
                              lambda lands, me, a=a, k=k, place=place: place(lands[a], me ^ k)))
        for j in (2, 4, 6):
            for a, place in enumerate(places):
                sibling.append((1, lambda ins, lands, me, a=a, j=j, place=place: place(lands[a], me ^ j),
                                lambda lands, me, a=a, j=j, place=place: place(lands[a], me ^ j),
                                lambda lands, me, a=a, j=j, place=place: place(lands[a], me ^ 1 ^ j)))
        return chips, sibling

    in_sends = two_step_sends([window])
    rest_sends = two_step_sends([_slot, _slot])
    layer_sends = two_step_sends([window, _slot, _slot])

    def in_land(l):
        return _place_own_window(f"place_w_in_{l}", (d, n_cols), w_in_b[l], me)

    def rest_lands(l):
        return [_place_own((NDEV, 3, WIDTH, dsh), BF16, w_branch_b[l][None], (me, 0, 0, 0)),
                _place_own((NDEV, dsh, d), BF16, w_out_b[l][None], (me, 0, 0))]

    def gather_start(name, shards, lands, sends, after):
        return _exchange_start(f"{name}_chips_start", shards, lands, sends[0], after)

    def gather_pass_on(name, started, after, sends):
        _, lands = _exchange_wait(f"{name}_chips_wait", started, after, sends[0])
        return _exchange_start(f"{name}_sibling_start", [], lands, sends[1])

    def gather_finish(name, started, after, sends):
        return _exchange_wait(f"{name}_sibling_wait", started, after, sends[1])[1]

    def branch_out_weights(w_branch_l, w_out_l):
        return w_branch_l.transpose(1, 2, 0, 3).reshape(3, WIDTH, d), w_out_l.reshape(d, d)

    gathering = gather_start("gather_w_in_0", [w_in_b[0]], [in_land(0)], in_sends, mod_mine)
    rest_gathering = gather_start("gather_rest_0", [w_branch_b[0], w_out_b[0]], rest_lands(0), rest_sends,
                                  gathering[4])
    next_gathering = None
    if n_layers > 1:
        next_gathering = gather_start("gather_weights_1", [w_in_b[1], w_branch_b[1], w_out_b[1]],
                                      [in_land(1)] + rest_lands(1), layer_sends, rest_gathering[4])
    passing = gather_pass_on("gather_w_in_0", gathering, (next_gathering or rest_gathering)[4], in_sends)
    w_in_l = gather_finish("gather_w_in_0", passing, passing[4], in_sends)[0]

    lbs = _lower_bound_table(lower_bounds)
    norm_w4 = jnp.tile(hgrn_norm_w, (1, WIDTH // HG_HEAD_DIM))

    saved = []
    xl = x0
    for l in range(n_layers):
        shift, scale, gate = mod_mine[l, 0], mod_mine[l, 1], mod_mine[l, 2]
        proj, h_t = _ln_proj(xl, shift, scale, w_in_l, f"ln_proj_{l}")
        o_a, totals = _sb_fwd(proj, f"sb_fwd_{l}")
        if l == 0:
            rest_passing = gather_pass_on("gather_rest_0", rest_gathering, o_a, rest_sends)
        lb_l = lbs[l:l + 1] + rest_passing[4][0, 0] if l == 0 else lbs[l:l + 1]
        o_b = _hgrn_fwd(proj, lb_l, f"hgrn_fwd_{l}")
        if l == 0:
            wb_l, wo_l = branch_out_weights(*gather_finish("gather_rest_0", rest_passing, o_b, rest_sends))
            if n_layers > 1:
                next_passing = gather_pass_on("gather_weights_1", next_gathering, o_b, layer_sends)
                gate = gate + next_passing[4][0, 0]
        x_new, merged, ycat, *loss_term = _merge_fwd(
            xl, proj, o_a, o_b, gate, norm_w4[l:l + 1], conv_full[l], wb_l, wo_l, ln_g[l:l + 1], ln_b[l:l + 1],
            f"merge_fwd_{l}", target=target if l == n_layers - 1 else None)
        saved.append((xl, proj, h_t, o_a, totals, o_b, merged, ycat, w_in_l, wb_l, wo_l))
        if l == 0 and n_layers > 1:
            w_in_l, w_branch_l, w_out_l = gather_finish("gather_weights_1", next_passing, x_new, layer_sends)
            wb_l, wo_l = branch_out_weights(w_branch_l, w_out_l)
        xl = x_new

    dx, loss_part = xl, loss_term[0]

    pair_sends = [(1, lambda ins, lands, me, j=j: window(ins[0], me ^ 1 ^ j),
                   lambda lands, me, jj=jj: lands[0].at[jj], lambda lands, me, jj=jj: lands[0].at[jj])
                  for jj, j in enumerate((0, 2, 4, 6))]
    chip_sum_sends = [(j, lambda ins, lands, me, jj=jj: ins[0].at[jj],
                       lambda lands, me, jj=jj: lands[0].at[jj], lambda lands, me, jj=jj: lands[0].at[jj])
                      for jj, j in ((1, 2), (2, 4), (3, 6))]
    rest_scatter = _direct_sends([(0, 0, _slot, _slot), (1, 1, _slot, _slot)])
    scattering = [None] * n_layers
    small_grads = [None] * n_layers
    dmod = [None] * n_layers
    tie = None
    for l in reversed(range(n_layers)):
        xl, proj, h_t, o_a, totals, o_b, merged, ycat, w_in_l, wb_l, wo_l = saved[l]
        scale, gate = mod_mine[l, 1], mod_mine[l, 2]
        if tie is not None:
            gate = gate + tie[0, 0]
        dres, dycat, dproj, gwo_by_owner, gwb_by_owner, mvec = _merge_bwd(
            dx, xl, merged, ycat, proj, gate, wb_l, wo_l, ln_g[l:l + 1], f"merge_bwd_{l}")
        lands = [_place_own((NDEV, 3, WIDTH, dsh), BF16, lax.dynamic_slice_in_dim(gwb_by_owner, me, 1, axis=0),
                            (me, 0, 0, 0)),
                 _place_own((NDEV, dsh, d), BF16, lax.dynamic_slice_in_dim(gwo_by_owner, me, 1, axis=0),
                            (me, 0, 0))]
        rest_started = _exchange_start(f"scatter_rest_{l}_start", [gwb_by_owner, gwo_by_owner], lands, rest_scatter)
        dproj, do_a, do_b, bvec = _branch_bwd(dycat, proj, o_a, o_b, norm_w4[l:l + 1] + rest_started[4][0, 0],
                                              conv_full[l], dproj, f"branch_bwd_{l}")
        dproj = _sb_bwd(proj, do_a, totals, dproj, f"sb_bwd_{l}")
        dproj, dlb = _hgrn_bwd(proj, do_b, lbs[l:l + 1], dproj, f"hgrn_bwd_{l}")
        gwi = _gw_matmul(h_t, dproj, f"gw_matmul_{l}")
        swapping = _exchange_start(f"scatter_in_{l}_sibling_start", [gwi], [lax.empty((4, d, shard), BF16)], pair_sends)
        if l > 0:
            dh = _dh_matmul(dproj, w_in_l, swapping[4], f"dh_matmul_{l}")
        (gwi,), (stage,) = _exchange_wait(f"scatter_in_{l}_sibling_wait", swapping, dh if l > 0 else swapping[4],
                                          pair_sends)
        land, chip_sums = _pair_sum(gwi, stage, me, f"pair_sum_{l}")
        in_started = _exchange_start(f"scatter_in_{l}_chips_start", [chip_sums], [land], chip_sum_sends)
        scattering[l] = (in_started, rest_started)
        tie = in_started[4]
        if l == 0:
            dh = _dh_matmul(dproj, w_in_l, tie, f"dh_matmul_{l}")
        dx, lvec = _ln_bwd(dh, xl, scale + tie[0, 0], dres, f"ln_bwd_{l}")
        dmod[l] = jnp.concatenate([lvec[0], lvec[1], mvec[2]])
        norm_grad = bvec[0].reshape(WIDTH // HG_HEAD_DIM, HG_HEAD_DIM).sum(axis=0)
        small_grads[l] = jnp.concatenate([mvec[0], mvec[1], norm_grad, dlb[0], bvec[1:4].reshape(-1)])
    grad_x = dx[None]

    flat = lambda a: a.reshape(-1, a.shape[-1])
    big = {"w_in": (w_in, m_w_in, v_w_in), "w_branch": (w_branch, m_w_branch, v_w_branch),
           "w_out": (w_out, m_w_out, v_w_out)}
    big_results = {n: None for n in big}

    def adam_layer(l, after):
        in_started, rest_started = scattering[l]
        p_branch_l, p_out_l = _exchange_wait(f"scatter_rest_{l}_wait", rest_started, after, rest_scatter)[1]
        p_in_l = _exchange_wait(f"scatter_in_{l}_chips_wait", in_started, after, chip_sum_sends)[1][0]
        parts = {"w_in": p_in_l, "w_branch": p_branch_l.reshape(NDEV, 3 * WIDTH, dsh), "w_out": p_out_l}
        last = None
        for n, (w, m, v) in big.items():
            rows_per_layer = flat(w).shape[0] // n_layers
            big_results[n] = _sum_adamw(parts[n], flat(w), flat(m), flat(v), f"adamw_{n}_{l}",
                                        first_row=l * rows_per_layer, into=big_results[n], after=last)
            last = big_results[n][3]
        return last

    after_adam = None
    for l in reversed(range(1, n_layers)):
        after_adam = adam_layer(l, tie)

    small_vec = jnp.concatenate(dmod + small_grads + [loss_part.reshape(1)])
    n_small = small_vec.shape[0]
    small_all = _all_gather_small("gather_small_grads", _pad_rows(small_vec, BLK), after=after_adam)
    small_sum = _sum_parts(small_all, "sum_small_grads").reshape(-1)[:n_small]
    dmod_all = small_all.reshape(NDEV, -1)[:, :n_layers * 3 * d].reshape(NDEV, n_layers, 3 * d)

    loss = small_sum[n_small - 1]

    off = n_layers * 3 * d
    grad_b_mod = small_sum[:off].reshape(n_layers, 3 * d)
    per_layer = 2 * d + HG_HEAD_DIM + WIDTH + 3 * WIDTH
    g_ln_g, g_ln_b, g_norm, g_lbs, g_conv = [], [], [], [], []
    for l in range(n_layers):
        seg = small_sum[off + l * per_layer: off + (l + 1) * per_layer]
        g_ln_g.append(seg[:d])
        g_ln_b.append(seg[d:2 * d])
        g_norm.append(seg[2 * d:2 * d + HG_HEAD_DIM])
        g_lbs.append(seg[2 * d + HG_HEAD_DIM:2 * d + HG_HEAD_DIM + WIDTH])
        g_conv.append(seg[2 * d + HG_HEAD_DIM + WIDTH:].reshape(3, WIDTH))
    grad_ln_g, grad_ln_b = jnp.stack(g_ln_g), jnp.stack(g_ln_b)
    grad_norm = jnp.stack(g_norm)
    _, lbs_vjp = jax.vjp(_lower_bound_table, lower_bounds)
    grad_lower = lbs_vjp(jnp.stack(g_lbs))[0]
    grad_conv = lax.dynamic_slice_in_dim(jnp.stack(g_conv), me * cw_cols, cw_cols, axis=2)

    dmod_mine = lax.dynamic_slice_in_dim(dmod_all, me * cm, cm, axis=2).transpose(1, 0, 2)
    grad_w_mod = _wmod_grad(c_all.T, dmod_mine)

    adam_layer(0, grad_w_mod)
    r_w_in, r_w_branch, r_w_out = ([o.reshape(big[n][0].shape) for o in big_results[n]]
                                   for n in ("w_in", "w_branch", "w_out"))
    r_w_mod = [o.reshape(w_mod.shape) for o in
               _sum_adamw(grad_w_mod.reshape(1, -1, cm), flat(w_mod), flat(m_w_mod), flat(v_w_mod), "adamw_w_mod")]

    small_names = ["b_mod", "conv_w", "hgrn_norm_w", "lower_bounds", "ln_g", "ln_b"]
    small_g = [grad_b_mod, grad_conv, grad_norm, grad_lower, grad_ln_g, grad_ln_b]
    small_w = [b_mod, conv_w, hgrn_norm_w, lower_bounds, ln_g, ln_b]
    small_m = [m_b_mod, m_conv_w, m_hgrn_norm_w, m_lower_bounds, m_ln_g, m_ln_b]
    small_v = [v_b_mod, v_conv_w, v_hgrn_norm_w, v_lower_bounds, v_ln_g, v_ln_b]
    as_rows = lambda a: a.reshape(-1, a.shape[-1])
    updates = _adamw_small([as_rows(a) for a in small_g], [as_rows(a) for a in small_w],
                           [as_rows(a) for a in small_m], [as_rows(a) for a in small_v])
    r_small = {n: [g] + [u.reshape(w.shape) for u in upd]
               for n, g, w, upd in zip(small_names, small_g, small_w, updates)}

    results = {"w_mod": r_w_mod, "w_in": r_w_in, "w_branch": r_w_branch, "w_out": r_w_out, **r_small}
    order = ["w_mod", "b_mod", "w_in", "conv_w", "hgrn_norm_w", "lower_bounds", "w_branch", "w_out", "ln_g", "ln_b"]
    outs = [loss, grad_x]
    for idx in range(4):
        outs.extend(results[n][idx] for n in order)
    return tuple(outs)
```

```python
import jax
import jax.numpy as jnp
from jax import lax
from jax.experimental import pallas as pl
from jax.experimental.pallas import tpu as pltpu

F32 = jnp.float32
BF16 = jnp.bfloat16
NDEV = 8
N_LAYERS = 2
SB_HEAD_DIM = 64
HG_HEAD_DIM = 128
WIDTH = 512
BLK = 128
LN_EPS = 1e-5
RMS_EPS = 1e-6
ALPHA = (2.0 * N_LAYERS) ** 0.25
ADAM_LR, ADAM_B1, ADAM_B2, ADAM_EPS, ADAM_WD, ADAM_STEP = 0.001, 0.9, 0.999, 1e-08, 0.01, 10
VMEM_LIMIT = 56 * 1024 * 1024
MESH = pl.DeviceIdType.MESH
HG_LEVELS = (64, 32, 16, 8, 4, 2, 1)


def _pcall(body, *, name, out_shape, grid=None, in_specs=None, out_specs=None, scratch_shapes=(),
           semantics=None, aliases=None, after=None):
    if after is not None:
        n_in = len(in_specs)
        inner = body
        body = lambda *refs: inner(*refs[:n_in], *refs[n_in + 1:])
        in_specs = list(in_specs) + [pl.BlockSpec(memory_space=pl.ANY)]
    kwargs = {}
    if grid is not None:
        kwargs["grid"] = grid
    if in_specs is not None:
        kwargs["in_specs"] = in_specs
    if out_specs is not None:
        kwargs["out_specs"] = out_specs
    if aliases:
        kwargs["input_output_aliases"] = aliases
    call = pl.pallas_call(
        body, name=name, out_shape=out_shape, scratch_shapes=list(scratch_shapes),
        compiler_params=pltpu.CompilerParams(dimension_semantics=semantics, vmem_limit_bytes=VMEM_LIMIT),
        interpret=False, **kwargs)
    return call if after is None else (lambda *operands: call(*operands, after))


def _dot(a, b):
    return jnp.dot(a, b, preferred_element_type=F32)


def _dot_nt(a, b):
    return lax.dot_general(a, b, (((1,), (1,)), ((), ())), preferred_element_type=F32)


def _dot_tn(a, b):
    return lax.dot_general(a, b, (((0,), (0,)), ((), ())), preferred_element_type=F32)


def _dot_01_l(m_bf16, x):
    x1 = x.astype(BF16)
    x2 = (x - x1.astype(F32)).astype(BF16)
    return _dot(jnp.concatenate([m_bf16, m_bf16], axis=1), jnp.concatenate([x1, x2], axis=0))


def _sigmoid(x):
    return 1.0 / (1.0 + jnp.exp(-x))


def _silu_and_grad(x):
    s = _sigmoid(x)
    return x * s, s * (1.0 + x * (1.0 - s))


LOG2E = 1.4426950408889634
MASKED_SCORE = -1e30


def _softplus2_parts(z2):
    minus_abs = lax.bitcast_convert_type(lax.bitcast_convert_type(z2, jnp.int32) | jnp.int32(-2 ** 31), F32)
    sp2 = jnp.maximum(z2, 0.0) + jnp.log2(1.0 + jnp.exp2(minus_abs))
    return sp2, jnp.exp2(z2 - sp2)


def _split2_lanes(x):
    x1 = x.astype(BF16)
    return jnp.concatenate([x1, (x - x1.astype(F32)).astype(BF16)], axis=1)


def _iota2(shape, dim):
    return lax.broadcasted_iota(jnp.int32, shape, dim)


def _standardize(x):
    mu = jnp.mean(x, axis=-1, keepdims=True)
    xc = x - mu
    var = jnp.mean(xc * xc, axis=-1, keepdims=True)
    rstd = lax.rsqrt(var + LN_EPS)
    return xc * rstd, rstd


def _standardize_bwd(xhat, rstd, dxhat):
    m1 = jnp.mean(dxhat, axis=-1, keepdims=True)
    m2 = jnp.mean(dxhat * xhat, axis=-1, keepdims=True)
    return rstd * (dxhat - m1 - xhat * m2)


def _my_index():
    return 4 * lax.axis_index("x") + 2 * lax.axis_index("y") + lax.axis_index("c")


def _exchange(name, ins, out_shapes, transfers, in_vmem, after=None):
    n_in, n_out, n_t = len(ins), len(out_shapes), len(transfers)

    def body(*refs):
        n_skip = n_in + (0 if after is None else 1)
        in_refs, out_refs = refs[:n_in], refs[n_skip:n_skip + n_out]
        send_sems, recv_sems, local_sems = refs[n_skip + n_out:]
        x, y, c = lax.axis_index("x"), lax.axis_index("y"), lax.axis_index("c")
        me = 4 * x + 2 * y + c
        started = []
        for t, (i, o, src_fn, dst_fn) in enumerate(transfers):
            own = pltpu.make_async_copy(src_fn(in_refs[i], me), dst_fn(out_refs[o], me), local_sems.at[t])
            own.start()
            started.append(own)
        arrivals = []
        for k in range(1, NDEV):
            px = x ^ ((k >> 2) & 1)
            py = y ^ ((k >> 1) & 1)
            pc = c ^ (k & 1)
            peer = 4 * px + 2 * py + pc
            for t, (i, o, src_fn, dst_fn) in enumerate(transfers):
                sem = t * (NDEV - 1) + k - 1
                push = pltpu.make_async_remote_copy(
                    src_ref=src_fn(in_refs[i], peer), dst_ref=dst_fn(out_refs[o], me),
                    send_sem=send_sems.at[sem], recv_sem=recv_sems.at[sem],
                    device_id=(px, py, pc), device_id_type=MESH)
                push.start()
                started.append(push)
                arrivals.append(pltpu.make_async_remote_copy(
                    src_ref=src_fn(in_refs[i], peer), dst_ref=dst_fn(out_refs[o], peer),
                    send_sem=send_sems.at[sem], recv_sem=recv_sems.at[sem],
                    device_id=(px, py, pc), device_id_type=MESH))
        for arrival in arrivals:
            arrival.wait_recv()
        for cp in started[n_t:]:
            cp.wait_send()
        for own in started[:n_t]:
            own.wait()

    space = pltpu.VMEM if in_vmem else pl.ANY
    spec = pl.BlockSpec(memory_space=space)
    extra = [] if after is None else [after]
    return _pcall(
        body, name=name, out_shape=out_shapes,
        in_specs=[spec] * n_in + [pl.BlockSpec(memory_space=pl.ANY)] * len(extra), out_specs=[spec] * n_out,
        scratch_shapes=[pltpu.SemaphoreType.DMA((n_t * (NDEV - 1),)),
                        pltpu.SemaphoreType.DMA((n_t * (NDEV - 1),)),
                        pltpu.SemaphoreType.DMA((n_t,))])(*ins, *extra)


def _whole(ref, dev):
    return ref


def _slot(ref, dev):
    return ref.at[dev]


def _all_gather_small(name, v, after=None):
    out = _exchange(name, [v], [jax.ShapeDtypeStruct((NDEV,) + v.shape, v.dtype)],
                    [(0, 0, _whole, _slot)], in_vmem=True, after=after)
    return out[0]


_HBM_SPEC = pl.BlockSpec(memory_space=pltpu.HBM)
_SEM_SPEC = pl.BlockSpec(memory_space=pltpu.SEMAPHORE)
_DATAFLOW = pltpu.SideEffectType.DATAFLOW_SIDE_EFFECTING


def _peer(x, y, c, k):
    px = x ^ ((k >> 2) & 1)
    py = y ^ ((k >> 1) & 1)
    pc = c ^ (k & 1)
    return (px, py, pc), 4 * px + 2 * py + pc


def _direct_sends(transfers):
    sends = []
    for k in range(1, NDEV):
        for i, o, src_fn, dst_fn in transfers:
            sends.append((k,
                          lambda ins, lands, me, i=i, k=k, src_fn=src_fn: src_fn(ins[i], me ^ k),
                          lambda lands, me, o=o, dst_fn=dst_fn: dst_fn(lands[o], me),
                          lambda lands, me, o=o, k=k, dst_fn=dst_fn: dst_fn(lands[o], me ^ k)))
    return sends


def _exchange_start(name, ins, lands, sends, after=None):
    n_in, n_buf = len(ins), len(ins) + len(lands)
    n_sem = len(sends)

    def body(*refs):
        in_refs, land_refs = refs[:n_in], refs[n_in:n_buf]
        n_skip = n_buf + (0 if after is None else 1)
        send_sems, recv_sems, token = refs[n_skip], refs[n_skip + 1], refs[-1]
        x, y, c = lax.axis_index("x"), lax.axis_index("y"), lax.axis_index("c")
        me = 4 * x + 2 * y + c
        for t, (k, src_fn, dst_fn, _) in enumerate(sends):
            pltpu.make_async_remote_copy(
                src_ref=src_fn(in_refs, land_refs, me), dst_ref=dst_fn(land_refs, me),
                send_sem=send_sems.at[t], recv_sem=recv_sems.at[t],
                device_id=_peer(x, y, c, k)[0], device_id_type=MESH).start()
        token[...] = jnp.zeros_like(token)

    bufs = [pltpu.with_memory_space_constraint(a, pltpu.HBM) for a in list(ins) + list(lands)]
    extra = [] if after is None else [after]
    outs = pl.pallas_call(
        body, name=name,
        out_shape=(pltpu.SemaphoreType.DMA((n_sem,)), pltpu.SemaphoreType.DMA((n_sem,)))
        + tuple(pltpu.HBM(a.shape, a.dtype) for a in bufs) + (jax.ShapeDtypeStruct((8, BLK), F32),),
        in_specs=[_HBM_SPEC] * n_buf + [pl.BlockSpec(memory_space=pl.ANY)] * len(extra),
        out_specs=(_SEM_SPEC, _SEM_SPEC) + (_HBM_SPEC,) * n_buf + (pl.BlockSpec(memory_space=pltpu.VMEM),),
        input_output_aliases={b: 2 + b for b in range(n_buf)},
        compiler_params=pltpu.CompilerParams(has_side_effects=_DATAFLOW),
        interpret=False)(*bufs, *extra)
    return outs[0], outs[1], list(outs[2:2 + n_in]), list(outs[2 + n_in:2 + n_buf]), outs[-1]


def _exchange_wait(name, started, after, sends):
    send_sems, recv_sems, ins, lands, _ = started
    n_in, n_buf = len(ins), len(ins) + len(lands)

    def body(*refs):
        in_refs, land_refs = refs[:n_in], refs[n_in:n_buf]
        send_sems, recv_sems = refs[n_buf], refs[n_buf + 1]
        x, y, c = lax.axis_index("x"), lax.axis_index("y"), lax.axis_index("c")
        me = 4 * x + 2 * y + c
        for t, (k, src_fn, _, rcv_fn) in enumerate(sends):
            cp = pltpu.make_async_remote_copy(
                src_ref=src_fn(in_refs, land_refs, me), dst_ref=rcv_fn(land_refs, me),
                send_sem=send_sems.at[t], recv_sem=recv_sems.at[t],
                device_id=_peer(x, y, c, k)[0], device_id_type=MESH)
            cp.wait_send()
            cp.wait_recv()

    bufs = list(ins) + list(lands)
    outs = pl.pallas_call(
        body, name=name, out_shape=tuple(pltpu.HBM(a.shape, a.dtype) for a in bufs),
        in_specs=[_HBM_SPEC] * n_buf + [_SEM_SPEC, _SEM_SPEC, pl.BlockSpec(memory_space=pl.ANY)],
        out_specs=(_HBM_SPEC,) * n_buf,
        input_output_aliases={b: b for b in range(n_buf)},
        compiler_params=pltpu.CompilerParams(has_side_effects=_DATAFLOW),
        interpret=False)(*bufs, send_sems, recv_sems, after)
    return list(outs[:n_in]), list(outs[n_in:])


def _place_own(shape, dtype, own, start):
    return lax.dynamic_update_slice(lax.empty(shape, dtype), own, start)


def _place_own_window(name, shape, own, me):
    rows, cols = own.shape

    def body(me_ref, zone_in, own_ref, zone_ref):
        del me_ref, zone_in
        zone_ref[...] = own_ref[...]

    return pl.pallas_call(
        body, name=name, out_shape=jax.ShapeDtypeStruct(shape, own.dtype),
        grid_spec=pltpu.PrefetchScalarGridSpec(
            num_scalar_prefetch=1, grid=(1,),
            in_specs=[pl.BlockSpec(memory_space=pl.ANY), pl.BlockSpec((rows, cols), lambda i, me_ref: (0, 0))],
            out_specs=pl.BlockSpec((rows, cols), lambda i, me_ref: (0, me_ref[0]))),
        input_output_aliases={1: 0},
        compiler_params=pltpu.CompilerParams(dimension_semantics=("arbitrary",), vmem_limit_bytes=VMEM_LIMIT),
        interpret=False)(me.reshape(1).astype(jnp.int32), lax.empty(shape, own.dtype), own)


def _mod_fwd(c_all, w_mod, b_mod_mine):
    n_layers, _, cm = w_mod.shape

    def body(c_ref, w_ref, b_ref, o_ref):
        for l in range(n_layers):
            o_ref[l] = jnp.dot(c_ref[...], w_ref[l], preferred_element_type=F32,
                               precision=lax.Precision.HIGHEST) + b_ref[l]

    return _pcall(body, name="mod_fwd", out_shape=jax.ShapeDtypeStruct((n_layers, NDEV, cm), F32))(
        c_all, w_mod, b_mod_mine)


def _ln_proj(x, shift, scale, w_full, name):
    s_len, d = x.shape
    n = w_full.shape[1]
    tm = min(1024, s_len)
    tn = 2304

    def body(x_ref, sh_ref, sc_ref, w_ref, proj_ref, ht_ref, h_scr):
        @pl.when(pl.program_id(1) == 0)
        def _():
            xs, _ = _standardize(x_ref[...])
            h = xs * (1.0 + sc_ref[...]) + sh_ref[...]
            h_scr[...] = h.astype(BF16)
            ht_ref[...] = h.T.astype(BF16)

        proj_ref[...] = _dot(h_scr[...], w_ref[...])

    return _pcall(
        body, name=name,
        out_shape=(jax.ShapeDtypeStruct((s_len, n), F32), jax.ShapeDtypeStruct((d, s_len), BF16)),
        grid=(s_len // tm, n // tn),
        in_specs=[pl.BlockSpec((tm, d), lambda i, j: (i, 0)),
                  pl.BlockSpec((1, d), lambda i, j: (0, 0)),
                  pl.BlockSpec((1, d), lambda i, j: (0, 0)),
                  pl.BlockSpec((d, tn), lambda i, j: (0, j))],
        out_specs=(pl.BlockSpec((tm, tn), lambda i, j: (i, j)),
                   pl.BlockSpec((d, tm), lambda i, j: (0, i))),
        scratch_shapes=[pltpu.VMEM((tm, d), BF16)],
        semantics=("arbitrary", "arbitrary"))(x, shift, scale, w_full)


SB_Q_ROWS = 256
SB_K_BLOCKS = 2


def _sb_fwd(proj, name):
    s_len = proj.shape[0]
    n_pairs = WIDTH // BLK
    qr = min(SB_Q_ROWS, s_len)
    gb = SB_K_BLOCKS
    kw = gb * BLK
    nq = s_len // qr
    assert qr == kw

    def body(q_ref, k_ref, v_ref, o_ref, tot_ref):
        lane = _iota2((1, BLK), 1)
        row = _iota2((BLK, BLK), 0)
        col = _iota2((BLK, BLK), 1)
        half = jnp.concatenate([(row >= col).astype(BF16), jnp.ones((BLK, BLK), BF16)], axis=1)
        suffix_and_sum = jnp.concatenate([half, half], axis=0)
        strict = _iota2((qr, kw), 1) < _iota2((qr, kw), 0)
        head_lanes = [(lane // SB_HEAD_DIM) == hh for hh in range(2)]

        def scores(gi, qms, masked):
            c0 = pl.multiple_of(gi * kw, kw)
            kb = k_ref[pl.ds(c0, kw), :].astype(BF16)
            z2s = [_dot_nt(qms[hh], kb) for hh in range(2)]
            if masked:
                z2s = [jnp.where(strict, z2, MASKED_SCORE) for z2 in z2s]
            return tuple(z2s)

        def accumulate(gi, z2s, carry):
            c0 = pl.multiple_of(gi * kw, kw)
            vb = v_ref[pl.ds(c0, kw), :].astype(BF16)
            sp2s = [_softplus2_parts(z2)[0] for z2 in z2s]
            terms = [[_split2_lanes(sp2[:, b * BLK:(b + 1) * BLK]) for b in range(gb)] for sp2 in sp2s]
            sums = [[_dot(t, suffix_and_sum) for t in head_terms] for head_terms in terms]
            weights, laters = [], []
            for hh in range(2):
                later = carry[2 * hh + 1]
                parts = [None] * gb
                for b in reversed(range(gb)):
                    parts[b] = sums[hh][b][:, :BLK] + later
                    later = later + sums[hh][b][:, BLK:]
                weights.append(jnp.exp2(z2s[hh] - jnp.concatenate(parts, axis=1)).astype(BF16))
                laters.append(later)
            outs = [_dot(weights[hh], vb) for hh in range(2)]
            return (carry[0] + outs[0], laters[0], carry[2] + outs[1], laters[1])

        def queries(i):
            qf = q_ref[pl.ds(pl.multiple_of(i * qr, qr), qr), :] * (SB_HEAD_DIM ** -0.5 * LOG2E)
            return [jnp.where(head_lanes[hh], qf, 0.0).astype(BF16) for hh in range(2)]

        def qtile(i, first_scores):
            r0 = pl.multiple_of(i * qr, qr)
            qms = queries(i)
            zero = jnp.zeros((qr, BLK), F32)

            def step(jj, state):
                gi = i - 1 - jj
                return scores(gi, qms, False) + accumulate(gi + 1, state[:2], state[2:])

            state = lax.fori_loop(0, i, step, first_scores + (zero,) * 4)
            nxt = jnp.minimum(i + 1, nq - 1)
            next_scores = scores(nxt, queries(nxt), True)
            carry = accumulate(0, state[:2], state[2:])
            o_ref[pl.ds(r0, qr), :] = jnp.where(head_lanes[0], carry[0], carry[2])
            tot_ref[0, pl.ds(r0, qr), :] = carry[1]
            tot_ref[1, pl.ds(r0, qr), :] = carry[3]
            return next_scores

        lax.fori_loop(0, nq, qtile, scores(0, queries(0), True))

    col_spec = lambda off: pl.BlockSpec((s_len, BLK), lambda p: (0, off + p))
    return _pcall(
        body, name=name,
        out_shape=(jax.ShapeDtypeStruct((s_len, WIDTH), F32),
                   jax.ShapeDtypeStruct((2 * n_pairs, s_len, BLK), F32)),
        grid=(n_pairs,),
        in_specs=[col_spec(0), col_spec(n_pairs), col_spec(2 * n_pairs)],
        out_specs=(pl.BlockSpec((s_len, BLK), lambda p: (0, p)),
                   pl.BlockSpec((2, s_len, BLK), lambda p: (p, 0, 0))),
        semantics=("arbitrary",))(proj, proj, proj)


def _hg_masks(mask_ref):
    row = _iota2((BLK, BLK), 0)
    col = _iota2((BLK, BLK), 1)
    for v, m in enumerate(HG_LEVELS):
        same = (row // (2 * m)) == (col // (2 * m))
        mask_ref[v] = (same & ((row & m) != 0) & ((col & m) == 0)).astype(F32)


def _hg_mid(b, m):
    if m >= 4:
        n = BLK // (2 * m)
        mid = b.reshape(n, 2 * m, BLK)[:, m - 1:m, :]
        return jnp.broadcast_to(mid, (n, 2 * m, BLK)).reshape(BLK, BLK)
    pos = _iota2((BLK, BLK), 0) & (2 * m - 1)
    out = b
    for p in range(2 * m):
        delta = (m - 1) - p
        if delta != 0:
            out = jnp.where(pos == p, pltpu.roll(b, (-delta) % BLK, 0), out)
    return out


def _hg_chunk_inputs(qraw, fpre, lb):
    sig = _sigmoid(fpre)
    f = lb + (1.0 - lb) * sig
    g = jnp.log(f)
    q, dq_fac = _silu_and_grad(qraw)
    return q, dq_fac, f, sig, g


HG_GROUP = 4


def _neg_abs(x):
    return lax.bitcast_convert_type(lax.bitcast_convert_type(x, jnp.int32) | jnp.int32(-2 ** 31), F32)


def _hg_level_terms(qs, ks, bs, m):
    es = [jnp.exp(_neg_abs(b - _hg_mid(b, m))) for b in bs]
    qts = [(q * e).astype(BF16) for q, e in zip(qs, es)]
    kts = [(k * e).astype(BF16) for k, e in zip(ks, es)]
    return es, qts, kts


def _hg_load(refs, r0, lb_v, lower_incl):
    q_ref, f_ref, i_ref = refs
    heads = []
    for h in range(HG_GROUP):
        sl = slice(h * HG_HEAD_DIM, (h + 1) * HG_HEAD_DIM)
        heads.append(_hg_chunk_inputs(q_ref[pl.ds(r0, BLK), sl], f_ref[pl.ds(r0, BLK), sl], lb_v[:, sl])
                     + (i_ref[pl.ds(r0, BLK), sl],))
    bs = [_dot_01_l(lower_incl, hd[4]) for hd in heads]
    return heads, bs


def _hgrn_fwd(proj, lb, name, after=None):
    s_len = proj.shape[0]
    nc = s_len // BLK
    gw = HG_GROUP * HG_HEAD_DIM
    n_groups = WIDTH // gw
    base = 4 * WIDTH // gw

    def body(q_ref, f_ref, i_ref, lb_ref, o_ref, mask_ref):
        _hg_masks(mask_ref)
        row = _iota2((BLK, BLK), 0)
        col = _iota2((BLK, BLK), 1)
        lower_incl = (col <= row).astype(BF16)
        lb_v = lb_ref[...]

        def chunk(ci, sts):
            r0 = pl.multiple_of(ci * BLK, BLK)
            heads, bs = _hg_load((q_ref, f_ref, i_ref), r0, lb_v, lower_incl)
            qs = [hd[0] for hd in heads]
            ks = [1.0 - hd[2] for hd in heads]
            vs = [hd[5] for hd in heads]
            vbs = [v.astype(BF16) for v in vs]
            b_ends = [b[BLK - 1:BLK, :] for b in bs]
            inters = [_dot_nt((q * jnp.exp(b)).astype(BF16), st.astype(BF16)) for q, b, st in zip(qs, bs, sts)]
            scs = [None] * HG_GROUP
            for v_idx, m in enumerate(HG_LEVELS):
                _, qts, kts = _hg_level_terms(qs, ks, bs, m)
                terms = [_dot_nt(qt, kt) for qt, kt in zip(qts, kts)]
                msk = mask_ref[v_idx]
                scs = [t * msk if sc is None else sc + t * msk for sc, t in zip(scs, terms)]
            intras = [_dot(sc.astype(BF16), vb) for sc, vb in zip(scs, vbs)]
            k_decs = [(k * jnp.exp(b_end - b)).astype(BF16) for k, b, b_end in zip(ks, bs, b_ends)]
            grown = [_dot_tn(vb, k_dec) for vb, k_dec in zip(vbs, k_decs)]
            for h in range(HG_GROUP):
                diag = jnp.sum(qs[h] * ks[h], axis=-1, keepdims=True)
                o_ref[pl.ds(r0, BLK), h * HG_HEAD_DIM:(h + 1) * HG_HEAD_DIM] = inters[h] + intras[h] + diag * vs[h]
            return tuple(st * jnp.exp(b_end) + g for st, b_end, g in zip(sts, b_ends, grown))

        lax.fori_loop(0, nc, chunk, (jnp.zeros((HG_HEAD_DIM, HG_HEAD_DIM), F32),) * HG_GROUP)

    col_spec = lambda off: pl.BlockSpec((s_len, gw), lambda h: (0, off + h))
    return _pcall(
        body, name=name, out_shape=jax.ShapeDtypeStruct((s_len, WIDTH), F32),
        grid=(n_groups,),
        in_specs=[col_spec(base), col_spec(base + n_groups), col_spec(base + 2 * n_groups),
                  pl.BlockSpec((1, gw), lambda h: (0, h))],
        out_specs=pl.BlockSpec((s_len, gw), lambda h: (0, h)),
        scratch_shapes=[pltpu.VMEM((len(HG_LEVELS), BLK, BLK), F32)],
        semantics=("arbitrary",), after=after)(proj, proj, proj, lb)


def _rms_heads(o_b, norm_w):
    n_parts, h_parts, r_parts = [], [], []
    for h in range(WIDTH // HG_HEAD_DIM):
        sl = slice(h * HG_HEAD_DIM, (h + 1) * HG_HEAD_DIM)
        o = o_b[:, sl]
        rstd = lax.rsqrt(jnp.mean(o * o, axis=-1, keepdims=True) + RMS_EPS)
        ohat = o * rstd
        h_parts.append(ohat)
        n_parts.append(ohat * norm_w[:, sl])
        r_parts.append(jnp.broadcast_to(rstd, o.shape))
    cat = lambda parts: jnp.concatenate(parts, axis=-1)
    return cat(n_parts), cat(h_parts), cat(r_parts)


def _shift_rows_down(halo, cur, k):
    tm = cur.shape[0]
    ext = jnp.concatenate([halo, cur], axis=0)
    return pltpu.roll(ext, k, 0)[8:8 + tm]


def _shift_rows_up(cur, halo, k):
    tm = cur.shape[0]
    ext = jnp.concatenate([cur, halo], axis=0)
    return pltpu.roll(ext, (tm + 8 - k) % (tm + 8), 0)[0:tm]


def _merge_fwd(x, proj, o_a, o_b, gate, norm_w, conv_w, wb, w_out, ln_g, ln_b, name, target=None, after=None):
    s_len, d = x.shape
    tm = min(256, s_len)
    hb = tm // 8
    n_in = 19 + (0 if target is None else 1)

    def body(*refs):
        (x_ref, oa_ref, za_ref, ob_ref, zb_ref, pre_ref, post_ref, u_ref, zc_ref, hpre_ref, hu_ref, g_ref,
         gate_ref, nw_ref, cw_ref, wb_ref, wo_ref, lg_ref, lbias_ref) = refs[:19]
        xn_ref, mg_ref, yc_ref = refs[n_in:n_in + 3]
        i = pl.program_id(0)
        sa, _ = _silu_and_grad(za_ref[...])
        y_a = (oa_ref[...] * sa).astype(BF16)
        n_b, _, _ = _rms_heads(ob_ref[...], nw_ref[...])
        sb, _ = _silu_and_grad(zb_ref[...])
        y_b = (n_b * sb).astype(BF16)
        a = pre_ref[...] * u_ref[...]
        halo = jnp.where(i > 0, hpre_ref[...] * hu_ref[...], 0.0)
        cw = cw_ref[...]
        conv = cw[0:1] * _shift_rows_down(halo, a, 2) + cw[1:2] * _shift_rows_down(halo, a, 1) + cw[2:3] * a
        sc, _ = _silu_and_grad(zc_ref[...])
        y_c = (post_ref[...] * conv * sc).astype(BF16)
        merged = None
        for k, yk in enumerate((y_a, y_b, y_c)):
            yc_ref[:, k * WIDTH:(k + 1) * WIDTH] = yk
            term = _sigmoid(g_ref[:, k * d:(k + 1) * d]) * _dot(yk, wb_ref[k])
            merged = term if merged is None else merged + term
        mb = merged.astype(BF16)
        mg_ref[...] = mb
        y = _dot(mb, wo_ref[...])
        r = ALPHA * x_ref[...] + (1.0 + gate_ref[...]) * y
        rhat, _ = _standardize(r)
        xn = rhat * lg_ref[...] + lbias_ref[...]
        if target is None:
            xn_ref[...] = xn
        else:
            t_ref, loss_ref = refs[19], refs[n_in + 3]

            @pl.when(i == 0)
            def _():
                loss_ref[...] = jnp.zeros_like(loss_ref)

            e = xn - t_ref[...]
            xn_ref[...] = e * (1.0 / d)
            part = jnp.sum(jnp.sum(e * e, axis=-1, keepdims=True), axis=0, keepdims=True)
            loss_ref[...] += part * (0.5 / d)

    wcol = lambda cb: pl.BlockSpec((tm, WIDTH), lambda i: (i, cb))
    halo_spec = lambda cb: pl.BlockSpec((8, WIDTH), lambda i: (jnp.maximum(i * hb - 1, 0), cb))
    vec = lambda w: pl.BlockSpec((1, w), lambda i: (0, 0))
    tile = pl.BlockSpec((tm, d), lambda i: (i, 0))
    with_loss = target is not None
    return _pcall(
        body, name=name,
        out_shape=(jax.ShapeDtypeStruct((s_len, d), F32), jax.ShapeDtypeStruct((s_len, d), BF16),
                   jax.ShapeDtypeStruct((s_len, 3 * WIDTH), BF16))
        + ((jax.ShapeDtypeStruct((1, 1), F32),) if with_loss else ()),
        grid=(s_len // tm,),
        in_specs=[tile,
                  wcol(0), wcol(3), wcol(0), wcol(7), wcol(8), wcol(9), wcol(10), wcol(11),
                  halo_spec(8), halo_spec(10),
                  pl.BlockSpec((tm, 3 * d), lambda i: (i, 2)),
                  vec(d), vec(WIDTH),
                  pl.BlockSpec((3, WIDTH), lambda i: (0, 0)),
                  pl.BlockSpec((3, WIDTH, d), lambda i: (0, 0, 0)),
                  pl.BlockSpec((d, d), lambda i: (0, 0)),
                  vec(d), vec(d)] + ([tile] if with_loss else []),
        out_specs=(tile, tile, pl.BlockSpec((tm, 3 * WIDTH), lambda i: (i, 0)))
        + ((pl.BlockSpec((1, 1), lambda i: (0, 0)),) if with_loss else ()),
        semantics=("arbitrary",), after=after)(x, o_a, proj, o_b, proj, proj, proj, proj, proj, proj, proj, proj,
                                  gate, norm_w, conv_w, wb, w_out, ln_g, ln_b, *([target] if with_loss else []))


def _merge_bwd(dxn, x, merged, ycat, proj, gate, wb, w_out, ln_g, name, after=None):
    s_len, d = x.shape
    tm = min(256, s_len)
    dsh = d // NDEV
    n_tiles = s_len // tm

    def body(dxn_ref, x_ref, mg_ref, yc_ref, g_ref, gate_ref, wb_ref, wo_ref, lg_ref,
             dres_ref, dyc_ref, dg_ref, gwo_out, gwb_out, vec_ref, gwo_ref, gwb_ref):
        @pl.when(pl.program_id(0) == 0)
        def _():
            gwo_ref[...] = jnp.zeros_like(gwo_ref)
            gwb_ref[...] = jnp.zeros_like(gwb_ref)
            vec_ref[...] = jnp.zeros_like(vec_ref)

        mb = mg_ref[...]
        one_gate = 1.0 + gate_ref[...]
        y = _dot(mb, wo_ref[...])
        r = ALPHA * x_ref[...] + one_gate * y
        rhat, rstd = _standardize(r)
        dxn = dxn_ref[...]
        dr = _standardize_bwd(rhat, rstd, dxn * lg_ref[...])
        vec_ref[0:1, :] += jnp.sum(dxn * rhat, axis=0, keepdims=True)
        vec_ref[1:2, :] += jnp.sum(dxn, axis=0, keepdims=True)
        vec_ref[2:3, :] += jnp.sum(dr * y, axis=0, keepdims=True)
        dres_ref[...] = ALPHA * dr
        dy = (one_gate * dr).astype(BF16)
        gwo_ref[...] += _dot_tn(mb, dy)
        dmerged = _dot_nt(dy, wo_ref[...])
        for k in range(3):
            yk = yc_ref[:, k * WIDTH:(k + 1) * WIDTH]
            sg = _sigmoid(g_ref[:, k * d:(k + 1) * d])
            pk = _dot(yk, wb_ref[k])
            dg_ref[:, k * d:(k + 1) * d] = (dmerged * pk * sg * (1.0 - sg)).astype(BF16)
            dpk = (dmerged * sg).astype(BF16)
            dyc_ref[:, k * WIDTH:(k + 1) * WIDTH] = _dot_nt(dpk, wb_ref[k])
            gwb_ref[k] += _dot_tn(yk, dpk)

        @pl.when(pl.program_id(0) == n_tiles - 1)
        def _():
            for o in range(NDEV):
                gwo_out[o] = gwo_ref[o * dsh:(o + 1) * dsh, :].astype(BF16)
                for k in range(3):
                    gwb_out[o, k] = gwb_ref[k, :, o * dsh:(o + 1) * dsh].astype(BF16)

    tile = lambda w: pl.BlockSpec((tm, w), lambda i: (i, 0))
    vec = pl.BlockSpec((1, d), lambda i: (0, 0))
    return _pcall(
        body, name=name,
        out_shape=(jax.ShapeDtypeStruct((s_len, d), F32), jax.ShapeDtypeStruct((s_len, 3 * WIDTH), F32),
                   jax.ShapeDtypeStruct(proj.shape, BF16), jax.ShapeDtypeStruct((NDEV, dsh, d), BF16),
                   jax.ShapeDtypeStruct((NDEV, 3, WIDTH, dsh), BF16), jax.ShapeDtypeStruct((8, d), F32)),
        grid=(n_tiles,),
        in_specs=[tile(d), tile(d), tile(d), tile(3 * WIDTH),
                  pl.BlockSpec((tm, 3 * d), lambda i: (i, 2)),
                  vec, pl.BlockSpec((3, WIDTH, d), lambda i: (0, 0, 0)),
                  pl.BlockSpec((d, d), lambda i: (0, 0)), vec],
        out_specs=(tile(d), tile(3 * WIDTH), pl.BlockSpec((tm, 3 * d), lambda i: (i, 2)),
                   pl.BlockSpec((NDEV, dsh, d), lambda i: (0, 0, 0)),
                   pl.BlockSpec((NDEV, 3, WIDTH, dsh), lambda i: (0, 0, 0, 0)),
                   pl.BlockSpec((8, d), lambda i: (0, 0))),
        scratch_shapes=[pltpu.VMEM((d, d), F32), pltpu.VMEM((3, WIDTH, d), F32)],
        semantics=("arbitrary",), after=after)(dxn, x, merged, ycat, proj, gate, wb, w_out, ln_g)


def _branch_bwd(dycat, proj, o_a, o_b, norm_w, conv_w, dproj, name, after=None):
    s_len = proj.shape[0]
    tm = min(256, s_len)
    hb = tm // 8
    n_tiles = s_len // tm

    def body(dya_ref, dyb_ref, dyc_ref, oa_ref, za_ref, ob_ref, zb_ref, pre_ref, post_ref, u_ref, zc_ref,
             hpre_ref, hu_ref, ndyc_ref, npost_ref, nzc_ref, nw_ref, cw_ref, dproj_in,
             dproj_ref, doa_ref, dob_ref, vec_ref, dza_scr, dzb_scr, dc_scr, sems):
        del dproj_in
        i = pl.program_id(0)

        @pl.when(i == 0)
        def _():
            vec_ref[...] = jnp.zeros_like(vec_ref)

        sa, dsa = _silu_and_grad(za_ref[...])
        dya = dya_ref[...]
        doa_ref[...] = dya * sa
        dza_scr[...] = (dya * oa_ref[...] * dsa).astype(BF16)
        nw = nw_ref[...]
        n_b, ohat, rstd = _rms_heads(ob_ref[...], nw)
        sb, dsb = _silu_and_grad(zb_ref[...])
        dyb = dyb_ref[...]
        dzb_scr[...] = (dyb * n_b * dsb).astype(BF16)
        dn = dyb * sb
        vec_ref[0:1, :] += jnp.sum(dn * ohat, axis=0, keepdims=True)
        dnw = dn * nw
        parts = []
        for h in range(WIDTH // HG_HEAD_DIM):
            sl = slice(h * HG_HEAD_DIM, (h + 1) * HG_HEAD_DIM)
            m2 = jnp.mean(dnw[:, sl] * ohat[:, sl], axis=-1, keepdims=True)
            parts.append(rstd[:, sl] * (dnw[:, sl] - ohat[:, sl] * m2))
        dob_ref[...] = jnp.concatenate(parts, axis=-1)
        cw = cw_ref[...]
        pre, u, post = pre_ref[...], u_ref[...], post_ref[...]
        a = pre * u
        halo = jnp.where(i > 0, hpre_ref[...] * hu_ref[...], 0.0)
        a1 = _shift_rows_down(halo, a, 1)
        a2 = _shift_rows_down(halo, a, 2)
        conv = cw[0:1] * a2 + cw[1:2] * a1 + cw[2:3] * a
        sc, dsc = _silu_and_grad(zc_ref[...])
        dyc = dyc_ref[...]
        dconv = dyc * post * sc
        nsc, _ = _silu_and_grad(nzc_ref[...])
        nxt = jnp.where(i < n_tiles - 1, ndyc_ref[...] * npost_ref[...] * nsc, 0.0)
        da = cw[2:3] * dconv + cw[1:2] * _shift_rows_up(dconv, nxt, 1) + cw[0:1] * _shift_rows_up(dconv, nxt, 2)
        dc_scr[:, 0 * WIDTH:1 * WIDTH] = (da * u).astype(BF16)
        dc_scr[:, 1 * WIDTH:2 * WIDTH] = (dyc * conv * sc).astype(BF16)
        dc_scr[:, 2 * WIDTH:3 * WIDTH] = (da * pre).astype(BF16)
        dc_scr[:, 3 * WIDTH:4 * WIDTH] = (dyc * post * conv * dsc).astype(BF16)
        vec_ref[1:2, :] += jnp.sum(dconv * a2, axis=0, keepdims=True)
        vec_ref[2:3, :] += jnp.sum(dconv * a1, axis=0, keepdims=True)
        vec_ref[3:4, :] += jnp.sum(dconv * a, axis=0, keepdims=True)
        rows = pl.ds(pl.multiple_of(i * tm, tm), tm)
        copies = [pltpu.make_async_copy(dza_scr, dproj_ref.at[rows, 3 * WIDTH:4 * WIDTH], sems.at[0]),
                  pltpu.make_async_copy(dzb_scr, dproj_ref.at[rows, 7 * WIDTH:8 * WIDTH], sems.at[1]),
                  pltpu.make_async_copy(dc_scr, dproj_ref.at[rows, 8 * WIDTH:12 * WIDTH], sems.at[2])]
        for cp in copies:
            cp.start()
        for cp in copies:
            cp.wait()

    wcol = lambda cb: pl.BlockSpec((tm, WIDTH), lambda i: (i, cb))
    prev = lambda cb: pl.BlockSpec((8, WIDTH), lambda i: (jnp.maximum(i * hb - 1, 0), cb))
    nxt = lambda cb: pl.BlockSpec((8, WIDTH), lambda i: (jnp.minimum((i + 1) * hb, s_len // 8 - 1), cb))
    anyspec = pl.BlockSpec(memory_space=pl.ANY)
    out = jax.ShapeDtypeStruct((s_len, WIDTH), F32)
    return _pcall(
        body, name=name,
        out_shape=(jax.ShapeDtypeStruct(dproj.shape, dproj.dtype), out, out, jax.ShapeDtypeStruct((8, WIDTH), F32)),
        grid=(n_tiles,),
        in_specs=[wcol(0), wcol(1), wcol(2), wcol(0), wcol(3), wcol(0), wcol(7), wcol(8), wcol(9), wcol(10), wcol(11),
                  prev(8), prev(10), nxt(2), nxt(9), nxt(11),
                  pl.BlockSpec((1, WIDTH), lambda i: (0, 0)), pl.BlockSpec((3, WIDTH), lambda i: (0, 0)), anyspec],
        out_specs=(anyspec, wcol(0), wcol(0), pl.BlockSpec((8, WIDTH), lambda i: (0, 0))),
        scratch_shapes=[pltpu.VMEM((tm, WIDTH), BF16), pltpu.VMEM((tm, WIDTH), BF16),
                        pltpu.VMEM((tm, 4 * WIDTH), BF16), pltpu.SemaphoreType.DMA((3,))],
        aliases={18: 0},
        semantics=("arbitrary",), after=after)(dycat, dycat, dycat, o_a, proj, o_b, proj, proj, proj, proj, proj,
                                  proj, proj, dycat, proj, proj, norm_w, conv_w, dproj)


def _sb_bwd(proj, do_a, totals, dproj, name):
    s_len = proj.shape[0]
    n_pairs = WIDTH // BLK
    scale = SB_HEAD_DIM ** -0.5
    qr = min(SB_Q_ROWS, s_len)
    gb = SB_K_BLOCKS
    kw = gb * BLK
    nq = s_len // qr
    assert qr == kw

    def body(q_ref, k_ref, v_ref, do_ref, tot_ref, dproj_in, dproj_ref, dq_ref, dk_ref, dv_ref, out_scr, sems):
        del dproj_in
        lane = _iota2((1, BLK), 1)
        row = _iota2((BLK, BLK), 0)
        col = _iota2((BLK, BLK), 1)
        ones = jnp.ones((BLK, BLK), BF16)
        twice = lambda m: jnp.concatenate([m, m], axis=0)
        before_and_sum = twice(jnp.concatenate([(row < col).astype(BF16), ones], axis=1))
        upto_and_sum = twice(jnp.concatenate([(row <= col).astype(BF16), ones], axis=1))
        strict = _iota2((qr, kw), 1) < _iota2((qr, kw), 0)
        head_lanes = [(lane // SB_HEAD_DIM) == hh for hh in range(2)]
        dk_ref[...] = jnp.zeros_like(dk_ref)
        dv_ref[...] = jnp.zeros_like(dv_ref)

        def scores(gi, qms, masked):
            c0 = pl.multiple_of(gi * kw, kw)
            kb = k_ref[pl.ds(c0, kw), :].astype(BF16)
            z2s = [_dot_nt(qms[hh], kb) for hh in range(2)]
            if masked:
                z2s = [jnp.where(strict, z2, MASKED_SCORE) for z2 in z2s]
            return tuple(z2s)

        def process(gi, z2s, qms, doms, totals_i, carry):
            c0 = pl.multiple_of(gi * kw, kw)
            kb = k_ref[pl.ds(c0, kw), :].astype(BF16)
            vb = v_ref[pl.ds(c0, kw), :].astype(BF16)
            das = [_dot_nt(doms[hh], vb) for hh in range(2)]
            halves = [_softplus2_parts(z2) for z2 in z2s]
            terms = [[_split2_lanes(sp2[:, b * BLK:(b + 1) * BLK]) for b in range(gb)] for sp2, _ in halves]
            sums = [[_dot(t, before_and_sum) for t in head_terms] for head_terms in terms]
            weights, gmats, l_befores = [], [], []
            for hh in range(2):
                l_before = carry[3 * hh + 1]
                parts = []
                for b in range(gb):
                    parts.append(totals_i[hh] - l_before - sums[hh][b][:, :BLK])
                    l_before = l_before + sums[hh][b][:, BLK:]
                a = jnp.exp2(z2s[hh] - jnp.concatenate(parts, axis=1))
                weights.append(a.astype(BF16))
                gmats.append(a * das[hh])
                l_befores.append(l_before)
            terms = [[_split2_lanes(g[:, b * BLK:(b + 1) * BLK]) for b in range(gb)] for g in gmats]
            sums = [[_dot(t, upto_and_sum) for t in head_terms] for head_terms in terms]
            dzs, g_befores = [], []
            for hh in range(2):
                g_before = carry[3 * hh + 2]
                parts = []
                for b in range(gb):
                    parts.append(g_before + sums[hh][b][:, :BLK])
                    g_before = g_before + sums[hh][b][:, BLK:]
                dzs.append((gmats[hh] - halves[hh][1] * jnp.concatenate(parts, axis=1)).astype(BF16))
                g_befores.append(g_before)
            dk_t = _dot_tn(jnp.concatenate(qms, axis=0), jnp.concatenate(dzs, axis=0))
            dv_t = _dot_tn(jnp.concatenate(doms, axis=0), jnp.concatenate(weights, axis=0))
            dqs = [_dot(dzs[hh], kb) for hh in range(2)]
            dk_ref[:, pl.ds(c0, kw)] += dk_t * (1.0 / LOG2E)
            dv_ref[:, pl.ds(c0, kw)] += dv_t
            return (carry[0] + dqs[0], l_befores[0], g_befores[0], carry[3] + dqs[1], l_befores[1], g_befores[1])

        def queries(i):
            qf = q_ref[pl.ds(pl.multiple_of(i * qr, qr), qr), :] * (scale * LOG2E)
            return [jnp.where(head_lanes[hh], qf, 0.0).astype(BF16) for hh in range(2)]

        def qtile(i, first_scores):
            r0 = pl.multiple_of(i * qr, qr)
            qms = queries(i)
            dof = do_ref[pl.ds(r0, qr), :]
            doms = [jnp.where(head_lanes[hh], dof, 0.0).astype(BF16) for hh in range(2)]
            totals_i = [tot_ref[hh, pl.ds(r0, qr), :] for hh in range(2)]
            zero = jnp.zeros((qr, BLK), F32)

            def step(gi, state):
                return scores(gi + 1, qms, False) + process(gi, state[:2], qms, doms, totals_i, state[2:])

            def before_diagonal(state):
                return scores(i, qms, True) + process(i - 1, state[:2], qms, doms, totals_i, state[2:])

            state = lax.fori_loop(0, i - 1, step, first_scores + (zero,) * 6)
            state = lax.cond(i > 0, before_diagonal, lambda st: st, state)
            nxt = jnp.minimum(i + 1, nq - 1)
            next_scores = scores(0, queries(nxt), False)
            carry = process(i, state[:2], qms, doms, totals_i, state[2:])
            dq_ref[pl.ds(r0, qr), :] = jnp.where(head_lanes[0], carry[0], carry[3]) * scale
            return next_scores

        lax.fori_loop(0, nq, qtile, scores(0, queries(0), True))
        pair = pl.program_id(0)
        copies = []
        for t, value in enumerate((dq_ref[...], dk_ref[...].T, dv_ref[...].T)):
            out_scr[t] = value.astype(BF16)
            col = pl.multiple_of((t * n_pairs + pair) * BLK, BLK)
            copies.append(pltpu.make_async_copy(out_scr.at[t], dproj_ref.at[:, pl.ds(col, BLK)], sems.at[t]))
            copies[-1].start()
        for cp in copies:
            cp.wait()

    col_spec = lambda off: pl.BlockSpec((s_len, BLK), lambda p: (0, off + p))
    anyspec = pl.BlockSpec(memory_space=pl.ANY)
    return _pcall(
        body, name=name, out_shape=jax.ShapeDtypeStruct(dproj.shape, dproj.dtype), grid=(n_pairs,),
        in_specs=[col_spec(0), col_spec(n_pairs), col_spec(2 * n_pairs), col_spec(0),
                  pl.BlockSpec((2, s_len, BLK), lambda p: (p, 0, 0)), anyspec],
        out_specs=anyspec,
        scratch_shapes=[pltpu.VMEM((s_len, BLK), F32), pltpu.VMEM((BLK, s_len), F32), pltpu.VMEM((BLK, s_len), F32),
                        pltpu.VMEM((3, s_len, BLK), BF16), pltpu.SemaphoreType.DMA((3,))],
        aliases={5: 0},
        semantics=("arbitrary",))(proj, proj, proj, do_a, totals, dproj)


def _hgrn_bwd(proj, do_b, lb, dproj, name):
    s_len = proj.shape[0]
    nc = s_len // BLK
    gw = HG_GROUP * HG_HEAD_DIM
    n_groups = WIDTH // gw
    base = 4 * WIDTH // gw
    heads_of = range(HG_GROUP)

    def body(q_ref, f_ref, i_ref, do_ref, lb_ref, dproj_in, dproj_ref, dlb_ref, mask_ref, st_ref, out_scr, sems):
        del dproj_in
        _hg_masks(mask_ref)
        row = _iota2((BLK, BLK), 0)
        col = _iota2((BLK, BLK), 1)
        lower_incl = (col <= row).astype(BF16)
        upper_incl = (col >= row).astype(BF16)
        lb_v = lb_ref[...]
        refs = (q_ref, f_ref, i_ref)

        def fwd_chunk(ci, sts):
            for h in heads_of:
                st_ref[ci, h] = sts[h]
            heads, bs = _hg_load(refs, pl.multiple_of(ci * BLK, BLK), lb_v, lower_incl)
            b_ends = [b[BLK - 1:BLK, :] for b in bs]
            k_decs = [((1.0 - hd[2]) * jnp.exp(b_end - b)).astype(BF16) for hd, b, b_end in zip(heads, bs, b_ends)]
            grown = [_dot_tn(hd[5].astype(BF16), k_dec) for hd, k_dec in zip(heads, k_decs)]
            return tuple(st * jnp.exp(b_end) + g for st, b_end, g in zip(sts, b_ends, grown))

        zero_state = (jnp.zeros((HG_HEAD_DIM, HG_HEAD_DIM), F32),) * HG_GROUP
        lax.fori_loop(0, nc, fwd_chunk, zero_state)

        def bwd_chunk(cc, carry):
            dsts, suffixes, dlbs = carry
            ci = nc - 1 - cc
            r0 = pl.multiple_of(ci * BLK, BLK)
            heads, bs = _hg_load(refs, r0, lb_v, lower_incl)
            qs = [hd[0] for hd in heads]
            fs = [hd[2] for hd in heads]
            ks = [1.0 - f for f in fs]
            vs = [hd[5] for hd in heads]
            vbs = [v.astype(BF16) for v in vs]
            dos = [do_ref[pl.ds(r0, BLK), h * HG_HEAD_DIM:(h + 1) * HG_HEAD_DIM] for h in heads_of]
            dobs = [do.astype(BF16) for do in dos]
            b_ends = [b[BLK - 1:BLK, :] for b in bs]
            e_qs = [jnp.exp(b) for b in bs]
            e_ks = [jnp.exp(b_end - b) for b, b_end in zip(bs, b_ends)]
            qes = [(q * e).astype(BF16) for q, e in zip(qs, e_qs)]
            khs = [(k * e).astype(BF16) for k, e in zip(ks, e_ks)]
            st_terms = [_split2_lanes(st_ref[ci, h]) for h in heads_of]
            ds_terms = [_split2_lanes(dst) for dst in dsts]
            dqes = [_dot(dob, t[:, :HG_HEAD_DIM]) + _dot(dob, t[:, HG_HEAD_DIM:]) for dob, t in zip(dobs, st_terms)]
            dkhs = [_dot(vb, t[:, :HG_HEAD_DIM]) + _dot(vb, t[:, HG_HEAD_DIM:]) for vb, t in zip(vbs, ds_terms)]
            dvs = [_dot_nt(kh, t[:, :HG_HEAD_DIM]) for kh, t in zip(khs, ds_terms)]
            grown = [_dot_tn(dob, qe) for dob, qe in zip(dobs, qes)]
            das = [_dot_nt(dob, vb) for dob, vb in zip(dobs, vbs)]
            dqs = [e * dqe for e, dqe in zip(e_qs, dqes)]
            dks = [e * dkh for e, dkh in zip(e_ks, dkhs)]
            dlogs = [qe.astype(F32) * dqe - kh.astype(F32) * dkh for qe, dqe, kh, dkh in zip(qes, dqes, khs, dkhs)]
            scs = [None] * HG_GROUP
            for v_idx, m in enumerate(HG_LEVELS):
                es, qms, kms = _hg_level_terms(qs, ks, bs, m)
                msk = mask_ref[v_idx]
                terms = [_dot_nt(qm, km) for qm, km in zip(qms, kms)]
                pms = [(da * msk).astype(BF16) for da in das]
                dqms = [_dot(pm, km) for pm, km in zip(pms, kms)]
                dkms = [_dot_tn(pm, qm) for pm, qm in zip(pms, qms)]
                scs = [t * msk if sc is None else sc + t * msk for sc, t in zip(scs, terms)]
                dqs = [dq + dqm * e for dq, dqm, e in zip(dqs, dqms, es)]
                dks = [dk + dkm * e for dk, dkm, e in zip(dks, dkms, es)]
                dlogs = [dl + (qm.astype(F32) * dqm - km.astype(F32) * dkm)
                         for dl, qm, dqm, km, dkm in zip(dlogs, qms, dqms, kms, dkms)]
            intras = [_dot_tn(sc.astype(BF16), dob) for sc, dob in zip(scs, dobs)]
            dgs = [_dot_01_l(upper_incl, dl) + sfx for dl, sfx in zip(dlogs, suffixes)]
            new_dlbs = []
            for h in heads_of:
                q, dq_fac, f, sig = heads[h][0], heads[h][1], heads[h][2], heads[h][3]
                a_diag = jnp.sum(dos[h] * vs[h], axis=-1, keepdims=True)
                s_diag = jnp.sum(q * ks[h], axis=-1, keepdims=True)
                dq = dqs[h] + a_diag * ks[h]
                dk = dks[h] + a_diag * q
                dv = dvs[h] + intras[h] + s_diag * dos[h]
                dfull = dgs[h] / f - dk
                sl = slice(h * HG_HEAD_DIM, (h + 1) * HG_HEAD_DIM)
                out_scr[0, pl.ds(r0, BLK), sl] = (dq * dq_fac).astype(BF16)
                out_scr[1, pl.ds(r0, BLK), sl] = (dfull * (1.0 - lb_v[:, sl]) * sig * (1.0 - sig)).astype(BF16)
                out_scr[2, pl.ds(r0, BLK), sl] = dv.astype(BF16)
                new_dlbs.append(dlbs[h] + jnp.sum(dfull * (1.0 - sig), axis=0, keepdims=True))
            new_dsts = tuple(dst * jnp.exp(b_end) + g for dst, b_end, g in zip(dsts, b_ends, grown))
            return new_dsts, tuple(dg[0:1, :] for dg in dgs), tuple(new_dlbs)

        zero_row = (jnp.zeros((1, HG_HEAD_DIM), F32),) * HG_GROUP
        _, _, dlbs = lax.fori_loop(0, nc, bwd_chunk, (zero_state, zero_row, zero_row))
        dlb_ref[...] = jnp.broadcast_to(jnp.concatenate(dlbs, axis=1), dlb_ref.shape)
        group = pl.program_id(0)
        copies = []
        for t in range(3):
            col = pl.multiple_of((base + t * n_groups + group) * gw, gw)
            copies.append(pltpu.make_async_copy(out_scr.at[t], dproj_ref.at[:, pl.ds(col, gw)], sems.at[t]))
            copies[-1].start()
        for cp in copies:
            cp.wait()

    col_spec = lambda off: pl.BlockSpec((s_len, gw), lambda h: (0, off + h))
    anyspec = pl.BlockSpec(memory_space=pl.ANY)
    return _pcall(
        body, name=name,
        out_shape=(jax.ShapeDtypeStruct(dproj.shape, dproj.dtype), jax.ShapeDtypeStruct((8, WIDTH), F32)),
        grid=(n_groups,),
        in_specs=[col_spec(base), col_spec(base + n_groups), col_spec(base + 2 * n_groups), col_spec(0),
                  pl.BlockSpec((1, gw), lambda h: (0, h)), anyspec],
        out_specs=(anyspec, pl.BlockSpec((8, gw), lambda h: (0, h))),
        scratch_shapes=[pltpu.VMEM((len(HG_LEVELS), BLK, BLK), F32),
                        pltpu.VMEM((nc, HG_GROUP, HG_HEAD_DIM, HG_HEAD_DIM), F32),
                        pltpu.VMEM((3, s_len, gw), BF16), pltpu.SemaphoreType.DMA((3,))],
        aliases={5: 0},
        semantics=("arbitrary",))(proj, proj, proj, do_b, lb, dproj)


def _dh_matmul(dproj, w_full, after, name):
    s_len, n = dproj.shape
    d = w_full.shape[0]
    tm = min(1024, s_len)
    tk = 4608

    def body(dp_ref, w_ref, after_ref, dh_ref):
        del after_ref
        part = _dot_nt(dp_ref[...], w_ref[...])

        @pl.when(pl.program_id(1) == 0)
        def _():
            dh_ref[...] = part

        @pl.when(pl.program_id(1) > 0)
        def _():
            dh_ref[...] += part

    return _pcall(
        body, name=name, out_shape=jax.ShapeDtypeStruct((s_len, d), F32),
        grid=(s_len // tm, n // tk),
        in_specs=[pl.BlockSpec((tm, tk), lambda i, k: (i, k)), pl.BlockSpec((d, tk), lambda i, k: (0, k)),
                  pl.BlockSpec(memory_space=pl.ANY)],
        out_specs=pl.BlockSpec((tm, d), lambda i, k: (i, 0)),
        semantics=("arbitrary", "arbitrary"))(dproj, w_full, after)


def _gw_matmul(h_t, dproj, name):
    d, s_len = h_t.shape
    n = dproj.shape[1]
    tn = 2304

    def body(ht_ref, dp_ref, gw_ref):
        gw_ref[...] = _dot(ht_ref[...], dp_ref[...]).astype(BF16)

    return _pcall(
        body, name=name, out_shape=jax.ShapeDtypeStruct((d, n), BF16),
        grid=(n // tn,),
        in_specs=[pl.BlockSpec((d, s_len), lambda j: (0, 0)), pl.BlockSpec((s_len, tn), lambda j: (0, j))],
        out_specs=pl.BlockSpec((d, tn), lambda j: (0, j)),
        semantics=("arbitrary",))(h_t, dproj)


def _ln_bwd(dh, x, scale, dres, name, after=None):
    s_len, d = x.shape
    tm = min(512, s_len)

    def body(dh_ref, x_ref, sc_ref, dres_ref, dx_ref, vec_ref):
        @pl.when(pl.program_id(0) == 0)
        def _():
            vec_ref[...] = jnp.zeros_like(vec_ref)

        dh = dh_ref[...]
        xs, rstd = _standardize(x_ref[...])
        vec_ref[0:1, :] += jnp.sum(dh, axis=0, keepdims=True)
        vec_ref[1:2, :] += jnp.sum(dh * xs, axis=0, keepdims=True)
        dx_ref[...] = _standardize_bwd(xs, rstd, dh * (1.0 + sc_ref[...])) + dres_ref[...]

    tile = pl.BlockSpec((tm, d), lambda i: (i, 0))
    return _pcall(body, name=name, grid=(s_len // tm,),
                  out_shape=(jax.ShapeDtypeStruct((s_len, d), F32), jax.ShapeDtypeStruct((8, d), F32)),
                  in_specs=[tile, tile, pl.BlockSpec((1, d), lambda i: (0, 0)), tile],
                  out_specs=(tile, pl.BlockSpec((8, d), lambda i: (0, 0))),
                  semantics=("arbitrary",), after=after)(dh, x, scale, dres)


def _wmod_grad(c_t, dmod):
    d = c_t.shape[0]
    n_layers, _, cm = dmod.shape

    def body(c_ref, dm_ref, o_ref):
        for l in range(n_layers):
            acc = None
            for b in range(NDEV):
                term = c_ref[:, b:b + 1] * dm_ref[l, b:b + 1, :]
                acc = term if acc is None else acc + term
            o_ref[l] = acc

    return _pcall(body, name="wmod_grad", out_shape=jax.ShapeDtypeStruct((n_layers, d, cm), F32))(c_t, dmod)


def _sum_adamw(parts, w, m, v, name, first_row=0, into=None, after=None):
    n_src, range_rows, cols = parts.shape
    rows = w.shape[0]
    tr = range_rows
    for cand in (512, 256, 128, 64, 32, 16, 8):
        if range_rows % cand == 0 and cand * cols * 4 <= (2 << 20):
            tr = cand
            break
    first_tile = first_row // tr
    assert first_row % tr == 0
    n_extra = (0 if into is None else 4) + (0 if after is None else 1)

    def body(p_ref, w_ref, m_ref, v_ref, *rest):
        g_ref, d_ref, nm_ref, nv_ref = rest[n_extra:]
        g = p_ref[0].astype(F32)
        for s in range(1, n_src):
            g = g + p_ref[s].astype(F32)
        g_ref[...] = g
        d_ref[...], nm_ref[...], nv_ref[...] = _adamw_step(g, w_ref[...], m_ref[...], v_ref[...])

    tile = pl.BlockSpec((tr, cols), lambda i: (i + first_tile, 0))
    anyspec = pl.BlockSpec(memory_space=pl.ANY)
    out = jax.ShapeDtypeStruct((rows, cols), F32)
    extra = ([] if into is None else list(into)) + ([] if after is None else [after])
    aliases = {} if into is None else {4 + k: k for k in range(4)}
    return _pcall(body, name=name, grid=(range_rows // tr,), out_shape=(out,) * 4,
                  in_specs=[pl.BlockSpec((n_src, tr, cols), lambda i: (0, i, 0)), tile, tile, tile]
                  + [anyspec] * len(extra),
                  out_specs=(tile,) * 4, aliases=aliases, semantics=("arbitrary",))(parts, w, m, v, *extra)


def _adamw_step(g, w, m, v):
    nm = ADAM_B1 * m + (1.0 - ADAM_B1) * g
    nv = ADAM_B2 * v + (1.0 - ADAM_B2) * (g * g)
    m_hat = nm / (1.0 - ADAM_B1 ** ADAM_STEP)
    v_hat = nv / (1.0 - ADAM_B2 ** ADAM_STEP)
    return -ADAM_LR * (m_hat / (jnp.sqrt(v_hat) + ADAM_EPS) + ADAM_WD * w), nm, nv


def _adamw_small(gs, ws, ms, vs):
    n = len(gs)

    def body(*refs):
        for p in range(n):
            results = _adamw_step(*(refs[k * n + p][...] for k in range(4)))
            for k in range(3):
                refs[(4 + k) * n + p][...] = results[k]

    shapes = [jax.ShapeDtypeStruct(w.shape, F32) for w in ws]
    outs = _pcall(body, name="adamw_small", out_shape=shapes * 3)(*gs, *ws, *ms, *vs)
    return [(outs[p], outs[n + p], outs[2 * n + p]) for p in range(n)]


def _sum_parts(parts, name):
    n_src = parts.shape[0]

    def body(p_ref, o_ref):
        acc = p_ref[0]
        for s in range(1, n_src):
            acc = acc + p_ref[s]
        o_ref[...] = acc

    return _pcall(body, name=name, out_shape=jax.ShapeDtypeStruct(parts.shape[1:], F32))(parts)


def _pair_sum(gw, stage, me, name):
    d = gw.shape[0]
    n_slots, _, shard = stage.shape

    def body(me_ref, g_ref, s_ref, own_ref, o_ref):
        del me_ref
        total = (g_ref[...].astype(F32) + s_ref[0].astype(F32)).astype(BF16)
        o_ref[0] = total

        @pl.when(pl.program_id(0) == 0)
        def _():
            own_ref[0] = total

    slot = pl.BlockSpec((1, d, shard), lambda jj, me_ref: (jj, 0, 0))
    out = jax.ShapeDtypeStruct(stage.shape, BF16)
    return pl.pallas_call(
        body, name=name, out_shape=(out, out),
        grid_spec=pltpu.PrefetchScalarGridSpec(
            num_scalar_prefetch=1, grid=(n_slots,),
            in_specs=[pl.BlockSpec((d, shard), lambda jj, me_ref: (0, me_ref[0] ^ (2 * jj))), slot],
            out_specs=(pl.BlockSpec((1, d, shard), lambda jj, me_ref: (0, 0, 0)), slot)),
        compiler_params=pltpu.CompilerParams(dimension_semantics=("arbitrary",), vmem_limit_bytes=VMEM_LIMIT),
        interpret=False)(me.reshape(1).astype(jnp.int32), gw, stage)


def _lower_bound_table(lower_bounds):
    p = jax.nn.softmax(lower_bounds.astype(F32), axis=0)
    return jnp.cumsum(p, axis=0) - p[0:1]


def _pad_rows(v, width):
    n = v.shape[0]
    rows = -(-n // width)
    rows = -(-rows // 8) * 8
    return jnp.pad(v, (0, rows * width - n)).reshape(rows, width)


def kernel(x, c, w_mod, b_mod, w_in, conv_w, hgrn_norm_w, lower_bounds, w_branch, w_out, ln_g, ln_b, loss_target, m_w_mod, m_b_mod, m_w_in, m_conv_w, m_hgrn_norm_w, m_lower_bounds, m_w_branch, m_w_out, m_ln_g, m_ln_b, v_w_mod, v_b_mod, v_w_in, v_conv_w, v_hgrn_norm_w, v_lower_bounds, v_w_branch, v_w_out, v_ln_g, v_ln_b):
    n_layers = N_LAYERS
    s_len, d = x.shape[1], x.shape[2]
    n_cols = w_in.shape[2] * NDEV
    cw_cols = conv_w.shape[2]
    cm = w_mod.shape[2]
    me = _my_index()
    x0 = x[0]
    target = loss_target[0]

    small = _pad_rows(jnp.concatenate([c.reshape(-1), conv_w.reshape(-1)]), BLK)
    small_all = _all_gather_small("gather_c_conv", small).reshape(NDEV, -1)
    c_all = small_all[:, :d]
    conv_full = small_all[:, d:d + n_layers * 3 * cw_cols].reshape(NDEV, n_layers, 3, cw_cols)
    conv_full = conv_full.transpose(1, 2, 0, 3).reshape(n_layers, 3, WIDTH)

    b_mod_mine = lax.dynamic_slice_in_dim(b_mod, me * cm, cm, axis=1).reshape(n_layers, 1, cm)
    mod_cols = _mod_fwd(c_all, w_mod, b_mod_mine)
    mod_all = _all_gather_small("gather_mod", mod_cols.reshape(n_layers * NDEV, cm))
    mod_all = mod_all.reshape(NDEV, n_layers, NDEV, cm)
    mod_mine = lax.dynamic_index_in_dim(mod_all, me, axis=2, keepdims=False)
    mod_mine = mod_mine.transpose(1, 0, 2).reshape(n_layers, 3, 1, d)

    shard = w_in.shape[2]
    dsh = d // NDEV
    w_in_b, w_branch_b, w_out_b = w_in.astype(BF16), w_branch.astype(BF16), w_out.astype(BF16)
    window = lambda ref, dev: ref.at[:, pl.ds(pl.multiple_of(dev * shard, BLK), shard)]

    def two_step_sends(places):
        chips, sibling = [], []
        for k in (1, 2, 4, 6):
            for a, place in enumerate(places):
                chips.append((k, lambda ins, lands, me, a=a: ins[a],
                              lambda lands, me, a=a, place=place: place(lands[a], me),
                              lambda lands, me, a=a, k=k, place=place: place(lands[a], me ^ k)))
        for j in (2, 4, 6):
            for a, place in enumerate(places):
                sibling.append((1, lambda ins, lands, me, a=a, j=j, place=place: place(lands[a], me ^ j),
                                lambda lands, me, a=a, j=j, place=place: place(lands[a], me ^ j),
                                lambda lands, me, a=a, j=j, place=place: place(lands[a], me ^ 1 ^ j)))
        return chips, sibling

    in_sends = two_step_sends([window])
    rest_sends = two_step_sends([_slot, _slot])
    layer_sends = two_step_sends([window, _slot, _slot])

    def in_land(l):
        return _place_own_window(f"place_w_in_{l}", (d, n_cols), w_in_b[l], me)

    def rest_lands(l):
        return [_place_own((NDEV, 3, WIDTH, dsh), BF16, w_branch_b[l][None], (me, 0, 0, 0)),
                _place_own((NDEV, dsh, d), BF16, w_out_b[l][None], (me, 0, 0))]

    def gather_start(name, shards, lands, sends, after):
        return _exchange_start(f"{name}_chips_start", shards, lands, sends[0], after)

    def gather_pass_on(name, started, after, sends):
        _, lands = _exchange_wait(f"{name}_chips_wait", started, after, sends[0])
        return _exchange_start(f"{name}_sibling_start", [], lands, sends[1])

    def gather_finish(name, started, after, sends):
        return _exchange_wait(f"{name}_sibling_wait", started, after, sends[1])[1]

    def branch_out_weights(w_branch_l, w_out_l):
        return w_branch_l.transpose(1, 2, 0, 3).reshape(3, WIDTH, d), w_out_l.reshape(d, d)

    gathering = gather_start("gather_w_in_0", [w_in_b[0]], [in_land(0)], in_sends, mod_mine)
    rest_gathering = gather_start("gather_rest_0", [w_branch_b[0], w_out_b[0]], rest_lands(0), rest_sends,
                                  gathering[4])
    next_gathering = None
    if n_layers > 1:
        next_gathering = gather_start("gather_weights_1", [w_in_b[1], w_branch_b[1], w_out_b[1]],
                                      [in_land(1)] + rest_lands(1), layer_sends, rest_gathering[4])
    passing = gather_pass_on("gather_w_in_0", gathering, (next_gathering or rest_gathering)[4], in_sends)
    w_in_l = gather_finish("gather_w_in_0", passing, passing[4], in_sends)[0]

    lbs = _lower_bound_table(lower_bounds)
    norm_w4 = jnp.tile(hgrn_norm_w, (1, WIDTH // HG_HEAD_DIM))

    saved = []
    xl = x0
    for l in range(n_layers):
        shift, scale, gate = mod_mine[l, 0], mod_mine[l, 1], mod_mine[l, 2]
        merge_after = None
        proj, h_t = _ln_proj(xl, shift, scale, w_in_l, f"ln_proj_{l}")
        o_a, totals = _sb_fwd(proj, f"sb_fwd_{l}")
        if l == 0:
            rest_passing = gather_pass_on("gather_rest_0", rest_gathering, o_a, rest_sends)
        o_b = _hgrn_fwd(proj, lbs[l:l + 1], f"hgrn_fwd_{l}", after=rest_passing[4] if l == 0 else None)
        if l == 0:
            wb_l, wo_l = branch_out_weights(*gather_finish("gather_rest_0", rest_passing, o_b, rest_sends))
            if n_layers > 1:
                next_passing = gather_pass_on("gather_weights_1", next_gathering, o_b, layer_sends)
                merge_after = next_passing[4]
        x_new, merged, ycat, *loss_term = _merge_fwd(
            xl, proj, o_a, o_b, gate, norm_w4[l:l + 1], conv_full[l], wb_l, wo_l, ln_g[l:l + 1], ln_b[l:l + 1],
            f"merge_fwd_{l}", target=target if l == n_layers - 1 else None, after=merge_after)
        saved.append((xl, proj, h_t, o_a, totals, o_b, merged, ycat, w_in_l, wb_l, wo_l))
        if l == 0 and n_layers > 1:
            w_in_l, w_branch_l, w_out_l = gather_finish("gather_weights_1", next_passing, x_new, layer_sends)
            wb_l, wo_l = branch_out_weights(w_branch_l, w_out_l)
        xl = x_new

    dx, loss_part = xl, loss_term[0]

    pair_sends = [(1, lambda ins, lands, me, j=j: window(ins[0], me ^ 1 ^ j),
                   lambda lands, me, jj=jj: lands[0].at[jj], lambda lands, me, jj=jj: lands[0].at[jj])
                  for jj, j in enumerate((0, 2, 4, 6))]
    chip_sum_sends = [(j, lambda ins, lands, me, jj=jj: ins[0].at[jj],
                       lambda lands, me, jj=jj: lands[0].at[jj], lambda lands, me, jj=jj: lands[0].at[jj])
                      for jj, j in ((1, 2), (2, 4), (3, 6))]
    rest_scatter = _direct_sends([(0, 0, _slot, _slot), (1, 1, _slot, _slot)])
    scattering = [None] * n_layers
    small_grads = [None] * n_layers
    dmod = [None] * n_layers
    tie = None
    for l in reversed(range(n_layers)):
        xl, proj, h_t, o_a, totals, o_b, merged, ycat, w_in_l, wb_l, wo_l = saved[l]
        scale, gate = mod_mine[l, 1], mod_mine[l, 2]
        dres, dycat, dproj, gwo_by_owner, gwb_by_owner, mvec = _merge_bwd(
            dx, xl, merged, ycat, proj, gate, wb_l, wo_l, ln_g[l:l + 1], f"merge_bwd_{l}", after=tie)
        lands = [_place_own((NDEV, 3, WIDTH, dsh), BF16, lax.dynamic_slice_in_dim(gwb_by_owner, me, 1, axis=0),
                            (me, 0, 0, 0)),
                 _place_own((NDEV, dsh, d), BF16, lax.dynamic_slice_in_dim(gwo_by_owner, me, 1, axis=0),
                            (me, 0, 0))]
        rest_started = _exchange_start(f"scatter_rest_{l}_start", [gwb_by_owner, gwo_by_owner], lands, rest_scatter)
        dproj, do_a, do_b, bvec = _branch_bwd(dycat, proj, o_a, o_b, norm_w4[l:l + 1], conv_full[l], dproj,
                                              f"branch_bwd_{l}", after=rest_started[4])
        dproj = _sb_bwd(proj, do_a, totals, dproj, f"sb_bwd_{l}")
        dproj, dlb = _hgrn_bwd(proj, do_b, lbs[l:l + 1], dproj, f"hgrn_bwd_{l}")
        gwi = _gw_matmul(h_t, dproj, f"gw_matmul_{l}")
        swapping = _exchange_start(f"scatter_in_{l}_sibling_start", [gwi], [lax.empty((4, d, shard), BF16)], pair_sends)
        if l > 0:
            dh = _dh_matmul(dproj, w_in_l, swapping[4], f"dh_matmul_{l}")
        (gwi,), (stage,) = _exchange_wait(f"scatter_in_{l}_sibling_wait", swapping, dh if l > 0 else swapping[4],
                                          pair_sends)
        land, chip_sums = _pair_sum(gwi, stage, me, f"pair_sum_{l}")
        in_started = _exchange_start(f"scatter_in_{l}_chips_start", [chip_sums], [land], chip_sum_sends)
        scattering[l] = (in_started, rest_started)
        tie = in_started[4]
        if l == 0:
            dh = _dh_matmul(dproj, w_in_l, tie, f"dh_matmul_{l}")
        dx, lvec = _ln_bwd(dh, xl, scale, dres, f"ln_bwd_{l}", after=tie)
        dmod[l] = jnp.concatenate([lvec[0], lvec[1], mvec[2]])
        norm_grad = bvec[0].reshape(WIDTH // HG_HEAD_DIM, HG_HEAD_DIM).sum(axis=0)
        small_grads[l] = jnp.concatenate([mvec[0], mvec[1], norm_grad, dlb[0], bvec[1:4].reshape(-1)])
    grad_x = dx[None]

    flat = lambda a: a.reshape(-1, a.shape[-1])
    big = {"w_in": (w_in, m_w_in, v_w_in), "w_branch": (w_branch, m_w_branch, v_w_branch),
           "w_out": (w_out, m_w_out, v_w_out)}
    big_results = {n: None for n in big}

    def adam_layer(l, after):
        in_started, rest_started = scattering[l]
        p_branch_l, p_out_l = _exchange_wait(f"scatter_rest_{l}_wait", rest_started, after, rest_scatter)[1]
        p_in_l = _exchange_wait(f"scatter_in_{l}_chips_wait", in_started, after, chip_sum_sends)[1][0]
        parts = {"w_in": p_in_l, "w_branch": p_branch_l.reshape(NDEV, 3 * WIDTH, dsh), "w_out": p_out_l}
        last = None
        for n, (w, m, v) in big.items():
            rows_per_layer = flat(w).shape[0] // n_layers
            big_results[n] = _sum_adamw(parts[n], flat(w), flat(m), flat(v), f"adamw_{n}_{l}",
                                        first_row=l * rows_per_layer, into=big_results[n], after=last)
            last = big_results[n][3]
        return last

    after_adam = None
    for l in reversed(range(1, n_layers)):
        after_adam = adam_layer(l, tie)

    small_vec = jnp.concatenate(dmod + small_grads + [loss_part.reshape(1)])
    n_small = small_vec.shape[0]
    small_all = _all_gather_small("gather_small_grads", _pad_rows(small_vec, BLK), after=after_adam)
    small_sum = _sum_parts(small_all, "sum_small_grads").reshape(-1)[:n_small]
    dmod_all = small_all.reshape(NDEV, -1)[:, :n_layers * 3 * d].reshape(NDEV, n_layers, 3 * d)

    loss = small_sum[n_small - 1]

    off = n_layers * 3 * d
    grad_b_mod = small_sum[:off].reshape(n_layers, 3 * d)
    per_layer = 2 * d + HG_HEAD_DIM + WIDTH + 3 * WIDTH
    g_ln_g, g_ln_b, g_norm, g_lbs, g_conv = [], [], [], [], []
    for l in range(n_layers):
        seg = small_sum[off + l * per_layer: off + (l + 1) * per_layer]
        g_ln_g.append(seg[:d])
        g_ln_b.append(seg[d:2 * d])
        g_norm.append(seg[2 * d:2 * d + HG_HEAD_DIM])
        g_lbs.append(seg[2 * d + HG_HEAD_DIM:2 * d + HG_HEAD_DIM + WIDTH])
        g_conv.append(seg[2 * d + HG_HEAD_DIM + WIDTH:].reshape(3, WIDTH))
    grad_ln_g, grad_ln_b = jnp.stack(g_ln_g), jnp.stack(g_ln_b)
    grad_norm = jnp.stack(g_norm)
    _, lbs_vjp = jax.vjp(_lower_bound_table, lower_bounds)
    grad_lower = lbs_vjp(jnp.stack(g_lbs))[0]
    grad_conv = lax.dynamic_slice_in_dim(jnp.stack(g_conv), me * cw_cols, cw_cols, axis=2)

    dmod_mine = lax.dynamic_slice_in_dim(dmod_all, me * cm, cm, axis=2).transpose(1, 0, 2)
    grad_w_mod = _wmod_grad(c_all.T, dmod_mine)

    adam_layer(0, grad_w_mod)
    r_w_in, r_w_branch, r_w_out = ([o.reshape(big[n][0].shape) for o in big_results[n]]
                                   for n in ("w_in", "w_branch", "w_out"))
    r_w_mod = [o.reshape(w_mod.shape) for o in
               _sum_adamw(grad_w_mod.reshape(1, -1, cm), flat(w_mod), flat(m_w_mod), flat(v_w_mod), "adamw_w_mod")]

    small_names = ["b_mod", "conv_w", "hgrn_norm_w", "lower_bounds", "ln_g", "ln_b"]
    small_g = [grad_b_mod, grad_conv, grad_norm, grad_lower, grad_ln_g, grad_ln_b]
    small_w = [b_mod, conv_w, hgrn_norm_w, lower_bounds, ln_g, ln_b]
    small_m = [m_b_mod, m_conv_w, m_hgrn_norm_w, m_lower_bounds, m_ln_g, m_ln_b]
    small_v = [v_b_mod, v_conv_w, v_hgrn_norm_w, v_lower_bounds, v_ln_g, v_ln_b]
    as_rows = lambda a: a.reshape(-1, a.shape[-1])
    updates = _adamw_small([as_rows(a) for a in small_g], [as_rows(a) for a in small_w],
                           [as_rows(a) for a in small_m], [as_rows(a) for a in small_v])
    r_small = {n: [g] + [u.reshape(w.shape) for u in upd]
               for n, g, w, upd in zip(small_names, small_g, small_w, updates)}

    results = {"w_mod": r_w_mod, "w_in": r_w_in, "w_branch": r_w_branch, "w_out": r_w_out, **r_small}
    order = ["w_mod", "b_mod", "w_in", "conv_w", "hgrn_norm_w", "lower_bounds", "w_branch", "w_out", "ln_g", "ln_b"]
    outs = [loss, grad_x]
    for idx in range(4):
        outs.extend(results[n][idx] for n in order)
    return tuple(outs)
```

```python
import jax
import jax.numpy as jnp
from jax import lax
from jax.experimental import pallas as pl
from jax.experimental.pallas import tpu as pltpu

F32 = jnp.float32
BF16 = jnp.bfloat16
NDEV = 8
N_LAYERS = 2
SB_HEAD_DIM = 64
HG_HEAD_DIM = 128
WIDTH = 512
BLK = 128
LN_EPS = 1e-5
RMS_EPS = 1e-6
ALPHA = (2.0 * N_LAYERS) ** 0.25
ADAM_LR, ADAM_B1, ADAM_B2, ADAM_EPS, ADAM_WD, ADAM_STEP = 0.001, 0.9, 0.999, 1e-08, 0.01, 10
VMEM_LIMIT = 56 * 1024 * 1024
MESH = pl.DeviceIdType.MESH
HG_LEVELS = (64, 32, 16, 8, 4, 2, 1)


def _pcall(body, *, name, out_shape, grid=None, in_specs=None, out_specs=None, scratch_shapes=(),
           semantics=None, aliases=None, after=None):
    if after is not None:
        n_in = len(in_specs)
        inner = body
        body = lambda *refs: inner(*refs[:n_in], *refs[n_in + 1:])
        in_specs = list(in_specs) + [pl.BlockSpec(memory_space=pl.ANY)]
    kwargs = {}
    if grid is not None:
        kwargs["grid"] = grid
    if in_specs is not None:
        kwargs["in_specs"] = in_specs
    if out_specs is not None:
        kwargs["out_specs"] = out_specs
    if aliases:
        kwargs["input_output_aliases"] = aliases
    call = pl.pallas_call(
        body, name=name, out_shape=out_shape, scratch_shapes=list(scratch_shapes),
        compiler_params=pltpu.CompilerParams(dimension_semantics=semantics, vmem_limit_bytes=VMEM_LIMIT),
        interpret=False, **kwargs)
    return call if after is None else (lambda *operands: call(*operands, after))


def _dot(a, b):
    return jnp.dot(a, b, preferred_element_type=F32)


def _dot_nt(a, b):
    return lax.dot_general(a, b, (((1,), (1,)), ((), ())), preferred_element_type=F32)


def _dot_tn(a, b):
    return lax.dot_general(a, b, (((0,), (0,)), ((), ())), preferred_element_type=F32)


def _dot_01_l(m_bf16, x):
    x1 = x.astype(BF16)
    x2 = (x - x1.astype(F32)).astype(BF16)
    return _dot(jnp.concatenate([m_bf16, m_bf16], axis=1), jnp.concatenate([x1, x2], axis=0))


def _sigmoid(x):
    return 1.0 / (1.0 + jnp.exp(-x))


def _silu_and_grad(x):
    s = _sigmoid(x)
    return x * s, s * (1.0 + x * (1.0 - s))


LOG2E = 1.4426950408889634
MASKED_SCORE = -1e30


def _softplus2_parts(z2):
    minus_abs = lax.bitcast_convert_type(lax.bitcast_convert_type(z2, jnp.int32) | jnp.int32(-2 ** 31), F32)
    sp2 = jnp.maximum(z2, 0.0) + jnp.log2(1.0 + jnp.exp2(minus_abs))
    return sp2, jnp.exp2(z2 - sp2)


def _split2_lanes(x):
    x1 = x.astype(BF16)
    return jnp.concatenate([x1, (x - x1.astype(F32)).astype(BF16)], axis=1)


def _iota2(shape, dim):
    return lax.broadcasted_iota(jnp.int32, shape, dim)


def _standardize(x):
    mu = jnp.mean(x, axis=-1, keepdims=True)
    xc = x - mu
    var = jnp.mean(xc * xc, axis=-1, keepdims=True)
    rstd = lax.rsqrt(var + LN_EPS)
    return xc * rstd, rstd


def _standardize_bwd(xhat, rstd, dxhat):
    m1 = jnp.mean(dxhat, axis=-1, keepdims=True)
    m2 = jnp.mean(dxhat * xhat, axis=-1, keepdims=True)
    return rstd * (dxhat - m1 - xhat * m2)


def _my_index():
    return 4 * lax.axis_index("x") + 2 * lax.axis_index("y") + lax.axis_index("c")


def _exchange(name, ins, out_shapes, transfers, in_vmem, after=None):
    n_in, n_out, n_t = len(ins), len(out_shapes), len(transfers)

    def body(*refs):
        n_skip = n_in + (0 if after is None else 1)
        in_refs, out_refs = refs[:n_in], refs[n_skip:n_skip + n_out]
        send_sems, recv_sems, local_sems = refs[n_skip + n_out:]
        x, y, c = lax.axis_index("x"), lax.axis_index("y"), lax.axis_index("c")
        me = 4 * x + 2 * y + c
        started = []
        for t, (i, o, src_fn, dst_fn) in enumerate(transfers):
            own = pltpu.make_async_copy(src_fn(in_refs[i], me), dst_fn(out_refs[o], me), local_sems.at[t])
            own.start()
            started.append(own)
        arrivals = []
        for k in range(1, NDEV):
            px = x ^ ((k >> 2) & 1)
            py = y ^ ((k >> 1) & 1)
            pc = c ^ (k & 1)
            peer = 4 * px + 2 * py + pc
            for t, (i, o, src_fn, dst_fn) in enumerate(transfers):
                sem = t * (NDEV - 1) + k - 1
                push = pltpu.make_async_remote_copy(
                    src_ref=src_fn(in_refs[i], peer), dst_ref=dst_fn(out_refs[o], me),
                    send_sem=send_sems.at[sem], recv_sem=recv_sems.at[sem],
                    device_id=(px, py, pc), device_id_type=MESH)
                push.start()
                started.append(push)
                arrivals.append(pltpu.make_async_remote_copy(
                    src_ref=src_fn(in_refs[i], peer), dst_ref=dst_fn(out_refs[o], peer),
                    send_sem=send_sems.at[sem], recv_sem=recv_sems.at[sem],
                    device_id=(px, py, pc), device_id_type=MESH))
        for arrival in arrivals:
            arrival.wait_recv()
        for cp in started[n_t:]:
            cp.wait_send()
        for own in started[:n_t]:
            own.wait()

    space = pltpu.VMEM if in_vmem else pl.ANY
    spec = pl.BlockSpec(memory_space=space)
    extra = [] if after is None else [after]
    return _pcall(
        body, name=name, out_shape=out_shapes,
        in_specs=[spec] * n_in + [pl.BlockSpec(memory_space=pl.ANY)] * len(extra), out_specs=[spec] * n_out,
        scratch_shapes=[pltpu.SemaphoreType.DMA((n_t * (NDEV - 1),)),
                        pltpu.SemaphoreType.DMA((n_t * (NDEV - 1),)),
                        pltpu.SemaphoreType.DMA((n_t,))])(*ins, *extra)


def _whole(ref, dev):
    return ref


def _slot(ref, dev):
    return ref.at[dev]


def _all_gather_small(name, v, after=None):
    out = _exchange(name, [v], [jax.ShapeDtypeStruct((NDEV,) + v.shape, v.dtype)],
                    [(0, 0, _whole, _slot)], in_vmem=True, after=after)
    return out[0]


_HBM_SPEC = pl.BlockSpec(memory_space=pltpu.HBM)
_SEM_SPEC = pl.BlockSpec(memory_space=pltpu.SEMAPHORE)
_DATAFLOW = pltpu.SideEffectType.DATAFLOW_SIDE_EFFECTING


def _peer(x, y, c, k):
    px = x ^ ((k >> 2) & 1)
    py = y ^ ((k >> 1) & 1)
    pc = c ^ (k & 1)
    return (px, py, pc), 4 * px + 2 * py + pc


def _direct_sends(transfers):
    sends = []
    for k in range(1, NDEV):
        for i, o, src_fn, dst_fn in transfers:
            sends.append((k,
                          lambda ins, lands, me, i=i, k=k, src_fn=src_fn: src_fn(ins[i], me ^ k),
                          lambda lands, me, o=o, dst_fn=dst_fn: dst_fn(lands[o], me),
                          lambda lands, me, o=o, k=k, dst_fn=dst_fn: dst_fn(lands[o], me ^ k)))
    return sends


def _exchange_start(name, ins, lands, sends, after=None):
    n_in, n_buf = len(ins), len(ins) + len(lands)
    n_sem = len(sends)

    def body(*refs):
        in_refs, land_refs = refs[:n_in], refs[n_in:n_buf]
        n_skip = n_buf + (0 if after is None else 1)
        send_sems, recv_sems, token = refs[n_skip], refs[n_skip + 1], refs[-1]
        x, y, c = lax.axis_index("x"), lax.axis_index("y"), lax.axis_index("c")
        me = 4 * x + 2 * y + c
        for t, (k, src_fn, dst_fn, _) in enumerate(sends):
            pltpu.make_async_remote_copy(
                src_ref=src_fn(in_refs, land_refs, me), dst_ref=dst_fn(land_refs, me),
                send_sem=send_sems.at[t], recv_sem=recv_sems.at[t],
                device_id=_peer(x, y, c, k)[0], device_id_type=MESH).start()
        token[...] = jnp.zeros_like(token)

    bufs = [pltpu.with_memory_space_constraint(a, pltpu.HBM) for a in list(ins) + list(lands)]
    extra = [] if after is None else [after]
    outs = pl.pallas_call(
        body, name=name,
        out_shape=(pltpu.SemaphoreType.DMA((n_sem,)), pltpu.SemaphoreType.DMA((n_sem,)))
        + tuple(pltpu.HBM(a.shape, a.dtype) for a in bufs) + (jax.ShapeDtypeStruct((8, BLK), F32),),
        in_specs=[_HBM_SPEC] * n_buf + [pl.BlockSpec(memory_space=pl.ANY)] * len(extra),
        out_specs=(_SEM_SPEC, _SEM_SPEC) + (_HBM_SPEC,) * n_buf + (pl.BlockSpec(memory_space=pltpu.VMEM),),
        input_output_aliases={b: 2 + b for b in range(n_buf)},
        compiler_params=pltpu.CompilerParams(has_side_effects=_DATAFLOW),
        interpret=False)(*bufs, *extra)
    return outs[0], outs[1], list(outs[2:2 + n_in]), list(outs[2 + n_in:2 + n_buf]), outs[-1]


def _exchange_wait(name, started, after, sends):
    send_sems, recv_sems, ins, lands, _ = started
    n_in, n_buf = len(ins), len(ins) + len(lands)

    def body(*refs):
        in_refs, land_refs = refs[:n_in], refs[n_in:n_buf]
        send_sems, recv_sems = refs[n_buf], refs[n_buf + 1]
        x, y, c = lax.axis_index("x"), lax.axis_index("y"), lax.axis_index("c")
        me = 4 * x + 2 * y + c
        for t, (k, src_fn, _, rcv_fn) in enumerate(sends):
            cp = pltpu.make_async_remote_copy(
                src_ref=src_fn(in_refs, land_refs, me), dst_ref=rcv_fn(land_refs, me),
                send_sem=send_sems.at[t], recv_sem=recv_sems.at[t],
                device_id=_peer(x, y, c, k)[0], device_id_type=MESH)
            cp.wait_send()
            cp.wait_recv()

    bufs = list(ins) + list(lands)
    outs = pl.pallas_call(
        body, name=name, out_shape=tuple(pltpu.HBM(a.shape, a.dtype) for a in bufs),
        in_specs=[_HBM_SPEC] * n_buf + [_SEM_SPEC, _SEM_SPEC, pl.BlockSpec(memory_space=pl.ANY)],
        out_specs=(_HBM_SPEC,) * n_buf,
        input_output_aliases={b: b for b in range(n_buf)},
        compiler_params=pltpu.CompilerParams(has_side_effects=_DATAFLOW),
        interpret=False)(*bufs, send_sems, recv_sems, after)
    return list(outs[:n_in]), list(outs[n_in:])


def _place_own(shape, dtype, own, start):
    return lax.dynamic_update_slice(lax.empty(shape, dtype), own, start)


def _place_own_window(name, shape, own, me):
    rows, cols = own.shape

    def body(me_ref, zone_in, own_ref, zone_ref):
        del me_ref, zone_in
        zone_ref[...] = own_ref[...]

    return pl.pallas_call(
        body, name=name, out_shape=jax.ShapeDtypeStruct(shape, own.dtype),
        grid_spec=pltpu.PrefetchScalarGridSpec(
            num_scalar_prefetch=1, grid=(1,),
            in_specs=[pl.BlockSpec(memory_space=pl.ANY), pl.BlockSpec((rows, cols), lambda i, me_ref: (0, 0))],
            out_specs=pl.BlockSpec((rows, cols), lambda i, me_ref: (0, me_ref[0]))),
        input_output_aliases={1: 0},
        compiler_params=pltpu.CompilerParams(dimension_semantics=("arbitrary",), vmem_limit_bytes=VMEM_LIMIT),
        interpret=False)(me.reshape(1).astype(jnp.int32), lax.empty(shape, own.dtype), own)


def _mod_fwd(c_all, w_mod, b_mod_mine):
    n_layers, _, cm = w_mod.shape

    def body(c_ref, w_ref, b_ref, o_ref):
        for l in range(n_layers):
            o_ref[l] = jnp.dot(c_ref[...], w_ref[l], preferred_element_type=F32,
                               precision=lax.Precision.HIGHEST) + b_ref[l]

    return _pcall(body, name="mod_fwd", out_shape=jax.ShapeDtypeStruct((n_layers, NDEV, cm), F32))(
        c_all, w_mod, b_mod_mine)


def _ln_proj(x, shift, scale, w_full, name):
    s_len, d = x.shape
    n = w_full.shape[1]
    tm = min(1024, s_len)
    tn = 2304

    def body(x_ref, sh_ref, sc_ref, w_ref, proj_ref, ht_ref, h_scr):
        @pl.when(pl.program_id(1) == 0)
        def _():
            xs, _ = _standardize(x_ref[...])
            h = xs * (1.0 + sc_ref[...]) + sh_ref[...]
            h_scr[...] = h.astype(BF16)
            ht_ref[...] = h.T.astype(BF16)

        proj_ref[...] = _dot(h_scr[...], w_ref[...])

    return _pcall(
        body, name=name,
        out_shape=(jax.ShapeDtypeStruct((s_len, n), F32), jax.ShapeDtypeStruct((d, s_len), BF16)),
        grid=(s_len // tm, n // tn),
        in_specs=[pl.BlockSpec((tm, d), lambda i, j: (i, 0)),
                  pl.BlockSpec((1, d), lambda i, j: (0, 0)),
                  pl.BlockSpec((1, d), lambda i, j: (0, 0)),
                  pl.BlockSpec((d, tn), lambda i, j: (0, j))],
        out_specs=(pl.BlockSpec((tm, tn), lambda i, j: (i, j)),
                   pl.BlockSpec((d, tm), lambda i, j: (0, i))),
        scratch_shapes=[pltpu.VMEM((tm, d), BF16)],
        semantics=("arbitrary", "arbitrary"))(x, shift, scale, w_full)


SB_Q_ROWS = 256
SB_K_BLOCKS = 2


def _sb_fwd(proj, name):
    s_len = proj.shape[0]
    n_pairs = WIDTH // BLK
    qr = min(SB_Q_ROWS, s_len)
    gb = SB_K_BLOCKS
    kw = gb * BLK
    nq = s_len // qr
    assert qr == kw

    def body(q_ref, k_ref, v_ref, o_ref, tot_ref):
        lane = _iota2((1, BLK), 1)
        row = _iota2((BLK, BLK), 0)
        col = _iota2((BLK, BLK), 1)
        half = jnp.concatenate([(row >= col).astype(BF16), jnp.ones((BLK, BLK), BF16)], axis=1)
        suffix_and_sum = jnp.concatenate([half, half], axis=0)
        strict = _iota2((qr, kw), 1) < _iota2((qr, kw), 0)
        head_lanes = [(lane // SB_HEAD_DIM) == hh for hh in range(2)]

        def scores(gi, qms, masked):
            c0 = pl.multiple_of(gi * kw, kw)
            kb = k_ref[pl.ds(c0, kw), :].astype(BF16)
            z2s = [_dot_nt(qms[hh], kb) for hh in range(2)]
            if masked:
                z2s = [jnp.where(strict, z2, MASKED_SCORE) for z2 in z2s]
            return tuple(z2s)

        def accumulate(gi, z2s, carry):
            c0 = pl.multiple_of(gi * kw, kw)
            vb = v_ref[pl.ds(c0, kw), :].astype(BF16)
            sp2s = [_softplus2_parts(z2)[0] for z2 in z2s]
            terms = [[_split2_lanes(sp2[:, b * BLK:(b + 1) * BLK]) for b in range(gb)] for sp2 in sp2s]
            sums = [[_dot(t, suffix_and_sum) for t in head_terms] for head_terms in terms]
            weights, laters = [], []
            for hh in range(2):
                later = carry[2 * hh + 1]
                parts = [None] * gb
                for b in reversed(range(gb)):
                    parts[b] = sums[hh][b][:, :BLK] + later
                    later = later + sums[hh][b][:, BLK:]
                weights.append(jnp.exp2(z2s[hh] - jnp.concatenate(parts, axis=1)).astype(BF16))
                laters.append(later)
            outs = [_dot(weights[hh], vb) for hh in range(2)]
            return (carry[0] + outs[0], laters[0], carry[2] + outs[1], laters[1])

        def queries(i):
            qf = q_ref[pl.ds(pl.multiple_of(i * qr, qr), qr), :] * (SB_HEAD_DIM ** -0.5 * LOG2E)
            return [jnp.where(head_lanes[hh], qf, 0.0).astype(BF16) for hh in range(2)]

        def qtile(i, first_scores):
            r0 = pl.multiple_of(i * qr, qr)
            qms = queries(i)
            zero = jnp.zeros((qr, BLK), F32)

            def step(jj, state):
                gi = i - 1 - jj
                return scores(gi, qms, False) + accumulate(gi + 1, state[:2], state[2:])

            state = lax.fori_loop(0, i, step, first_scores + (zero,) * 4)
            nxt = jnp.minimum(i + 1, nq - 1)
            next_scores = scores(nxt, queries(nxt), True)
            carry = accumulate(0, state[:2], state[2:])
            o_ref[pl.ds(r0, qr), :] = jnp.where(head_lanes[0], carry[0], carry[2])
            tot_ref[0, pl.ds(r0, qr), :] = carry[1]
            tot_ref[1, pl.ds(r0, qr), :] = carry[3]
            return next_scores

        lax.fori_loop(0, nq, qtile, scores(0, queries(0), True))

    col_spec = lambda off: pl.BlockSpec((s_len, BLK), lambda p: (0, off + p))
    return _pcall(
        body, name=name,
        out_shape=(jax.ShapeDtypeStruct((s_len, WIDTH), F32),
                   jax.ShapeDtypeStruct((2 * n_pairs, s_len, BLK), F32)),
        grid=(n_pairs,),
        in_specs=[col_spec(0), col_spec(n_pairs), col_spec(2 * n_pairs)],
        out_specs=(pl.BlockSpec((s_len, BLK), lambda p: (0, p)),
                   pl.BlockSpec((2, s_len, BLK), lambda p: (p, 0, 0))),
        semantics=("arbitrary",))(proj, proj, proj)


def _hg_masks(mask_ref):
    row = _iota2((BLK, BLK), 0)
    col = _iota2((BLK, BLK), 1)
    for v, m in enumerate(HG_LEVELS):
        same = (row // (2 * m)) == (col // (2 * m))
        mask_ref[v] = (same & ((row & m) != 0) & ((col & m) == 0)).astype(F32)


def _hg_mid(b, m):
    if m >= 4:
        n = BLK // (2 * m)
        mid = b.reshape(n, 2 * m, BLK)[:, m - 1:m, :]
        return jnp.broadcast_to(mid, (n, 2 * m, BLK)).reshape(BLK, BLK)
    pos = _iota2((BLK, BLK), 0) & (2 * m - 1)
    out = b
    for p in range(2 * m):
        delta = (m - 1) - p
        if delta != 0:
            out = jnp.where(pos == p, pltpu.roll(b, (-delta) % BLK, 0), out)
    return out


def _hg_chunk_inputs(qraw, fpre, lb):
    sig = _sigmoid(fpre)
    f = lb + (1.0 - lb) * sig
    g = jnp.log(f)
    q, dq_fac = _silu_and_grad(qraw)
    return q, dq_fac, f, sig, g


HG_GROUP = 4


def _neg_abs(x):
    return lax.bitcast_convert_type(lax.bitcast_convert_type(x, jnp.int32) | jnp.int32(-2 ** 31), F32)


def _hg_level_terms(qs, ks, bs, m):
    es = [jnp.exp(_neg_abs(b - _hg_mid(b, m))) for b in bs]
    qts = [(q * e).astype(BF16) for q, e in zip(qs, es)]
    kts = [(k * e).astype(BF16) for k, e in zip(ks, es)]
    return es, qts, kts


def _hg_load(refs, r0, lb_v, lower_incl):
    q_ref, f_ref, i_ref = refs
    heads = []
    for h in range(HG_GROUP):
        sl = slice(h * HG_HEAD_DIM, (h + 1) * HG_HEAD_DIM)
        heads.append(_hg_chunk_inputs(q_ref[pl.ds(r0, BLK), sl], f_ref[pl.ds(r0, BLK), sl], lb_v[:, sl])
                     + (i_ref[pl.ds(r0, BLK), sl],))
    bs = [_dot_01_l(lower_incl, hd[4]) for hd in heads]
    return heads, bs


def _hgrn_fwd(proj, lb, name, after=None):
    s_len = proj.shape[0]
    nc = s_len // BLK
    gw = HG_GROUP * HG_HEAD_DIM
    n_groups = WIDTH // gw
    base = 4 * WIDTH // gw

    def body(q_ref, f_ref, i_ref, lb_ref, o_ref, mask_ref):
        _hg_masks(mask_ref)
        row = _iota2((BLK, BLK), 0)
        col = _iota2((BLK, BLK), 1)
        lower_incl = (col <= row).astype(BF16)
        lb_v = lb_ref[...]

        def chunk(ci, sts):
            r0 = pl.multiple_of(ci * BLK, BLK)
            heads, bs = _hg_load((q_ref, f_ref, i_ref), r0, lb_v, lower_incl)
            qs = [hd[0] for hd in heads]
            ks = [1.0 - hd[2] for hd in heads]
            vs = [hd[5] for hd in heads]
            vbs = [v.astype(BF16) for v in vs]
            b_ends = [b[BLK - 1:BLK, :] for b in bs]
            inters = [_dot_nt((q * jnp.exp(b)).astype(BF16), st.astype(BF16)) for q, b, st in zip(qs, bs, sts)]
            scs = [None] * HG_GROUP
            for v_idx, m in enumerate(HG_LEVELS):
                _, qts, kts = _hg_level_terms(qs, ks, bs, m)
                terms = [_dot_nt(qt, kt) for qt, kt in zip(qts, kts)]
                msk = mask_ref[v_idx]
                scs = [t * msk if sc is None else sc + t * msk for sc, t in zip(scs, terms)]
            intras = [_dot(sc.astype(BF16), vb) for sc, vb in zip(scs, vbs)]
            k_decs = [(k * jnp.exp(b_end - b)).astype(BF16) for k, b, b_end in zip(ks, bs, b_ends)]
            grown = [_dot_tn(vb, k_dec) for vb, k_dec in zip(vbs, k_decs)]
            for h in range(HG_GROUP):
                diag = jnp.sum(qs[h] * ks[h], axis=-1, keepdims=True)
                o_ref[pl.ds(r0, BLK), h * HG_HEAD_DIM:(h + 1) * HG_HEAD_DIM] = inters[h] + intras[h] + diag * vs[h]
            return tuple(st * jnp.exp(b_end) + g for st, b_end, g in zip(sts, b_ends, grown))

        lax.fori_loop(0, nc, chunk, (jnp.zeros((HG_HEAD_DIM, HG_HEAD_DIM), F32),) * HG_GROUP)

    col_spec = lambda off: pl.BlockSpec((s_len, gw), lambda h: (0, off + h))
    return _pcall(
        body, name=name, out_shape=jax.ShapeDtypeStruct((s_len, WIDTH), F32),
        grid=(n_groups,),
        in_specs=[col_spec(base), col_spec(base + n_groups), col_spec(base + 2 * n_groups),
                  pl.BlockSpec((1, gw), lambda h: (0, h))],
        out_specs=pl.BlockSpec((s_len, gw), lambda h: (0, h)),
        scratch_shapes=[pltpu.VMEM((len(HG_LEVELS), BLK, BLK), F32)],
        semantics=("arbitrary",), after=after)(proj, proj, proj, lb)


def _rms_heads(o_b, norm_w):
    n_parts, h_parts, r_parts = [], [], []
    for h in range(WIDTH // HG_HEAD_DIM):
        sl = slice(h * HG_HEAD_DIM, (h + 1) * HG_HEAD_DIM)
        o = o_b[:, sl]
        rstd = lax.rsqrt(jnp.mean(o * o, axis=-1, keepdims=True) + RMS_EPS)
        ohat = o * rstd
        h_parts.append(ohat)
        n_parts.append(ohat * norm_w[:, sl])
        r_parts.append(jnp.broadcast_to(rstd, o.shape))
    cat = lambda parts: jnp.concatenate(parts, axis=-1)
    return cat(n_parts), cat(h_parts), cat(r_parts)


def _shift_rows_down(halo, cur, k):
    tm = cur.shape[0]
    ext = jnp.concatenate([halo, cur], axis=0)
    return pltpu.roll(ext, k, 0)[8:8 + tm]


def _shift_rows_up(cur, halo, k):
    tm = cur.shape[0]
    ext = jnp.concatenate([cur, halo], axis=0)
    return pltpu.roll(ext, (tm + 8 - k) % (tm + 8), 0)[0:tm]


def _merge_fwd(x, proj, o_a, o_b, gate, norm_w, conv_w, wb, w_out, ln_g, ln_b, name, target=None, after=None):
    s_len, d = x.shape
    tm = min(256, s_len)
    hb = tm // 8
    n_in = 19 + (0 if target is None else 1)

    def body(*refs):
        (x_ref, oa_ref, za_ref, ob_ref, zb_ref, pre_ref, post_ref, u_ref, zc_ref, hpre_ref, hu_ref, g_ref,
         gate_ref, nw_ref, cw_ref, wb_ref, wo_ref, lg_ref, lbias_ref) = refs[:19]
        xn_ref, mg_ref, yc_ref = refs[n_in:n_in + 3]
        i = pl.program_id(0)
        sa, _ = _silu_and_grad(za_ref[...])
        y_a = (oa_ref[...] * sa).astype(BF16)
        n_b, _, _ = _rms_heads(ob_ref[...], nw_ref[...])
        sb, _ = _silu_and_grad(zb_ref[...])
        y_b = (n_b * sb).astype(BF16)
        a = pre_ref[...] * u_ref[...]
        halo = jnp.where(i > 0, hpre_ref[...] * hu_ref[...], 0.0)
        cw = cw_ref[...]
        conv = cw[0:1] * _shift_rows_down(halo, a, 2) + cw[1:2] * _shift_rows_down(halo, a, 1) + cw[2:3] * a
        sc, _ = _silu_and_grad(zc_ref[...])
        y_c = (post_ref[...] * conv * sc).astype(BF16)
        merged = None
        for k, yk in enumerate((y_a, y_b, y_c)):
            yc_ref[:, k * WIDTH:(k + 1) * WIDTH] = yk
            term = _sigmoid(g_ref[:, k * d:(k + 1) * d]) * _dot(yk, wb_ref[k])
            merged = term if merged is None else merged + term
        mb = merged.astype(BF16)
        mg_ref[...] = mb
        y = _dot(mb, wo_ref[...])
        r = ALPHA * x_ref[...] + (1.0 + gate_ref[...]) * y
        rhat, _ = _standardize(r)
        xn = rhat * lg_ref[...] + lbias_ref[...]
        if target is None:
            xn_ref[...] = xn
        else:
            t_ref, loss_ref = refs[19], refs[n_in + 3]

            @pl.when(i == 0)
            def _():
                loss_ref[...] = jnp.zeros_like(loss_ref)

            e = xn - t_ref[...]
            xn_ref[...] = e * (1.0 / d)
            part = jnp.sum(jnp.sum(e * e, axis=-1, keepdims=True), axis=0, keepdims=True)
            loss_ref[...] += part * (0.5 / d)

    wcol = lambda cb: pl.BlockSpec((tm, WIDTH), lambda i: (i, cb))
    halo_spec = lambda cb: pl.BlockSpec((8, WIDTH), lambda i: (jnp.maximum(i * hb - 1, 0), cb))
    vec = lambda w: pl.BlockSpec((1, w), lambda i: (0, 0))
    tile = pl.BlockSpec((tm, d), lambda i: (i, 0))
    with_loss = target is not None
    return _pcall(
        body, name=name,
        out_shape=(jax.ShapeDtypeStruct((s_len, d), F32), jax.ShapeDtypeStruct((s_len, d), BF16),
                   jax.ShapeDtypeStruct((s_len, 3 * WIDTH), BF16))
        + ((jax.ShapeDtypeStruct((1, 1), F32),) if with_loss else ()),
        grid=(s_len // tm,),
        in_specs=[tile,
                  wcol(0), wcol(3), wcol(0), wcol(7), wcol(8), wcol(9), wcol(10), wcol(11),
                  halo_spec(8), halo_spec(10),
                  pl.BlockSpec((tm, 3 * d), lambda i: (i, 2)),
                  vec(d), vec(WIDTH),
                  pl.BlockSpec((3, WIDTH), lambda i: (0, 0)),
                  pl.BlockSpec((3, WIDTH, d), lambda i: (0, 0, 0)),
                  pl.BlockSpec((d, d), lambda i: (0, 0)),
                  vec(d), vec(d)] + ([tile] if with_loss else []),
        out_specs=(tile, tile, pl.BlockSpec((tm, 3 * WIDTH), lambda i: (i, 0)))
        + ((pl.BlockSpec((1, 1), lambda i: (0, 0)),) if with_loss else ()),
        semantics=("arbitrary",), after=after)(x, o_a, proj, o_b, proj, proj, proj, proj, proj, proj, proj, proj,
                                  gate, norm_w, conv_w, wb, w_out, ln_g, ln_b, *([target] if with_loss else []))


def _merge_bwd(dxn, x, merged, ycat, proj, gate, wb, w_out, ln_g, name, after=None):
    s_len, d = x.shape
    tm = min(256, s_len)
    dsh = d // NDEV
    n_tiles = s_len // tm

    def body(dxn_ref, x_ref, mg_ref, yc_ref, g_ref, gate_ref, wb_ref, wo_ref, lg_ref,
             dres_ref, dyc_ref, dg_ref, gwo_out, gwb_out, vec_ref, gwo_ref, gwb_ref):
        @pl.when(pl.program_id(0) == 0)
        def _():
            gwo_ref[...] = jnp.zeros_like(gwo_ref)
            gwb_ref[...] = jnp.zeros_like(gwb_ref)
            vec_ref[...] = jnp.zeros_like(vec_ref)

        mb = mg_ref[...]
        one_gate = 1.0 + gate_ref[...]
        y = _dot(mb, wo_ref[...])
        r = ALPHA * x_ref[...] + one_gate * y
        rhat, rstd = _standardize(r)
        dxn = dxn_ref[...]
        dr = _standardize_bwd(rhat, rstd, dxn * lg_ref[...])
        vec_ref[0:1, :] += jnp.sum(dxn * rhat, axis=0, keepdims=True)
        vec_ref[1:2, :] += jnp.sum(dxn, axis=0, keepdims=True)
        vec_ref[2:3, :] += jnp.sum(dr * y, axis=0, keepdims=True)
        dres_ref[...] = ALPHA * dr
        dy = (one_gate * dr).astype(BF16)
        gwo_ref[...] += _dot_tn(mb, dy)
        dmerged = _dot_nt(dy, wo_ref[...])
        for k in range(3):
            yk = yc_ref[:, k * WIDTH:(k + 1) * WIDTH]
            sg = _sigmoid(g_ref[:, k * d:(k + 1) * d])
            pk = _dot(yk, wb_ref[k])
            dg_ref[:, k * d:(k + 1) * d] = (dmerged * pk * sg * (1.0 - sg)).astype(BF16)
            dpk = (dmerged * sg).astype(BF16)
            dyc_ref[:, k * WIDTH:(k + 1) * WIDTH] = _dot_nt(dpk, wb_ref[k])
            gwb_ref[k] += _dot_tn(yk, dpk)

        @pl.when(pl.program_id(0) == n_tiles - 1)
        def _():
            for o in range(NDEV):
                gwo_out[o] = gwo_ref[o * dsh:(o + 1) * dsh, :].astype(BF16)
                for k in range(3):
                    gwb_out[o, k] = gwb_ref[k, :, o * dsh:(o + 1) * dsh].astype(BF16)

    tile = lambda w: pl.BlockSpec((tm, w), lambda i: (i, 0))
    vec = pl.BlockSpec((1, d), lambda i: (0, 0))
    return _pcall(
        body, name=name,
        out_shape=(jax.ShapeDtypeStruct((s_len, d), F32), jax.ShapeDtypeStruct((s_len, 3 * WIDTH), F32),
                   jax.ShapeDtypeStruct(proj.shape, BF16), jax.ShapeDtypeStruct((NDEV, dsh, d), BF16),
                   jax.ShapeDtypeStruct((NDEV, 3, WIDTH, dsh), BF16), jax.ShapeDtypeStruct((8, d), F32)),
        grid=(n_tiles,),
        in_specs=[tile(d), tile(d), tile(d), tile(3 * WIDTH),
                  pl.BlockSpec((tm, 3 * d), lambda i: (i, 2)),
                  vec, pl.BlockSpec((3, WIDTH, d), lambda i: (0, 0, 0)),
                  pl.BlockSpec((d, d), lambda i: (0, 0)), vec],
        out_specs=(tile(d), tile(3 * WIDTH), pl.BlockSpec((tm, 3 * d), lambda i: (i, 2)),
                   pl.BlockSpec((NDEV, dsh, d), lambda i: (0, 0, 0)),
                   pl.BlockSpec((NDEV, 3, WIDTH, dsh), lambda i: (0, 0, 0, 0)),
                   pl.BlockSpec((8, d), lambda i: (0, 0))),
        scratch_shapes=[pltpu.VMEM((d, d), F32), pltpu.VMEM((3, WIDTH, d), F32)],
        semantics=("arbitrary",), after=after)(dxn, x, merged, ycat, proj, gate, wb, w_out, ln_g)


def _branch_bwd(dycat, proj, o_a, o_b, norm_w, conv_w, dproj, name, after=None):
    s_len = proj.shape[0]
    tm = min(256, s_len)
    hb = tm // 8
    n_tiles = s_len // tm

    def body(dya_ref, dyb_ref, dyc_ref, oa_ref, za_ref, ob_ref, zb_ref, pre_ref, post_ref, u_ref, zc_ref,
             hpre_ref, hu_ref, ndyc_ref, npost_ref, nzc_ref, nw_ref, cw_ref, dproj_in,
             dproj_ref, doa_ref, dob_ref, vec_ref, dza_scr, dzb_scr, dc_scr, sems):
        del dproj_in
        i = pl.program_id(0)

        @pl.when(i == 0)
        def _():
            vec_ref[...] = jnp.zeros_like(vec_ref)

        sa, dsa = _silu_and_grad(za_ref[...])
        dya = dya_ref[...]
        doa_ref[...] = dya * sa
        dza_scr[...] = (dya * oa_ref[...] * dsa).astype(BF16)
        nw = nw_ref[...]
        n_b, ohat, rstd = _rms_heads(ob_ref[...], nw)
        sb, dsb = _silu_and_grad(zb_ref[...])
        dyb = dyb_ref[...]
        dzb_scr[...] = (dyb * n_b * dsb).astype(BF16)
        dn = dyb * sb
        vec_ref[0:1, :] += jnp.sum(dn * ohat, axis=0, keepdims=True)
        dnw = dn * nw
        parts = []
        for h in range(WIDTH // HG_HEAD_DIM):
            sl = slice(h * HG_HEAD_DIM, (h + 1) * HG_HEAD_DIM)
            m2 = jnp.mean(dnw[:, sl] * ohat[:, sl], axis=-1, keepdims=True)
            parts.append(rstd[:, sl] * (dnw[:, sl] - ohat[:, sl] * m2))
        dob_ref[...] = jnp.concatenate(parts, axis=-1)
        cw = cw_ref[...]
        pre, u, post = pre_ref[...], u_ref[...], post_ref[...]
        a = pre * u
        halo = jnp.where(i > 0, hpre_ref[...] * hu_ref[...], 0.0)
        a1 = _shift_rows_down(halo, a, 1)
        a2 = _shift_rows_down(halo, a, 2)
        conv = cw[0:1] * a2 + cw[1:2] * a1 + cw[2:3] * a
        sc, dsc = _silu_and_grad(zc_ref[...])
        dyc = dyc_ref[...]
        dconv = dyc * post * sc
        nsc, _ = _silu_and_grad(nzc_ref[...])
        nxt = jnp.where(i < n_tiles - 1, ndyc_ref[...] * npost_ref[...] * nsc, 0.0)
        da = cw[2:3] * dconv + cw[1:2] * _shift_rows_up(dconv, nxt, 1) + cw[0:1] * _shift_rows_up(dconv, nxt, 2)
        dc_scr[:, 0 * WIDTH:1 * WIDTH] = (da * u).astype(BF16)
        dc_scr[:, 1 * WIDTH:2 * WIDTH] = (dyc * conv * sc).astype(BF16)
        dc_scr[:, 2 * WIDTH:3 * WIDTH] = (da * pre).astype(BF16)
        dc_scr[:, 3 * WIDTH:4 * WIDTH] = (dyc * post * conv * dsc).astype(BF16)
        vec_ref[1:2, :] += jnp.sum(dconv * a2, axis=0, keepdims=True)
        vec_ref[2:3, :] += jnp.sum(dconv * a1, axis=0, keepdims=True)
        vec_ref[3:4, :] += jnp.sum(dconv * a, axis=0, keepdims=True)
        rows = pl.ds(pl.multiple_of(i * tm, tm), tm)
        copies = [pltpu.make_async_copy(dza_scr, dproj_ref.at[rows, 3 * WIDTH:4 * WIDTH], sems.at[0]),
                  pltpu.make_async_copy(dzb_scr, dproj_ref.at[rows, 7 * WIDTH:8 * WIDTH], sems.at[1]),
                  pltpu.make_async_copy(dc_scr, dproj_ref.at[rows, 8 * WIDTH:12 * WIDTH], sems.at[2])]
        for cp in copies:
            cp.start()
        for cp in copies:
            cp.wait()

    wcol = lambda cb: pl.BlockSpec((tm, WIDTH), lambda i: (i, cb))
    prev = lambda cb: pl.BlockSpec((8, WIDTH), lambda i: (jnp.maximum(i * hb - 1, 0), cb))
    nxt = lambda cb: pl.BlockSpec((8, WIDTH), lambda i: (jnp.minimum((i + 1) * hb, s_len // 8 - 1), cb))
    anyspec = pl.BlockSpec(memory_space=pl.ANY)
    out = jax.ShapeDtypeStruct((s_len, WIDTH), F32)
    return _pcall(
        body, name=name,
        out_shape=(jax.ShapeDtypeStruct(dproj.shape, dproj.dtype), out, out, jax.ShapeDtypeStruct((8, WIDTH), F32)),
        grid=(n_tiles,),
        in_specs=[wcol(0), wcol(1), wcol(2), wcol(0), wcol(3), wcol(0), wcol(7), wcol(8), wcol(9), wcol(10), wcol(11),
                  prev(8), prev(10), nxt(2), nxt(9), nxt(11),
                  pl.BlockSpec((1, WIDTH), lambda i: (0, 0)), pl.BlockSpec((3, WIDTH), lambda i: (0, 0)), anyspec],
        out_specs=(anyspec, wcol(0), wcol(0), pl.BlockSpec((8, WIDTH), lambda i: (0, 0))),
        scratch_shapes=[pltpu.VMEM((tm, WIDTH), BF16), pltpu.VMEM((tm, WIDTH), BF16),
                        pltpu.VMEM((tm, 4 * WIDTH), BF16), pltpu.SemaphoreType.DMA((3,))],
        aliases={18: 0},
        semantics=("arbitrary",), after=after)(dycat, dycat, dycat, o_a, proj, o_b, proj, proj, proj, proj, proj,
                                  proj, proj, dycat, proj, proj, norm_w, conv_w, dproj)


def _sb_bwd(proj, do_a, totals, dproj, name):
    s_len = proj.shape[0]
    n_pairs = WIDTH // BLK
    scale = SB_HEAD_DIM ** -0.5
    qr = min(SB_Q_ROWS, s_len)
    gb = SB_K_BLOCKS
    kw = gb * BLK
    nq = s_len // qr
    assert qr == kw

    def body(q_ref, k_ref, v_ref, do_ref, tot_ref, dproj_in, dproj_ref, dq_ref, dk_ref, dv_ref, out_scr, sems):
        del dproj_in
        lane = _iota2((1, BLK), 1)
        row = _iota2((BLK, BLK), 0)
        col = _iota2((BLK, BLK), 1)
        ones = jnp.ones((BLK, BLK), BF16)
        twice = lambda m: jnp.concatenate([m, m], axis=0)
        before_and_sum = twice(jnp.concatenate([(row < col).astype(BF16), ones], axis=1))
        upto_and_sum = twice(jnp.concatenate([(row <= col).astype(BF16), ones], axis=1))
        strict = _iota2((qr, kw), 1) < _iota2((qr, kw), 0)
        head_lanes = [(lane // SB_HEAD_DIM) == hh for hh in range(2)]
        dk_ref[...] = jnp.zeros_like(dk_ref)
        dv_ref[...] = jnp.zeros_like(dv_ref)

        def scores(gi, qms, masked):
            c0 = pl.multiple_of(gi * kw, kw)
            kb = k_ref[pl.ds(c0, kw), :].astype(BF16)
            z2s = [_dot_nt(qms[hh], kb) for hh in range(2)]
            if masked:
                z2s = [jnp.where(strict, z2, MASKED_SCORE) for z2 in z2s]
            return tuple(z2s)

        def process(gi, z2s, qms, doms, totals_i, carry):
            c0 = pl.multiple_of(gi * kw, kw)
            kb = k_ref[pl.ds(c0, kw), :].astype(BF16)
            vb = v_ref[pl.ds(c0, kw), :].astype(BF16)
            das = [_dot_nt(doms[hh], vb) for hh in range(2)]
            halves = [_softplus2_parts(z2) for z2 in z2s]
            terms = [[_split2_lanes(sp2[:, b * BLK:(b + 1) * BLK]) for b in range(gb)] for sp2, _ in halves]
            sums = [[_dot(t, before_and_sum) for t in head_terms] for head_terms in terms]
            weights, gmats, l_befores = [], [], []
            for hh in range(2):
                l_before = carry[3 * hh + 1]
                parts = []
                for b in range(gb):
                    parts.append(totals_i[hh] - l_before - sums[hh][b][:, :BLK])
                    l_before = l_before + sums[hh][b][:, BLK:]
                a = jnp.exp2(z2s[hh] - jnp.concatenate(parts, axis=1))
                weights.append(a.astype(BF16))
                gmats.append(a * das[hh])
                l_befores.append(l_before)
            terms = [[_split2_lanes(g[:, b * BLK:(b + 1) * BLK]) for b in range(gb)] for g in gmats]
            sums = [[_dot(t, upto_and_sum) for t in head_terms] for head_terms in terms]
            dzs, g_befores = [], []
            for hh in range(2):
                g_before = carry[3 * hh + 2]
                parts = []
                for b in range(gb):
                    parts.append(g_before + sums[hh][b][:, :BLK])
                    g_before = g_before + sums[hh][b][:, BLK:]
                dzs.append((gmats[hh] - halves[hh][1] * jnp.concatenate(parts, axis=1)).astype(BF16))
                g_befores.append(g_before)
            dk_t = _dot_tn(jnp.concatenate(qms, axis=0), jnp.concatenate(dzs, axis=0))
            dv_t = _dot_tn(jnp.concatenate(doms, axis=0), jnp.concatenate(weights, axis=0))
            dqs = [_dot(dzs[hh], kb) for hh in range(2)]
            dk_ref[:, pl.ds(c0, kw)] += dk_t * (1.0 / LOG2E)
            dv_ref[:, pl.ds(c0, kw)] += dv_t
            return (carry[0] + dqs[0], l_befores[0], g_befores[0], carry[3] + dqs[1], l_befores[1], g_befores[1])

        def queries(i):
            qf = q_ref[pl.ds(pl.multiple_of(i * qr, qr), qr), :] * (scale * LOG2E)
            return [jnp.where(head_lanes[hh], qf, 0.0).astype(BF16) for hh in range(2)]

        def qtile(i, first_scores):
            r0 = pl.multiple_of(i * qr, qr)
            qms = queries(i)
            dof = do_ref[pl.ds(r0, qr), :]
            doms = [jnp.where(head_lanes[hh], dof, 0.0).astype(BF16) for hh in range(2)]
            totals_i = [tot_ref[hh, pl.ds(r0, qr), :] for hh in range(2)]
            zero = jnp.zeros((qr, BLK), F32)

            def step(gi, state):
                return scores(gi + 1, qms, False) + process(gi, state[:2], qms, doms, totals_i, state[2:])

            def before_diagonal(state):
                return scores(i, qms, True) + process(i - 1, state[:2], qms, doms, totals_i, state[2:])

            state = lax.fori_loop(0, i - 1, step, first_scores + (zero,) * 6)
            state = lax.cond(i > 0, before_diagonal, lambda st: st, state)
            nxt = jnp.minimum(i + 1, nq - 1)
            next_scores = scores(0, queries(nxt), False)
            carry = process(i, state[:2], qms, doms, totals_i, state[2:])
            dq_ref[pl.ds(r0, qr), :] = jnp.where(head_lanes[0], carry[0], carry[3]) * scale
            return next_scores

        lax.fori_loop(0, nq, qtile, scores(0, queries(0), True))
        pair = pl.program_id(0)
        copies = []
        for t, value in enumerate((dq_ref[...], dk_ref[...].T, dv_ref[...].T)):
            out_scr[t] = value.astype(BF16)
            col = pl.multiple_of((t * n_pairs + pair) * BLK, BLK)
            copies.append(pltpu.make_async_copy(out_scr.at[t], dproj_ref.at[:, pl.ds(col, BLK)], sems.at[t]))
            copies[-1].start()
        for cp in copies:
            cp.wait()

    col_spec = lambda off: pl.BlockSpec((s_len, BLK), lambda p: (0, off + p))
    anyspec = pl.BlockSpec(memory_space=pl.ANY)
    return _pcall(
        body, name=name, out_shape=jax.ShapeDtypeStruct(dproj.shape, dproj.dtype), grid=(n_pairs,),
        in_specs=[col_spec(0), col_spec(n_pairs), col_spec(2 * n_pairs), col_spec(0),
                  pl.BlockSpec((2, s_len, BLK), lambda p: (p, 0, 0)), anyspec],
        out_specs=anyspec,
        scratch_shapes=[pltpu.VMEM((s_len, BLK), F32), pltpu.VMEM((BLK, s_len), F32), pltpu.VMEM((BLK, s_len), F32),
                        pltpu.VMEM((3, s_len, BLK), BF16), pltpu.SemaphoreType.DMA((3,))],
        aliases={5: 0},
        semantics=("arbitrary",))(proj, proj, proj, do_a, totals, dproj)


def _hgrn_bwd(proj, do_b, lb, dproj, name):
    s_len = proj.shape[0]
    nc = s_len // BLK
    gw = HG_GROUP * HG_HEAD_DIM
    n_groups = WIDTH // gw
    base = 4 * WIDTH // gw
    heads_of = range(HG_GROUP)

    def body(q_ref, f_ref, i_ref, do_ref, lb_ref, dproj_in, dproj_ref, dlb_ref, mask_ref, st_ref, out_scr, sems):
        del dproj_in
        _hg_masks(mask_ref)
        row = _iota2((BLK, BLK), 0)
        col = _iota2((BLK, BLK), 1)
        lower_incl = (col <= row).astype(BF16)
        upper_incl = (col >= row).astype(BF16)
        lb_v = lb_ref[...]
        refs = (q_ref, f_ref, i_ref)

        def fwd_chunk(ci, sts):
            for h in heads_of:
                st_ref[ci, h] = sts[h]
            heads, bs = _hg_load(refs, pl.multiple_of(ci * BLK, BLK), lb_v, lower_incl)
            b_ends = [b[BLK - 1:BLK, :] for b in bs]
            k_decs = [((1.0 - hd[2]) * jnp.exp(b_end - b)).astype(BF16) for hd, b, b_end in zip(heads, bs, b_ends)]
            grown = [_dot_tn(hd[5].astype(BF16), k_dec) for hd, k_dec in zip(heads, k_decs)]
            return tuple(st * jnp.exp(b_end) + g for st, b_end, g in zip(sts, b_ends, grown))

        zero_state = (jnp.zeros((HG_HEAD_DIM, HG_HEAD_DIM), F32),) * HG_GROUP
        lax.fori_loop(0, nc, fwd_chunk, zero_state)

        def bwd_chunk(cc, carry):
            dsts, suffixes, dlbs = carry
            ci = nc - 1 - cc
            r0 = pl.multiple_of(ci * BLK, BLK)
            heads, bs = _hg_load(refs, r0, lb_v, lower_incl)
            qs = [hd[0] for hd in heads]
            fs = [hd[2] for hd in heads]
            ks = [1.0 - f for f in fs]
            vs = [hd[5] for hd in heads]
            vbs = [v.astype(BF16) for v in vs]
            dos = [do_ref[pl.ds(r0, BLK), h * HG_HEAD_DIM:(h + 1) * HG_HEAD_DIM] for h in heads_of]
            dobs = [do.astype(BF16) for do in dos]
            b_ends = [b[BLK - 1:BLK, :] for b in bs]
            e_qs = [jnp.exp(b) for b in bs]
            e_ks = [jnp.exp(b_end - b) for b, b_end in zip(bs, b_ends)]
            qes = [(q * e).astype(BF16) for q, e in zip(qs, e_qs)]
            khs = [(k * e).astype(BF16) for k, e in zip(ks, e_ks)]
            st_terms = [_split2_lanes(st_ref[ci, h]) for h in heads_of]
            ds_terms = [_split2_lanes(dst) for dst in dsts]
            dqes = [_dot(dob, t[:, :HG_HEAD_DIM]) + _dot(dob, t[:, HG_HEAD_DIM:]) for dob, t in zip(dobs, st_terms)]
            dkhs = [_dot(vb, t[:, :HG_HEAD_DIM]) + _dot(vb, t[:, HG_HEAD_DIM:]) for vb, t in zip(vbs, ds_terms)]
            dvs = [_dot_nt(kh, t[:, :HG_HEAD_DIM]) for kh, t in zip(khs, ds_terms)]
            grown = [_dot_tn(dob, qe) for dob, qe in zip(dobs, qes)]
            das = [_dot_nt(dob, vb) for dob, vb in zip(dobs, vbs)]
            dqs = [e * dqe for e, dqe in zip(e_qs, dqes)]
            dks = [e * dkh for e, dkh in zip(e_ks, dkhs)]
            dlogs = [qe.astype(F32) * dqe - kh.astype(F32) * dkh for qe, dqe, kh, dkh in zip(qes, dqes, khs, dkhs)]
            scs = [None] * HG_GROUP
            for v_idx, m in enumerate(HG_LEVELS):
                es, qms, kms = _hg_level_terms(qs, ks, bs, m)
                msk = mask_ref[v_idx]
                terms = [_dot_nt(qm, km) for qm, km in zip(qms, kms)]
                pms = [(da * msk).astype(BF16) for da in das]
                dqms = [_dot(pm, km) for pm, km in zip(pms, kms)]
                dkms = [_dot_tn(pm, qm) for pm, qm in zip(pms, qms)]
                scs = [t * msk if sc is None else sc + t * msk for sc, t in zip(scs, terms)]
                dqs = [dq + dqm * e for dq, dqm, e in zip(dqs, dqms, es)]
                dks = [dk + dkm * e for dk, dkm, e in zip(dks, dkms, es)]
                dlogs = [dl + (qm.astype(F32) * dqm - km.astype(F32) * dkm)
                         for dl, qm, dqm, km, dkm in zip(dlogs, qms, dqms, kms, dkms)]
            intras = [_dot_tn(sc.astype(BF16), dob) for sc, dob in zip(scs, dobs)]
            dgs = [_dot_01_l(upper_incl, dl) + sfx for dl, sfx in zip(dlogs, suffixes)]
            new_dlbs = []
            for h in heads_of:
                q, dq_fac, f, sig = heads[h][0], heads[h][1], heads[h][2], heads[h][3]
                a_diag = jnp.sum(dos[h] * vs[h], axis=-1, keepdims=True)
                s_diag = jnp.sum(q * ks[h], axis=-1, keepdims=True)
                dq = dqs[h] + a_diag * ks[h]
                dk = dks[h] + a_diag * q
                dv = dvs[h] + intras[h] + s_diag * dos[h]
                dfull = dgs[h] / f - dk
                sl = slice(h * HG_HEAD_DIM, (h + 1) * HG_HEAD_DIM)
                out_scr[0, pl.ds(r0, BLK), sl] = (dq * dq_fac).astype(BF16)
                out_scr[1, pl.ds(r0, BLK), sl] = (dfull * (1.0 - lb_v[:, sl]) * sig * (1.0 - sig)).astype(BF16)
                out_scr[2, pl.ds(r0, BLK), sl] = dv.astype(BF16)
                new_dlbs.append(dlbs[h] + jnp.sum(dfull * (1.0 - sig), axis=0, keepdims=True))
            new_dsts = tuple(dst * jnp.exp(b_end) + g for dst, b_end, g in zip(dsts, b_ends, grown))
            return new_dsts, tuple(dg[0:1, :] for dg in dgs), tuple(new_dlbs)

        zero_row = (jnp.zeros((1, HG_HEAD_DIM), F32),) * HG_GROUP
        _, _, dlbs = lax.fori_loop(0, nc, bwd_chunk, (zero_state, zero_row, zero_row))
        dlb_ref[...] = jnp.broadcast_to(jnp.concatenate(dlbs, axis=1), dlb_ref.shape)
        group = pl.program_id(0)
        copies = []
        for t in range(3):
            col = pl.multiple_of((base + t * n_groups + group) * gw, gw)
            copies.append(pltpu.make_async_copy(out_scr.at[t], dproj_ref.at[:, pl.ds(col, gw)], sems.at[t]))
            copies[-1].start()
        for cp in copies:
            cp.wait()

    col_spec = lambda off: pl.BlockSpec((s_len, gw), lambda h: (0, off + h))
    anyspec = pl.BlockSpec(memory_space=pl.ANY)
    return _pcall(
        body, name=name,
        out_shape=(jax.ShapeDtypeStruct(dproj.shape, dproj.dtype), jax.ShapeDtypeStruct((8, WIDTH), F32)),
        grid=(n_groups,),
        in_specs=[col_spec(base), col_spec(base + n_groups), col_spec(base + 2 * n_groups), col_spec(0),
                  pl.BlockSpec((1, gw), lambda h: (0, h)), anyspec],
        out_specs=(anyspec, pl.BlockSpec((8, gw), lambda h: (0, h))),
        scratch_shapes=[pltpu.VMEM((len(HG_LEVELS), BLK, BLK), F32),
                        pltpu.VMEM((nc, HG_GROUP, HG_HEAD_DIM, HG_HEAD_DIM), F32),
                        pltpu.VMEM((3, s_len, gw), BF16), pltpu.SemaphoreType.DMA((3,))],
        aliases={5: 0},
        semantics=("arbitrary",))(proj, proj, proj, do_b, lb, dproj)


def _dh_matmul(dproj, w_full, after, name):
    s_len, n = dproj.shape
    d = w_full.shape[0]
    tm = min(1024, s_len)
    tk = 4608

    def body(dp_ref, w_ref, after_ref, dh_ref):
        del after_ref
        part = _dot_nt(dp_ref[...], w_ref[...])

        @pl.when(pl.program_id(1) == 0)
        def _():
            dh_ref[...] = part

        @pl.when(pl.program_id(1) > 0)
        def _():
            dh_ref[...] += part

    return _pcall(
        body, name=name, out_shape=jax.ShapeDtypeStruct((s_len, d), F32),
        grid=(s_len // tm, n // tk),
        in_specs=[pl.BlockSpec((tm, tk), lambda i, k: (i, k)), pl.BlockSpec((d, tk), lambda i, k: (0, k)),
                  pl.BlockSpec(memory_space=pl.ANY)],
        out_specs=pl.BlockSpec((tm, d), lambda i, k: (i, 0)),
        semantics=("arbitrary", "arbitrary"))(dproj, w_full, after)


def _gw_matmul(h_t, dproj, name):
    d, s_len = h_t.shape
    n = dproj.shape[1]
    tn = 2304

    def body(ht_ref, dp_ref, gw_ref):
        gw_ref[...] = _dot(ht_ref[...], dp_ref[...]).astype(BF16)

    return _pcall(
        body, name=name, out_shape=jax.ShapeDtypeStruct((d, n), BF16),
        grid=(n // tn,),
        in_specs=[pl.BlockSpec((d, s_len), lambda j: (0, 0)), pl.BlockSpec((s_len, tn), lambda j: (0, j))],
        out_specs=pl.BlockSpec((d, tn), lambda j: (0, j)),
        semantics=("arbitrary",))(h_t, dproj)


def _ln_bwd(dh, x, scale, dres, name, after=None):
    s_len, d = x.shape
    tm = min(512, s_len)

    def body(dh_ref, x_ref, sc_ref, dres_ref, dx_ref, vec_ref):
        @pl.when(pl.program_id(0) == 0)
        def _():
            vec_ref[...] = jnp.zeros_like(vec_ref)

        dh = dh_ref[...]
        xs, rstd = _standardize(x_ref[...])
        vec_ref[0:1, :] += jnp.sum(dh, axis=0, keepdims=True)
        vec_ref[1:2, :] += jnp.sum(dh * xs, axis=0, keepdims=True)
        dx_ref[...] = _standardize_bwd(xs, rstd, dh * (1.0 + sc_ref[...])) + dres_ref[...]

    tile = pl.BlockSpec((tm, d), lambda i: (i, 0))
    return _pcall(body, name=name, grid=(s_len // tm,),
                  out_shape=(jax.ShapeDtypeStruct((s_len, d), F32), jax.ShapeDtypeStruct((8, d), F32)),
                  in_specs=[tile, tile, pl.BlockSpec((1, d), lambda i: (0, 0)), tile],
                  out_specs=(tile, pl.BlockSpec((8, d), lambda i: (0, 0))),
                  semantics=("arbitrary",), after=after)(dh, x, scale, dres)


def _wmod_grad(c_t, dmod):
    d = c_t.shape[0]
    n_layers, _, cm = dmod.shape

    def body(c_ref, dm_ref, o_ref):
        for l in range(n_layers):
            acc = None
            for b in range(NDEV):
                term = c_ref[:, b:b + 1] * dm_ref[l, b:b + 1, :]
                acc = term if acc is None else acc + term
            o_ref[l] = acc

    return _pcall(body, name="wmod_grad", out_shape=jax.ShapeDtypeStruct((n_layers, d, cm), F32))(c_t, dmod)


def _sum_adamw(parts, w, m, v, name, first_row=0, into=None, after=None):
    n_src, range_rows, cols = parts.shape
    rows = w.shape[0]
    tr = range_rows
    for cand in (512, 256, 128, 64, 32, 16, 8):
        if range_rows % cand == 0 and cand * cols * 4 <= (2 << 20):
            tr = cand
            break
    first_tile = first_row // tr
    assert first_row % tr == 0
    n_extra = (0 if into is None else 4) + (0 if after is None else 1)

    def body(p_ref, w_ref, m_ref, v_ref, *rest):
        g_ref, d_ref, nm_ref, nv_ref = rest[n_extra:]
        g = p_ref[0].astype(F32)
        for s in range(1, n_src):
            g = g + p_ref[s].astype(F32)
        g_ref[...] = g
        d_ref[...], nm_ref[...], nv_ref[...] = _adamw_step(g, w_ref[...], m_ref[...], v_ref[...])

    tile = pl.BlockSpec((tr, cols), lambda i: (i + first_tile, 0))
    anyspec = pl.BlockSpec(memory_space=pl.ANY)
    out = jax.ShapeDtypeStruct((rows, cols), F32)
    extra = ([] if into is None else list(into)) + ([] if after is None else [after])
    aliases = {} if into is None else {4 + k: k for k in range(4)}
    return _pcall(body, name=name, grid=(range_rows // tr,), out_shape=(out,) * 4,
                  in_specs=[pl.BlockSpec((n_src, tr, cols), lambda i: (0, i, 0)), tile, tile, tile]
                  + [anyspec] * len(extra),
                  out_specs=(tile,) * 4, aliases=aliases, semantics=("arbitrary",))(parts, w, m, v, *extra)


def _adamw_step(g, w, m, v):
    nm = ADAM_B1 * m + (1.0 - ADAM_B1) * g
    nv = ADAM_B2 * v + (1.0 - ADAM_B2) * (g * g)
    m_hat = nm / (1.0 - ADAM_B1 ** ADAM_STEP)
    v_hat = nv / (1.0 - ADAM_B2 ** ADAM_STEP)
    return -ADAM_LR * (m_hat / (jnp.sqrt(v_hat) + ADAM_EPS) + ADAM_WD * w), nm, nv


def _adamw_small(gs, ws, ms, vs):
    n = len(gs)

    def body(*refs):
        for p in range(n):
            results = _adamw_step(*(refs[k * n + p][...] for k in range(4)))
            for k in range(3):
                refs[(4 + k) * n + p][...] = results[k]

    shapes = [jax.ShapeDtypeStruct(w.shape, F32) for w in ws]
    outs = _pcall(body, name="adamw_small", out_shape=shapes * 3)(*gs, *ws, *ms, *vs)
    return [(outs[p], outs[n + p], outs[2 * n + p]) for p in range(n)]


def _sum_parts(parts, name):
    n_src = parts.shape[0]

    def body(p_ref, o_ref):
        acc = p_ref[0]
        for s in range(1, n_src):
            acc = acc + p_ref[s]
        o_ref[...] = acc

    return _pcall(body, name=name, out_shape=jax.ShapeDtypeStruct(parts.shape[1:], F32))(parts)


def _pair_sum(gw, stage, me, name):
    d = gw.shape[0]
    n_slots, _, shard = stage.shape

    def body(me_ref, g_ref, s_ref, own_ref, o_ref):
        del me_ref
        total = (g_ref[...].astype(F32) + s_ref[0].astype(F32)).astype(BF16)
        o_ref[0] = total

        @pl.when(pl.program_id(0) == 0)
        def _():
            own_ref[0] = total

    slot = pl.BlockSpec((1, d, shard), lambda jj, me_ref: (jj, 0, 0))
    out = jax.ShapeDtypeStruct(stage.shape, BF16)
    return pl.pallas_call(
        body, name=name, out_shape=(out, out),
        grid_spec=pltpu.PrefetchScalarGridSpec(
            num_scalar_prefetch=1, grid=(n_slots,),
            in_specs=[pl.BlockSpec((d, shard), lambda jj, me_ref: (0, me_ref[0] ^ (2 * jj))), slot],
            out_specs=(pl.BlockSpec((1, d, shard), lambda jj, me_ref: (0, 0, 0)), slot)),
        compiler_params=pltpu.CompilerParams(dimension_semantics=("arbitrary",), vmem_limit_bytes=VMEM_LIMIT),
        interpret=False)(me.reshape(1).astype(jnp.int32), gw, stage)


def _lower_bound_table(lower_bounds):
    p = jax.nn.softmax(lower_bounds.astype(F32), axis=0)
    return jnp.cumsum(p, axis=0) - p[0:1]


def _pad_rows(v, width):
    n = v.shape[0]
    rows = -(-n // width)
    rows = -(-rows // 8) * 8
    return jnp.pad(v, (0, rows * width - n)).reshape(rows, width)


def kernel(x, c, w_mod, b_mod, w_in, conv_w, hgrn_norm_w, lower_bounds, w_branch, w_out, ln_g, ln_b, loss_target, m_w_mod, m_b_mod, m_w_in, m_conv_w, m_hgrn_norm_w, m_lower_bounds, m_w_branch, m_w_out, m_ln_g, m_ln_b, v_w_mod, v_b_mod, v_w_in, v_conv_w, v_hgrn_norm_w, v_lower_bounds, v_w_branch, v_w_out, v_ln_g, v_ln_b):
    n_layers = N_LAYERS
    s_len, d = x.shape[1], x.shape[2]
    n_cols = w_in.shape[2] * NDEV
    cw_cols = conv_w.shape[2]
    cm = w_mod.shape[2]
    me = _my_index()
    x0 = x[0]
    target = loss_target[0]

    small = _pad_rows(jnp.concatenate([c.reshape(-1), conv_w.reshape(-1)]), BLK)
    small_all = _all_gather_small("gather_c_conv", small).reshape(NDEV, -1)
    c_all = small_all[:, :d]
    conv_full = small_all[:, d:d + n_layers * 3 * cw_cols].reshape(NDEV, n_layers, 3, cw_cols)
    conv_full = conv_full.transpose(1, 2, 0, 3).reshape(n_layers, 3, WIDTH)

    b_mod_mine = lax.dynamic_slice_in_dim(b_mod, me * cm, cm, axis=1).reshape(n_layers, 1, cm)
    mod_cols = _mod_fwd(c_all, w_mod, b_mod_mine).reshape(n_layers * NDEV, cm)

    shard = w_in.shape[2]
    dsh = d // NDEV
    w_in_b, w_branch_b, w_out_b = w_in.astype(BF16), w_branch.astype(BF16), w_out.astype(BF16)
    window = lambda ref, dev: ref.at[:, pl.ds(pl.multiple_of(dev * shard, BLK), shard)]

    def two_step_sends(places):
        chips, sibling = [], []
        for k in (1, 2, 4, 6):
            for a, place in enumerate(places):
                chips.append((k, lambda ins, lands, me, a=a: ins[a],
                              lambda lands, me, a=a, place=place: place(lands[a], me),
                              lambda lands, me, a=a, k=k, place=place: place(lands[a], me ^ k)))
        for j in (2, 4, 6):
            for a, place in enumerate(places):
                sibling.append((1, lambda ins, lands, me, a=a, j=j, place=place: place(lands[a], me ^ j),
                                lambda lands, me, a=a, j=j, place=place: place(lands[a], me ^ j),
                                lambda lands, me, a=a, j=j, place=place: place(lands[a], me ^ 1 ^ j)))
        return chips, sibling

    in_sends = two_step_sends([window])
    rest_sends = two_step_sends([_slot, _slot])
    layer_sends = two_step_sends([window, _slot, _slot])

    def in_land(l):
        return _place_own_window(f"place_w_in_{l}", (d, n_cols), w_in_b[l], me)

    def rest_lands(l):
        return [_place_own((NDEV, 3, WIDTH, dsh), BF16, w_branch_b[l][None], (me, 0, 0, 0)),
                _place_own((NDEV, dsh, d), BF16, w_out_b[l][None], (me, 0, 0))]

    def gather_start(name, shards, lands, sends, after):
        return _exchange_start(f"{name}_chips_start", shards, lands, sends[0], after)

    def gather_pass_on(name, started, after, sends):
        _, lands = _exchange_wait(f"{name}_chips_wait", started, after, sends[0])
        return _exchange_start(f"{name}_sibling_start", [], lands, sends[1])

    def gather_finish(name, started, after, sends):
        return _exchange_wait(f"{name}_sibling_wait", started, after, sends[1])[1]

    def branch_out_weights(w_branch_l, w_out_l):
        return w_branch_l.transpose(1, 2, 0, 3).reshape(3, WIDTH, d), w_out_l.reshape(d, d)

    mod_sends = [(k, lambda ins, lands, me: ins[1], lambda lands, me: lands[1].at[me],
                  lambda lands, me, k=k: lands[1].at[me ^ k]) for k in range(1, NDEV)]
    first_sends = (mod_sends + in_sends[0], in_sends[1])
    mod_land = _place_own((NDEV,) + mod_cols.shape, F32, mod_cols[None], (me, 0, 0))
    gathering = gather_start("gather_w_in_0", [w_in_b[0], mod_cols], [in_land(0), mod_land], first_sends, None)
    rest_gathering = gather_start("gather_rest_0", [w_branch_b[0], w_out_b[0]], rest_lands(0), rest_sends,
                                  gathering[4])
    next_gathering = None
    if n_layers > 1:
        next_gathering = gather_start("gather_weights_1", [w_in_b[1], w_branch_b[1], w_out_b[1]],
                                      [in_land(1)] + rest_lands(1), layer_sends, rest_gathering[4])
    _, (w_in_land, mod_all) = _exchange_wait("gather_w_in_0_chips_wait", gathering,
                                             (next_gathering or rest_gathering)[4], first_sends[0])
    passing = _exchange_start("gather_w_in_0_sibling_start", [], [w_in_land], in_sends[1])
    w_in_l = gather_finish("gather_w_in_0", passing, passing[4], in_sends)[0]
    mod_all = mod_all.reshape(NDEV, n_layers, NDEV, cm)
    mod_mine = lax.dynamic_index_in_dim(mod_all, me, axis=2, keepdims=False)
    mod_mine = mod_mine.transpose(1, 0, 2).reshape(n_layers, 3, 1, d)

    lbs = _lower_bound_table(lower_bounds)
    norm_w4 = jnp.tile(hgrn_norm_w, (1, WIDTH // HG_HEAD_DIM))

    saved = []
    xl = x0
    for l in range(n_layers):
        shift, scale, gate = mod_mine[l, 0], mod_mine[l, 1], mod_mine[l, 2]
        merge_after = None
        proj, h_t = _ln_proj(xl, shift, scale, w_in_l, f"ln_proj_{l}")
        o_a, totals = _sb_fwd(proj, f"sb_fwd_{l}")
        if l == 0:
            rest_passing = gather_pass_on("gather_rest_0", rest_gathering, o_a, rest_sends)
        o_b = _hgrn_fwd(proj, lbs[l:l + 1], f"hgrn_fwd_{l}", after=rest_passing[4] if l == 0 else None)
        if l == 0:
            wb_l, wo_l = branch_out_weights(*gather_finish("gather_rest_0", rest_passing, o_b, rest_sends))
            if n_layers > 1:
                next_passing = gather_pass_on("gather_weights_1", next_gathering, o_b, layer_sends)
                merge_after = next_passing[4]
        x_new, merged, ycat, *loss_term = _merge_fwd(
            xl, proj, o_a, o_b, gate, norm_w4[l:l + 1], conv_full[l], wb_l, wo_l, ln_g[l:l + 1], ln_b[l:l + 1],
            f"merge_fwd_{l}", target=target if l == n_layers - 1 else None, after=merge_after)
        saved.append((xl, proj, h_t, o_a, totals, o_b, merged, ycat, w_in_l, wb_l, wo_l))
        if l == 0 and n_layers > 1:
            w_in_l, w_branch_l, w_out_l = gather_finish("gather_weights_1", next_passing, x_new, layer_sends)
            wb_l, wo_l = branch_out_weights(w_branch_l, w_out_l)
        xl = x_new

    dx, loss_part = xl, loss_term[0]

    pair_sends = [(1, lambda ins, lands, me, j=j: window(ins[0], me ^ 1 ^ j),
                   lambda lands, me, jj=jj: lands[0].at[jj], lambda lands, me, jj=jj: lands[0].at[jj])
                  for jj, j in enumerate((0, 2, 4, 6))]
    chip_sum_sends = [(j, lambda ins, lands, me, jj=jj: ins[0].at[jj],
                       lambda lands, me, jj=jj: lands[0].at[jj], lambda lands, me, jj=jj: lands[0].at[jj])
                      for jj, j in ((1, 2), (2, 4), (3, 6))]
    rest_scatter = _direct_sends([(0, 0, _slot, _slot), (1, 1, _slot, _slot)])
    scattering = [None] * n_layers
    small_grads = [None] * n_layers
    dmod = [None] * n_layers
    tie = None
    for l in reversed(range(n_layers)):
        xl, proj, h_t, o_a, totals, o_b, merged, ycat, w_in_l, wb_l, wo_l = saved[l]
        scale, gate = mod_mine[l, 1], mod_mine[l, 2]
        dres, dycat, dproj, gwo_by_owner, gwb_by_owner, mvec = _merge_bwd(
            dx, xl, merged, ycat, proj, gate, wb_l, wo_l, ln_g[l:l + 1], f"merge_bwd_{l}", after=tie)
        lands = [_place_own((NDEV, 3, WIDTH, dsh), BF16, lax.dynamic_slice_in_dim(gwb_by_owner, me, 1, axis=0),
                            (me, 0, 0, 0)),
                 _place_own((NDEV, dsh, d), BF16, lax.dynamic_slice_in_dim(gwo_by_owner, me, 1, axis=0),
                            (me, 0, 0))]
        rest_started = _exchange_start(f"scatter_rest_{l}_start", [gwb_by_owner, gwo_by_owner], lands, rest_scatter)
        dproj, do_a, do_b, bvec = _branch_bwd(dycat, proj, o_a, o_b, norm_w4[l:l + 1], conv_full[l], dproj,
                                              f"branch_bwd_{l}", after=rest_started[4])
        dproj = _sb_bwd(proj, do_a, totals, dproj, f"sb_bwd_{l}")
        dproj, dlb = _hgrn_bwd(proj, do_b, lbs[l:l + 1], dproj, f"hgrn_bwd_{l}")
        gwi = _gw_matmul(h_t, dproj, f"gw_matmul_{l}")
        swapping = _exchange_start(f"scatter_in_{l}_sibling_start", [gwi], [lax.empty((4, d, shard), BF16)], pair_sends)
        if l > 0:
            dh = _dh_matmul(dproj, w_in_l, swapping[4], f"dh_matmul_{l}")
        (gwi,), (stage,) = _exchange_wait(f"scatter_in_{l}_sibling_wait", swapping, dh if l > 0 else swapping[4],
                                          pair_sends)
        land, chip_sums = _pair_sum(gwi, stage, me, f"pair_sum_{l}")
        in_started = _exchange_start(f"scatter_in_{l}_chips_start", [chip_sums], [land], chip_sum_sends)
        scattering[l] = (in_started, rest_started)
        tie = in_started[4]
        if l == 0:
            dh = _dh_matmul(dproj, w_in_l, tie, f"dh_matmul_{l}")
        dx, lvec = _ln_bwd(dh, xl, scale, dres, f"ln_bwd_{l}", after=tie)
        dmod[l] = jnp.concatenate([lvec[0], lvec[1], mvec[2]])
        norm_grad = bvec[0].reshape(WIDTH // HG_HEAD_DIM, HG_HEAD_DIM).sum(axis=0)
        small_grads[l] = jnp.concatenate([mvec[0], mvec[1], norm_grad, dlb[0], bvec[1:4].reshape(-1)])
    grad_x = dx[None]

    flat = lambda a: a.reshape(-1, a.shape[-1])
    big = {"w_in": (w_in, m_w_in, v_w_in), "w_branch": (w_branch, m_w_branch, v_w_branch),
           "w_out": (w_out, m_w_out, v_w_out)}
    big_results = {n: None for n in big}

    def adam_layer(l, after):
        in_started, rest_started = scattering[l]
        p_branch_l, p_out_l = _exchange_wait(f"scatter_rest_{l}_wait", rest_started, after, rest_scatter)[1]
        p_in_l = _exchange_wait(f"scatter_in_{l}_chips_wait", in_started, after, chip_sum_sends)[1][0]
        parts = {"w_in": p_in_l, "w_branch": p_branch_l.reshape(NDEV, 3 * WIDTH, dsh), "w_out": p_out_l}
        last = None
        for n, (w, m, v) in big.items():
            rows_per_layer = flat(w).shape[0] // n_layers
            big_results[n] = _sum_adamw(parts[n], flat(w), flat(m), flat(v), f"adamw_{n}_{l}",
                                        first_row=l * rows_per_layer, into=big_results[n], after=last)
            last = big_results[n][3]
        return last

    after_adam = None
    for l in reversed(range(1, n_layers)):
        after_adam = adam_layer(l, tie)

    small_vec = jnp.concatenate(dmod + small_grads + [loss_part.reshape(1)])
    n_small = small_vec.shape[0]
    small_all = _all_gather_small("gather_small_grads", _pad_rows(small_vec, BLK), after=after_adam)
    small_sum = _sum_parts(small_all, "sum_small_grads").reshape(-1)[:n_small]
    dmod_all = small_all.reshape(NDEV, -1)[:, :n_layers * 3 * d].reshape(NDEV, n_layers, 3 * d)

    loss = small_sum[n_small - 1]

    off = n_layers * 3 * d
    grad_b_mod = small_sum[:off].reshape(n_layers, 3 * d)
    per_layer = 2 * d + HG_HEAD_DIM + WIDTH + 3 * WIDTH
    g_ln_g, g_ln_b, g_norm, g_lbs, g_conv = [], [], [], [], []
    for l in range(n_layers):
        seg = small_sum[off + l * per_layer: off + (l + 1) * per_layer]
        g_ln_g.append(seg[:d])
        g_ln_b.append(seg[d:2 * d])
        g_norm.append(seg[2 * d:2 * d + HG_HEAD_DIM])
        g_lbs.append(seg[2 * d + HG_HEAD_DIM:2 * d + HG_HEAD_DIM + WIDTH])
        g_conv.append(seg[2 * d + HG_HEAD_DIM + WIDTH:].reshape(3, WIDTH))
    grad_ln_g, grad_ln_b = jnp.stack(g_ln_g), jnp.stack(g_ln_b)
    grad_norm = jnp.stack(g_norm)
    _, lbs_vjp = jax.vjp(_lower_bound_table, lower_bounds)
    grad_lower = lbs_vjp(jnp.stack(g_lbs))[0]
    grad_conv = lax.dynamic_slice_in_dim(jnp.stack(g_conv), me * cw_cols, cw_cols, axis=2)

    dmod_mine = lax.dynamic_slice_in_dim(dmod_all, me * cm, cm, axis=2).transpose(1, 0, 2)
    grad_w_mod = _wmod_grad(c_all.T, dmod_mine)

    adam_layer(0, grad_w_mod)
    r_w_in, r_w_branch, r_w_out = ([o.reshape(big[n][0].shape) for o in big_results[n]]
                                   for n in ("w_in", "w_branch", "w_out"))
    r_w_mod = [o.reshape(w_mod.shape) for o in
               _sum_adamw(grad_w_mod.reshape(1, -1, cm), flat(w_mod), flat(m_w_mod), flat(v_w_mod), "adamw_w_mod")]

    small_names = ["b_mod", "conv_w", "hgrn_norm_w", "lower_bounds", "ln_g", "ln_b"]
    small_g = [grad_b_mod, grad_conv, grad_norm, grad_lower, grad_ln_g, grad_ln_b]
    small_w = [b_mod, conv_w, hgrn_norm_w, lower_bounds, ln_g, ln_b]
    small_m = [m_b_mod, m_conv_w, m_hgrn_norm_w, m_lower_bounds, m_ln_g, m_ln_b]
    small_v = [v_b_mod, v_conv_w, v_hgrn_norm_w, v_lower_bounds, v_ln_g, v_ln_b]
    as_rows = lambda a: a.reshape(-1, a.shape[-1])
    updates = _adamw_small([as_rows(a) for a in small_g], [as_rows(a) for a in small_w],
                           [as_rows(a) for a in small_m], [as_rows(a) for a in small_v])
    r_small = {n: [g] + [u.reshape(w.shape) for u in upd]
               for n, g, w, upd in zip(small_names, small_g, small_w, updates)}

    results = {"w_mod": r_w_mod, "w_in": r_w_in, "w_branch": r_w_branch, "w_out": r_w_out, **r_small}
    order = ["w_mod", "b_mod", "w_in", "conv_w", "hgrn_norm_w", "lower_bounds", "w_branch", "w_out", "ln_g", "ln_b"]
    outs = [loss, grad_x]
    for idx in range(4):
        outs.extend(results[n][idx] for n in order)
    return tuple(outs)
```

```python
import jax
import jax.numpy as jnp
from jax import lax
from jax.experimental import pallas as pl
from jax.experimental.pallas import tpu as pltpu

F32 = jnp.float32
BF16 = jnp.bfloat16
NDEV = 8
N_LAYERS = 2
SB_HEAD_DIM = 64
HG_HEAD_DIM = 128
WIDTH = 512
BLK = 128
LN_EPS = 1e-5
RMS_EPS = 1e-6
ALPHA = (2.0 * N_LAYERS) ** 0.25
ADAM_LR, ADAM_B1, ADAM_B2, ADAM_EPS, ADAM_WD, ADAM_STEP = 0.001, 0.9, 0.999, 1e-08, 0.01, 10
VMEM_LIMIT = 56 * 1024 * 1024
MESH = pl.DeviceIdType.MESH
HG_LEVELS = (64, 32, 16, 8, 4, 2, 1)


def _pcall(body, *, name, out_shape, grid=None, in_specs=None, out_specs=None, scratch_shapes=(),
           semantics=None, aliases=None, after=None):
    if after is not None:
        n_in = len(in_specs)
        inner = body
        body = lambda *refs: inner(*refs[:n_in], *refs[n_in + 1:])
        in_specs = list(in_specs) + [pl.BlockSpec(memory_space=pl.ANY)]
    kwargs = {}
    if grid is not None:
        kwargs["grid"] = grid
    if in_specs is not None:
        kwargs["in_specs"] = in_specs
    if out_specs is not None:
        kwargs["out_specs"] = out_specs
    if aliases:
        kwargs["input_output_aliases"] = aliases
    call = pl.pallas_call(
        body, name=name, out_shape=out_shape, scratch_shapes=list(scratch_shapes),
        compiler_params=pltpu.CompilerParams(dimension_semantics=semantics, vmem_limit_bytes=VMEM_LIMIT),
        interpret=False, **kwargs)
    return call if after is None else (lambda *operands: call(*operands, after))


def _dot(a, b):
    return jnp.dot(a, b, preferred_element_type=F32)


def _dot_nt(a, b):
    return lax.dot_general(a, b, (((1,), (1,)), ((), ())), preferred_element_type=F32)


def _dot_tn(a, b):
    return lax.dot_general(a, b, (((0,), (0,)), ((), ())), preferred_element_type=F32)


def _dot_01_l(m_bf16, x):
    x1 = x.astype(BF16)
    x2 = (x - x1.astype(F32)).astype(BF16)
    return _dot(jnp.concatenate([m_bf16, m_bf16], axis=1), jnp.concatenate([x1, x2], axis=0))


def _sigmoid(x):
    return 1.0 / (1.0 + jnp.exp(-x))


def _silu_and_grad(x):
    s = _sigmoid(x)
    return x * s, s * (1.0 + x * (1.0 - s))


LOG2E = 1.4426950408889634
MASKED_SCORE = -1e30


def _softplus2_parts(z2):
    minus_abs = lax.bitcast_convert_type(lax.bitcast_convert_type(z2, jnp.int32) | jnp.int32(-2 ** 31), F32)
    sp2 = jnp.maximum(z2, 0.0) + jnp.log2(1.0 + jnp.exp2(minus_abs))
    return sp2, jnp.exp2(z2 - sp2)


def _split2_lanes(x):
    x1 = x.astype(BF16)
    return jnp.concatenate([x1, (x - x1.astype(F32)).astype(BF16)], axis=1)


def _iota2(shape, dim):
    return lax.broadcasted_iota(jnp.int32, shape, dim)


def _standardize(x):
    mu = jnp.mean(x, axis=-1, keepdims=True)
    xc = x - mu
    var = jnp.mean(xc * xc, axis=-1, keepdims=True)
    rstd = lax.rsqrt(var + LN_EPS)
    return xc * rstd, rstd


def _standardize_bwd(xhat, rstd, dxhat):
    m1 = jnp.mean(dxhat, axis=-1, keepdims=True)
    m2 = jnp.mean(dxhat * xhat, axis=-1, keepdims=True)
    return rstd * (dxhat - m1 - xhat * m2)


def _my_index():
    return 4 * lax.axis_index("x") + 2 * lax.axis_index("y") + lax.axis_index("c")


def _exchange(name, ins, out_shapes, transfers, in_vmem, after=None):
    n_in, n_out, n_t = len(ins), len(out_shapes), len(transfers)

    def body(*refs):
        n_skip = n_in + (0 if after is None else 1)
        in_refs, out_refs = refs[:n_in], refs[n_skip:n_skip + n_out]
        send_sems, recv_sems, local_sems = refs[n_skip + n_out:]
        x, y, c = lax.axis_index("x"), lax.axis_index("y"), lax.axis_index("c")
        me = 4 * x + 2 * y + c
        started = []
        for t, (i, o, src_fn, dst_fn) in enumerate(transfers):
            own = pltpu.make_async_copy(src_fn(in_refs[i], me), dst_fn(out_refs[o], me), local_sems.at[t])
            own.start()
            started.append(own)
        arrivals = []
        for k in range(1, NDEV):
            px = x ^ ((k >> 2) & 1)
            py = y ^ ((k >> 1) & 1)
            pc = c ^ (k & 1)
            peer = 4 * px + 2 * py + pc
            for t, (i, o, src_fn, dst_fn) in enumerate(transfers):
                sem = t * (NDEV - 1) + k - 1
                push = pltpu.make_async_remote_copy(
                    src_ref=src_fn(in_refs[i], peer), dst_ref=dst_fn(out_refs[o], me),
                    send_sem=send_sems.at[sem], recv_sem=recv_sems.at[sem],
                    device_id=(px, py, pc), device_id_type=MESH)
                push.start()
                started.append(push)
                arrivals.append(pltpu.make_async_remote_copy(
                    src_ref=src_fn(in_refs[i], peer), dst_ref=dst_fn(out_refs[o], peer),
                    send_sem=send_sems.at[sem], recv_sem=recv_sems.at[sem],
                    device_id=(px, py, pc), device_id_type=MESH))
        for arrival in arrivals:
            arrival.wait_recv()
        for cp in started[n_t:]:
            cp.wait_send()
        for own in started[:n_t]:
            own.wait()

    space = pltpu.VMEM if in_vmem else pl.ANY
    spec = pl.BlockSpec(memory_space=space)
    extra = [] if after is None else [after]
    return _pcall(
        body, name=name, out_shape=out_shapes,
        in_specs=[spec] * n_in + [pl.BlockSpec(memory_space=pl.ANY)] * len(extra), out_specs=[spec] * n_out,
        scratch_shapes=[pltpu.SemaphoreType.DMA((n_t * (NDEV - 1),)),
                        pltpu.SemaphoreType.DMA((n_t * (NDEV - 1),)),
                        pltpu.SemaphoreType.DMA((n_t,))])(*ins, *extra)


def _whole(ref, dev):
    return ref


def _slot(ref, dev):
    return ref.at[dev]


def _all_gather_small(name, v, after=None):
    out = _exchange(name, [v], [jax.ShapeDtypeStruct((NDEV,) + v.shape, v.dtype)],
                    [(0, 0, _whole, _slot)], in_vmem=True, after=after)
    return out[0]


_HBM_SPEC = pl.BlockSpec(memory_space=pltpu.HBM)
_SEM_SPEC = pl.BlockSpec(memory_space=pltpu.SEMAPHORE)
_DATAFLOW = pltpu.SideEffectType.DATAFLOW_SIDE_EFFECTING


def _peer(x, y, c, k):
    px = x ^ ((k >> 2) & 1)
    py = y ^ ((k >> 1) & 1)
    pc = c ^ (k & 1)
    return (px, py, pc), 4 * px + 2 * py + pc


def _direct_sends(transfers):
    sends = []
    for k in range(1, NDEV):
        for i, o, src_fn, dst_fn in transfers:
            sends.append((k,
                          lambda ins, lands, me, i=i, k=k, src_fn=src_fn: src_fn(ins[i], me ^ k),
                          lambda lands, me, o=o, dst_fn=dst_fn: dst_fn(lands[o], me),
                          lambda lands, me, o=o, k=k, dst_fn=dst_fn: dst_fn(lands[o], me ^ k)))
    return sends


def _exchange_start(name, ins, lands, sends, after=None):
    n_in, n_buf = len(ins), len(ins) + len(lands)
    n_sem = len(sends)

    def body(*refs):
        in_refs, land_refs = refs[:n_in], refs[n_in:n_buf]
        n_skip = n_buf + (0 if after is None else 1)
        send_sems, recv_sems, token = refs[n_skip], refs[n_skip + 1], refs[-1]
        x, y, c = lax.axis_index("x"), lax.axis_index("y"), lax.axis_index("c")
        me = 4 * x + 2 * y + c
        for t, (k, src_fn, dst_fn, _) in enumerate(sends):
            pltpu.make_async_remote_copy(
                src_ref=src_fn(in_refs, land_refs, me), dst_ref=dst_fn(land_refs, me),
                send_sem=send_sems.at[t], recv_sem=recv_sems.at[t],
                device_id=_peer(x, y, c, k)[0], device_id_type=MESH).start()
        token[...] = jnp.zeros_like(token)

    bufs = [pltpu.with_memory_space_constraint(a, pltpu.HBM) for a in list(ins) + list(lands)]
    extra = [] if after is None else [after]
    outs = pl.pallas_call(
        body, name=name,
        out_shape=(pltpu.SemaphoreType.DMA((n_sem,)), pltpu.SemaphoreType.DMA((n_sem,)))
        + tuple(pltpu.HBM(a.shape, a.dtype) for a in bufs) + (jax.ShapeDtypeStruct((8, BLK), F32),),
        in_specs=[_HBM_SPEC] * n_buf + [pl.BlockSpec(memory_space=pl.ANY)] * len(extra),
        out_specs=(_SEM_SPEC, _SEM_SPEC) + (_HBM_SPEC,) * n_buf + (pl.BlockSpec(memory_space=pltpu.VMEM),),
        input_output_aliases={b: 2 + b for b in range(n_buf)},
        compiler_params=pltpu.CompilerParams(has_side_effects=_DATAFLOW),
        interpret=False)(*bufs, *extra)
    return outs[0], outs[1], list(outs[2:2 + n_in]), list(outs[2 + n_in:2 + n_buf]), outs[-1]


def _exchange_wait(name, started, after, sends):
    send_sems, recv_sems, ins, lands, _ = started
    n_in, n_buf = len(ins), len(ins) + len(lands)

    def body(*refs):
        in_refs, land_refs = refs[:n_in], refs[n_in:n_buf]
        send_sems, recv_sems = refs[n_buf], refs[n_buf + 1]
        x, y, c = lax.axis_index("x"), lax.axis_index("y"), lax.axis_index("c")
        me = 4 * x + 2 * y + c
        for t, (k, src_fn, _, rcv_fn) in enumerate(sends):
            cp = pltpu.make_async_remote_copy(
                src_ref=src_fn(in_refs, land_refs, me), dst_ref=rcv_fn(land_refs, me),
                send_sem=send_sems.at[t], recv_sem=recv_sems.at[t],
                device_id=_peer(x, y, c, k)[0], device_id_type=MESH)
            cp.wait_send()
            cp.wait_recv()

    bufs = list(ins) + list(lands)
    outs = pl.pallas_call(
        body, name=name, out_shape=tuple(pltpu.HBM(a.shape, a.dtype) for a in bufs),
        in_specs=[_HBM_SPEC] * n_buf + [_SEM_SPEC, _SEM_SPEC, pl.BlockSpec(memory_space=pl.ANY)],
        out_specs=(_HBM_SPEC,) * n_buf,
        input_output_aliases={b: b for b in range(n_buf)},
        compiler_params=pltpu.CompilerParams(has_side_effects=_DATAFLOW),
        interpret=False)(*bufs, send_sems, recv_sems, after)
    return list(outs[:n_in]), list(outs[n_in:])


def _place_own(shape, dtype, own, start):
    return lax.dynamic_update_slice(lax.empty(shape, dtype), own, start)


def _place_own_window(name, shape, own, me):
    rows, cols = own.shape

    def body(me_ref, zone_in, own_ref, zone_ref):
        del me_ref, zone_in
        zone_ref[...] = own_ref[...]

    return pl.pallas_call(
        body, name=name, out_shape=jax.ShapeDtypeStruct(shape, own.dtype),
        grid_spec=pltpu.PrefetchScalarGridSpec(
            num_scalar_prefetch=1, grid=(1,),
            in_specs=[pl.BlockSpec(memory_space=pl.ANY), pl.BlockSpec((rows, cols), lambda i, me_ref: (0, 0))],
            out_specs=pl.BlockSpec((rows, cols), lambda i, me_ref: (0, me_ref[0]))),
        input_output_aliases={1: 0},
        compiler_params=pltpu.CompilerParams(dimension_semantics=("arbitrary",), vmem_limit_bytes=VMEM_LIMIT),
        interpret=False)(me.reshape(1).astype(jnp.int32), lax.empty(shape, own.dtype), own)


def _mod_fwd(c_all, w_mod, b_mod_mine):
    n_layers, _, cm = w_mod.shape

    def body(c_ref, w_ref, b_ref, o_ref):
        for l in range(n_layers):
            o_ref[l] = jnp.dot(c_ref[...], w_ref[l], preferred_element_type=F32,
                               precision=lax.Precision.HIGHEST) + b_ref[l]

    return _pcall(body, name="mod_fwd", out_shape=jax.ShapeDtypeStruct((n_layers, NDEV, cm), F32))(
        c_all, w_mod, b_mod_mine)


def _ln_proj(x, shift, scale, w_full, name):
    s_len, d = x.shape
    n = w_full.shape[1]
    tm = min(1024, s_len)
    tn = 2304

    def body(x_ref, sh_ref, sc_ref, w_ref, proj_ref, ht_ref, h_scr):
        @pl.when(pl.program_id(1) == 0)
        def _():
            xs, _ = _standardize(x_ref[...])
            h = xs * (1.0 + sc_ref[...]) + sh_ref[...]
            h_scr[...] = h.astype(BF16)
            ht_ref[...] = h.T.astype(BF16)

        proj_ref[...] = _dot(h_scr[...], w_ref[...])

    return _pcall(
        body, name=name,
        out_shape=(jax.ShapeDtypeStruct((s_len, n), F32), jax.ShapeDtypeStruct((d, s_len), BF16)),
        grid=(s_len // tm, n // tn),
        in_specs=[pl.BlockSpec((tm, d), lambda i, j: (i, 0)),
                  pl.BlockSpec((1, d), lambda i, j: (0, 0)),
                  pl.BlockSpec((1, d), lambda i, j: (0, 0)),
                  pl.BlockSpec((d, tn), lambda i, j: (0, j))],
        out_specs=(pl.BlockSpec((tm, tn), lambda i, j: (i, j)),
                   pl.BlockSpec((d, tm), lambda i, j: (0, i))),
        scratch_shapes=[pltpu.VMEM((tm, d), BF16)],
        semantics=("arbitrary", "arbitrary"))(x, shift, scale, w_full)


SB_Q_ROWS = 256
SB_K_BLOCKS = 2


def _sb_fwd(proj, name):
    s_len = proj.shape[0]
    n_pairs = WIDTH // BLK
    qr = min(SB_Q_ROWS, s_len)
    gb = SB_K_BLOCKS
    kw = gb * BLK
    nq = s_len // qr
    assert qr == kw

    def body(q_ref, k_ref, v_ref, o_ref, tot_ref):
        lane = _iota2((1, BLK), 1)
        row = _iota2((BLK, BLK), 0)
        col = _iota2((BLK, BLK), 1)
        half = jnp.concatenate([(row >= col).astype(BF16), jnp.ones((BLK, BLK), BF16)], axis=1)
        suffix_and_sum = jnp.concatenate([half, half], axis=0)
        strict = _iota2((qr, kw), 1) < _iota2((qr, kw), 0)
        head_lanes = [(lane // SB_HEAD_DIM) == hh for hh in range(2)]

        def scores(gi, qms, masked):
            c0 = pl.multiple_of(gi * kw, kw)
            kb = k_ref[pl.ds(c0, kw), :].astype(BF16)
            z2s = [_dot_nt(qms[hh], kb) for hh in range(2)]
            if masked:
                z2s = [jnp.where(strict, z2, MASKED_SCORE) for z2 in z2s]
            return tuple(z2s)

        def accumulate(gi, z2s, carry):
            c0 = pl.multiple_of(gi * kw, kw)
            vb = v_ref[pl.ds(c0, kw), :].astype(BF16)
            sp2s = [_softplus2_parts(z2)[0] for z2 in z2s]
            terms = [[_split2_lanes(sp2[:, b * BLK:(b + 1) * BLK]) for b in range(gb)] for sp2 in sp2s]
            sums = [[_dot(t, suffix_and_sum) for t in head_terms] for head_terms in terms]
            weights, laters = [], []
            for hh in range(2):
                later = carry[2 * hh + 1]
                parts = [None] * gb
                for b in reversed(range(gb)):
                    parts[b] = sums[hh][b][:, :BLK] + later
                    later = later + sums[hh][b][:, BLK:]
                weights.append(jnp.exp2(z2s[hh] - jnp.concatenate(parts, axis=1)).astype(BF16))
                laters.append(later)
            outs = [_dot(weights[hh], vb) for hh in range(2)]
            return (carry[0] + outs[0], laters[0], carry[2] + outs[1], laters[1])

        def queries(i):
            qf = q_ref[pl.ds(pl.multiple_of(i * qr, qr), qr), :] * (SB_HEAD_DIM ** -0.5 * LOG2E)
            return [jnp.where(head_lanes[hh], qf, 0.0).astype(BF16) for hh in range(2)]

        def qtile(i, first_scores):
            r0 = pl.multiple_of(i * qr, qr)
            qms = queries(i)
            zero = jnp.zeros((qr, BLK), F32)

            def step(jj, state):
                gi = i - 1 - jj
                return scores(gi, qms, False) + accumulate(gi + 1, state[:2], state[2:])

            state = lax.fori_loop(0, i, step, first_scores + (zero,) * 4)
            nxt = jnp.minimum(i + 1, nq - 1)
            next_scores = scores(nxt, queries(nxt), True)
            carry = accumulate(0, state[:2], state[2:])
            o_ref[pl.ds(r0, qr), :] = jnp.where(head_lanes[0], carry[0], carry[2])
            tot_ref[0, pl.ds(r0, qr), :] = carry[1]
            tot_ref[1, pl.ds(r0, qr), :] = carry[3]
            return next_scores

        lax.fori_loop(0, nq, qtile, scores(0, queries(0), True))

    col_spec = lambda off: pl.BlockSpec((s_len, BLK), lambda p: (0, off + p))
    return _pcall(
        body, name=name,
        out_shape=(jax.ShapeDtypeStruct((s_len, WIDTH), F32),
                   jax.ShapeDtypeStruct((2 * n_pairs, s_len, BLK), F32)),
        grid=(n_pairs,),
        in_specs=[col_spec(0), col_spec(n_pairs), col_spec(2 * n_pairs)],
        out_specs=(pl.BlockSpec((s_len, BLK), lambda p: (0, p)),
                   pl.BlockSpec((2, s_len, BLK), lambda p: (p, 0, 0))),
        semantics=("arbitrary",))(proj, proj, proj)


def _hg_masks(mask_ref):
    row = _iota2((BLK, BLK), 0)
    col = _iota2((BLK, BLK), 1)
    for v, m in enumerate(HG_LEVELS):
        same = (row // (2 * m)) == (col // (2 * m))
        mask_ref[v] = (same & ((row & m) != 0) & ((col & m) == 0)).astype(F32)


def _hg_mid(b, m):
    if m >= 4:
        n = BLK // (2 * m)
        mid = b.reshape(n, 2 * m, BLK)[:, m - 1:m, :]
        return jnp.broadcast_to(mid, (n, 2 * m, BLK)).reshape(BLK, BLK)
    pos = _iota2((BLK, BLK), 0) & (2 * m - 1)
    out = b
    for p in range(2 * m):
        delta = (m - 1) - p
        if delta != 0:
            out = jnp.where(pos == p, pltpu.roll(b, (-delta) % BLK, 0), out)
    return out


def _hg_chunk_inputs(qraw, fpre, lb):
    sig = _sigmoid(fpre)
    f = lb + (1.0 - lb) * sig
    g = jnp.log(f)
    q, dq_fac = _silu_and_grad(qraw)
    return q, dq_fac, f, sig, g


HG_GROUP = 4


def _neg_abs(x):
    return lax.bitcast_convert_type(lax.bitcast_convert_type(x, jnp.int32) | jnp.int32(-2 ** 31), F32)


def _hg_level_terms(qs, ks, bs, m):
    es = [jnp.exp(_neg_abs(b - _hg_mid(b, m))) for b in bs]
    qts = [(q * e).astype(BF16) for q, e in zip(qs, es)]
    kts = [(k * e).astype(BF16) for k, e in zip(ks, es)]
    return es, qts, kts


def _hg_load(refs, r0, lb_v, lower_incl):
    q_ref, f_ref, i_ref = refs
    heads = []
    for h in range(HG_GROUP):
        sl = slice(h * HG_HEAD_DIM, (h + 1) * HG_HEAD_DIM)
        heads.append(_hg_chunk_inputs(q_ref[pl.ds(r0, BLK), sl], f_ref[pl.ds(r0, BLK), sl], lb_v[:, sl])
                     + (i_ref[pl.ds(r0, BLK), sl],))
    bs = [_dot_01_l(lower_incl, hd[4]) for hd in heads]
    return heads, bs


def _hgrn_fwd(proj, lb, name, after=None):
    s_len = proj.shape[0]
    nc = s_len // BLK
    gw = HG_GROUP * HG_HEAD_DIM
    n_groups = WIDTH // gw
    base = 4 * WIDTH // gw

    def body(q_ref, f_ref, i_ref, lb_ref, o_ref, mask_ref):
        _hg_masks(mask_ref)
        row = _iota2((BLK, BLK), 0)
        col = _iota2((BLK, BLK), 1)
        lower_incl = (col <= row).astype(BF16)
        lb_v = lb_ref[...]

        def chunk(ci, sts):
            r0 = pl.multiple_of(ci * BLK, BLK)
            heads, bs = _hg_load((q_ref, f_ref, i_ref), r0, lb_v, lower_incl)
            qs = [hd[0] for hd in heads]
            ks = [1.0 - hd[2] for hd in heads]
            vs = [hd[5] for hd in heads]
            vbs = [v.astype(BF16) for v in vs]
            b_ends = [b[BLK - 1:BLK, :] for b in bs]
            inters = [_dot_nt((q * jnp.exp(b)).astype(BF16), st.astype(BF16)) for q, b, st in zip(qs, bs, sts)]
            scs = [None] * HG_GROUP
            for v_idx, m in enumerate(HG_LEVELS):
                _, qts, kts = _hg_level_terms(qs, ks, bs, m)
                terms = [_dot_nt(qt, kt) for qt, kt in zip(qts, kts)]
                msk = mask_ref[v_idx]
                scs = [t * msk if sc is None else sc + t * msk for sc, t in zip(scs, terms)]
            intras = [_dot(sc.astype(BF16), vb) for sc, vb in zip(scs, vbs)]
            k_decs = [(k * jnp.exp(b_end - b)).astype(BF16) for k, b, b_end in zip(ks, bs, b_ends)]
            grown = [_dot_tn(vb, k_dec) for vb, k_dec in zip(vbs, k_decs)]
            for h in range(HG_GROUP):
                diag = jnp.sum(qs[h] * ks[h], axis=-1, keepdims=True)
                o_ref[pl.ds(r0, BLK), h * HG_HEAD_DIM:(h + 1) * HG_HEAD_DIM] = inters[h] + intras[h] + diag * vs[h]
            return tuple(st * jnp.exp(b_end) + g for st, b_end, g in zip(sts, b_ends, grown))

        lax.fori_loop(0, nc, chunk, (jnp.zeros((HG_HEAD_DIM, HG_HEAD_DIM), F32),) * HG_GROUP)

    col_spec = lambda off: pl.BlockSpec((s_len, gw), lambda h: (0, off + h))
    return _pcall(
        body, name=name, out_shape=jax.ShapeDtypeStruct((s_len, WIDTH), F32),
        grid=(n_groups,),
        in_specs=[col_spec(base), col_spec(base + n_groups), col_spec(base + 2 * n_groups),
                  pl.BlockSpec((1, gw), lambda h: (0, h))],
        out_specs=pl.BlockSpec((s_len, gw), lambda h: (0, h)),
        scratch_shapes=[pltpu.VMEM((len(HG_LEVELS), BLK, BLK), F32)],
        semantics=("arbitrary",), after=after)(proj, proj, proj, lb)


def _rms_heads(o_b, norm_w):
    n_parts, h_parts, r_parts = [], [], []
    for h in range(WIDTH // HG_HEAD_DIM):
        sl = slice(h * HG_HEAD_DIM, (h + 1) * HG_HEAD_DIM)
        o = o_b[:, sl]
        rstd = lax.rsqrt(jnp.mean(o * o, axis=-1, keepdims=True) + RMS_EPS)
        ohat = o * rstd
        h_parts.append(ohat)
        n_parts.append(ohat * norm_w[:, sl])
        r_parts.append(jnp.broadcast_to(rstd, o.shape))
    cat = lambda parts: jnp.concatenate(parts, axis=-1)
    return cat(n_parts), cat(h_parts), cat(r_parts)


def _shift_rows_down(halo, cur, k):
    tm = cur.shape[0]
    ext = jnp.concatenate([halo, cur], axis=0)
    return pltpu.roll(ext, k, 0)[8:8 + tm]


def _shift_rows_up(cur, halo, k):
    tm = cur.shape[0]
    ext = jnp.concatenate([cur, halo], axis=0)
    return pltpu.roll(ext, (tm + 8 - k) % (tm + 8), 0)[0:tm]


def _merge_fwd(x, proj, o_a, o_b, gate, norm_w, conv_w, wb, w_out, ln_g, ln_b, name, target=None, after=None):
    s_len, d = x.shape
    tm = min(256, s_len)
    hb = tm // 8
    n_in = 19 + (0 if target is None else 1)

    def body(*refs):
        (x_ref, oa_ref, za_ref, ob_ref, zb_ref, pre_ref, post_ref, u_ref, zc_ref, hpre_ref, hu_ref, g_ref,
         gate_ref, nw_ref, cw_ref, wb_ref, wo_ref, lg_ref, lbias_ref) = refs[:19]
        xn_ref, mg_ref, yc_ref = refs[n_in:n_in + 3]
        i = pl.program_id(0)
        sa, _ = _silu_and_grad(za_ref[...])
        y_a = (oa_ref[...] * sa).astype(BF16)
        n_b, _, _ = _rms_heads(ob_ref[...], nw_ref[...])
        sb, _ = _silu_and_grad(zb_ref[...])
        y_b = (n_b * sb).astype(BF16)
        a = pre_ref[...] * u_ref[...]
        halo = jnp.where(i > 0, hpre_ref[...] * hu_ref[...], 0.0)
        cw = cw_ref[...]
        conv = cw[0:1] * _shift_rows_down(halo, a, 2) + cw[1:2] * _shift_rows_down(halo, a, 1) + cw[2:3] * a
        sc, _ = _silu_and_grad(zc_ref[...])
        y_c = (post_ref[...] * conv * sc).astype(BF16)
        merged = None
        for k, yk in enumerate((y_a, y_b, y_c)):
            yc_ref[:, k * WIDTH:(k + 1) * WIDTH] = yk
            term = _sigmoid(g_ref[:, k * d:(k + 1) * d]) * _dot(yk, wb_ref[k])
            merged = term if merged is None else merged + term
        mb = merged.astype(BF16)
        mg_ref[...] = mb
        y = _dot(mb, wo_ref[...])
        r = ALPHA * x_ref[...] + (1.0 + gate_ref[...]) * y
        rhat, _ = _standardize(r)
        xn = rhat * lg_ref[...] + lbias_ref[...]
        if target is None:
            xn_ref[...] = xn
        else:
            t_ref, loss_ref = refs[19], refs[n_in + 3]

            @pl.when(i == 0)
            def _():
                loss_ref[...] = jnp.zeros_like(loss_ref)

            e = xn - t_ref[...]
            xn_ref[...] = e * (1.0 / d)
            part = jnp.sum(jnp.sum(e * e, axis=-1, keepdims=True), axis=0, keepdims=True)
            loss_ref[...] += part * (0.5 / d)

    wcol = lambda cb: pl.BlockSpec((tm, WIDTH), lambda i: (i, cb))
    halo_spec = lambda cb: pl.BlockSpec((8, WIDTH), lambda i: (jnp.maximum(i * hb - 1, 0), cb))
    vec = lambda w: pl.BlockSpec((1, w), lambda i: (0, 0))
    tile = pl.BlockSpec((tm, d), lambda i: (i, 0))
    with_loss = target is not None
    return _pcall(
        body, name=name,
        out_shape=(jax.ShapeDtypeStruct((s_len, d), F32), jax.ShapeDtypeStruct((s_len, d), BF16),
                   jax.ShapeDtypeStruct((s_len, 3 * WIDTH), BF16))
        + ((jax.ShapeDtypeStruct((1, 1), F32),) if with_loss else ()),
        grid=(s_len // tm,),
        in_specs=[tile,
                  wcol(0), wcol(3), wcol(0), wcol(7), wcol(8), wcol(9), wcol(10), wcol(11),
                  halo_spec(8), halo_spec(10),
                  pl.BlockSpec((tm, 3 * d), lambda i: (i, 2)),
                  vec(d), vec(WIDTH),
                  pl.BlockSpec((3, WIDTH), lambda i: (0, 0)),
                  pl.BlockSpec((3, WIDTH, d), lambda i: (0, 0, 0)),
                  pl.BlockSpec((d, d), lambda i: (0, 0)),
                  vec(d), vec(d)] + ([tile] if with_loss else []),
        out_specs=(tile, tile, pl.BlockSpec((tm, 3 * WIDTH), lambda i: (i, 0)))
        + ((pl.BlockSpec((1, 1), lambda i: (0, 0)),) if with_loss else ()),
        semantics=("arbitrary",), after=after)(x, o_a, proj, o_b, proj, proj, proj, proj, proj, proj, proj, proj,
                                  gate, norm_w, conv_w, wb, w_out, ln_g, ln_b, *([target] if with_loss else []))


def _merge_bwd(dxn, x, merged, ycat, proj, gate, wb, w_out, ln_g, name, after=None):
    s_len, d = x.shape
    tm = min(256, s_len)
    dsh = d // NDEV
    n_tiles = s_len // tm

    def body(dxn_ref, x_ref, mg_ref, yc_ref, g_ref, gate_ref, wb_ref, wo_ref, lg_ref,
             dres_ref, dyc_ref, dg_ref, gwo_out, gwb_out, vec_ref, gwo_ref, gwb_ref):
        @pl.when(pl.program_id(0) == 0)
        def _():
            gwo_ref[...] = jnp.zeros_like(gwo_ref)
            gwb_ref[...] = jnp.zeros_like(gwb_ref)
            vec_ref[...] = jnp.zeros_like(vec_ref)

        mb = mg_ref[...]
        one_gate = 1.0 + gate_ref[...]
        y = _dot(mb, wo_ref[...])
        r = ALPHA * x_ref[...] + one_gate * y
        rhat, rstd = _standardize(r)
        dxn = dxn_ref[...]
        dr = _standardize_bwd(rhat, rstd, dxn * lg_ref[...])
        vec_ref[0:1, :] += jnp.sum(dxn * rhat, axis=0, keepdims=True)
        vec_ref[1:2, :] += jnp.sum(dxn, axis=0, keepdims=True)
        vec_ref[2:3, :] += jnp.sum(dr * y, axis=0, keepdims=True)
        dres_ref[...] = ALPHA * dr
        dy = (one_gate * dr).astype(BF16)
        gwo_ref[...] += _dot_tn(mb, dy)
        dmerged = _dot_nt(dy, wo_ref[...])
        for k in range(3):
            yk = yc_ref[:, k * WIDTH:(k + 1) * WIDTH]
            sg = _sigmoid(g_ref[:, k * d:(k + 1) * d])
            pk = _dot(yk, wb_ref[k])
            dg_ref[:, k * d:(k + 1) * d] = (dmerged * pk * sg * (1.0 - sg)).astype(BF16)
            dpk = (dmerged * sg).astype(BF16)
            dyc_ref[:, k * WIDTH:(k + 1) * WIDTH] = _dot_nt(dpk, wb_ref[k])
            gwb_ref[k] += _dot_tn(yk, dpk)

        @pl.when(pl.program_id(0) == n_tiles - 1)
        def _():
            for o in range(NDEV):
                gwo_out[o] = gwo_ref[o * dsh:(o + 1) * dsh, :].astype(BF16)
                for k in range(3):
                    gwb_out[o, k] = gwb_ref[k, :, o * dsh:(o + 1) * dsh].astype(BF16)

    tile = lambda w: pl.BlockSpec((tm, w), lambda i: (i, 0))
    vec = pl.BlockSpec((1, d), lambda i: (0, 0))
    return _pcall(
        body, name=name,
        out_shape=(jax.ShapeDtypeStruct((s_len, d), F32), jax.ShapeDtypeStruct((s_len, 3 * WIDTH), F32),
                   jax.ShapeDtypeStruct(proj.shape, BF16), jax.ShapeDtypeStruct((NDEV, dsh, d), BF16),
                   jax.ShapeDtypeStruct((NDEV, 3, WIDTH, dsh), BF16), jax.ShapeDtypeStruct((8, d), F32)),
        grid=(n_tiles,),
        in_specs=[tile(d), tile(d), tile(d), tile(3 * WIDTH),
                  pl.BlockSpec((tm, 3 * d), lambda i: (i, 2)),
                  vec, pl.BlockSpec((3, WIDTH, d), lambda i: (0, 0, 0)),
                  pl.BlockSpec((d, d), lambda i: (0, 0)), vec],
        out_specs=(tile(d), tile(3 * WIDTH), pl.BlockSpec((tm, 3 * d), lambda i: (i, 2)),
                   pl.BlockSpec((NDEV, dsh, d), lambda i: (0, 0, 0)),
                   pl.BlockSpec((NDEV, 3, WIDTH, dsh), lambda i: (0, 0, 0, 0)),
                   pl.BlockSpec((8, d), lambda i: (0, 0))),
        scratch_shapes=[pltpu.VMEM((d, d), F32), pltpu.VMEM((3, WIDTH, d), F32)],
        semantics=("arbitrary",), after=after)(dxn, x, merged, ycat, proj, gate, wb, w_out, ln_g)


def _branch_bwd(dycat, proj, o_a, o_b, norm_w, conv_w, dproj, name, after=None):
    s_len = proj.shape[0]
    tm = min(512, s_len)
    hb = tm // 8
    n_tiles = s_len // tm

    def body(dya_ref, dyb_ref, dyc_ref, oa_ref, za_ref, ob_ref, zb_ref, pre_ref, post_ref, u_ref, zc_ref,
             hpre_ref, hu_ref, ndyc_ref, npost_ref, nzc_ref, nw_ref, cw_ref, dproj_in,
             dproj_ref, doa_ref, dob_ref, vec_ref, dza_scr, dzb_scr, dc_scr, sems):
        del dproj_in
        i = pl.program_id(0)

        @pl.when(i == 0)
        def _():
            vec_ref[...] = jnp.zeros_like(vec_ref)

        sa, dsa = _silu_and_grad(za_ref[...])
        dya = dya_ref[...]
        doa_ref[...] = dya * sa
        dza_scr[...] = (dya * oa_ref[...] * dsa).astype(BF16)
        nw = nw_ref[...]
        n_b, ohat, rstd = _rms_heads(ob_ref[...], nw)
        sb, dsb = _silu_and_grad(zb_ref[...])
        dyb = dyb_ref[...]
        dzb_scr[...] = (dyb * n_b * dsb).astype(BF16)
        dn = dyb * sb
        vec_ref[0:1, :] += jnp.sum(dn * ohat, axis=0, keepdims=True)
        dnw = dn * nw
        parts = []
        for h in range(WIDTH // HG_HEAD_DIM):
            sl = slice(h * HG_HEAD_DIM, (h + 1) * HG_HEAD_DIM)
            m2 = jnp.mean(dnw[:, sl] * ohat[:, sl], axis=-1, keepdims=True)
            parts.append(rstd[:, sl] * (dnw[:, sl] - ohat[:, sl] * m2))
        dob_ref[...] = jnp.concatenate(parts, axis=-1)
        cw = cw_ref[...]
        pre, u, post = pre_ref[...], u_ref[...], post_ref[...]
        a = pre * u
        halo = jnp.where(i > 0, hpre_ref[...] * hu_ref[...], 0.0)
        a1 = _shift_rows_down(halo, a, 1)
        a2 = _shift_rows_down(halo, a, 2)
        conv = cw[0:1] * a2 + cw[1:2] * a1 + cw[2:3] * a
        sc, dsc = _silu_and_grad(zc_ref[...])
        dyc = dyc_ref[...]
        dconv = dyc * post * sc
        nsc, _ = _silu_and_grad(nzc_ref[...])
        nxt = jnp.where(i < n_tiles - 1, ndyc_ref[...] * npost_ref[...] * nsc, 0.0)
        da = cw[2:3] * dconv + cw[1:2] * _shift_rows_up(dconv, nxt, 1) + cw[0:1] * _shift_rows_up(dconv, nxt, 2)
        dc_scr[:, 0 * WIDTH:1 * WIDTH] = (da * u).astype(BF16)
        dc_scr[:, 1 * WIDTH:2 * WIDTH] = (dyc * conv * sc).astype(BF16)
        dc_scr[:, 2 * WIDTH:3 * WIDTH] = (da * pre).astype(BF16)
        dc_scr[:, 3 * WIDTH:4 * WIDTH] = (dyc * post * conv * dsc).astype(BF16)
        vec_ref[1:2, :] += jnp.sum(dconv * a2, axis=0, keepdims=True)
        vec_ref[2:3, :] += jnp.sum(dconv * a1, axis=0, keepdims=True)
        vec_ref[3:4, :] += jnp.sum(dconv * a, axis=0, keepdims=True)
        rows = pl.ds(pl.multiple_of(i * tm, tm), tm)
        copies = [pltpu.make_async_copy(dza_scr, dproj_ref.at[rows, 3 * WIDTH:4 * WIDTH], sems.at[0]),
                  pltpu.make_async_copy(dzb_scr, dproj_ref.at[rows, 7 * WIDTH:8 * WIDTH], sems.at[1]),
                  pltpu.make_async_copy(dc_scr, dproj_ref.at[rows, 8 * WIDTH:12 * WIDTH], sems.at[2])]
        for cp in copies:
            cp.start()
        for cp in copies:
            cp.wait()

    wcol = lambda cb: pl.BlockSpec((tm, WIDTH), lambda i: (i, cb))
    prev = lambda cb: pl.BlockSpec((8, WIDTH), lambda i: (jnp.maximum(i * hb - 1, 0), cb))
    nxt = lambda cb: pl.BlockSpec((8, WIDTH), lambda i: (jnp.minimum((i + 1) * hb, s_len // 8 - 1), cb))
    anyspec = pl.BlockSpec(memory_space=pl.ANY)
    out = jax.ShapeDtypeStruct((s_len, WIDTH), F32)
    return _pcall(
        body, name=name,
        out_shape=(jax.ShapeDtypeStruct(dproj.shape, dproj.dtype), out, out, jax.ShapeDtypeStruct((8, WIDTH), F32)),
        grid=(n_tiles,),
        in_specs=[wcol(0), wcol(1), wcol(2), wcol(0), wcol(3), wcol(0), wcol(7), wcol(8), wcol(9), wcol(10), wcol(11),
                  prev(8), prev(10), nxt(2), nxt(9), nxt(11),
                  pl.BlockSpec((1, WIDTH), lambda i: (0, 0)), pl.BlockSpec((3, WIDTH), lambda i: (0, 0)), anyspec],
        out_specs=(anyspec, wcol(0), wcol(0), pl.BlockSpec((8, WIDTH), lambda i: (0, 0))),
        scratch_shapes=[pltpu.VMEM((tm, WIDTH), BF16), pltpu.VMEM((tm, WIDTH), BF16),
                        pltpu.VMEM((tm, 4 * WIDTH), BF16), pltpu.SemaphoreType.DMA((3,))],
        aliases={18: 0},
        semantics=("arbitrary",), after=after)(dycat, dycat, dycat, o_a, proj, o_b, proj, proj, proj, proj, proj,
                                  proj, proj, dycat, proj, proj, norm_w, conv_w, dproj)


def _sb_bwd(proj, do_a, totals, dproj, name):
    s_len = proj.shape[0]
    n_pairs = WIDTH // BLK
    scale = SB_HEAD_DIM ** -0.5
    qr = min(SB_Q_ROWS, s_len)
    gb = SB_K_BLOCKS
    kw = gb * BLK
    nq = s_len // qr
    assert qr == kw

    def body(q_ref, k_ref, v_ref, do_ref, tot_ref, dproj_in, dproj_ref, dq_ref, dk_ref, dv_ref, out_scr, sems):
        del dproj_in
        lane = _iota2((1, BLK), 1)
        row = _iota2((BLK, BLK), 0)
        col = _iota2((BLK, BLK), 1)
        ones = jnp.ones((BLK, BLK), BF16)
        twice = lambda m: jnp.concatenate([m, m], axis=0)
        before_and_sum = twice(jnp.concatenate([(row < col).astype(BF16), ones], axis=1))
        upto_and_sum = twice(jnp.concatenate([(row <= col).astype(BF16), ones], axis=1))
        strict = _iota2((qr, kw), 1) < _iota2((qr, kw), 0)
        head_lanes = [(lane // SB_HEAD_DIM) == hh for hh in range(2)]
        dk_ref[...] = jnp.zeros_like(dk_ref)
        dv_ref[...] = jnp.zeros_like(dv_ref)

        def scores(gi, qms, masked):
            c0 = pl.multiple_of(gi * kw, kw)
            kb = k_ref[pl.ds(c0, kw), :].astype(BF16)
            z2s = [_dot_nt(qms[hh], kb) for hh in range(2)]
            if masked:
                z2s = [jnp.where(strict, z2, MASKED_SCORE) for z2 in z2s]
            return tuple(z2s)

        def process(gi, z2s, qms, doms, totals_i, carry):
            c0 = pl.multiple_of(gi * kw, kw)
            kb = k_ref[pl.ds(c0, kw), :].astype(BF16)
            vb = v_ref[pl.ds(c0, kw), :].astype(BF16)
            das = [_dot_nt(doms[hh], vb) for hh in range(2)]
            halves = [_softplus2_parts(z2) for z2 in z2s]
            terms = [[_split2_lanes(sp2[:, b * BLK:(b + 1) * BLK]) for b in range(gb)] for sp2, _ in halves]
            sums = [[_dot(t, before_and_sum) for t in head_terms] for head_terms in terms]
            weights, gmats, l_befores = [], [], []
            for hh in range(2):
                l_before = carry[3 * hh + 1]
                parts = []
                for b in range(gb):
                    parts.append(totals_i[hh] - l_before - sums[hh][b][:, :BLK])
                    l_before = l_before + sums[hh][b][:, BLK:]
                a = jnp.exp2(z2s[hh] - jnp.concatenate(parts, axis=1))
                weights.append(a.astype(BF16))
                gmats.append(a * das[hh])
                l_befores.append(l_before)
            terms = [[_split2_lanes(g[:, b * BLK:(b + 1) * BLK]) for b in range(gb)] for g in gmats]
            sums = [[_dot(t, upto_and_sum) for t in head_terms] for head_terms in terms]
            dzs, g_befores = [], []
            for hh in range(2):
                g_before = carry[3 * hh + 2]
                parts = []
                for b in range(gb):
                    parts.append(g_before + sums[hh][b][:, :BLK])
                    g_before = g_before + sums[hh][b][:, BLK:]
                dzs.append((gmats[hh] - halves[hh][1] * jnp.concatenate(parts, axis=1)).astype(BF16))
                g_befores.append(g_before)
            dk_t = _dot_tn(jnp.concatenate(qms, axis=0), jnp.concatenate(dzs, axis=0))
            dv_t = _dot_tn(jnp.concatenate(doms, axis=0), jnp.concatenate(weights, axis=0))
            dqs = [_dot(dzs[hh], kb) for hh in range(2)]
            dk_ref[:, pl.ds(c0, kw)] += dk_t * (1.0 / LOG2E)
            dv_ref[:, pl.ds(c0, kw)] += dv_t
            return (carry[0] + dqs[0], l_befores[0], g_befores[0], carry[3] + dqs[1], l_befores[1], g_befores[1])

        def queries(i):
            qf = q_ref[pl.ds(pl.multiple_of(i * qr, qr), qr), :] * (scale * LOG2E)
            return [jnp.where(head_lanes[hh], qf, 0.0).astype(BF16) for hh in range(2)]

        def qtile(i, first_scores):
            r0 = pl.multiple_of(i * qr, qr)
            qms = queries(i)
            dof = do_ref[pl.ds(r0, qr), :]
            doms = [jnp.where(head_lanes[hh], dof, 0.0).astype(BF16) for hh in range(2)]
            totals_i = [tot_ref[hh, pl.ds(r0, qr), :] for hh in range(2)]
            zero = jnp.zeros((qr, BLK), F32)

            def step(gi, state):
                return scores(gi + 1, qms, False) + process(gi, state[:2], qms, doms, totals_i, state[2:])

            def before_diagonal(state):
                return scores(i, qms, True) + process(i - 1, state[:2], qms, doms, totals_i, state[2:])

            state = lax.fori_loop(0, i - 1, step, first_scores + (zero,) * 6)
            state = lax.cond(i > 0, before_diagonal, lambda st: st, state)
            nxt = jnp.minimum(i + 1, nq - 1)
            next_scores = scores(0, queries(nxt), False)
            carry = process(i, state[:2], qms, doms, totals_i, state[2:])
            dq_ref[pl.ds(r0, qr), :] = jnp.where(head_lanes[0], carry[0], carry[3]) * scale
            return next_scores

        lax.fori_loop(0, nq, qtile, scores(0, queries(0), True))
        pair = pl.program_id(0)
        copies = []
        for t, value in enumerate((dq_ref[...], dk_ref[...].T, dv_ref[...].T)):
            out_scr[t] = value.astype(BF16)
            col = pl.multiple_of((t * n_pairs + pair) * BLK, BLK)
            copies.append(pltpu.make_async_copy(out_scr.at[t], dproj_ref.at[:, pl.ds(col, BLK)], sems.at[t]))
            copies[-1].start()
        for cp in copies:
            cp.wait()

    col_spec = lambda off: pl.BlockSpec((s_len, BLK), lambda p: (0, off + p))
    anyspec = pl.BlockSpec(memory_space=pl.ANY)
    return _pcall(
        body, name=name, out_shape=jax.ShapeDtypeStruct(dproj.shape, dproj.dtype), grid=(n_pairs,),
        in_specs=[col_spec(0), col_spec(n_pairs), col_spec(2 * n_pairs), col_spec(0),
                  pl.BlockSpec((2, s_len, BLK), lambda p: (p, 0, 0)), anyspec],
        out_specs=anyspec,
        scratch_shapes=[pltpu.VMEM((s_len, BLK), F32), pltpu.VMEM((BLK, s_len), F32), pltpu.VMEM((BLK, s_len), F32),
                        pltpu.VMEM((3, s_len, BLK), BF16), pltpu.SemaphoreType.DMA((3,))],
        aliases={5: 0},
        semantics=("arbitrary",))(proj, proj, proj, do_a, totals, dproj)


def _hgrn_bwd(proj, do_b, lb, dproj, name):
    s_len = proj.shape[0]
    nc = s_len // BLK
    gw = HG_GROUP * HG_HEAD_DIM
    n_groups = WIDTH // gw
    base = 4 * WIDTH // gw
    heads_of = range(HG_GROUP)

    def body(q_ref, f_ref, i_ref, do_ref, lb_ref, dproj_in, dproj_ref, dlb_ref, mask_ref, st_ref, out_scr, sems):
        del dproj_in
        _hg_masks(mask_ref)
        row = _iota2((BLK, BLK), 0)
        col = _iota2((BLK, BLK), 1)
        lower_incl = (col <= row).astype(BF16)
        upper_incl = (col >= row).astype(BF16)
        lb_v = lb_ref[...]
        refs = (q_ref, f_ref, i_ref)

        def fwd_chunk(ci, sts):
            for h in heads_of:
                st_ref[ci, h] = sts[h]
            heads, bs = _hg_load(refs, pl.multiple_of(ci * BLK, BLK), lb_v, lower_incl)
            b_ends = [b[BLK - 1:BLK, :] for b in bs]
            k_decs = [((1.0 - hd[2]) * jnp.exp(b_end - b)).astype(BF16) for hd, b, b_end in zip(heads, bs, b_ends)]
            grown = [_dot_tn(hd[5].astype(BF16), k_dec) for hd, k_dec in zip(heads, k_decs)]
            return tuple(st * jnp.exp(b_end) + g for st, b_end, g in zip(sts, b_ends, grown))

        zero_state = (jnp.zeros((HG_HEAD_DIM, HG_HEAD_DIM), F32),) * HG_GROUP
        lax.fori_loop(0, nc, fwd_chunk, zero_state)

        def bwd_chunk(cc, carry):
            dsts, suffixes, dlbs = carry
            ci = nc - 1 - cc
            r0 = pl.multiple_of(ci * BLK, BLK)
            heads, bs = _hg_load(refs, r0, lb_v, lower_incl)
            qs = [hd[0] for hd in heads]
            fs = [hd[2] for hd in heads]
            ks = [1.0 - f for f in fs]
            vs = [hd[5] for hd in heads]
            vbs = [v.astype(BF16) for v in vs]
            dos = [do_ref[pl.ds(r0, BLK), h * HG_HEAD_DIM:(h + 1) * HG_HEAD_DIM] for h in heads_of]
            dobs = [do.astype(BF16) for do in dos]
            b_ends = [b[BLK - 1:BLK, :] for b in bs]
            e_qs = [jnp.exp(b) for b in bs]
            e_ks = [jnp.exp(b_end - b) for b, b_end in zip(bs, b_ends)]
            qes = [(q * e).astype(BF16) for q, e in zip(qs, e_qs)]
            khs = [(k * e).astype(BF16) for k, e in zip(ks, e_ks)]
            st_terms = [_split2_lanes(st_ref[ci, h]) for h in heads_of]
            ds_terms = [_split2_lanes(dst) for dst in dsts]
            dqes = [_dot(dob, t[:, :HG_HEAD_DIM]) + _dot(dob, t[:, HG_HEAD_DIM:]) for dob, t in zip(dobs, st_terms)]
            dkhs = [_dot(vb, t[:, :HG_HEAD_DIM]) + _dot(vb, t[:, HG_HEAD_DIM:]) for vb, t in zip(vbs, ds_terms)]
            dvs = [_dot_nt(kh, t[:, :HG_HEAD_DIM]) for kh, t in zip(khs, ds_terms)]
            grown = [_dot_tn(dob, qe) for dob, qe in zip(dobs, qes)]
            das = [_dot_nt(dob, vb) for dob, vb in zip(dobs, vbs)]
            dqs = [e * dqe for e, dqe in zip(e_qs, dqes)]
            dks = [e * dkh for e, dkh in zip(e_ks, dkhs)]
            dlogs = [qe.astype(F32) * dqe - kh.astype(F32) * dkh for qe, dqe, kh, dkh in zip(qes, dqes, khs, dkhs)]
            scs = [None] * HG_GROUP
            for v_idx, m in enumerate(HG_LEVELS):
                es, qms, kms = _hg_level_terms(qs, ks, bs, m)
                msk = mask_ref[v_idx]
                terms = [_dot_nt(qm, km) for qm, km in zip(qms, kms)]
                pms = [(da * msk).astype(BF16) for da in das]
                dqms = [_dot(pm, km) for pm, km in zip(pms, kms)]
                dkms = [_dot_tn(pm, qm) for pm, qm in zip(pms, qms)]
                scs = [t * msk if sc is None else sc + t * msk for sc, t in zip(scs, terms)]
                dqs = [dq + dqm * e for dq, dqm, e in zip(dqs, dqms, es)]
                dks = [dk + dkm * e for dk, dkm, e in zip(dks, dkms, es)]
                dlogs = [dl + (qm.astype(F32) * dqm - km.astype(F32) * dkm)
                         for dl, qm, dqm, km, dkm in zip(dlogs, qms, dqms, kms, dkms)]
            intras = [_dot_tn(sc.astype(BF16), dob) for sc, dob in zip(scs, dobs)]
            dgs = [_dot_01_l(upper_incl, dl) + sfx for dl, sfx in zip(dlogs, suffixes)]
            new_dlbs = []
            for h in heads_of:
                q, dq_fac, f, sig = heads[h][0], heads[h][1], heads[h][2], heads[h][3]
                a_diag = jnp.sum(dos[h] * vs[h], axis=-1, keepdims=True)
                s_diag = jnp.sum(q * ks[h], axis=-1, keepdims=True)
                dq = dqs[h] + a_diag * ks[h]
                dk = dks[h] + a_diag * q
                dv = dvs[h] + intras[h] + s_diag * dos[h]
                dfull = dgs[h] / f - dk
                sl = slice(h * HG_HEAD_DIM, (h + 1) * HG_HEAD_DIM)
                out_scr[0, pl.ds(r0, BLK), sl] = (dq * dq_fac).astype(BF16)
                out_scr[1, pl.ds(r0, BLK), sl] = (dfull * (1.0 - lb_v[:, sl]) * sig * (1.0 - sig)).astype(BF16)
                out_scr[2, pl.ds(r0, BLK), sl] = dv.astype(BF16)
                new_dlbs.append(dlbs[h] + jnp.sum(dfull * (1.0 - sig), axis=0, keepdims=True))
            new_dsts = tuple(dst * jnp.exp(b_end) + g for dst, b_end, g in zip(dsts, b_ends, grown))
            return new_dsts, tuple(dg[0:1, :] for dg in dgs), tuple(new_dlbs)

        zero_row = (jnp.zeros((1, HG_HEAD_DIM), F32),) * HG_GROUP
        _, _, dlbs = lax.fori_loop(0, nc, bwd_chunk, (zero_state, zero_row, zero_row))
        dlb_ref[...] = jnp.broadcast_to(jnp.concatenate(dlbs, axis=1), dlb_ref.shape)
        group = pl.program_id(0)
        copies = []
        for t in range(3):
            col = pl.multiple_of((base + t * n_groups + group) * gw, gw)
            copies.append(pltpu.make_async_copy(out_scr.at[t], dproj_ref.at[:, pl.ds(col, gw)], sems.at[t]))
            copies[-1].start()
        for cp in copies:
            cp.wait()

    col_spec = lambda off: pl.BlockSpec((s_len, gw), lambda h: (0, off + h))
    anyspec = pl.BlockSpec(memory_space=pl.ANY)
    return _pcall(
        body, name=name,
        out_shape=(jax.ShapeDtypeStruct(dproj.shape, dproj.dtype), jax.ShapeDtypeStruct((8, WIDTH), F32)),
        grid=(n_groups,),
        in_specs=[col_spec(base), col_spec(base + n_groups), col_spec(base + 2 * n_groups), col_spec(0),
                  pl.BlockSpec((1, gw), lambda h: (0, h)), anyspec],
        out_specs=(anyspec, pl.BlockSpec((8, gw), lambda h: (0, h))),
        scratch_shapes=[pltpu.VMEM((len(HG_LEVELS), BLK, BLK), F32),
                        pltpu.VMEM((nc, HG_GROUP, HG_HEAD_DIM, HG_HEAD_DIM), F32),
                        pltpu.VMEM((3, s_len, gw), BF16), pltpu.SemaphoreType.DMA((3,))],
        aliases={5: 0},
        semantics=("arbitrary",))(proj, proj, proj, do_b, lb, dproj)


def _dh_matmul(dproj, w_full, after, name):
    s_len, n = dproj.shape
    d = w_full.shape[0]
    tm = min(1024, s_len)
    tk = 4608

    def body(dp_ref, w_ref, after_ref, dh_ref):
        del after_ref
        part = _dot_nt(dp_ref[...], w_ref[...])

        @pl.when(pl.program_id(1) == 0)
        def _():
            dh_ref[...] = part

        @pl.when(pl.program_id(1) > 0)
        def _():
            dh_ref[...] += part

    return _pcall(
        body, name=name, out_shape=jax.ShapeDtypeStruct((s_len, d), F32),
        grid=(s_len // tm, n // tk),
        in_specs=[pl.BlockSpec((tm, tk), lambda i, k: (i, k)), pl.BlockSpec((d, tk), lambda i, k: (0, k)),
                  pl.BlockSpec(memory_space=pl.ANY)],
        out_specs=pl.BlockSpec((tm, d), lambda i, k: (i, 0)),
        semantics=("arbitrary", "arbitrary"))(dproj, w_full, after)


def _gw_matmul(h_t, dproj, name):
    d, s_len = h_t.shape
    n = dproj.shape[1]
    tn = 2304

    def body(ht_ref, dp_ref, gw_ref):
        gw_ref[...] = _dot(ht_ref[...], dp_ref[...]).astype(BF16)

    return _pcall(
        body, name=name, out_shape=jax.ShapeDtypeStruct((d, n), BF16),
        grid=(n // tn,),
        in_specs=[pl.BlockSpec((d, s_len), lambda j: (0, 0)), pl.BlockSpec((s_len, tn), lambda j: (0, j))],
        out_specs=pl.BlockSpec((d, tn), lambda j: (0, j)),
        semantics=("arbitrary",))(h_t, dproj)


def _ln_bwd(dh, x, scale, dres, name, after=None):
    s_len, d = x.shape
    tm = min(1024, s_len)

    def body(dh_ref, x_ref, sc_ref, dres_ref, dx_ref, vec_ref):
        @pl.when(pl.program_id(0) == 0)
        def _():
            vec_ref[...] = jnp.zeros_like(vec_ref)

        dh = dh_ref[...]
        xs, rstd = _standardize(x_ref[...])
        vec_ref[0:1, :] += jnp.sum(dh, axis=0, keepdims=True)
        vec_ref[1:2, :] += jnp.sum(dh * xs, axis=0, keepdims=True)
        dx_ref[...] = _standardize_bwd(xs, rstd, dh * (1.0 + sc_ref[...])) + dres_ref[...]

    tile = pl.BlockSpec((tm, d), lambda i: (i, 0))
    return _pcall(body, name=name, grid=(s_len // tm,),
                  out_shape=(jax.ShapeDtypeStruct((s_len, d), F32), jax.ShapeDtypeStruct((8, d), F32)),
                  in_specs=[tile, tile, pl.BlockSpec((1, d), lambda i: (0, 0)), tile],
                  out_specs=(tile, pl.BlockSpec((8, d), lambda i: (0, 0))),
                  semantics=("arbitrary",), after=after)(dh, x, scale, dres)


def _wmod_grad(c_t, dmod):
    d = c_t.shape[0]
    n_layers, _, cm = dmod.shape

    def body(c_ref, dm_ref, o_ref):
        for l in range(n_layers):
            acc = None
            for b in range(NDEV):
                term = c_ref[:, b:b + 1] * dm_ref[l, b:b + 1, :]
                acc = term if acc is None else acc + term
            o_ref[l] = acc

    return _pcall(body, name="wmod_grad", out_shape=jax.ShapeDtypeStruct((n_layers, d, cm), F32))(c_t, dmod)


def _sum_adamw(parts, w, m, v, name, first_row=0, into=None, after=None):
    n_src, range_rows, cols = parts.shape
    rows = w.shape[0]
    tr = range_rows
    for cand in (512, 256, 128, 64, 32, 16, 8):
        if range_rows % cand == 0 and cand * cols * 4 <= (2 << 20):
            tr = cand
            break
    first_tile = first_row // tr
    assert first_row % tr == 0
    n_extra = (0 if into is None else 4) + (0 if after is None else 1)

    def body(p_ref, w_ref, m_ref, v_ref, *rest):
        g_ref, d_ref, nm_ref, nv_ref = rest[n_extra:]
        g = p_ref[0].astype(F32)
        for s in range(1, n_src):
            g = g + p_ref[s].astype(F32)
        g_ref[...] = g
        d_ref[...], nm_ref[...], nv_ref[...] = _adamw_step(g, w_ref[...], m_ref[...], v_ref[...])

    tile = pl.BlockSpec((tr, cols), lambda i: (i + first_tile, 0))
    anyspec = pl.BlockSpec(memory_space=pl.ANY)
    out = jax.ShapeDtypeStruct((rows, cols), F32)
    extra = ([] if into is None else list(into)) + ([] if after is None else [after])
    aliases = {} if into is None else {4 + k: k for k in range(4)}
    return _pcall(body, name=name, grid=(range_rows // tr,), out_shape=(out,) * 4,
                  in_specs=[pl.BlockSpec((n_src, tr, cols), lambda i: (0, i, 0)), tile, tile, tile]
                  + [anyspec] * len(extra),
                  out_specs=(tile,) * 4, aliases=aliases, semantics=("arbitrary",))(parts, w, m, v, *extra)


def _adamw_step(g, w, m, v):
    nm = ADAM_B1 * m + (1.0 - ADAM_B1) * g
    nv = ADAM_B2 * v + (1.0 - ADAM_B2) * (g * g)
    m_hat = nm / (1.0 - ADAM_B1 ** ADAM_STEP)
    v_hat = nv / (1.0 - ADAM_B2 ** ADAM_STEP)
    return -ADAM_LR * (m_hat / (jnp.sqrt(v_hat) + ADAM_EPS) + ADAM_WD * w), nm, nv


def _adamw_small(gs, ws, ms, vs):
    n = len(gs)

    def body(*refs):
        for p in range(n):
            results = _adamw_step(*(refs[k * n + p][...] for k in range(4)))
            for k in range(3):
                refs[(4 + k) * n + p][...] = results[k]

    shapes = [jax.ShapeDtypeStruct(w.shape, F32) for w in ws]
    outs = _pcall(body, name="adamw_small", out_shape=shapes * 3)(*gs, *ws, *ms, *vs)
    return [(outs[p], outs[n + p], outs[2 * n + p]) for p in range(n)]


def _sum_parts(parts, name):
    n_src = parts.shape[0]

    def body(p_ref, o_ref):
        acc = p_ref[0]
        for s in range(1, n_src):
            acc = acc + p_ref[s]
        o_ref[...] = acc

    return _pcall(body, name=name, out_shape=jax.ShapeDtypeStruct(parts.shape[1:], F32))(parts)


def _pair_sum(gw, stage, me, name):
    d = gw.shape[0]
    n_slots, _, shard = stage.shape

    def body(me_ref, g_ref, s_ref, own_ref, o_ref):
        del me_ref
        total = (g_ref[...].astype(F32) + s_ref[0].astype(F32)).astype(BF16)
        o_ref[0] = total

        @pl.when(pl.program_id(0) == 0)
        def _():
            own_ref[0] = total

    slot = pl.BlockSpec((1, d, shard), lambda jj, me_ref: (jj, 0, 0))
    out = jax.ShapeDtypeStruct(stage.shape, BF16)
    return pl.pallas_call(
        body, name=name, out_shape=(out, out),
        grid_spec=pltpu.PrefetchScalarGridSpec(
            num_scalar_prefetch=1, grid=(n_slots,),
            in_specs=[pl.BlockSpec((d, shard), lambda jj, me_ref: (0, me_ref[0] ^ (2 * jj))), slot],
            out_specs=(pl.BlockSpec((1, d, shard), lambda jj, me_ref: (0, 0, 0)), slot)),
        compiler_params=pltpu.CompilerParams(dimension_semantics=("arbitrary",), vmem_limit_bytes=VMEM_LIMIT),
        interpret=False)(me.reshape(1).astype(jnp.int32), gw, stage)


def _lower_bound_table(lower_bounds):
    p = jax.nn.softmax(lower_bounds.astype(F32), axis=0)
    return jnp.cumsum(p, axis=0) - p[0:1]


def _pad_rows(v, width):
    n = v.shape[0]
    rows = -(-n // width)
    rows = -(-rows // 8) * 8
    return jnp.pad(v, (0, rows * width - n)).reshape(rows, width)


def kernel(x, c, w_mod, b_mod, w_in, conv_w, hgrn_norm_w, lower_bounds, w_branch, w_out, ln_g, ln_b, loss_target, m_w_mod, m_b_mod, m_w_in, m_conv_w, m_hgrn_norm_w, m_lower_bounds, m_w_branch, m_w_out, m_ln_g, m_ln_b, v_w_mod, v_b_mod, v_w_in, v_conv_w, v_hgrn_norm_w, v_lower_bounds, v_w_branch, v_w_out, v_ln_g, v_ln_b):
    n_layers = N_LAYERS
    s_len, d = x.shape[1], x.shape[2]
    n_cols = w_in.shape[2] * NDEV
    cw_cols = conv_w.shape[2]
    cm = w_mod.shape[2]
    me = _my_index()
    x0 = x[0]
    target = loss_target[0]

    small = _pad_rows(jnp.concatenate([c.reshape(-1), conv_w.reshape(-1)]), BLK)
    small_all = _all_gather_small("gather_c_conv", small).reshape(NDEV, -1)
    c_all = small_all[:, :d]
    conv_full = small_all[:, d:d + n_layers * 3 * cw_cols].reshape(NDEV, n_layers, 3, cw_cols)
    conv_full = conv_full.transpose(1, 2, 0, 3).reshape(n_layers, 3, WIDTH)

    b_mod_mine = lax.dynamic_slice_in_dim(b_mod, me * cm, cm, axis=1).reshape(n_layers, 1, cm)
    mod_cols = _mod_fwd(c_all, w_mod, b_mod_mine).reshape(n_layers * NDEV, cm)

    shard = w_in.shape[2]
    dsh = d // NDEV
    w_in_b, w_branch_b, w_out_b = w_in.astype(BF16), w_branch.astype(BF16), w_out.astype(BF16)
    window = lambda ref, dev: ref.at[:, pl.ds(pl.multiple_of(dev * shard, BLK), shard)]

    def two_step_sends(places):
        chips, sibling = [], []
        for k in (1, 2, 4, 6):
            for a, place in enumerate(places):
                chips.append((k, lambda ins, lands, me, a=a: ins[a],
                              lambda lands, me, a=a, place=place: place(lands[a], me),
                              lambda lands, me, a=a, k=k, place=place: place(lands[a], me ^ k)))
        for j in (2, 4, 6):
            for a, place in enumerate(places):
                sibling.append((1, lambda ins, lands, me, a=a, j=j, place=place: place(lands[a], me ^ j),
                                lambda lands, me, a=a, j=j, place=place: place(lands[a], me ^ j),
                                lambda lands, me, a=a, j=j, place=place: place(lands[a], me ^ 1 ^ j)))
        return chips, sibling

    in_sends = two_step_sends([window])
    rest_sends = two_step_sends([_slot, _slot])
    layer_sends = two_step_sends([window, _slot, _slot])

    def in_land(l):
        return _place_own_window(f"place_w_in_{l}", (d, n_cols), w_in_b[l], me)

    def rest_lands(l):
        return [_place_own((NDEV, 3, WIDTH, dsh), BF16, w_branch_b[l][None], (me, 0, 0, 0)),
                _place_own((NDEV, dsh, d), BF16, w_out_b[l][None], (me, 0, 0))]

    def gather_start(name, shards, lands, sends, after):
        return _exchange_start(f"{name}_chips_start", shards, lands, sends[0], after)

    def gather_pass_on(name, started, after, sends):
        _, lands = _exchange_wait(f"{name}_chips_wait", started, after, sends[0])
        return _exchange_start(f"{name}_sibling_start", [], lands, sends[1])

    def gather_finish(name, started, after, sends):
        return _exchange_wait(f"{name}_sibling_wait", started, after, sends[1])[1]

    def branch_out_weights(w_branch_l, w_out_l):
        return w_branch_l.transpose(1, 2, 0, 3).reshape(3, WIDTH, d), w_out_l.reshape(d, d)

    mod_sends = [(k, lambda ins, lands, me: ins[1], lambda lands, me: lands[1].at[me],
                  lambda lands, me, k=k: lands[1].at[me ^ k]) for k in range(1, NDEV)]
    first_sends = (mod_sends + in_sends[0], in_sends[1])
    mod_land = _place_own((NDEV,) + mod_cols.shape, F32, mod_cols[None], (me, 0, 0))
    gathering = gather_start("gather_w_in_0", [w_in_b[0], mod_cols], [in_land(0), mod_land], first_sends, None)
    rest_gathering = gather_start("gather_rest_0", [w_branch_b[0], w_out_b[0]], rest_lands(0), rest_sends,
                                  gathering[4])
    next_gathering = None
    if n_layers > 1:
        next_gathering = gather_start("gather_weights_1", [w_in_b[1], w_branch_b[1], w_out_b[1]],
                                      [in_land(1)] + rest_lands(1), layer_sends, rest_gathering[4])
    _, (w_in_land, mod_all) = _exchange_wait("gather_w_in_0_chips_wait", gathering,
                                             (next_gathering or rest_gathering)[4], first_sends[0])
    passing = _exchange_start("gather_w_in_0_sibling_start", [], [w_in_land], in_sends[1])
    w_in_l = gather_finish("gather_w_in_0", passing, passing[4], in_sends)[0]
    mod_all = mod_all.reshape(NDEV, n_layers, NDEV, cm)
    mod_mine = lax.dynamic_index_in_dim(mod_all, me, axis=2, keepdims=False)
    mod_mine = mod_mine.transpose(1, 0, 2).reshape(n_layers, 3, 1, d)

    lbs = _lower_bound_table(lower_bounds)
    norm_w4 = jnp.tile(hgrn_norm_w, (1, WIDTH // HG_HEAD_DIM))

    saved = []
    xl = x0
    for l in range(n_layers):
        shift, scale, gate = mod_mine[l, 0], mod_mine[l, 1], mod_mine[l, 2]
        merge_after = None
        proj, h_t = _ln_proj(xl, shift, scale, w_in_l, f"ln_proj_{l}")
        o_a, totals = _sb_fwd(proj, f"sb_fwd_{l}")
        if l == 0:
            rest_passing = gather_pass_on("gather_rest_0", rest_gathering, o_a, rest_sends)
        o_b = _hgrn_fwd(proj, lbs[l:l + 1], f"hgrn_fwd_{l}", after=rest_passing[4] if l == 0 else None)
        if l == 0:
            wb_l, wo_l = branch_out_weights(*gather_finish("gather_rest_0", rest_passing, o_b, rest_sends))
            if n_layers > 1:
                next_passing = gather_pass_on("gather_weights_1", next_gathering, o_b, layer_sends)
                merge_after = next_passing[4]
        x_new, merged, ycat, *loss_term = _merge_fwd(
            xl, proj, o_a, o_b, gate, norm_w4[l:l + 1], conv_full[l], wb_l, wo_l, ln_g[l:l + 1], ln_b[l:l + 1],
            f"merge_fwd_{l}", target=target if l == n_layers - 1 else None, after=merge_after)
        saved.append((xl, proj, h_t, o_a, totals, o_b, merged, ycat, w_in_l, wb_l, wo_l))
        if l == 0 and n_layers > 1:
            w_in_l, w_branch_l, w_out_l = gather_finish("gather_weights_1", next_passing, x_new, layer_sends)
            wb_l, wo_l = branch_out_weights(w_branch_l, w_out_l)
        xl = x_new

    dx, loss_part = xl, loss_term[0]

    pair_sends = [(1, lambda ins, lands, me, j=j: window(ins[0], me ^ 1 ^ j),
                   lambda lands, me, jj=jj: lands[0].at[jj], lambda lands, me, jj=jj: lands[0].at[jj])
                  for jj, j in enumerate((0, 2, 4, 6))]
    chip_sum_sends = [(j, lambda ins, lands, me, jj=jj: ins[0].at[jj],
                       lambda lands, me, jj=jj: lands[0].at[jj], lambda lands, me, jj=jj: lands[0].at[jj])
                      for jj, j in ((1, 2), (2, 4), (3, 6))]
    rest_scatter = _direct_sends([(0, 0, _slot, _slot), (1, 1, _slot, _slot)])
    scattering = [None] * n_layers
    small_grads = [None] * n_layers
    dmod = [None] * n_layers
    tie = None
    for l in reversed(range(n_layers)):
        xl, proj, h_t, o_a, totals, o_b, merged, ycat, w_in_l, wb_l, wo_l = saved[l]
        scale, gate = mod_mine[l, 1], mod_mine[l, 2]
        dres, dycat, dproj, gwo_by_owner, gwb_by_owner, mvec = _merge_bwd(
            dx, xl, merged, ycat, proj, gate, wb_l, wo_l, ln_g[l:l + 1], f"merge_bwd_{l}", after=tie)
        lands = [_place_own((NDEV, 3, WIDTH, dsh), BF16, lax.dynamic_slice_in_dim(gwb_by_owner, me, 1, axis=0),
                            (me, 0, 0, 0)),
                 _place_own((NDEV, dsh, d), BF16, lax.dynamic_slice_in_dim(gwo_by_owner, me, 1, axis=0),
                            (me, 0, 0))]
        rest_started = _exchange_start(f"scatter_rest_{l}_start", [gwb_by_owner, gwo_by_owner], lands, rest_scatter)
        dproj, do_a, do_b, bvec = _branch_bwd(dycat, proj, o_a, o_b, norm_w4[l:l + 1], conv_full[l], dproj,
                                              f"branch_bwd_{l}", after=rest_started[4])
        dproj = _sb_bwd(proj, do_a, totals, dproj, f"sb_bwd_{l}")
        dproj, dlb = _hgrn_bwd(proj, do_b, lbs[l:l + 1], dproj, f"hgrn_bwd_{l}")
        gwi = _gw_matmul(h_t, dproj, f"gw_matmul_{l}")
        swapping = _exchange_start(f"scatter_in_{l}_sibling_start", [gwi], [lax.empty((4, d, shard), BF16)], pair_sends)
        if l > 0:
            dh = _dh_matmul(dproj, w_in_l, swapping[4], f"dh_matmul_{l}")
        (gwi,), (stage,) = _exchange_wait(f"scatter_in_{l}_sibling_wait", swapping, dh if l > 0 else swapping[4],
                                          pair_sends)
        land, chip_sums = _pair_sum(gwi, stage, me, f"pair_sum_{l}")
        in_started = _exchange_start(f"scatter_in_{l}_chips_start", [chip_sums], [land], chip_sum_sends)
        scattering[l] = (in_started, rest_started)
        tie = in_started[4]
        if l == 0:
            dh = _dh_matmul(dproj, w_in_l, tie, f"dh_matmul_{l}")
        dx, lvec = _ln_bwd(dh, xl, scale, dres, f"ln_bwd_{l}", after=tie)
        dmod[l] = jnp.concatenate([lvec[0], lvec[1], mvec[2]])
        norm_grad = bvec[0].reshape(WIDTH // HG_HEAD_DIM, HG_HEAD_DIM).sum(axis=0)
        small_grads[l] = jnp.concatenate([mvec[0], mvec[1], norm_grad, dlb[0], bvec[1:4].reshape(-1)])
    grad_x = dx[None]

    flat = lambda a: a.reshape(-1, a.shape[-1])
    big = {"w_in": (w_in, m_w_in, v_w_in), "w_branch": (w_branch, m_w_branch, v_w_branch),
           "w_out": (w_out, m_w_out, v_w_out)}
    big_results = {n: None for n in big}

    def adam_layer(l, after):
        in_started, rest_started = scattering[l]
        p_branch_l, p_out_l = _exchange_wait(f"scatter_rest_{l}_wait", rest_started, after, rest_scatter)[1]
        p_in_l = _exchange_wait(f"scatter_in_{l}_chips_wait", in_started, after, chip_sum_sends)[1][0]
        parts = {"w_in": p_in_l, "w_branch": p_branch_l.reshape(NDEV, 3 * WIDTH, dsh), "w_out": p_out_l}
        last = None
        for n, (w, m, v) in big.items():
            rows_per_layer = flat(w).shape[0] // n_layers
            big_results[n] = _sum_adamw(parts[n], flat(w), flat(m), flat(v), f"adamw_{n}_{l}",
                                        first_row=l * rows_per_layer, into=big_results[n], after=last)
            last = big_results[n][3]
        return last

    after_adam = None
    for l in reversed(range(1, n_layers)):
        after_adam = adam_layer(l, tie)

    small_vec = jnp.concatenate(dmod + small_grads + [loss_part.reshape(1)])
    n_small = small_vec.shape[0]
    small_all = _all_gather_small("gather_small_grads", _pad_rows(small_vec, BLK), after=after_adam)
    small_sum = _sum_parts(small_all, "sum_small_grads").reshape(-1)[:n_small]
    dmod_all = small_all.reshape(NDEV, -1)[:, :n_layers * 3 * d].reshape(NDEV, n_layers, 3 * d)

    loss = small_sum[n_small - 1]

    off = n_layers * 3 * d
    grad_b_mod = small_sum[:off].reshape(n_layers, 3 * d)
    per_layer = 2 * d + HG_HEAD_DIM + WIDTH + 3 * WIDTH
    g_ln_g, g_ln_b, g_norm, g_lbs, g_conv = [], [], [], [], []
    for l in range(n_layers):
        seg = small_sum[off + l * per_layer: off + (l + 1) * per_layer]
        g_ln_g.append(seg[:d])
        g_ln_b.append(seg[d:2 * d])
        g_norm.append(seg[2 * d:2 * d + HG_HEAD_DIM])
        g_lbs.append(seg[2 * d + HG_HEAD_DIM:2 * d + HG_HEAD_DIM + WIDTH])
        g_conv.append(seg[2 * d + HG_HEAD_DIM + WIDTH:].reshape(3, WIDTH))
    grad_ln_g, grad_ln_b = jnp.stack(g_ln_g), jnp.stack(g_ln_b)
    grad_norm = jnp.stack(g_norm)
    _, lbs_vjp = jax.vjp(_lower_bound_table, lower_bounds)
    grad_lower = lbs_vjp(jnp.stack(g_lbs))[0]
    grad_conv = lax.dynamic_slice_in_dim(jnp.stack(g_conv), me * cw_cols, cw_cols, axis=2)

    dmod_mine = lax.dynamic_slice_in_dim(dmod_all, me * cm, cm, axis=2).transpose(1, 0, 2)
    grad_w_mod = _wmod_grad(c_all.T, dmod_mine)

    adam_layer(0, grad_w_mod)
    r_w_in, r_w_branch, r_w_out = ([o.reshape(big[n][0].shape) for o in big_results[n]]
                                   for n in ("w_in", "w_branch", "w_out"))
    r_w_mod = [o.reshape(w_mod.shape) for o in
               _sum_adamw(grad_w_mod.reshape(1, -1, cm), flat(w_mod), flat(m_w_mod), flat(v_w_mod), "adamw_w_mod")]

    small_names = ["b_mod", "conv_w", "hgrn_norm_w", "lower_bounds", "ln_g", "ln_b"]
    small_g = [grad_b_mod, grad_conv, grad_norm, grad_lower, grad_ln_g, grad_ln_b]
    small_w = [b_mod, conv_w, hgrn_norm_w, lower_bounds, ln_g, ln_b]
    small_m = [m_b_mod, m_conv_w, m_hgrn_norm_w, m_lower_bounds, m_ln_g, m_ln_b]
    small_v = [v_b_mod, v_conv_w, v_hgrn_norm_w, v_lower_bounds, v_ln_g, v_ln_b]
    as_rows = lambda a: a.reshape(-1, a.shape[-1])
    updates = _adamw_small([as_rows(a) for a in small_g], [as_rows(a) for a in small_w],
                           [as_rows(a) for a in small_m], [as_rows(a) for a in small_v])
    r_small = {n: [g] + [u.reshape(w.shape) for u in upd]
               for n, g, w, upd in zip(small_names, small_g, small_w, updates)}

    results = {"w_mod": r_w_mod, "w_in": r_w_in, "w_branch": r_w_branch, "w_out": r_w_out, **r_small}
    order = ["w_mod", "b_mod", "w_in", "conv_w", "hgrn_norm_w", "lower_bounds", "w_branch", "w_out", "ln_g", "ln_b"]
    outs = [loss, grad_x]
    for idx in range(4):
        outs.extend(results[n][idx] for n in order)
    return tuple(outs)
```

```python
import jax
import jax.numpy as jnp
from jax import lax
from jax.experimental import pallas as pl
from jax.experimental.pallas import tpu as pltpu

F32 = jnp.float32
BF16 = jnp.bfloat16
NDEV = 8
N_LAYERS = 2
SB_HEAD_DIM = 64
HG_HEAD_DIM = 128
WIDTH = 512
BLK = 128
LN_EPS = 1e-5
RMS_EPS = 1e-6
ALPHA = (2.0 * N_LAYERS) ** 0.25
ADAM_LR, ADAM_B1, ADAM_B2, ADAM_EPS, ADAM_WD, ADAM_STEP = 0.001, 0.9, 0.999, 1e-08, 0.01, 10
VMEM_LIMIT = 56 * 1024 * 1024
MESH = pl.DeviceIdType.MESH
HG_LEVELS = (64, 32, 16, 8, 4, 2, 1)


def _pcall(body, *, name, out_shape, grid=None, in_specs=None, out_specs=None, scratch_shapes=(),
           semantics=None, aliases=None, after=None):
    if after is not None:
        n_in = len(in_specs)
        inner = body
        body = lambda *refs: inner(*refs[:n_in], *refs[n_in + 1:])
        in_specs = list(in_specs) + [pl.BlockSpec(memory_space=pl.ANY)]
    kwargs = {}
    if grid is not None:
        kwargs["grid"] = grid
    if in_specs is not None:
        kwargs["in_specs"] = in_specs
    if out_specs is not None:
        kwargs["out_specs"] = out_specs
    if aliases:
        kwargs["input_output_aliases"] = aliases
    call = pl.pallas_call(
        body, name=name, out_shape=out_shape, scratch_shapes=list(scratch_shapes),
        compiler_params=pltpu.CompilerParams(dimension_semantics=semantics, vmem_limit_bytes=VMEM_LIMIT),
        interpret=False, **kwargs)
    return call if after is None else (lambda *operands: call(*operands, after))


def _dot(a, b):
    return jnp.dot(a, b, preferred_element_type=F32)


def _dot_nt(a, b):
    return lax.dot_general(a, b, (((1,), (1,)), ((), ())), preferred_element_type=F32)


def _dot_tn(a, b):
    return lax.dot_general(a, b, (((0,), (0,)), ((), ())), preferred_element_type=F32)


def _dot_01_l(m_bf16, x):
    x1 = x.astype(BF16)
    x2 = (x - x1.astype(F32)).astype(BF16)
    return _dot(jnp.concatenate([m_bf16, m_bf16], axis=1), jnp.concatenate([x1, x2], axis=0))


def _sigmoid(x):
    return 1.0 / (1.0 + jnp.exp(-x))


def _silu_and_grad(x):
    s = _sigmoid(x)
    return x * s, s * (1.0 + x * (1.0 - s))


LOG2E = 1.4426950408889634
MASKED_SCORE = -1e30


def _softplus2_parts(z2):
    minus_abs = lax.bitcast_convert_type(lax.bitcast_convert_type(z2, jnp.int32) | jnp.int32(-2 ** 31), F32)
    sp2 = jnp.maximum(z2, 0.0) + jnp.log2(1.0 + jnp.exp2(minus_abs))
    return sp2, jnp.exp2(z2 - sp2)


def _split2_lanes(x):
    x1 = x.astype(BF16)
    return jnp.concatenate([x1, (x - x1.astype(F32)).astype(BF16)], axis=1)


def _iota2(shape, dim):
    return lax.broadcasted_iota(jnp.int32, shape, dim)


def _standardize(x):
    mu = jnp.mean(x, axis=-1, keepdims=True)
    xc = x - mu
    var = jnp.mean(xc * xc, axis=-1, keepdims=True)
    rstd = lax.rsqrt(var + LN_EPS)
    return xc * rstd, rstd


def _standardize_bwd(xhat, rstd, dxhat):
    m1 = jnp.mean(dxhat, axis=-1, keepdims=True)
    m2 = jnp.mean(dxhat * xhat, axis=-1, keepdims=True)
    return rstd * (dxhat - m1 - xhat * m2)


def _my_index():
    return 4 * lax.axis_index("x") + 2 * lax.axis_index("y") + lax.axis_index("c")


def _exchange(name, ins, out_shapes, transfers, in_vmem, after=None):
    n_in, n_out, n_t = len(ins), len(out_shapes), len(transfers)

    def body(*refs):
        n_skip = n_in + (0 if after is None else 1)
        in_refs, out_refs = refs[:n_in], refs[n_skip:n_skip + n_out]
        send_sems, recv_sems, local_sems = refs[n_skip + n_out:]
        x, y, c = lax.axis_index("x"), lax.axis_index("y"), lax.axis_index("c")
        me = 4 * x + 2 * y + c
        started = []
        for t, (i, o, src_fn, dst_fn) in enumerate(transfers):
            own = pltpu.make_async_copy(src_fn(in_refs[i], me), dst_fn(out_refs[o], me), local_sems.at[t])
            own.start()
            started.append(own)
        arrivals = []
        for k in range(1, NDEV):
            px = x ^ ((k >> 2) & 1)
            py = y ^ ((k >> 1) & 1)
            pc = c ^ (k & 1)
            peer = 4 * px + 2 * py + pc
            for t, (i, o, src_fn, dst_fn) in enumerate(transfers):
                sem = t * (NDEV - 1) + k - 1
                push = pltpu.make_async_remote_copy(
                    src_ref=src_fn(in_refs[i], peer), dst_ref=dst_fn(out_refs[o], me),
                    send_sem=send_sems.at[sem], recv_sem=recv_sems.at[sem],
                    device_id=(px, py, pc), device_id_type=MESH)
                push.start()
                started.append(push)
                arrivals.append(pltpu.make_async_remote_copy(
                    src_ref=src_fn(in_refs[i], peer), dst_ref=dst_fn(out_refs[o], peer),
                    send_sem=send_sems.at[sem], recv_sem=recv_sems.at[sem],
                    device_id=(px, py, pc), device_id_type=MESH))
        for arrival in arrivals:
            arrival.wait_recv()
        for cp in started[n_t:]:
            cp.wait_send()
        for own in started[:n_t]:
            own.wait()

    space = pltpu.VMEM if in_vmem else pl.ANY
    spec = pl.BlockSpec(memory_space=space)
    extra = [] if after is None else [after]
    return _pcall(
        body, name=name, out_shape=out_shapes,
        in_specs=[spec] * n_in + [pl.BlockSpec(memory_space=pl.ANY)] * len(extra), out_specs=[spec] * n_out,
        scratch_shapes=[pltpu.SemaphoreType.DMA((n_t * (NDEV - 1),)),
                        pltpu.SemaphoreType.DMA((n_t * (NDEV - 1),)),
                        pltpu.SemaphoreType.DMA((n_t,))])(*ins, *extra)


def _whole(ref, dev):
    return ref


def _slot(ref, dev):
    return ref.at[dev]


def _all_gather_small(name, v, after=None):
    out = _exchange(name, [v], [jax.ShapeDtypeStruct((NDEV,) + v.shape, v.dtype)],
                    [(0, 0, _whole, _slot)], in_vmem=True, after=after)
    return out[0]


_HBM_SPEC = pl.BlockSpec(memory_space=pltpu.HBM)
_SEM_SPEC = pl.BlockSpec(memory_space=pltpu.SEMAPHORE)
_DATAFLOW = pltpu.SideEffectType.DATAFLOW_SIDE_EFFECTING


def _peer(x, y, c, k):
    px = x ^ ((k >> 2) & 1)
    py = y ^ ((k >> 1) & 1)
    pc = c ^ (k & 1)
    return (px, py, pc), 4 * px + 2 * py + pc


def _direct_sends(transfers):
    sends = []
    for k in range(1, NDEV):
        for i, o, src_fn, dst_fn in transfers:
            sends.append((k,
                          lambda ins, lands, me, i=i, k=k, src_fn=src_fn: src_fn(ins[i], me ^ k),
                          lambda lands, me, o=o, dst_fn=dst_fn: dst_fn(lands[o], me),
                          lambda lands, me, o=o, k=k, dst_fn=dst_fn: dst_fn(lands[o], me ^ k)))
    return sends


def _exchange_start(name, ins, lands, sends, after=None):
    n_in, n_buf = len(ins), len(ins) + len(lands)
    n_sem = len(sends)

    def body(*refs):
        in_refs, land_refs = refs[:n_in], refs[n_in:n_buf]
        n_skip = n_buf + (0 if after is None else 1)
        send_sems, recv_sems, token = refs[n_skip], refs[n_skip + 1], refs[-1]
        x, y, c = lax.axis_index("x"), lax.axis_index("y"), lax.axis_index("c")
        me = 4 * x + 2 * y + c
        for t, (k, src_fn, dst_fn, _) in enumerate(sends):
            pltpu.make_async_remote_copy(
                src_ref=src_fn(in_refs, land_refs, me), dst_ref=dst_fn(land_refs, me),
                send_sem=send_sems.at[t], recv_sem=recv_sems.at[t],
                device_id=_peer(x, y, c, k)[0], device_id_type=MESH).start()
        token[...] = jnp.zeros_like(token)

    bufs = [pltpu.with_memory_space_constraint(a, pltpu.HBM) for a in list(ins) + list(lands)]
    extra = [] if after is None else [after]
    outs = pl.pallas_call(
        body, name=name,
        out_shape=(pltpu.SemaphoreType.DMA((n_sem,)), pltpu.SemaphoreType.DMA((n_sem,)))
        + tuple(pltpu.HBM(a.shape, a.dtype) for a in bufs) + (jax.ShapeDtypeStruct((8, BLK), F32),),
        in_specs=[_HBM_SPEC] * n_buf + [pl.BlockSpec(memory_space=pl.ANY)] * len(extra),
        out_specs=(_SEM_SPEC, _SEM_SPEC) + (_HBM_SPEC,) * n_buf + (pl.BlockSpec(memory_space=pltpu.VMEM),),
        input_output_aliases={b: 2 + b for b in range(n_buf)},
        compiler_params=pltpu.CompilerParams(has_side_effects=_DATAFLOW),
        interpret=False)(*bufs, *extra)
    return outs[0], outs[1], list(outs[2:2 + n_in]), list(outs[2 + n_in:2 + n_buf]), outs[-1]


def _exchange_wait(name, started, after, sends):
    send_sems, recv_sems, ins, lands, _ = started
    n_in, n_buf = len(ins), len(ins) + len(lands)

    def body(*refs):
        in_refs, land_refs = refs[:n_in], refs[n_in:n_buf]
        send_sems, recv_sems = refs[n_buf], refs[n_buf + 1]
        x, y, c = lax.axis_index("x"), lax.axis_index("y"), lax.axis_index("c")
        me = 4 * x + 2 * y + c
        for t, (k, src_fn, _, rcv_fn) in enumerate(sends):
            cp = pltpu.make_async_remote_copy(
                src_ref=src_fn(in_refs, land_refs, me), dst_ref=rcv_fn(land_refs, me),
                send_sem=send_sems.at[t], recv_sem=recv_sems.at[t],
                device_id=_peer(x, y, c, k)[0], device_id_type=MESH)
            cp.wait_send()
            cp.wait_recv()

    bufs = list(ins) + list(lands)
    outs = pl.pallas_call(
        body, name=name, out_shape=tuple(pltpu.HBM(a.shape, a.dtype) for a in bufs),
        in_specs=[_HBM_SPEC] * n_buf + [_SEM_SPEC, _SEM_SPEC, pl.BlockSpec(memory_space=pl.ANY)],
        out_specs=(_HBM_SPEC,) * n_buf,
        input_output_aliases={b: b for b in range(n_buf)},
        compiler_params=pltpu.CompilerParams(has_side_effects=_DATAFLOW),
        interpret=False)(*bufs, send_sems, recv_sems, after)
    return list(outs[:n_in]), list(outs[n_in:])


def _place_own(shape, dtype, own, start):
    return lax.dynamic_update_slice(lax.empty(shape, dtype), own, start)


def _place_own_window(name, shape, own, me):
    rows, cols = own.shape

    def body(me_ref, zone_in, own_ref, zone_ref):
        del me_ref, zone_in
        zone_ref[...] = own_ref[...]

    return pl.pallas_call(
        body, name=name, out_shape=jax.ShapeDtypeStruct(shape, own.dtype),
        grid_spec=pltpu.PrefetchScalarGridSpec(
            num_scalar_prefetch=1, grid=(1,),
            in_specs=[pl.BlockSpec(memory_space=pl.ANY), pl.BlockSpec((rows, cols), lambda i, me_ref: (0, 0))],
            out_specs=pl.BlockSpec((rows, cols), lambda i, me_ref: (0, me_ref[0]))),
        input_output_aliases={1: 0},
        compiler_params=pltpu.CompilerParams(dimension_semantics=("arbitrary",), vmem_limit_bytes=VMEM_LIMIT),
        interpret=False)(me.reshape(1).astype(jnp.int32), lax.empty(shape, own.dtype), own)


def _mod_fwd(c_all, w_mod, b_mod_mine):
    n_layers, _, cm = w_mod.shape

    def body(c_ref, w_ref, b_ref, o_ref):
        for l in range(n_layers):
            o_ref[l] = jnp.dot(c_ref[...], w_ref[l], preferred_element_type=F32,
                               precision=lax.Precision.HIGHEST) + b_ref[l]

    return _pcall(body, name="mod_fwd", out_shape=jax.ShapeDtypeStruct((n_layers, NDEV, cm), F32))(
        c_all, w_mod, b_mod_mine)


def _ln_proj(x, shift, scale, w_full, name):
    s_len, d = x.shape
    n = w_full.shape[1]
    tm = min(1024, s_len)
    tn = 2304

    def body(x_ref, sh_ref, sc_ref, w_ref, proj_ref, ht_ref, h_scr):
        @pl.when(pl.program_id(1) == 0)
        def _():
            xs, _ = _standardize(x_ref[...])
            h = xs * (1.0 + sc_ref[...]) + sh_ref[...]
            h_scr[...] = h.astype(BF16)
            ht_ref[...] = h.T.astype(BF16)

        proj_ref[...] = _dot(h_scr[...], w_ref[...])

    return _pcall(
        body, name=name,
        out_shape=(jax.ShapeDtypeStruct((s_len, n), F32), jax.ShapeDtypeStruct((d, s_len), BF16)),
        grid=(s_len // tm, n // tn),
        in_specs=[pl.BlockSpec((tm, d), lambda i, j: (i, 0)),
                  pl.BlockSpec((1, d), lambda i, j: (0, 0)),
                  pl.BlockSpec((1, d), lambda i, j: (0, 0)),
                  pl.BlockSpec((d, tn), lambda i, j: (0, j))],
        out_specs=(pl.BlockSpec((tm, tn), lambda i, j: (i, j)),
                   pl.BlockSpec((d, tm), lambda i, j: (0, i))),
        scratch_shapes=[pltpu.VMEM((tm, d), BF16)],
        semantics=("arbitrary", "arbitrary"))(x, shift, scale, w_full)


SB_Q_ROWS = 256
SB_K_BLOCKS = 2


def _sb_fwd(proj, name):
    s_len = proj.shape[0]
    n_pairs = WIDTH // BLK
    qr = min(SB_Q_ROWS, s_len)
    gb = SB_K_BLOCKS
    kw = gb * BLK
    nq = s_len // qr
    assert qr == kw

    def body(q_ref, k_ref, v_ref, o_ref, tot_ref):
        lane = _iota2((1, BLK), 1)
        row = _iota2((BLK, BLK), 0)
        col = _iota2((BLK, BLK), 1)
        half = jnp.concatenate([(row >= col).astype(BF16), jnp.ones((BLK, BLK), BF16)], axis=1)
        suffix_and_sum = jnp.concatenate([half, half], axis=0)
        strict = _iota2((qr, kw), 1) < _iota2((qr, kw), 0)
        head_lanes = [(lane // SB_HEAD_DIM) == hh for hh in range(2)]

        def scores(gi, qms, masked):
            c0 = pl.multiple_of(gi * kw, kw)
            kb = k_ref[pl.ds(c0, kw), :].astype(BF16)
            z2s = [_dot_nt(qms[hh], kb) for hh in range(2)]
            if masked:
                z2s = [jnp.where(strict, z2, MASKED_SCORE) for z2 in z2s]
            return tuple(z2s)

        def accumulate(gi, z2s, carry):
            c0 = pl.multiple_of(gi * kw, kw)
            vb = v_ref[pl.ds(c0, kw), :].astype(BF16)
            sp2s = [_softplus2_parts(z2)[0] for z2 in z2s]
            terms = [[_split2_lanes(sp2[:, b * BLK:(b + 1) * BLK]) for b in range(gb)] for sp2 in sp2s]
            sums = [[_dot(t, suffix_and_sum) for t in head_terms] for head_terms in terms]
            weights, laters = [], []
            for hh in range(2):
                later = carry[2 * hh + 1]
                parts = [None] * gb
                for b in reversed(range(gb)):
                    parts[b] = sums[hh][b][:, :BLK] + later
                    later = later + sums[hh][b][:, BLK:]
                weights.append(jnp.exp2(z2s[hh] - jnp.concatenate(parts, axis=1)).astype(BF16))
                laters.append(later)
            outs = [_dot(weights[hh], vb) for hh in range(2)]
            return (carry[0] + outs[0], laters[0], carry[2] + outs[1], laters[1])

        def queries(i):
            qf = q_ref[pl.ds(pl.multiple_of(i * qr, qr), qr), :] * (SB_HEAD_DIM ** -0.5 * LOG2E)
            return [jnp.where(head_lanes[hh], qf, 0.0).astype(BF16) for hh in range(2)]

        def qtile(i, first_scores):
            r0 = pl.multiple_of(i * qr, qr)
            qms = queries(i)
            zero = jnp.zeros((qr, BLK), F32)

            def step(jj, state):
                gi = i - 1 - jj
                return scores(gi, qms, False) + accumulate(gi + 1, state[:2], state[2:])

            state = lax.fori_loop(0, i, step, first_scores + (zero,) * 4)
            nxt = jnp.minimum(i + 1, nq - 1)
            next_scores = scores(nxt, queries(nxt), True)
            carry = accumulate(0, state[:2], state[2:])
            o_ref[pl.ds(r0, qr), :] = jnp.where(head_lanes[0], carry[0], carry[2])
            tot_ref[0, pl.ds(r0, qr), :] = carry[1]
            tot_ref[1, pl.ds(r0, qr), :] = carry[3]
            return next_scores

        lax.fori_loop(0, nq, qtile, scores(0, queries(0), True))

    col_spec = lambda off: pl.BlockSpec((s_len, BLK), lambda p: (0, off + p))
    return _pcall(
        body, name=name,
        out_shape=(jax.ShapeDtypeStruct((s_len, WIDTH), F32),
                   jax.ShapeDtypeStruct((2 * n_pairs, s_len, BLK), F32)),
        grid=(n_pairs,),
        in_specs=[col_spec(0), col_spec(n_pairs), col_spec(2 * n_pairs)],
        out_specs=(pl.BlockSpec((s_len, BLK), lambda p: (0, p)),
                   pl.BlockSpec((2, s_len, BLK), lambda p: (p, 0, 0))),
        semantics=("arbitrary",))(proj, proj, proj)


def _hg_masks(mask_ref):
    row = _iota2((BLK, BLK), 0)
    col = _iota2((BLK, BLK), 1)
    for v, m in enumerate(HG_LEVELS):
        same = (row // (2 * m)) == (col // (2 * m))
        mask_ref[v] = (same & ((row & m) != 0) & ((col & m) == 0)).astype(F32)


def _hg_mid(b, m):
    if m >= 4:
        n = BLK // (2 * m)
        mid = b.reshape(n, 2 * m, BLK)[:, m - 1:m, :]
        return jnp.broadcast_to(mid, (n, 2 * m, BLK)).reshape(BLK, BLK)
    pos = _iota2((BLK, BLK), 0) & (2 * m - 1)
    out = b
    for p in range(2 * m):
        delta = (m - 1) - p
        if delta != 0:
            out = jnp.where(pos == p, pltpu.roll(b, (-delta) % BLK, 0), out)
    return out


def _hg_chunk_inputs(qraw, fpre, lb):
    sig = _sigmoid(fpre)
    f = lb + (1.0 - lb) * sig
    g = jnp.log(f)
    q, dq_fac = _silu_and_grad(qraw)
    return q, dq_fac, f, sig, g


HG_GROUP = 4


def _neg_abs(x):
    return lax.bitcast_convert_type(lax.bitcast_convert_type(x, jnp.int32) | jnp.int32(-2 ** 31), F32)


def _hg_level_terms(qs, ks, bs, m):
    es = [jnp.exp(_neg_abs(b - _hg_mid(b, m))) for b in bs]
    qts = [(q * e).astype(BF16) for q, e in zip(qs, es)]
    kts = [(k * e).astype(BF16) for k, e in zip(ks, es)]
    return es, qts, kts


def _hg_load(refs, r0, lb_v, lower_incl):
    q_ref, f_ref, i_ref = refs
    heads = []
    for h in range(HG_GROUP):
        sl = slice(h * HG_HEAD_DIM, (h + 1) * HG_HEAD_DIM)
        heads.append(_hg_chunk_inputs(q_ref[pl.ds(r0, BLK), sl], f_ref[pl.ds(r0, BLK), sl], lb_v[:, sl])
                     + (i_ref[pl.ds(r0, BLK), sl],))
    bs = [_dot_01_l(lower_incl, hd[4]) for hd in heads]
    return heads, bs


def _hgrn_fwd(proj, lb, name, after=None):
    s_len = proj.shape[0]
    nc = s_len // BLK
    gw = HG_GROUP * HG_HEAD_DIM
    n_groups = WIDTH // gw
    base = 4 * WIDTH // gw

    def body(q_ref, f_ref, i_ref, lb_ref, o_ref, mask_ref):
        _hg_masks(mask_ref)
        row = _iota2((BLK, BLK), 0)
        col = _iota2((BLK, BLK), 1)
        lower_incl = (col <= row).astype(BF16)
        lb_v = lb_ref[...]

        def chunk(ci, sts):
            r0 = pl.multiple_of(ci * BLK, BLK)
            heads, bs = _hg_load((q_ref, f_ref, i_ref), r0, lb_v, lower_incl)
            qs = [hd[0] for hd in heads]
            ks = [1.0 - hd[2] for hd in heads]
            vs = [hd[5] for hd in heads]
            vbs = [v.astype(BF16) for v in vs]
            b_ends = [b[BLK - 1:BLK, :] for b in bs]
            inters = [_dot_nt((q * jnp.exp(b)).astype(BF16), st.astype(BF16)) for q, b, st in zip(qs, bs, sts)]
            scs = [None] * HG_GROUP
            for v_idx, m in enumerate(HG_LEVELS):
                _, qts, kts = _hg_level_terms(qs, ks, bs, m)
                terms = [_dot_nt(qt, kt) for qt, kt in zip(qts, kts)]
                msk = mask_ref[v_idx]
                scs = [t * msk if sc is None else sc + t * msk for sc, t in zip(scs, terms)]
            intras = [_dot(sc.astype(BF16), vb) for sc, vb in zip(scs, vbs)]
            k_decs = [(k * jnp.exp(b_end - b)).astype(BF16) for k, b, b_end in zip(ks, bs, b_ends)]
            grown = [_dot_tn(vb, k_dec) for vb, k_dec in zip(vbs, k_decs)]
            for h in range(HG_GROUP):
                diag = jnp.sum(qs[h] * ks[h], axis=-1, keepdims=True)
                o_ref[pl.ds(r0, BLK), h * HG_HEAD_DIM:(h + 1) * HG_HEAD_DIM] = inters[h] + intras[h] + diag * vs[h]
            return tuple(st * jnp.exp(b_end) + g for st, b_end, g in zip(sts, b_ends, grown))

        lax.fori_loop(0, nc, chunk, (jnp.zeros((HG_HEAD_DIM, HG_HEAD_DIM), F32),) * HG_GROUP)

    col_spec = lambda off: pl.BlockSpec((s_len, gw), lambda h: (0, off + h))
    return _pcall(
        body, name=name, out_shape=jax.ShapeDtypeStruct((s_len, WIDTH), F32),
        grid=(n_groups,),
        in_specs=[col_spec(base), col_spec(base + n_groups), col_spec(base + 2 * n_groups),
                  pl.BlockSpec((1, gw), lambda h: (0, h))],
        out_specs=pl.BlockSpec((s_len, gw), lambda h: (0, h)),
        scratch_shapes=[pltpu.VMEM((len(HG_LEVELS), BLK, BLK), F32)],
        semantics=("arbitrary",), after=after)(proj, proj, proj, lb)


def _rms_heads(o_b, norm_w):
    n_parts, h_parts, r_parts = [], [], []
    for h in range(WIDTH // HG_HEAD_DIM):
        sl = slice(h * HG_HEAD_DIM, (h + 1) * HG_HEAD_DIM)
        o = o_b[:, sl]
        rstd = lax.rsqrt(jnp.mean(o * o, axis=-1, keepdims=True) + RMS_EPS)
        ohat = o * rstd
        h_parts.append(ohat)
        n_parts.append(ohat * norm_w[:, sl])
        r_parts.append(jnp.broadcast_to(rstd, o.shape))
    cat = lambda parts: jnp.concatenate(parts, axis=-1)
    return cat(n_parts), cat(h_parts), cat(r_parts)


def _shift_rows_down(halo, cur, k):
    tm = cur.shape[0]
    ext = jnp.concatenate([halo, cur], axis=0)
    return pltpu.roll(ext, k, 0)[8:8 + tm]


def _shift_rows_up(cur, halo, k):
    tm = cur.shape[0]
    ext = jnp.concatenate([cur, halo], axis=0)
    return pltpu.roll(ext, (tm + 8 - k) % (tm + 8), 0)[0:tm]


def _merge_fwd(x, proj, o_a, o_b, gate, norm_w, conv_w, wb, w_out, ln_g, ln_b, name, target=None, after=None):
    s_len, d = x.shape
    tm = min(256, s_len)
    hb = tm // 8
    n_in = 19 + (0 if target is None else 1)

    def body(*refs):
        (x_ref, oa_ref, za_ref, ob_ref, zb_ref, pre_ref, post_ref, u_ref, zc_ref, hpre_ref, hu_ref, g_ref,
         gate_ref, nw_ref, cw_ref, wb_ref, wo_ref, lg_ref, lbias_ref) = refs[:19]
        xn_ref, mg_ref, yc_ref = refs[n_in:n_in + 3]
        i = pl.program_id(0)
        sa, _ = _silu_and_grad(za_ref[...])
        y_a = (oa_ref[...] * sa).astype(BF16)
        n_b, _, _ = _rms_heads(ob_ref[...], nw_ref[...])
        sb, _ = _silu_and_grad(zb_ref[...])
        y_b = (n_b * sb).astype(BF16)
        a = pre_ref[...] * u_ref[...]
        halo = jnp.where(i > 0, hpre_ref[...] * hu_ref[...], 0.0)
        cw = cw_ref[...]
        conv = cw[0:1] * _shift_rows_down(halo, a, 2) + cw[1:2] * _shift_rows_down(halo, a, 1) + cw[2:3] * a
        sc, _ = _silu_and_grad(zc_ref[...])
        y_c = (post_ref[...] * conv * sc).astype(BF16)
        merged = None
        for k, yk in enumerate((y_a, y_b, y_c)):
            yc_ref[:, k * WIDTH:(k + 1) * WIDTH] = yk
            term = _sigmoid(g_ref[:, k * d:(k + 1) * d]) * _dot(yk, wb_ref[k])
            merged = term if merged is None else merged + term
        mb = merged.astype(BF16)
        mg_ref[...] = mb
        y = _dot(mb, wo_ref[...])
        r = ALPHA * x_ref[...] + (1.0 + gate_ref[...]) * y
        rhat, _ = _standardize(r)
        xn = rhat * lg_ref[...] + lbias_ref[...]
        if target is None:
            xn_ref[...] = xn
        else:
            t_ref, loss_ref = refs[19], refs[n_in + 3]

            @pl.when(i == 0)
            def _():
                loss_ref[...] = jnp.zeros_like(loss_ref)

            e = xn - t_ref[...]
            xn_ref[...] = e * (1.0 / d)
            part = jnp.sum(jnp.sum(e * e, axis=-1, keepdims=True), axis=0, keepdims=True)
            loss_ref[...] += part * (0.5 / d)

    wcol = lambda cb: pl.BlockSpec((tm, WIDTH), lambda i: (i, cb))
    halo_spec = lambda cb: pl.BlockSpec((8, WIDTH), lambda i: (jnp.maximum(i * hb - 1, 0), cb))
    vec = lambda w: pl.BlockSpec((1, w), lambda i: (0, 0))
    tile = pl.BlockSpec((tm, d), lambda i: (i, 0))
    with_loss = target is not None
    return _pcall(
        body, name=name,
        out_shape=(jax.ShapeDtypeStruct((s_len, d), F32), jax.ShapeDtypeStruct((s_len, d), BF16),
                   jax.ShapeDtypeStruct((s_len, 3 * WIDTH), BF16))
        + ((jax.ShapeDtypeStruct((1, 1), F32),) if with_loss else ()),
        grid=(s_len // tm,),
        in_specs=[tile,
                  wcol(0), wcol(3), wcol(0), wcol(7), wcol(8), wcol(9), wcol(10), wcol(11),
                  halo_spec(8), halo_spec(10),
                  pl.BlockSpec((tm, 3 * d), lambda i: (i, 2)),
                  vec(d), vec(WIDTH),
                  pl.BlockSpec((3, WIDTH), lambda i: (0, 0)),
                  pl.BlockSpec((3, WIDTH, d), lambda i: (0, 0, 0)),
                  pl.BlockSpec((d, d), lambda i: (0, 0)),
                  vec(d), vec(d)] + ([tile] if with_loss else []),
        out_specs=(tile, tile, pl.BlockSpec((tm, 3 * WIDTH), lambda i: (i, 0)))
        + ((pl.BlockSpec((1, 1), lambda i: (0, 0)),) if with_loss else ()),
        semantics=("arbitrary",), after=after)(x, o_a, proj, o_b, proj, proj, proj, proj, proj, proj, proj, proj,
                                  gate, norm_w, conv_w, wb, w_out, ln_g, ln_b, *([target] if with_loss else []))


def _merge_bwd(dxn, x, merged, ycat, proj, gate, wb, w_out, ln_g, name, after=None):
    s_len, d = x.shape
    tm = min(256, s_len)
    dsh = d // NDEV
    n_tiles = s_len // tm

    def body(dxn_ref, x_ref, mg_ref, yc_ref, g_ref, gate_ref, wb_ref, wo_ref, lg_ref,
             dres_ref, dyc_ref, dg_ref, gwo_out, gwb_out, vec_ref, gwo_ref, gwb_ref):
        @pl.when(pl.program_id(0) == 0)
        def _():
            gwo_ref[...] = jnp.zeros_like(gwo_ref)
            gwb_ref[...] = jnp.zeros_like(gwb_ref)
            vec_ref[...] = jnp.zeros_like(vec_ref)

        mb = mg_ref[...]
        one_gate = 1.0 + gate_ref[...]
        y = _dot(mb, wo_ref[...])
        r = ALPHA * x_ref[...] + one_gate * y
        rhat, rstd = _standardize(r)
        dxn = dxn_ref[...]
        dr = _standardize_bwd(rhat, rstd, dxn * lg_ref[...])
        vec_ref[0:1, :] += jnp.sum(dxn * rhat, axis=0, keepdims=True)
        vec_ref[1:2, :] += jnp.sum(dxn, axis=0, keepdims=True)
        vec_ref[2:3, :] += jnp.sum(dr * y, axis=0, keepdims=True)
        dres_ref[...] = ALPHA * dr
        dy = (one_gate * dr).astype(BF16)
        gwo_ref[...] += _dot_tn(mb, dy)
        dmerged = _dot_nt(dy, wo_ref[...])
        for k in range(3):
            yk = yc_ref[:, k * WIDTH:(k + 1) * WIDTH]
            sg = _sigmoid(g_ref[:, k * d:(k + 1) * d])
            pk = _dot(yk, wb_ref[k])
            dg_ref[:, k * d:(k + 1) * d] = (dmerged * pk * sg * (1.0 - sg)).astype(BF16)
            dpk = (dmerged * sg).astype(BF16)
            dyc_ref[:, k * WIDTH:(k + 1) * WIDTH] = _dot_nt(dpk, wb_ref[k])
            gwb_ref[k] += _dot_tn(yk, dpk)

        @pl.when(pl.program_id(0) == n_tiles - 1)
        def _():
            for o in range(NDEV):
                gwo_out[o] = gwo_ref[o * dsh:(o + 1) * dsh, :].astype(BF16)
                for k in range(3):
                    gwb_out[o, k] = gwb_ref[k, :, o * dsh:(o + 1) * dsh].astype(BF16)

    tile = lambda w: pl.BlockSpec((tm, w), lambda i: (i, 0))
    vec = pl.BlockSpec((1, d), lambda i: (0, 0))
    return _pcall(
        body, name=name,
        out_shape=(jax.ShapeDtypeStruct((s_len, d), F32), jax.ShapeDtypeStruct((s_len, 3 * WIDTH), F32),
                   jax.ShapeDtypeStruct(proj.shape, BF16), jax.ShapeDtypeStruct((NDEV, dsh, d), BF16),
                   jax.ShapeDtypeStruct((NDEV, 3, WIDTH, dsh), BF16), jax.ShapeDtypeStruct((8, d), F32)),
        grid=(n_tiles,),
        in_specs=[tile(d), tile(d), tile(d), tile(3 * WIDTH),
                  pl.BlockSpec((tm, 3 * d), lambda i: (i, 2)),
                  vec, pl.BlockSpec((3, WIDTH, d), lambda i: (0, 0, 0)),
                  pl.BlockSpec((d, d), lambda i: (0, 0)), vec],
        out_specs=(tile(d), tile(3 * WIDTH), pl.BlockSpec((tm, 3 * d), lambda i: (i, 2)),
                   pl.BlockSpec((NDEV, dsh, d), lambda i: (0, 0, 0)),
                   pl.BlockSpec((NDEV, 3, WIDTH, dsh), lambda i: (0, 0, 0, 0)),
                   pl.BlockSpec((8, d), lambda i: (0, 0))),
        scratch_shapes=[pltpu.VMEM((d, d), F32), pltpu.VMEM((3, WIDTH, d), F32)],
        semantics=("arbitrary",), after=after)(dxn, x, merged, ycat, proj, gate, wb, w_out, ln_g)


def _branch_bwd(dycat, proj, o_a, o_b, norm_w, conv_w, dproj, name, after=None):
    s_len = proj.shape[0]
    tm = min(256, s_len)
    hb = tm // 8
    n_tiles = s_len // tm

    def body(dya_ref, dyb_ref, dyc_ref, oa_ref, za_ref, ob_ref, zb_ref, pre_ref, post_ref, u_ref, zc_ref,
             hpre_ref, hu_ref, ndyc_ref, npost_ref, nzc_ref, nw_ref, cw_ref, dproj_in,
             dproj_ref, doa_ref, dob_ref, vec_ref, dza_scr, dzb_scr, dc_scr, sems):
        del dproj_in
        i = pl.program_id(0)

        @pl.when(i == 0)
        def _():
            vec_ref[...] = jnp.zeros_like(vec_ref)

        sa, dsa = _silu_and_grad(za_ref[...])
        dya = dya_ref[...]
        doa_ref[...] = dya * sa
        dza_scr[...] = (dya * oa_ref[...] * dsa).astype(BF16)
        nw = nw_ref[...]
        n_b, ohat, rstd = _rms_heads(ob_ref[...], nw)
        sb, dsb = _silu_and_grad(zb_ref[...])
        dyb = dyb_ref[...]
        dzb_scr[...] = (dyb * n_b * dsb).astype(BF16)
        dn = dyb * sb
        vec_ref[0:1, :] += jnp.sum(dn * ohat, axis=0, keepdims=True)
        dnw = dn * nw
        parts = []
        for h in range(WIDTH // HG_HEAD_DIM):
            sl = slice(h * HG_HEAD_DIM, (h + 1) * HG_HEAD_DIM)
            m2 = jnp.mean(dnw[:, sl] * ohat[:, sl], axis=-1, keepdims=True)
            parts.append(rstd[:, sl] * (dnw[:, sl] - ohat[:, sl] * m2))
        dob_ref[...] = jnp.concatenate(parts, axis=-1)
        cw = cw_ref[...]
        pre, u, post = pre_ref[...], u_ref[...], post_ref[...]
        a = pre * u
        halo = jnp.where(i > 0, hpre_ref[...] * hu_ref[...], 0.0)
        a1 = _shift_rows_down(halo, a, 1)
        a2 = _shift_rows_down(halo, a, 2)
        conv = cw[0:1] * a2 + cw[1:2] * a1 + cw[2:3] * a
        sc, dsc = _silu_and_grad(zc_ref[...])
        dyc = dyc_ref[...]
        dconv = dyc * post * sc
        nsc, _ = _silu_and_grad(nzc_ref[...])
        nxt = jnp.where(i < n_tiles - 1, ndyc_ref[...] * npost_ref[...] * nsc, 0.0)
        da = cw[2:3] * dconv + cw[1:2] * _shift_rows_up(dconv, nxt, 1) + cw[0:1] * _shift_rows_up(dconv, nxt, 2)
        dc_scr[:, 0 * WIDTH:1 * WIDTH] = (da * u).astype(BF16)
        dc_scr[:, 1 * WIDTH:2 * WIDTH] = (dyc * conv * sc).astype(BF16)
        dc_scr[:, 2 * WIDTH:3 * WIDTH] = (da * pre).astype(BF16)
        dc_scr[:, 3 * WIDTH:4 * WIDTH] = (dyc * post * conv * dsc).astype(BF16)
        vec_ref[1:2, :] += jnp.sum(dconv * a2, axis=0, keepdims=True)
        vec_ref[2:3, :] += jnp.sum(dconv * a1, axis=0, keepdims=True)
        vec_ref[3:4, :] += jnp.sum(dconv * a, axis=0, keepdims=True)
        rows = pl.ds(pl.multiple_of(i * tm, tm), tm)
        copies = [pltpu.make_async_copy(dza_scr, dproj_ref.at[rows, 3 * WIDTH:4 * WIDTH], sems.at[0]),
                  pltpu.make_async_copy(dzb_scr, dproj_ref.at[rows, 7 * WIDTH:8 * WIDTH], sems.at[1]),
                  pltpu.make_async_copy(dc_scr, dproj_ref.at[rows, 8 * WIDTH:12 * WIDTH], sems.at[2])]
        for cp in copies:
            cp.start()
        for cp in copies:
            cp.wait()

    wcol = lambda cb: pl.BlockSpec((tm, WIDTH), lambda i: (i, cb))
    prev = lambda cb: pl.BlockSpec((8, WIDTH), lambda i: (jnp.maximum(i * hb - 1, 0), cb))
    nxt = lambda cb: pl.BlockSpec((8, WIDTH), lambda i: (jnp.minimum((i + 1) * hb, s_len // 8 - 1), cb))
    anyspec = pl.BlockSpec(memory_space=pl.ANY)
    out = jax.ShapeDtypeStruct((s_len, WIDTH), F32)
    return _pcall(
        body, name=name,
        out_shape=(jax.ShapeDtypeStruct(dproj.shape, dproj.dtype), out, out, jax.ShapeDtypeStruct((8, WIDTH), F32)),
        grid=(n_tiles,),
        in_specs=[wcol(0), wcol(1), wcol(2), wcol(0), wcol(3), wcol(0), wcol(7), wcol(8), wcol(9), wcol(10), wcol(11),
                  prev(8), prev(10), nxt(2), nxt(9), nxt(11),
                  pl.BlockSpec((1, WIDTH), lambda i: (0, 0)), pl.BlockSpec((3, WIDTH), lambda i: (0, 0)), anyspec],
        out_specs=(anyspec, wcol(0), wcol(0), pl.BlockSpec((8, WIDTH), lambda i: (0, 0))),
        scratch_shapes=[pltpu.VMEM((tm, WIDTH), BF16), pltpu.VMEM((tm, WIDTH), BF16),
                        pltpu.VMEM((tm, 4 * WIDTH), BF16), pltpu.SemaphoreType.DMA((3,))],
        aliases={18: 0},
        semantics=("arbitrary",), after=after)(dycat, dycat, dycat, o_a, proj, o_b, proj, proj, proj, proj, proj,
                                  proj, proj, dycat, proj, proj, norm_w, conv_w, dproj)


def _sb_bwd(proj, do_a, totals, dproj, name):
    s_len = proj.shape[0]
    n_pairs = WIDTH // BLK
    scale = SB_HEAD_DIM ** -0.5
    qr = min(SB_Q_ROWS, s_len)
    gb = SB_K_BLOCKS
    kw = gb * BLK
    nq = s_len // qr
    assert qr == kw

    def body(q_ref, k_ref, v_ref, do_ref, tot_ref, dproj_in, dproj_ref, dq_ref, dk_ref, dv_ref, out_scr, sems):
        del dproj_in
        lane = _iota2((1, BLK), 1)
        row = _iota2((BLK, BLK), 0)
        col = _iota2((BLK, BLK), 1)
        ones = jnp.ones((BLK, BLK), BF16)
        twice = lambda m: jnp.concatenate([m, m], axis=0)
        before_and_sum = twice(jnp.concatenate([(row < col).astype(BF16), ones], axis=1))
        upto_and_sum = twice(jnp.concatenate([(row <= col).astype(BF16), ones], axis=1))
        strict = _iota2((qr, kw), 1) < _iota2((qr, kw), 0)
        head_lanes = [(lane // SB_HEAD_DIM) == hh for hh in range(2)]
        dk_ref[...] = jnp.zeros_like(dk_ref)
        dv_ref[...] = jnp.zeros_like(dv_ref)

        def scores(gi, qms, masked):
            c0 = pl.multiple_of(gi * kw, kw)
            kb = k_ref[pl.ds(c0, kw), :].astype(BF16)
            z2s = [_dot_nt(qms[hh], kb) for hh in range(2)]
            if masked:
                z2s = [jnp.where(strict, z2, MASKED_SCORE) for z2 in z2s]
            return tuple(z2s)

        def process(gi, z2s, qms, doms, totals_i, carry):
            c0 = pl.multiple_of(gi * kw, kw)
            kb = k_ref[pl.ds(c0, kw), :].astype(BF16)
            vb = v_ref[pl.ds(c0, kw), :].astype(BF16)
            das = [_dot_nt(doms[hh], vb) for hh in range(2)]
            halves = [_softplus2_parts(z2) for z2 in z2s]
            terms = [[_split2_lanes(sp2[:, b * BLK:(b + 1) * BLK]) for b in range(gb)] for sp2, _ in halves]
            sums = [[_dot(t, before_and_sum) for t in head_terms] for head_terms in terms]
            weights, gmats, l_befores = [], [], []
            for hh in range(2):
                l_before = carry[3 * hh + 1]
                parts = []
                for b in range(gb):
                    parts.append(totals_i[hh] - l_before - sums[hh][b][:, :BLK])
                    l_before = l_before + sums[hh][b][:, BLK:]
                a = jnp.exp2(z2s[hh] - jnp.concatenate(parts, axis=1))
                weights.append(a.astype(BF16))
                gmats.append(a * das[hh])
                l_befores.append(l_before)
            terms = [[_split2_lanes(g[:, b * BLK:(b + 1) * BLK]) for b in range(gb)] for g in gmats]
            sums = [[_dot(t, upto_and_sum) for t in head_terms] for head_terms in terms]
            dzs, g_befores = [], []
            for hh in range(2):
                g_before = carry[3 * hh + 2]
                parts = []
                for b in range(gb):
                    parts.append(g_before + sums[hh][b][:, :BLK])
                    g_before = g_before + sums[hh][b][:, BLK:]
                dzs.append((gmats[hh] - halves[hh][1] * jnp.concatenate(parts, axis=1)).astype(BF16))
                g_befores.append(g_before)
            dk_t = _dot_tn(jnp.concatenate(qms, axis=0), jnp.concatenate(dzs, axis=0))
            dv_t = _dot_tn(jnp.concatenate(doms, axis=0), jnp.concatenate(weights, axis=0))
            dqs = [_dot(dzs[hh], kb) for hh in range(2)]
            dk_ref[:, pl.ds(c0, kw)] += dk_t * (1.0 / LOG2E)
            dv_ref[:, pl.ds(c0, kw)] += dv_t
            return (carry[0] + dqs[0], l_befores[0], g_befores[0], carry[3] + dqs[1], l_befores[1], g_befores[1])

        def queries(i):
            qf = q_ref[pl.ds(pl.multiple_of(i * qr, qr), qr), :] * (scale * LOG2E)
            return [jnp.where(head_lanes[hh], qf, 0.0).astype(BF16) for hh in range(2)]

        def qtile(i, first_scores, first_tile):
            r0 = pl.multiple_of(i * qr, qr)
            qms = queries(i)
            dof = do_ref[pl.ds(r0, qr), :]
            doms = [jnp.where(head_lanes[hh], dof, 0.0).astype(BF16) for hh in range(2)]
            totals_i = [tot_ref[hh, pl.ds(r0, qr), :] for hh in range(2)]
            zero = jnp.zeros((qr, BLK), F32)
            state = first_scores + (zero,) * 6
            if not first_tile:
                def step(gi, state):
                    return scores(gi + 1, qms, False) + process(gi, state[:2], qms, doms, totals_i, state[2:])

                state = lax.fori_loop(0, i - 1, step, state)
                state = scores(i, qms, True) + process(i - 1, state[:2], qms, doms, totals_i, state[2:])
            nxt = jnp.minimum(i + 1, nq - 1)
            next_scores = scores(0, queries(nxt), False)
            carry = process(i, state[:2], qms, doms, totals_i, state[2:])
            dq_ref[pl.ds(r0, qr), :] = jnp.where(head_lanes[0], carry[0], carry[3]) * scale
            return next_scores

        lax.fori_loop(1, nq, lambda i, sc: qtile(i, sc, False), qtile(0, scores(0, queries(0), True), True))
        pair = pl.program_id(0)
        copies = []
        for t, value in enumerate((dq_ref[...], dk_ref[...].T, dv_ref[...].T)):
            out_scr[t] = value.astype(BF16)
            col = pl.multiple_of((t * n_pairs + pair) * BLK, BLK)
            copies.append(pltpu.make_async_copy(out_scr.at[t], dproj_ref.at[:, pl.ds(col, BLK)], sems.at[t]))
            copies[-1].start()
        for cp in copies:
            cp.wait()

    col_spec = lambda off: pl.BlockSpec((s_len, BLK), lambda p: (0, off + p))
    anyspec = pl.BlockSpec(memory_space=pl.ANY)
    return _pcall(
        body, name=name, out_shape=jax.ShapeDtypeStruct(dproj.shape, dproj.dtype), grid=(n_pairs,),
        in_specs=[col_spec(0), col_spec(n_pairs), col_spec(2 * n_pairs), col_spec(0),
                  pl.BlockSpec((2, s_len, BLK), lambda p: (p, 0, 0)), anyspec],
        out_specs=anyspec,
        scratch_shapes=[pltpu.VMEM((s_len, BLK), F32), pltpu.VMEM((BLK, s_len), F32), pltpu.VMEM((BLK, s_len), F32),
                        pltpu.VMEM((3, s_len, BLK), BF16), pltpu.SemaphoreType.DMA((3,))],
        aliases={5: 0},
        semantics=("arbitrary",))(proj, proj, proj, do_a, totals, dproj)


def _hgrn_bwd(proj, do_b, lb, dproj, name):
    s_len = proj.shape[0]
    nc = s_len // BLK
    gw = HG_GROUP * HG_HEAD_DIM
    n_groups = WIDTH // gw
    base = 4 * WIDTH // gw
    heads_of = range(HG_GROUP)

    def body(q_ref, f_ref, i_ref, do_ref, lb_ref, dproj_in, dproj_ref, dlb_ref, mask_ref, st_ref, out_scr, sems):
        del dproj_in
        _hg_masks(mask_ref)
        row = _iota2((BLK, BLK), 0)
        col = _iota2((BLK, BLK), 1)
        lower_incl = (col <= row).astype(BF16)
        upper_incl = (col >= row).astype(BF16)
        lb_v = lb_ref[...]
        refs = (q_ref, f_ref, i_ref)

        def fwd_chunk(ci, sts):
            for h in heads_of:
                st_ref[ci, h] = sts[h]
            heads, bs = _hg_load(refs, pl.multiple_of(ci * BLK, BLK), lb_v, lower_incl)
            b_ends = [b[BLK - 1:BLK, :] for b in bs]
            k_decs = [((1.0 - hd[2]) * jnp.exp(b_end - b)).astype(BF16) for hd, b, b_end in zip(heads, bs, b_ends)]
            grown = [_dot_tn(hd[5].astype(BF16), k_dec) for hd, k_dec in zip(heads, k_decs)]
            return tuple(st * jnp.exp(b_end) + g for st, b_end, g in zip(sts, b_ends, grown))

        zero_state = (jnp.zeros((HG_HEAD_DIM, HG_HEAD_DIM), F32),) * HG_GROUP
        lax.fori_loop(0, nc, fwd_chunk, zero_state)

        def bwd_chunk(cc, carry):
            dsts, suffixes, dlbs = carry
            ci = nc - 1 - cc
            r0 = pl.multiple_of(ci * BLK, BLK)
            heads, bs = _hg_load(refs, r0, lb_v, lower_incl)
            qs = [hd[0] for hd in heads]
            fs = [hd[2] for hd in heads]
            ks = [1.0 - f for f in fs]
            vs = [hd[5] for hd in heads]
            vbs = [v.astype(BF16) for v in vs]
            dos = [do_ref[pl.ds(r0, BLK), h * HG_HEAD_DIM:(h + 1) * HG_HEAD_DIM] for h in heads_of]
            dobs = [do.astype(BF16) for do in dos]
            b_ends = [b[BLK - 1:BLK, :] for b in bs]
            e_qs = [jnp.exp(b) for b in bs]
            e_ks = [jnp.exp(b_end - b) for b, b_end in zip(bs, b_ends)]
            qes = [(q * e).astype(BF16) for q, e in zip(qs, e_qs)]
            khs = [(k * e).astype(BF16) for k, e in zip(ks, e_ks)]
            st_terms = [_split2_lanes(st_ref[ci, h]) for h in heads_of]
            ds_terms = [_split2_lanes(dst) for dst in dsts]
            dqes = [_dot(dob, t[:, :HG_HEAD_DIM]) + _dot(dob, t[:, HG_HEAD_DIM:]) for dob, t in zip(dobs, st_terms)]
            dkhs = [_dot(vb, t[:, :HG_HEAD_DIM]) + _dot(vb, t[:, HG_HEAD_DIM:]) for vb, t in zip(vbs, ds_terms)]
            dvs = [_dot_nt(kh, t[:, :HG_HEAD_DIM]) for kh, t in zip(khs, ds_terms)]
            grown = [_dot_tn(dob, qe) for dob, qe in zip(dobs, qes)]
            das = [_dot_nt(dob, vb) for dob, vb in zip(dobs, vbs)]
            dqs = [e * dqe for e, dqe in zip(e_qs, dqes)]
            dks = [e * dkh for e, dkh in zip(e_ks, dkhs)]
            dlogs = [qe.astype(F32) * dqe - kh.astype(F32) * dkh for qe, dqe, kh, dkh in zip(qes, dqes, khs, dkhs)]
            scs = [None] * HG_GROUP
            for v_idx, m in enumerate(HG_LEVELS):
                es, qms, kms = _hg_level_terms(qs, ks, bs, m)
                msk = mask_ref[v_idx]
                terms = [_dot_nt(qm, km) for qm, km in zip(qms, kms)]
                pms = [(da * msk).astype(BF16) for da in das]
                dqms = [_dot(pm, km) for pm, km in zip(pms, kms)]
                dkms = [_dot_tn(pm, qm) for pm, qm in zip(pms, qms)]
                scs = [t * msk if sc is None else sc + t * msk for sc, t in zip(scs, terms)]
                dqs = [dq + dqm * e for dq, dqm, e in zip(dqs, dqms, es)]
                dks = [dk + dkm * e for dk, dkm, e in zip(dks, dkms, es)]
                dlogs = [dl + (qm.astype(F32) * dqm - km.astype(F32) * dkm)
                         for dl, qm, dqm, km, dkm in zip(dlogs, qms, dqms, kms, dkms)]
            intras = [_dot_tn(sc.astype(BF16), dob) for sc, dob in zip(scs, dobs)]
            dgs = [_dot_01_l(upper_incl, dl) + sfx for dl, sfx in zip(dlogs, suffixes)]
            new_dlbs = []
            for h in heads_of:
                q, dq_fac, f, sig = heads[h][0], heads[h][1], heads[h][2], heads[h][3]
                a_diag = jnp.sum(dos[h] * vs[h], axis=-1, keepdims=True)
                s_diag = jnp.sum(q * ks[h], axis=-1, keepdims=True)
                dq = dqs[h] + a_diag * ks[h]
                dk = dks[h] + a_diag * q
                dv = dvs[h] + intras[h] + s_diag * dos[h]
                dfull = dgs[h] / f - dk
                sl = slice(h * HG_HEAD_DIM, (h + 1) * HG_HEAD_DIM)
                out_scr[0, pl.ds(r0, BLK), sl] = (dq * dq_fac).astype(BF16)
                out_scr[1, pl.ds(r0, BLK), sl] = (dfull * (1.0 - lb_v[:, sl]) * sig * (1.0 - sig)).astype(BF16)
                out_scr[2, pl.ds(r0, BLK), sl] = dv.astype(BF16)
                new_dlbs.append(dlbs[h] + jnp.sum(dfull * (1.0 - sig), axis=0, keepdims=True))
            new_dsts = tuple(dst * jnp.exp(b_end) + g for dst, b_end, g in zip(dsts, b_ends, grown))
            return new_dsts, tuple(dg[0:1, :] for dg in dgs), tuple(new_dlbs)

        zero_row = (jnp.zeros((1, HG_HEAD_DIM), F32),) * HG_GROUP
        _, _, dlbs = lax.fori_loop(0, nc, bwd_chunk, (zero_state, zero_row, zero_row))
        dlb_ref[...] = jnp.broadcast_to(jnp.concatenate(dlbs, axis=1), dlb_ref.shape)
        group = pl.program_id(0)
        copies = []
        for t in range(3):
            col = pl.multiple_of((base + t * n_groups + group) * gw, gw)
            copies.append(pltpu.make_async_copy(out_scr.at[t], dproj_ref.at[:, pl.ds(col, gw)], sems.at[t]))
            copies[-1].start()
        for cp in copies:
            cp.wait()

    col_spec = lambda off: pl.BlockSpec((s_len, gw), lambda h: (0, off + h))
    anyspec = pl.BlockSpec(memory_space=pl.ANY)
    return _pcall(
        body, name=name,
        out_shape=(jax.ShapeDtypeStruct(dproj.shape, dproj.dtype), jax.ShapeDtypeStruct((8, WIDTH), F32)),
        grid=(n_groups,),
        in_specs=[col_spec(base), col_spec(base + n_groups), col_spec(base + 2 * n_groups), col_spec(0),
                  pl.BlockSpec((1, gw), lambda h: (0, h)), anyspec],
        out_specs=(anyspec, pl.BlockSpec((8, gw), lambda h: (0, h))),
        scratch_shapes=[pltpu.VMEM((len(HG_LEVELS), BLK, BLK), F32),
                        pltpu.VMEM((nc, HG_GROUP, HG_HEAD_DIM, HG_HEAD_DIM), F32),
                        pltpu.VMEM((3, s_len, gw), BF16), pltpu.SemaphoreType.DMA((3,))],
        aliases={5: 0},
        semantics=("arbitrary",))(proj, proj, proj, do_b, lb, dproj)


def _dh_matmul(dproj, w_full, after, name):
    s_len, n = dproj.shape
    d = w_full.shape[0]
    tm = min(1024, s_len)
    tk = 4608

    def body(dp_ref, w_ref, after_ref, dh_ref):
        del after_ref
        part = _dot_nt(dp_ref[...], w_ref[...])

        @pl.when(pl.program_id(1) == 0)
        def _():
            dh_ref[...] = part

        @pl.when(pl.program_id(1) > 0)
        def _():
            dh_ref[...] += part

    return _pcall(
        body, name=name, out_shape=jax.ShapeDtypeStruct((s_len, d), F32),
        grid=(s_len // tm, n // tk),
        in_specs=[pl.BlockSpec((tm, tk), lambda i, k: (i, k)), pl.BlockSpec((d, tk), lambda i, k: (0, k)),
                  pl.BlockSpec(memory_space=pl.ANY)],
        out_specs=pl.BlockSpec((tm, d), lambda i, k: (i, 0)),
        semantics=("arbitrary", "arbitrary"))(dproj, w_full, after)


def _gw_matmul(h_t, dproj, name):
    d, s_len = h_t.shape
    n = dproj.shape[1]
    tn = 2304

    def body(ht_ref, dp_ref, gw_ref):
        gw_ref[...] = _dot(ht_ref[...], dp_ref[...]).astype(BF16)

    return _pcall(
        body, name=name, out_shape=jax.ShapeDtypeStruct((d, n), BF16),
        grid=(n // tn,),
        in_specs=[pl.BlockSpec((d, s_len), lambda j: (0, 0)), pl.BlockSpec((s_len, tn), lambda j: (0, j))],
        out_specs=pl.BlockSpec((d, tn), lambda j: (0, j)),
        semantics=("arbitrary",))(h_t, dproj)


def _ln_bwd(dh, x, scale, dres, name, after=None):
    s_len, d = x.shape
    tm = min(512, s_len)

    def body(dh_ref, x_ref, sc_ref, dres_ref, dx_ref, vec_ref):
        @pl.when(pl.program_id(0) == 0)
        def _():
            vec_ref[...] = jnp.zeros_like(vec_ref)

        dh = dh_ref[...]
        xs, rstd = _standardize(x_ref[...])
        vec_ref[0:1, :] += jnp.sum(dh, axis=0, keepdims=True)
        vec_ref[1:2, :] += jnp.sum(dh * xs, axis=0, keepdims=True)
        dx_ref[...] = _standardize_bwd(xs, rstd, dh * (1.0 + sc_ref[...])) + dres_ref[...]

    tile = pl.BlockSpec((tm, d), lambda i: (i, 0))
    return _pcall(body, name=name, grid=(s_len // tm,),
                  out_shape=(jax.ShapeDtypeStruct((s_len, d), F32), jax.ShapeDtypeStruct((8, d), F32)),
                  in_specs=[tile, tile, pl.BlockSpec((1, d), lambda i: (0, 0)), tile],
                  out_specs=(tile, pl.BlockSpec((8, d), lambda i: (0, 0))),
                  semantics=("arbitrary",), after=after)(dh, x, scale, dres)


def _wmod_grad(c_t, dmod):
    d = c_t.shape[0]
    n_layers, _, cm = dmod.shape

    def body(c_ref, dm_ref, o_ref):
        for l in range(n_layers):
            acc = None
            for b in range(NDEV):
                term = c_ref[:, b:b + 1] * dm_ref[l, b:b + 1, :]
                acc = term if acc is None else acc + term
            o_ref[l] = acc

    return _pcall(body, name="wmod_grad", out_shape=jax.ShapeDtypeStruct((n_layers, d, cm), F32))(c_t, dmod)


def _sum_adamw(parts, w, m, v, name, first_row=0, into=None, after=None):
    n_src, range_rows, cols = parts.shape
    rows = w.shape[0]
    tr = range_rows
    for cand in (512, 256, 128, 64, 32, 16, 8):
        if range_rows % cand == 0 and cand * cols * 4 <= (2 << 20):
            tr = cand
            break
    first_tile = first_row // tr
    assert first_row % tr == 0
    n_extra = (0 if into is None else 4) + (0 if after is None else 1)

    def body(p_ref, w_ref, m_ref, v_ref, *rest):
        g_ref, d_ref, nm_ref, nv_ref = rest[n_extra:]
        g = p_ref[0].astype(F32)
        for s in range(1, n_src):
            g = g + p_ref[s].astype(F32)
        g_ref[...] = g
        d_ref[...], nm_ref[...], nv_ref[...] = _adamw_step(g, w_ref[...], m_ref[...], v_ref[...])

    tile = pl.BlockSpec((tr, cols), lambda i: (i + first_tile, 0))
    anyspec = pl.BlockSpec(memory_space=pl.ANY)
    out = jax.ShapeDtypeStruct((rows, cols), F32)
    extra = ([] if into is None else list(into)) + ([] if after is None else [after])
    aliases = {} if into is None else {4 + k: k for k in range(4)}
    return _pcall(body, name=name, grid=(range_rows // tr,), out_shape=(out,) * 4,
                  in_specs=[pl.BlockSpec((n_src, tr, cols), lambda i: (0, i, 0)), tile, tile, tile]
                  + [anyspec] * len(extra),
                  out_specs=(tile,) * 4, aliases=aliases, semantics=("arbitrary",))(parts, w, m, v, *extra)


def _adamw_step(g, w, m, v):
    nm = ADAM_B1 * m + (1.0 - ADAM_B1) * g
    nv = ADAM_B2 * v + (1.0 - ADAM_B2) * (g * g)
    m_hat = nm / (1.0 - ADAM_B1 ** ADAM_STEP)
    v_hat = nv / (1.0 - ADAM_B2 ** ADAM_STEP)
    return -ADAM_LR * (m_hat / (jnp.sqrt(v_hat) + ADAM_EPS) + ADAM_WD * w), nm, nv


def _adamw_small(gs, ws, ms, vs):
    n = len(gs)

    def body(*refs):
        for p in range(n):
            results = _adamw_step(*(refs[k * n + p][...] for k in range(4)))
            for k in range(3):
                refs[(4 + k) * n + p][...] = results[k]

    shapes = [jax.ShapeDtypeStruct(w.shape, F32) for w in ws]
    outs = _pcall(body, name="adamw_small", out_shape=shapes * 3)(*gs, *ws, *ms, *vs)
    return [(outs[p], outs[n + p], outs[2 * n + p]) for p in range(n)]


def _sum_parts(parts, name):
    n_src = parts.shape[0]

    def body(p_ref, o_ref):
        acc = p_ref[0]
        for s in range(1, n_src):
            acc = acc + p_ref[s]
        o_ref[...] = acc

    return _pcall(body, name=name, out_shape=jax.ShapeDtypeStruct(parts.shape[1:], F32))(parts)


def _pair_sum(gw, stage, me, name):
    d = gw.shape[0]
    n_slots, _, shard = stage.shape

    def body(me_ref, g_ref, s_ref, own_ref, o_ref):
        del me_ref
        total = (g_ref[...].astype(F32) + s_ref[0].astype(F32)).astype(BF16)
        o_ref[0] = total

        @pl.when(pl.program_id(0) == 0)
        def _():
            own_ref[0] = total

    slot = pl.BlockSpec((1, d, shard), lambda jj, me_ref: (jj, 0, 0))
    out = jax.ShapeDtypeStruct(stage.shape, BF16)
    return pl.pallas_call(
        body, name=name, out_shape=(out, out),
        grid_spec=pltpu.PrefetchScalarGridSpec(
            num_scalar_prefetch=1, grid=(n_slots,),
            in_specs=[pl.BlockSpec((d, shard), lambda jj, me_ref: (0, me_ref[0] ^ (2 * jj))), slot],
            out_specs=(pl.BlockSpec((1, d, shard), lambda jj, me_ref: (0, 0, 0)), slot)),
        compiler_params=pltpu.CompilerParams(dimension_semantics=("arbitrary",), vmem_limit_bytes=VMEM_LIMIT),
        interpret=False)(me.reshape(1).astype(jnp.int32), gw, stage)


def _lower_bound_table(lower_bounds):
    p = jax.nn.softmax(lower_bounds.astype(F32), axis=0)
    return jnp.cumsum(p, axis=0) - p[0:1]


def _pad_rows(v, width):
    n = v.shape[0]
    rows = -(-n // width)
    rows = -(-rows // 8) * 8
    return jnp.pad(v, (0, rows * width - n)).reshape(rows, width)


def kernel(x, c, w_mod, b_mod, w_in, conv_w, hgrn_norm_w, lower_bounds, w_branch, w_out, ln_g, ln_b, loss_target, m_w_mod, m_b_mod, m_w_in, m_conv_w, m_hgrn_norm_w, m_lower_bounds, m_w_branch, m_w_out, m_ln_g, m_ln_b, v_w_mod, v_b_mod, v_w_in, v_conv_w, v_hgrn_norm_w, v_lower_bounds, v_w_branch, v_w_out, v_ln_g, v_ln_b):
    n_layers = N_LAYERS
    s_len, d = x.shape[1], x.shape[2]
    n_cols = w_in.shape[2] * NDEV
    cw_cols = conv_w.shape[2]
    cm = w_mod.shape[2]
    me = _my_index()
    x0 = x[0]
    target = loss_target[0]

    small = _pad_rows(jnp.concatenate([c.reshape(-1), conv_w.reshape(-1)]), BLK)
    small_all = _all_gather_small("gather_c_conv", small).reshape(NDEV, -1)
    c_all = small_all[:, :d]
    conv_full = small_all[:, d:d + n_layers * 3 * cw_cols].reshape(NDEV, n_layers, 3, cw_cols)
    conv_full = conv_full.transpose(1, 2, 0, 3).reshape(n_layers, 3, WIDTH)

    b_mod_mine = lax.dynamic_slice_in_dim(b_mod, me * cm, cm, axis=1).reshape(n_layers, 1, cm)
    mod_cols = _mod_fwd(c_all, w_mod, b_mod_mine).reshape(n_layers * NDEV, cm)

    shard = w_in.shape[2]
    dsh = d // NDEV
    w_in_b, w_branch_b, w_out_b = w_in.astype(BF16), w_branch.astype(BF16), w_out.astype(BF16)
    window = lambda ref, dev: ref.at[:, pl.ds(pl.multiple_of(dev * shard, BLK), shard)]

    def two_step_sends(places):
        chips, sibling = [], []
        for k in (1, 2, 4, 6):
            for a, place in enumerate(places):
                chips.append((k, lambda ins, lands, me, a=a: ins[a],
                              lambda lands, me, a=a, place=place: place(lands[a], me),
                              lambda lands, me, a=a, k=k, place=place: place(lands[a], me ^ k)))
        for j in (2, 4, 6):
            for a, place in enumerate(places):
                sibling.append((1, lambda ins, lands, me, a=a, j=j, place=place: place(lands[a], me ^ j),
                                lambda lands, me, a=a, j=j, place=place: place(lands[a], me ^ j),
                                lambda lands, me, a=a, j=j, place=place: place(lands[a], me ^ 1 ^ j)))
        return chips, sibling

    in_sends = two_step_sends([window])
    rest_sends = two_step_sends([_slot, _slot])
    layer_sends = two_step_sends([window, _slot, _slot])

    def in_land(l):
        return _place_own_window(f"place_w_in_{l}", (d, n_cols), w_in_b[l], me)

    def rest_lands(l):
        return [_place_own((NDEV, 3, WIDTH, dsh), BF16, w_branch_b[l][None], (me, 0, 0, 0)),
                _place_own((NDEV, dsh, d), BF16, w_out_b[l][None], (me, 0, 0))]

    def gather_start(name, shards, lands, sends, after):
        return _exchange_start(f"{name}_chips_start", shards, lands, sends[0], after)

    def gather_pass_on(name, started, after, sends):
        _, lands = _exchange_wait(f"{name}_chips_wait", started, after, sends[0])
        return _exchange_start(f"{name}_sibling_start", [], lands, sends[1])

    def gather_finish(name, started, after, sends):
        return _exchange_wait(f"{name}_sibling_wait", started, after, sends[1])[1]

    def branch_out_weights(w_branch_l, w_out_l):
        return w_branch_l.transpose(1, 2, 0, 3).reshape(3, WIDTH, d), w_out_l.reshape(d, d)

    mod_sends = [(k, lambda ins, lands, me: ins[1], lambda lands, me: lands[1].at[me],
                  lambda lands, me, k=k: lands[1].at[me ^ k]) for k in range(1, NDEV)]
    first_sends = (mod_sends + in_sends[0], in_sends[1])
    mod_land = _place_own((NDEV,) + mod_cols.shape, F32, mod_cols[None], (me, 0, 0))
    gathering = gather_start("gather_w_in_0", [w_in_b[0], mod_cols], [in_land(0), mod_land], first_sends, None)
    rest_gathering = gather_start("gather_rest_0", [w_branch_b[0], w_out_b[0]], rest_lands(0), rest_sends,
                                  gathering[4])
    next_gathering = None
    if n_layers > 1:
        next_gathering = gather_start("gather_weights_1", [w_in_b[1], w_branch_b[1], w_out_b[1]],
                                      [in_land(1)] + rest_lands(1), layer_sends, rest_gathering[4])
    _, (w_in_land, mod_all) = _exchange_wait("gather_w_in_0_chips_wait", gathering,
                                             (next_gathering or rest_gathering)[4], first_sends[0])
    passing = _exchange_start("gather_w_in_0_sibling_start", [], [w_in_land], in_sends[1])
    w_in_l = gather_finish("gather_w_in_0", passing, passing[4], in_sends)[0]
    mod_all = mod_all.reshape(NDEV, n_layers, NDEV, cm)
    mod_mine = lax.dynamic_index_in_dim(mod_all, me, axis=2, keepdims=False)
    mod_mine = mod_mine.transpose(1, 0, 2).reshape(n_layers, 3, 1, d)

    lbs = _lower_bound_table(lower_bounds)
    norm_w4 = jnp.tile(hgrn_norm_w, (1, WIDTH // HG_HEAD_DIM))

    saved = []
    xl = x0
    for l in range(n_layers):
        shift, scale, gate = mod_mine[l, 0], mod_mine[l, 1], mod_mine[l, 2]
        merge_after = None
        proj, h_t = _ln_proj(xl, shift, scale, w_in_l, f"ln_proj_{l}")
        o_a, totals = _sb_fwd(proj, f"sb_fwd_{l}")
        if l == 0:
            rest_passing = gather_pass_on("gather_rest_0", rest_gathering, o_a, rest_sends)
        o_b = _hgrn_fwd(proj, lbs[l:l + 1], f"hgrn_fwd_{l}", after=rest_passing[4] if l == 0 else None)
        if l == 0:
            wb_l, wo_l = branch_out_weights(*gather_finish("gather_rest_0", rest_passing, o_b, rest_sends))
            if n_layers > 1:
                next_passing = gather_pass_on("gather_weights_1", next_gathering, o_b, layer_sends)
                merge_after = next_passing[4]
        x_new, merged, ycat, *loss_term = _merge_fwd(
            xl, proj, o_a, o_b, gate, norm_w4[l:l + 1], conv_full[l], wb_l, wo_l, ln_g[l:l + 1], ln_b[l:l + 1],
            f"merge_fwd_{l}", target=target if l == n_layers - 1 else None, after=merge_after)
        saved.append((xl, proj, h_t, o_a, totals, o_b, merged, ycat, w_in_l, wb_l, wo_l))
        if l == 0 and n_layers > 1:
            w_in_l, w_branch_l, w_out_l = gather_finish("gather_weights_1", next_passing, x_new, layer_sends)
            wb_l, wo_l = branch_out_weights(w_branch_l, w_out_l)
        xl = x_new

    dx, loss_part = xl, loss_term[0]

    pair_sends = [(1, lambda ins, lands, me, j=j: window(ins[0], me ^ 1 ^ j),
                   lambda lands, me, jj=jj: lands[0].at[jj], lambda lands, me, jj=jj: lands[0].at[jj])
                  for jj, j in enumerate((0, 2, 4, 6))]
    chip_sum_sends = [(j, lambda ins, lands, me, jj=jj: ins[0].at[jj],
                       lambda lands, me, jj=jj: lands[0].at[jj], lambda lands, me, jj=jj: lands[0].at[jj])
                      for jj, j in ((1, 2), (2, 4), (3, 6))]
    rest_scatter = _direct_sends([(0, 0, _slot, _slot), (1, 1, _slot, _slot)])
    scattering = [None] * n_layers
    small_grads = [None] * n_layers
    dmod = [None] * n_layers
    tie = None
    for l in reversed(range(n_layers)):
        xl, proj, h_t, o_a, totals, o_b, merged, ycat, w_in_l, wb_l, wo_l = saved[l]
        scale, gate = mod_mine[l, 1], mod_mine[l, 2]
        dres, dycat, dproj, gwo_by_owner, gwb_by_owner, mvec = _merge_bwd(
            dx, xl, merged, ycat, proj, gate, wb_l, wo_l, ln_g[l:l + 1], f"merge_bwd_{l}", after=tie)
        lands = [_place_own((NDEV, 3, WIDTH, dsh), BF16, lax.dynamic_slice_in_dim(gwb_by_owner, me, 1, axis=0),
                            (me, 0, 0, 0)),
                 _place_own((NDEV, dsh, d), BF16, lax.dynamic_slice_in_dim(gwo_by_owner, me, 1, axis=0),
                            (me, 0, 0))]
        rest_started = _exchange_start(f"scatter_rest_{l}_start", [gwb_by_owner, gwo_by_owner], lands, rest_scatter)
        dproj, do_a, do_b, bvec = _branch_bwd(dycat, proj, o_a, o_b, norm_w4[l:l + 1], conv_full[l], dproj,
                                              f"branch_bwd_{l}", after=rest_started[4])
        dproj = _sb_bwd(proj, do_a, totals, dproj, f"sb_bwd_{l}")
        dproj, dlb = _hgrn_bwd(proj, do_b, lbs[l:l + 1], dproj, f"hgrn_bwd_{l}")
        gwi = _gw_matmul(h_t, dproj, f"gw_matmul_{l}")
        swapping = _exchange_start(f"scatter_in_{l}_sibling_start", [gwi], [lax.empty((4, d, shard), BF16)], pair_sends)
        if l > 0:
            dh = _dh_matmul(dproj, w_in_l, swapping[4], f"dh_matmul_{l}")
        (gwi,), (stage,) = _exchange_wait(f"scatter_in_{l}_sibling_wait", swapping, dh if l > 0 else swapping[4],
                                          pair_sends)
        land, chip_sums = _pair_sum(gwi, stage, me, f"pair_sum_{l}")
        in_started = _exchange_start(f"scatter_in_{l}_chips_start", [chip_sums], [land], chip_sum_sends)
        scattering[l] = (in_started, rest_started)
        tie = in_started[4]
        if l == 0:
            dh = _dh_matmul(dproj, w_in_l, tie, f"dh_matmul_{l}")
        dx, lvec = _ln_bwd(dh, xl, scale, dres, f"ln_bwd_{l}", after=tie)
        dmod[l] = jnp.concatenate([lvec[0], lvec[1], mvec[2]])
        norm_grad = bvec[0].reshape(WIDTH // HG_HEAD_DIM, HG_HEAD_DIM).sum(axis=0)
        small_grads[l] = jnp.concatenate([mvec[0], mvec[1], norm_grad, dlb[0], bvec[1:4].reshape(-1)])
    grad_x = dx[None]

    flat = lambda a: a.reshape(-1, a.shape[-1])
    big = {"w_in": (w_in, m_w_in, v_w_in), "w_branch": (w_branch, m_w_branch, v_w_branch),
           "w_out": (w_out, m_w_out, v_w_out)}
    big_results = {n: None for n in big}

    def adam_layer(l, after):
        in_started, rest_started = scattering[l]
        p_branch_l, p_out_l = _exchange_wait(f"scatter_rest_{l}_wait", rest_started, after, rest_scatter)[1]
        p_in_l = _exchange_wait(f"scatter_in_{l}_chips_wait", in_started, after, chip_sum_sends)[1][0]
        parts = {"w_in": p_in_l, "w_branch": p_branch_l.reshape(NDEV, 3 * WIDTH, dsh), "w_out": p_out_l}
        last = None
        for n, (w, m, v) in big.items():
            rows_per_layer = flat(w).shape[0] // n_layers
            big_results[n] = _sum_adamw(parts[n], flat(w), flat(m), flat(v), f"adamw_{n}_{l}",
                                        first_row=l * rows_per_layer, into=big_results[n], after=last)
            last = big_results[n][3]
        return last

    after_adam = None
    for l in reversed(range(1, n_layers)):
        after_adam = adam_layer(l, tie)

    small_vec = jnp.concatenate(dmod + small_grads + [loss_part.reshape(1)])
    n_small = small_vec.shape[0]
    small_all = _all_gather_small("gather_small_grads", _pad_rows(small_vec, BLK), after=after_adam)
    small_sum = _sum_parts(small_all, "sum_small_grads").reshape(-1)[:n_small]
    dmod_all = small_all.reshape(NDEV, -1)[:, :n_layers * 3 * d].reshape(NDEV, n_layers, 3 * d)

    loss = small_sum[n_small - 1]

    off = n_layers * 3 * d
    grad_b_mod = small_sum[:off].reshape(n_layers, 3 * d)
    per_layer = 2 * d + HG_HEAD_DIM + WIDTH + 3 * WIDTH
    g_ln_g, g_ln_b, g_norm, g_lbs, g_conv = [], [], [], [], []
    for l in range(n_layers):
        seg = small_sum[off + l * per_layer: off + (l + 1) * per_layer]
        g_ln_g.append(seg[:d])
        g_ln_b.append(seg[d:2 * d])
        g_norm.append(seg[2 * d:2 * d + HG_HEAD_DIM])
        g_lbs.append(seg[2 * d + HG_HEAD_DIM:2 * d + HG_HEAD_DIM + WIDTH])
        g_conv.append(seg[2 * d + HG_HEAD_DIM + WIDTH:].reshape(3, WIDTH))
    grad_ln_g, grad_ln_b = jnp.stack(g_ln_g), jnp.stack(g_ln_b)
    grad_norm = jnp.stack(g_norm)
    _, lbs_vjp = jax.vjp(_lower_bound_table, lower_bounds)
    grad_lower = lbs_vjp(jnp.stack(g_lbs))[0]
    grad_conv = lax.dynamic_slice_in_dim(jnp.stack(g_conv), me * cw_cols, cw_cols, axis=2)

    dmod_mine = lax.dynamic_slice_in_dim(dmod_all, me * cm, cm, axis=2).transpose(1, 0, 2)
    grad_w_mod = _wmod_grad(c_all.T, dmod_mine)

    adam_layer(0, grad_w_mod)
    r_w_in, r_w_branch, r_w_out = ([o.reshape(big[n][0].shape) for o in big_results[n]]
                                   for n in ("w_in", "w_branch", "w_out"))
    r_w_mod = [o.reshape(w_mod.shape) for o in
               _sum_adamw(grad_w_mod.reshape(1, -1, cm), flat(w_mod), flat(m_w_mod), flat(v_w_mod), "adamw_w_mod")]

    small_names = ["b_mod", "conv_w", "hgrn_norm_w", "lower_bounds", "ln_g", "ln_b"]
    small_g = [grad_b_mod, grad_conv, grad_norm, grad_lower, grad_ln_g, grad_ln_b]
    small_w = [b_mod, conv_w, hgrn_norm_w, lower_bounds, ln_g, ln_b]
    small_m = [m_b_mod, m_conv_w, m_hgrn_norm_w, m_lower_bounds, m_ln_g, m_ln_b]
    small_v = [v_b_mod, v_conv_w, v_hgrn_norm_w, v_lower_bounds, v_ln_g, v_ln_b]
    as_rows = lambda a: a.reshape(-1, a.shape[-1])
    updates = _adamw_small([as_rows(a) for a in small_g], [as_rows(a) for a in small_w],
                           [as_rows(a) for a in small_m], [as_rows(a) for a in small_v])
    r_small = {n: [g] + [u.reshape(w.shape) for u in upd]
               for n, g, w, upd in zip(small_names, small_g, small_w, updates)}

    results = {"w_mod": r_w_mod, "w_in": r_w_in, "w_branch": r_w_branch, "w_out": r_w_out, **r_small}
    order = ["w_mod", "b_mod", "w_in", "conv_w", "hgrn_norm_w", "lower_bounds", "w_branch", "w_out", "ln_g", "ln_b"]
    outs = [loss, grad_x]
    for idx in range(4):
        outs.extend(results[n][idx] for n in order)
    return tuple(outs)
```
